```python
import math
import jax, jax.numpy as jnp
from jax import lax
import numpy as np

D_MODEL = 1024
BATCH = 8
SEQ = 2048
DEPTH = 2

HEAD_DIM = 64
SB_HEADS = 8
SWA_Q_HEADS = 8
SWA_KV_HEADS = 2
SWA_GROUP = SWA_Q_HEADS // SWA_KV_HEADS
WINDOW = 128
BLOCK = 128
N_BUCKETS = 32
MAX_DISTANCE = 128
D_FF = 2816
EPS = 1e-6
NEG_INF = -1e30

SB_WIDTH = SB_HEADS * HEAD_DIM
SWA_WIDTH = SWA_Q_HEADS * HEAD_DIM
KV_WIDTH = SWA_KV_HEADS * HEAD_DIM
MIX_WIDTH = SB_WIDTH + SWA_WIDTH
IN_WIDTH = 3 * SB_WIDTH + SWA_WIDTH + 2 * KV_WIDTH
SPLITS = (SB_WIDTH, 2 * SB_WIDTH, 3 * SB_WIDTH,
          3 * SB_WIDTH + SWA_WIDTH, 3 * SB_WIDTH + SWA_WIDTH + KV_WIDTH)

kernel_name = "hymba_stickbreak_swa_sink_macaron"


def rms_norm(x, g):
    xf = x.astype(jnp.float32)
    y = xf * lax.rsqrt(jnp.mean(xf * xf, axis=-1, keepdims=True) + EPS)
    return (y * g.astype(jnp.float32)).astype(x.dtype)


def swiglu(x, w_gu, w_down):
    gate, up = jnp.split(x @ w_gu, 2, axis=-1)
    return (jax.nn.silu(gate) * up) @ w_down


def t5_causal_bucket(dist):
    max_exact = N_BUCKETS // 2
    d = jnp.maximum(dist, 1).astype(jnp.float32)
    large = max_exact + (jnp.log(d / max_exact) / math.log(MAX_DISTANCE / max_exact)
                         * (N_BUCKETS - max_exact)).astype(jnp.int32)
    large = jnp.minimum(large, N_BUCKETS - 1)
    return jnp.where(dist < max_exact, dist, large)


def stick_breaking_attention(q, k, v):
    B, S = q.shape[:2]
    nblk = S // BLOCK
    outs = []
    for i in range(nblk):
        L = (i + 1) * BLOCK
        qi = q[:, i * BLOCK:L]
        z = jnp.einsum('bqhd,bkhd->bhqk', qi, k[:, :L],
                       preferred_element_type=jnp.float32) * (HEAD_DIM ** -0.5)
        t_pos = i * BLOCK + jnp.arange(BLOCK)[:, None]
        s_pos = jnp.arange(L)[None, :]
        causal = s_pos < t_pos
        neg_log_keep = jnp.where(causal, jax.nn.softplus(z), 0.0)
        suffix = lax.cumsum(neg_log_keep, axis=3, reverse=True) - neg_log_keep
        log_w = jax.nn.log_sigmoid(z) - suffix
        w = jnp.where(causal, jnp.exp(log_w), 0.0).astype(v.dtype)
        outs.append(jnp.einsum('bhqk,bkhd->bqhd', w, v[:, :L]))
    return jnp.concatenate(outs, axis=1).reshape(B, S, SB_WIDTH)


def sliding_window_sink_attention(q, k, v, sinks, rel_bias):
    B, S = q.shape[:2]
    nblk = S // BLOCK
    qb = q.reshape(B, nblk, BLOCK, SWA_KV_HEADS, SWA_GROUP, HEAD_DIM)

    def band(t):
        tp = jnp.pad(t, ((0, 0), (BLOCK, 0), (0, 0), (0, 0)))
        prev = tp[:, :S].reshape(B, nblk, BLOCK, SWA_KV_HEADS, HEAD_DIM)
        cur = t.reshape(B, nblk, BLOCK, SWA_KV_HEADS, HEAD_DIM)
        return jnp.concatenate([prev, cur], axis=2)

    kb, vb = band(k), band(v)
    scores = jnp.einsum('bnqhgd,bnkhd->bnhgqk', qb, kb,
                        preferred_element_type=jnp.float32) * (HEAD_DIM ** -0.5)
    a = jnp.arange(BLOCK)[:, None]
    c = jnp.arange(2 * BLOCK)[None, :]
    dist = BLOCK + a - c
    bias = rel_bias.astype(jnp.float32)[t5_causal_bucket(jnp.maximum(dist, 0))]
    bias = bias.transpose(2, 0, 1).reshape(SWA_KV_HEADS, SWA_GROUP, BLOCK, 2 * BLOCK)
    in_band = (dist >= 0) & (dist < WINDOW)
    key_exists = (jnp.arange(nblk)[:, None] > 0) | (c >= BLOCK)
    mask = in_band[None] & key_exists[:, None, :]
    scores = jnp.where(mask[None, :, None, None], scores + bias, NEG_INF)
    sink = sinks.astype(jnp.float32).reshape(SWA_KV_HEADS, SWA_GROUP)[None, None, :, :, None, None]
    m = jnp.maximum(jnp.max(scores, axis=-1, keepdims=True), sink)
    p = jnp.exp(scores - m)
    p = (p / (jnp.sum(p, axis=-1, keepdims=True) + jnp.exp(sink - m))).astype(v.dtype)
    out = jnp.einsum('bnhgqk,bnkhd->bnqhgd', p, vb)
    return out.reshape(B, S, SWA_WIDTH)


def _fwd_setup_inputs(seed: int = 0) -> dict:
    key = jax.random.key(seed)
    ks = jax.random.split(key, 16)
    f32 = jnp.float32

    def w(k, shape, fan_in):
        return jax.random.normal(k, shape, f32) * (fan_in ** -0.5)

    def gain(k, shape):
        return 1.0 + 0.02 * jax.random.normal(k, shape, f32)

    return {
        "x": jax.random.normal(ks[0], (BATCH, SEQ, D_MODEL), f32),
        "norm_ffn1": gain(ks[1], (DEPTH, D_MODEL)),
        "w_ffn1_gu": w(ks[2], (DEPTH, D_MODEL, 2 * D_FF), D_MODEL),
        "w_ffn1_down": w(ks[3], (DEPTH, D_FF, D_MODEL), D_FF),
        "norm_mix": gain(ks[4], (DEPTH, D_MODEL)),
        "w_in": w(ks[5], (DEPTH, D_MODEL, IN_WIDTH), D_MODEL),
        "sinks": 0.5 * jax.random.normal(ks[6], (DEPTH, SWA_Q_HEADS), f32),
        "norm_out_sb": gain(ks[7], (DEPTH, SB_WIDTH)),
        "norm_out_swa": gain(ks[8], (DEPTH, SWA_WIDTH)),
        "w_out": w(ks[9], (DEPTH, MIX_WIDTH, D_MODEL), MIX_WIDTH),
        "norm_ffn2": gain(ks[10], (DEPTH, D_MODEL)),
        "w_ffn2_gu": w(ks[11], (DEPTH, D_MODEL, 2 * D_FF), D_MODEL),
        "w_ffn2_down": w(ks[12], (DEPTH, D_FF, D_MODEL), D_FF),
        "rel_bias": 0.5 * jax.random.normal(ks[13], (N_BUCKETS, SWA_Q_HEADS), f32),
        "norm_final": gain(ks[14], (D_MODEL,)),
    }


def _fwd_reference(x, norm_ffn1, w_ffn1_gu, w_ffn1_down, norm_mix, w_in, sinks,
              norm_out_sb, norm_out_swa, w_out, norm_ffn2, w_ffn2_gu, w_ffn2_down,
              rel_bias, norm_final):
    B, S, _ = x.shape
    h = x
    for l in range(DEPTH):
        h = h + 0.5 * swiglu(rms_norm(h, norm_ffn1[l]), w_ffn1_gu[l], w_ffn1_down[l])
        n = rms_norm(h, norm_mix[l])
        proj = n @ w_in[l]
        q_sb, k_sb, v_sb, q_sw, k_sw, v_sw = jnp.split(proj, SPLITS, axis=-1)
        o_sb = stick_breaking_attention(
            q_sb.reshape(B, S, SB_HEADS, HEAD_DIM),
            k_sb.reshape(B, S, SB_HEADS, HEAD_DIM),
            v_sb.reshape(B, S, SB_HEADS, HEAD_DIM))
        o_sw = sliding_window_sink_attention(
            q_sw.reshape(B, S, SWA_Q_HEADS, HEAD_DIM),
            k_sw.reshape(B, S, SWA_KV_HEADS, HEAD_DIM),
            v_sw.reshape(B, S, SWA_KV_HEADS, HEAD_DIM),
            sinks[l], rel_bias)
        mixed = jnp.concatenate([rms_norm(o_sb, norm_out_sb[l]),
                                 rms_norm(o_sw, norm_out_swa[l])], axis=-1)
        h = h + mixed @ w_out[l]
        h = h + 0.5 * swiglu(rms_norm(h, norm_ffn2[l]), w_ffn2_gu[l], w_ffn2_down[l])
    return rms_norm(h, norm_final)


import jax as _jax
import jax.numpy as _jnp

TWIN_FORMAT = 'train_step'
FWD_PARAMS = ['x', 'norm_ffn1', 'w_ffn1_gu', 'w_ffn1_down', 'norm_mix', 'w_in', 'sinks', 'norm_out_sb', 'norm_out_swa', 'w_out', 'norm_ffn2', 'w_ffn2_gu', 'w_ffn2_down', 'rel_bias', 'norm_final']
TWIN_WEIGHTS = ['norm_ffn1', 'w_ffn1_gu', 'w_ffn1_down', 'norm_mix', 'w_in', 'sinks', 'norm_out_sb', 'norm_out_swa', 'w_out', 'norm_ffn2', 'w_ffn2_gu', 'w_ffn2_down', 'rel_bias', 'norm_final']
TWIN_DIFF_INPUT = 'x'
TWIN_INPUTS = ['x', 'norm_ffn1', 'w_ffn1_gu', 'w_ffn1_down', 'norm_mix', 'w_in', 'sinks', 'norm_out_sb', 'norm_out_swa', 'w_out', 'norm_ffn2', 'w_ffn2_gu', 'w_ffn2_down', 'rel_bias', 'norm_final', 'loss_target', 'm_norm_ffn1', 'm_w_ffn1_gu', 'm_w_ffn1_down', 'm_norm_mix', 'm_w_in', 'm_sinks', 'm_norm_out_sb', 'm_norm_out_swa', 'm_w_out', 'm_norm_ffn2', 'm_w_ffn2_gu', 'm_w_ffn2_down', 'm_rel_bias', 'm_norm_final', 'v_norm_ffn1', 'v_w_ffn1_gu', 'v_w_ffn1_down', 'v_norm_mix', 'v_w_in', 'v_sinks', 'v_norm_out_sb', 'v_norm_out_swa', 'v_w_out', 'v_norm_ffn2', 'v_w_ffn2_gu', 'v_w_ffn2_down', 'v_rel_bias', 'v_norm_final']
TWIN_OUTPUTS = ['loss', 'grad_x', 'grad_norm_ffn1', 'grad_w_ffn1_gu', 'grad_w_ffn1_down', 'grad_norm_mix', 'grad_w_in', 'grad_sinks', 'grad_norm_out_sb', 'grad_norm_out_swa', 'grad_w_out', 'grad_norm_ffn2', 'grad_w_ffn2_gu', 'grad_w_ffn2_down', 'grad_rel_bias', 'grad_norm_final', 'delta_norm_ffn1', 'delta_w_ffn1_gu', 'delta_w_ffn1_down', 'delta_norm_mix', 'delta_w_in', 'delta_sinks', 'delta_norm_out_sb', 'delta_norm_out_swa', 'delta_w_out', 'delta_norm_ffn2', 'delta_w_ffn2_gu', 'delta_w_ffn2_down', 'delta_rel_bias', 'delta_norm_final', 'new_m_norm_ffn1', 'new_m_w_ffn1_gu', 'new_m_w_ffn1_down', 'new_m_norm_mix', 'new_m_w_in', 'new_m_sinks', 'new_m_norm_out_sb', 'new_m_norm_out_swa', 'new_m_w_out', 'new_m_norm_ffn2', 'new_m_w_ffn2_gu', 'new_m_w_ffn2_down', 'new_m_rel_bias', 'new_m_norm_final', 'new_v_norm_ffn1', 'new_v_w_ffn1_gu', 'new_v_w_ffn1_down', 'new_v_norm_mix', 'new_v_w_in', 'new_v_sinks', 'new_v_norm_out_sb', 'new_v_norm_out_swa', 'new_v_w_out', 'new_v_norm_ffn2', 'new_v_w_ffn2_gu', 'new_v_w_ffn2_down', 'new_v_rel_bias', 'new_v_norm_final']
TWIN_LEAF_KINDS = {'loss': 'loss', 'grad_x': 'grad_x', 'grad_norm_ffn1': 'grad_w', 'grad_w_ffn1_gu': 'grad_w', 'grad_w_ffn1_down': 'grad_w', 'grad_norm_mix': 'grad_w', 'grad_w_in': 'grad_w', 'grad_sinks': 'grad_w', 'grad_norm_out_sb': 'grad_w', 'grad_norm_out_swa': 'grad_w', 'grad_w_out': 'grad_w', 'grad_norm_ffn2': 'grad_w', 'grad_w_ffn2_gu': 'grad_w', 'grad_w_ffn2_down': 'grad_w', 'grad_rel_bias': 'grad_w', 'grad_norm_final': 'grad_w', 'delta_norm_ffn1': 'delta_w', 'delta_w_ffn1_gu': 'delta_w', 'delta_w_ffn1_down': 'delta_w', 'delta_norm_mix': 'delta_w', 'delta_w_in': 'delta_w', 'delta_sinks': 'delta_w', 'delta_norm_out_sb': 'delta_w', 'delta_norm_out_swa': 'delta_w', 'delta_w_out': 'delta_w', 'delta_norm_ffn2': 'delta_w', 'delta_w_ffn2_gu': 'delta_w', 'delta_w_ffn2_down': 'delta_w', 'delta_rel_bias': 'delta_w', 'delta_norm_final': 'delta_w', 'new_m_norm_ffn1': 'new_m', 'new_m_w_ffn1_gu': 'new_m', 'new_m_w_ffn1_down': 'new_m', 'new_m_norm_mix': 'new_m', 'new_m_w_in': 'new_m', 'new_m_sinks': 'new_m', 'new_m_norm_out_sb': 'new_m', 'new_m_norm_out_swa': 'new_m', 'new_m_w_out': 'new_m', 'new_m_norm_ffn2': 'new_m', 'new_m_w_ffn2_gu': 'new_m', 'new_m_w_ffn2_down': 'new_m', 'new_m_rel_bias': 'new_m', 'new_m_norm_final': 'new_m', 'new_v_norm_ffn1': 'new_v', 'new_v_w_ffn1_gu': 'new_v', 'new_v_w_ffn1_down': 'new_v', 'new_v_norm_mix': 'new_v', 'new_v_w_in': 'new_v', 'new_v_sinks': 'new_v', 'new_v_norm_out_sb': 'new_v', 'new_v_norm_out_swa': 'new_v', 'new_v_w_out': 'new_v', 'new_v_norm_ffn2': 'new_v', 'new_v_w_ffn2_gu': 'new_v', 'new_v_w_ffn2_down': 'new_v', 'new_v_rel_bias': 'new_v', 'new_v_norm_final': 'new_v'}


def _forward(args):
    return _fwd_reference(*[args[k] for k in FWD_PARAMS])


def _output_shape():
    out = _jax.eval_shape(lambda: _forward(_fwd_setup_inputs(0)))
    return out.shape, out.dtype

N_MICROBATCH = 1
ADAM_LR = 0.001
ADAM_B1 = 0.9
ADAM_B2 = 0.999
ADAM_EPS = 1e-08
ADAM_WD = 0.01
ADAM_STEP = 10
PER_EXAMPLE_BATCH_AXIS = {'x': 0, 'loss_target': 0}
SHARED_INPUTS = []
_WEIGHT_DTYPES = {'norm_ffn1': _jnp.float32, 'w_ffn1_gu': _jnp.float32, 'w_ffn1_down': _jnp.float32, 'norm_mix': _jnp.float32, 'w_in': _jnp.float32, 'sinks': _jnp.float32, 'norm_out_sb': _jnp.float32, 'norm_out_swa': _jnp.float32, 'w_out': _jnp.float32, 'norm_ffn2': _jnp.float32, 'w_ffn2_gu': _jnp.float32, 'w_ffn2_down': _jnp.float32, 'rel_bias': _jnp.float32, 'norm_final': _jnp.float32}
MOMENT_SCALE = {'norm_ffn1': 5.908279e-02, 'w_ffn1_gu': 2.398368e-02, 'w_ffn1_down': 3.913275e-02, 'norm_mix': 1.190949e-01, 'w_in': 7.852008e-02, 'sinks': 4.083893e-02, 'norm_out_sb': 8.809063e-02, 'norm_out_swa': 9.736912e-02, 'w_out': 8.891671e-02, 'norm_ffn2': 3.439121e-02, 'w_ffn2_gu': 1.462697e-02, 'w_ffn2_down': 2.395465e-02, 'rel_bias': 1.239752e-01, 'norm_final': 1.597962e+01}


def _to_microbatches(a, axis):
    t = _jnp.moveaxis(a, axis, 0)
    t = t.reshape((N_MICROBATCH, t.shape[0] // N_MICROBATCH) + t.shape[1:])
    return _jnp.moveaxis(t, 1, axis + 1)


def setup_inputs(seed: int = 0) -> dict:
    inp = _fwd_setup_inputs(seed)
    key = _jax.random.fold_in(_jax.random.key(seed), 7919)
    shape, _ = _output_shape()
    out = dict(inp)
    out["loss_target"] = _jax.random.normal(_jax.random.fold_in(key, 0), shape, _jnp.float32)
    for i, name in enumerate(TWIN_WEIGHTS):
        w = inp[name].astype(_jnp.float32)
        if MOMENT_SCALE is None:
            s = _jnp.sqrt(_jnp.mean(_jnp.square(w)) + 1e-30)
        else:
            s = MOMENT_SCALE[name]
        km, kv = _jax.random.split(_jax.random.fold_in(key, i + 1))
        out[name] = w
        out["m_" + name] = s * _jax.random.normal(km, w.shape, _jnp.float32)
        out["v_" + name] = (s * s) * _jax.random.uniform(kv, w.shape, _jnp.float32, 0.5, 1.5)
    if N_MICROBATCH > 1:
        for name, axis in PER_EXAMPLE_BATCH_AXIS.items():
            out[name] = _to_microbatches(out[name], axis)
    return {'x': out['x'], 'norm_ffn1': out['norm_ffn1'], 'w_ffn1_gu': out['w_ffn1_gu'], 'w_ffn1_down': out['w_ffn1_down'], 'norm_mix': out['norm_mix'], 'w_in': out['w_in'], 'sinks': out['sinks'], 'norm_out_sb': out['norm_out_sb'], 'norm_out_swa': out['norm_out_swa'], 'w_out': out['w_out'], 'norm_ffn2': out['norm_ffn2'], 'w_ffn2_gu': out['w_ffn2_gu'], 'w_ffn2_down': out['w_ffn2_down'], 'rel_bias': out['rel_bias'], 'norm_final': out['norm_final'], 'loss_target': out['loss_target'], 'm_norm_ffn1': out['m_norm_ffn1'], 'm_w_ffn1_gu': out['m_w_ffn1_gu'], 'm_w_ffn1_down': out['m_w_ffn1_down'], 'm_norm_mix': out['m_norm_mix'], 'm_w_in': out['m_w_in'], 'm_sinks': out['m_sinks'], 'm_norm_out_sb': out['m_norm_out_sb'], 'm_norm_out_swa': out['m_norm_out_swa'], 'm_w_out': out['m_w_out'], 'm_norm_ffn2': out['m_norm_ffn2'], 'm_w_ffn2_gu': out['m_w_ffn2_gu'], 'm_w_ffn2_down': out['m_w_ffn2_down'], 'm_rel_bias': out['m_rel_bias'], 'm_norm_final': out['m_norm_final'], 'v_norm_ffn1': out['v_norm_ffn1'], 'v_w_ffn1_gu': out['v_w_ffn1_gu'], 'v_w_ffn1_down': out['v_w_ffn1_down'], 'v_norm_mix': out['v_norm_mix'], 'v_w_in': out['v_w_in'], 'v_sinks': out['v_sinks'], 'v_norm_out_sb': out['v_norm_out_sb'], 'v_norm_out_swa': out['v_norm_out_swa'], 'v_w_out': out['v_w_out'], 'v_norm_ffn2': out['v_norm_ffn2'], 'v_w_ffn2_gu': out['v_w_ffn2_gu'], 'v_w_ffn2_down': out['v_w_ffn2_down'], 'v_rel_bias': out['v_rel_bias'], 'v_norm_final': out['v_norm_final']}


def _loss(weights, diff, rest, loss_target):
    with _jax.named_scope("forward"):
        args = {**rest, TWIN_DIFF_INPUT: diff, **{k: w.astype(_WEIGHT_DTYPES[k]) for k, w in weights.items()}}
        y = _forward(args)
    with _jax.named_scope("loss_head"):
        err = _jnp.square(y.astype(_jnp.float32) - loss_target)
        return 0.5 * _jnp.sum(_jnp.mean(err, axis=-1)) if err.ndim else 0.5 * err


def _adamw(w, g, m, v):
    m = ADAM_B1 * m + (1.0 - ADAM_B1) * g
    v = ADAM_B2 * v + (1.0 - ADAM_B2) * _jnp.square(g)
    m_hat = m / (1.0 - ADAM_B1 ** ADAM_STEP)
    v_hat = v / (1.0 - ADAM_B2 ** ADAM_STEP)
    delta = -ADAM_LR * (m_hat / (_jnp.sqrt(v_hat) + ADAM_EPS) + ADAM_WD * w)
    return delta, m, v


def reference(x, norm_ffn1, w_ffn1_gu, w_ffn1_down, norm_mix, w_in, sinks, norm_out_sb, norm_out_swa, w_out, norm_ffn2, w_ffn2_gu, w_ffn2_down, rel_bias, norm_final, loss_target, m_norm_ffn1, m_w_ffn1_gu, m_w_ffn1_down, m_norm_mix, m_w_in, m_sinks, m_norm_out_sb, m_norm_out_swa, m_w_out, m_norm_ffn2, m_w_ffn2_gu, m_w_ffn2_down, m_rel_bias, m_norm_final, v_norm_ffn1, v_w_ffn1_gu, v_w_ffn1_down, v_norm_mix, v_w_in, v_sinks, v_norm_out_sb, v_norm_out_swa, v_w_out, v_norm_ffn2, v_w_ffn2_gu, v_w_ffn2_down, v_rel_bias, v_norm_final):
    given = dict(x=x, norm_ffn1=norm_ffn1, w_ffn1_gu=w_ffn1_gu, w_ffn1_down=w_ffn1_down, norm_mix=norm_mix, w_in=w_in, sinks=sinks, norm_out_sb=norm_out_sb, norm_out_swa=norm_out_swa, w_out=w_out, norm_ffn2=norm_ffn2, w_ffn2_gu=w_ffn2_gu, w_ffn2_down=w_ffn2_down, rel_bias=rel_bias, norm_final=norm_final, loss_target=loss_target, m_norm_ffn1=m_norm_ffn1, m_w_ffn1_gu=m_w_ffn1_gu, m_w_ffn1_down=m_w_ffn1_down, m_norm_mix=m_norm_mix, m_w_in=m_w_in, m_sinks=m_sinks, m_norm_out_sb=m_norm_out_sb, m_norm_out_swa=m_norm_out_swa, m_w_out=m_w_out, m_norm_ffn2=m_norm_ffn2, m_w_ffn2_gu=m_w_ffn2_gu, m_w_ffn2_down=m_w_ffn2_down, m_rel_bias=m_rel_bias, m_norm_final=m_norm_final, v_norm_ffn1=v_norm_ffn1, v_w_ffn1_gu=v_w_ffn1_gu, v_w_ffn1_down=v_w_ffn1_down, v_norm_mix=v_norm_mix, v_w_in=v_w_in, v_sinks=v_sinks, v_norm_out_sb=v_norm_out_sb, v_norm_out_swa=v_norm_out_swa, v_w_out=v_w_out, v_norm_ffn2=v_norm_ffn2, v_w_ffn2_gu=v_w_ffn2_gu, v_w_ffn2_down=v_w_ffn2_down, v_rel_bias=v_rel_bias, v_norm_final=v_norm_final)
    weights = {n: given[n] for n in TWIN_WEIGHTS}
    shared = {n: given[n] for n in SHARED_INPUTS}
    per_example = {n: given[n] for n in ['x']}
    grad_fn = _jax.value_and_grad(_loss, argnums=(0, 1))

    def one_microbatch(ex, loss_target):
        ex = dict(ex)
        diff = ex.pop(TWIN_DIFF_INPUT)
        return grad_fn(weights, diff, {**shared, **ex}, loss_target)

    if N_MICROBATCH == 1:
        loss, (grad_w, grad_x) = one_microbatch(per_example, given["loss_target"])
    else:
        def body(carry, xs):
            loss_sum, grad_sum = carry
            l_k, (gw_k, gx_k) = one_microbatch(xs[0], xs[1])
            with _jax.named_scope("update"):
                return (loss_sum + l_k, _jax.tree.map(_jnp.add, grad_sum, gw_k)), gx_k

        init = (_jnp.zeros((), _jnp.float32), _jax.tree.map(_jnp.zeros_like, weights))
        (loss, grad_w), grad_x = _jax.lax.scan(body, init, (per_example, given["loss_target"]))
    with _jax.named_scope("update"):
        delta_w, new_m, new_v = {}, {}, {}
        for n in TWIN_WEIGHTS:
            delta_w[n], new_m[n], new_v[n] = _adamw(weights[n], grad_w[n], given["m_" + n], given["v_" + n])
    return (loss, grad_x, *[grad_w[n] for n in TWIN_WEIGHTS], *[delta_w[n] for n in TWIN_WEIGHTS],
            *[new_m[n] for n in TWIN_WEIGHTS], *[new_v[n] for n in TWIN_WEIGHTS])
```

```python
import functools
import math

import numpy as np
import jax
import jax.numpy as jnp
from jax import lax
from jax.experimental import pallas as pl
from jax.experimental.pallas import tpu as pltpu

F32 = jnp.float32
BF16 = jnp.bfloat16

D_MODEL = 1024
DEPTH = 2
HEAD_DIM = 64
BLK = 128
N_BUCKETS = 32
MAX_DISTANCE = 128
D_FF = 2816
EPS = 1e-6
NEG_INF = -1e30
SB_W = 512
SWA_W = 512
KV_W = 128
IN_W = 2304
SCALE = HEAD_DIM ** -0.5
N_CHIPS = 4
FS = 2 * D_FF // N_CHIPS
LANES = 128
V7X_VMEM_LIMIT = 56 * 2 ** 20
TM = 512

ADAM_LR = 0.001
ADAM_B1 = 0.9
ADAM_B2 = 0.999
ADAM_EPS = 1e-08
ADAM_WD = 0.01
ADAM_STEP = 10

MESH = pl.DeviceIdType.MESH
ANY = pl.BlockSpec(memory_space=pl.ANY)


def _params(n_grid):
    return pltpu.CompilerParams(dimension_semantics=("arbitrary",) * n_grid, vmem_limit_bytes=V7X_VMEM_LIMIT)


def _dot(a, b):
    return jnp.dot(a, b, preferred_element_type=F32)


def _dot_nt(a, b):
    return lax.dot_general(a, b, (((1,), (1,)), ((), ())), preferred_element_type=F32)


def _dot_tn(a, b):
    return lax.dot_general(a, b, (((0,), (0,)), ((), ())), preferred_element_type=F32)


def _rms_fwd(x, g):
    r = lax.rsqrt(jnp.mean(x * x, axis=-1, keepdims=True) + EPS)
    xh = x * r
    return xh * g, xh, r


def _rms_bwd(dy, xh, r, g):
    u = dy * g
    dx = r * (u - xh * jnp.mean(u * xh, axis=-1, keepdims=True))
    dg = jnp.sum(dy * xh, axis=0, keepdims=True)
    return dx, dg


def _softplus(z):
    return jnp.maximum(z, 0.0) + jnp.log1p(jnp.exp(-jnp.abs(z)))


def _norm_cast(h, g):
    t, w = h.shape

    def body(h_ref, g_ref, n_ref):
        y, _, _ = _rms_fwd(h_ref[...], g_ref[...])
        n_ref[...] = y.astype(BF16)

    return pl.pallas_call(
        body, name="norm_cast", grid=(t // TM,),
        in_specs=[pl.BlockSpec((TM, w), lambda i: (i, 0)), pl.BlockSpec((1, w), lambda i: (0, 0))],
        out_specs=pl.BlockSpec((TM, w), lambda i: (i, 0)),
        out_shape=jax.ShapeDtypeStruct((t, w), BF16), compiler_params=_params(1))(h, g)


def _ffn_gu(n, wgu, l):
    t, d = n.shape

    def body(n_ref, wg_ref, wu_ref, gu_ref, act_ref):
        x = n_ref[...]
        g = _dot(x, wg_ref[...])
        u = _dot(x, wu_ref[...])
        gu_ref[0] = g.astype(BF16)
        gu_ref[1] = u.astype(BF16)
        act_ref[...] = (g * jax.nn.sigmoid(g) * u).astype(BF16)

    return pl.pallas_call(
        body, name="ffn_gu", grid=(2, t // TM),
        in_specs=[pl.BlockSpec((TM, d), lambda j, i: (i, 0)),
                  pl.BlockSpec((None, None, d, FS), lambda j, i: (l, j, 0, 0)),
                  pl.BlockSpec((None, None, d, FS), lambda j, i: (l, j + 2, 0, 0))],
        out_specs=[pl.BlockSpec((2, TM, FS), lambda j, i: (0, i, j)), pl.BlockSpec((TM, FS), lambda j, i: (i, j))],
        out_shape=[jax.ShapeDtypeStruct((2, t, D_FF), BF16), jax.ShapeDtypeStruct((t, D_FF), BF16)],
        compiler_params=_params(2))(n, wgu, wgu)


def _down_res(act, wdn, h, l):
    t, f = act.shape
    d = h.shape[1]

    def body(a_ref, w_ref, h_ref, o_ref):
        o_ref[...] = h_ref[...] + 0.5 * _dot(a_ref[...], w_ref[...])

    return pl.pallas_call(
        body, name="down_res", grid=(t // TM,),
        in_specs=[pl.BlockSpec((TM, f), lambda i: (i, 0)), pl.BlockSpec((None, f, d), lambda i: (l, 0, 0)),
                  pl.BlockSpec((TM, d), lambda i: (i, 0))],
        out_specs=pl.BlockSpec((TM, d), lambda i: (i, 0)),
        out_shape=jax.ShapeDtypeStruct((t, d), F32), compiler_params=_params(1))(act, wdn, h)


def _proj(n, w_in, l):
    t, d = n.shape
    w = w_in.shape[2]

    def body(n_ref, w_ref, o_ref):
        o_ref[...] = _dot(n_ref[...], w_ref[...]).astype(BF16)

    return pl.pallas_call(
        body, name="proj", grid=(t // TM,),
        in_specs=[pl.BlockSpec((TM, d), lambda i: (i, 0)), pl.BlockSpec((None, d, w), lambda i: (l, 0, 0))],
        out_specs=pl.BlockSpec((TM, w), lambda i: (i, 0)),
        out_shape=jax.ShapeDtypeStruct((t, w), BF16), compiler_params=_params(1))(n, w_in)


def _out_res(o_sb, o_sw, g_sb, g_sw, w_out, h, l):
    t, d = h.shape

    def body(a_ref, b_ref, ga_ref, gb_ref, w_ref, h_ref, o_ref, mix_ref):
        ya, _, _ = _rms_fwd(a_ref[...], ga_ref[...])
        yb, _, _ = _rms_fwd(b_ref[...], gb_ref[...])
        mixed = jnp.concatenate([ya.astype(BF16), yb.astype(BF16)], axis=1)
        mix_ref[...] = mixed
        o_ref[...] = h_ref[...] + _dot(mixed, w_ref[...])

    return pl.pallas_call(
        body, name="out_res", grid=(t // TM,),
        in_specs=[pl.BlockSpec((TM, SB_W), lambda i: (i, 0)), pl.BlockSpec((TM, SWA_W), lambda i: (i, 0)),
                  pl.BlockSpec((1, SB_W), lambda i: (0, 0)), pl.BlockSpec((1, SWA_W), lambda i: (0, 0)),
                  pl.BlockSpec((None, d, d), lambda i: (l, 0, 0)), pl.BlockSpec((TM, d), lambda i: (i, 0))],
        out_specs=[pl.BlockSpec((TM, d), lambda i: (i, 0)), pl.BlockSpec((TM, d), lambda i: (i, 0))],
        out_shape=[jax.ShapeDtypeStruct((t, d), F32), jax.ShapeDtypeStruct((t, d), BF16)],
        compiler_params=_params(1))(o_sb, o_sw, g_sb, g_sw, w_out, h)


def _loss_head(h, g, tgt):
    t, d = h.shape

    def body(h_ref, g_ref, t_ref, dh_ref, dg_ref, loss_ref):
        @pl.when(pl.program_id(0) == 0)
        def _():
            dg_ref[...] = jnp.zeros_like(dg_ref)
            loss_ref[...] = jnp.zeros_like(loss_ref)

        gg = g_ref[...]
        y, xh, r = _rms_fwd(h_ref[...], gg)
        err = y - t_ref[...]
        part = 0.5 * jnp.sum(jnp.sum(err * err, axis=1, keepdims=True) / d, axis=0, keepdims=True)
        loss_ref[...] += jnp.broadcast_to(part, loss_ref.shape)
        dx, dg = _rms_bwd(err / d, xh, r, gg)
        dh_ref[...] = dx
        dg_ref[...] += dg

    return pl.pallas_call(
        body, name="loss_head", grid=(t // TM,),
        in_specs=[pl.BlockSpec((TM, d), lambda i: (i, 0)), pl.BlockSpec((1, d), lambda i: (0, 0)),
                  pl.BlockSpec((TM, d), lambda i: (i, 0))],
        out_specs=[pl.BlockSpec((TM, d), lambda i: (i, 0)), pl.BlockSpec((1, d), lambda i: (0, 0)),
                   pl.BlockSpec((1, LANES), lambda i: (0, 0))],
        out_shape=[jax.ShapeDtypeStruct((t, d), F32), jax.ShapeDtypeStruct((1, d), F32),
                   jax.ShapeDtypeStruct((1, LANES), F32)],
        compiler_params=_params(1))(h, g, tgt)


def _ffn_dact(dh, wdn, gu, l):
    t, d = dh.shape

    def body(dh_ref, w_ref, gu_ref, o_ref):
        da = 0.5 * _dot_nt(dh_ref[...].astype(BF16), w_ref[...])
        g = gu_ref[0].astype(F32)
        u = gu_ref[1].astype(F32)
        sig = jax.nn.sigmoid(g)
        silu = g * sig
        o_ref[0] = (da * u * (sig * (1.0 + g * (1.0 - sig)))).astype(BF16)
        o_ref[1] = (da * silu).astype(BF16)

    return pl.pallas_call(
        body, name="ffn_dact", grid=(2, t // TM),
        in_specs=[pl.BlockSpec((TM, d), lambda j, i: (i, 0)), pl.BlockSpec((None, FS, d), lambda j, i: (l, j, 0)),
                  pl.BlockSpec((2, TM, FS), lambda j, i: (0, i, j))],
        out_specs=pl.BlockSpec((2, TM, FS), lambda j, i: (0, i, j)),
        out_shape=jax.ShapeDtypeStruct((2, t, D_FF), BF16), compiler_params=_params(2))(dh, wdn, gu)


def _dn_norm_bwd(a, a_spec, w, w_spec, nk, dh, h_in, g):
    t, d = dh.shape

    def body(a_ref, w_ref, dh_ref, h_ref, g_ref, o_ref, dg_ref, acc_ref):
        i, k = pl.program_id(0), pl.program_id(1)

        @pl.when(k == 0)
        def _():
            acc_ref[...] = jnp.zeros_like(acc_ref)

        acc_ref[...] += _dot_nt(a_ref[...], w_ref[...])

        @pl.when(k == nk - 1)
        def _():
            gg = g_ref[...]
            _, xh, r = _rms_fwd(h_ref[...], gg)
            dx, dg = _rms_bwd(acc_ref[...], xh, r, gg)
            o_ref[...] = dh_ref[...] + dx

            @pl.when(i == 0)
            def _():
                dg_ref[...] = dg

            @pl.when(i > 0)
            def _():
                dg_ref[...] += dg

    row = pl.BlockSpec((TM, d), lambda i, k: (i, 0))
    return pl.pallas_call(
        body, name="dn_norm_bwd", grid=(t // TM, nk),
        in_specs=[a_spec, w_spec, row, row, pl.BlockSpec((1, d), lambda i, k: (0, 0))],
        out_specs=[row, pl.BlockSpec((1, d), lambda i, k: (0, 0))],
        out_shape=[jax.ShapeDtypeStruct((t, d), F32), jax.ShapeDtypeStruct((1, d), F32)],
        scratch_shapes=[pltpu.VMEM((TM, d), F32)], compiler_params=_params(2))(a, w, dh, h_in, g)


def _ffn_dn(dgu, wgu, dh, h_in, g, l):
    d = dh.shape[1]
    return _dn_norm_bwd(
        dgu, pl.BlockSpec((None, TM, FS), lambda i, k: (k // 2, i, k % 2)),
        wgu, pl.BlockSpec((None, None, d, FS), lambda i, k: (l, k, 0, 0)), N_CHIPS, dh, h_in, g)


def _mix_dn(dproj, w_in, dh, h_in, g, l):
    d = dh.shape[1]
    w = dproj.shape[1]
    return _dn_norm_bwd(
        dproj, pl.BlockSpec((TM, w), lambda i, k: (i, 0)),
        w_in, pl.BlockSpec((None, d, w), lambda i, k: (l, 0, 0)), 1, dh, h_in, g)


def _dmixed(dh, w_out, o_sb, o_sw, g_sb, g_sw, l):
    t, d = dh.shape

    def body(dh_ref, w_ref, a_ref, b_ref, ga_ref, gb_ref, o_ref, dga_ref, dgb_ref):
        i = pl.program_id(0)
        dm = _dot_nt(dh_ref[...].astype(BF16), w_ref[...])
        _, xa, ra = _rms_fwd(a_ref[...], ga_ref[...])
        _, xb, rb = _rms_fwd(b_ref[...], gb_ref[...])
        da, dga = _rms_bwd(dm[:, :SB_W], xa, ra, ga_ref[...])
        db, dgb = _rms_bwd(dm[:, SB_W:], xb, rb, gb_ref[...])
        o_ref[...] = jnp.concatenate([da.astype(BF16), db.astype(BF16)], axis=1)

        @pl.when(i == 0)
        def _():
            dga_ref[...] = dga
            dgb_ref[...] = dgb

        @pl.when(i > 0)
        def _():
            dga_ref[...] += dga
            dgb_ref[...] += dgb

    return pl.pallas_call(
        body, name="dmixed", grid=(t // TM,),
        in_specs=[pl.BlockSpec((TM, d), lambda i: (i, 0)), pl.BlockSpec((None, d, d), lambda i: (l, 0, 0)),
                  pl.BlockSpec((TM, SB_W), lambda i: (i, 0)), pl.BlockSpec((TM, SWA_W), lambda i: (i, 0)),
                  pl.BlockSpec((1, SB_W), lambda i: (0, 0)), pl.BlockSpec((1, SWA_W), lambda i: (0, 0))],
        out_specs=[pl.BlockSpec((TM, d), lambda i: (i, 0)), pl.BlockSpec((1, SB_W), lambda i: (0, 0)),
                   pl.BlockSpec((1, SWA_W), lambda i: (0, 0))],
        out_shape=[jax.ShapeDtypeStruct((t, d), BF16), jax.ShapeDtypeStruct((1, SB_W), F32),
                   jax.ShapeDtypeStruct((1, SWA_W), F32)],
        compiler_params=_params(1))(dh, w_out, o_sb, o_sw, g_sb, g_sw)


def _wgrad(name, a, a_spec, b, b_spec, grid, out_shape, out_spec, scale, prev):
    def body(*refs):
        a_ref, b_ref, o_ref = refs[0], refs[1], refs[-1]
        r = _dot_tn(a_ref[...], b_ref[...].astype(BF16))
        o_ref[...] = r if scale == 1.0 else scale * r

    ins, specs, alias = [a, b], [a_spec, b_spec], {}
    if prev is not None:
        ins.append(prev)
        specs.append(ANY)
        alias = {2: 0}
    return pl.pallas_call(
        body, name=name, grid=grid, in_specs=specs, out_specs=out_spec,
        out_shape=jax.ShapeDtypeStruct(out_shape, F32), input_output_aliases=alias,
        compiler_params=_params(len(grid)))(*ins)


def _wgrad_gu(n, dgu, l, prev):
    t, d = n.shape
    return _wgrad(
        "wgrad_gu", n, pl.BlockSpec((t, TM), lambda s, r: (0, r)),
        dgu, pl.BlockSpec((None, t, FS), lambda s, r: (s // 2, 0, s % 2)), (N_CHIPS, d // TM),
        (DEPTH, N_CHIPS, d, FS), pl.BlockSpec((None, None, TM, FS), lambda s, r: (l, s, r, 0)), 1.0, prev)


def _wgrad_down(act, dh, l, prev):
    t, d = dh.shape
    return _wgrad(
        "wgrad_down", act, pl.BlockSpec((t, FS), lambda s, r: (0, s)), dh, pl.BlockSpec((t, TM), lambda s, r: (0, r)),
        (2, d // TM), (DEPTH, D_FF, d), pl.BlockSpec((None, FS, TM), lambda s, r: (l, s, r)), 0.5, prev)


def _wgrad_out(mixed, dh, l, prev):
    t, d = dh.shape
    return _wgrad(
        "wgrad_out", mixed, pl.BlockSpec((t, TM), lambda s: (0, s)), dh, pl.BlockSpec((t, d), lambda s: (0, 0)),
        (d // TM,), (DEPTH, d, d), pl.BlockSpec((None, TM, d), lambda s: (l, s, 0)), 1.0, prev)


def _wgrad_in(n, dproj, l, prev):
    t, d = n.shape
    w = dproj.shape[1]
    tw = w // 3
    return _wgrad(
        "wgrad_in", n, pl.BlockSpec((t, d), lambda s: (0, 0)), dproj, pl.BlockSpec((t, tw), lambda s: (0, s)),
        (3,), (DEPTH, d, w), pl.BlockSpec((None, d, tw), lambda s: (l, 0, s)), 1.0, prev)


def _tri(rel):
    row = lax.broadcasted_iota(jnp.int32, (BLK, BLK), 0)
    col = lax.broadcasted_iota(jnp.int32, (BLK, BLK), 1)
    m = rel(row, col).astype(BF16)
    return jnp.concatenate([m, m], axis=0)


def _scan_dot(x, tri2):
    hi = x.astype(BF16)
    lo = (x - hi.astype(F32)).astype(BF16)
    return _dot(jnp.concatenate([hi, lo], axis=1), tri2)


def _head_masks():
    lane = lax.broadcasted_iota(jnp.int32, (1, LANES), 1)
    return [lane < HEAD_DIM, lane >= HEAD_DIM]


def _sb_fwd(proj):
    t = proj.shape[0]
    nq = t // BLK

    def body(q_ref, k_ref, v_ref, o_ref, tot_ref):
        hm = _head_masks()
        row = lax.broadcasted_iota(jnp.int32, (BLK, BLK), 0)
        col = lax.broadcasted_iota(jnp.int32, (BLK, BLK), 1)
        causal = col < row
        after = _tri(lambda r, c: r > c)

        def block(qh, kb, carry, acc, diag):
            ks = pl.ds(pl.multiple_of(kb * BLK, BLK), BLK)
            z = _dot_nt(qh, k_ref[ks, :]) * SCALE
            sp = _softplus(z)
            spm = jnp.where(causal, sp, 0.0) if diag else sp
            w = jnp.exp(z - sp - (carry + _scan_dot(spm, after)))
            if diag:
                w = jnp.where(causal, w, 0.0)
            acc = acc + _dot(w.astype(BF16), v_ref[ks, :])
            return carry + jnp.sum(spm, axis=1, keepdims=True), acc

        def qblock(qi, _):
            qs = pl.ds(pl.multiple_of(qi * BLK, BLK), BLK)
            q = q_ref[qs, :]
            res = []
            for h in range(2):
                qh = jnp.where(hm[h], q, jnp.zeros_like(q))
                c0 = block(qh, qi, jnp.zeros((BLK, 1), F32), jnp.zeros((BLK, LANES), F32), True)
                res.append(lax.fori_loop(0, qi, lambda n, c: block(qh, qi - 1 - n, c[0], c[1], False), c0))
            o_ref[qs, :] = jnp.where(hm[0], res[0][1], res[1][1])
            for h in range(2):
                tot_ref[h, qs, :] = jnp.broadcast_to(res[h][0], (BLK, LANES))
            return 0

        lax.fori_loop(0, nq, qblock, 0)

    col_blk = lambda off: pl.BlockSpec((t, LANES), lambda p: (0, off + p))
    return pl.pallas_call(
        body, name="sb_fwd", grid=(4,), in_specs=[col_blk(0), col_blk(4), col_blk(8)],
        out_specs=[pl.BlockSpec((t, LANES), lambda p: (0, p)), pl.BlockSpec((2, t, LANES), lambda p: (p, 0, 0))],
        out_shape=[jax.ShapeDtypeStruct((t, SB_W), F32), jax.ShapeDtypeStruct((8, t, LANES), F32)],
        compiler_params=_params(1))(proj, proj, proj)


def _sb_bwd(proj, d_o, tot):
    t = proj.shape[0]
    nq = t // BLK

    def body(q_ref, k_ref, v_ref, do_ref, tot_ref, dq_ref, dk_ref, dv_ref, dk_acc, dv_acc):
        hm = _head_masks()
        row = lax.broadcasted_iota(jnp.int32, (BLK, BLK), 0)
        col = lax.broadcasted_iota(jnp.int32, (BLK, BLK), 1)
        causal = col < row
        before = _tri(lambda r, c: r < c)
        upto = _tri(lambda r, c: r <= c)
        dk_acc[...] = jnp.zeros_like(dk_acc)
        dv_acc[...] = jnp.zeros_like(dv_acc)

        def block(qh, doh, tt, kb, pre, ecum, dq, diag):
            ks = pl.ds(pl.multiple_of(kb * BLK, BLK), BLK)
            k = k_ref[ks, :]
            v = v_ref[ks, :]
            z = _dot_nt(qh, k) * SCALE
            sp = _softplus(z)
            spm = jnp.where(causal, sp, 0.0) if diag else sp
            logw = z - (tt - (pre + _scan_dot(spm, before)))
            if diag:
                logw = jnp.minimum(logw, 0.0)
            w = jnp.exp(logw)
            if diag:
                w = jnp.where(causal, w, 0.0)
            e = w * _dot_nt(doh, v)
            dz = (e - jnp.exp(z - sp) * (ecum + _scan_dot(e, upto))) * SCALE
            if diag:
                dz = jnp.where(causal, dz, 0.0)
            dzb = dz.astype(BF16)
            dq = dq + _dot(dzb, k)
            dk_acc[ks, :] += _dot_tn(dzb, qh)
            dv_acc[ks, :] += _dot_tn(w.astype(BF16), doh)
            return pre + jnp.sum(spm, axis=1, keepdims=True), ecum + jnp.sum(e, axis=1, keepdims=True), dq

        def qblock(qi, _):
            qs = pl.ds(pl.multiple_of(qi * BLK, BLK), BLK)
            q = q_ref[qs, :]
            do = do_ref[qs, :]
            res = []
            for h in range(2):
                qh = jnp.where(hm[h], q, jnp.zeros_like(q))
                doh = jnp.where(hm[h], do, jnp.zeros_like(do))
                tt = tot_ref[h, qs, :]
                c0 = (jnp.zeros((BLK, 1), F32), jnp.zeros((BLK, 1), F32), jnp.zeros((BLK, LANES), F32))
                c = lax.fori_loop(0, qi, lambda kb, c: block(qh, doh, tt, kb, c[0], c[1], c[2], False), c0)
                res.append(block(qh, doh, tt, qi, c[0], c[1], c[2], True)[2])
            dq_ref[qs, :] = jnp.where(hm[0], res[0], res[1]).astype(BF16)
            return 0

        lax.fori_loop(0, nq, qblock, 0)
        dk_ref[...] = dk_acc[...].astype(BF16)
        dv_ref[...] = dv_acc[...].astype(BF16)

    col_blk = lambda off: pl.BlockSpec((t, LANES), lambda p: (0, off + p))
    out = jax.ShapeDtypeStruct((t, SB_W), BF16)
    return pl.pallas_call(
        body, name="sb_bwd", grid=(4,),
        in_specs=[col_blk(0), col_blk(4), col_blk(8), col_blk(0), pl.BlockSpec((2, t, LANES), lambda p: (p, 0, 0))],
        out_specs=[col_blk(0), col_blk(0), col_blk(0)], out_shape=[out, out, out],
        scratch_shapes=[pltpu.VMEM((t, LANES), F32), pltpu.VMEM((t, LANES), F32)],
        compiler_params=_params(1))(proj, proj, proj, d_o, tot)


def _bucket_table():
    a = np.arange(BLK)[:, None]
    c = np.arange(2 * BLK)[None, :]
    dist = np.maximum(BLK + a - c, 0)
    max_exact = N_BUCKETS // 2
    dd = np.maximum(dist, 1).astype(np.float32)
    large = max_exact + (np.log(dd / max_exact) / math.log(MAX_DISTANCE / max_exact)
                         * (N_BUCKETS - max_exact)).astype(np.int32)
    large = np.minimum(large, N_BUCKETS - 1)
    return np.where(dist < max_exact, dist, large).astype(np.int32)


def _swa_masks():
    row = lax.broadcasted_iota(jnp.int32, (BLK, BLK), 0)
    col = lax.broadcasted_iota(jnp.int32, (BLK, BLK), 1)
    return col <= row, col > row


def _to_kv_lanes(x, hl, kvh, kvmask):
    x = x.astype(F32)
    if hl != kvh:
        x = pltpu.roll(x, HEAD_DIM, 1)
    return jnp.where(kvmask, x, 0.0).astype(BF16)


def _swa_fwd(proj, bias, sinks_b):
    t = proj.shape[0]
    nq = t // BLK

    def body(q_ref, k_ref, v_ref, bias_ref, sink_ref, o_ref, lse_ref):
        hm = _head_masks()
        m_own, m_prev = _swa_masks()

        def head(qs, ps, pair, hl, prev):
            hq = 2 * pair + hl
            kvh = hq // 4
            qh = _to_kv_lanes(q_ref[qs, pair * LANES:(pair + 1) * LANES], hl, kvh, hm[kvh])
            sink = sink_ref[hq:hq + 1, 0:1]
            s_c = jnp.where(m_own, _dot_nt(qh, k_ref[qs, :]) * SCALE + bias_ref[hq, :, BLK:], NEG_INF)
            m = jnp.maximum(jnp.max(s_c, axis=1, keepdims=True), sink)
            if prev:
                s_p = jnp.where(m_prev, _dot_nt(qh, k_ref[ps, :]) * SCALE + bias_ref[hq, :, :BLK], NEG_INF)
                m = jnp.maximum(m, jnp.max(s_p, axis=1, keepdims=True))
            p_c = jnp.exp(s_c - m)
            den = jnp.sum(p_c, axis=1, keepdims=True) + jnp.exp(sink - m)
            if prev:
                p_p = jnp.exp(s_p - m)
                den = den + jnp.sum(p_p, axis=1, keepdims=True)
            inv = 1.0 / den
            o = _dot((p_c * inv).astype(BF16), v_ref[qs, :])
            if prev:
                o = o + _dot((p_p * inv).astype(BF16), v_ref[ps, :])
            if hl != kvh:
                o = pltpu.roll(o, HEAD_DIM, 1)
            lse_ref[hq, qs, :] = jnp.broadcast_to(m + jnp.log(den), (BLK, LANES))
            return o

        def qblock(i, pair, prev):
            qs = pl.ds(pl.multiple_of(i * BLK, BLK), BLK)
            ps = pl.ds(pl.multiple_of(jnp.maximum(i - 1, 0) * BLK, BLK), BLK)
            o = [head(qs, ps, pair, hl, prev) for hl in range(2)]
            o_ref[qs, pair * LANES:(pair + 1) * LANES] = jnp.where(hm[0], o[0], o[1])

        for pair in range(4):
            qblock(0, pair, False)

            def step(i, _):
                qblock(i, pair, True)
                return 0

            lax.fori_loop(1, nq, step, 0)

    return pl.pallas_call(
        body, name="swa_fwd", grid=(1,),
        in_specs=[pl.BlockSpec((t, SWA_W), lambda i: (0, 3)), pl.BlockSpec((t, KV_W), lambda i: (0, 16)),
                  pl.BlockSpec((t, KV_W), lambda i: (0, 17)), pl.BlockSpec((8, BLK, 2 * BLK), lambda i: (0, 0, 0)),
                  pl.BlockSpec((8, LANES), lambda i: (0, 0))],
        out_specs=[pl.BlockSpec((t, SWA_W), lambda i: (0, 0)), pl.BlockSpec((8, t, LANES), lambda i: (0, 0, 0))],
        out_shape=[jax.ShapeDtypeStruct((t, SWA_W), F32), jax.ShapeDtypeStruct((8, t, LANES), F32)],
        compiler_params=_params(1))(proj, proj, proj, bias, sinks_b)


def _swa_bwd(proj, d_o, lse, bias, sinks_b, dbias_in):
    t = proj.shape[0]
    nq = t // BLK

    def body(q_ref, k_ref, v_ref, do_ref, lse_ref, bias_ref, sink_ref, dbi_ref,
             dq_ref, dk_ref, dv_ref, dsink_ref, dbias_ref, dk_acc, dv_acc):
        hm = _head_masks()
        m_own, m_prev = _swa_masks()
        dk_acc[...] = jnp.zeros_like(dk_acc)
        dv_acc[...] = jnp.zeros_like(dv_acc)
        dbias_ref[...] = dbi_ref[...]

        def head(qs, ps, pair, hl, prev, dsink):
            hq = 2 * pair + hl
            kvh = hq // 4
            lanes = slice(pair * LANES, (pair + 1) * LANES)
            qh = _to_kv_lanes(q_ref[qs, lanes], hl, kvh, hm[kvh])
            doh = _to_kv_lanes(do_ref[qs, lanes], hl, kvh, hm[kvh])
            sink = sink_ref[hq:hq + 1, 0:1]
            lse_h = lse_ref[hq, qs, :]
            kc = k_ref[qs, :]
            vc = v_ref[qs, :]
            p_c = jnp.exp(jnp.where(m_own, _dot_nt(qh, kc) * SCALE + bias_ref[hq, :, BLK:], NEG_INF) - lse_h)
            dp_c = _dot_nt(doh, vc)
            delta = jnp.sum(p_c * dp_c, axis=1, keepdims=True)
            if prev:
                kp = k_ref[ps, :]
                vp = v_ref[ps, :]
                p_p = jnp.exp(jnp.where(m_prev, _dot_nt(qh, kp) * SCALE + bias_ref[hq, :, :BLK], NEG_INF) - lse_h)
                dp_p = _dot_nt(doh, vp)
                delta = delta + jnp.sum(p_p * dp_p, axis=1, keepdims=True)
            ds_c = p_c * (dp_c - delta)
            dsink = dsink - jnp.sum(jnp.exp(sink - lse_h[:, 0:1]) * delta, axis=0, keepdims=True)
            dbias_ref[hq, :, BLK:] += ds_c
            ds_cb = ds_c.astype(BF16)
            dq = _dot(ds_cb, kc)
            dk_acc[qs, :] += _dot_tn(ds_cb, qh) * SCALE
            dv_acc[qs, :] += _dot_tn(p_c.astype(BF16), doh)
            if prev:
                ds_p = p_p * (dp_p - delta)
                dbias_ref[hq, :, :BLK] += ds_p
                ds_pb = ds_p.astype(BF16)
                dq = dq + _dot(ds_pb, kp)
                dk_acc[ps, :] += _dot_tn(ds_pb, qh) * SCALE
                dv_acc[ps, :] += _dot_tn(p_p.astype(BF16), doh)
            dq = dq * SCALE
            if hl != kvh:
                dq = pltpu.roll(dq, HEAD_DIM, 1)
            return dq, dsink

        def qblock(i, pair, prev, dsinks):
            qs = pl.ds(pl.multiple_of(i * BLK, BLK), BLK)
            ps = pl.ds(pl.multiple_of(jnp.maximum(i - 1, 0) * BLK, BLK), BLK)
            r = [head(qs, ps, pair, hl, prev, dsinks[hl]) for hl in range(2)]
            dq_ref[qs, pair * LANES:(pair + 1) * LANES] = jnp.where(hm[0], r[0][0], r[1][0]).astype(BF16)
            return r[0][1], r[1][1]

        for pair in range(4):
            ds0 = qblock(0, pair, False, (jnp.zeros((1, 1), F32), jnp.zeros((1, 1), F32)))
            ds = lax.fori_loop(1, nq, lambda i, c: qblock(i, pair, True, c), ds0)
            for hl in range(2):
                dsink_ref[2 * pair + hl:2 * pair + hl + 1, :] = jnp.broadcast_to(ds[hl], (1, LANES))

        dk_ref[...] = dk_acc[...].astype(BF16)
        dv_ref[...] = dv_acc[...].astype(BF16)

    full3 = pl.BlockSpec((8, BLK, 2 * BLK), lambda i: (0, 0, 0))
    kv = jax.ShapeDtypeStruct((t, KV_W), BF16)
    return pl.pallas_call(
        body, name="swa_bwd", grid=(1,),
        in_specs=[pl.BlockSpec((t, SWA_W), lambda i: (0, 3)), pl.BlockSpec((t, KV_W), lambda i: (0, 16)),
                  pl.BlockSpec((t, KV_W), lambda i: (0, 17)), pl.BlockSpec((t, SWA_W), lambda i: (0, 1)),
                  pl.BlockSpec((8, t, LANES), lambda i: (0, 0, 0)), full3, pl.BlockSpec((8, LANES), lambda i: (0, 0)),
                  full3],
        out_specs=[pl.BlockSpec((t, SWA_W), lambda i: (0, 0)), pl.BlockSpec((t, KV_W), lambda i: (0, 0)),
                   pl.BlockSpec((t, KV_W), lambda i: (0, 0)), pl.BlockSpec((8, LANES), lambda i: (0, 0)), full3],
        out_shape=[jax.ShapeDtypeStruct((t, SWA_W), BF16), kv, kv, jax.ShapeDtypeStruct((8, LANES), F32),
                   jax.ShapeDtypeStruct((8, BLK, 2 * BLK), F32)],
        scratch_shapes=[pltpu.VMEM((t, KV_W), F32), pltpu.VMEM((t, KV_W), F32)],
        compiler_params=_params(1))(proj, proj, proj, d_o, lse, bias, sinks_b, dbias_in)


def _bias_grad(dbias, buckets):
    def body(d_ref, b_ref, o_ref):
        lane = lax.broadcasted_iota(jnp.int32, (1, LANES), 1)
        bk = b_ref[...]
        for h in range(8):
            d = d_ref[h]
            acc = jnp.zeros((1, LANES), F32)
            for b in range(N_BUCKETS):
                s = jnp.sum(jnp.sum(jnp.where(bk == b, d, 0.0), axis=0, keepdims=True), axis=1, keepdims=True)
                acc = acc + jnp.where(lane == b, s, 0.0)
            o_ref[h:h + 1, :] = acc

    return pl.pallas_call(
        body, name="bias_grad", grid=(1,),
        in_specs=[pl.BlockSpec((8, BLK, 2 * BLK), lambda i: (0, 0, 0)), pl.BlockSpec((BLK, 2 * BLK), lambda i: (0, 0))],
        out_specs=pl.BlockSpec((8, LANES), lambda i: (0, 0)),
        out_shape=jax.ShapeDtypeStruct((8, LANES), F32), compiler_params=_params(1))(dbias, buckets)


def _row(a):
    return a.reshape(1, -1)


def _local_step(x, tgt, wts, small):
    buckets = jnp.asarray(_bucket_table())
    bias = jnp.transpose(small["rel_bias"][buckets], (2, 0, 1))
    saved = []
    h = x
    for l in range(DEPTH):
        s = {"h0": h}
        s["n1"] = _norm_cast(h, _row(small["norm_ffn1"][l]))
        s["gu1"], s["act1"] = _ffn_gu(s["n1"], wts["ffn1_gu"], l)
        h = _down_res(s["act1"], wts["ffn1_down"], h, l)
        s["h1"] = h
        s["nm"] = _norm_cast(h, _row(small["norm_mix"][l]))
        s["proj"] = _proj(s["nm"], wts["w_in"], l)
        s["sinks_b"] = jnp.broadcast_to(small["sinks"][l][:, None], (8, LANES))
        s["o_sb"], s["tot"] = _sb_fwd(s["proj"])
        s["o_sw"], s["lse"] = _swa_fwd(s["proj"], bias, s["sinks_b"])
        h, s["mixed"] = _out_res(s["o_sb"], s["o_sw"], _row(small["norm_out_sb"][l]), _row(small["norm_out_swa"][l]),
                                 wts["w_out"], h, l)
        s["h2"] = h
        s["n2"] = _norm_cast(h, _row(small["norm_ffn2"][l]))
        s["gu2"], s["act2"] = _ffn_gu(s["n2"], wts["ffn2_gu"], l)
        h = _down_res(s["act2"], wts["ffn2_down"], h, l)
        saved.append(s)

    dh, dg_final, loss = _loss_head(h, _row(small["norm_final"]), tgt)

    gw = {k: None for k in ("ffn1_gu", "ffn1_down", "w_in", "w_out", "ffn2_gu", "ffn2_down")}
    gs = {k: [None] * DEPTH for k in ("norm_ffn1", "norm_mix", "sinks", "norm_out_sb", "norm_out_swa", "norm_ffn2")}
    dbias = jnp.zeros((8, BLK, 2 * BLK), F32)
    for l in reversed(range(DEPTH)):
        s = saved[l]
        dgu = _ffn_dact(dh, wts["ffn2_down"], s["gu2"], l)
        gw["ffn2_down"] = _wgrad_down(s["act2"], dh, l, gw["ffn2_down"])
        gw["ffn2_gu"] = _wgrad_gu(s["n2"], dgu, l, gw["ffn2_gu"])
        dh, gs["norm_ffn2"][l] = _ffn_dn(dgu, wts["ffn2_gu"], dh, s["h2"], _row(small["norm_ffn2"][l]), l)
        gw["w_out"] = _wgrad_out(s["mixed"], dh, l, gw["w_out"])
        d_o, gs["norm_out_sb"][l], gs["norm_out_swa"][l] = _dmixed(
            dh, wts["w_out"], s["o_sb"], s["o_sw"], _row(small["norm_out_sb"][l]), _row(small["norm_out_swa"][l]), l)
        dq_sb, dk_sb, dv_sb = _sb_bwd(s["proj"], d_o, s["tot"])
        dq_sw, dk_sw, dv_sw, dsink, dbias = _swa_bwd(s["proj"], d_o, s["lse"], bias, s["sinks_b"], dbias)
        gs["sinks"][l] = dsink[:, 0]
        dproj = jnp.concatenate([dq_sb, dk_sb, dv_sb, dq_sw, dk_sw, dv_sw], axis=1)
        gw["w_in"] = _wgrad_in(s["nm"], dproj, l, gw["w_in"])
        dh, gs["norm_mix"][l] = _mix_dn(dproj, wts["w_in"], dh, s["h1"], _row(small["norm_mix"][l]), l)
        dgu = _ffn_dact(dh, wts["ffn1_down"], s["gu1"], l)
        gw["ffn1_down"] = _wgrad_down(s["act1"], dh, l, gw["ffn1_down"])
        gw["ffn1_gu"] = _wgrad_gu(s["n1"], dgu, l, gw["ffn1_gu"])
        dh, gs["norm_ffn1"][l] = _ffn_dn(dgu, wts["ffn1_gu"], dh, s["h0"], _row(small["norm_ffn1"][l]), l)

    gsmall = {k: jnp.stack([a.reshape(-1) for a in v]) for k, v in gs.items()}
    gsmall["rel_bias"] = jnp.transpose(_bias_grad(dbias, buckets)[:, :N_BUCKETS])
    gsmall["norm_final"] = dg_final.reshape(-1)
    return loss, dh, gw, gsmall


def _place():
    x, y, c = lax.axis_index("x"), lax.axis_index("y"), lax.axis_index("c")
    return x, y, c, 2 * x + y


def _chip_core(k, c):
    return (k // 2, k % 2, c)


def _all_gather(shards):
    n = len(shards)

    def body(*refs):
        ins, outs = refs[:n], refs[n:2 * n]
        ici_s, ici_r, d2d_s, d2d_r, loc = refs[2 * n:]
        x, y, c, me = _place()
        sib = (x, y, 1 - c)
        groups = [(t, l) for t in range(n) for l in range(DEPTH)]

        def ici(t, l, j, to):
            s = (t * DEPTH + l) * 3 + j
            return pltpu.make_async_remote_copy(
                src_ref=ins[t].at[l, c], dst_ref=outs[t].at[l, me, c], send_sem=ici_s.at[s], recv_sem=ici_r.at[s],
                device_id=to, device_id_type=MESH)

        def landed(t, l, j, half):
            src = (me + 3 - j) % N_CHIPS
            return outs[t].at[l, src, half]

        def d2d(t, l, j, half):
            s = (t * DEPTH + l) * 3 + j
            return pltpu.make_async_remote_copy(
                src_ref=landed(t, l, j, half), dst_ref=landed(t, l, j, half), send_sem=d2d_s.at[s],
                recv_sem=d2d_r.at[s], device_id=sib, device_id_type=MESH)

        own = [pltpu.make_async_copy(ins[t].at[l], outs[t].at[l, me], loc.at[t * DEPTH + l]) for t, l in groups]
        for cp in own:
            cp.start()
        sends = [ici(t, l, j, _chip_core((me + 1 + j) % N_CHIPS, c)) for t, l in groups for j in range(3)]
        for cp in sends:
            cp.start()
        passed = []
        for t, l in groups:
            for j in range(3):
                s = (t * DEPTH + l) * 3 + j
                pltpu.make_async_remote_copy(
                    src_ref=ins[t].at[l, c], dst_ref=landed(t, l, j, c), send_sem=ici_s.at[s], recv_sem=ici_r.at[s],
                    device_id=sib, device_id_type=MESH).wait_recv()
                passed.append(d2d(t, l, j, c))
                passed[-1].start()
        for t, l in groups:
            for j in range(3):
                d2d(t, l, j, 1 - c).wait_recv()
        for cp in sends + passed:
            cp.wait_send()
        for cp in own:
            cp.wait()

    n_cp = n * DEPTH * 3
    return pl.pallas_call(
        body, name="all_gather_weights", in_specs=[ANY] * n, out_specs=[ANY] * n,
        out_shape=[jax.ShapeDtypeStruct((DEPTH, N_CHIPS) + a.shape[1:], a.dtype) for a in shards],
        scratch_shapes=[pltpu.SemaphoreType.DMA((n_cp,))] * 4 + [pltpu.SemaphoreType.DMA((n * DEPTH,))],
        compiler_params=pltpu.CompilerParams(vmem_limit_bytes=V7X_VMEM_LIMIT))(*shards)


def _sibling_exchange(grads):
    n = len(grads)

    def body(*refs):
        ins, outs = refs[:n], refs[n:2 * n]
        ssem, rsem = refs[2 * n:]
        x, y, c, _ = _place()
        cps = [pltpu.make_async_remote_copy(
            src_ref=ins[t].at[:, :, 1 - c], dst_ref=outs[t], send_sem=ssem.at[t], recv_sem=rsem.at[t],
            device_id=(x, y, 1 - c), device_id_type=MESH) for t in range(n)]
        for cp in cps:
            cp.start()
        for cp in cps:
            cp.wait()

    return pl.pallas_call(
        body, name="grad_sibling_exchange", in_specs=[ANY] * n, out_specs=[ANY] * n,
        out_shape=[jax.ShapeDtypeStruct(a.shape[:2] + a.shape[3:], a.dtype) for a in grads],
        scratch_shapes=[pltpu.SemaphoreType.DMA((n,))] * 2,
        compiler_params=pltpu.CompilerParams(vmem_limit_bytes=V7X_VMEM_LIMIT))(*grads)


def _rows_per_block(rows, cols, copies):
    best = 16
    for tr in range(16, rows + 1, 16):
        if rows % tr == 0 and copies * tr * cols * 4 <= 6 * 2 ** 20:
            best = tr
    assert rows % best == 0
    return best


def _chip_sum(g5, xbuf, cm):
    _, _, _, r2, cols = g5.shape
    tr = _rows_per_block(r2, cols, N_CHIPS)

    def body(cm_ref, g_ref, x_ref, pb_ref, po_ref):
        pb_ref[...] = (g_ref[...] + x_ref[...]).astype(BF16)
        me = cm_ref[1]
        po_ref[...] = g_ref[me] + x_ref[me]

    return pl.pallas_call(
        body, name="grad_chip_sum",
        grid_spec=pltpu.PrefetchScalarGridSpec(
            num_scalar_prefetch=1, grid=(DEPTH, r2 // tr),
            in_specs=[pl.BlockSpec((None, N_CHIPS, None, tr, cols), lambda l, r, cm: (l, 0, cm[0], r, 0)),
                      pl.BlockSpec((None, N_CHIPS, tr, cols), lambda l, r, cm: (l, 0, r, 0))],
            out_specs=[pl.BlockSpec((N_CHIPS, None, tr, cols), lambda l, r, cm: (0, l, r, 0)),
                       pl.BlockSpec((None, tr, cols), lambda l, r, cm: (l, r, 0))]),
        out_shape=[jax.ShapeDtypeStruct((N_CHIPS, DEPTH, r2, cols), BF16), jax.ShapeDtypeStruct((DEPTH, r2, cols), F32)],
        compiler_params=_params(2))(cm, g5, xbuf)


def _chip_exchange(parts):
    n = len(parts)

    def body(*refs):
        ins, outs = refs[:n], refs[n:2 * n]
        ssem, rsem = refs[2 * n:]
        _, _, c, me = _place()
        cps = []
        for t in range(n):
            for j in range(3):
                to = (me + 1 + j) % N_CHIPS
                cps.append(pltpu.make_async_remote_copy(
                    src_ref=ins[t].at[to], dst_ref=outs[t].at[j], send_sem=ssem.at[3 * t + j],
                    recv_sem=rsem.at[3 * t + j], device_id=_chip_core(to, c), device_id_type=MESH))
        for cp in cps:
            cp.start()
        for cp in cps:
            cp.wait()

    return pl.pallas_call(
        body, name="grad_chip_exchange", in_specs=[ANY] * n, out_specs=[ANY] * n,
        out_shape=[jax.ShapeDtypeStruct((3,) + a.shape[1:], a.dtype) for a in parts],
        scratch_shapes=[pltpu.SemaphoreType.DMA((3 * n,))] * 2,
        compiler_params=pltpu.CompilerParams(vmem_limit_bytes=V7X_VMEM_LIMIT))(*parts)


def _total_sum(pown, rbuf):
    _, r2, cols = pown.shape
    tr = _rows_per_block(r2, cols, 3)

    def body(p_ref, r_ref, o_ref):
        acc = p_ref[...]
        for j in range(3):
            acc = acc + r_ref[j].astype(F32)
        o_ref[...] = acc

    return pl.pallas_call(
        body, name="grad_total_sum", grid=(DEPTH, r2 // tr),
        in_specs=[pl.BlockSpec((None, tr, cols), lambda l, r: (l, r, 0)),
                  pl.BlockSpec((3, None, tr, cols), lambda l, r: (0, l, r, 0))],
        out_specs=pl.BlockSpec((None, tr, cols), lambda l, r: (l, r, 0)),
        out_shape=jax.ShapeDtypeStruct(pown.shape, F32), compiler_params=_params(2))(pown, rbuf)


def _halves_exchange(halves):
    n = len(halves)

    def body(*refs):
        ins, outs = refs[:n], refs[n:2 * n]
        ssem, rsem, loc = refs[2 * n:]
        x, y, c, _ = _place()
        own = [pltpu.make_async_copy(ins[t], outs[t].at[:, c], loc.at[t]) for t in range(n)]
        cps = [pltpu.make_async_remote_copy(
            src_ref=ins[t], dst_ref=outs[t].at[:, c], send_sem=ssem.at[t], recv_sem=rsem.at[t],
            device_id=(x, y, 1 - c), device_id_type=MESH) for t in range(n)]
        for cp in own + cps:
            cp.start()
        for cp in cps:
            cp.wait_send()
        for t in range(n):
            pltpu.make_async_remote_copy(
                src_ref=ins[t], dst_ref=outs[t].at[:, 1 - c], send_sem=ssem.at[t], recv_sem=rsem.at[t],
                device_id=(x, y, 1 - c), device_id_type=MESH).wait_recv()
        for cp in own:
            cp.wait()

    return pl.pallas_call(
        body, name="grad_halves_exchange", in_specs=[ANY] * n, out_specs=[ANY] * n,
        out_shape=[jax.ShapeDtypeStruct((a.shape[0], 2) + a.shape[1:], a.dtype) for a in halves],
        scratch_shapes=[pltpu.SemaphoreType.DMA((n,))] * 3,
        compiler_params=pltpu.CompilerParams(vmem_limit_bytes=V7X_VMEM_LIMIT))(*halves)


def _small_allreduce(v):
    rows = v.shape[0]
    n_dev = 2 * N_CHIPS

    def body(v_ref, o_ref, buf, ssem, rsem):
        x, y, c, _ = _place()
        me = 4 * x + 2 * y + c
        buf[me] = v_ref[...]

        def copy(d, slot, to):
            return pltpu.make_async_remote_copy(
                src_ref=v_ref, dst_ref=buf.at[slot], send_sem=ssem.at[d - 1], recv_sem=rsem.at[d - 1],
                device_id=(to // 4, (to // 2) % 2, to % 2), device_id_type=MESH)

        cps = [copy(d, me, (me + d) % n_dev) for d in range(1, n_dev)]
        for cp in cps:
            cp.start()
        for d in range(1, n_dev):
            copy(d, (me + n_dev - d) % n_dev, me).wait_recv()
        for cp in cps:
            cp.wait_send()
        acc = buf[0]
        for i in range(1, n_dev):
            acc = acc + buf[i]
        o_ref[...] = acc

    vm = pl.BlockSpec(memory_space=pltpu.VMEM)
    return pl.pallas_call(
        body, name="small_allreduce", in_specs=[vm], out_specs=vm,
        out_shape=jax.ShapeDtypeStruct(v.shape, F32),
        scratch_shapes=[pltpu.VMEM((n_dev, rows, LANES), F32), pltpu.SemaphoreType.DMA((n_dev - 1,)),
                        pltpu.SemaphoreType.DMA((n_dev - 1,))],
        compiler_params=pltpu.CompilerParams(vmem_limit_bytes=V7X_VMEM_LIMIT))(v)


def _adamw(w, g, m, v):
    rows, cols = w.shape
    tr = rows
    for cand in range(8, rows + 1, 8):
        if rows % cand == 0 and cand * cols * 4 <= 2 ** 21:
            tr = cand

    def body(w_ref, g_ref, m_ref, v_ref, d_ref, m2_ref, v2_ref):
        g = g_ref[...]
        m2 = ADAM_B1 * m_ref[...] + (1.0 - ADAM_B1) * g
        v2 = ADAM_B2 * v_ref[...] + (1.0 - ADAM_B2) * (g * g)
        m_hat = m2 / (1.0 - ADAM_B1 ** ADAM_STEP)
        v_hat = v2 / (1.0 - ADAM_B2 ** ADAM_STEP)
        d_ref[...] = -ADAM_LR * (m_hat / (jnp.sqrt(v_hat) + ADAM_EPS) + ADAM_WD * w_ref[...])
        m2_ref[...] = m2
        v2_ref[...] = v2

    spec = pl.BlockSpec((tr, cols), lambda i: (i, 0))
    out = jax.ShapeDtypeStruct((rows, cols), F32)
    return pl.pallas_call(
        body, name="adamw", grid=(rows // tr,), in_specs=[spec] * 4, out_specs=[spec] * 3, out_shape=[out] * 3,
        compiler_params=_params(1))(w, g, m, v)


SMALL = ("norm_ffn1", "norm_mix", "sinks", "norm_out_sb", "norm_out_swa", "norm_ffn2", "rel_bias", "norm_final")
BIG = ("ffn1_gu", "ffn1_down", "w_in", "w_out", "ffn2_gu", "ffn2_down")


def _pack(parts):
    rows = []
    for a in parts:
        a = a.reshape(-1).astype(F32)
        rows.append(jnp.pad(a, (0, -a.shape[0] % LANES)).reshape(-1, LANES))
    out = jnp.concatenate(rows, axis=0)
    return jnp.pad(out, ((0, -out.shape[0] % 8), (0, 0)))


def _unpack(packed, like):
    out, r = [], 0
    for a in like:
        n = math.prod(a.shape)
        nr = -(-n // LANES)
        out.append(packed[r:r + nr].reshape(-1)[:n].reshape(a.shape))
        r += nr
    return out


def kernel(x, norm_ffn1, w_ffn1_gu, w_ffn1_down, norm_mix, w_in, sinks, norm_out_sb, norm_out_swa, w_out, norm_ffn2, w_ffn2_gu, w_ffn2_down, rel_bias, norm_final, loss_target, m_norm_ffn1, m_w_ffn1_gu, m_w_ffn1_down, m_norm_mix, m_w_in, m_sinks, m_norm_out_sb, m_norm_out_swa, m_w_out, m_norm_ffn2, m_w_ffn2_gu, m_w_ffn2_down, m_rel_bias, m_norm_final, v_norm_ffn1, v_w_ffn1_gu, v_w_ffn1_down, v_norm_mix, v_w_in, v_sinks, v_norm_out_sb, v_norm_out_swa, v_w_out, v_norm_ffn2, v_w_ffn2_gu, v_w_ffn2_down, v_rel_bias, v_norm_final):
    big_w = dict(ffn1_gu=w_ffn1_gu, ffn1_down=w_ffn1_down, w_in=w_in, w_out=w_out, ffn2_gu=w_ffn2_gu, ffn2_down=w_ffn2_down)
    big_m = dict(ffn1_gu=m_w_ffn1_gu, ffn1_down=m_w_ffn1_down, w_in=m_w_in, w_out=m_w_out, ffn2_gu=m_w_ffn2_gu, ffn2_down=m_w_ffn2_down)
    big_v = dict(ffn1_gu=v_w_ffn1_gu, ffn1_down=v_w_ffn1_down, w_in=v_w_in, w_out=v_w_out, ffn2_gu=v_w_ffn2_gu, ffn2_down=v_w_ffn2_down)
    small_w = dict(norm_ffn1=norm_ffn1, norm_mix=norm_mix, sinks=sinks, norm_out_sb=norm_out_sb, norm_out_swa=norm_out_swa,
                   norm_ffn2=norm_ffn2, rel_bias=rel_bias, norm_final=norm_final)
    small_m = dict(norm_ffn1=m_norm_ffn1, norm_mix=m_norm_mix, sinks=m_sinks, norm_out_sb=m_norm_out_sb,
                   norm_out_swa=m_norm_out_swa, norm_ffn2=m_norm_ffn2, rel_bias=m_rel_bias, norm_final=m_norm_final)
    small_v = dict(norm_ffn1=v_norm_ffn1, norm_mix=v_norm_mix, sinks=v_sinks, norm_out_sb=v_norm_out_sb,
                   norm_out_swa=v_norm_out_swa, norm_ffn2=v_norm_ffn2, rel_bias=v_rel_bias, norm_final=v_norm_final)
    d = D_MODEL

    def halves(a):
        _, r, c = a.shape
        return a.astype(BF16).reshape(DEPTH, 2, r // 2, c)

    gathered = _all_gather([halves(big_w[k]) for k in BIG])
    full = {k: a.reshape((DEPTH, N_CHIPS, a.shape[3] * 2, a.shape[4])) for k, a in zip(BIG, gathered)}
    wts = {
        "ffn1_gu": full["ffn1_gu"], "ffn2_gu": full["ffn2_gu"],
        "ffn1_down": full["ffn1_down"].reshape(DEPTH, D_FF, d), "ffn2_down": full["ffn2_down"].reshape(DEPTH, D_FF, d),
        "w_out": full["w_out"].reshape(DEPTH, d, d),
        "w_in": jnp.transpose(full["w_in"], (0, 2, 1, 3)).reshape(DEPTH, d, IN_W),
    }

    loss_row, dx, gw, gsmall = _local_step(x[0], loss_target[0], wts, small_w)

    _, _, c, me = _place()
    cm = jnp.stack([c, me]).astype(jnp.int32)
    stacks = {
        "ffn1_gu": gw["ffn1_gu"], "ffn2_gu": gw["ffn2_gu"],
        "ffn1_down": gw["ffn1_down"].reshape(DEPTH, N_CHIPS, D_FF // N_CHIPS, d),
        "ffn2_down": gw["ffn2_down"].reshape(DEPTH, N_CHIPS, D_FF // N_CHIPS, d),
        "w_out": gw["w_out"].reshape(DEPTH, N_CHIPS, d // N_CHIPS, d),
        "w_in": jnp.transpose(gw["w_in"].reshape(DEPTH, d, N_CHIPS, IN_W // N_CHIPS), (0, 2, 1, 3)),
    }
    g5 = [stacks[k].reshape(DEPTH, N_CHIPS, 2, stacks[k].shape[2] // 2, stacks[k].shape[3]) for k in BIG]
    from_sibling = _sibling_exchange(g5)
    sums = [_chip_sum(a, b, cm) for a, b in zip(g5, from_sibling)]
    landed = _chip_exchange([s[0] for s in sums])
    mine = [_total_sum(s[1], r) for s, r in zip(sums, landed)]
    reduced = _halves_exchange(mine)
    grads = {k: a.reshape(big_w[k].shape) for k, a in zip(BIG, reduced)}

    red = _small_allreduce(_pack([gsmall[k] for k in SMALL] + [loss_row[0, :1]]))
    small_like = [small_w[k] for k in SMALL]
    gs = _unpack(red, small_like + [loss_row[0, :1]])
    loss = gs[-1][0]
    gs = dict(zip(SMALL, gs[:-1]))

    out_g, out_d, out_m, out_v = {}, {}, {}, {}
    for k in BIG:
        shp = big_w[k].shape
        flat = lambda a: a.reshape(shp[0] * shp[1], shp[2])
        dlt, m2, v2 = _adamw(flat(big_w[k]), flat(grads[k]), flat(big_m[k]), flat(big_v[k]))
        out_g[k], out_d[k], out_m[k], out_v[k] = grads[k], dlt.reshape(shp), m2.reshape(shp), v2.reshape(shp)
    pk = lambda dct: _pack([dct[k] for k in SMALL])
    dlt, m2, v2 = _adamw(pk(small_w), pk(gs), pk(small_m), pk(small_v))
    for dst, packed in ((out_d, dlt), (out_m, m2), (out_v, v2)):
        dst.update(zip(SMALL, _unpack(packed, small_like)))
    out_g.update(gs)

    order = ("norm_ffn1", "ffn1_gu", "ffn1_down", "norm_mix", "w_in", "sinks", "norm_out_sb", "norm_out_swa", "w_out",
             "norm_ffn2", "ffn2_gu", "ffn2_down", "rel_bias", "norm_final")
    return (loss, dx.reshape(x.shape), *[out_g[k] for k in order], *[out_d[k] for k in order],
            *[out_m[k] for k in order], *[out_v[k] for k in order])
```

```python
import functools
import math

import numpy as np
import jax
import jax.numpy as jnp
from jax import lax
from jax.experimental import pallas as pl
from jax.experimental.pallas import tpu as pltpu

F32 = jnp.float32
BF16 = jnp.bfloat16

D_MODEL = 1024
DEPTH = 2
HEAD_DIM = 64
BLK = 128
N_BUCKETS = 32
MAX_DISTANCE = 128
D_FF = 2816
EPS = 1e-6
NEG_INF = -1e30
SB_W = 512
SWA_W = 512
KV_W = 128
IN_W = 2304
SCALE = HEAD_DIM ** -0.5
N_CHIPS = 4
FS = 2 * D_FF // N_CHIPS
LANES = 128
V7X_VMEM_LIMIT = 56 * 2 ** 20
TM = 512
SB_KT = 512

ADAM_LR = 0.001
ADAM_B1 = 0.9
ADAM_B2 = 0.999
ADAM_EPS = 1e-08
ADAM_WD = 0.01
ADAM_STEP = 10

MESH = pl.DeviceIdType.MESH
ANY = pl.BlockSpec(memory_space=pl.ANY)


def _params(n_grid):
    return pltpu.CompilerParams(dimension_semantics=("arbitrary",) * n_grid, vmem_limit_bytes=V7X_VMEM_LIMIT)


def _dot(a, b):
    return jnp.dot(a, b, preferred_element_type=F32)


def _dot_nt(a, b):
    return lax.dot_general(a, b, (((1,), (1,)), ((), ())), preferred_element_type=F32)


def _dot_tn(a, b):
    return lax.dot_general(a, b, (((0,), (0,)), ((), ())), preferred_element_type=F32)


def _rms_fwd(x, g):
    r = lax.rsqrt(jnp.mean(x * x, axis=-1, keepdims=True) + EPS)
    xh = x * r
    return xh * g, xh, r


def _rms_bwd(dy, xh, r, g):
    u = dy * g
    dx = r * (u - xh * jnp.mean(u * xh, axis=-1, keepdims=True))
    dg = jnp.sum(dy * xh, axis=0, keepdims=True)
    return dx, dg


def _softplus(z):
    return jnp.maximum(z, 0.0) + jnp.log(1.0 + jnp.exp(-jnp.abs(z)))


def _norm_cast(h, g):
    t, w = h.shape

    def body(h_ref, g_ref, n_ref):
        y, _, _ = _rms_fwd(h_ref[...], g_ref[...])
        n_ref[...] = y.astype(BF16)

    return pl.pallas_call(
        body, name="norm_cast", grid=(t // TM,),
        in_specs=[pl.BlockSpec((TM, w), lambda i: (i, 0)), pl.BlockSpec((1, w), lambda i: (0, 0))],
        out_specs=pl.BlockSpec((TM, w), lambda i: (i, 0)),
        out_shape=jax.ShapeDtypeStruct((t, w), BF16), compiler_params=_params(1))(h, g)


def _ffn_gu(n, wgu, l):
    t, d = n.shape

    def body(n_ref, wg_ref, wu_ref, gu_ref, act_ref):
        x = n_ref[...]
        g = _dot(x, wg_ref[...])
        u = _dot(x, wu_ref[...])
        gu_ref[0] = g.astype(BF16)
        gu_ref[1] = u.astype(BF16)
        act_ref[...] = (g * jax.nn.sigmoid(g) * u).astype(BF16)

    return pl.pallas_call(
        body, name="ffn_gu", grid=(2, t // TM),
        in_specs=[pl.BlockSpec((TM, d), lambda j, i: (i, 0)),
                  pl.BlockSpec((None, None, d, FS), lambda j, i: (l, j, 0, 0)),
                  pl.BlockSpec((None, None, d, FS), lambda j, i: (l, j + 2, 0, 0))],
        out_specs=[pl.BlockSpec((2, TM, FS), lambda j, i: (0, i, j)), pl.BlockSpec((TM, FS), lambda j, i: (i, j))],
        out_shape=[jax.ShapeDtypeStruct((2, t, D_FF), BF16), jax.ShapeDtypeStruct((t, D_FF), BF16)],
        compiler_params=_params(2))(n, wgu, wgu)


def _down_res(act, wdn, h, l):
    t, f = act.shape
    d = h.shape[1]

    def body(a_ref, w_ref, h_ref, o_ref):
        o_ref[...] = h_ref[...] + 0.5 * _dot(a_ref[...], w_ref[...])

    return pl.pallas_call(
        body, name="down_res", grid=(t // TM,),
        in_specs=[pl.BlockSpec((TM, f), lambda i: (i, 0)), pl.BlockSpec((None, f, d), lambda i: (l, 0, 0)),
                  pl.BlockSpec((TM, d), lambda i: (i, 0))],
        out_specs=pl.BlockSpec((TM, d), lambda i: (i, 0)),
        out_shape=jax.ShapeDtypeStruct((t, d), F32), compiler_params=_params(1))(act, wdn, h)


def _proj(n, w_in, l):
    t, d = n.shape
    w = w_in.shape[2]

    def body(n_ref, w_ref, o_ref):
        o_ref[...] = _dot(n_ref[...], w_ref[...]).astype(BF16)

    return pl.pallas_call(
        body, name="proj", grid=(t // TM,),
        in_specs=[pl.BlockSpec((TM, d), lambda i: (i, 0)), pl.BlockSpec((None, d, w), lambda i: (l, 0, 0))],
        out_specs=pl.BlockSpec((TM, w), lambda i: (i, 0)),
        out_shape=jax.ShapeDtypeStruct((t, w), BF16), compiler_params=_params(1))(n, w_in)


def _out_res(o_sb, o_sw, g_sb, g_sw, w_out, h, l):
    t, d = h.shape

    def body(a_ref, b_ref, ga_ref, gb_ref, w_ref, h_ref, o_ref, mix_ref):
        ya, _, _ = _rms_fwd(a_ref[...], ga_ref[...])
        yb, _, _ = _rms_fwd(b_ref[...], gb_ref[...])
        mixed = jnp.concatenate([ya.astype(BF16), yb.astype(BF16)], axis=1)
        mix_ref[...] = mixed
        o_ref[...] = h_ref[...] + _dot(mixed, w_ref[...])

    return pl.pallas_call(
        body, name="out_res", grid=(t // TM,),
        in_specs=[pl.BlockSpec((TM, SB_W), lambda i: (i, 0)), pl.BlockSpec((TM, SWA_W), lambda i: (i, 0)),
                  pl.BlockSpec((1, SB_W), lambda i: (0, 0)), pl.BlockSpec((1, SWA_W), lambda i: (0, 0)),
                  pl.BlockSpec((None, d, d), lambda i: (l, 0, 0)), pl.BlockSpec((TM, d), lambda i: (i, 0))],
        out_specs=[pl.BlockSpec((TM, d), lambda i: (i, 0)), pl.BlockSpec((TM, d), lambda i: (i, 0))],
        out_shape=[jax.ShapeDtypeStruct((t, d), F32), jax.ShapeDtypeStruct((t, d), BF16)],
        compiler_params=_params(1))(o_sb, o_sw, g_sb, g_sw, w_out, h)


def _loss_head(h, g, tgt):
    t, d = h.shape

    def body(h_ref, g_ref, t_ref, dh_ref, dg_ref, loss_ref):
        @pl.when(pl.program_id(0) == 0)
        def _():
            dg_ref[...] = jnp.zeros_like(dg_ref)
            loss_ref[...] = jnp.zeros_like(loss_ref)

        gg = g_ref[...]
        y, xh, r = _rms_fwd(h_ref[...], gg)
        err = y - t_ref[...]
        part = 0.5 * jnp.sum(jnp.sum(err * err, axis=1, keepdims=True) / d, axis=0, keepdims=True)
        loss_ref[...] += jnp.broadcast_to(part, loss_ref.shape)
        dx, dg = _rms_bwd(err / d, xh, r, gg)
        dh_ref[...] = dx
        dg_ref[...] += dg

    return pl.pallas_call(
        body, name="loss_head", grid=(t // TM,),
        in_specs=[pl.BlockSpec((TM, d), lambda i: (i, 0)), pl.BlockSpec((1, d), lambda i: (0, 0)),
                  pl.BlockSpec((TM, d), lambda i: (i, 0))],
        out_specs=[pl.BlockSpec((TM, d), lambda i: (i, 0)), pl.BlockSpec((1, d), lambda i: (0, 0)),
                   pl.BlockSpec((1, LANES), lambda i: (0, 0))],
        out_shape=[jax.ShapeDtypeStruct((t, d), F32), jax.ShapeDtypeStruct((1, d), F32),
                   jax.ShapeDtypeStruct((1, LANES), F32)],
        compiler_params=_params(1))(h, g, tgt)


def _ffn_dact(dh, wdn, gu, l):
    t, d = dh.shape

    def body(dh_ref, w_ref, gu_ref, o_ref):
        da = 0.5 * _dot_nt(dh_ref[...].astype(BF16), w_ref[...])
        g = gu_ref[0].astype(F32)
        u = gu_ref[1].astype(F32)
        sig = jax.nn.sigmoid(g)
        silu = g * sig
        o_ref[0] = (da * u * (sig * (1.0 + g * (1.0 - sig)))).astype(BF16)
        o_ref[1] = (da * silu).astype(BF16)

    return pl.pallas_call(
        body, name="ffn_dact", grid=(2, t // TM),
        in_specs=[pl.BlockSpec((TM, d), lambda j, i: (i, 0)), pl.BlockSpec((None, FS, d), lambda j, i: (l, j, 0)),
                  pl.BlockSpec((2, TM, FS), lambda j, i: (0, i, j))],
        out_specs=pl.BlockSpec((2, TM, FS), lambda j, i: (0, i, j)),
        out_shape=jax.ShapeDtypeStruct((2, t, D_FF), BF16), compiler_params=_params(2))(dh, wdn, gu)


def _dn_norm_bwd(a, a_spec, w, w_spec, nk, dh, h_in, g):
    t, d = dh.shape

    def body(a_ref, w_ref, dh_ref, h_ref, g_ref, o_ref, dg_ref, acc_ref):
        i, k = pl.program_id(0), pl.program_id(1)

        @pl.when(k == 0)
        def _():
            acc_ref[...] = jnp.zeros_like(acc_ref)

        acc_ref[...] += _dot_nt(a_ref[...], w_ref[...])

        @pl.when(k == nk - 1)
        def _():
            gg = g_ref[...]
            _, xh, r = _rms_fwd(h_ref[...], gg)
            dx, dg = _rms_bwd(acc_ref[...], xh, r, gg)
            o_ref[...] = dh_ref[...] + dx

            @pl.when(i == 0)
            def _():
                dg_ref[...] = dg

            @pl.when(i > 0)
            def _():
                dg_ref[...] += dg

    row = pl.BlockSpec((TM, d), lambda i, k: (i, 0))
    return pl.pallas_call(
        body, name="dn_norm_bwd", grid=(t // TM, nk),
        in_specs=[a_spec, w_spec, row, row, pl.BlockSpec((1, d), lambda i, k: (0, 0))],
        out_specs=[row, pl.BlockSpec((1, d), lambda i, k: (0, 0))],
        out_shape=[jax.ShapeDtypeStruct((t, d), F32), jax.ShapeDtypeStruct((1, d), F32)],
        scratch_shapes=[pltpu.VMEM((TM, d), F32)], compiler_params=_params(2))(a, w, dh, h_in, g)


def _ffn_dn(dgu, wgu, dh, h_in, g, l):
    d = dh.shape[1]
    return _dn_norm_bwd(
        dgu, pl.BlockSpec((None, TM, FS), lambda i, k: (k // 2, i, k % 2)),
        wgu, pl.BlockSpec((None, None, d, FS), lambda i, k: (l, k, 0, 0)), N_CHIPS, dh, h_in, g)


def _mix_dn(dproj, w_in, dh, h_in, g, l):
    d = dh.shape[1]
    w = dproj.shape[1]
    return _dn_norm_bwd(
        dproj, pl.BlockSpec((TM, w), lambda i, k: (i, 0)),
        w_in, pl.BlockSpec((None, d, w), lambda i, k: (l, 0, 0)), 1, dh, h_in, g)


def _dmixed(dh, w_out, o_sb, o_sw, g_sb, g_sw, l):
    t, d = dh.shape

    def body(dh_ref, w_ref, a_ref, b_ref, ga_ref, gb_ref, o_ref, dga_ref, dgb_ref):
        i = pl.program_id(0)
        dm = _dot_nt(dh_ref[...].astype(BF16), w_ref[...])
        _, xa, ra = _rms_fwd(a_ref[...], ga_ref[...])
        _, xb, rb = _rms_fwd(b_ref[...], gb_ref[...])
        da, dga = _rms_bwd(dm[:, :SB_W], xa, ra, ga_ref[...])
        db, dgb = _rms_bwd(dm[:, SB_W:], xb, rb, gb_ref[...])
        o_ref[...] = jnp.concatenate([da.astype(BF16), db.astype(BF16)], axis=1)

        @pl.when(i == 0)
        def _():
            dga_ref[...] = dga
            dgb_ref[...] = dgb

        @pl.when(i > 0)
        def _():
            dga_ref[...] += dga
            dgb_ref[...] += dgb

    return pl.pallas_call(
        body, name="dmixed", grid=(t // TM,),
        in_specs=[pl.BlockSpec((TM, d), lambda i: (i, 0)), pl.BlockSpec((None, d, d), lambda i: (l, 0, 0)),
                  pl.BlockSpec((TM, SB_W), lambda i: (i, 0)), pl.BlockSpec((TM, SWA_W), lambda i: (i, 0)),
                  pl.BlockSpec((1, SB_W), lambda i: (0, 0)), pl.BlockSpec((1, SWA_W), lambda i: (0, 0))],
        out_specs=[pl.BlockSpec((TM, d), lambda i: (i, 0)), pl.BlockSpec((1, SB_W), lambda i: (0, 0)),
                   pl.BlockSpec((1, SWA_W), lambda i: (0, 0))],
        out_shape=[jax.ShapeDtypeStruct((t, d), BF16), jax.ShapeDtypeStruct((1, SB_W), F32),
                   jax.ShapeDtypeStruct((1, SWA_W), F32)],
        compiler_params=_params(1))(dh, w_out, o_sb, o_sw, g_sb, g_sw)


def _wgrad(name, a, a_spec, b, b_spec, grid, out_shape, out_spec, scale, prev):
    def body(*refs):
        a_ref, b_ref, o_ref = refs[0], refs[1], refs[-1]
        r = _dot_tn(a_ref[...], b_ref[...].astype(BF16))
        o_ref[...] = r if scale == 1.0 else scale * r

    ins, specs, alias = [a, b], [a_spec, b_spec], {}
    if prev is not None:
        ins.append(prev)
        specs.append(ANY)
        alias = {2: 0}
    return pl.pallas_call(
        body, name=name, grid=grid, in_specs=specs, out_specs=out_spec,
        out_shape=jax.ShapeDtypeStruct(out_shape, F32), input_output_aliases=alias,
        compiler_params=_params(len(grid)))(*ins)


def _wgrad_gu(n, dgu, l, prev):
    t, d = n.shape
    return _wgrad(
        "wgrad_gu", n, pl.BlockSpec((t, TM), lambda s, r: (0, r)),
        dgu, pl.BlockSpec((None, t, FS), lambda s, r: (s // 2, 0, s % 2)), (N_CHIPS, d // TM),
        (DEPTH, N_CHIPS, d, FS), pl.BlockSpec((None, None, TM, FS), lambda s, r: (l, s, r, 0)), 1.0, prev)


def _wgrad_down(act, dh, l, prev):
    t, d = dh.shape
    return _wgrad(
        "wgrad_down", act, pl.BlockSpec((t, FS), lambda s, r: (0, s)), dh, pl.BlockSpec((t, TM), lambda s, r: (0, r)),
        (2, d // TM), (DEPTH, D_FF, d), pl.BlockSpec((None, FS, TM), lambda s, r: (l, s, r)), 0.5, prev)


def _wgrad_out(mixed, dh, l, prev):
    t, d = dh.shape
    return _wgrad(
        "wgrad_out", mixed, pl.BlockSpec((t, TM), lambda s: (0, s)), dh, pl.BlockSpec((t, d), lambda s: (0, 0)),
        (d // TM,), (DEPTH, d, d), pl.BlockSpec((None, TM, d), lambda s: (l, s, 0)), 1.0, prev)


def _wgrad_in(n, dproj, l, prev):
    t, d = n.shape
    w = dproj.shape[1]
    tw = w // 3
    return _wgrad(
        "wgrad_in", n, pl.BlockSpec((t, d), lambda s: (0, 0)), dproj, pl.BlockSpec((t, tw), lambda s: (0, s)),
        (3,), (DEPTH, d, w), pl.BlockSpec((None, d, tw), lambda s: (l, 0, s)), 1.0, prev)


def _tri(rel):
    row = lax.broadcasted_iota(jnp.int32, (BLK, BLK), 0)
    col = lax.broadcasted_iota(jnp.int32, (BLK, BLK), 1)
    m = rel(row, col).astype(BF16)
    return jnp.concatenate([m, m], axis=0)


def _scan_dot(x, tri2):
    hi = x.astype(BF16)
    lo = (x - hi.astype(F32)).astype(BF16)
    return _dot(jnp.concatenate([hi, lo], axis=1), tri2)


def _head_masks():
    lane = lax.broadcasted_iota(jnp.int32, (1, LANES), 1)
    return [lane < HEAD_DIM, lane >= HEAD_DIM]


def _sb_fwd(proj):
    t = proj.shape[0]
    nq = t // BLK
    nb = SB_KT // BLK

    def body(q_ref, k_ref, v_ref, o_ref, tot_ref):
        hm = _head_masks()
        dcol = lax.broadcasted_iota(jnp.int32, (BLK, SB_KT), 1) - lax.broadcasted_iota(jnp.int32, (BLK, SB_KT), 0)
        after = _tri(lambda r, c: r > c)

        def tile(qh, kt, carry, acc, limit):
            ks = pl.ds(pl.multiple_of(kt * SB_KT, SB_KT), SB_KT)
            z = _dot_nt(qh, k_ref[ks, :])
            sp = _softplus(z)
            valid = None if limit is None else dcol < limit
            spm = sp if valid is None else jnp.where(valid, sp, 0.0)
            sufs = [None] * nb
            for b in reversed(range(nb)):
                blk = spm[:, b * BLK:(b + 1) * BLK]
                sufs[b] = carry + _scan_dot(blk, after)
                carry = carry + jnp.sum(blk, axis=1, keepdims=True)
            w = jnp.exp(z - sp - jnp.concatenate(sufs, axis=1))
            if valid is not None:
                w = jnp.where(valid, w, 0.0)
            return carry, acc + _dot(w.astype(BF16), v_ref[ks, :])

        def qblock(qi, _):
            qs = pl.ds(pl.multiple_of(qi * BLK, BLK), BLK)
            q = q_ref[qs, :] * SCALE
            kd = qi // nb
            limit = (qi - kd * nb) * BLK
            res = []
            for h in range(2):
                qh = jnp.where(hm[h], q, jnp.zeros_like(q))
                c0 = tile(qh, kd, jnp.zeros((BLK, 1), F32), jnp.zeros((BLK, LANES), F32), limit)
                res.append(lax.fori_loop(0, kd, lambda n, c: tile(qh, kd - 1 - n, c[0], c[1], None), c0))
            o_ref[qs, :] = jnp.where(hm[0], res[0][1], res[1][1])
            for h in range(2):
                tot_ref[h, qs, :] = jnp.broadcast_to(res[h][0], (BLK, LANES))
            return 0

        lax.fori_loop(0, nq, qblock, 0)

    col_blk = lambda off: pl.BlockSpec((t, LANES), lambda p: (0, off + p))
    return pl.pallas_call(
        body, name="sb_fwd", grid=(4,), in_specs=[col_blk(0), col_blk(4), col_blk(8)],
        out_specs=[pl.BlockSpec((t, LANES), lambda p: (0, p)), pl.BlockSpec((2, t, LANES), lambda p: (p, 0, 0))],
        out_shape=[jax.ShapeDtypeStruct((t, SB_W), F32), jax.ShapeDtypeStruct((8, t, LANES), F32)],
        compiler_params=_params(1))(proj, proj, proj)


def _sb_bwd(proj, d_o, tot):
    t = proj.shape[0]
    nq = t // BLK
    nb = SB_KT // BLK

    def body(q_ref, k_ref, v_ref, do_ref, tot_ref, dq_ref, dk_ref, dv_ref, dk_acc, dv_acc):
        hm = _head_masks()
        dcol = lax.broadcasted_iota(jnp.int32, (BLK, SB_KT), 1) - lax.broadcasted_iota(jnp.int32, (BLK, SB_KT), 0)
        before = _tri(lambda r, c: r < c)
        upto = _tri(lambda r, c: r <= c)
        dk_acc[...] = jnp.zeros_like(dk_acc)
        dv_acc[...] = jnp.zeros_like(dv_acc)

        def tile(qh, doh, tt, kt, pre, ecum, dq, limit):
            ks = pl.ds(pl.multiple_of(kt * SB_KT, SB_KT), SB_KT)
            k = k_ref[ks, :]
            v = v_ref[ks, :]
            z = _dot_nt(qh, k)
            sp = _softplus(z)
            valid = None if limit is None else dcol < limit
            spm = sp if valid is None else jnp.where(valid, sp, 0.0)
            pres = []
            for b in range(nb):
                blk = spm[:, b * BLK:(b + 1) * BLK]
                pres.append(pre + _scan_dot(blk, before))
                pre = pre + jnp.sum(blk, axis=1, keepdims=True)
            logw = z - (tt - jnp.concatenate(pres, axis=1))
            if valid is not None:
                logw = jnp.minimum(logw, 0.0)
            w = jnp.exp(logw)
            if valid is not None:
                w = jnp.where(valid, w, 0.0)
            e = w * _dot_nt(doh, v)
            incs = []
            for b in range(nb):
                blk = e[:, b * BLK:(b + 1) * BLK]
                incs.append(ecum + _scan_dot(blk, upto))
                ecum = ecum + jnp.sum(blk, axis=1, keepdims=True)
            dz = e - jnp.exp(z - sp) * jnp.concatenate(incs, axis=1)
            if valid is not None:
                dz = jnp.where(valid, dz, 0.0)
            dzb = dz.astype(BF16)
            dk_acc[ks, :] += _dot_tn(dzb, qh)
            dv_acc[ks, :] += _dot_tn(w.astype(BF16), doh)
            return pre, ecum, dq + _dot(dzb, k)

        def qblock(qi, _):
            qs = pl.ds(pl.multiple_of(qi * BLK, BLK), BLK)
            q = q_ref[qs, :] * SCALE
            do = do_ref[qs, :]
            kd = qi // nb
            limit = (qi - kd * nb) * BLK
            res = []
            for h in range(2):
                qh = jnp.where(hm[h], q, jnp.zeros_like(q))
                doh = jnp.where(hm[h], do, jnp.zeros_like(do))
                tt = tot_ref[h, qs, 0:1]
                c0 = (jnp.zeros((BLK, 1), F32), jnp.zeros((BLK, 1), F32), jnp.zeros((BLK, LANES), F32))
                c = lax.fori_loop(0, kd, lambda kt, c: tile(qh, doh, tt, kt, c[0], c[1], c[2], None), c0)
                res.append(tile(qh, doh, tt, kd, c[0], c[1], c[2], limit)[2])
            dq_ref[qs, :] = (jnp.where(hm[0], res[0], res[1]) * SCALE).astype(BF16)
            return 0

        lax.fori_loop(0, nq, qblock, 0)
        dk_ref[...] = dk_acc[...].astype(BF16)
        dv_ref[...] = dv_acc[...].astype(BF16)

    col_blk = lambda off: pl.BlockSpec((t, LANES), lambda p: (0, off + p))
    out = jax.ShapeDtypeStruct((t, SB_W), BF16)
    return pl.pallas_call(
        body, name="sb_bwd", grid=(4,),
        in_specs=[col_blk(0), col_blk(4), col_blk(8), col_blk(0), pl.BlockSpec((2, t, LANES), lambda p: (p, 0, 0))],
        out_specs=[col_blk(0), col_blk(0), col_blk(0)], out_shape=[out, out, out],
        scratch_shapes=[pltpu.VMEM((t, LANES), F32), pltpu.VMEM((t, LANES), F32)],
        compiler_params=_params(1))(proj, proj, proj, d_o, tot)


def _bucket_table():
    a = np.arange(BLK)[:, None]
    c = np.arange(2 * BLK)[None, :]
    dist = np.maximum(BLK + a - c, 0)
    max_exact = N_BUCKETS // 2
    dd = np.maximum(dist, 1).astype(np.float32)
    large = max_exact + (np.log(dd / max_exact) / math.log(MAX_DISTANCE / max_exact)
                         * (N_BUCKETS - max_exact)).astype(np.int32)
    large = np.minimum(large, N_BUCKETS - 1)
    return np.where(dist < max_exact, dist, large).astype(np.int32)


def _swa_masks():
    row = lax.broadcasted_iota(jnp.int32, (BLK, BLK), 0)
    col = lax.broadcasted_iota(jnp.int32, (BLK, BLK), 1)
    return col <= row, col > row


def _to_kv_lanes(x, hl, kvh, kvmask):
    x = x.astype(F32)
    if hl != kvh:
        x = pltpu.roll(x, HEAD_DIM, 1)
    return jnp.where(kvmask, x, 0.0).astype(BF16)


def _swa_fwd(proj, bias, sinks_b):
    t = proj.shape[0]
    nq = t // BLK

    def body(q_ref, k_ref, v_ref, bias_ref, sink_ref, o_ref, lse_ref):
        hm = _head_masks()
        m_own, m_prev = _swa_masks()

        def head(qs, ps, pair, hl, prev):
            hq = 2 * pair + hl
            kvh = hq // 4
            qh = _to_kv_lanes(q_ref[qs, pair * LANES:(pair + 1) * LANES], hl, kvh, hm[kvh])
            sink = sink_ref[hq:hq + 1, 0:1]
            s_c = jnp.where(m_own, _dot_nt(qh, k_ref[qs, :]) * SCALE + bias_ref[hq, :, BLK:], NEG_INF)
            m = jnp.maximum(jnp.max(s_c, axis=1, keepdims=True), sink)
            if prev:
                s_p = jnp.where(m_prev, _dot_nt(qh, k_ref[ps, :]) * SCALE + bias_ref[hq, :, :BLK], NEG_INF)
                m = jnp.maximum(m, jnp.max(s_p, axis=1, keepdims=True))
            p_c = jnp.exp(s_c - m)
            den = jnp.sum(p_c, axis=1, keepdims=True) + jnp.exp(sink - m)
            if prev:
                p_p = jnp.exp(s_p - m)
                den = den + jnp.sum(p_p, axis=1, keepdims=True)
            inv = 1.0 / den
            o = _dot((p_c * inv).astype(BF16), v_ref[qs, :])
            if prev:
                o = o + _dot((p_p * inv).astype(BF16), v_ref[ps, :])
            if hl != kvh:
                o = pltpu.roll(o, HEAD_DIM, 1)
            lse_ref[hq, qs, :] = jnp.broadcast_to(m + jnp.log(den), (BLK, LANES))
            return o

        def qblock(i, pair, prev):
            qs = pl.ds(pl.multiple_of(i * BLK, BLK), BLK)
            ps = pl.ds(pl.multiple_of(jnp.maximum(i - 1, 0) * BLK, BLK), BLK)
            o = [head(qs, ps, pair, hl, prev) for hl in range(2)]
            o_ref[qs, pair * LANES:(pair + 1) * LANES] = jnp.where(hm[0], o[0], o[1])

        for pair in range(4):
            qblock(0, pair, False)

            def step(i, _):
                qblock(i, pair, True)
                return 0

            lax.fori_loop(1, nq, step, 0)

    return pl.pallas_call(
        body, name="swa_fwd", grid=(1,),
        in_specs=[pl.BlockSpec((t, SWA_W), lambda i: (0, 3)), pl.BlockSpec((t, KV_W), lambda i: (0, 16)),
                  pl.BlockSpec((t, KV_W), lambda i: (0, 17)), pl.BlockSpec((8, BLK, 2 * BLK), lambda i: (0, 0, 0)),
                  pl.BlockSpec((8, LANES), lambda i: (0, 0))],
        out_specs=[pl.BlockSpec((t, SWA_W), lambda i: (0, 0)), pl.BlockSpec((8, t, LANES), lambda i: (0, 0, 0))],
        out_shape=[jax.ShapeDtypeStruct((t, SWA_W), F32), jax.ShapeDtypeStruct((8, t, LANES), F32)],
        compiler_params=_params(1))(proj, proj, proj, bias, sinks_b)


def _swa_bwd(proj, d_o, lse, bias, sinks_b, dbias_in):
    t = proj.shape[0]
    nq = t // BLK

    def body(q_ref, k_ref, v_ref, do_ref, lse_ref, bias_ref, sink_ref, dbi_ref,
             dq_ref, dk_ref, dv_ref, dsink_ref, dbias_ref, dk_acc, dv_acc):
        hm = _head_masks()
        m_own, m_prev = _swa_masks()
        dk_acc[...] = jnp.zeros_like(dk_acc)
        dv_acc[...] = jnp.zeros_like(dv_acc)
        dbias_ref[...] = dbi_ref[...]

        def head(qs, ps, pair, hl, prev, dsink):
            hq = 2 * pair + hl
            kvh = hq // 4
            lanes = slice(pair * LANES, (pair + 1) * LANES)
            qh = _to_kv_lanes(q_ref[qs, lanes], hl, kvh, hm[kvh])
            doh = _to_kv_lanes(do_ref[qs, lanes], hl, kvh, hm[kvh])
            sink = sink_ref[hq:hq + 1, 0:1]
            lse_h = lse_ref[hq, qs, :]
            kc = k_ref[qs, :]
            vc = v_ref[qs, :]
            p_c = jnp.exp(jnp.where(m_own, _dot_nt(qh, kc) * SCALE + bias_ref[hq, :, BLK:], NEG_INF) - lse_h)
            dp_c = _dot_nt(doh, vc)
            delta = jnp.sum(p_c * dp_c, axis=1, keepdims=True)
            if prev:
                kp = k_ref[ps, :]
                vp = v_ref[ps, :]
                p_p = jnp.exp(jnp.where(m_prev, _dot_nt(qh, kp) * SCALE + bias_ref[hq, :, :BLK], NEG_INF) - lse_h)
                dp_p = _dot_nt(doh, vp)
                delta = delta + jnp.sum(p_p * dp_p, axis=1, keepdims=True)
            ds_c = p_c * (dp_c - delta)
            dsink = dsink - jnp.sum(jnp.exp(sink - lse_h[:, 0:1]) * delta, axis=0, keepdims=True)
            dbias_ref[hq, :, BLK:] += ds_c
            ds_cb = ds_c.astype(BF16)
            dq = _dot(ds_cb, kc)
            dk_acc[qs, :] += _dot_tn(ds_cb, qh) * SCALE
            dv_acc[qs, :] += _dot_tn(p_c.astype(BF16), doh)
            if prev:
                ds_p = p_p * (dp_p - delta)
                dbias_ref[hq, :, :BLK] += ds_p
                ds_pb = ds_p.astype(BF16)
                dq = dq + _dot(ds_pb, kp)
                dk_acc[ps, :] += _dot_tn(ds_pb, qh) * SCALE
                dv_acc[ps, :] += _dot_tn(p_p.astype(BF16), doh)
            dq = dq * SCALE
            if hl != kvh:
                dq = pltpu.roll(dq, HEAD_DIM, 1)
            return dq, dsink

        def qblock(i, pair, prev, dsinks):
            qs = pl.ds(pl.multiple_of(i * BLK, BLK), BLK)
            ps = pl.ds(pl.multiple_of(jnp.maximum(i - 1, 0) * BLK, BLK), BLK)
            r = [head(qs, ps, pair, hl, prev, dsinks[hl]) for hl in range(2)]
            dq_ref[qs, pair * LANES:(pair + 1) * LANES] = jnp.where(hm[0], r[0][0], r[1][0]).astype(BF16)
            return r[0][1], r[1][1]

        for pair in range(4):
            ds0 = qblock(0, pair, False, (jnp.zeros((1, 1), F32), jnp.zeros((1, 1), F32)))
            ds = lax.fori_loop(1, nq, lambda i, c: qblock(i, pair, True, c), ds0)
            for hl in range(2):
                dsink_ref[2 * pair + hl:2 * pair + hl + 1, :] = jnp.broadcast_to(ds[hl], (1, LANES))

        dk_ref[...] = dk_acc[...].astype(BF16)
        dv_ref[...] = dv_acc[...].astype(BF16)

    full3 = pl.BlockSpec((8, BLK, 2 * BLK), lambda i: (0, 0, 0))
    kv = jax.ShapeDtypeStruct((t, KV_W), BF16)
    return pl.pallas_call(
        body, name="swa_bwd", grid=(1,),
        in_specs=[pl.BlockSpec((t, SWA_W), lambda i: (0, 3)), pl.BlockSpec((t, KV_W), lambda i: (0, 16)),
                  pl.BlockSpec((t, KV_W), lambda i: (0, 17)), pl.BlockSpec((t, SWA_W), lambda i: (0, 1)),
                  pl.BlockSpec((8, t, LANES), lambda i: (0, 0, 0)), full3, pl.BlockSpec((8, LANES), lambda i: (0, 0)),
                  full3],
        out_specs=[pl.BlockSpec((t, SWA_W), lambda i: (0, 0)), pl.BlockSpec((t, KV_W), lambda i: (0, 0)),
                   pl.BlockSpec((t, KV_W), lambda i: (0, 0)), pl.BlockSpec((8, LANES), lambda i: (0, 0)), full3],
        out_shape=[jax.ShapeDtypeStruct((t, SWA_W), BF16), kv, kv, jax.ShapeDtypeStruct((8, LANES), F32),
                   jax.ShapeDtypeStruct((8, BLK, 2 * BLK), F32)],
        scratch_shapes=[pltpu.VMEM((t, KV_W), F32), pltpu.VMEM((t, KV_W), F32)],
        compiler_params=_params(1))(proj, proj, proj, d_o, lse, bias, sinks_b, dbias_in)


def _bias_table(rel_bias, buckets):
    def body(rb_ref, b_ref, o_ref):
        bk = b_ref[...]
        for h in range(8):
            acc = jnp.zeros((BLK, 2 * BLK), F32)
            for b in range(N_BUCKETS):
                acc = jnp.where(bk == b, rb_ref[b, h], acc)
            o_ref[h] = acc

    return pl.pallas_call(
        body, name="bias_table", grid=(1,),
        in_specs=[pl.BlockSpec(memory_space=pltpu.SMEM), pl.BlockSpec((BLK, 2 * BLK), lambda i: (0, 0))],
        out_specs=pl.BlockSpec((8, BLK, 2 * BLK), lambda i: (0, 0, 0)),
        out_shape=jax.ShapeDtypeStruct((8, BLK, 2 * BLK), F32), compiler_params=_params(1))(rel_bias, buckets)


def _bias_grad(dbias, buckets):
    def body(d_ref, b_ref, o_ref):
        lane = lax.broadcasted_iota(jnp.int32, (1, LANES), 1)
        bk = b_ref[...]
        for h in range(8):
            d = d_ref[h]
            acc = jnp.zeros((1, LANES), F32)
            for b in range(N_BUCKETS):
                s = jnp.sum(jnp.sum(jnp.where(bk == b, d, 0.0), axis=0, keepdims=True), axis=1, keepdims=True)
                acc = acc + jnp.where(lane == b, s, 0.0)
            o_ref[h:h + 1, :] = acc

    return pl.pallas_call(
        body, name="bias_grad", grid=(1,),
        in_specs=[pl.BlockSpec((8, BLK, 2 * BLK), lambda i: (0, 0, 0)), pl.BlockSpec((BLK, 2 * BLK), lambda i: (0, 0))],
        out_specs=pl.BlockSpec((8, LANES), lambda i: (0, 0)),
        out_shape=jax.ShapeDtypeStruct((8, LANES), F32), compiler_params=_params(1))(dbias, buckets)


def _row(a):
    return a.reshape(1, -1)


def _local_step(x, tgt, wts, small):
    buckets = jnp.asarray(_bucket_table())
    bias = _bias_table(small["rel_bias"], buckets)
    saved = []
    h = x
    for l in range(DEPTH):
        s = {"h0": h}
        s["n1"] = _norm_cast(h, _row(small["norm_ffn1"][l]))
        s["gu1"], s["act1"] = _ffn_gu(s["n1"], wts["ffn1_gu"], l)
        h = _down_res(s["act1"], wts["ffn1_down"], h, l)
        s["h1"] = h
        s["nm"] = _norm_cast(h, _row(small["norm_mix"][l]))
        s["proj"] = _proj(s["nm"], wts["w_in"], l)
        s["sinks_b"] = jnp.broadcast_to(small["sinks"][l][:, None], (8, LANES))
        s["o_sb"], s["tot"] = _sb_fwd(s["proj"])
        s["o_sw"], s["lse"] = _swa_fwd(s["proj"], bias, s["sinks_b"])
        h, s["mixed"] = _out_res(s["o_sb"], s["o_sw"], _row(small["norm_out_sb"][l]), _row(small["norm_out_swa"][l]),
                                 wts["w_out"], h, l)
        s["h2"] = h
        s["n2"] = _norm_cast(h, _row(small["norm_ffn2"][l]))
        s["gu2"], s["act2"] = _ffn_gu(s["n2"], wts["ffn2_gu"], l)
        h = _down_res(s["act2"], wts["ffn2_down"], h, l)
        saved.append(s)

    dh, dg_final, loss = _loss_head(h, _row(small["norm_final"]), tgt)

    gw = {k: None for k in ("ffn1_gu", "ffn1_down", "w_in", "w_out", "ffn2_gu", "ffn2_down")}
    gs = {k: [None] * DEPTH for k in ("norm_ffn1", "norm_mix", "sinks", "norm_out_sb", "norm_out_swa", "norm_ffn2")}
    dbias = jnp.zeros((8, BLK, 2 * BLK), F32)
    for l in reversed(range(DEPTH)):
        s = saved[l]
        dgu = _ffn_dact(dh, wts["ffn2_down"], s["gu2"], l)
        gw["ffn2_down"] = _wgrad_down(s["act2"], dh, l, gw["ffn2_down"])
        gw["ffn2_gu"] = _wgrad_gu(s["n2"], dgu, l, gw["ffn2_gu"])
        dh, gs["norm_ffn2"][l] = _ffn_dn(dgu, wts["ffn2_gu"], dh, s["h2"], _row(small["norm_ffn2"][l]), l)
        gw["w_out"] = _wgrad_out(s["mixed"], dh, l, gw["w_out"])
        d_o, gs["norm_out_sb"][l], gs["norm_out_swa"][l] = _dmixed(
            dh, wts["w_out"], s["o_sb"], s["o_sw"], _row(small["norm_out_sb"][l]), _row(small["norm_out_swa"][l]), l)
        dq_sb, dk_sb, dv_sb = _sb_bwd(s["proj"], d_o, s["tot"])
        dq_sw, dk_sw, dv_sw, dsink, dbias = _swa_bwd(s["proj"], d_o, s["lse"], bias, s["sinks_b"], dbias)
        gs["sinks"][l] = dsink[:, 0]
        dproj = jnp.concatenate([dq_sb, dk_sb, dv_sb, dq_sw, dk_sw, dv_sw], axis=1)
        gw["w_in"] = _wgrad_in(s["nm"], dproj, l, gw["w_in"])
        dh, gs["norm_mix"][l] = _mix_dn(dproj, wts["w_in"], dh, s["h1"], _row(small["norm_mix"][l]), l)
        dgu = _ffn_dact(dh, wts["ffn1_down"], s["gu1"], l)
        gw["ffn1_down"] = _wgrad_down(s["act1"], dh, l, gw["ffn1_down"])
        gw["ffn1_gu"] = _wgrad_gu(s["n1"], dgu, l, gw["ffn1_gu"])
        dh, gs["norm_ffn1"][l] = _ffn_dn(dgu, wts["ffn1_gu"], dh, s["h0"], _row(small["norm_ffn1"][l]), l)

    gsmall = {k: jnp.stack([a.reshape(-1) for a in v]) for k, v in gs.items()}
    gsmall["rel_bias"] = jnp.transpose(_bias_grad(dbias, buckets)[:, :N_BUCKETS])
    gsmall["norm_final"] = dg_final.reshape(-1)
    return loss, dh, gw, gsmall


def _place():
    x, y, c = lax.axis_index("x"), lax.axis_index("y"), lax.axis_index("c")
    return x, y, c, 2 * x + y


def _chip_core(k, c):
    return (k // 2, k % 2, c)


def _place_own(w, me1):
    _, rows, cols = w.shape
    tr = _rows_per_block(rows, cols, 1)

    def body(me_ref, w_ref, o_ref):
        o_ref[...] = w_ref[...].astype(BF16)

    return pl.pallas_call(
        body, name="place_own",
        grid_spec=pltpu.PrefetchScalarGridSpec(
            num_scalar_prefetch=1, grid=(DEPTH, rows // tr),
            in_specs=[pl.BlockSpec((None, tr, cols), lambda l, r, me: (l, r, 0))],
            out_specs=pl.BlockSpec((None, None, tr, cols), lambda l, r, me: (l, me[0], r, 0))),
        out_shape=jax.ShapeDtypeStruct((DEPTH, N_CHIPS, rows, cols), BF16), compiler_params=_params(2))(me1, w)


def _all_gather(bufs):
    n = len(bufs)

    def body(*refs):
        outs = refs[n:2 * n]
        ici_s, ici_r, d2d_s, d2d_r = refs[2 * n:]
        x, y, c, me = _place()
        sib = (x, y, 1 - c)
        groups = [(t, l) for t in range(n) for l in range(DEPTH)]

        def ici(t, l, j, to):
            s = (t * DEPTH + l) * 3 + j
            return pltpu.make_async_remote_copy(
                src_ref=outs[t].at[l, me, c], dst_ref=outs[t].at[l, me, c], send_sem=ici_s.at[s], recv_sem=ici_r.at[s],
                device_id=to, device_id_type=MESH)

        def landed(t, l, j, half):
            src = (me + 3 - j) % N_CHIPS
            return outs[t].at[l, src, half]

        def d2d(t, l, j, half):
            s = (t * DEPTH + l) * 3 + j
            return pltpu.make_async_remote_copy(
                src_ref=landed(t, l, j, half), dst_ref=landed(t, l, j, half), send_sem=d2d_s.at[s],
                recv_sem=d2d_r.at[s], device_id=sib, device_id_type=MESH)

        sends = [ici(t, l, j, _chip_core((me + 1 + j) % N_CHIPS, c)) for t, l in groups for j in range(3)]
        for cp in sends:
            cp.start()
        passed = []
        for t, l in groups:
            for j in range(3):
                s = (t * DEPTH + l) * 3 + j
                pltpu.make_async_remote_copy(
                    src_ref=landed(t, l, j, c), dst_ref=landed(t, l, j, c), send_sem=ici_s.at[s],
                    recv_sem=ici_r.at[s], device_id=sib, device_id_type=MESH).wait_recv()
                passed.append(d2d(t, l, j, c))
                passed[-1].start()
        for t, l in groups:
            for j in range(3):
                d2d(t, l, j, 1 - c).wait_recv()
        for cp in sends + passed:
            cp.wait_send()

    n_cp = n * DEPTH * 3
    return pl.pallas_call(
        body, name="all_gather_weights", in_specs=[ANY] * n, out_specs=[ANY] * n,
        out_shape=[jax.ShapeDtypeStruct(a.shape, a.dtype) for a in bufs],
        input_output_aliases={t: t for t in range(n)},
        scratch_shapes=[pltpu.SemaphoreType.DMA((n_cp,))] * 4,
        compiler_params=pltpu.CompilerParams(vmem_limit_bytes=V7X_VMEM_LIMIT))(*bufs)


def _sibling_exchange(grads):
    n = len(grads)

    def body(*refs):
        ins, outs = refs[:n], refs[n:2 * n]
        ssem, rsem = refs[2 * n:]
        x, y, c, _ = _place()
        cps = [pltpu.make_async_remote_copy(
            src_ref=ins[t].at[:, :, 1 - c], dst_ref=outs[t], send_sem=ssem.at[t], recv_sem=rsem.at[t],
            device_id=(x, y, 1 - c), device_id_type=MESH) for t in range(n)]
        for cp in cps:
            cp.start()
        for cp in cps:
            cp.wait()

    return pl.pallas_call(
        body, name="grad_sibling_exchange", in_specs=[ANY] * n, out_specs=[ANY] * n,
        out_shape=[jax.ShapeDtypeStruct(a.shape[:2] + a.shape[3:], a.dtype) for a in grads],
        scratch_shapes=[pltpu.SemaphoreType.DMA((n,))] * 2,
        compiler_params=pltpu.CompilerParams(vmem_limit_bytes=V7X_VMEM_LIMIT))(*grads)


def _rows_per_block(rows, cols, copies):
    best = 16
    for tr in range(16, rows + 1, 16):
        if rows % tr == 0 and copies * tr * cols * 4 <= 6 * 2 ** 20:
            best = tr
    assert rows % best == 0
    return best


def _chip_sum(g5, xbuf, cm):
    _, _, _, r2, cols = g5.shape
    tr = _rows_per_block(r2, cols, N_CHIPS)

    def body(cm_ref, g_ref, x_ref, pb_ref, po_ref):
        pb_ref[...] = (g_ref[...] + x_ref[...]).astype(BF16)
        me = cm_ref[1]
        po_ref[...] = g_ref[me] + x_ref[me]

    return pl.pallas_call(
        body, name="grad_chip_sum",
        grid_spec=pltpu.PrefetchScalarGridSpec(
            num_scalar_prefetch=1, grid=(DEPTH, r2 // tr),
            in_specs=[pl.BlockSpec((None, N_CHIPS, None, tr, cols), lambda l, r, cm: (l, 0, cm[0], r, 0)),
                      pl.BlockSpec((None, N_CHIPS, tr, cols), lambda l, r, cm: (l, 0, r, 0))],
            out_specs=[pl.BlockSpec((N_CHIPS, None, tr, cols), lambda l, r, cm: (0, l, r, 0)),
                       pl.BlockSpec((None, tr, cols), lambda l, r, cm: (l, r, 0))]),
        out_shape=[jax.ShapeDtypeStruct((N_CHIPS, DEPTH, r2, cols), BF16), jax.ShapeDtypeStruct((DEPTH, r2, cols), F32)],
        compiler_params=_params(2))(cm, g5, xbuf)


def _chip_exchange(parts):
    n = len(parts)

    def body(*refs):
        ins, outs = refs[:n], refs[n:2 * n]
        ssem, rsem = refs[2 * n:]
        _, _, c, me = _place()
        cps = []
        for t in range(n):
            for j in range(3):
                to = (me + 1 + j) % N_CHIPS
                cps.append(pltpu.make_async_remote_copy(
                    src_ref=ins[t].at[to], dst_ref=outs[t].at[j], send_sem=ssem.at[3 * t + j],
                    recv_sem=rsem.at[3 * t + j], device_id=_chip_core(to, c), device_id_type=MESH))
        for cp in cps:
            cp.start()
        for cp in cps:
            cp.wait()

    return pl.pallas_call(
        body, name="grad_chip_exchange", in_specs=[ANY] * n, out_specs=[ANY] * n,
        out_shape=[jax.ShapeDtypeStruct((3,) + a.shape[1:], a.dtype) for a in parts],
        scratch_shapes=[pltpu.SemaphoreType.DMA((3 * n,))] * 2,
        compiler_params=pltpu.CompilerParams(vmem_limit_bytes=V7X_VMEM_LIMIT))(*parts)


def _total_sum(pown, rbuf, cm):
    _, r2, cols = pown.shape
    tr = _rows_per_block(r2, cols, 3)

    def body(cm_ref, p_ref, r_ref, o_ref):
        acc = p_ref[...]
        for j in range(3):
            acc = acc + r_ref[j].astype(F32)
        o_ref[...] = acc

    return pl.pallas_call(
        body, name="grad_total_sum",
        grid_spec=pltpu.PrefetchScalarGridSpec(
            num_scalar_prefetch=1, grid=(DEPTH, r2 // tr),
            in_specs=[pl.BlockSpec((None, tr, cols), lambda l, r, cm: (l, r, 0)),
                      pl.BlockSpec((3, None, tr, cols), lambda l, r, cm: (0, l, r, 0))],
            out_specs=pl.BlockSpec((None, None, tr, cols), lambda l, r, cm: (l, cm[0], r, 0))),
        out_shape=jax.ShapeDtypeStruct((DEPTH, 2, r2, cols), F32), compiler_params=_params(2))(cm, pown, rbuf)


def _halves_exchange(bufs):
    n = len(bufs)

    def body(*refs):
        outs = refs[n:2 * n]
        ssem, rsem = refs[2 * n:]
        x, y, c, _ = _place()

        def copy(t, l, half):
            return pltpu.make_async_remote_copy(
                src_ref=outs[t].at[l, half], dst_ref=outs[t].at[l, half], send_sem=ssem.at[DEPTH * t + l],
                recv_sem=rsem.at[DEPTH * t + l], device_id=(x, y, 1 - c), device_id_type=MESH)

        cps = [copy(t, l, c) for t in range(n) for l in range(DEPTH)]
        for cp in cps:
            cp.start()
        for t in range(n):
            for l in range(DEPTH):
                copy(t, l, 1 - c).wait_recv()
        for cp in cps:
            cp.wait_send()

    return pl.pallas_call(
        body, name="grad_halves_exchange", in_specs=[ANY] * n, out_specs=[ANY] * n,
        out_shape=[jax.ShapeDtypeStruct(a.shape, a.dtype) for a in bufs],
        input_output_aliases={t: t for t in range(n)},
        scratch_shapes=[pltpu.SemaphoreType.DMA((DEPTH * n,))] * 2,
        compiler_params=pltpu.CompilerParams(vmem_limit_bytes=V7X_VMEM_LIMIT))(*bufs)


def _small_allreduce(v):
    rows = v.shape[0]
    n_dev = 2 * N_CHIPS

    def body(v_ref, o_ref, buf, ssem, rsem):
        x, y, c, _ = _place()
        me = 4 * x + 2 * y + c
        buf[me] = v_ref[...]

        def copy(d, slot, to):
            return pltpu.make_async_remote_copy(
                src_ref=v_ref, dst_ref=buf.at[slot], send_sem=ssem.at[d - 1], recv_sem=rsem.at[d - 1],
                device_id=(to // 4, (to // 2) % 2, to % 2), device_id_type=MESH)

        cps = [copy(d, me, (me + d) % n_dev) for d in range(1, n_dev)]
        for cp in cps:
            cp.start()
        for d in range(1, n_dev):
            copy(d, (me + n_dev - d) % n_dev, me).wait_recv()
        for cp in cps:
            cp.wait_send()
        acc = buf[0]
        for i in range(1, n_dev):
            acc = acc + buf[i]
        o_ref[...] = acc

    vm = pl.BlockSpec(memory_space=pltpu.VMEM)
    return pl.pallas_call(
        body, name="small_allreduce", in_specs=[vm], out_specs=vm,
        out_shape=jax.ShapeDtypeStruct(v.shape, F32),
        scratch_shapes=[pltpu.VMEM((n_dev, rows, LANES), F32), pltpu.SemaphoreType.DMA((n_dev - 1,)),
                        pltpu.SemaphoreType.DMA((n_dev - 1,))],
        compiler_params=pltpu.CompilerParams(vmem_limit_bytes=V7X_VMEM_LIMIT))(v)


def _adamw(w, g, m, v):
    rows, cols = w.shape
    tr = rows
    for cand in range(8, rows + 1, 8):
        if rows % cand == 0 and cand * cols * 4 <= 2 ** 21:
            tr = cand

    def body(w_ref, g_ref, m_ref, v_ref, d_ref, m2_ref, v2_ref):
        g = g_ref[...]
        m2 = ADAM_B1 * m_ref[...] + (1.0 - ADAM_B1) * g
        v2 = ADAM_B2 * v_ref[...] + (1.0 - ADAM_B2) * (g * g)
        m_hat = m2 / (1.0 - ADAM_B1 ** ADAM_STEP)
        v_hat = v2 / (1.0 - ADAM_B2 ** ADAM_STEP)
        d_ref[...] = -ADAM_LR * (m_hat / (jnp.sqrt(v_hat) + ADAM_EPS) + ADAM_WD * w_ref[...])
        m2_ref[...] = m2
        v2_ref[...] = v2

    spec = pl.BlockSpec((tr, cols), lambda i: (i, 0))
    out = jax.ShapeDtypeStruct((rows, cols), F32)
    return pl.pallas_call(
        body, name="adamw", grid=(rows // tr,), in_specs=[spec] * 4, out_specs=[spec] * 3, out_shape=[out] * 3,
        compiler_params=_params(1))(w, g, m, v)


SMALL = ("norm_ffn1", "norm_mix", "sinks", "norm_out_sb", "norm_out_swa", "norm_ffn2", "rel_bias", "norm_final")
BIG = ("ffn1_gu", "ffn1_down", "w_in", "w_out", "ffn2_gu", "ffn2_down")


def _pack(parts):
    rows = []
    for a in parts:
        a = a.reshape(-1).astype(F32)
        rows.append(jnp.pad(a, (0, -a.shape[0] % LANES)).reshape(-1, LANES))
    out = jnp.concatenate(rows, axis=0)
    return jnp.pad(out, ((0, -out.shape[0] % 8), (0, 0)))


def _unpack(packed, like):
    out, r = [], 0
    for a in like:
        n = math.prod(a.shape)
        nr = -(-n // LANES)
        out.append(packed[r:r + nr].reshape(-1)[:n].reshape(a.shape))
        r += nr
    return out


def kernel(x, norm_ffn1, w_ffn1_gu, w_ffn1_down, norm_mix, w_in, sinks, norm_out_sb, norm_out_swa, w_out, norm_ffn2, w_ffn2_gu, w_ffn2_down, rel_bias, norm_final, loss_target, m_norm_ffn1, m_w_ffn1_gu, m_w_ffn1_down, m_norm_mix, m_w_in, m_sinks, m_norm_out_sb, m_norm_out_swa, m_w_out, m_norm_ffn2, m_w_ffn2_gu, m_w_ffn2_down, m_rel_bias, m_norm_final, v_norm_ffn1, v_w_ffn1_gu, v_w_ffn1_down, v_norm_mix, v_w_in, v_sinks, v_norm_out_sb, v_norm_out_swa, v_w_out, v_norm_ffn2, v_w_ffn2_gu, v_w_ffn2_down, v_rel_bias, v_norm_final):
    big_w = dict(ffn1_gu=w_ffn1_gu, ffn1_down=w_ffn1_down, w_in=w_in, w_out=w_out, ffn2_gu=w_ffn2_gu, ffn2_down=w_ffn2_down)
    big_m = dict(ffn1_gu=m_w_ffn1_gu, ffn1_down=m_w_ffn1_down, w_in=m_w_in, w_out=m_w_out, ffn2_gu=m_w_ffn2_gu, ffn2_down=m_w_ffn2_down)
    big_v = dict(ffn1_gu=v_w_ffn1_gu, ffn1_down=v_w_ffn1_down, w_in=v_w_in, w_out=v_w_out, ffn2_gu=v_w_ffn2_gu, ffn2_down=v_w_ffn2_down)
    small_w = dict(norm_ffn1=norm_ffn1, norm_mix=norm_mix, sinks=sinks, norm_out_sb=norm_out_sb, norm_out_swa=norm_out_swa,
                   norm_ffn2=norm_ffn2, rel_bias=rel_bias, norm_final=norm_final)
    small_m = dict(norm_ffn1=m_norm_ffn1, norm_mix=m_norm_mix, sinks=m_sinks, norm_out_sb=m_norm_out_sb,
                   norm_out_swa=m_norm_out_swa, norm_ffn2=m_norm_ffn2, rel_bias=m_rel_bias, norm_final=m_norm_final)
    small_v = dict(norm_ffn1=v_norm_ffn1, norm_mix=v_norm_mix, sinks=v_sinks, norm_out_sb=v_norm_out_sb,
                   norm_out_swa=v_norm_out_swa, norm_ffn2=v_norm_ffn2, rel_bias=v_rel_bias, norm_final=v_norm_final)
    d = D_MODEL

    _, _, c, me = _place()
    cm = jnp.stack([c, me]).astype(jnp.int32)

    def placed(a):
        _, r, cols = a.shape
        return _place_own(a, cm[1:]).reshape(DEPTH, N_CHIPS, 2, r // 2, cols)

    gathered = _all_gather([placed(big_w[k]) for k in BIG])
    full = {k: a.reshape((DEPTH, N_CHIPS, a.shape[3] * 2, a.shape[4])) for k, a in zip(BIG, gathered)}
    wts = {
        "ffn1_gu": full["ffn1_gu"], "ffn2_gu": full["ffn2_gu"],
        "ffn1_down": full["ffn1_down"].reshape(DEPTH, D_FF, d), "ffn2_down": full["ffn2_down"].reshape(DEPTH, D_FF, d),
        "w_out": full["w_out"].reshape(DEPTH, d, d),
        "w_in": jnp.transpose(full["w_in"], (0, 2, 1, 3)).reshape(DEPTH, d, IN_W),
    }

    loss_row, dx, gw, gsmall = _local_step(x[0], loss_target[0], wts, small_w)

    stacks = {
        "ffn1_gu": gw["ffn1_gu"], "ffn2_gu": gw["ffn2_gu"],
        "ffn1_down": gw["ffn1_down"].reshape(DEPTH, N_CHIPS, D_FF // N_CHIPS, d),
        "ffn2_down": gw["ffn2_down"].reshape(DEPTH, N_CHIPS, D_FF // N_CHIPS, d),
        "w_out": gw["w_out"].reshape(DEPTH, N_CHIPS, d // N_CHIPS, d),
        "w_in": jnp.transpose(gw["w_in"].reshape(DEPTH, d, N_CHIPS, IN_W // N_CHIPS), (0, 2, 1, 3)),
    }
    g5 = [stacks[k].reshape(DEPTH, N_CHIPS, 2, stacks[k].shape[2] // 2, stacks[k].shape[3]) for k in BIG]
    from_sibling = _sibling_exchange(g5)
    sums = [_chip_sum(a, b, cm) for a, b in zip(g5, from_sibling)]
    landed = _chip_exchange([s[0] for s in sums])
    mine = [_total_sum(s[1], r, cm) for s, r in zip(sums, landed)]
    reduced = _halves_exchange(mine)
    grads = {k: a.reshape(big_w[k].shape) for k, a in zip(BIG, reduced)}

    red = _small_allreduce(_pack([gsmall[k] for k in SMALL] + [loss_row[0, :1]]))
    small_like = [small_w[k] for k in SMALL]
    gs = _unpack(red, small_like + [loss_row[0, :1]])
    loss = gs[-1][0]
    gs = dict(zip(SMALL, gs[:-1]))

    out_g, out_d, out_m, out_v = {}, {}, {}, {}
    for k in BIG:
        shp = big_w[k].shape
        flat = lambda a: a.reshape(shp[0] * shp[1], shp[2])
        dlt, m2, v2 = _adamw(flat(big_w[k]), flat(grads[k]), flat(big_m[k]), flat(big_v[k]))
        out_g[k], out_d[k], out_m[k], out_v[k] = grads[k], dlt.reshape(shp), m2.reshape(shp), v2.reshape(shp)
    pk = lambda dct: _pack([dct[k] for k in SMALL])
    dlt, m2, v2 = _adamw(pk(small_w), pk(gs), pk(small_m), pk(small_v))
    for dst, packed in ((out_d, dlt), (out_m, m2), (out_v, v2)):
        dst.update(zip(SMALL, _unpack(packed, small_like)))
    out_g.update(gs)

    order = ("norm_ffn1", "ffn1_gu", "ffn1_down", "norm_mix", "w_in", "sinks", "norm_out_sb", "norm_out_swa", "w_out",
             "norm_ffn2", "ffn2_gu", "ffn2_down", "rel_bias", "norm_final")
    return (loss, dx.reshape(x.shape), *[out_g[k] for k in order], *[out_d[k] for k in order],
            *[out_m[k] for k in order], *[out_v[k] for k in order])
```

```python
import functools
import math

import numpy as np
import jax
import jax.numpy as jnp
from jax import lax
from jax.experimental import pallas as pl
from jax.experimental.pallas import tpu as pltpu

F32 = jnp.float32
BF16 = jnp.bfloat16

D_MODEL = 1024
DEPTH = 2
HEAD_DIM = 64
BLK = 128
N_BUCKETS = 32
MAX_DISTANCE = 128
D_FF = 2816
EPS = 1e-6
NEG_INF = -1e30
SB_W = 512
SWA_W = 512
KV_W = 128
IN_W = 2304
SCALE = HEAD_DIM ** -0.5
N_CHIPS = 4
FS = 2 * D_FF // N_CHIPS
LANES = 128
V7X_VMEM_LIMIT = 56 * 2 ** 20
TM = 512
SB_KT = 512

ADAM_LR = 0.001
ADAM_B1 = 0.9
ADAM_B2 = 0.999
ADAM_EPS = 1e-08
ADAM_WD = 0.01
ADAM_STEP = 10

MESH = pl.DeviceIdType.MESH
ANY = pl.BlockSpec(memory_space=pl.ANY)


def _params(n_grid):
    return pltpu.CompilerParams(dimension_semantics=("arbitrary",) * n_grid, vmem_limit_bytes=V7X_VMEM_LIMIT)


def _dot(a, b):
    return jnp.dot(a, b, preferred_element_type=F32)


def _dot_nt(a, b):
    return lax.dot_general(a, b, (((1,), (1,)), ((), ())), preferred_element_type=F32)


def _dot_tn(a, b):
    return lax.dot_general(a, b, (((0,), (0,)), ((), ())), preferred_element_type=F32)


def _rms_fwd(x, g):
    r = lax.rsqrt(jnp.mean(x * x, axis=-1, keepdims=True) + EPS)
    xh = x * r
    return xh * g, xh, r


def _rms_bwd(dy, xh, r, g):
    u = dy * g
    dx = r * (u - xh * jnp.mean(u * xh, axis=-1, keepdims=True))
    dg = jnp.sum(dy * xh, axis=0, keepdims=True)
    return dx, dg


def _softplus(z):
    return jnp.maximum(z, 0.0) + jnp.log(1.0 + jnp.exp(-jnp.abs(z)))


def _norm_cast(h, g):
    t, w = h.shape

    def body(h_ref, g_ref, n_ref):
        y, _, _ = _rms_fwd(h_ref[...], g_ref[...])
        n_ref[...] = y.astype(BF16)

    return pl.pallas_call(
        body, name="norm_cast", grid=(t // TM,),
        in_specs=[pl.BlockSpec((TM, w), lambda i: (i, 0)), pl.BlockSpec((1, w), lambda i: (0, 0))],
        out_specs=pl.BlockSpec((TM, w), lambda i: (i, 0)),
        out_shape=jax.ShapeDtypeStruct((t, w), BF16), compiler_params=_params(1))(h, g)


def _ffn_gu(n, wgu, l):
    t, d = n.shape

    def body(n_ref, wg_ref, wu_ref, gu_ref, act_ref):
        x = n_ref[...]
        g = _dot(x, wg_ref[...])
        u = _dot(x, wu_ref[...])
        gu_ref[0] = g.astype(BF16)
        gu_ref[1] = u.astype(BF16)
        act_ref[...] = (g * jax.nn.sigmoid(g) * u).astype(BF16)

    return pl.pallas_call(
        body, name="ffn_gu", grid=(2, t // TM),
        in_specs=[pl.BlockSpec((TM, d), lambda j, i: (i, 0)),
                  pl.BlockSpec((None, None, d, FS), lambda j, i: (l, j, 0, 0)),
                  pl.BlockSpec((None, None, d, FS), lambda j, i: (l, j + 2, 0, 0))],
        out_specs=[pl.BlockSpec((2, TM, FS), lambda j, i: (0, i, j)), pl.BlockSpec((TM, FS), lambda j, i: (i, j))],
        out_shape=[jax.ShapeDtypeStruct((2, t, D_FF), BF16), jax.ShapeDtypeStruct((t, D_FF), BF16)],
        compiler_params=_params(2))(n, wgu, wgu)


def _down_res(act, wdn, h, l):
    t, f = act.shape
    d = h.shape[1]

    def body(a_ref, w_ref, h_ref, o_ref):
        o_ref[...] = h_ref[...] + 0.5 * _dot(a_ref[...], w_ref[...])

    return pl.pallas_call(
        body, name="down_res", grid=(t // TM,),
        in_specs=[pl.BlockSpec((TM, f), lambda i: (i, 0)), pl.BlockSpec((None, f, d), lambda i: (l, 0, 0)),
                  pl.BlockSpec((TM, d), lambda i: (i, 0))],
        out_specs=pl.BlockSpec((TM, d), lambda i: (i, 0)),
        out_shape=jax.ShapeDtypeStruct((t, d), F32), compiler_params=_params(1))(act, wdn, h)


def _proj(n, w_in, l):
    t, d = n.shape
    w = w_in.shape[2]

    def body(n_ref, w_ref, o_ref):
        o_ref[...] = _dot(n_ref[...], w_ref[...]).astype(BF16)

    return pl.pallas_call(
        body, name="proj", grid=(t // TM,),
        in_specs=[pl.BlockSpec((TM, d), lambda i: (i, 0)), pl.BlockSpec((None, d, w), lambda i: (l, 0, 0))],
        out_specs=pl.BlockSpec((TM, w), lambda i: (i, 0)),
        out_shape=jax.ShapeDtypeStruct((t, w), BF16), compiler_params=_params(1))(n, w_in)


def _out_res(o_sb, o_sw, g_sb, g_sw, w_out, h, l):
    t, d = h.shape

    def body(a_ref, b_ref, ga_ref, gb_ref, w_ref, h_ref, o_ref, mix_ref):
        ya, _, _ = _rms_fwd(a_ref[...], ga_ref[...])
        yb, _, _ = _rms_fwd(b_ref[...], gb_ref[...])
        mixed = jnp.concatenate([ya.astype(BF16), yb.astype(BF16)], axis=1)
        mix_ref[...] = mixed
        o_ref[...] = h_ref[...] + _dot(mixed, w_ref[...])

    return pl.pallas_call(
        body, name="out_res", grid=(t // TM,),
        in_specs=[pl.BlockSpec((TM, SB_W), lambda i: (i, 0)), pl.BlockSpec((TM, SWA_W), lambda i: (i, 0)),
                  pl.BlockSpec((1, SB_W), lambda i: (0, 0)), pl.BlockSpec((1, SWA_W), lambda i: (0, 0)),
                  pl.BlockSpec((None, d, d), lambda i: (l, 0, 0)), pl.BlockSpec((TM, d), lambda i: (i, 0))],
        out_specs=[pl.BlockSpec((TM, d), lambda i: (i, 0)), pl.BlockSpec((TM, d), lambda i: (i, 0))],
        out_shape=[jax.ShapeDtypeStruct((t, d), F32), jax.ShapeDtypeStruct((t, d), BF16)],
        compiler_params=_params(1))(o_sb, o_sw, g_sb, g_sw, w_out, h)


def _loss_head(h, g, tgt):
    t, d = h.shape

    def body(h_ref, g_ref, t_ref, dh_ref, dg_ref, loss_ref):
        @pl.when(pl.program_id(0) == 0)
        def _():
            dg_ref[...] = jnp.zeros_like(dg_ref)
            loss_ref[...] = jnp.zeros_like(loss_ref)

        gg = g_ref[...]
        y, xh, r = _rms_fwd(h_ref[...], gg)
        err = y - t_ref[...]
        part = 0.5 * jnp.sum(jnp.sum(err * err, axis=1, keepdims=True) / d, axis=0, keepdims=True)
        loss_ref[...] += jnp.broadcast_to(part, loss_ref.shape)
        dx, dg = _rms_bwd(err / d, xh, r, gg)
        dh_ref[...] = dx
        dg_ref[...] += dg

    return pl.pallas_call(
        body, name="loss_head", grid=(t // TM,),
        in_specs=[pl.BlockSpec((TM, d), lambda i: (i, 0)), pl.BlockSpec((1, d), lambda i: (0, 0)),
                  pl.BlockSpec((TM, d), lambda i: (i, 0))],
        out_specs=[pl.BlockSpec((TM, d), lambda i: (i, 0)), pl.BlockSpec((1, d), lambda i: (0, 0)),
                   pl.BlockSpec((1, LANES), lambda i: (0, 0))],
        out_shape=[jax.ShapeDtypeStruct((t, d), F32), jax.ShapeDtypeStruct((1, d), F32),
                   jax.ShapeDtypeStruct((1, LANES), F32)],
        compiler_params=_params(1))(h, g, tgt)


def _ffn_dact(dh, wdn, gu, l):
    t, d = dh.shape

    def body(dh_ref, w_ref, gu_ref, o_ref):
        da = 0.5 * _dot_nt(dh_ref[...].astype(BF16), w_ref[...])
        g = gu_ref[0].astype(F32)
        u = gu_ref[1].astype(F32)
        sig = jax.nn.sigmoid(g)
        silu = g * sig
        o_ref[0] = (da * u * (sig * (1.0 + g * (1.0 - sig)))).astype(BF16)
        o_ref[1] = (da * silu).astype(BF16)

    return pl.pallas_call(
        body, name="ffn_dact", grid=(2, t // TM),
        in_specs=[pl.BlockSpec((TM, d), lambda j, i: (i, 0)), pl.BlockSpec((None, FS, d), lambda j, i: (l, j, 0)),
                  pl.BlockSpec((2, TM, FS), lambda j, i: (0, i, j))],
        out_specs=pl.BlockSpec((2, TM, FS), lambda j, i: (0, i, j)),
        out_shape=jax.ShapeDtypeStruct((2, t, D_FF), BF16), compiler_params=_params(2))(dh, wdn, gu)


def _dn_norm_bwd(a, a_spec, w, w_spec, nk, dh, h_in, g):
    t, d = dh.shape

    def body(a_ref, w_ref, dh_ref, h_ref, g_ref, o_ref, dg_ref, acc_ref):
        i, k = pl.program_id(0), pl.program_id(1)

        @pl.when(k == 0)
        def _():
            acc_ref[...] = jnp.zeros_like(acc_ref)

        acc_ref[...] += _dot_nt(a_ref[...], w_ref[...])

        @pl.when(k == nk - 1)
        def _():
            gg = g_ref[...]
            _, xh, r = _rms_fwd(h_ref[...], gg)
            dx, dg = _rms_bwd(acc_ref[...], xh, r, gg)
            o_ref[...] = dh_ref[...] + dx

            @pl.when(i == 0)
            def _():
                dg_ref[...] = dg

            @pl.when(i > 0)
            def _():
                dg_ref[...] += dg

    row = pl.BlockSpec((TM, d), lambda i, k: (i, 0))
    return pl.pallas_call(
        body, name="dn_norm_bwd", grid=(t // TM, nk),
        in_specs=[a_spec, w_spec, row, row, pl.BlockSpec((1, d), lambda i, k: (0, 0))],
        out_specs=[row, pl.BlockSpec((1, d), lambda i, k: (0, 0))],
        out_shape=[jax.ShapeDtypeStruct((t, d), F32), jax.ShapeDtypeStruct((1, d), F32)],
        scratch_shapes=[pltpu.VMEM((TM, d), F32)], compiler_params=_params(2))(a, w, dh, h_in, g)


def _ffn_dn(dgu, wgu, dh, h_in, g, l):
    d = dh.shape[1]
    return _dn_norm_bwd(
        dgu, pl.BlockSpec((None, TM, FS), lambda i, k: (k // 2, i, k % 2)),
        wgu, pl.BlockSpec((None, None, d, FS), lambda i, k: (l, k, 0, 0)), N_CHIPS, dh, h_in, g)


def _mix_dn(dproj, w_in, dh, h_in, g, l):
    d = dh.shape[1]
    w = dproj.shape[1]
    return _dn_norm_bwd(
        dproj, pl.BlockSpec((TM, w), lambda i, k: (i, 0)),
        w_in, pl.BlockSpec((None, d, w), lambda i, k: (l, 0, 0)), 1, dh, h_in, g)


def _dmixed(dh, w_out, o_sb, o_sw, g_sb, g_sw, l):
    t, d = dh.shape

    def body(dh_ref, w_ref, a_ref, b_ref, ga_ref, gb_ref, o_ref, dga_ref, dgb_ref):
        i = pl.program_id(0)
        dm = _dot_nt(dh_ref[...].astype(BF16), w_ref[...])
        _, xa, ra = _rms_fwd(a_ref[...], ga_ref[...])
        _, xb, rb = _rms_fwd(b_ref[...], gb_ref[...])
        da, dga = _rms_bwd(dm[:, :SB_W], xa, ra, ga_ref[...])
        db, dgb = _rms_bwd(dm[:, SB_W:], xb, rb, gb_ref[...])
        o_ref[...] = jnp.concatenate([da.astype(BF16), db.astype(BF16)], axis=1)

        @pl.when(i == 0)
        def _():
            dga_ref[...] = dga
            dgb_ref[...] = dgb

        @pl.when(i > 0)
        def _():
            dga_ref[...] += dga
            dgb_ref[...] += dgb

    return pl.pallas_call(
        body, name="dmixed", grid=(t // TM,),
        in_specs=[pl.BlockSpec((TM, d), lambda i: (i, 0)), pl.BlockSpec((None, d, d), lambda i: (l, 0, 0)),
                  pl.BlockSpec((TM, SB_W), lambda i: (i, 0)), pl.BlockSpec((TM, SWA_W), lambda i: (i, 0)),
                  pl.BlockSpec((1, SB_W), lambda i: (0, 0)), pl.BlockSpec((1, SWA_W), lambda i: (0, 0))],
        out_specs=[pl.BlockSpec((TM, d), lambda i: (i, 0)), pl.BlockSpec((1, SB_W), lambda i: (0, 0)),
                   pl.BlockSpec((1, SWA_W), lambda i: (0, 0))],
        out_shape=[jax.ShapeDtypeStruct((t, d), BF16), jax.ShapeDtypeStruct((1, SB_W), F32),
                   jax.ShapeDtypeStruct((1, SWA_W), F32)],
        compiler_params=_params(1))(dh, w_out, o_sb, o_sw, g_sb, g_sw)


def _wgrad(name, a, a_spec, b, b_spec, grid, out_shape, out_spec, scale, prev):
    def body(*refs):
        a_ref, b_ref, o_ref = refs[0], refs[1], refs[-1]
        r = _dot_tn(a_ref[...], b_ref[...].astype(BF16))
        o_ref[...] = r if scale == 1.0 else scale * r

    ins, specs, alias = [a, b], [a_spec, b_spec], {}
    if prev is not None:
        ins.append(prev)
        specs.append(ANY)
        alias = {2: 0}
    return pl.pallas_call(
        body, name=name, grid=grid, in_specs=specs, out_specs=out_spec,
        out_shape=jax.ShapeDtypeStruct(out_shape, F32), input_output_aliases=alias,
        compiler_params=_params(len(grid)))(*ins)


def _wgrad_gu(n, dgu, l, prev):
    t, d = n.shape
    return _wgrad(
        "wgrad_gu", n, pl.BlockSpec((t, TM), lambda s, r: (0, r)),
        dgu, pl.BlockSpec((None, t, FS), lambda s, r: (s // 2, 0, s % 2)), (N_CHIPS, d // TM),
        (DEPTH, N_CHIPS, d, FS), pl.BlockSpec((None, None, TM, FS), lambda s, r: (l, s, r, 0)), 1.0, prev)


def _wgrad_down(act, dh, l, prev):
    t, d = dh.shape
    return _wgrad(
        "wgrad_down", act, pl.BlockSpec((t, FS), lambda s, r: (0, s)), dh, pl.BlockSpec((t, TM), lambda s, r: (0, r)),
        (2, d // TM), (DEPTH, D_FF, d), pl.BlockSpec((None, FS, TM), lambda s, r: (l, s, r)), 0.5, prev)


def _wgrad_out(mixed, dh, l, prev):
    t, d = dh.shape
    return _wgrad(
        "wgrad_out", mixed, pl.BlockSpec((t, TM), lambda s: (0, s)), dh, pl.BlockSpec((t, d), lambda s: (0, 0)),
        (d // TM,), (DEPTH, d, d), pl.BlockSpec((None, TM, d), lambda s: (l, s, 0)), 1.0, prev)


def _wgrad_in(n, dproj, l, prev):
    t, d = n.shape
    w = dproj.shape[1]
    tw = w // 3
    return _wgrad(
        "wgrad_in", n, pl.BlockSpec((t, d), lambda s: (0, 0)), dproj, pl.BlockSpec((t, tw), lambda s: (0, s)),
        (3,), (DEPTH, d, w), pl.BlockSpec((None, d, tw), lambda s: (l, 0, s)), 1.0, prev)


def _tri(rel):
    row = lax.broadcasted_iota(jnp.int32, (BLK, BLK), 0)
    col = lax.broadcasted_iota(jnp.int32, (BLK, BLK), 1)
    m = rel(row, col).astype(BF16)
    return jnp.concatenate([m, m], axis=0)


def _scan_dot(x, tri2):
    hi = x.astype(BF16)
    lo = (x - hi.astype(F32)).astype(BF16)
    return _dot(jnp.concatenate([hi, lo], axis=1), tri2)


def _head_masks():
    lane = lax.broadcasted_iota(jnp.int32, (1, LANES), 1)
    return [lane < HEAD_DIM, lane >= HEAD_DIM]


def _sb_fwd(proj):
    t = proj.shape[0]
    nq = t // BLK
    nb = SB_KT // BLK

    def body(q_ref, k_ref, v_ref, o_ref, tot_ref):
        hm = _head_masks()
        dcol = lax.broadcasted_iota(jnp.int32, (BLK, SB_KT), 1) - lax.broadcasted_iota(jnp.int32, (BLK, SB_KT), 0)
        dcol = jnp.concatenate([dcol, dcol], axis=0)
        after = _tri(lambda r, c: r > c)

        def tile(qh, kt, carry, acc, limit):
            ks = pl.ds(pl.multiple_of(kt * SB_KT, SB_KT), SB_KT)
            z = _dot_nt(qh, k_ref[ks, :])
            sp = _softplus(z)
            valid = None if limit is None else dcol < limit
            spm = sp if valid is None else jnp.where(valid, sp, 0.0)
            sufs = [None] * nb
            for b in reversed(range(nb)):
                blk = spm[:, b * BLK:(b + 1) * BLK]
                sufs[b] = carry + _scan_dot(blk, after)
                carry = carry + jnp.sum(blk, axis=1, keepdims=True)
            w = jnp.exp(z - sp - jnp.concatenate(sufs, axis=1))
            if valid is not None:
                w = jnp.where(valid, w, 0.0)
            return carry, acc + _dot(w.astype(BF16), v_ref[ks, :])

        def qblock(qi, _):
            qs = pl.ds(pl.multiple_of(qi * BLK, BLK), BLK)
            q = q_ref[qs, :] * SCALE
            kd = qi // nb
            limit = (qi - kd * nb) * BLK
            qh = jnp.concatenate([jnp.where(m, q, jnp.zeros_like(q)) for m in hm], axis=0)
            c0 = tile(qh, kd, jnp.zeros((2 * BLK, 1), F32), jnp.zeros((2 * BLK, LANES), F32), limit)
            carry, acc = lax.fori_loop(0, kd, lambda n, c: tile(qh, kd - 1 - n, c[0], c[1], None), c0)
            o_ref[qs, :] = jnp.where(hm[0], acc[:BLK], acc[BLK:])
            for h in range(2):
                tot_ref[h, qs, :] = jnp.broadcast_to(carry[h * BLK:(h + 1) * BLK], (BLK, LANES))
            return 0

        lax.fori_loop(0, nq, qblock, 0)

    col_blk = lambda off: pl.BlockSpec((t, LANES), lambda p: (0, off + p))
    return pl.pallas_call(
        body, name="sb_fwd", grid=(4,), in_specs=[col_blk(0), col_blk(4), col_blk(8)],
        out_specs=[pl.BlockSpec((t, LANES), lambda p: (0, p)), pl.BlockSpec((2, t, LANES), lambda p: (p, 0, 0))],
        out_shape=[jax.ShapeDtypeStruct((t, SB_W), F32), jax.ShapeDtypeStruct((8, t, LANES), F32)],
        compiler_params=_params(1))(proj, proj, proj)


def _sb_bwd(proj, d_o, tot):
    t = proj.shape[0]
    nq = t // BLK
    nb = SB_KT // BLK

    def body(q_ref, k_ref, v_ref, do_ref, tot_ref, dq_ref, dk_ref, dv_ref, dk_acc, dv_acc):
        hm = _head_masks()
        dcol = lax.broadcasted_iota(jnp.int32, (BLK, SB_KT), 1) - lax.broadcasted_iota(jnp.int32, (BLK, SB_KT), 0)
        dcol = jnp.concatenate([dcol, dcol], axis=0)
        before = _tri(lambda r, c: r < c)
        upto = _tri(lambda r, c: r <= c)
        dk_acc[...] = jnp.zeros_like(dk_acc)
        dv_acc[...] = jnp.zeros_like(dv_acc)

        def tile(qh, doh, tt, kt, pre, ecum, dq, limit):
            ks = pl.ds(pl.multiple_of(kt * SB_KT, SB_KT), SB_KT)
            k = k_ref[ks, :]
            v = v_ref[ks, :]
            z = _dot_nt(qh, k)
            sp = _softplus(z)
            valid = None if limit is None else dcol < limit
            spm = sp if valid is None else jnp.where(valid, sp, 0.0)
            pres = []
            for b in range(nb):
                blk = spm[:, b * BLK:(b + 1) * BLK]
                pres.append(pre + _scan_dot(blk, before))
                pre = pre + jnp.sum(blk, axis=1, keepdims=True)
            logw = z - (tt - jnp.concatenate(pres, axis=1))
            if valid is not None:
                logw = jnp.minimum(logw, 0.0)
            w = jnp.exp(logw)
            if valid is not None:
                w = jnp.where(valid, w, 0.0)
            e = w * _dot_nt(doh, v)
            incs = []
            for b in range(nb):
                blk = e[:, b * BLK:(b + 1) * BLK]
                incs.append(ecum + _scan_dot(blk, upto))
                ecum = ecum + jnp.sum(blk, axis=1, keepdims=True)
            dz = e - jnp.exp(z - sp) * jnp.concatenate(incs, axis=1)
            if valid is not None:
                dz = jnp.where(valid, dz, 0.0)
            dzb = dz.astype(BF16)
            dk_acc[ks, :] += _dot_tn(dzb, qh)
            dv_acc[ks, :] += _dot_tn(w.astype(BF16), doh)
            return pre, ecum, dq + _dot(dzb, k)

        def qblock(qi, _):
            qs = pl.ds(pl.multiple_of(qi * BLK, BLK), BLK)
            q = q_ref[qs, :] * SCALE
            do = do_ref[qs, :]
            kd = qi // nb
            limit = (qi - kd * nb) * BLK
            qh = jnp.concatenate([jnp.where(m, q, jnp.zeros_like(q)) for m in hm], axis=0)
            doh = jnp.concatenate([jnp.where(m, do, jnp.zeros_like(do)) for m in hm], axis=0)
            tt = jnp.concatenate([tot_ref[h, qs, 0:1] for h in range(2)], axis=0)
            c0 = (jnp.zeros((2 * BLK, 1), F32), jnp.zeros((2 * BLK, 1), F32), jnp.zeros((2 * BLK, LANES), F32))
            c = lax.fori_loop(0, kd, lambda kt, c: tile(qh, doh, tt, kt, c[0], c[1], c[2], None), c0)
            dq = tile(qh, doh, tt, kd, c[0], c[1], c[2], limit)[2]
            dq_ref[qs, :] = (jnp.where(hm[0], dq[:BLK], dq[BLK:]) * SCALE).astype(BF16)
            return 0

        lax.fori_loop(0, nq, qblock, 0)
        dk_ref[...] = dk_acc[...].astype(BF16)
        dv_ref[...] = dv_acc[...].astype(BF16)

    col_blk = lambda off: pl.BlockSpec((t, LANES), lambda p: (0, off + p))
    out = jax.ShapeDtypeStruct((t, SB_W), BF16)
    return pl.pallas_call(
        body, name="sb_bwd", grid=(4,),
        in_specs=[col_blk(0), col_blk(4), col_blk(8), col_blk(0), pl.BlockSpec((2, t, LANES), lambda p: (p, 0, 0))],
        out_specs=[col_blk(0), col_blk(0), col_blk(0)], out_shape=[out, out, out],
        scratch_shapes=[pltpu.VMEM((t, LANES), F32), pltpu.VMEM((t, LANES), F32)],
        compiler_params=_params(1))(proj, proj, proj, d_o, tot)


def _bucket_table():
    a = np.arange(BLK)[:, None]
    c = np.arange(2 * BLK)[None, :]
    dist = np.maximum(BLK + a - c, 0)
    max_exact = N_BUCKETS // 2
    dd = np.maximum(dist, 1).astype(np.float32)
    large = max_exact + (np.log(dd / max_exact) / math.log(MAX_DISTANCE / max_exact)
                         * (N_BUCKETS - max_exact)).astype(np.int32)
    large = np.minimum(large, N_BUCKETS - 1)
    return np.where(dist < max_exact, dist, large).astype(np.int32)


SWA_G = 4


def _swa_band_masks():
    row = lax.broadcasted_iota(jnp.int32, (SWA_G * BLK, 2 * BLK), 0) & (BLK - 1)
    col = lax.broadcasted_iota(jnp.int32, (SWA_G * BLK, 2 * BLK), 1)
    own = lax.broadcasted_iota(jnp.int32, (SWA_G * BLK, BLK), 1) <= (
        lax.broadcasted_iota(jnp.int32, (SWA_G * BLK, BLK), 0) & (BLK - 1))
    return (col > row) & ((col < BLK) | (col - BLK <= row)), own


def _swa_stack(ref, qs, kvh, kvmask, scale):
    parts = []
    for g in range(SWA_G):
        hq = SWA_G * kvh + g
        x = ref[qs, (hq // 2) * LANES:(hq // 2 + 1) * LANES].astype(F32)
        if hq % 2 != kvh:
            x = pltpu.roll(x, HEAD_DIM, 1)
        parts.append(jnp.where(kvmask, x * scale, 0.0).astype(BF16))
    return jnp.concatenate(parts, axis=0)


def _swa_unstack(x4, kvh, hm):
    heads = []
    for g in range(SWA_G):
        x = x4[g * BLK:(g + 1) * BLK]
        heads.append(pltpu.roll(x, HEAD_DIM, 1) if g % 2 != kvh else x)
    return [jnp.where(hm[0], heads[0], heads[1]), jnp.where(hm[0], heads[2], heads[3])]


def _swa_scores(q4, kb, bias_ref, kvh, mask, cols):
    bias4 = jnp.concatenate([bias_ref[SWA_G * kvh + g, :, cols] for g in range(SWA_G)], axis=0)
    return jnp.where(mask, _dot_nt(q4, kb) + bias4, NEG_INF)


def _swa_sinks(sink_ref, kvh):
    return jnp.concatenate([jnp.broadcast_to(sink_ref[SWA_G * kvh + g:SWA_G * kvh + g + 1, 0:1], (BLK, 1))
                            for g in range(SWA_G)], axis=0)


def _swa_fwd(proj, bias, sinks_b):
    t = proj.shape[0]
    nq = t // BLK

    def body(q_ref, k_ref, v_ref, bias_ref, sink_ref, o_ref, lse_ref):
        hm = _head_masks()
        band, own = _swa_band_masks()

        def qblock(i, kvh, prev):
            qs = pl.ds(pl.multiple_of(i * BLK, BLK), BLK)
            if prev:
                ks, mask, cols = pl.ds(pl.multiple_of((i - 1) * BLK, BLK), 2 * BLK), band, slice(None)
            else:
                ks, mask, cols = qs, own, slice(BLK, None)
            q4 = _swa_stack(q_ref, qs, kvh, hm[kvh], SCALE)
            sink4 = _swa_sinks(sink_ref, kvh)
            s = _swa_scores(q4, k_ref[ks, :], bias_ref, kvh, mask, cols)
            m = jnp.maximum(jnp.max(s, axis=1, keepdims=True), sink4)
            p = jnp.exp(s - m)
            den = jnp.sum(p, axis=1, keepdims=True) + jnp.exp(sink4 - m)
            o4 = _dot((p * (1.0 / den)).astype(BF16), v_ref[ks, :])
            lse4 = m + jnp.log(den)
            for g in range(SWA_G):
                lse_ref[SWA_G * kvh + g, qs, :] = jnp.broadcast_to(lse4[g * BLK:(g + 1) * BLK], (BLK, LANES))
            for pp, o in enumerate(_swa_unstack(o4, kvh, hm)):
                o_ref[qs, (2 * kvh + pp) * LANES:(2 * kvh + pp + 1) * LANES] = o

        for kvh in range(2):
            qblock(0, kvh, False)

            def step(i, _):
                qblock(i, kvh, True)
                return 0

            lax.fori_loop(1, nq, step, 0)

    return pl.pallas_call(
        body, name="swa_fwd", grid=(1,),
        in_specs=[pl.BlockSpec((t, SWA_W), lambda i: (0, 3)), pl.BlockSpec((t, KV_W), lambda i: (0, 16)),
                  pl.BlockSpec((t, KV_W), lambda i: (0, 17)), pl.BlockSpec((8, BLK, 2 * BLK), lambda i: (0, 0, 0)),
                  pl.BlockSpec((8, LANES), lambda i: (0, 0))],
        out_specs=[pl.BlockSpec((t, SWA_W), lambda i: (0, 0)), pl.BlockSpec((8, t, LANES), lambda i: (0, 0, 0))],
        out_shape=[jax.ShapeDtypeStruct((t, SWA_W), F32), jax.ShapeDtypeStruct((8, t, LANES), F32)],
        compiler_params=_params(1))(proj, proj, proj, bias, sinks_b)


def _swa_bwd(proj, d_o, lse, bias, sinks_b, dbias_in):
    t = proj.shape[0]
    nq = t // BLK

    def body(q_ref, k_ref, v_ref, do_ref, lse_ref, bias_ref, sink_ref, dbi_ref,
             dq_ref, dk_ref, dv_ref, dsink_ref, dbias_ref, dk_acc, dv_acc):
        hm = _head_masks()
        band, own = _swa_band_masks()
        dk_acc[...] = jnp.zeros_like(dk_acc)
        dv_acc[...] = jnp.zeros_like(dv_acc)
        dbias_ref[...] = dbi_ref[...]

        def qblock(i, kvh, prev, dsink4):
            qs = pl.ds(pl.multiple_of(i * BLK, BLK), BLK)
            if prev:
                ks, mask, cols = pl.ds(pl.multiple_of((i - 1) * BLK, BLK), 2 * BLK), band, slice(None)
            else:
                ks, mask, cols = qs, own, slice(BLK, None)
            q4 = _swa_stack(q_ref, qs, kvh, hm[kvh], SCALE)
            do4 = _swa_stack(do_ref, qs, kvh, hm[kvh], 1.0)
            sink4 = _swa_sinks(sink_ref, kvh)
            lse4 = jnp.concatenate([lse_ref[SWA_G * kvh + g, qs, 0:1] for g in range(SWA_G)], axis=0)
            kb = k_ref[ks, :]
            p = jnp.exp(_swa_scores(q4, kb, bias_ref, kvh, mask, cols) - lse4)
            dp = _dot_nt(do4, v_ref[ks, :])
            delta = jnp.sum(p * dp, axis=1, keepdims=True)
            ds = p * (dp - delta)
            for g in range(SWA_G):
                dbias_ref[SWA_G * kvh + g, :, cols] += ds[g * BLK:(g + 1) * BLK]
            dsb = ds.astype(BF16)
            dk_acc[ks, :] += _dot_tn(dsb, q4)
            dv_acc[ks, :] += _dot_tn(p.astype(BF16), do4)
            for pp, dq in enumerate(_swa_unstack(_dot(dsb, kb) * SCALE, kvh, hm)):
                dq_ref[qs, (2 * kvh + pp) * LANES:(2 * kvh + pp + 1) * LANES] = dq.astype(BF16)
            return dsink4 - jnp.exp(sink4 - lse4) * delta

        for kvh in range(2):
            ds0 = qblock(0, kvh, False, jnp.zeros((SWA_G * BLK, 1), F32))
            ds4 = lax.fori_loop(1, nq, lambda i, c: qblock(i, kvh, True, c), ds0)
            for g in range(SWA_G):
                hq = SWA_G * kvh + g
                dsink_ref[hq:hq + 1, :] = jnp.broadcast_to(
                    jnp.sum(ds4[g * BLK:(g + 1) * BLK], axis=0, keepdims=True), (1, LANES))

        dk_ref[...] = dk_acc[...].astype(BF16)
        dv_ref[...] = dv_acc[...].astype(BF16)

    full3 = pl.BlockSpec((8, BLK, 2 * BLK), lambda i: (0, 0, 0))
    kv = jax.ShapeDtypeStruct((t, KV_W), BF16)
    return pl.pallas_call(
        body, name="swa_bwd", grid=(1,),
        in_specs=[pl.BlockSpec((t, SWA_W), lambda i: (0, 3)), pl.BlockSpec((t, KV_W), lambda i: (0, 16)),
                  pl.BlockSpec((t, KV_W), lambda i: (0, 17)), pl.BlockSpec((t, SWA_W), lambda i: (0, 1)),
                  pl.BlockSpec((8, t, LANES), lambda i: (0, 0, 0)), full3, pl.BlockSpec((8, LANES), lambda i: (0, 0)),
                  full3],
        out_specs=[pl.BlockSpec((t, SWA_W), lambda i: (0, 0)), pl.BlockSpec((t, KV_W), lambda i: (0, 0)),
                   pl.BlockSpec((t, KV_W), lambda i: (0, 0)), pl.BlockSpec((8, LANES), lambda i: (0, 0)), full3],
        out_shape=[jax.ShapeDtypeStruct((t, SWA_W), BF16), kv, kv, jax.ShapeDtypeStruct((8, LANES), F32),
                   jax.ShapeDtypeStruct((8, BLK, 2 * BLK), F32)],
        scratch_shapes=[pltpu.VMEM((t, KV_W), F32), pltpu.VMEM((t, KV_W), F32)],
        compiler_params=_params(1))(proj, proj, proj, d_o, lse, bias, sinks_b, dbias_in)


def _bias_table(rel_bias, buckets):
    def body(rb_ref, b_ref, o_ref):
        bk = b_ref[...]
        for h in range(8):
            acc = jnp.zeros((BLK, 2 * BLK), F32)
            for b in range(N_BUCKETS):
                acc = jnp.where(bk == b, rb_ref[b, h], acc)
            o_ref[h] = acc

    return pl.pallas_call(
        body, name="bias_table", grid=(1,),
        in_specs=[pl.BlockSpec(memory_space=pltpu.SMEM), pl.BlockSpec((BLK, 2 * BLK), lambda i: (0, 0))],
        out_specs=pl.BlockSpec((8, BLK, 2 * BLK), lambda i: (0, 0, 0)),
        out_shape=jax.ShapeDtypeStruct((8, BLK, 2 * BLK), F32), compiler_params=_params(1))(rel_bias, buckets)


def _bias_grad(dbias, buckets):
    def body(d_ref, b_ref, o_ref):
        lane = lax.broadcasted_iota(jnp.int32, (1, LANES), 1)
        bk = b_ref[...]
        for h in range(8):
            d = d_ref[h]
            acc = jnp.zeros((1, LANES), F32)
            for b in range(N_BUCKETS):
                s = jnp.sum(jnp.sum(jnp.where(bk == b, d, 0.0), axis=0, keepdims=True), axis=1, keepdims=True)
                acc = acc + jnp.where(lane == b, s, 0.0)
            o_ref[h:h + 1, :] = acc

    return pl.pallas_call(
        body, name="bias_grad", grid=(1,),
        in_specs=[pl.BlockSpec((8, BLK, 2 * BLK), lambda i: (0, 0, 0)), pl.BlockSpec((BLK, 2 * BLK), lambda i: (0, 0))],
        out_specs=pl.BlockSpec((8, LANES), lambda i: (0, 0)),
        out_shape=jax.ShapeDtypeStruct((8, LANES), F32), compiler_params=_params(1))(dbias, buckets)


def _row(a):
    return a.reshape(1, -1)


def _local_step(x, tgt, wts, small):
    buckets = jnp.asarray(_bucket_table())
    bias = _bias_table(small["rel_bias"], buckets)
    saved = []
    h = x
    for l in range(DEPTH):
        s = {"h0": h}
        s["n1"] = _norm_cast(h, _row(small["norm_ffn1"][l]))
        s["gu1"], s["act1"] = _ffn_gu(s["n1"], wts["ffn1_gu"], l)
        h = _down_res(s["act1"], wts["ffn1_down"], h, l)
        s["h1"] = h
        s["nm"] = _norm_cast(h, _row(small["norm_mix"][l]))
        s["proj"] = _proj(s["nm"], wts["w_in"], l)
        s["sinks_b"] = jnp.broadcast_to(small["sinks"][l][:, None], (8, LANES))
        s["o_sb"], s["tot"] = _sb_fwd(s["proj"])
        s["o_sw"], s["lse"] = _swa_fwd(s["proj"], bias, s["sinks_b"])
        h, s["mixed"] = _out_res(s["o_sb"], s["o_sw"], _row(small["norm_out_sb"][l]), _row(small["norm_out_swa"][l]),
                                 wts["w_out"], h, l)
        s["h2"] = h
        s["n2"] = _norm_cast(h, _row(small["norm_ffn2"][l]))
        s["gu2"], s["act2"] = _ffn_gu(s["n2"], wts["ffn2_gu"], l)
        h = _down_res(s["act2"], wts["ffn2_down"], h, l)
        saved.append(s)

    dh, dg_final, loss = _loss_head(h, _row(small["norm_final"]), tgt)

    gw = {k: None for k in ("ffn1_gu", "ffn1_down", "w_in", "w_out", "ffn2_gu", "ffn2_down")}
    gs = {k: [None] * DEPTH for k in ("norm_ffn1", "norm_mix", "sinks", "norm_out_sb", "norm_out_swa", "norm_ffn2")}
    dbias = jnp.zeros((8, BLK, 2 * BLK), F32)
    for l in reversed(range(DEPTH)):
        s = saved[l]
        dgu = _ffn_dact(dh, wts["ffn2_down"], s["gu2"], l)
        gw["ffn2_down"] = _wgrad_down(s["act2"], dh, l, gw["ffn2_down"])
        gw["ffn2_gu"] = _wgrad_gu(s["n2"], dgu, l, gw["ffn2_gu"])
        dh, gs["norm_ffn2"][l] = _ffn_dn(dgu, wts["ffn2_gu"], dh, s["h2"], _row(small["norm_ffn2"][l]), l)
        gw["w_out"] = _wgrad_out(s["mixed"], dh, l, gw["w_out"])
        d_o, gs["norm_out_sb"][l], gs["norm_out_swa"][l] = _dmixed(
            dh, wts["w_out"], s["o_sb"], s["o_sw"], _row(small["norm_out_sb"][l]), _row(small["norm_out_swa"][l]), l)
        dq_sb, dk_sb, dv_sb = _sb_bwd(s["proj"], d_o, s["tot"])
        dq_sw, dk_sw, dv_sw, dsink, dbias = _swa_bwd(s["proj"], d_o, s["lse"], bias, s["sinks_b"], dbias)
        gs["sinks"][l] = dsink[:, 0]
        dproj = jnp.concatenate([dq_sb, dk_sb, dv_sb, dq_sw, dk_sw, dv_sw], axis=1)
        gw["w_in"] = _wgrad_in(s["nm"], dproj, l, gw["w_in"])
        dh, gs["norm_mix"][l] = _mix_dn(dproj, wts["w_in"], dh, s["h1"], _row(small["norm_mix"][l]), l)
        dgu = _ffn_dact(dh, wts["ffn1_down"], s["gu1"], l)
        gw["ffn1_down"] = _wgrad_down(s["act1"], dh, l, gw["ffn1_down"])
        gw["ffn1_gu"] = _wgrad_gu(s["n1"], dgu, l, gw["ffn1_gu"])
        dh, gs["norm_ffn1"][l] = _ffn_dn(dgu, wts["ffn1_gu"], dh, s["h0"], _row(small["norm_ffn1"][l]), l)

    gsmall = {k: jnp.stack([a.reshape(-1) for a in v]) for k, v in gs.items()}
    gsmall["rel_bias"] = jnp.transpose(_bias_grad(dbias, buckets)[:, :N_BUCKETS])
    gsmall["norm_final"] = dg_final.reshape(-1)
    return loss, dh, gw, gsmall


def _place():
    x, y, c = lax.axis_index("x"), lax.axis_index("y"), lax.axis_index("c")
    return x, y, c, 2 * x + y


def _chip_core(k, c):
    return (k // 2, k % 2, c)


def _place_own(w, me1):
    _, rows, cols = w.shape
    tr = _rows_per_block(rows, cols, 1)

    def body(me_ref, w_ref, o_ref):
        o_ref[...] = w_ref[...].astype(BF16)

    return pl.pallas_call(
        body, name="place_own",
        grid_spec=pltpu.PrefetchScalarGridSpec(
            num_scalar_prefetch=1, grid=(DEPTH, rows // tr),
            in_specs=[pl.BlockSpec((None, tr, cols), lambda l, r, me: (l, r, 0))],
            out_specs=pl.BlockSpec((None, None, tr, cols), lambda l, r, me: (l, me[0], r, 0))),
        out_shape=jax.ShapeDtypeStruct((DEPTH, N_CHIPS, rows, cols), BF16), compiler_params=_params(2))(me1, w)


def _all_gather(bufs):
    n = len(bufs)

    def body(*refs):
        outs = refs[n:2 * n]
        ici_s, ici_r, d2d_s, d2d_r = refs[2 * n:]
        x, y, c, me = _place()
        sib = (x, y, 1 - c)
        groups = [(t, l) for t in range(n) for l in range(DEPTH)]

        def ici(t, l, j, to):
            s = (t * DEPTH + l) * 3 + j
            return pltpu.make_async_remote_copy(
                src_ref=outs[t].at[l, me, c], dst_ref=outs[t].at[l, me, c], send_sem=ici_s.at[s], recv_sem=ici_r.at[s],
                device_id=to, device_id_type=MESH)

        def landed(t, l, j, half):
            src = (me + 3 - j) % N_CHIPS
            return outs[t].at[l, src, half]

        def d2d(t, l, j, half):
            s = (t * DEPTH + l) * 3 + j
            return pltpu.make_async_remote_copy(
                src_ref=landed(t, l, j, half), dst_ref=landed(t, l, j, half), send_sem=d2d_s.at[s],
                recv_sem=d2d_r.at[s], device_id=sib, device_id_type=MESH)

        sends = [ici(t, l, j, _chip_core((me + 1 + j) % N_CHIPS, c)) for t, l in groups for j in range(3)]
        for cp in sends:
            cp.start()
        passed = []
        for t, l in groups:
            for j in range(3):
                s = (t * DEPTH + l) * 3 + j
                pltpu.make_async_remote_copy(
                    src_ref=landed(t, l, j, c), dst_ref=landed(t, l, j, c), send_sem=ici_s.at[s],
                    recv_sem=ici_r.at[s], device_id=sib, device_id_type=MESH).wait_recv()
                passed.append(d2d(t, l, j, c))
                passed[-1].start()
        for t, l in groups:
            for j in range(3):
                d2d(t, l, j, 1 - c).wait_recv()
        for cp in sends + passed:
            cp.wait_send()

    n_cp = n * DEPTH * 3
    return pl.pallas_call(
        body, name="all_gather_weights", in_specs=[ANY] * n, out_specs=[ANY] * n,
        out_shape=[jax.ShapeDtypeStruct(a.shape, a.dtype) for a in bufs],
        input_output_aliases={t: t for t in range(n)},
        scratch_shapes=[pltpu.SemaphoreType.DMA((n_cp,))] * 4,
        compiler_params=pltpu.CompilerParams(vmem_limit_bytes=V7X_VMEM_LIMIT))(*bufs)


def _sibling_exchange(grads):
    n = len(grads)

    def body(*refs):
        ins, outs = refs[:n], refs[n:2 * n]
        ssem, rsem = refs[2 * n:]
        x, y, c, _ = _place()
        cps = [pltpu.make_async_remote_copy(
            src_ref=ins[t].at[:, :, 1 - c], dst_ref=outs[t], send_sem=ssem.at[t], recv_sem=rsem.at[t],
            device_id=(x, y, 1 - c), device_id_type=MESH) for t in range(n)]
        for cp in cps:
            cp.start()
        for cp in cps:
            cp.wait()

    return pl.pallas_call(
        body, name="grad_sibling_exchange", in_specs=[ANY] * n, out_specs=[ANY] * n,
        out_shape=[jax.ShapeDtypeStruct(a.shape[:2] + a.shape[3:], a.dtype) for a in grads],
        scratch_shapes=[pltpu.SemaphoreType.DMA((n,))] * 2,
        compiler_params=pltpu.CompilerParams(vmem_limit_bytes=V7X_VMEM_LIMIT))(*grads)


def _rows_per_block(rows, cols, copies):
    best = 16
    for tr in range(16, rows + 1, 16):
        if rows % tr == 0 and copies * tr * cols * 4 <= 6 * 2 ** 20:
            best = tr
    assert rows % best == 0
    return best


def _chip_sum(g5, xbuf, cm):
    _, _, _, r2, cols = g5.shape
    tr = _rows_per_block(r2, cols, N_CHIPS)

    def body(cm_ref, g_ref, x_ref, pb_ref, po_ref):
        pb_ref[...] = (g_ref[...] + x_ref[...]).astype(BF16)
        me = cm_ref[1]
        po_ref[...] = g_ref[me] + x_ref[me]

    return pl.pallas_call(
        body, name="grad_chip_sum",
        grid_spec=pltpu.PrefetchScalarGridSpec(
            num_scalar_prefetch=1, grid=(DEPTH, r2 // tr),
            in_specs=[pl.BlockSpec((None, N_CHIPS, None, tr, cols), lambda l, r, cm: (l, 0, cm[0], r, 0)),
                      pl.BlockSpec((None, N_CHIPS, tr, cols), lambda l, r, cm: (l, 0, r, 0))],
            out_specs=[pl.BlockSpec((N_CHIPS, None, tr, cols), lambda l, r, cm: (0, l, r, 0)),
                       pl.BlockSpec((None, tr, cols), lambda l, r, cm: (l, r, 0))]),
        out_shape=[jax.ShapeDtypeStruct((N_CHIPS, DEPTH, r2, cols), BF16), jax.ShapeDtypeStruct((DEPTH, r2, cols), F32)],
        compiler_params=_params(2))(cm, g5, xbuf)


def _chip_exchange(parts):
    n = len(parts)

    def body(*refs):
        ins, outs = refs[:n], refs[n:2 * n]
        ssem, rsem = refs[2 * n:]
        _, _, c, me = _place()
        cps = []
        for t in range(n):
            for j in range(3):
                to = (me + 1 + j) % N_CHIPS
                cps.append(pltpu.make_async_remote_copy(
                    src_ref=ins[t].at[to], dst_ref=outs[t].at[j], send_sem=ssem.at[3 * t + j],
                    recv_sem=rsem.at[3 * t + j], device_id=_chip_core(to, c), device_id_type=MESH))
        for cp in cps:
            cp.start()
        for cp in cps:
            cp.wait()

    return pl.pallas_call(
        body, name="grad_chip_exchange", in_specs=[ANY] * n, out_specs=[ANY] * n,
        out_shape=[jax.ShapeDtypeStruct((3,) + a.shape[1:], a.dtype) for a in parts],
        scratch_shapes=[pltpu.SemaphoreType.DMA((3 * n,))] * 2,
        compiler_params=pltpu.CompilerParams(vmem_limit_bytes=V7X_VMEM_LIMIT))(*parts)


def _total_sum(pown, rbuf, cm):
    _, r2, cols = pown.shape
    tr = _rows_per_block(r2, cols, 3)

    def body(cm_ref, p_ref, r_ref, o_ref):
        acc = p_ref[...]
        for j in range(3):
            acc = acc + r_ref[j].astype(F32)
        o_ref[...] = acc

    return pl.pallas_call(
        body, name="grad_total_sum",
        grid_spec=pltpu.PrefetchScalarGridSpec(
            num_scalar_prefetch=1, grid=(DEPTH, r2 // tr),
            in_specs=[pl.BlockSpec((None, tr, cols), lambda l, r, cm: (l, r, 0)),
                      pl.BlockSpec((3, None, tr, cols), lambda l, r, cm: (0, l, r, 0))],
            out_specs=pl.BlockSpec((None, None, tr, cols), lambda l, r, cm: (l, cm[0], r, 0))),
        out_shape=jax.ShapeDtypeStruct((DEPTH, 2, r2, cols), F32), compiler_params=_params(2))(cm, pown, rbuf)


def _halves_exchange(bufs):
    n = len(bufs)

    def body(*refs):
        outs = refs[n:2 * n]
        ssem, rsem = refs[2 * n:]
        x, y, c, _ = _place()

        def copy(t, l, half):
            return pltpu.make_async_remote_copy(
                src_ref=outs[t].at[l, half], dst_ref=outs[t].at[l, half], send_sem=ssem.at[DEPTH * t + l],
                recv_sem=rsem.at[DEPTH * t + l], device_id=(x, y, 1 - c), device_id_type=MESH)

        cps = [copy(t, l, c) for t in range(n) for l in range(DEPTH)]
        for cp in cps:
            cp.start()
        for t in range(n):
            for l in range(DEPTH):
                copy(t, l, 1 - c).wait_recv()
        for cp in cps:
            cp.wait_send()

    return pl.pallas_call(
        body, name="grad_halves_exchange", in_specs=[ANY] * n, out_specs=[ANY] * n,
        out_shape=[jax.ShapeDtypeStruct(a.shape, a.dtype) for a in bufs],
        input_output_aliases={t: t for t in range(n)},
        scratch_shapes=[pltpu.SemaphoreType.DMA((DEPTH * n,))] * 2,
        compiler_params=pltpu.CompilerParams(vmem_limit_bytes=V7X_VMEM_LIMIT))(*bufs)


def _small_allreduce(v):
    rows = v.shape[0]
    n_dev = 2 * N_CHIPS

    def body(v_ref, o_ref, buf, ssem, rsem):
        x, y, c, _ = _place()
        me = 4 * x + 2 * y + c
        buf[me] = v_ref[...]

        def copy(d, slot, to):
            return pltpu.make_async_remote_copy(
                src_ref=v_ref, dst_ref=buf.at[slot], send_sem=ssem.at[d - 1], recv_sem=rsem.at[d - 1],
                device_id=(to // 4, (to // 2) % 2, to % 2), device_id_type=MESH)

        cps = [copy(d, me, (me + d) % n_dev) for d in range(1, n_dev)]
        for cp in cps:
            cp.start()
        for d in range(1, n_dev):
            copy(d, (me + n_dev - d) % n_dev, me).wait_recv()
        for cp in cps:
            cp.wait_send()
        acc = buf[0]
        for i in range(1, n_dev):
            acc = acc + buf[i]
        o_ref[...] = acc

    vm = pl.BlockSpec(memory_space=pltpu.VMEM)
    return pl.pallas_call(
        body, name="small_allreduce", in_specs=[vm], out_specs=vm,
        out_shape=jax.ShapeDtypeStruct(v.shape, F32),
        scratch_shapes=[pltpu.VMEM((n_dev, rows, LANES), F32), pltpu.SemaphoreType.DMA((n_dev - 1,)),
                        pltpu.SemaphoreType.DMA((n_dev - 1,))],
        compiler_params=pltpu.CompilerParams(vmem_limit_bytes=V7X_VMEM_LIMIT))(v)


def _adamw(w, g, m, v):
    rows, cols = w.shape
    tr = rows
    for cand in range(8, rows + 1, 8):
        if rows % cand == 0 and cand * cols * 4 <= 2 ** 21:
            tr = cand

    def body(w_ref, g_ref, m_ref, v_ref, d_ref, m2_ref, v2_ref):
        g = g_ref[...]
        m2 = ADAM_B1 * m_ref[...] + (1.0 - ADAM_B1) * g
        v2 = ADAM_B2 * v_ref[...] + (1.0 - ADAM_B2) * (g * g)
        m_hat = m2 / (1.0 - ADAM_B1 ** ADAM_STEP)
        v_hat = v2 / (1.0 - ADAM_B2 ** ADAM_STEP)
        d_ref[...] = -ADAM_LR * (m_hat / (jnp.sqrt(v_hat) + ADAM_EPS) + ADAM_WD * w_ref[...])
        m2_ref[...] = m2
        v2_ref[...] = v2

    spec = pl.BlockSpec((tr, cols), lambda i: (i, 0))
    out = jax.ShapeDtypeStruct((rows, cols), F32)
    return pl.pallas_call(
        body, name="adamw", grid=(rows // tr,), in_specs=[spec] * 4, out_specs=[spec] * 3, out_shape=[out] * 3,
        compiler_params=_params(1))(w, g, m, v)


SMALL = ("norm_ffn1", "norm_mix", "sinks", "norm_out_sb", "norm_out_swa", "norm_ffn2", "rel_bias", "norm_final")
BIG = ("ffn1_gu", "ffn1_down", "w_in", "w_out", "ffn2_gu", "ffn2_down")


def _pack(parts):
    rows = []
    for a in parts:
        a = a.reshape(-1).astype(F32)
        rows.append(jnp.pad(a, (0, -a.shape[0] % LANES)).reshape(-1, LANES))
    out = jnp.concatenate(rows, axis=0)
    return jnp.pad(out, ((0, -out.shape[0] % 8), (0, 0)))


def _unpack(packed, like):
    out, r = [], 0
    for a in like:
        n = math.prod(a.shape)
        nr = -(-n // LANES)
        out.append(packed[r:r + nr].reshape(-1)[:n].reshape(a.shape))
        r += nr
    return out


def kernel(x, norm_ffn1, w_ffn1_gu, w_ffn1_down, norm_mix, w_in, sinks, norm_out_sb, norm_out_swa, w_out, norm_ffn2, w_ffn2_gu, w_ffn2_down, rel_bias, norm_final, loss_target, m_norm_ffn1, m_w_ffn1_gu, m_w_ffn1_down, m_norm_mix, m_w_in, m_sinks, m_norm_out_sb, m_norm_out_swa, m_w_out, m_norm_ffn2, m_w_ffn2_gu, m_w_ffn2_down, m_rel_bias, m_norm_final, v_norm_ffn1, v_w_ffn1_gu, v_w_ffn1_down, v_norm_mix, v_w_in, v_sinks, v_norm_out_sb, v_norm_out_swa, v_w_out, v_norm_ffn2, v_w_ffn2_gu, v_w_ffn2_down, v_rel_bias, v_norm_final):
    big_w = dict(ffn1_gu=w_ffn1_gu, ffn1_down=w_ffn1_down, w_in=w_in, w_out=w_out, ffn2_gu=w_ffn2_gu, ffn2_down=w_ffn2_down)
    big_m = dict(ffn1_gu=m_w_ffn1_gu, ffn1_down=m_w_ffn1_down, w_in=m_w_in, w_out=m_w_out, ffn2_gu=m_w_ffn2_gu, ffn2_down=m_w_ffn2_down)
    big_v = dict(ffn1_gu=v_w_ffn1_gu, ffn1_down=v_w_ffn1_down, w_in=v_w_in, w_out=v_w_out, ffn2_gu=v_w_ffn2_gu, ffn2_down=v_w_ffn2_down)
    small_w = dict(norm_ffn1=norm_ffn1, norm_mix=norm_mix, sinks=sinks, norm_out_sb=norm_out_sb, norm_out_swa=norm_out_swa,
                   norm_ffn2=norm_ffn2, rel_bias=rel_bias, norm_final=norm_final)
    small_m = dict(norm_ffn1=m_norm_ffn1, norm_mix=m_norm_mix, sinks=m_sinks, norm_out_sb=m_norm_out_sb,
                   norm_out_swa=m_norm_out_swa, norm_ffn2=m_norm_ffn2, rel_bias=m_rel_bias, norm_final=m_norm_final)
    small_v = dict(norm_ffn1=v_norm_ffn1, norm_mix=v_norm_mix, sinks=v_sinks, norm_out_sb=v_norm_out_sb,
                   norm_out_swa=v_norm_out_swa, norm_ffn2=v_norm_ffn2, rel_bias=v_rel_bias, norm_final=v_norm_final)
    d = D_MODEL

    _, _, c, me = _place()
    cm = jnp.stack([c, me]).astype(jnp.int32)

    def placed(a):
        _, r, cols = a.shape
        return _place_own(a, cm[1:]).reshape(DEPTH, N_CHIPS, 2, r // 2, cols)

    gathered = _all_gather([placed(big_w[k]) for k in BIG])
    full = {k: a.reshape((DEPTH, N_CHIPS, a.shape[3] * 2, a.shape[4])) for k, a in zip(BIG, gathered)}
    wts = {
        "ffn1_gu": full["ffn1_gu"], "ffn2_gu": full["ffn2_gu"],
        "ffn1_down": full["ffn1_down"].reshape(DEPTH, D_FF, d), "ffn2_down": full["ffn2_down"].reshape(DEPTH, D_FF, d),
        "w_out": full["w_out"].reshape(DEPTH, d, d),
        "w_in": jnp.transpose(full["w_in"], (0, 2, 1, 3)).reshape(DEPTH, d, IN_W),
    }

    loss_row, dx, gw, gsmall = _local_step(x[0], loss_target[0], wts, small_w)

    stacks = {
        "ffn1_gu": gw["ffn1_gu"], "ffn2_gu": gw["ffn2_gu"],
        "ffn1_down": gw["ffn1_down"].reshape(DEPTH, N_CHIPS, D_FF // N_CHIPS, d),
        "ffn2_down": gw["ffn2_down"].reshape(DEPTH, N_CHIPS, D_FF // N_CHIPS, d),
        "w_out": gw["w_out"].reshape(DEPTH, N_CHIPS, d // N_CHIPS, d),
        "w_in": jnp.transpose(gw["w_in"].reshape(DEPTH, d, N_CHIPS, IN_W // N_CHIPS), (0, 2, 1, 3)),
    }
    g5 = [stacks[k].reshape(DEPTH, N_CHIPS, 2, stacks[k].shape[2] // 2, stacks[k].shape[3]) for k in BIG]
    from_sibling = _sibling_exchange(g5)
    sums = [_chip_sum(a, b, cm) for a, b in zip(g5, from_sibling)]
    landed = _chip_exchange([s[0] for s in sums])
    mine = [_total_sum(s[1], r, cm) for s, r in zip(sums, landed)]
    reduced = _halves_exchange(mine)
    grads = {k: a.reshape(big_w[k].shape) for k, a in zip(BIG, reduced)}

    red = _small_allreduce(_pack([gsmall[k] for k in SMALL] + [loss_row[0, :1]]))
    small_like = [small_w[k] for k in SMALL]
    gs = _unpack(red, small_like + [loss_row[0, :1]])
    loss = gs[-1][0]
    gs = dict(zip(SMALL, gs[:-1]))

    out_g, out_d, out_m, out_v = {}, {}, {}, {}
    for k in BIG:
        shp = big_w[k].shape
        flat = lambda a: a.reshape(shp[0] * shp[1], shp[2])
        dlt, m2, v2 = _adamw(flat(big_w[k]), flat(grads[k]), flat(big_m[k]), flat(big_v[k]))
        out_g[k], out_d[k], out_m[k], out_v[k] = grads[k], dlt.reshape(shp), m2.reshape(shp), v2.reshape(shp)
    pk = lambda dct: _pack([dct[k] for k in SMALL])
    dlt, m2, v2 = _adamw(pk(small_w), pk(gs), pk(small_m), pk(small_v))
    for dst, packed in ((out_d, dlt), (out_m, m2), (out_v, v2)):
        dst.update(zip(SMALL, _unpack(packed, small_like)))
    out_g.update(gs)

    order = ("norm_ffn1", "ffn1_gu", "ffn1_down", "norm_mix", "w_in", "sinks", "norm_out_sb", "norm_out_swa", "w_out",
             "norm_ffn2", "ffn2_gu", "ffn2_down", "rel_bias", "norm_final")
    return (loss, dx.reshape(x.shape), *[out_g[k] for k in order], *[out_d[k] for k in order],
            *[out_m[k] for k in order], *[out_v[k] for k in order])
```

```python
import math

import numpy as np
import jax
import jax.numpy as jnp
from jax import lax
from jax.experimental import pallas as pl
from jax.experimental.pallas import tpu as pltpu

F32 = jnp.float32
BF16 = jnp.bfloat16

D_MODEL = 1024
DEPTH = 2
HEAD_DIM = 64
BLK = 128
N_BUCKETS = 32
MAX_DISTANCE = 128
D_FF = 2816
EPS = 1e-6
NEG_INF = -1e30
SB_W = 512
SWA_W = 512
KV_W = 128
IN_W = 2304
SCALE = HEAD_DIM ** -0.5
N_CHIPS = 4
FS = 2 * D_FF // N_CHIPS
LANES = 128
V7X_VMEM_LIMIT = 56 * 2 ** 20
TM = 512
SB_KT = 512
SWA_G = 4

ADAM_LR = 0.001
ADAM_B1 = 0.9
ADAM_B2 = 0.999
ADAM_EPS = 1e-08
ADAM_WD = 0.01
ADAM_STEP = 10

MESH = pl.DeviceIdType.MESH
ANY = pl.BlockSpec(memory_space=pl.ANY)
HBM = pl.BlockSpec(memory_space=pltpu.HBM)
SEM = pl.BlockSpec(memory_space=pltpu.SEMAPHORE)
EFFECT = pltpu.SideEffectType.DATAFLOW_SIDE_EFFECTING


def _params(n_grid):
    return pltpu.CompilerParams(dimension_semantics=("arbitrary",) * n_grid, vmem_limit_bytes=V7X_VMEM_LIMIT)


def _dot(a, b):
    return jnp.dot(a, b, preferred_element_type=F32)


def _dot_nt(a, b):
    return lax.dot_general(a, b, (((1,), (1,)), ((), ())), preferred_element_type=F32)


def _dot_tn(a, b):
    return lax.dot_general(a, b, (((0,), (0,)), ((), ())), preferred_element_type=F32)


def _rms_fwd(x, g):
    r = lax.rsqrt(jnp.mean(x * x, axis=-1, keepdims=True) + EPS)
    xh = x * r
    return xh * g, xh, r


def _rms_bwd(dy, xh, r, g):
    u = dy * g
    dx = r * (u - xh * jnp.mean(u * xh, axis=-1, keepdims=True))
    dg = jnp.sum(dy * xh, axis=0, keepdims=True)
    return dx, dg


def _softplus(z):
    return jnp.maximum(z, 0.0) + jnp.log(1.0 + jnp.exp(-jnp.abs(z)))


def _norm_cast(h, g):
    t, w = h.shape

    def body(h_ref, g_ref, n_ref):
        y, _, _ = _rms_fwd(h_ref[...], g_ref[...])
        n_ref[...] = y.astype(BF16)

    return pl.pallas_call(
        body, name="norm_cast", grid=(t // TM,),
        in_specs=[pl.BlockSpec((TM, w), lambda i: (i, 0)), pl.BlockSpec((1, w), lambda i: (0, 0))],
        out_specs=pl.BlockSpec((TM, w), lambda i: (i, 0)),
        out_shape=jax.ShapeDtypeStruct((t, w), BF16), compiler_params=_params(1))(h, g)


def _ffn_gu(n, wgu):
    t, d = n.shape

    def body(n_ref, wg_ref, wu_ref, gu_ref, act_ref):
        x = n_ref[...]
        g = _dot(x, wg_ref[...])
        u = _dot(x, wu_ref[...])
        gu_ref[0] = g.astype(BF16)
        gu_ref[1] = u.astype(BF16)
        act_ref[...] = (g * jax.nn.sigmoid(g) * u).astype(BF16)

    return pl.pallas_call(
        body, name="ffn_gu", grid=(2, t // TM),
        in_specs=[pl.BlockSpec((TM, d), lambda j, i: (i, 0)),
                  pl.BlockSpec((None, d, FS), lambda j, i: (j, 0, 0)),
                  pl.BlockSpec((None, d, FS), lambda j, i: (j + 2, 0, 0))],
        out_specs=[pl.BlockSpec((2, TM, FS), lambda j, i: (0, i, j)), pl.BlockSpec((TM, FS), lambda j, i: (i, j))],
        out_shape=[jax.ShapeDtypeStruct((2, t, D_FF), BF16), jax.ShapeDtypeStruct((t, D_FF), BF16)],
        compiler_params=_params(2))(n, wgu, wgu)


def _down_res(act, wdn, h):
    t, f = act.shape
    d = h.shape[1]

    def body(a_ref, w_ref, h_ref, o_ref):
        o_ref[...] = h_ref[...] + 0.5 * _dot(a_ref[...], w_ref[...])

    return pl.pallas_call(
        body, name="down_res", grid=(t // TM,),
        in_specs=[pl.BlockSpec((TM, f), lambda i: (i, 0)), pl.BlockSpec((f, d), lambda i: (0, 0)),
                  pl.BlockSpec((TM, d), lambda i: (i, 0))],
        out_specs=pl.BlockSpec((TM, d), lambda i: (i, 0)),
        out_shape=jax.ShapeDtypeStruct((t, d), F32), compiler_params=_params(1))(act, wdn, h)


def _proj(n, w_in):
    t, d = n.shape
    w = w_in.shape[1]

    def body(n_ref, w_ref, o_ref):
        o_ref[...] = _dot(n_ref[...], w_ref[...]).astype(BF16)

    return pl.pallas_call(
        body, name="proj", grid=(t // TM,),
        in_specs=[pl.BlockSpec((TM, d), lambda i: (i, 0)), pl.BlockSpec((d, w), lambda i: (0, 0))],
        out_specs=pl.BlockSpec((TM, w), lambda i: (i, 0)),
        out_shape=jax.ShapeDtypeStruct((t, w), BF16), compiler_params=_params(1))(n, w_in)


def _out_res(o_sb, o_sw, g_sb, g_sw, w_out, h):
    t, d = h.shape

    def body(a_ref, b_ref, ga_ref, gb_ref, w_ref, h_ref, o_ref, mix_ref):
        ya, _, _ = _rms_fwd(a_ref[...], ga_ref[...])
        yb, _, _ = _rms_fwd(b_ref[...], gb_ref[...])
        mixed = jnp.concatenate([ya.astype(BF16), yb.astype(BF16)], axis=1)
        mix_ref[...] = mixed
        o_ref[...] = h_ref[...] + _dot(mixed, w_ref[...])

    return pl.pallas_call(
        body, name="out_res", grid=(t // TM,),
        in_specs=[pl.BlockSpec((TM, SB_W), lambda i: (i, 0)), pl.BlockSpec((TM, SWA_W), lambda i: (i, 0)),
                  pl.BlockSpec((1, SB_W), lambda i: (0, 0)), pl.BlockSpec((1, SWA_W), lambda i: (0, 0)),
                  pl.BlockSpec((d, d), lambda i: (0, 0)), pl.BlockSpec((TM, d), lambda i: (i, 0))],
        out_specs=[pl.BlockSpec((TM, d), lambda i: (i, 0)), pl.BlockSpec((TM, d), lambda i: (i, 0))],
        out_shape=[jax.ShapeDtypeStruct((t, d), F32), jax.ShapeDtypeStruct((t, d), BF16)],
        compiler_params=_params(1))(o_sb, o_sw, g_sb, g_sw, w_out, h)


def _loss_head(h, g, tgt):
    t, d = h.shape

    def body(h_ref, g_ref, t_ref, dh_ref, dg_ref, loss_ref):
        @pl.when(pl.program_id(0) == 0)
        def _():
            dg_ref[...] = jnp.zeros_like(dg_ref)
            loss_ref[...] = jnp.zeros_like(loss_ref)

        gg = g_ref[...]
        y, xh, r = _rms_fwd(h_ref[...], gg)
        err = y - t_ref[...]
        part = 0.5 * jnp.sum(jnp.sum(err * err, axis=1, keepdims=True) / d, axis=0, keepdims=True)
        loss_ref[...] += jnp.broadcast_to(part, loss_ref.shape)
        dx, dg = _rms_bwd(err / d, xh, r, gg)
        dh_ref[...] = dx
        dg_ref[...] += dg

    return pl.pallas_call(
        body, name="loss_head", grid=(t // TM,),
        in_specs=[pl.BlockSpec((TM, d), lambda i: (i, 0)), pl.BlockSpec((1, d), lambda i: (0, 0)),
                  pl.BlockSpec((TM, d), lambda i: (i, 0))],
        out_specs=[pl.BlockSpec((TM, d), lambda i: (i, 0)), pl.BlockSpec((1, d), lambda i: (0, 0)),
                   pl.BlockSpec((1, LANES), lambda i: (0, 0))],
        out_shape=[jax.ShapeDtypeStruct((t, d), F32), jax.ShapeDtypeStruct((1, d), F32),
                   jax.ShapeDtypeStruct((1, LANES), F32)],
        compiler_params=_params(1))(h, g, tgt)


def _ffn_dact(dh, wdn, gu):
    t, d = dh.shape

    def body(dh_ref, w_ref, gu_ref, o_ref):
        da = 0.5 * _dot_nt(dh_ref[...].astype(BF16), w_ref[...])
        g = gu_ref[0].astype(F32)
        u = gu_ref[1].astype(F32)
        sig = jax.nn.sigmoid(g)
        silu = g * sig
        o_ref[0] = (da * u * (sig * (1.0 + g * (1.0 - sig)))).astype(BF16)
        o_ref[1] = (da * silu).astype(BF16)

    return pl.pallas_call(
        body, name="ffn_dact", grid=(2, t // TM),
        in_specs=[pl.BlockSpec((TM, d), lambda j, i: (i, 0)), pl.BlockSpec((FS, d), lambda j, i: (j, 0)),
                  pl.BlockSpec((2, TM, FS), lambda j, i: (0, i, j))],
        out_specs=pl.BlockSpec((2, TM, FS), lambda j, i: (0, i, j)),
        out_shape=jax.ShapeDtypeStruct((2, t, D_FF), BF16), compiler_params=_params(2))(dh, wdn, gu)


def _dn_norm_bwd(a, a_spec, w, w_spec, nk, dh, h_in, g):
    t, d = dh.shape

    def body(a_ref, w_ref, dh_ref, h_ref, g_ref, o_ref, dg_ref, acc_ref):
        i, k = pl.program_id(0), pl.program_id(1)

        @pl.when(k == 0)
        def _():
            acc_ref[...] = jnp.zeros_like(acc_ref)

        acc_ref[...] += _dot_nt(a_ref[...], w_ref[...])

        @pl.when(k == nk - 1)
        def _():
            gg = g_ref[...]
            _, xh, r = _rms_fwd(h_ref[...], gg)
            dx, dg = _rms_bwd(acc_ref[...], xh, r, gg)
            o_ref[...] = dh_ref[...] + dx

            @pl.when(i == 0)
            def _():
                dg_ref[...] = dg

            @pl.when(i > 0)
            def _():
                dg_ref[...] += dg

    row = pl.BlockSpec((TM, d), lambda i, k: (i, 0))
    return pl.pallas_call(
        body, name="dn_norm_bwd", grid=(t // TM, nk),
        in_specs=[a_spec, w_spec, row, row, pl.BlockSpec((1, d), lambda i, k: (0, 0))],
        out_specs=[row, pl.BlockSpec((1, d), lambda i, k: (0, 0))],
        out_shape=[jax.ShapeDtypeStruct((t, d), F32), jax.ShapeDtypeStruct((1, d), F32)],
        scratch_shapes=[pltpu.VMEM((TM, d), F32)], compiler_params=_params(2))(a, w, dh, h_in, g)


def _ffn_dn(dgu, wgu, dh, h_in, g):
    d = dh.shape[1]
    return _dn_norm_bwd(
        dgu, pl.BlockSpec((None, TM, FS), lambda i, k: (k // 2, i, k % 2)),
        wgu, pl.BlockSpec((None, d, FS), lambda i, k: (k, 0, 0)), N_CHIPS, dh, h_in, g)


def _mix_dn(dproj, w_in, dh, h_in, g):
    d = dh.shape[1]
    w = dproj.shape[1]
    return _dn_norm_bwd(
        dproj, pl.BlockSpec((TM, w), lambda i, k: (i, 0)),
        w_in, pl.BlockSpec((d, w), lambda i, k: (0, 0)), 1, dh, h_in, g)


def _dmixed(dh, w_out, o_sb, o_sw, g_sb, g_sw):
    t, d = dh.shape

    def body(dh_ref, w_ref, a_ref, b_ref, ga_ref, gb_ref, o_ref, dga_ref, dgb_ref):
        i = pl.program_id(0)
        dm = _dot_nt(dh_ref[...].astype(BF16), w_ref[...])
        _, xa, ra = _rms_fwd(a_ref[...], ga_ref[...])
        _, xb, rb = _rms_fwd(b_ref[...], gb_ref[...])
        da, dga = _rms_bwd(dm[:, :SB_W], xa, ra, ga_ref[...])
        db, dgb = _rms_bwd(dm[:, SB_W:], xb, rb, gb_ref[...])
        o_ref[...] = jnp.concatenate([da.astype(BF16), db.astype(BF16)], axis=1)

        @pl.when(i == 0)
        def _():
            dga_ref[...] = dga
            dgb_ref[...] = dgb

        @pl.when(i > 0)
        def _():
            dga_ref[...] += dga
            dgb_ref[...] += dgb

    return pl.pallas_call(
        body, name="dmixed", grid=(t // TM,),
        in_specs=[pl.BlockSpec((TM, d), lambda i: (i, 0)), pl.BlockSpec((d, d), lambda i: (0, 0)),
                  pl.BlockSpec((TM, SB_W), lambda i: (i, 0)), pl.BlockSpec((TM, SWA_W), lambda i: (i, 0)),
                  pl.BlockSpec((1, SB_W), lambda i: (0, 0)), pl.BlockSpec((1, SWA_W), lambda i: (0, 0))],
        out_specs=[pl.BlockSpec((TM, d), lambda i: (i, 0)), pl.BlockSpec((1, SB_W), lambda i: (0, 0)),
                   pl.BlockSpec((1, SWA_W), lambda i: (0, 0))],
        out_shape=[jax.ShapeDtypeStruct((t, d), BF16), jax.ShapeDtypeStruct((1, SB_W), F32),
                   jax.ShapeDtypeStruct((1, SWA_W), F32)],
        compiler_params=_params(1))(dh, w_out, o_sb, o_sw, g_sb, g_sw)


def _wgrad(name, a, a_spec, b, b_spec, grid, out_shape, out_spec, scale):
    def body(a_ref, b_ref, o_ref):
        r = _dot_tn(a_ref[...], b_ref[...].astype(BF16))
        o_ref[...] = r if scale == 1.0 else scale * r

    return pl.pallas_call(
        body, name=name, grid=grid, in_specs=[a_spec, b_spec], out_specs=out_spec,
        out_shape=jax.ShapeDtypeStruct(out_shape, F32), compiler_params=_params(len(grid)))(a, b)


def _wgrad_gu(n, dgu):
    t, d = n.shape
    return _wgrad(
        "wgrad_gu", n, pl.BlockSpec((t, TM), lambda s, r: (0, r)),
        dgu, pl.BlockSpec((None, t, FS), lambda s, r: (s // 2, 0, s % 2)), (N_CHIPS, d // TM),
        (N_CHIPS, d, FS), pl.BlockSpec((None, TM, FS), lambda s, r: (s, r, 0)), 1.0)


def _wgrad_down(act, dh):
    t, d = dh.shape
    return _wgrad(
        "wgrad_down", act, pl.BlockSpec((t, FS), lambda s, r: (0, s)), dh, pl.BlockSpec((t, TM), lambda s, r: (0, r)),
        (2, d // TM), (D_FF, d), pl.BlockSpec((FS, TM), lambda s, r: (s, r)), 0.5)


def _wgrad_out(mixed, dh):
    t, d = dh.shape
    return _wgrad(
        "wgrad_out", mixed, pl.BlockSpec((t, TM), lambda s: (0, s)), dh, pl.BlockSpec((t, d), lambda s: (0, 0)),
        (d // TM,), (d, d), pl.BlockSpec((TM, d), lambda s: (s, 0)), 1.0)


def _wgrad_in(n, dproj):
    t, d = n.shape
    w = dproj.shape[1]
    tw = w // 3
    return _wgrad(
        "wgrad_in", n, pl.BlockSpec((t, d), lambda s: (0, 0)), dproj, pl.BlockSpec((t, tw), lambda s: (0, s)),
        (3,), (d, w), pl.BlockSpec((d, tw), lambda s: (0, s)), 1.0)


def _tri(rel):
    row = lax.broadcasted_iota(jnp.int32, (BLK, BLK), 0)
    col = lax.broadcasted_iota(jnp.int32, (BLK, BLK), 1)
    m = rel(row, col).astype(BF16)
    return jnp.concatenate([m, m], axis=0)


def _scan_dot(x, tri2):
    hi = x.astype(BF16)
    lo = (x - hi.astype(F32)).astype(BF16)
    return _dot(jnp.concatenate([hi, lo], axis=1), tri2)


def _head_masks():
    lane = lax.broadcasted_iota(jnp.int32, (1, LANES), 1)
    return [lane < HEAD_DIM, lane >= HEAD_DIM]


def _sb_dcol():
    dcol = lax.broadcasted_iota(jnp.int32, (BLK, SB_KT), 1) - lax.broadcasted_iota(jnp.int32, (BLK, SB_KT), 0)
    return jnp.concatenate([dcol, dcol], axis=0)


def _sb_fwd(proj):
    t = proj.shape[0]
    nq = t // BLK
    nb = SB_KT // BLK

    def body(q_ref, k_ref, v_ref, o_ref, tot_ref):
        hm = _head_masks()
        dcol = _sb_dcol()
        after = _tri(lambda r, c: r > c)

        def tile(qh, kt, carry, acc, limit):
            ks = pl.ds(pl.multiple_of(kt * SB_KT, SB_KT), SB_KT)
            z = _dot_nt(qh, k_ref[ks, :])
            sp = _softplus(z)
            valid = None if limit is None else dcol < limit
            spm = sp if valid is None else jnp.where(valid, sp, 0.0)
            sufs = [None] * nb
            for b in reversed(range(nb)):
                blk = spm[:, b * BLK:(b + 1) * BLK]
                sufs[b] = carry + _scan_dot(blk, after)
                carry = carry + jnp.sum(blk, axis=1, keepdims=True)
            w = jnp.exp(z - sp - jnp.concatenate(sufs, axis=1))
            if valid is not None:
                w = jnp.where(valid, w, 0.0)
            return carry, acc + _dot(w.astype(BF16), v_ref[ks, :])

        def qblock(qi, _):
            qs = pl.ds(pl.multiple_of(qi * BLK, BLK), BLK)
            q = q_ref[qs, :] * SCALE
            kd = qi // nb
            limit = (qi - kd * nb) * BLK
            qh = jnp.concatenate([jnp.where(m, q, jnp.zeros_like(q)) for m in hm], axis=0)
            c0 = tile(qh, kd, jnp.zeros((2 * BLK, 1), F32), jnp.zeros((2 * BLK, LANES), F32), limit)
            carry, acc = lax.fori_loop(0, kd, lambda n, c: tile(qh, kd - 1 - n, c[0], c[1], None), c0)
            o_ref[qs, :] = jnp.where(hm[0], acc[:BLK], acc[BLK:])
            for h in range(2):
                tot_ref[h, qs, :] = jnp.broadcast_to(carry[h * BLK:(h + 1) * BLK], (BLK, LANES))
            return 0

        lax.fori_loop(0, nq, qblock, 0)

    col_blk = lambda off: pl.BlockSpec((t, LANES), lambda p: (0, off + p))
    return pl.pallas_call(
        body, name="sb_fwd", grid=(4,), in_specs=[col_blk(0), col_blk(4), col_blk(8)],
        out_specs=[pl.BlockSpec((t, LANES), lambda p: (0, p)), pl.BlockSpec((2, t, LANES), lambda p: (p, 0, 0))],
        out_shape=[jax.ShapeDtypeStruct((t, SB_W), F32), jax.ShapeDtypeStruct((8, t, LANES), F32)],
        compiler_params=_params(1))(proj, proj, proj)


def _sb_bwd(proj, d_o, tot):
    t = proj.shape[0]
    nq = t // BLK
    nb = SB_KT // BLK

    def body(q_ref, k_ref, v_ref, do_ref, tot_ref, dq_ref, dk_ref, dv_ref, dk_acc, dv_acc):
        hm = _head_masks()
        dcol = _sb_dcol()
        before = _tri(lambda r, c: r < c)
        upto = _tri(lambda r, c: r <= c)
        dk_acc[...] = jnp.zeros_like(dk_acc)
        dv_acc[...] = jnp.zeros_like(dv_acc)

        def tile(qh, doh, tt, kt, pre, ecum, dq, limit):
            ks = pl.ds(pl.multiple_of(kt * SB_KT, SB_KT), SB_KT)
            k = k_ref[ks, :]
            v = v_ref[ks, :]
            z = _dot_nt(qh, k)
            sp = _softplus(z)
            valid = None if limit is None else dcol < limit
            spm = sp if valid is None else jnp.where(valid, sp, 0.0)
            pres = []
            for b in range(nb):
                blk = spm[:, b * BLK:(b + 1) * BLK]
                pres.append(pre + _scan_dot(blk, before))
                pre = pre + jnp.sum(blk, axis=1, keepdims=True)
            logw = z - (tt - jnp.concatenate(pres, axis=1))
            if valid is not None:
                logw = jnp.minimum(logw, 0.0)
            w = jnp.exp(logw)
            if valid is not None:
                w = jnp.where(valid, w, 0.0)
            e = w * _dot_nt(doh, v)
            incs = []
            for b in range(nb):
                blk = e[:, b * BLK:(b + 1) * BLK]
                incs.append(ecum + _scan_dot(blk, upto))
                ecum = ecum + jnp.sum(blk, axis=1, keepdims=True)
            dz = e - jnp.exp(z - sp) * jnp.concatenate(incs, axis=1)
            if valid is not None:
                dz = jnp.where(valid, dz, 0.0)
            dzb = dz.astype(BF16)
            dk_acc[ks, :] += _dot_tn(dzb, qh)
            dv_acc[ks, :] += _dot_tn(w.astype(BF16), doh)
            return pre, ecum, dq + _dot(dzb, k)

        def qblock(qi, _):
            qs = pl.ds(pl.multiple_of(qi * BLK, BLK), BLK)
            q = q_ref[qs, :] * SCALE
            do = do_ref[qs, :]
            kd = qi // nb
            limit = (qi - kd * nb) * BLK
            qh = jnp.concatenate([jnp.where(m, q, jnp.zeros_like(q)) for m in hm], axis=0)
            doh = jnp.concatenate([jnp.where(m, do, jnp.zeros_like(do)) for m in hm], axis=0)
            tt = jnp.concatenate([tot_ref[h, qs, 0:1] for h in range(2)], axis=0)
            c0 = (jnp.zeros((2 * BLK, 1), F32), jnp.zeros((2 * BLK, 1), F32), jnp.zeros((2 * BLK, LANES), F32))
            c = lax.fori_loop(0, kd, lambda kt, c: tile(qh, doh, tt, kt, c[0], c[1], c[2], None), c0)
            dq = tile(qh, doh, tt, kd, c[0], c[1], c[2], limit)[2]
            dq_ref[qs, :] = (jnp.where(hm[0], dq[:BLK], dq[BLK:]) * SCALE).astype(BF16)
            return 0

        lax.fori_loop(0, nq, qblock, 0)
        dk_ref[...] = dk_acc[...].astype(BF16)
        dv_ref[...] = dv_acc[...].astype(BF16)

    col_blk = lambda off: pl.BlockSpec((t, LANES), lambda p: (0, off + p))
    out = jax.ShapeDtypeStruct((t, SB_W), BF16)
    return pl.pallas_call(
        body, name="sb_bwd", grid=(4,),
        in_specs=[col_blk(0), col_blk(4), col_blk(8), col_blk(0), pl.BlockSpec((2, t, LANES), lambda p: (p, 0, 0))],
        out_specs=[col_blk(0), col_blk(0), col_blk(0)], out_shape=[out, out, out],
        scratch_shapes=[pltpu.VMEM((t, LANES), F32), pltpu.VMEM((t, LANES), F32)],
        compiler_params=_params(1))(proj, proj, proj, d_o, tot)


def _bucket_table():
    a = np.arange(BLK)[:, None]
    c = np.arange(2 * BLK)[None, :]
    dist = np.maximum(BLK + a - c, 0)
    max_exact = N_BUCKETS // 2
    dd = np.maximum(dist, 1).astype(np.float32)
    large = max_exact + (np.log(dd / max_exact) / math.log(MAX_DISTANCE / max_exact)
                         * (N_BUCKETS - max_exact)).astype(np.int32)
    large = np.minimum(large, N_BUCKETS - 1)
    return np.where(dist < max_exact, dist, large).astype(np.int32)


def _swa_band_masks():
    row = lax.broadcasted_iota(jnp.int32, (SWA_G * BLK, 2 * BLK), 0) & (BLK - 1)
    col = lax.broadcasted_iota(jnp.int32, (SWA_G * BLK, 2 * BLK), 1)
    own = lax.broadcasted_iota(jnp.int32, (SWA_G * BLK, BLK), 1) <= (
        lax.broadcasted_iota(jnp.int32, (SWA_G * BLK, BLK), 0) & (BLK - 1))
    return (col > row) & ((col < BLK) | (col - BLK <= row)), own


def _swa_stack(ref, qs, kvh, kvmask, scale):
    parts = []
    for g in range(SWA_G):
        hq = SWA_G * kvh + g
        x = ref[qs, (hq // 2) * LANES:(hq // 2 + 1) * LANES].astype(F32)
        if hq % 2 != kvh:
            x = pltpu.roll(x, HEAD_DIM, 1)
        parts.append(jnp.where(kvmask, x * scale, 0.0).astype(BF16))
    return jnp.concatenate(parts, axis=0)


def _swa_unstack(x4, kvh, hm):
    heads = []
    for g in range(SWA_G):
        x = x4[g * BLK:(g + 1) * BLK]
        heads.append(pltpu.roll(x, HEAD_DIM, 1) if g % 2 != kvh else x)
    return [jnp.where(hm[0], heads[0], heads[1]), jnp.where(hm[0], heads[2], heads[3])]


def _swa_scores(q4, kb, bias_ref, kvh, mask, cols):
    bias4 = jnp.concatenate([bias_ref[SWA_G * kvh + g, :, cols] for g in range(SWA_G)], axis=0)
    return jnp.where(mask, _dot_nt(q4, kb) + bias4, NEG_INF)


def _swa_sinks(sink_ref, kvh):
    return jnp.concatenate([jnp.broadcast_to(sink_ref[SWA_G * kvh + g:SWA_G * kvh + g + 1, 0:1], (BLK, 1))
                            for g in range(SWA_G)], axis=0)


def _swa_fwd(proj, bias, sinks_b):
    t = proj.shape[0]
    nq = t // BLK

    def body(q_ref, k_ref, v_ref, bias_ref, sink_ref, o_ref, lse_ref):
        hm = _head_masks()
        band, own = _swa_band_masks()

        def qblock(i, kvh, prev):
            qs = pl.ds(pl.multiple_of(i * BLK, BLK), BLK)
            if prev:
                ks, mask, cols = pl.ds(pl.multiple_of((i - 1) * BLK, BLK), 2 * BLK), band, slice(None)
            else:
                ks, mask, cols = qs, own, slice(BLK, None)
            q4 = _swa_stack(q_ref, qs, kvh, hm[kvh], SCALE)
            sink4 = _swa_sinks(sink_ref, kvh)
            s = _swa_scores(q4, k_ref[ks, :], bias_ref, kvh, mask, cols)
            m = jnp.maximum(jnp.max(s, axis=1, keepdims=True), sink4)
            p = jnp.exp(s - m)
            den = jnp.sum(p, axis=1, keepdims=True) + jnp.exp(sink4 - m)
            o4 = _dot((p * (1.0 / den)).astype(BF16), v_ref[ks, :])
            lse4 = m + jnp.log(den)
            for g in range(SWA_G):
                lse_ref[SWA_G * kvh + g, qs, :] = jnp.broadcast_to(lse4[g * BLK:(g + 1) * BLK], (BLK, LANES))
            for pp, o in enumerate(_swa_unstack(o4, kvh, hm)):
                o_ref[qs, (2 * kvh + pp) * LANES:(2 * kvh + pp + 1) * LANES] = o

        for kvh in range(2):
            qblock(0, kvh, False)

            def step(i, _):
                qblock(i, kvh, True)
                return 0

            lax.fori_loop(1, nq, step, 0)

    return pl.pallas_call(
        body, name="swa_fwd", grid=(1,),
        in_specs=[pl.BlockSpec((t, SWA_W), lambda i: (0, 3)), pl.BlockSpec((t, KV_W), lambda i: (0, 16)),
                  pl.BlockSpec((t, KV_W), lambda i: (0, 17)), pl.BlockSpec((8, BLK, 2 * BLK), lambda i: (0, 0, 0)),
                  pl.BlockSpec((8, LANES), lambda i: (0, 0))],
        out_specs=[pl.BlockSpec((t, SWA_W), lambda i: (0, 0)), pl.BlockSpec((8, t, LANES), lambda i: (0, 0, 0))],
        out_shape=[jax.ShapeDtypeStruct((t, SWA_W), F32), jax.ShapeDtypeStruct((8, t, LANES), F32)],
        compiler_params=_params(1))(proj, proj, proj, bias, sinks_b)


def _swa_bwd(proj, d_o, lse, bias, sinks_b, dbias_in):
    t = proj.shape[0]
    nq = t // BLK

    def body(q_ref, k_ref, v_ref, do_ref, lse_ref, bias_ref, sink_ref, dbi_ref,
             dq_ref, dk_ref, dv_ref, dsink_ref, dbias_ref, dk_acc, dv_acc):
        hm = _head_masks()
        band, own = _swa_band_masks()
        dk_acc[...] = jnp.zeros_like(dk_acc)
        dv_acc[...] = jnp.zeros_like(dv_acc)
        dbias_ref[...] = dbi_ref[...]

        def qblock(i, kvh, prev, dsink4):
            qs = pl.ds(pl.multiple_of(i * BLK, BLK), BLK)
            if prev:
                ks, mask, cols = pl.ds(pl.multiple_of((i - 1) * BLK, BLK), 2 * BLK), band, slice(None)
            else:
                ks, mask, cols = qs, own, slice(BLK, None)
            q4 = _swa_stack(q_ref, qs, kvh, hm[kvh], SCALE)
            do4 = _swa_stack(do_ref, qs, kvh, hm[kvh], 1.0)
            sink4 = _swa_sinks(sink_ref, kvh)
            lse4 = jnp.concatenate([lse_ref[SWA_G * kvh + g, qs, 0:1] for g in range(SWA_G)], axis=0)
            kb = k_ref[ks, :]
            p = jnp.exp(_swa_scores(q4, kb, bias_ref, kvh, mask, cols) - lse4)
            dp = _dot_nt(do4, v_ref[ks, :])
            delta = jnp.sum(p * dp, axis=1, keepdims=True)
            ds = p * (dp - delta)
            for g in range(SWA_G):
                dbias_ref[SWA_G * kvh + g, :, cols] += ds[g * BLK:(g + 1) * BLK]
            dsb = ds.astype(BF16)
            dk_acc[ks, :] += _dot_tn(dsb, q4)
            dv_acc[ks, :] += _dot_tn(p.astype(BF16), do4)
            for pp, dq in enumerate(_swa_unstack(_dot(dsb, kb) * SCALE, kvh, hm)):
                dq_ref[qs, (2 * kvh + pp) * LANES:(2 * kvh + pp + 1) * LANES] = dq.astype(BF16)
            return dsink4 - jnp.exp(sink4 - lse4) * delta

        for kvh in range(2):
            ds0 = qblock(0, kvh, False, jnp.zeros((SWA_G * BLK, 1), F32))
            ds4 = lax.fori_loop(1, nq, lambda i, c: qblock(i, kvh, True, c), ds0)
            for g in range(SWA_G):
                hq = SWA_G * kvh + g
                dsink_ref[hq:hq + 1, :] = jnp.broadcast_to(
                    jnp.sum(ds4[g * BLK:(g + 1) * BLK], axis=0, keepdims=True), (1, LANES))

        dk_ref[...] = dk_acc[...].astype(BF16)
        dv_ref[...] = dv_acc[...].astype(BF16)

    full3 = pl.BlockSpec((8, BLK, 2 * BLK), lambda i: (0, 0, 0))
    kv = jax.ShapeDtypeStruct((t, KV_W), BF16)
    return pl.pallas_call(
        body, name="swa_bwd", grid=(1,),
        in_specs=[pl.BlockSpec((t, SWA_W), lambda i: (0, 3)), pl.BlockSpec((t, KV_W), lambda i: (0, 16)),
                  pl.BlockSpec((t, KV_W), lambda i: (0, 17)), pl.BlockSpec((t, SWA_W), lambda i: (0, 1)),
                  pl.BlockSpec((8, t, LANES), lambda i: (0, 0, 0)), full3, pl.BlockSpec((8, LANES), lambda i: (0, 0)),
                  full3],
        out_specs=[pl.BlockSpec((t, SWA_W), lambda i: (0, 0)), pl.BlockSpec((t, KV_W), lambda i: (0, 0)),
                   pl.BlockSpec((t, KV_W), lambda i: (0, 0)), pl.BlockSpec((8, LANES), lambda i: (0, 0)), full3],
        out_shape=[jax.ShapeDtypeStruct((t, SWA_W), BF16), kv, kv, jax.ShapeDtypeStruct((8, LANES), F32),
                   jax.ShapeDtypeStruct((8, BLK, 2 * BLK), F32)],
        scratch_shapes=[pltpu.VMEM((t, KV_W), F32), pltpu.VMEM((t, KV_W), F32)],
        compiler_params=_params(1))(proj, proj, proj, d_o, lse, bias, sinks_b, dbias_in)


def _bias_table(rel_bias, buckets):
    def body(rb_ref, b_ref, o_ref):
        bk = b_ref[...]
        for h in range(8):
            acc = jnp.zeros((BLK, 2 * BLK), F32)
            for b in range(N_BUCKETS):
                acc = jnp.where(bk == b, rb_ref[b, h], acc)
            o_ref[h] = acc

    return pl.pallas_call(
        body, name="bias_table", grid=(1,),
        in_specs=[pl.BlockSpec(memory_space=pltpu.SMEM), pl.BlockSpec((BLK, 2 * BLK), lambda i: (0, 0))],
        out_specs=pl.BlockSpec((8, BLK, 2 * BLK), lambda i: (0, 0, 0)),
        out_shape=jax.ShapeDtypeStruct((8, BLK, 2 * BLK), F32), compiler_params=_params(1))(rel_bias, buckets)


def _bias_grad(dbias, buckets):
    def body(d_ref, b_ref, o_ref):
        lane = lax.broadcasted_iota(jnp.int32, (1, LANES), 1)
        bk = b_ref[...]
        for h in range(8):
            d = d_ref[h]
            acc = jnp.zeros((1, LANES), F32)
            for b in range(N_BUCKETS):
                s = jnp.sum(jnp.sum(jnp.where(bk == b, d, 0.0), axis=0, keepdims=True), axis=1, keepdims=True)
                acc = acc + jnp.where(lane == b, s, 0.0)
            o_ref[h:h + 1, :] = acc

    return pl.pallas_call(
        body, name="bias_grad", grid=(1,),
        in_specs=[pl.BlockSpec((8, BLK, 2 * BLK), lambda i: (0, 0, 0)), pl.BlockSpec((BLK, 2 * BLK), lambda i: (0, 0))],
        out_specs=pl.BlockSpec((8, LANES), lambda i: (0, 0)),
        out_shape=jax.ShapeDtypeStruct((8, LANES), F32), compiler_params=_params(1))(dbias, buckets)


def _row(a):
    return a.reshape(1, -1)


def _fwd_ffn1_sb(h, w, small, l):
    s = {"h0": h}
    s["n1"] = _norm_cast(h, _row(small["norm_ffn1"][l]))
    s["gu1"], s["act1"] = _ffn_gu(s["n1"], w["ffn1_gu"])
    s["h1"] = _down_res(s["act1"], w["ffn1_down"], h)
    s["nm"] = _norm_cast(s["h1"], _row(small["norm_mix"][l]))
    s["proj"] = _proj(s["nm"], w["w_in"])
    s["o_sb"], s["tot"] = _sb_fwd(s["proj"])
    return s


def _fwd_swa_ffn2(s, w, small, l, bias):
    s["sinks_b"] = jnp.broadcast_to(small["sinks"][l][:, None], (8, LANES))
    s["o_sw"], s["lse"] = _swa_fwd(s["proj"], bias, s["sinks_b"])
    s["h2"], s["mixed"] = _out_res(s["o_sb"], s["o_sw"], _row(small["norm_out_sb"][l]), _row(small["norm_out_swa"][l]),
                                  w["w_out"], s["h1"])
    s["n2"] = _norm_cast(s["h2"], _row(small["norm_ffn2"][l]))
    s["gu2"], s["act2"] = _ffn_gu(s["n2"], w["ffn2_gu"])
    return _down_res(s["act2"], w["ffn2_down"], s["h2"])


def _bwd_ffn(dh, s, w, small, l, which):
    h_in, norm = (s["h0"], "norm_ffn1") if which == 1 else (s["h2"], "norm_ffn2")
    dgu = _ffn_dact(dh, w[f"ffn{which}_down"], s[f"gu{which}"])
    g_down = _wgrad_down(s[f"act{which}"], dh)
    g_gu = _wgrad_gu(s[f"n{which}"], dgu)
    dh, dg = _ffn_dn(dgu, w[f"ffn{which}_gu"], dh, h_in, _row(small[norm][l]))
    return dh, {f"ffn{which}_down": g_down, f"ffn{which}_gu": g_gu}, {norm: dg}


def _bwd_mix(dh, s, w, small, l, bias, dbias):
    g_out = _wgrad_out(s["mixed"], dh)
    d_o, dg_sb, dg_sw = _dmixed(dh, w["w_out"], s["o_sb"], s["o_sw"], _row(small["norm_out_sb"][l]),
                                _row(small["norm_out_swa"][l]))
    dq_sb, dk_sb, dv_sb = _sb_bwd(s["proj"], d_o, s["tot"])
    dq_sw, dk_sw, dv_sw, dsink, dbias = _swa_bwd(s["proj"], d_o, s["lse"], bias, s["sinks_b"], dbias)
    dproj = jnp.concatenate([dq_sb, dk_sb, dv_sb, dq_sw, dk_sw, dv_sw], axis=1)
    g_in = _wgrad_in(s["nm"], dproj)
    dh, dg_mix = _mix_dn(dproj, w["w_in"], dh, s["h1"], _row(small["norm_mix"][l]))
    gs = {"norm_out_sb": dg_sb, "norm_out_swa": dg_sw, "sinks": dsink[:, 0], "norm_mix": dg_mix}
    return dh, {"w_out": g_out, "w_in": g_in}, gs, dbias


def _place():
    x, y, c = lax.axis_index("x"), lax.axis_index("y"), lax.axis_index("c")
    return x, y, c, 2 * x + y


def _chip_core(k, c):
    return (k // 2, k % 2, c)


def _rows_per_block(rows, cols, copies):
    best = 16
    for tr in range(16, rows + 1, 16):
        if rows % tr == 0 and copies * tr * cols * 4 <= 6 * 2 ** 20:
            best = tr
    assert rows % best == 0
    return best


def _place_own(w, l, me1):
    _, rows, cols = w.shape
    tr = _rows_per_block(rows, cols, 1)

    def body(me_ref, w_ref, o_ref):
        o_ref[...] = w_ref[...].astype(BF16)

    return pl.pallas_call(
        body, name="place_own",
        grid_spec=pltpu.PrefetchScalarGridSpec(
            num_scalar_prefetch=1, grid=(rows // tr,),
            in_specs=[pl.BlockSpec((None, tr, cols), lambda r, me: (l, r, 0))],
            out_specs=pl.BlockSpec((None, tr, cols), lambda r, me: (me[0], r, 0))),
        out_shape=jax.ShapeDtypeStruct((N_CHIPS, rows, cols), BF16), compiler_params=_params(1))(me1, w)


def _plan_gather_ici(bufs):
    _, _, c, me = _place()
    return [(b.at[me, c], b.at[me, c], b.at[(me + 3 - j) % N_CHIPS, c], _chip_core((me + 1 + j) % N_CHIPS, c))
            for b in bufs for j in range(3)]


def _plan_gather_d2d(bufs):
    x, y, c, me = _place()
    return [(b.at[(me + 3 - j) % N_CHIPS, c], b.at[(me + 3 - j) % N_CHIPS, c], b.at[(me + 3 - j) % N_CHIPS, 1 - c],
             (x, y, 1 - c)) for b in bufs for j in range(3)]


def _plan_grad_sibling(bufs):
    x, y, c, _ = _place()
    n = len(bufs) // 2
    return [(g.at[:, 1 - c], z, z, (x, y, 1 - c)) for g, z in zip(bufs[:n], bufs[n:])]


def _plan_grad_chips(bufs):
    _, _, c, me = _place()
    n = len(bufs) // 2
    return [(p.at[(me + 1 + j) % N_CHIPS], z.at[j], z.at[j], _chip_core((me + 1 + j) % N_CHIPS, c))
            for p, z in zip(bufs[:n], bufs[n:]) for j in range(3)]


def _plan_grad_halves(bufs):
    x, y, c, _ = _place()
    return [(b.at[c], b.at[c], b.at[1 - c], (x, y, 1 - c)) for b in bufs]


def _remote(src, dst, send_sem, recv_sem, to):
    return pltpu.make_async_remote_copy(src_ref=src, dst_ref=dst, send_sem=send_sem, recv_sem=recv_sem,
                                        device_id=to, device_id_type=MESH)


def _exchange_now(name, plan, bufs, n_copies):
    n = len(bufs)

    def body(*refs):
        outs, (ssem, rsem) = refs[n:2 * n], refs[2 * n:]
        copies = plan(outs)
        for i, (src, dst, _, to) in enumerate(copies):
            _remote(src, dst, ssem.at[i], rsem.at[i], to).start()
        for i, (src, dst, land, to) in enumerate(copies):
            _remote(land, land, ssem.at[i], rsem.at[i], to).wait_recv()
        for i, (src, dst, _, to) in enumerate(copies):
            _remote(src, dst, ssem.at[i], rsem.at[i], to).wait_send()

    return pl.pallas_call(
        body, name=name, in_specs=[ANY] * n, out_specs=[ANY] * n,
        out_shape=[jax.ShapeDtypeStruct(a.shape, a.dtype) for a in bufs],
        input_output_aliases={t: t for t in range(n)},
        scratch_shapes=[pltpu.SemaphoreType.DMA((n_copies,))] * 2,
        compiler_params=pltpu.CompilerParams(vmem_limit_bytes=V7X_VMEM_LIMIT))(*bufs)


def _exchange_start(name, plan, bufs, n_copies, after):
    n = len(bufs)

    def body(*refs):
        ins = refs[:n]
        ssem, rsem = refs[n + 1], refs[n + 2]
        token = refs[-1]
        for i, (src, dst, _, to) in enumerate(plan(ins)):
            _remote(src, dst, ssem.at[i], rsem.at[i], to).start()
        token[...] = jnp.zeros_like(token)

    out = pl.pallas_call(
        body, name=name,
        out_shape=(pltpu.SemaphoreType.DMA((n_copies,)), pltpu.SemaphoreType.DMA((n_copies,)),
                   *[pltpu.HBM(a.shape, a.dtype) for a in bufs], jax.ShapeDtypeStruct((8, LANES), F32)),
        in_specs=[HBM] * n + [ANY], out_specs=(SEM, SEM, *[HBM] * n, pl.BlockSpec(memory_space=pltpu.VMEM)),
        input_output_aliases={t: 2 + t for t in range(n)},
        compiler_params=pltpu.CompilerParams(has_side_effects=EFFECT),
    )(*[pltpu.with_memory_space_constraint(a, pltpu.HBM) for a in bufs], after)
    return (out[0], out[1]), list(out[2:2 + n]), out[-1]


def _exchange_wait(name, plan, bufs, sems, n_copies, after):
    n = len(bufs)

    def body(*refs):
        ins = refs[:n]
        ssem, rsem = refs[n], refs[n + 1]
        for i, (src, dst, land, to) in enumerate(plan(ins)):
            _remote(src, dst, ssem.at[i], rsem.at[i], to).wait_send()
            _remote(land, land, ssem.at[i], rsem.at[i], to).wait_recv()

    return list(pl.pallas_call(
        body, name=name, out_shape=[pltpu.HBM(a.shape, a.dtype) for a in bufs],
        in_specs=[HBM] * n + [SEM, SEM, ANY], out_specs=[HBM] * n,
        input_output_aliases={t: t for t in range(n)},
        compiler_params=pltpu.CompilerParams(has_side_effects=EFFECT),
    )(*bufs, sems[0], sems[1], after))


def _tie(x, token):
    return lax.optimization_barrier((x, token))[0]


def _gather_now(bufs):
    n = len(bufs)
    n_cp = 3 * n

    def body(*refs):
        outs = refs[n:2 * n]
        ici_s, ici_r, d2d_s, d2d_r = refs[2 * n:]
        first = _plan_gather_ici(outs)
        second = _plan_gather_d2d(outs)
        for i, (src, dst, _, to) in enumerate(first):
            _remote(src, dst, ici_s.at[i], ici_r.at[i], to).start()
        for i, (src, dst, _, to) in enumerate(second):
            land = first[i][2]
            _remote(land, land, ici_s.at[i], ici_r.at[i], to).wait_recv()
            _remote(src, dst, d2d_s.at[i], d2d_r.at[i], to).start()
        for i, (_, _, land, to) in enumerate(second):
            _remote(land, land, d2d_s.at[i], d2d_r.at[i], to).wait_recv()
        for i in range(n_cp):
            _remote(first[i][0], first[i][1], ici_s.at[i], ici_r.at[i], first[i][3]).wait_send()
            _remote(second[i][0], second[i][1], d2d_s.at[i], d2d_r.at[i], second[i][3]).wait_send()

    return pl.pallas_call(
        body, name="gather_layer0", in_specs=[ANY] * n, out_specs=[ANY] * n,
        out_shape=[jax.ShapeDtypeStruct(a.shape, a.dtype) for a in bufs],
        input_output_aliases={t: t for t in range(n)},
        scratch_shapes=[pltpu.SemaphoreType.DMA((n_cp,))] * 4,
        compiler_params=pltpu.CompilerParams(vmem_limit_bytes=V7X_VMEM_LIMIT))(*bufs)


def _chip_sum(g, xbuf, cm):
    _, _, r2, cols = g.shape
    tr = _rows_per_block(r2, cols, N_CHIPS)

    def body(cm_ref, g_ref, x_ref, pb_ref, po_ref):
        pb_ref[...] = (g_ref[...] + x_ref[...]).astype(BF16)
        me = cm_ref[1]
        po_ref[...] = g_ref[me] + x_ref[me]

    return pl.pallas_call(
        body, name="grad_chip_sum",
        grid_spec=pltpu.PrefetchScalarGridSpec(
            num_scalar_prefetch=1, grid=(r2 // tr,),
            in_specs=[pl.BlockSpec((N_CHIPS, None, tr, cols), lambda r, cm: (0, cm[0], r, 0)),
                      pl.BlockSpec((N_CHIPS, tr, cols), lambda r, cm: (0, r, 0))],
            out_specs=[pl.BlockSpec((N_CHIPS, tr, cols), lambda r, cm: (0, r, 0)),
                       pl.BlockSpec((tr, cols), lambda r, cm: (r, 0))]),
        out_shape=[jax.ShapeDtypeStruct((N_CHIPS, r2, cols), BF16), jax.ShapeDtypeStruct((r2, cols), F32)],
        compiler_params=_params(1))(cm, g, xbuf)


def _total_sum(pown, rbuf, cm):
    r2, cols = pown.shape
    tr = _rows_per_block(r2, cols, 3)

    def body(cm_ref, p_ref, r_ref, o_ref):
        acc = p_ref[...]
        for j in range(3):
            acc = acc + r_ref[j].astype(F32)
        o_ref[...] = acc

    return pl.pallas_call(
        body, name="grad_total_sum",
        grid_spec=pltpu.PrefetchScalarGridSpec(
            num_scalar_prefetch=1, grid=(r2 // tr,),
            in_specs=[pl.BlockSpec((tr, cols), lambda r, cm: (r, 0)),
                      pl.BlockSpec((3, tr, cols), lambda r, cm: (0, r, 0))],
            out_specs=pl.BlockSpec((None, tr, cols), lambda r, cm: (cm[0], r, 0))),
        out_shape=jax.ShapeDtypeStruct((2, r2, cols), F32), compiler_params=_params(1))(cm, pown, rbuf)


def _small_allreduce(v):
    rows = v.shape[0]
    n_dev = 2 * N_CHIPS

    def body(v_ref, o_ref, buf, ssem, rsem):
        x, y, c, _ = _place()
        me = 4 * x + 2 * y + c
        buf[me] = v_ref[...]

        def copy(d, slot, to):
            return _remote(v_ref, buf.at[slot], ssem.at[d - 1], rsem.at[d - 1], (to // 4, (to // 2) % 2, to % 2))

        cps = [copy(d, me, (me + d) % n_dev) for d in range(1, n_dev)]
        for cp in cps:
            cp.start()
        for d in range(1, n_dev):
            copy(d, (me + n_dev - d) % n_dev, me).wait_recv()
        for cp in cps:
            cp.wait_send()
        acc = buf[0]
        for i in range(1, n_dev):
            acc = acc + buf[i]
        o_ref[...] = acc

    vm = pl.BlockSpec(memory_space=pltpu.VMEM)
    return pl.pallas_call(
        body, name="small_allreduce", in_specs=[vm], out_specs=vm,
        out_shape=jax.ShapeDtypeStruct(v.shape, F32),
        scratch_shapes=[pltpu.VMEM((n_dev, rows, LANES), F32), pltpu.SemaphoreType.DMA((n_dev - 1,)),
                        pltpu.SemaphoreType.DMA((n_dev - 1,))],
        compiler_params=pltpu.CompilerParams(vmem_limit_bytes=V7X_VMEM_LIMIT))(v)


def _adamw_math(w, g, m, v):
    m2 = ADAM_B1 * m + (1.0 - ADAM_B1) * g
    v2 = ADAM_B2 * v + (1.0 - ADAM_B2) * (g * g)
    m_hat = m2 / (1.0 - ADAM_B1 ** ADAM_STEP)
    v_hat = v2 / (1.0 - ADAM_B2 ** ADAM_STEP)
    return -ADAM_LR * (m_hat / (jnp.sqrt(v_hat) + ADAM_EPS) + ADAM_WD * w), m2, v2


def _adamw_layer(w, g, m, v, l, prev):
    _, rows, cols = w.shape
    tr = rows
    for cand in range(8, rows + 1, 8):
        if rows % cand == 0 and cand * cols * 4 <= 2 ** 21:
            tr = cand

    def body(w_ref, g_ref, m_ref, v_ref, *outs):
        go_ref, d_ref, m2_ref, v2_ref = outs[-4:]
        g = g_ref[...]
        go_ref[...] = g
        d_ref[...], m2_ref[...], v2_ref[...] = _adamw_math(w_ref[...], g, m_ref[...], v_ref[...])

    stack = pl.BlockSpec((None, tr, cols), lambda i: (l, i, 0))
    ins, specs, alias = [w, g, m, v], [stack, pl.BlockSpec((tr, cols), lambda i: (i, 0)), stack, stack], {}
    if prev is not None:
        ins += list(prev)
        specs += [ANY] * 4
        alias = {4 + i: i for i in range(4)}
    return pl.pallas_call(
        body, name="adamw", grid=(rows // tr,), in_specs=specs, out_specs=[stack] * 4,
        out_shape=[jax.ShapeDtypeStruct(w.shape, F32)] * 4, input_output_aliases=alias,
        compiler_params=_params(1))(*ins)


def _adamw_small(w, g, m, v):
    def body(w_ref, g_ref, m_ref, v_ref, d_ref, m2_ref, v2_ref):
        d_ref[...], m2_ref[...], v2_ref[...] = _adamw_math(w_ref[...], g_ref[...], m_ref[...], v_ref[...])

    spec = pl.BlockSpec(w.shape, lambda i: (0, 0))
    return pl.pallas_call(
        body, name="adamw_small", grid=(1,), in_specs=[spec] * 4, out_specs=[spec] * 3,
        out_shape=[jax.ShapeDtypeStruct(w.shape, F32)] * 3, compiler_params=_params(1))(w, g, m, v)


SMALL = ("norm_ffn1", "norm_mix", "sinks", "norm_out_sb", "norm_out_swa", "norm_ffn2", "rel_bias", "norm_final")
BIG = ("ffn1_gu", "ffn1_down", "w_in", "w_out", "ffn2_gu", "ffn2_down")


def _pack(parts):
    rows = []
    for a in parts:
        a = a.reshape(-1).astype(F32)
        rows.append(jnp.pad(a, (0, -a.shape[0] % LANES)).reshape(-1, LANES))
    out = jnp.concatenate(rows, axis=0)
    return jnp.pad(out, ((0, -out.shape[0] % 8), (0, 0)))


def _unpack(packed, like):
    out, r = [], 0
    for a in like:
        n = math.prod(a.shape)
        nr = -(-n // LANES)
        out.append(packed[r:r + nr].reshape(-1)[:n].reshape(a.shape))
        r += nr
    return out


def _halved(a):
    k, r, cols = a.shape
    return a.reshape(k, 2, r // 2, cols)


def _layer_weights(bufs):
    full = {k: a.reshape(N_CHIPS, a.shape[2] * 2, a.shape[3]) for k, a in zip(BIG, bufs)}
    d = D_MODEL
    return {
        "ffn1_gu": full["ffn1_gu"], "ffn2_gu": full["ffn2_gu"],
        "ffn1_down": full["ffn1_down"].reshape(D_FF, d), "ffn2_down": full["ffn2_down"].reshape(D_FF, d),
        "w_out": full["w_out"].reshape(d, d),
        "w_in": jnp.transpose(full["w_in"], (1, 0, 2)).reshape(d, IN_W),
    }


def _grad_stacks(gw):
    d = D_MODEL
    stacks = {
        "ffn1_gu": gw["ffn1_gu"], "ffn2_gu": gw["ffn2_gu"],
        "ffn1_down": gw["ffn1_down"].reshape(N_CHIPS, D_FF // N_CHIPS, d),
        "ffn2_down": gw["ffn2_down"].reshape(N_CHIPS, D_FF // N_CHIPS, d),
        "w_out": gw["w_out"].reshape(N_CHIPS, d // N_CHIPS, d),
        "w_in": jnp.transpose(gw["w_in"].reshape(d, N_CHIPS, IN_W // N_CHIPS), (1, 0, 2)),
    }
    return [_halved(stacks[k]) for k in BIG]


def _empty_like_hbm(shape, dtype):
    return pltpu.with_memory_space_constraint(lax.empty(shape, dtype), pltpu.HBM)


def kernel(x, norm_ffn1, w_ffn1_gu, w_ffn1_down, norm_mix, w_in, sinks, norm_out_sb, norm_out_swa, w_out, norm_ffn2, w_ffn2_gu, w_ffn2_down, rel_bias, norm_final, loss_target, m_norm_ffn1, m_w_ffn1_gu, m_w_ffn1_down, m_norm_mix, m_w_in, m_sinks, m_norm_out_sb, m_norm_out_swa, m_w_out, m_norm_ffn2, m_w_ffn2_gu, m_w_ffn2_down, m_rel_bias, m_norm_final, v_norm_ffn1, v_w_ffn1_gu, v_w_ffn1_down, v_norm_mix, v_w_in, v_sinks, v_norm_out_sb, v_norm_out_swa, v_w_out, v_norm_ffn2, v_w_ffn2_gu, v_w_ffn2_down, v_rel_bias, v_norm_final):
    big_w = dict(ffn1_gu=w_ffn1_gu, ffn1_down=w_ffn1_down, w_in=w_in, w_out=w_out, ffn2_gu=w_ffn2_gu, ffn2_down=w_ffn2_down)
    big_m = dict(ffn1_gu=m_w_ffn1_gu, ffn1_down=m_w_ffn1_down, w_in=m_w_in, w_out=m_w_out, ffn2_gu=m_w_ffn2_gu, ffn2_down=m_w_ffn2_down)
    big_v = dict(ffn1_gu=v_w_ffn1_gu, ffn1_down=v_w_ffn1_down, w_in=v_w_in, w_out=v_w_out, ffn2_gu=v_w_ffn2_gu, ffn2_down=v_w_ffn2_down)
    small = dict(norm_ffn1=norm_ffn1, norm_mix=norm_mix, sinks=sinks, norm_out_sb=norm_out_sb, norm_out_swa=norm_out_swa,
                 norm_ffn2=norm_ffn2, rel_bias=rel_bias, norm_final=norm_final)
    small_m = dict(norm_ffn1=m_norm_ffn1, norm_mix=m_norm_mix, sinks=m_sinks, norm_out_sb=m_norm_out_sb,
                   norm_out_swa=m_norm_out_swa, norm_ffn2=m_norm_ffn2, rel_bias=m_rel_bias, norm_final=m_norm_final)
    small_v = dict(norm_ffn1=v_norm_ffn1, norm_mix=v_norm_mix, sinks=v_sinks, norm_out_sb=v_norm_out_sb,
                   norm_out_swa=v_norm_out_swa, norm_ffn2=v_norm_ffn2, rel_bias=v_rel_bias, norm_final=v_norm_final)
    _, _, c, me = _place()
    cm = jnp.stack([c, me]).astype(jnp.int32)
    n_big = len(BIG)
    buckets = jnp.asarray(_bucket_table())
    bias = _bias_table(rel_bias, buckets)

    placed = [[_halved(_place_own(big_w[k], l, cm[1:])) for k in BIG] for l in range(DEPTH)]
    gathered0 = _gather_now(placed[0])
    w0 = _layer_weights(gathered0)
    sems, bufs, token = _exchange_start("gather1_ici_start", _plan_gather_ici, placed[1], 3 * n_big, gathered0[0])
    s0 = _fwd_ffn1_sb(_tie(x[0], token), w0, small, 0)
    bufs = _exchange_wait("gather1_ici_wait", _plan_gather_ici, bufs, sems, 3 * n_big, s0["o_sb"])
    sems, bufs, token = _exchange_start("gather1_d2d_start", _plan_gather_d2d, bufs, 3 * n_big, s0["o_sb"])
    s0["proj"] = _tie(s0["proj"], token)
    h = _fwd_swa_ffn2(s0, w0, small, 0, bias)
    w1 = _layer_weights(_exchange_wait("gather1_d2d_wait", _plan_gather_d2d, bufs, sems, 3 * n_big, h))
    s1 = _fwd_ffn1_sb(h, w1, small, 1)
    h = _fwd_swa_ffn2(s1, w1, small, 1, bias)
    dh, dg_final, loss_row = _loss_head(h, _row(norm_final), loss_target[0])

    gsm = [dict() for _ in range(DEPTH)]
    dbias = jnp.zeros((8, BLK, 2 * BLK), F32)
    dh, gw1, gs = _bwd_ffn(dh, s1, w1, small, 1, 2)
    gsm[1].update(gs)
    dh, gw, gs, dbias = _bwd_mix(dh, s1, w1, small, 1, bias, dbias)
    gw1.update(gw)
    gsm[1].update(gs)
    dh, gw, gs = _bwd_ffn(dh, s1, w1, small, 1, 1)
    gw1.update(gw)
    gsm[1].update(gs)

    def landing(stacks, lead, dtype):
        return [_empty_like_hbm((lead,) + a.shape[2:], dtype) for a in stacks]

    g1 = _grad_stacks(gw1)
    sems, bufs, token = _exchange_start("grad1_sibling_start", _plan_grad_sibling, g1 + landing(g1, N_CHIPS, F32),
                                        n_big, dh)
    dh, gw0, gs = _bwd_ffn(_tie(dh, token), s0, w0, small, 0, 2)
    gsm[0].update(gs)
    bufs = _exchange_wait("grad1_sibling_wait", _plan_grad_sibling, bufs, sems, n_big, dh)
    sums1 = [_chip_sum(g, z, cm) for g, z in zip(bufs[:n_big], bufs[n_big:])]
    parts = [s[0] for s in sums1]
    sems, bufs, token = _exchange_start("grad1_chips_start", _plan_grad_chips, parts + landing(g1, 3, BF16),
                                        3 * n_big, dh)
    dh, gw, gs, dbias = _bwd_mix(_tie(dh, token), s0, w0, small, 0, bias, dbias)
    gw0.update(gw)
    gsm[0].update(gs)
    bufs = _exchange_wait("grad1_chips_wait", _plan_grad_chips, bufs, sems, 3 * n_big, dh)
    halves1 = [_total_sum(s[1], z, cm) for s, z in zip(sums1, bufs[n_big:])]
    sems, bufs, token = _exchange_start("grad1_halves_start", _plan_grad_halves, halves1, n_big, dh)
    dh, gw, gs = _bwd_ffn(_tie(dh, token), s0, w0, small, 0, 1)
    gw0.update(gw)
    gsm[0].update(gs)
    reduced1 = _exchange_wait("grad1_halves_wait", _plan_grad_halves, bufs, sems, n_big, dh)

    g0 = _grad_stacks(gw0)
    bufs = _exchange_now("grad0_sibling", _plan_grad_sibling, g0 + landing(g0, N_CHIPS, F32), n_big)
    sums0 = [_chip_sum(g, z, cm) for g, z in zip(bufs[:n_big], bufs[n_big:])]
    bufs = _exchange_now("grad0_chips", _plan_grad_chips, [s[0] for s in sums0] + landing(g0, 3, BF16), 3 * n_big)
    halves0 = [_total_sum(s[1], z, cm) for s, z in zip(sums0, bufs[n_big:])]
    reduced0 = _exchange_now("grad0_halves", _plan_grad_halves, halves0, n_big)

    gsmall = {k: jnp.stack([gsm[l][k].reshape(-1) for l in range(DEPTH)]) for k in gsm[0]}
    gsmall["rel_bias"] = jnp.transpose(_bias_grad(dbias, buckets)[:, :N_BUCKETS])
    gsmall["norm_final"] = dg_final.reshape(-1)
    red = _small_allreduce(_pack([gsmall[k] for k in SMALL] + [loss_row[0, :1]]))
    small_like = [small[k] for k in SMALL]
    gs = _unpack(red, small_like + [loss_row[0, :1]])
    loss = gs[-1][0]
    gs = dict(zip(SMALL, gs[:-1]))

    out_g, out_d, out_m, out_v = {}, {}, {}, {}
    for i, k in enumerate(BIG):
        shp = big_w[k].shape
        stacks = _adamw_layer(big_w[k], reduced1[i].reshape(shp[1:]), big_m[k], big_v[k], 1, None)
        stacks = _adamw_layer(big_w[k], reduced0[i].reshape(shp[1:]), big_m[k], big_v[k], 0, stacks)
        out_g[k], out_d[k], out_m[k], out_v[k] = stacks
    pk = lambda dct: _pack([dct[k] for k in SMALL])
    dlt, m2, v2 = _adamw_small(pk(small), pk(gs), pk(small_m), pk(small_v))
    for dst, packed in ((out_d, dlt), (out_m, m2), (out_v, v2)):
        dst.update(zip(SMALL, _unpack(packed, small_like)))
    out_g.update(gs)

    order = ("norm_ffn1", "ffn1_gu", "ffn1_down", "norm_mix", "w_in", "sinks", "norm_out_sb", "norm_out_swa", "w_out",
             "norm_ffn2", "ffn2_gu", "ffn2_down", "rel_bias", "norm_final")
    return (loss, dh.reshape(x.shape), *[out_g[k] for k in order], *[out_d[k] for k in order],
            *[out_m[k] for k in order], *[out_v[k] for k in order])
```

```python
import math

import numpy as np
import jax
import jax.numpy as jnp
from jax import lax
from jax.experimental import pallas as pl
from jax.experimental.pallas import tpu as pltpu

F32 = jnp.float32
BF16 = jnp.bfloat16

D_MODEL = 1024
DEPTH = 2
HEAD_DIM = 64
BLK = 128
N_BUCKETS = 32
MAX_DISTANCE = 128
D_FF = 2816
EPS = 1e-6
NEG_INF = -1e30
SB_W = 512
SWA_W = 512
KV_W = 128
IN_W = 2304
SCALE = HEAD_DIM ** -0.5
N_CHIPS = 4
FS = 2 * D_FF // N_CHIPS
LANES = 128
V7X_VMEM_LIMIT = 56 * 2 ** 20
TM = 512
SB_KT = 512
SWA_G = 4

ADAM_LR = 0.001
ADAM_B1 = 0.9
ADAM_B2 = 0.999
ADAM_EPS = 1e-08
ADAM_WD = 0.01
ADAM_STEP = 10

MESH = pl.DeviceIdType.MESH
ANY = pl.BlockSpec(memory_space=pl.ANY)
HBM = pl.BlockSpec(memory_space=pltpu.HBM)
SEM = pl.BlockSpec(memory_space=pltpu.SEMAPHORE)
EFFECT = pltpu.SideEffectType.DATAFLOW_SIDE_EFFECTING


def _params(n_grid):
    return pltpu.CompilerParams(dimension_semantics=("arbitrary",) * n_grid, vmem_limit_bytes=V7X_VMEM_LIMIT)


_PREVIOUS = [None]


def _call(body, *, name, in_specs, out_specs, out_shape, grid=(), num_scalar_prefetch=0, scratch_shapes=(),
          input_output_aliases=None, compiler_params=None):
    n_in = len(in_specs)

    def run(*args):
        dep = _PREVIOUS[0]
        specs = list(in_specs) + ([ANY] if dep is not None else [])
        k = num_scalar_prefetch + n_in
        fn = body if dep is None else (lambda *refs: body(*refs[:k], *refs[k + 1:]))
        if num_scalar_prefetch:
            shape = dict(grid_spec=pltpu.PrefetchScalarGridSpec(
                num_scalar_prefetch=num_scalar_prefetch, grid=grid, in_specs=specs, out_specs=out_specs,
                scratch_shapes=scratch_shapes))
        else:
            shape = dict(grid=grid, in_specs=specs, out_specs=out_specs, scratch_shapes=scratch_shapes)
        out = pl.pallas_call(fn, name=name, out_shape=out_shape, input_output_aliases=input_output_aliases or {},
                             compiler_params=compiler_params, **shape)(*args, *([] if dep is None else [dep]))
        _PREVIOUS[0] = jax.tree.leaves(out)[-1]
        return out

    return run


def _dot(a, b):
    return jnp.dot(a, b, preferred_element_type=F32)


def _dot_nt(a, b):
    return lax.dot_general(a, b, (((1,), (1,)), ((), ())), preferred_element_type=F32)


def _dot_tn(a, b):
    return lax.dot_general(a, b, (((0,), (0,)), ((), ())), preferred_element_type=F32)


def _rms_fwd(x, g):
    r = lax.rsqrt(jnp.mean(x * x, axis=-1, keepdims=True) + EPS)
    xh = x * r
    return xh * g, xh, r


def _rms_bwd(dy, xh, r, g):
    u = dy * g
    dx = r * (u - xh * jnp.mean(u * xh, axis=-1, keepdims=True))
    dg = jnp.sum(dy * xh, axis=0, keepdims=True)
    return dx, dg


def _softplus(z):
    return jnp.maximum(z, 0.0) + jnp.log(1.0 + jnp.exp(-jnp.abs(z)))


def _norm_cast(h, g):
    t, w = h.shape

    def body(h_ref, g_ref, n_ref):
        y, _, _ = _rms_fwd(h_ref[...], g_ref[...])
        n_ref[...] = y.astype(BF16)

    return _call(
        body, name="norm_cast", grid=(t // TM,),
        in_specs=[pl.BlockSpec((TM, w), lambda i: (i, 0)), pl.BlockSpec((1, w), lambda i: (0, 0))],
        out_specs=pl.BlockSpec((TM, w), lambda i: (i, 0)),
        out_shape=jax.ShapeDtypeStruct((t, w), BF16), compiler_params=_params(1))(h, g)


def _ffn_gu(n, wgu):
    t, d = n.shape

    def body(n_ref, wg_ref, wu_ref, gu_ref, act_ref):
        x = n_ref[...]
        g = _dot(x, wg_ref[...])
        u = _dot(x, wu_ref[...])
        gu_ref[0] = g.astype(BF16)
        gu_ref[1] = u.astype(BF16)
        act_ref[...] = (g * jax.nn.sigmoid(g) * u).astype(BF16)

    return _call(
        body, name="ffn_gu", grid=(2, t // TM),
        in_specs=[pl.BlockSpec((TM, d), lambda j, i: (i, 0)),
                  pl.BlockSpec((None, d, FS), lambda j, i: (j, 0, 0)),
                  pl.BlockSpec((None, d, FS), lambda j, i: (j + 2, 0, 0))],
        out_specs=[pl.BlockSpec((2, TM, FS), lambda j, i: (0, i, j)), pl.BlockSpec((TM, FS), lambda j, i: (i, j))],
        out_shape=[jax.ShapeDtypeStruct((2, t, D_FF), BF16), jax.ShapeDtypeStruct((t, D_FF), BF16)],
        compiler_params=_params(2))(n, wgu, wgu)


def _down_res(act, wdn, h):
    t, f = act.shape
    d = h.shape[1]

    def body(a_ref, w_ref, h_ref, o_ref):
        o_ref[...] = h_ref[...] + 0.5 * _dot(a_ref[...], w_ref[...])

    return _call(
        body, name="down_res", grid=(t // TM,),
        in_specs=[pl.BlockSpec((TM, f), lambda i: (i, 0)), pl.BlockSpec((f, d), lambda i: (0, 0)),
                  pl.BlockSpec((TM, d), lambda i: (i, 0))],
        out_specs=pl.BlockSpec((TM, d), lambda i: (i, 0)),
        out_shape=jax.ShapeDtypeStruct((t, d), F32), compiler_params=_params(1))(act, wdn, h)


def _proj(n, w_in):
    t, d = n.shape
    w = w_in.shape[1]

    def body(n_ref, w_ref, o_ref):
        o_ref[...] = _dot(n_ref[...], w_ref[...]).astype(BF16)

    return _call(
        body, name="proj", grid=(t // TM,),
        in_specs=[pl.BlockSpec((TM, d), lambda i: (i, 0)), pl.BlockSpec((d, w), lambda i: (0, 0))],
        out_specs=pl.BlockSpec((TM, w), lambda i: (i, 0)),
        out_shape=jax.ShapeDtypeStruct((t, w), BF16), compiler_params=_params(1))(n, w_in)


def _out_res(o_sb, o_sw, g_sb, g_sw, w_out, h):
    t, d = h.shape

    def body(a_ref, b_ref, ga_ref, gb_ref, w_ref, h_ref, o_ref, mix_ref):
        ya, _, _ = _rms_fwd(a_ref[...], ga_ref[...])
        yb, _, _ = _rms_fwd(b_ref[...], gb_ref[...])
        mixed = jnp.concatenate([ya.astype(BF16), yb.astype(BF16)], axis=1)
        mix_ref[...] = mixed
        o_ref[...] = h_ref[...] + _dot(mixed, w_ref[...])

    return _call(
        body, name="out_res", grid=(t // TM,),
        in_specs=[pl.BlockSpec((TM, SB_W), lambda i: (i, 0)), pl.BlockSpec((TM, SWA_W), lambda i: (i, 0)),
                  pl.BlockSpec((1, SB_W), lambda i: (0, 0)), pl.BlockSpec((1, SWA_W), lambda i: (0, 0)),
                  pl.BlockSpec((d, d), lambda i: (0, 0)), pl.BlockSpec((TM, d), lambda i: (i, 0))],
        out_specs=[pl.BlockSpec((TM, d), lambda i: (i, 0)), pl.BlockSpec((TM, d), lambda i: (i, 0))],
        out_shape=[jax.ShapeDtypeStruct((t, d), F32), jax.ShapeDtypeStruct((t, d), BF16)],
        compiler_params=_params(1))(o_sb, o_sw, g_sb, g_sw, w_out, h)


def _loss_head(h, g, tgt):
    t, d = h.shape

    def body(h_ref, g_ref, t_ref, dh_ref, dg_ref, loss_ref):
        @pl.when(pl.program_id(0) == 0)
        def _():
            dg_ref[...] = jnp.zeros_like(dg_ref)
            loss_ref[...] = jnp.zeros_like(loss_ref)

        gg = g_ref[...]
        y, xh, r = _rms_fwd(h_ref[...], gg)
        err = y - t_ref[...]
        part = 0.5 * jnp.sum(jnp.sum(err * err, axis=1, keepdims=True) / d, axis=0, keepdims=True)
        loss_ref[...] += jnp.broadcast_to(part, loss_ref.shape)
        dx, dg = _rms_bwd(err / d, xh, r, gg)
        dh_ref[...] = dx
        dg_ref[...] += dg

    return _call(
        body, name="loss_head", grid=(t // TM,),
        in_specs=[pl.BlockSpec((TM, d), lambda i: (i, 0)), pl.BlockSpec((1, d), lambda i: (0, 0)),
                  pl.BlockSpec((TM, d), lambda i: (i, 0))],
        out_specs=[pl.BlockSpec((TM, d), lambda i: (i, 0)), pl.BlockSpec((1, d), lambda i: (0, 0)),
                   pl.BlockSpec((1, LANES), lambda i: (0, 0))],
        out_shape=[jax.ShapeDtypeStruct((t, d), F32), jax.ShapeDtypeStruct((1, d), F32),
                   jax.ShapeDtypeStruct((1, LANES), F32)],
        compiler_params=_params(1))(h, g, tgt)


def _ffn_dact(dh, wdn, gu):
    t, d = dh.shape

    def body(dh_ref, w_ref, gu_ref, o_ref):
        da = 0.5 * _dot_nt(dh_ref[...].astype(BF16), w_ref[...])
        g = gu_ref[0].astype(F32)
        u = gu_ref[1].astype(F32)
        sig = jax.nn.sigmoid(g)
        silu = g * sig
        o_ref[0] = (da * u * (sig * (1.0 + g * (1.0 - sig)))).astype(BF16)
        o_ref[1] = (da * silu).astype(BF16)

    return _call(
        body, name="ffn_dact", grid=(2, t // TM),
        in_specs=[pl.BlockSpec((TM, d), lambda j, i: (i, 0)), pl.BlockSpec((FS, d), lambda j, i: (j, 0)),
                  pl.BlockSpec((2, TM, FS), lambda j, i: (0, i, j))],
        out_specs=pl.BlockSpec((2, TM, FS), lambda j, i: (0, i, j)),
        out_shape=jax.ShapeDtypeStruct((2, t, D_FF), BF16), compiler_params=_params(2))(dh, wdn, gu)


def _dn_norm_bwd(a, a_spec, w, w_spec, nk, dh, h_in, g):
    t, d = dh.shape

    def body(a_ref, w_ref, dh_ref, h_ref, g_ref, o_ref, dg_ref, acc_ref):
        i, k = pl.program_id(0), pl.program_id(1)

        @pl.when(k == 0)
        def _():
            acc_ref[...] = jnp.zeros_like(acc_ref)

        acc_ref[...] += _dot_nt(a_ref[...], w_ref[...])

        @pl.when(k == nk - 1)
        def _():
            gg = g_ref[...]
            _, xh, r = _rms_fwd(h_ref[...], gg)
            dx, dg = _rms_bwd(acc_ref[...], xh, r, gg)
            o_ref[...] = dh_ref[...] + dx

            @pl.when(i == 0)
            def _():
                dg_ref[...] = dg

            @pl.when(i > 0)
            def _():
                dg_ref[...] += dg

    row = pl.BlockSpec((TM, d), lambda i, k: (i, 0))
    return _call(
        body, name="dn_norm_bwd", grid=(t // TM, nk),
        in_specs=[a_spec, w_spec, row, row, pl.BlockSpec((1, d), lambda i, k: (0, 0))],
        out_specs=[row, pl.BlockSpec((1, d), lambda i, k: (0, 0))],
        out_shape=[jax.ShapeDtypeStruct((t, d), F32), jax.ShapeDtypeStruct((1, d), F32)],
        scratch_shapes=[pltpu.VMEM((TM, d), F32)], compiler_params=_params(2))(a, w, dh, h_in, g)


def _ffn_dn(dgu, wgu, dh, h_in, g):
    d = dh.shape[1]
    return _dn_norm_bwd(
        dgu, pl.BlockSpec((None, TM, FS), lambda i, k: (k // 2, i, k % 2)),
        wgu, pl.BlockSpec((None, d, FS), lambda i, k: (k, 0, 0)), N_CHIPS, dh, h_in, g)


def _mix_dn(dproj, w_in, dh, h_in, g):
    d = dh.shape[1]
    w = dproj.shape[1]
    return _dn_norm_bwd(
        dproj, pl.BlockSpec((TM, w), lambda i, k: (i, 0)),
        w_in, pl.BlockSpec((d, w), lambda i, k: (0, 0)), 1, dh, h_in, g)


def _dmixed(dh, w_out, o_sb, o_sw, g_sb, g_sw):
    t, d = dh.shape

    def body(dh_ref, w_ref, a_ref, b_ref, ga_ref, gb_ref, o_ref, dga_ref, dgb_ref):
        i = pl.program_id(0)
        dm = _dot_nt(dh_ref[...].astype(BF16), w_ref[...])
        _, xa, ra = _rms_fwd(a_ref[...], ga_ref[...])
        _, xb, rb = _rms_fwd(b_ref[...], gb_ref[...])
        da, dga = _rms_bwd(dm[:, :SB_W], xa, ra, ga_ref[...])
        db, dgb = _rms_bwd(dm[:, SB_W:], xb, rb, gb_ref[...])
        o_ref[...] = jnp.concatenate([da.astype(BF16), db.astype(BF16)], axis=1)

        @pl.when(i == 0)
        def _():
            dga_ref[...] = dga
            dgb_ref[...] = dgb

        @pl.when(i > 0)
        def _():
            dga_ref[...] += dga
            dgb_ref[...] += dgb

    return _call(
        body, name="dmixed", grid=(t // TM,),
        in_specs=[pl.BlockSpec((TM, d), lambda i: (i, 0)), pl.BlockSpec((d, d), lambda i: (0, 0)),
                  pl.BlockSpec((TM, SB_W), lambda i: (i, 0)), pl.BlockSpec((TM, SWA_W), lambda i: (i, 0)),
                  pl.BlockSpec((1, SB_W), lambda i: (0, 0)), pl.BlockSpec((1, SWA_W), lambda i: (0, 0))],
        out_specs=[pl.BlockSpec((TM, d), lambda i: (i, 0)), pl.BlockSpec((1, SB_W), lambda i: (0, 0)),
                   pl.BlockSpec((1, SWA_W), lambda i: (0, 0))],
        out_shape=[jax.ShapeDtypeStruct((t, d), BF16), jax.ShapeDtypeStruct((1, SB_W), F32),
                   jax.ShapeDtypeStruct((1, SWA_W), F32)],
        compiler_params=_params(1))(dh, w_out, o_sb, o_sw, g_sb, g_sw)


def _wgrad(name, a, a_spec, b, b_spec, grid, out_shape, out_spec, scale):
    def body(a_ref, b_ref, o_ref):
        r = _dot_tn(a_ref[...], b_ref[...].astype(BF16))
        o_ref[...] = r if scale == 1.0 else scale * r

    return _call(
        body, name=name, grid=grid, in_specs=[a_spec, b_spec], out_specs=out_spec,
        out_shape=jax.ShapeDtypeStruct(out_shape, F32), compiler_params=_params(len(grid)))(a, b)


def _wgrad_gu(n, dgu):
    t, d = n.shape
    return _wgrad(
        "wgrad_gu", n, pl.BlockSpec((t, TM), lambda s, r: (0, r)),
        dgu, pl.BlockSpec((None, t, FS), lambda s, r: (s // 2, 0, s % 2)), (N_CHIPS, d // TM),
        (N_CHIPS, d, FS), pl.BlockSpec((None, TM, FS), lambda s, r: (s, r, 0)), 1.0)


def _wgrad_down(act, dh):
    t, d = dh.shape
    return _wgrad(
        "wgrad_down", act, pl.BlockSpec((t, FS), lambda s, r: (0, s)), dh, pl.BlockSpec((t, TM), lambda s, r: (0, r)),
        (2, d // TM), (D_FF, d), pl.BlockSpec((FS, TM), lambda s, r: (s, r)), 0.5)


def _wgrad_out(mixed, dh):
    t, d = dh.shape
    return _wgrad(
        "wgrad_out", mixed, pl.BlockSpec((t, TM), lambda s: (0, s)), dh, pl.BlockSpec((t, d), lambda s: (0, 0)),
        (d // TM,), (d, d), pl.BlockSpec((TM, d), lambda s: (s, 0)), 1.0)


def _wgrad_in(n, dproj):
    t, d = n.shape
    w = dproj.shape[1]
    tw = w // 3
    return _wgrad(
        "wgrad_in", n, pl.BlockSpec((t, d), lambda s: (0, 0)), dproj, pl.BlockSpec((t, tw), lambda s: (0, s)),
        (3,), (d, w), pl.BlockSpec((d, tw), lambda s: (0, s)), 1.0)


def _tri(rel):
    row = lax.broadcasted_iota(jnp.int32, (BLK, BLK), 0)
    col = lax.broadcasted_iota(jnp.int32, (BLK, BLK), 1)
    m = rel(row, col).astype(BF16)
    return jnp.concatenate([m, m], axis=0)


def _scan_dot(x, tri2):
    hi = x.astype(BF16)
    lo = (x - hi.astype(F32)).astype(BF16)
    return _dot(jnp.concatenate([hi, lo], axis=1), tri2)


def _head_masks():
    lane = lax.broadcasted_iota(jnp.int32, (1, LANES), 1)
    return [lane < HEAD_DIM, lane >= HEAD_DIM]


def _sb_dcol():
    dcol = lax.broadcasted_iota(jnp.int32, (BLK, SB_KT), 1) - lax.broadcasted_iota(jnp.int32, (BLK, SB_KT), 0)
    return jnp.concatenate([dcol, dcol], axis=0)


def _sb_fwd(proj):
    t = proj.shape[0]
    nq = t // BLK
    nb = SB_KT // BLK

    def body(q_ref, k_ref, v_ref, o_ref, tot_ref):
        hm = _head_masks()
        dcol = _sb_dcol()
        after = _tri(lambda r, c: r > c)

        def tile(qh, kt, carry, acc, limit):
            ks = pl.ds(pl.multiple_of(kt * SB_KT, SB_KT), SB_KT)
            z = _dot_nt(qh, k_ref[ks, :])
            sp = _softplus(z)
            valid = None if limit is None else dcol < limit
            spm = sp if valid is None else jnp.where(valid, sp, 0.0)
            sufs = [None] * nb
            for b in reversed(range(nb)):
                blk = spm[:, b * BLK:(b + 1) * BLK]
                sufs[b] = carry + _scan_dot(blk, after)
                carry = carry + jnp.sum(blk, axis=1, keepdims=True)
            w = jnp.exp(z - sp - jnp.concatenate(sufs, axis=1))
            if valid is not None:
                w = jnp.where(valid, w, 0.0)
            return carry, acc + _dot(w.astype(BF16), v_ref[ks, :])

        def qblock(qi, _):
            qs = pl.ds(pl.multiple_of(qi * BLK, BLK), BLK)
            q = q_ref[qs, :] * SCALE
            kd = qi // nb
            limit = (qi - kd * nb) * BLK
            qh = jnp.concatenate([jnp.where(m, q, jnp.zeros_like(q)) for m in hm], axis=0)
            c0 = tile(qh, kd, jnp.zeros((2 * BLK, 1), F32), jnp.zeros((2 * BLK, LANES), F32), limit)
            carry, acc = lax.fori_loop(0, kd, lambda n, c: tile(qh, kd - 1 - n, c[0], c[1], None), c0)
            o_ref[qs, :] = jnp.where(hm[0], acc[:BLK], acc[BLK:])
            for h in range(2):
                tot_ref[h, qs, :] = jnp.broadcast_to(carry[h * BLK:(h + 1) * BLK], (BLK, LANES))
            return 0

        lax.fori_loop(0, nq, qblock, 0)

    col_blk = lambda off: pl.BlockSpec((t, LANES), lambda p: (0, off + p))
    return _call(
        body, name="sb_fwd", grid=(4,), in_specs=[col_blk(0), col_blk(4), col_blk(8)],
        out_specs=[pl.BlockSpec((t, LANES), lambda p: (0, p)), pl.BlockSpec((2, t, LANES), lambda p: (p, 0, 0))],
        out_shape=[jax.ShapeDtypeStruct((t, SB_W), F32), jax.ShapeDtypeStruct((8, t, LANES), F32)],
        compiler_params=_params(1))(proj, proj, proj)


def _sb_bwd(proj, d_o, tot):
    t = proj.shape[0]
    nq = t // BLK
    nb = SB_KT // BLK

    def body(q_ref, k_ref, v_ref, do_ref, tot_ref, dq_ref, dk_ref, dv_ref, dk_acc, dv_acc):
        hm = _head_masks()
        dcol = _sb_dcol()
        before = _tri(lambda r, c: r < c)
        upto = _tri(lambda r, c: r <= c)
        dk_acc[...] = jnp.zeros_like(dk_acc)
        dv_acc[...] = jnp.zeros_like(dv_acc)

        def tile(qh, doh, tt, kt, pre, ecum, dq, limit):
            ks = pl.ds(pl.multiple_of(kt * SB_KT, SB_KT), SB_KT)
            k = k_ref[ks, :]
            v = v_ref[ks, :]
            z = _dot_nt(qh, k)
            sp = _softplus(z)
            valid = None if limit is None else dcol < limit
            spm = sp if valid is None else jnp.where(valid, sp, 0.0)
            pres = []
            for b in range(nb):
                blk = spm[:, b * BLK:(b + 1) * BLK]
                pres.append(pre + _scan_dot(blk, before))
                pre = pre + jnp.sum(blk, axis=1, keepdims=True)
            logw = z - (tt - jnp.concatenate(pres, axis=1))
            if valid is not None:
                logw = jnp.minimum(logw, 0.0)
            w = jnp.exp(logw)
            if valid is not None:
                w = jnp.where(valid, w, 0.0)
            e = w * _dot_nt(doh, v)
            incs = []
            for b in range(nb):
                blk = e[:, b * BLK:(b + 1) * BLK]
                incs.append(ecum + _scan_dot(blk, upto))
                ecum = ecum + jnp.sum(blk, axis=1, keepdims=True)
            dz = e - jnp.exp(z - sp) * jnp.concatenate(incs, axis=1)
            if valid is not None:
                dz = jnp.where(valid, dz, 0.0)
            dzb = dz.astype(BF16)
            dk_acc[ks, :] += _dot_tn(dzb, qh)
            dv_acc[ks, :] += _dot_tn(w.astype(BF16), doh)
            return pre, ecum, dq + _dot(dzb, k)

        def qblock(qi, _):
            qs = pl.ds(pl.multiple_of(qi * BLK, BLK), BLK)
            q = q_ref[qs, :] * SCALE
            do = do_ref[qs, :]
            kd = qi // nb
            limit = (qi - kd * nb) * BLK
            qh = jnp.concatenate([jnp.where(m, q, jnp.zeros_like(q)) for m in hm], axis=0)
            doh = jnp.concatenate([jnp.where(m, do, jnp.zeros_like(do)) for m in hm], axis=0)
            tt = jnp.concatenate([tot_ref[h, qs, 0:1] for h in range(2)], axis=0)
            c0 = (jnp.zeros((2 * BLK, 1), F32), jnp.zeros((2 * BLK, 1), F32), jnp.zeros((2 * BLK, LANES), F32))
            c = lax.fori_loop(0, kd, lambda kt, c: tile(qh, doh, tt, kt, c[0], c[1], c[2], None), c0)
            dq = tile(qh, doh, tt, kd, c[0], c[1], c[2], limit)[2]
            dq_ref[qs, :] = (jnp.where(hm[0], dq[:BLK], dq[BLK:]) * SCALE).astype(BF16)
            return 0

        lax.fori_loop(0, nq, qblock, 0)
        dk_ref[...] = dk_acc[...].astype(BF16)
        dv_ref[...] = dv_acc[...].astype(BF16)

    col_blk = lambda off: pl.BlockSpec((t, LANES), lambda p: (0, off + p))
    out = jax.ShapeDtypeStruct((t, SB_W), BF16)
    return _call(
        body, name="sb_bwd", grid=(4,),
        in_specs=[col_blk(0), col_blk(4), col_blk(8), col_blk(0), pl.BlockSpec((2, t, LANES), lambda p: (p, 0, 0))],
        out_specs=[col_blk(0), col_blk(0), col_blk(0)], out_shape=[out, out, out],
        scratch_shapes=[pltpu.VMEM((t, LANES), F32), pltpu.VMEM((t, LANES), F32)],
        compiler_params=_params(1))(proj, proj, proj, d_o, tot)


def _bucket_table():
    a = np.arange(BLK)[:, None]
    c = np.arange(2 * BLK)[None, :]
    dist = np.maximum(BLK + a - c, 0)
    max_exact = N_BUCKETS // 2
    dd = np.maximum(dist, 1).astype(np.float32)
    large = max_exact + (np.log(dd / max_exact) / math.log(MAX_DISTANCE / max_exact)
                         * (N_BUCKETS - max_exact)).astype(np.int32)
    large = np.minimum(large, N_BUCKETS - 1)
    return np.where(dist < max_exact, dist, large).astype(np.int32)


def _swa_band_masks():
    row = lax.broadcasted_iota(jnp.int32, (SWA_G * BLK, 2 * BLK), 0) & (BLK - 1)
    col = lax.broadcasted_iota(jnp.int32, (SWA_G * BLK, 2 * BLK), 1)
    own = lax.broadcasted_iota(jnp.int32, (SWA_G * BLK, BLK), 1) <= (
        lax.broadcasted_iota(jnp.int32, (SWA_G * BLK, BLK), 0) & (BLK - 1))
    return (col > row) & ((col < BLK) | (col - BLK <= row)), own


def _swa_stack(ref, qs, kvh, kvmask, scale):
    parts = []
    for g in range(SWA_G):
        hq = SWA_G * kvh + g
        x = ref[qs, (hq // 2) * LANES:(hq // 2 + 1) * LANES].astype(F32)
        if hq % 2 != kvh:
            x = pltpu.roll(x, HEAD_DIM, 1)
        parts.append(jnp.where(kvmask, x * scale, 0.0).astype(BF16))
    return jnp.concatenate(parts, axis=0)


def _swa_unstack(x4, kvh, hm):
    heads = []
    for g in range(SWA_G):
        x = x4[g * BLK:(g + 1) * BLK]
        heads.append(pltpu.roll(x, HEAD_DIM, 1) if g % 2 != kvh else x)
    return [jnp.where(hm[0], heads[0], heads[1]), jnp.where(hm[0], heads[2], heads[3])]


def _swa_scores(q4, kb, bias_ref, kvh, mask, cols):
    bias4 = jnp.concatenate([bias_ref[SWA_G * kvh + g, :, cols] for g in range(SWA_G)], axis=0)
    return jnp.where(mask, _dot_nt(q4, kb) + bias4, NEG_INF)


def _swa_sinks(sink_ref, kvh):
    return jnp.concatenate([jnp.broadcast_to(sink_ref[SWA_G * kvh + g:SWA_G * kvh + g + 1, 0:1], (BLK, 1))
                            for g in range(SWA_G)], axis=0)


def _swa_fwd(proj, bias, sinks_b):
    t = proj.shape[0]
    nq = t // BLK

    def body(q_ref, k_ref, v_ref, bias_ref, sink_ref, o_ref, lse_ref):
        hm = _head_masks()
        band, own = _swa_band_masks()

        def qblock(i, kvh, prev):
            qs = pl.ds(pl.multiple_of(i * BLK, BLK), BLK)
            if prev:
                ks, mask, cols = pl.ds(pl.multiple_of((i - 1) * BLK, BLK), 2 * BLK), band, slice(None)
            else:
                ks, mask, cols = qs, own, slice(BLK, None)
            q4 = _swa_stack(q_ref, qs, kvh, hm[kvh], SCALE)
            sink4 = _swa_sinks(sink_ref, kvh)
            s = _swa_scores(q4, k_ref[ks, :], bias_ref, kvh, mask, cols)
            m = jnp.maximum(jnp.max(s, axis=1, keepdims=True), sink4)
            p = jnp.exp(s - m)
            den = jnp.sum(p, axis=1, keepdims=True) + jnp.exp(sink4 - m)
            o4 = _dot((p * (1.0 / den)).astype(BF16), v_ref[ks, :])
            lse4 = m + jnp.log(den)
            for g in range(SWA_G):
                lse_ref[SWA_G * kvh + g, qs, :] = jnp.broadcast_to(lse4[g * BLK:(g + 1) * BLK], (BLK, LANES))
            for pp, o in enumerate(_swa_unstack(o4, kvh, hm)):
                o_ref[qs, (2 * kvh + pp) * LANES:(2 * kvh + pp + 1) * LANES] = o

        for kvh in range(2):
            qblock(0, kvh, False)

            def step(i, _):
                qblock(i, kvh, True)
                return 0

            lax.fori_loop(1, nq, step, 0)

    return _call(
        body, name="swa_fwd", grid=(1,),
        in_specs=[pl.BlockSpec((t, SWA_W), lambda i: (0, 3)), pl.BlockSpec((t, KV_W), lambda i: (0, 16)),
                  pl.BlockSpec((t, KV_W), lambda i: (0, 17)), pl.BlockSpec((8, BLK, 2 * BLK), lambda i: (0, 0, 0)),
                  pl.BlockSpec((8, LANES), lambda i: (0, 0))],
        out_specs=[pl.BlockSpec((t, SWA_W), lambda i: (0, 0)), pl.BlockSpec((8, t, LANES), lambda i: (0, 0, 0))],
        out_shape=[jax.ShapeDtypeStruct((t, SWA_W), F32), jax.ShapeDtypeStruct((8, t, LANES), F32)],
        compiler_params=_params(1))(proj, proj, proj, bias, sinks_b)


def _swa_bwd(proj, d_o, lse, bias, sinks_b, dbias_in):
    t = proj.shape[0]
    nq = t // BLK

    def body(q_ref, k_ref, v_ref, do_ref, lse_ref, bias_ref, sink_ref, dbi_ref,
             dq_ref, dk_ref, dv_ref, dsink_ref, dbias_ref, dk_acc, dv_acc):
        hm = _head_masks()
        band, own = _swa_band_masks()
        dk_acc[...] = jnp.zeros_like(dk_acc)
        dv_acc[...] = jnp.zeros_like(dv_acc)
        dbias_ref[...] = dbi_ref[...]

        def qblock(i, kvh, prev, dsink4):
            qs = pl.ds(pl.multiple_of(i * BLK, BLK), BLK)
            if prev:
                ks, mask, cols = pl.ds(pl.multiple_of((i - 1) * BLK, BLK), 2 * BLK), band, slice(None)
            else:
                ks, mask, cols = qs, own, slice(BLK, None)
            q4 = _swa_stack(q_ref, qs, kvh, hm[kvh], SCALE)
            do4 = _swa_stack(do_ref, qs, kvh, hm[kvh], 1.0)
            sink4 = _swa_sinks(sink_ref, kvh)
            lse4 = jnp.concatenate([lse_ref[SWA_G * kvh + g, qs, 0:1] for g in range(SWA_G)], axis=0)
            kb = k_ref[ks, :]
            p = jnp.exp(_swa_scores(q4, kb, bias_ref, kvh, mask, cols) - lse4)
            dp = _dot_nt(do4, v_ref[ks, :])
            delta = jnp.sum(p * dp, axis=1, keepdims=True)
            ds = p * (dp - delta)
            for g in range(SWA_G):
                dbias_ref[SWA_G * kvh + g, :, cols] += ds[g * BLK:(g + 1) * BLK]
            dsb = ds.astype(BF16)
            dk_acc[ks, :] += _dot_tn(dsb, q4)
            dv_acc[ks, :] += _dot_tn(p.astype(BF16), do4)
            for pp, dq in enumerate(_swa_unstack(_dot(dsb, kb) * SCALE, kvh, hm)):
                dq_ref[qs, (2 * kvh + pp) * LANES:(2 * kvh + pp + 1) * LANES] = dq.astype(BF16)
            return dsink4 - jnp.exp(sink4 - lse4) * delta

        for kvh in range(2):
            ds0 = qblock(0, kvh, False, jnp.zeros((SWA_G * BLK, 1), F32))
            ds4 = lax.fori_loop(1, nq, lambda i, c: qblock(i, kvh, True, c), ds0)
            for g in range(SWA_G):
                hq = SWA_G * kvh + g
                dsink_ref[hq:hq + 1, :] = jnp.broadcast_to(
                    jnp.sum(ds4[g * BLK:(g + 1) * BLK], axis=0, keepdims=True), (1, LANES))

        dk_ref[...] = dk_acc[...].astype(BF16)
        dv_ref[...] = dv_acc[...].astype(BF16)

    full3 = pl.BlockSpec((8, BLK, 2 * BLK), lambda i: (0, 0, 0))
    kv = jax.ShapeDtypeStruct((t, KV_W), BF16)
    return _call(
        body, name="swa_bwd", grid=(1,),
        in_specs=[pl.BlockSpec((t, SWA_W), lambda i: (0, 3)), pl.BlockSpec((t, KV_W), lambda i: (0, 16)),
                  pl.BlockSpec((t, KV_W), lambda i: (0, 17)), pl.BlockSpec((t, SWA_W), lambda i: (0, 1)),
                  pl.BlockSpec((8, t, LANES), lambda i: (0, 0, 0)), full3, pl.BlockSpec((8, LANES), lambda i: (0, 0)),
                  full3],
        out_specs=[pl.BlockSpec((t, SWA_W), lambda i: (0, 0)), pl.BlockSpec((t, KV_W), lambda i: (0, 0)),
                   pl.BlockSpec((t, KV_W), lambda i: (0, 0)), pl.BlockSpec((8, LANES), lambda i: (0, 0)), full3],
        out_shape=[jax.ShapeDtypeStruct((t, SWA_W), BF16), kv, kv, jax.ShapeDtypeStruct((8, LANES), F32),
                   jax.ShapeDtypeStruct((8, BLK, 2 * BLK), F32)],
        scratch_shapes=[pltpu.VMEM((t, KV_W), F32), pltpu.VMEM((t, KV_W), F32)],
        compiler_params=_params(1))(proj, proj, proj, d_o, lse, bias, sinks_b, dbias_in)


def _bias_table(rel_bias, buckets):
    def body(rb_ref, b_ref, o_ref):
        bk = b_ref[...]
        for h in range(8):
            acc = jnp.zeros((BLK, 2 * BLK), F32)
            for b in range(N_BUCKETS):
                acc = jnp.where(bk == b, rb_ref[b, h], acc)
            o_ref[h] = acc

    return _call(
        body, name="bias_table", grid=(1,),
        in_specs=[pl.BlockSpec(memory_space=pltpu.SMEM), pl.BlockSpec((BLK, 2 * BLK), lambda i: (0, 0))],
        out_specs=pl.BlockSpec((8, BLK, 2 * BLK), lambda i: (0, 0, 0)),
        out_shape=jax.ShapeDtypeStruct((8, BLK, 2 * BLK), F32), compiler_params=_params(1))(rel_bias, buckets)


def _bias_grad(dbias, buckets):
    def body(d_ref, b_ref, o_ref):
        lane = lax.broadcasted_iota(jnp.int32, (1, LANES), 1)
        bk = b_ref[...]
        for h in range(8):
            d = d_ref[h]
            acc = jnp.zeros((1, LANES), F32)
            for b in range(N_BUCKETS):
                s = jnp.sum(jnp.sum(jnp.where(bk == b, d, 0.0), axis=0, keepdims=True), axis=1, keepdims=True)
                acc = acc + jnp.where(lane == b, s, 0.0)
            o_ref[h:h + 1, :] = acc

    return _call(
        body, name="bias_grad", grid=(1,),
        in_specs=[pl.BlockSpec((8, BLK, 2 * BLK), lambda i: (0, 0, 0)), pl.BlockSpec((BLK, 2 * BLK), lambda i: (0, 0))],
        out_specs=pl.BlockSpec((8, LANES), lambda i: (0, 0)),
        out_shape=jax.ShapeDtypeStruct((8, LANES), F32), compiler_params=_params(1))(dbias, buckets)


def _row(a):
    return a.reshape(1, -1)


def _fwd_ffn1_sb(h, w, small, l):
    s = {"h0": h}
    s["n1"] = _norm_cast(h, _row(small["norm_ffn1"][l]))
    s["gu1"], s["act1"] = _ffn_gu(s["n1"], w["ffn1_gu"])
    s["h1"] = _down_res(s["act1"], w["ffn1_down"], h)
    s["nm"] = _norm_cast(s["h1"], _row(small["norm_mix"][l]))
    s["proj"] = _proj(s["nm"], w["w_in"])
    s["o_sb"], s["tot"] = _sb_fwd(s["proj"])
    return s


def _fwd_swa_ffn2(s, w, small, l, bias):
    s["sinks_b"] = jnp.broadcast_to(small["sinks"][l][:, None], (8, LANES))
    s["o_sw"], s["lse"] = _swa_fwd(s["proj"], bias, s["sinks_b"])
    s["h2"], s["mixed"] = _out_res(s["o_sb"], s["o_sw"], _row(small["norm_out_sb"][l]), _row(small["norm_out_swa"][l]),
                                  w["w_out"], s["h1"])
    s["n2"] = _norm_cast(s["h2"], _row(small["norm_ffn2"][l]))
    s["gu2"], s["act2"] = _ffn_gu(s["n2"], w["ffn2_gu"])
    return _down_res(s["act2"], w["ffn2_down"], s["h2"])


def _bwd_ffn(dh, s, w, small, l, which):
    h_in, norm = (s["h0"], "norm_ffn1") if which == 1 else (s["h2"], "norm_ffn2")
    dgu = _ffn_dact(dh, w[f"ffn{which}_down"], s[f"gu{which}"])
    g_down = _wgrad_down(s[f"act{which}"], dh)
    g_gu = _wgrad_gu(s[f"n{which}"], dgu)
    dh, dg = _ffn_dn(dgu, w[f"ffn{which}_gu"], dh, h_in, _row(small[norm][l]))
    return dh, {f"ffn{which}_down": g_down, f"ffn{which}_gu": g_gu}, {norm: dg}


def _bwd_mix(dh, s, w, small, l, bias, dbias):
    g_out = _wgrad_out(s["mixed"], dh)
    d_o, dg_sb, dg_sw = _dmixed(dh, w["w_out"], s["o_sb"], s["o_sw"], _row(small["norm_out_sb"][l]),
                                _row(small["norm_out_swa"][l]))
    dq_sb, dk_sb, dv_sb = _sb_bwd(s["proj"], d_o, s["tot"])
    dq_sw, dk_sw, dv_sw, dsink, dbias = _swa_bwd(s["proj"], d_o, s["lse"], bias, s["sinks_b"], dbias)
    dproj = jnp.concatenate([dq_sb, dk_sb, dv_sb, dq_sw, dk_sw, dv_sw], axis=1)
    g_in = _wgrad_in(s["nm"], dproj)
    dh, dg_mix = _mix_dn(dproj, w["w_in"], dh, s["h1"], _row(small["norm_mix"][l]))
    gs = {"norm_out_sb": dg_sb, "norm_out_swa": dg_sw, "sinks": dsink[:, 0], "norm_mix": dg_mix}
    return dh, {"w_out": g_out, "w_in": g_in}, gs, dbias


def _place():
    x, y, c = lax.axis_index("x"), lax.axis_index("y"), lax.axis_index("c")
    return x, y, c, 2 * x + y


def _chip_core(k, c):
    return (k // 2, k % 2, c)


def _rows_per_block(rows, cols, copies):
    best = 16
    for tr in range(16, rows + 1, 16):
        if rows % tr == 0 and copies * tr * cols * 4 <= 6 * 2 ** 20:
            best = tr
    assert rows % best == 0
    return best


def _place_own(w, l, me1):
    _, rows, cols = w.shape
    tr = _rows_per_block(rows, cols, 1)

    def body(me_ref, w_ref, o_ref):
        o_ref[...] = w_ref[...].astype(BF16)

    return _call(
        body, name="place_own",
        num_scalar_prefetch=1, grid=(rows // tr,),
        in_specs=[pl.BlockSpec((None, tr, cols), lambda r, me: (l, r, 0))],
        out_specs=pl.BlockSpec((None, tr, cols), lambda r, me: (me[0], r, 0)),
        out_shape=jax.ShapeDtypeStruct((N_CHIPS, rows, cols), BF16), compiler_params=_params(1))(me1, w)


def _plan_gather_ici(bufs):
    _, _, c, me = _place()
    return [(b.at[me, c], b.at[me, c], b.at[(me + 3 - j) % N_CHIPS, c], _chip_core((me + 1 + j) % N_CHIPS, c))
            for b in bufs for j in range(3)]


def _plan_gather_d2d(bufs):
    x, y, c, me = _place()
    return [(b.at[(me + 3 - j) % N_CHIPS, c], b.at[(me + 3 - j) % N_CHIPS, c], b.at[(me + 3 - j) % N_CHIPS, 1 - c],
             (x, y, 1 - c)) for b in bufs for j in range(3)]


def _plan_grad_sibling(bufs):
    x, y, c, _ = _place()
    n = len(bufs) // 2
    return [(g.at[:, 1 - c], z, z, (x, y, 1 - c)) for g, z in zip(bufs[:n], bufs[n:])]


def _plan_grad_chips(bufs):
    _, _, c, me = _place()
    n = len(bufs) // 2
    return [(p.at[(me + 1 + j) % N_CHIPS], z.at[j], z.at[j], _chip_core((me + 1 + j) % N_CHIPS, c))
            for p, z in zip(bufs[:n], bufs[n:]) for j in range(3)]


def _plan_grad_halves(bufs):
    x, y, c, _ = _place()
    return [(b.at[c], b.at[c], b.at[1 - c], (x, y, 1 - c)) for b in bufs]


def _remote(src, dst, send_sem, recv_sem, to):
    return pltpu.make_async_remote_copy(src_ref=src, dst_ref=dst, send_sem=send_sem, recv_sem=recv_sem,
                                        device_id=to, device_id_type=MESH)


def _exchange_now(name, plan, bufs, n_copies):
    n = len(bufs)

    def body(*refs):
        outs, (ssem, rsem) = refs[n:2 * n], refs[2 * n:]
        copies = plan(outs)
        for i, (src, dst, _, to) in enumerate(copies):
            _remote(src, dst, ssem.at[i], rsem.at[i], to).start()
        for i, (src, dst, land, to) in enumerate(copies):
            _remote(land, land, ssem.at[i], rsem.at[i], to).wait_recv()
        for i, (src, dst, _, to) in enumerate(copies):
            _remote(src, dst, ssem.at[i], rsem.at[i], to).wait_send()

    return _call(
        body, name=name, in_specs=[ANY] * n, out_specs=[ANY] * n,
        out_shape=[jax.ShapeDtypeStruct(a.shape, a.dtype) for a in bufs],
        input_output_aliases={t: t for t in range(n)},
        scratch_shapes=[pltpu.SemaphoreType.DMA((n_copies,))] * 2,
        compiler_params=pltpu.CompilerParams(vmem_limit_bytes=V7X_VMEM_LIMIT))(*bufs)


def _exchange_start(name, plan, bufs, n_copies):
    n = len(bufs)

    def body(*refs):
        ins = refs[:n]
        ssem, rsem = refs[n], refs[n + 1]
        token = refs[-1]
        for i, (src, dst, _, to) in enumerate(plan(ins)):
            _remote(src, dst, ssem.at[i], rsem.at[i], to).start()
        token[...] = jnp.zeros_like(token)

    out = _call(
        body, name=name,
        out_shape=(pltpu.SemaphoreType.DMA((n_copies,)), pltpu.SemaphoreType.DMA((n_copies,)),
                   *[pltpu.HBM(a.shape, a.dtype) for a in bufs], jax.ShapeDtypeStruct((8, LANES), F32)),
        in_specs=[HBM] * n, out_specs=(SEM, SEM, *[HBM] * n, pl.BlockSpec(memory_space=pltpu.VMEM)),
        input_output_aliases={t: 2 + t for t in range(n)},
        compiler_params=pltpu.CompilerParams(has_side_effects=EFFECT),
    )(*[pltpu.with_memory_space_constraint(a, pltpu.HBM) for a in bufs])
    return (out[0], out[1]), list(out[2:2 + n])


def _exchange_wait(name, plan, bufs, sems):
    n = len(bufs)

    def body(*refs):
        ins = refs[:n]
        ssem, rsem = refs[n], refs[n + 1]
        for i, (src, dst, land, to) in enumerate(plan(ins)):
            _remote(src, dst, ssem.at[i], rsem.at[i], to).wait_send()
            _remote(land, land, ssem.at[i], rsem.at[i], to).wait_recv()

    return list(_call(
        body, name=name, out_shape=[pltpu.HBM(a.shape, a.dtype) for a in bufs],
        in_specs=[HBM] * n + [SEM, SEM], out_specs=[HBM] * n,
        input_output_aliases={t: t for t in range(n)},
        compiler_params=pltpu.CompilerParams(has_side_effects=EFFECT),
    )(*bufs, sems[0], sems[1]))


def _gather_now(bufs):
    n = len(bufs)
    n_cp = 3 * n

    def body(*refs):
        outs = refs[n:2 * n]
        ici_s, ici_r, d2d_s, d2d_r = refs[2 * n:]
        first = _plan_gather_ici(outs)
        second = _plan_gather_d2d(outs)
        for i, (src, dst, _, to) in enumerate(first):
            _remote(src, dst, ici_s.at[i], ici_r.at[i], to).start()
        for i, (src, dst, _, to) in enumerate(second):
            land = first[i][2]
            _remote(land, land, ici_s.at[i], ici_r.at[i], to).wait_recv()
            _remote(src, dst, d2d_s.at[i], d2d_r.at[i], to).start()
        for i, (_, _, land, to) in enumerate(second):
            _remote(land, land, d2d_s.at[i], d2d_r.at[i], to).wait_recv()
        for i in range(n_cp):
            _remote(first[i][0], first[i][1], ici_s.at[i], ici_r.at[i], first[i][3]).wait_send()
            _remote(second[i][0], second[i][1], d2d_s.at[i], d2d_r.at[i], second[i][3]).wait_send()

    return _call(
        body, name="gather_layer0", in_specs=[ANY] * n, out_specs=[ANY] * n,
        out_shape=[jax.ShapeDtypeStruct(a.shape, a.dtype) for a in bufs],
        input_output_aliases={t: t for t in range(n)},
        scratch_shapes=[pltpu.SemaphoreType.DMA((n_cp,))] * 4,
        compiler_params=pltpu.CompilerParams(vmem_limit_bytes=V7X_VMEM_LIMIT))(*bufs)


def _chip_sum(g, xbuf, cm):
    _, _, r2, cols = g.shape
    tr = _rows_per_block(r2, cols, N_CHIPS)

    def body(cm_ref, g_ref, x_ref, pb_ref, po_ref):
        pb_ref[...] = (g_ref[...] + x_ref[...]).astype(BF16)
        me = cm_ref[1]
        po_ref[...] = g_ref[me] + x_ref[me]

    return _call(
        body, name="grad_chip_sum",
        num_scalar_prefetch=1, grid=(r2 // tr,),
        in_specs=[pl.BlockSpec((N_CHIPS, None, tr, cols), lambda r, cm: (0, cm[0], r, 0)),
                  pl.BlockSpec((N_CHIPS, tr, cols), lambda r, cm: (0, r, 0))],
        out_specs=[pl.BlockSpec((N_CHIPS, tr, cols), lambda r, cm: (0, r, 0)),
                   pl.BlockSpec((tr, cols), lambda r, cm: (r, 0))],
        out_shape=[jax.ShapeDtypeStruct((N_CHIPS, r2, cols), BF16), jax.ShapeDtypeStruct((r2, cols), F32)],
        compiler_params=_params(1))(cm, g, xbuf)


def _total_sum(pown, rbuf, cm):
    r2, cols = pown.shape
    tr = _rows_per_block(r2, cols, 3)

    def body(cm_ref, p_ref, r_ref, o_ref):
        acc = p_ref[...]
        for j in range(3):
            acc = acc + r_ref[j].astype(F32)
        o_ref[...] = acc

    return _call(
        body, name="grad_total_sum",
        num_scalar_prefetch=1, grid=(r2 // tr,),
        in_specs=[pl.BlockSpec((tr, cols), lambda r, cm: (r, 0)),
                  pl.BlockSpec((3, tr, cols), lambda r, cm: (0, r, 0))],
        out_specs=pl.BlockSpec((None, tr, cols), lambda r, cm: (cm[0], r, 0)),
        out_shape=jax.ShapeDtypeStruct((2, r2, cols), F32), compiler_params=_params(1))(cm, pown, rbuf)


def _small_allreduce(v):
    rows = v.shape[0]
    n_dev = 2 * N_CHIPS

    def body(v_ref, o_ref, buf, ssem, rsem):
        x, y, c, _ = _place()
        me = 4 * x + 2 * y + c
        buf[me] = v_ref[...]

        def copy(d, slot, to):
            return _remote(v_ref, buf.at[slot], ssem.at[d - 1], rsem.at[d - 1], (to // 4, (to // 2) % 2, to % 2))

        cps = [copy(d, me, (me + d) % n_dev) for d in range(1, n_dev)]
        for cp in cps:
            cp.start()
        for d in range(1, n_dev):
            copy(d, (me + n_dev - d) % n_dev, me).wait_recv()
        for cp in cps:
            cp.wait_send()
        acc = buf[0]
        for i in range(1, n_dev):
            acc = acc + buf[i]
        o_ref[...] = acc

    vm = pl.BlockSpec(memory_space=pltpu.VMEM)
    return _call(
        body, name="small_allreduce", in_specs=[vm], out_specs=vm,
        out_shape=jax.ShapeDtypeStruct(v.shape, F32),
        scratch_shapes=[pltpu.VMEM((n_dev, rows, LANES), F32), pltpu.SemaphoreType.DMA((n_dev - 1,)),
                        pltpu.SemaphoreType.DMA((n_dev - 1,))],
        compiler_params=pltpu.CompilerParams(vmem_limit_bytes=V7X_VMEM_LIMIT))(v)


def _adamw_math(w, g, m, v):
    m2 = ADAM_B1 * m + (1.0 - ADAM_B1) * g
    v2 = ADAM_B2 * v + (1.0 - ADAM_B2) * (g * g)
    m_hat = m2 / (1.0 - ADAM_B1 ** ADAM_STEP)
    v_hat = v2 / (1.0 - ADAM_B2 ** ADAM_STEP)
    return -ADAM_LR * (m_hat / (jnp.sqrt(v_hat) + ADAM_EPS) + ADAM_WD * w), m2, v2


def _adamw_layer(w, g, m, v, l, prev):
    _, rows, cols = w.shape
    tr = rows
    for cand in range(8, rows + 1, 8):
        if rows % cand == 0 and cand * cols * 4 <= 2 ** 21:
            tr = cand

    def body(w_ref, g_ref, m_ref, v_ref, *outs):
        go_ref, d_ref, m2_ref, v2_ref = outs[-4:]
        g = g_ref[...]
        go_ref[...] = g
        d_ref[...], m2_ref[...], v2_ref[...] = _adamw_math(w_ref[...], g, m_ref[...], v_ref[...])

    stack = pl.BlockSpec((None, tr, cols), lambda i: (l, i, 0))
    ins, specs, alias = [w, g, m, v], [stack, pl.BlockSpec((tr, cols), lambda i: (i, 0)), stack, stack], {}
    if prev is not None:
        ins += list(prev)
        specs += [ANY] * 4
        alias = {4 + i: i for i in range(4)}
    return _call(
        body, name="adamw", grid=(rows // tr,), in_specs=specs, out_specs=[stack] * 4,
        out_shape=[jax.ShapeDtypeStruct(w.shape, F32)] * 4, input_output_aliases=alias,
        compiler_params=_params(1))(*ins)


def _adamw_small(w, g, m, v):
    def body(w_ref, g_ref, m_ref, v_ref, d_ref, m2_ref, v2_ref):
        d_ref[...], m2_ref[...], v2_ref[...] = _adamw_math(w_ref[...], g_ref[...], m_ref[...], v_ref[...])

    spec = pl.BlockSpec(w.shape, lambda i: (0, 0))
    return _call(
        body, name="adamw_small", grid=(1,), in_specs=[spec] * 4, out_specs=[spec] * 3,
        out_shape=[jax.ShapeDtypeStruct(w.shape, F32)] * 3, compiler_params=_params(1))(w, g, m, v)


SMALL = ("norm_ffn1", "norm_mix", "sinks", "norm_out_sb", "norm_out_swa", "norm_ffn2", "rel_bias", "norm_final")
BIG = ("ffn1_gu", "ffn1_down", "w_in", "w_out", "ffn2_gu", "ffn2_down")


def _pack(parts):
    rows = []
    for a in parts:
        a = a.reshape(-1).astype(F32)
        rows.append(jnp.pad(a, (0, -a.shape[0] % LANES)).reshape(-1, LANES))
    out = jnp.concatenate(rows, axis=0)
    return jnp.pad(out, ((0, -out.shape[0] % 8), (0, 0)))


def _unpack(packed, like):
    out, r = [], 0
    for a in like:
        n = math.prod(a.shape)
        nr = -(-n // LANES)
        out.append(packed[r:r + nr].reshape(-1)[:n].reshape(a.shape))
        r += nr
    return out


def _halved(a):
    k, r, cols = a.shape
    return a.reshape(k, 2, r // 2, cols)


def _layer_weights(bufs):
    full = {k: a.reshape(N_CHIPS, a.shape[2] * 2, a.shape[3]) for k, a in zip(BIG, bufs)}
    d = D_MODEL
    return {
        "ffn1_gu": full["ffn1_gu"], "ffn2_gu": full["ffn2_gu"],
        "ffn1_down": full["ffn1_down"].reshape(D_FF, d), "ffn2_down": full["ffn2_down"].reshape(D_FF, d),
        "w_out": full["w_out"].reshape(d, d),
        "w_in": jnp.transpose(full["w_in"], (1, 0, 2)).reshape(d, IN_W),
    }


def _grad_stacks(gw):
    d = D_MODEL
    stacks = {
        "ffn1_gu": gw["ffn1_gu"], "ffn2_gu": gw["ffn2_gu"],
        "ffn1_down": gw["ffn1_down"].reshape(N_CHIPS, D_FF // N_CHIPS, d),
        "ffn2_down": gw["ffn2_down"].reshape(N_CHIPS, D_FF // N_CHIPS, d),
        "w_out": gw["w_out"].reshape(N_CHIPS, d // N_CHIPS, d),
        "w_in": jnp.transpose(gw["w_in"].reshape(d, N_CHIPS, IN_W // N_CHIPS), (1, 0, 2)),
    }
    return [_halved(stacks[k]) for k in BIG]


def _empty_like_hbm(shape, dtype):
    return pltpu.with_memory_space_constraint(lax.empty(shape, dtype), pltpu.HBM)


def kernel(x, norm_ffn1, w_ffn1_gu, w_ffn1_down, norm_mix, w_in, sinks, norm_out_sb, norm_out_swa, w_out, norm_ffn2, w_ffn2_gu, w_ffn2_down, rel_bias, norm_final, loss_target, m_norm_ffn1, m_w_ffn1_gu, m_w_ffn1_down, m_norm_mix, m_w_in, m_sinks, m_norm_out_sb, m_norm_out_swa, m_w_out, m_norm_ffn2, m_w_ffn2_gu, m_w_ffn2_down, m_rel_bias, m_norm_final, v_norm_ffn1, v_w_ffn1_gu, v_w_ffn1_down, v_norm_mix, v_w_in, v_sinks, v_norm_out_sb, v_norm_out_swa, v_w_out, v_norm_ffn2, v_w_ffn2_gu, v_w_ffn2_down, v_rel_bias, v_norm_final):
    big_w = dict(ffn1_gu=w_ffn1_gu, ffn1_down=w_ffn1_down, w_in=w_in, w_out=w_out, ffn2_gu=w_ffn2_gu, ffn2_down=w_ffn2_down)
    big_m = dict(ffn1_gu=m_w_ffn1_gu, ffn1_down=m_w_ffn1_down, w_in=m_w_in, w_out=m_w_out, ffn2_gu=m_w_ffn2_gu, ffn2_down=m_w_ffn2_down)
    big_v = dict(ffn1_gu=v_w_ffn1_gu, ffn1_down=v_w_ffn1_down, w_in=v_w_in, w_out=v_w_out, ffn2_gu=v_w_ffn2_gu, ffn2_down=v_w_ffn2_down)
    small = dict(norm_ffn1=norm_ffn1, norm_mix=norm_mix, sinks=sinks, norm_out_sb=norm_out_sb, norm_out_swa=norm_out_swa,
                 norm_ffn2=norm_ffn2, rel_bias=rel_bias, norm_final=norm_final)
    small_m = dict(norm_ffn1=m_norm_ffn1, norm_mix=m_norm_mix, sinks=m_sinks, norm_out_sb=m_norm_out_sb,
                   norm_out_swa=m_norm_out_swa, norm_ffn2=m_norm_ffn2, rel_bias=m_rel_bias, norm_final=m_norm_final)
    small_v = dict(norm_ffn1=v_norm_ffn1, norm_mix=v_norm_mix, sinks=v_sinks, norm_out_sb=v_norm_out_sb,
                   norm_out_swa=v_norm_out_swa, norm_ffn2=v_norm_ffn2, rel_bias=v_rel_bias, norm_final=v_norm_final)
    _PREVIOUS[0] = None
    _, _, c, me = _place()
    cm = jnp.stack([c, me]).astype(jnp.int32)
    n_big = len(BIG)
    buckets = jnp.asarray(_bucket_table())
    bias = _bias_table(rel_bias, buckets)

    placed = [[_halved(_place_own(big_w[k], l, cm[1:])) for k in BIG] for l in range(DEPTH)]
    w0 = _layer_weights(_gather_now(placed[0]))
    sems, bufs = _exchange_start("gather1_ici_start", _plan_gather_ici, placed[1], 3 * n_big)
    s0 = _fwd_ffn1_sb(x[0], w0, small, 0)
    bufs = _exchange_wait("gather1_ici_wait", _plan_gather_ici, bufs, sems)
    sems, bufs = _exchange_start("gather1_d2d_start", _plan_gather_d2d, bufs, 3 * n_big)
    h = _fwd_swa_ffn2(s0, w0, small, 0, bias)
    w1 = _layer_weights(_exchange_wait("gather1_d2d_wait", _plan_gather_d2d, bufs, sems))
    s1 = _fwd_ffn1_sb(h, w1, small, 1)
    h = _fwd_swa_ffn2(s1, w1, small, 1, bias)
    dh, dg_final, loss_row = _loss_head(h, _row(norm_final), loss_target[0])

    gsm = [dict() for _ in range(DEPTH)]
    dbias = jnp.zeros((8, BLK, 2 * BLK), F32)
    dh, gw1, gs = _bwd_ffn(dh, s1, w1, small, 1, 2)
    gsm[1].update(gs)
    dh, gw, gs, dbias = _bwd_mix(dh, s1, w1, small, 1, bias, dbias)
    gw1.update(gw)
    gsm[1].update(gs)
    dh, gw, gs = _bwd_ffn(dh, s1, w1, small, 1, 1)
    gw1.update(gw)
    gsm[1].update(gs)

    def landing(stacks, lead, dtype):
        return [_empty_like_hbm((lead,) + a.shape[2:], dtype) for a in stacks]

    g1 = _grad_stacks(gw1)
    sems, bufs = _exchange_start("grad1_sibling_start", _plan_grad_sibling, g1 + landing(g1, N_CHIPS, F32), n_big)
    dh, gw0, gs = _bwd_ffn(dh, s0, w0, small, 0, 2)
    gsm[0].update(gs)
    bufs = _exchange_wait("grad1_sibling_wait", _plan_grad_sibling, bufs, sems)
    sums1 = [_chip_sum(g, z, cm) for g, z in zip(bufs[:n_big], bufs[n_big:])]
    parts = [s[0] for s in sums1]
    sems, bufs = _exchange_start("grad1_chips_start", _plan_grad_chips, parts + landing(g1, 3, BF16), 3 * n_big)
    dh, gw, gs, dbias = _bwd_mix(dh, s0, w0, small, 0, bias, dbias)
    gw0.update(gw)
    gsm[0].update(gs)
    bufs = _exchange_wait("grad1_chips_wait", _plan_grad_chips, bufs, sems)
    halves1 = [_total_sum(s[1], z, cm) for s, z in zip(sums1, bufs[n_big:])]
    sems, bufs = _exchange_start("grad1_halves_start", _plan_grad_halves, halves1, n_big)
    dh, gw, gs = _bwd_ffn(dh, s0, w0, small, 0, 1)
    gw0.update(gw)
    gsm[0].update(gs)
    reduced1 = _exchange_wait("grad1_halves_wait", _plan_grad_halves, bufs, sems)

    def adamw_big(reduced, l, prev):
        return [_adamw_layer(big_w[k], reduced[i].reshape(big_w[k].shape[1:]), big_m[k], big_v[k], l,
                             None if prev is None else prev[i]) for i, k in enumerate(BIG)]

    g0 = _grad_stacks(gw0)
    sems, bufs = _exchange_start("grad0_sibling_start", _plan_grad_sibling, g0 + landing(g0, N_CHIPS, F32), n_big)
    stacks = adamw_big(reduced1, 1, None)
    bufs = _exchange_wait("grad0_sibling_wait", _plan_grad_sibling, bufs, sems)
    sums0 = [_chip_sum(g, z, cm) for g, z in zip(bufs[:n_big], bufs[n_big:])]
    sems, bufs = _exchange_start("grad0_chips_start", _plan_grad_chips, [s[0] for s in sums0] + landing(g0, 3, BF16),
                                 3 * n_big)
    gsmall = {k: jnp.stack([gsm[l][k].reshape(-1) for l in range(DEPTH)]) for k in gsm[0]}
    gsmall["rel_bias"] = jnp.transpose(_bias_grad(dbias, buckets)[:, :N_BUCKETS])
    gsmall["norm_final"] = dg_final.reshape(-1)
    small_like = [small[k] for k in SMALL]
    pk = lambda dct: _pack([dct[k] for k in SMALL])
    packed_w, packed_m, packed_v = pk(small), pk(small_m), pk(small_v)
    bufs = _exchange_wait("grad0_chips_wait", _plan_grad_chips, bufs, sems)
    halves0 = [_total_sum(s[1], z, cm) for s, z in zip(sums0, bufs[n_big:])]
    reduced0 = _exchange_now("grad0_halves", _plan_grad_halves, halves0, n_big)
    stacks = adamw_big(reduced0, 0, stacks)

    red = _small_allreduce(_pack([gsmall[k] for k in SMALL] + [loss_row[0, :1]]))
    gs = _unpack(red, small_like + [loss_row[0, :1]])
    loss = gs[-1][0]
    gs = dict(zip(SMALL, gs[:-1]))
    dlt, m2, v2 = _adamw_small(packed_w, pk(gs), packed_m, packed_v)

    out_g, out_d, out_m, out_v = {}, {}, {}, {}
    for k, st in zip(BIG, stacks):
        out_g[k], out_d[k], out_m[k], out_v[k] = st
    for dst, packed in ((out_d, dlt), (out_m, m2), (out_v, v2)):
        dst.update(zip(SMALL, _unpack(packed, small_like)))
    out_g.update(gs)

    order = ("norm_ffn1", "ffn1_gu", "ffn1_down", "norm_mix", "w_in", "sinks", "norm_out_sb", "norm_out_swa", "w_out",
             "norm_ffn2", "ffn2_gu", "ffn2_down", "rel_bias", "norm_final")
    return (loss, dh.reshape(x.shape), *[out_g[k] for k in order], *[out_d[k] for k in order],
            *[out_m[k] for k in order], *[out_v[k] for k in order])
```

```python
import math

import numpy as np
import jax
import jax.numpy as jnp
from jax import lax
from jax.experimental import pallas as pl
from jax.experimental.pallas import tpu as pltpu

F32 = jnp.float32
BF16 = jnp.bfloat16

D_MODEL = 1024
DEPTH = 2
HEAD_DIM = 64
BLK = 128
N_BUCKETS = 32
MAX_DISTANCE = 128
D_FF = 2816
EPS = 1e-6
NEG_INF = -1e30
SB_W = 512
SWA_W = 512
KV_W = 128
IN_W = 2304
SCALE = HEAD_DIM ** -0.5
N_CHIPS = 4
FS = 2 * D_FF // N_CHIPS
LANES = 128
V7X_VMEM_LIMIT = 56 * 2 ** 20
TM = 512
SB_KT = 512
SWA_G = 4

ADAM_LR = 0.001
ADAM_B1 = 0.9
ADAM_B2 = 0.999
ADAM_EPS = 1e-08
ADAM_WD = 0.01
ADAM_STEP = 10

MESH = pl.DeviceIdType.MESH
ANY = pl.BlockSpec(memory_space=pl.ANY)
HBM = pl.BlockSpec(memory_space=pltpu.HBM)
SEM = pl.BlockSpec(memory_space=pltpu.SEMAPHORE)
EFFECT = pltpu.SideEffectType.DATAFLOW_SIDE_EFFECTING


def _params(n_grid):
    return pltpu.CompilerParams(dimension_semantics=("arbitrary",) * n_grid, vmem_limit_bytes=V7X_VMEM_LIMIT)


_PREVIOUS = [None]


def _call(body, *, name, in_specs, out_specs, out_shape, grid=(), num_scalar_prefetch=0, scratch_shapes=(),
          input_output_aliases=None, compiler_params=None):
    n_in = len(in_specs)

    def run(*args):
        dep = _PREVIOUS[0]
        specs = list(in_specs) + ([ANY] if dep is not None else [])
        k = num_scalar_prefetch + n_in
        fn = body if dep is None else (lambda *refs: body(*refs[:k], *refs[k + 1:]))
        if num_scalar_prefetch:
            shape = dict(grid_spec=pltpu.PrefetchScalarGridSpec(
                num_scalar_prefetch=num_scalar_prefetch, grid=grid, in_specs=specs, out_specs=out_specs,
                scratch_shapes=scratch_shapes))
        else:
            shape = dict(grid=grid, in_specs=specs, out_specs=out_specs, scratch_shapes=scratch_shapes)
        out = pl.pallas_call(fn, name=name, out_shape=out_shape, input_output_aliases=input_output_aliases or {},
                             compiler_params=compiler_params, **shape)(*args, *([] if dep is None else [dep]))
        _PREVIOUS[0] = jax.tree.leaves(out)[-1]
        return out

    return run


def _dot(a, b):
    return jnp.dot(a, b, preferred_element_type=F32)


def _dot_nt(a, b):
    return lax.dot_general(a, b, (((1,), (1,)), ((), ())), preferred_element_type=F32)


def _dot_tn(a, b):
    return lax.dot_general(a, b, (((0,), (0,)), ((), ())), preferred_element_type=F32)


def _rms_fwd(x, g):
    r = lax.rsqrt(jnp.mean(x * x, axis=-1, keepdims=True) + EPS)
    xh = x * r
    return xh * g, xh, r


def _rms_bwd(dy, xh, r, g):
    u = dy * g
    dx = r * (u - xh * jnp.mean(u * xh, axis=-1, keepdims=True))
    dg = jnp.sum(dy * xh, axis=0, keepdims=True)
    return dx, dg


def _softplus(z):
    return jnp.maximum(z, 0.0) + jnp.log(1.0 + jnp.exp(-jnp.abs(z)))


def _norm_cast(h, g):
    t, w = h.shape

    def body(h_ref, g_ref, n_ref):
        y, _, _ = _rms_fwd(h_ref[...], g_ref[...])
        n_ref[...] = y.astype(BF16)

    return _call(
        body, name="norm_cast", grid=(t // TM,),
        in_specs=[pl.BlockSpec((TM, w), lambda i: (i, 0)), pl.BlockSpec((1, w), lambda i: (0, 0))],
        out_specs=pl.BlockSpec((TM, w), lambda i: (i, 0)),
        out_shape=jax.ShapeDtypeStruct((t, w), BF16), compiler_params=_params(1))(h, g)


def _ffn_gu(n, wgu):
    t, d = n.shape

    def body(n_ref, wg_ref, wu_ref, gu_ref, act_ref):
        x = n_ref[...]
        g = _dot(x, wg_ref[...])
        u = _dot(x, wu_ref[...])
        gu_ref[0] = g.astype(BF16)
        gu_ref[1] = u.astype(BF16)
        act_ref[...] = (g * jax.nn.sigmoid(g) * u).astype(BF16)

    return _call(
        body, name="ffn_gu", grid=(2, t // TM),
        in_specs=[pl.BlockSpec((TM, d), lambda j, i: (i, 0)),
                  pl.BlockSpec((None, d, FS), lambda j, i: (j, 0, 0)),
                  pl.BlockSpec((None, d, FS), lambda j, i: (j + 2, 0, 0))],
        out_specs=[pl.BlockSpec((2, TM, FS), lambda j, i: (0, i, j)), pl.BlockSpec((TM, FS), lambda j, i: (i, j))],
        out_shape=[jax.ShapeDtypeStruct((2, t, D_FF), BF16), jax.ShapeDtypeStruct((t, D_FF), BF16)],
        compiler_params=_params(2))(n, wgu, wgu)


def _down_res(act, wdn, h):
    t, f = act.shape
    d = h.shape[1]

    def body(a_ref, w_ref, h_ref, o_ref):
        o_ref[...] = h_ref[...] + 0.5 * _dot(a_ref[...], w_ref[...])

    return _call(
        body, name="down_res", grid=(t // TM,),
        in_specs=[pl.BlockSpec((TM, f), lambda i: (i, 0)), pl.BlockSpec((f, d), lambda i: (0, 0)),
                  pl.BlockSpec((TM, d), lambda i: (i, 0))],
        out_specs=pl.BlockSpec((TM, d), lambda i: (i, 0)),
        out_shape=jax.ShapeDtypeStruct((t, d), F32), compiler_params=_params(1))(act, wdn, h)


def _proj(n, w_in):
    t, d = n.shape
    w = w_in.shape[1]

    def body(n_ref, w_ref, o_ref):
        o_ref[...] = _dot(n_ref[...], w_ref[...]).astype(BF16)

    return _call(
        body, name="proj", grid=(t // TM,),
        in_specs=[pl.BlockSpec((TM, d), lambda i: (i, 0)), pl.BlockSpec((d, w), lambda i: (0, 0))],
        out_specs=pl.BlockSpec((TM, w), lambda i: (i, 0)),
        out_shape=jax.ShapeDtypeStruct((t, w), BF16), compiler_params=_params(1))(n, w_in)


def _out_res(o_sb, o_sw, g_sb, g_sw, w_out, h):
    t, d = h.shape

    def body(a_ref, b_ref, ga_ref, gb_ref, w_ref, h_ref, o_ref, mix_ref):
        ya, _, _ = _rms_fwd(a_ref[...], ga_ref[...])
        yb, _, _ = _rms_fwd(b_ref[...], gb_ref[...])
        mixed = jnp.concatenate([ya.astype(BF16), yb.astype(BF16)], axis=1)
        mix_ref[...] = mixed
        o_ref[...] = h_ref[...] + _dot(mixed, w_ref[...])

    return _call(
        body, name="out_res", grid=(t // TM,),
        in_specs=[pl.BlockSpec((TM, SB_W), lambda i: (i, 0)), pl.BlockSpec((TM, SWA_W), lambda i: (i, 0)),
                  pl.BlockSpec((1, SB_W), lambda i: (0, 0)), pl.BlockSpec((1, SWA_W), lambda i: (0, 0)),
                  pl.BlockSpec((d, d), lambda i: (0, 0)), pl.BlockSpec((TM, d), lambda i: (i, 0))],
        out_specs=[pl.BlockSpec((TM, d), lambda i: (i, 0)), pl.BlockSpec((TM, d), lambda i: (i, 0))],
        out_shape=[jax.ShapeDtypeStruct((t, d), F32), jax.ShapeDtypeStruct((t, d), BF16)],
        compiler_params=_params(1))(o_sb, o_sw, g_sb, g_sw, w_out, h)


def _loss_head(h, g, tgt):
    t, d = h.shape

    def body(h_ref, g_ref, t_ref, dh_ref, dg_ref, loss_ref):
        @pl.when(pl.program_id(0) == 0)
        def _():
            dg_ref[...] = jnp.zeros_like(dg_ref)
            loss_ref[...] = jnp.zeros_like(loss_ref)

        gg = g_ref[...]
        y, xh, r = _rms_fwd(h_ref[...], gg)
        err = y - t_ref[...]
        part = 0.5 * jnp.sum(jnp.sum(err * err, axis=1, keepdims=True) / d, axis=0, keepdims=True)
        loss_ref[...] += jnp.broadcast_to(part, loss_ref.shape)
        dx, dg = _rms_bwd(err / d, xh, r, gg)
        dh_ref[...] = dx
        dg_ref[...] += dg

    return _call(
        body, name="loss_head", grid=(t // TM,),
        in_specs=[pl.BlockSpec((TM, d), lambda i: (i, 0)), pl.BlockSpec((1, d), lambda i: (0, 0)),
                  pl.BlockSpec((TM, d), lambda i: (i, 0))],
        out_specs=[pl.BlockSpec((TM, d), lambda i: (i, 0)), pl.BlockSpec((1, d), lambda i: (0, 0)),
                   pl.BlockSpec((1, LANES), lambda i: (0, 0))],
        out_shape=[jax.ShapeDtypeStruct((t, d), F32), jax.ShapeDtypeStruct((1, d), F32),
                   jax.ShapeDtypeStruct((1, LANES), F32)],
        compiler_params=_params(1))(h, g, tgt)


def _ffn_dact(dh, wdn, gu):
    t, d = dh.shape

    def body(dh_ref, w_ref, gu_ref, o_ref):
        da = 0.5 * _dot_nt(dh_ref[...].astype(BF16), w_ref[...])
        g = gu_ref[0].astype(F32)
        u = gu_ref[1].astype(F32)
        sig = jax.nn.sigmoid(g)
        silu = g * sig
        o_ref[0] = (da * u * (sig * (1.0 + g * (1.0 - sig)))).astype(BF16)
        o_ref[1] = (da * silu).astype(BF16)

    return _call(
        body, name="ffn_dact", grid=(2, t // TM),
        in_specs=[pl.BlockSpec((TM, d), lambda j, i: (i, 0)), pl.BlockSpec((FS, d), lambda j, i: (j, 0)),
                  pl.BlockSpec((2, TM, FS), lambda j, i: (0, i, j))],
        out_specs=pl.BlockSpec((2, TM, FS), lambda j, i: (0, i, j)),
        out_shape=jax.ShapeDtypeStruct((2, t, D_FF), BF16), compiler_params=_params(2))(dh, wdn, gu)


def _dn_norm_bwd(a, a_spec, w, w_spec, nk, dh, h_in, g):
    t, d = dh.shape

    def body(a_ref, w_ref, dh_ref, h_ref, g_ref, o_ref, dg_ref, acc_ref):
        i, k = pl.program_id(0), pl.program_id(1)

        @pl.when(k == 0)
        def _():
            acc_ref[...] = jnp.zeros_like(acc_ref)

        acc_ref[...] += _dot_nt(a_ref[...], w_ref[...])

        @pl.when(k == nk - 1)
        def _():
            gg = g_ref[...]
            _, xh, r = _rms_fwd(h_ref[...], gg)
            dx, dg = _rms_bwd(acc_ref[...], xh, r, gg)
            o_ref[...] = dh_ref[...] + dx

            @pl.when(i == 0)
            def _():
                dg_ref[...] = dg

            @pl.when(i > 0)
            def _():
                dg_ref[...] += dg

    row = pl.BlockSpec((TM, d), lambda i, k: (i, 0))
    return _call(
        body, name="dn_norm_bwd", grid=(t // TM, nk),
        in_specs=[a_spec, w_spec, row, row, pl.BlockSpec((1, d), lambda i, k: (0, 0))],
        out_specs=[row, pl.BlockSpec((1, d), lambda i, k: (0, 0))],
        out_shape=[jax.ShapeDtypeStruct((t, d), F32), jax.ShapeDtypeStruct((1, d), F32)],
        scratch_shapes=[pltpu.VMEM((TM, d), F32)], compiler_params=_params(2))(a, w, dh, h_in, g)


def _ffn_dn(dgu, wgu, dh, h_in, g):
    d = dh.shape[1]
    return _dn_norm_bwd(
        dgu, pl.BlockSpec((None, TM, FS), lambda i, k: (k // 2, i, k % 2)),
        wgu, pl.BlockSpec((None, d, FS), lambda i, k: (k, 0, 0)), N_CHIPS, dh, h_in, g)


def _mix_dn(dproj, w_in, dh, h_in, g):
    d = dh.shape[1]
    w = dproj.shape[1]
    return _dn_norm_bwd(
        dproj, pl.BlockSpec((TM, w), lambda i, k: (i, 0)),
        w_in, pl.BlockSpec((d, w), lambda i, k: (0, 0)), 1, dh, h_in, g)


def _dmixed(dh, w_out, o_sb, o_sw, g_sb, g_sw):
    t, d = dh.shape

    def body(dh_ref, w_ref, a_ref, b_ref, ga_ref, gb_ref, o_ref, dga_ref, dgb_ref):
        i = pl.program_id(0)
        dm = _dot_nt(dh_ref[...].astype(BF16), w_ref[...])
        _, xa, ra = _rms_fwd(a_ref[...], ga_ref[...])
        _, xb, rb = _rms_fwd(b_ref[...], gb_ref[...])
        da, dga = _rms_bwd(dm[:, :SB_W], xa, ra, ga_ref[...])
        db, dgb = _rms_bwd(dm[:, SB_W:], xb, rb, gb_ref[...])
        o_ref[...] = jnp.concatenate([da.astype(BF16), db.astype(BF16)], axis=1)

        @pl.when(i == 0)
        def _():
            dga_ref[...] = dga
            dgb_ref[...] = dgb

        @pl.when(i > 0)
        def _():
            dga_ref[...] += dga
            dgb_ref[...] += dgb

    return _call(
        body, name="dmixed", grid=(t // TM,),
        in_specs=[pl.BlockSpec((TM, d), lambda i: (i, 0)), pl.BlockSpec((d, d), lambda i: (0, 0)),
                  pl.BlockSpec((TM, SB_W), lambda i: (i, 0)), pl.BlockSpec((TM, SWA_W), lambda i: (i, 0)),
                  pl.BlockSpec((1, SB_W), lambda i: (0, 0)), pl.BlockSpec((1, SWA_W), lambda i: (0, 0))],
        out_specs=[pl.BlockSpec((TM, d), lambda i: (i, 0)), pl.BlockSpec((1, SB_W), lambda i: (0, 0)),
                   pl.BlockSpec((1, SWA_W), lambda i: (0, 0))],
        out_shape=[jax.ShapeDtypeStruct((t, d), BF16), jax.ShapeDtypeStruct((1, SB_W), F32),
                   jax.ShapeDtypeStruct((1, SWA_W), F32)],
        compiler_params=_params(1))(dh, w_out, o_sb, o_sw, g_sb, g_sw)


def _wgrad(name, a, a_spec, b, b_spec, grid, out_shape, out_spec, scale):
    def body(a_ref, b_ref, o_ref):
        r = _dot_tn(a_ref[...], b_ref[...].astype(BF16))
        o_ref[...] = r if scale == 1.0 else scale * r

    return _call(
        body, name=name, grid=grid, in_specs=[a_spec, b_spec], out_specs=out_spec,
        out_shape=jax.ShapeDtypeStruct(out_shape, F32), compiler_params=_params(len(grid)))(a, b)


def _wgrad_gu(n, dgu):
    t, d = n.shape
    return _wgrad(
        "wgrad_gu", n, pl.BlockSpec((t, TM), lambda s, r: (0, r)),
        dgu, pl.BlockSpec((None, t, FS), lambda s, r: (s // 2, 0, s % 2)), (N_CHIPS, d // TM),
        (N_CHIPS, d, FS), pl.BlockSpec((None, TM, FS), lambda s, r: (s, r, 0)), 1.0)


def _wgrad_down(act, dh):
    t, d = dh.shape
    return _wgrad(
        "wgrad_down", act, pl.BlockSpec((t, FS), lambda s, r: (0, s)), dh, pl.BlockSpec((t, TM), lambda s, r: (0, r)),
        (2, d // TM), (D_FF, d), pl.BlockSpec((FS, TM), lambda s, r: (s, r)), 0.5)


def _wgrad_out(mixed, dh):
    t, d = dh.shape
    return _wgrad(
        "wgrad_out", mixed, pl.BlockSpec((t, TM), lambda s: (0, s)), dh, pl.BlockSpec((t, d), lambda s: (0, 0)),
        (d // TM,), (d, d), pl.BlockSpec((TM, d), lambda s: (s, 0)), 1.0)


def _wgrad_in(n, dproj):
    t, d = n.shape
    w = dproj.shape[1]
    tw = w // 3
    return _wgrad(
        "wgrad_in", n, pl.BlockSpec((t, d), lambda s: (0, 0)), dproj, pl.BlockSpec((t, tw), lambda s: (0, s)),
        (3,), (d, w), pl.BlockSpec((d, tw), lambda s: (0, s)), 1.0)


def _tri(rel):
    row = lax.broadcasted_iota(jnp.int32, (BLK, BLK), 0)
    col = lax.broadcasted_iota(jnp.int32, (BLK, BLK), 1)
    m = rel(row, col).astype(BF16)
    return jnp.concatenate([m, m], axis=0)


def _scan_dot(x, tri2):
    hi = x.astype(BF16)
    lo = (x - hi.astype(F32)).astype(BF16)
    return _dot(jnp.concatenate([hi, lo], axis=1), tri2)


def _head_masks():
    lane = lax.broadcasted_iota(jnp.int32, (1, LANES), 1)
    return [lane < HEAD_DIM, lane >= HEAD_DIM]


def _sb_dcol():
    dcol = lax.broadcasted_iota(jnp.int32, (BLK, SB_KT), 1) - lax.broadcasted_iota(jnp.int32, (BLK, SB_KT), 0)
    return jnp.concatenate([dcol, dcol], axis=0)


def _sb_fwd(proj):
    t = proj.shape[0]
    nq = t // BLK
    nb = SB_KT // BLK

    def body(q_ref, k_ref, v_ref, o_ref, tot_ref):
        hm = _head_masks()
        dcol = _sb_dcol()
        after = _tri(lambda r, c: r > c)

        def tile(qh, kt, carry, acc, limit):
            ks = pl.ds(pl.multiple_of(kt * SB_KT, SB_KT), SB_KT)
            z = _dot_nt(qh, k_ref[ks, :])
            sp = _softplus(z)
            valid = None if limit is None else dcol < limit
            spm = sp if valid is None else jnp.where(valid, sp, 0.0)
            sufs = [None] * nb
            for b in reversed(range(nb)):
                blk = spm[:, b * BLK:(b + 1) * BLK]
                sufs[b] = carry + _scan_dot(blk, after)
                carry = carry + jnp.sum(blk, axis=1, keepdims=True)
            w = jnp.exp(z - sp - jnp.concatenate(sufs, axis=1))
            if valid is not None:
                w = jnp.where(valid, w, 0.0)
            return carry, acc + _dot(w.astype(BF16), v_ref[ks, :])

        def qblock(qi, _):
            qs = pl.ds(pl.multiple_of(qi * BLK, BLK), BLK)
            q = q_ref[qs, :] * SCALE
            kd = qi // nb
            limit = (qi - kd * nb) * BLK
            qh = jnp.concatenate([jnp.where(m, q, jnp.zeros_like(q)) for m in hm], axis=0)
            c0 = tile(qh, kd, jnp.zeros((2 * BLK, 1), F32), jnp.zeros((2 * BLK, LANES), F32), limit)
            carry, acc = lax.fori_loop(0, kd, lambda n, c: tile(qh, kd - 1 - n, c[0], c[1], None), c0)
            o_ref[qs, :] = jnp.where(hm[0], acc[:BLK], acc[BLK:])
            for h in range(2):
                tot_ref[h, qs, :] = jnp.broadcast_to(carry[h * BLK:(h + 1) * BLK], (BLK, LANES))
            return 0

        lax.fori_loop(0, nq, qblock, 0)

    col_blk = lambda off: pl.BlockSpec((t, LANES), lambda p: (0, off + p))
    return _call(
        body, name="sb_fwd", grid=(4,), in_specs=[col_blk(0), col_blk(4), col_blk(8)],
        out_specs=[pl.BlockSpec((t, LANES), lambda p: (0, p)), pl.BlockSpec((2, t, LANES), lambda p: (p, 0, 0))],
        out_shape=[jax.ShapeDtypeStruct((t, SB_W), F32), jax.ShapeDtypeStruct((8, t, LANES), F32)],
        compiler_params=_params(1))(proj, proj, proj)


def _sb_bwd(proj, d_o, tot):
    t = proj.shape[0]
    nq = t // BLK
    nb = SB_KT // BLK

    def body(q_ref, k_ref, v_ref, do_ref, tot_ref, dq_ref, dk_ref, dv_ref, dk_acc, dv_acc):
        hm = _head_masks()
        dcol = _sb_dcol()
        before = _tri(lambda r, c: r < c)
        upto = _tri(lambda r, c: r <= c)
        dk_acc[...] = jnp.zeros_like(dk_acc)
        dv_acc[...] = jnp.zeros_like(dv_acc)

        def tile(qh, doh, tt, kt, pre, ecum, dq, limit):
            ks = pl.ds(pl.multiple_of(kt * SB_KT, SB_KT), SB_KT)
            k = k_ref[ks, :]
            v = v_ref[ks, :]
            z = _dot_nt(qh, k)
            sp = _softplus(z)
            valid = None if limit is None else dcol < limit
            spm = sp if valid is None else jnp.where(valid, sp, 0.0)
            pres = []
            for b in range(nb):
                blk = spm[:, b * BLK:(b + 1) * BLK]
                pres.append(pre + _scan_dot(blk, before))
                pre = pre + jnp.sum(blk, axis=1, keepdims=True)
            logw = z - (tt - jnp.concatenate(pres, axis=1))
            if valid is not None:
                logw = jnp.minimum(logw, 0.0)
            w = jnp.exp(logw)
            if valid is not None:
                w = jnp.where(valid, w, 0.0)
            e = w * _dot_nt(doh, v)
            incs = []
            for b in range(nb):
                blk = e[:, b * BLK:(b + 1) * BLK]
                incs.append(ecum + _scan_dot(blk, upto))
                ecum = ecum + jnp.sum(blk, axis=1, keepdims=True)
            dz = e - jnp.exp(z - sp) * jnp.concatenate(incs, axis=1)
            if valid is not None:
                dz = jnp.where(valid, dz, 0.0)
            dzb = dz.astype(BF16)
            dk_acc[ks, :] += _dot_tn(dzb, qh)
            dv_acc[ks, :] += _dot_tn(w.astype(BF16), doh)
            return pre, ecum, dq + _dot(dzb, k)

        def qblock(qi, _):
            qs = pl.ds(pl.multiple_of(qi * BLK, BLK), BLK)
            q = q_ref[qs, :] * SCALE
            do = do_ref[qs, :]
            kd = qi // nb
            limit = (qi - kd * nb) * BLK
            qh = jnp.concatenate([jnp.where(m, q, jnp.zeros_like(q)) for m in hm], axis=0)
            doh = jnp.concatenate([jnp.where(m, do, jnp.zeros_like(do)) for m in hm], axis=0)
            tt = jnp.concatenate([tot_ref[h, qs, 0:1] for h in range(2)], axis=0)
            c0 = (jnp.zeros((2 * BLK, 1), F32), jnp.zeros((2 * BLK, 1), F32), jnp.zeros((2 * BLK, LANES), F32))
            c = lax.fori_loop(0, kd, lambda kt, c: tile(qh, doh, tt, kt, c[0], c[1], c[2], None), c0)
            dq = tile(qh, doh, tt, kd, c[0], c[1], c[2], limit)[2]
            dq_ref[qs, :] = (jnp.where(hm[0], dq[:BLK], dq[BLK:]) * SCALE).astype(BF16)
            return 0

        lax.fori_loop(0, nq, qblock, 0)
        dk_ref[...] = dk_acc[...].astype(BF16)
        dv_ref[...] = dv_acc[...].astype(BF16)

    col_blk = lambda off: pl.BlockSpec((t, LANES), lambda p: (0, off + p))
    out = jax.ShapeDtypeStruct((t, SB_W), BF16)
    return _call(
        body, name="sb_bwd", grid=(4,),
        in_specs=[col_blk(0), col_blk(4), col_blk(8), col_blk(0), pl.BlockSpec((2, t, LANES), lambda p: (p, 0, 0))],
        out_specs=[col_blk(0), col_blk(0), col_blk(0)], out_shape=[out, out, out],
        scratch_shapes=[pltpu.VMEM((t, LANES), F32), pltpu.VMEM((t, LANES), F32)],
        compiler_params=_params(1))(proj, proj, proj, d_o, tot)


def _bucket_table():
    a = np.arange(BLK)[:, None]
    c = np.arange(2 * BLK)[None, :]
    dist = np.maximum(BLK + a - c, 0)
    max_exact = N_BUCKETS // 2
    dd = np.maximum(dist, 1).astype(np.float32)
    large = max_exact + (np.log(dd / max_exact) / math.log(MAX_DISTANCE / max_exact)
                         * (N_BUCKETS - max_exact)).astype(np.int32)
    large = np.minimum(large, N_BUCKETS - 1)
    return np.where(dist < max_exact, dist, large).astype(np.int32)


def _swa_band_masks():
    row = lax.broadcasted_iota(jnp.int32, (SWA_G * BLK, 2 * BLK), 0) & (BLK - 1)
    col = lax.broadcasted_iota(jnp.int32, (SWA_G * BLK, 2 * BLK), 1)
    own = lax.broadcasted_iota(jnp.int32, (SWA_G * BLK, BLK), 1) <= (
        lax.broadcasted_iota(jnp.int32, (SWA_G * BLK, BLK), 0) & (BLK - 1))
    return (col > row) & ((col < BLK) | (col - BLK <= row)), own


def _swa_stack(ref, qs, kvh, kvmask, scale):
    parts = []
    for g in range(SWA_G):
        hq = SWA_G * kvh + g
        x = ref[qs, (hq // 2) * LANES:(hq // 2 + 1) * LANES].astype(F32)
        if hq % 2 != kvh:
            x = pltpu.roll(x, HEAD_DIM, 1)
        parts.append(jnp.where(kvmask, x * scale, 0.0).astype(BF16))
    return jnp.concatenate(parts, axis=0)


def _swa_unstack(x4, kvh, hm):
    heads = []
    for g in range(SWA_G):
        x = x4[g * BLK:(g + 1) * BLK]
        heads.append(pltpu.roll(x, HEAD_DIM, 1) if g % 2 != kvh else x)
    return [jnp.where(hm[0], heads[0], heads[1]), jnp.where(hm[0], heads[2], heads[3])]


def _swa_scores(q4, kb, bias_ref, kvh, mask, cols):
    bias4 = jnp.concatenate([bias_ref[SWA_G * kvh + g, :, cols] for g in range(SWA_G)], axis=0)
    return jnp.where(mask, _dot_nt(q4, kb) + bias4, NEG_INF)


def _swa_sinks(sink_ref, kvh):
    return jnp.concatenate([jnp.broadcast_to(sink_ref[SWA_G * kvh + g:SWA_G * kvh + g + 1, 0:1], (BLK, 1))
                            for g in range(SWA_G)], axis=0)


def _swa_fwd(proj, bias, sinks_b):
    t = proj.shape[0]
    nq = t // BLK

    def body(q_ref, k_ref, v_ref, bias_ref, sink_ref, o_ref, lse_ref):
        hm = _head_masks()
        band, own = _swa_band_masks()

        def qblock(i, kvh, prev):
            qs = pl.ds(pl.multiple_of(i * BLK, BLK), BLK)
            if prev:
                ks, mask, cols = pl.ds(pl.multiple_of((i - 1) * BLK, BLK), 2 * BLK), band, slice(None)
            else:
                ks, mask, cols = qs, own, slice(BLK, None)
            q4 = _swa_stack(q_ref, qs, kvh, hm[kvh], SCALE)
            sink4 = _swa_sinks(sink_ref, kvh)
            s = _swa_scores(q4, k_ref[ks, :], bias_ref, kvh, mask, cols)
            m = jnp.maximum(jnp.max(s, axis=1, keepdims=True), sink4)
            p = jnp.exp(s - m)
            den = jnp.sum(p, axis=1, keepdims=True) + jnp.exp(sink4 - m)
            o4 = _dot((p * (1.0 / den)).astype(BF16), v_ref[ks, :])
            lse4 = m + jnp.log(den)
            for g in range(SWA_G):
                lse_ref[SWA_G * kvh + g, qs, :] = jnp.broadcast_to(lse4[g * BLK:(g + 1) * BLK], (BLK, LANES))
            for pp, o in enumerate(_swa_unstack(o4, kvh, hm)):
                o_ref[qs, (2 * kvh + pp) * LANES:(2 * kvh + pp + 1) * LANES] = o

        for kvh in range(2):
            qblock(0, kvh, False)

            def step(i, _):
                qblock(i, kvh, True)
                return 0

            lax.fori_loop(1, nq, step, 0)

    return _call(
        body, name="swa_fwd", grid=(1,),
        in_specs=[pl.BlockSpec((t, SWA_W), lambda i: (0, 3)), pl.BlockSpec((t, KV_W), lambda i: (0, 16)),
                  pl.BlockSpec((t, KV_W), lambda i: (0, 17)), pl.BlockSpec((8, BLK, 2 * BLK), lambda i: (0, 0, 0)),
                  pl.BlockSpec((8, LANES), lambda i: (0, 0))],
        out_specs=[pl.BlockSpec((t, SWA_W), lambda i: (0, 0)), pl.BlockSpec((8, t, LANES), lambda i: (0, 0, 0))],
        out_shape=[jax.ShapeDtypeStruct((t, SWA_W), F32), jax.ShapeDtypeStruct((8, t, LANES), F32)],
        compiler_params=_params(1))(proj, proj, proj, bias, sinks_b)


def _swa_bwd(proj, d_o, lse, bias, sinks_b, dbias_in):
    t = proj.shape[0]
    nq = t // BLK

    def body(q_ref, k_ref, v_ref, do_ref, lse_ref, bias_ref, sink_ref, dbi_ref,
             dq_ref, dk_ref, dv_ref, dsink_ref, dbias_ref, dk_acc, dv_acc):
        hm = _head_masks()
        band, own = _swa_band_masks()
        dk_acc[...] = jnp.zeros_like(dk_acc)
        dv_acc[...] = jnp.zeros_like(dv_acc)
        dbias_ref[...] = dbi_ref[...]

        def qblock(i, kvh, prev, dsink4):
            qs = pl.ds(pl.multiple_of(i * BLK, BLK), BLK)
            if prev:
                ks, mask, cols = pl.ds(pl.multiple_of((i - 1) * BLK, BLK), 2 * BLK), band, slice(None)
            else:
                ks, mask, cols = qs, own, slice(BLK, None)
            q4 = _swa_stack(q_ref, qs, kvh, hm[kvh], SCALE)
            do4 = _swa_stack(do_ref, qs, kvh, hm[kvh], 1.0)
            sink4 = _swa_sinks(sink_ref, kvh)
            lse4 = jnp.concatenate([lse_ref[SWA_G * kvh + g, qs, 0:1] for g in range(SWA_G)], axis=0)
            kb = k_ref[ks, :]
            p = jnp.exp(_swa_scores(q4, kb, bias_ref, kvh, mask, cols) - lse4)
            dp = _dot_nt(do4, v_ref[ks, :])
            delta = jnp.sum(p * dp, axis=1, keepdims=True)
            ds = p * (dp - delta)
            for g in range(SWA_G):
                dbias_ref[SWA_G * kvh + g, :, cols] += ds[g * BLK:(g + 1) * BLK]
            dsb = ds.astype(BF16)
            dk_acc[ks, :] += _dot_tn(dsb, q4)
            dv_acc[ks, :] += _dot_tn(p.astype(BF16), do4)
            for pp, dq in enumerate(_swa_unstack(_dot(dsb, kb) * SCALE, kvh, hm)):
                dq_ref[qs, (2 * kvh + pp) * LANES:(2 * kvh + pp + 1) * LANES] = dq.astype(BF16)
            return dsink4 - jnp.exp(sink4 - lse4) * delta

        for kvh in range(2):
            ds0 = qblock(0, kvh, False, jnp.zeros((SWA_G * BLK, 1), F32))
            ds4 = lax.fori_loop(1, nq, lambda i, c: qblock(i, kvh, True, c), ds0)
            for g in range(SWA_G):
                hq = SWA_G * kvh + g
                dsink_ref[hq:hq + 1, :] = jnp.broadcast_to(
                    jnp.sum(ds4[g * BLK:(g + 1) * BLK], axis=0, keepdims=True), (1, LANES))

        dk_ref[...] = dk_acc[...].astype(BF16)
        dv_ref[...] = dv_acc[...].astype(BF16)

    full3 = pl.BlockSpec((8, BLK, 2 * BLK), lambda i: (0, 0, 0))
    kv = jax.ShapeDtypeStruct((t, KV_W), BF16)
    return _call(
        body, name="swa_bwd", grid=(1,),
        in_specs=[pl.BlockSpec((t, SWA_W), lambda i: (0, 3)), pl.BlockSpec((t, KV_W), lambda i: (0, 16)),
                  pl.BlockSpec((t, KV_W), lambda i: (0, 17)), pl.BlockSpec((t, SWA_W), lambda i: (0, 1)),
                  pl.BlockSpec((8, t, LANES), lambda i: (0, 0, 0)), full3, pl.BlockSpec((8, LANES), lambda i: (0, 0)),
                  full3],
        out_specs=[pl.BlockSpec((t, SWA_W), lambda i: (0, 0)), pl.BlockSpec((t, KV_W), lambda i: (0, 0)),
                   pl.BlockSpec((t, KV_W), lambda i: (0, 0)), pl.BlockSpec((8, LANES), lambda i: (0, 0)), full3],
        out_shape=[jax.ShapeDtypeStruct((t, SWA_W), BF16), kv, kv, jax.ShapeDtypeStruct((8, LANES), F32),
                   jax.ShapeDtypeStruct((8, BLK, 2 * BLK), F32)],
        scratch_shapes=[pltpu.VMEM((t, KV_W), F32), pltpu.VMEM((t, KV_W), F32)],
        compiler_params=_params(1))(proj, proj, proj, d_o, lse, bias, sinks_b, dbias_in)


def _bias_table(rel_bias, buckets):
    def body(rb_ref, b_ref, o_ref):
        bk = b_ref[...]
        for h in range(8):
            acc = jnp.zeros((BLK, 2 * BLK), F32)
            for b in range(N_BUCKETS):
                acc = jnp.where(bk == b, rb_ref[b, h], acc)
            o_ref[h] = acc

    return _call(
        body, name="bias_table", grid=(1,),
        in_specs=[pl.BlockSpec(memory_space=pltpu.SMEM), pl.BlockSpec((BLK, 2 * BLK), lambda i: (0, 0))],
        out_specs=pl.BlockSpec((8, BLK, 2 * BLK), lambda i: (0, 0, 0)),
        out_shape=jax.ShapeDtypeStruct((8, BLK, 2 * BLK), F32), compiler_params=_params(1))(rel_bias, buckets)


def _bias_grad(dbias, buckets):
    def body(d_ref, b_ref, o_ref):
        lane = lax.broadcasted_iota(jnp.int32, (1, LANES), 1)
        bk = b_ref[...]
        for h in range(8):
            d = d_ref[h]
            acc = jnp.zeros((1, LANES), F32)
            for b in range(N_BUCKETS):
                s = jnp.sum(jnp.sum(jnp.where(bk == b, d, 0.0), axis=0, keepdims=True), axis=1, keepdims=True)
                acc = acc + jnp.where(lane == b, s, 0.0)
            o_ref[h:h + 1, :] = acc

    return _call(
        body, name="bias_grad", grid=(1,),
        in_specs=[pl.BlockSpec((8, BLK, 2 * BLK), lambda i: (0, 0, 0)), pl.BlockSpec((BLK, 2 * BLK), lambda i: (0, 0))],
        out_specs=pl.BlockSpec((8, LANES), lambda i: (0, 0)),
        out_shape=jax.ShapeDtypeStruct((8, LANES), F32), compiler_params=_params(1))(dbias, buckets)


def _row(a):
    return a.reshape(1, -1)


def _fwd_ffn1(h, w, small, l):
    s = {"h0": h}
    s["n1"] = _norm_cast(h, _row(small["norm_ffn1"][l]))
    s["gu1"], s["act1"] = _ffn_gu(s["n1"], w["ffn1_gu"])
    s["h1"] = _down_res(s["act1"], w["ffn1_down"], h)
    s["nm"] = _norm_cast(s["h1"], _row(small["norm_mix"][l]))
    return s


def _fwd_proj_sb(s, w):
    s["proj"] = _proj(s["nm"], w["w_in"])
    s["o_sb"], s["tot"] = _sb_fwd(s["proj"])


def _fwd_swa(s, small, l, bias):
    s["sinks_b"] = jnp.broadcast_to(small["sinks"][l][:, None], (8, LANES))
    s["o_sw"], s["lse"] = _swa_fwd(s["proj"], bias, s["sinks_b"])


def _fwd_out_ffn2(s, w, small, l):
    s["h2"], s["mixed"] = _out_res(s["o_sb"], s["o_sw"], _row(small["norm_out_sb"][l]), _row(small["norm_out_swa"][l]),
                                  w["w_out"], s["h1"])
    s["n2"] = _norm_cast(s["h2"], _row(small["norm_ffn2"][l]))
    s["gu2"], s["act2"] = _ffn_gu(s["n2"], w["ffn2_gu"])
    return _down_res(s["act2"], w["ffn2_down"], s["h2"])


def _bwd_ffn_dact(dh, s, w, which):
    return _ffn_dact(dh, w[f"ffn{which}_down"], s[f"gu{which}"])


def _bwd_ffn_rest(dh, dgu, s, w, small, l, which):
    h_in, norm = (s["h0"], "norm_ffn1") if which == 1 else (s["h2"], "norm_ffn2")
    g_down = _wgrad_down(s[f"act{which}"], dh)
    g_gu = _wgrad_gu(s[f"n{which}"], dgu)
    dh, dg = _ffn_dn(dgu, w[f"ffn{which}_gu"], dh, h_in, _row(small[norm][l]))
    return dh, {f"ffn{which}_down": g_down, f"ffn{which}_gu": g_gu}, {norm: dg}


def _bwd_ffn(dh, s, w, small, l, which):
    return _bwd_ffn_rest(dh, _bwd_ffn_dact(dh, s, w, which), s, w, small, l, which)


def _bwd_mix(dh, s, w, small, l, bias, dbias):
    g_out = _wgrad_out(s["mixed"], dh)
    d_o, dg_sb, dg_sw = _dmixed(dh, w["w_out"], s["o_sb"], s["o_sw"], _row(small["norm_out_sb"][l]),
                                _row(small["norm_out_swa"][l]))
    dq_sb, dk_sb, dv_sb = _sb_bwd(s["proj"], d_o, s["tot"])
    dq_sw, dk_sw, dv_sw, dsink, dbias = _swa_bwd(s["proj"], d_o, s["lse"], bias, s["sinks_b"], dbias)
    dproj = jnp.concatenate([dq_sb, dk_sb, dv_sb, dq_sw, dk_sw, dv_sw], axis=1)
    g_in = _wgrad_in(s["nm"], dproj)
    dh, dg_mix = _mix_dn(dproj, w["w_in"], dh, s["h1"], _row(small["norm_mix"][l]))
    gs = {"norm_out_sb": dg_sb, "norm_out_swa": dg_sw, "sinks": dsink[:, 0], "norm_mix": dg_mix}
    return dh, {"w_out": g_out, "w_in": g_in}, gs, dbias


def _place():
    x, y, c = lax.axis_index("x"), lax.axis_index("y"), lax.axis_index("c")
    return x, y, c, 2 * x + y


def _chip_core(k, c):
    return (k // 2, k % 2, c)


def _rows_per_block(rows, cols, copies):
    best = 16
    for tr in range(16, rows + 1, 16):
        if rows % tr == 0 and copies * tr * cols * 4 <= 6 * 2 ** 20:
            best = tr
    assert rows % best == 0
    return best


def _place_own(w, l, me1):
    _, rows, cols = w.shape
    tr = _rows_per_block(rows, cols, 1)

    def body(me_ref, w_ref, o_ref):
        o_ref[...] = w_ref[...].astype(BF16)

    return _call(
        body, name="place_own",
        num_scalar_prefetch=1, grid=(rows // tr,),
        in_specs=[pl.BlockSpec((None, tr, cols), lambda r, me: (l, r, 0))],
        out_specs=pl.BlockSpec((None, tr, cols), lambda r, me: (me[0], r, 0)),
        out_shape=jax.ShapeDtypeStruct((N_CHIPS, rows, cols), BF16), compiler_params=_params(1))(me1, w)


def _plan_gather_ici(bufs):
    _, _, c, me = _place()
    return [(b.at[me, c], b.at[me, c], b.at[(me + 3 - j) % N_CHIPS, c], _chip_core((me + 1 + j) % N_CHIPS, c))
            for b in bufs for j in range(3)]


def _plan_gather_d2d(bufs):
    x, y, c, me = _place()
    return [(b.at[(me + 3 - j) % N_CHIPS, c], b.at[(me + 3 - j) % N_CHIPS, c], b.at[(me + 3 - j) % N_CHIPS, 1 - c],
             (x, y, 1 - c)) for b in bufs for j in range(3)]


def _plan_grad_sibling(bufs):
    x, y, c, _ = _place()
    n = len(bufs) // 2
    return [(g.at[:, 1 - c], z, z, (x, y, 1 - c)) for g, z in zip(bufs[:n], bufs[n:])]


def _plan_grad_chips(bufs):
    _, _, c, me = _place()
    n = len(bufs) // 2
    return [(p.at[(me + 1 + j) % N_CHIPS], z.at[j], z.at[j], _chip_core((me + 1 + j) % N_CHIPS, c))
            for p, z in zip(bufs[:n], bufs[n:]) for j in range(3)]


def _plan_grad_halves(bufs):
    x, y, c, _ = _place()
    return [(b.at[c], b.at[c], b.at[1 - c], (x, y, 1 - c)) for b in bufs]


def _remote(src, dst, send_sem, recv_sem, to):
    return pltpu.make_async_remote_copy(src_ref=src, dst_ref=dst, send_sem=send_sem, recv_sem=recv_sem,
                                        device_id=to, device_id_type=MESH)


def _exchange_now(name, plan, bufs, n_copies):
    n = len(bufs)

    def body(*refs):
        outs, (ssem, rsem) = refs[n:2 * n], refs[2 * n:]
        copies = plan(outs)
        for i, (src, dst, _, to) in enumerate(copies):
            _remote(src, dst, ssem.at[i], rsem.at[i], to).start()
        for i, (src, dst, land, to) in enumerate(copies):
            _remote(land, land, ssem.at[i], rsem.at[i], to).wait_recv()
        for i, (src, dst, _, to) in enumerate(copies):
            _remote(src, dst, ssem.at[i], rsem.at[i], to).wait_send()

    return _call(
        body, name=name, in_specs=[ANY] * n, out_specs=[ANY] * n,
        out_shape=[jax.ShapeDtypeStruct(a.shape, a.dtype) for a in bufs],
        input_output_aliases={t: t for t in range(n)},
        scratch_shapes=[pltpu.SemaphoreType.DMA((n_copies,))] * 2,
        compiler_params=pltpu.CompilerParams(vmem_limit_bytes=V7X_VMEM_LIMIT))(*bufs)


def _exchange_start(name, plan, bufs, n_copies):
    n = len(bufs)

    def body(*refs):
        ins = refs[:n]
        ssem, rsem = refs[n], refs[n + 1]
        token = refs[-1]
        for i, (src, dst, _, to) in enumerate(plan(ins)):
            _remote(src, dst, ssem.at[i], rsem.at[i], to).start()
        token[...] = jnp.zeros_like(token)

    out = _call(
        body, name=name,
        out_shape=(pltpu.SemaphoreType.DMA((n_copies,)), pltpu.SemaphoreType.DMA((n_copies,)),
                   *[pltpu.HBM(a.shape, a.dtype) for a in bufs], jax.ShapeDtypeStruct((8, LANES), F32)),
        in_specs=[HBM] * n, out_specs=(SEM, SEM, *[HBM] * n, pl.BlockSpec(memory_space=pltpu.VMEM)),
        input_output_aliases={t: 2 + t for t in range(n)},
        compiler_params=pltpu.CompilerParams(has_side_effects=EFFECT),
    )(*[pltpu.with_memory_space_constraint(a, pltpu.HBM) for a in bufs])
    return (out[0], out[1]), list(out[2:2 + n])


def _exchange_wait(name, plan, bufs, sems):
    n = len(bufs)

    def body(*refs):
        ins = refs[:n]
        ssem, rsem = refs[n], refs[n + 1]
        for i, (src, dst, land, to) in enumerate(plan(ins)):
            _remote(src, dst, ssem.at[i], rsem.at[i], to).wait_send()
            _remote(land, land, ssem.at[i], rsem.at[i], to).wait_recv()

    return list(_call(
        body, name=name, out_shape=[pltpu.HBM(a.shape, a.dtype) for a in bufs],
        in_specs=[HBM] * n + [SEM, SEM], out_specs=[HBM] * n,
        input_output_aliases={t: t for t in range(n)},
        compiler_params=pltpu.CompilerParams(has_side_effects=EFFECT),
    )(*bufs, sems[0], sems[1]))


def _gather_now(bufs):
    n = len(bufs)
    n_cp = 3 * n

    def body(*refs):
        outs = refs[n:2 * n]
        ici_s, ici_r, d2d_s, d2d_r = refs[2 * n:]
        first = _plan_gather_ici(outs)
        second = _plan_gather_d2d(outs)
        for i, (src, dst, _, to) in enumerate(first):
            _remote(src, dst, ici_s.at[i], ici_r.at[i], to).start()
        for i, (src, dst, _, to) in enumerate(second):
            land = first[i][2]
            _remote(land, land, ici_s.at[i], ici_r.at[i], to).wait_recv()
            _remote(src, dst, d2d_s.at[i], d2d_r.at[i], to).start()
        for i, (_, _, land, to) in enumerate(second):
            _remote(land, land, d2d_s.at[i], d2d_r.at[i], to).wait_recv()
        for i in range(n_cp):
            _remote(first[i][0], first[i][1], ici_s.at[i], ici_r.at[i], first[i][3]).wait_send()
            _remote(second[i][0], second[i][1], d2d_s.at[i], d2d_r.at[i], second[i][3]).wait_send()

    return _call(
        body, name="gather_layer0", in_specs=[ANY] * n, out_specs=[ANY] * n,
        out_shape=[jax.ShapeDtypeStruct(a.shape, a.dtype) for a in bufs],
        input_output_aliases={t: t for t in range(n)},
        scratch_shapes=[pltpu.SemaphoreType.DMA((n_cp,))] * 4,
        compiler_params=pltpu.CompilerParams(vmem_limit_bytes=V7X_VMEM_LIMIT))(*bufs)


def _chip_sum(g, xbuf, cm):
    _, _, r2, cols = g.shape
    tr = _rows_per_block(r2, cols, N_CHIPS)

    def body(cm_ref, g_ref, x_ref, pb_ref, po_ref):
        pb_ref[...] = (g_ref[...] + x_ref[...]).astype(BF16)
        me = cm_ref[1]
        po_ref[...] = g_ref[me] + x_ref[me]

    return _call(
        body, name="grad_chip_sum",
        num_scalar_prefetch=1, grid=(r2 // tr,),
        in_specs=[pl.BlockSpec((N_CHIPS, None, tr, cols), lambda r, cm: (0, cm[0], r, 0)),
                  pl.BlockSpec((N_CHIPS, tr, cols), lambda r, cm: (0, r, 0))],
        out_specs=[pl.BlockSpec((N_CHIPS, tr, cols), lambda r, cm: (0, r, 0)),
                   pl.BlockSpec((tr, cols), lambda r, cm: (r, 0))],
        out_shape=[jax.ShapeDtypeStruct((N_CHIPS, r2, cols), BF16), jax.ShapeDtypeStruct((r2, cols), F32)],
        compiler_params=_params(1))(cm, g, xbuf)


def _total_sum(pown, rbuf, cm):
    r2, cols = pown.shape
    tr = _rows_per_block(r2, cols, 3)

    def body(cm_ref, p_ref, r_ref, o_ref):
        acc = p_ref[...]
        for j in range(3):
            acc = acc + r_ref[j].astype(F32)
        o_ref[...] = acc

    return _call(
        body, name="grad_total_sum",
        num_scalar_prefetch=1, grid=(r2 // tr,),
        in_specs=[pl.BlockSpec((tr, cols), lambda r, cm: (r, 0)),
                  pl.BlockSpec((3, tr, cols), lambda r, cm: (0, r, 0))],
        out_specs=pl.BlockSpec((None, tr, cols), lambda r, cm: (cm[0], r, 0)),
        out_shape=jax.ShapeDtypeStruct((2, r2, cols), F32), compiler_params=_params(1))(cm, pown, rbuf)


def _small_allreduce(v):
    rows = v.shape[0]
    n_dev = 2 * N_CHIPS

    def body(v_ref, o_ref, buf, ssem, rsem):
        x, y, c, _ = _place()
        me = 4 * x + 2 * y + c
        buf[me] = v_ref[...]

        def copy(d, slot, to):
            return _remote(v_ref, buf.at[slot], ssem.at[d - 1], rsem.at[d - 1], (to // 4, (to // 2) % 2, to % 2))

        cps = [copy(d, me, (me + d) % n_dev) for d in range(1, n_dev)]
        for cp in cps:
            cp.start()
        for d in range(1, n_dev):
            copy(d, (me + n_dev - d) % n_dev, me).wait_recv()
        for cp in cps:
            cp.wait_send()
        acc = buf[0]
        for i in range(1, n_dev):
            acc = acc + buf[i]
        o_ref[...] = acc

    vm = pl.BlockSpec(memory_space=pltpu.VMEM)
    return _call(
        body, name="small_allreduce", in_specs=[vm], out_specs=vm,
        out_shape=jax.ShapeDtypeStruct(v.shape, F32),
        scratch_shapes=[pltpu.VMEM((n_dev, rows, LANES), F32), pltpu.SemaphoreType.DMA((n_dev - 1,)),
                        pltpu.SemaphoreType.DMA((n_dev - 1,))],
        compiler_params=pltpu.CompilerParams(vmem_limit_bytes=V7X_VMEM_LIMIT))(v)


def _adamw_math(w, g, m, v):
    m2 = ADAM_B1 * m + (1.0 - ADAM_B1) * g
    v2 = ADAM_B2 * v + (1.0 - ADAM_B2) * (g * g)
    m_hat = m2 / (1.0 - ADAM_B1 ** ADAM_STEP)
    v_hat = v2 / (1.0 - ADAM_B2 ** ADAM_STEP)
    return -ADAM_LR * (m_hat / (jnp.sqrt(v_hat) + ADAM_EPS) + ADAM_WD * w), m2, v2


def _adamw_layer(w, g, m, v, l, prev):
    _, rows, cols = w.shape
    tr = rows
    for cand in range(8, rows + 1, 8):
        if rows % cand == 0 and cand * cols * 4 <= 2 ** 21:
            tr = cand

    def body(w_ref, g_ref, m_ref, v_ref, *outs):
        go_ref, d_ref, m2_ref, v2_ref = outs[-4:]
        g = g_ref[...]
        go_ref[...] = g
        d_ref[...], m2_ref[...], v2_ref[...] = _adamw_math(w_ref[...], g, m_ref[...], v_ref[...])

    stack = pl.BlockSpec((None, tr, cols), lambda i: (l, i, 0))
    ins, specs, alias = [w, g, m, v], [stack, pl.BlockSpec((tr, cols), lambda i: (i, 0)), stack, stack], {}
    if prev is not None:
        ins += list(prev)
        specs += [ANY] * 4
        alias = {4 + i: i for i in range(4)}
    return _call(
        body, name="adamw", grid=(rows // tr,), in_specs=specs, out_specs=[stack] * 4,
        out_shape=[jax.ShapeDtypeStruct(w.shape, F32)] * 4, input_output_aliases=alias,
        compiler_params=_params(1))(*ins)


def _adamw_small(w, g, m, v):
    def body(w_ref, g_ref, m_ref, v_ref, d_ref, m2_ref, v2_ref):
        d_ref[...], m2_ref[...], v2_ref[...] = _adamw_math(w_ref[...], g_ref[...], m_ref[...], v_ref[...])

    spec = pl.BlockSpec(w.shape, lambda i: (0, 0))
    return _call(
        body, name="adamw_small", grid=(1,), in_specs=[spec] * 4, out_specs=[spec] * 3,
        out_shape=[jax.ShapeDtypeStruct(w.shape, F32)] * 3, compiler_params=_params(1))(w, g, m, v)


SMALL = ("norm_ffn1", "norm_mix", "sinks", "norm_out_sb", "norm_out_swa", "norm_ffn2", "rel_bias", "norm_final")
BIG = ("ffn1_gu", "ffn1_down", "w_in", "w_out", "ffn2_gu", "ffn2_down")


def _pack(parts):
    rows = []
    for a in parts:
        a = a.reshape(-1).astype(F32)
        rows.append(jnp.pad(a, (0, -a.shape[0] % LANES)).reshape(-1, LANES))
    out = jnp.concatenate(rows, axis=0)
    return jnp.pad(out, ((0, -out.shape[0] % 8), (0, 0)))


def _unpack(packed, like):
    out, r = [], 0
    for a in like:
        n = math.prod(a.shape)
        nr = -(-n // LANES)
        out.append(packed[r:r + nr].reshape(-1)[:n].reshape(a.shape))
        r += nr
    return out


def _halved(a):
    k, r, cols = a.shape
    return a.reshape(k, 2, r // 2, cols)


def _weight_view(k, buf):
    full = buf.reshape(N_CHIPS, buf.shape[2] * 2, buf.shape[3])
    if k.endswith("_gu"):
        return full
    if k == "w_in":
        return jnp.transpose(full, (1, 0, 2)).reshape(D_MODEL, IN_W)
    return full.reshape(-1, D_MODEL)


def _grad_stack(k, g):
    if k == "w_in":
        g = jnp.transpose(g.reshape(D_MODEL, N_CHIPS, IN_W // N_CHIPS), (1, 0, 2))
    elif not k.endswith("_gu"):
        g = g.reshape(N_CHIPS, g.shape[0] // N_CHIPS, D_MODEL)
    return _halved(g)


def _empty_like_hbm(shape, dtype):
    return pltpu.with_memory_space_constraint(lax.empty(shape, dtype), pltpu.HBM)


def kernel(x, norm_ffn1, w_ffn1_gu, w_ffn1_down, norm_mix, w_in, sinks, norm_out_sb, norm_out_swa, w_out, norm_ffn2, w_ffn2_gu, w_ffn2_down, rel_bias, norm_final, loss_target, m_norm_ffn1, m_w_ffn1_gu, m_w_ffn1_down, m_norm_mix, m_w_in, m_sinks, m_norm_out_sb, m_norm_out_swa, m_w_out, m_norm_ffn2, m_w_ffn2_gu, m_w_ffn2_down, m_rel_bias, m_norm_final, v_norm_ffn1, v_w_ffn1_gu, v_w_ffn1_down, v_norm_mix, v_w_in, v_sinks, v_norm_out_sb, v_norm_out_swa, v_w_out, v_norm_ffn2, v_w_ffn2_gu, v_w_ffn2_down, v_rel_bias, v_norm_final):
    big_w = dict(ffn1_gu=w_ffn1_gu, ffn1_down=w_ffn1_down, w_in=w_in, w_out=w_out, ffn2_gu=w_ffn2_gu, ffn2_down=w_ffn2_down)
    big_m = dict(ffn1_gu=m_w_ffn1_gu, ffn1_down=m_w_ffn1_down, w_in=m_w_in, w_out=m_w_out, ffn2_gu=m_w_ffn2_gu, ffn2_down=m_w_ffn2_down)
    big_v = dict(ffn1_gu=v_w_ffn1_gu, ffn1_down=v_w_ffn1_down, w_in=v_w_in, w_out=v_w_out, ffn2_gu=v_w_ffn2_gu, ffn2_down=v_w_ffn2_down)
    small = dict(norm_ffn1=norm_ffn1, norm_mix=norm_mix, sinks=sinks, norm_out_sb=norm_out_sb, norm_out_swa=norm_out_swa,
                 norm_ffn2=norm_ffn2, rel_bias=rel_bias, norm_final=norm_final)
    small_m = dict(norm_ffn1=m_norm_ffn1, norm_mix=m_norm_mix, sinks=m_sinks, norm_out_sb=m_norm_out_sb,
                   norm_out_swa=m_norm_out_swa, norm_ffn2=m_norm_ffn2, rel_bias=m_rel_bias, norm_final=m_norm_final)
    small_v = dict(norm_ffn1=v_norm_ffn1, norm_mix=v_norm_mix, sinks=v_sinks, norm_out_sb=v_norm_out_sb,
                   norm_out_swa=v_norm_out_swa, norm_ffn2=v_norm_ffn2, rel_bias=v_rel_bias, norm_final=v_norm_final)
    _PREVIOUS[0] = None
    _, _, c, me = _place()
    cm = jnp.stack([c, me]).astype(jnp.int32)
    buckets = jnp.asarray(_bucket_table())
    ffn1, mix_in, rest = ("ffn1_gu", "ffn1_down"), ("w_in",), ("w_out", "ffn2_gu", "ffn2_down")

    def place(l, keys):
        return [_halved(_place_own(big_w[k], l, cm[1:])) for k in keys]

    def views(keys, bufs):
        return {k: _weight_view(k, b) for k, b in zip(keys, bufs)}

    def gather_start(tag, bufs):
        return _exchange_start(f"gather{tag}_ici_start", _plan_gather_ici, bufs, 3 * len(bufs))

    def gather_pass(tag, flight):
        bufs = _exchange_wait(f"gather{tag}_ici_wait", _plan_gather_ici, flight[1], flight[0])
        return _exchange_start(f"gather{tag}_d2d_start", _plan_gather_d2d, bufs, 3 * len(bufs))

    def gather_done(tag, keys, flight):
        return views(keys, _exchange_wait(f"gather{tag}_d2d_wait", _plan_gather_d2d, flight[1], flight[0]))

    w0 = views(ffn1, _gather_now(place(0, ffn1)))
    fly_in0 = gather_start("0b", place(0, mix_in))
    fly_rest0 = gather_start("0c", place(0, rest))
    bias = _bias_table(rel_bias, buckets)
    fly_ffn1 = gather_start("1a", place(1, ffn1))
    fly_rest1 = gather_start("1b", place(1, mix_in + rest))

    s0 = _fwd_ffn1(x[0], w0, small, 0)
    w0.update(gather_done("0b", mix_in, gather_pass("0b", fly_in0)))
    _fwd_proj_sb(s0, w0)
    fly_rest0 = gather_pass("0c", fly_rest0)
    _fwd_swa(s0, small, 0, bias)
    w0.update(gather_done("0c", rest, fly_rest0))
    h = _fwd_out_ffn2(s0, w0, small, 0)
    fly_ffn1 = gather_pass("1a", fly_ffn1)
    fly_rest1 = gather_pass("1b", fly_rest1)
    w1 = gather_done("1a", ffn1, fly_ffn1)
    s1 = _fwd_ffn1(h, w1, small, 1)
    w1.update(gather_done("1b", mix_in + rest, fly_rest1))
    _fwd_proj_sb(s1, w1)
    _fwd_swa(s1, small, 1, bias)
    h = _fwd_out_ffn2(s1, w1, small, 1)
    dh, dg_final, loss_row = _loss_head(h, _row(norm_final), loss_target[0])

    def landing(stacks, lead, dtype):
        return [_empty_like_hbm((lead,) + a.shape[2:], dtype) for a in stacks]

    def reduce_begin(tag, keys, gw):
        stacks = [_grad_stack(k, gw[k]) for k in keys]
        flight = _exchange_start(f"grad{tag}_sibling_start", _plan_grad_sibling,
                                 stacks + landing(stacks, N_CHIPS, F32), len(keys))
        return dict(tag=tag, keys=keys, stacks=stacks, flight=flight)

    def reduce_chips(st):
        n, (sems, bufs) = len(st["keys"]), st["flight"]
        bufs = _exchange_wait(f"grad{st['tag']}_sibling_wait", _plan_grad_sibling, bufs, sems)
        st["sums"] = [_chip_sum(g, z, cm) for g, z in zip(bufs[:n], bufs[n:])]
        st["flight"] = _exchange_start(f"grad{st['tag']}_chips_start", _plan_grad_chips,
                                       [s[0] for s in st["sums"]] + landing(st["stacks"], 3, BF16), 3 * n)

    def reduce_halves(st):
        n, (sems, bufs) = len(st["keys"]), st["flight"]
        bufs = _exchange_wait(f"grad{st['tag']}_chips_wait", _plan_grad_chips, bufs, sems)
        halves = [_total_sum(s[1], z, cm) for s, z in zip(st["sums"], bufs[n:])]
        st["flight"] = _exchange_start(f"grad{st['tag']}_halves_start", _plan_grad_halves, halves, n)

    def reduce_end(st):
        sems, bufs = st["flight"]
        bufs = _exchange_wait(f"grad{st['tag']}_halves_wait", _plan_grad_halves, bufs, sems)
        return {k: b.reshape(big_w[k].shape[1:]) for k, b in zip(st["keys"], bufs)}

    def adamw(reduced, l, prev):
        return {k: _adamw_layer(big_w[k], g, big_m[k], big_v[k], l, None if prev is None else prev[k])
                for k, g in reduced.items()}

    gsm = [dict() for _ in range(DEPTH)]
    dbias = jnp.zeros((8, BLK, 2 * BLK), F32)
    dh, gw1, gs = _bwd_ffn(dh, s1, w1, small, 1, 2)
    gsm[1].update(gs)
    dh, gw, gs, dbias = _bwd_mix(dh, s1, w1, small, 1, bias, dbias)
    gw1.update(gw)
    gsm[1].update(gs)
    dh, gw, gs = _bwd_ffn(dh, s1, w1, small, 1, 1)
    gw1.update(gw)
    gsm[1].update(gs)

    red1 = reduce_begin("1", BIG, gw1)
    dh, gw0, gs = _bwd_ffn(dh, s0, w0, small, 0, 2)
    gsm[0].update(gs)
    reduce_chips(red1)
    dh, gw, gs, dbias = _bwd_mix(dh, s0, w0, small, 0, bias, dbias)
    gw0.update(gw)
    gsm[0].update(gs)
    red0a = reduce_begin("0a", ("ffn2_gu", "ffn2_down", "w_out", "w_in"), gw0)
    reduce_halves(red1)
    dgu = _bwd_ffn_dact(dh, s0, w0, 1)
    reduce_chips(red0a)
    dh, gw, gs = _bwd_ffn_rest(dh, dgu, s0, w0, small, 0, 1)
    gsm[0].update(gs)
    red0b = reduce_begin("0b", ffn1, gw)
    reduced1 = reduce_end(red1)
    stacks = adamw({k: reduced1[k] for k in ffn1}, 1, None)

    gsmall = {k: jnp.stack([gsm[l][k].reshape(-1) for l in range(DEPTH)]) for k in gsm[0]}
    gsmall["rel_bias"] = jnp.transpose(_bias_grad(dbias, buckets)[:, :N_BUCKETS])
    gsmall["norm_final"] = dg_final.reshape(-1)
    small_like = [small[k] for k in SMALL]
    pk = lambda dct: _pack([dct[k] for k in SMALL])
    red = _small_allreduce(_pack([gsmall[k] for k in SMALL] + [loss_row[0, :1]]))
    gs = _unpack(red, small_like + [loss_row[0, :1]])
    loss = gs[-1][0]
    gs = dict(zip(SMALL, gs[:-1]))

    reduce_chips(red0b)
    stacks.update(adamw({k: reduced1[k] for k in mix_in + rest}, 1, None))
    dlt, m2, v2 = _adamw_small(pk(small), pk(gs), pk(small_m), pk(small_v))
    reduce_halves(red0a)
    stacks.update(adamw(reduce_end(red0a), 0, stacks))
    reduce_halves(red0b)
    stacks.update(adamw(reduce_end(red0b), 0, stacks))

    out_g, out_d, out_m, out_v = {}, {}, {}, {}
    for k in BIG:
        out_g[k], out_d[k], out_m[k], out_v[k] = stacks[k]
    for dst, packed in ((out_d, dlt), (out_m, m2), (out_v, v2)):
        dst.update(zip(SMALL, _unpack(packed, small_like)))
    out_g.update(gs)

    order = ("norm_ffn1", "ffn1_gu", "ffn1_down", "norm_mix", "w_in", "sinks", "norm_out_sb", "norm_out_swa", "w_out",
             "norm_ffn2", "ffn2_gu", "ffn2_down", "rel_bias", "norm_final")
    return (loss, dh.reshape(x.shape), *[out_g[k] for k in order], *[out_d[k] for k in order],
            *[out_m[k] for k in order], *[out_v[k] for k in order])
```

```python
import math

import numpy as np
import jax
import jax.numpy as jnp
from jax import lax
from jax.experimental import pallas as pl
from jax.experimental.pallas import tpu as pltpu

F32 = jnp.float32
BF16 = jnp.bfloat16

D_MODEL = 1024
DEPTH = 2
HEAD_DIM = 64
BLK = 128
N_BUCKETS = 32
MAX_DISTANCE = 128
D_FF = 2816
EPS = 1e-6
NEG_INF = -1e30
SB_W = 512
SWA_W = 512
KV_W = 128
IN_W = 2304
SCALE = HEAD_DIM ** -0.5
N_CHIPS = 4
FS = 2 * D_FF // N_CHIPS
LANES = 128
V7X_VMEM_LIMIT = 56 * 2 ** 20
TM = 512
SB_KT = 512
SWA_G = 4

ADAM_LR = 0.001
ADAM_B1 = 0.9
ADAM_B2 = 0.999
ADAM_EPS = 1e-08
ADAM_WD = 0.01
ADAM_STEP = 10

MESH = pl.DeviceIdType.MESH
ANY = pl.BlockSpec(memory_space=pl.ANY)
HBM = pl.BlockSpec(memory_space=pltpu.HBM)
SEM = pl.BlockSpec(memory_space=pltpu.SEMAPHORE)
EFFECT = pltpu.SideEffectType.DATAFLOW_SIDE_EFFECTING


def _params(n_grid):
    return pltpu.CompilerParams(dimension_semantics=("arbitrary",) * n_grid, vmem_limit_bytes=V7X_VMEM_LIMIT)


_PREVIOUS = [None]


def _call(body, *, name, in_specs, out_specs, out_shape, grid=(), num_scalar_prefetch=0, scratch_shapes=(),
          input_output_aliases=None, compiler_params=None, hbm_args=0):
    n_in = len(in_specs)

    def run(*args):
        dep = _PREVIOUS[0]
        if any(dep is a for a in args):
            dep = None
        args = [pltpu.with_memory_space_constraint(a, pltpu.HBM) if i < hbm_args else a for i, a in enumerate(args)]
        specs = list(in_specs) + ([ANY] if dep is not None else [])
        k = num_scalar_prefetch + n_in
        fn = body if dep is None else (lambda *refs: body(*refs[:k], *refs[k + 1:]))
        if num_scalar_prefetch:
            shape = dict(grid_spec=pltpu.PrefetchScalarGridSpec(
                num_scalar_prefetch=num_scalar_prefetch, grid=grid, in_specs=specs, out_specs=out_specs,
                scratch_shapes=scratch_shapes))
        else:
            shape = dict(grid=grid, in_specs=specs, out_specs=out_specs, scratch_shapes=scratch_shapes)
        out = pl.pallas_call(fn, name=name, out_shape=out_shape, input_output_aliases=input_output_aliases or {},
                             compiler_params=compiler_params, **shape)(*args, *([] if dep is None else [dep]))
        _PREVIOUS[0] = jax.tree.leaves(out)[-1]
        return out

    return run


def _dot(a, b):
    return jnp.dot(a, b, preferred_element_type=F32)


def _dot_nt(a, b):
    return lax.dot_general(a, b, (((1,), (1,)), ((), ())), preferred_element_type=F32)


def _dot_tn(a, b):
    return lax.dot_general(a, b, (((0,), (0,)), ((), ())), preferred_element_type=F32)


def _rms_fwd(x, g):
    r = lax.rsqrt(jnp.mean(x * x, axis=-1, keepdims=True) + EPS)
    xh = x * r
    return xh * g, xh, r


def _rms_bwd(dy, xh, r, g):
    u = dy * g
    dx = r * (u - xh * jnp.mean(u * xh, axis=-1, keepdims=True))
    dg = jnp.sum(dy * xh, axis=0, keepdims=True)
    return dx, dg


def _softplus(z):
    neg_abs = lax.bitcast_convert_type(lax.bitcast_convert_type(z, jnp.int32) | jnp.int32(-2 ** 31), F32)
    sp = jnp.maximum(z, 0.0) + jnp.log(1.0 + jnp.exp(neg_abs))
    return sp, z - sp


def _norm_cast(h, g):
    t, w = h.shape

    def body(h_ref, g_ref, n_ref):
        y, _, _ = _rms_fwd(h_ref[...], g_ref[...])
        n_ref[...] = y.astype(BF16)

    return _call(
        body, name="norm_cast", grid=(t // TM,),
        in_specs=[pl.BlockSpec((TM, w), lambda i: (i, 0)), pl.BlockSpec((1, w), lambda i: (0, 0))],
        out_specs=pl.BlockSpec((TM, w), lambda i: (i, 0)),
        out_shape=jax.ShapeDtypeStruct((t, w), BF16), compiler_params=_params(1))(h, g)


def _ffn_gu(n, wgu):
    t, d = n.shape

    def body(n_ref, wg_ref, wu_ref, gu_ref, act_ref):
        x = n_ref[...]
        g = _dot(x, wg_ref[...])
        u = _dot(x, wu_ref[...])
        gu_ref[0] = g.astype(BF16)
        gu_ref[1] = u.astype(BF16)
        act_ref[...] = (g * jax.nn.sigmoid(g) * u).astype(BF16)

    return _call(
        body, name="ffn_gu", grid=(2, t // TM),
        in_specs=[pl.BlockSpec((TM, d), lambda j, i: (i, 0)),
                  pl.BlockSpec((None, d, FS), lambda j, i: (j, 0, 0)),
                  pl.BlockSpec((None, d, FS), lambda j, i: (j + 2, 0, 0))],
        out_specs=[pl.BlockSpec((2, TM, FS), lambda j, i: (0, i, j)), pl.BlockSpec((TM, FS), lambda j, i: (i, j))],
        out_shape=[jax.ShapeDtypeStruct((2, t, D_FF), BF16), jax.ShapeDtypeStruct((t, D_FF), BF16)],
        compiler_params=_params(2))(n, wgu, wgu)


def _down_res(act, wdn, h):
    t, f = act.shape
    d = h.shape[1]

    def body(a_ref, w_ref, h_ref, o_ref):
        o_ref[...] = h_ref[...] + 0.5 * _dot(a_ref[...], w_ref[...])

    return _call(
        body, name="down_res", grid=(t // TM,),
        in_specs=[pl.BlockSpec((TM, f), lambda i: (i, 0)), pl.BlockSpec((f, d), lambda i: (0, 0)),
                  pl.BlockSpec((TM, d), lambda i: (i, 0))],
        out_specs=pl.BlockSpec((TM, d), lambda i: (i, 0)),
        out_shape=jax.ShapeDtypeStruct((t, d), F32), compiler_params=_params(1))(act, wdn, h)


def _proj(n, w_in):
    t, d = n.shape
    w = w_in.shape[1]

    def body(n_ref, w_ref, o_ref):
        o_ref[...] = _dot(n_ref[...], w_ref[...]).astype(BF16)

    return _call(
        body, name="proj", grid=(t // TM,),
        in_specs=[pl.BlockSpec((TM, d), lambda i: (i, 0)), pl.BlockSpec((d, w), lambda i: (0, 0))],
        out_specs=pl.BlockSpec((TM, w), lambda i: (i, 0)),
        out_shape=jax.ShapeDtypeStruct((t, w), BF16), compiler_params=_params(1))(n, w_in)


def _out_res(o_sb, o_sw, g_sb, g_sw, w_out, h):
    t, d = h.shape

    def body(a_ref, b_ref, ga_ref, gb_ref, w_ref, h_ref, o_ref, mix_ref):
        ya, _, _ = _rms_fwd(a_ref[...], ga_ref[...])
        yb, _, _ = _rms_fwd(b_ref[...], gb_ref[...])
        mixed = jnp.concatenate([ya.astype(BF16), yb.astype(BF16)], axis=1)
        mix_ref[...] = mixed
        o_ref[...] = h_ref[...] + _dot(mixed, w_ref[...])

    return _call(
        body, name="out_res", grid=(t // TM,),
        in_specs=[pl.BlockSpec((TM, SB_W), lambda i: (i, 0)), pl.BlockSpec((TM, SWA_W), lambda i: (i, 0)),
                  pl.BlockSpec((1, SB_W), lambda i: (0, 0)), pl.BlockSpec((1, SWA_W), lambda i: (0, 0)),
                  pl.BlockSpec((d, d), lambda i: (0, 0)), pl.BlockSpec((TM, d), lambda i: (i, 0))],
        out_specs=[pl.BlockSpec((TM, d), lambda i: (i, 0)), pl.BlockSpec((TM, d), lambda i: (i, 0))],
        out_shape=[jax.ShapeDtypeStruct((t, d), F32), jax.ShapeDtypeStruct((t, d), BF16)],
        compiler_params=_params(1))(o_sb, o_sw, g_sb, g_sw, w_out, h)


def _loss_head(h, g, tgt):
    t, d = h.shape

    def body(h_ref, g_ref, t_ref, dh_ref, dg_ref, loss_ref):
        @pl.when(pl.program_id(0) == 0)
        def _():
            dg_ref[...] = jnp.zeros_like(dg_ref)
            loss_ref[...] = jnp.zeros_like(loss_ref)

        gg = g_ref[...]
        y, xh, r = _rms_fwd(h_ref[...], gg)
        err = y - t_ref[...]
        part = 0.5 * jnp.sum(jnp.sum(err * err, axis=1, keepdims=True) / d, axis=0, keepdims=True)
        loss_ref[...] += jnp.broadcast_to(part, loss_ref.shape)
        dx, dg = _rms_bwd(err / d, xh, r, gg)
        dh_ref[...] = dx
        dg_ref[...] += dg

    return _call(
        body, name="loss_head", grid=(t // TM,),
        in_specs=[pl.BlockSpec((TM, d), lambda i: (i, 0)), pl.BlockSpec((1, d), lambda i: (0, 0)),
                  pl.BlockSpec((TM, d), lambda i: (i, 0))],
        out_specs=[pl.BlockSpec((TM, d), lambda i: (i, 0)), pl.BlockSpec((1, d), lambda i: (0, 0)),
                   pl.BlockSpec((1, LANES), lambda i: (0, 0))],
        out_shape=[jax.ShapeDtypeStruct((t, d), F32), jax.ShapeDtypeStruct((1, d), F32),
                   jax.ShapeDtypeStruct((1, LANES), F32)],
        compiler_params=_params(1))(h, g, tgt)


def _ffn_dact(dh, wdn, gu):
    t, d = dh.shape

    def body(dh_ref, w_ref, gu_ref, o_ref):
        da = 0.5 * _dot_nt(dh_ref[...].astype(BF16), w_ref[...])
        g = gu_ref[0].astype(F32)
        u = gu_ref[1].astype(F32)
        sig = jax.nn.sigmoid(g)
        silu = g * sig
        o_ref[0] = (da * u * (sig * (1.0 + g * (1.0 - sig)))).astype(BF16)
        o_ref[1] = (da * silu).astype(BF16)

    return _call(
        body, name="ffn_dact", grid=(2, t // TM),
        in_specs=[pl.BlockSpec((TM, d), lambda j, i: (i, 0)), pl.BlockSpec((FS, d), lambda j, i: (j, 0)),
                  pl.BlockSpec((2, TM, FS), lambda j, i: (0, i, j))],
        out_specs=pl.BlockSpec((2, TM, FS), lambda j, i: (0, i, j)),
        out_shape=jax.ShapeDtypeStruct((2, t, D_FF), BF16), compiler_params=_params(2))(dh, wdn, gu)


def _dn_norm_bwd(a, a_spec, w, w_spec, nk, dh, h_in, g):
    t, d = dh.shape

    def body(a_ref, w_ref, dh_ref, h_ref, g_ref, o_ref, dg_ref, acc_ref):
        i, k = pl.program_id(0), pl.program_id(1)

        @pl.when(k == 0)
        def _():
            acc_ref[...] = jnp.zeros_like(acc_ref)

        acc_ref[...] += _dot_nt(a_ref[...], w_ref[...])

        @pl.when(k == nk - 1)
        def _():
            gg = g_ref[...]
            _, xh, r = _rms_fwd(h_ref[...], gg)
            dx, dg = _rms_bwd(acc_ref[...], xh, r, gg)
            o_ref[...] = dh_ref[...] + dx

            @pl.when(i == 0)
            def _():
                dg_ref[...] = dg

            @pl.when(i > 0)
            def _():
                dg_ref[...] += dg

    row = pl.BlockSpec((TM, d), lambda i, k: (i, 0))
    return _call(
        body, name="dn_norm_bwd", grid=(t // TM, nk),
        in_specs=[a_spec, w_spec, row, row, pl.BlockSpec((1, d), lambda i, k: (0, 0))],
        out_specs=[row, pl.BlockSpec((1, d), lambda i, k: (0, 0))],
        out_shape=[jax.ShapeDtypeStruct((t, d), F32), jax.ShapeDtypeStruct((1, d), F32)],
        scratch_shapes=[pltpu.VMEM((TM, d), F32)], compiler_params=_params(2))(a, w, dh, h_in, g)


def _ffn_dn(dgu, wgu, dh, h_in, g):
    d = dh.shape[1]
    return _dn_norm_bwd(
        dgu, pl.BlockSpec((None, TM, FS), lambda i, k: (k // 2, i, k % 2)),
        wgu, pl.BlockSpec((None, d, FS), lambda i, k: (k, 0, 0)), N_CHIPS, dh, h_in, g)


def _mix_dn(dproj, w_in, dh, h_in, g):
    d = dh.shape[1]
    w = dproj.shape[1]
    return _dn_norm_bwd(
        dproj, pl.BlockSpec((TM, w), lambda i, k: (i, 0)),
        w_in, pl.BlockSpec((d, w), lambda i, k: (0, 0)), 1, dh, h_in, g)


def _dmixed(dh, w_out, o_sb, o_sw, g_sb, g_sw):
    t, d = dh.shape

    def body(dh_ref, w_ref, a_ref, b_ref, ga_ref, gb_ref, o_ref, dga_ref, dgb_ref):
        i = pl.program_id(0)
        dm = _dot_nt(dh_ref[...].astype(BF16), w_ref[...])
        _, xa, ra = _rms_fwd(a_ref[...], ga_ref[...])
        _, xb, rb = _rms_fwd(b_ref[...], gb_ref[...])
        da, dga = _rms_bwd(dm[:, :SB_W], xa, ra, ga_ref[...])
        db, dgb = _rms_bwd(dm[:, SB_W:], xb, rb, gb_ref[...])
        o_ref[...] = jnp.concatenate([da.astype(BF16), db.astype(BF16)], axis=1)

        @pl.when(i == 0)
        def _():
            dga_ref[...] = dga
            dgb_ref[...] = dgb

        @pl.when(i > 0)
        def _():
            dga_ref[...] += dga
            dgb_ref[...] += dgb

    return _call(
        body, name="dmixed", grid=(t // TM,),
        in_specs=[pl.BlockSpec((TM, d), lambda i: (i, 0)), pl.BlockSpec((d, d), lambda i: (0, 0)),
                  pl.BlockSpec((TM, SB_W), lambda i: (i, 0)), pl.BlockSpec((TM, SWA_W), lambda i: (i, 0)),
                  pl.BlockSpec((1, SB_W), lambda i: (0, 0)), pl.BlockSpec((1, SWA_W), lambda i: (0, 0))],
        out_specs=[pl.BlockSpec((TM, d), lambda i: (i, 0)), pl.BlockSpec((1, SB_W), lambda i: (0, 0)),
                   pl.BlockSpec((1, SWA_W), lambda i: (0, 0))],
        out_shape=[jax.ShapeDtypeStruct((t, d), BF16), jax.ShapeDtypeStruct((1, SB_W), F32),
                   jax.ShapeDtypeStruct((1, SWA_W), F32)],
        compiler_params=_params(1))(dh, w_out, o_sb, o_sw, g_sb, g_sw)


def _wgrad(name, a, a_spec, b, b_spec, grid, out_shape, out_spec, scale):
    def body(a_ref, b_ref, o_ref):
        r = _dot_tn(a_ref[...], b_ref[...].astype(BF16))
        o_ref[...] = r if scale == 1.0 else scale * r

    return _call(
        body, name=name, grid=grid, in_specs=[a_spec, b_spec], out_specs=out_spec,
        out_shape=jax.ShapeDtypeStruct(out_shape, F32), compiler_params=_params(len(grid)))(a, b)


def _wgrad_gu(n, dgu):
    t, d = n.shape
    return _wgrad(
        "wgrad_gu", n, pl.BlockSpec((t, TM), lambda s, r: (0, r)),
        dgu, pl.BlockSpec((None, t, FS), lambda s, r: (s // 2, 0, s % 2)), (N_CHIPS, d // TM),
        (N_CHIPS, d, FS), pl.BlockSpec((None, TM, FS), lambda s, r: (s, r, 0)), 1.0)


def _wgrad_down(act, dh):
    t, d = dh.shape
    return _wgrad(
        "wgrad_down", act, pl.BlockSpec((t, FS), lambda s, r: (0, s)), dh, pl.BlockSpec((t, TM), lambda s, r: (0, r)),
        (2, d // TM), (D_FF, d), pl.BlockSpec((FS, TM), lambda s, r: (s, r)), 0.5)


def _wgrad_out(mixed, dh):
    t, d = dh.shape
    return _wgrad(
        "wgrad_out", mixed, pl.BlockSpec((t, TM), lambda s: (0, s)), dh, pl.BlockSpec((t, d), lambda s: (0, 0)),
        (d // TM,), (d, d), pl.BlockSpec((TM, d), lambda s: (s, 0)), 1.0)


def _wgrad_in(n, dproj):
    t, d = n.shape
    w = dproj.shape[1]
    tw = w // 3
    return _wgrad(
        "wgrad_in", n, pl.BlockSpec((t, d), lambda s: (0, 0)), dproj, pl.BlockSpec((t, tw), lambda s: (0, s)),
        (3,), (d, w), pl.BlockSpec((d, tw), lambda s: (0, s)), 1.0)


def _tri(rel):
    row = lax.broadcasted_iota(jnp.int32, (BLK, BLK), 0)
    col = lax.broadcasted_iota(jnp.int32, (BLK, BLK), 1)
    m = rel(row, col).astype(BF16)
    return jnp.concatenate([m, m], axis=0)


def _scan_dot(x, tri2):
    hi = x.astype(BF16)
    lo = (x - hi.astype(F32)).astype(BF16)
    return _dot(jnp.concatenate([hi, lo], axis=1), tri2)


def _head_masks():
    lane = lax.broadcasted_iota(jnp.int32, (1, LANES), 1)
    return [lane < HEAD_DIM, lane >= HEAD_DIM]


def _sb_dcol():
    dcol = lax.broadcasted_iota(jnp.int32, (BLK, SB_KT), 1) - lax.broadcasted_iota(jnp.int32, (BLK, SB_KT), 0)
    return jnp.concatenate([dcol, dcol], axis=0)


def _sb_fwd(proj):
    t = proj.shape[0]
    nq = t // BLK
    nb = SB_KT // BLK

    def body(q_ref, k_ref, v_ref, o_ref, tot_ref):
        hm = _head_masks()
        dcol = _sb_dcol()
        after = _tri(lambda r, c: r > c)

        def tile(qh, kt, carry, acc, limit):
            ks = pl.ds(pl.multiple_of(kt * SB_KT, SB_KT), SB_KT)
            z = _dot_nt(qh, k_ref[ks, :])
            sp, zs = _softplus(z)
            valid = None if limit is None else dcol < limit
            spm = sp if valid is None else jnp.where(valid, sp, 0.0)
            sufs = [None] * nb
            for b in reversed(range(nb)):
                blk = spm[:, b * BLK:(b + 1) * BLK]
                sufs[b] = carry + _scan_dot(blk, after)
                carry = carry + jnp.sum(blk, axis=1, keepdims=True)
            w = jnp.exp(zs - jnp.concatenate(sufs, axis=1))
            if valid is not None:
                w = jnp.where(valid, w, 0.0)
            return carry, acc + _dot(w.astype(BF16), v_ref[ks, :])

        def qblock(qi, _):
            qs = pl.ds(pl.multiple_of(qi * BLK, BLK), BLK)
            q = q_ref[qs, :] * SCALE
            kd = qi // nb
            limit = (qi - kd * nb) * BLK
            qh = jnp.concatenate([jnp.where(m, q, jnp.zeros_like(q)) for m in hm], axis=0)
            c0 = tile(qh, kd, jnp.zeros((2 * BLK, 1), F32), jnp.zeros((2 * BLK, LANES), F32), limit)
            carry, acc = lax.fori_loop(0, kd, lambda n, c: tile(qh, kd - 1 - n, c[0], c[1], None), c0)
            o_ref[qs, :] = jnp.where(hm[0], acc[:BLK], acc[BLK:])
            for h in range(2):
                tot_ref[h, qs, :] = jnp.broadcast_to(carry[h * BLK:(h + 1) * BLK], (BLK, LANES))
            return 0

        lax.fori_loop(0, nq, qblock, 0)

    col_blk = lambda off: pl.BlockSpec((t, LANES), lambda p: (0, off + p))
    return _call(
        body, name="sb_fwd", grid=(4,), in_specs=[col_blk(0), col_blk(4), col_blk(8)],
        out_specs=[pl.BlockSpec((t, LANES), lambda p: (0, p)), pl.BlockSpec((2, t, LANES), lambda p: (p, 0, 0))],
        out_shape=[jax.ShapeDtypeStruct((t, SB_W), F32), jax.ShapeDtypeStruct((8, t, LANES), F32)],
        compiler_params=_params(1))(proj, proj, proj)


def _sb_bwd(proj, d_o, tot):
    t = proj.shape[0]
    nq = t // BLK
    nb = SB_KT // BLK

    def body(q_ref, k_ref, v_ref, do_ref, tot_ref, dq_ref, dk_ref, dv_ref, dk_acc, dv_acc):
        hm = _head_masks()
        dcol = _sb_dcol()
        before = _tri(lambda r, c: r < c)
        upto = _tri(lambda r, c: r <= c)
        dk_acc[...] = jnp.zeros_like(dk_acc)
        dv_acc[...] = jnp.zeros_like(dv_acc)

        def tile(qh, doh, tt, kt, pre, ecum, dq, limit):
            ks = pl.ds(pl.multiple_of(kt * SB_KT, SB_KT), SB_KT)
            k = k_ref[ks, :]
            v = v_ref[ks, :]
            z = _dot_nt(qh, k)
            sp, zs = _softplus(z)
            valid = None if limit is None else dcol < limit
            spm = sp if valid is None else jnp.where(valid, sp, 0.0)
            pres = []
            for b in range(nb):
                blk = spm[:, b * BLK:(b + 1) * BLK]
                pres.append(pre + _scan_dot(blk, before))
                pre = pre + jnp.sum(blk, axis=1, keepdims=True)
            logw = z - (tt - jnp.concatenate(pres, axis=1))
            if valid is not None:
                logw = jnp.minimum(logw, 0.0)
            w = jnp.exp(logw)
            if valid is not None:
                w = jnp.where(valid, w, 0.0)
            e = w * _dot_nt(doh, v)
            incs = []
            for b in range(nb):
                blk = e[:, b * BLK:(b + 1) * BLK]
                incs.append(ecum + _scan_dot(blk, upto))
                ecum = ecum + jnp.sum(blk, axis=1, keepdims=True)
            dz = e - jnp.exp(zs) * jnp.concatenate(incs, axis=1)
            if valid is not None:
                dz = jnp.where(valid, dz, 0.0)
            dzb = dz.astype(BF16)
            dk_acc[ks, :] += _dot_tn(dzb, qh)
            dv_acc[ks, :] += _dot_tn(w.astype(BF16), doh)
            return pre, ecum, dq + _dot(dzb, k)

        def qblock(qi, _):
            qs = pl.ds(pl.multiple_of(qi * BLK, BLK), BLK)
            q = q_ref[qs, :] * SCALE
            do = do_ref[qs, :]
            kd = qi // nb
            limit = (qi - kd * nb) * BLK
            qh = jnp.concatenate([jnp.where(m, q, jnp.zeros_like(q)) for m in hm], axis=0)
            doh = jnp.concatenate([jnp.where(m, do, jnp.zeros_like(do)) for m in hm], axis=0)
            tt = jnp.concatenate([tot_ref[h, qs, 0:1] for h in range(2)], axis=0)
            c0 = (jnp.zeros((2 * BLK, 1), F32), jnp.zeros((2 * BLK, 1), F32), jnp.zeros((2 * BLK, LANES), F32))
            c = lax.fori_loop(0, kd, lambda kt, c: tile(qh, doh, tt, kt, c[0], c[1], c[2], None), c0)
            dq = tile(qh, doh, tt, kd, c[0], c[1], c[2], limit)[2]
            dq_ref[qs, :] = (jnp.where(hm[0], dq[:BLK], dq[BLK:]) * SCALE).astype(BF16)
            return 0

        lax.fori_loop(0, nq, qblock, 0)
        dk_ref[...] = dk_acc[...].astype(BF16)
        dv_ref[...] = dv_acc[...].astype(BF16)

    col_blk = lambda off: pl.BlockSpec((t, LANES), lambda p: (0, off + p))
    out = jax.ShapeDtypeStruct((t, SB_W), BF16)
    return _call(
        body, name="sb_bwd", grid=(4,),
        in_specs=[col_blk(0), col_blk(4), col_blk(8), col_blk(0), pl.BlockSpec((2, t, LANES), lambda p: (p, 0, 0))],
        out_specs=[col_blk(0), col_blk(0), col_blk(0)], out_shape=[out, out, out],
        scratch_shapes=[pltpu.VMEM((t, LANES), F32), pltpu.VMEM((t, LANES), F32)],
        compiler_params=_params(1))(proj, proj, proj, d_o, tot)


def _bucket_table():
    a = np.arange(BLK)[:, None]
    c = np.arange(2 * BLK)[None, :]
    dist = np.maximum(BLK + a - c, 0)
    max_exact = N_BUCKETS // 2
    dd = np.maximum(dist, 1).astype(np.float32)
    large = max_exact + (np.log(dd / max_exact) / math.log(MAX_DISTANCE / max_exact)
                         * (N_BUCKETS - max_exact)).astype(np.int32)
    large = np.minimum(large, N_BUCKETS - 1)
    return np.where(dist < max_exact, dist, large).astype(np.int32)


def _swa_band_masks():
    row = lax.broadcasted_iota(jnp.int32, (SWA_G * BLK, 2 * BLK), 0) & (BLK - 1)
    col = lax.broadcasted_iota(jnp.int32, (SWA_G * BLK, 2 * BLK), 1)
    own = lax.broadcasted_iota(jnp.int32, (SWA_G * BLK, BLK), 1) <= (
        lax.broadcasted_iota(jnp.int32, (SWA_G * BLK, BLK), 0) & (BLK - 1))
    return (col > row) & ((col < BLK) | (col - BLK <= row)), own


def _swa_stack(ref, qs, kvh, kvmask, scale):
    parts = []
    for g in range(SWA_G):
        hq = SWA_G * kvh + g
        x = ref[qs, (hq // 2) * LANES:(hq // 2 + 1) * LANES].astype(F32)
        if hq % 2 != kvh:
            x = pltpu.roll(x, HEAD_DIM, 1)
        parts.append(jnp.where(kvmask, x * scale, 0.0).astype(BF16))
    return jnp.concatenate(parts, axis=0)


def _swa_unstack(x4, kvh, hm):
    heads = []
    for g in range(SWA_G):
        x = x4[g * BLK:(g + 1) * BLK]
        heads.append(pltpu.roll(x, HEAD_DIM, 1) if g % 2 != kvh else x)
    return [jnp.where(hm[0], heads[0], heads[1]), jnp.where(hm[0], heads[2], heads[3])]


def _swa_scores(q4, kb, bias_ref, kvh, mask, cols):
    bias4 = jnp.concatenate([bias_ref[SWA_G * kvh + g, :, cols] for g in range(SWA_G)], axis=0)
    return jnp.where(mask, _dot_nt(q4, kb) + bias4, NEG_INF)


def _swa_sinks(sink_ref, kvh):
    return jnp.concatenate([jnp.broadcast_to(sink_ref[SWA_G * kvh + g:SWA_G * kvh + g + 1, 0:1], (BLK, 1))
                            for g in range(SWA_G)], axis=0)


def _swa_fwd(proj, bias, sinks_b):
    t = proj.shape[0]
    nq = t // BLK

    def body(q_ref, k_ref, v_ref, bias_ref, sink_ref, o_ref, lse_ref):
        hm = _head_masks()
        band, own = _swa_band_masks()

        def qblock(i, kvh, prev):
            qs = pl.ds(pl.multiple_of(i * BLK, BLK), BLK)
            if prev:
                ks, mask, cols = pl.ds(pl.multiple_of((i - 1) * BLK, BLK), 2 * BLK), band, slice(None)
            else:
                ks, mask, cols = qs, own, slice(BLK, None)
            q4 = _swa_stack(q_ref, qs, kvh, hm[kvh], SCALE)
            sink4 = _swa_sinks(sink_ref, kvh)
            s = _swa_scores(q4, k_ref[ks, :], bias_ref, kvh, mask, cols)
            m = jnp.maximum(jnp.max(s, axis=1, keepdims=True), sink4)
            p = jnp.exp(s - m)
            den = jnp.sum(p, axis=1, keepdims=True) + jnp.exp(sink4 - m)
            o4 = _dot((p * (1.0 / den)).astype(BF16), v_ref[ks, :])
            lse4 = m + jnp.log(den)
            for g in range(SWA_G):
                lse_ref[SWA_G * kvh + g, qs, :] = jnp.broadcast_to(lse4[g * BLK:(g + 1) * BLK], (BLK, LANES))
            for pp, o in enumerate(_swa_unstack(o4, kvh, hm)):
                o_ref[qs, (2 * kvh + pp) * LANES:(2 * kvh + pp + 1) * LANES] = o

        for kvh in range(2):
            qblock(0, kvh, False)

            def step(i, _):
                qblock(i, kvh, True)
                return 0

            lax.fori_loop(1, nq, step, 0)

    return _call(
        body, name="swa_fwd", grid=(1,),
        in_specs=[pl.BlockSpec((t, SWA_W), lambda i: (0, 3)), pl.BlockSpec((t, KV_W), lambda i: (0, 16)),
                  pl.BlockSpec((t, KV_W), lambda i: (0, 17)), pl.BlockSpec((8, BLK, 2 * BLK), lambda i: (0, 0, 0)),
                  pl.BlockSpec((8, LANES), lambda i: (0, 0))],
        out_specs=[pl.BlockSpec((t, SWA_W), lambda i: (0, 0)), pl.BlockSpec((8, t, LANES), lambda i: (0, 0, 0))],
        out_shape=[jax.ShapeDtypeStruct((t, SWA_W), F32), jax.ShapeDtypeStruct((8, t, LANES), F32)],
        compiler_params=_params(1))(proj, proj, proj, bias, sinks_b)


def _swa_bwd(proj, d_o, lse, bias, sinks_b, dbias_in):
    t = proj.shape[0]
    nq = t // BLK

    def body(q_ref, k_ref, v_ref, do_ref, lse_ref, bias_ref, sink_ref, dbi_ref,
             dq_ref, dk_ref, dv_ref, dsink_ref, dbias_ref, dk_acc, dv_acc):
        hm = _head_masks()
        band, own = _swa_band_masks()
        dk_acc[...] = jnp.zeros_like(dk_acc)
        dv_acc[...] = jnp.zeros_like(dv_acc)
        dbias_ref[...] = dbi_ref[...]

        def qblock(i, kvh, prev, dsink4):
            qs = pl.ds(pl.multiple_of(i * BLK, BLK), BLK)
            if prev:
                ks, mask, cols = pl.ds(pl.multiple_of((i - 1) * BLK, BLK), 2 * BLK), band, slice(None)
            else:
                ks, mask, cols = qs, own, slice(BLK, None)
            q4 = _swa_stack(q_ref, qs, kvh, hm[kvh], SCALE)
            do4 = _swa_stack(do_ref, qs, kvh, hm[kvh], 1.0)
            sink4 = _swa_sinks(sink_ref, kvh)
            lse4 = jnp.concatenate([lse_ref[SWA_G * kvh + g, qs, 0:1] for g in range(SWA_G)], axis=0)
            kb = k_ref[ks, :]
            p = jnp.exp(_swa_scores(q4, kb, bias_ref, kvh, mask, cols) - lse4)
            dp = _dot_nt(do4, v_ref[ks, :])
            delta = jnp.sum(p * dp, axis=1, keepdims=True)
            ds = p * (dp - delta)
            for g in range(SWA_G):
                dbias_ref[SWA_G * kvh + g, :, cols] += ds[g * BLK:(g + 1) * BLK]
            dsb = ds.astype(BF16)
            dk_acc[ks, :] += _dot_tn(dsb, q4)
            dv_acc[ks, :] += _dot_tn(p.astype(BF16), do4)
            for pp, dq in enumerate(_swa_unstack(_dot(dsb, kb) * SCALE, kvh, hm)):
                dq_ref[qs, (2 * kvh + pp) * LANES:(2 * kvh + pp + 1) * LANES] = dq.astype(BF16)
            return dsink4 - jnp.exp(sink4 - lse4) * delta

        for kvh in range(2):
            ds0 = qblock(0, kvh, False, jnp.zeros((SWA_G * BLK, 1), F32))
            ds4 = lax.fori_loop(1, nq, lambda i, c: qblock(i, kvh, True, c), ds0)
            for g in range(SWA_G):
                hq = SWA_G * kvh + g
                dsink_ref[hq:hq + 1, :] = jnp.broadcast_to(
                    jnp.sum(ds4[g * BLK:(g + 1) * BLK], axis=0, keepdims=True), (1, LANES))

        dk_ref[...] = dk_acc[...].astype(BF16)
        dv_ref[...] = dv_acc[...].astype(BF16)

    full3 = pl.BlockSpec((8, BLK, 2 * BLK), lambda i: (0, 0, 0))
    kv = jax.ShapeDtypeStruct((t, KV_W), BF16)
    return _call(
        body, name="swa_bwd", grid=(1,),
        in_specs=[pl.BlockSpec((t, SWA_W), lambda i: (0, 3)), pl.BlockSpec((t, KV_W), lambda i: (0, 16)),
                  pl.BlockSpec((t, KV_W), lambda i: (0, 17)), pl.BlockSpec((t, SWA_W), lambda i: (0, 1)),
                  pl.BlockSpec((8, t, LANES), lambda i: (0, 0, 0)), full3, pl.BlockSpec((8, LANES), lambda i: (0, 0)),
                  full3],
        out_specs=[pl.BlockSpec((t, SWA_W), lambda i: (0, 0)), pl.BlockSpec((t, KV_W), lambda i: (0, 0)),
                   pl.BlockSpec((t, KV_W), lambda i: (0, 0)), pl.BlockSpec((8, LANES), lambda i: (0, 0)), full3],
        out_shape=[jax.ShapeDtypeStruct((t, SWA_W), BF16), kv, kv, jax.ShapeDtypeStruct((8, LANES), F32),
                   jax.ShapeDtypeStruct((8, BLK, 2 * BLK), F32)],
        scratch_shapes=[pltpu.VMEM((t, KV_W), F32), pltpu.VMEM((t, KV_W), F32)],
        compiler_params=_params(1))(proj, proj, proj, d_o, lse, bias, sinks_b, dbias_in)


def _bias_table(rel_bias, buckets):
    def body(rb_ref, b_ref, o_ref):
        bk = b_ref[...]
        for h in range(8):
            acc = jnp.zeros((BLK, 2 * BLK), F32)
            for b in range(N_BUCKETS):
                acc = jnp.where(bk == b, rb_ref[b, h], acc)
            o_ref[h] = acc

    return _call(
        body, name="bias_table", grid=(1,),
        in_specs=[pl.BlockSpec(memory_space=pltpu.SMEM), pl.BlockSpec((BLK, 2 * BLK), lambda i: (0, 0))],
        out_specs=pl.BlockSpec((8, BLK, 2 * BLK), lambda i: (0, 0, 0)),
        out_shape=jax.ShapeDtypeStruct((8, BLK, 2 * BLK), F32), compiler_params=_params(1))(rel_bias, buckets)


def _bias_grad(dbias, buckets):
    def body(d_ref, b_ref, o_ref):
        lane = lax.broadcasted_iota(jnp.int32, (1, LANES), 1)
        bk = b_ref[...]
        for h in range(8):
            d = d_ref[h]
            acc = jnp.zeros((1, LANES), F32)
            for b in range(N_BUCKETS):
                s = jnp.sum(jnp.sum(jnp.where(bk == b, d, 0.0), axis=0, keepdims=True), axis=1, keepdims=True)
                acc = acc + jnp.where(lane == b, s, 0.0)
            o_ref[h:h + 1, :] = acc

    return _call(
        body, name="bias_grad", grid=(1,),
        in_specs=[pl.BlockSpec((8, BLK, 2 * BLK), lambda i: (0, 0, 0)), pl.BlockSpec((BLK, 2 * BLK), lambda i: (0, 0))],
        out_specs=pl.BlockSpec((8, LANES), lambda i: (0, 0)),
        out_shape=jax.ShapeDtypeStruct((8, LANES), F32), compiler_params=_params(1))(dbias, buckets)


def _row(a):
    return a.reshape(1, -1)


def _fwd_ffn1(h, w, small, l):
    s = {"h0": h}
    s["n1"] = _norm_cast(h, _row(small["norm_ffn1"][l]))
    s["gu1"], s["act1"] = _ffn_gu(s["n1"], w["ffn1_gu"])
    s["h1"] = _down_res(s["act1"], w["ffn1_down"], h)
    s["nm"] = _norm_cast(s["h1"], _row(small["norm_mix"][l]))
    return s


def _fwd_proj_sb(s, w):
    s["proj"] = _proj(s["nm"], w["w_in"])
    s["o_sb"], s["tot"] = _sb_fwd(s["proj"])


def _fwd_swa(s, small, l, bias):
    s["sinks_b"] = jnp.broadcast_to(small["sinks"][l][:, None], (8, LANES))
    s["o_sw"], s["lse"] = _swa_fwd(s["proj"], bias, s["sinks_b"])


def _fwd_out_ffn2(s, w, small, l):
    s["h2"], s["mixed"] = _out_res(s["o_sb"], s["o_sw"], _row(small["norm_out_sb"][l]), _row(small["norm_out_swa"][l]),
                                  w["w_out"], s["h1"])
    s["n2"] = _norm_cast(s["h2"], _row(small["norm_ffn2"][l]))
    s["gu2"], s["act2"] = _ffn_gu(s["n2"], w["ffn2_gu"])
    return _down_res(s["act2"], w["ffn2_down"], s["h2"])


def _bwd_ffn_dact(dh, s, w, which):
    return _ffn_dact(dh, w[f"ffn{which}_down"], s[f"gu{which}"])


def _bwd_ffn_rest(dh, dgu, s, w, small, l, which):
    h_in, norm = (s["h0"], "norm_ffn1") if which == 1 else (s["h2"], "norm_ffn2")
    g_down = _wgrad_down(s[f"act{which}"], dh)
    g_gu = _wgrad_gu(s[f"n{which}"], dgu)
    dh, dg = _ffn_dn(dgu, w[f"ffn{which}_gu"], dh, h_in, _row(small[norm][l]))
    return dh, {f"ffn{which}_down": g_down, f"ffn{which}_gu": g_gu}, {norm: dg}


def _bwd_ffn(dh, s, w, small, l, which):
    return _bwd_ffn_rest(dh, _bwd_ffn_dact(dh, s, w, which), s, w, small, l, which)


def _bwd_mix(dh, s, w, small, l, bias, dbias):
    g_out = _wgrad_out(s["mixed"], dh)
    d_o, dg_sb, dg_sw = _dmixed(dh, w["w_out"], s["o_sb"], s["o_sw"], _row(small["norm_out_sb"][l]),
                                _row(small["norm_out_swa"][l]))
    dq_sb, dk_sb, dv_sb = _sb_bwd(s["proj"], d_o, s["tot"])
    dq_sw, dk_sw, dv_sw, dsink, dbias = _swa_bwd(s["proj"], d_o, s["lse"], bias, s["sinks_b"], dbias)
    dproj = jnp.concatenate([dq_sb, dk_sb, dv_sb, dq_sw, dk_sw, dv_sw], axis=1)
    g_in = _wgrad_in(s["nm"], dproj)
    dh, dg_mix = _mix_dn(dproj, w["w_in"], dh, s["h1"], _row(small["norm_mix"][l]))
    gs = {"norm_out_sb": dg_sb, "norm_out_swa": dg_sw, "sinks": dsink[:, 0], "norm_mix": dg_mix}
    return dh, {"w_out": g_out, "w_in": g_in}, gs, dbias


def _place():
    x, y, c = lax.axis_index("x"), lax.axis_index("y"), lax.axis_index("c")
    return x, y, c, 2 * x + y


def _chip_core(k, c):
    return (k // 2, k % 2, c)


def _rows_per_block(rows, cols, copies):
    best = 16
    for tr in range(16, rows + 1, 16):
        if rows % tr == 0 and copies * tr * cols * 4 <= 6 * 2 ** 20:
            best = tr
    assert rows % best == 0
    return best


def _place_own(w, l, me1):
    _, rows, cols = w.shape
    tr = _rows_per_block(rows, cols, 1)

    def body(me_ref, w_ref, o_ref):
        o_ref[...] = w_ref[...].astype(BF16)

    return _call(
        body, name="place_own",
        num_scalar_prefetch=1, grid=(rows // tr,),
        in_specs=[pl.BlockSpec((None, tr, cols), lambda r, me: (l, r, 0))],
        out_specs=pl.BlockSpec((None, tr, cols), lambda r, me: (me[0], r, 0)),
        out_shape=jax.ShapeDtypeStruct((N_CHIPS, rows, cols), BF16), compiler_params=_params(1))(me1, w)


def _plan_gather_ici(bufs):
    _, _, c, me = _place()
    return [(b.at[me, c], b.at[me, c], b.at[(me + 3 - j) % N_CHIPS, c], _chip_core((me + 1 + j) % N_CHIPS, c))
            for b in bufs for j in range(3)]


def _plan_gather_d2d(bufs):
    x, y, c, me = _place()
    return [(b.at[(me + 3 - j) % N_CHIPS, c], b.at[(me + 3 - j) % N_CHIPS, c], b.at[(me + 3 - j) % N_CHIPS, 1 - c],
             (x, y, 1 - c)) for b in bufs for j in range(3)]


def _plan_grad_sibling(bufs):
    x, y, c, _ = _place()
    n = len(bufs) // 2
    return [(g.at[:, 1 - c], z, z, (x, y, 1 - c)) for g, z in zip(bufs[:n], bufs[n:])]


def _plan_grad_chips(bufs):
    _, _, c, me = _place()
    n = len(bufs) // 2
    return [(p.at[(me + 1 + j) % N_CHIPS], z.at[j], z.at[j], _chip_core((me + 1 + j) % N_CHIPS, c))
            for p, z in zip(bufs[:n], bufs[n:]) for j in range(3)]


def _plan_grad_halves(bufs):
    x, y, c, _ = _place()
    return [(b.at[c], b.at[c], b.at[1 - c], (x, y, 1 - c)) for b in bufs]


def _remote(src, dst, send_sem, recv_sem, to):
    return pltpu.make_async_remote_copy(src_ref=src, dst_ref=dst, send_sem=send_sem, recv_sem=recv_sem,
                                        device_id=to, device_id_type=MESH)


def _exchange_start(name, plan, bufs, n_copies):
    n = len(bufs)

    def body(*refs):
        ins = refs[:n]
        ssem, rsem = refs[n], refs[n + 1]
        token = refs[-1]
        for i, (src, dst, _, to) in enumerate(plan(ins)):
            _remote(src, dst, ssem.at[i], rsem.at[i], to).start()
        token[...] = jnp.zeros_like(token)

    out = _call(
        body, name=name,
        out_shape=(pltpu.SemaphoreType.DMA((n_copies,)), pltpu.SemaphoreType.DMA((n_copies,)),
                   *[pltpu.HBM(a.shape, a.dtype) for a in bufs], jax.ShapeDtypeStruct((8, LANES), F32)),
        in_specs=[HBM] * n, out_specs=(SEM, SEM, *[HBM] * n, pl.BlockSpec(memory_space=pltpu.VMEM)),
        input_output_aliases={t: 2 + t for t in range(n)}, hbm_args=n,
        compiler_params=pltpu.CompilerParams(has_side_effects=EFFECT),
    )(*bufs)
    return (out[0], out[1]), list(out[2:2 + n])


def _exchange_wait(name, plan, bufs, sems):
    n = len(bufs)

    def body(*refs):
        ins = refs[:n]
        ssem, rsem = refs[n], refs[n + 1]
        for i, (src, dst, land, to) in enumerate(plan(ins)):
            _remote(src, dst, ssem.at[i], rsem.at[i], to).wait_send()
            _remote(land, land, ssem.at[i], rsem.at[i], to).wait_recv()

    return list(_call(
        body, name=name, out_shape=[pltpu.HBM(a.shape, a.dtype) for a in bufs],
        in_specs=[HBM] * n + [SEM, SEM], out_specs=[HBM] * n,
        input_output_aliases={t: t for t in range(n)},
        compiler_params=pltpu.CompilerParams(has_side_effects=EFFECT),
    )(*bufs, sems[0], sems[1]))


def _gather_now(bufs):
    n = len(bufs)
    n_cp = 3 * n

    def body(*refs):
        outs = refs[n:2 * n]
        ici_s, ici_r, d2d_s, d2d_r = refs[2 * n:]
        first = _plan_gather_ici(outs)
        second = _plan_gather_d2d(outs)
        for i, (src, dst, _, to) in enumerate(first):
            _remote(src, dst, ici_s.at[i], ici_r.at[i], to).start()
        for i, (src, dst, _, to) in enumerate(second):
            land = first[i][2]
            _remote(land, land, ici_s.at[i], ici_r.at[i], to).wait_recv()
            _remote(src, dst, d2d_s.at[i], d2d_r.at[i], to).start()
        for i, (_, _, land, to) in enumerate(second):
            _remote(land, land, d2d_s.at[i], d2d_r.at[i], to).wait_recv()
        for i in range(n_cp):
            _remote(first[i][0], first[i][1], ici_s.at[i], ici_r.at[i], first[i][3]).wait_send()
            _remote(second[i][0], second[i][1], d2d_s.at[i], d2d_r.at[i], second[i][3]).wait_send()

    return _call(
        body, name="gather_layer0", in_specs=[ANY] * n, out_specs=[ANY] * n,
        out_shape=[jax.ShapeDtypeStruct(a.shape, a.dtype) for a in bufs],
        input_output_aliases={t: t for t in range(n)},
        scratch_shapes=[pltpu.SemaphoreType.DMA((n_cp,))] * 4,
        compiler_params=pltpu.CompilerParams(vmem_limit_bytes=V7X_VMEM_LIMIT))(*bufs)


def _chip_sum(g, xbuf, cm):
    _, _, r2, cols = g.shape
    tr = _rows_per_block(r2, cols, N_CHIPS)

    def body(cm_ref, g_ref, x_ref, pb_ref, po_ref):
        pb_ref[...] = (g_ref[...] + x_ref[...]).astype(BF16)
        me = cm_ref[1]
        po_ref[...] = g_ref[me] + x_ref[me]

    return _call(
        body, name="grad_chip_sum",
        num_scalar_prefetch=1, grid=(r2 // tr,),
        in_specs=[pl.BlockSpec((N_CHIPS, None, tr, cols), lambda r, cm: (0, cm[0], r, 0)),
                  pl.BlockSpec((N_CHIPS, tr, cols), lambda r, cm: (0, r, 0))],
        out_specs=[pl.BlockSpec((N_CHIPS, tr, cols), lambda r, cm: (0, r, 0)),
                   pl.BlockSpec((tr, cols), lambda r, cm: (r, 0))],
        out_shape=[jax.ShapeDtypeStruct((N_CHIPS, r2, cols), BF16), jax.ShapeDtypeStruct((r2, cols), F32)],
        compiler_params=_params(1))(cm, g, xbuf)


def _total_sum(pown, rbuf, cm):
    r2, cols = pown.shape
    tr = _rows_per_block(r2, cols, 3)

    def body(cm_ref, p_ref, r_ref, o_ref):
        acc = p_ref[...]
        for j in range(3):
            acc = acc + r_ref[j].astype(F32)
        o_ref[...] = acc

    return _call(
        body, name="grad_total_sum",
        num_scalar_prefetch=1, grid=(r2 // tr,),
        in_specs=[pl.BlockSpec((tr, cols), lambda r, cm: (r, 0)),
                  pl.BlockSpec((3, tr, cols), lambda r, cm: (0, r, 0))],
        out_specs=pl.BlockSpec((None, tr, cols), lambda r, cm: (cm[0], r, 0)),
        out_shape=jax.ShapeDtypeStruct((2, r2, cols), F32), compiler_params=_params(1))(cm, pown, rbuf)


def _small_allreduce(v):
    rows = v.shape[0]
    n_dev = 2 * N_CHIPS

    def body(v_ref, o_ref, buf, ssem, rsem):
        x, y, c, _ = _place()
        me = 4 * x + 2 * y + c
        buf[me] = v_ref[...]

        def copy(d, slot, to):
            return _remote(v_ref, buf.at[slot], ssem.at[d - 1], rsem.at[d - 1], (to // 4, (to // 2) % 2, to % 2))

        cps = [copy(d, me, (me + d) % n_dev) for d in range(1, n_dev)]
        for cp in cps:
            cp.start()
        for d in range(1, n_dev):
            copy(d, (me + n_dev - d) % n_dev, me).wait_recv()
        for cp in cps:
            cp.wait_send()
        acc = buf[0]
        for i in range(1, n_dev):
            acc = acc + buf[i]
        o_ref[...] = acc

    vm = pl.BlockSpec(memory_space=pltpu.VMEM)
    return _call(
        body, name="small_allreduce", in_specs=[vm], out_specs=vm,
        out_shape=jax.ShapeDtypeStruct(v.shape, F32),
        scratch_shapes=[pltpu.VMEM((n_dev, rows, LANES), F32), pltpu.SemaphoreType.DMA((n_dev - 1,)),
                        pltpu.SemaphoreType.DMA((n_dev - 1,))],
        compiler_params=pltpu.CompilerParams(vmem_limit_bytes=V7X_VMEM_LIMIT))(v)


def _adamw_math(w, g, m, v):
    m2 = ADAM_B1 * m + (1.0 - ADAM_B1) * g
    v2 = ADAM_B2 * v + (1.0 - ADAM_B2) * (g * g)
    m_hat = m2 / (1.0 - ADAM_B1 ** ADAM_STEP)
    v_hat = v2 / (1.0 - ADAM_B2 ** ADAM_STEP)
    return -ADAM_LR * (m_hat / (jnp.sqrt(v_hat) + ADAM_EPS) + ADAM_WD * w), m2, v2


def _adamw_layer(w, g, m, v, l, prev):
    _, rows, cols = w.shape
    tr = rows
    for cand in range(8, rows + 1, 8):
        if rows % cand == 0 and cand * cols * 4 <= 2 ** 21:
            tr = cand

    def body(w_ref, g_ref, m_ref, v_ref, *outs):
        go_ref, d_ref, m2_ref, v2_ref = outs[-4:]
        g = g_ref[...]
        go_ref[...] = g
        d_ref[...], m2_ref[...], v2_ref[...] = _adamw_math(w_ref[...], g, m_ref[...], v_ref[...])

    stack = pl.BlockSpec((None, tr, cols), lambda i: (l, i, 0))
    ins, specs, alias = [w, g, m, v], [stack, pl.BlockSpec((tr, cols), lambda i: (i, 0)), stack, stack], {}
    if prev is not None:
        ins += list(prev)
        specs += [ANY] * 4
        alias = {4 + i: i for i in range(4)}
    return _call(
        body, name="adamw", grid=(rows // tr,), in_specs=specs, out_specs=[stack] * 4,
        out_shape=[jax.ShapeDtypeStruct(w.shape, F32)] * 4, input_output_aliases=alias,
        compiler_params=_params(1))(*ins)


def _adamw_small(w, g, m, v):
    def body(w_ref, g_ref, m_ref, v_ref, d_ref, m2_ref, v2_ref):
        d_ref[...], m2_ref[...], v2_ref[...] = _adamw_math(w_ref[...], g_ref[...], m_ref[...], v_ref[...])

    spec = pl.BlockSpec(w.shape, lambda i: (0, 0))
    return _call(
        body, name="adamw_small", grid=(1,), in_specs=[spec] * 4, out_specs=[spec] * 3,
        out_shape=[jax.ShapeDtypeStruct(w.shape, F32)] * 3, compiler_params=_params(1))(w, g, m, v)


SMALL = ("norm_ffn1", "norm_mix", "sinks", "norm_out_sb", "norm_out_swa", "norm_ffn2", "rel_bias", "norm_final")
BIG = ("ffn1_gu", "ffn1_down", "w_in", "w_out", "ffn2_gu", "ffn2_down")


def _pack(parts):
    rows = []
    for a in parts:
        a = a.reshape(-1).astype(F32)
        rows.append(jnp.pad(a, (0, -a.shape[0] % LANES)).reshape(-1, LANES))
    out = jnp.concatenate(rows, axis=0)
    return jnp.pad(out, ((0, -out.shape[0] % 8), (0, 0)))


def _unpack(packed, like):
    out, r = [], 0
    for a in like:
        n = math.prod(a.shape)
        nr = -(-n // LANES)
        out.append(packed[r:r + nr].reshape(-1)[:n].reshape(a.shape))
        r += nr
    return out


def _halved(a):
    k, r, cols = a.shape
    return a.reshape(k, 2, r // 2, cols)


def _weight_view(k, buf):
    full = buf.reshape(N_CHIPS, buf.shape[2] * 2, buf.shape[3])
    if k.endswith("_gu"):
        return full
    if k == "w_in":
        return jnp.transpose(full, (1, 0, 2)).reshape(D_MODEL, IN_W)
    return full.reshape(-1, D_MODEL)


def _grad_stack(k, g):
    if k == "w_in":
        g = jnp.transpose(g.reshape(D_MODEL, N_CHIPS, IN_W // N_CHIPS), (1, 0, 2))
    elif not k.endswith("_gu"):
        g = g.reshape(N_CHIPS, g.shape[0] // N_CHIPS, D_MODEL)
    return _halved(g)


def _empty_like_hbm(shape, dtype):
    return pltpu.with_memory_space_constraint(lax.empty(shape, dtype), pltpu.HBM)


def kernel(x, norm_ffn1, w_ffn1_gu, w_ffn1_down, norm_mix, w_in, sinks, norm_out_sb, norm_out_swa, w_out, norm_ffn2, w_ffn2_gu, w_ffn2_down, rel_bias, norm_final, loss_target, m_norm_ffn1, m_w_ffn1_gu, m_w_ffn1_down, m_norm_mix, m_w_in, m_sinks, m_norm_out_sb, m_norm_out_swa, m_w_out, m_norm_ffn2, m_w_ffn2_gu, m_w_ffn2_down, m_rel_bias, m_norm_final, v_norm_ffn1, v_w_ffn1_gu, v_w_ffn1_down, v_norm_mix, v_w_in, v_sinks, v_norm_out_sb, v_norm_out_swa, v_w_out, v_norm_ffn2, v_w_ffn2_gu, v_w_ffn2_down, v_rel_bias, v_norm_final):
    big_w = dict(ffn1_gu=w_ffn1_gu, ffn1_down=w_ffn1_down, w_in=w_in, w_out=w_out, ffn2_gu=w_ffn2_gu, ffn2_down=w_ffn2_down)
    big_m = dict(ffn1_gu=m_w_ffn1_gu, ffn1_down=m_w_ffn1_down, w_in=m_w_in, w_out=m_w_out, ffn2_gu=m_w_ffn2_gu, ffn2_down=m_w_ffn2_down)
    big_v = dict(ffn1_gu=v_w_ffn1_gu, ffn1_down=v_w_ffn1_down, w_in=v_w_in, w_out=v_w_out, ffn2_gu=v_w_ffn2_gu, ffn2_down=v_w_ffn2_down)
    small = dict(norm_ffn1=norm_ffn1, norm_mix=norm_mix, sinks=sinks, norm_out_sb=norm_out_sb, norm_out_swa=norm_out_swa,
                 norm_ffn2=norm_ffn2, rel_bias=rel_bias, norm_final=norm_final)
    small_m = dict(norm_ffn1=m_norm_ffn1, norm_mix=m_norm_mix, sinks=m_sinks, norm_out_sb=m_norm_out_sb,
                   norm_out_swa=m_norm_out_swa, norm_ffn2=m_norm_ffn2, rel_bias=m_rel_bias, norm_final=m_norm_final)
    small_v = dict(norm_ffn1=v_norm_ffn1, norm_mix=v_norm_mix, sinks=v_sinks, norm_out_sb=v_norm_out_sb,
                   norm_out_swa=v_norm_out_swa, norm_ffn2=v_norm_ffn2, rel_bias=v_rel_bias, norm_final=v_norm_final)
    _PREVIOUS[0] = None
    _, _, c, me = _place()
    cm = jnp.stack([c, me]).astype(jnp.int32)
    buckets = jnp.asarray(_bucket_table())
    ffn1, mix_in, rest = ("ffn1_gu", "ffn1_down"), ("w_in",), ("w_out", "ffn2_gu", "ffn2_down")

    def place(l, keys):
        return [_halved(_place_own(big_w[k], l, cm[1:])) for k in keys]

    def views(keys, bufs):
        return {k: _weight_view(k, b) for k, b in zip(keys, bufs)}

    def gather_start(tag, bufs):
        return _exchange_start(f"gather{tag}_ici_start", _plan_gather_ici, bufs, 3 * len(bufs))

    def gather_pass(tag, flight):
        bufs = _exchange_wait(f"gather{tag}_ici_wait", _plan_gather_ici, flight[1], flight[0])
        return _exchange_start(f"gather{tag}_d2d_start", _plan_gather_d2d, bufs, 3 * len(bufs))

    def gather_done(tag, keys, flight):
        return views(keys, _exchange_wait(f"gather{tag}_d2d_wait", _plan_gather_d2d, flight[1], flight[0]))

    w0 = views(ffn1, _gather_now(place(0, ffn1)))
    fly_in0 = gather_start("0b", place(0, mix_in))
    fly_rest0 = gather_start("0c", place(0, rest))
    bias = _bias_table(rel_bias, buckets)
    fly_ffn1 = gather_start("1a", place(1, ffn1))
    fly_rest1 = gather_start("1b", place(1, mix_in + rest))

    s0 = _fwd_ffn1(x[0], w0, small, 0)
    w0.update(gather_done("0b", mix_in, gather_pass("0b", fly_in0)))
    _fwd_proj_sb(s0, w0)
    fly_rest0 = gather_pass("0c", fly_rest0)
    _fwd_swa(s0, small, 0, bias)
    w0.update(gather_done("0c", rest, fly_rest0))
    h = _fwd_out_ffn2(s0, w0, small, 0)
    fly_ffn1 = gather_pass("1a", fly_ffn1)
    fly_rest1 = gather_pass("1b", fly_rest1)
    w1 = gather_done("1a", ffn1, fly_ffn1)
    s1 = _fwd_ffn1(h, w1, small, 1)
    w1.update(gather_done("1b", mix_in + rest, fly_rest1))
    _fwd_proj_sb(s1, w1)
    _fwd_swa(s1, small, 1, bias)
    h = _fwd_out_ffn2(s1, w1, small, 1)
    dh, dg_final, loss_row = _loss_head(h, _row(norm_final), loss_target[0])

    def landing(stacks, lead, dtype):
        return [_empty_like_hbm((lead,) + a.shape[2:], dtype) for a in stacks]

    def reduce_begin(tag, keys, gw):
        stacks = [_grad_stack(k, gw[k]) for k in keys]
        flight = _exchange_start(f"grad{tag}_sibling_start", _plan_grad_sibling,
                                 stacks + landing(stacks, N_CHIPS, F32), len(keys))
        return dict(tag=tag, keys=keys, stacks=stacks, flight=flight)

    def reduce_chips(st):
        n, (sems, bufs) = len(st["keys"]), st["flight"]
        bufs = _exchange_wait(f"grad{st['tag']}_sibling_wait", _plan_grad_sibling, bufs, sems)
        st["sums"] = [_chip_sum(g, z, cm) for g, z in zip(bufs[:n], bufs[n:])]
        st["flight"] = _exchange_start(f"grad{st['tag']}_chips_start", _plan_grad_chips,
                                       [s[0] for s in st["sums"]] + landing(st["stacks"], 3, BF16), 3 * n)

    def reduce_halves(st):
        n, (sems, bufs) = len(st["keys"]), st["flight"]
        bufs = _exchange_wait(f"grad{st['tag']}_chips_wait", _plan_grad_chips, bufs, sems)
        halves = [_total_sum(s[1], z, cm) for s, z in zip(st["sums"], bufs[n:])]
        st["flight"] = _exchange_start(f"grad{st['tag']}_halves_start", _plan_grad_halves, halves, n)

    def reduce_end(st):
        sems, bufs = st["flight"]
        bufs = _exchange_wait(f"grad{st['tag']}_halves_wait", _plan_grad_halves, bufs, sems)
        return {k: b.reshape(big_w[k].shape[1:]) for k, b in zip(st["keys"], bufs)}

    def adamw(reduced, l, prev):
        return {k: _adamw_layer(big_w[k], g, big_m[k], big_v[k], l, None if prev is None else prev[k])
                for k, g in reduced.items()}

    gsm = [dict() for _ in range(DEPTH)]
    dbias = jnp.zeros((8, BLK, 2 * BLK), F32)
    dh, gw1, gs = _bwd_ffn(dh, s1, w1, small, 1, 2)
    gsm[1].update(gs)
    dh, gw, gs, dbias = _bwd_mix(dh, s1, w1, small, 1, bias, dbias)
    gw1.update(gw)
    gsm[1].update(gs)
    dh, gw, gs = _bwd_ffn(dh, s1, w1, small, 1, 1)
    gw1.update(gw)
    gsm[1].update(gs)

    red1 = reduce_begin("1", BIG, gw1)
    dh, gw0, gs = _bwd_ffn(dh, s0, w0, small, 0, 2)
    gsm[0].update(gs)
    reduce_chips(red1)
    dh, gw, gs, dbias = _bwd_mix(dh, s0, w0, small, 0, bias, dbias)
    gw0.update(gw)
    gsm[0].update(gs)
    red0a = reduce_begin("0a", ("ffn2_gu", "ffn2_down", "w_out", "w_in"), gw0)
    reduce_halves(red1)
    dgu = _bwd_ffn_dact(dh, s0, w0, 1)
    reduce_chips(red0a)
    dh, gw, gs = _bwd_ffn_rest(dh, dgu, s0, w0, small, 0, 1)
    gsm[0].update(gs)
    red0b = reduce_begin("0b", ffn1, gw)
    reduced1 = reduce_end(red1)
    stacks = adamw({k: reduced1[k] for k in ffn1}, 1, None)

    gsmall = {k: jnp.stack([gsm[l][k].reshape(-1) for l in range(DEPTH)]) for k in gsm[0]}
    gsmall["rel_bias"] = jnp.transpose(_bias_grad(dbias, buckets)[:, :N_BUCKETS])
    gsmall["norm_final"] = dg_final.reshape(-1)
    small_like = [small[k] for k in SMALL]
    pk = lambda dct: _pack([dct[k] for k in SMALL])
    red = _small_allreduce(_pack([gsmall[k] for k in SMALL] + [loss_row[0, :1]]))
    gs = _unpack(red, small_like + [loss_row[0, :1]])
    loss = gs[-1][0]
    gs = dict(zip(SMALL, gs[:-1]))

    reduce_chips(red0b)
    stacks.update(adamw({k: reduced1[k] for k in mix_in + rest}, 1, None))
    dlt, m2, v2 = _adamw_small(pk(small), pk(gs), pk(small_m), pk(small_v))
    reduce_halves(red0a)
    stacks.update(adamw(reduce_end(red0a), 0, stacks))
    reduce_halves(red0b)
    stacks.update(adamw(reduce_end(red0b), 0, stacks))

    out_g, out_d, out_m, out_v = {}, {}, {}, {}
    for k in BIG:
        out_g[k], out_d[k], out_m[k], out_v[k] = stacks[k]
    for dst, packed in ((out_d, dlt), (out_m, m2), (out_v, v2)):
        dst.update(zip(SMALL, _unpack(packed, small_like)))
    out_g.update(gs)

    order = ("norm_ffn1", "ffn1_gu", "ffn1_down", "norm_mix", "w_in", "sinks", "norm_out_sb", "norm_out_swa", "w_out",
             "norm_ffn2", "ffn2_gu", "ffn2_down", "rel_bias", "norm_final")
    return (loss, dh.reshape(x.shape), *[out_g[k] for k in order], *[out_d[k] for k in order],
            *[out_m[k] for k in order], *[out_v[k] for k in order])
```

```python
import math

import numpy as np
import jax
import jax.numpy as jnp
from jax import lax
from jax.experimental import pallas as pl
from jax.experimental.pallas import tpu as pltpu

F32 = jnp.float32
BF16 = jnp.bfloat16

D_MODEL = 1024
DEPTH = 2
HEAD_DIM = 64
BLK = 128
N_BUCKETS = 32
MAX_DISTANCE = 128
D_FF = 2816
EPS = 1e-6
NEG_INF = -1e30
SB_W = 512
SWA_W = 512
KV_W = 128
IN_W = 2304
SCALE = HEAD_DIM ** -0.5
N_CHIPS = 4
FS = 2 * D_FF // N_CHIPS
LANES = 128
V7X_VMEM_LIMIT = 56 * 2 ** 20
TM = 512
SB_KT = 512
SWA_G = 4

ADAM_LR = 0.001
ADAM_B1 = 0.9
ADAM_B2 = 0.999
ADAM_EPS = 1e-08
ADAM_WD = 0.01
ADAM_STEP = 10

MESH = pl.DeviceIdType.MESH
ANY = pl.BlockSpec(memory_space=pl.ANY)
HBM = pl.BlockSpec(memory_space=pltpu.HBM)
SEM = pl.BlockSpec(memory_space=pltpu.SEMAPHORE)
EFFECT = pltpu.SideEffectType.DATAFLOW_SIDE_EFFECTING


def _params(n_grid):
    return pltpu.CompilerParams(dimension_semantics=("arbitrary",) * n_grid, vmem_limit_bytes=V7X_VMEM_LIMIT)


_PREVIOUS = [None]


def _call(body, *, name, in_specs, out_specs, out_shape, grid=(), num_scalar_prefetch=0, scratch_shapes=(),
          input_output_aliases=None, compiler_params=None, hbm_args=0):
    n_in = len(in_specs)

    def run(*args):
        dep = _PREVIOUS[0]
        if any(dep is a for a in args):
            dep = None
        args = [pltpu.with_memory_space_constraint(a, pltpu.HBM) if i < hbm_args else a for i, a in enumerate(args)]
        specs = list(in_specs) + ([ANY] if dep is not None else [])
        k = num_scalar_prefetch + n_in
        fn = body if dep is None else (lambda *refs: body(*refs[:k], *refs[k + 1:]))
        if num_scalar_prefetch:
            shape = dict(grid_spec=pltpu.PrefetchScalarGridSpec(
                num_scalar_prefetch=num_scalar_prefetch, grid=grid, in_specs=specs, out_specs=out_specs,
                scratch_shapes=scratch_shapes))
        else:
            shape = dict(grid=grid, in_specs=specs, out_specs=out_specs, scratch_shapes=scratch_shapes)
        out = pl.pallas_call(fn, name=name, out_shape=out_shape, input_output_aliases=input_output_aliases or {},
                             compiler_params=compiler_params, **shape)(*args, *([] if dep is None else [dep]))
        _PREVIOUS[0] = jax.tree.leaves(out)[-1]
        return out

    return run


def _dot(a, b):
    return jnp.dot(a, b, preferred_element_type=F32)


def _dot_nt(a, b):
    return lax.dot_general(a, b, (((1,), (1,)), ((), ())), preferred_element_type=F32)


def _dot_tn(a, b):
    return lax.dot_general(a, b, (((0,), (0,)), ((), ())), preferred_element_type=F32)


def _rms_fwd(x, g):
    r = lax.rsqrt(jnp.mean(x * x, axis=-1, keepdims=True) + EPS)
    xh = x * r
    return xh * g, xh, r


def _rms_bwd(dy, xh, r, g):
    u = dy * g
    dx = r * (u - xh * jnp.mean(u * xh, axis=-1, keepdims=True))
    dg = jnp.sum(dy * xh, axis=0, keepdims=True)
    return dx, dg


def _softplus(z):
    neg_abs = lax.bitcast_convert_type(lax.bitcast_convert_type(z, jnp.int32) | jnp.int32(-2 ** 31), F32)
    sp = jnp.maximum(z, 0.0) + jnp.log(1.0 + jnp.exp(neg_abs))
    return sp, z - sp


def _norm_cast(h, g):
    t, w = h.shape

    def body(h_ref, g_ref, n_ref):
        y, _, _ = _rms_fwd(h_ref[...], g_ref[...])
        n_ref[...] = y.astype(BF16)

    return _call(
        body, name="norm_cast", grid=(t // TM,),
        in_specs=[pl.BlockSpec((TM, w), lambda i: (i, 0)), pl.BlockSpec((1, w), lambda i: (0, 0))],
        out_specs=pl.BlockSpec((TM, w), lambda i: (i, 0)),
        out_shape=jax.ShapeDtypeStruct((t, w), BF16), compiler_params=_params(1))(h, g)


def _ffn_gu(n, wgu):
    t, d = n.shape

    def body(n_ref, wg_ref, wu_ref, gu_ref, act_ref):
        x = n_ref[...]
        g = _dot(x, wg_ref[...])
        u = _dot(x, wu_ref[...])
        sig = jax.nn.sigmoid(g)
        silu = g * sig
        gu_ref[0] = (u * (sig + silu * (1.0 - sig))).astype(BF16)
        gu_ref[1] = silu.astype(BF16)
        act_ref[...] = (silu * u).astype(BF16)

    return _call(
        body, name="ffn_gu", grid=(2, t // TM),
        in_specs=[pl.BlockSpec((TM, d), lambda j, i: (i, 0)),
                  pl.BlockSpec((None, d, FS), lambda j, i: (j, 0, 0)),
                  pl.BlockSpec((None, d, FS), lambda j, i: (j + 2, 0, 0))],
        out_specs=[pl.BlockSpec((2, TM, FS), lambda j, i: (0, i, j)), pl.BlockSpec((TM, FS), lambda j, i: (i, j))],
        out_shape=[jax.ShapeDtypeStruct((2, t, D_FF), BF16), jax.ShapeDtypeStruct((t, D_FF), BF16)],
        compiler_params=_params(2))(n, wgu, wgu)


def _down_res(act, wdn, h):
    t, f = act.shape
    d = h.shape[1]

    def body(a_ref, w_ref, h_ref, o_ref):
        o_ref[...] = h_ref[...] + 0.5 * _dot(a_ref[...], w_ref[...])

    return _call(
        body, name="down_res", grid=(t // TM,),
        in_specs=[pl.BlockSpec((TM, f), lambda i: (i, 0)), pl.BlockSpec((f, d), lambda i: (0, 0)),
                  pl.BlockSpec((TM, d), lambda i: (i, 0))],
        out_specs=pl.BlockSpec((TM, d), lambda i: (i, 0)),
        out_shape=jax.ShapeDtypeStruct((t, d), F32), compiler_params=_params(1))(act, wdn, h)


def _proj(n, w_in):
    t, d = n.shape
    w = w_in.shape[1]

    def body(n_ref, w_ref, o_ref):
        o_ref[...] = _dot(n_ref[...], w_ref[...]).astype(BF16)

    return _call(
        body, name="proj", grid=(t // TM,),
        in_specs=[pl.BlockSpec((TM, d), lambda i: (i, 0)), pl.BlockSpec((d, w), lambda i: (0, 0))],
        out_specs=pl.BlockSpec((TM, w), lambda i: (i, 0)),
        out_shape=jax.ShapeDtypeStruct((t, w), BF16), compiler_params=_params(1))(n, w_in)


def _out_res(o_sb, o_sw, g_sb, g_sw, w_out, h):
    t, d = h.shape

    def body(a_ref, b_ref, ga_ref, gb_ref, w_ref, h_ref, o_ref, mix_ref):
        ya, _, _ = _rms_fwd(a_ref[...], ga_ref[...])
        yb, _, _ = _rms_fwd(b_ref[...], gb_ref[...])
        mixed = jnp.concatenate([ya.astype(BF16), yb.astype(BF16)], axis=1)
        mix_ref[...] = mixed
        o_ref[...] = h_ref[...] + _dot(mixed, w_ref[...])

    return _call(
        body, name="out_res", grid=(t // TM,),
        in_specs=[pl.BlockSpec((TM, SB_W), lambda i: (i, 0)), pl.BlockSpec((TM, SWA_W), lambda i: (i, 0)),
                  pl.BlockSpec((1, SB_W), lambda i: (0, 0)), pl.BlockSpec((1, SWA_W), lambda i: (0, 0)),
                  pl.BlockSpec((d, d), lambda i: (0, 0)), pl.BlockSpec((TM, d), lambda i: (i, 0))],
        out_specs=[pl.BlockSpec((TM, d), lambda i: (i, 0)), pl.BlockSpec((TM, d), lambda i: (i, 0))],
        out_shape=[jax.ShapeDtypeStruct((t, d), F32), jax.ShapeDtypeStruct((t, d), BF16)],
        compiler_params=_params(1))(o_sb, o_sw, g_sb, g_sw, w_out, h)


def _loss_head(h, g, tgt):
    t, d = h.shape

    def body(h_ref, g_ref, t_ref, dh_ref, dg_ref, loss_ref):
        @pl.when(pl.program_id(0) == 0)
        def _():
            dg_ref[...] = jnp.zeros_like(dg_ref)
            loss_ref[...] = jnp.zeros_like(loss_ref)

        gg = g_ref[...]
        y, xh, r = _rms_fwd(h_ref[...], gg)
        err = y - t_ref[...]
        part = 0.5 * jnp.sum(jnp.sum(err * err, axis=1, keepdims=True) / d, axis=0, keepdims=True)
        loss_ref[...] += jnp.broadcast_to(part, loss_ref.shape)
        dx, dg = _rms_bwd(err / d, xh, r, gg)
        dh_ref[...] = dx
        dg_ref[...] += dg

    return _call(
        body, name="loss_head", grid=(t // TM,),
        in_specs=[pl.BlockSpec((TM, d), lambda i: (i, 0)), pl.BlockSpec((1, d), lambda i: (0, 0)),
                  pl.BlockSpec((TM, d), lambda i: (i, 0))],
        out_specs=[pl.BlockSpec((TM, d), lambda i: (i, 0)), pl.BlockSpec((1, d), lambda i: (0, 0)),
                   pl.BlockSpec((1, LANES), lambda i: (0, 0))],
        out_shape=[jax.ShapeDtypeStruct((t, d), F32), jax.ShapeDtypeStruct((1, d), F32),
                   jax.ShapeDtypeStruct((1, LANES), F32)],
        compiler_params=_params(1))(h, g, tgt)


def _ffn_dact(dh, wdn, gu):
    t, d = dh.shape
    tm = TM

    def body(dh_ref, w_ref, gu_ref, o_ref):
        da = 0.5 * _dot_nt(dh_ref[...].astype(BF16), w_ref[...])
        o_ref[0] = (da * gu_ref[0].astype(F32)).astype(BF16)
        o_ref[1] = (da * gu_ref[1].astype(F32)).astype(BF16)

    return _call(
        body, name="ffn_dact", grid=(2, t // tm),
        in_specs=[pl.BlockSpec((tm, d), lambda j, i: (i, 0)), pl.BlockSpec((FS, d), lambda j, i: (j, 0)),
                  pl.BlockSpec((2, tm, FS), lambda j, i: (0, i, j))],
        out_specs=pl.BlockSpec((2, tm, FS), lambda j, i: (0, i, j)),
        out_shape=jax.ShapeDtypeStruct((2, t, D_FF), BF16), compiler_params=_params(2))(dh, wdn, gu)


def _dn_norm_bwd(a, a_spec, w, w_spec, nk, dh, h_in, g):
    t, d = dh.shape

    def body(a_ref, w_ref, dh_ref, h_ref, g_ref, o_ref, dg_ref, acc_ref):
        i, k = pl.program_id(0), pl.program_id(1)

        if nk > 1:
            @pl.when(k == 0)
            def _():
                acc_ref[...] = _dot_nt(a_ref[...], w_ref[...])

            @pl.when((k > 0) & (k < nk - 1))
            def _():
                acc_ref[...] += _dot_nt(a_ref[...], w_ref[...])

        @pl.when(k == nk - 1)
        def _():
            gg = g_ref[...]
            dg = jnp.zeros_like(gg)
            for rows in (slice(0, TM // 2), slice(TM // 2, TM)):
                dn = _dot_nt(a_ref[rows, :], w_ref[...])
                if nk > 1:
                    dn = dn + acc_ref[rows, :]
                _, xh, r = _rms_fwd(h_ref[rows, :], gg)
                dx, dg_rows = _rms_bwd(dn, xh, r, gg)
                o_ref[rows, :] = dh_ref[rows, :] + dx
                dg = dg + dg_rows

            @pl.when(i == 0)
            def _():
                dg_ref[...] = dg

            @pl.when(i > 0)
            def _():
                dg_ref[...] += dg

    row = pl.BlockSpec((TM, d), lambda i, k: (i, 0))
    return _call(
        body, name="dn_norm_bwd", grid=(t // TM, nk),
        in_specs=[a_spec, w_spec, row, row, pl.BlockSpec((1, d), lambda i, k: (0, 0))],
        out_specs=[row, pl.BlockSpec((1, d), lambda i, k: (0, 0))],
        out_shape=[jax.ShapeDtypeStruct((t, d), F32), jax.ShapeDtypeStruct((1, d), F32)],
        scratch_shapes=[pltpu.VMEM((TM, d), F32)], compiler_params=_params(2))(a, w, dh, h_in, g)


def _ffn_dn(dgu, wgu, dh, h_in, g):
    d = dh.shape[1]
    return _dn_norm_bwd(
        dgu, pl.BlockSpec((None, TM, FS), lambda i, k: (k // 2, i, k % 2)),
        wgu, pl.BlockSpec((None, d, FS), lambda i, k: (k, 0, 0)), N_CHIPS, dh, h_in, g)


def _mix_dn(dproj, w_in, dh, h_in, g):
    d = dh.shape[1]
    w = dproj.shape[1]
    return _dn_norm_bwd(
        dproj, pl.BlockSpec((TM, w), lambda i, k: (i, 0)),
        w_in, pl.BlockSpec((d, w), lambda i, k: (0, 0)), 1, dh, h_in, g)


def _dmixed(dh, w_out, o_sb, o_sw, g_sb, g_sw):
    t, d = dh.shape

    def body(dh_ref, w_ref, a_ref, b_ref, ga_ref, gb_ref, o_ref, dga_ref, dgb_ref):
        i = pl.program_id(0)
        dm = _dot_nt(dh_ref[...].astype(BF16), w_ref[...])
        _, xa, ra = _rms_fwd(a_ref[...], ga_ref[...])
        _, xb, rb = _rms_fwd(b_ref[...], gb_ref[...])
        da, dga = _rms_bwd(dm[:, :SB_W], xa, ra, ga_ref[...])
        db, dgb = _rms_bwd(dm[:, SB_W:], xb, rb, gb_ref[...])
        o_ref[...] = jnp.concatenate([da.astype(BF16), db.astype(BF16)], axis=1)

        @pl.when(i == 0)
        def _():
            dga_ref[...] = dga
            dgb_ref[...] = dgb

        @pl.when(i > 0)
        def _():
            dga_ref[...] += dga
            dgb_ref[...] += dgb

    return _call(
        body, name="dmixed", grid=(t // TM,),
        in_specs=[pl.BlockSpec((TM, d), lambda i: (i, 0)), pl.BlockSpec((d, d), lambda i: (0, 0)),
                  pl.BlockSpec((TM, SB_W), lambda i: (i, 0)), pl.BlockSpec((TM, SWA_W), lambda i: (i, 0)),
                  pl.BlockSpec((1, SB_W), lambda i: (0, 0)), pl.BlockSpec((1, SWA_W), lambda i: (0, 0))],
        out_specs=[pl.BlockSpec((TM, d), lambda i: (i, 0)), pl.BlockSpec((1, SB_W), lambda i: (0, 0)),
                   pl.BlockSpec((1, SWA_W), lambda i: (0, 0))],
        out_shape=[jax.ShapeDtypeStruct((t, d), BF16), jax.ShapeDtypeStruct((1, SB_W), F32),
                   jax.ShapeDtypeStruct((1, SWA_W), F32)],
        compiler_params=_params(1))(dh, w_out, o_sb, o_sw, g_sb, g_sw)


def _wgrad(name, a, a_spec, b, b_spec, grid, out_shape, out_spec, scale):
    def body(a_ref, b_ref, o_ref):
        r = _dot_tn(a_ref[...], b_ref[...].astype(BF16))
        o_ref[...] = r if scale == 1.0 else scale * r

    return _call(
        body, name=name, grid=grid, in_specs=[a_spec, b_spec], out_specs=out_spec,
        out_shape=jax.ShapeDtypeStruct(out_shape, F32), compiler_params=_params(len(grid)))(a, b)


def _wgrad_gu(n, dgu):
    t, d = n.shape
    return _wgrad(
        "wgrad_gu", n, pl.BlockSpec((t, TM), lambda s, r: (0, r)),
        dgu, pl.BlockSpec((None, t, FS), lambda s, r: (s // 2, 0, s % 2)), (N_CHIPS, d // TM),
        (N_CHIPS, d, FS), pl.BlockSpec((None, TM, FS), lambda s, r: (s, r, 0)), 1.0)


def _wgrad_down(act, dh):
    t, d = dh.shape
    return _wgrad(
        "wgrad_down", act, pl.BlockSpec((t, FS), lambda s, r: (0, s)), dh, pl.BlockSpec((t, TM), lambda s, r: (0, r)),
        (2, d // TM), (D_FF, d), pl.BlockSpec((FS, TM), lambda s, r: (s, r)), 0.5)


def _wgrad_out(mixed, dh):
    t, d = dh.shape
    return _wgrad(
        "wgrad_out", mixed, pl.BlockSpec((t, TM), lambda s: (0, s)), dh, pl.BlockSpec((t, d), lambda s: (0, 0)),
        (d // TM,), (d, d), pl.BlockSpec((TM, d), lambda s: (s, 0)), 1.0)


def _wgrad_in(n, dproj):
    t, d = n.shape
    w = dproj.shape[1]
    tw = w // 3
    return _wgrad(
        "wgrad_in", n, pl.BlockSpec((t, d), lambda s: (0, 0)), dproj, pl.BlockSpec((t, tw), lambda s: (0, s)),
        (3,), (d, w), pl.BlockSpec((d, tw), lambda s: (0, s)), 1.0)


def _tri(rel):
    row = lax.broadcasted_iota(jnp.int32, (BLK, BLK), 0)
    col = lax.broadcasted_iota(jnp.int32, (BLK, BLK), 1)
    m = rel(row, col).astype(BF16)
    return jnp.concatenate([m, m], axis=0)


def _scan_dot(x, tri2):
    hi = x.astype(BF16)
    lo = (x - hi.astype(F32)).astype(BF16)
    return _dot(jnp.concatenate([hi, lo], axis=1), tri2)


def _head_masks():
    lane = lax.broadcasted_iota(jnp.int32, (1, LANES), 1)
    return [lane < HEAD_DIM, lane >= HEAD_DIM]


def _sb_dcol():
    dcol = lax.broadcasted_iota(jnp.int32, (BLK, SB_KT), 1) - lax.broadcasted_iota(jnp.int32, (BLK, SB_KT), 0)
    return jnp.concatenate([dcol, dcol], axis=0)


def _sb_fwd(proj):
    t = proj.shape[0]
    nq = t // BLK
    nb = SB_KT // BLK

    def body(q_ref, k_ref, v_ref, o_ref, tot_ref):
        hm = _head_masks()
        dcol = _sb_dcol()
        after = _tri(lambda r, c: r > c)

        def tile(qh, kt, carry, acc, limit):
            ks = pl.ds(pl.multiple_of(kt * SB_KT, SB_KT), SB_KT)
            z = _dot_nt(qh, k_ref[ks, :])
            sp, zs = _softplus(z)
            valid = None if limit is None else dcol < limit
            spm = sp if valid is None else jnp.where(valid, sp, 0.0)
            sufs = [None] * nb
            for b in reversed(range(nb)):
                blk = spm[:, b * BLK:(b + 1) * BLK]
                sufs[b] = carry + _scan_dot(blk, after)
                carry = carry + jnp.sum(blk, axis=1, keepdims=True)
            w = jnp.exp(zs - jnp.concatenate(sufs, axis=1))
            if valid is not None:
                w = jnp.where(valid, w, 0.0)
            return carry, acc + _dot(w.astype(BF16), v_ref[ks, :])

        def qblock(qi, _):
            qs = pl.ds(pl.multiple_of(qi * BLK, BLK), BLK)
            q = q_ref[qs, :] * SCALE
            kd = qi // nb
            limit = (qi - kd * nb) * BLK
            qh = jnp.concatenate([jnp.where(m, q, jnp.zeros_like(q)) for m in hm], axis=0)
            c0 = tile(qh, kd, jnp.zeros((2 * BLK, 1), F32), jnp.zeros((2 * BLK, LANES), F32), limit)
            carry, acc = lax.fori_loop(0, kd, lambda n, c: tile(qh, kd - 1 - n, c[0], c[1], None), c0)
            o_ref[qs, :] = jnp.where(hm[0], acc[:BLK], acc[BLK:])
            for h in range(2):
                tot_ref[h, qs, :] = jnp.broadcast_to(carry[h * BLK:(h + 1) * BLK], (BLK, LANES))
            return 0

        lax.fori_loop(0, nq, qblock, 0)

    col_blk = lambda off: pl.BlockSpec((t, LANES), lambda p: (0, off + p))
    return _call(
        body, name="sb_fwd", grid=(4,), in_specs=[col_blk(0), col_blk(4), col_blk(8)],
        out_specs=[pl.BlockSpec((t, LANES), lambda p: (0, p)), pl.BlockSpec((2, t, LANES), lambda p: (p, 0, 0))],
        out_shape=[jax.ShapeDtypeStruct((t, SB_W), F32), jax.ShapeDtypeStruct((8, t, LANES), F32)],
        compiler_params=_params(1))(proj, proj, proj)


def _sb_bwd(proj, d_o, tot):
    t = proj.shape[0]
    nq = t // BLK
    nb = SB_KT // BLK

    def body(q_ref, k_ref, v_ref, do_ref, tot_ref, dq_ref, dk_ref, dv_ref, dk_acc, dv_acc):
        hm = _head_masks()
        dcol = _sb_dcol()
        before = _tri(lambda r, c: r < c)
        upto = _tri(lambda r, c: r <= c)
        dk_acc[...] = jnp.zeros_like(dk_acc)
        dv_acc[...] = jnp.zeros_like(dv_acc)

        def tile(qh, doh, tt, kt, pre, ecum, dq, limit):
            ks = pl.ds(pl.multiple_of(kt * SB_KT, SB_KT), SB_KT)
            k = k_ref[ks, :]
            v = v_ref[ks, :]
            z = _dot_nt(qh, k)
            sp, zs = _softplus(z)
            valid = None if limit is None else dcol < limit
            spm = sp if valid is None else jnp.where(valid, sp, 0.0)
            pres = []
            for b in range(nb):
                blk = spm[:, b * BLK:(b + 1) * BLK]
                pres.append(pre + _scan_dot(blk, before))
                pre = pre + jnp.sum(blk, axis=1, keepdims=True)
            logw = z - (tt - jnp.concatenate(pres, axis=1))
            if valid is not None:
                logw = jnp.minimum(logw, 0.0)
            w = jnp.exp(logw)
            if valid is not None:
                w = jnp.where(valid, w, 0.0)
            e = w * _dot_nt(doh, v)
            incs = []
            for b in range(nb):
                blk = e[:, b * BLK:(b + 1) * BLK]
                incs.append(ecum + _scan_dot(blk, upto))
                ecum = ecum + jnp.sum(blk, axis=1, keepdims=True)
            dz = e - jnp.exp(zs) * jnp.concatenate(incs, axis=1)
            if valid is not None:
                dz = jnp.where(valid, dz, 0.0)
            dzb = dz.astype(BF16)
            dk_acc[ks, :] += _dot_tn(dzb, qh)
            dv_acc[ks, :] += _dot_tn(w.astype(BF16), doh)
            return pre, ecum, dq + _dot(dzb, k)

        def qblock(qi, _):
            qs = pl.ds(pl.multiple_of(qi * BLK, BLK), BLK)
            q = q_ref[qs, :] * SCALE
            do = do_ref[qs, :]
            kd = qi // nb
            limit = (qi - kd * nb) * BLK
            qh = jnp.concatenate([jnp.where(m, q, jnp.zeros_like(q)) for m in hm], axis=0)
            doh = jnp.concatenate([jnp.where(m, do, jnp.zeros_like(do)) for m in hm], axis=0)
            tt = jnp.concatenate([tot_ref[h, qs, 0:1] for h in range(2)], axis=0)
            c0 = (jnp.zeros((2 * BLK, 1), F32), jnp.zeros((2 * BLK, 1), F32), jnp.zeros((2 * BLK, LANES), F32))
            c = lax.fori_loop(0, kd, lambda kt, c: tile(qh, doh, tt, kt, c[0], c[1], c[2], None), c0)
            dq = tile(qh, doh, tt, kd, c[0], c[1], c[2], limit)[2]
            dq_ref[qs, :] = (jnp.where(hm[0], dq[:BLK], dq[BLK:]) * SCALE).astype(BF16)
            return 0

        lax.fori_loop(0, nq, qblock, 0)
        dk_ref[...] = dk_acc[...].astype(BF16)
        dv_ref[...] = dv_acc[...].astype(BF16)

    col_blk = lambda off: pl.BlockSpec((t, LANES), lambda p: (0, off + p))
    out = jax.ShapeDtypeStruct((t, SB_W), BF16)
    return _call(
        body, name="sb_bwd", grid=(4,),
        in_specs=[col_blk(0), col_blk(4), col_blk(8), col_blk(0), pl.BlockSpec((2, t, LANES), lambda p: (p, 0, 0))],
        out_specs=[col_blk(0), col_blk(0), col_blk(0)], out_shape=[out, out, out],
        scratch_shapes=[pltpu.VMEM((t, LANES), F32), pltpu.VMEM((t, LANES), F32)],
        compiler_params=_params(1))(proj, proj, proj, d_o, tot)


def _bucket_table():
    a = np.arange(BLK)[:, None]
    c = np.arange(2 * BLK)[None, :]
    dist = np.maximum(BLK + a - c, 0)
    max_exact = N_BUCKETS // 2
    dd = np.maximum(dist, 1).astype(np.float32)
    large = max_exact + (np.log(dd / max_exact) / math.log(MAX_DISTANCE / max_exact)
                         * (N_BUCKETS - max_exact)).astype(np.int32)
    large = np.minimum(large, N_BUCKETS - 1)
    return np.where(dist < max_exact, dist, large).astype(np.int32)


def _swa_band_masks():
    row = lax.broadcasted_iota(jnp.int32, (SWA_G * BLK, 2 * BLK), 0) & (BLK - 1)
    col = lax.broadcasted_iota(jnp.int32, (SWA_G * BLK, 2 * BLK), 1)
    own = lax.broadcasted_iota(jnp.int32, (SWA_G * BLK, BLK), 1) <= (
        lax.broadcasted_iota(jnp.int32, (SWA_G * BLK, BLK), 0) & (BLK - 1))
    return (col > row) & ((col < BLK) | (col - BLK <= row)), own


def _swa_stack(ref, qs, kvh, kvmask, scale):
    parts = []
    for g in range(SWA_G):
        hq = SWA_G * kvh + g
        x = ref[qs, (hq // 2) * LANES:(hq // 2 + 1) * LANES].astype(F32)
        if hq % 2 != kvh:
            x = pltpu.roll(x, HEAD_DIM, 1)
        parts.append(jnp.where(kvmask, x * scale, 0.0).astype(BF16))
    return jnp.concatenate(parts, axis=0)


def _swa_unstack(x4, kvh, hm):
    heads = []
    for g in range(SWA_G):
        x = x4[g * BLK:(g + 1) * BLK]
        heads.append(pltpu.roll(x, HEAD_DIM, 1) if g % 2 != kvh else x)
    return [jnp.where(hm[0], heads[0], heads[1]), jnp.where(hm[0], heads[2], heads[3])]


def _swa_scores(q4, kb, bias_ref, kvh, mask, cols):
    bias4 = jnp.concatenate([bias_ref[SWA_G * kvh + g, :, cols] for g in range(SWA_G)], axis=0)
    return jnp.where(mask, _dot_nt(q4, kb) + bias4, NEG_INF)


def _swa_sinks(sink_ref, kvh):
    return jnp.concatenate([jnp.broadcast_to(sink_ref[SWA_G * kvh + g:SWA_G * kvh + g + 1, 0:1], (BLK, 1))
                            for g in range(SWA_G)], axis=0)


def _swa_fwd(proj, bias, sinks_b):
    t = proj.shape[0]
    nq = t // BLK

    def body(q_ref, k_ref, v_ref, bias_ref, sink_ref, o_ref, lse_ref):
        hm = _head_masks()
        band, own = _swa_band_masks()

        def qblock(i, kvh, prev):
            qs = pl.ds(pl.multiple_of(i * BLK, BLK), BLK)
            if prev:
                ks, mask, cols = pl.ds(pl.multiple_of((i - 1) * BLK, BLK), 2 * BLK), band, slice(None)
            else:
                ks, mask, cols = qs, own, slice(BLK, None)
            q4 = _swa_stack(q_ref, qs, kvh, hm[kvh], SCALE)
            sink4 = _swa_sinks(sink_ref, kvh)
            s = _swa_scores(q4, k_ref[ks, :], bias_ref, kvh, mask, cols)
            m = jnp.maximum(jnp.max(s, axis=1, keepdims=True), sink4)
            p = jnp.exp(s - m)
            den = jnp.sum(p, axis=1, keepdims=True) + jnp.exp(sink4 - m)
            o4 = _dot((p * (1.0 / den)).astype(BF16), v_ref[ks, :])
            lse4 = m + jnp.log(den)
            for g in range(SWA_G):
                lse_ref[SWA_G * kvh + g, qs, :] = jnp.broadcast_to(lse4[g * BLK:(g + 1) * BLK], (BLK, LANES))
            for pp, o in enumerate(_swa_unstack(o4, kvh, hm)):
                o_ref[qs, (2 * kvh + pp) * LANES:(2 * kvh + pp + 1) * LANES] = o

        for kvh in range(2):
            qblock(0, kvh, False)

            def step(i, _):
                qblock(i, kvh, True)
                return 0

            lax.fori_loop(1, nq, step, 0)

    return _call(
        body, name="swa_fwd", grid=(1,),
        in_specs=[pl.BlockSpec((t, SWA_W), lambda i: (0, 3)), pl.BlockSpec((t, KV_W), lambda i: (0, 16)),
                  pl.BlockSpec((t, KV_W), lambda i: (0, 17)), pl.BlockSpec((8, BLK, 2 * BLK), lambda i: (0, 0, 0)),
                  pl.BlockSpec((8, LANES), lambda i: (0, 0))],
        out_specs=[pl.BlockSpec((t, SWA_W), lambda i: (0, 0)), pl.BlockSpec((8, t, LANES), lambda i: (0, 0, 0))],
        out_shape=[jax.ShapeDtypeStruct((t, SWA_W), F32), jax.ShapeDtypeStruct((8, t, LANES), F32)],
        compiler_params=_params(1))(proj, proj, proj, bias, sinks_b)


def _swa_bwd(proj, d_o, lse, bias, sinks_b, dbias_in):
    t = proj.shape[0]
    nq = t // BLK

    def body(q_ref, k_ref, v_ref, do_ref, lse_ref, bias_ref, sink_ref, dbi_ref,
             dq_ref, dk_ref, dv_ref, dsink_ref, dbias_ref, dk_acc, dv_acc):
        hm = _head_masks()
        band, own = _swa_band_masks()
        dk_acc[...] = jnp.zeros_like(dk_acc)
        dv_acc[...] = jnp.zeros_like(dv_acc)
        dbias_ref[...] = dbi_ref[...]

        def qblock(i, kvh, prev, dsink4):
            qs = pl.ds(pl.multiple_of(i * BLK, BLK), BLK)
            if prev:
                ks, mask, cols = pl.ds(pl.multiple_of((i - 1) * BLK, BLK), 2 * BLK), band, slice(None)
            else:
                ks, mask, cols = qs, own, slice(BLK, None)
            q4 = _swa_stack(q_ref, qs, kvh, hm[kvh], SCALE)
            do4 = _swa_stack(do_ref, qs, kvh, hm[kvh], 1.0)
            sink4 = _swa_sinks(sink_ref, kvh)
            lse4 = jnp.concatenate([lse_ref[SWA_G * kvh + g, qs, 0:1] for g in range(SWA_G)], axis=0)
            kb = k_ref[ks, :]
            p = jnp.exp(_swa_scores(q4, kb, bias_ref, kvh, mask, cols) - lse4)
            dp = _dot_nt(do4, v_ref[ks, :])
            delta = jnp.sum(p * dp, axis=1, keepdims=True)
            ds = p * (dp - delta)
            for g in range(SWA_G):
                dbias_ref[SWA_G * kvh + g, :, cols] += ds[g * BLK:(g + 1) * BLK]
            dsb = ds.astype(BF16)
            dk_acc[ks, :] += _dot_tn(dsb, q4)
            dv_acc[ks, :] += _dot_tn(p.astype(BF16), do4)
            for pp, dq in enumerate(_swa_unstack(_dot(dsb, kb) * SCALE, kvh, hm)):
                dq_ref[qs, (2 * kvh + pp) * LANES:(2 * kvh + pp + 1) * LANES] = dq.astype(BF16)
            return dsink4 - jnp.exp(sink4 - lse4) * delta

        for kvh in range(2):
            ds0 = qblock(0, kvh, False, jnp.zeros((SWA_G * BLK, 1), F32))
            ds4 = lax.fori_loop(1, nq, lambda i, c: qblock(i, kvh, True, c), ds0)
            for g in range(SWA_G):
                hq = SWA_G * kvh + g
                dsink_ref[hq:hq + 1, :] = jnp.broadcast_to(
                    jnp.sum(ds4[g * BLK:(g + 1) * BLK], axis=0, keepdims=True), (1, LANES))

        dk_ref[...] = dk_acc[...].astype(BF16)
        dv_ref[...] = dv_acc[...].astype(BF16)

    full3 = pl.BlockSpec((8, BLK, 2 * BLK), lambda i: (0, 0, 0))
    kv = jax.ShapeDtypeStruct((t, KV_W), BF16)
    return _call(
        body, name="swa_bwd", grid=(1,),
        in_specs=[pl.BlockSpec((t, SWA_W), lambda i: (0, 3)), pl.BlockSpec((t, KV_W), lambda i: (0, 16)),
                  pl.BlockSpec((t, KV_W), lambda i: (0, 17)), pl.BlockSpec((t, SWA_W), lambda i: (0, 1)),
                  pl.BlockSpec((8, t, LANES), lambda i: (0, 0, 0)), full3, pl.BlockSpec((8, LANES), lambda i: (0, 0)),
                  full3],
        out_specs=[pl.BlockSpec((t, SWA_W), lambda i: (0, 0)), pl.BlockSpec((t, KV_W), lambda i: (0, 0)),
                   pl.BlockSpec((t, KV_W), lambda i: (0, 0)), pl.BlockSpec((8, LANES), lambda i: (0, 0)), full3],
        out_shape=[jax.ShapeDtypeStruct((t, SWA_W), BF16), kv, kv, jax.ShapeDtypeStruct((8, LANES), F32),
                   jax.ShapeDtypeStruct((8, BLK, 2 * BLK), F32)],
        scratch_shapes=[pltpu.VMEM((t, KV_W), F32), pltpu.VMEM((t, KV_W), F32)],
        compiler_params=_params(1))(proj, proj, proj, d_o, lse, bias, sinks_b, dbias_in)


def _bias_table(rel_bias, buckets):
    def body(rb_ref, b_ref, o_ref):
        bk = b_ref[...]
        for h in range(8):
            acc = jnp.zeros((BLK, 2 * BLK), F32)
            for b in range(N_BUCKETS):
                acc = jnp.where(bk == b, rb_ref[b, h], acc)
            o_ref[h] = acc

    return _call(
        body, name="bias_table", grid=(1,),
        in_specs=[pl.BlockSpec(memory_space=pltpu.SMEM), pl.BlockSpec((BLK, 2 * BLK), lambda i: (0, 0))],
        out_specs=pl.BlockSpec((8, BLK, 2 * BLK), lambda i: (0, 0, 0)),
        out_shape=jax.ShapeDtypeStruct((8, BLK, 2 * BLK), F32), compiler_params=_params(1))(rel_bias, buckets)


def _bias_grad(dbias, buckets):
    def body(d_ref, b_ref, o_ref):
        lane = lax.broadcasted_iota(jnp.int32, (1, LANES), 1)
        bk = b_ref[...]
        for h in range(8):
            d = d_ref[h]
            acc = jnp.zeros((1, LANES), F32)
            for b in range(N_BUCKETS):
                s = jnp.sum(jnp.sum(jnp.where(bk == b, d, 0.0), axis=0, keepdims=True), axis=1, keepdims=True)
                acc = acc + jnp.where(lane == b, s, 0.0)
            o_ref[h:h + 1, :] = acc

    return _call(
        body, name="bias_grad", grid=(1,),
        in_specs=[pl.BlockSpec((8, BLK, 2 * BLK), lambda i: (0, 0, 0)), pl.BlockSpec((BLK, 2 * BLK), lambda i: (0, 0))],
        out_specs=pl.BlockSpec((8, LANES), lambda i: (0, 0)),
        out_shape=jax.ShapeDtypeStruct((8, LANES), F32), compiler_params=_params(1))(dbias, buckets)


def _row(a):
    return a.reshape(1, -1)


def _fwd_ffn1(h, w, small, l):
    s = {"h0": h}
    s["n1"] = _norm_cast(h, _row(small["norm_ffn1"][l]))
    s["gu1"], s["act1"] = _ffn_gu(s["n1"], w["ffn1_gu"])
    s["h1"] = _down_res(s["act1"], w["ffn1_down"], h)
    s["nm"] = _norm_cast(s["h1"], _row(small["norm_mix"][l]))
    return s


def _fwd_proj_sb(s, w):
    s["proj"] = _proj(s["nm"], w["w_in"])
    s["o_sb"], s["tot"] = _sb_fwd(s["proj"])


def _fwd_swa(s, small, l, bias):
    s["sinks_b"] = jnp.broadcast_to(small["sinks"][l][:, None], (8, LANES))
    s["o_sw"], s["lse"] = _swa_fwd(s["proj"], bias, s["sinks_b"])


def _fwd_out_ffn2(s, w, small, l):
    s["h2"], s["mixed"] = _out_res(s["o_sb"], s["o_sw"], _row(small["norm_out_sb"][l]), _row(small["norm_out_swa"][l]),
                                  w["w_out"], s["h1"])
    s["n2"] = _norm_cast(s["h2"], _row(small["norm_ffn2"][l]))
    s["gu2"], s["act2"] = _ffn_gu(s["n2"], w["ffn2_gu"])
    return _down_res(s["act2"], w["ffn2_down"], s["h2"])


def _bwd_ffn_dact(dh, s, w, which):
    return _ffn_dact(dh, w[f"ffn{which}_down"], s[f"gu{which}"])


def _bwd_ffn_rest(dh, dgu, s, w, small, l, which):
    h_in, norm = (s["h0"], "norm_ffn1") if which == 1 else (s["h2"], "norm_ffn2")
    g_down = _wgrad_down(s[f"act{which}"], dh)
    g_gu = _wgrad_gu(s[f"n{which}"], dgu)
    dh, dg = _ffn_dn(dgu, w[f"ffn{which}_gu"], dh, h_in, _row(small[norm][l]))
    return dh, {f"ffn{which}_down": g_down, f"ffn{which}_gu": g_gu}, {norm: dg}


def _bwd_ffn(dh, s, w, small, l, which):
    return _bwd_ffn_rest(dh, _bwd_ffn_dact(dh, s, w, which), s, w, small, l, which)


def _bwd_mix(dh, s, w, small, l, bias, dbias):
    g_out = _wgrad_out(s["mixed"], dh)
    d_o, dg_sb, dg_sw = _dmixed(dh, w["w_out"], s["o_sb"], s["o_sw"], _row(small["norm_out_sb"][l]),
                                _row(small["norm_out_swa"][l]))
    dq_sb, dk_sb, dv_sb = _sb_bwd(s["proj"], d_o, s["tot"])
    dq_sw, dk_sw, dv_sw, dsink, dbias = _swa_bwd(s["proj"], d_o, s["lse"], bias, s["sinks_b"], dbias)
    dproj = jnp.concatenate([dq_sb, dk_sb, dv_sb, dq_sw, dk_sw, dv_sw], axis=1)
    g_in = _wgrad_in(s["nm"], dproj)
    dh, dg_mix = _mix_dn(dproj, w["w_in"], dh, s["h1"], _row(small["norm_mix"][l]))
    gs = {"norm_out_sb": dg_sb, "norm_out_swa": dg_sw, "sinks": dsink[:, 0], "norm_mix": dg_mix}
    return dh, {"w_out": g_out, "w_in": g_in}, gs, dbias


def _place():
    x, y, c = lax.axis_index("x"), lax.axis_index("y"), lax.axis_index("c")
    return x, y, c, 2 * x + y


def _chip_core(k, c):
    return (k // 2, k % 2, c)


def _rows_per_block(rows, cols, copies):
    best = 16
    for tr in range(16, rows + 1, 16):
        if rows % tr == 0 and copies * tr * cols * 4 <= 2 * 2 ** 20:
            best = tr
    assert rows % best == 0
    return best


def _place_own(w, l, me1):
    _, rows, cols = w.shape
    tr = _rows_per_block(rows, cols, 1)

    def body(me_ref, w_ref, o_ref):
        o_ref[...] = w_ref[...].astype(BF16)

    return _call(
        body, name="place_own",
        num_scalar_prefetch=1, grid=(rows // tr,),
        in_specs=[pl.BlockSpec((None, tr, cols), lambda r, me: (l, r, 0))],
        out_specs=pl.BlockSpec((None, tr, cols), lambda r, me: (me[0], r, 0)),
        out_shape=jax.ShapeDtypeStruct((N_CHIPS, rows, cols), BF16), compiler_params=_params(1))(me1, w)


def _plan_gather_ici(bufs):
    _, _, c, me = _place()
    return [(b.at[me, c], b.at[me, c], b.at[(me + 3 - j) % N_CHIPS, c], _chip_core((me + 1 + j) % N_CHIPS, c))
            for b in bufs for j in range(3)]


def _plan_gather_d2d(bufs):
    x, y, c, me = _place()
    return [(b.at[(me + 3 - j) % N_CHIPS, c], b.at[(me + 3 - j) % N_CHIPS, c], b.at[(me + 3 - j) % N_CHIPS, 1 - c],
             (x, y, 1 - c)) for b in bufs for j in range(3)]


def _plan_grad_sibling(bufs):
    x, y, c, _ = _place()
    n = len(bufs) // 2
    return [(g.at[:, 1 - c], z, z, (x, y, 1 - c)) for g, z in zip(bufs[:n], bufs[n:])]


def _plan_grad_chips(bufs):
    _, _, c, me = _place()
    n = len(bufs) // 2
    return [(p.at[(me + 1 + j) % N_CHIPS], z.at[j], z.at[j], _chip_core((me + 1 + j) % N_CHIPS, c))
            for p, z in zip(bufs[:n], bufs[n:]) for j in range(3)]


def _plan_grad_halves(bufs):
    x, y, c, _ = _place()
    return [(b.at[c], b.at[c], b.at[1 - c], (x, y, 1 - c)) for b in bufs]


def _remote(src, dst, send_sem, recv_sem, to):
    return pltpu.make_async_remote_copy(src_ref=src, dst_ref=dst, send_sem=send_sem, recv_sem=recv_sem,
                                        device_id=to, device_id_type=MESH)


def _exchange_start(name, plan, bufs, n_copies):
    n = len(bufs)

    def body(*refs):
        ins = refs[:n]
        ssem, rsem = refs[n], refs[n + 1]
        token = refs[-1]
        for i, (src, dst, _, to) in enumerate(plan(ins)):
            _remote(src, dst, ssem.at[i], rsem.at[i], to).start()
        token[...] = jnp.zeros_like(token)

    out = _call(
        body, name=name,
        out_shape=(pltpu.SemaphoreType.DMA((n_copies,)), pltpu.SemaphoreType.DMA((n_copies,)),
                   *[pltpu.HBM(a.shape, a.dtype) for a in bufs], jax.ShapeDtypeStruct((8, LANES), F32)),
        in_specs=[HBM] * n, out_specs=(SEM, SEM, *[HBM] * n, pl.BlockSpec(memory_space=pltpu.VMEM)),
        input_output_aliases={t: 2 + t for t in range(n)}, hbm_args=n,
        compiler_params=pltpu.CompilerParams(has_side_effects=EFFECT),
    )(*bufs)
    return (out[0], out[1]), list(out[2:2 + n])


def _exchange_wait(name, plan, bufs, sems):
    n = len(bufs)

    def body(*refs):
        ins = refs[:n]
        ssem, rsem = refs[n], refs[n + 1]
        for i, (src, dst, land, to) in enumerate(plan(ins)):
            _remote(src, dst, ssem.at[i], rsem.at[i], to).wait_send()
            _remote(land, land, ssem.at[i], rsem.at[i], to).wait_recv()

    return list(_call(
        body, name=name, out_shape=[pltpu.HBM(a.shape, a.dtype) for a in bufs],
        in_specs=[HBM] * n + [SEM, SEM], out_specs=[HBM] * n,
        input_output_aliases={t: t for t in range(n)},
        compiler_params=pltpu.CompilerParams(has_side_effects=EFFECT),
    )(*bufs, sems[0], sems[1]))


def _gather_now(bufs):
    n = len(bufs)
    n_cp = 3 * n

    def body(*refs):
        outs = refs[n:2 * n]
        ici_s, ici_r, d2d_s, d2d_r = refs[2 * n:]
        first = _plan_gather_ici(outs)
        second = _plan_gather_d2d(outs)
        for i, (src, dst, _, to) in enumerate(first):
            _remote(src, dst, ici_s.at[i], ici_r.at[i], to).start()
        for i, (src, dst, _, to) in enumerate(second):
            land = first[i][2]
            _remote(land, land, ici_s.at[i], ici_r.at[i], to).wait_recv()
            _remote(src, dst, d2d_s.at[i], d2d_r.at[i], to).start()
        for i, (_, _, land, to) in enumerate(second):
            _remote(land, land, d2d_s.at[i], d2d_r.at[i], to).wait_recv()
        for i in range(n_cp):
            _remote(first[i][0], first[i][1], ici_s.at[i], ici_r.at[i], first[i][3]).wait_send()
            _remote(second[i][0], second[i][1], d2d_s.at[i], d2d_r.at[i], second[i][3]).wait_send()

    return _call(
        body, name="gather_layer0", in_specs=[ANY] * n, out_specs=[ANY] * n,
        out_shape=[jax.ShapeDtypeStruct(a.shape, a.dtype) for a in bufs],
        input_output_aliases={t: t for t in range(n)},
        scratch_shapes=[pltpu.SemaphoreType.DMA((n_cp,))] * 4,
        compiler_params=pltpu.CompilerParams(vmem_limit_bytes=V7X_VMEM_LIMIT))(*bufs)


def _chip_sum(g, xbuf, cm):
    _, _, r2, cols = g.shape
    tr = _rows_per_block(r2, cols, N_CHIPS)

    def body(cm_ref, g_ref, x_ref, pb_ref, po_ref):
        pb_ref[...] = (g_ref[...] + x_ref[...]).astype(BF16)
        me = cm_ref[1]
        po_ref[...] = g_ref[me] + x_ref[me]

    return _call(
        body, name="grad_chip_sum",
        num_scalar_prefetch=1, grid=(r2 // tr,),
        in_specs=[pl.BlockSpec((N_CHIPS, None, tr, cols), lambda r, cm: (0, cm[0], r, 0)),
                  pl.BlockSpec((N_CHIPS, tr, cols), lambda r, cm: (0, r, 0))],
        out_specs=[pl.BlockSpec((N_CHIPS, tr, cols), lambda r, cm: (0, r, 0)),
                   pl.BlockSpec((tr, cols), lambda r, cm: (r, 0))],
        out_shape=[jax.ShapeDtypeStruct((N_CHIPS, r2, cols), BF16), jax.ShapeDtypeStruct((r2, cols), F32)],
        compiler_params=_params(1))(cm, g, xbuf)


def _total_sum(pown, rbuf, cm):
    r2, cols = pown.shape
    tr = _rows_per_block(r2, cols, 3)

    def body(cm_ref, p_ref, r_ref, o_ref):
        acc = p_ref[...]
        for j in range(3):
            acc = acc + r_ref[j].astype(F32)
        o_ref[...] = acc

    return _call(
        body, name="grad_total_sum",
        num_scalar_prefetch=1, grid=(r2 // tr,),
        in_specs=[pl.BlockSpec((tr, cols), lambda r, cm: (r, 0)),
                  pl.BlockSpec((3, tr, cols), lambda r, cm: (0, r, 0))],
        out_specs=pl.BlockSpec((None, tr, cols), lambda r, cm: (cm[0], r, 0)),
        out_shape=jax.ShapeDtypeStruct((2, r2, cols), F32), compiler_params=_params(1))(cm, pown, rbuf)


def _small_allreduce(v):
    rows = v.shape[0]
    n_dev = 2 * N_CHIPS

    def body(v_ref, o_ref, buf, ssem, rsem):
        x, y, c, _ = _place()
        me = 4 * x + 2 * y + c
        buf[me] = v_ref[...]

        def copy(d, slot, to):
            return _remote(v_ref, buf.at[slot], ssem.at[d - 1], rsem.at[d - 1], (to // 4, (to // 2) % 2, to % 2))

        cps = [copy(d, me, (me + d) % n_dev) for d in range(1, n_dev)]
        for cp in cps:
            cp.start()
        for d in range(1, n_dev):
            copy(d, (me + n_dev - d) % n_dev, me).wait_recv()
        for cp in cps:
            cp.wait_send()
        acc = buf[0]
        for i in range(1, n_dev):
            acc = acc + buf[i]
        o_ref[...] = acc

    vm = pl.BlockSpec(memory_space=pltpu.VMEM)
    return _call(
        body, name="small_allreduce", in_specs=[vm], out_specs=vm,
        out_shape=jax.ShapeDtypeStruct(v.shape, F32),
        scratch_shapes=[pltpu.VMEM((n_dev, rows, LANES), F32), pltpu.SemaphoreType.DMA((n_dev - 1,)),
                        pltpu.SemaphoreType.DMA((n_dev - 1,))],
        compiler_params=pltpu.CompilerParams(vmem_limit_bytes=V7X_VMEM_LIMIT))(v)


def _adamw_math(w, g, m, v):
    m2 = ADAM_B1 * m + (1.0 - ADAM_B1) * g
    v2 = ADAM_B2 * v + (1.0 - ADAM_B2) * (g * g)
    m_hat = m2 / (1.0 - ADAM_B1 ** ADAM_STEP)
    v_hat = v2 / (1.0 - ADAM_B2 ** ADAM_STEP)
    return -ADAM_LR * (m_hat / (jnp.sqrt(v_hat) + ADAM_EPS) + ADAM_WD * w), m2, v2


def _adamw_layer(w, g, m, v, l, prev):
    _, rows, cols = w.shape
    tr = rows
    for cand in range(8, rows + 1, 8):
        if rows % cand == 0 and cand * cols * 4 <= 2 ** 19:
            tr = cand

    def body(w_ref, g_ref, m_ref, v_ref, *outs):
        go_ref, d_ref, m2_ref, v2_ref = outs[-4:]
        g = g_ref[...]
        go_ref[...] = g
        d_ref[...], m2_ref[...], v2_ref[...] = _adamw_math(w_ref[...], g, m_ref[...], v_ref[...])

    stack = pl.BlockSpec((None, tr, cols), lambda i: (l, i, 0))
    ins, specs, alias = [w, g, m, v], [stack, pl.BlockSpec((tr, cols), lambda i: (i, 0)), stack, stack], {}
    if prev is not None:
        ins += list(prev)
        specs += [ANY] * 4
        alias = {4 + i: i for i in range(4)}
    return _call(
        body, name="adamw", grid=(rows // tr,), in_specs=specs, out_specs=[stack] * 4,
        out_shape=[jax.ShapeDtypeStruct(w.shape, F32)] * 4, input_output_aliases=alias,
        compiler_params=_params(1))(*ins)


def _adamw_small(w, g, m, v):
    def body(w_ref, g_ref, m_ref, v_ref, d_ref, m2_ref, v2_ref):
        d_ref[...], m2_ref[...], v2_ref[...] = _adamw_math(w_ref[...], g_ref[...], m_ref[...], v_ref[...])

    spec = pl.BlockSpec(w.shape, lambda i: (0, 0))
    return _call(
        body, name="adamw_small", grid=(1,), in_specs=[spec] * 4, out_specs=[spec] * 3,
        out_shape=[jax.ShapeDtypeStruct(w.shape, F32)] * 3, compiler_params=_params(1))(w, g, m, v)


SMALL = ("norm_ffn1", "norm_mix", "sinks", "norm_out_sb", "norm_out_swa", "norm_ffn2", "rel_bias", "norm_final")
BIG = ("ffn1_gu", "ffn1_down", "w_in", "w_out", "ffn2_gu", "ffn2_down")


def _pack(parts):
    rows = []
    for a in parts:
        a = a.reshape(-1).astype(F32)
        rows.append(jnp.pad(a, (0, -a.shape[0] % LANES)).reshape(-1, LANES))
    out = jnp.concatenate(rows, axis=0)
    return jnp.pad(out, ((0, -out.shape[0] % 8), (0, 0)))


def _unpack(packed, like):
    out, r = [], 0
    for a in like:
        n = math.prod(a.shape)
        nr = -(-n // LANES)
        out.append(packed[r:r + nr].reshape(-1)[:n].reshape(a.shape))
        r += nr
    return out


def _halved(a):
    k, r, cols = a.shape
    return a.reshape(k, 2, r // 2, cols)


def _weight_view(k, buf):
    full = buf.reshape(N_CHIPS, buf.shape[2] * 2, buf.shape[3])
    if k.endswith("_gu"):
        return full
    if k == "w_in":
        return jnp.transpose(full, (1, 0, 2)).reshape(D_MODEL, IN_W)
    return full.reshape(-1, D_MODEL)


def _grad_stack(k, g):
    if k == "w_in":
        g = jnp.transpose(g.reshape(D_MODEL, N_CHIPS, IN_W // N_CHIPS), (1, 0, 2))
    elif not k.endswith("_gu"):
        g = g.reshape(N_CHIPS, g.shape[0] // N_CHIPS, D_MODEL)
    return _halved(g)


def _empty_like_hbm(shape, dtype):
    return pltpu.with_memory_space_constraint(lax.empty(shape, dtype), pltpu.HBM)


def kernel(x, norm_ffn1, w_ffn1_gu, w_ffn1_down, norm_mix, w_in, sinks, norm_out_sb, norm_out_swa, w_out, norm_ffn2, w_ffn2_gu, w_ffn2_down, rel_bias, norm_final, loss_target, m_norm_ffn1, m_w_ffn1_gu, m_w_ffn1_down, m_norm_mix, m_w_in, m_sinks, m_norm_out_sb, m_norm_out_swa, m_w_out, m_norm_ffn2, m_w_ffn2_gu, m_w_ffn2_down, m_rel_bias, m_norm_final, v_norm_ffn1, v_w_ffn1_gu, v_w_ffn1_down, v_norm_mix, v_w_in, v_sinks, v_norm_out_sb, v_norm_out_swa, v_w_out, v_norm_ffn2, v_w_ffn2_gu, v_w_ffn2_down, v_rel_bias, v_norm_final):
    big_w = dict(ffn1_gu=w_ffn1_gu, ffn1_down=w_ffn1_down, w_in=w_in, w_out=w_out, ffn2_gu=w_ffn2_gu, ffn2_down=w_ffn2_down)
    big_m = dict(ffn1_gu=m_w_ffn1_gu, ffn1_down=m_w_ffn1_down, w_in=m_w_in, w_out=m_w_out, ffn2_gu=m_w_ffn2_gu, ffn2_down=m_w_ffn2_down)
    big_v = dict(ffn1_gu=v_w_ffn1_gu, ffn1_down=v_w_ffn1_down, w_in=v_w_in, w_out=v_w_out, ffn2_gu=v_w_ffn2_gu, ffn2_down=v_w_ffn2_down)
    small = dict(norm_ffn1=norm_ffn1, norm_mix=norm_mix, sinks=sinks, norm_out_sb=norm_out_sb, norm_out_swa=norm_out_swa,
                 norm_ffn2=norm_ffn2, rel_bias=rel_bias, norm_final=norm_final)
    small_m = dict(norm_ffn1=m_norm_ffn1, norm_mix=m_norm_mix, sinks=m_sinks, norm_out_sb=m_norm_out_sb,
                   norm_out_swa=m_norm_out_swa, norm_ffn2=m_norm_ffn2, rel_bias=m_rel_bias, norm_final=m_norm_final)
    small_v = dict(norm_ffn1=v_norm_ffn1, norm_mix=v_norm_mix, sinks=v_sinks, norm_out_sb=v_norm_out_sb,
                   norm_out_swa=v_norm_out_swa, norm_ffn2=v_norm_ffn2, rel_bias=v_rel_bias, norm_final=v_norm_final)
    _PREVIOUS[0] = None
    _, _, c, me = _place()
    cm = jnp.stack([c, me]).astype(jnp.int32)
    buckets = jnp.asarray(_bucket_table())
    ffn1, mix_in, rest = ("ffn1_gu", "ffn1_down"), ("w_in",), ("w_out", "ffn2_gu", "ffn2_down")

    def place(l, keys):
        return [_halved(_place_own(big_w[k], l, cm[1:])) for k in keys]

    def views(keys, bufs):
        return {k: _weight_view(k, b) for k, b in zip(keys, bufs)}

    def gather_start(tag, bufs):
        return _exchange_start(f"gather{tag}_ici_start", _plan_gather_ici, bufs, 3 * len(bufs))

    def gather_pass(tag, flight):
        bufs = _exchange_wait(f"gather{tag}_ici_wait", _plan_gather_ici, flight[1], flight[0])
        return _exchange_start(f"gather{tag}_d2d_start", _plan_gather_d2d, bufs, 3 * len(bufs))

    def gather_done(tag, keys, flight):
        return views(keys, _exchange_wait(f"gather{tag}_d2d_wait", _plan_gather_d2d, flight[1], flight[0]))

    w0 = views(ffn1, _gather_now(place(0, ffn1)))
    fly_in0 = gather_start("0b", place(0, mix_in))
    fly_rest0 = gather_start("0c", place(0, rest))
    bias = _bias_table(rel_bias, buckets)
    fly_ffn1 = gather_start("1a", place(1, ffn1))
    fly_rest1 = gather_start("1b", place(1, mix_in + rest))

    s0 = _fwd_ffn1(x[0], w0, small, 0)
    w0.update(gather_done("0b", mix_in, gather_pass("0b", fly_in0)))
    _fwd_proj_sb(s0, w0)
    fly_rest0 = gather_pass("0c", fly_rest0)
    _fwd_swa(s0, small, 0, bias)
    w0.update(gather_done("0c", rest, fly_rest0))
    h = _fwd_out_ffn2(s0, w0, small, 0)
    fly_ffn1 = gather_pass("1a", fly_ffn1)
    fly_rest1 = gather_pass("1b", fly_rest1)
    w1 = gather_done("1a", ffn1, fly_ffn1)
    s1 = _fwd_ffn1(h, w1, small, 1)
    w1.update(gather_done("1b", mix_in + rest, fly_rest1))
    _fwd_proj_sb(s1, w1)
    _fwd_swa(s1, small, 1, bias)
    h = _fwd_out_ffn2(s1, w1, small, 1)
    dh, dg_final, loss_row = _loss_head(h, _row(norm_final), loss_target[0])

    def landing(stacks, lead, dtype):
        return [_empty_like_hbm((lead,) + a.shape[2:], dtype) for a in stacks]

    def reduce_begin(tag, keys, gw):
        stacks = [_grad_stack(k, gw[k]) for k in keys]
        flight = _exchange_start(f"grad{tag}_sibling_start", _plan_grad_sibling,
                                 stacks + landing(stacks, N_CHIPS, F32), len(keys))
        return dict(tag=tag, keys=keys, stacks=stacks, flight=flight)

    def reduce_chips(st):
        n, (sems, bufs) = len(st["keys"]), st["flight"]
        bufs = _exchange_wait(f"grad{st['tag']}_sibling_wait", _plan_grad_sibling, bufs, sems)
        st["sums"] = [_chip_sum(g, z, cm) for g, z in zip(bufs[:n], bufs[n:])]
        st["flight"] = _exchange_start(f"grad{st['tag']}_chips_start", _plan_grad_chips,
                                       [s[0] for s in st["sums"]] + landing(st["stacks"], 3, BF16), 3 * n)

    def reduce_halves(st):
        n, (sems, bufs) = len(st["keys"]), st["flight"]
        bufs = _exchange_wait(f"grad{st['tag']}_chips_wait", _plan_grad_chips, bufs, sems)
        halves = [_total_sum(s[1], z, cm) for s, z in zip(st["sums"], bufs[n:])]
        st["flight"] = _exchange_start(f"grad{st['tag']}_halves_start", _plan_grad_halves, halves, n)

    def reduce_end(st):
        sems, bufs = st["flight"]
        bufs = _exchange_wait(f"grad{st['tag']}_halves_wait", _plan_grad_halves, bufs, sems)
        return {k: b.reshape(big_w[k].shape[1:]) for k, b in zip(st["keys"], bufs)}

    def adamw(reduced, l, prev):
        return {k: _adamw_layer(big_w[k], g, big_m[k], big_v[k], l, None if prev is None else prev[k])
                for k, g in reduced.items()}

    gsm = [dict() for _ in range(DEPTH)]
    dbias = jnp.zeros((8, BLK, 2 * BLK), F32)
    dh, gw1, gs = _bwd_ffn(dh, s1, w1, small, 1, 2)
    gsm[1].update(gs)
    dh, gw, gs, dbias = _bwd_mix(dh, s1, w1, small, 1, bias, dbias)
    gw1.update(gw)
    gsm[1].update(gs)
    dh, gw, gs = _bwd_ffn(dh, s1, w1, small, 1, 1)
    gw1.update(gw)
    gsm[1].update(gs)

    red1 = reduce_begin("1", BIG, gw1)
    dh, gw0, gs = _bwd_ffn(dh, s0, w0, small, 0, 2)
    gsm[0].update(gs)
    reduce_chips(red1)
    dh, gw, gs, dbias = _bwd_mix(dh, s0, w0, small, 0, bias, dbias)
    gw0.update(gw)
    gsm[0].update(gs)
    red0a = reduce_begin("0a", ("ffn2_gu", "ffn2_down", "w_out", "w_in"), gw0)
    reduce_halves(red1)
    dgu = _bwd_ffn_dact(dh, s0, w0, 1)
    reduce_chips(red0a)
    dh, gw, gs = _bwd_ffn_rest(dh, dgu, s0, w0, small, 0, 1)
    gsm[0].update(gs)
    red0b = reduce_begin("0b", ffn1, gw)
    reduced1 = reduce_end(red1)
    stacks = adamw({k: reduced1[k] for k in ffn1}, 1, None)

    gsmall = {k: jnp.stack([gsm[l][k].reshape(-1) for l in range(DEPTH)]) for k in gsm[0]}
    gsmall["rel_bias"] = jnp.transpose(_bias_grad(dbias, buckets)[:, :N_BUCKETS])
    gsmall["norm_final"] = dg_final.reshape(-1)
    small_like = [small[k] for k in SMALL]
    pk = lambda dct: _pack([dct[k] for k in SMALL])
    red = _small_allreduce(_pack([gsmall[k] for k in SMALL] + [loss_row[0, :1]]))
    gs = _unpack(red, small_like + [loss_row[0, :1]])
    loss = gs[-1][0]
    gs = dict(zip(SMALL, gs[:-1]))

    reduce_chips(red0b)
    stacks.update(adamw({k: reduced1[k] for k in mix_in + rest}, 1, None))
    dlt, m2, v2 = _adamw_small(pk(small), pk(gs), pk(small_m), pk(small_v))
    reduce_halves(red0a)
    stacks.update(adamw(reduce_end(red0a), 0, stacks))
    reduce_halves(red0b)
    stacks.update(adamw(reduce_end(red0b), 0, stacks))

    out_g, out_d, out_m, out_v = {}, {}, {}, {}
    for k in BIG:
        out_g[k], out_d[k], out_m[k], out_v[k] = stacks[k]
    for dst, packed in ((out_d, dlt), (out_m, m2), (out_v, v2)):
        dst.update(zip(SMALL, _unpack(packed, small_like)))
    out_g.update(gs)

    order = ("norm_ffn1", "ffn1_gu", "ffn1_down", "norm_mix", "w_in", "sinks", "norm_out_sb", "norm_out_swa", "w_out",
             "norm_ffn2", "ffn2_gu", "ffn2_down", "rel_bias", "norm_final")
    return (loss, dh.reshape(x.shape), *[out_g[k] for k in order], *[out_d[k] for k in order],
            *[out_m[k] for k in order], *[out_v[k] for k in order])
```

```python
import math

import numpy as np
import jax
import jax.numpy as jnp
from jax import lax
from jax.experimental import pallas as pl
from jax.experimental.pallas import tpu as pltpu

F32 = jnp.float32
BF16 = jnp.bfloat16

D_MODEL = 1024
DEPTH = 2
HEAD_DIM = 64
BLK = 128
N_BUCKETS = 32
MAX_DISTANCE = 128
D_FF = 2816
EPS = 1e-6
NEG_INF = -1e30
SB_W = 512
SWA_W = 512
KV_W = 128
IN_W = 2304
SCALE = HEAD_DIM ** -0.5
N_CHIPS = 4
FS = 2 * D_FF // N_CHIPS
LANES = 128
V7X_VMEM_LIMIT = 56 * 2 ** 20
TM = 512
SB_KT = 512
SWA_G = 4

ADAM_LR = 0.001
ADAM_B1 = 0.9
ADAM_B2 = 0.999
ADAM_EPS = 1e-08
ADAM_WD = 0.01
ADAM_STEP = 10

MESH = pl.DeviceIdType.MESH
ANY = pl.BlockSpec(memory_space=pl.ANY)
HBM = pl.BlockSpec(memory_space=pltpu.HBM)
SEM = pl.BlockSpec(memory_space=pltpu.SEMAPHORE)
EFFECT = pltpu.SideEffectType.DATAFLOW_SIDE_EFFECTING


def _params(n_grid):
    return pltpu.CompilerParams(dimension_semantics=("arbitrary",) * n_grid, vmem_limit_bytes=V7X_VMEM_LIMIT)


_PREVIOUS = [None]


def _call(body, *, name, in_specs, out_specs, out_shape, grid=(), num_scalar_prefetch=0, scratch_shapes=(),
          input_output_aliases=None, compiler_params=None, hbm_args=0):
    n_in = len(in_specs)

    def run(*args):
        dep = _PREVIOUS[0]
        if any(dep is a for a in args):
            dep = None
        args = [pltpu.with_memory_space_constraint(a, pltpu.HBM) if i < hbm_args else a for i, a in enumerate(args)]
        specs = list(in_specs) + ([ANY] if dep is not None else [])
        k = num_scalar_prefetch + n_in
        fn = body if dep is None else (lambda *refs: body(*refs[:k], *refs[k + 1:]))
        if num_scalar_prefetch:
            shape = dict(grid_spec=pltpu.PrefetchScalarGridSpec(
                num_scalar_prefetch=num_scalar_prefetch, grid=grid, in_specs=specs, out_specs=out_specs,
                scratch_shapes=scratch_shapes))
        else:
            shape = dict(grid=grid, in_specs=specs, out_specs=out_specs, scratch_shapes=scratch_shapes)
        out = pl.pallas_call(fn, name=name, out_shape=out_shape, input_output_aliases=input_output_aliases or {},
                             compiler_params=compiler_params, **shape)(*args, *([] if dep is None else [dep]))
        _PREVIOUS[0] = jax.tree.leaves(out)[-1]
        return out

    return run


def _dot(a, b):
    return jnp.dot(a, b, preferred_element_type=F32)


def _dot_nt(a, b):
    return lax.dot_general(a, b, (((1,), (1,)), ((), ())), preferred_element_type=F32)


def _dot_tn(a, b):
    return lax.dot_general(a, b, (((0,), (0,)), ((), ())), preferred_element_type=F32)


def _rms_fwd(x, g):
    r = lax.rsqrt(jnp.mean(x * x, axis=-1, keepdims=True) + EPS)
    xh = x * r
    return xh * g, xh, r


def _rms_bwd(dy, xh, r, g):
    u = dy * g
    dx = r * (u - xh * jnp.mean(u * xh, axis=-1, keepdims=True))
    dg = jnp.sum(dy * xh, axis=0, keepdims=True)
    return dx, dg


def _softplus(z):
    neg_abs = lax.bitcast_convert_type(lax.bitcast_convert_type(z, jnp.int32) | jnp.int32(-2 ** 31), F32)
    sp = jnp.maximum(z, 0.0) + jnp.log(1.0 + jnp.exp(neg_abs))
    return sp, z - sp


def _norm_cast(h, g):
    t, w = h.shape

    def body(h_ref, g_ref, n_ref):
        y, _, _ = _rms_fwd(h_ref[...], g_ref[...])
        n_ref[...] = y.astype(BF16)

    return _call(
        body, name="norm_cast", grid=(t // TM,),
        in_specs=[pl.BlockSpec((TM, w), lambda i: (i, 0)), pl.BlockSpec((1, w), lambda i: (0, 0))],
        out_specs=pl.BlockSpec((TM, w), lambda i: (i, 0)),
        out_shape=jax.ShapeDtypeStruct((t, w), BF16), compiler_params=_params(1))(h, g)


def _ffn_gu(n, wgu):
    t, d = n.shape

    def body(n_ref, wg_ref, wu_ref, gu_ref, act_ref):
        x = n_ref[...]
        g = _dot(x, wg_ref[...])
        u = _dot(x, wu_ref[...])
        sig = jax.nn.sigmoid(g)
        silu = g * sig
        gu_ref[0] = (u * (sig + silu * (1.0 - sig))).astype(BF16)
        gu_ref[1] = silu.astype(BF16)
        act_ref[...] = (silu * u).astype(BF16)

    return _call(
        body, name="ffn_gu", grid=(2, t // TM),
        in_specs=[pl.BlockSpec((TM, d), lambda j, i: (i, 0)),
                  pl.BlockSpec((None, d, FS), lambda j, i: (j, 0, 0)),
                  pl.BlockSpec((None, d, FS), lambda j, i: (j + 2, 0, 0))],
        out_specs=[pl.BlockSpec((2, TM, FS), lambda j, i: (0, i, j)), pl.BlockSpec((TM, FS), lambda j, i: (i, j))],
        out_shape=[jax.ShapeDtypeStruct((2, t, D_FF), BF16), jax.ShapeDtypeStruct((t, D_FF), BF16)],
        compiler_params=_params(2))(n, wgu, wgu)


def _down_res(act, wdn, h, g_next):
    t, f = act.shape
    d = h.shape[1]

    def body(a_ref, w_ref, h_ref, g_ref, o_ref, n_ref):
        out = h_ref[...] + 0.5 * _dot(a_ref[...], w_ref[...])
        o_ref[...] = out
        n_ref[...] = _rms_fwd(out, g_ref[...])[0].astype(BF16)

    row = pl.BlockSpec((TM, d), lambda i: (i, 0))
    return _call(
        body, name="down_res", grid=(t // TM,),
        in_specs=[pl.BlockSpec((TM, f), lambda i: (i, 0)), pl.BlockSpec((f, d), lambda i: (0, 0)), row,
                  pl.BlockSpec((1, d), lambda i: (0, 0))],
        out_specs=[row, row],
        out_shape=[jax.ShapeDtypeStruct((t, d), F32), jax.ShapeDtypeStruct((t, d), BF16)],
        compiler_params=_params(1))(act, wdn, h, g_next)


def _proj(n, w_in):
    t, d = n.shape
    w = w_in.shape[1]

    def body(n_ref, w_ref, o_ref):
        o_ref[...] = _dot(n_ref[...], w_ref[...]).astype(BF16)

    return _call(
        body, name="proj", grid=(t // TM,),
        in_specs=[pl.BlockSpec((TM, d), lambda i: (i, 0)), pl.BlockSpec((d, w), lambda i: (0, 0))],
        out_specs=pl.BlockSpec((TM, w), lambda i: (i, 0)),
        out_shape=jax.ShapeDtypeStruct((t, w), BF16), compiler_params=_params(1))(n, w_in)


def _out_res(o_sb, o_sw, g_sb, g_sw, w_out, h, g_next):
    t, d = h.shape

    def body(a_ref, b_ref, ga_ref, gb_ref, w_ref, h_ref, g_ref, o_ref, mix_ref, n_ref):
        ya, _, _ = _rms_fwd(a_ref[...], ga_ref[...])
        yb, _, _ = _rms_fwd(b_ref[...], gb_ref[...])
        mixed = jnp.concatenate([ya.astype(BF16), yb.astype(BF16)], axis=1)
        mix_ref[...] = mixed
        out = h_ref[...] + _dot(mixed, w_ref[...])
        o_ref[...] = out
        n_ref[...] = _rms_fwd(out, g_ref[...])[0].astype(BF16)

    row = pl.BlockSpec((TM, d), lambda i: (i, 0))
    return _call(
        body, name="out_res", grid=(t // TM,),
        in_specs=[pl.BlockSpec((TM, SB_W), lambda i: (i, 0)), pl.BlockSpec((TM, SWA_W), lambda i: (i, 0)),
                  pl.BlockSpec((1, SB_W), lambda i: (0, 0)), pl.BlockSpec((1, SWA_W), lambda i: (0, 0)),
                  pl.BlockSpec((d, d), lambda i: (0, 0)), row, pl.BlockSpec((1, d), lambda i: (0, 0))],
        out_specs=[row, row, row],
        out_shape=[jax.ShapeDtypeStruct((t, d), F32), jax.ShapeDtypeStruct((t, d), BF16),
                   jax.ShapeDtypeStruct((t, d), BF16)],
        compiler_params=_params(1))(o_sb, o_sw, g_sb, g_sw, w_out, h, g_next)


def _loss_head(h, g, tgt):
    t, d = h.shape

    def body(h_ref, g_ref, t_ref, dh_ref, dg_ref, loss_ref):
        @pl.when(pl.program_id(0) == 0)
        def _():
            dg_ref[...] = jnp.zeros_like(dg_ref)
            loss_ref[...] = jnp.zeros_like(loss_ref)

        gg = g_ref[...]
        y, xh, r = _rms_fwd(h_ref[...], gg)
        err = y - t_ref[...]
        part = 0.5 * jnp.sum(jnp.sum(err * err, axis=1, keepdims=True) / d, axis=0, keepdims=True)
        loss_ref[...] += jnp.broadcast_to(part, loss_ref.shape)
        dx, dg = _rms_bwd(err / d, xh, r, gg)
        dh_ref[...] = dx
        dg_ref[...] += dg

    return _call(
        body, name="loss_head", grid=(t // TM,),
        in_specs=[pl.BlockSpec((TM, d), lambda i: (i, 0)), pl.BlockSpec((1, d), lambda i: (0, 0)),
                  pl.BlockSpec((TM, d), lambda i: (i, 0))],
        out_specs=[pl.BlockSpec((TM, d), lambda i: (i, 0)), pl.BlockSpec((1, d), lambda i: (0, 0)),
                   pl.BlockSpec((1, LANES), lambda i: (0, 0))],
        out_shape=[jax.ShapeDtypeStruct((t, d), F32), jax.ShapeDtypeStruct((1, d), F32),
                   jax.ShapeDtypeStruct((1, LANES), F32)],
        compiler_params=_params(1))(h, g, tgt)


def _ffn_dact(dh, wdn, gu):
    t, d = dh.shape
    tm = TM

    def body(dh_ref, w_ref, gu_ref, o_ref):
        da = 0.5 * _dot_nt(dh_ref[...].astype(BF16), w_ref[...])
        o_ref[0] = (da * gu_ref[0].astype(F32)).astype(BF16)
        o_ref[1] = (da * gu_ref[1].astype(F32)).astype(BF16)

    return _call(
        body, name="ffn_dact", grid=(2, t // tm),
        in_specs=[pl.BlockSpec((tm, d), lambda j, i: (i, 0)), pl.BlockSpec((FS, d), lambda j, i: (j, 0)),
                  pl.BlockSpec((2, tm, FS), lambda j, i: (0, i, j))],
        out_specs=pl.BlockSpec((2, tm, FS), lambda j, i: (0, i, j)),
        out_shape=jax.ShapeDtypeStruct((2, t, D_FF), BF16), compiler_params=_params(2))(dh, wdn, gu)


def _dn_norm_bwd(a, a_spec, w, w_spec, nk, dh, h_in, g):
    t, d = dh.shape

    def body(a_ref, w_ref, dh_ref, h_ref, g_ref, o_ref, dg_ref, acc_ref):
        i, k = pl.program_id(0), pl.program_id(1)

        if nk > 1:
            @pl.when(k == 0)
            def _():
                acc_ref[...] = _dot_nt(a_ref[...], w_ref[...])

            @pl.when((k > 0) & (k < nk - 1))
            def _():
                acc_ref[...] += _dot_nt(a_ref[...], w_ref[...])

        @pl.when(k == nk - 1)
        def _():
            gg = g_ref[...]
            dg = jnp.zeros_like(gg)
            for rows in (slice(0, TM // 2), slice(TM // 2, TM)):
                dn = _dot_nt(a_ref[rows, :], w_ref[...])
                if nk > 1:
                    dn = dn + acc_ref[rows, :]
                _, xh, r = _rms_fwd(h_ref[rows, :], gg)
                dx, dg_rows = _rms_bwd(dn, xh, r, gg)
                o_ref[rows, :] = dh_ref[rows, :] + dx
                dg = dg + dg_rows

            @pl.when(i == 0)
            def _():
                dg_ref[...] = dg

            @pl.when(i > 0)
            def _():
                dg_ref[...] += dg

    row = pl.BlockSpec((TM, d), lambda i, k: (i, 0))
    return _call(
        body, name="dn_norm_bwd", grid=(t // TM, nk),
        in_specs=[a_spec, w_spec, row, row, pl.BlockSpec((1, d), lambda i, k: (0, 0))],
        out_specs=[row, pl.BlockSpec((1, d), lambda i, k: (0, 0))],
        out_shape=[jax.ShapeDtypeStruct((t, d), F32), jax.ShapeDtypeStruct((1, d), F32)],
        scratch_shapes=[pltpu.VMEM((TM, d), F32)], compiler_params=_params(2))(a, w, dh, h_in, g)


def _ffn_dn(dgu, wgu, dh, h_in, g):
    d = dh.shape[1]
    return _dn_norm_bwd(
        dgu, pl.BlockSpec((None, TM, FS), lambda i, k: (k // 2, i, k % 2)),
        wgu, pl.BlockSpec((None, d, FS), lambda i, k: (k, 0, 0)), N_CHIPS, dh, h_in, g)


def _mix_dn(dproj, w_in, dh, h_in, g):
    d = dh.shape[1]
    w = dproj.shape[1]
    return _dn_norm_bwd(
        dproj, pl.BlockSpec((TM, w), lambda i, k: (i, 0)),
        w_in, pl.BlockSpec((d, w), lambda i, k: (0, 0)), 1, dh, h_in, g)


def _dmixed(dh, w_out, o_sb, o_sw, g_sb, g_sw):
    t, d = dh.shape

    def body(dh_ref, w_ref, a_ref, b_ref, ga_ref, gb_ref, o_ref, dga_ref, dgb_ref):
        i = pl.program_id(0)
        dm = _dot_nt(dh_ref[...].astype(BF16), w_ref[...])
        _, xa, ra = _rms_fwd(a_ref[...], ga_ref[...])
        _, xb, rb = _rms_fwd(b_ref[...], gb_ref[...])
        da, dga = _rms_bwd(dm[:, :SB_W], xa, ra, ga_ref[...])
        db, dgb = _rms_bwd(dm[:, SB_W:], xb, rb, gb_ref[...])
        o_ref[...] = jnp.concatenate([da.astype(BF16), db.astype(BF16)], axis=1)

        @pl.when(i == 0)
        def _():
            dga_ref[...] = dga
            dgb_ref[...] = dgb

        @pl.when(i > 0)
        def _():
            dga_ref[...] += dga
            dgb_ref[...] += dgb

    return _call(
        body, name="dmixed", grid=(t // TM,),
        in_specs=[pl.BlockSpec((TM, d), lambda i: (i, 0)), pl.BlockSpec((d, d), lambda i: (0, 0)),
                  pl.BlockSpec((TM, SB_W), lambda i: (i, 0)), pl.BlockSpec((TM, SWA_W), lambda i: (i, 0)),
                  pl.BlockSpec((1, SB_W), lambda i: (0, 0)), pl.BlockSpec((1, SWA_W), lambda i: (0, 0))],
        out_specs=[pl.BlockSpec((TM, d), lambda i: (i, 0)), pl.BlockSpec((1, SB_W), lambda i: (0, 0)),
                   pl.BlockSpec((1, SWA_W), lambda i: (0, 0))],
        out_shape=[jax.ShapeDtypeStruct((t, d), BF16), jax.ShapeDtypeStruct((1, SB_W), F32),
                   jax.ShapeDtypeStruct((1, SWA_W), F32)],
        compiler_params=_params(1))(dh, w_out, o_sb, o_sw, g_sb, g_sw)


def _wgrad(name, a, a_spec, b, b_spec, grid, out_shape, out_spec, scale):
    def body(a_ref, b_ref, o_ref):
        r = _dot_tn(a_ref[...], b_ref[...].astype(BF16))
        o_ref[...] = r if scale == 1.0 else scale * r

    return _call(
        body, name=name, grid=grid, in_specs=[a_spec, b_spec], out_specs=out_spec,
        out_shape=jax.ShapeDtypeStruct(out_shape, F32), compiler_params=_params(len(grid)))(a, b)


def _wgrad_gu(n, dgu):
    t, d = n.shape
    return _wgrad(
        "wgrad_gu", n, pl.BlockSpec((t, TM), lambda s, r: (0, r)),
        dgu, pl.BlockSpec((None, t, FS), lambda s, r: (s // 2, 0, s % 2)), (N_CHIPS, d // TM),
        (N_CHIPS, d, FS), pl.BlockSpec((None, TM, FS), lambda s, r: (s, r, 0)), 1.0)


def _wgrad_down(act, dh):
    t, d = dh.shape
    return _wgrad(
        "wgrad_down", act, pl.BlockSpec((t, FS), lambda s, r: (0, s)), dh, pl.BlockSpec((t, TM), lambda s, r: (0, r)),
        (2, d // TM), (D_FF, d), pl.BlockSpec((FS, TM), lambda s, r: (s, r)), 0.5)


def _wgrad_out(mixed, dh):
    t, d = dh.shape
    return _wgrad(
        "wgrad_out", mixed, pl.BlockSpec((t, TM), lambda s: (0, s)), dh, pl.BlockSpec((t, d), lambda s: (0, 0)),
        (d // TM,), (d, d), pl.BlockSpec((TM, d), lambda s: (s, 0)), 1.0)


def _wgrad_in(n, dproj):
    t, d = n.shape
    w = dproj.shape[1]
    tw = w // 3
    return _wgrad(
        "wgrad_in", n, pl.BlockSpec((t, d), lambda s: (0, 0)), dproj, pl.BlockSpec((t, tw), lambda s: (0, s)),
        (3,), (d, w), pl.BlockSpec((d, tw), lambda s: (0, s)), 1.0)


def _tri(rel):
    row = lax.broadcasted_iota(jnp.int32, (BLK, BLK), 0)
    col = lax.broadcasted_iota(jnp.int32, (BLK, BLK), 1)
    m = rel(row, col).astype(BF16)
    return jnp.concatenate([m, m], axis=0)


def _scan_dot(x, tri2):
    hi = x.astype(BF16)
    lo = (x - hi.astype(F32)).astype(BF16)
    return _dot(jnp.concatenate([hi, lo], axis=1), tri2)


def _head_masks():
    lane = lax.broadcasted_iota(jnp.int32, (1, LANES), 1)
    return [lane < HEAD_DIM, lane >= HEAD_DIM]


def _sb_dcol():
    dcol = lax.broadcasted_iota(jnp.int32, (BLK, SB_KT), 1) - lax.broadcasted_iota(jnp.int32, (BLK, SB_KT), 0)
    return jnp.concatenate([dcol, dcol], axis=0)


def _sb_fwd(proj):
    t = proj.shape[0]
    nq = t // BLK
    nb = SB_KT // BLK

    def body(q_ref, k_ref, v_ref, o_ref, tot_ref):
        hm = _head_masks()
        dcol = _sb_dcol()
        after = _tri(lambda r, c: r > c)

        def tile(qh, kt, carry, acc, limit):
            ks = pl.ds(pl.multiple_of(kt * SB_KT, SB_KT), SB_KT)
            z = _dot_nt(qh, k_ref[ks, :])
            sp, zs = _softplus(z)
            valid = None if limit is None else dcol < limit
            spm = sp if valid is None else jnp.where(valid, sp, 0.0)
            sufs = [None] * nb
            for b in reversed(range(nb)):
                blk = spm[:, b * BLK:(b + 1) * BLK]
                sufs[b] = carry + _scan_dot(blk, after)
                carry = carry + jnp.sum(blk, axis=1, keepdims=True)
            w = jnp.exp(zs - jnp.concatenate(sufs, axis=1))
            if valid is not None:
                w = jnp.where(valid, w, 0.0)
            return carry, acc + _dot(w.astype(BF16), v_ref[ks, :])

        def qblock(qi, _):
            qs = pl.ds(pl.multiple_of(qi * BLK, BLK), BLK)
            q = q_ref[qs, :] * SCALE
            kd = qi // nb
            limit = (qi - kd * nb) * BLK
            qh = jnp.concatenate([jnp.where(m, q, jnp.zeros_like(q)) for m in hm], axis=0)
            c0 = tile(qh, kd, jnp.zeros((2 * BLK, 1), F32), jnp.zeros((2 * BLK, LANES), F32), limit)
            carry, acc = lax.fori_loop(0, kd, lambda n, c: tile(qh, kd - 1 - n, c[0], c[1], None), c0)
            o_ref[qs, :] = jnp.where(hm[0], acc[:BLK], acc[BLK:])
            for h in range(2):
                tot_ref[h, qs, :] = jnp.broadcast_to(carry[h * BLK:(h + 1) * BLK], (BLK, LANES))
            return 0

        lax.fori_loop(0, nq, qblock, 0)

    col_blk = lambda off: pl.BlockSpec((t, LANES), lambda p: (0, off + p))
    return _call(
        body, name="sb_fwd", grid=(4,), in_specs=[col_blk(0), col_blk(4), col_blk(8)],
        out_specs=[pl.BlockSpec((t, LANES), lambda p: (0, p)), pl.BlockSpec((2, t, LANES), lambda p: (p, 0, 0))],
        out_shape=[jax.ShapeDtypeStruct((t, SB_W), F32), jax.ShapeDtypeStruct((8, t, LANES), F32)],
        compiler_params=_params(1))(proj, proj, proj)


def _sb_bwd(proj, d_o, tot):
    t = proj.shape[0]
    nq = t // BLK
    nb = SB_KT // BLK

    def body(q_ref, k_ref, v_ref, do_ref, tot_ref, dq_ref, dk_ref, dv_ref, dk_acc, dv_acc):
        hm = _head_masks()
        dcol = _sb_dcol()
        before = _tri(lambda r, c: r < c)
        upto = _tri(lambda r, c: r <= c)
        dk_acc[...] = jnp.zeros_like(dk_acc)
        dv_acc[...] = jnp.zeros_like(dv_acc)

        def tile(qh, doh, tt, kt, pre, ecum, dq, limit):
            ks = pl.ds(pl.multiple_of(kt * SB_KT, SB_KT), SB_KT)
            k = k_ref[ks, :]
            v = v_ref[ks, :]
            z = _dot_nt(qh, k)
            sp, zs = _softplus(z)
            valid = None if limit is None else dcol < limit
            spm = sp if valid is None else jnp.where(valid, sp, 0.0)
            pres = []
            for b in range(nb):
                blk = spm[:, b * BLK:(b + 1) * BLK]
                pres.append(pre + _scan_dot(blk, before))
                pre = pre + jnp.sum(blk, axis=1, keepdims=True)
            logw = z - (tt - jnp.concatenate(pres, axis=1))
            if valid is not None:
                logw = jnp.minimum(logw, 0.0)
            w = jnp.exp(logw)
            if valid is not None:
                w = jnp.where(valid, w, 0.0)
            e = w * _dot_nt(doh, v)
            incs = []
            for b in range(nb):
                blk = e[:, b * BLK:(b + 1) * BLK]
                incs.append(ecum + _scan_dot(blk, upto))
                ecum = ecum + jnp.sum(blk, axis=1, keepdims=True)
            dz = e - jnp.exp(zs) * jnp.concatenate(incs, axis=1)
            if valid is not None:
                dz = jnp.where(valid, dz, 0.0)
            dzb = dz.astype(BF16)
            dk_acc[ks, :] += _dot_tn(dzb, qh)
            dv_acc[ks, :] += _dot_tn(w.astype(BF16), doh)
            return pre, ecum, dq + _dot(dzb, k)

        def qblock(qi, _):
            qs = pl.ds(pl.multiple_of(qi * BLK, BLK), BLK)
            q = q_ref[qs, :] * SCALE
            do = do_ref[qs, :]
            kd = qi // nb
            limit = (qi - kd * nb) * BLK
            qh = jnp.concatenate([jnp.where(m, q, jnp.zeros_like(q)) for m in hm], axis=0)
            doh = jnp.concatenate([jnp.where(m, do, jnp.zeros_like(do)) for m in hm], axis=0)
            tt = jnp.concatenate([tot_ref[h, qs, 0:1] for h in range(2)], axis=0)
            c0 = (jnp.zeros((2 * BLK, 1), F32), jnp.zeros((2 * BLK, 1), F32), jnp.zeros((2 * BLK, LANES), F32))
            c = lax.fori_loop(0, kd, lambda kt, c: tile(qh, doh, tt, kt, c[0], c[1], c[2], None), c0)
            dq = tile(qh, doh, tt, kd, c[0], c[1], c[2], limit)[2]
            dq_ref[qs, :] = (jnp.where(hm[0], dq[:BLK], dq[BLK:]) * SCALE).astype(BF16)
            return 0

        lax.fori_loop(0, nq, qblock, 0)
        dk_ref[...] = dk_acc[...].astype(BF16)
        dv_ref[...] = dv_acc[...].astype(BF16)

    col_blk = lambda off: pl.BlockSpec((t, LANES), lambda p: (0, off + p))
    out = jax.ShapeDtypeStruct((t, SB_W), BF16)
    return _call(
        body, name="sb_bwd", grid=(4,),
        in_specs=[col_blk(0), col_blk(4), col_blk(8), col_blk(0), pl.BlockSpec((2, t, LANES), lambda p: (p, 0, 0))],
        out_specs=[col_blk(0), col_blk(0), col_blk(0)], out_shape=[out, out, out],
        scratch_shapes=[pltpu.VMEM((t, LANES), F32), pltpu.VMEM((t, LANES), F32)],
        compiler_params=_params(1))(proj, proj, proj, d_o, tot)


def _bucket_table():
    a = np.arange(BLK)[:, None]
    c = np.arange(2 * BLK)[None, :]
    dist = np.maximum(BLK + a - c, 0)
    max_exact = N_BUCKETS // 2
    dd = np.maximum(dist, 1).astype(np.float32)
    large = max_exact + (np.log(dd / max_exact) / math.log(MAX_DISTANCE / max_exact)
                         * (N_BUCKETS - max_exact)).astype(np.int32)
    large = np.minimum(large, N_BUCKETS - 1)
    return np.where(dist < max_exact, dist, large).astype(np.int32)


def _swa_band_masks():
    row = lax.broadcasted_iota(jnp.int32, (SWA_G * BLK, 2 * BLK), 0) & (BLK - 1)
    col = lax.broadcasted_iota(jnp.int32, (SWA_G * BLK, 2 * BLK), 1)
    own = lax.broadcasted_iota(jnp.int32, (SWA_G * BLK, BLK), 1) <= (
        lax.broadcasted_iota(jnp.int32, (SWA_G * BLK, BLK), 0) & (BLK - 1))
    return (col > row) & ((col < BLK) | (col - BLK <= row)), own


def _swa_stack(ref, qs, kvh, kvmask, scale):
    parts = []
    for g in range(SWA_G):
        hq = SWA_G * kvh + g
        x = ref[qs, (hq // 2) * LANES:(hq // 2 + 1) * LANES].astype(F32)
        if hq % 2 != kvh:
            x = pltpu.roll(x, HEAD_DIM, 1)
        parts.append(jnp.where(kvmask, x * scale, 0.0).astype(BF16))
    return jnp.concatenate(parts, axis=0)


def _swa_unstack(x4, kvh, hm):
    heads = []
    for g in range(SWA_G):
        x = x4[g * BLK:(g + 1) * BLK]
        heads.append(pltpu.roll(x, HEAD_DIM, 1) if g % 2 != kvh else x)
    return [jnp.where(hm[0], heads[0], heads[1]), jnp.where(hm[0], heads[2], heads[3])]


def _swa_scores(q4, kb, bias_ref, kvh, mask, cols):
    bias4 = jnp.concatenate([bias_ref[SWA_G * kvh + g, :, cols] for g in range(SWA_G)], axis=0)
    return jnp.where(mask, _dot_nt(q4, kb) + bias4, NEG_INF)


def _swa_sinks(sink_ref, kvh):
    return jnp.concatenate([jnp.broadcast_to(sink_ref[SWA_G * kvh + g:SWA_G * kvh + g + 1, 0:1], (BLK, 1))
                            for g in range(SWA_G)], axis=0)


def _swa_fwd(proj, bias, sinks_b):
    t = proj.shape[0]
    nq = t // BLK

    def body(q_ref, k_ref, v_ref, bias_ref, sink_ref, o_ref, lse_ref):
        hm = _head_masks()
        band, own = _swa_band_masks()

        def qblock(i, kvh, prev):
            qs = pl.ds(pl.multiple_of(i * BLK, BLK), BLK)
            if prev:
                ks, mask, cols = pl.ds(pl.multiple_of((i - 1) * BLK, BLK), 2 * BLK), band, slice(None)
            else:
                ks, mask, cols = qs, own, slice(BLK, None)
            q4 = _swa_stack(q_ref, qs, kvh, hm[kvh], SCALE)
            sink4 = _swa_sinks(sink_ref, kvh)
            s = _swa_scores(q4, k_ref[ks, :], bias_ref, kvh, mask, cols)
            m = jnp.maximum(jnp.max(s, axis=1, keepdims=True), sink4)
            p = jnp.exp(s - m)
            den = jnp.sum(p, axis=1, keepdims=True) + jnp.exp(sink4 - m)
            o4 = _dot((p * (1.0 / den)).astype(BF16), v_ref[ks, :])
            lse4 = m + jnp.log(den)
            for g in range(SWA_G):
                lse_ref[SWA_G * kvh + g, qs, :] = jnp.broadcast_to(lse4[g * BLK:(g + 1) * BLK], (BLK, LANES))
            for pp, o in enumerate(_swa_unstack(o4, kvh, hm)):
                o_ref[qs, (2 * kvh + pp) * LANES:(2 * kvh + pp + 1) * LANES] = o

        for kvh in range(2):
            qblock(0, kvh, False)

            def step(i, _):
                qblock(i, kvh, True)
                return 0

            lax.fori_loop(1, nq, step, 0)

    return _call(
        body, name="swa_fwd", grid=(1,),
        in_specs=[pl.BlockSpec((t, SWA_W), lambda i: (0, 3)), pl.BlockSpec((t, KV_W), lambda i: (0, 16)),
                  pl.BlockSpec((t, KV_W), lambda i: (0, 17)), pl.BlockSpec((8, BLK, 2 * BLK), lambda i: (0, 0, 0)),
                  pl.BlockSpec((8, LANES), lambda i: (0, 0))],
        out_specs=[pl.BlockSpec((t, SWA_W), lambda i: (0, 0)), pl.BlockSpec((8, t, LANES), lambda i: (0, 0, 0))],
        out_shape=[jax.ShapeDtypeStruct((t, SWA_W), F32), jax.ShapeDtypeStruct((8, t, LANES), F32)],
        compiler_params=_params(1))(proj, proj, proj, bias, sinks_b)


def _swa_bwd(proj, d_o, lse, bias, sinks_b, dbias_in):
    t = proj.shape[0]
    nq = t // BLK

    def body(q_ref, k_ref, v_ref, do_ref, lse_ref, bias_ref, sink_ref, dbi_ref,
             dq_ref, dk_ref, dv_ref, dsink_ref, dbias_ref, dk_acc, dv_acc):
        hm = _head_masks()
        band, own = _swa_band_masks()
        dk_acc[...] = jnp.zeros_like(dk_acc)
        dv_acc[...] = jnp.zeros_like(dv_acc)
        dbias_ref[...] = dbi_ref[...]

        def qblock(i, kvh, prev, dsink4):
            qs = pl.ds(pl.multiple_of(i * BLK, BLK), BLK)
            if prev:
                ks, mask, cols = pl.ds(pl.multiple_of((i - 1) * BLK, BLK), 2 * BLK), band, slice(None)
            else:
                ks, mask, cols = qs, own, slice(BLK, None)
            q4 = _swa_stack(q_ref, qs, kvh, hm[kvh], SCALE)
            do4 = _swa_stack(do_ref, qs, kvh, hm[kvh], 1.0)
            sink4 = _swa_sinks(sink_ref, kvh)
            lse4 = jnp.concatenate([lse_ref[SWA_G * kvh + g, qs, 0:1] for g in range(SWA_G)], axis=0)
            kb = k_ref[ks, :]
            p = jnp.exp(_swa_scores(q4, kb, bias_ref, kvh, mask, cols) - lse4)
            dp = _dot_nt(do4, v_ref[ks, :])
            delta = jnp.sum(p * dp, axis=1, keepdims=True)
            ds = p * (dp - delta)
            for g in range(SWA_G):
                dbias_ref[SWA_G * kvh + g, :, cols] += ds[g * BLK:(g + 1) * BLK]
            dsb = ds.astype(BF16)
            dk_acc[ks, :] += _dot_tn(dsb, q4)
            dv_acc[ks, :] += _dot_tn(p.astype(BF16), do4)
            for pp, dq in enumerate(_swa_unstack(_dot(dsb, kb) * SCALE, kvh, hm)):
                dq_ref[qs, (2 * kvh + pp) * LANES:(2 * kvh + pp + 1) * LANES] = dq.astype(BF16)
            return dsink4 - jnp.exp(sink4 - lse4) * delta

        for kvh in range(2):
            ds0 = qblock(0, kvh, False, jnp.zeros((SWA_G * BLK, 1), F32))
            ds4 = lax.fori_loop(1, nq, lambda i, c: qblock(i, kvh, True, c), ds0)
            for g in range(SWA_G):
                hq = SWA_G * kvh + g
                dsink_ref[hq:hq + 1, :] = jnp.broadcast_to(
                    jnp.sum(ds4[g * BLK:(g + 1) * BLK], axis=0, keepdims=True), (1, LANES))

        dk_ref[...] = dk_acc[...].astype(BF16)
        dv_ref[...] = dv_acc[...].astype(BF16)

    full3 = pl.BlockSpec((8, BLK, 2 * BLK), lambda i: (0, 0, 0))
    kv = jax.ShapeDtypeStruct((t, KV_W), BF16)
    return _call(
        body, name="swa_bwd", grid=(1,),
        in_specs=[pl.BlockSpec((t, SWA_W), lambda i: (0, 3)), pl.BlockSpec((t, KV_W), lambda i: (0, 16)),
                  pl.BlockSpec((t, KV_W), lambda i: (0, 17)), pl.BlockSpec((t, SWA_W), lambda i: (0, 1)),
                  pl.BlockSpec((8, t, LANES), lambda i: (0, 0, 0)), full3, pl.BlockSpec((8, LANES), lambda i: (0, 0)),
                  full3],
        out_specs=[pl.BlockSpec((t, SWA_W), lambda i: (0, 0)), pl.BlockSpec((t, KV_W), lambda i: (0, 0)),
                   pl.BlockSpec((t, KV_W), lambda i: (0, 0)), pl.BlockSpec((8, LANES), lambda i: (0, 0)), full3],
        out_shape=[jax.ShapeDtypeStruct((t, SWA_W), BF16), kv, kv, jax.ShapeDtypeStruct((8, LANES), F32),
                   jax.ShapeDtypeStruct((8, BLK, 2 * BLK), F32)],
        scratch_shapes=[pltpu.VMEM((t, KV_W), F32), pltpu.VMEM((t, KV_W), F32)],
        compiler_params=_params(1))(proj, proj, proj, d_o, lse, bias, sinks_b, dbias_in)


def _bias_table(rel_bias, buckets):
    def body(rb_ref, b_ref, o_ref):
        bk = b_ref[...]
        for h in range(8):
            acc = jnp.zeros((BLK, 2 * BLK), F32)
            for b in range(N_BUCKETS):
                acc = jnp.where(bk == b, rb_ref[b, h], acc)
            o_ref[h] = acc

    return _call(
        body, name="bias_table", grid=(1,),
        in_specs=[pl.BlockSpec(memory_space=pltpu.SMEM), pl.BlockSpec((BLK, 2 * BLK), lambda i: (0, 0))],
        out_specs=pl.BlockSpec((8, BLK, 2 * BLK), lambda i: (0, 0, 0)),
        out_shape=jax.ShapeDtypeStruct((8, BLK, 2 * BLK), F32), compiler_params=_params(1))(rel_bias, buckets)


def _bias_grad(dbias, buckets):
    def body(d_ref, b_ref, o_ref):
        lane = lax.broadcasted_iota(jnp.int32, (1, LANES), 1)
        bk = b_ref[...]
        for h in range(8):
            d = d_ref[h]
            acc = jnp.zeros((1, LANES), F32)
            for b in range(N_BUCKETS):
                s = jnp.sum(jnp.sum(jnp.where(bk == b, d, 0.0), axis=0, keepdims=True), axis=1, keepdims=True)
                acc = acc + jnp.where(lane == b, s, 0.0)
            o_ref[h:h + 1, :] = acc

    return _call(
        body, name="bias_grad", grid=(1,),
        in_specs=[pl.BlockSpec((8, BLK, 2 * BLK), lambda i: (0, 0, 0)), pl.BlockSpec((BLK, 2 * BLK), lambda i: (0, 0))],
        out_specs=pl.BlockSpec((8, LANES), lambda i: (0, 0)),
        out_shape=jax.ShapeDtypeStruct((8, LANES), F32), compiler_params=_params(1))(dbias, buckets)


def _row(a):
    return a.reshape(1, -1)


def _fwd_ffn1(h, n1, w, small, l):
    s = {"h0": h, "n1": n1}
    s["gu1"], s["act1"] = _ffn_gu(n1, w["ffn1_gu"])
    s["h1"], s["nm"] = _down_res(s["act1"], w["ffn1_down"], h, _row(small["norm_mix"][l]))
    return s


def _fwd_proj_sb(s, w):
    s["proj"] = _proj(s["nm"], w["w_in"])
    s["o_sb"], s["tot"] = _sb_fwd(s["proj"])


def _fwd_swa(s, small, l, bias):
    s["sinks_b"] = jnp.broadcast_to(small["sinks"][l][:, None], (8, LANES))
    s["o_sw"], s["lse"] = _swa_fwd(s["proj"], bias, s["sinks_b"])


def _fwd_out_ffn2(s, w, small, l, g_after):
    s["h2"], s["mixed"], s["n2"] = _out_res(
        s["o_sb"], s["o_sw"], _row(small["norm_out_sb"][l]), _row(small["norm_out_swa"][l]), w["w_out"], s["h1"],
        _row(small["norm_ffn2"][l]))
    s["gu2"], s["act2"] = _ffn_gu(s["n2"], w["ffn2_gu"])
    return _down_res(s["act2"], w["ffn2_down"], s["h2"], g_after)


def _bwd_ffn_dact(dh, s, w, which):
    return _ffn_dact(dh, w[f"ffn{which}_down"], s[f"gu{which}"])


def _bwd_ffn_rest(dh, dgu, s, w, small, l, which):
    h_in, norm = (s["h0"], "norm_ffn1") if which == 1 else (s["h2"], "norm_ffn2")
    g_down = _wgrad_down(s[f"act{which}"], dh)
    g_gu = _wgrad_gu(s[f"n{which}"], dgu)
    dh, dg = _ffn_dn(dgu, w[f"ffn{which}_gu"], dh, h_in, _row(small[norm][l]))
    return dh, {f"ffn{which}_down": g_down, f"ffn{which}_gu": g_gu}, {norm: dg}


def _bwd_ffn(dh, s, w, small, l, which):
    return _bwd_ffn_rest(dh, _bwd_ffn_dact(dh, s, w, which), s, w, small, l, which)


def _bwd_mix(dh, s, w, small, l, bias, dbias):
    g_out = _wgrad_out(s["mixed"], dh)
    d_o, dg_sb, dg_sw = _dmixed(dh, w["w_out"], s["o_sb"], s["o_sw"], _row(small["norm_out_sb"][l]),
                                _row(small["norm_out_swa"][l]))
    dq_sb, dk_sb, dv_sb = _sb_bwd(s["proj"], d_o, s["tot"])
    dq_sw, dk_sw, dv_sw, dsink, dbias = _swa_bwd(s["proj"], d_o, s["lse"], bias, s["sinks_b"], dbias)
    dproj = jnp.concatenate([dq_sb, dk_sb, dv_sb, dq_sw, dk_sw, dv_sw], axis=1)
    g_in = _wgrad_in(s["nm"], dproj)
    dh, dg_mix = _mix_dn(dproj, w["w_in"], dh, s["h1"], _row(small["norm_mix"][l]))
    gs = {"norm_out_sb": dg_sb, "norm_out_swa": dg_sw, "sinks": dsink[:, 0], "norm_mix": dg_mix}
    return dh, {"w_out": g_out, "w_in": g_in}, gs, dbias


def _place():
    x, y, c = lax.axis_index("x"), lax.axis_index("y"), lax.axis_index("c")
    return x, y, c, 2 * x + y


def _chip_core(k, c):
    return (k // 2, k % 2, c)


def _rows_per_block(rows, cols, copies):
    best = 16
    for tr in range(16, rows + 1, 16):
        if rows % tr == 0 and copies * tr * cols * 4 <= 6 * 2 ** 20:
            best = tr
    assert rows % best == 0
    return best


def _place_own(w, l, me1):
    _, rows, cols = w.shape
    tr = _rows_per_block(rows, cols, 1)

    def body(me_ref, w_ref, o_ref):
        o_ref[...] = w_ref[...].astype(BF16)

    return _call(
        body, name="place_own",
        num_scalar_prefetch=1, grid=(rows // tr,),
        in_specs=[pl.BlockSpec((None, tr, cols), lambda r, me: (l, r, 0))],
        out_specs=pl.BlockSpec((None, tr, cols), lambda r, me: (me[0], r, 0)),
        out_shape=jax.ShapeDtypeStruct((N_CHIPS, rows, cols), BF16), compiler_params=_params(1))(me1, w)


def _plan_gather_ici(bufs):
    _, _, c, me = _place()
    return [(b.at[me, c], b.at[me, c], b.at[(me + 3 - j) % N_CHIPS, c], _chip_core((me + 1 + j) % N_CHIPS, c))
            for b in bufs for j in range(3)]


def _plan_gather_d2d(bufs):
    x, y, c, me = _place()
    return [(b.at[(me + 3 - j) % N_CHIPS, c], b.at[(me + 3 - j) % N_CHIPS, c], b.at[(me + 3 - j) % N_CHIPS, 1 - c],
             (x, y, 1 - c)) for b in bufs for j in range(3)]


def _plan_grad_sibling(bufs):
    x, y, c, _ = _place()
    n = len(bufs) // 2
    return [(g.at[:, 1 - c], z, z, (x, y, 1 - c)) for g, z in zip(bufs[:n], bufs[n:])]


def _plan_grad_chips(bufs):
    _, _, c, me = _place()
    n = len(bufs) // 2
    return [(p.at[(me + 1 + j) % N_CHIPS], z.at[j], z.at[j], _chip_core((me + 1 + j) % N_CHIPS, c))
            for p, z in zip(bufs[:n], bufs[n:]) for j in range(3)]


def _plan_grad_halves(bufs):
    x, y, c, _ = _place()
    return [(b.at[c], b.at[c], b.at[1 - c], (x, y, 1 - c)) for b in bufs]


def _remote(src, dst, send_sem, recv_sem, to):
    return pltpu.make_async_remote_copy(src_ref=src, dst_ref=dst, send_sem=send_sem, recv_sem=recv_sem,
                                        device_id=to, device_id_type=MESH)


def _exchange_start(name, plan, bufs, n_copies):
    n = len(bufs)

    def body(*refs):
        ins = refs[:n]
        ssem, rsem = refs[n], refs[n + 1]
        token = refs[-1]
        for i, (src, dst, _, to) in enumerate(plan(ins)):
            _remote(src, dst, ssem.at[i], rsem.at[i], to).start()
        token[...] = jnp.zeros_like(token)

    out = _call(
        body, name=name,
        out_shape=(pltpu.SemaphoreType.DMA((n_copies,)), pltpu.SemaphoreType.DMA((n_copies,)),
                   *[pltpu.HBM(a.shape, a.dtype) for a in bufs], jax.ShapeDtypeStruct((8, LANES), F32)),
        in_specs=[HBM] * n, out_specs=(SEM, SEM, *[HBM] * n, pl.BlockSpec(memory_space=pltpu.VMEM)),
        input_output_aliases={t: 2 + t for t in range(n)}, hbm_args=n,
        compiler_params=pltpu.CompilerParams(has_side_effects=EFFECT),
    )(*bufs)
    return (out[0], out[1]), list(out[2:2 + n])


def _exchange_wait(name, plan, bufs, sems):
    n = len(bufs)

    def body(*refs):
        ins = refs[:n]
        ssem, rsem = refs[n], refs[n + 1]
        for i, (src, dst, land, to) in enumerate(plan(ins)):
            _remote(src, dst, ssem.at[i], rsem.at[i], to).wait_send()
            _remote(land, land, ssem.at[i], rsem.at[i], to).wait_recv()

    return list(_call(
        body, name=name, out_shape=[pltpu.HBM(a.shape, a.dtype) for a in bufs],
        in_specs=[HBM] * n + [SEM, SEM], out_specs=[HBM] * n,
        input_output_aliases={t: t for t in range(n)},
        compiler_params=pltpu.CompilerParams(has_side_effects=EFFECT),
    )(*bufs, sems[0], sems[1]))


def _gather_now(bufs):
    n = len(bufs)
    n_cp = 3 * n

    def body(*refs):
        outs = refs[n:2 * n]
        ici_s, ici_r, d2d_s, d2d_r = refs[2 * n:]
        first = _plan_gather_ici(outs)
        second = _plan_gather_d2d(outs)
        for i, (src, dst, _, to) in enumerate(first):
            _remote(src, dst, ici_s.at[i], ici_r.at[i], to).start()
        for i, (src, dst, _, to) in enumerate(second):
            land = first[i][2]
            _remote(land, land, ici_s.at[i], ici_r.at[i], to).wait_recv()
            _remote(src, dst, d2d_s.at[i], d2d_r.at[i], to).start()
        for i, (_, _, land, to) in enumerate(second):
            _remote(land, land, d2d_s.at[i], d2d_r.at[i], to).wait_recv()
        for i in range(n_cp):
            _remote(first[i][0], first[i][1], ici_s.at[i], ici_r.at[i], first[i][3]).wait_send()
            _remote(second[i][0], second[i][1], d2d_s.at[i], d2d_r.at[i], second[i][3]).wait_send()

    return _call(
        body, name="gather_layer0", in_specs=[ANY] * n, out_specs=[ANY] * n,
        out_shape=[jax.ShapeDtypeStruct(a.shape, a.dtype) for a in bufs],
        input_output_aliases={t: t for t in range(n)},
        scratch_shapes=[pltpu.SemaphoreType.DMA((n_cp,))] * 4,
        compiler_params=pltpu.CompilerParams(vmem_limit_bytes=V7X_VMEM_LIMIT))(*bufs)


def _chip_sum(g, xbuf, cm):
    _, _, r2, cols = g.shape
    tr = _rows_per_block(r2, cols, N_CHIPS)

    def body(cm_ref, g_ref, x_ref, pb_ref, po_ref):
        pb_ref[...] = (g_ref[...] + x_ref[...]).astype(BF16)
        me = cm_ref[1]
        po_ref[...] = g_ref[me] + x_ref[me]

    return _call(
        body, name="grad_chip_sum",
        num_scalar_prefetch=1, grid=(r2 // tr,),
        in_specs=[pl.BlockSpec((N_CHIPS, None, tr, cols), lambda r, cm: (0, cm[0], r, 0)),
                  pl.BlockSpec((N_CHIPS, tr, cols), lambda r, cm: (0, r, 0))],
        out_specs=[pl.BlockSpec((N_CHIPS, tr, cols), lambda r, cm: (0, r, 0)),
                   pl.BlockSpec((tr, cols), lambda r, cm: (r, 0))],
        out_shape=[jax.ShapeDtypeStruct((N_CHIPS, r2, cols), BF16), jax.ShapeDtypeStruct((r2, cols), F32)],
        compiler_params=_params(1))(cm, g, xbuf)


def _total_sum(pown, rbuf, cm):
    r2, cols = pown.shape
    tr = _rows_per_block(r2, cols, 3)

    def body(cm_ref, p_ref, r_ref, o_ref):
        acc = p_ref[...]
        for j in range(3):
            acc = acc + r_ref[j].astype(F32)
        o_ref[...] = acc

    return _call(
        body, name="grad_total_sum",
        num_scalar_prefetch=1, grid=(r2 // tr,),
        in_specs=[pl.BlockSpec((tr, cols), lambda r, cm: (r, 0)),
                  pl.BlockSpec((3, tr, cols), lambda r, cm: (0, r, 0))],
        out_specs=pl.BlockSpec((None, tr, cols), lambda r, cm: (cm[0], r, 0)),
        out_shape=jax.ShapeDtypeStruct((2, r2, cols), F32), compiler_params=_params(1))(cm, pown, rbuf)


def _small_allreduce(v):
    rows = v.shape[0]
    n_dev = 2 * N_CHIPS

    def body(v_ref, o_ref, buf, ssem, rsem):
        x, y, c, _ = _place()
        me = 4 * x + 2 * y + c
        buf[me] = v_ref[...]

        def copy(d, slot, to):
            return _remote(v_ref, buf.at[slot], ssem.at[d - 1], rsem.at[d - 1], (to // 4, (to // 2) % 2, to % 2))

        cps = [copy(d, me, (me + d) % n_dev) for d in range(1, n_dev)]
        for cp in cps:
            cp.start()
        for d in range(1, n_dev):
            copy(d, (me + n_dev - d) % n_dev, me).wait_recv()
        for cp in cps:
            cp.wait_send()
        acc = buf[0]
        for i in range(1, n_dev):
            acc = acc + buf[i]
        o_ref[...] = acc

    vm = pl.BlockSpec(memory_space=pltpu.VMEM)
    return _call(
        body, name="small_allreduce", in_specs=[vm], out_specs=vm,
        out_shape=jax.ShapeDtypeStruct(v.shape, F32),
        scratch_shapes=[pltpu.VMEM((n_dev, rows, LANES), F32), pltpu.SemaphoreType.DMA((n_dev - 1,)),
                        pltpu.SemaphoreType.DMA((n_dev - 1,))],
        compiler_params=pltpu.CompilerParams(vmem_limit_bytes=V7X_VMEM_LIMIT))(v)


def _adamw_math(w, g, m, v):
    m2 = ADAM_B1 * m + (1.0 - ADAM_B1) * g
    v2 = ADAM_B2 * v + (1.0 - ADAM_B2) * (g * g)
    m_hat = m2 / (1.0 - ADAM_B1 ** ADAM_STEP)
    v_hat = v2 / (1.0 - ADAM_B2 ** ADAM_STEP)
    return -ADAM_LR * (m_hat / (jnp.sqrt(v_hat) + ADAM_EPS) + ADAM_WD * w), m2, v2


def _adamw_layer(w, g, m, v, l, prev):
    _, rows, cols = w.shape
    tr = rows
    for cand in range(8, rows + 1, 8):
        if rows % cand == 0 and cand * cols * 4 <= 2 ** 21:
            tr = cand

    def body(w_ref, g_ref, m_ref, v_ref, *outs):
        go_ref, d_ref, m2_ref, v2_ref = outs[-4:]
        g = g_ref[...]
        go_ref[...] = g
        d_ref[...], m2_ref[...], v2_ref[...] = _adamw_math(w_ref[...], g, m_ref[...], v_ref[...])

    stack = pl.BlockSpec((None, tr, cols), lambda i: (l, i, 0))
    ins, specs, alias = [w, g, m, v], [stack, pl.BlockSpec((tr, cols), lambda i: (i, 0)), stack, stack], {}
    if prev is not None:
        ins += list(prev)
        specs += [ANY] * 4
        alias = {4 + i: i for i in range(4)}
    return _call(
        body, name="adamw", grid=(rows // tr,), in_specs=specs, out_specs=[stack] * 4,
        out_shape=[jax.ShapeDtypeStruct(w.shape, F32)] * 4, input_output_aliases=alias,
        compiler_params=_params(1))(*ins)


def _adamw_small(w, g, m, v):
    def body(w_ref, g_ref, m_ref, v_ref, d_ref, m2_ref, v2_ref):
        d_ref[...], m2_ref[...], v2_ref[...] = _adamw_math(w_ref[...], g_ref[...], m_ref[...], v_ref[...])

    spec = pl.BlockSpec(w.shape, lambda i: (0, 0))
    return _call(
        body, name="adamw_small", grid=(1,), in_specs=[spec] * 4, out_specs=[spec] * 3,
        out_shape=[jax.ShapeDtypeStruct(w.shape, F32)] * 3, compiler_params=_params(1))(w, g, m, v)


SMALL = ("norm_ffn1", "norm_mix", "sinks", "norm_out_sb", "norm_out_swa", "norm_ffn2", "rel_bias", "norm_final")
BIG = ("ffn1_gu", "ffn1_down", "w_in", "w_out", "ffn2_gu", "ffn2_down")


def _pack(parts):
    rows = []
    for a in parts:
        a = a.reshape(-1).astype(F32)
        rows.append(jnp.pad(a, (0, -a.shape[0] % LANES)).reshape(-1, LANES))
    out = jnp.concatenate(rows, axis=0)
    return jnp.pad(out, ((0, -out.shape[0] % 8), (0, 0)))


def _unpack(packed, like):
    out, r = [], 0
    for a in like:
        n = math.prod(a.shape)
        nr = -(-n // LANES)
        out.append(packed[r:r + nr].reshape(-1)[:n].reshape(a.shape))
        r += nr
    return out


def _halved(a):
    k, r, cols = a.shape
    return a.reshape(k, 2, r // 2, cols)


def _weight_view(k, buf):
    full = buf.reshape(N_CHIPS, buf.shape[2] * 2, buf.shape[3])
    if k.endswith("_gu"):
        return full
    if k == "w_in":
        return jnp.transpose(full, (1, 0, 2)).reshape(D_MODEL, IN_W)
    return full.reshape(-1, D_MODEL)


def _grad_stack(k, g):
    if k == "w_in":
        g = jnp.transpose(g.reshape(D_MODEL, N_CHIPS, IN_W // N_CHIPS), (1, 0, 2))
    elif not k.endswith("_gu"):
        g = g.reshape(N_CHIPS, g.shape[0] // N_CHIPS, D_MODEL)
    return _halved(g)


def _empty_like_hbm(shape, dtype):
    return pltpu.with_memory_space_constraint(lax.empty(shape, dtype), pltpu.HBM)


def kernel(x, norm_ffn1, w_ffn1_gu, w_ffn1_down, norm_mix, w_in, sinks, norm_out_sb, norm_out_swa, w_out, norm_ffn2, w_ffn2_gu, w_ffn2_down, rel_bias, norm_final, loss_target, m_norm_ffn1, m_w_ffn1_gu, m_w_ffn1_down, m_norm_mix, m_w_in, m_sinks, m_norm_out_sb, m_norm_out_swa, m_w_out, m_norm_ffn2, m_w_ffn2_gu, m_w_ffn2_down, m_rel_bias, m_norm_final, v_norm_ffn1, v_w_ffn1_gu, v_w_ffn1_down, v_norm_mix, v_w_in, v_sinks, v_norm_out_sb, v_norm_out_swa, v_w_out, v_norm_ffn2, v_w_ffn2_gu, v_w_ffn2_down, v_rel_bias, v_norm_final):
    big_w = dict(ffn1_gu=w_ffn1_gu, ffn1_down=w_ffn1_down, w_in=w_in, w_out=w_out, ffn2_gu=w_ffn2_gu, ffn2_down=w_ffn2_down)
    big_m = dict(ffn1_gu=m_w_ffn1_gu, ffn1_down=m_w_ffn1_down, w_in=m_w_in, w_out=m_w_out, ffn2_gu=m_w_ffn2_gu, ffn2_down=m_w_ffn2_down)
    big_v = dict(ffn1_gu=v_w_ffn1_gu, ffn1_down=v_w_ffn1_down, w_in=v_w_in, w_out=v_w_out, ffn2_gu=v_w_ffn2_gu, ffn2_down=v_w_ffn2_down)
    small = dict(norm_ffn1=norm_ffn1, norm_mix=norm_mix, sinks=sinks, norm_out_sb=norm_out_sb, norm_out_swa=norm_out_swa,
                 norm_ffn2=norm_ffn2, rel_bias=rel_bias, norm_final=norm_final)
    small_m = dict(norm_ffn1=m_norm_ffn1, norm_mix=m_norm_mix, sinks=m_sinks, norm_out_sb=m_norm_out_sb,
                   norm_out_swa=m_norm_out_swa, norm_ffn2=m_norm_ffn2, rel_bias=m_rel_bias, norm_final=m_norm_final)
    small_v = dict(norm_ffn1=v_norm_ffn1, norm_mix=v_norm_mix, sinks=v_sinks, norm_out_sb=v_norm_out_sb,
                   norm_out_swa=v_norm_out_swa, norm_ffn2=v_norm_ffn2, rel_bias=v_rel_bias, norm_final=v_norm_final)
    _PREVIOUS[0] = None
    _, _, c, me = _place()
    cm = jnp.stack([c, me]).astype(jnp.int32)
    buckets = jnp.asarray(_bucket_table())
    ffn1, mix_in, rest = ("ffn1_gu", "ffn1_down"), ("w_in",), ("w_out", "ffn2_gu", "ffn2_down")

    def place(l, keys):
        return [_halved(_place_own(big_w[k], l, cm[1:])) for k in keys]

    def views(keys, bufs):
        return {k: _weight_view(k, b) for k, b in zip(keys, bufs)}

    def gather_start(tag, bufs):
        return _exchange_start(f"gather{tag}_ici_start", _plan_gather_ici, bufs, 3 * len(bufs))

    def gather_pass(tag, flight):
        bufs = _exchange_wait(f"gather{tag}_ici_wait", _plan_gather_ici, flight[1], flight[0])
        return _exchange_start(f"gather{tag}_d2d_start", _plan_gather_d2d, bufs, 3 * len(bufs))

    def gather_done(tag, keys, flight):
        return views(keys, _exchange_wait(f"gather{tag}_d2d_wait", _plan_gather_d2d, flight[1], flight[0]))

    w0 = views(ffn1, _gather_now(place(0, ffn1)))
    fly_in0 = gather_start("0b", place(0, mix_in))
    fly_rest0 = gather_start("0c", place(0, rest))
    bias = _bias_table(rel_bias, buckets)
    fly_ffn1 = gather_start("1a", place(1, ffn1))
    fly_rest1 = gather_start("1b", place(1, mix_in + rest))

    s0 = _fwd_ffn1(x[0], _norm_cast(x[0], _row(norm_ffn1[0])), w0, small, 0)
    w0.update(gather_done("0b", mix_in, gather_pass("0b", fly_in0)))
    _fwd_proj_sb(s0, w0)
    fly_rest0 = gather_pass("0c", fly_rest0)
    _fwd_swa(s0, small, 0, bias)
    w0.update(gather_done("0c", rest, fly_rest0))
    h, n1 = _fwd_out_ffn2(s0, w0, small, 0, _row(norm_ffn1[1]))
    fly_ffn1 = gather_pass("1a", fly_ffn1)
    fly_rest1 = gather_pass("1b", fly_rest1)
    w1 = gather_done("1a", ffn1, fly_ffn1)
    s1 = _fwd_ffn1(h, n1, w1, small, 1)
    w1.update(gather_done("1b", mix_in + rest, fly_rest1))
    _fwd_proj_sb(s1, w1)
    _fwd_swa(s1, small, 1, bias)
    h, _ = _fwd_out_ffn2(s1, w1, small, 1, _row(norm_final))
    dh, dg_final, loss_row = _loss_head(h, _row(norm_final), loss_target[0])

    def landing(stacks, lead, dtype):
        return [_empty_like_hbm((lead,) + a.shape[2:], dtype) for a in stacks]

    def reduce_begin(tag, keys, gw):
        stacks = [_grad_stack(k, gw[k]) for k in keys]
        flight = _exchange_start(f"grad{tag}_sibling_start", _plan_grad_sibling,
                                 stacks + landing(stacks, N_CHIPS, F32), len(keys))
        return dict(tag=tag, keys=keys, stacks=stacks, flight=flight)

    def reduce_chips(st):
        n, (sems, bufs) = len(st["keys"]), st["flight"]
        bufs = _exchange_wait(f"grad{st['tag']}_sibling_wait", _plan_grad_sibling, bufs, sems)
        st["sums"] = [_chip_sum(g, z, cm) for g, z in zip(bufs[:n], bufs[n:])]
        st["flight"] = _exchange_start(f"grad{st['tag']}_chips_start", _plan_grad_chips,
                                       [s[0] for s in st["sums"]] + landing(st["stacks"], 3, BF16), 3 * n)

    def reduce_halves(st):
        n, (sems, bufs) = len(st["keys"]), st["flight"]
        bufs = _exchange_wait(f"grad{st['tag']}_chips_wait", _plan_grad_chips, bufs, sems)
        halves = [_total_sum(s[1], z, cm) for s, z in zip(st["sums"], bufs[n:])]
        st["flight"] = _exchange_start(f"grad{st['tag']}_halves_start", _plan_grad_halves, halves, n)

    def reduce_end(st):
        sems, bufs = st["flight"]
        bufs = _exchange_wait(f"grad{st['tag']}_halves_wait", _plan_grad_halves, bufs, sems)
        return {k: b.reshape(big_w[k].shape[1:]) for k, b in zip(st["keys"], bufs)}

    def adamw(reduced, l, prev):
        return {k: _adamw_layer(big_w[k], g, big_m[k], big_v[k], l, None if prev is None else prev[k])
                for k, g in reduced.items()}

    gsm = [dict() for _ in range(DEPTH)]
    dbias = jnp.zeros((8, BLK, 2 * BLK), F32)
    dh, gw1, gs = _bwd_ffn(dh, s1, w1, small, 1, 2)
    gsm[1].update(gs)
    dh, gw, gs, dbias = _bwd_mix(dh, s1, w1, small, 1, bias, dbias)
    gw1.update(gw)
    gsm[1].update(gs)
    dh, gw, gs = _bwd_ffn(dh, s1, w1, small, 1, 1)
    gw1.update(gw)
    gsm[1].update(gs)

    red1 = reduce_begin("1", BIG, gw1)
    dh, gw0, gs = _bwd_ffn(dh, s0, w0, small, 0, 2)
    gsm[0].update(gs)
    reduce_chips(red1)
    dh, gw, gs, dbias = _bwd_mix(dh, s0, w0, small, 0, bias, dbias)
    gw0.update(gw)
    gsm[0].update(gs)
    red0a = reduce_begin("0a", ("ffn2_gu", "ffn2_down", "w_out", "w_in"), gw0)
    reduce_halves(red1)
    dgu = _bwd_ffn_dact(dh, s0, w0, 1)
    reduce_chips(red0a)
    dh, gw, gs = _bwd_ffn_rest(dh, dgu, s0, w0, small, 0, 1)
    gsm[0].update(gs)
    red0b = reduce_begin("0b", ffn1, gw)
    reduced1 = reduce_end(red1)
    stacks = adamw({k: reduced1[k] for k in ffn1}, 1, None)

    gsmall = {k: jnp.stack([gsm[l][k].reshape(-1) for l in range(DEPTH)]) for k in gsm[0]}
    gsmall["rel_bias"] = jnp.transpose(_bias_grad(dbias, buckets)[:, :N_BUCKETS])
    gsmall["norm_final"] = dg_final.reshape(-1)
    small_like = [small[k] for k in SMALL]
    pk = lambda dct: _pack([dct[k] for k in SMALL])
    red = _small_allreduce(_pack([gsmall[k] for k in SMALL] + [loss_row[0, :1]]))
    gs = _unpack(red, small_like + [loss_row[0, :1]])
    loss = gs[-1][0]
    gs = dict(zip(SMALL, gs[:-1]))

    reduce_chips(red0b)
    stacks.update(adamw({k: reduced1[k] for k in mix_in + rest}, 1, None))
    dlt, m2, v2 = _adamw_small(pk(small), pk(gs), pk(small_m), pk(small_v))
    reduce_halves(red0a)
    stacks.update(adamw(reduce_end(red0a), 0, stacks))
    reduce_halves(red0b)
    stacks.update(adamw(reduce_end(red0b), 0, stacks))

    out_g, out_d, out_m, out_v = {}, {}, {}, {}
    for k in BIG:
        out_g[k], out_d[k], out_m[k], out_v[k] = stacks[k]
    for dst, packed in ((out_d, dlt), (out_m, m2), (out_v, v2)):
        dst.update(zip(SMALL, _unpack(packed, small_like)))
    out_g.update(gs)

    order = ("norm_ffn1", "ffn1_gu", "ffn1_down", "norm_mix", "w_in", "sinks", "norm_out_sb", "norm_out_swa", "w_out",
             "norm_ffn2", "ffn2_gu", "ffn2_down", "rel_bias", "norm_final")
    return (loss, dh.reshape(x.shape), *[out_g[k] for k in order], *[out_d[k] for k in order],
            *[out_m[k] for k in order], *[out_v[k] for k in order])
```

```python
import math

import numpy as np
import jax
import jax.numpy as jnp
from jax import lax
from jax.experimental import pallas as pl
from jax.experimental.pallas import tpu as pltpu

F32 = jnp.float32
BF16 = jnp.bfloat16

D_MODEL = 1024
DEPTH = 2
HEAD_DIM = 64
BLK = 128
N_BUCKETS = 32
MAX_DISTANCE = 128
D_FF = 2816
EPS = 1e-6
NEG_INF = -1e30
SB_W = 512
SWA_W = 512
KV_W = 128
IN_W = 2304
SCALE = HEAD_DIM ** -0.5
N_CHIPS = 4
FS = 2 * D_FF // N_CHIPS
LANES = 128
V7X_VMEM_LIMIT = 56 * 2 ** 20
TM = 512
SB_KT = 512
SWA_G = 4
SWA_UNROLL = 3

ADAM_LR = 0.001
ADAM_B1 = 0.9
ADAM_B2 = 0.999
ADAM_EPS = 1e-08
ADAM_WD = 0.01
ADAM_STEP = 10

MESH = pl.DeviceIdType.MESH
ANY = pl.BlockSpec(memory_space=pl.ANY)
HBM = pl.BlockSpec(memory_space=pltpu.HBM)
SEM = pl.BlockSpec(memory_space=pltpu.SEMAPHORE)
EFFECT = pltpu.SideEffectType.DATAFLOW_SIDE_EFFECTING


def _params(n_grid):
    return pltpu.CompilerParams(dimension_semantics=("arbitrary",) * n_grid, vmem_limit_bytes=V7X_VMEM_LIMIT)


_PREVIOUS = [None]


def _call(body, *, name, in_specs, out_specs, out_shape, grid=(), num_scalar_prefetch=0, scratch_shapes=(),
          input_output_aliases=None, compiler_params=None, hbm_args=0):
    n_in = len(in_specs)

    def run(*args):
        dep = _PREVIOUS[0]
        if any(dep is a for a in args):
            dep = None
        args = [pltpu.with_memory_space_constraint(a, pltpu.HBM) if i < hbm_args else a for i, a in enumerate(args)]
        specs = list(in_specs) + ([ANY] if dep is not None else [])
        k = num_scalar_prefetch + n_in
        fn = body if dep is None else (lambda *refs: body(*refs[:k], *refs[k + 1:]))
        if num_scalar_prefetch:
            shape = dict(grid_spec=pltpu.PrefetchScalarGridSpec(
                num_scalar_prefetch=num_scalar_prefetch, grid=grid, in_specs=specs, out_specs=out_specs,
                scratch_shapes=scratch_shapes))
        else:
            shape = dict(grid=grid, in_specs=specs, out_specs=out_specs, scratch_shapes=scratch_shapes)
        out = pl.pallas_call(fn, name=name, out_shape=out_shape, input_output_aliases=input_output_aliases or {},
                             compiler_params=compiler_params, **shape)(*args, *([] if dep is None else [dep]))
        _PREVIOUS[0] = jax.tree.leaves(out)[-1]
        return out

    return run


def _dot(a, b):
    return jnp.dot(a, b, preferred_element_type=F32)


def _dot_nt(a, b):
    return lax.dot_general(a, b, (((1,), (1,)), ((), ())), preferred_element_type=F32)


def _dot_tn(a, b):
    return lax.dot_general(a, b, (((0,), (0,)), ((), ())), preferred_element_type=F32)


def _rms_fwd(x, g):
    r = lax.rsqrt(jnp.mean(x * x, axis=-1, keepdims=True) + EPS)
    xh = x * r
    return xh * g, xh, r


def _rms_bwd(dy, xh, r, g):
    u = dy * g
    dx = r * (u - xh * jnp.mean(u * xh, axis=-1, keepdims=True))
    dg = jnp.sum(dy * xh, axis=0, keepdims=True)
    return dx, dg


def _softplus(z):
    neg_abs = lax.bitcast_convert_type(lax.bitcast_convert_type(z, jnp.int32) | jnp.int32(-2 ** 31), F32)
    sp = jnp.maximum(z, 0.0) + jnp.log(1.0 + jnp.exp(neg_abs))
    return sp, z - sp


def _norm_cast(h, g):
    t, w = h.shape

    def body(h_ref, g_ref, n_ref):
        y, _, _ = _rms_fwd(h_ref[...], g_ref[...])
        n_ref[...] = y.astype(BF16)

    return _call(
        body, name="norm_cast", grid=(t // TM,),
        in_specs=[pl.BlockSpec((TM, w), lambda i: (i, 0)), pl.BlockSpec((1, w), lambda i: (0, 0))],
        out_specs=pl.BlockSpec((TM, w), lambda i: (i, 0)),
        out_shape=jax.ShapeDtypeStruct((t, w), BF16), compiler_params=_params(1))(h, g)


def _ffn_gu(n, wgu):
    t, d = n.shape

    def body(n_ref, wg_ref, wu_ref, gu_ref, act_ref):
        x = n_ref[...]
        g = _dot(x, wg_ref[...])
        u = _dot(x, wu_ref[...])
        sig = jax.nn.sigmoid(g)
        silu = g * sig
        gu_ref[0] = (u * (sig + silu * (1.0 - sig))).astype(BF16)
        gu_ref[1] = silu.astype(BF16)
        act_ref[...] = (silu * u).astype(BF16)

    return _call(
        body, name="ffn_gu", grid=(2, t // TM),
        in_specs=[pl.BlockSpec((TM, d), lambda j, i: (i, 0)),
                  pl.BlockSpec((None, d, FS), lambda j, i: (j, 0, 0)),
                  pl.BlockSpec((None, d, FS), lambda j, i: (j + 2, 0, 0))],
        out_specs=[pl.BlockSpec((2, TM, FS), lambda j, i: (0, i, j)), pl.BlockSpec((TM, FS), lambda j, i: (i, j))],
        out_shape=[jax.ShapeDtypeStruct((2, t, D_FF), BF16), jax.ShapeDtypeStruct((t, D_FF), BF16)],
        compiler_params=_params(2))(n, wgu, wgu)


def _down_res(act, wdn, h, g_next):
    t, f = act.shape
    d = h.shape[1]

    def body(a_ref, w_ref, h_ref, g_ref, o_ref, n_ref):
        out = h_ref[...] + 0.5 * _dot(a_ref[...], w_ref[...])
        o_ref[...] = out
        n_ref[...] = _rms_fwd(out, g_ref[...])[0].astype(BF16)

    row = pl.BlockSpec((TM, d), lambda i: (i, 0))
    return _call(
        body, name="down_res", grid=(t // TM,),
        in_specs=[pl.BlockSpec((TM, f), lambda i: (i, 0)), pl.BlockSpec((f, d), lambda i: (0, 0)), row,
                  pl.BlockSpec((1, d), lambda i: (0, 0))],
        out_specs=[row, row],
        out_shape=[jax.ShapeDtypeStruct((t, d), F32), jax.ShapeDtypeStruct((t, d), BF16)],
        compiler_params=_params(1))(act, wdn, h, g_next)


def _proj(n, w_in_t):
    t, d = n.shape
    w = w_in_t.shape[0]

    def body(n_ref, w_ref, o_ref):
        o_ref[...] = _dot_nt(n_ref[...], w_ref[...]).astype(BF16)

    return _call(
        body, name="proj", grid=(t // TM,),
        in_specs=[pl.BlockSpec((TM, d), lambda i: (i, 0)), pl.BlockSpec((w, d), lambda i: (0, 0))],
        out_specs=pl.BlockSpec((TM, w), lambda i: (i, 0)),
        out_shape=jax.ShapeDtypeStruct((t, w), BF16), compiler_params=_params(1))(n, w_in_t)


def _out_res(o_sb, o_sw, g_sb, g_sw, w_out, h, g_next):
    t, d = h.shape

    def body(a_ref, b_ref, ga_ref, gb_ref, w_ref, h_ref, g_ref, o_ref, mix_ref, n_ref):
        ya, _, _ = _rms_fwd(a_ref[...], ga_ref[...])
        yb, _, _ = _rms_fwd(b_ref[...], gb_ref[...])
        mixed = jnp.concatenate([ya.astype(BF16), yb.astype(BF16)], axis=1)
        mix_ref[...] = mixed
        out = h_ref[...] + _dot(mixed, w_ref[...])
        o_ref[...] = out
        n_ref[...] = _rms_fwd(out, g_ref[...])[0].astype(BF16)

    row = pl.BlockSpec((TM, d), lambda i: (i, 0))
    return _call(
        body, name="out_res", grid=(t // TM,),
        in_specs=[pl.BlockSpec((TM, SB_W), lambda i: (i, 0)), pl.BlockSpec((TM, SWA_W), lambda i: (i, 0)),
                  pl.BlockSpec((1, SB_W), lambda i: (0, 0)), pl.BlockSpec((1, SWA_W), lambda i: (0, 0)),
                  pl.BlockSpec((d, d), lambda i: (0, 0)), row, pl.BlockSpec((1, d), lambda i: (0, 0))],
        out_specs=[row, row, row],
        out_shape=[jax.ShapeDtypeStruct((t, d), F32), jax.ShapeDtypeStruct((t, d), BF16),
                   jax.ShapeDtypeStruct((t, d), BF16)],
        compiler_params=_params(1))(o_sb, o_sw, g_sb, g_sw, w_out, h, g_next)


def _loss_head(h, g, tgt):
    t, d = h.shape

    def body(h_ref, g_ref, t_ref, dh_ref, dg_ref, loss_ref):
        @pl.when(pl.program_id(0) == 0)
        def _():
            dg_ref[...] = jnp.zeros_like(dg_ref)
            loss_ref[...] = jnp.zeros_like(loss_ref)

        gg = g_ref[...]
        y, xh, r = _rms_fwd(h_ref[...], gg)
        err = y - t_ref[...]
        part = 0.5 * jnp.sum(jnp.sum(err * err, axis=1, keepdims=True) / d, axis=0, keepdims=True)
        loss_ref[...] += jnp.broadcast_to(part, loss_ref.shape)
        dx, dg = _rms_bwd(err / d, xh, r, gg)
        dh_ref[...] = dx
        dg_ref[...] += dg

    return _call(
        body, name="loss_head", grid=(t // TM,),
        in_specs=[pl.BlockSpec((TM, d), lambda i: (i, 0)), pl.BlockSpec((1, d), lambda i: (0, 0)),
                  pl.BlockSpec((TM, d), lambda i: (i, 0))],
        out_specs=[pl.BlockSpec((TM, d), lambda i: (i, 0)), pl.BlockSpec((1, d), lambda i: (0, 0)),
                   pl.BlockSpec((1, LANES), lambda i: (0, 0))],
        out_shape=[jax.ShapeDtypeStruct((t, d), F32), jax.ShapeDtypeStruct((1, d), F32),
                   jax.ShapeDtypeStruct((1, LANES), F32)],
        compiler_params=_params(1))(h, g, tgt)


def _ffn_dact(dh, wdn, gu):
    t, d = dh.shape
    tm = TM

    def body(dh_ref, w_ref, gu_ref, o_ref):
        da = 0.5 * _dot_nt(dh_ref[...].astype(BF16), w_ref[...])
        o_ref[0] = (da * gu_ref[0].astype(F32)).astype(BF16)
        o_ref[1] = (da * gu_ref[1].astype(F32)).astype(BF16)

    return _call(
        body, name="ffn_dact", grid=(2, t // tm),
        in_specs=[pl.BlockSpec((tm, d), lambda j, i: (i, 0)), pl.BlockSpec((FS, d), lambda j, i: (j, 0)),
                  pl.BlockSpec((2, tm, FS), lambda j, i: (0, i, j))],
        out_specs=pl.BlockSpec((2, tm, FS), lambda j, i: (0, i, j)),
        out_shape=jax.ShapeDtypeStruct((2, t, D_FF), BF16), compiler_params=_params(2))(dh, wdn, gu)


def _dn_norm_bwd(a, a_spec, w, w_spec, nk, dh, h_in, g, w_transposed=False):
    t, d = dh.shape
    mm = _dot if w_transposed else _dot_nt

    def body(a_ref, w_ref, dh_ref, h_ref, g_ref, o_ref, dg_ref, acc_ref):
        i, k = pl.program_id(0), pl.program_id(1)

        if nk > 1:
            @pl.when(k == 0)
            def _():
                acc_ref[...] = mm(a_ref[...], w_ref[...])

            @pl.when((k > 0) & (k < nk - 1))
            def _():
                acc_ref[...] += mm(a_ref[...], w_ref[...])

        @pl.when(k == nk - 1)
        def _():
            gg = g_ref[...]
            dg = jnp.zeros_like(gg)
            for rows in (slice(0, TM // 2), slice(TM // 2, TM)):
                dn = mm(a_ref[rows, :], w_ref[...])
                if nk > 1:
                    dn = dn + acc_ref[rows, :]
                _, xh, r = _rms_fwd(h_ref[rows, :], gg)
                dx, dg_rows = _rms_bwd(dn, xh, r, gg)
                o_ref[rows, :] = dh_ref[rows, :] + dx
                dg = dg + dg_rows

            @pl.when(i == 0)
            def _():
                dg_ref[...] = dg

            @pl.when(i > 0)
            def _():
                dg_ref[...] += dg

    row = pl.BlockSpec((TM, d), lambda i, k: (i, 0))
    return _call(
        body, name="dn_norm_bwd", grid=(t // TM, nk),
        in_specs=[a_spec, w_spec, row, row, pl.BlockSpec((1, d), lambda i, k: (0, 0))],
        out_specs=[row, pl.BlockSpec((1, d), lambda i, k: (0, 0))],
        out_shape=[jax.ShapeDtypeStruct((t, d), F32), jax.ShapeDtypeStruct((1, d), F32)],
        scratch_shapes=[pltpu.VMEM((TM, d), F32)], compiler_params=_params(2))(a, w, dh, h_in, g)


def _ffn_dn(dgu, wgu, dh, h_in, g):
    d = dh.shape[1]
    return _dn_norm_bwd(
        dgu, pl.BlockSpec((None, TM, FS), lambda i, k: (k // 2, i, k % 2)),
        wgu, pl.BlockSpec((None, d, FS), lambda i, k: (k, 0, 0)), N_CHIPS, dh, h_in, g)


def _mix_dn(dproj, w_in_t, dh, h_in, g):
    d = dh.shape[1]
    w = dproj.shape[1]
    return _dn_norm_bwd(
        dproj, pl.BlockSpec((TM, w), lambda i, k: (i, 0)),
        w_in_t, pl.BlockSpec((w, d), lambda i, k: (0, 0)), 1, dh, h_in, g, w_transposed=True)


def _dmixed(dh, w_out, o_sb, o_sw, g_sb, g_sw):
    t, d = dh.shape

    def body(dh_ref, w_ref, a_ref, b_ref, ga_ref, gb_ref, o_ref, dga_ref, dgb_ref):
        i = pl.program_id(0)
        dm = _dot_nt(dh_ref[...].astype(BF16), w_ref[...])
        _, xa, ra = _rms_fwd(a_ref[...], ga_ref[...])
        _, xb, rb = _rms_fwd(b_ref[...], gb_ref[...])
        da, dga = _rms_bwd(dm[:, :SB_W], xa, ra, ga_ref[...])
        db, dgb = _rms_bwd(dm[:, SB_W:], xb, rb, gb_ref[...])
        o_ref[...] = jnp.concatenate([da.astype(BF16), db.astype(BF16)], axis=1)

        @pl.when(i == 0)
        def _():
            dga_ref[...] = dga
            dgb_ref[...] = dgb

        @pl.when(i > 0)
        def _():
            dga_ref[...] += dga
            dgb_ref[...] += dgb

    return _call(
        body, name="dmixed", grid=(t // TM,),
        in_specs=[pl.BlockSpec((TM, d), lambda i: (i, 0)), pl.BlockSpec((d, d), lambda i: (0, 0)),
                  pl.BlockSpec((TM, SB_W), lambda i: (i, 0)), pl.BlockSpec((TM, SWA_W), lambda i: (i, 0)),
                  pl.BlockSpec((1, SB_W), lambda i: (0, 0)), pl.BlockSpec((1, SWA_W), lambda i: (0, 0))],
        out_specs=[pl.BlockSpec((TM, d), lambda i: (i, 0)), pl.BlockSpec((1, SB_W), lambda i: (0, 0)),
                   pl.BlockSpec((1, SWA_W), lambda i: (0, 0))],
        out_shape=[jax.ShapeDtypeStruct((t, d), BF16), jax.ShapeDtypeStruct((1, SB_W), F32),
                   jax.ShapeDtypeStruct((1, SWA_W), F32)],
        compiler_params=_params(1))(dh, w_out, o_sb, o_sw, g_sb, g_sw)


def _wgrad(name, a, a_spec, b, b_spec, grid, out_shape, out_spec, scale):
    def body(a_ref, b_ref, o_ref):
        r = _dot_tn(a_ref[...], b_ref[...].astype(BF16))
        o_ref[...] = r if scale == 1.0 else scale * r

    return _call(
        body, name=name, grid=grid, in_specs=[a_spec, b_spec], out_specs=out_spec,
        out_shape=jax.ShapeDtypeStruct(out_shape, F32), compiler_params=_params(len(grid)))(a, b)


def _wgrad_gu(n, dgu):
    t, d = n.shape
    return _wgrad(
        "wgrad_gu", n, pl.BlockSpec((t, TM), lambda s, r: (0, r)),
        dgu, pl.BlockSpec((None, t, FS), lambda s, r: (s // 2, 0, s % 2)), (N_CHIPS, d // TM),
        (N_CHIPS, d, FS), pl.BlockSpec((None, TM, FS), lambda s, r: (s, r, 0)), 1.0)


def _wgrad_down(act, dh):
    t, d = dh.shape
    return _wgrad(
        "wgrad_down", act, pl.BlockSpec((t, FS), lambda s, r: (0, s)), dh, pl.BlockSpec((t, TM), lambda s, r: (0, r)),
        (2, d // TM), (D_FF, d), pl.BlockSpec((FS, TM), lambda s, r: (s, r)), 0.5)


def _wgrad_out(mixed, dh):
    t, d = dh.shape
    return _wgrad(
        "wgrad_out", mixed, pl.BlockSpec((t, TM), lambda s: (0, s)), dh, pl.BlockSpec((t, d), lambda s: (0, 0)),
        (d // TM,), (d, d), pl.BlockSpec((TM, d), lambda s: (s, 0)), 1.0)


def _wgrad_in(n, dproj):
    t, d = n.shape
    w = dproj.shape[1]
    tw = w // 3
    return _wgrad(
        "wgrad_in", dproj, pl.BlockSpec((t, tw), lambda s: (0, s)), n, pl.BlockSpec((t, d), lambda s: (0, 0)),
        (3,), (w, d), pl.BlockSpec((tw, d), lambda s: (s, 0)), 1.0)


def _tri(rel):
    row = lax.broadcasted_iota(jnp.int32, (BLK, BLK), 0)
    col = lax.broadcasted_iota(jnp.int32, (BLK, BLK), 1)
    m = rel(row, col).astype(BF16)
    return jnp.concatenate([m, m], axis=0)


def _scan_dot(x, tri2):
    hi = x.astype(BF16)
    lo = (x - hi.astype(F32)).astype(BF16)
    return _dot(jnp.concatenate([hi, lo], axis=1), tri2)


def _head_masks():
    lane = lax.broadcasted_iota(jnp.int32, (1, LANES), 1)
    return [lane < HEAD_DIM, lane >= HEAD_DIM]


def _sb_dcol():
    dcol = lax.broadcasted_iota(jnp.int32, (BLK, SB_KT), 1) - lax.broadcasted_iota(jnp.int32, (BLK, SB_KT), 0)
    return jnp.concatenate([dcol, dcol], axis=0)


def _sb_fwd(proj):
    t = proj.shape[0]
    nq = t // BLK
    nb = SB_KT // BLK

    def body(q_ref, k_ref, v_ref, o_ref, tot_ref):
        hm = _head_masks()
        dcol = _sb_dcol()
        after = _tri(lambda r, c: r > c)

        def tile(qh, kt, carry, acc, limit):
            ks = pl.ds(pl.multiple_of(kt * SB_KT, SB_KT), SB_KT)
            z = _dot_nt(qh, k_ref[ks, :])
            sp, zs = _softplus(z)
            valid = None if limit is None else dcol < limit
            spm = sp if valid is None else jnp.where(valid, sp, 0.0)
            sufs = [None] * nb
            for b in reversed(range(nb)):
                blk = spm[:, b * BLK:(b + 1) * BLK]
                sufs[b] = carry + _scan_dot(blk, after)
                carry = carry + jnp.sum(blk, axis=1, keepdims=True)
            w = jnp.exp(zs - jnp.concatenate(sufs, axis=1))
            if valid is not None:
                w = jnp.where(valid, w, 0.0)
            return carry, acc + _dot(w.astype(BF16), v_ref[ks, :])

        def qblock(qi, _):
            qs = pl.ds(pl.multiple_of(qi * BLK, BLK), BLK)
            q = q_ref[qs, :] * SCALE
            kd = qi // nb
            limit = (qi - kd * nb) * BLK
            qh = jnp.concatenate([jnp.where(m, q, jnp.zeros_like(q)) for m in hm], axis=0)
            c0 = tile(qh, kd, jnp.zeros((2 * BLK, 1), F32), jnp.zeros((2 * BLK, LANES), F32), limit)
            carry, acc = lax.fori_loop(0, kd, lambda n, c: tile(qh, kd - 1 - n, c[0], c[1], None), c0)
            o_ref[qs, :] = jnp.where(hm[0], acc[:BLK], acc[BLK:])
            for h in range(2):
                tot_ref[h, qs, :] = jnp.broadcast_to(carry[h * BLK:(h + 1) * BLK], (BLK, LANES))
            return 0

        lax.fori_loop(0, nq, qblock, 0)

    col_blk = lambda off: pl.BlockSpec((t, LANES), lambda p: (0, off + p))
    return _call(
        body, name="sb_fwd", grid=(4,), in_specs=[col_blk(0), col_blk(4), col_blk(8)],
        out_specs=[pl.BlockSpec((t, LANES), lambda p: (0, p)), pl.BlockSpec((2, t, LANES), lambda p: (p, 0, 0))],
        out_shape=[jax.ShapeDtypeStruct((t, SB_W), F32), jax.ShapeDtypeStruct((8, t, LANES), F32)],
        compiler_params=_params(1))(proj, proj, proj)


def _sb_bwd(proj, d_o, tot):
    t = proj.shape[0]
    nq = t // BLK
    nb = SB_KT // BLK

    def body(q_ref, k_ref, v_ref, do_ref, tot_ref, dq_ref, dk_ref, dv_ref, dk_acc, dv_acc):
        hm = _head_masks()
        dcol = _sb_dcol()
        before = _tri(lambda r, c: r < c)
        upto = _tri(lambda r, c: r <= c)
        dk_acc[...] = jnp.zeros_like(dk_acc)
        dv_acc[...] = jnp.zeros_like(dv_acc)

        def tile(qh, doh, tt, kt, pre, ecum, dq, limit):
            ks = pl.ds(pl.multiple_of(kt * SB_KT, SB_KT), SB_KT)
            k = k_ref[ks, :]
            v = v_ref[ks, :]
            z = _dot_nt(qh, k)
            sp, zs = _softplus(z)
            valid = None if limit is None else dcol < limit
            spm = sp if valid is None else jnp.where(valid, sp, 0.0)
            pres = []
            for b in range(nb):
                blk = spm[:, b * BLK:(b + 1) * BLK]
                pres.append(pre + _scan_dot(blk, before))
                pre = pre + jnp.sum(blk, axis=1, keepdims=True)
            logw = z - (tt - jnp.concatenate(pres, axis=1))
            if valid is not None:
                logw = jnp.minimum(logw, 0.0)
            w = jnp.exp(logw)
            if valid is not None:
                w = jnp.where(valid, w, 0.0)
            e = w * _dot_nt(doh, v)
            incs = []
            for b in range(nb):
                blk = e[:, b * BLK:(b + 1) * BLK]
                incs.append(ecum + _scan_dot(blk, upto))
                ecum = ecum + jnp.sum(blk, axis=1, keepdims=True)
            dz = e - jnp.exp(zs) * jnp.concatenate(incs, axis=1)
            if valid is not None:
                dz = jnp.where(valid, dz, 0.0)
            dzb = dz.astype(BF16)
            dk_acc[ks, :] += _dot_tn(dzb, qh)
            dv_acc[ks, :] += _dot_tn(w.astype(BF16), doh)
            return pre, ecum, dq + _dot(dzb, k)

        def qblock(qi, _):
            qs = pl.ds(pl.multiple_of(qi * BLK, BLK), BLK)
            q = q_ref[qs, :] * SCALE
            do = do_ref[qs, :]
            kd = qi // nb
            limit = (qi - kd * nb) * BLK
            qh = jnp.concatenate([jnp.where(m, q, jnp.zeros_like(q)) for m in hm], axis=0)
            doh = jnp.concatenate([jnp.where(m, do, jnp.zeros_like(do)) for m in hm], axis=0)
            tt = jnp.concatenate([tot_ref[h, qs, 0:1] for h in range(2)], axis=0)
            c0 = (jnp.zeros((2 * BLK, 1), F32), jnp.zeros((2 * BLK, 1), F32), jnp.zeros((2 * BLK, LANES), F32))
            c = lax.fori_loop(0, kd, lambda kt, c: tile(qh, doh, tt, kt, c[0], c[1], c[2], None), c0)
            dq = tile(qh, doh, tt, kd, c[0], c[1], c[2], limit)[2]
            dq_ref[qs, :] = (jnp.where(hm[0], dq[:BLK], dq[BLK:]) * SCALE).astype(BF16)
            return 0

        lax.fori_loop(0, nq, qblock, 0)
        dk_ref[...] = dk_acc[...].astype(BF16)
        dv_ref[...] = dv_acc[...].astype(BF16)

    col_blk = lambda off: pl.BlockSpec((t, LANES), lambda p: (0, off + p))
    out = jax.ShapeDtypeStruct((t, SB_W), BF16)
    return _call(
        body, name="sb_bwd", grid=(4,),
        in_specs=[col_blk(0), col_blk(4), col_blk(8), col_blk(0), pl.BlockSpec((2, t, LANES), lambda p: (p, 0, 0))],
        out_specs=[col_blk(0), col_blk(0), col_blk(0)], out_shape=[out, out, out],
        scratch_shapes=[pltpu.VMEM((t, LANES), F32), pltpu.VMEM((t, LANES), F32)],
        compiler_params=_params(1))(proj, proj, proj, d_o, tot)


def _bucket_table():
    a = np.arange(BLK)[:, None]
    c = np.arange(2 * BLK)[None, :]
    dist = np.maximum(BLK + a - c, 0)
    max_exact = N_BUCKETS // 2
    dd = np.maximum(dist, 1).astype(np.float32)
    large = max_exact + (np.log(dd / max_exact) / math.log(MAX_DISTANCE / max_exact)
                         * (N_BUCKETS - max_exact)).astype(np.int32)
    large = np.minimum(large, N_BUCKETS - 1)
    return np.where(dist < max_exact, dist, large).astype(np.int32)


def _swa_band_masks():
    row = lax.broadcasted_iota(jnp.int32, (SWA_G * BLK, 2 * BLK), 0) & (BLK - 1)
    col = lax.broadcasted_iota(jnp.int32, (SWA_G * BLK, 2 * BLK), 1)
    own = lax.broadcasted_iota(jnp.int32, (SWA_G * BLK, BLK), 1) <= (
        lax.broadcasted_iota(jnp.int32, (SWA_G * BLK, BLK), 0) & (BLK - 1))
    return (col > row) & ((col < BLK) | (col - BLK <= row)), own


def _swa_stack(ref, qs, kvh, kvmask, scale):
    parts = []
    for g in range(SWA_G):
        hq = SWA_G * kvh + g
        x = ref[qs, (hq // 2) * LANES:(hq // 2 + 1) * LANES].astype(F32)
        if hq % 2 != kvh:
            x = pltpu.roll(x, HEAD_DIM, 1)
        parts.append(jnp.where(kvmask, x * scale, 0.0).astype(BF16))
    return jnp.concatenate(parts, axis=0)


def _swa_unstack(x4, kvh, hm):
    heads = []
    for g in range(SWA_G):
        x = x4[g * BLK:(g + 1) * BLK]
        heads.append(pltpu.roll(x, HEAD_DIM, 1) if g % 2 != kvh else x)
    return [jnp.where(hm[0], heads[0], heads[1]), jnp.where(hm[0], heads[2], heads[3])]


def _swa_scores(q4, kb, bias_ref, kvh, mask, cols):
    bias4 = jnp.concatenate([bias_ref[SWA_G * kvh + g, :, cols] for g in range(SWA_G)], axis=0)
    return jnp.where(mask, _dot_nt(q4, kb) + bias4, NEG_INF)


def _swa_sinks(sink_ref, kvh):
    return jnp.concatenate([jnp.broadcast_to(sink_ref[SWA_G * kvh + g:SWA_G * kvh + g + 1, 0:1], (BLK, 1))
                            for g in range(SWA_G)], axis=0)


def _swa_fwd(proj, bias, sinks_b):
    t = proj.shape[0]
    nq = t // BLK

    def body(q_ref, k_ref, v_ref, bias_ref, sink_ref, o_ref, lse_ref):
        hm = _head_masks()
        band, own = _swa_band_masks()

        def qblock(i, kvh, prev):
            qs = pl.ds(pl.multiple_of(i * BLK, BLK), BLK)
            if prev:
                ks, mask, cols = pl.ds(pl.multiple_of((i - 1) * BLK, BLK), 2 * BLK), band, slice(None)
            else:
                ks, mask, cols = qs, own, slice(BLK, None)
            q4 = _swa_stack(q_ref, qs, kvh, hm[kvh], SCALE)
            sink4 = _swa_sinks(sink_ref, kvh)
            s = _swa_scores(q4, k_ref[ks, :], bias_ref, kvh, mask, cols)
            m = jnp.maximum(jnp.max(s, axis=1, keepdims=True), sink4)
            p = jnp.exp(s - m)
            den = jnp.sum(p, axis=1, keepdims=True) + jnp.exp(sink4 - m)
            o4 = _dot((p * (1.0 / den)).astype(BF16), v_ref[ks, :])
            lse4 = m + jnp.log(den)
            for g in range(SWA_G):
                lse_ref[SWA_G * kvh + g, qs, :] = jnp.broadcast_to(lse4[g * BLK:(g + 1) * BLK], (BLK, LANES))
            for pp, o in enumerate(_swa_unstack(o4, kvh, hm)):
                o_ref[qs, (2 * kvh + pp) * LANES:(2 * kvh + pp + 1) * LANES] = o

        for kvh in range(2):
            qblock(0, kvh, False)

            def step(i, _):
                qblock(i, kvh, True)
                return 0

            lax.fori_loop(1, nq, step, 0, unroll=SWA_UNROLL)

    return _call(
        body, name="swa_fwd", grid=(1,),
        in_specs=[pl.BlockSpec((t, SWA_W), lambda i: (0, 3)), pl.BlockSpec((t, KV_W), lambda i: (0, 16)),
                  pl.BlockSpec((t, KV_W), lambda i: (0, 17)), pl.BlockSpec((8, BLK, 2 * BLK), lambda i: (0, 0, 0)),
                  pl.BlockSpec((8, LANES), lambda i: (0, 0))],
        out_specs=[pl.BlockSpec((t, SWA_W), lambda i: (0, 0)), pl.BlockSpec((8, t, LANES), lambda i: (0, 0, 0))],
        out_shape=[jax.ShapeDtypeStruct((t, SWA_W), F32), jax.ShapeDtypeStruct((8, t, LANES), F32)],
        compiler_params=_params(1))(proj, proj, proj, bias, sinks_b)


def _swa_bwd(proj, d_o, lse, bias, sinks_b, dbias_in):
    t = proj.shape[0]
    nq = t // BLK

    def body(q_ref, k_ref, v_ref, do_ref, lse_ref, bias_ref, sink_ref, dbi_ref,
             dq_ref, dk_ref, dv_ref, dsink_ref, dbias_ref, dk_acc, dv_acc):
        hm = _head_masks()
        band, own = _swa_band_masks()
        dk_acc[...] = jnp.zeros_like(dk_acc)
        dv_acc[...] = jnp.zeros_like(dv_acc)
        dbias_ref[...] = dbi_ref[...]

        def qblock(i, kvh, prev, dsink4):
            qs = pl.ds(pl.multiple_of(i * BLK, BLK), BLK)
            if prev:
                ks, mask, cols = pl.ds(pl.multiple_of((i - 1) * BLK, BLK), 2 * BLK), band, slice(None)
            else:
                ks, mask, cols = qs, own, slice(BLK, None)
            q4 = _swa_stack(q_ref, qs, kvh, hm[kvh], SCALE)
            do4 = _swa_stack(do_ref, qs, kvh, hm[kvh], 1.0)
            sink4 = _swa_sinks(sink_ref, kvh)
            lse4 = jnp.concatenate([lse_ref[SWA_G * kvh + g, qs, 0:1] for g in range(SWA_G)], axis=0)
            kb = k_ref[ks, :]
            p = jnp.exp(_swa_scores(q4, kb, bias_ref, kvh, mask, cols) - lse4)
            dp = _dot_nt(do4, v_ref[ks, :])
            delta = jnp.sum(p * dp, axis=1, keepdims=True)
            ds = p * (dp - delta)
            for g in range(SWA_G):
                dbias_ref[SWA_G * kvh + g, :, cols] += ds[g * BLK:(g + 1) * BLK]
            dsb = ds.astype(BF16)
            dk_acc[ks, :] += _dot_tn(dsb, q4)
            dv_acc[ks, :] += _dot_tn(p.astype(BF16), do4)
            for pp, dq in enumerate(_swa_unstack(_dot(dsb, kb) * SCALE, kvh, hm)):
                dq_ref[qs, (2 * kvh + pp) * LANES:(2 * kvh + pp + 1) * LANES] = dq.astype(BF16)
            return dsink4 - jnp.exp(sink4 - lse4) * delta

        for kvh in range(2):
            ds0 = qblock(0, kvh, False, jnp.zeros((SWA_G * BLK, 1), F32))
            ds4 = lax.fori_loop(1, nq, lambda i, c: qblock(i, kvh, True, c), ds0, unroll=SWA_UNROLL)
            for g in range(SWA_G):
                hq = SWA_G * kvh + g
                dsink_ref[hq:hq + 1, :] = jnp.broadcast_to(
                    jnp.sum(ds4[g * BLK:(g + 1) * BLK], axis=0, keepdims=True), (1, LANES))

        dk_ref[...] = dk_acc[...].astype(BF16)
        dv_ref[...] = dv_acc[...].astype(BF16)

    full3 = pl.BlockSpec((8, BLK, 2 * BLK), lambda i: (0, 0, 0))
    kv = jax.ShapeDtypeStruct((t, KV_W), BF16)
    return _call(
        body, name="swa_bwd", grid=(1,),
        in_specs=[pl.BlockSpec((t, SWA_W), lambda i: (0, 3)), pl.BlockSpec((t, KV_W), lambda i: (0, 16)),
                  pl.BlockSpec((t, KV_W), lambda i: (0, 17)), pl.BlockSpec((t, SWA_W), lambda i: (0, 1)),
                  pl.BlockSpec((8, t, LANES), lambda i: (0, 0, 0)), full3, pl.BlockSpec((8, LANES), lambda i: (0, 0)),
                  full3],
        out_specs=[pl.BlockSpec((t, SWA_W), lambda i: (0, 0)), pl.BlockSpec((t, KV_W), lambda i: (0, 0)),
                   pl.BlockSpec((t, KV_W), lambda i: (0, 0)), pl.BlockSpec((8, LANES), lambda i: (0, 0)), full3],
        out_shape=[jax.ShapeDtypeStruct((t, SWA_W), BF16), kv, kv, jax.ShapeDtypeStruct((8, LANES), F32),
                   jax.ShapeDtypeStruct((8, BLK, 2 * BLK), F32)],
        scratch_shapes=[pltpu.VMEM((t, KV_W), F32), pltpu.VMEM((t, KV_W), F32)],
        compiler_params=_params(1))(proj, proj, proj, d_o, lse, bias, sinks_b, dbias_in)


def _bias_table(rel_bias, buckets):
    def body(rb_ref, b_ref, o_ref):
        bk = b_ref[...]
        for h in range(8):
            acc = jnp.zeros((BLK, 2 * BLK), F32)
            for b in range(N_BUCKETS):
                acc = jnp.where(bk == b, rb_ref[b, h], acc)
            o_ref[h] = acc

    return _call(
        body, name="bias_table", grid=(1,),
        in_specs=[pl.BlockSpec(memory_space=pltpu.SMEM), pl.BlockSpec((BLK, 2 * BLK), lambda i: (0, 0))],
        out_specs=pl.BlockSpec((8, BLK, 2 * BLK), lambda i: (0, 0, 0)),
        out_shape=jax.ShapeDtypeStruct((8, BLK, 2 * BLK), F32), compiler_params=_params(1))(rel_bias, buckets)


def _bias_grad(dbias, buckets):
    def body(d_ref, b_ref, o_ref):
        lane = lax.broadcasted_iota(jnp.int32, (1, LANES), 1)
        bk = b_ref[...]
        for h in range(8):
            d = d_ref[h]
            acc = jnp.zeros((1, LANES), F32)
            for b in range(N_BUCKETS):
                s = jnp.sum(jnp.sum(jnp.where(bk == b, d, 0.0), axis=0, keepdims=True), axis=1, keepdims=True)
                acc = acc + jnp.where(lane == b, s, 0.0)
            o_ref[h:h + 1, :] = acc

    return _call(
        body, name="bias_grad", grid=(1,),
        in_specs=[pl.BlockSpec((8, BLK, 2 * BLK), lambda i: (0, 0, 0)), pl.BlockSpec((BLK, 2 * BLK), lambda i: (0, 0))],
        out_specs=pl.BlockSpec((8, LANES), lambda i: (0, 0)),
        out_shape=jax.ShapeDtypeStruct((8, LANES), F32), compiler_params=_params(1))(dbias, buckets)


def _row(a):
    return a.reshape(1, -1)


def _fwd_ffn1(h, n1, w, small, l):
    s = {"h0": h, "n1": n1}
    s["gu1"], s["act1"] = _ffn_gu(n1, w["ffn1_gu"])
    s["h1"], s["nm"] = _down_res(s["act1"], w["ffn1_down"], h, _row(small["norm_mix"][l]))
    return s


def _fwd_proj_sb(s, w):
    s["proj"] = _proj(s["nm"], w["w_in"])
    s["o_sb"], s["tot"] = _sb_fwd(s["proj"])


def _fwd_swa(s, small, l, bias):
    s["sinks_b"] = jnp.broadcast_to(small["sinks"][l][:, None], (8, LANES))
    s["o_sw"], s["lse"] = _swa_fwd(s["proj"], bias, s["sinks_b"])


def _fwd_out_ffn2(s, w, small, l, g_after):
    s["h2"], s["mixed"], s["n2"] = _out_res(
        s["o_sb"], s["o_sw"], _row(small["norm_out_sb"][l]), _row(small["norm_out_swa"][l]), w["w_out"], s["h1"],
        _row(small["norm_ffn2"][l]))
    s["gu2"], s["act2"] = _ffn_gu(s["n2"], w["ffn2_gu"])
    return _down_res(s["act2"], w["ffn2_down"], s["h2"], g_after)


def _bwd_ffn_dact(dh, s, w, which):
    return _ffn_dact(dh, w[f"ffn{which}_down"], s[f"gu{which}"])


def _bwd_ffn_rest(dh, dgu, s, w, small, l, which):
    h_in, norm = (s["h0"], "norm_ffn1") if which == 1 else (s["h2"], "norm_ffn2")
    g_down = _wgrad_down(s[f"act{which}"], dh)
    g_gu = _wgrad_gu(s[f"n{which}"], dgu)
    dh, dg = _ffn_dn(dgu, w[f"ffn{which}_gu"], dh, h_in, _row(small[norm][l]))
    return dh, {f"ffn{which}_down": g_down, f"ffn{which}_gu": g_gu}, {norm: dg}


def _bwd_ffn(dh, s, w, small, l, which):
    return _bwd_ffn_rest(dh, _bwd_ffn_dact(dh, s, w, which), s, w, small, l, which)


def _bwd_mix(dh, s, w, small, l, bias, dbias):
    g_out = _wgrad_out(s["mixed"], dh)
    d_o, dg_sb, dg_sw = _dmixed(dh, w["w_out"], s["o_sb"], s["o_sw"], _row(small["norm_out_sb"][l]),
                                _row(small["norm_out_swa"][l]))
    dq_sb, dk_sb, dv_sb = _sb_bwd(s["proj"], d_o, s["tot"])
    dq_sw, dk_sw, dv_sw, dsink, dbias = _swa_bwd(s["proj"], d_o, s["lse"], bias, s["sinks_b"], dbias)
    dproj = jnp.concatenate([dq_sb, dk_sb, dv_sb, dq_sw, dk_sw, dv_sw], axis=1)
    g_in = _wgrad_in(s["nm"], dproj)
    dh, dg_mix = _mix_dn(dproj, w["w_in"], dh, s["h1"], _row(small["norm_mix"][l]))
    gs = {"norm_out_sb": dg_sb, "norm_out_swa": dg_sw, "sinks": dsink[:, 0], "norm_mix": dg_mix}
    return dh, {"w_out": g_out, "w_in": g_in}, gs, dbias


def _place():
    x, y, c = lax.axis_index("x"), lax.axis_index("y"), lax.axis_index("c")
    return x, y, c, 2 * x + y


def _chip_core(k, c):
    return (k // 2, k % 2, c)


def _rows_per_block(rows, cols, copies):
    best = 16
    for tr in range(16, rows + 1, 16):
        if rows % tr == 0 and copies * tr * cols * 4 <= 6 * 2 ** 20:
            best = tr
    assert rows % best == 0
    return best


def _place_own(w, l, me1):
    _, rows, cols = w.shape
    tr = _rows_per_block(rows // 2, cols, 1)
    per_half = rows // 2 // tr

    def body(me_ref, w_ref, o_ref):
        o_ref[...] = w_ref[...].astype(BF16)

    return _call(
        body, name="place_own",
        num_scalar_prefetch=1, grid=(rows // tr,),
        in_specs=[pl.BlockSpec((None, tr, cols), lambda r, me: (l, r, 0))],
        out_specs=pl.BlockSpec((None, None, tr, cols), lambda r, me: (me[0], r // per_half, r % per_half, 0)),
        out_shape=jax.ShapeDtypeStruct((N_CHIPS, 2, rows // 2, cols), BF16), compiler_params=_params(1))(me1, w)


def _plan_gather_ici(bufs):
    _, _, c, me = _place()
    return [(b.at[me, c], b.at[me, c], b.at[(me + 3 - j) % N_CHIPS, c], _chip_core((me + 1 + j) % N_CHIPS, c))
            for b in bufs for j in range(3)]


def _plan_gather_d2d(bufs):
    x, y, c, me = _place()
    return [(b.at[(me + 3 - j) % N_CHIPS, c], b.at[(me + 3 - j) % N_CHIPS, c], b.at[(me + 3 - j) % N_CHIPS, 1 - c],
             (x, y, 1 - c)) for b in bufs for j in range(3)]


def _plan_grad_sibling(bufs):
    x, y, c, _ = _place()
    n = len(bufs) // 2
    return [(g.at[:, 1 - c], z, z, (x, y, 1 - c)) for g, z in zip(bufs[:n], bufs[n:])]


def _plan_grad_chips(bufs):
    _, _, c, me = _place()
    n = len(bufs) // 2
    return [(p.at[(me + 1 + j) % N_CHIPS], z.at[j], z.at[j], _chip_core((me + 1 + j) % N_CHIPS, c))
            for p, z in zip(bufs[:n], bufs[n:]) for j in range(3)]


def _plan_grad_halves(bufs):
    x, y, c, _ = _place()
    return [(b.at[c], b.at[c], b.at[1 - c], (x, y, 1 - c)) for b in bufs]


def _remote(src, dst, send_sem, recv_sem, to):
    return pltpu.make_async_remote_copy(src_ref=src, dst_ref=dst, send_sem=send_sem, recv_sem=recv_sem,
                                        device_id=to, device_id_type=MESH)


def _exchange_start(name, plan, bufs, n_copies):
    n = len(bufs)

    def body(*refs):
        ins = refs[:n]
        ssem, rsem = refs[n], refs[n + 1]
        token = refs[-1]
        for i, (src, dst, _, to) in enumerate(plan(ins)):
            _remote(src, dst, ssem.at[i], rsem.at[i], to).start()
        token[...] = jnp.zeros_like(token)

    out = _call(
        body, name=name,
        out_shape=(pltpu.SemaphoreType.DMA((n_copies,)), pltpu.SemaphoreType.DMA((n_copies,)),
                   *[pltpu.HBM(a.shape, a.dtype) for a in bufs], jax.ShapeDtypeStruct((8, LANES), F32)),
        in_specs=[HBM] * n, out_specs=(SEM, SEM, *[HBM] * n, pl.BlockSpec(memory_space=pltpu.VMEM)),
        input_output_aliases={t: 2 + t for t in range(n)}, hbm_args=n,
        compiler_params=pltpu.CompilerParams(has_side_effects=EFFECT),
    )(*bufs)
    return (out[0], out[1]), list(out[2:2 + n])


def _exchange_wait(name, plan, bufs, sems):
    n = len(bufs)

    def body(*refs):
        ins = refs[:n]
        ssem, rsem = refs[n], refs[n + 1]
        for i, (src, dst, land, to) in enumerate(plan(ins)):
            _remote(src, dst, ssem.at[i], rsem.at[i], to).wait_send()
            _remote(land, land, ssem.at[i], rsem.at[i], to).wait_recv()

    return list(_call(
        body, name=name, out_shape=[pltpu.HBM(a.shape, a.dtype) for a in bufs],
        in_specs=[HBM] * n + [SEM, SEM], out_specs=[HBM] * n,
        input_output_aliases={t: t for t in range(n)},
        compiler_params=pltpu.CompilerParams(has_side_effects=EFFECT),
    )(*bufs, sems[0], sems[1]))


def _gather_now(bufs):
    n = len(bufs)
    n_cp = 3 * n

    def body(*refs):
        outs = refs[n:2 * n]
        ici_s, ici_r, d2d_s, d2d_r = refs[2 * n:]
        first = _plan_gather_ici(outs)
        second = _plan_gather_d2d(outs)
        for i, (src, dst, _, to) in enumerate(first):
            _remote(src, dst, ici_s.at[i], ici_r.at[i], to).start()
        for i, (src, dst, _, to) in enumerate(second):
            land = first[i][2]
            _remote(land, land, ici_s.at[i], ici_r.at[i], to).wait_recv()
            _remote(src, dst, d2d_s.at[i], d2d_r.at[i], to).start()
        for i, (_, _, land, to) in enumerate(second):
            _remote(land, land, d2d_s.at[i], d2d_r.at[i], to).wait_recv()
        for i in range(n_cp):
            _remote(first[i][0], first[i][1], ici_s.at[i], ici_r.at[i], first[i][3]).wait_send()
            _remote(second[i][0], second[i][1], d2d_s.at[i], d2d_r.at[i], second[i][3]).wait_send()

    return _call(
        body, name="gather_layer0", in_specs=[ANY] * n, out_specs=[ANY] * n,
        out_shape=[jax.ShapeDtypeStruct(a.shape, a.dtype) for a in bufs],
        input_output_aliases={t: t for t in range(n)},
        scratch_shapes=[pltpu.SemaphoreType.DMA((n_cp,))] * 4,
        compiler_params=pltpu.CompilerParams(vmem_limit_bytes=V7X_VMEM_LIMIT))(*bufs)


def _chip_sum(g, xbuf, cm):
    _, _, r2, cols = g.shape
    tr = _rows_per_block(r2, cols, N_CHIPS)

    def body(cm_ref, g_ref, x_ref, pb_ref, po_ref):
        pb_ref[...] = (g_ref[...] + x_ref[...]).astype(BF16)
        me = cm_ref[1]
        po_ref[...] = g_ref[me] + x_ref[me]

    return _call(
        body, name="grad_chip_sum",
        num_scalar_prefetch=1, grid=(r2 // tr,),
        in_specs=[pl.BlockSpec((N_CHIPS, None, tr, cols), lambda r, cm: (0, cm[0], r, 0)),
                  pl.BlockSpec((N_CHIPS, tr, cols), lambda r, cm: (0, r, 0))],
        out_specs=[pl.BlockSpec((N_CHIPS, tr, cols), lambda r, cm: (0, r, 0)),
                   pl.BlockSpec((tr, cols), lambda r, cm: (r, 0))],
        out_shape=[jax.ShapeDtypeStruct((N_CHIPS, r2, cols), BF16), jax.ShapeDtypeStruct((r2, cols), F32)],
        compiler_params=_params(1))(cm, g, xbuf)


def _total_sum(pown, rbuf, cm):
    r2, cols = pown.shape
    tr = _rows_per_block(r2, cols, 3)

    def body(cm_ref, p_ref, r_ref, o_ref):
        acc = p_ref[...]
        for j in range(3):
            acc = acc + r_ref[j].astype(F32)
        o_ref[...] = acc

    return _call(
        body, name="grad_total_sum",
        num_scalar_prefetch=1, grid=(r2 // tr,),
        in_specs=[pl.BlockSpec((tr, cols), lambda r, cm: (r, 0)),
                  pl.BlockSpec((3, tr, cols), lambda r, cm: (0, r, 0))],
        out_specs=pl.BlockSpec((None, tr, cols), lambda r, cm: (cm[0], r, 0)),
        out_shape=jax.ShapeDtypeStruct((2, r2, cols), F32), compiler_params=_params(1))(cm, pown, rbuf)


def _small_allreduce(v):
    rows = v.shape[0]
    n_dev = 2 * N_CHIPS

    def body(v_ref, o_ref, buf, ssem, rsem):
        x, y, c, _ = _place()
        me = 4 * x + 2 * y + c
        buf[me] = v_ref[...]

        def copy(d, slot, to):
            return _remote(v_ref, buf.at[slot], ssem.at[d - 1], rsem.at[d - 1], (to // 4, (to // 2) % 2, to % 2))

        cps = [copy(d, me, (me + d) % n_dev) for d in range(1, n_dev)]
        for cp in cps:
            cp.start()
        for d in range(1, n_dev):
            copy(d, (me + n_dev - d) % n_dev, me).wait_recv()
        for cp in cps:
            cp.wait_send()
        acc = buf[0]
        for i in range(1, n_dev):
            acc = acc + buf[i]
        o_ref[...] = acc

    vm = pl.BlockSpec(memory_space=pltpu.VMEM)
    return _call(
        body, name="small_allreduce", in_specs=[vm], out_specs=vm,
        out_shape=jax.ShapeDtypeStruct(v.shape, F32),
        scratch_shapes=[pltpu.VMEM((n_dev, rows, LANES), F32), pltpu.SemaphoreType.DMA((n_dev - 1,)),
                        pltpu.SemaphoreType.DMA((n_dev - 1,))],
        compiler_params=pltpu.CompilerParams(vmem_limit_bytes=V7X_VMEM_LIMIT))(v)


def _adamw_math(w, g, m, v):
    m2 = ADAM_B1 * m + (1.0 - ADAM_B1) * g
    v2 = ADAM_B2 * v + (1.0 - ADAM_B2) * (g * g)
    m_hat = m2 / (1.0 - ADAM_B1 ** ADAM_STEP)
    v_hat = v2 / (1.0 - ADAM_B2 ** ADAM_STEP)
    return -ADAM_LR * (m_hat / (jnp.sqrt(v_hat) + ADAM_EPS) + ADAM_WD * w), m2, v2


def _adamw_layer(w, g, m, v, l, prev):
    _, rows, cols = w.shape
    tr = rows
    for cand in range(8, rows + 1, 8):
        if rows % cand == 0 and cand * cols * 4 <= 2 ** 21:
            tr = cand

    def body(w_ref, g_ref, m_ref, v_ref, *outs):
        go_ref, d_ref, m2_ref, v2_ref = outs[-4:]
        g = g_ref[...]
        go_ref[...] = g
        d_ref[...], m2_ref[...], v2_ref[...] = _adamw_math(w_ref[...], g, m_ref[...], v_ref[...])

    stack = pl.BlockSpec((None, tr, cols), lambda i: (l, i, 0))
    ins, specs, alias = [w, g, m, v], [stack, pl.BlockSpec((tr, cols), lambda i: (i, 0)), stack, stack], {}
    if prev is not None:
        ins += list(prev)
        specs += [ANY] * 4
        alias = {4 + i: i for i in range(4)}
    return _call(
        body, name="adamw", grid=(rows // tr,), in_specs=specs, out_specs=[stack] * 4,
        out_shape=[jax.ShapeDtypeStruct(w.shape, F32)] * 4, input_output_aliases=alias,
        compiler_params=_params(1))(*ins)


def _adamw_small(w, g, m, v):
    def body(w_ref, g_ref, m_ref, v_ref, d_ref, m2_ref, v2_ref):
        d_ref[...], m2_ref[...], v2_ref[...] = _adamw_math(w_ref[...], g_ref[...], m_ref[...], v_ref[...])

    spec = pl.BlockSpec(w.shape, lambda i: (0, 0))
    return _call(
        body, name="adamw_small", grid=(1,), in_specs=[spec] * 4, out_specs=[spec] * 3,
        out_shape=[jax.ShapeDtypeStruct(w.shape, F32)] * 3, compiler_params=_params(1))(w, g, m, v)


SMALL = ("norm_ffn1", "norm_mix", "sinks", "norm_out_sb", "norm_out_swa", "norm_ffn2", "rel_bias", "norm_final")
BIG = ("ffn1_gu", "ffn1_down", "w_in", "w_out", "ffn2_gu", "ffn2_down")


def _pack(parts):
    rows = []
    for a in parts:
        a = a.reshape(-1).astype(F32)
        rows.append(jnp.pad(a, (0, -a.shape[0] % LANES)).reshape(-1, LANES))
    out = jnp.concatenate(rows, axis=0)
    return jnp.pad(out, ((0, -out.shape[0] % 8), (0, 0)))


def _unpack(packed, like):
    out, r = [], 0
    for a in like:
        n = math.prod(a.shape)
        nr = -(-n // LANES)
        out.append(packed[r:r + nr].reshape(-1)[:n].reshape(a.shape))
        r += nr
    return out


def _halved(a):
    k, r, cols = a.shape
    return a.reshape(k, 2, r // 2, cols)


def _weight_view(k, buf):
    full = buf.reshape(N_CHIPS, buf.shape[2] * 2, buf.shape[3])
    return full if k.endswith("_gu") else full.reshape(-1, D_MODEL)


def _grad_stack(k, g):
    if not k.endswith("_gu"):
        g = g.reshape(N_CHIPS, g.shape[0] // N_CHIPS, D_MODEL)
    return _halved(g)


def _empty_like_hbm(shape, dtype):
    return pltpu.with_memory_space_constraint(lax.empty(shape, dtype), pltpu.HBM)


def kernel(x, norm_ffn1, w_ffn1_gu, w_ffn1_down, norm_mix, w_in, sinks, norm_out_sb, norm_out_swa, w_out, norm_ffn2, w_ffn2_gu, w_ffn2_down, rel_bias, norm_final, loss_target, m_norm_ffn1, m_w_ffn1_gu, m_w_ffn1_down, m_norm_mix, m_w_in, m_sinks, m_norm_out_sb, m_norm_out_swa, m_w_out, m_norm_ffn2, m_w_ffn2_gu, m_w_ffn2_down, m_rel_bias, m_norm_final, v_norm_ffn1, v_w_ffn1_gu, v_w_ffn1_down, v_norm_mix, v_w_in, v_sinks, v_norm_out_sb, v_norm_out_swa, v_w_out, v_norm_ffn2, v_w_ffn2_gu, v_w_ffn2_down, v_rel_bias, v_norm_final):
    big_w = dict(ffn1_gu=w_ffn1_gu, ffn1_down=w_ffn1_down, w_in=w_in, w_out=w_out, ffn2_gu=w_ffn2_gu, ffn2_down=w_ffn2_down)
    big_m = dict(ffn1_gu=m_w_ffn1_gu, ffn1_down=m_w_ffn1_down, w_in=m_w_in, w_out=m_w_out, ffn2_gu=m_w_ffn2_gu, ffn2_down=m_w_ffn2_down)
    big_v = dict(ffn1_gu=v_w_ffn1_gu, ffn1_down=v_w_ffn1_down, w_in=v_w_in, w_out=v_w_out, ffn2_gu=v_w_ffn2_gu, ffn2_down=v_w_ffn2_down)
    small = dict(norm_ffn1=norm_ffn1, norm_mix=norm_mix, sinks=sinks, norm_out_sb=norm_out_sb, norm_out_swa=norm_out_swa,
                 norm_ffn2=norm_ffn2, rel_bias=rel_bias, norm_final=norm_final)
    small_m = dict(norm_ffn1=m_norm_ffn1, norm_mix=m_norm_mix, sinks=m_sinks, norm_out_sb=m_norm_out_sb,
                   norm_out_swa=m_norm_out_swa, norm_ffn2=m_norm_ffn2, rel_bias=m_rel_bias, norm_final=m_norm_final)
    small_v = dict(norm_ffn1=v_norm_ffn1, norm_mix=v_norm_mix, sinks=v_sinks, norm_out_sb=v_norm_out_sb,
                   norm_out_swa=v_norm_out_swa, norm_ffn2=v_norm_ffn2, rel_bias=v_rel_bias, norm_final=v_norm_final)
    for dct in (big_w, big_m, big_v):
        dct["w_in"] = jnp.swapaxes(dct["w_in"], 1, 2)
    _PREVIOUS[0] = None
    _, _, c, me = _place()
    cm = jnp.stack([c, me]).astype(jnp.int32)
    buckets = jnp.asarray(_bucket_table())
    ffn1, mix_in, rest = ("ffn1_gu", "ffn1_down"), ("w_in",), ("w_out", "ffn2_gu", "ffn2_down")

    def place(l, keys):
        return [_place_own(big_w[k], l, cm[1:]) for k in keys]

    def views(keys, bufs):
        return {k: _weight_view(k, b) for k, b in zip(keys, bufs)}

    def gather_start(tag, bufs):
        return _exchange_start(f"gather{tag}_ici_start", _plan_gather_ici, bufs, 3 * len(bufs))

    def gather_pass(tag, flight):
        bufs = _exchange_wait(f"gather{tag}_ici_wait", _plan_gather_ici, flight[1], flight[0])
        return _exchange_start(f"gather{tag}_d2d_start", _plan_gather_d2d, bufs, 3 * len(bufs))

    def gather_done(tag, keys, flight):
        return views(keys, _exchange_wait(f"gather{tag}_d2d_wait", _plan_gather_d2d, flight[1], flight[0]))

    w0 = views(ffn1, _gather_now(place(0, ffn1)))
    fly_in0 = gather_start("0b", place(0, mix_in))
    fly_rest0 = gather_start("0c", place(0, rest))
    bias = _bias_table(rel_bias, buckets)
    fly_ffn1 = gather_start("1a", place(1, ffn1))
    fly_rest1 = gather_start("1b", place(1, mix_in + rest))

    s0 = _fwd_ffn1(x[0], _norm_cast(x[0], _row(norm_ffn1[0])), w0, small, 0)
    w0.update(gather_done("0b", mix_in, gather_pass("0b", fly_in0)))
    _fwd_proj_sb(s0, w0)
    fly_rest0 = gather_pass("0c", fly_rest0)
    _fwd_swa(s0, small, 0, bias)
    w0.update(gather_done("0c", rest, fly_rest0))
    h, n1 = _fwd_out_ffn2(s0, w0, small, 0, _row(norm_ffn1[1]))
    fly_ffn1 = gather_pass("1a", fly_ffn1)
    fly_rest1 = gather_pass("1b", fly_rest1)
    w1 = gather_done("1a", ffn1, fly_ffn1)
    s1 = _fwd_ffn1(h, n1, w1, small, 1)
    w1.update(gather_done("1b", mix_in + rest, fly_rest1))
    _fwd_proj_sb(s1, w1)
    _fwd_swa(s1, small, 1, bias)
    h, _ = _fwd_out_ffn2(s1, w1, small, 1, _row(norm_final))
    dh, dg_final, loss_row = _loss_head(h, _row(norm_final), loss_target[0])

    def landing(stacks, lead, dtype):
        return [_empty_like_hbm((lead,) + a.shape[2:], dtype) for a in stacks]

    def reduce_begin(tag, keys, gw):
        stacks = [_grad_stack(k, gw[k]) for k in keys]
        flight = _exchange_start(f"grad{tag}_sibling_start", _plan_grad_sibling,
                                 stacks + landing(stacks, N_CHIPS, F32), len(keys))
        return dict(tag=tag, keys=keys, stacks=stacks, flight=flight)

    def reduce_chips(st):
        n, (sems, bufs) = len(st["keys"]), st["flight"]
        bufs = _exchange_wait(f"grad{st['tag']}_sibling_wait", _plan_grad_sibling, bufs, sems)
        st["sums"] = [_chip_sum(g, z, cm) for g, z in zip(bufs[:n], bufs[n:])]
        st["flight"] = _exchange_start(f"grad{st['tag']}_chips_start", _plan_grad_chips,
                                       [s[0] for s in st["sums"]] + landing(st["stacks"], 3, BF16), 3 * n)

    def reduce_halves(st):
        n, (sems, bufs) = len(st["keys"]), st["flight"]
        bufs = _exchange_wait(f"grad{st['tag']}_chips_wait", _plan_grad_chips, bufs, sems)
        halves = [_total_sum(s[1], z, cm) for s, z in zip(st["sums"], bufs[n:])]
        st["flight"] = _exchange_start(f"grad{st['tag']}_halves_start", _plan_grad_halves, halves, n)

    def reduce_end(st):
        sems, bufs = st["flight"]
        bufs = _exchange_wait(f"grad{st['tag']}_halves_wait", _plan_grad_halves, bufs, sems)
        return {k: b.reshape(big_w[k].shape[1:]) for k, b in zip(st["keys"], bufs)}

    def adamw(reduced, l, prev):
        return {k: _adamw_layer(big_w[k], g, big_m[k], big_v[k], l, None if prev is None else prev[k])
                for k, g in reduced.items()}

    gsm = [dict() for _ in range(DEPTH)]
    dbias = jnp.zeros((8, BLK, 2 * BLK), F32)
    dh, gw1, gs = _bwd_ffn(dh, s1, w1, small, 1, 2)
    gsm[1].update(gs)
    dh, gw, gs, dbias = _bwd_mix(dh, s1, w1, small, 1, bias, dbias)
    gw1.update(gw)
    gsm[1].update(gs)
    dh, gw, gs = _bwd_ffn(dh, s1, w1, small, 1, 1)
    gw1.update(gw)
    gsm[1].update(gs)

    red1 = reduce_begin("1", BIG, gw1)
    dh, gw0, gs = _bwd_ffn(dh, s0, w0, small, 0, 2)
    gsm[0].update(gs)
    reduce_chips(red1)
    dh, gw, gs, dbias = _bwd_mix(dh, s0, w0, small, 0, bias, dbias)
    gw0.update(gw)
    gsm[0].update(gs)
    red0a = reduce_begin("0a", ("ffn2_gu", "ffn2_down", "w_out", "w_in"), gw0)
    reduce_halves(red1)
    dgu = _bwd_ffn_dact(dh, s0, w0, 1)
    reduce_chips(red0a)
    dh, gw, gs = _bwd_ffn_rest(dh, dgu, s0, w0, small, 0, 1)
    gsm[0].update(gs)
    red0b = reduce_begin("0b", ffn1, gw)
    reduced1 = reduce_end(red1)
    stacks = adamw({k: reduced1[k] for k in ffn1}, 1, None)

    gsmall = {k: jnp.stack([gsm[l][k].reshape(-1) for l in range(DEPTH)]) for k in gsm[0]}
    gsmall["rel_bias"] = jnp.transpose(_bias_grad(dbias, buckets)[:, :N_BUCKETS])
    gsmall["norm_final"] = dg_final.reshape(-1)
    small_like = [small[k] for k in SMALL]
    pk = lambda dct: _pack([dct[k] for k in SMALL])
    red = _small_allreduce(_pack([gsmall[k] for k in SMALL] + [loss_row[0, :1]]))
    gs = _unpack(red, small_like + [loss_row[0, :1]])
    loss = gs[-1][0]
    gs = dict(zip(SMALL, gs[:-1]))

    reduce_chips(red0b)
    stacks.update(adamw({k: reduced1[k] for k in mix_in + rest}, 1, None))
    dlt, m2, v2 = _adamw_small(pk(small), pk(gs), pk(small_m), pk(small_v))
    reduce_halves(red0a)
    stacks.update(adamw(reduce_end(red0a), 0, stacks))
    reduce_halves(red0b)
    stacks.update(adamw(reduce_end(red0b), 0, stacks))

    out_g, out_d, out_m, out_v = {}, {}, {}, {}
    for k in BIG:
        out_g[k], out_d[k], out_m[k], out_v[k] = [jnp.swapaxes(a, 1, 2) if k == "w_in" else a for a in stacks[k]]
    for dst, packed in ((out_d, dlt), (out_m, m2), (out_v, v2)):
        dst.update(zip(SMALL, _unpack(packed, small_like)))
    out_g.update(gs)

    order = ("norm_ffn1", "ffn1_gu", "ffn1_down", "norm_mix", "w_in", "sinks", "norm_out_sb", "norm_out_swa", "w_out",
             "norm_ffn2", "ffn2_gu", "ffn2_down", "rel_bias", "norm_final")
    return (loss, dh.reshape(x.shape), *[out_g[k] for k in order], *[out_d[k] for k in order],
            *[out_m[k] for k in order], *[out_v[k] for k in order])
```

```python
import math

import numpy as np
import jax
import jax.numpy as jnp
from jax import lax
from jax.experimental import pallas as pl
from jax.experimental.pallas import tpu as pltpu

F32 = jnp.float32
BF16 = jnp.bfloat16

D_MODEL = 1024
DEPTH = 2
HEAD_DIM = 64
BLK = 128
N_BUCKETS = 32
MAX_DISTANCE = 128
D_FF = 2816
EPS = 1e-6
NEG_INF = -1e30
SB_W = 512
SWA_W = 512
KV_W = 128
IN_W = 2304
SCALE = HEAD_DIM ** -0.5
N_CHIPS = 4
FS = 2 * D_FF // N_CHIPS
LANES = 128
V7X_VMEM_LIMIT = 56 * 2 ** 20
TM = 512
SB_KT = 512
SWA_G = 4
SWA_UNROLL = 3

ADAM_LR = 0.001
ADAM_B1 = 0.9
ADAM_B2 = 0.999
ADAM_EPS = 1e-08
ADAM_WD = 0.01
ADAM_STEP = 10

MESH = pl.DeviceIdType.MESH
ANY = pl.BlockSpec(memory_space=pl.ANY)
HBM = pl.BlockSpec(memory_space=pltpu.HBM)
SEM = pl.BlockSpec(memory_space=pltpu.SEMAPHORE)
EFFECT = pltpu.SideEffectType.DATAFLOW_SIDE_EFFECTING


def _params(n_grid):
    return pltpu.CompilerParams(dimension_semantics=("arbitrary",) * n_grid, vmem_limit_bytes=V7X_VMEM_LIMIT)


_PREVIOUS = [None]


def _call(body, *, name, in_specs, out_specs, out_shape, grid=(), num_scalar_prefetch=0, scratch_shapes=(),
          input_output_aliases=None, compiler_params=None, hbm_args=0):
    n_in = len(in_specs)

    def run(*args):
        dep = _PREVIOUS[0]
        if any(dep is a for a in args):
            dep = None
        args = [pltpu.with_memory_space_constraint(a, pltpu.HBM) if i < hbm_args else a for i, a in enumerate(args)]
        specs = list(in_specs) + ([ANY] if dep is not None else [])
        k = num_scalar_prefetch + n_in
        fn = body if dep is None else (lambda *refs: body(*refs[:k], *refs[k + 1:]))
        if num_scalar_prefetch:
            shape = dict(grid_spec=pltpu.PrefetchScalarGridSpec(
                num_scalar_prefetch=num_scalar_prefetch, grid=grid, in_specs=specs, out_specs=out_specs,
                scratch_shapes=scratch_shapes))
        else:
            shape = dict(grid=grid, in_specs=specs, out_specs=out_specs, scratch_shapes=scratch_shapes)
        out = pl.pallas_call(fn, name=name, out_shape=out_shape, input_output_aliases=input_output_aliases or {},
                             compiler_params=compiler_params, **shape)(*args, *([] if dep is None else [dep]))
        _PREVIOUS[0] = jax.tree.leaves(out)[-1]
        return out

    return run


def _dot(a, b):
    return jnp.dot(a, b, preferred_element_type=F32)


def _dot_nt(a, b):
    return lax.dot_general(a, b, (((1,), (1,)), ((), ())), preferred_element_type=F32)


def _dot_tn(a, b):
    return lax.dot_general(a, b, (((0,), (0,)), ((), ())), preferred_element_type=F32)


def _rms_fwd(x, g):
    r = lax.rsqrt(jnp.mean(x * x, axis=-1, keepdims=True) + EPS)
    xh = x * r
    return xh * g, xh, r


def _rms_bwd(dy, xh, r, g):
    u = dy * g
    dx = r * (u - xh * jnp.mean(u * xh, axis=-1, keepdims=True))
    dg = jnp.sum(dy * xh, axis=0, keepdims=True)
    return dx, dg


def _softplus(z):
    neg_abs = lax.bitcast_convert_type(lax.bitcast_convert_type(z, jnp.int32) | jnp.int32(-2 ** 31), F32)
    sp = jnp.maximum(z, 0.0) + jnp.log(1.0 + jnp.exp(neg_abs))
    return sp, z - sp


def _norm_cast(h, g):
    t, w = h.shape

    def body(h_ref, g_ref, n_ref):
        y, _, _ = _rms_fwd(h_ref[...], g_ref[...])
        n_ref[...] = y.astype(BF16)

    return _call(
        body, name="norm_cast", grid=(t // TM,),
        in_specs=[pl.BlockSpec((TM, w), lambda i: (i, 0)), pl.BlockSpec((1, w), lambda i: (0, 0))],
        out_specs=pl.BlockSpec((TM, w), lambda i: (i, 0)),
        out_shape=jax.ShapeDtypeStruct((t, w), BF16), compiler_params=_params(1))(h, g)


def _ffn_gu(n, wgu):
    t, d = n.shape

    def body(n_ref, wg_ref, wu_ref, gu_ref, act_ref):
        x = n_ref[...]
        g = _dot(x, wg_ref[...])
        u = _dot(x, wu_ref[...])
        sig = jax.nn.sigmoid(g)
        silu = g * sig
        gu_ref[0] = (u * (sig + silu * (1.0 - sig))).astype(BF16)
        gu_ref[1] = silu.astype(BF16)
        act_ref[...] = (silu * u).astype(BF16)

    return _call(
        body, name="ffn_gu", grid=(2, t // TM),
        in_specs=[pl.BlockSpec((TM, d), lambda j, i: (i, 0)),
                  pl.BlockSpec((None, d, FS), lambda j, i: (j, 0, 0)),
                  pl.BlockSpec((None, d, FS), lambda j, i: (j + 2, 0, 0))],
        out_specs=[pl.BlockSpec((2, TM, FS), lambda j, i: (0, i, j)), pl.BlockSpec((TM, FS), lambda j, i: (i, j))],
        out_shape=[jax.ShapeDtypeStruct((2, t, D_FF), BF16), jax.ShapeDtypeStruct((t, D_FF), BF16)],
        compiler_params=_params(2))(n, wgu, wgu)


def _down_res(act, wdn, h, g_next):
    t, f = act.shape
    d = h.shape[1]

    def body(a_ref, w_ref, h_ref, g_ref, o_ref, n_ref):
        out = h_ref[...] + 0.5 * _dot(a_ref[...], w_ref[...])
        o_ref[...] = out
        n_ref[...] = _rms_fwd(out, g_ref[...])[0].astype(BF16)

    row = pl.BlockSpec((TM, d), lambda i: (i, 0))
    return _call(
        body, name="down_res", grid=(t // TM,),
        in_specs=[pl.BlockSpec((TM, f), lambda i: (i, 0)), pl.BlockSpec((f, d), lambda i: (0, 0)), row,
                  pl.BlockSpec((1, d), lambda i: (0, 0))],
        out_specs=[row, row],
        out_shape=[jax.ShapeDtypeStruct((t, d), F32), jax.ShapeDtypeStruct((t, d), BF16)],
        compiler_params=_params(1))(act, wdn, h, g_next)


def _proj(n, w_in_t):
    t, d = n.shape
    w = w_in_t.shape[0]

    def body(n_ref, w_ref, o_ref):
        o_ref[...] = _dot_nt(n_ref[...], w_ref[...]).astype(BF16)

    return _call(
        body, name="proj", grid=(t // TM,),
        in_specs=[pl.BlockSpec((TM, d), lambda i: (i, 0)), pl.BlockSpec((w, d), lambda i: (0, 0))],
        out_specs=pl.BlockSpec((TM, w), lambda i: (i, 0)),
        out_shape=jax.ShapeDtypeStruct((t, w), BF16), compiler_params=_params(1))(n, w_in_t)


def _out_res(o_sb, o_sw, g_sb, g_sw, w_out, h, g_next):
    t, d = h.shape

    def body(a_ref, b_ref, ga_ref, gb_ref, w_ref, h_ref, g_ref, o_ref, mix_ref, n_ref):
        ya, _, _ = _rms_fwd(a_ref[...], ga_ref[...])
        yb, _, _ = _rms_fwd(b_ref[...], gb_ref[...])
        mixed = jnp.concatenate([ya.astype(BF16), yb.astype(BF16)], axis=1)
        mix_ref[...] = mixed
        out = h_ref[...] + _dot(mixed, w_ref[...])
        o_ref[...] = out
        n_ref[...] = _rms_fwd(out, g_ref[...])[0].astype(BF16)

    row = pl.BlockSpec((TM, d), lambda i: (i, 0))
    return _call(
        body, name="out_res", grid=(t // TM,),
        in_specs=[pl.BlockSpec((TM, SB_W), lambda i: (i, 0)), pl.BlockSpec((TM, SWA_W), lambda i: (i, 0)),
                  pl.BlockSpec((1, SB_W), lambda i: (0, 0)), pl.BlockSpec((1, SWA_W), lambda i: (0, 0)),
                  pl.BlockSpec((d, d), lambda i: (0, 0)), row, pl.BlockSpec((1, d), lambda i: (0, 0))],
        out_specs=[row, row, row],
        out_shape=[jax.ShapeDtypeStruct((t, d), F32), jax.ShapeDtypeStruct((t, d), BF16),
                   jax.ShapeDtypeStruct((t, d), BF16)],
        compiler_params=_params(1))(o_sb, o_sw, g_sb, g_sw, w_out, h, g_next)


def _loss_head(h, g, tgt):
    t, d = h.shape

    def body(h_ref, g_ref, t_ref, dh_ref, dg_ref, loss_ref):
        @pl.when(pl.program_id(0) == 0)
        def _():
            dg_ref[...] = jnp.zeros_like(dg_ref)
            loss_ref[...] = jnp.zeros_like(loss_ref)

        gg = g_ref[...]
        y, xh, r = _rms_fwd(h_ref[...], gg)
        err = y - t_ref[...]
        part = 0.5 * jnp.sum(jnp.sum(err * err, axis=1, keepdims=True) / d, axis=0, keepdims=True)
        loss_ref[...] += jnp.broadcast_to(part, loss_ref.shape)
        dx, dg = _rms_bwd(err / d, xh, r, gg)
        dh_ref[...] = dx
        dg_ref[...] += dg

    return _call(
        body, name="loss_head", grid=(t // TM,),
        in_specs=[pl.BlockSpec((TM, d), lambda i: (i, 0)), pl.BlockSpec((1, d), lambda i: (0, 0)),
                  pl.BlockSpec((TM, d), lambda i: (i, 0))],
        out_specs=[pl.BlockSpec((TM, d), lambda i: (i, 0)), pl.BlockSpec((1, d), lambda i: (0, 0)),
                   pl.BlockSpec((1, LANES), lambda i: (0, 0))],
        out_shape=[jax.ShapeDtypeStruct((t, d), F32), jax.ShapeDtypeStruct((1, d), F32),
                   jax.ShapeDtypeStruct((1, LANES), F32)],
        compiler_params=_params(1))(h, g, tgt)


def _ffn_dact(dh, wdn, gu):
    t, d = dh.shape
    tm = TM

    def body(dh_ref, w_ref, gu_ref, o_ref):
        da = 0.5 * _dot_nt(dh_ref[...].astype(BF16), w_ref[...])
        o_ref[0] = (da * gu_ref[0].astype(F32)).astype(BF16)
        o_ref[1] = (da * gu_ref[1].astype(F32)).astype(BF16)

    return _call(
        body, name="ffn_dact", grid=(2, t // tm),
        in_specs=[pl.BlockSpec((tm, d), lambda j, i: (i, 0)), pl.BlockSpec((FS, d), lambda j, i: (j, 0)),
                  pl.BlockSpec((2, tm, FS), lambda j, i: (0, i, j))],
        out_specs=pl.BlockSpec((2, tm, FS), lambda j, i: (0, i, j)),
        out_shape=jax.ShapeDtypeStruct((2, t, D_FF), BF16), compiler_params=_params(2))(dh, wdn, gu)


def _dn_norm_bwd(a, a_spec, w, w_spec, nk, dh, h_in, g, w_transposed=False):
    t, d = dh.shape
    mm = _dot if w_transposed else _dot_nt

    def body(a_ref, w_ref, dh_ref, h_ref, g_ref, o_ref, dg_ref, acc_ref):
        i, k = pl.program_id(0), pl.program_id(1)

        if nk > 1:
            @pl.when(k == 0)
            def _():
                acc_ref[...] = mm(a_ref[...], w_ref[...])

            @pl.when((k > 0) & (k < nk - 1))
            def _():
                acc_ref[...] += mm(a_ref[...], w_ref[...])

        @pl.when(k == nk - 1)
        def _():
            gg = g_ref[...]
            dg = jnp.zeros_like(gg)
            for rows in (slice(0, TM // 2), slice(TM // 2, TM)):
                dn = mm(a_ref[rows, :], w_ref[...])
                if nk > 1:
                    dn = dn + acc_ref[rows, :]
                _, xh, r = _rms_fwd(h_ref[rows, :], gg)
                dx, dg_rows = _rms_bwd(dn, xh, r, gg)
                o_ref[rows, :] = dh_ref[rows, :] + dx
                dg = dg + dg_rows

            @pl.when(i == 0)
            def _():
                dg_ref[...] = dg

            @pl.when(i > 0)
            def _():
                dg_ref[...] += dg

    row = pl.BlockSpec((TM, d), lambda i, k: (i, 0))
    return _call(
        body, name="dn_norm_bwd", grid=(t // TM, nk),
        in_specs=[a_spec, w_spec, row, row, pl.BlockSpec((1, d), lambda i, k: (0, 0))],
        out_specs=[row, pl.BlockSpec((1, d), lambda i, k: (0, 0))],
        out_shape=[jax.ShapeDtypeStruct((t, d), F32), jax.ShapeDtypeStruct((1, d), F32)],
        scratch_shapes=[pltpu.VMEM((TM, d), F32)], compiler_params=_params(2))(a, w, dh, h_in, g)


def _ffn_dn(dgu, wgu, dh, h_in, g):
    d = dh.shape[1]
    return _dn_norm_bwd(
        dgu, pl.BlockSpec((None, TM, FS), lambda i, k: (k // 2, i, k % 2)),
        wgu, pl.BlockSpec((None, d, FS), lambda i, k: (k, 0, 0)), N_CHIPS, dh, h_in, g)


def _mix_dn(dproj, w_in_t, dh, h_in, g):
    d = dh.shape[1]
    w = dproj.shape[1]
    return _dn_norm_bwd(
        dproj, pl.BlockSpec((TM, w), lambda i, k: (i, 0)),
        w_in_t, pl.BlockSpec((w, d), lambda i, k: (0, 0)), 1, dh, h_in, g, w_transposed=True)


def _dmixed(dh, w_out, o_sb, o_sw, g_sb, g_sw):
    t, d = dh.shape

    def body(dh_ref, w_ref, a_ref, b_ref, ga_ref, gb_ref, o_ref, dga_ref, dgb_ref):
        i = pl.program_id(0)
        dm = _dot_nt(dh_ref[...].astype(BF16), w_ref[...])
        _, xa, ra = _rms_fwd(a_ref[...], ga_ref[...])
        _, xb, rb = _rms_fwd(b_ref[...], gb_ref[...])
        da, dga = _rms_bwd(dm[:, :SB_W], xa, ra, ga_ref[...])
        db, dgb = _rms_bwd(dm[:, SB_W:], xb, rb, gb_ref[...])
        o_ref[...] = jnp.concatenate([da.astype(BF16), db.astype(BF16)], axis=1)

        @pl.when(i == 0)
        def _():
            dga_ref[...] = dga
            dgb_ref[...] = dgb

        @pl.when(i > 0)
        def _():
            dga_ref[...] += dga
            dgb_ref[...] += dgb

    return _call(
        body, name="dmixed", grid=(t // TM,),
        in_specs=[pl.BlockSpec((TM, d), lambda i: (i, 0)), pl.BlockSpec((d, d), lambda i: (0, 0)),
                  pl.BlockSpec((TM, SB_W), lambda i: (i, 0)), pl.BlockSpec((TM, SWA_W), lambda i: (i, 0)),
                  pl.BlockSpec((1, SB_W), lambda i: (0, 0)), pl.BlockSpec((1, SWA_W), lambda i: (0, 0))],
        out_specs=[pl.BlockSpec((TM, d), lambda i: (i, 0)), pl.BlockSpec((1, SB_W), lambda i: (0, 0)),
                   pl.BlockSpec((1, SWA_W), lambda i: (0, 0))],
        out_shape=[jax.ShapeDtypeStruct((t, d), BF16), jax.ShapeDtypeStruct((1, SB_W), F32),
                   jax.ShapeDtypeStruct((1, SWA_W), F32)],
        compiler_params=_params(1))(dh, w_out, o_sb, o_sw, g_sb, g_sw)


def _wgrad(name, a, a_spec, b, b_spec, grid, out_shape, out_spec, scale):
    def body(a_ref, b_ref, o_ref):
        r = _dot_tn(a_ref[...], b_ref[...].astype(BF16))
        o_ref[...] = r if scale == 1.0 else scale * r

    return _call(
        body, name=name, grid=grid, in_specs=[a_spec, b_spec], out_specs=out_spec,
        out_shape=jax.ShapeDtypeStruct(out_shape, F32), compiler_params=_params(len(grid)))(a, b)


def _wgrad_gu(n, dgu):
    t, d = n.shape
    return _wgrad(
        "wgrad_gu", n, pl.BlockSpec((t, TM), lambda s, r: (0, r)),
        dgu, pl.BlockSpec((None, t, FS), lambda s, r: (s // 2, 0, s % 2)), (N_CHIPS, d // TM),
        (N_CHIPS, d, FS), pl.BlockSpec((None, TM, FS), lambda s, r: (s, r, 0)), 1.0)


def _wgrad_down(act, dh):
    t, d = dh.shape
    return _wgrad(
        "wgrad_down", act, pl.BlockSpec((t, FS), lambda s, r: (0, s)), dh, pl.BlockSpec((t, TM), lambda s, r: (0, r)),
        (2, d // TM), (D_FF, d), pl.BlockSpec((FS, TM), lambda s, r: (s, r)), 0.5)


def _wgrad_out(mixed, dh):
    t, d = dh.shape
    return _wgrad(
        "wgrad_out", mixed, pl.BlockSpec((t, TM), lambda s: (0, s)), dh, pl.BlockSpec((t, d), lambda s: (0, 0)),
        (d // TM,), (d, d), pl.BlockSpec((TM, d), lambda s: (s, 0)), 1.0)


def _wgrad_in(n, dproj):
    t, d = n.shape
    w = dproj.shape[1]
    tw = w // 3
    return _wgrad(
        "wgrad_in", dproj, pl.BlockSpec((t, tw), lambda s: (0, s)), n, pl.BlockSpec((t, d), lambda s: (0, 0)),
        (3,), (w, d), pl.BlockSpec((tw, d), lambda s: (s, 0)), 1.0)


def _tri(rel):
    row = lax.broadcasted_iota(jnp.int32, (BLK, BLK), 0)
    col = lax.broadcasted_iota(jnp.int32, (BLK, BLK), 1)
    m = rel(row, col).astype(BF16)
    return jnp.concatenate([m, m], axis=0)


def _scan_dot(x, tri2):
    hi = x.astype(BF16)
    lo = (x - hi.astype(F32)).astype(BF16)
    return _dot(jnp.concatenate([hi, lo], axis=1), tri2)


def _head_masks():
    lane = lax.broadcasted_iota(jnp.int32, (1, LANES), 1)
    return [lane < HEAD_DIM, lane >= HEAD_DIM]


def _sb_dcol():
    dcol = lax.broadcasted_iota(jnp.int32, (BLK, SB_KT), 1) - lax.broadcasted_iota(jnp.int32, (BLK, SB_KT), 0)
    return jnp.concatenate([dcol, dcol], axis=0)


def _sb_fwd(proj):
    t = proj.shape[0]
    nq = t // BLK
    nb = SB_KT // BLK

    def body(q_ref, k_ref, v_ref, o_ref, tot_ref):
        hm = _head_masks()
        dcol = _sb_dcol()
        after = _tri(lambda r, c: r > c)

        def tile(qh, kt, carry, acc, limit):
            ks = pl.ds(pl.multiple_of(kt * SB_KT, SB_KT), SB_KT)
            z = _dot_nt(qh, k_ref[ks, :])
            sp, zs = _softplus(z)
            valid = None if limit is None else dcol < limit
            spm = sp if valid is None else jnp.where(valid, sp, 0.0)
            sufs = [None] * nb
            for b in reversed(range(nb)):
                blk = spm[:, b * BLK:(b + 1) * BLK]
                sufs[b] = carry + _scan_dot(blk, after)
                carry = carry + jnp.sum(blk, axis=1, keepdims=True)
            w = jnp.exp(zs - jnp.concatenate(sufs, axis=1))
            if valid is not None:
                w = jnp.where(valid, w, 0.0)
            return carry, acc + _dot(w.astype(BF16), v_ref[ks, :])

        def qblock(qi, _):
            qs = pl.ds(pl.multiple_of(qi * BLK, BLK), BLK)
            q = q_ref[qs, :] * SCALE
            kd = qi // nb
            limit = (qi - kd * nb) * BLK
            qh = jnp.concatenate([jnp.where(m, q, jnp.zeros_like(q)) for m in hm], axis=0)
            c0 = tile(qh, kd, jnp.zeros((2 * BLK, 1), F32), jnp.zeros((2 * BLK, LANES), F32), limit)
            carry, acc = lax.fori_loop(0, kd, lambda n, c: tile(qh, kd - 1 - n, c[0], c[1], None), c0)
            o_ref[qs, :] = jnp.where(hm[0], acc[:BLK], acc[BLK:])
            for h in range(2):
                tot_ref[h, qs, :] = jnp.broadcast_to(carry[h * BLK:(h + 1) * BLK], (BLK, LANES))
            return 0

        lax.fori_loop(0, nq, qblock, 0)

    col_blk = lambda off: pl.BlockSpec((t, LANES), lambda p: (0, off + p))
    return _call(
        body, name="sb_fwd", grid=(4,), in_specs=[col_blk(0), col_blk(4), col_blk(8)],
        out_specs=[pl.BlockSpec((t, LANES), lambda p: (0, p)), pl.BlockSpec((2, t, LANES), lambda p: (p, 0, 0))],
        out_shape=[jax.ShapeDtypeStruct((t, SB_W), F32), jax.ShapeDtypeStruct((8, t, LANES), F32)],
        compiler_params=_params(1))(proj, proj, proj)


def _sb_bwd(proj, d_o, tot):
    t = proj.shape[0]
    nq = t // BLK
    nb = SB_KT // BLK

    def body(q_ref, k_ref, v_ref, do_ref, tot_ref, dq_ref, dk_ref, dv_ref, dk_acc, dv_acc):
        hm = _head_masks()
        dcol = _sb_dcol()
        before = _tri(lambda r, c: r < c)
        upto = _tri(lambda r, c: r <= c)
        dk_acc[...] = jnp.zeros_like(dk_acc)
        dv_acc[...] = jnp.zeros_like(dv_acc)

        def tile(qh, doh, tt, kt, pre, ecum, dq, limit):
            ks = pl.ds(pl.multiple_of(kt * SB_KT, SB_KT), SB_KT)
            k = k_ref[ks, :]
            v = v_ref[ks, :]
            z = _dot_nt(qh, k)
            sp, zs = _softplus(z)
            valid = None if limit is None else dcol < limit
            spm = sp if valid is None else jnp.where(valid, sp, 0.0)
            pres = []
            for b in range(nb):
                blk = spm[:, b * BLK:(b + 1) * BLK]
                pres.append(pre + _scan_dot(blk, before))
                pre = pre + jnp.sum(blk, axis=1, keepdims=True)
            logw = z - (tt - jnp.concatenate(pres, axis=1))
            if valid is not None:
                logw = jnp.minimum(logw, 0.0)
            w = jnp.exp(logw)
            if valid is not None:
                w = jnp.where(valid, w, 0.0)
            e = w * _dot_nt(doh, v)
            incs = []
            for b in range(nb):
                blk = e[:, b * BLK:(b + 1) * BLK]
                incs.append(ecum + _scan_dot(blk, upto))
                ecum = ecum + jnp.sum(blk, axis=1, keepdims=True)
            dz = e - jnp.exp(zs) * jnp.concatenate(incs, axis=1)
            if valid is not None:
                dz = jnp.where(valid, dz, 0.0)
            dzb = dz.astype(BF16)
            dk_acc[ks, :] += _dot_tn(dzb, qh)
            dv_acc[ks, :] += _dot_tn(w.astype(BF16), doh)
            return pre, ecum, dq + _dot(dzb, k)

        def qblock(qi, _):
            qs = pl.ds(pl.multiple_of(qi * BLK, BLK), BLK)
            q = q_ref[qs, :] * SCALE
            do = do_ref[qs, :]
            kd = qi // nb
            limit = (qi - kd * nb) * BLK
            qh = jnp.concatenate([jnp.where(m, q, jnp.zeros_like(q)) for m in hm], axis=0)
            doh = jnp.concatenate([jnp.where(m, do, jnp.zeros_like(do)) for m in hm], axis=0)
            tt = jnp.concatenate([tot_ref[h, qs, 0:1] for h in range(2)], axis=0)
            c0 = (jnp.zeros((2 * BLK, 1), F32), jnp.zeros((2 * BLK, 1), F32), jnp.zeros((2 * BLK, LANES), F32))
            c = lax.fori_loop(0, kd, lambda kt, c: tile(qh, doh, tt, kt, c[0], c[1], c[2], None), c0)
            dq = tile(qh, doh, tt, kd, c[0], c[1], c[2], limit)[2]
            dq_ref[qs, :] = (jnp.where(hm[0], dq[:BLK], dq[BLK:]) * SCALE).astype(BF16)
            return 0

        lax.fori_loop(0, nq, qblock, 0)
        dk_ref[...] = dk_acc[...].astype(BF16)
        dv_ref[...] = dv_acc[...].astype(BF16)

    col_blk = lambda off: pl.BlockSpec((t, LANES), lambda p: (0, off + p))
    out = jax.ShapeDtypeStruct((t, SB_W), BF16)
    return _call(
        body, name="sb_bwd", grid=(4,),
        in_specs=[col_blk(0), col_blk(4), col_blk(8), col_blk(0), pl.BlockSpec((2, t, LANES), lambda p: (p, 0, 0))],
        out_specs=[col_blk(0), col_blk(0), col_blk(0)], out_shape=[out, out, out],
        scratch_shapes=[pltpu.VMEM((t, LANES), F32), pltpu.VMEM((t, LANES), F32)],
        compiler_params=_params(1))(proj, proj, proj, d_o, tot)


def _bucket_table():
    a = np.arange(BLK)[:, None]
    c = np.arange(2 * BLK)[None, :]
    dist = np.maximum(BLK + a - c, 0)
    max_exact = N_BUCKETS // 2
    dd = np.maximum(dist, 1).astype(np.float32)
    large = max_exact + (np.log(dd / max_exact) / math.log(MAX_DISTANCE / max_exact)
                         * (N_BUCKETS - max_exact)).astype(np.int32)
    large = np.minimum(large, N_BUCKETS - 1)
    return np.where(dist < max_exact, dist, large).astype(np.int32)


def _swa_band_masks():
    row = lax.broadcasted_iota(jnp.int32, (SWA_G * BLK, 2 * BLK), 0) & (BLK - 1)
    col = lax.broadcasted_iota(jnp.int32, (SWA_G * BLK, 2 * BLK), 1)
    own = lax.broadcasted_iota(jnp.int32, (SWA_G * BLK, BLK), 1) <= (
        lax.broadcasted_iota(jnp.int32, (SWA_G * BLK, BLK), 0) & (BLK - 1))
    return (col > row) & ((col < BLK) | (col - BLK <= row)), own


def _swa_stack(ref, qs, kvh, kvmask, scale):
    parts = []
    for g in range(SWA_G):
        hq = SWA_G * kvh + g
        x = ref[qs, (hq // 2) * LANES:(hq // 2 + 1) * LANES].astype(F32)
        if hq % 2 != kvh:
            x = pltpu.roll(x, HEAD_DIM, 1)
        parts.append(jnp.where(kvmask, x * scale, 0.0).astype(BF16))
    return jnp.concatenate(parts, axis=0)


def _swa_unstack(x4, kvh, hm):
    heads = []
    for g in range(SWA_G):
        x = x4[g * BLK:(g + 1) * BLK]
        heads.append(pltpu.roll(x, HEAD_DIM, 1) if g % 2 != kvh else x)
    return [jnp.where(hm[0], heads[0], heads[1]), jnp.where(hm[0], heads[2], heads[3])]


def _swa_scores(q4, kb, bias_ref, kvh, mask, cols):
    bias4 = jnp.concatenate([bias_ref[SWA_G * kvh + g, :, cols] for g in range(SWA_G)], axis=0)
    return jnp.where(mask, _dot_nt(q4, kb) + bias4, NEG_INF)


def _swa_sinks(sink_ref, kvh):
    return jnp.concatenate([jnp.broadcast_to(sink_ref[SWA_G * kvh + g:SWA_G * kvh + g + 1, 0:1], (BLK, 1))
                            for g in range(SWA_G)], axis=0)


def _swa_fwd(proj, bias, sinks_b):
    t = proj.shape[0]
    nq = t // BLK

    def body(q_ref, k_ref, v_ref, bias_ref, sink_ref, o_ref, lse_ref):
        hm = _head_masks()
        band, own = _swa_band_masks()

        def qblock(i, kvh, prev):
            qs = pl.ds(pl.multiple_of(i * BLK, BLK), BLK)
            if prev:
                ks, mask, cols = pl.ds(pl.multiple_of((i - 1) * BLK, BLK), 2 * BLK), band, slice(None)
            else:
                ks, mask, cols = qs, own, slice(BLK, None)
            q4 = _swa_stack(q_ref, qs, kvh, hm[kvh], SCALE)
            sink4 = _swa_sinks(sink_ref, kvh)
            s = _swa_scores(q4, k_ref[ks, :], bias_ref, kvh, mask, cols)
            m = jnp.maximum(jnp.max(s, axis=1, keepdims=True), sink4)
            p = jnp.exp(s - m)
            den = jnp.sum(p, axis=1, keepdims=True) + jnp.exp(sink4 - m)
            o4 = _dot((p * (1.0 / den)).astype(BF16), v_ref[ks, :])
            lse4 = m + jnp.log(den)
            for g in range(SWA_G):
                lse_ref[SWA_G * kvh + g, qs, :] = jnp.broadcast_to(lse4[g * BLK:(g + 1) * BLK], (BLK, LANES))
            for pp, o in enumerate(_swa_unstack(o4, kvh, hm)):
                o_ref[qs, (2 * kvh + pp) * LANES:(2 * kvh + pp + 1) * LANES] = o

        for kvh in range(2):
            qblock(0, kvh, False)

            def step(i, _):
                qblock(i, kvh, True)
                return 0

            lax.fori_loop(1, nq, step, 0, unroll=SWA_UNROLL)

    return _call(
        body, name="swa_fwd", grid=(1,),
        in_specs=[pl.BlockSpec((t, SWA_W), lambda i: (0, 3)), pl.BlockSpec((t, KV_W), lambda i: (0, 16)),
                  pl.BlockSpec((t, KV_W), lambda i: (0, 17)), pl.BlockSpec((8, BLK, 2 * BLK), lambda i: (0, 0, 0)),
                  pl.BlockSpec((8, LANES), lambda i: (0, 0))],
        out_specs=[pl.BlockSpec((t, SWA_W), lambda i: (0, 0)), pl.BlockSpec((8, t, LANES), lambda i: (0, 0, 0))],
        out_shape=[jax.ShapeDtypeStruct((t, SWA_W), F32), jax.ShapeDtypeStruct((8, t, LANES), F32)],
        compiler_params=_params(1))(proj, proj, proj, bias, sinks_b)


def _swa_bwd(proj, d_o, lse, bias, sinks_b, dbias_in):
    t = proj.shape[0]
    nq = t // BLK

    def body(q_ref, k_ref, v_ref, do_ref, lse_ref, bias_ref, sink_ref, dbi_ref,
             dq_ref, dk_ref, dv_ref, dsink_ref, dbias_ref, dk_acc, dv_acc):
        hm = _head_masks()
        band, own = _swa_band_masks()
        dk_acc[...] = jnp.zeros_like(dk_acc)
        dv_acc[...] = jnp.zeros_like(dv_acc)
        dbias_ref[...] = dbi_ref[...]

        def qblock(i, kvh, prev, dsink4):
            qs = pl.ds(pl.multiple_of(i * BLK, BLK), BLK)
            if prev:
                ks, mask, cols = pl.ds(pl.multiple_of((i - 1) * BLK, BLK), 2 * BLK), band, slice(None)
            else:
                ks, mask, cols = qs, own, slice(BLK, None)
            q4 = _swa_stack(q_ref, qs, kvh, hm[kvh], SCALE)
            do4 = _swa_stack(do_ref, qs, kvh, hm[kvh], 1.0)
            sink4 = _swa_sinks(sink_ref, kvh)
            lse4 = jnp.concatenate([lse_ref[SWA_G * kvh + g, qs, 0:1] for g in range(SWA_G)], axis=0)
            kb = k_ref[ks, :]
            p = jnp.exp(_swa_scores(q4, kb, bias_ref, kvh, mask, cols) - lse4)
            dp = _dot_nt(do4, v_ref[ks, :])
            delta = jnp.sum(p * dp, axis=1, keepdims=True)
            ds = p * (dp - delta)
            for g in range(SWA_G):
                dbias_ref[SWA_G * kvh + g, :, cols] += ds[g * BLK:(g + 1) * BLK]
            dsb = ds.astype(BF16)
            dk_acc[ks, :] += _dot_tn(dsb, q4)
            dv_acc[ks, :] += _dot_tn(p.astype(BF16), do4)
            for pp, dq in enumerate(_swa_unstack(_dot(dsb, kb) * SCALE, kvh, hm)):
                dq_ref[qs, (2 * kvh + pp) * LANES:(2 * kvh + pp + 1) * LANES] = dq.astype(BF16)
            return dsink4 - jnp.exp(sink4 - lse4) * delta

        for kvh in range(2):
            ds0 = qblock(0, kvh, False, jnp.zeros((SWA_G * BLK, 1), F32))
            ds4 = lax.fori_loop(1, nq, lambda i, c: qblock(i, kvh, True, c), ds0, unroll=SWA_UNROLL)
            for g in range(SWA_G):
                hq = SWA_G * kvh + g
                dsink_ref[hq:hq + 1, :] = jnp.broadcast_to(
                    jnp.sum(ds4[g * BLK:(g + 1) * BLK], axis=0, keepdims=True), (1, LANES))

        dk_ref[...] = dk_acc[...].astype(BF16)
        dv_ref[...] = dv_acc[...].astype(BF16)

    full3 = pl.BlockSpec((8, BLK, 2 * BLK), lambda i: (0, 0, 0))
    kv = jax.ShapeDtypeStruct((t, KV_W), BF16)
    return _call(
        body, name="swa_bwd", grid=(1,),
        in_specs=[pl.BlockSpec((t, SWA_W), lambda i: (0, 3)), pl.BlockSpec((t, KV_W), lambda i: (0, 16)),
                  pl.BlockSpec((t, KV_W), lambda i: (0, 17)), pl.BlockSpec((t, SWA_W), lambda i: (0, 1)),
                  pl.BlockSpec((8, t, LANES), lambda i: (0, 0, 0)), full3, pl.BlockSpec((8, LANES), lambda i: (0, 0)),
                  full3],
        out_specs=[pl.BlockSpec((t, SWA_W), lambda i: (0, 0)), pl.BlockSpec((t, KV_W), lambda i: (0, 0)),
                   pl.BlockSpec((t, KV_W), lambda i: (0, 0)), pl.BlockSpec((8, LANES), lambda i: (0, 0)), full3],
        out_shape=[jax.ShapeDtypeStruct((t, SWA_W), BF16), kv, kv, jax.ShapeDtypeStruct((8, LANES), F32),
                   jax.ShapeDtypeStruct((8, BLK, 2 * BLK), F32)],
        scratch_shapes=[pltpu.VMEM((t, KV_W), F32), pltpu.VMEM((t, KV_W), F32)],
        compiler_params=_params(1))(proj, proj, proj, d_o, lse, bias, sinks_b, dbias_in)


def _bias_table(rel_bias, buckets):
    def body(rb_ref, b_ref, o_ref):
        bk = b_ref[...]
        for h in range(8):
            acc = jnp.zeros((BLK, 2 * BLK), F32)
            for b in range(N_BUCKETS):
                acc = jnp.where(bk == b, rb_ref[b, h], acc)
            o_ref[h] = acc

    return _call(
        body, name="bias_table", grid=(1,),
        in_specs=[pl.BlockSpec(memory_space=pltpu.SMEM), pl.BlockSpec((BLK, 2 * BLK), lambda i: (0, 0))],
        out_specs=pl.BlockSpec((8, BLK, 2 * BLK), lambda i: (0, 0, 0)),
        out_shape=jax.ShapeDtypeStruct((8, BLK, 2 * BLK), F32), compiler_params=_params(1))(rel_bias, buckets)


def _bias_grad(dbias, buckets):
    def body(d_ref, b_ref, o_ref):
        lane = lax.broadcasted_iota(jnp.int32, (1, LANES), 1)
        bk = b_ref[...]
        for h in range(8):
            d = d_ref[h]
            acc = jnp.zeros((1, LANES), F32)
            for b in range(N_BUCKETS):
                s = jnp.sum(jnp.sum(jnp.where(bk == b, d, 0.0), axis=0, keepdims=True), axis=1, keepdims=True)
                acc = acc + jnp.where(lane == b, s, 0.0)
            o_ref[h:h + 1, :] = acc

    return _call(
        body, name="bias_grad", grid=(1,),
        in_specs=[pl.BlockSpec((8, BLK, 2 * BLK), lambda i: (0, 0, 0)), pl.BlockSpec((BLK, 2 * BLK), lambda i: (0, 0))],
        out_specs=pl.BlockSpec((8, LANES), lambda i: (0, 0)),
        out_shape=jax.ShapeDtypeStruct((8, LANES), F32), compiler_params=_params(1))(dbias, buckets)


def _row(a):
    return a.reshape(1, -1)


def _fwd_ffn1(h, n1, w, small, l):
    s = {"h0": h, "n1": n1}
    s["gu1"], s["act1"] = _ffn_gu(n1, w["ffn1_gu"])
    s["h1"], s["nm"] = _down_res(s["act1"], w["ffn1_down"], h, _row(small["norm_mix"][l]))
    return s


def _fwd_proj_sb(s, w):
    s["proj"] = _proj(s["nm"], w["w_in"])
    s["o_sb"], s["tot"] = _sb_fwd(s["proj"])


def _fwd_swa(s, small, l, bias):
    s["sinks_b"] = jnp.broadcast_to(small["sinks"][l][:, None], (8, LANES))
    s["o_sw"], s["lse"] = _swa_fwd(s["proj"], bias, s["sinks_b"])


def _fwd_out_ffn2(s, w, small, l, g_after):
    s["h2"], s["mixed"], s["n2"] = _out_res(
        s["o_sb"], s["o_sw"], _row(small["norm_out_sb"][l]), _row(small["norm_out_swa"][l]), w["w_out"], s["h1"],
        _row(small["norm_ffn2"][l]))
    s["gu2"], s["act2"] = _ffn_gu(s["n2"], w["ffn2_gu"])
    return _down_res(s["act2"], w["ffn2_down"], s["h2"], g_after)


def _bwd_ffn_dact(dh, s, w, which):
    return _ffn_dact(dh, w[f"ffn{which}_down"], s[f"gu{which}"])


def _bwd_ffn_rest(dh, dgu, s, w, small, l, which):
    h_in, norm = (s["h0"], "norm_ffn1") if which == 1 else (s["h2"], "norm_ffn2")
    g_down = _wgrad_down(s[f"act{which}"], dh)
    g_gu = _wgrad_gu(s[f"n{which}"], dgu)
    dh, dg = _ffn_dn(dgu, w[f"ffn{which}_gu"], dh, h_in, _row(small[norm][l]))
    return dh, {f"ffn{which}_down": g_down, f"ffn{which}_gu": g_gu}, {norm: dg}


def _bwd_ffn(dh, s, w, small, l, which):
    return _bwd_ffn_rest(dh, _bwd_ffn_dact(dh, s, w, which), s, w, small, l, which)


def _bwd_mix(dh, s, w, small, l, bias, dbias):
    g_out = _wgrad_out(s["mixed"], dh)
    d_o, dg_sb, dg_sw = _dmixed(dh, w["w_out"], s["o_sb"], s["o_sw"], _row(small["norm_out_sb"][l]),
                                _row(small["norm_out_swa"][l]))
    dq_sb, dk_sb, dv_sb = _sb_bwd(s["proj"], d_o, s["tot"])
    dq_sw, dk_sw, dv_sw, dsink, dbias = _swa_bwd(s["proj"], d_o, s["lse"], bias, s["sinks_b"], dbias)
    dproj = jnp.concatenate([dq_sb, dk_sb, dv_sb, dq_sw, dk_sw, dv_sw], axis=1)
    g_in = _wgrad_in(s["nm"], dproj)
    dh, dg_mix = _mix_dn(dproj, w["w_in"], dh, s["h1"], _row(small["norm_mix"][l]))
    gs = {"norm_out_sb": dg_sb, "norm_out_swa": dg_sw, "sinks": dsink[:, 0], "norm_mix": dg_mix}
    return dh, {"w_out": g_out, "w_in": g_in}, gs, dbias


def _place():
    x, y, c = lax.axis_index("x"), lax.axis_index("y"), lax.axis_index("c")
    return x, y, c, 2 * x + y


def _chip_core(k, c):
    return (k // 2, k % 2, c)


def _rows_per_block(rows, cols, copies):
    best = 16
    for tr in range(16, rows + 1, 16):
        if rows % tr == 0 and copies * tr * cols * 4 <= 6 * 2 ** 20:
            best = tr
    assert rows % best == 0
    return best


def _place_own(w, l, me1):
    _, rows, cols = w.shape
    tr = _rows_per_block(rows // 2, cols, 1)
    per_half = rows // 2 // tr

    def body(me_ref, w_ref, o_ref):
        o_ref[...] = w_ref[...].astype(BF16)

    return _call(
        body, name="place_own",
        num_scalar_prefetch=1, grid=(rows // tr,),
        in_specs=[pl.BlockSpec((None, tr, cols), lambda r, me: (l, r, 0))],
        out_specs=pl.BlockSpec((None, None, tr, cols), lambda r, me: (me[0], r // per_half, r % per_half, 0)),
        out_shape=jax.ShapeDtypeStruct((N_CHIPS, 2, rows // 2, cols), BF16), compiler_params=_params(1))(me1, w)


def _plan_gather_ici(bufs):
    _, _, c, me = _place()
    return [(b.at[me, c], b.at[me, c], b.at[(me + 3 - j) % N_CHIPS, c], _chip_core((me + 1 + j) % N_CHIPS, c))
            for b in bufs for j in range(3)]


def _plan_gather_d2d(bufs):
    x, y, c, me = _place()
    return [(b.at[(me + 3 - j) % N_CHIPS, c], b.at[(me + 3 - j) % N_CHIPS, c], b.at[(me + 3 - j) % N_CHIPS, 1 - c],
             (x, y, 1 - c)) for b in bufs for j in range(3)]


def _plan_grad_sibling(bufs):
    x, y, c, _ = _place()
    n = len(bufs) // 2
    return [(g.at[:, 1 - c], z, z, (x, y, 1 - c)) for g, z in zip(bufs[:n], bufs[n:])]


def _plan_grad_chips(bufs):
    _, _, c, me = _place()
    n = len(bufs) // 2
    return [(p.at[j], z.at[j], z.at[j], _chip_core((me + 1 + j) % N_CHIPS, c))
            for p, z in zip(bufs[:n], bufs[n:]) for j in range(3)]


def _plan_grad_halves(bufs):
    x, y, c, _ = _place()
    return [(b.at[c], b.at[c], b.at[1 - c], (x, y, 1 - c)) for b in bufs]


def _remote(src, dst, send_sem, recv_sem, to):
    return pltpu.make_async_remote_copy(src_ref=src, dst_ref=dst, send_sem=send_sem, recv_sem=recv_sem,
                                        device_id=to, device_id_type=MESH)


def _exchange_start(name, plan, bufs, n_copies):
    n = len(bufs)

    def body(*refs):
        ins = refs[:n]
        ssem, rsem = refs[n], refs[n + 1]
        token = refs[-1]
        for i, (src, dst, _, to) in enumerate(plan(ins)):
            _remote(src, dst, ssem.at[i], rsem.at[i], to).start()
        token[...] = jnp.zeros_like(token)

    out = _call(
        body, name=name,
        out_shape=(pltpu.SemaphoreType.DMA((n_copies,)), pltpu.SemaphoreType.DMA((n_copies,)),
                   *[pltpu.HBM(a.shape, a.dtype) for a in bufs], jax.ShapeDtypeStruct((8, LANES), F32)),
        in_specs=[HBM] * n, out_specs=(SEM, SEM, *[HBM] * n, pl.BlockSpec(memory_space=pltpu.VMEM)),
        input_output_aliases={t: 2 + t for t in range(n)}, hbm_args=n,
        compiler_params=pltpu.CompilerParams(has_side_effects=EFFECT),
    )(*bufs)
    return (out[0], out[1]), list(out[2:2 + n])


def _exchange_wait(name, plan, bufs, sems):
    n = len(bufs)

    def body(*refs):
        ins = refs[:n]
        ssem, rsem = refs[n], refs[n + 1]
        for i, (src, dst, land, to) in enumerate(plan(ins)):
            _remote(src, dst, ssem.at[i], rsem.at[i], to).wait_send()
            _remote(land, land, ssem.at[i], rsem.at[i], to).wait_recv()

    return list(_call(
        body, name=name, out_shape=[pltpu.HBM(a.shape, a.dtype) for a in bufs],
        in_specs=[HBM] * n + [SEM, SEM], out_specs=[HBM] * n,
        input_output_aliases={t: t for t in range(n)},
        compiler_params=pltpu.CompilerParams(has_side_effects=EFFECT),
    )(*bufs, sems[0], sems[1]))


def _chip_sum(g, xbuf, cm):
    _, _, r2, cols = g.shape
    tr = _rows_per_block(r2, cols, 1)

    def body(cm_ref, g_ref, x_ref, o_ref):
        o_ref[...] = (g_ref[...] + x_ref[...]).astype(BF16)

    return _call(
        body, name="grad_chip_sum",
        num_scalar_prefetch=1, grid=(3, r2 // tr),
        in_specs=[pl.BlockSpec((None, None, tr, cols), lambda j, r, cm: ((cm[1] + 1 + j) % N_CHIPS, cm[0], r, 0)),
                  pl.BlockSpec((None, tr, cols), lambda j, r, cm: ((cm[1] + 1 + j) % N_CHIPS, r, 0))],
        out_specs=pl.BlockSpec((None, tr, cols), lambda j, r, cm: (j, r, 0)),
        out_shape=jax.ShapeDtypeStruct((3, r2, cols), BF16), compiler_params=_params(2))(cm, g, xbuf)


def _total_sum(g, xbuf, rbuf, cm):
    _, _, r2, cols = g.shape
    tr = _rows_per_block(r2, cols, 3)

    def body(cm_ref, g_ref, x_ref, r_ref, o_ref):
        acc = g_ref[...] + x_ref[...]
        for j in range(3):
            acc = acc + r_ref[j].astype(F32)
        o_ref[...] = acc

    return _call(
        body, name="grad_total_sum",
        num_scalar_prefetch=1, grid=(r2 // tr,),
        in_specs=[pl.BlockSpec((None, None, tr, cols), lambda r, cm: (cm[1], cm[0], r, 0)),
                  pl.BlockSpec((None, tr, cols), lambda r, cm: (cm[1], r, 0)),
                  pl.BlockSpec((3, tr, cols), lambda r, cm: (0, r, 0))],
        out_specs=pl.BlockSpec((None, tr, cols), lambda r, cm: (cm[0], r, 0)),
        out_shape=jax.ShapeDtypeStruct((2, r2, cols), F32), compiler_params=_params(1))(cm, g, xbuf, rbuf)


def _small_allreduce(v):
    rows = v.shape[0]
    n_dev = 2 * N_CHIPS

    def body(v_ref, o_ref, buf, ssem, rsem):
        x, y, c, _ = _place()
        me = 4 * x + 2 * y + c
        buf[me] = v_ref[...]

        def copy(d, slot, to):
            return _remote(v_ref, buf.at[slot], ssem.at[d - 1], rsem.at[d - 1], (to // 4, (to // 2) % 2, to % 2))

        cps = [copy(d, me, (me + d) % n_dev) for d in range(1, n_dev)]
        for cp in cps:
            cp.start()
        for d in range(1, n_dev):
            copy(d, (me + n_dev - d) % n_dev, me).wait_recv()
        for cp in cps:
            cp.wait_send()
        acc = buf[0]
        for i in range(1, n_dev):
            acc = acc + buf[i]
        o_ref[...] = acc

    vm = pl.BlockSpec(memory_space=pltpu.VMEM)
    return _call(
        body, name="small_allreduce", in_specs=[vm], out_specs=vm,
        out_shape=jax.ShapeDtypeStruct(v.shape, F32),
        scratch_shapes=[pltpu.VMEM((n_dev, rows, LANES), F32), pltpu.SemaphoreType.DMA((n_dev - 1,)),
                        pltpu.SemaphoreType.DMA((n_dev - 1,))],
        compiler_params=pltpu.CompilerParams(vmem_limit_bytes=V7X_VMEM_LIMIT))(v)


def _adamw_math(w, g, m, v):
    m2 = ADAM_B1 * m + (1.0 - ADAM_B1) * g
    v2 = ADAM_B2 * v + (1.0 - ADAM_B2) * (g * g)
    m_hat = m2 / (1.0 - ADAM_B1 ** ADAM_STEP)
    v_hat = v2 / (1.0 - ADAM_B2 ** ADAM_STEP)
    return -ADAM_LR * (m_hat / (jnp.sqrt(v_hat) + ADAM_EPS) + ADAM_WD * w), m2, v2


def _adamw_layer(w, g, m, v, l, prev):
    _, rows, cols = w.shape
    tr = rows
    for cand in range(8, rows + 1, 8):
        if rows % cand == 0 and cand * cols * 4 <= 2 ** 21:
            tr = cand

    def body(w_ref, g_ref, m_ref, v_ref, *outs):
        go_ref, d_ref, m2_ref, v2_ref = outs[-4:]
        g = g_ref[...]
        go_ref[...] = g
        d_ref[...], m2_ref[...], v2_ref[...] = _adamw_math(w_ref[...], g, m_ref[...], v_ref[...])

    stack = pl.BlockSpec((None, tr, cols), lambda i: (l, i, 0))
    ins, specs, alias = [w, g, m, v], [stack, pl.BlockSpec((tr, cols), lambda i: (i, 0)), stack, stack], {}
    if prev is not None:
        ins += list(prev)
        specs += [ANY] * 4
        alias = {4 + i: i for i in range(4)}
    return _call(
        body, name="adamw", grid=(rows // tr,), in_specs=specs, out_specs=[stack] * 4,
        out_shape=[jax.ShapeDtypeStruct(w.shape, F32)] * 4, input_output_aliases=alias,
        compiler_params=_params(1))(*ins)


def _adamw_small(w, g, m, v):
    def body(w_ref, g_ref, m_ref, v_ref, d_ref, m2_ref, v2_ref):
        d_ref[...], m2_ref[...], v2_ref[...] = _adamw_math(w_ref[...], g_ref[...], m_ref[...], v_ref[...])

    spec = pl.BlockSpec(w.shape, lambda i: (0, 0))
    return _call(
        body, name="adamw_small", grid=(1,), in_specs=[spec] * 4, out_specs=[spec] * 3,
        out_shape=[jax.ShapeDtypeStruct(w.shape, F32)] * 3, compiler_params=_params(1))(w, g, m, v)


SMALL = ("norm_ffn1", "norm_mix", "sinks", "norm_out_sb", "norm_out_swa", "norm_ffn2", "rel_bias", "norm_final")
BIG = ("ffn1_gu", "ffn1_down", "w_in", "w_out", "ffn2_gu", "ffn2_down")


def _pack(parts):
    flat, n = [], 0
    for a in parts:
        a = a.reshape(-1).astype(F32)
        gap = -a.shape[0] % LANES
        flat += [a] + ([jnp.zeros((gap,), F32)] if gap else [])
        n += a.shape[0] + gap
    tail = -(n // LANES) % 8 * LANES
    return jnp.concatenate(flat + ([jnp.zeros((tail,), F32)] if tail else [])).reshape(-1, LANES)


def _unpack(packed, like):
    out, r = [], 0
    for a in like:
        n = math.prod(a.shape)
        nr = -(-n // LANES)
        out.append(packed[r:r + nr].reshape(-1)[:n].reshape(a.shape))
        r += nr
    return out


def _halved(a):
    k, r, cols = a.shape
    return a.reshape(k, 2, r // 2, cols)


def _weight_view(k, buf):
    full = buf.reshape(N_CHIPS, buf.shape[2] * 2, buf.shape[3])
    return full if k.endswith("_gu") else full.reshape(-1, D_MODEL)


def _grad_stack(k, g):
    if not k.endswith("_gu"):
        g = g.reshape(N_CHIPS, g.shape[0] // N_CHIPS, D_MODEL)
    return _halved(g)


def _empty_like_hbm(shape, dtype):
    return pltpu.with_memory_space_constraint(lax.empty(shape, dtype), pltpu.HBM)


def kernel(x, norm_ffn1, w_ffn1_gu, w_ffn1_down, norm_mix, w_in, sinks, norm_out_sb, norm_out_swa, w_out, norm_ffn2, w_ffn2_gu, w_ffn2_down, rel_bias, norm_final, loss_target, m_norm_ffn1, m_w_ffn1_gu, m_w_ffn1_down, m_norm_mix, m_w_in, m_sinks, m_norm_out_sb, m_norm_out_swa, m_w_out, m_norm_ffn2, m_w_ffn2_gu, m_w_ffn2_down, m_rel_bias, m_norm_final, v_norm_ffn1, v_w_ffn1_gu, v_w_ffn1_down, v_norm_mix, v_w_in, v_sinks, v_norm_out_sb, v_norm_out_swa, v_w_out, v_norm_ffn2, v_w_ffn2_gu, v_w_ffn2_down, v_rel_bias, v_norm_final):
    big_w = dict(ffn1_gu=w_ffn1_gu, ffn1_down=w_ffn1_down, w_in=w_in, w_out=w_out, ffn2_gu=w_ffn2_gu, ffn2_down=w_ffn2_down)
    big_m = dict(ffn1_gu=m_w_ffn1_gu, ffn1_down=m_w_ffn1_down, w_in=m_w_in, w_out=m_w_out, ffn2_gu=m_w_ffn2_gu, ffn2_down=m_w_ffn2_down)
    big_v = dict(ffn1_gu=v_w_ffn1_gu, ffn1_down=v_w_ffn1_down, w_in=v_w_in, w_out=v_w_out, ffn2_gu=v_w_ffn2_gu, ffn2_down=v_w_ffn2_down)
    small = dict(norm_ffn1=norm_ffn1, norm_mix=norm_mix, sinks=sinks, norm_out_sb=norm_out_sb, norm_out_swa=norm_out_swa,
                 norm_ffn2=norm_ffn2, rel_bias=rel_bias, norm_final=norm_final)
    small_m = dict(norm_ffn1=m_norm_ffn1, norm_mix=m_norm_mix, sinks=m_sinks, norm_out_sb=m_norm_out_sb,
                   norm_out_swa=m_norm_out_swa, norm_ffn2=m_norm_ffn2, rel_bias=m_rel_bias, norm_final=m_norm_final)
    small_v = dict(norm_ffn1=v_norm_ffn1, norm_mix=v_norm_mix, sinks=v_sinks, norm_out_sb=v_norm_out_sb,
                   norm_out_swa=v_norm_out_swa, norm_ffn2=v_norm_ffn2, rel_bias=v_rel_bias, norm_final=v_norm_final)
    for dct in (big_w, big_m, big_v):
        dct["w_in"] = jnp.swapaxes(dct["w_in"], 1, 2)
    _PREVIOUS[0] = None
    _, _, c, me = _place()
    cm = jnp.stack([c, me]).astype(jnp.int32)
    buckets = jnp.asarray(_bucket_table())
    ffn1, mix_in, rest = ("ffn1_gu", "ffn1_down"), ("w_in",), ("w_out", "ffn2_gu", "ffn2_down")

    def place(l, keys):
        return [_place_own(big_w[k], l, cm[1:]) for k in keys]

    def views(keys, bufs):
        return {k: _weight_view(k, b) for k, b in zip(keys, bufs)}

    def gather_start(tag, bufs):
        return _exchange_start(f"gather{tag}_ici_start", _plan_gather_ici, bufs, 3 * len(bufs))

    def gather_pass(tag, flight):
        bufs = _exchange_wait(f"gather{tag}_ici_wait", _plan_gather_ici, flight[1], flight[0])
        return _exchange_start(f"gather{tag}_d2d_start", _plan_gather_d2d, bufs, 3 * len(bufs))

    def gather_done(tag, keys, flight):
        return views(keys, _exchange_wait(f"gather{tag}_d2d_wait", _plan_gather_d2d, flight[1], flight[0]))

    fly_ffn0 = gather_start("0a", place(0, ffn1))
    fly_in0 = gather_start("0b", place(0, mix_in))
    fly_rest0 = gather_start("0c", place(0, rest))
    bias = _bias_table(rel_bias, buckets)
    fly_ffn1 = gather_start("1a", place(1, ffn1))
    fly_rest1 = gather_start("1b", place(1, mix_in + rest))
    n1 = _norm_cast(x[0], _row(norm_ffn1[0]))
    w0 = gather_done("0a", ffn1, gather_pass("0a", fly_ffn0))

    s0 = _fwd_ffn1(x[0], n1, w0, small, 0)
    w0.update(gather_done("0b", mix_in, gather_pass("0b", fly_in0)))
    _fwd_proj_sb(s0, w0)
    fly_rest0 = gather_pass("0c", fly_rest0)
    _fwd_swa(s0, small, 0, bias)
    w0.update(gather_done("0c", rest, fly_rest0))
    h, n1 = _fwd_out_ffn2(s0, w0, small, 0, _row(norm_ffn1[1]))
    fly_ffn1 = gather_pass("1a", fly_ffn1)
    fly_rest1 = gather_pass("1b", fly_rest1)
    w1 = gather_done("1a", ffn1, fly_ffn1)
    s1 = _fwd_ffn1(h, n1, w1, small, 1)
    w1.update(gather_done("1b", mix_in + rest, fly_rest1))
    _fwd_proj_sb(s1, w1)
    _fwd_swa(s1, small, 1, bias)
    h, _ = _fwd_out_ffn2(s1, w1, small, 1, _row(norm_final))
    dh, dg_final, loss_row = _loss_head(h, _row(norm_final), loss_target[0])

    def landing(stacks, lead, dtype):
        return [_empty_like_hbm((lead,) + a.shape[2:], dtype) for a in stacks]

    def reduce_begin(tag, keys, gw):
        stacks = [_grad_stack(k, gw[k]) for k in keys]
        flight = _exchange_start(f"grad{tag}_sibling_start", _plan_grad_sibling,
                                 stacks + landing(stacks, N_CHIPS, F32), len(keys))
        return dict(tag=tag, keys=keys, stacks=stacks, flight=flight)

    def reduce_chips(st):
        n, (sems, bufs) = len(st["keys"]), st["flight"]
        bufs = _exchange_wait(f"grad{st['tag']}_sibling_wait", _plan_grad_sibling, bufs, sems)
        st["own"] = list(zip(bufs[:n], bufs[n:]))
        st["flight"] = _exchange_start(f"grad{st['tag']}_chips_start", _plan_grad_chips,
                                       [_chip_sum(g, z, cm) for g, z in st["own"]] + landing(st["stacks"], 3, BF16),
                                       3 * n)

    def reduce_halves(st):
        n, (sems, bufs) = len(st["keys"]), st["flight"]
        bufs = _exchange_wait(f"grad{st['tag']}_chips_wait", _plan_grad_chips, bufs, sems)
        halves = [_total_sum(g, x, z, cm) for (g, x), z in zip(st["own"], bufs[n:])]
        st["flight"] = _exchange_start(f"grad{st['tag']}_halves_start", _plan_grad_halves, halves, n)

    def reduce_end(st):
        sems, bufs = st["flight"]
        bufs = _exchange_wait(f"grad{st['tag']}_halves_wait", _plan_grad_halves, bufs, sems)
        return {k: b.reshape(big_w[k].shape[1:]) for k, b in zip(st["keys"], bufs)}

    def adamw(reduced, l, prev):
        return {k: _adamw_layer(big_w[k], g, big_m[k], big_v[k], l, None if prev is None else prev[k])
                for k, g in reduced.items()}

    gsm = [dict() for _ in range(DEPTH)]
    dbias = jnp.zeros((8, BLK, 2 * BLK), F32)
    dh, gw1, gs = _bwd_ffn(dh, s1, w1, small, 1, 2)
    gsm[1].update(gs)
    dh, gw, gs, dbias = _bwd_mix(dh, s1, w1, small, 1, bias, dbias)
    gw1.update(gw)
    gsm[1].update(gs)
    dh, gw, gs = _bwd_ffn(dh, s1, w1, small, 1, 1)
    gw1.update(gw)
    gsm[1].update(gs)

    red1 = reduce_begin("1", BIG, gw1)
    dh, gw0, gs = _bwd_ffn(dh, s0, w0, small, 0, 2)
    gsm[0].update(gs)
    reduce_chips(red1)
    dh, gw, gs, dbias = _bwd_mix(dh, s0, w0, small, 0, bias, dbias)
    gw0.update(gw)
    gsm[0].update(gs)
    red0a = reduce_begin("0a", ("ffn2_gu", "ffn2_down", "w_out", "w_in"), gw0)
    reduce_halves(red1)
    dgu = _bwd_ffn_dact(dh, s0, w0, 1)
    reduce_chips(red0a)
    dh, gw, gs = _bwd_ffn_rest(dh, dgu, s0, w0, small, 0, 1)
    gsm[0].update(gs)
    red0b = reduce_begin("0b", ffn1, gw)
    reduced1 = reduce_end(red1)
    stacks = adamw({k: reduced1[k] for k in ffn1}, 1, None)

    gsmall = {k: jnp.stack([gsm[l][k].reshape(-1) for l in range(DEPTH)]) for k in gsm[0]}
    gsmall["rel_bias"] = jnp.transpose(_bias_grad(dbias, buckets)[:, :N_BUCKETS])
    gsmall["norm_final"] = dg_final.reshape(-1)
    small_like = [small[k] for k in SMALL]
    pk = lambda dct: _pack([dct[k] for k in SMALL])
    red = _small_allreduce(_pack([gsmall[k] for k in SMALL] + [loss_row[0, :1]]))
    gs = _unpack(red, small_like + [loss_row[0, :1]])
    loss = gs[-1][0]
    gs = dict(zip(SMALL, gs[:-1]))

    reduce_chips(red0b)
    stacks.update(adamw({k: reduced1[k] for k in mix_in + rest}, 1, None))
    dlt, m2, v2 = _adamw_small(pk(small), pk(gs), pk(small_m), pk(small_v))
    reduce_halves(red0a)
    stacks.update(adamw(reduce_end(red0a), 0, stacks))
    reduce_halves(red0b)
    stacks.update(adamw(reduce_end(red0b), 0, stacks))

    out_g, out_d, out_m, out_v = {}, {}, {}, {}
    for k in BIG:
        out_g[k], out_d[k], out_m[k], out_v[k] = [jnp.swapaxes(a, 1, 2) if k == "w_in" else a for a in stacks[k]]
    for dst, packed in ((out_d, dlt), (out_m, m2), (out_v, v2)):
        dst.update(zip(SMALL, _unpack(packed, small_like)))
    out_g.update(gs)

    order = ("norm_ffn1", "ffn1_gu", "ffn1_down", "norm_mix", "w_in", "sinks", "norm_out_sb", "norm_out_swa", "w_out",
             "norm_ffn2", "ffn2_gu", "ffn2_down", "rel_bias", "norm_final")
    return (loss, dh.reshape(x.shape), *[out_g[k] for k in order], *[out_d[k] for k in order],
            *[out_m[k] for k in order], *[out_v[k] for k in order])
```

```python
import math

import numpy as np
import jax
import jax.numpy as jnp
from jax import lax
from jax.experimental import pallas as pl
from jax.experimental.pallas import tpu as pltpu

F32 = jnp.float32
BF16 = jnp.bfloat16

D_MODEL = 1024
DEPTH = 2
HEAD_DIM = 64
BLK = 128
N_BUCKETS = 32
MAX_DISTANCE = 128
D_FF = 2816
EPS = 1e-6
NEG_INF = -1e30
SB_W = 512
SWA_W = 512
KV_W = 128
IN_W = 2304
SCALE = HEAD_DIM ** -0.5
N_CHIPS = 4
FS = 2 * D_FF // N_CHIPS
LANES = 128
V7X_VMEM_LIMIT = 56 * 2 ** 20
TM = 512
SB_KT = 512
SWA_G = 4
SWA_UNROLL = 3

ADAM_LR = 0.001
ADAM_B1 = 0.9
ADAM_B2 = 0.999
ADAM_EPS = 1e-08
ADAM_WD = 0.01
ADAM_STEP = 10

MESH = pl.DeviceIdType.MESH
ANY = pl.BlockSpec(memory_space=pl.ANY)
HBM = pl.BlockSpec(memory_space=pltpu.HBM)
SEM = pl.BlockSpec(memory_space=pltpu.SEMAPHORE)
EFFECT = pltpu.SideEffectType.DATAFLOW_SIDE_EFFECTING


def _params(n_grid):
    return pltpu.CompilerParams(dimension_semantics=("arbitrary",) * n_grid, vmem_limit_bytes=V7X_VMEM_LIMIT)


_PREVIOUS = [None]


def _call(body, *, name, in_specs, out_specs, out_shape, grid=(), num_scalar_prefetch=0, scratch_shapes=(),
          input_output_aliases=None, compiler_params=None, hbm_args=0):
    n_in = len(in_specs)

    def run(*args):
        dep = _PREVIOUS[0]
        if any(dep is a for a in args):
            dep = None
        args = [pltpu.with_memory_space_constraint(a, pltpu.HBM) if i < hbm_args else a for i, a in enumerate(args)]
        specs = list(in_specs) + ([ANY] if dep is not None else [])
        k = num_scalar_prefetch + n_in
        fn = body if dep is None else (lambda *refs: body(*refs[:k], *refs[k + 1:]))
        if num_scalar_prefetch:
            shape = dict(grid_spec=pltpu.PrefetchScalarGridSpec(
                num_scalar_prefetch=num_scalar_prefetch, grid=grid, in_specs=specs, out_specs=out_specs,
                scratch_shapes=scratch_shapes))
        else:
            shape = dict(grid=grid, in_specs=specs, out_specs=out_specs, scratch_shapes=scratch_shapes)
        out = pl.pallas_call(fn, name=name, out_shape=out_shape, input_output_aliases=input_output_aliases or {},
                             compiler_params=compiler_params, **shape)(*args, *([] if dep is None else [dep]))
        _PREVIOUS[0] = jax.tree.leaves(out)[-1]
        return out

    return run


def _dot(a, b):
    return jnp.dot(a, b, preferred_element_type=F32)


def _dot_nt(a, b):
    return lax.dot_general(a, b, (((1,), (1,)), ((), ())), preferred_element_type=F32)


def _dot_tn(a, b):
    return lax.dot_general(a, b, (((0,), (0,)), ((), ())), preferred_element_type=F32)


def _rms_fwd(x, g):
    r = lax.rsqrt(jnp.mean(x * x, axis=-1, keepdims=True) + EPS)
    xh = x * r
    return xh * g, xh, r


def _rms_bwd(dy, xh, r, g):
    u = dy * g
    dx = r * (u - xh * jnp.mean(u * xh, axis=-1, keepdims=True))
    dg = jnp.sum(dy * xh, axis=0, keepdims=True)
    return dx, dg


def _softplus(z):
    neg_abs = lax.bitcast_convert_type(lax.bitcast_convert_type(z, jnp.int32) | jnp.int32(-2 ** 31), F32)
    sp = jnp.maximum(z, 0.0) + jnp.log(1.0 + jnp.exp(neg_abs))
    return sp, z - sp


def _norm_cast(h, g):
    t, w = h.shape

    def body(h_ref, g_ref, n_ref):
        y, _, _ = _rms_fwd(h_ref[...], g_ref[...])
        n_ref[...] = y.astype(BF16)

    return _call(
        body, name="norm_cast", grid=(t // TM,),
        in_specs=[pl.BlockSpec((TM, w), lambda i: (i, 0)), pl.BlockSpec((1, w), lambda i: (0, 0))],
        out_specs=pl.BlockSpec((TM, w), lambda i: (i, 0)),
        out_shape=jax.ShapeDtypeStruct((t, w), BF16), compiler_params=_params(1))(h, g)


def _ffn_gu(n, wgu):
    t, d = n.shape

    def body(n_ref, wg_ref, wu_ref, gu_ref, act_ref):
        x = n_ref[...]
        g = _dot(x, wg_ref[...])
        u = _dot(x, wu_ref[...])
        sig = jax.nn.sigmoid(g)
        silu = g * sig
        gu_ref[0] = (u * (sig + silu * (1.0 - sig))).astype(BF16)
        gu_ref[1] = silu.astype(BF16)
        act_ref[...] = (silu * u).astype(BF16)

    return _call(
        body, name="ffn_gu", grid=(2, t // TM),
        in_specs=[pl.BlockSpec((TM, d), lambda j, i: (i, 0)),
                  pl.BlockSpec((None, d, FS), lambda j, i: (j, 0, 0)),
                  pl.BlockSpec((None, d, FS), lambda j, i: (j + 2, 0, 0))],
        out_specs=[pl.BlockSpec((2, TM, FS), lambda j, i: (0, i, j)), pl.BlockSpec((TM, FS), lambda j, i: (i, j))],
        out_shape=[jax.ShapeDtypeStruct((2, t, D_FF), BF16), jax.ShapeDtypeStruct((t, D_FF), BF16)],
        compiler_params=_params(2))(n, wgu, wgu)


def _down_res(act, wdn, h, g_next):
    t, f = act.shape
    d = h.shape[1]

    def body(a_ref, w_ref, h_ref, g_ref, o_ref, n_ref):
        out = h_ref[...] + 0.5 * _dot(a_ref[...], w_ref[...])
        o_ref[...] = out
        n_ref[...] = _rms_fwd(out, g_ref[...])[0].astype(BF16)

    row = pl.BlockSpec((TM, d), lambda i: (i, 0))
    return _call(
        body, name="down_res", grid=(t // TM,),
        in_specs=[pl.BlockSpec((TM, f), lambda i: (i, 0)), pl.BlockSpec((f, d), lambda i: (0, 0)), row,
                  pl.BlockSpec((1, d), lambda i: (0, 0))],
        out_specs=[row, row],
        out_shape=[jax.ShapeDtypeStruct((t, d), F32), jax.ShapeDtypeStruct((t, d), BF16)],
        compiler_params=_params(1))(act, wdn, h, g_next)


def _proj(n, w_in_t):
    t, d = n.shape
    w = w_in_t.shape[0]

    def body(n_ref, w_ref, o_ref):
        o_ref[...] = _dot_nt(n_ref[...], w_ref[...]).astype(BF16)

    return _call(
        body, name="proj", grid=(t // TM,),
        in_specs=[pl.BlockSpec((TM, d), lambda i: (i, 0)), pl.BlockSpec((w, d), lambda i: (0, 0))],
        out_specs=pl.BlockSpec((TM, w), lambda i: (i, 0)),
        out_shape=jax.ShapeDtypeStruct((t, w), BF16), compiler_params=_params(1))(n, w_in_t)


def _out_res(o_sb, o_sw, g_sb, g_sw, w_out, h, g_next):
    t, d = h.shape

    def body(a_ref, b_ref, ga_ref, gb_ref, w_ref, h_ref, g_ref, o_ref, mix_ref, n_ref):
        ya, _, _ = _rms_fwd(a_ref[...], ga_ref[...])
        yb, _, _ = _rms_fwd(b_ref[...], gb_ref[...])
        mixed = jnp.concatenate([ya.astype(BF16), yb.astype(BF16)], axis=1)
        mix_ref[...] = mixed
        out = h_ref[...] + _dot(mixed, w_ref[...])
        o_ref[...] = out
        n_ref[...] = _rms_fwd(out, g_ref[...])[0].astype(BF16)

    row = pl.BlockSpec((TM, d), lambda i: (i, 0))
    return _call(
        body, name="out_res", grid=(t // TM,),
        in_specs=[pl.BlockSpec((TM, SB_W), lambda i: (i, 0)), pl.BlockSpec((TM, SWA_W), lambda i: (i, 0)),
                  pl.BlockSpec((1, SB_W), lambda i: (0, 0)), pl.BlockSpec((1, SWA_W), lambda i: (0, 0)),
                  pl.BlockSpec((d, d), lambda i: (0, 0)), row, pl.BlockSpec((1, d), lambda i: (0, 0))],
        out_specs=[row, row, row],
        out_shape=[jax.ShapeDtypeStruct((t, d), F32), jax.ShapeDtypeStruct((t, d), BF16),
                   jax.ShapeDtypeStruct((t, d), BF16)],
        compiler_params=_params(1))(o_sb, o_sw, g_sb, g_sw, w_out, h, g_next)


def _loss_head(h, g, tgt):
    t, d = h.shape

    def body(h_ref, g_ref, t_ref, dh_ref, dg_ref, loss_ref):
        @pl.when(pl.program_id(0) == 0)
        def _():
            dg_ref[...] = jnp.zeros_like(dg_ref)
            loss_ref[...] = jnp.zeros_like(loss_ref)

        gg = g_ref[...]
        y, xh, r = _rms_fwd(h_ref[...], gg)
        err = y - t_ref[...]
        part = 0.5 * jnp.sum(jnp.sum(err * err, axis=1, keepdims=True) / d, axis=0, keepdims=True)
        loss_ref[...] += jnp.broadcast_to(part, loss_ref.shape)
        dx, dg = _rms_bwd(err / d, xh, r, gg)
        dh_ref[...] = dx
        dg_ref[...] += dg

    return _call(
        body, name="loss_head", grid=(t // TM,),
        in_specs=[pl.BlockSpec((TM, d), lambda i: (i, 0)), pl.BlockSpec((1, d), lambda i: (0, 0)),
                  pl.BlockSpec((TM, d), lambda i: (i, 0))],
        out_specs=[pl.BlockSpec((TM, d), lambda i: (i, 0)), pl.BlockSpec((1, d), lambda i: (0, 0)),
                   pl.BlockSpec((1, LANES), lambda i: (0, 0))],
        out_shape=[jax.ShapeDtypeStruct((t, d), F32), jax.ShapeDtypeStruct((1, d), F32),
                   jax.ShapeDtypeStruct((1, LANES), F32)],
        compiler_params=_params(1))(h, g, tgt)


def _ffn_dact(dh, wdn, gu):
    t, d = dh.shape
    tm = TM

    def body(dh_ref, w_ref, gu_ref, o_ref):
        da = 0.5 * _dot_nt(dh_ref[...].astype(BF16), w_ref[...])
        o_ref[0] = (da * gu_ref[0].astype(F32)).astype(BF16)
        o_ref[1] = (da * gu_ref[1].astype(F32)).astype(BF16)

    return _call(
        body, name="ffn_dact", grid=(2, t // tm),
        in_specs=[pl.BlockSpec((tm, d), lambda j, i: (i, 0)), pl.BlockSpec((FS, d), lambda j, i: (j, 0)),
                  pl.BlockSpec((2, tm, FS), lambda j, i: (0, i, j))],
        out_specs=pl.BlockSpec((2, tm, FS), lambda j, i: (0, i, j)),
        out_shape=jax.ShapeDtypeStruct((2, t, D_FF), BF16), compiler_params=_params(2))(dh, wdn, gu)


def _dn_norm_bwd(a, a_spec, w, w_spec, nk, dh, h_in, g, w_transposed=False):
    t, d = dh.shape
    mm = _dot if w_transposed else _dot_nt

    def body(a_ref, w_ref, dh_ref, h_ref, g_ref, o_ref, dg_ref, acc_ref):
        i, k = pl.program_id(0), pl.program_id(1)

        if nk > 1:
            @pl.when(k == 0)
            def _():
                acc_ref[...] = mm(a_ref[...], w_ref[...])

            @pl.when((k > 0) & (k < nk - 1))
            def _():
                acc_ref[...] += mm(a_ref[...], w_ref[...])

        @pl.when(k == nk - 1)
        def _():
            gg = g_ref[...]
            dg = jnp.zeros_like(gg)
            for rows in (slice(0, TM // 2), slice(TM // 2, TM)):
                dn = mm(a_ref[rows, :], w_ref[...])
                if nk > 1:
                    dn = dn + acc_ref[rows, :]
                _, xh, r = _rms_fwd(h_ref[rows, :], gg)
                dx, dg_rows = _rms_bwd(dn, xh, r, gg)
                o_ref[rows, :] = dh_ref[rows, :] + dx
                dg = dg + dg_rows

            @pl.when(i == 0)
            def _():
                dg_ref[...] = dg

            @pl.when(i > 0)
            def _():
                dg_ref[...] += dg

    row = pl.BlockSpec((TM, d), lambda i, k: (i, 0))
    return _call(
        body, name="dn_norm_bwd", grid=(t // TM, nk),
        in_specs=[a_spec, w_spec, row, row, pl.BlockSpec((1, d), lambda i, k: (0, 0))],
        out_specs=[row, pl.BlockSpec((1, d), lambda i, k: (0, 0))],
        out_shape=[jax.ShapeDtypeStruct((t, d), F32), jax.ShapeDtypeStruct((1, d), F32)],
        scratch_shapes=[pltpu.VMEM((TM, d), F32)], compiler_params=_params(2))(a, w, dh, h_in, g)


def _ffn_dn(dgu, wgu, dh, h_in, g):
    d = dh.shape[1]
    return _dn_norm_bwd(
        dgu, pl.BlockSpec((None, TM, FS), lambda i, k: (k // 2, i, k % 2)),
        wgu, pl.BlockSpec((None, d, FS), lambda i, k: (k, 0, 0)), N_CHIPS, dh, h_in, g)


def _mix_dn(dproj, w_in_t, dh, h_in, g):
    d = dh.shape[1]
    w = dproj.shape[1]
    return _dn_norm_bwd(
        dproj, pl.BlockSpec((TM, w), lambda i, k: (i, 0)),
        w_in_t, pl.BlockSpec((w, d), lambda i, k: (0, 0)), 1, dh, h_in, g, w_transposed=True)


def _dmixed(dh, w_out, o_sb, o_sw, g_sb, g_sw):
    t, d = dh.shape

    def body(dh_ref, w_ref, a_ref, b_ref, ga_ref, gb_ref, o_ref, dga_ref, dgb_ref):
        i = pl.program_id(0)
        dm = _dot_nt(dh_ref[...].astype(BF16), w_ref[...])
        _, xa, ra = _rms_fwd(a_ref[...], ga_ref[...])
        _, xb, rb = _rms_fwd(b_ref[...], gb_ref[...])
        da, dga = _rms_bwd(dm[:, :SB_W], xa, ra, ga_ref[...])
        db, dgb = _rms_bwd(dm[:, SB_W:], xb, rb, gb_ref[...])
        o_ref[...] = jnp.concatenate([da.astype(BF16), db.astype(BF16)], axis=1)

        @pl.when(i == 0)
        def _():
            dga_ref[...] = dga
            dgb_ref[...] = dgb

        @pl.when(i > 0)
        def _():
            dga_ref[...] += dga
            dgb_ref[...] += dgb

    return _call(
        body, name="dmixed", grid=(t // TM,),
        in_specs=[pl.BlockSpec((TM, d), lambda i: (i, 0)), pl.BlockSpec((d, d), lambda i: (0, 0)),
                  pl.BlockSpec((TM, SB_W), lambda i: (i, 0)), pl.BlockSpec((TM, SWA_W), lambda i: (i, 0)),
                  pl.BlockSpec((1, SB_W), lambda i: (0, 0)), pl.BlockSpec((1, SWA_W), lambda i: (0, 0))],
        out_specs=[pl.BlockSpec((TM, d), lambda i: (i, 0)), pl.BlockSpec((1, SB_W), lambda i: (0, 0)),
                   pl.BlockSpec((1, SWA_W), lambda i: (0, 0))],
        out_shape=[jax.ShapeDtypeStruct((t, d), BF16), jax.ShapeDtypeStruct((1, SB_W), F32),
                   jax.ShapeDtypeStruct((1, SWA_W), F32)],
        compiler_params=_params(1))(dh, w_out, o_sb, o_sw, g_sb, g_sw)


def _wgrad(name, a, a_spec, b, b_spec, grid, out_shape, out_spec, scale):
    def body(a_ref, b_ref, o_ref):
        r = _dot_tn(a_ref[...], b_ref[...].astype(BF16))
        o_ref[...] = r if scale == 1.0 else scale * r

    return _call(
        body, name=name, grid=grid, in_specs=[a_spec, b_spec], out_specs=out_spec,
        out_shape=jax.ShapeDtypeStruct(out_shape, F32), compiler_params=_params(len(grid)))(a, b)


def _wgrad_gu(n, dgu):
    t, d = n.shape
    return _wgrad(
        "wgrad_gu", n, pl.BlockSpec((t, TM), lambda s, r: (0, r)),
        dgu, pl.BlockSpec((None, t, FS), lambda s, r: (s // 2, 0, s % 2)), (N_CHIPS, d // TM),
        (N_CHIPS, d, FS), pl.BlockSpec((None, TM, FS), lambda s, r: (s, r, 0)), 1.0)


def _wgrad_down(act, dh):
    t, d = dh.shape
    return _wgrad(
        "wgrad_down", act, pl.BlockSpec((t, FS), lambda s, r: (0, s)), dh, pl.BlockSpec((t, TM), lambda s, r: (0, r)),
        (2, d // TM), (D_FF, d), pl.BlockSpec((FS, TM), lambda s, r: (s, r)), 0.5)


def _wgrad_out(mixed, dh):
    t, d = dh.shape
    return _wgrad(
        "wgrad_out", mixed, pl.BlockSpec((t, TM), lambda s: (0, s)), dh, pl.BlockSpec((t, d), lambda s: (0, 0)),
        (d // TM,), (d, d), pl.BlockSpec((TM, d), lambda s: (s, 0)), 1.0)


def _wgrad_in(n, dproj):
    t, d = n.shape
    w = dproj.shape[1]
    tw = w // 3
    return _wgrad(
        "wgrad_in", dproj, pl.BlockSpec((t, tw), lambda s: (0, s)), n, pl.BlockSpec((t, d), lambda s: (0, 0)),
        (3,), (w, d), pl.BlockSpec((tw, d), lambda s: (s, 0)), 1.0)


def _tri(rel):
    row = lax.broadcasted_iota(jnp.int32, (BLK, BLK), 0)
    col = lax.broadcasted_iota(jnp.int32, (BLK, BLK), 1)
    m = rel(row, col).astype(BF16)
    return jnp.concatenate([m, m], axis=0)


def _scan_dot(x, tri2):
    hi = x.astype(BF16)
    lo = (x - hi.astype(F32)).astype(BF16)
    return _dot(jnp.concatenate([hi, lo], axis=1), tri2)


def _head_masks():
    lane = lax.broadcasted_iota(jnp.int32, (1, LANES), 1)
    return [lane < HEAD_DIM, lane >= HEAD_DIM]


SB_PAIRS = 2
SB_ROWS = 2 * SB_PAIRS * BLK


def _sb_dcol():
    dcol = lax.broadcasted_iota(jnp.int32, (BLK, SB_KT), 1) - lax.broadcasted_iota(jnp.int32, (BLK, SB_KT), 0)
    return jnp.concatenate([dcol] * (2 * SB_PAIRS), axis=0)


def _sb_stack(x, hm):
    return jnp.concatenate([jnp.where(m, x[:, p * LANES:(p + 1) * LANES], jnp.zeros((BLK, LANES), x.dtype))
                            for p in range(SB_PAIRS) for m in hm], axis=0)


def _sb_unstack(y, hm):
    return jnp.concatenate([jnp.where(hm[0], y[2 * p * BLK:(2 * p + 1) * BLK], y[(2 * p + 1) * BLK:(2 * p + 2) * BLK])
                            for p in range(SB_PAIRS)], axis=1)


def _sb_pairs():
    return [(slice(2 * p * BLK, (2 * p + 2) * BLK), slice(p * LANES, (p + 1) * LANES)) for p in range(SB_PAIRS)]


def _sb_fwd(proj):
    t = proj.shape[0]
    nq = t // BLK
    nb = SB_KT // BLK
    wide = SB_PAIRS * LANES

    def body(q_ref, k_ref, v_ref, o_ref, tot_ref):
        hm = _head_masks()
        dcol = _sb_dcol()
        pairs = _sb_pairs()
        after = _tri(lambda r, c: r > c)

        def tile(qh, kt, carry, acc, limit):
            ks = pl.ds(pl.multiple_of(kt * SB_KT, SB_KT), SB_KT)
            z = jnp.concatenate([_dot_nt(qh[rows], k_ref[ks, lanes]) for rows, lanes in pairs], axis=0)
            sp, zs = _softplus(z)
            valid = None if limit is None else dcol < limit
            spm = sp if valid is None else jnp.where(valid, sp, 0.0)
            sufs = [None] * nb
            for b in reversed(range(nb)):
                blk = spm[:, b * BLK:(b + 1) * BLK]
                sufs[b] = carry + _scan_dot(blk, after)
                carry = carry + jnp.sum(blk, axis=1, keepdims=True)
            w = jnp.exp(zs - jnp.concatenate(sufs, axis=1))
            if valid is not None:
                w = jnp.where(valid, w, 0.0)
            wb = w.astype(BF16)
            return carry, acc + jnp.concatenate([_dot(wb[rows], v_ref[ks, lanes]) for rows, lanes in pairs], axis=0)

        def qblock(qi, _):
            qs = pl.ds(pl.multiple_of(qi * BLK, BLK), BLK)
            kd = qi // nb
            limit = (qi - kd * nb) * BLK
            qh = _sb_stack(q_ref[qs, :] * SCALE, hm)
            c0 = tile(qh, kd, jnp.zeros((SB_ROWS, 1), F32), jnp.zeros((SB_ROWS, LANES), F32), limit)
            carry, acc = lax.fori_loop(0, kd, lambda n, c: tile(qh, kd - 1 - n, c[0], c[1], None), c0)
            o_ref[qs, :] = _sb_unstack(acc, hm)
            for h in range(2 * SB_PAIRS):
                tot_ref[h, qs, :] = jnp.broadcast_to(carry[h * BLK:(h + 1) * BLK], (BLK, LANES))
            return 0

        lax.fori_loop(0, nq, qblock, 0)

    col_blk = lambda off: pl.BlockSpec((t, wide), lambda g: (0, off + g))
    n_steps = SB_W // wide
    return _call(
        body, name="sb_fwd", grid=(n_steps,), in_specs=[col_blk(0), col_blk(n_steps), col_blk(2 * n_steps)],
        out_specs=[col_blk(0), pl.BlockSpec((2 * SB_PAIRS, t, LANES), lambda g: (g, 0, 0))],
        out_shape=[jax.ShapeDtypeStruct((t, SB_W), F32), jax.ShapeDtypeStruct((8, t, LANES), F32)],
        compiler_params=_params(1))(proj, proj, proj)


def _sb_bwd(proj, d_o, tot):
    t = proj.shape[0]
    nq = t // BLK
    nb = SB_KT // BLK
    wide = SB_PAIRS * LANES

    def body(q_ref, k_ref, v_ref, do_ref, tot_ref, dq_ref, dk_ref, dv_ref, dk_acc, dv_acc):
        hm = _head_masks()
        dcol = _sb_dcol()
        pairs = _sb_pairs()
        before = _tri(lambda r, c: r < c)
        upto = _tri(lambda r, c: r <= c)
        dk_acc[...] = jnp.zeros_like(dk_acc)
        dv_acc[...] = jnp.zeros_like(dv_acc)

        def tile(qh, doh, tt, kt, pre, ecum, dq, limit):
            ks = pl.ds(pl.multiple_of(kt * SB_KT, SB_KT), SB_KT)
            k = k_ref[ks, :]
            v = v_ref[ks, :]
            z = jnp.concatenate([_dot_nt(qh[rows], k[:, lanes]) for rows, lanes in pairs], axis=0)
            sp, zs = _softplus(z)
            valid = None if limit is None else dcol < limit
            spm = sp if valid is None else jnp.where(valid, sp, 0.0)
            pres = []
            for b in range(nb):
                blk = spm[:, b * BLK:(b + 1) * BLK]
                pres.append(pre + _scan_dot(blk, before))
                pre = pre + jnp.sum(blk, axis=1, keepdims=True)
            logw = z - (tt - jnp.concatenate(pres, axis=1))
            if valid is not None:
                logw = jnp.minimum(logw, 0.0)
            w = jnp.exp(logw)
            if valid is not None:
                w = jnp.where(valid, w, 0.0)
            e = w * jnp.concatenate([_dot_nt(doh[rows], v[:, lanes]) for rows, lanes in pairs], axis=0)
            incs = []
            for b in range(nb):
                blk = e[:, b * BLK:(b + 1) * BLK]
                incs.append(ecum + _scan_dot(blk, upto))
                ecum = ecum + jnp.sum(blk, axis=1, keepdims=True)
            dz = e - jnp.exp(zs) * jnp.concatenate(incs, axis=1)
            if valid is not None:
                dz = jnp.where(valid, dz, 0.0)
            dzb = dz.astype(BF16)
            wb = w.astype(BF16)
            for rows, lanes in pairs:
                dk_acc[ks, lanes] += _dot_tn(dzb[rows], qh[rows])
                dv_acc[ks, lanes] += _dot_tn(wb[rows], doh[rows])
            return pre, ecum, dq + jnp.concatenate([_dot(dzb[rows], k[:, lanes]) for rows, lanes in pairs], axis=0)

        def qblock(qi, _):
            qs = pl.ds(pl.multiple_of(qi * BLK, BLK), BLK)
            kd = qi // nb
            limit = (qi - kd * nb) * BLK
            qh = _sb_stack(q_ref[qs, :] * SCALE, hm)
            doh = _sb_stack(do_ref[qs, :], hm)
            tt = jnp.concatenate([tot_ref[h, qs, 0:1] for h in range(2 * SB_PAIRS)], axis=0)
            c0 = (jnp.zeros((SB_ROWS, 1), F32), jnp.zeros((SB_ROWS, 1), F32), jnp.zeros((SB_ROWS, LANES), F32))
            c = lax.fori_loop(0, kd, lambda kt, c: tile(qh, doh, tt, kt, c[0], c[1], c[2], None), c0)
            dq = tile(qh, doh, tt, kd, c[0], c[1], c[2], limit)[2]
            dq_ref[qs, :] = (_sb_unstack(dq, hm) * SCALE).astype(BF16)
            return 0

        lax.fori_loop(0, nq, qblock, 0)
        dk_ref[...] = dk_acc[...].astype(BF16)
        dv_ref[...] = dv_acc[...].astype(BF16)

    col_blk = lambda off: pl.BlockSpec((t, wide), lambda g: (0, off + g))
    n_steps = SB_W // wide
    out = jax.ShapeDtypeStruct((t, SB_W), BF16)
    return _call(
        body, name="sb_bwd", grid=(n_steps,),
        in_specs=[col_blk(0), col_blk(n_steps), col_blk(2 * n_steps), col_blk(0),
                  pl.BlockSpec((2 * SB_PAIRS, t, LANES), lambda g: (g, 0, 0))],
        out_specs=[col_blk(0), col_blk(0), col_blk(0)], out_shape=[out, out, out],
        scratch_shapes=[pltpu.VMEM((t, wide), F32), pltpu.VMEM((t, wide), F32)],
        compiler_params=_params(1))(proj, proj, proj, d_o, tot)


def _bucket_table():
    a = np.arange(BLK)[:, None]
    c = np.arange(2 * BLK)[None, :]
    dist = np.maximum(BLK + a - c, 0)
    max_exact = N_BUCKETS // 2
    dd = np.maximum(dist, 1).astype(np.float32)
    large = max_exact + (np.log(dd / max_exact) / math.log(MAX_DISTANCE / max_exact)
                         * (N_BUCKETS - max_exact)).astype(np.int32)
    large = np.minimum(large, N_BUCKETS - 1)
    return np.where(dist < max_exact, dist, large).astype(np.int32)


def _swa_band_masks():
    row = lax.broadcasted_iota(jnp.int32, (SWA_G * BLK, 2 * BLK), 0) & (BLK - 1)
    col = lax.broadcasted_iota(jnp.int32, (SWA_G * BLK, 2 * BLK), 1)
    own = lax.broadcasted_iota(jnp.int32, (SWA_G * BLK, BLK), 1) <= (
        lax.broadcasted_iota(jnp.int32, (SWA_G * BLK, BLK), 0) & (BLK - 1))
    return (col > row) & ((col < BLK) | (col - BLK <= row)), own


def _swa_stack(ref, qs, kvh, kvmask, scale):
    parts = []
    for g in range(SWA_G):
        hq = SWA_G * kvh + g
        x = ref[qs, (hq // 2) * LANES:(hq // 2 + 1) * LANES].astype(F32)
        if hq % 2 != kvh:
            x = pltpu.roll(x, HEAD_DIM, 1)
        parts.append(jnp.where(kvmask, x * scale, 0.0).astype(BF16))
    return jnp.concatenate(parts, axis=0)


def _swa_unstack(x4, kvh, hm):
    heads = []
    for g in range(SWA_G):
        x = x4[g * BLK:(g + 1) * BLK]
        heads.append(pltpu.roll(x, HEAD_DIM, 1) if g % 2 != kvh else x)
    return [jnp.where(hm[0], heads[0], heads[1]), jnp.where(hm[0], heads[2], heads[3])]


def _swa_scores(q4, kb, bias_ref, kvh, mask, cols):
    bias4 = jnp.concatenate([bias_ref[SWA_G * kvh + g, :, cols] for g in range(SWA_G)], axis=0)
    return jnp.where(mask, _dot_nt(q4, kb) + bias4, NEG_INF)


def _swa_sinks(sink_ref, kvh):
    return jnp.concatenate([jnp.broadcast_to(sink_ref[SWA_G * kvh + g:SWA_G * kvh + g + 1, 0:1], (BLK, 1))
                            for g in range(SWA_G)], axis=0)


def _swa_fwd(proj, bias, sinks_b):
    t = proj.shape[0]
    nq = t // BLK

    def body(q_ref, k_ref, v_ref, bias_ref, sink_ref, o_ref, lse_ref):
        hm = _head_masks()
        band, own = _swa_band_masks()

        def qblock(i, kvh, prev):
            qs = pl.ds(pl.multiple_of(i * BLK, BLK), BLK)
            if prev:
                ks, mask, cols = pl.ds(pl.multiple_of((i - 1) * BLK, BLK), 2 * BLK), band, slice(None)
            else:
                ks, mask, cols = qs, own, slice(BLK, None)
            q4 = _swa_stack(q_ref, qs, kvh, hm[kvh], SCALE)
            sink4 = _swa_sinks(sink_ref, kvh)
            s = _swa_scores(q4, k_ref[ks, :], bias_ref, kvh, mask, cols)
            m = jnp.maximum(jnp.max(s, axis=1, keepdims=True), sink4)
            p = jnp.exp(s - m)
            den = jnp.sum(p, axis=1, keepdims=True) + jnp.exp(sink4 - m)
            o4 = _dot((p * (1.0 / den)).astype(BF16), v_ref[ks, :])
            lse4 = m + jnp.log(den)
            for g in range(SWA_G):
                lse_ref[SWA_G * kvh + g, qs, :] = jnp.broadcast_to(lse4[g * BLK:(g + 1) * BLK], (BLK, LANES))
            for pp, o in enumerate(_swa_unstack(o4, kvh, hm)):
                o_ref[qs, (2 * kvh + pp) * LANES:(2 * kvh + pp + 1) * LANES] = o

        for kvh in range(2):
            qblock(0, kvh, False)

            def step(i, _):
                qblock(i, kvh, True)
                return 0

            lax.fori_loop(1, nq, step, 0, unroll=SWA_UNROLL)

    return _call(
        body, name="swa_fwd", grid=(1,),
        in_specs=[pl.BlockSpec((t, SWA_W), lambda i: (0, 3)), pl.BlockSpec((t, KV_W), lambda i: (0, 16)),
                  pl.BlockSpec((t, KV_W), lambda i: (0, 17)), pl.BlockSpec((8, BLK, 2 * BLK), lambda i: (0, 0, 0)),
                  pl.BlockSpec((8, LANES), lambda i: (0, 0))],
        out_specs=[pl.BlockSpec((t, SWA_W), lambda i: (0, 0)), pl.BlockSpec((8, t, LANES), lambda i: (0, 0, 0))],
        out_shape=[jax.ShapeDtypeStruct((t, SWA_W), F32), jax.ShapeDtypeStruct((8, t, LANES), F32)],
        compiler_params=_params(1))(proj, proj, proj, bias, sinks_b)


def _swa_bwd(proj, d_o, lse, bias, sinks_b, dbias_in):
    t = proj.shape[0]
    nq = t // BLK

    def body(q_ref, k_ref, v_ref, do_ref, lse_ref, bias_ref, sink_ref, dbi_ref,
             dq_ref, dk_ref, dv_ref, dsink_ref, dbias_ref, dk_acc, dv_acc):
        hm = _head_masks()
        band, own = _swa_band_masks()
        dk_acc[...] = jnp.zeros_like(dk_acc)
        dv_acc[...] = jnp.zeros_like(dv_acc)
        dbias_ref[...] = dbi_ref[...]

        def qblock(i, kvh, prev, dsink4):
            qs = pl.ds(pl.multiple_of(i * BLK, BLK), BLK)
            if prev:
                ks, mask, cols = pl.ds(pl.multiple_of((i - 1) * BLK, BLK), 2 * BLK), band, slice(None)
            else:
                ks, mask, cols = qs, own, slice(BLK, None)
            q4 = _swa_stack(q_ref, qs, kvh, hm[kvh], SCALE)
            do4 = _swa_stack(do_ref, qs, kvh, hm[kvh], 1.0)
            sink4 = _swa_sinks(sink_ref, kvh)
            lse4 = jnp.concatenate([lse_ref[SWA_G * kvh + g, qs, 0:1] for g in range(SWA_G)], axis=0)
            kb = k_ref[ks, :]
            p = jnp.exp(_swa_scores(q4, kb, bias_ref, kvh, mask, cols) - lse4)
            dp = _dot_nt(do4, v_ref[ks, :])
            delta = jnp.sum(p * dp, axis=1, keepdims=True)
            ds = p * (dp - delta)
            for g in range(SWA_G):
                dbias_ref[SWA_G * kvh + g, :, cols] += ds[g * BLK:(g + 1) * BLK]
            dsb = ds.astype(BF16)
            dk_acc[ks, :] += _dot_tn(dsb, q4)
            dv_acc[ks, :] += _dot_tn(p.astype(BF16), do4)
            for pp, dq in enumerate(_swa_unstack(_dot(dsb, kb) * SCALE, kvh, hm)):
                dq_ref[qs, (2 * kvh + pp) * LANES:(2 * kvh + pp + 1) * LANES] = dq.astype(BF16)
            return dsink4 - jnp.exp(sink4 - lse4) * delta

        for kvh in range(2):
            ds0 = qblock(0, kvh, False, jnp.zeros((SWA_G * BLK, 1), F32))
            ds4 = lax.fori_loop(1, nq, lambda i, c: qblock(i, kvh, True, c), ds0, unroll=SWA_UNROLL)
            for g in range(SWA_G):
                hq = SWA_G * kvh + g
                dsink_ref[hq:hq + 1, :] = jnp.broadcast_to(
                    jnp.sum(ds4[g * BLK:(g + 1) * BLK], axis=0, keepdims=True), (1, LANES))

        dk_ref[...] = dk_acc[...].astype(BF16)
        dv_ref[...] = dv_acc[...].astype(BF16)

    full3 = pl.BlockSpec((8, BLK, 2 * BLK), lambda i: (0, 0, 0))
    kv = jax.ShapeDtypeStruct((t, KV_W), BF16)
    return _call(
        body, name="swa_bwd", grid=(1,),
        in_specs=[pl.BlockSpec((t, SWA_W), lambda i: (0, 3)), pl.BlockSpec((t, KV_W), lambda i: (0, 16)),
                  pl.BlockSpec((t, KV_W), lambda i: (0, 17)), pl.BlockSpec((t, SWA_W), lambda i: (0, 1)),
                  pl.BlockSpec((8, t, LANES), lambda i: (0, 0, 0)), full3, pl.BlockSpec((8, LANES), lambda i: (0, 0)),
                  full3],
        out_specs=[pl.BlockSpec((t, SWA_W), lambda i: (0, 0)), pl.BlockSpec((t, KV_W), lambda i: (0, 0)),
                   pl.BlockSpec((t, KV_W), lambda i: (0, 0)), pl.BlockSpec((8, LANES), lambda i: (0, 0)), full3],
        out_shape=[jax.ShapeDtypeStruct((t, SWA_W), BF16), kv, kv, jax.ShapeDtypeStruct((8, LANES), F32),
                   jax.ShapeDtypeStruct((8, BLK, 2 * BLK), F32)],
        scratch_shapes=[pltpu.VMEM((t, KV_W), F32), pltpu.VMEM((t, KV_W), F32)],
        compiler_params=_params(1))(proj, proj, proj, d_o, lse, bias, sinks_b, dbias_in)


def _bias_table(rel_bias, buckets):
    def body(rb_ref, b_ref, o_ref):
        bk = b_ref[...]
        for h in range(8):
            acc = jnp.zeros((BLK, 2 * BLK), F32)
            for b in range(N_BUCKETS):
                acc = jnp.where(bk == b, rb_ref[b, h], acc)
            o_ref[h] = acc

    return _call(
        body, name="bias_table", grid=(1,),
        in_specs=[pl.BlockSpec(memory_space=pltpu.SMEM), pl.BlockSpec((BLK, 2 * BLK), lambda i: (0, 0))],
        out_specs=pl.BlockSpec((8, BLK, 2 * BLK), lambda i: (0, 0, 0)),
        out_shape=jax.ShapeDtypeStruct((8, BLK, 2 * BLK), F32), compiler_params=_params(1))(rel_bias, buckets)


def _bias_grad(dbias, buckets):
    def body(d_ref, b_ref, o_ref):
        lane = lax.broadcasted_iota(jnp.int32, (1, LANES), 1)
        bk = b_ref[...]
        for h in range(8):
            d = d_ref[h]
            acc = jnp.zeros((1, LANES), F32)
            for b in range(N_BUCKETS):
                s = jnp.sum(jnp.sum(jnp.where(bk == b, d, 0.0), axis=0, keepdims=True), axis=1, keepdims=True)
                acc = acc + jnp.where(lane == b, s, 0.0)
            o_ref[h:h + 1, :] = acc

    return _call(
        body, name="bias_grad", grid=(1,),
        in_specs=[pl.BlockSpec((8, BLK, 2 * BLK), lambda i: (0, 0, 0)), pl.BlockSpec((BLK, 2 * BLK), lambda i: (0, 0))],
        out_specs=pl.BlockSpec((8, LANES), lambda i: (0, 0)),
        out_shape=jax.ShapeDtypeStruct((8, LANES), F32), compiler_params=_params(1))(dbias, buckets)


def _row(a):
    return a.reshape(1, -1)


def _fwd_ffn1(h, n1, w, small, l):
    s = {"h0": h, "n1": n1}
    s["gu1"], s["act1"] = _ffn_gu(n1, w["ffn1_gu"])
    s["h1"], s["nm"] = _down_res(s["act1"], w["ffn1_down"], h, _row(small["norm_mix"][l]))
    return s


def _fwd_proj_sb(s, w):
    s["proj"] = _proj(s["nm"], w["w_in"])
    s["o_sb"], s["tot"] = _sb_fwd(s["proj"])


def _fwd_swa(s, small, l, bias):
    s["sinks_b"] = jnp.broadcast_to(small["sinks"][l][:, None], (8, LANES))
    s["o_sw"], s["lse"] = _swa_fwd(s["proj"], bias, s["sinks_b"])


def _fwd_out_ffn2(s, w, small, l, g_after):
    s["h2"], s["mixed"], s["n2"] = _out_res(
        s["o_sb"], s["o_sw"], _row(small["norm_out_sb"][l]), _row(small["norm_out_swa"][l]), w["w_out"], s["h1"],
        _row(small["norm_ffn2"][l]))
    s["gu2"], s["act2"] = _ffn_gu(s["n2"], w["ffn2_gu"])
    return _down_res(s["act2"], w["ffn2_down"], s["h2"], g_after)


def _bwd_ffn_dact(dh, s, w, which):
    return _ffn_dact(dh, w[f"ffn{which}_down"], s[f"gu{which}"])


def _bwd_ffn_rest(dh, dgu, s, w, small, l, which):
    h_in, norm = (s["h0"], "norm_ffn1") if which == 1 else (s["h2"], "norm_ffn2")
    g_down = _wgrad_down(s[f"act{which}"], dh)
    g_gu = _wgrad_gu(s[f"n{which}"], dgu)
    dh, dg = _ffn_dn(dgu, w[f"ffn{which}_gu"], dh, h_in, _row(small[norm][l]))
    return dh, {f"ffn{which}_down": g_down, f"ffn{which}_gu": g_gu}, {norm: dg}


def _bwd_ffn(dh, s, w, small, l, which):
    return _bwd_ffn_rest(dh, _bwd_ffn_dact(dh, s, w, which), s, w, small, l, which)


def _bwd_mix(dh, s, w, small, l, bias, dbias):
    g_out = _wgrad_out(s["mixed"], dh)
    d_o, dg_sb, dg_sw = _dmixed(dh, w["w_out"], s["o_sb"], s["o_sw"], _row(small["norm_out_sb"][l]),
                                _row(small["norm_out_swa"][l]))
    dq_sb, dk_sb, dv_sb = _sb_bwd(s["proj"], d_o, s["tot"])
    dq_sw, dk_sw, dv_sw, dsink, dbias = _swa_bwd(s["proj"], d_o, s["lse"], bias, s["sinks_b"], dbias)
    dproj = jnp.concatenate([dq_sb, dk_sb, dv_sb, dq_sw, dk_sw, dv_sw], axis=1)
    g_in = _wgrad_in(s["nm"], dproj)
    dh, dg_mix = _mix_dn(dproj, w["w_in"], dh, s["h1"], _row(small["norm_mix"][l]))
    gs = {"norm_out_sb": dg_sb, "norm_out_swa": dg_sw, "sinks": dsink[:, 0], "norm_mix": dg_mix}
    return dh, {"w_out": g_out, "w_in": g_in}, gs, dbias


def _place():
    x, y, c = lax.axis_index("x"), lax.axis_index("y"), lax.axis_index("c")
    return x, y, c, 2 * x + y


def _chip_core(k, c):
    return (k // 2, k % 2, c)


def _rows_per_block(rows, cols, copies):
    best = 16
    for tr in range(16, rows + 1, 16):
        if rows % tr == 0 and copies * tr * cols * 4 <= 6 * 2 ** 20:
            best = tr
    assert rows % best == 0
    return best


def _place_own(w, l, me1):
    _, rows, cols = w.shape
    tr = _rows_per_block(rows // 2, cols, 1)
    per_half = rows // 2 // tr

    def body(me_ref, w_ref, o_ref):
        o_ref[...] = w_ref[...].astype(BF16)

    return _call(
        body, name="place_own",
        num_scalar_prefetch=1, grid=(rows // tr,),
        in_specs=[pl.BlockSpec((None, tr, cols), lambda r, me: (l, r, 0))],
        out_specs=pl.BlockSpec((None, None, tr, cols), lambda r, me: (me[0], r // per_half, r % per_half, 0)),
        out_shape=jax.ShapeDtypeStruct((N_CHIPS, 2, rows // 2, cols), BF16), compiler_params=_params(1))(me1, w)


def _plan_gather_ici(bufs):
    _, _, c, me = _place()
    return [(b.at[me, c], b.at[me, c], b.at[(me + 3 - j) % N_CHIPS, c], _chip_core((me + 1 + j) % N_CHIPS, c))
            for b in bufs for j in range(3)]


def _plan_gather_d2d(bufs):
    x, y, c, me = _place()
    return [(b.at[(me + 3 - j) % N_CHIPS, c], b.at[(me + 3 - j) % N_CHIPS, c], b.at[(me + 3 - j) % N_CHIPS, 1 - c],
             (x, y, 1 - c)) for b in bufs for j in range(3)]


def _plan_grad_sibling(bufs):
    x, y, c, _ = _place()
    n = len(bufs) // 2
    return [(g.at[:, 1 - c], z, z, (x, y, 1 - c)) for g, z in zip(bufs[:n], bufs[n:])]


def _plan_grad_chips(bufs):
    _, _, c, me = _place()
    n = len(bufs) // 2
    return [(p.at[j], z.at[j], z.at[j], _chip_core((me + 1 + j) % N_CHIPS, c))
            for p, z in zip(bufs[:n], bufs[n:]) for j in range(3)]


def _plan_grad_halves(bufs):
    x, y, c, _ = _place()
    return [(b.at[c], b.at[c], b.at[1 - c], (x, y, 1 - c)) for b in bufs]


def _remote(src, dst, send_sem, recv_sem, to):
    return pltpu.make_async_remote_copy(src_ref=src, dst_ref=dst, send_sem=send_sem, recv_sem=recv_sem,
                                        device_id=to, device_id_type=MESH)


def _exchange_start(name, plan, bufs, n_copies):
    n = len(bufs)

    def body(*refs):
        ins = refs[:n]
        ssem, rsem = refs[n], refs[n + 1]
        token = refs[-1]
        for i, (src, dst, _, to) in enumerate(plan(ins)):
            _remote(src, dst, ssem.at[i], rsem.at[i], to).start()
        token[...] = jnp.zeros_like(token)

    out = _call(
        body, name=name,
        out_shape=(pltpu.SemaphoreType.DMA((n_copies,)), pltpu.SemaphoreType.DMA((n_copies,)),
                   *[pltpu.HBM(a.shape, a.dtype) for a in bufs], jax.ShapeDtypeStruct((8, LANES), F32)),
        in_specs=[HBM] * n, out_specs=(SEM, SEM, *[HBM] * n, pl.BlockSpec(memory_space=pltpu.VMEM)),
        input_output_aliases={t: 2 + t for t in range(n)}, hbm_args=n,
        compiler_params=pltpu.CompilerParams(has_side_effects=EFFECT),
    )(*bufs)
    return (out[0], out[1]), list(out[2:2 + n])


def _exchange_wait(name, plan, bufs, sems):
    n = len(bufs)

    def body(*refs):
        ins = refs[:n]
        ssem, rsem = refs[n], refs[n + 1]
        for i, (src, dst, land, to) in enumerate(plan(ins)):
            _remote(src, dst, ssem.at[i], rsem.at[i], to).wait_send()
            _remote(land, land, ssem.at[i], rsem.at[i], to).wait_recv()

    return list(_call(
        body, name=name, out_shape=[pltpu.HBM(a.shape, a.dtype) for a in bufs],
        in_specs=[HBM] * n + [SEM, SEM], out_specs=[HBM] * n,
        input_output_aliases={t: t for t in range(n)},
        compiler_params=pltpu.CompilerParams(has_side_effects=EFFECT),
    )(*bufs, sems[0], sems[1]))


def _chip_sum(g, xbuf, cm):
    _, _, r2, cols = g.shape
    tr = _rows_per_block(r2, cols, 1)

    def body(cm_ref, g_ref, x_ref, o_ref):
        o_ref[...] = (g_ref[...] + x_ref[...]).astype(BF16)

    return _call(
        body, name="grad_chip_sum",
        num_scalar_prefetch=1, grid=(3, r2 // tr),
        in_specs=[pl.BlockSpec((None, None, tr, cols), lambda j, r, cm: ((cm[1] + 1 + j) % N_CHIPS, cm[0], r, 0)),
                  pl.BlockSpec((None, tr, cols), lambda j, r, cm: ((cm[1] + 1 + j) % N_CHIPS, r, 0))],
        out_specs=pl.BlockSpec((None, tr, cols), lambda j, r, cm: (j, r, 0)),
        out_shape=jax.ShapeDtypeStruct((3, r2, cols), BF16), compiler_params=_params(2))(cm, g, xbuf)


def _total_sum(g, xbuf, rbuf, cm):
    _, _, r2, cols = g.shape
    tr = _rows_per_block(r2, cols, 3)

    def body(cm_ref, g_ref, x_ref, r_ref, o_ref):
        acc = g_ref[...] + x_ref[...]
        for j in range(3):
            acc = acc + r_ref[j].astype(F32)
        o_ref[...] = acc

    return _call(
        body, name="grad_total_sum",
        num_scalar_prefetch=1, grid=(r2 // tr,),
        in_specs=[pl.BlockSpec((None, None, tr, cols), lambda r, cm: (cm[1], cm[0], r, 0)),
                  pl.BlockSpec((None, tr, cols), lambda r, cm: (cm[1], r, 0)),
                  pl.BlockSpec((3, tr, cols), lambda r, cm: (0, r, 0))],
        out_specs=pl.BlockSpec((None, tr, cols), lambda r, cm: (cm[0], r, 0)),
        out_shape=jax.ShapeDtypeStruct((2, r2, cols), F32), compiler_params=_params(1))(cm, g, xbuf, rbuf)


def _small_allreduce(v):
    rows = v.shape[0]
    n_dev = 2 * N_CHIPS

    def body(v_ref, o_ref, buf, ssem, rsem):
        x, y, c, _ = _place()
        me = 4 * x + 2 * y + c
        buf[me] = v_ref[...]

        def copy(d, slot, to):
            return _remote(v_ref, buf.at[slot], ssem.at[d - 1], rsem.at[d - 1], (to // 4, (to // 2) % 2, to % 2))

        cps = [copy(d, me, (me + d) % n_dev) for d in range(1, n_dev)]
        for cp in cps:
            cp.start()
        for d in range(1, n_dev):
            copy(d, (me + n_dev - d) % n_dev, me).wait_recv()
        for cp in cps:
            cp.wait_send()
        acc = buf[0]
        for i in range(1, n_dev):
            acc = acc + buf[i]
        o_ref[...] = acc

    vm = pl.BlockSpec(memory_space=pltpu.VMEM)
    return _call(
        body, name="small_allreduce", in_specs=[vm], out_specs=vm,
        out_shape=jax.ShapeDtypeStruct(v.shape, F32),
        scratch_shapes=[pltpu.VMEM((n_dev, rows, LANES), F32), pltpu.SemaphoreType.DMA((n_dev - 1,)),
                        pltpu.SemaphoreType.DMA((n_dev - 1,))],
        compiler_params=pltpu.CompilerParams(vmem_limit_bytes=V7X_VMEM_LIMIT))(v)


def _adamw_math(w, g, m, v):
    m2 = ADAM_B1 * m + (1.0 - ADAM_B1) * g
    v2 = ADAM_B2 * v + (1.0 - ADAM_B2) * (g * g)
    m_hat = m2 / (1.0 - ADAM_B1 ** ADAM_STEP)
    v_hat = v2 / (1.0 - ADAM_B2 ** ADAM_STEP)
    return -ADAM_LR * (m_hat / (jnp.sqrt(v_hat) + ADAM_EPS) + ADAM_WD * w), m2, v2


def _adamw_layer(w, g, m, v, l, prev):
    _, rows, cols = w.shape
    tr = rows
    for cand in range(8, rows + 1, 8):
        if rows % cand == 0 and cand * cols * 4 <= 2 ** 21:
            tr = cand

    def body(w_ref, g_ref, m_ref, v_ref, *outs):
        go_ref, d_ref, m2_ref, v2_ref = outs[-4:]
        g = g_ref[...]
        go_ref[...] = g
        d_ref[...], m2_ref[...], v2_ref[...] = _adamw_math(w_ref[...], g, m_ref[...], v_ref[...])

    stack = pl.BlockSpec((None, tr, cols), lambda i: (l, i, 0))
    ins, specs, alias = [w, g, m, v], [stack, pl.BlockSpec((tr, cols), lambda i: (i, 0)), stack, stack], {}
    if prev is not None:
        ins += list(prev)
        specs += [ANY] * 4
        alias = {4 + i: i for i in range(4)}
    return _call(
        body, name="adamw", grid=(rows // tr,), in_specs=specs, out_specs=[stack] * 4,
        out_shape=[jax.ShapeDtypeStruct(w.shape, F32)] * 4, input_output_aliases=alias,
        compiler_params=_params(1))(*ins)


def _adamw_small(w, g, m, v):
    def body(w_ref, g_ref, m_ref, v_ref, d_ref, m2_ref, v2_ref):
        d_ref[...], m2_ref[...], v2_ref[...] = _adamw_math(w_ref[...], g_ref[...], m_ref[...], v_ref[...])

    spec = pl.BlockSpec(w.shape, lambda i: (0, 0))
    return _call(
        body, name="adamw_small", grid=(1,), in_specs=[spec] * 4, out_specs=[spec] * 3,
        out_shape=[jax.ShapeDtypeStruct(w.shape, F32)] * 3, compiler_params=_params(1))(w, g, m, v)


SMALL = ("norm_ffn1", "norm_mix", "sinks", "norm_out_sb", "norm_out_swa", "norm_ffn2", "rel_bias", "norm_final")
BIG = ("ffn1_gu", "ffn1_down", "w_in", "w_out", "ffn2_gu", "ffn2_down")


def _pack(parts):
    flat, n = [], 0
    for a in parts:
        a = a.reshape(-1).astype(F32)
        gap = -a.shape[0] % LANES
        flat += [a] + ([jnp.zeros((gap,), F32)] if gap else [])
        n += a.shape[0] + gap
    tail = -(n // LANES) % 8 * LANES
    return jnp.concatenate(flat + ([jnp.zeros((tail,), F32)] if tail else [])).reshape(-1, LANES)


def _unpack(packed, like):
    out, r = [], 0
    for a in like:
        n = math.prod(a.shape)
        nr = -(-n // LANES)
        out.append(packed[r:r + nr].reshape(-1)[:n].reshape(a.shape))
        r += nr
    return out


def _halved(a):
    k, r, cols = a.shape
    return a.reshape(k, 2, r // 2, cols)


def _weight_view(k, buf):
    full = buf.reshape(N_CHIPS, buf.shape[2] * 2, buf.shape[3])
    return full if k.endswith("_gu") else full.reshape(-1, D_MODEL)


def _grad_stack(k, g):
    if not k.endswith("_gu"):
        g = g.reshape(N_CHIPS, g.shape[0] // N_CHIPS, D_MODEL)
    return _halved(g)


def _empty_like_hbm(shape, dtype):
    return pltpu.with_memory_space_constraint(lax.empty(shape, dtype), pltpu.HBM)


def kernel(x, norm_ffn1, w_ffn1_gu, w_ffn1_down, norm_mix, w_in, sinks, norm_out_sb, norm_out_swa, w_out, norm_ffn2, w_ffn2_gu, w_ffn2_down, rel_bias, norm_final, loss_target, m_norm_ffn1, m_w_ffn1_gu, m_w_ffn1_down, m_norm_mix, m_w_in, m_sinks, m_norm_out_sb, m_norm_out_swa, m_w_out, m_norm_ffn2, m_w_ffn2_gu, m_w_ffn2_down, m_rel_bias, m_norm_final, v_norm_ffn1, v_w_ffn1_gu, v_w_ffn1_down, v_norm_mix, v_w_in, v_sinks, v_norm_out_sb, v_norm_out_swa, v_w_out, v_norm_ffn2, v_w_ffn2_gu, v_w_ffn2_down, v_rel_bias, v_norm_final):
    big_w = dict(ffn1_gu=w_ffn1_gu, ffn1_down=w_ffn1_down, w_in=w_in, w_out=w_out, ffn2_gu=w_ffn2_gu, ffn2_down=w_ffn2_down)
    big_m = dict(ffn1_gu=m_w_ffn1_gu, ffn1_down=m_w_ffn1_down, w_in=m_w_in, w_out=m_w_out, ffn2_gu=m_w_ffn2_gu, ffn2_down=m_w_ffn2_down)
    big_v = dict(ffn1_gu=v_w_ffn1_gu, ffn1_down=v_w_ffn1_down, w_in=v_w_in, w_out=v_w_out, ffn2_gu=v_w_ffn2_gu, ffn2_down=v_w_ffn2_down)
    small = dict(norm_ffn1=norm_ffn1, norm_mix=norm_mix, sinks=sinks, norm_out_sb=norm_out_sb, norm_out_swa=norm_out_swa,
                 norm_ffn2=norm_ffn2, rel_bias=rel_bias, norm_final=norm_final)
    small_m = dict(norm_ffn1=m_norm_ffn1, norm_mix=m_norm_mix, sinks=m_sinks, norm_out_sb=m_norm_out_sb,
                   norm_out_swa=m_norm_out_swa, norm_ffn2=m_norm_ffn2, rel_bias=m_rel_bias, norm_final=m_norm_final)
    small_v = dict(norm_ffn1=v_norm_ffn1, norm_mix=v_norm_mix, sinks=v_sinks, norm_out_sb=v_norm_out_sb,
                   norm_out_swa=v_norm_out_swa, norm_ffn2=v_norm_ffn2, rel_bias=v_rel_bias, norm_final=v_norm_final)
    for dct in (big_w, big_m, big_v):
        dct["w_in"] = jnp.swapaxes(dct["w_in"], 1, 2)
    _PREVIOUS[0] = None
    _, _, c, me = _place()
    cm = jnp.stack([c, me]).astype(jnp.int32)
    buckets = jnp.asarray(_bucket_table())
    ffn1, mix_in, rest = ("ffn1_gu", "ffn1_down"), ("w_in",), ("w_out", "ffn2_gu", "ffn2_down")

    def place(l, keys):
        return [_place_own(big_w[k], l, cm[1:]) for k in keys]

    def views(keys, bufs):
        return {k: _weight_view(k, b) for k, b in zip(keys, bufs)}

    def gather_start(tag, bufs):
        return _exchange_start(f"gather{tag}_ici_start", _plan_gather_ici, bufs, 3 * len(bufs))

    def gather_pass(tag, flight):
        bufs = _exchange_wait(f"gather{tag}_ici_wait", _plan_gather_ici, flight[1], flight[0])
        return _exchange_start(f"gather{tag}_d2d_start", _plan_gather_d2d, bufs, 3 * len(bufs))

    def gather_done(tag, keys, flight):
        return views(keys, _exchange_wait(f"gather{tag}_d2d_wait", _plan_gather_d2d, flight[1], flight[0]))

    fly_ffn0 = gather_start("0a", place(0, ffn1))
    fly_in0 = gather_start("0b", place(0, mix_in))
    fly_rest0 = gather_start("0c", place(0, rest))
    bias = _bias_table(rel_bias, buckets)
    fly_ffn1 = gather_start("1a", place(1, ffn1))
    fly_rest1 = gather_start("1b", place(1, mix_in + rest))
    n1 = _norm_cast(x[0], _row(norm_ffn1[0]))
    w0 = gather_done("0a", ffn1, gather_pass("0a", fly_ffn0))

    s0 = _fwd_ffn1(x[0], n1, w0, small, 0)
    w0.update(gather_done("0b", mix_in, gather_pass("0b", fly_in0)))
    _fwd_proj_sb(s0, w0)
    fly_rest0 = gather_pass("0c", fly_rest0)
    _fwd_swa(s0, small, 0, bias)
    w0.update(gather_done("0c", rest, fly_rest0))
    h, n1 = _fwd_out_ffn2(s0, w0, small, 0, _row(norm_ffn1[1]))
    fly_ffn1 = gather_pass("1a", fly_ffn1)
    fly_rest1 = gather_pass("1b", fly_rest1)
    w1 = gather_done("1a", ffn1, fly_ffn1)
    s1 = _fwd_ffn1(h, n1, w1, small, 1)
    w1.update(gather_done("1b", mix_in + rest, fly_rest1))
    _fwd_proj_sb(s1, w1)
    _fwd_swa(s1, small, 1, bias)
    h, _ = _fwd_out_ffn2(s1, w1, small, 1, _row(norm_final))
    dh, dg_final, loss_row = _loss_head(h, _row(norm_final), loss_target[0])

    def landing(stacks, lead, dtype):
        return [_empty_like_hbm((lead,) + a.shape[2:], dtype) for a in stacks]

    def reduce_begin(tag, keys, gw):
        stacks = [_grad_stack(k, gw[k]) for k in keys]
        flight = _exchange_start(f"grad{tag}_sibling_start", _plan_grad_sibling,
                                 stacks + landing(stacks, N_CHIPS, F32), len(keys))
        return dict(tag=tag, keys=keys, stacks=stacks, flight=flight)

    def reduce_chips(st):
        n, (sems, bufs) = len(st["keys"]), st["flight"]
        bufs = _exchange_wait(f"grad{st['tag']}_sibling_wait", _plan_grad_sibling, bufs, sems)
        st["own"] = list(zip(bufs[:n], bufs[n:]))
        st["flight"] = _exchange_start(f"grad{st['tag']}_chips_start", _plan_grad_chips,
                                       [_chip_sum(g, z, cm) for g, z in st["own"]] + landing(st["stacks"], 3, BF16),
                                       3 * n)

    def reduce_halves(st):
        n, (sems, bufs) = len(st["keys"]), st["flight"]
        bufs = _exchange_wait(f"grad{st['tag']}_chips_wait", _plan_grad_chips, bufs, sems)
        halves = [_total_sum(g, x, z, cm) for (g, x), z in zip(st["own"], bufs[n:])]
        st["flight"] = _exchange_start(f"grad{st['tag']}_halves_start", _plan_grad_halves, halves, n)

    def reduce_end(st):
        sems, bufs = st["flight"]
        bufs = _exchange_wait(f"grad{st['tag']}_halves_wait", _plan_grad_halves, bufs, sems)
        return {k: b.reshape(big_w[k].shape[1:]) for k, b in zip(st["keys"], bufs)}

    def adamw(reduced, l, prev):
        return {k: _adamw_layer(big_w[k], g, big_m[k], big_v[k], l, None if prev is None else prev[k])
                for k, g in reduced.items()}

    gsm = [dict() for _ in range(DEPTH)]
    dbias = jnp.zeros((8, BLK, 2 * BLK), F32)
    dh, gw1, gs = _bwd_ffn(dh, s1, w1, small, 1, 2)
    gsm[1].update(gs)
    dh, gw, gs, dbias = _bwd_mix(dh, s1, w1, small, 1, bias, dbias)
    gw1.update(gw)
    gsm[1].update(gs)
    dh, gw, gs = _bwd_ffn(dh, s1, w1, small, 1, 1)
    gw1.update(gw)
    gsm[1].update(gs)

    red1 = reduce_begin("1", BIG, gw1)
    dh, gw0, gs = _bwd_ffn(dh, s0, w0, small, 0, 2)
    gsm[0].update(gs)
    reduce_chips(red1)
    dh, gw, gs, dbias = _bwd_mix(dh, s0, w0, small, 0, bias, dbias)
    gw0.update(gw)
    gsm[0].update(gs)
    red0a = reduce_begin("0a", ("ffn2_gu", "ffn2_down", "w_out", "w_in"), gw0)
    reduce_halves(red1)
    dgu = _bwd_ffn_dact(dh, s0, w0, 1)
    reduce_chips(red0a)
    dh, gw, gs = _bwd_ffn_rest(dh, dgu, s0, w0, small, 0, 1)
    gsm[0].update(gs)
    red0b = reduce_begin("0b", ffn1, gw)
    reduced1 = reduce_end(red1)
    stacks = adamw({k: reduced1[k] for k in ffn1}, 1, None)

    gsmall = {k: jnp.stack([gsm[l][k].reshape(-1) for l in range(DEPTH)]) for k in gsm[0]}
    gsmall["rel_bias"] = jnp.transpose(_bias_grad(dbias, buckets)[:, :N_BUCKETS])
    gsmall["norm_final"] = dg_final.reshape(-1)
    small_like = [small[k] for k in SMALL]
    pk = lambda dct: _pack([dct[k] for k in SMALL])
    red = _small_allreduce(_pack([gsmall[k] for k in SMALL] + [loss_row[0, :1]]))
    gs = _unpack(red, small_like + [loss_row[0, :1]])
    loss = gs[-1][0]
    gs = dict(zip(SMALL, gs[:-1]))

    reduce_chips(red0b)
    stacks.update(adamw({k: reduced1[k] for k in mix_in + rest}, 1, None))
    dlt, m2, v2 = _adamw_small(pk(small), pk(gs), pk(small_m), pk(small_v))
    reduce_halves(red0a)
    stacks.update(adamw(reduce_end(red0a), 0, stacks))
    reduce_halves(red0b)
    stacks.update(adamw(reduce_end(red0b), 0, stacks))

    out_g, out_d, out_m, out_v = {}, {}, {}, {}
    for k in BIG:
        out_g[k], out_d[k], out_m[k], out_v[k] = [jnp.swapaxes(a, 1, 2) if k == "w_in" else a for a in stacks[k]]
    for dst, packed in ((out_d, dlt), (out_m, m2), (out_v, v2)):
        dst.update(zip(SMALL, _unpack(packed, small_like)))
    out_g.update(gs)

    order = ("norm_ffn1", "ffn1_gu", "ffn1_down", "norm_mix", "w_in", "sinks", "norm_out_sb", "norm_out_swa", "w_out",
             "norm_ffn2", "ffn2_gu", "ffn2_down", "rel_bias", "norm_final")
    return (loss, dh.reshape(x.shape), *[out_g[k] for k in order], *[out_d[k] for k in order],
            *[out_m[k] for k in order], *[out_v[k] for k in order])
```

```python
import math

import numpy as np
import jax
import jax.numpy as jnp
from jax import lax
from jax.experimental import pallas as pl
from jax.experimental.pallas import tpu as pltpu

F32 = jnp.float32
BF16 = jnp.bfloat16

D_MODEL = 1024
DEPTH = 2
HEAD_DIM = 64
BLK = 128
N_BUCKETS = 32
MAX_DISTANCE = 128
D_FF = 2816
EPS = 1e-6
NEG_INF = -1e30
SB_W = 512
SWA_W = 512
KV_W = 128
IN_W = 2304
SCALE = HEAD_DIM ** -0.5
N_CHIPS = 4
FS = 2 * D_FF // N_CHIPS
LANES = 128
V7X_VMEM_LIMIT = 56 * 2 ** 20
TM = 512
SB_KT = 512
SWA_G = 4

ADAM_LR = 0.001
ADAM_B1 = 0.9
ADAM_B2 = 0.999
ADAM_EPS = 1e-08
ADAM_WD = 0.01
ADAM_STEP = 10

MESH = pl.DeviceIdType.MESH
ANY = pl.BlockSpec(memory_space=pl.ANY)
HBM = pl.BlockSpec(memory_space=pltpu.HBM)
SEM = pl.BlockSpec(memory_space=pltpu.SEMAPHORE)
EFFECT = pltpu.SideEffectType.DATAFLOW_SIDE_EFFECTING


def _params(n_grid):
    return pltpu.CompilerParams(dimension_semantics=("arbitrary",) * n_grid, vmem_limit_bytes=V7X_VMEM_LIMIT)


_PREVIOUS = [None]


def _call(body, *, name, in_specs, out_specs, out_shape, grid=(), num_scalar_prefetch=0, scratch_shapes=(),
          input_output_aliases=None, compiler_params=None, hbm_args=0):
    n_in = len(in_specs)

    def run(*args):
        dep = _PREVIOUS[0]
        if any(dep is a for a in args):
            dep = None
        args = [pltpu.with_memory_space_constraint(a, pltpu.HBM) if i < hbm_args else a for i, a in enumerate(args)]
        specs = list(in_specs) + ([ANY] if dep is not None else [])
        k = num_scalar_prefetch + n_in
        fn = body if dep is None else (lambda *refs: body(*refs[:k], *refs[k + 1:]))
        if num_scalar_prefetch:
            shape = dict(grid_spec=pltpu.PrefetchScalarGridSpec(
                num_scalar_prefetch=num_scalar_prefetch, grid=grid, in_specs=specs, out_specs=out_specs,
                scratch_shapes=scratch_shapes))
        else:
            shape = dict(grid=grid, in_specs=specs, out_specs=out_specs, scratch_shapes=scratch_shapes)
        out = pl.pallas_call(fn, name=name, out_shape=out_shape, input_output_aliases=input_output_aliases or {},
                             compiler_params=compiler_params, **shape)(*args, *([] if dep is None else [dep]))
        _PREVIOUS[0] = jax.tree.leaves(out)[-1]
        return out

    return run


def _dot(a, b):
    return jnp.dot(a, b, preferred_element_type=F32)


def _dot_nt(a, b):
    return lax.dot_general(a, b, (((1,), (1,)), ((), ())), preferred_element_type=F32)


def _dot_tn(a, b):
    return lax.dot_general(a, b, (((0,), (0,)), ((), ())), preferred_element_type=F32)


def _rms_fwd(x, g):
    r = lax.rsqrt(jnp.mean(x * x, axis=-1, keepdims=True) + EPS)
    xh = x * r
    return xh * g, xh, r


def _rms_bwd(dy, xh, r, g):
    u = dy * g
    dx = r * (u - xh * jnp.mean(u * xh, axis=-1, keepdims=True))
    dg = jnp.sum(dy * xh, axis=0, keepdims=True)
    return dx, dg


def _softplus(z):
    neg_abs = lax.bitcast_convert_type(lax.bitcast_convert_type(z, jnp.int32) | jnp.int32(-2 ** 31), F32)
    sp = jnp.maximum(z, 0.0) + jnp.log(1.0 + jnp.exp(neg_abs))
    return sp, z - sp


def _norm_cast(h, g):
    t, w = h.shape

    def body(h_ref, g_ref, n_ref):
        y, _, _ = _rms_fwd(h_ref[...], g_ref[...])
        n_ref[...] = y.astype(BF16)

    return _call(
        body, name="norm_cast", grid=(t // TM,),
        in_specs=[pl.BlockSpec((TM, w), lambda i: (i, 0)), pl.BlockSpec((1, w), lambda i: (0, 0))],
        out_specs=pl.BlockSpec((TM, w), lambda i: (i, 0)),
        out_shape=jax.ShapeDtypeStruct((t, w), BF16), compiler_params=_params(1))(h, g)


def _ffn_gu(n, wgu):
    t, d = n.shape

    def body(n_ref, wg_ref, wu_ref, gu_ref, act_ref):
        x = n_ref[...]
        g = _dot(x, wg_ref[...])
        u = _dot(x, wu_ref[...])
        sig = jax.nn.sigmoid(g)
        silu = g * sig
        gu_ref[0] = (u * (sig + silu * (1.0 - sig))).astype(BF16)
        gu_ref[1] = silu.astype(BF16)
        act_ref[...] = (silu * u).astype(BF16)

    return _call(
        body, name="ffn_gu", grid=(2, t // TM),
        in_specs=[pl.BlockSpec((TM, d), lambda j, i: (i, 0)),
                  pl.BlockSpec((None, d, FS), lambda j, i: (j, 0, 0)),
                  pl.BlockSpec((None, d, FS), lambda j, i: (j + 2, 0, 0))],
        out_specs=[pl.BlockSpec((2, TM, FS), lambda j, i: (0, i, j)), pl.BlockSpec((TM, FS), lambda j, i: (i, j))],
        out_shape=[jax.ShapeDtypeStruct((2, t, D_FF), BF16), jax.ShapeDtypeStruct((t, D_FF), BF16)],
        compiler_params=_params(2))(n, wgu, wgu)


def _down_res(act, wdn, h, g_next):
    t, f = act.shape
    d = h.shape[1]

    def body(a_ref, w_ref, h_ref, g_ref, o_ref, n_ref):
        out = h_ref[...] + 0.5 * _dot(a_ref[...], w_ref[...])
        o_ref[...] = out
        n_ref[...] = _rms_fwd(out, g_ref[...])[0].astype(BF16)

    row = pl.BlockSpec((TM, d), lambda i: (i, 0))
    return _call(
        body, name="down_res", grid=(t // TM,),
        in_specs=[pl.BlockSpec((TM, f), lambda i: (i, 0)), pl.BlockSpec((f, d), lambda i: (0, 0)), row,
                  pl.BlockSpec((1, d), lambda i: (0, 0))],
        out_specs=[row, row],
        out_shape=[jax.ShapeDtypeStruct((t, d), F32), jax.ShapeDtypeStruct((t, d), BF16)],
        compiler_params=_params(1))(act, wdn, h, g_next)


def _proj(n, w_in_t):
    t, d = n.shape
    w = w_in_t.shape[0]

    def body(n_ref, w_ref, o_ref):
        o_ref[...] = _dot_nt(n_ref[...], w_ref[...]).astype(BF16)

    return _call(
        body, name="proj", grid=(t // TM,),
        in_specs=[pl.BlockSpec((TM, d), lambda i: (i, 0)), pl.BlockSpec((w, d), lambda i: (0, 0))],
        out_specs=pl.BlockSpec((TM, w), lambda i: (i, 0)),
        out_shape=jax.ShapeDtypeStruct((t, w), BF16), compiler_params=_params(1))(n, w_in_t)


def _out_res(o_sb, o_sw, g_sb, g_sw, w_out, h, g_next):
    t, d = h.shape

    def body(a_ref, b_ref, ga_ref, gb_ref, w_ref, h_ref, g_ref, o_ref, mix_ref, n_ref):
        ya, _, _ = _rms_fwd(a_ref[...], ga_ref[...])
        yb, _, _ = _rms_fwd(b_ref[...], gb_ref[...])
        mixed = jnp.concatenate([ya.astype(BF16), yb.astype(BF16)], axis=1)
        mix_ref[...] = mixed
        out = h_ref[...] + _dot(mixed, w_ref[...])
        o_ref[...] = out
        n_ref[...] = _rms_fwd(out, g_ref[...])[0].astype(BF16)

    row = pl.BlockSpec((TM, d), lambda i: (i, 0))
    return _call(
        body, name="out_res", grid=(t // TM,),
        in_specs=[pl.BlockSpec((TM, SB_W), lambda i: (i, 0)), pl.BlockSpec((TM, SWA_W), lambda i: (i, 0)),
                  pl.BlockSpec((1, SB_W), lambda i: (0, 0)), pl.BlockSpec((1, SWA_W), lambda i: (0, 0)),
                  pl.BlockSpec((d, d), lambda i: (0, 0)), row, pl.BlockSpec((1, d), lambda i: (0, 0))],
        out_specs=[row, row, row],
        out_shape=[jax.ShapeDtypeStruct((t, d), F32), jax.ShapeDtypeStruct((t, d), BF16),
                   jax.ShapeDtypeStruct((t, d), BF16)],
        compiler_params=_params(1))(o_sb, o_sw, g_sb, g_sw, w_out, h, g_next)


def _loss_head(h, g, tgt):
    t, d = h.shape

    def body(h_ref, g_ref, t_ref, dh_ref, dg_ref, loss_ref):
        @pl.when(pl.program_id(0) == 0)
        def _():
            dg_ref[...] = jnp.zeros_like(dg_ref)
            loss_ref[...] = jnp.zeros_like(loss_ref)

        gg = g_ref[...]
        y, xh, r = _rms_fwd(h_ref[...], gg)
        err = y - t_ref[...]
        part = 0.5 * jnp.sum(jnp.sum(err * err, axis=1, keepdims=True) / d, axis=0, keepdims=True)
        loss_ref[...] += jnp.broadcast_to(part, loss_ref.shape)
        dx, dg = _rms_bwd(err / d, xh, r, gg)
        dh_ref[...] = dx
        dg_ref[...] += dg

    return _call(
        body, name="loss_head", grid=(t // TM,),
        in_specs=[pl.BlockSpec((TM, d), lambda i: (i, 0)), pl.BlockSpec((1, d), lambda i: (0, 0)),
                  pl.BlockSpec((TM, d), lambda i: (i, 0))],
        out_specs=[pl.BlockSpec((TM, d), lambda i: (i, 0)), pl.BlockSpec((1, d), lambda i: (0, 0)),
                   pl.BlockSpec((1, LANES), lambda i: (0, 0))],
        out_shape=[jax.ShapeDtypeStruct((t, d), F32), jax.ShapeDtypeStruct((1, d), F32),
                   jax.ShapeDtypeStruct((1, LANES), F32)],
        compiler_params=_params(1))(h, g, tgt)


def _ffn_dact(dh, wdn, gu):
    t, d = dh.shape
    tm = TM

    def body(dh_ref, w_ref, gu_ref, o_ref):
        da = 0.5 * _dot_nt(dh_ref[...].astype(BF16), w_ref[...])
        o_ref[0] = (da * gu_ref[0].astype(F32)).astype(BF16)
        o_ref[1] = (da * gu_ref[1].astype(F32)).astype(BF16)

    return _call(
        body, name="ffn_dact", grid=(2, t // tm),
        in_specs=[pl.BlockSpec((tm, d), lambda j, i: (i, 0)), pl.BlockSpec((FS, d), lambda j, i: (j, 0)),
                  pl.BlockSpec((2, tm, FS), lambda j, i: (0, i, j))],
        out_specs=pl.BlockSpec((2, tm, FS), lambda j, i: (0, i, j)),
        out_shape=jax.ShapeDtypeStruct((2, t, D_FF), BF16), compiler_params=_params(2))(dh, wdn, gu)


def _dn_norm_bwd(a, a_spec, w, w_spec, nk, dh, h_in, g, w_transposed=False):
    t, d = dh.shape
    mm = _dot if w_transposed else _dot_nt

    def body(a_ref, w_ref, dh_ref, h_ref, g_ref, o_ref, dg_ref, acc_ref):
        i, k = pl.program_id(0), pl.program_id(1)

        if nk > 1:
            @pl.when(k == 0)
            def _():
                acc_ref[...] = mm(a_ref[...], w_ref[...])

            @pl.when((k > 0) & (k < nk - 1))
            def _():
                acc_ref[...] += mm(a_ref[...], w_ref[...])

        @pl.when(k == nk - 1)
        def _():
            gg = g_ref[...]
            dg = jnp.zeros_like(gg)
            for rows in (slice(0, TM // 2), slice(TM // 2, TM)):
                dn = mm(a_ref[rows, :], w_ref[...])
                if nk > 1:
                    dn = dn + acc_ref[rows, :]
                _, xh, r = _rms_fwd(h_ref[rows, :], gg)
                dx, dg_rows = _rms_bwd(dn, xh, r, gg)
                o_ref[rows, :] = dh_ref[rows, :] + dx
                dg = dg + dg_rows

            @pl.when(i == 0)
            def _():
                dg_ref[...] = dg

            @pl.when(i > 0)
            def _():
                dg_ref[...] += dg

    row = pl.BlockSpec((TM, d), lambda i, k: (i, 0))
    return _call(
        body, name="dn_norm_bwd", grid=(t // TM, nk),
        in_specs=[a_spec, w_spec, row, row, pl.BlockSpec((1, d), lambda i, k: (0, 0))],
        out_specs=[row, pl.BlockSpec((1, d), lambda i, k: (0, 0))],
        out_shape=[jax.ShapeDtypeStruct((t, d), F32), jax.ShapeDtypeStruct((1, d), F32)],
        scratch_shapes=[pltpu.VMEM((TM, d), F32)], compiler_params=_params(2))(a, w, dh, h_in, g)


def _ffn_dn(dgu, wgu, dh, h_in, g):
    d = dh.shape[1]
    return _dn_norm_bwd(
        dgu, pl.BlockSpec((None, TM, FS), lambda i, k: (k // 2, i, k % 2)),
        wgu, pl.BlockSpec((None, d, FS), lambda i, k: (k, 0, 0)), N_CHIPS, dh, h_in, g)


def _mix_dn(dproj, w_in_t, dh, h_in, g):
    d = dh.shape[1]
    w = dproj.shape[1]
    return _dn_norm_bwd(
        dproj, pl.BlockSpec((TM, w), lambda i, k: (i, 0)),
        w_in_t, pl.BlockSpec((w, d), lambda i, k: (0, 0)), 1, dh, h_in, g, w_transposed=True)


def _dmixed(dh, w_out, o_sb, o_sw, g_sb, g_sw):
    t, d = dh.shape

    def body(dh_ref, w_ref, a_ref, b_ref, ga_ref, gb_ref, o_ref, dga_ref, dgb_ref):
        i = pl.program_id(0)
        dm = _dot_nt(dh_ref[...].astype(BF16), w_ref[...])
        _, xa, ra = _rms_fwd(a_ref[...], ga_ref[...])
        _, xb, rb = _rms_fwd(b_ref[...], gb_ref[...])
        da, dga = _rms_bwd(dm[:, :SB_W], xa, ra, ga_ref[...])
        db, dgb = _rms_bwd(dm[:, SB_W:], xb, rb, gb_ref[...])
        o_ref[...] = jnp.concatenate([da.astype(BF16), db.astype(BF16)], axis=1)

        @pl.when(i == 0)
        def _():
            dga_ref[...] = dga
            dgb_ref[...] = dgb

        @pl.when(i > 0)
        def _():
            dga_ref[...] += dga
            dgb_ref[...] += dgb

    return _call(
        body, name="dmixed", grid=(t // TM,),
        in_specs=[pl.BlockSpec((TM, d), lambda i: (i, 0)), pl.BlockSpec((d, d), lambda i: (0, 0)),
                  pl.BlockSpec((TM, SB_W), lambda i: (i, 0)), pl.BlockSpec((TM, SWA_W), lambda i: (i, 0)),
                  pl.BlockSpec((1, SB_W), lambda i: (0, 0)), pl.BlockSpec((1, SWA_W), lambda i: (0, 0))],
        out_specs=[pl.BlockSpec((TM, d), lambda i: (i, 0)), pl.BlockSpec((1, SB_W), lambda i: (0, 0)),
                   pl.BlockSpec((1, SWA_W), lambda i: (0, 0))],
        out_shape=[jax.ShapeDtypeStruct((t, d), BF16), jax.ShapeDtypeStruct((1, SB_W), F32),
                   jax.ShapeDtypeStruct((1, SWA_W), F32)],
        compiler_params=_params(1))(dh, w_out, o_sb, o_sw, g_sb, g_sw)


def _wgrad(name, a, a_spec, b, b_spec, grid, out_shape, out_spec, scale):
    def body(a_ref, b_ref, o_ref):
        r = _dot_tn(a_ref[...], b_ref[...].astype(BF16))
        o_ref[...] = r if scale == 1.0 else scale * r

    return _call(
        body, name=name, grid=grid, in_specs=[a_spec, b_spec], out_specs=out_spec,
        out_shape=jax.ShapeDtypeStruct(out_shape, F32), compiler_params=_params(len(grid)))(a, b)


def _wgrad_gu(n, dgu):
    t, d = n.shape
    return _wgrad(
        "wgrad_gu", n, pl.BlockSpec((t, TM), lambda s, r: (0, r)),
        dgu, pl.BlockSpec((None, t, FS), lambda s, r: (s // 2, 0, s % 2)), (N_CHIPS, d // TM),
        (N_CHIPS, d, FS), pl.BlockSpec((None, TM, FS), lambda s, r: (s, r, 0)), 1.0)


def _wgrad_down(act, dh):
    t, d = dh.shape
    return _wgrad(
        "wgrad_down", act, pl.BlockSpec((t, FS), lambda s, r: (0, s)), dh, pl.BlockSpec((t, TM), lambda s, r: (0, r)),
        (2, d // TM), (D_FF, d), pl.BlockSpec((FS, TM), lambda s, r: (s, r)), 0.5)


def _wgrad_out(mixed, dh):
    t, d = dh.shape
    return _wgrad(
        "wgrad_out", mixed, pl.BlockSpec((t, TM), lambda s: (0, s)), dh, pl.BlockSpec((t, d), lambda s: (0, 0)),
        (d // TM,), (d, d), pl.BlockSpec((TM, d), lambda s: (s, 0)), 1.0)


def _wgrad_in(n, dproj):
    t, d = n.shape
    w = dproj.shape[1]
    tw = w // 3
    return _wgrad(
        "wgrad_in", dproj, pl.BlockSpec((t, tw), lambda s: (0, s)), n, pl.BlockSpec((t, d), lambda s: (0, 0)),
        (3,), (w, d), pl.BlockSpec((tw, d), lambda s: (s, 0)), 1.0)


def _tri(rel):
    row = lax.broadcasted_iota(jnp.int32, (BLK, BLK), 0)
    col = lax.broadcasted_iota(jnp.int32, (BLK, BLK), 1)
    m = rel(row, col).astype(BF16)
    return jnp.concatenate([m, m], axis=0)


def _scan_dot(x, tri2):
    hi = x.astype(BF16)
    lo = (x - hi.astype(F32)).astype(BF16)
    return _dot(jnp.concatenate([hi, lo], axis=1), tri2)


def _head_masks():
    lane = lax.broadcasted_iota(jnp.int32, (1, LANES), 1)
    return [lane < HEAD_DIM, lane >= HEAD_DIM]


SB_PAIRS = 2
SB_ROWS = 2 * SB_PAIRS * BLK


def _sb_dcol():
    dcol = lax.broadcasted_iota(jnp.int32, (BLK, SB_KT), 1) - lax.broadcasted_iota(jnp.int32, (BLK, SB_KT), 0)
    return jnp.concatenate([dcol] * (2 * SB_PAIRS), axis=0)


def _sb_stack(x, hm):
    return jnp.concatenate([jnp.where(m, x[:, p * LANES:(p + 1) * LANES], jnp.zeros((BLK, LANES), x.dtype))
                            for p in range(SB_PAIRS) for m in hm], axis=0)


def _sb_unstack(y, hm):
    return jnp.concatenate([jnp.where(hm[0], y[2 * p * BLK:(2 * p + 1) * BLK], y[(2 * p + 1) * BLK:(2 * p + 2) * BLK])
                            for p in range(SB_PAIRS)], axis=1)


def _sb_pairs():
    return [(slice(2 * p * BLK, (2 * p + 2) * BLK), slice(p * LANES, (p + 1) * LANES)) for p in range(SB_PAIRS)]


def _sb_fwd(proj):
    t = proj.shape[0]
    nq = t // BLK
    nb = SB_KT // BLK
    wide = SB_PAIRS * LANES

    def body(q_ref, k_ref, v_ref, o_ref, tot_ref):
        hm = _head_masks()
        dcol = _sb_dcol()
        pairs = _sb_pairs()
        after = _tri(lambda r, c: r > c)

        def tile(qh, kt, carry, acc, limit):
            ks = pl.ds(pl.multiple_of(kt * SB_KT, SB_KT), SB_KT)
            z = jnp.concatenate([_dot_nt(qh[rows], k_ref[ks, lanes]) for rows, lanes in pairs], axis=0)
            sp, zs = _softplus(z)
            valid = None if limit is None else dcol < limit
            spm = sp if valid is None else jnp.where(valid, sp, 0.0)
            sufs = [None] * nb
            for b in reversed(range(nb)):
                blk = spm[:, b * BLK:(b + 1) * BLK]
                sufs[b] = carry + _scan_dot(blk, after)
                carry = carry + jnp.sum(blk, axis=1, keepdims=True)
            w = jnp.exp(zs - jnp.concatenate(sufs, axis=1))
            if valid is not None:
                w = jnp.where(valid, w, 0.0)
            wb = w.astype(BF16)
            return carry, acc + jnp.concatenate([_dot(wb[rows], v_ref[ks, lanes]) for rows, lanes in pairs], axis=0)

        def qblock(qi, _):
            qs = pl.ds(pl.multiple_of(qi * BLK, BLK), BLK)
            kd = qi // nb
            limit = (qi - kd * nb) * BLK
            qh = _sb_stack(q_ref[qs, :] * SCALE, hm)
            c0 = tile(qh, kd, jnp.zeros((SB_ROWS, 1), F32), jnp.zeros((SB_ROWS, LANES), F32), limit)
            carry, acc = lax.fori_loop(0, kd, lambda n, c: tile(qh, kd - 1 - n, c[0], c[1], None), c0)
            o_ref[qs, :] = _sb_unstack(acc, hm)
            for h in range(2 * SB_PAIRS):
                tot_ref[h, qs, :] = jnp.broadcast_to(carry[h * BLK:(h + 1) * BLK], (BLK, LANES))
            return 0

        lax.fori_loop(0, nq, qblock, 0)

    col_blk = lambda off: pl.BlockSpec((t, wide), lambda g: (0, off + g))
    n_steps = SB_W // wide
    return _call(
        body, name="sb_fwd", grid=(n_steps,), in_specs=[col_blk(0), col_blk(n_steps), col_blk(2 * n_steps)],
        out_specs=[col_blk(0), pl.BlockSpec((2 * SB_PAIRS, t, LANES), lambda g: (g, 0, 0))],
        out_shape=[jax.ShapeDtypeStruct((t, SB_W), F32), jax.ShapeDtypeStruct((8, t, LANES), F32)],
        compiler_params=_params(1))(proj, proj, proj)


def _sb_bwd(proj, d_o, tot):
    t = proj.shape[0]
    nq = t // BLK
    nb = SB_KT // BLK
    wide = SB_PAIRS * LANES

    def body(q_ref, k_ref, v_ref, do_ref, tot_ref, dq_ref, dk_ref, dv_ref, dk_acc, dv_acc):
        hm = _head_masks()
        dcol = _sb_dcol()
        pairs = _sb_pairs()
        before = _tri(lambda r, c: r < c)
        upto = _tri(lambda r, c: r <= c)
        dk_acc[...] = jnp.zeros_like(dk_acc)
        dv_acc[...] = jnp.zeros_like(dv_acc)

        def tile(qh, doh, tt, kt, pre, ecum, dq, limit):
            ks = pl.ds(pl.multiple_of(kt * SB_KT, SB_KT), SB_KT)
            k = k_ref[ks, :]
            v = v_ref[ks, :]
            z = jnp.concatenate([_dot_nt(qh[rows], k[:, lanes]) for rows, lanes in pairs], axis=0)
            sp, zs = _softplus(z)
            valid = None if limit is None else dcol < limit
            spm = sp if valid is None else jnp.where(valid, sp, 0.0)
            pres = []
            for b in range(nb):
                blk = spm[:, b * BLK:(b + 1) * BLK]
                pres.append(pre + _scan_dot(blk, before))
                pre = pre + jnp.sum(blk, axis=1, keepdims=True)
            logw = z - (tt - jnp.concatenate(pres, axis=1))
            if valid is not None:
                logw = jnp.minimum(logw, 0.0)
            w = jnp.exp(logw)
            if valid is not None:
                w = jnp.where(valid, w, 0.0)
            e = w * jnp.concatenate([_dot_nt(doh[rows], v[:, lanes]) for rows, lanes in pairs], axis=0)
            incs = []
            for b in range(nb):
                blk = e[:, b * BLK:(b + 1) * BLK]
                incs.append(ecum + _scan_dot(blk, upto))
                ecum = ecum + jnp.sum(blk, axis=1, keepdims=True)
            dz = e - jnp.exp(zs) * jnp.concatenate(incs, axis=1)
            if valid is not None:
                dz = jnp.where(valid, dz, 0.0)
            dzb = dz.astype(BF16)
            wb = w.astype(BF16)
            for rows, lanes in pairs:
                dk_acc[ks, lanes] += _dot_tn(dzb[rows], qh[rows])
                dv_acc[ks, lanes] += _dot_tn(wb[rows], doh[rows])
            return pre, ecum, dq + jnp.concatenate([_dot(dzb[rows], k[:, lanes]) for rows, lanes in pairs], axis=0)

        def qblock(qi, _):
            qs = pl.ds(pl.multiple_of(qi * BLK, BLK), BLK)
            kd = qi // nb
            limit = (qi - kd * nb) * BLK
            qh = _sb_stack(q_ref[qs, :] * SCALE, hm)
            doh = _sb_stack(do_ref[qs, :], hm)
            tt = jnp.concatenate([tot_ref[h, qs, 0:1] for h in range(2 * SB_PAIRS)], axis=0)
            c0 = (jnp.zeros((SB_ROWS, 1), F32), jnp.zeros((SB_ROWS, 1), F32), jnp.zeros((SB_ROWS, LANES), F32))
            c = lax.fori_loop(0, kd, lambda kt, c: tile(qh, doh, tt, kt, c[0], c[1], c[2], None), c0)
            dq = tile(qh, doh, tt, kd, c[0], c[1], c[2], limit)[2]
            dq_ref[qs, :] = (_sb_unstack(dq, hm) * SCALE).astype(BF16)
            return 0

        lax.fori_loop(0, nq, qblock, 0)
        dk_ref[...] = dk_acc[...].astype(BF16)
        dv_ref[...] = dv_acc[...].astype(BF16)

    col_blk = lambda off: pl.BlockSpec((t, wide), lambda g: (0, off + g))
    n_steps = SB_W // wide
    out = jax.ShapeDtypeStruct((t, SB_W), BF16)
    return _call(
        body, name="sb_bwd", grid=(n_steps,),
        in_specs=[col_blk(0), col_blk(n_steps), col_blk(2 * n_steps), col_blk(0),
                  pl.BlockSpec((2 * SB_PAIRS, t, LANES), lambda g: (g, 0, 0))],
        out_specs=[col_blk(0), col_blk(0), col_blk(0)], out_shape=[out, out, out],
        scratch_shapes=[pltpu.VMEM((t, wide), F32), pltpu.VMEM((t, wide), F32)],
        compiler_params=_params(1))(proj, proj, proj, d_o, tot)


def _bucket_table():
    a = np.arange(BLK)[:, None]
    c = np.arange(2 * BLK)[None, :]
    dist = np.maximum(BLK + a - c, 0)
    max_exact = N_BUCKETS // 2
    dd = np.maximum(dist, 1).astype(np.float32)
    large = max_exact + (np.log(dd / max_exact) / math.log(MAX_DISTANCE / max_exact)
                         * (N_BUCKETS - max_exact)).astype(np.int32)
    large = np.minimum(large, N_BUCKETS - 1)
    return np.where(dist < max_exact, dist, large).astype(np.int32)


SWA_H = 8


def _swa_band_masks():
    row = lax.broadcasted_iota(jnp.int32, (SWA_H * BLK, 2 * BLK), 0) & (BLK - 1)
    col = lax.broadcasted_iota(jnp.int32, (SWA_H * BLK, 2 * BLK), 1)
    own = lax.broadcasted_iota(jnp.int32, (SWA_H * BLK, BLK), 1) <= (
        lax.broadcasted_iota(jnp.int32, (SWA_H * BLK, BLK), 0) & (BLK - 1))
    return (col > row) & ((col < BLK) | (col - BLK <= row)), own


def _swa_stack(ref, qs, hm, scale):
    parts = []
    for hq in range(SWA_H):
        kvh = hq // SWA_G
        x = ref[qs, (hq // 2) * LANES:(hq // 2 + 1) * LANES].astype(F32)
        if hq % 2 != kvh:
            x = pltpu.roll(x, HEAD_DIM, 1)
        parts.append(jnp.where(hm[kvh], x * scale, 0.0).astype(BF16))
    return jnp.concatenate(parts, axis=0)


def _swa_unstack(x8, hm):
    heads = []
    for hq in range(SWA_H):
        x = x8[hq * BLK:(hq + 1) * BLK]
        heads.append(pltpu.roll(x, HEAD_DIM, 1) if hq % 2 != hq // SWA_G else x)
    return [jnp.where(hm[0], heads[2 * p], heads[2 * p + 1]) for p in range(SWA_H // 2)]


def _swa_scores(q8, kb, bias_ref, mask, cols):
    bias8 = jnp.concatenate([bias_ref[hq, :, cols] for hq in range(SWA_H)], axis=0)
    return jnp.where(mask, _dot_nt(q8, kb) + bias8, NEG_INF)


def _swa_sinks(sink_ref):
    return jnp.concatenate([jnp.broadcast_to(sink_ref[hq:hq + 1, 0:1], (BLK, 1)) for hq in range(SWA_H)], axis=0)


def _swa_fwd(proj, bias, sinks_b):
    t = proj.shape[0]
    nq = t // BLK

    def body(q_ref, k_ref, v_ref, bias_ref, sink_ref, o_ref, lse_ref):
        hm = _head_masks()
        band, own = _swa_band_masks()

        def qblock(i, prev):
            qs = pl.ds(pl.multiple_of(i * BLK, BLK), BLK)
            if prev:
                ks, mask, cols = pl.ds(pl.multiple_of((i - 1) * BLK, BLK), 2 * BLK), band, slice(None)
            else:
                ks, mask, cols = qs, own, slice(BLK, None)
            q8 = _swa_stack(q_ref, qs, hm, SCALE)
            sink8 = _swa_sinks(sink_ref)
            s = _swa_scores(q8, k_ref[ks, :], bias_ref, mask, cols)
            m = jnp.maximum(jnp.max(s, axis=1, keepdims=True), sink8)
            p = jnp.exp(s - m)
            den = jnp.sum(p, axis=1, keepdims=True) + jnp.exp(sink8 - m)
            o8 = _dot((p * (1.0 / den)).astype(BF16), v_ref[ks, :])
            lse8 = m + jnp.log(den)
            for hq in range(SWA_H):
                lse_ref[hq, qs, :] = jnp.broadcast_to(lse8[hq * BLK:(hq + 1) * BLK], (BLK, LANES))
            for pp, o in enumerate(_swa_unstack(o8, hm)):
                o_ref[qs, pp * LANES:(pp + 1) * LANES] = o

        qblock(0, False)

        def step(i, _):
            qblock(i, True)
            return 0

        lax.fori_loop(1, nq, step, 0)

    return _call(
        body, name="swa_fwd", grid=(1,),
        in_specs=[pl.BlockSpec((t, SWA_W), lambda i: (0, 3)), pl.BlockSpec((t, KV_W), lambda i: (0, 16)),
                  pl.BlockSpec((t, KV_W), lambda i: (0, 17)), pl.BlockSpec((8, BLK, 2 * BLK), lambda i: (0, 0, 0)),
                  pl.BlockSpec((8, LANES), lambda i: (0, 0))],
        out_specs=[pl.BlockSpec((t, SWA_W), lambda i: (0, 0)), pl.BlockSpec((8, t, LANES), lambda i: (0, 0, 0))],
        out_shape=[jax.ShapeDtypeStruct((t, SWA_W), F32), jax.ShapeDtypeStruct((8, t, LANES), F32)],
        compiler_params=_params(1))(proj, proj, proj, bias, sinks_b)


def _swa_bwd(proj, d_o, lse, bias, sinks_b, dbias_in):
    t = proj.shape[0]
    nq = t // BLK

    def body(q_ref, k_ref, v_ref, do_ref, lse_ref, bias_ref, sink_ref, dbi_ref,
             dq_ref, dk_ref, dv_ref, dsink_ref, dbias_ref, dk_acc, dv_acc):
        hm = _head_masks()
        band, own = _swa_band_masks()
        dk_acc[...] = jnp.zeros_like(dk_acc)
        dv_acc[...] = jnp.zeros_like(dv_acc)
        dbias_ref[...] = dbi_ref[...]

        def qblock(i, prev, dsink8):
            qs = pl.ds(pl.multiple_of(i * BLK, BLK), BLK)
            if prev:
                ks, mask, cols = pl.ds(pl.multiple_of((i - 1) * BLK, BLK), 2 * BLK), band, slice(None)
            else:
                ks, mask, cols = qs, own, slice(BLK, None)
            q8 = _swa_stack(q_ref, qs, hm, SCALE)
            do8 = _swa_stack(do_ref, qs, hm, 1.0)
            sink8 = _swa_sinks(sink_ref)
            lse8 = jnp.concatenate([lse_ref[hq, qs, 0:1] for hq in range(SWA_H)], axis=0)
            kb = k_ref[ks, :]
            p = jnp.exp(_swa_scores(q8, kb, bias_ref, mask, cols) - lse8)
            dp = _dot_nt(do8, v_ref[ks, :])
            delta = jnp.sum(p * dp, axis=1, keepdims=True)
            ds = p * (dp - delta)
            for hq in range(SWA_H):
                dbias_ref[hq, :, cols] += ds[hq * BLK:(hq + 1) * BLK]
            dsb = ds.astype(BF16)
            dk_acc[ks, :] += _dot_tn(dsb, q8)
            dv_acc[ks, :] += _dot_tn(p.astype(BF16), do8)
            for pp, dq in enumerate(_swa_unstack(_dot(dsb, kb) * SCALE, hm)):
                dq_ref[qs, pp * LANES:(pp + 1) * LANES] = dq.astype(BF16)
            return dsink8 - jnp.exp(sink8 - lse8) * delta

        ds0 = qblock(0, False, jnp.zeros((SWA_H * BLK, 1), F32))
        ds8 = lax.fori_loop(1, nq, lambda i, c: qblock(i, True, c), ds0)
        for hq in range(SWA_H):
            dsink_ref[hq:hq + 1, :] = jnp.broadcast_to(
                jnp.sum(ds8[hq * BLK:(hq + 1) * BLK], axis=0, keepdims=True), (1, LANES))

        dk_ref[...] = dk_acc[...].astype(BF16)
        dv_ref[...] = dv_acc[...].astype(BF16)

    full3 = pl.BlockSpec((8, BLK, 2 * BLK), lambda i: (0, 0, 0))
    kv = jax.ShapeDtypeStruct((t, KV_W), BF16)
    return _call(
        body, name="swa_bwd", grid=(1,),
        in_specs=[pl.BlockSpec((t, SWA_W), lambda i: (0, 3)), pl.BlockSpec((t, KV_W), lambda i: (0, 16)),
                  pl.BlockSpec((t, KV_W), lambda i: (0, 17)), pl.BlockSpec((t, SWA_W), lambda i: (0, 1)),
                  pl.BlockSpec((8, t, LANES), lambda i: (0, 0, 0)), full3, pl.BlockSpec((8, LANES), lambda i: (0, 0)),
                  full3],
        out_specs=[pl.BlockSpec((t, SWA_W), lambda i: (0, 0)), pl.BlockSpec((t, KV_W), lambda i: (0, 0)),
                   pl.BlockSpec((t, KV_W), lambda i: (0, 0)), pl.BlockSpec((8, LANES), lambda i: (0, 0)), full3],
        out_shape=[jax.ShapeDtypeStruct((t, SWA_W), BF16), kv, kv, jax.ShapeDtypeStruct((8, LANES), F32),
                   jax.ShapeDtypeStruct((8, BLK, 2 * BLK), F32)],
        scratch_shapes=[pltpu.VMEM((t, KV_W), F32), pltpu.VMEM((t, KV_W), F32)],
        compiler_params=_params(1))(proj, proj, proj, d_o, lse, bias, sinks_b, dbias_in)


def _bias_table(rel_bias, buckets):
    def body(rb_ref, b_ref, o_ref):
        bk = b_ref[...]
        for h in range(8):
            acc = jnp.zeros((BLK, 2 * BLK), F32)
            for b in range(N_BUCKETS):
                acc = jnp.where(bk == b, rb_ref[b, h], acc)
            o_ref[h] = acc

    return _call(
        body, name="bias_table", grid=(1,),
        in_specs=[pl.BlockSpec(memory_space=pltpu.SMEM), pl.BlockSpec((BLK, 2 * BLK), lambda i: (0, 0))],
        out_specs=pl.BlockSpec((8, BLK, 2 * BLK), lambda i: (0, 0, 0)),
        out_shape=jax.ShapeDtypeStruct((8, BLK, 2 * BLK), F32), compiler_params=_params(1))(rel_bias, buckets)


def _bias_grad(dbias, buckets):
    def body(d_ref, b_ref, o_ref):
        lane = lax.broadcasted_iota(jnp.int32, (1, LANES), 1)
        bk = b_ref[...]
        for h in range(8):
            d = d_ref[h]
            acc = jnp.zeros((1, LANES), F32)
            for b in range(N_BUCKETS):
                s = jnp.sum(jnp.sum(jnp.where(bk == b, d, 0.0), axis=0, keepdims=True), axis=1, keepdims=True)
                acc = acc + jnp.where(lane == b, s, 0.0)
            o_ref[h:h + 1, :] = acc

    return _call(
        body, name="bias_grad", grid=(1,),
        in_specs=[pl.BlockSpec((8, BLK, 2 * BLK), lambda i: (0, 0, 0)), pl.BlockSpec((BLK, 2 * BLK), lambda i: (0, 0))],
        out_specs=pl.BlockSpec((8, LANES), lambda i: (0, 0)),
        out_shape=jax.ShapeDtypeStruct((8, LANES), F32), compiler_params=_params(1))(dbias, buckets)


def _row(a):
    return a.reshape(1, -1)


def _fwd_ffn1(h, n1, w, small, l):
    s = {"h0": h, "n1": n1}
    s["gu1"], s["act1"] = _ffn_gu(n1, w["ffn1_gu"])
    s["h1"], s["nm"] = _down_res(s["act1"], w["ffn1_down"], h, _row(small["norm_mix"][l]))
    return s


def _fwd_proj_sb(s, w):
    s["proj"] = _proj(s["nm"], w["w_in"])
    s["o_sb"], s["tot"] = _sb_fwd(s["proj"])


def _fwd_swa(s, small, l, bias):
    s["sinks_b"] = jnp.broadcast_to(small["sinks"][l][:, None], (8, LANES))
    s["o_sw"], s["lse"] = _swa_fwd(s["proj"], bias, s["sinks_b"])


def _fwd_out_ffn2(s, w, small, l, g_after):
    s["h2"], s["mixed"], s["n2"] = _out_res(
        s["o_sb"], s["o_sw"], _row(small["norm_out_sb"][l]), _row(small["norm_out_swa"][l]), w["w_out"], s["h1"],
        _row(small["norm_ffn2"][l]))
    s["gu2"], s["act2"] = _ffn_gu(s["n2"], w["ffn2_gu"])
    return _down_res(s["act2"], w["ffn2_down"], s["h2"], g_after)


def _bwd_ffn_dact(dh, s, w, which):
    return _ffn_dact(dh, w[f"ffn{which}_down"], s[f"gu{which}"])


def _bwd_ffn_rest(dh, dgu, s, w, small, l, which):
    h_in, norm = (s["h0"], "norm_ffn1") if which == 1 else (s["h2"], "norm_ffn2")
    g_down = _wgrad_down(s[f"act{which}"], dh)
    g_gu = _wgrad_gu(s[f"n{which}"], dgu)
    dh, dg = _ffn_dn(dgu, w[f"ffn{which}_gu"], dh, h_in, _row(small[norm][l]))
    return dh, {f"ffn{which}_down": g_down, f"ffn{which}_gu": g_gu}, {norm: dg}


def _bwd_ffn(dh, s, w, small, l, which):
    return _bwd_ffn_rest(dh, _bwd_ffn_dact(dh, s, w, which), s, w, small, l, which)


def _bwd_mix(dh, s, w, small, l, bias, dbias):
    g_out = _wgrad_out(s["mixed"], dh)
    d_o, dg_sb, dg_sw = _dmixed(dh, w["w_out"], s["o_sb"], s["o_sw"], _row(small["norm_out_sb"][l]),
                                _row(small["norm_out_swa"][l]))
    dq_sb, dk_sb, dv_sb = _sb_bwd(s["proj"], d_o, s["tot"])
    dq_sw, dk_sw, dv_sw, dsink, dbias = _swa_bwd(s["proj"], d_o, s["lse"], bias, s["sinks_b"], dbias)
    dproj = jnp.concatenate([dq_sb, dk_sb, dv_sb, dq_sw, dk_sw, dv_sw], axis=1)
    g_in = _wgrad_in(s["nm"], dproj)
    dh, dg_mix = _mix_dn(dproj, w["w_in"], dh, s["h1"], _row(small["norm_mix"][l]))
    gs = {"norm_out_sb": dg_sb, "norm_out_swa": dg_sw, "sinks": dsink[:, 0], "norm_mix": dg_mix}
    return dh, {"w_out": g_out, "w_in": g_in}, gs, dbias


def _place():
    x, y, c = lax.axis_index("x"), lax.axis_index("y"), lax.axis_index("c")
    return x, y, c, 2 * x + y


def _chip_core(k, c):
    return (k // 2, k % 2, c)


def _rows_per_block(rows, cols, copies):
    best = 16
    for tr in range(16, rows + 1, 16):
        if rows % tr == 0 and copies * tr * cols * 4 <= 6 * 2 ** 20:
            best = tr
    assert rows % best == 0
    return best


def _place_own(w, l, me1):
    _, rows, cols = w.shape
    tr = _rows_per_block(rows // 2, cols, 1)
    per_half = rows // 2 // tr

    def body(me_ref, w_ref, o_ref):
        o_ref[...] = w_ref[...].astype(BF16)

    return _call(
        body, name="place_own",
        num_scalar_prefetch=1, grid=(rows // tr,),
        in_specs=[pl.BlockSpec((None, tr, cols), lambda r, me: (l, r, 0))],
        out_specs=pl.BlockSpec((None, None, tr, cols), lambda r, me: (me[0], r // per_half, r % per_half, 0)),
        out_shape=jax.ShapeDtypeStruct((N_CHIPS, 2, rows // 2, cols), BF16), compiler_params=_params(1))(me1, w)


def _plan_gather_ici(bufs):
    _, _, c, me = _place()
    return [(b.at[me, c], b.at[me, c], b.at[(me + 3 - j) % N_CHIPS, c], _chip_core((me + 1 + j) % N_CHIPS, c))
            for b in bufs for j in range(3)]


def _plan_gather_d2d(bufs):
    x, y, c, me = _place()
    return [(b.at[(me + 3 - j) % N_CHIPS, c], b.at[(me + 3 - j) % N_CHIPS, c], b.at[(me + 3 - j) % N_CHIPS, 1 - c],
             (x, y, 1 - c)) for b in bufs for j in range(3)]


def _plan_grad_sibling(bufs):
    x, y, c, _ = _place()
    n = len(bufs) // 2
    return [(g.at[:, 1 - c], z, z, (x, y, 1 - c)) for g, z in zip(bufs[:n], bufs[n:])]


def _plan_grad_chips(bufs):
    _, _, c, me = _place()
    n = len(bufs) // 2
    return [(p.at[j], z.at[j], z.at[j], _chip_core((me + 1 + j) % N_CHIPS, c))
            for p, z in zip(bufs[:n], bufs[n:]) for j in range(3)]


def _plan_grad_halves(bufs):
    x, y, c, _ = _place()
    return [(b.at[c], b.at[c], b.at[1 - c], (x, y, 1 - c)) for b in bufs]


def _remote(src, dst, send_sem, recv_sem, to):
    return pltpu.make_async_remote_copy(src_ref=src, dst_ref=dst, send_sem=send_sem, recv_sem=recv_sem,
                                        device_id=to, device_id_type=MESH)


def _exchange_start(name, plan, bufs, n_copies):
    n = len(bufs)

    def body(*refs):
        ins = refs[:n]
        ssem, rsem = refs[n], refs[n + 1]
        token = refs[-1]
        for i, (src, dst, _, to) in enumerate(plan(ins)):
            _remote(src, dst, ssem.at[i], rsem.at[i], to).start()
        token[...] = jnp.zeros_like(token)

    out = _call(
        body, name=name,
        out_shape=(pltpu.SemaphoreType.DMA((n_copies,)), pltpu.SemaphoreType.DMA((n_copies,)),
                   *[pltpu.HBM(a.shape, a.dtype) for a in bufs], jax.ShapeDtypeStruct((8, LANES), F32)),
        in_specs=[HBM] * n, out_specs=(SEM, SEM, *[HBM] * n, pl.BlockSpec(memory_space=pltpu.VMEM)),
        input_output_aliases={t: 2 + t for t in range(n)}, hbm_args=n,
        compiler_params=pltpu.CompilerParams(has_side_effects=EFFECT),
    )(*bufs)
    return (out[0], out[1]), list(out[2:2 + n])


def _exchange_wait(name, plan, bufs, sems):
    n = len(bufs)

    def body(*refs):
        ins = refs[:n]
        ssem, rsem = refs[n], refs[n + 1]
        for i, (src, dst, land, to) in enumerate(plan(ins)):
            _remote(src, dst, ssem.at[i], rsem.at[i], to).wait_send()
            _remote(land, land, ssem.at[i], rsem.at[i], to).wait_recv()

    return list(_call(
        body, name=name, out_shape=[pltpu.HBM(a.shape, a.dtype) for a in bufs],
        in_specs=[HBM] * n + [SEM, SEM], out_specs=[HBM] * n,
        input_output_aliases={t: t for t in range(n)},
        compiler_params=pltpu.CompilerParams(has_side_effects=EFFECT),
    )(*bufs, sems[0], sems[1]))


def _chip_sum(g, xbuf, cm):
    _, _, r2, cols = g.shape
    tr = _rows_per_block(r2, cols, 1)

    def body(cm_ref, g_ref, x_ref, o_ref):
        o_ref[...] = (g_ref[...] + x_ref[...]).astype(BF16)

    return _call(
        body, name="grad_chip_sum",
        num_scalar_prefetch=1, grid=(3, r2 // tr),
        in_specs=[pl.BlockSpec((None, None, tr, cols), lambda j, r, cm: ((cm[1] + 1 + j) % N_CHIPS, cm[0], r, 0)),
                  pl.BlockSpec((None, tr, cols), lambda j, r, cm: ((cm[1] + 1 + j) % N_CHIPS, r, 0))],
        out_specs=pl.BlockSpec((None, tr, cols), lambda j, r, cm: (j, r, 0)),
        out_shape=jax.ShapeDtypeStruct((3, r2, cols), BF16), compiler_params=_params(2))(cm, g, xbuf)


def _total_sum(g, xbuf, rbuf, cm):
    _, _, r2, cols = g.shape
    tr = _rows_per_block(r2, cols, 3)

    def body(cm_ref, g_ref, x_ref, r_ref, o_ref):
        acc = g_ref[...] + x_ref[...]
        for j in range(3):
            acc = acc + r_ref[j].astype(F32)
        o_ref[...] = acc

    return _call(
        body, name="grad_total_sum",
        num_scalar_prefetch=1, grid=(r2 // tr,),
        in_specs=[pl.BlockSpec((None, None, tr, cols), lambda r, cm: (cm[1], cm[0], r, 0)),
                  pl.BlockSpec((None, tr, cols), lambda r, cm: (cm[1], r, 0)),
                  pl.BlockSpec((3, tr, cols), lambda r, cm: (0, r, 0))],
        out_specs=pl.BlockSpec((None, tr, cols), lambda r, cm: (cm[0], r, 0)),
        out_shape=jax.ShapeDtypeStruct((2, r2, cols), F32), compiler_params=_params(1))(cm, g, xbuf, rbuf)


def _small_allreduce(v):
    rows = v.shape[0]
    n_dev = 2 * N_CHIPS

    def body(v_ref, o_ref, buf, ssem, rsem):
        x, y, c, _ = _place()
        me = 4 * x + 2 * y + c
        buf[me] = v_ref[...]

        def copy(d, slot, to):
            return _remote(v_ref, buf.at[slot], ssem.at[d - 1], rsem.at[d - 1], (to // 4, (to // 2) % 2, to % 2))

        cps = [copy(d, me, (me + d) % n_dev) for d in range(1, n_dev)]
        for cp in cps:
            cp.start()
        for d in range(1, n_dev):
            copy(d, (me + n_dev - d) % n_dev, me).wait_recv()
        for cp in cps:
            cp.wait_send()
        acc = buf[0]
        for i in range(1, n_dev):
            acc = acc + buf[i]
        o_ref[...] = acc

    vm = pl.BlockSpec(memory_space=pltpu.VMEM)
    return _call(
        body, name="small_allreduce", in_specs=[vm], out_specs=vm,
        out_shape=jax.ShapeDtypeStruct(v.shape, F32),
        scratch_shapes=[pltpu.VMEM((n_dev, rows, LANES), F32), pltpu.SemaphoreType.DMA((n_dev - 1,)),
                        pltpu.SemaphoreType.DMA((n_dev - 1,))],
        compiler_params=pltpu.CompilerParams(vmem_limit_bytes=V7X_VMEM_LIMIT))(v)


def _adamw_math(w, g, m, v):
    m2 = ADAM_B1 * m + (1.0 - ADAM_B1) * g
    v2 = ADAM_B2 * v + (1.0 - ADAM_B2) * (g * g)
    m_hat = m2 / (1.0 - ADAM_B1 ** ADAM_STEP)
    v_hat = v2 / (1.0 - ADAM_B2 ** ADAM_STEP)
    return -ADAM_LR * (m_hat / (jnp.sqrt(v_hat) + ADAM_EPS) + ADAM_WD * w), m2, v2


def _adamw_layer(w, g, m, v, l, prev):
    _, rows, cols = w.shape
    tr = rows
    for cand in range(8, rows + 1, 8):
        if rows % cand == 0 and cand * cols * 4 <= 2 ** 21:
            tr = cand

    def body(w_ref, g_ref, m_ref, v_ref, *outs):
        go_ref, d_ref, m2_ref, v2_ref = outs[-4:]
        g = g_ref[...]
        go_ref[...] = g
        d_ref[...], m2_ref[...], v2_ref[...] = _adamw_math(w_ref[...], g, m_ref[...], v_ref[...])

    stack = pl.BlockSpec((None, tr, cols), lambda i: (l, i, 0))
    ins, specs, alias = [w, g, m, v], [stack, pl.BlockSpec((tr, cols), lambda i: (i, 0)), stack, stack], {}
    if prev is not None:
        ins += list(prev)
        specs += [ANY] * 4
        alias = {4 + i: i for i in range(4)}
    return _call(
        body, name="adamw", grid=(rows // tr,), in_specs=specs, out_specs=[stack] * 4,
        out_shape=[jax.ShapeDtypeStruct(w.shape, F32)] * 4, input_output_aliases=alias,
        compiler_params=_params(1))(*ins)


def _adamw_small(w, g, m, v):
    def body(w_ref, g_ref, m_ref, v_ref, d_ref, m2_ref, v2_ref):
        d_ref[...], m2_ref[...], v2_ref[...] = _adamw_math(w_ref[...], g_ref[...], m_ref[...], v_ref[...])

    spec = pl.BlockSpec(w.shape, lambda i: (0, 0))
    return _call(
        body, name="adamw_small", grid=(1,), in_specs=[spec] * 4, out_specs=[spec] * 3,
        out_shape=[jax.ShapeDtypeStruct(w.shape, F32)] * 3, compiler_params=_params(1))(w, g, m, v)


SMALL = ("norm_ffn1", "norm_mix", "sinks", "norm_out_sb", "norm_out_swa", "norm_ffn2", "rel_bias", "norm_final")
BIG = ("ffn1_gu", "ffn1_down", "w_in", "w_out", "ffn2_gu", "ffn2_down")


def _pack(parts):
    flat, n = [], 0
    for a in parts:
        a = a.reshape(-1).astype(F32)
        gap = -a.shape[0] % LANES
        flat += [a] + ([jnp.zeros((gap,), F32)] if gap else [])
        n += a.shape[0] + gap
    tail = -(n // LANES) % 8 * LANES
    return jnp.concatenate(flat + ([jnp.zeros((tail,), F32)] if tail else [])).reshape(-1, LANES)


def _unpack(packed, like):
    out, r = [], 0
    for a in like:
        n = math.prod(a.shape)
        nr = -(-n // LANES)
        out.append(packed[r:r + nr].reshape(-1)[:n].reshape(a.shape))
        r += nr
    return out


def _halved(a):
    k, r, cols = a.shape
    return a.reshape(k, 2, r // 2, cols)


def _weight_view(k, buf):
    full = buf.reshape(N_CHIPS, buf.shape[2] * 2, buf.shape[3])
    return full if k.endswith("_gu") else full.reshape(-1, D_MODEL)


def _grad_stack(k, g):
    if not k.endswith("_gu"):
        g = g.reshape(N_CHIPS, g.shape[0] // N_CHIPS, D_MODEL)
    return _halved(g)


def _empty_like_hbm(shape, dtype):
    return pltpu.with_memory_space_constraint(lax.empty(shape, dtype), pltpu.HBM)


def kernel(x, norm_ffn1, w_ffn1_gu, w_ffn1_down, norm_mix, w_in, sinks, norm_out_sb, norm_out_swa, w_out, norm_ffn2, w_ffn2_gu, w_ffn2_down, rel_bias, norm_final, loss_target, m_norm_ffn1, m_w_ffn1_gu, m_w_ffn1_down, m_norm_mix, m_w_in, m_sinks, m_norm_out_sb, m_norm_out_swa, m_w_out, m_norm_ffn2, m_w_ffn2_gu, m_w_ffn2_down, m_rel_bias, m_norm_final, v_norm_ffn1, v_w_ffn1_gu, v_w_ffn1_down, v_norm_mix, v_w_in, v_sinks, v_norm_out_sb, v_norm_out_swa, v_w_out, v_norm_ffn2, v_w_ffn2_gu, v_w_ffn2_down, v_rel_bias, v_norm_final):
    big_w = dict(ffn1_gu=w_ffn1_gu, ffn1_down=w_ffn1_down, w_in=w_in, w_out=w_out, ffn2_gu=w_ffn2_gu, ffn2_down=w_ffn2_down)
    big_m = dict(ffn1_gu=m_w_ffn1_gu, ffn1_down=m_w_ffn1_down, w_in=m_w_in, w_out=m_w_out, ffn2_gu=m_w_ffn2_gu, ffn2_down=m_w_ffn2_down)
    big_v = dict(ffn1_gu=v_w_ffn1_gu, ffn1_down=v_w_ffn1_down, w_in=v_w_in, w_out=v_w_out, ffn2_gu=v_w_ffn2_gu, ffn2_down=v_w_ffn2_down)
    small = dict(norm_ffn1=norm_ffn1, norm_mix=norm_mix, sinks=sinks, norm_out_sb=norm_out_sb, norm_out_swa=norm_out_swa,
                 norm_ffn2=norm_ffn2, rel_bias=rel_bias, norm_final=norm_final)
    small_m = dict(norm_ffn1=m_norm_ffn1, norm_mix=m_norm_mix, sinks=m_sinks, norm_out_sb=m_norm_out_sb,
                   norm_out_swa=m_norm_out_swa, norm_ffn2=m_norm_ffn2, rel_bias=m_rel_bias, norm_final=m_norm_final)
    small_v = dict(norm_ffn1=v_norm_ffn1, norm_mix=v_norm_mix, sinks=v_sinks, norm_out_sb=v_norm_out_sb,
                   norm_out_swa=v_norm_out_swa, norm_ffn2=v_norm_ffn2, rel_bias=v_rel_bias, norm_final=v_norm_final)
    for dct in (big_w, big_m, big_v):
        dct["w_in"] = jnp.swapaxes(dct["w_in"], 1, 2)
    _PREVIOUS[0] = None
    _, _, c, me = _place()
    cm = jnp.stack([c, me]).astype(jnp.int32)
    buckets = jnp.asarray(_bucket_table())
    ffn1, mix_in, rest = ("ffn1_gu", "ffn1_down"), ("w_in",), ("w_out", "ffn2_gu", "ffn2_down")

    def place(l, keys):
        return [_place_own(big_w[k], l, cm[1:]) for k in keys]

    def views(keys, bufs):
        return {k: _weight_view(k, b) for k, b in zip(keys, bufs)}

    def gather_start(tag, bufs):
        return _exchange_start(f"gather{tag}_ici_start", _plan_gather_ici, bufs, 3 * len(bufs))

    def gather_pass(tag, flight):
        bufs = _exchange_wait(f"gather{tag}_ici_wait", _plan_gather_ici, flight[1], flight[0])
        return _exchange_start(f"gather{tag}_d2d_start", _plan_gather_d2d, bufs, 3 * len(bufs))

    def gather_done(tag, keys, flight):
        return views(keys, _exchange_wait(f"gather{tag}_d2d_wait", _plan_gather_d2d, flight[1], flight[0]))

    fly_ffn0 = gather_start("0a", place(0, ffn1))
    fly_in0 = gather_start("0b", place(0, mix_in))
    fly_rest0 = gather_start("0c", place(0, rest))
    bias = _bias_table(rel_bias, buckets)
    fly_ffn1 = gather_start("1a", place(1, ffn1))
    fly_rest1 = gather_start("1b", place(1, mix_in + rest))
    n1 = _norm_cast(x[0], _row(norm_ffn1[0]))
    w0 = gather_done("0a", ffn1, gather_pass("0a", fly_ffn0))

    s0 = _fwd_ffn1(x[0], n1, w0, small, 0)
    w0.update(gather_done("0b", mix_in, gather_pass("0b", fly_in0)))
    _fwd_proj_sb(s0, w0)
    fly_rest0 = gather_pass("0c", fly_rest0)
    _fwd_swa(s0, small, 0, bias)
    w0.update(gather_done("0c", rest, fly_rest0))
    h, n1 = _fwd_out_ffn2(s0, w0, small, 0, _row(norm_ffn1[1]))
    fly_ffn1 = gather_pass("1a", fly_ffn1)
    fly_rest1 = gather_pass("1b", fly_rest1)
    w1 = gather_done("1a", ffn1, fly_ffn1)
    s1 = _fwd_ffn1(h, n1, w1, small, 1)
    w1.update(gather_done("1b", mix_in + rest, fly_rest1))
    _fwd_proj_sb(s1, w1)
    _fwd_swa(s1, small, 1, bias)
    h, _ = _fwd_out_ffn2(s1, w1, small, 1, _row(norm_final))
    dh, dg_final, loss_row = _loss_head(h, _row(norm_final), loss_target[0])

    def landing(stacks, lead, dtype):
        return [_empty_like_hbm((lead,) + a.shape[2:], dtype) for a in stacks]

    def reduce_begin(tag, keys, gw):
        stacks = [_grad_stack(k, gw[k]) for k in keys]
        flight = _exchange_start(f"grad{tag}_sibling_start", _plan_grad_sibling,
                                 stacks + landing(stacks, N_CHIPS, F32), len(keys))
        return dict(tag=tag, keys=keys, stacks=stacks, flight=flight)

    def reduce_chips(st):
        n, (sems, bufs) = len(st["keys"]), st["flight"]
        bufs = _exchange_wait(f"grad{st['tag']}_sibling_wait", _plan_grad_sibling, bufs, sems)
        st["own"] = list(zip(bufs[:n], bufs[n:]))
        st["flight"] = _exchange_start(f"grad{st['tag']}_chips_start", _plan_grad_chips,
                                       [_chip_sum(g, z, cm) for g, z in st["own"]] + landing(st["stacks"], 3, BF16),
                                       3 * n)

    def reduce_halves(st):
        n, (sems, bufs) = len(st["keys"]), st["flight"]
        bufs = _exchange_wait(f"grad{st['tag']}_chips_wait", _plan_grad_chips, bufs, sems)
        halves = [_total_sum(g, x, z, cm) for (g, x), z in zip(st["own"], bufs[n:])]
        st["flight"] = _exchange_start(f"grad{st['tag']}_halves_start", _plan_grad_halves, halves, n)

    def reduce_end(st):
        sems, bufs = st["flight"]
        bufs = _exchange_wait(f"grad{st['tag']}_halves_wait", _plan_grad_halves, bufs, sems)
        return {k: b.reshape(big_w[k].shape[1:]) for k, b in zip(st["keys"], bufs)}

    def adamw(reduced, l, prev):
        return {k: _adamw_layer(big_w[k], g, big_m[k], big_v[k], l, None if prev is None else prev[k])
                for k, g in reduced.items()}

    gsm = [dict() for _ in range(DEPTH)]
    dbias = jnp.zeros((8, BLK, 2 * BLK), F32)
    dh, gw1, gs = _bwd_ffn(dh, s1, w1, small, 1, 2)
    gsm[1].update(gs)
    dh, gw, gs, dbias = _bwd_mix(dh, s1, w1, small, 1, bias, dbias)
    gw1.update(gw)
    gsm[1].update(gs)
    dh, gw, gs = _bwd_ffn(dh, s1, w1, small, 1, 1)
    gw1.update(gw)
    gsm[1].update(gs)

    red1 = reduce_begin("1", BIG, gw1)
    dh, gw0, gs = _bwd_ffn(dh, s0, w0, small, 0, 2)
    gsm[0].update(gs)
    reduce_chips(red1)
    dh, gw, gs, dbias = _bwd_mix(dh, s0, w0, small, 0, bias, dbias)
    gw0.update(gw)
    gsm[0].update(gs)
    red0a = reduce_begin("0a", ("ffn2_gu", "ffn2_down", "w_out", "w_in"), gw0)
    reduce_halves(red1)
    dgu = _bwd_ffn_dact(dh, s0, w0, 1)
    reduce_chips(red0a)
    dh, gw, gs = _bwd_ffn_rest(dh, dgu, s0, w0, small, 0, 1)
    gsm[0].update(gs)
    red0b = reduce_begin("0b", ffn1, gw)
    reduced1 = reduce_end(red1)
    stacks = adamw({k: reduced1[k] for k in ffn1}, 1, None)

    gsmall = {k: jnp.stack([gsm[l][k].reshape(-1) for l in range(DEPTH)]) for k in gsm[0]}
    gsmall["rel_bias"] = jnp.transpose(_bias_grad(dbias, buckets)[:, :N_BUCKETS])
    gsmall["norm_final"] = dg_final.reshape(-1)
    small_like = [small[k] for k in SMALL]
    pk = lambda dct: _pack([dct[k] for k in SMALL])
    red = _small_allreduce(_pack([gsmall[k] for k in SMALL] + [loss_row[0, :1]]))
    gs = _unpack(red, small_like + [loss_row[0, :1]])
    loss = gs[-1][0]
    gs = dict(zip(SMALL, gs[:-1]))

    reduce_chips(red0b)
    stacks.update(adamw({k: reduced1[k] for k in mix_in + rest}, 1, None))
    dlt, m2, v2 = _adamw_small(pk(small), pk(gs), pk(small_m), pk(small_v))
    reduce_halves(red0a)
    stacks.update(adamw(reduce_end(red0a), 0, stacks))
    reduce_halves(red0b)
    stacks.update(adamw(reduce_end(red0b), 0, stacks))

    out_g, out_d, out_m, out_v = {}, {}, {}, {}
    for k in BIG:
        out_g[k], out_d[k], out_m[k], out_v[k] = [jnp.swapaxes(a, 1, 2) if k == "w_in" else a for a in stacks[k]]
    for dst, packed in ((out_d, dlt), (out_m, m2), (out_v, v2)):
        dst.update(zip(SMALL, _unpack(packed, small_like)))
    out_g.update(gs)

    order = ("norm_ffn1", "ffn1_gu", "ffn1_down", "norm_mix", "w_in", "sinks", "norm_out_sb", "norm_out_swa", "w_out",
             "norm_ffn2", "ffn2_gu", "ffn2_down", "rel_bias", "norm_final")
    return (loss, dh.reshape(x.shape), *[out_g[k] for k in order], *[out_d[k] for k in order],
            *[out_m[k] for k in order], *[out_v[k] for k in order])
```

```python
import math

import numpy as np
import jax
import jax.numpy as jnp
from jax import lax
from jax.experimental import pallas as pl
from jax.experimental.pallas import tpu as pltpu

F32 = jnp.float32
BF16 = jnp.bfloat16

D_MODEL = 1024
DEPTH = 2
HEAD_DIM = 64
BLK = 128
N_BUCKETS = 32
MAX_DISTANCE = 128
D_FF = 2816
EPS = 1e-6
NEG_INF = -1e30
SB_W = 512
SWA_W = 512
KV_W = 128
IN_W = 2304
SCALE = HEAD_DIM ** -0.5
N_CHIPS = 4
FS = 2 * D_FF // N_CHIPS
LANES = 128
V7X_VMEM_LIMIT = 56 * 2 ** 20
TM = 512
SB_KT = 512
SWA_G = 4

ADAM_LR = 0.001
ADAM_B1 = 0.9
ADAM_B2 = 0.999
ADAM_EPS = 1e-08
ADAM_WD = 0.01
ADAM_STEP = 10

MESH = pl.DeviceIdType.MESH
ANY = pl.BlockSpec(memory_space=pl.ANY)
HBM = pl.BlockSpec(memory_space=pltpu.HBM)
SEM = pl.BlockSpec(memory_space=pltpu.SEMAPHORE)
EFFECT = pltpu.SideEffectType.DATAFLOW_SIDE_EFFECTING


def _params(n_grid):
    return pltpu.CompilerParams(dimension_semantics=("arbitrary",) * n_grid, vmem_limit_bytes=V7X_VMEM_LIMIT)


_PREVIOUS = [None]


def _call(body, *, name, in_specs, out_specs, out_shape, grid=(), num_scalar_prefetch=0, scratch_shapes=(),
          input_output_aliases=None, compiler_params=None, hbm_args=0):
    n_in = len(in_specs)

    def run(*args):
        dep = _PREVIOUS[0]
        if any(dep is a for a in args):
            dep = None
        args = [pltpu.with_memory_space_constraint(a, pltpu.HBM) if i < hbm_args else a for i, a in enumerate(args)]
        specs = list(in_specs) + ([ANY] if dep is not None else [])
        k = num_scalar_prefetch + n_in
        fn = body if dep is None else (lambda *refs: body(*refs[:k], *refs[k + 1:]))
        if num_scalar_prefetch:
            shape = dict(grid_spec=pltpu.PrefetchScalarGridSpec(
                num_scalar_prefetch=num_scalar_prefetch, grid=grid, in_specs=specs, out_specs=out_specs,
                scratch_shapes=scratch_shapes))
        else:
            shape = dict(grid=grid, in_specs=specs, out_specs=out_specs, scratch_shapes=scratch_shapes)
        out = pl.pallas_call(fn, name=name, out_shape=out_shape, input_output_aliases=input_output_aliases or {},
                             compiler_params=compiler_params, **shape)(*args, *([] if dep is None else [dep]))
        _PREVIOUS[0] = jax.tree.leaves(out)[-1]
        return out

    return run


def _dot(a, b):
    return jnp.dot(a, b, preferred_element_type=F32)


def _dot_nt(a, b):
    return lax.dot_general(a, b, (((1,), (1,)), ((), ())), preferred_element_type=F32)


def _dot_tn(a, b):
    return lax.dot_general(a, b, (((0,), (0,)), ((), ())), preferred_element_type=F32)


def _rms_fwd(x, g):
    r = lax.rsqrt(jnp.mean(x * x, axis=-1, keepdims=True) + EPS)
    xh = x * r
    return xh * g, xh, r


def _rms_bwd(dy, xh, r, g):
    u = dy * g
    dx = r * (u - xh * jnp.mean(u * xh, axis=-1, keepdims=True))
    dg = jnp.sum(dy * xh, axis=0, keepdims=True)
    return dx, dg


def _softplus(z):
    neg_abs = lax.bitcast_convert_type(lax.bitcast_convert_type(z, jnp.int32) | jnp.int32(-2 ** 31), F32)
    sp = jnp.maximum(z, 0.0) + jnp.log(1.0 + jnp.exp(neg_abs))
    return sp, z - sp


def _norm_cast(h, g):
    t, w = h.shape

    def body(h_ref, g_ref, n_ref):
        y, _, _ = _rms_fwd(h_ref[...], g_ref[...])
        n_ref[...] = y.astype(BF16)

    return _call(
        body, name="norm_cast", grid=(t // TM,),
        in_specs=[pl.BlockSpec((TM, w), lambda i: (i, 0)), pl.BlockSpec((1, w), lambda i: (0, 0))],
        out_specs=pl.BlockSpec((TM, w), lambda i: (i, 0)),
        out_shape=jax.ShapeDtypeStruct((t, w), BF16), compiler_params=_params(1))(h, g)


def _ffn_gu(n, wgu):
    t, d = n.shape

    def body(n_ref, wg_ref, wu_ref, gu_ref, act_ref):
        x = n_ref[...]
        g = _dot(x, wg_ref[...])
        u = _dot(x, wu_ref[...])
        sig = jax.nn.sigmoid(g)
        silu = g * sig
        gu_ref[0] = (u * (sig + silu * (1.0 - sig))).astype(BF16)
        gu_ref[1] = silu.astype(BF16)
        act_ref[...] = (silu * u).astype(BF16)

    return _call(
        body, name="ffn_gu", grid=(2, t // TM),
        in_specs=[pl.BlockSpec((TM, d), lambda j, i: (i, 0)),
                  pl.BlockSpec((None, d, FS), lambda j, i: (j, 0, 0)),
                  pl.BlockSpec((None, d, FS), lambda j, i: (j + 2, 0, 0))],
        out_specs=[pl.BlockSpec((2, TM, FS), lambda j, i: (0, i, j)), pl.BlockSpec((TM, FS), lambda j, i: (i, j))],
        out_shape=[jax.ShapeDtypeStruct((2, t, D_FF), BF16), jax.ShapeDtypeStruct((t, D_FF), BF16)],
        compiler_params=_params(2))(n, wgu, wgu)


def _down_res(act, wdn, h, g_next):
    t, f = act.shape
    d = h.shape[1]

    def body(a_ref, w_ref, h_ref, g_ref, o_ref, n_ref):
        out = h_ref[...] + 0.5 * _dot(a_ref[...], w_ref[...])
        o_ref[...] = out
        n_ref[...] = _rms_fwd(out, g_ref[...])[0].astype(BF16)

    row = pl.BlockSpec((TM, d), lambda i: (i, 0))
    return _call(
        body, name="down_res", grid=(t // TM,),
        in_specs=[pl.BlockSpec((TM, f), lambda i: (i, 0)), pl.BlockSpec((f, d), lambda i: (0, 0)), row,
                  pl.BlockSpec((1, d), lambda i: (0, 0))],
        out_specs=[row, row],
        out_shape=[jax.ShapeDtypeStruct((t, d), F32), jax.ShapeDtypeStruct((t, d), BF16)],
        compiler_params=_params(1))(act, wdn, h, g_next)


def _proj(n, w_in_t):
    t, d = n.shape
    w = w_in_t.shape[0]

    def body(n_ref, w_ref, o_ref):
        o_ref[...] = _dot_nt(n_ref[...], w_ref[...]).astype(BF16)

    return _call(
        body, name="proj", grid=(t // TM,),
        in_specs=[pl.BlockSpec((TM, d), lambda i: (i, 0)), pl.BlockSpec((w, d), lambda i: (0, 0))],
        out_specs=pl.BlockSpec((TM, w), lambda i: (i, 0)),
        out_shape=jax.ShapeDtypeStruct((t, w), BF16), compiler_params=_params(1))(n, w_in_t)


def _out_res(o_sb, o_sw, g_sb, g_sw, w_out, h, g_next):
    t, d = h.shape

    def body(a_ref, b_ref, ga_ref, gb_ref, w_ref, h_ref, g_ref, o_ref, mix_ref, n_ref):
        ya, _, _ = _rms_fwd(a_ref[...], ga_ref[...])
        yb, _, _ = _rms_fwd(b_ref[...], gb_ref[...])
        mixed = jnp.concatenate([ya.astype(BF16), yb.astype(BF16)], axis=1)
        mix_ref[...] = mixed
        out = h_ref[...] + _dot(mixed, w_ref[...])
        o_ref[...] = out
        n_ref[...] = _rms_fwd(out, g_ref[...])[0].astype(BF16)

    row = pl.BlockSpec((TM, d), lambda i: (i, 0))
    return _call(
        body, name="out_res", grid=(t // TM,),
        in_specs=[pl.BlockSpec((TM, SB_W), lambda i: (i, 0)), pl.BlockSpec((TM, SWA_W), lambda i: (i, 0)),
                  pl.BlockSpec((1, SB_W), lambda i: (0, 0)), pl.BlockSpec((1, SWA_W), lambda i: (0, 0)),
                  pl.BlockSpec((d, d), lambda i: (0, 0)), row, pl.BlockSpec((1, d), lambda i: (0, 0))],
        out_specs=[row, row, row],
        out_shape=[jax.ShapeDtypeStruct((t, d), F32), jax.ShapeDtypeStruct((t, d), BF16),
                   jax.ShapeDtypeStruct((t, d), BF16)],
        compiler_params=_params(1))(o_sb, o_sw, g_sb, g_sw, w_out, h, g_next)


def _loss_head(h, g, tgt):
    t, d = h.shape

    def body(h_ref, g_ref, t_ref, dh_ref, dg_ref, loss_ref):
        @pl.when(pl.program_id(0) == 0)
        def _():
            dg_ref[...] = jnp.zeros_like(dg_ref)
            loss_ref[...] = jnp.zeros_like(loss_ref)

        gg = g_ref[...]
        y, xh, r = _rms_fwd(h_ref[...], gg)
        err = y - t_ref[...]
        part = 0.5 * jnp.sum(jnp.sum(err * err, axis=1, keepdims=True) / d, axis=0, keepdims=True)
        loss_ref[...] += jnp.broadcast_to(part, loss_ref.shape)
        dx, dg = _rms_bwd(err / d, xh, r, gg)
        dh_ref[...] = dx
        dg_ref[...] += dg

    return _call(
        body, name="loss_head", grid=(t // TM,),
        in_specs=[pl.BlockSpec((TM, d), lambda i: (i, 0)), pl.BlockSpec((1, d), lambda i: (0, 0)),
                  pl.BlockSpec((TM, d), lambda i: (i, 0))],
        out_specs=[pl.BlockSpec((TM, d), lambda i: (i, 0)), pl.BlockSpec((1, d), lambda i: (0, 0)),
                   pl.BlockSpec((1, LANES), lambda i: (0, 0))],
        out_shape=[jax.ShapeDtypeStruct((t, d), F32), jax.ShapeDtypeStruct((1, d), F32),
                   jax.ShapeDtypeStruct((1, LANES), F32)],
        compiler_params=_params(1))(h, g, tgt)


def _ffn_dact(dh, wdn, gu):
    t, d = dh.shape
    tm = TM

    def body(dh_ref, w_ref, gu_ref, o_ref):
        da = 0.5 * _dot_nt(dh_ref[...].astype(BF16), w_ref[...])
        o_ref[0] = (da * gu_ref[0].astype(F32)).astype(BF16)
        o_ref[1] = (da * gu_ref[1].astype(F32)).astype(BF16)

    return _call(
        body, name="ffn_dact", grid=(2, t // tm),
        in_specs=[pl.BlockSpec((tm, d), lambda j, i: (i, 0)), pl.BlockSpec((FS, d), lambda j, i: (j, 0)),
                  pl.BlockSpec((2, tm, FS), lambda j, i: (0, i, j))],
        out_specs=pl.BlockSpec((2, tm, FS), lambda j, i: (0, i, j)),
        out_shape=jax.ShapeDtypeStruct((2, t, D_FF), BF16), compiler_params=_params(2))(dh, wdn, gu)


def _dn_norm_bwd(a, a_spec, w, w_spec, nk, dh, h_in, g, w_transposed=False, tm=TM):
    t, d = dh.shape
    mm = _dot if w_transposed else _dot_nt

    def body(a_ref, w_ref, dh_ref, h_ref, g_ref, o_ref, dg_ref, acc_ref):
        i, k = pl.program_id(0), pl.program_id(1)

        if nk > 1:
            @pl.when(k == 0)
            def _():
                acc_ref[...] = mm(a_ref[...], w_ref[...])

            @pl.when((k > 0) & (k < nk - 1))
            def _():
                acc_ref[...] += mm(a_ref[...], w_ref[...])

        @pl.when(k == nk - 1)
        def _():
            gg = g_ref[...]
            dg = jnp.zeros_like(gg)
            for rows in (slice(r, r + TM // 2) for r in range(0, tm, TM // 2)):
                dn = mm(a_ref[rows, :], w_ref[...])
                if nk > 1:
                    dn = dn + acc_ref[rows, :]
                _, xh, r = _rms_fwd(h_ref[rows, :], gg)
                dx, dg_rows = _rms_bwd(dn, xh, r, gg)
                o_ref[rows, :] = dh_ref[rows, :] + dx
                dg = dg + dg_rows

            @pl.when(i == 0)
            def _():
                dg_ref[...] = dg

            @pl.when(i > 0)
            def _():
                dg_ref[...] += dg

    row = pl.BlockSpec((tm, d), lambda i, k: (i, 0))
    return _call(
        body, name="dn_norm_bwd", grid=(t // tm, nk),
        in_specs=[a_spec, w_spec, row, row, pl.BlockSpec((1, d), lambda i, k: (0, 0))],
        out_specs=[row, pl.BlockSpec((1, d), lambda i, k: (0, 0))],
        out_shape=[jax.ShapeDtypeStruct((t, d), F32), jax.ShapeDtypeStruct((1, d), F32)],
        scratch_shapes=[pltpu.VMEM((tm, d), F32)], compiler_params=_params(2))(a, w, dh, h_in, g)


def _ffn_dn(dgu, wgu, dh, h_in, g):
    d = dh.shape[1]
    tm = 2 * TM
    return _dn_norm_bwd(
        dgu, pl.BlockSpec((None, tm, FS), lambda i, k: (k // 2, i, k % 2)),
        wgu, pl.BlockSpec((None, d, FS), lambda i, k: (k, 0, 0)), N_CHIPS, dh, h_in, g, tm=tm)


def _mix_dn(dproj, w_in_t, dh, h_in, g):
    d = dh.shape[1]
    w = dproj.shape[1]
    return _dn_norm_bwd(
        dproj, pl.BlockSpec((TM, w), lambda i, k: (i, 0)),
        w_in_t, pl.BlockSpec((w, d), lambda i, k: (0, 0)), 1, dh, h_in, g, w_transposed=True)


def _dmixed(dh, w_out, o_sb, o_sw, g_sb, g_sw):
    t, d = dh.shape

    def body(dh_ref, w_ref, a_ref, b_ref, ga_ref, gb_ref, o_ref, dga_ref, dgb_ref):
        i = pl.program_id(0)
        dm = _dot_nt(dh_ref[...].astype(BF16), w_ref[...])
        _, xa, ra = _rms_fwd(a_ref[...], ga_ref[...])
        _, xb, rb = _rms_fwd(b_ref[...], gb_ref[...])
        da, dga = _rms_bwd(dm[:, :SB_W], xa, ra, ga_ref[...])
        db, dgb = _rms_bwd(dm[:, SB_W:], xb, rb, gb_ref[...])
        o_ref[...] = jnp.concatenate([da.astype(BF16), db.astype(BF16)], axis=1)

        @pl.when(i == 0)
        def _():
            dga_ref[...] = dga
            dgb_ref[...] = dgb

        @pl.when(i > 0)
        def _():
            dga_ref[...] += dga
            dgb_ref[...] += dgb

    return _call(
        body, name="dmixed", grid=(t // TM,),
        in_specs=[pl.BlockSpec((TM, d), lambda i: (i, 0)), pl.BlockSpec((d, d), lambda i: (0, 0)),
                  pl.BlockSpec((TM, SB_W), lambda i: (i, 0)), pl.BlockSpec((TM, SWA_W), lambda i: (i, 0)),
                  pl.BlockSpec((1, SB_W), lambda i: (0, 0)), pl.BlockSpec((1, SWA_W), lambda i: (0, 0))],
        out_specs=[pl.BlockSpec((TM, d), lambda i: (i, 0)), pl.BlockSpec((1, SB_W), lambda i: (0, 0)),
                   pl.BlockSpec((1, SWA_W), lambda i: (0, 0))],
        out_shape=[jax.ShapeDtypeStruct((t, d), BF16), jax.ShapeDtypeStruct((1, SB_W), F32),
                   jax.ShapeDtypeStruct((1, SWA_W), F32)],
        compiler_params=_params(1))(dh, w_out, o_sb, o_sw, g_sb, g_sw)


def _wgrad(name, a, a_spec, b, b_spec, grid, out_shape, out_spec, scale):
    def body(a_ref, b_ref, o_ref):
        r = _dot_tn(a_ref[...], b_ref[...].astype(BF16))
        o_ref[...] = r if scale == 1.0 else scale * r

    return _call(
        body, name=name, grid=grid, in_specs=[a_spec, b_spec], out_specs=out_spec,
        out_shape=jax.ShapeDtypeStruct(out_shape, F32), compiler_params=_params(len(grid)))(a, b)


def _wgrad_gu(n, dgu):
    t, d = n.shape
    return _wgrad(
        "wgrad_gu", n, pl.BlockSpec((t, TM), lambda s, r: (0, r)),
        dgu, pl.BlockSpec((None, t, FS), lambda s, r: (s // 2, 0, s % 2)), (N_CHIPS, d // TM),
        (N_CHIPS, d, FS), pl.BlockSpec((None, TM, FS), lambda s, r: (s, r, 0)), 1.0)


def _wgrad_down(act, dh):
    t, d = dh.shape
    return _wgrad(
        "wgrad_down", act, pl.BlockSpec((t, FS), lambda s, r: (0, s)), dh, pl.BlockSpec((t, TM), lambda s, r: (0, r)),
        (2, d // TM), (D_FF, d), pl.BlockSpec((FS, TM), lambda s, r: (s, r)), 0.5)


def _wgrad_out(mixed, dh):
    t, d = dh.shape
    return _wgrad(
        "wgrad_out", mixed, pl.BlockSpec((t, TM), lambda s: (0, s)), dh, pl.BlockSpec((t, d), lambda s: (0, 0)),
        (d // TM,), (d, d), pl.BlockSpec((TM, d), lambda s: (s, 0)), 1.0)


def _wgrad_in(n, dproj):
    t, d = n.shape
    w = dproj.shape[1]
    tw = w // 3
    return _wgrad(
        "wgrad_in", dproj, pl.BlockSpec((t, tw), lambda s: (0, s)), n, pl.BlockSpec((t, d), lambda s: (0, 0)),
        (3,), (w, d), pl.BlockSpec((tw, d), lambda s: (s, 0)), 1.0)


def _tri(rel):
    row = lax.broadcasted_iota(jnp.int32, (BLK, BLK), 0)
    col = lax.broadcasted_iota(jnp.int32, (BLK, BLK), 1)
    m = rel(row, col).astype(BF16)
    return jnp.concatenate([m, m], axis=0)


def _scan_dot(x, tri2):
    hi = x.astype(BF16)
    lo = (x - hi.astype(F32)).astype(BF16)
    return _dot(jnp.concatenate([hi, lo], axis=1), tri2)


def _head_masks():
    lane = lax.broadcasted_iota(jnp.int32, (1, LANES), 1)
    return [lane < HEAD_DIM, lane >= HEAD_DIM]


SB_PAIRS = 2
SB_ROWS = 2 * SB_PAIRS * BLK


def _sb_dcol():
    dcol = lax.broadcasted_iota(jnp.int32, (BLK, SB_KT), 1) - lax.broadcasted_iota(jnp.int32, (BLK, SB_KT), 0)
    return jnp.concatenate([dcol] * (2 * SB_PAIRS), axis=0)


def _sb_stack(x, hm):
    return jnp.concatenate([jnp.where(m, x[:, p * LANES:(p + 1) * LANES], jnp.zeros((BLK, LANES), x.dtype))
                            for p in range(SB_PAIRS) for m in hm], axis=0)


def _sb_unstack(y, hm):
    return jnp.concatenate([jnp.where(hm[0], y[2 * p * BLK:(2 * p + 1) * BLK], y[(2 * p + 1) * BLK:(2 * p + 2) * BLK])
                            for p in range(SB_PAIRS)], axis=1)


def _sb_pairs():
    return [(slice(2 * p * BLK, (2 * p + 2) * BLK), slice(p * LANES, (p + 1) * LANES)) for p in range(SB_PAIRS)]


def _sb_fwd(proj):
    t = proj.shape[0]
    nq = t // BLK
    nb = SB_KT // BLK
    wide = SB_PAIRS * LANES

    def body(q_ref, k_ref, v_ref, o_ref, tot_ref):
        hm = _head_masks()
        dcol = _sb_dcol()
        pairs = _sb_pairs()
        after = _tri(lambda r, c: r > c)

        def tile(qh, kt, carry, acc, limit):
            ks = pl.ds(pl.multiple_of(kt * SB_KT, SB_KT), SB_KT)
            z = jnp.concatenate([_dot_nt(qh[rows], k_ref[ks, lanes]) for rows, lanes in pairs], axis=0)
            sp, zs = _softplus(z)
            valid = None if limit is None else dcol < limit
            spm = sp if valid is None else jnp.where(valid, sp, 0.0)
            sufs = [None] * nb
            for b in reversed(range(nb)):
                blk = spm[:, b * BLK:(b + 1) * BLK]
                sufs[b] = carry + _scan_dot(blk, after)
                carry = carry + jnp.sum(blk, axis=1, keepdims=True)
            w = jnp.exp(zs - jnp.concatenate(sufs, axis=1))
            if valid is not None:
                w = jnp.where(valid, w, 0.0)
            wb = w.astype(BF16)
            return carry, acc + jnp.concatenate([_dot(wb[rows], v_ref[ks, lanes]) for rows, lanes in pairs], axis=0)

        def qblock(qi, _):
            qs = pl.ds(pl.multiple_of(qi * BLK, BLK), BLK)
            kd = qi // nb
            limit = (qi - kd * nb) * BLK
            qh = _sb_stack(q_ref[qs, :] * SCALE, hm)
            c0 = tile(qh, kd, jnp.zeros((SB_ROWS, 1), F32), jnp.zeros((SB_ROWS, LANES), F32), limit)
            carry, acc = lax.fori_loop(0, kd, lambda n, c: tile(qh, kd - 1 - n, c[0], c[1], None), c0)
            o_ref[qs, :] = _sb_unstack(acc, hm)
            for h in range(2 * SB_PAIRS):
                tot_ref[h, qs, :] = jnp.broadcast_to(carry[h * BLK:(h + 1) * BLK], (BLK, LANES))
            return 0

        lax.fori_loop(0, nq, qblock, 0)

    col_blk = lambda off: pl.BlockSpec((t, wide), lambda g: (0, off + g))
    n_steps = SB_W // wide
    return _call(
        body, name="sb_fwd", grid=(n_steps,), in_specs=[col_blk(0), col_blk(n_steps), col_blk(2 * n_steps)],
        out_specs=[col_blk(0), pl.BlockSpec((2 * SB_PAIRS, t, LANES), lambda g: (g, 0, 0))],
        out_shape=[jax.ShapeDtypeStruct((t, SB_W), F32), jax.ShapeDtypeStruct((8, t, LANES), F32)],
        compiler_params=_params(1))(proj, proj, proj)


def _sb_bwd(proj, d_o, tot):
    t = proj.shape[0]
    nq = t // BLK
    nb = SB_KT // BLK
    wide = SB_PAIRS * LANES

    def body(q_ref, k_ref, v_ref, do_ref, tot_ref, dq_ref, dk_ref, dv_ref, dk_acc, dv_acc):
        hm = _head_masks()
        dcol = _sb_dcol()
        pairs = _sb_pairs()
        before = _tri(lambda r, c: r < c)
        upto = _tri(lambda r, c: r <= c)
        dk_acc[...] = jnp.zeros_like(dk_acc)
        dv_acc[...] = jnp.zeros_like(dv_acc)

        def tile(qh, doh, tt, kt, pre, ecum, dq, limit):
            ks = pl.ds(pl.multiple_of(kt * SB_KT, SB_KT), SB_KT)
            k = k_ref[ks, :]
            v = v_ref[ks, :]
            z = jnp.concatenate([_dot_nt(qh[rows], k[:, lanes]) for rows, lanes in pairs], axis=0)
            sp, zs = _softplus(z)
            valid = None if limit is None else dcol < limit
            spm = sp if valid is None else jnp.where(valid, sp, 0.0)
            pres = []
            for b in range(nb):
                blk = spm[:, b * BLK:(b + 1) * BLK]
                pres.append(pre + _scan_dot(blk, before))
                pre = pre + jnp.sum(blk, axis=1, keepdims=True)
            logw = z - (tt - jnp.concatenate(pres, axis=1))
            if valid is not None:
                logw = jnp.minimum(logw, 0.0)
            w = jnp.exp(logw)
            if valid is not None:
                w = jnp.where(valid, w, 0.0)
            e = w * jnp.concatenate([_dot_nt(doh[rows], v[:, lanes]) for rows, lanes in pairs], axis=0)
            incs = []
            for b in range(nb):
                blk = e[:, b * BLK:(b + 1) * BLK]
                incs.append(ecum + _scan_dot(blk, upto))
                ecum = ecum + jnp.sum(blk, axis=1, keepdims=True)
            dz = e - jnp.exp(zs) * jnp.concatenate(incs, axis=1)
            if valid is not None:
                dz = jnp.where(valid, dz, 0.0)
            dzb = dz.astype(BF16)
            wb = w.astype(BF16)
            for rows, lanes in pairs:
                dk_acc[ks, lanes] += _dot_tn(dzb[rows], qh[rows])
                dv_acc[ks, lanes] += _dot_tn(wb[rows], doh[rows])
            return pre, ecum, dq + jnp.concatenate([_dot(dzb[rows], k[:, lanes]) for rows, lanes in pairs], axis=0)

        def qblock(qi, _):
            qs = pl.ds(pl.multiple_of(qi * BLK, BLK), BLK)
            kd = qi // nb
            limit = (qi - kd * nb) * BLK
            qh = _sb_stack(q_ref[qs, :] * SCALE, hm)
            doh = _sb_stack(do_ref[qs, :], hm)
            tt = jnp.concatenate([tot_ref[h, qs, 0:1] for h in range(2 * SB_PAIRS)], axis=0)
            c0 = (jnp.zeros((SB_ROWS, 1), F32), jnp.zeros((SB_ROWS, 1), F32), jnp.zeros((SB_ROWS, LANES), F32))
            c = lax.fori_loop(0, kd, lambda kt, c: tile(qh, doh, tt, kt, c[0], c[1], c[2], None), c0)
            dq = tile(qh, doh, tt, kd, c[0], c[1], c[2], limit)[2]
            dq_ref[qs, :] = (_sb_unstack(dq, hm) * SCALE).astype(BF16)
            return 0

        lax.fori_loop(0, nq, qblock, 0)
        dk_ref[...] = dk_acc[...].astype(BF16)
        dv_ref[...] = dv_acc[...].astype(BF16)

    col_blk = lambda off: pl.BlockSpec((t, wide), lambda g: (0, off + g))
    n_steps = SB_W // wide
    out = jax.ShapeDtypeStruct((t, SB_W), BF16)
    return _call(
        body, name="sb_bwd", grid=(n_steps,),
        in_specs=[col_blk(0), col_blk(n_steps), col_blk(2 * n_steps), col_blk(0),
                  pl.BlockSpec((2 * SB_PAIRS, t, LANES), lambda g: (g, 0, 0))],
        out_specs=[col_blk(0), col_blk(0), col_blk(0)], out_shape=[out, out, out],
        scratch_shapes=[pltpu.VMEM((t, wide), F32), pltpu.VMEM((t, wide), F32)],
        compiler_params=_params(1))(proj, proj, proj, d_o, tot)


def _bucket_table():
    a = np.arange(BLK)[:, None]
    c = np.arange(2 * BLK)[None, :]
    dist = np.maximum(BLK + a - c, 0)
    max_exact = N_BUCKETS // 2
    dd = np.maximum(dist, 1).astype(np.float32)
    large = max_exact + (np.log(dd / max_exact) / math.log(MAX_DISTANCE / max_exact)
                         * (N_BUCKETS - max_exact)).astype(np.int32)
    large = np.minimum(large, N_BUCKETS - 1)
    return np.where(dist < max_exact, dist, large).astype(np.int32)


SWA_H = 8


def _swa_band_masks():
    row = lax.broadcasted_iota(jnp.int32, (SWA_H * BLK, 2 * BLK), 0) & (BLK - 1)
    col = lax.broadcasted_iota(jnp.int32, (SWA_H * BLK, 2 * BLK), 1)
    own = lax.broadcasted_iota(jnp.int32, (SWA_H * BLK, BLK), 1) <= (
        lax.broadcasted_iota(jnp.int32, (SWA_H * BLK, BLK), 0) & (BLK - 1))
    return (col > row) & ((col < BLK) | (col - BLK <= row)), own


def _swa_stack(ref, qs, hm, scale):
    parts = []
    for hq in range(SWA_H):
        kvh = hq // SWA_G
        x = ref[qs, (hq // 2) * LANES:(hq // 2 + 1) * LANES].astype(F32)
        if hq % 2 != kvh:
            x = pltpu.roll(x, HEAD_DIM, 1)
        parts.append(jnp.where(hm[kvh], x * scale, 0.0).astype(BF16))
    return jnp.concatenate(parts, axis=0)


def _swa_unstack(x8, hm):
    heads = []
    for hq in range(SWA_H):
        x = x8[hq * BLK:(hq + 1) * BLK]
        heads.append(pltpu.roll(x, HEAD_DIM, 1) if hq % 2 != hq // SWA_G else x)
    return [jnp.where(hm[0], heads[2 * p], heads[2 * p + 1]) for p in range(SWA_H // 2)]


def _swa_scores(q8, kb, bias_ref, mask, cols):
    bias8 = jnp.concatenate([bias_ref[hq, :, cols] for hq in range(SWA_H)], axis=0)
    return jnp.where(mask, _dot_nt(q8, kb) + bias8, NEG_INF)


def _swa_sinks(sink_ref):
    return jnp.concatenate([jnp.broadcast_to(sink_ref[hq:hq + 1, 0:1], (BLK, 1)) for hq in range(SWA_H)], axis=0)


def _swa_fwd(proj, bias, sinks_b):
    t = proj.shape[0]
    nq = t // BLK

    def body(q_ref, k_ref, v_ref, bias_ref, sink_ref, o_ref, lse_ref):
        hm = _head_masks()
        band, own = _swa_band_masks()

        def qblock(i, prev):
            qs = pl.ds(pl.multiple_of(i * BLK, BLK), BLK)
            if prev:
                ks, mask, cols = pl.ds(pl.multiple_of((i - 1) * BLK, BLK), 2 * BLK), band, slice(None)
            else:
                ks, mask, cols = qs, own, slice(BLK, None)
            q8 = _swa_stack(q_ref, qs, hm, SCALE)
            sink8 = _swa_sinks(sink_ref)
            s = _swa_scores(q8, k_ref[ks, :], bias_ref, mask, cols)
            m = jnp.maximum(jnp.max(s, axis=1, keepdims=True), sink8)
            p = jnp.exp(s - m)
            den = jnp.sum(p, axis=1, keepdims=True) + jnp.exp(sink8 - m)
            o8 = _dot((p * (1.0 / den)).astype(BF16), v_ref[ks, :])
            lse8 = m + jnp.log(den)
            for hq in range(SWA_H):
                lse_ref[hq, qs, :] = jnp.broadcast_to(lse8[hq * BLK:(hq + 1) * BLK], (BLK, LANES))
            for pp, o in enumerate(_swa_unstack(o8, hm)):
                o_ref[qs, pp * LANES:(pp + 1) * LANES] = o

        qblock(0, False)

        def step(i, _):
            qblock(i, True)
            return 0

        lax.fori_loop(1, nq, step, 0)

    return _call(
        body, name="swa_fwd", grid=(1,),
        in_specs=[pl.BlockSpec((t, SWA_W), lambda i: (0, 3)), pl.BlockSpec((t, KV_W), lambda i: (0, 16)),
                  pl.BlockSpec((t, KV_W), lambda i: (0, 17)), pl.BlockSpec((8, BLK, 2 * BLK), lambda i: (0, 0, 0)),
                  pl.BlockSpec((8, LANES), lambda i: (0, 0))],
        out_specs=[pl.BlockSpec((t, SWA_W), lambda i: (0, 0)), pl.BlockSpec((8, t, LANES), lambda i: (0, 0, 0))],
        out_shape=[jax.ShapeDtypeStruct((t, SWA_W), F32), jax.ShapeDtypeStruct((8, t, LANES), F32)],
        compiler_params=_params(1))(proj, proj, proj, bias, sinks_b)


def _swa_bwd(proj, d_o, lse, bias, sinks_b, dbias_in):
    t = proj.shape[0]
    nq = t // BLK

    def body(q_ref, k_ref, v_ref, do_ref, lse_ref, bias_ref, sink_ref, dbi_ref,
             dq_ref, dk_ref, dv_ref, dsink_ref, dbias_ref, dk_acc, dv_acc):
        hm = _head_masks()
        band, own = _swa_band_masks()
        dk_acc[...] = jnp.zeros_like(dk_acc)
        dv_acc[...] = jnp.zeros_like(dv_acc)
        dbias_ref[...] = dbi_ref[...]

        def qblock(i, prev, dsink8):
            qs = pl.ds(pl.multiple_of(i * BLK, BLK), BLK)
            if prev:
                ks, mask, cols = pl.ds(pl.multiple_of((i - 1) * BLK, BLK), 2 * BLK), band, slice(None)
            else:
                ks, mask, cols = qs, own, slice(BLK, None)
            q8 = _swa_stack(q_ref, qs, hm, SCALE)
            do8 = _swa_stack(do_ref, qs, hm, 1.0)
            sink8 = _swa_sinks(sink_ref)
            lse8 = jnp.concatenate([lse_ref[hq, qs, 0:1] for hq in range(SWA_H)], axis=0)
            kb = k_ref[ks, :]
            p = jnp.exp(_swa_scores(q8, kb, bias_ref, mask, cols) - lse8)
            dp = _dot_nt(do8, v_ref[ks, :])
            delta = jnp.sum(p * dp, axis=1, keepdims=True)
            ds = p * (dp - delta)
            for hq in range(SWA_H):
                dbias_ref[hq, :, cols] += ds[hq * BLK:(hq + 1) * BLK]
            dsb = ds.astype(BF16)
            dk_acc[ks, :] += _dot_tn(dsb, q8)
            dv_acc[ks, :] += _dot_tn(p.astype(BF16), do8)
            for pp, dq in enumerate(_swa_unstack(_dot(dsb, kb) * SCALE, hm)):
                dq_ref[qs, pp * LANES:(pp + 1) * LANES] = dq.astype(BF16)
            return dsink8 - jnp.exp(sink8 - lse8) * delta

        ds0 = qblock(0, False, jnp.zeros((SWA_H * BLK, 1), F32))
        ds8 = lax.fori_loop(1, nq, lambda i, c: qblock(i, True, c), ds0)
        for hq in range(SWA_H):
            dsink_ref[hq:hq + 1, :] = jnp.broadcast_to(
                jnp.sum(ds8[hq * BLK:(hq + 1) * BLK], axis=0, keepdims=True), (1, LANES))

        dk_ref[...] = dk_acc[...].astype(BF16)
        dv_ref[...] = dv_acc[...].astype(BF16)

    full3 = pl.BlockSpec((8, BLK, 2 * BLK), lambda i: (0, 0, 0))
    kv = jax.ShapeDtypeStruct((t, KV_W), BF16)
    return _call(
        body, name="swa_bwd", grid=(1,),
        in_specs=[pl.BlockSpec((t, SWA_W), lambda i: (0, 3)), pl.BlockSpec((t, KV_W), lambda i: (0, 16)),
                  pl.BlockSpec((t, KV_W), lambda i: (0, 17)), pl.BlockSpec((t, SWA_W), lambda i: (0, 1)),
                  pl.BlockSpec((8, t, LANES), lambda i: (0, 0, 0)), full3, pl.BlockSpec((8, LANES), lambda i: (0, 0)),
                  full3],
        out_specs=[pl.BlockSpec((t, SWA_W), lambda i: (0, 0)), pl.BlockSpec((t, KV_W), lambda i: (0, 0)),
                   pl.BlockSpec((t, KV_W), lambda i: (0, 0)), pl.BlockSpec((8, LANES), lambda i: (0, 0)), full3],
        out_shape=[jax.ShapeDtypeStruct((t, SWA_W), BF16), kv, kv, jax.ShapeDtypeStruct((8, LANES), F32),
                   jax.ShapeDtypeStruct((8, BLK, 2 * BLK), F32)],
        scratch_shapes=[pltpu.VMEM((t, KV_W), F32), pltpu.VMEM((t, KV_W), F32)],
        compiler_params=_params(1))(proj, proj, proj, d_o, lse, bias, sinks_b, dbias_in)


def _bias_table(rel_bias, buckets):
    def body(rb_ref, b_ref, o_ref):
        bk = b_ref[...]
        for h in range(8):
            acc = jnp.zeros((BLK, 2 * BLK), F32)
            for b in range(N_BUCKETS):
                acc = jnp.where(bk == b, rb_ref[b, h], acc)
            o_ref[h] = acc

    return _call(
        body, name="bias_table", grid=(1,),
        in_specs=[pl.BlockSpec(memory_space=pltpu.SMEM), pl.BlockSpec((BLK, 2 * BLK), lambda i: (0, 0))],
        out_specs=pl.BlockSpec((8, BLK, 2 * BLK), lambda i: (0, 0, 0)),
        out_shape=jax.ShapeDtypeStruct((8, BLK, 2 * BLK), F32), compiler_params=_params(1))(rel_bias, buckets)


def _bias_grad(dbias, buckets):
    def body(d_ref, b_ref, o_ref):
        lane = lax.broadcasted_iota(jnp.int32, (1, LANES), 1)
        bk = b_ref[...]
        for h in range(8):
            d = d_ref[h]
            acc = jnp.zeros((1, LANES), F32)
            for b in range(N_BUCKETS):
                s = jnp.sum(jnp.sum(jnp.where(bk == b, d, 0.0), axis=0, keepdims=True), axis=1, keepdims=True)
                acc = acc + jnp.where(lane == b, s, 0.0)
            o_ref[h:h + 1, :] = acc

    return _call(
        body, name="bias_grad", grid=(1,),
        in_specs=[pl.BlockSpec((8, BLK, 2 * BLK), lambda i: (0, 0, 0)), pl.BlockSpec((BLK, 2 * BLK), lambda i: (0, 0))],
        out_specs=pl.BlockSpec((8, LANES), lambda i: (0, 0)),
        out_shape=jax.ShapeDtypeStruct((8, LANES), F32), compiler_params=_params(1))(dbias, buckets)


def _row(a):
    return a.reshape(1, -1)


def _fwd_ffn1(h, n1, w, small, l):
    s = {"h0": h, "n1": n1}
    s["gu1"], s["act1"] = _ffn_gu(n1, w["ffn1_gu"])
    s["h1"], s["nm"] = _down_res(s["act1"], w["ffn1_down"], h, _row(small["norm_mix"][l]))
    return s


def _fwd_proj_sb(s, w):
    s["proj"] = _proj(s["nm"], w["w_in"])
    s["o_sb"], s["tot"] = _sb_fwd(s["proj"])


def _fwd_swa(s, small, l, bias):
    s["sinks_b"] = jnp.broadcast_to(small["sinks"][l][:, None], (8, LANES))
    s["o_sw"], s["lse"] = _swa_fwd(s["proj"], bias, s["sinks_b"])


def _fwd_out_ffn2(s, w, small, l, g_after):
    s["h2"], s["mixed"], s["n2"] = _out_res(
        s["o_sb"], s["o_sw"], _row(small["norm_out_sb"][l]), _row(small["norm_out_swa"][l]), w["w_out"], s["h1"],
        _row(small["norm_ffn2"][l]))
    s["gu2"], s["act2"] = _ffn_gu(s["n2"], w["ffn2_gu"])
    return _down_res(s["act2"], w["ffn2_down"], s["h2"], g_after)


def _bwd_ffn_dact(dh, s, w, which):
    return _ffn_dact(dh, w[f"ffn{which}_down"], s[f"gu{which}"])


def _bwd_ffn_rest(dh, dgu, s, w, small, l, which):
    h_in, norm = (s["h0"], "norm_ffn1") if which == 1 else (s["h2"], "norm_ffn2")
    g_down = _wgrad_down(s[f"act{which}"], dh)
    g_gu = _wgrad_gu(s[f"n{which}"], dgu)
    dh, dg = _ffn_dn(dgu, w[f"ffn{which}_gu"], dh, h_in, _row(small[norm][l]))
    return dh, {f"ffn{which}_down": g_down, f"ffn{which}_gu": g_gu}, {norm: dg}


def _bwd_ffn(dh, s, w, small, l, which):
    return _bwd_ffn_rest(dh, _bwd_ffn_dact(dh, s, w, which), s, w, small, l, which)


def _bwd_mix(dh, s, w, small, l, bias, dbias):
    g_out = _wgrad_out(s["mixed"], dh)
    d_o, dg_sb, dg_sw = _dmixed(dh, w["w_out"], s["o_sb"], s["o_sw"], _row(small["norm_out_sb"][l]),
                                _row(small["norm_out_swa"][l]))
    dq_sb, dk_sb, dv_sb = _sb_bwd(s["proj"], d_o, s["tot"])
    dq_sw, dk_sw, dv_sw, dsink, dbias = _swa_bwd(s["proj"], d_o, s["lse"], bias, s["sinks_b"], dbias)
    dproj = jnp.concatenate([dq_sb, dk_sb, dv_sb, dq_sw, dk_sw, dv_sw], axis=1)
    g_in = _wgrad_in(s["nm"], dproj)
    dh, dg_mix = _mix_dn(dproj, w["w_in"], dh, s["h1"], _row(small["norm_mix"][l]))
    gs = {"norm_out_sb": dg_sb, "norm_out_swa": dg_sw, "sinks": dsink[:, 0], "norm_mix": dg_mix}
    return dh, {"w_out": g_out, "w_in": g_in}, gs, dbias


def _place():
    x, y, c = lax.axis_index("x"), lax.axis_index("y"), lax.axis_index("c")
    return x, y, c, 2 * x + y


def _chip_core(k, c):
    return (k // 2, k % 2, c)


def _rows_per_block(rows, cols, copies):
    best = 16
    for tr in range(16, rows + 1, 16):
        if rows % tr == 0 and copies * tr * cols * 4 <= 6 * 2 ** 20:
            best = tr
    assert rows % best == 0
    return best


def _place_own(w, l, me1):
    _, rows, cols = w.shape
    tr = _rows_per_block(rows // 2, cols, 1)
    per_half = rows // 2 // tr

    def body(me_ref, w_ref, o_ref):
        o_ref[...] = w_ref[...].astype(BF16)

    return _call(
        body, name="place_own",
        num_scalar_prefetch=1, grid=(rows // tr,),
        in_specs=[pl.BlockSpec((None, tr, cols), lambda r, me: (l, r, 0))],
        out_specs=pl.BlockSpec((None, None, tr, cols), lambda r, me: (me[0], r // per_half, r % per_half, 0)),
        out_shape=jax.ShapeDtypeStruct((N_CHIPS, 2, rows // 2, cols), BF16), compiler_params=_params(1))(me1, w)


def _plan_gather_ici(bufs):
    _, _, c, me = _place()
    return [(b.at[me, c], b.at[me, c], b.at[(me + 3 - j) % N_CHIPS, c], _chip_core((me + 1 + j) % N_CHIPS, c))
            for b in bufs for j in range(3)]


def _plan_gather_d2d(bufs):
    x, y, c, me = _place()
    return [(b.at[(me + 3 - j) % N_CHIPS, c], b.at[(me + 3 - j) % N_CHIPS, c], b.at[(me + 3 - j) % N_CHIPS, 1 - c],
             (x, y, 1 - c)) for b in bufs for j in range(3)]


def _plan_grad_sibling(bufs):
    x, y, c, _ = _place()
    n = len(bufs) // 2
    return [(g.at[:, 1 - c], z, z, (x, y, 1 - c)) for g, z in zip(bufs[:n], bufs[n:])]


def _plan_grad_chips(bufs):
    _, _, c, me = _place()
    n = len(bufs) // 2
    return [(p.at[j], z.at[j], z.at[j], _chip_core((me + 1 + j) % N_CHIPS, c))
            for p, z in zip(bufs[:n], bufs[n:]) for j in range(3)]


def _plan_grad_halves(bufs):
    x, y, c, _ = _place()
    return [(b.at[c], b.at[c], b.at[1 - c], (x, y, 1 - c)) for b in bufs]


def _remote(src, dst, send_sem, recv_sem, to):
    return pltpu.make_async_remote_copy(src_ref=src, dst_ref=dst, send_sem=send_sem, recv_sem=recv_sem,
                                        device_id=to, device_id_type=MESH)


def _exchange_start(name, plan, bufs, n_copies):
    n = len(bufs)

    def body(*refs):
        ins = refs[:n]
        ssem, rsem = refs[n], refs[n + 1]
        token = refs[-1]
        for i, (src, dst, _, to) in enumerate(plan(ins)):
            _remote(src, dst, ssem.at[i], rsem.at[i], to).start()
        token[...] = jnp.zeros_like(token)

    out = _call(
        body, name=name,
        out_shape=(pltpu.SemaphoreType.DMA((n_copies,)), pltpu.SemaphoreType.DMA((n_copies,)),
                   *[pltpu.HBM(a.shape, a.dtype) for a in bufs], jax.ShapeDtypeStruct((8, LANES), F32)),
        in_specs=[HBM] * n, out_specs=(SEM, SEM, *[HBM] * n, pl.BlockSpec(memory_space=pltpu.VMEM)),
        input_output_aliases={t: 2 + t for t in range(n)}, hbm_args=n,
        compiler_params=pltpu.CompilerParams(has_side_effects=EFFECT),
    )(*bufs)
    return (out[0], out[1]), list(out[2:2 + n])


def _exchange_wait(name, plan, bufs, sems):
    n = len(bufs)

    def body(*refs):
        ins = refs[:n]
        ssem, rsem = refs[n], refs[n + 1]
        for i, (src, dst, land, to) in enumerate(plan(ins)):
            _remote(src, dst, ssem.at[i], rsem.at[i], to).wait_send()
            _remote(land, land, ssem.at[i], rsem.at[i], to).wait_recv()

    return list(_call(
        body, name=name, out_shape=[pltpu.HBM(a.shape, a.dtype) for a in bufs],
        in_specs=[HBM] * n + [SEM, SEM], out_specs=[HBM] * n,
        input_output_aliases={t: t for t in range(n)},
        compiler_params=pltpu.CompilerParams(has_side_effects=EFFECT),
    )(*bufs, sems[0], sems[1]))


def _exchange_pass(name, done, plan, bufs, sems, n_copies):
    n = len(bufs)

    def body(*refs):
        ins = refs[:n]
        old_s, old_r, ssem, rsem = refs[n], refs[n + 1], refs[n + 2], refs[n + 3]
        token = refs[-1]
        for i, (src, dst, land, to) in enumerate(done(ins)):
            _remote(src, dst, old_s.at[i], old_r.at[i], to).wait_send()
            _remote(land, land, old_s.at[i], old_r.at[i], to).wait_recv()
        for i, (src, dst, _, to) in enumerate(plan(ins)):
            _remote(src, dst, ssem.at[i], rsem.at[i], to).start()
        token[...] = jnp.zeros_like(token)

    out = _call(
        body, name=name,
        out_shape=(pltpu.SemaphoreType.DMA((n_copies,)), pltpu.SemaphoreType.DMA((n_copies,)),
                   *[pltpu.HBM(a.shape, a.dtype) for a in bufs], jax.ShapeDtypeStruct((8, LANES), F32)),
        in_specs=[HBM] * n + [SEM, SEM], out_specs=(SEM, SEM, *[HBM] * n, pl.BlockSpec(memory_space=pltpu.VMEM)),
        input_output_aliases={t: 2 + t for t in range(n)},
        compiler_params=pltpu.CompilerParams(has_side_effects=EFFECT),
    )(*bufs, sems[0], sems[1])
    return (out[0], out[1]), list(out[2:2 + n])


def _chip_sum(g, xbuf, cm):
    _, _, r2, cols = g.shape
    tr = _rows_per_block(r2, cols, 1)

    def body(cm_ref, g_ref, x_ref, o_ref):
        o_ref[...] = (g_ref[...] + x_ref[...]).astype(BF16)

    return _call(
        body, name="grad_chip_sum",
        num_scalar_prefetch=1, grid=(3, r2 // tr),
        in_specs=[pl.BlockSpec((None, None, tr, cols), lambda j, r, cm: ((cm[1] + 1 + j) % N_CHIPS, cm[0], r, 0)),
                  pl.BlockSpec((None, tr, cols), lambda j, r, cm: ((cm[1] + 1 + j) % N_CHIPS, r, 0))],
        out_specs=pl.BlockSpec((None, tr, cols), lambda j, r, cm: (j, r, 0)),
        out_shape=jax.ShapeDtypeStruct((3, r2, cols), BF16), compiler_params=_params(2))(cm, g, xbuf)


def _total_sum(g, xbuf, rbuf, cm):
    _, _, r2, cols = g.shape
    tr = _rows_per_block(r2, cols, 3)

    def body(cm_ref, g_ref, x_ref, r_ref, o_ref):
        acc = g_ref[...] + x_ref[...]
        for j in range(3):
            acc = acc + r_ref[j].astype(F32)
        o_ref[...] = acc

    return _call(
        body, name="grad_total_sum",
        num_scalar_prefetch=1, grid=(r2 // tr,),
        in_specs=[pl.BlockSpec((None, None, tr, cols), lambda r, cm: (cm[1], cm[0], r, 0)),
                  pl.BlockSpec((None, tr, cols), lambda r, cm: (cm[1], r, 0)),
                  pl.BlockSpec((3, tr, cols), lambda r, cm: (0, r, 0))],
        out_specs=pl.BlockSpec((None, tr, cols), lambda r, cm: (cm[0], r, 0)),
        out_shape=jax.ShapeDtypeStruct((2, r2, cols), F32), compiler_params=_params(1))(cm, g, xbuf, rbuf)


def _small_allreduce(v):
    rows = v.shape[0]
    n_dev = 2 * N_CHIPS

    def body(v_ref, o_ref, buf, ssem, rsem):
        x, y, c, _ = _place()
        me = 4 * x + 2 * y + c
        buf[me] = v_ref[...]

        def copy(d, slot, to):
            return _remote(v_ref, buf.at[slot], ssem.at[d - 1], rsem.at[d - 1], (to // 4, (to // 2) % 2, to % 2))

        cps = [copy(d, me, (me + d) % n_dev) for d in range(1, n_dev)]
        for cp in cps:
            cp.start()
        for d in range(1, n_dev):
            copy(d, (me + n_dev - d) % n_dev, me).wait_recv()
        for cp in cps:
            cp.wait_send()
        acc = buf[0]
        for i in range(1, n_dev):
            acc = acc + buf[i]
        o_ref[...] = acc

    vm = pl.BlockSpec(memory_space=pltpu.VMEM)
    return _call(
        body, name="small_allreduce", in_specs=[vm], out_specs=vm,
        out_shape=jax.ShapeDtypeStruct(v.shape, F32),
        scratch_shapes=[pltpu.VMEM((n_dev, rows, LANES), F32), pltpu.SemaphoreType.DMA((n_dev - 1,)),
                        pltpu.SemaphoreType.DMA((n_dev - 1,))],
        compiler_params=pltpu.CompilerParams(vmem_limit_bytes=V7X_VMEM_LIMIT))(v)


def _adamw_math(w, g, m, v):
    m2 = ADAM_B1 * m + (1.0 - ADAM_B1) * g
    v2 = ADAM_B2 * v + (1.0 - ADAM_B2) * (g * g)
    m_hat = m2 / (1.0 - ADAM_B1 ** ADAM_STEP)
    v_hat = v2 / (1.0 - ADAM_B2 ** ADAM_STEP)
    return -ADAM_LR * (m_hat / (jnp.sqrt(v_hat) + ADAM_EPS) + ADAM_WD * w), m2, v2


def _adamw_layer(w, g, m, v, l, prev):
    _, rows, cols = w.shape
    tr = rows
    for cand in range(8, rows + 1, 8):
        if rows % cand == 0 and cand * cols * 4 <= 2 ** 21:
            tr = cand

    def body(w_ref, g_ref, m_ref, v_ref, *outs):
        go_ref, d_ref, m2_ref, v2_ref = outs[-4:]
        g = g_ref[...]
        go_ref[...] = g
        d_ref[...], m2_ref[...], v2_ref[...] = _adamw_math(w_ref[...], g, m_ref[...], v_ref[...])

    stack = pl.BlockSpec((None, tr, cols), lambda i: (l, i, 0))
    ins, specs, alias = [w, g, m, v], [stack, pl.BlockSpec((tr, cols), lambda i: (i, 0)), stack, stack], {}
    if prev is not None:
        ins += list(prev)
        specs += [ANY] * 4
        alias = {4 + i: i for i in range(4)}
    return _call(
        body, name="adamw", grid=(rows // tr,), in_specs=specs, out_specs=[stack] * 4,
        out_shape=[jax.ShapeDtypeStruct(w.shape, F32)] * 4, input_output_aliases=alias,
        compiler_params=_params(1))(*ins)


def _adamw_small(w, g, m, v):
    def body(w_ref, g_ref, m_ref, v_ref, d_ref, m2_ref, v2_ref):
        d_ref[...], m2_ref[...], v2_ref[...] = _adamw_math(w_ref[...], g_ref[...], m_ref[...], v_ref[...])

    spec = pl.BlockSpec(w.shape, lambda i: (0, 0))
    return _call(
        body, name="adamw_small", grid=(1,), in_specs=[spec] * 4, out_specs=[spec] * 3,
        out_shape=[jax.ShapeDtypeStruct(w.shape, F32)] * 3, compiler_params=_params(1))(w, g, m, v)


SMALL = ("norm_ffn1", "norm_mix", "sinks", "norm_out_sb", "norm_out_swa", "norm_ffn2", "rel_bias", "norm_final")
BIG = ("ffn1_gu", "ffn1_down", "w_in", "w_out", "ffn2_gu", "ffn2_down")


def _pack(parts):
    flat, n = [], 0
    for a in parts:
        a = a.reshape(-1).astype(F32)
        gap = -a.shape[0] % LANES
        flat += [a] + ([jnp.zeros((gap,), F32)] if gap else [])
        n += a.shape[0] + gap
    tail = -(n // LANES) % 8 * LANES
    return jnp.concatenate(flat + ([jnp.zeros((tail,), F32)] if tail else [])).reshape(-1, LANES)


def _unpack(packed, like):
    out, r = [], 0
    for a in like:
        n = math.prod(a.shape)
        nr = -(-n // LANES)
        out.append(packed[r:r + nr].reshape(-1)[:n].reshape(a.shape))
        r += nr
    return out


def _halved(a):
    k, r, cols = a.shape
    return a.reshape(k, 2, r // 2, cols)


def _weight_view(k, buf):
    full = buf.reshape(N_CHIPS, buf.shape[2] * 2, buf.shape[3])
    return full if k.endswith("_gu") else full.reshape(-1, D_MODEL)


def _grad_stack(k, g):
    if not k.endswith("_gu"):
        g = g.reshape(N_CHIPS, g.shape[0] // N_CHIPS, D_MODEL)
    return _halved(g)


def _empty_like_hbm(shape, dtype):
    return pltpu.with_memory_space_constraint(lax.empty(shape, dtype), pltpu.HBM)


def kernel(x, norm_ffn1, w_ffn1_gu, w_ffn1_down, norm_mix, w_in, sinks, norm_out_sb, norm_out_swa, w_out, norm_ffn2, w_ffn2_gu, w_ffn2_down, rel_bias, norm_final, loss_target, m_norm_ffn1, m_w_ffn1_gu, m_w_ffn1_down, m_norm_mix, m_w_in, m_sinks, m_norm_out_sb, m_norm_out_swa, m_w_out, m_norm_ffn2, m_w_ffn2_gu, m_w_ffn2_down, m_rel_bias, m_norm_final, v_norm_ffn1, v_w_ffn1_gu, v_w_ffn1_down, v_norm_mix, v_w_in, v_sinks, v_norm_out_sb, v_norm_out_swa, v_w_out, v_norm_ffn2, v_w_ffn2_gu, v_w_ffn2_down, v_rel_bias, v_norm_final):
    big_w = dict(ffn1_gu=w_ffn1_gu, ffn1_down=w_ffn1_down, w_in=w_in, w_out=w_out, ffn2_gu=w_ffn2_gu, ffn2_down=w_ffn2_down)
    big_m = dict(ffn1_gu=m_w_ffn1_gu, ffn1_down=m_w_ffn1_down, w_in=m_w_in, w_out=m_w_out, ffn2_gu=m_w_ffn2_gu, ffn2_down=m_w_ffn2_down)
    big_v = dict(ffn1_gu=v_w_ffn1_gu, ffn1_down=v_w_ffn1_down, w_in=v_w_in, w_out=v_w_out, ffn2_gu=v_w_ffn2_gu, ffn2_down=v_w_ffn2_down)
    small = dict(norm_ffn1=norm_ffn1, norm_mix=norm_mix, sinks=sinks, norm_out_sb=norm_out_sb, norm_out_swa=norm_out_swa,
                 norm_ffn2=norm_ffn2, rel_bias=rel_bias, norm_final=norm_final)
    small_m = dict(norm_ffn1=m_norm_ffn1, norm_mix=m_norm_mix, sinks=m_sinks, norm_out_sb=m_norm_out_sb,
                   norm_out_swa=m_norm_out_swa, norm_ffn2=m_norm_ffn2, rel_bias=m_rel_bias, norm_final=m_norm_final)
    small_v = dict(norm_ffn1=v_norm_ffn1, norm_mix=v_norm_mix, sinks=v_sinks, norm_out_sb=v_norm_out_sb,
                   norm_out_swa=v_norm_out_swa, norm_ffn2=v_norm_ffn2, rel_bias=v_rel_bias, norm_final=v_norm_final)
    for dct in (big_w, big_m, big_v):
        dct["w_in"] = jnp.swapaxes(dct["w_in"], 1, 2)
    _PREVIOUS[0] = None
    _, _, c, me = _place()
    cm = jnp.stack([c, me]).astype(jnp.int32)
    buckets = jnp.asarray(_bucket_table())
    ffn1, mix_in, rest = ("ffn1_gu", "ffn1_down"), ("w_in",), ("w_out", "ffn2_gu", "ffn2_down")

    def place(l, keys):
        return [_place_own(big_w[k], l, cm[1:]) for k in keys]

    def views(keys, bufs):
        return {k: _weight_view(k, b) for k, b in zip(keys, bufs)}

    def gather_start(tag, bufs):
        return _exchange_start(f"gather{tag}_ici_start", _plan_gather_ici, bufs, 3 * len(bufs))

    def gather_pass(tag, flight):
        return _exchange_pass(f"gather{tag}_pass", _plan_gather_ici, _plan_gather_d2d, flight[1], flight[0],
                              3 * len(flight[1]))

    def gather_done(tag, keys, flight):
        return views(keys, _exchange_wait(f"gather{tag}_d2d_wait", _plan_gather_d2d, flight[1], flight[0]))

    fly_ffn0 = gather_start("0a", place(0, ffn1))
    fly_in0 = gather_start("0b", place(0, mix_in))
    fly_rest0 = gather_start("0c", place(0, rest))
    bias = _bias_table(rel_bias, buckets)
    fly_ffn1 = gather_start("1a", place(1, ffn1))
    fly_rest1 = gather_start("1b", place(1, mix_in + rest))
    n1 = _norm_cast(x[0], _row(norm_ffn1[0]))
    w0 = gather_done("0a", ffn1, gather_pass("0a", fly_ffn0))

    s0 = _fwd_ffn1(x[0], n1, w0, small, 0)
    w0.update(gather_done("0b", mix_in, gather_pass("0b", fly_in0)))
    _fwd_proj_sb(s0, w0)
    fly_rest0 = gather_pass("0c", fly_rest0)
    _fwd_swa(s0, small, 0, bias)
    w0.update(gather_done("0c", rest, fly_rest0))
    h, n1 = _fwd_out_ffn2(s0, w0, small, 0, _row(norm_ffn1[1]))
    fly_ffn1 = gather_pass("1a", fly_ffn1)
    fly_rest1 = gather_pass("1b", fly_rest1)
    w1 = gather_done("1a", ffn1, fly_ffn1)
    s1 = _fwd_ffn1(h, n1, w1, small, 1)
    w1.update(gather_done("1b", mix_in + rest, fly_rest1))
    _fwd_proj_sb(s1, w1)
    _fwd_swa(s1, small, 1, bias)
    h, _ = _fwd_out_ffn2(s1, w1, small, 1, _row(norm_final))
    dh, dg_final, loss_row = _loss_head(h, _row(norm_final), loss_target[0])

    def landing(stacks, lead, dtype):
        return [_empty_like_hbm((lead,) + a.shape[2:], dtype) for a in stacks]

    def reduce_begin(tag, keys, gw):
        stacks = [_grad_stack(k, gw[k]) for k in keys]
        flight = _exchange_start(f"grad{tag}_sibling_start", _plan_grad_sibling,
                                 stacks + landing(stacks, N_CHIPS, F32), len(keys))
        return dict(tag=tag, keys=keys, stacks=stacks, flight=flight)

    def reduce_chips(st):
        n, (sems, bufs) = len(st["keys"]), st["flight"]
        bufs = _exchange_wait(f"grad{st['tag']}_sibling_wait", _plan_grad_sibling, bufs, sems)
        st["own"] = list(zip(bufs[:n], bufs[n:]))
        st["flight"] = _exchange_start(f"grad{st['tag']}_chips_start", _plan_grad_chips,
                                       [_chip_sum(g, z, cm) for g, z in st["own"]] + landing(st["stacks"], 3, BF16),
                                       3 * n)

    def reduce_halves(st):
        n, (sems, bufs) = len(st["keys"]), st["flight"]
        bufs = _exchange_wait(f"grad{st['tag']}_chips_wait", _plan_grad_chips, bufs, sems)
        halves = [_total_sum(g, x, z, cm) for (g, x), z in zip(st["own"], bufs[n:])]
        st["flight"] = _exchange_start(f"grad{st['tag']}_halves_start", _plan_grad_halves, halves, n)

    def reduce_end(st):
        sems, bufs = st["flight"]
        bufs = _exchange_wait(f"grad{st['tag']}_halves_wait", _plan_grad_halves, bufs, sems)
        return {k: b.reshape(big_w[k].shape[1:]) for k, b in zip(st["keys"], bufs)}

    def adamw(reduced, l, prev):
        return {k: _adamw_layer(big_w[k], g, big_m[k], big_v[k], l, None if prev is None else prev[k])
                for k, g in reduced.items()}

    gsm = [dict() for _ in range(DEPTH)]
    dbias = jnp.zeros((8, BLK, 2 * BLK), F32)
    dh, gw1, gs = _bwd_ffn(dh, s1, w1, small, 1, 2)
    gsm[1].update(gs)
    dh, gw, gs, dbias = _bwd_mix(dh, s1, w1, small, 1, bias, dbias)
    gw1.update(gw)
    gsm[1].update(gs)
    dh, gw, gs = _bwd_ffn(dh, s1, w1, small, 1, 1)
    gw1.update(gw)
    gsm[1].update(gs)

    red1 = reduce_begin("1", BIG, gw1)
    dh, gw0, gs = _bwd_ffn(dh, s0, w0, small, 0, 2)
    gsm[0].update(gs)
    reduce_chips(red1)
    dh, gw, gs, dbias = _bwd_mix(dh, s0, w0, small, 0, bias, dbias)
    gw0.update(gw)
    gsm[0].update(gs)
    red0a = reduce_begin("0a", ("ffn2_gu", "ffn2_down", "w_out", "w_in"), gw0)
    reduce_halves(red1)
    dgu = _bwd_ffn_dact(dh, s0, w0, 1)
    reduce_chips(red0a)
    dh, gw, gs = _bwd_ffn_rest(dh, dgu, s0, w0, small, 0, 1)
    gsm[0].update(gs)
    red0b = reduce_begin("0b", ffn1, gw)
    reduced1 = reduce_end(red1)
    stacks = adamw({k: reduced1[k] for k in ffn1}, 1, None)

    gsmall = {k: jnp.stack([gsm[l][k].reshape(-1) for l in range(DEPTH)]) for k in gsm[0]}
    gsmall["rel_bias"] = jnp.transpose(_bias_grad(dbias, buckets)[:, :N_BUCKETS])
    gsmall["norm_final"] = dg_final.reshape(-1)
    small_like = [small[k] for k in SMALL]
    pk = lambda dct: _pack([dct[k] for k in SMALL])
    red = _small_allreduce(_pack([gsmall[k] for k in SMALL] + [loss_row[0, :1]]))
    gs = _unpack(red, small_like + [loss_row[0, :1]])
    loss = gs[-1][0]
    gs = dict(zip(SMALL, gs[:-1]))

    reduce_chips(red0b)
    stacks.update(adamw({k: reduced1[k] for k in mix_in + rest}, 1, None))
    dlt, m2, v2 = _adamw_small(pk(small), pk(gs), pk(small_m), pk(small_v))
    reduce_halves(red0a)
    stacks.update(adamw(reduce_end(red0a), 0, stacks))
    reduce_halves(red0b)
    stacks.update(adamw(reduce_end(red0b), 0, stacks))

    out_g, out_d, out_m, out_v = {}, {}, {}, {}
    for k in BIG:
        out_g[k], out_d[k], out_m[k], out_v[k] = [jnp.swapaxes(a, 1, 2) if k == "w_in" else a for a in stacks[k]]
    for dst, packed in ((out_d, dlt), (out_m, m2), (out_v, v2)):
        dst.update(zip(SMALL, _unpack(packed, small_like)))
    out_g.update(gs)

    order = ("norm_ffn1", "ffn1_gu", "ffn1_down", "norm_mix", "w_in", "sinks", "norm_out_sb", "norm_out_swa", "w_out",
             "norm_ffn2", "ffn2_gu", "ffn2_down", "rel_bias", "norm_final")
    return (loss, dh.reshape(x.shape), *[out_g[k] for k in order], *[out_d[k] for k in order],
            *[out_m[k] for k in order], *[out_v[k] for k in order])
```

```python
import math

import numpy as np
import jax
import jax.numpy as jnp
from jax import lax
from jax.experimental import pallas as pl
from jax.experimental.pallas import tpu as pltpu

F32 = jnp.float32
BF16 = jnp.bfloat16

D_MODEL = 1024
DEPTH = 2
HEAD_DIM = 64
BLK = 128
N_BUCKETS = 32
MAX_DISTANCE = 128
D_FF = 2816
EPS = 1e-6
NEG_INF = -1e30
SB_W = 512
SWA_W = 512
KV_W = 128
IN_W = 2304
SCALE = HEAD_DIM ** -0.5
N_CHIPS = 4
FS = 2 * D_FF // N_CHIPS
LANES = 128
V7X_VMEM_LIMIT = 56 * 2 ** 20
TM = 512
SB_KT = 512
SWA_G = 4

ADAM_LR = 0.001
ADAM_B1 = 0.9
ADAM_B2 = 0.999
ADAM_EPS = 1e-08
ADAM_WD = 0.01
ADAM_STEP = 10

MESH = pl.DeviceIdType.MESH
ANY = pl.BlockSpec(memory_space=pl.ANY)
HBM = pl.BlockSpec(memory_space=pltpu.HBM)
SEM = pl.BlockSpec(memory_space=pltpu.SEMAPHORE)
EFFECT = pltpu.SideEffectType.DATAFLOW_SIDE_EFFECTING


def _params(n_grid):
    return pltpu.CompilerParams(dimension_semantics=("arbitrary",) * n_grid, vmem_limit_bytes=V7X_VMEM_LIMIT)


_PREVIOUS = [None]


def _call(body, *, name, in_specs, out_specs, out_shape, grid=(), num_scalar_prefetch=0, scratch_shapes=(),
          input_output_aliases=None, compiler_params=None, hbm_args=0):
    n_in = len(in_specs)

    def run(*args):
        dep = _PREVIOUS[0]
        if any(dep is a for a in args):
            dep = None
        args = [pltpu.with_memory_space_constraint(a, pltpu.HBM) if i < hbm_args else a for i, a in enumerate(args)]
        specs = list(in_specs) + ([ANY] if dep is not None else [])
        k = num_scalar_prefetch + n_in
        fn = body if dep is None else (lambda *refs: body(*refs[:k], *refs[k + 1:]))
        if num_scalar_prefetch:
            shape = dict(grid_spec=pltpu.PrefetchScalarGridSpec(
                num_scalar_prefetch=num_scalar_prefetch, grid=grid, in_specs=specs, out_specs=out_specs,
                scratch_shapes=scratch_shapes))
        else:
            shape = dict(grid=grid, in_specs=specs, out_specs=out_specs, scratch_shapes=scratch_shapes)
        out = pl.pallas_call(fn, name=name, out_shape=out_shape, input_output_aliases=input_output_aliases or {},
                             compiler_params=compiler_params, **shape)(*args, *([] if dep is None else [dep]))
        _PREVIOUS[0] = jax.tree.leaves(out)[-1]
        return out

    return run


def _dot(a, b):
    return jnp.dot(a, b, preferred_element_type=F32)


def _dot_nt(a, b):
    return lax.dot_general(a, b, (((1,), (1,)), ((), ())), preferred_element_type=F32)


def _dot_tn(a, b):
    return lax.dot_general(a, b, (((0,), (0,)), ((), ())), preferred_element_type=F32)


def _rms_fwd(x, g):
    r = lax.rsqrt(jnp.mean(x * x, axis=-1, keepdims=True) + EPS)
    xh = x * r
    return xh * g, xh, r


def _rms_bwd(dy, xh, r, g):
    u = dy * g
    dx = r * (u - xh * jnp.mean(u * xh, axis=-1, keepdims=True))
    dg = jnp.sum(dy * xh, axis=0, keepdims=True)
    return dx, dg


def _softplus(z):
    neg_abs = lax.bitcast_convert_type(lax.bitcast_convert_type(z, jnp.int32) | jnp.int32(-2 ** 31), F32)
    sp = jnp.maximum(z, 0.0) + jnp.log(1.0 + jnp.exp(neg_abs))
    return sp, z - sp


def _norm_cast(h, g):
    t, w = h.shape

    def body(h_ref, g_ref, n_ref):
        y, _, _ = _rms_fwd(h_ref[...], g_ref[...])
        n_ref[...] = y.astype(BF16)

    return _call(
        body, name="norm_cast", grid=(t // TM,),
        in_specs=[pl.BlockSpec((TM, w), lambda i: (i, 0)), pl.BlockSpec((1, w), lambda i: (0, 0))],
        out_specs=pl.BlockSpec((TM, w), lambda i: (i, 0)),
        out_shape=jax.ShapeDtypeStruct((t, w), BF16), compiler_params=_params(1))(h, g)


def _ffn_gu(n, wgu):
    t, d = n.shape

    def body(n_ref, wg_ref, wu_ref, gu_ref, act_ref):
        x = n_ref[...]
        g = _dot(x, wg_ref[...])
        u = _dot(x, wu_ref[...])
        sig = jax.nn.sigmoid(g)
        silu = g * sig
        gu_ref[0] = (u * (sig + silu * (1.0 - sig))).astype(BF16)
        gu_ref[1] = silu.astype(BF16)
        act_ref[...] = (silu * u).astype(BF16)

    return _call(
        body, name="ffn_gu", grid=(2, t // TM),
        in_specs=[pl.BlockSpec((TM, d), lambda j, i: (i, 0)),
                  pl.BlockSpec((None, d, FS), lambda j, i: (j, 0, 0)),
                  pl.BlockSpec((None, d, FS), lambda j, i: (j + 2, 0, 0))],
        out_specs=[pl.BlockSpec((2, TM, FS), lambda j, i: (0, i, j)), pl.BlockSpec((TM, FS), lambda j, i: (i, j))],
        out_shape=[jax.ShapeDtypeStruct((2, t, D_FF), BF16), jax.ShapeDtypeStruct((t, D_FF), BF16)],
        compiler_params=_params(2))(n, wgu, wgu)


def _down_res(act, wdn, h, g_next):
    t, f = act.shape
    d = h.shape[1]

    def body(a_ref, w_ref, h_ref, g_ref, o_ref, n_ref):
        out = h_ref[...] + 0.5 * _dot(a_ref[...], w_ref[...])
        o_ref[...] = out
        n_ref[...] = _rms_fwd(out, g_ref[...])[0].astype(BF16)

    row = pl.BlockSpec((TM, d), lambda i: (i, 0))
    return _call(
        body, name="down_res", grid=(t // TM,),
        in_specs=[pl.BlockSpec((TM, f), lambda i: (i, 0)), pl.BlockSpec((f, d), lambda i: (0, 0)), row,
                  pl.BlockSpec((1, d), lambda i: (0, 0))],
        out_specs=[row, row],
        out_shape=[jax.ShapeDtypeStruct((t, d), F32), jax.ShapeDtypeStruct((t, d), BF16)],
        compiler_params=_params(1))(act, wdn, h, g_next)


def _proj(n, w_in_t):
    t, d = n.shape
    w = w_in_t.shape[0]

    def body(n_ref, w_ref, o_ref):
        o_ref[...] = _dot_nt(n_ref[...], w_ref[...]).astype(BF16)

    return _call(
        body, name="proj", grid=(t // TM,),
        in_specs=[pl.BlockSpec((TM, d), lambda i: (i, 0)), pl.BlockSpec((w, d), lambda i: (0, 0))],
        out_specs=pl.BlockSpec((TM, w), lambda i: (i, 0)),
        out_shape=jax.ShapeDtypeStruct((t, w), BF16), compiler_params=_params(1))(n, w_in_t)


def _out_res(o_sb, o_sw, g_sb, g_sw, w_out, h, g_next):
    t, d = h.shape

    def body(a_ref, b_ref, ga_ref, gb_ref, w_ref, h_ref, g_ref, o_ref, mix_ref, n_ref):
        ya, _, _ = _rms_fwd(a_ref[...], ga_ref[...])
        yb, _, _ = _rms_fwd(b_ref[...], gb_ref[...])
        mixed = jnp.concatenate([ya.astype(BF16), yb.astype(BF16)], axis=1)
        mix_ref[...] = mixed
        out = h_ref[...] + _dot(mixed, w_ref[...])
        o_ref[...] = out
        n_ref[...] = _rms_fwd(out, g_ref[...])[0].astype(BF16)

    row = pl.BlockSpec((TM, d), lambda i: (i, 0))
    return _call(
        body, name="out_res", grid=(t // TM,),
        in_specs=[pl.BlockSpec((TM, SB_W), lambda i: (i, 0)), pl.BlockSpec((TM, SWA_W), lambda i: (i, 0)),
                  pl.BlockSpec((1, SB_W), lambda i: (0, 0)), pl.BlockSpec((1, SWA_W), lambda i: (0, 0)),
                  pl.BlockSpec((d, d), lambda i: (0, 0)), row, pl.BlockSpec((1, d), lambda i: (0, 0))],
        out_specs=[row, row, row],
        out_shape=[jax.ShapeDtypeStruct((t, d), F32), jax.ShapeDtypeStruct((t, d), BF16),
                   jax.ShapeDtypeStruct((t, d), BF16)],
        compiler_params=_params(1))(o_sb, o_sw, g_sb, g_sw, w_out, h, g_next)


def _loss_head(h, g, tgt):
    t, d = h.shape

    def body(h_ref, g_ref, t_ref, dh_ref, dg_ref, loss_ref):
        @pl.when(pl.program_id(0) == 0)
        def _():
            dg_ref[...] = jnp.zeros_like(dg_ref)
            loss_ref[...] = jnp.zeros_like(loss_ref)

        gg = g_ref[...]
        y, xh, r = _rms_fwd(h_ref[...], gg)
        err = y - t_ref[...]
        part = 0.5 * jnp.sum(jnp.sum(err * err, axis=1, keepdims=True) / d, axis=0, keepdims=True)
        loss_ref[...] += jnp.broadcast_to(part, loss_ref.shape)
        dx, dg = _rms_bwd(err / d, xh, r, gg)
        dh_ref[...] = dx
        dg_ref[...] += dg

    return _call(
        body, name="loss_head", grid=(t // TM,),
        in_specs=[pl.BlockSpec((TM, d), lambda i: (i, 0)), pl.BlockSpec((1, d), lambda i: (0, 0)),
                  pl.BlockSpec((TM, d), lambda i: (i, 0))],
        out_specs=[pl.BlockSpec((TM, d), lambda i: (i, 0)), pl.BlockSpec((1, d), lambda i: (0, 0)),
                   pl.BlockSpec((1, LANES), lambda i: (0, 0))],
        out_shape=[jax.ShapeDtypeStruct((t, d), F32), jax.ShapeDtypeStruct((1, d), F32),
                   jax.ShapeDtypeStruct((1, LANES), F32)],
        compiler_params=_params(1))(h, g, tgt)


def _ffn_dact(dh, wdn, gu):
    t, d = dh.shape
    tm = TM

    def body(dh_ref, w_ref, gu_ref, o_ref):
        da = 0.5 * _dot_nt(dh_ref[...].astype(BF16), w_ref[...])
        o_ref[0] = (da * gu_ref[0].astype(F32)).astype(BF16)
        o_ref[1] = (da * gu_ref[1].astype(F32)).astype(BF16)

    return _call(
        body, name="ffn_dact", grid=(2, t // tm),
        in_specs=[pl.BlockSpec((tm, d), lambda j, i: (i, 0)), pl.BlockSpec((FS, d), lambda j, i: (j, 0)),
                  pl.BlockSpec((2, tm, FS), lambda j, i: (0, i, j))],
        out_specs=pl.BlockSpec((2, tm, FS), lambda j, i: (0, i, j)),
        out_shape=jax.ShapeDtypeStruct((2, t, D_FF), BF16), compiler_params=_params(2))(dh, wdn, gu)


def _dn_norm_bwd(a, a_spec, w, w_spec, nk, dh, h_in, g, w_transposed=False, tm=TM):
    t, d = dh.shape
    mm = _dot if w_transposed else _dot_nt

    def body(a_ref, w_ref, dh_ref, h_ref, g_ref, o_ref, dg_ref, acc_ref):
        i, k = pl.program_id(0), pl.program_id(1)

        if nk > 1:
            @pl.when(k == 0)
            def _():
                acc_ref[...] = mm(a_ref[...], w_ref[...])

            @pl.when((k > 0) & (k < nk - 1))
            def _():
                acc_ref[...] += mm(a_ref[...], w_ref[...])

        @pl.when(k == nk - 1)
        def _():
            gg = g_ref[...]
            dg = jnp.zeros_like(gg)
            for rows in (slice(r, r + TM // 2) for r in range(0, tm, TM // 2)):
                dn = mm(a_ref[rows, :], w_ref[...])
                if nk > 1:
                    dn = dn + acc_ref[rows, :]
                _, xh, r = _rms_fwd(h_ref[rows, :], gg)
                dx, dg_rows = _rms_bwd(dn, xh, r, gg)
                o_ref[rows, :] = dh_ref[rows, :] + dx
                dg = dg + dg_rows

            @pl.when(i == 0)
            def _():
                dg_ref[...] = dg

            @pl.when(i > 0)
            def _():
                dg_ref[...] += dg

    row = pl.BlockSpec((tm, d), lambda i, k: (i, 0))
    return _call(
        body, name="dn_norm_bwd", grid=(t // tm, nk),
        in_specs=[a_spec, w_spec, row, row, pl.BlockSpec((1, d), lambda i, k: (0, 0))],
        out_specs=[row, pl.BlockSpec((1, d), lambda i, k: (0, 0))],
        out_shape=[jax.ShapeDtypeStruct((t, d), F32), jax.ShapeDtypeStruct((1, d), F32)],
        scratch_shapes=[pltpu.VMEM((tm, d), F32)], compiler_params=_params(2))(a, w, dh, h_in, g)


def _ffn_dn(dgu, wgu, dh, h_in, g):
    d = dh.shape[1]
    tm = 2 * TM
    return _dn_norm_bwd(
        dgu, pl.BlockSpec((None, tm, FS), lambda i, k: (k // 2, i, k % 2)),
        wgu, pl.BlockSpec((None, d, FS), lambda i, k: (k, 0, 0)), N_CHIPS, dh, h_in, g, tm=tm)


def _mix_dn(dproj, w_in_t, dh, h_in, g):
    d = dh.shape[1]
    w = dproj.shape[1]
    return _dn_norm_bwd(
        dproj, pl.BlockSpec((TM, w), lambda i, k: (i, 0)),
        w_in_t, pl.BlockSpec((w, d), lambda i, k: (0, 0)), 1, dh, h_in, g, w_transposed=True)


def _dmixed(dh, w_out, o_sb, o_sw, g_sb, g_sw):
    t, d = dh.shape

    def body(dh_ref, w_ref, a_ref, b_ref, ga_ref, gb_ref, o_ref, dga_ref, dgb_ref):
        i = pl.program_id(0)
        dm = _dot_nt(dh_ref[...].astype(BF16), w_ref[...])
        _, xa, ra = _rms_fwd(a_ref[...], ga_ref[...])
        _, xb, rb = _rms_fwd(b_ref[...], gb_ref[...])
        da, dga = _rms_bwd(dm[:, :SB_W], xa, ra, ga_ref[...])
        db, dgb = _rms_bwd(dm[:, SB_W:], xb, rb, gb_ref[...])
        o_ref[...] = jnp.concatenate([da.astype(BF16), db.astype(BF16)], axis=1)

        @pl.when(i == 0)
        def _():
            dga_ref[...] = dga
            dgb_ref[...] = dgb

        @pl.when(i > 0)
        def _():
            dga_ref[...] += dga
            dgb_ref[...] += dgb

    return _call(
        body, name="dmixed", grid=(t // TM,),
        in_specs=[pl.BlockSpec((TM, d), lambda i: (i, 0)), pl.BlockSpec((d, d), lambda i: (0, 0)),
                  pl.BlockSpec((TM, SB_W), lambda i: (i, 0)), pl.BlockSpec((TM, SWA_W), lambda i: (i, 0)),
                  pl.BlockSpec((1, SB_W), lambda i: (0, 0)), pl.BlockSpec((1, SWA_W), lambda i: (0, 0))],
        out_specs=[pl.BlockSpec((TM, d), lambda i: (i, 0)), pl.BlockSpec((1, SB_W), lambda i: (0, 0)),
                   pl.BlockSpec((1, SWA_W), lambda i: (0, 0))],
        out_shape=[jax.ShapeDtypeStruct((t, d), BF16), jax.ShapeDtypeStruct((1, SB_W), F32),
                   jax.ShapeDtypeStruct((1, SWA_W), F32)],
        compiler_params=_params(1))(dh, w_out, o_sb, o_sw, g_sb, g_sw)


def _wgrad(name, a, a_spec, b, b_spec, grid, out_shape, out_spec, scale):
    def body(a_ref, b_ref, o_ref):
        r = _dot_tn(a_ref[...], b_ref[...].astype(BF16))
        o_ref[...] = r if scale == 1.0 else scale * r

    return _call(
        body, name=name, grid=grid, in_specs=[a_spec, b_spec], out_specs=out_spec,
        out_shape=jax.ShapeDtypeStruct(out_shape, F32), compiler_params=_params(len(grid)))(a, b)


def _wgrad_gu(n, dgu):
    t, d = n.shape
    return _wgrad(
        "wgrad_gu", n, pl.BlockSpec((t, TM), lambda s, r: (0, r)),
        dgu, pl.BlockSpec((None, t, FS), lambda s, r: (s // 2, 0, s % 2)), (N_CHIPS, d // TM),
        (N_CHIPS, d, FS), pl.BlockSpec((None, TM, FS), lambda s, r: (s, r, 0)), 1.0)


def _wgrad_down(act, dh):
    t, d = dh.shape
    return _wgrad(
        "wgrad_down", act, pl.BlockSpec((t, FS), lambda s, r: (0, s)), dh, pl.BlockSpec((t, TM), lambda s, r: (0, r)),
        (2, d // TM), (D_FF, d), pl.BlockSpec((FS, TM), lambda s, r: (s, r)), 0.5)


def _wgrad_out(mixed, dh):
    t, d = dh.shape
    return _wgrad(
        "wgrad_out", mixed, pl.BlockSpec((t, TM), lambda s: (0, s)), dh, pl.BlockSpec((t, d), lambda s: (0, 0)),
        (d // TM,), (d, d), pl.BlockSpec((TM, d), lambda s: (s, 0)), 1.0)


def _wgrad_in(n, dproj):
    t, d = n.shape
    w = dproj.shape[1]
    tw = w // 3
    return _wgrad(
        "wgrad_in", dproj, pl.BlockSpec((t, tw), lambda s: (0, s)), n, pl.BlockSpec((t, d), lambda s: (0, 0)),
        (3,), (w, d), pl.BlockSpec((tw, d), lambda s: (s, 0)), 1.0)


def _tri(rel):
    row = lax.broadcasted_iota(jnp.int32, (BLK, BLK), 0)
    col = lax.broadcasted_iota(jnp.int32, (BLK, BLK), 1)
    m = rel(row, col).astype(BF16)
    return jnp.concatenate([m, m], axis=0)


def _scan_dot(x, tri2):
    hi = x.astype(BF16)
    lo = (x - hi.astype(F32)).astype(BF16)
    return _dot(jnp.concatenate([hi, lo], axis=1), tri2)


def _head_masks():
    lane = lax.broadcasted_iota(jnp.int32, (1, LANES), 1)
    return [lane < HEAD_DIM, lane >= HEAD_DIM]


SB_PAIRS = 2
SB_ROWS = 2 * SB_PAIRS * BLK


def _sb_causal():
    row = lax.broadcasted_iota(jnp.int32, (SB_ROWS, BLK), 0) & (BLK - 1)
    return lax.broadcasted_iota(jnp.int32, (SB_ROWS, BLK), 1) < row


def _sb_mask_last(x, causal):
    own = jnp.where(causal, x[:, -BLK:], 0.0)
    return own if x.shape[1] == BLK else jnp.concatenate([x[:, :-BLK], own], axis=1)


def _sb_stack(x, hm):
    return jnp.concatenate([jnp.where(m, x[:, p * LANES:(p + 1) * LANES], jnp.zeros((BLK, LANES), x.dtype))
                            for p in range(SB_PAIRS) for m in hm], axis=0)


def _sb_unstack(y, hm):
    return jnp.concatenate([jnp.where(hm[0], y[2 * p * BLK:(2 * p + 1) * BLK], y[(2 * p + 1) * BLK:(2 * p + 2) * BLK])
                            for p in range(SB_PAIRS)], axis=1)


def _sb_pairs():
    return [(slice(2 * p * BLK, (2 * p + 2) * BLK), slice(p * LANES, (p + 1) * LANES)) for p in range(SB_PAIRS)]


def _sb_fwd(proj):
    t = proj.shape[0]
    nb = SB_KT // BLK
    wide = SB_PAIRS * LANES

    def body(q_ref, k_ref, v_ref, o_ref, tot_ref):
        hm = _head_masks()
        causal = _sb_causal()
        pairs = _sb_pairs()
        after = _tri(lambda r, c: r > c)

        def tile(qh, start, n_blk, carry, acc, own):
            ks = pl.ds(pl.multiple_of(start, BLK), n_blk * BLK)
            z = jnp.concatenate([_dot_nt(qh[rows], k_ref[ks, lanes]) for rows, lanes in pairs], axis=0)
            sp, zs = _softplus(z)
            spm = _sb_mask_last(sp, causal) if own else sp
            sufs = [None] * n_blk
            for b in reversed(range(n_blk)):
                blk = spm[:, b * BLK:(b + 1) * BLK]
                sufs[b] = carry + _scan_dot(blk, after)
                carry = carry + jnp.sum(blk, axis=1, keepdims=True)
            w = jnp.exp(zs - jnp.concatenate(sufs, axis=1))
            wb = (_sb_mask_last(w, causal) if own else w).astype(BF16)
            return carry, acc + jnp.concatenate([_dot(wb[rows], v_ref[ks, lanes]) for rows, lanes in pairs], axis=0)

        def qblock(g, j):
            qs = pl.ds(pl.multiple_of(g * SB_KT + j * BLK, BLK), BLK)
            qh = _sb_stack(q_ref[qs, :] * SCALE, hm)
            c0 = tile(qh, g * SB_KT, j + 1, jnp.zeros((SB_ROWS, 1), F32), jnp.zeros((SB_ROWS, LANES), F32), True)
            carry, acc = lax.fori_loop(0, g, lambda n, c: tile(qh, (g - 1 - n) * SB_KT, nb, c[0], c[1], False), c0)
            o_ref[qs, :] = _sb_unstack(acc, hm)
            for h in range(2 * SB_PAIRS):
                tot_ref[h, qs, :] = jnp.broadcast_to(carry[h * BLK:(h + 1) * BLK], (BLK, LANES))

        def group(g, _):
            for j in range(nb):
                qblock(g, j)
            return 0

        lax.fori_loop(0, t // SB_KT, group, 0)

    col_blk = lambda off: pl.BlockSpec((t, wide), lambda g: (0, off + g))
    n_steps = SB_W // wide
    return _call(
        body, name="sb_fwd", grid=(n_steps,), in_specs=[col_blk(0), col_blk(n_steps), col_blk(2 * n_steps)],
        out_specs=[col_blk(0), pl.BlockSpec((2 * SB_PAIRS, t, LANES), lambda g: (g, 0, 0))],
        out_shape=[jax.ShapeDtypeStruct((t, SB_W), F32), jax.ShapeDtypeStruct((8, t, LANES), F32)],
        compiler_params=_params(1))(proj, proj, proj)


def _sb_bwd(proj, d_o, tot):
    t = proj.shape[0]
    nb = SB_KT // BLK
    wide = SB_PAIRS * LANES

    def body(q_ref, k_ref, v_ref, do_ref, tot_ref, dq_ref, dk_ref, dv_ref, dk_acc, dv_acc):
        hm = _head_masks()
        causal = _sb_causal()
        pairs = _sb_pairs()
        before = _tri(lambda r, c: r < c)
        upto = _tri(lambda r, c: r <= c)
        dk_acc[...] = jnp.zeros_like(dk_acc)
        dv_acc[...] = jnp.zeros_like(dv_acc)

        def tile(qh, doh, tt, start, n_blk, pre, ecum, dq, own):
            ks = pl.ds(pl.multiple_of(start, BLK), n_blk * BLK)
            k = k_ref[ks, :]
            v = v_ref[ks, :]
            z = jnp.concatenate([_dot_nt(qh[rows], k[:, lanes]) for rows, lanes in pairs], axis=0)
            sp, zs = _softplus(z)
            spm = _sb_mask_last(sp, causal) if own else sp
            pres = []
            for b in range(n_blk):
                blk = spm[:, b * BLK:(b + 1) * BLK]
                pres.append(pre + _scan_dot(blk, before))
                pre = pre + jnp.sum(blk, axis=1, keepdims=True)
            logw = z - (tt - jnp.concatenate(pres, axis=1))
            if own:
                logw = jnp.minimum(logw, 0.0)
            w = jnp.exp(logw)
            if own:
                w = _sb_mask_last(w, causal)
            e = w * jnp.concatenate([_dot_nt(doh[rows], v[:, lanes]) for rows, lanes in pairs], axis=0)
            incs = []
            for b in range(n_blk):
                blk = e[:, b * BLK:(b + 1) * BLK]
                incs.append(ecum + _scan_dot(blk, upto))
                ecum = ecum + jnp.sum(blk, axis=1, keepdims=True)
            dz = e - jnp.exp(zs) * jnp.concatenate(incs, axis=1)
            if own:
                dz = _sb_mask_last(dz, causal)
            dzb = dz.astype(BF16)
            wb = w.astype(BF16)
            for rows, lanes in pairs:
                dk_acc[ks, lanes] += _dot_tn(dzb[rows], qh[rows])
                dv_acc[ks, lanes] += _dot_tn(wb[rows], doh[rows])
            return pre, ecum, dq + jnp.concatenate([_dot(dzb[rows], k[:, lanes]) for rows, lanes in pairs], axis=0)

        def qblock(g, j):
            qs = pl.ds(pl.multiple_of(g * SB_KT + j * BLK, BLK), BLK)
            qh = _sb_stack(q_ref[qs, :] * SCALE, hm)
            doh = _sb_stack(do_ref[qs, :], hm)
            tt = jnp.concatenate([tot_ref[h, qs, 0:1] for h in range(2 * SB_PAIRS)], axis=0)
            c0 = (jnp.zeros((SB_ROWS, 1), F32), jnp.zeros((SB_ROWS, 1), F32), jnp.zeros((SB_ROWS, LANES), F32))
            c = lax.fori_loop(0, g, lambda kt, c: tile(qh, doh, tt, kt * SB_KT, nb, c[0], c[1], c[2], False), c0)
            dq = tile(qh, doh, tt, g * SB_KT, j + 1, c[0], c[1], c[2], True)[2]
            dq_ref[qs, :] = (_sb_unstack(dq, hm) * SCALE).astype(BF16)

        def group(g, _):
            for j in range(nb):
                qblock(g, j)
            return 0

        lax.fori_loop(0, t // SB_KT, group, 0)
        dk_ref[...] = dk_acc[...].astype(BF16)
        dv_ref[...] = dv_acc[...].astype(BF16)

    col_blk = lambda off: pl.BlockSpec((t, wide), lambda g: (0, off + g))
    n_steps = SB_W // wide
    out = jax.ShapeDtypeStruct((t, SB_W), BF16)
    return _call(
        body, name="sb_bwd", grid=(n_steps,),
        in_specs=[col_blk(0), col_blk(n_steps), col_blk(2 * n_steps), col_blk(0),
                  pl.BlockSpec((2 * SB_PAIRS, t, LANES), lambda g: (g, 0, 0))],
        out_specs=[col_blk(0), col_blk(0), col_blk(0)], out_shape=[out, out, out],
        scratch_shapes=[pltpu.VMEM((t, wide), F32), pltpu.VMEM((t, wide), F32)],
        compiler_params=_params(1))(proj, proj, proj, d_o, tot)


def _bucket_table():
    a = np.arange(BLK)[:, None]
    c = np.arange(2 * BLK)[None, :]
    dist = np.maximum(BLK + a - c, 0)
    max_exact = N_BUCKETS // 2
    dd = np.maximum(dist, 1).astype(np.float32)
    large = max_exact + (np.log(dd / max_exact) / math.log(MAX_DISTANCE / max_exact)
                         * (N_BUCKETS - max_exact)).astype(np.int32)
    large = np.minimum(large, N_BUCKETS - 1)
    return np.where(dist < max_exact, dist, large).astype(np.int32)


SWA_H = 8


def _swa_band_masks():
    row = lax.broadcasted_iota(jnp.int32, (SWA_H * BLK, 2 * BLK), 0) & (BLK - 1)
    col = lax.broadcasted_iota(jnp.int32, (SWA_H * BLK, 2 * BLK), 1)
    own = lax.broadcasted_iota(jnp.int32, (SWA_H * BLK, BLK), 1) <= (
        lax.broadcasted_iota(jnp.int32, (SWA_H * BLK, BLK), 0) & (BLK - 1))
    return (col > row) & ((col < BLK) | (col - BLK <= row)), own


def _swa_stack(ref, qs, hm, scale):
    parts = []
    for hq in range(SWA_H):
        kvh = hq // SWA_G
        x = ref[qs, (hq // 2) * LANES:(hq // 2 + 1) * LANES].astype(F32)
        if hq % 2 != kvh:
            x = pltpu.roll(x, HEAD_DIM, 1)
        parts.append(jnp.where(hm[kvh], x * scale, 0.0).astype(BF16))
    return jnp.concatenate(parts, axis=0)


def _swa_unstack(x8, hm):
    heads = []
    for hq in range(SWA_H):
        x = x8[hq * BLK:(hq + 1) * BLK]
        heads.append(pltpu.roll(x, HEAD_DIM, 1) if hq % 2 != hq // SWA_G else x)
    return [jnp.where(hm[0], heads[2 * p], heads[2 * p + 1]) for p in range(SWA_H // 2)]


def _swa_scores(q8, kb, bias_ref, mask, cols):
    bias8 = jnp.concatenate([bias_ref[hq, :, cols] for hq in range(SWA_H)], axis=0)
    return jnp.where(mask, _dot_nt(q8, kb) + bias8, NEG_INF)


def _swa_sinks(sink_ref):
    return jnp.concatenate([jnp.broadcast_to(sink_ref[hq:hq + 1, 0:1], (BLK, 1)) for hq in range(SWA_H)], axis=0)


def _swa_fwd(proj, bias, sinks_b):
    t = proj.shape[0]
    nq = t // BLK

    def body(q_ref, k_ref, v_ref, bias_ref, sink_ref, o_ref, lse_ref):
        hm = _head_masks()
        band, own = _swa_band_masks()

        def qblock(i, prev):
            qs = pl.ds(pl.multiple_of(i * BLK, BLK), BLK)
            if prev:
                ks, mask, cols = pl.ds(pl.multiple_of((i - 1) * BLK, BLK), 2 * BLK), band, slice(None)
            else:
                ks, mask, cols = qs, own, slice(BLK, None)
            q8 = _swa_stack(q_ref, qs, hm, SCALE)
            sink8 = _swa_sinks(sink_ref)
            s = _swa_scores(q8, k_ref[ks, :], bias_ref, mask, cols)
            m = jnp.maximum(jnp.max(s, axis=1, keepdims=True), sink8)
            p = jnp.exp(s - m)
            den = jnp.sum(p, axis=1, keepdims=True) + jnp.exp(sink8 - m)
            o8 = _dot((p * (1.0 / den)).astype(BF16), v_ref[ks, :])
            lse8 = m + jnp.log(den)
            for hq in range(SWA_H):
                lse_ref[hq, qs, :] = jnp.broadcast_to(lse8[hq * BLK:(hq + 1) * BLK], (BLK, LANES))
            for pp, o in enumerate(_swa_unstack(o8, hm)):
                o_ref[qs, pp * LANES:(pp + 1) * LANES] = o

        qblock(0, False)

        def step(i, _):
            qblock(i, True)
            return 0

        lax.fori_loop(1, nq, step, 0)

    return _call(
        body, name="swa_fwd", grid=(1,),
        in_specs=[pl.BlockSpec((t, SWA_W), lambda i: (0, 3)), pl.BlockSpec((t, KV_W), lambda i: (0, 16)),
                  pl.BlockSpec((t, KV_W), lambda i: (0, 17)), pl.BlockSpec((8, BLK, 2 * BLK), lambda i: (0, 0, 0)),
                  pl.BlockSpec((8, LANES), lambda i: (0, 0))],
        out_specs=[pl.BlockSpec((t, SWA_W), lambda i: (0, 0)), pl.BlockSpec((8, t, LANES), lambda i: (0, 0, 0))],
        out_shape=[jax.ShapeDtypeStruct((t, SWA_W), F32), jax.ShapeDtypeStruct((8, t, LANES), F32)],
        compiler_params=_params(1))(proj, proj, proj, bias, sinks_b)


def _swa_bwd(proj, d_o, lse, bias, sinks_b, dbias_in):
    t = proj.shape[0]
    nq = t // BLK

    def body(q_ref, k_ref, v_ref, do_ref, lse_ref, bias_ref, sink_ref, dbi_ref,
             dq_ref, dk_ref, dv_ref, dsink_ref, dbias_ref, dk_acc, dv_acc):
        hm = _head_masks()
        band, own = _swa_band_masks()
        dk_acc[...] = jnp.zeros_like(dk_acc)
        dv_acc[...] = jnp.zeros_like(dv_acc)
        dbias_ref[...] = dbi_ref[...]

        def qblock(i, prev, dsink8):
            qs = pl.ds(pl.multiple_of(i * BLK, BLK), BLK)
            if prev:
                ks, mask, cols = pl.ds(pl.multiple_of((i - 1) * BLK, BLK), 2 * BLK), band, slice(None)
            else:
                ks, mask, cols = qs, own, slice(BLK, None)
            q8 = _swa_stack(q_ref, qs, hm, SCALE)
            do8 = _swa_stack(do_ref, qs, hm, 1.0)
            sink8 = _swa_sinks(sink_ref)
            lse8 = jnp.concatenate([lse_ref[hq, qs, 0:1] for hq in range(SWA_H)], axis=0)
            kb = k_ref[ks, :]
            p = jnp.exp(_swa_scores(q8, kb, bias_ref, mask, cols) - lse8)
            dp = _dot_nt(do8, v_ref[ks, :])
            delta = jnp.sum(p * dp, axis=1, keepdims=True)
            ds = p * (dp - delta)
            for hq in range(SWA_H):
                dbias_ref[hq, :, cols] += ds[hq * BLK:(hq + 1) * BLK]
            dsb = ds.astype(BF16)
            dk_acc[ks, :] += _dot_tn(dsb, q8)
            dv_acc[ks, :] += _dot_tn(p.astype(BF16), do8)
            for pp, dq in enumerate(_swa_unstack(_dot(dsb, kb) * SCALE, hm)):
                dq_ref[qs, pp * LANES:(pp + 1) * LANES] = dq.astype(BF16)
            return dsink8 - jnp.exp(sink8 - lse8) * delta

        ds0 = qblock(0, False, jnp.zeros((SWA_H * BLK, 1), F32))
        ds8 = lax.fori_loop(1, nq, lambda i, c: qblock(i, True, c), ds0)
        for hq in range(SWA_H):
            dsink_ref[hq:hq + 1, :] = jnp.broadcast_to(
                jnp.sum(ds8[hq * BLK:(hq + 1) * BLK], axis=0, keepdims=True), (1, LANES))

        dk_ref[...] = dk_acc[...].astype(BF16)
        dv_ref[...] = dv_acc[...].astype(BF16)

    full3 = pl.BlockSpec((8, BLK, 2 * BLK), lambda i: (0, 0, 0))
    kv = jax.ShapeDtypeStruct((t, KV_W), BF16)
    return _call(
        body, name="swa_bwd", grid=(1,),
        in_specs=[pl.BlockSpec((t, SWA_W), lambda i: (0, 3)), pl.BlockSpec((t, KV_W), lambda i: (0, 16)),
                  pl.BlockSpec((t, KV_W), lambda i: (0, 17)), pl.BlockSpec((t, SWA_W), lambda i: (0, 1)),
                  pl.BlockSpec((8, t, LANES), lambda i: (0, 0, 0)), full3, pl.BlockSpec((8, LANES), lambda i: (0, 0)),
                  full3],
        out_specs=[pl.BlockSpec((t, SWA_W), lambda i: (0, 0)), pl.BlockSpec((t, KV_W), lambda i: (0, 0)),
                   pl.BlockSpec((t, KV_W), lambda i: (0, 0)), pl.BlockSpec((8, LANES), lambda i: (0, 0)), full3],
        out_shape=[jax.ShapeDtypeStruct((t, SWA_W), BF16), kv, kv, jax.ShapeDtypeStruct((8, LANES), F32),
                   jax.ShapeDtypeStruct((8, BLK, 2 * BLK), F32)],
        scratch_shapes=[pltpu.VMEM((t, KV_W), F32), pltpu.VMEM((t, KV_W), F32)],
        compiler_params=_params(1))(proj, proj, proj, d_o, lse, bias, sinks_b, dbias_in)


def _bias_table(rel_bias, buckets):
    def body(rb_ref, b_ref, o_ref):
        bk = b_ref[...]
        for h in range(8):
            acc = jnp.zeros((BLK, 2 * BLK), F32)
            for b in range(N_BUCKETS):
                acc = jnp.where(bk == b, rb_ref[b, h], acc)
            o_ref[h] = acc

    return _call(
        body, name="bias_table", grid=(1,),
        in_specs=[pl.BlockSpec(memory_space=pltpu.SMEM), pl.BlockSpec((BLK, 2 * BLK), lambda i: (0, 0))],
        out_specs=pl.BlockSpec((8, BLK, 2 * BLK), lambda i: (0, 0, 0)),
        out_shape=jax.ShapeDtypeStruct((8, BLK, 2 * BLK), F32), compiler_params=_params(1))(rel_bias, buckets)


def _bias_grad(dbias, buckets):
    def body(d_ref, b_ref, o_ref):
        lane = lax.broadcasted_iota(jnp.int32, (1, LANES), 1)
        bk = b_ref[...]
        for h in range(8):
            d = d_ref[h]
            acc = jnp.zeros((1, LANES), F32)
            for b in range(N_BUCKETS):
                s = jnp.sum(jnp.sum(jnp.where(bk == b, d, 0.0), axis=0, keepdims=True), axis=1, keepdims=True)
                acc = acc + jnp.where(lane == b, s, 0.0)
            o_ref[h:h + 1, :] = acc

    return _call(
        body, name="bias_grad", grid=(1,),
        in_specs=[pl.BlockSpec((8, BLK, 2 * BLK), lambda i: (0, 0, 0)), pl.BlockSpec((BLK, 2 * BLK), lambda i: (0, 0))],
        out_specs=pl.BlockSpec((8, LANES), lambda i: (0, 0)),
        out_shape=jax.ShapeDtypeStruct((8, LANES), F32), compiler_params=_params(1))(dbias, buckets)


def _row(a):
    return a.reshape(1, -1)


def _fwd_ffn1(h, n1, w, small, l):
    s = {"h0": h, "n1": n1}
    s["gu1"], s["act1"] = _ffn_gu(n1, w["ffn1_gu"])
    s["h1"], s["nm"] = _down_res(s["act1"], w["ffn1_down"], h, _row(small["norm_mix"][l]))
    return s


def _fwd_proj_sb(s, w):
    s["proj"] = _proj(s["nm"], w["w_in"])
    s["o_sb"], s["tot"] = _sb_fwd(s["proj"])


def _fwd_swa(s, small, l, bias):
    s["sinks_b"] = jnp.broadcast_to(small["sinks"][l][:, None], (8, LANES))
    s["o_sw"], s["lse"] = _swa_fwd(s["proj"], bias, s["sinks_b"])


def _fwd_out_ffn2(s, w, small, l, g_after):
    s["h2"], s["mixed"], s["n2"] = _out_res(
        s["o_sb"], s["o_sw"], _row(small["norm_out_sb"][l]), _row(small["norm_out_swa"][l]), w["w_out"], s["h1"],
        _row(small["norm_ffn2"][l]))
    s["gu2"], s["act2"] = _ffn_gu(s["n2"], w["ffn2_gu"])
    return _down_res(s["act2"], w["ffn2_down"], s["h2"], g_after)


def _bwd_ffn_dact(dh, s, w, which):
    return _ffn_dact(dh, w[f"ffn{which}_down"], s[f"gu{which}"])


def _bwd_ffn_rest(dh, dgu, s, w, small, l, which):
    h_in, norm = (s["h0"], "norm_ffn1") if which == 1 else (s["h2"], "norm_ffn2")
    g_down = _wgrad_down(s[f"act{which}"], dh)
    g_gu = _wgrad_gu(s[f"n{which}"], dgu)
    dh, dg = _ffn_dn(dgu, w[f"ffn{which}_gu"], dh, h_in, _row(small[norm][l]))
    return dh, {f"ffn{which}_down": g_down, f"ffn{which}_gu": g_gu}, {norm: dg}


def _bwd_ffn(dh, s, w, small, l, which):
    return _bwd_ffn_rest(dh, _bwd_ffn_dact(dh, s, w, which), s, w, small, l, which)


def _bwd_mix(dh, s, w, small, l, bias, dbias):
    g_out = _wgrad_out(s["mixed"], dh)
    d_o, dg_sb, dg_sw = _dmixed(dh, w["w_out"], s["o_sb"], s["o_sw"], _row(small["norm_out_sb"][l]),
                                _row(small["norm_out_swa"][l]))
    dq_sb, dk_sb, dv_sb = _sb_bwd(s["proj"], d_o, s["tot"])
    dq_sw, dk_sw, dv_sw, dsink, dbias = _swa_bwd(s["proj"], d_o, s["lse"], bias, s["sinks_b"], dbias)
    dproj = jnp.concatenate([dq_sb, dk_sb, dv_sb, dq_sw, dk_sw, dv_sw], axis=1)
    g_in = _wgrad_in(s["nm"], dproj)
    dh, dg_mix = _mix_dn(dproj, w["w_in"], dh, s["h1"], _row(small["norm_mix"][l]))
    gs = {"norm_out_sb": dg_sb, "norm_out_swa": dg_sw, "sinks": dsink[:, 0], "norm_mix": dg_mix}
    return dh, {"w_out": g_out, "w_in": g_in}, gs, dbias


def _place():
    x, y, c = lax.axis_index("x"), lax.axis_index("y"), lax.axis_index("c")
    return x, y, c, 2 * x + y


def _chip_core(k, c):
    return (k // 2, k % 2, c)


def _rows_per_block(rows, cols, copies):
    best = 16
    for tr in range(16, rows + 1, 16):
        if rows % tr == 0 and copies * tr * cols * 4 <= 6 * 2 ** 20:
            best = tr
    assert rows % best == 0
    return best


def _place_own(w, l, me1):
    _, rows, cols = w.shape
    tr = _rows_per_block(rows // 2, cols, 1)
    per_half = rows // 2 // tr

    def body(me_ref, w_ref, o_ref):
        o_ref[...] = w_ref[...].astype(BF16)

    return _call(
        body, name="place_own",
        num_scalar_prefetch=1, grid=(rows // tr,),
        in_specs=[pl.BlockSpec((None, tr, cols), lambda r, me: (l, r, 0))],
        out_specs=pl.BlockSpec((None, None, tr, cols), lambda r, me: (me[0], r // per_half, r % per_half, 0)),
        out_shape=jax.ShapeDtypeStruct((N_CHIPS, 2, rows // 2, cols), BF16), compiler_params=_params(1))(me1, w)


def _plan_gather_ici(bufs):
    _, _, c, me = _place()
    return [(b.at[me, c], b.at[me, c], b.at[(me + 3 - j) % N_CHIPS, c], _chip_core((me + 1 + j) % N_CHIPS, c))
            for b in bufs for j in range(3)]


def _plan_gather_d2d(bufs):
    x, y, c, me = _place()
    return [(b.at[(me + 3 - j) % N_CHIPS, c], b.at[(me + 3 - j) % N_CHIPS, c], b.at[(me + 3 - j) % N_CHIPS, 1 - c],
             (x, y, 1 - c)) for b in bufs for j in range(3)]


def _plan_grad_sibling(bufs):
    x, y, c, _ = _place()
    n = len(bufs) // 2
    return [(g.at[:, 1 - c], z, z, (x, y, 1 - c)) for g, z in zip(bufs[:n], bufs[n:])]


def _plan_grad_chips(bufs):
    _, _, c, me = _place()
    n = len(bufs) // 2
    return [(p.at[j], z.at[j], z.at[j], _chip_core((me + 1 + j) % N_CHIPS, c))
            for p, z in zip(bufs[:n], bufs[n:]) for j in range(3)]


def _plan_grad_halves(bufs):
    x, y, c, _ = _place()
    return [(b.at[c], b.at[c], b.at[1 - c], (x, y, 1 - c)) for b in bufs]


def _remote(src, dst, send_sem, recv_sem, to):
    return pltpu.make_async_remote_copy(src_ref=src, dst_ref=dst, send_sem=send_sem, recv_sem=recv_sem,
                                        device_id=to, device_id_type=MESH)


def _exchange_start(name, plan, bufs, n_copies):
    n = len(bufs)

    def body(*refs):
        ins = refs[:n]
        ssem, rsem = refs[n], refs[n + 1]
        token = refs[-1]
        for i, (src, dst, _, to) in enumerate(plan(ins)):
            _remote(src, dst, ssem.at[i], rsem.at[i], to).start()
        token[...] = jnp.zeros_like(token)

    out = _call(
        body, name=name,
        out_shape=(pltpu.SemaphoreType.DMA((n_copies,)), pltpu.SemaphoreType.DMA((n_copies,)),
                   *[pltpu.HBM(a.shape, a.dtype) for a in bufs], jax.ShapeDtypeStruct((8, LANES), F32)),
        in_specs=[HBM] * n, out_specs=(SEM, SEM, *[HBM] * n, pl.BlockSpec(memory_space=pltpu.VMEM)),
        input_output_aliases={t: 2 + t for t in range(n)}, hbm_args=n,
        compiler_params=pltpu.CompilerParams(has_side_effects=EFFECT),
    )(*bufs)
    return (out[0], out[1]), list(out[2:2 + n])


def _exchange_wait(name, plan, bufs, sems):
    n = len(bufs)

    def body(*refs):
        ins = refs[:n]
        ssem, rsem = refs[n], refs[n + 1]
        for i, (src, dst, land, to) in enumerate(plan(ins)):
            _remote(src, dst, ssem.at[i], rsem.at[i], to).wait_send()
            _remote(land, land, ssem.at[i], rsem.at[i], to).wait_recv()

    return list(_call(
        body, name=name, out_shape=[pltpu.HBM(a.shape, a.dtype) for a in bufs],
        in_specs=[HBM] * n + [SEM, SEM], out_specs=[HBM] * n,
        input_output_aliases={t: t for t in range(n)},
        compiler_params=pltpu.CompilerParams(has_side_effects=EFFECT),
    )(*bufs, sems[0], sems[1]))


def _exchange_pass(name, done, plan, bufs, sems, n_copies):
    n = len(bufs)

    def body(*refs):
        ins = refs[:n]
        old_s, old_r, ssem, rsem = refs[n], refs[n + 1], refs[n + 2], refs[n + 3]
        token = refs[-1]
        for i, (src, dst, land, to) in enumerate(done(ins)):
            _remote(src, dst, old_s.at[i], old_r.at[i], to).wait_send()
            _remote(land, land, old_s.at[i], old_r.at[i], to).wait_recv()
        for i, (src, dst, _, to) in enumerate(plan(ins)):
            _remote(src, dst, ssem.at[i], rsem.at[i], to).start()
        token[...] = jnp.zeros_like(token)

    out = _call(
        body, name=name,
        out_shape=(pltpu.SemaphoreType.DMA((n_copies,)), pltpu.SemaphoreType.DMA((n_copies,)),
                   *[pltpu.HBM(a.shape, a.dtype) for a in bufs], jax.ShapeDtypeStruct((8, LANES), F32)),
        in_specs=[HBM] * n + [SEM, SEM], out_specs=(SEM, SEM, *[HBM] * n, pl.BlockSpec(memory_space=pltpu.VMEM)),
        input_output_aliases={t: 2 + t for t in range(n)},
        compiler_params=pltpu.CompilerParams(has_side_effects=EFFECT),
    )(*bufs, sems[0], sems[1])
    return (out[0], out[1]), list(out[2:2 + n])


def _chip_sum(g, xbuf, cm):
    _, _, r2, cols = g.shape
    tr = _rows_per_block(r2, cols, 1)

    def body(cm_ref, g_ref, x_ref, o_ref):
        o_ref[...] = (g_ref[...] + x_ref[...]).astype(BF16)

    return _call(
        body, name="grad_chip_sum",
        num_scalar_prefetch=1, grid=(3, r2 // tr),
        in_specs=[pl.BlockSpec((None, None, tr, cols), lambda j, r, cm: ((cm[1] + 1 + j) % N_CHIPS, cm[0], r, 0)),
                  pl.BlockSpec((None, tr, cols), lambda j, r, cm: ((cm[1] + 1 + j) % N_CHIPS, r, 0))],
        out_specs=pl.BlockSpec((None, tr, cols), lambda j, r, cm: (j, r, 0)),
        out_shape=jax.ShapeDtypeStruct((3, r2, cols), BF16), compiler_params=_params(2))(cm, g, xbuf)


def _total_sum(g, xbuf, rbuf, cm):
    _, _, r2, cols = g.shape
    tr = _rows_per_block(r2, cols, 3)

    def body(cm_ref, g_ref, x_ref, r_ref, o_ref):
        acc = g_ref[...] + x_ref[...]
        for j in range(3):
            acc = acc + r_ref[j].astype(F32)
        o_ref[...] = acc

    return _call(
        body, name="grad_total_sum",
        num_scalar_prefetch=1, grid=(r2 // tr,),
        in_specs=[pl.BlockSpec((None, None, tr, cols), lambda r, cm: (cm[1], cm[0], r, 0)),
                  pl.BlockSpec((None, tr, cols), lambda r, cm: (cm[1], r, 0)),
                  pl.BlockSpec((3, tr, cols), lambda r, cm: (0, r, 0))],
        out_specs=pl.BlockSpec((None, tr, cols), lambda r, cm: (cm[0], r, 0)),
        out_shape=jax.ShapeDtypeStruct((2, r2, cols), F32), compiler_params=_params(1))(cm, g, xbuf, rbuf)


def _small_allreduce(v):
    rows = v.shape[0]
    n_dev = 2 * N_CHIPS

    def body(v_ref, o_ref, buf, ssem, rsem):
        x, y, c, _ = _place()
        me = 4 * x + 2 * y + c
        buf[me] = v_ref[...]

        def copy(d, slot, to):
            return _remote(v_ref, buf.at[slot], ssem.at[d - 1], rsem.at[d - 1], (to // 4, (to // 2) % 2, to % 2))

        cps = [copy(d, me, (me + d) % n_dev) for d in range(1, n_dev)]
        for cp in cps:
            cp.start()
        for d in range(1, n_dev):
            copy(d, (me + n_dev - d) % n_dev, me).wait_recv()
        for cp in cps:
            cp.wait_send()
        acc = buf[0]
        for i in range(1, n_dev):
            acc = acc + buf[i]
        o_ref[...] = acc

    vm = pl.BlockSpec(memory_space=pltpu.VMEM)
    return _call(
        body, name="small_allreduce", in_specs=[vm], out_specs=vm,
        out_shape=jax.ShapeDtypeStruct(v.shape, F32),
        scratch_shapes=[pltpu.VMEM((n_dev, rows, LANES), F32), pltpu.SemaphoreType.DMA((n_dev - 1,)),
                        pltpu.SemaphoreType.DMA((n_dev - 1,))],
        compiler_params=pltpu.CompilerParams(vmem_limit_bytes=V7X_VMEM_LIMIT))(v)


def _adamw_math(w, g, m, v):
    m2 = ADAM_B1 * m + (1.0 - ADAM_B1) * g
    v2 = ADAM_B2 * v + (1.0 - ADAM_B2) * (g * g)
    m_hat = m2 / (1.0 - ADAM_B1 ** ADAM_STEP)
    v_hat = v2 / (1.0 - ADAM_B2 ** ADAM_STEP)
    return -ADAM_LR * (m_hat / (jnp.sqrt(v_hat) + ADAM_EPS) + ADAM_WD * w), m2, v2


def _adamw_layer(w, g, m, v, l, prev):
    _, rows, cols = w.shape
    tr = rows
    for cand in range(8, rows + 1, 8):
        if rows % cand == 0 and cand * cols * 4 <= 2 ** 21:
            tr = cand

    def body(w_ref, g_ref, m_ref, v_ref, *outs):
        go_ref, d_ref, m2_ref, v2_ref = outs[-4:]
        g = g_ref[...]
        go_ref[...] = g
        d_ref[...], m2_ref[...], v2_ref[...] = _adamw_math(w_ref[...], g, m_ref[...], v_ref[...])

    stack = pl.BlockSpec((None, tr, cols), lambda i: (l, i, 0))
    ins, specs, alias = [w, g, m, v], [stack, pl.BlockSpec((tr, cols), lambda i: (i, 0)), stack, stack], {}
    if prev is not None:
        ins += list(prev)
        specs += [ANY] * 4
        alias = {4 + i: i for i in range(4)}
    return _call(
        body, name="adamw", grid=(rows // tr,), in_specs=specs, out_specs=[stack] * 4,
        out_shape=[jax.ShapeDtypeStruct(w.shape, F32)] * 4, input_output_aliases=alias,
        compiler_params=_params(1))(*ins)


def _adamw_small(w, g, m, v):
    def body(w_ref, g_ref, m_ref, v_ref, d_ref, m2_ref, v2_ref):
        d_ref[...], m2_ref[...], v2_ref[...] = _adamw_math(w_ref[...], g_ref[...], m_ref[...], v_ref[...])

    spec = pl.BlockSpec(w.shape, lambda i: (0, 0))
    return _call(
        body, name="adamw_small", grid=(1,), in_specs=[spec] * 4, out_specs=[spec] * 3,
        out_shape=[jax.ShapeDtypeStruct(w.shape, F32)] * 3, compiler_params=_params(1))(w, g, m, v)


SMALL = ("norm_ffn1", "norm_mix", "sinks", "norm_out_sb", "norm_out_swa", "norm_ffn2", "rel_bias", "norm_final")
BIG = ("ffn1_gu", "ffn1_down", "w_in", "w_out", "ffn2_gu", "ffn2_down")


def _pack(parts):
    flat, n = [], 0
    for a in parts:
        a = a.reshape(-1).astype(F32)
        gap = -a.shape[0] % LANES
        flat += [a] + ([jnp.zeros((gap,), F32)] if gap else [])
        n += a.shape[0] + gap
    tail = -(n // LANES) % 8 * LANES
    return jnp.concatenate(flat + ([jnp.zeros((tail,), F32)] if tail else [])).reshape(-1, LANES)


def _unpack(packed, like):
    out, r = [], 0
    for a in like:
        n = math.prod(a.shape)
        nr = -(-n // LANES)
        out.append(packed[r:r + nr].reshape(-1)[:n].reshape(a.shape))
        r += nr
    return out


def _halved(a):
    k, r, cols = a.shape
    return a.reshape(k, 2, r // 2, cols)


def _weight_view(k, buf):
    full = buf.reshape(N_CHIPS, buf.shape[2] * 2, buf.shape[3])
    return full if k.endswith("_gu") else full.reshape(-1, D_MODEL)


def _grad_stack(k, g):
    if not k.endswith("_gu"):
        g = g.reshape(N_CHIPS, g.shape[0] // N_CHIPS, D_MODEL)
    return _halved(g)


def _empty_like_hbm(shape, dtype):
    return pltpu.with_memory_space_constraint(lax.empty(shape, dtype), pltpu.HBM)


def kernel(x, norm_ffn1, w_ffn1_gu, w_ffn1_down, norm_mix, w_in, sinks, norm_out_sb, norm_out_swa, w_out, norm_ffn2, w_ffn2_gu, w_ffn2_down, rel_bias, norm_final, loss_target, m_norm_ffn1, m_w_ffn1_gu, m_w_ffn1_down, m_norm_mix, m_w_in, m_sinks, m_norm_out_sb, m_norm_out_swa, m_w_out, m_norm_ffn2, m_w_ffn2_gu, m_w_ffn2_down, m_rel_bias, m_norm_final, v_norm_ffn1, v_w_ffn1_gu, v_w_ffn1_down, v_norm_mix, v_w_in, v_sinks, v_norm_out_sb, v_norm_out_swa, v_w_out, v_norm_ffn2, v_w_ffn2_gu, v_w_ffn2_down, v_rel_bias, v_norm_final):
    big_w = dict(ffn1_gu=w_ffn1_gu, ffn1_down=w_ffn1_down, w_in=w_in, w_out=w_out, ffn2_gu=w_ffn2_gu, ffn2_down=w_ffn2_down)
    big_m = dict(ffn1_gu=m_w_ffn1_gu, ffn1_down=m_w_ffn1_down, w_in=m_w_in, w_out=m_w_out, ffn2_gu=m_w_ffn2_gu, ffn2_down=m_w_ffn2_down)
    big_v = dict(ffn1_gu=v_w_ffn1_gu, ffn1_down=v_w_ffn1_down, w_in=v_w_in, w_out=v_w_out, ffn2_gu=v_w_ffn2_gu, ffn2_down=v_w_ffn2_down)
    small = dict(norm_ffn1=norm_ffn1, norm_mix=norm_mix, sinks=sinks, norm_out_sb=norm_out_sb, norm_out_swa=norm_out_swa,
                 norm_ffn2=norm_ffn2, rel_bias=rel_bias, norm_final=norm_final)
    small_m = dict(norm_ffn1=m_norm_ffn1, norm_mix=m_norm_mix, sinks=m_sinks, norm_out_sb=m_norm_out_sb,
                   norm_out_swa=m_norm_out_swa, norm_ffn2=m_norm_ffn2, rel_bias=m_rel_bias, norm_final=m_norm_final)
    small_v = dict(norm_ffn1=v_norm_ffn1, norm_mix=v_norm_mix, sinks=v_sinks, norm_out_sb=v_norm_out_sb,
                   norm_out_swa=v_norm_out_swa, norm_ffn2=v_norm_ffn2, rel_bias=v_rel_bias, norm_final=v_norm_final)
    for dct in (big_w, big_m, big_v):
        dct["w_in"] = jnp.swapaxes(dct["w_in"], 1, 2)
    _PREVIOUS[0] = None
    _, _, c, me = _place()
    cm = jnp.stack([c, me]).astype(jnp.int32)
    buckets = jnp.asarray(_bucket_table())
    ffn1, mix_in, rest = ("ffn1_gu", "ffn1_down"), ("w_in",), ("w_out", "ffn2_gu", "ffn2_down")

    def place(l, keys):
        return [_place_own(big_w[k], l, cm[1:]) for k in keys]

    def views(keys, bufs):
        return {k: _weight_view(k, b) for k, b in zip(keys, bufs)}

    def gather_start(tag, bufs):
        return _exchange_start(f"gather{tag}_ici_start", _plan_gather_ici, bufs, 3 * len(bufs))

    def gather_pass(tag, flight):
        return _exchange_pass(f"gather{tag}_pass", _plan_gather_ici, _plan_gather_d2d, flight[1], flight[0],
                              3 * len(flight[1]))

    def gather_done(tag, keys, flight):
        return views(keys, _exchange_wait(f"gather{tag}_d2d_wait", _plan_gather_d2d, flight[1], flight[0]))

    fly_ffn0 = gather_start("0a", place(0, ffn1))
    fly_in0 = gather_start("0b", place(0, mix_in))
    fly_rest0 = gather_start("0c", place(0, rest))
    bias = _bias_table(rel_bias, buckets)
    fly_ffn1 = gather_start("1a", place(1, ffn1))
    fly_rest1 = gather_start("1b", place(1, mix_in + rest))
    n1 = _norm_cast(x[0], _row(norm_ffn1[0]))
    w0 = gather_done("0a", ffn1, gather_pass("0a", fly_ffn0))

    s0 = _fwd_ffn1(x[0], n1, w0, small, 0)
    w0.update(gather_done("0b", mix_in, gather_pass("0b", fly_in0)))
    _fwd_proj_sb(s0, w0)
    fly_rest0 = gather_pass("0c", fly_rest0)
    _fwd_swa(s0, small, 0, bias)
    w0.update(gather_done("0c", rest, fly_rest0))
    h, n1 = _fwd_out_ffn2(s0, w0, small, 0, _row(norm_ffn1[1]))
    fly_ffn1 = gather_pass("1a", fly_ffn1)
    fly_rest1 = gather_pass("1b", fly_rest1)
    w1 = gather_done("1a", ffn1, fly_ffn1)
    s1 = _fwd_ffn1(h, n1, w1, small, 1)
    w1.update(gather_done("1b", mix_in + rest, fly_rest1))
    _fwd_proj_sb(s1, w1)
    _fwd_swa(s1, small, 1, bias)
    h, _ = _fwd_out_ffn2(s1, w1, small, 1, _row(norm_final))
    dh, dg_final, loss_row = _loss_head(h, _row(norm_final), loss_target[0])

    def landing(stacks, lead, dtype):
        return [_empty_like_hbm((lead,) + a.shape[2:], dtype) for a in stacks]

    def reduce_begin(tag, keys, gw):
        stacks = [_grad_stack(k, gw[k]) for k in keys]
        flight = _exchange_start(f"grad{tag}_sibling_start", _plan_grad_sibling,
                                 stacks + landing(stacks, N_CHIPS, F32), len(keys))
        return dict(tag=tag, keys=keys, stacks=stacks, flight=flight)

    def reduce_chips(st):
        n, (sems, bufs) = len(st["keys"]), st["flight"]
        bufs = _exchange_wait(f"grad{st['tag']}_sibling_wait", _plan_grad_sibling, bufs, sems)
        st["own"] = list(zip(bufs[:n], bufs[n:]))
        st["flight"] = _exchange_start(f"grad{st['tag']}_chips_start", _plan_grad_chips,
                                       [_chip_sum(g, z, cm) for g, z in st["own"]] + landing(st["stacks"], 3, BF16),
                                       3 * n)

    def reduce_halves(st):
        n, (sems, bufs) = len(st["keys"]), st["flight"]
        bufs = _exchange_wait(f"grad{st['tag']}_chips_wait", _plan_grad_chips, bufs, sems)
        halves = [_total_sum(g, x, z, cm) for (g, x), z in zip(st["own"], bufs[n:])]
        st["flight"] = _exchange_start(f"grad{st['tag']}_halves_start", _plan_grad_halves, halves, n)

    def reduce_end(st):
        sems, bufs = st["flight"]
        bufs = _exchange_wait(f"grad{st['tag']}_halves_wait", _plan_grad_halves, bufs, sems)
        return {k: b.reshape(big_w[k].shape[1:]) for k, b in zip(st["keys"], bufs)}

    def adamw(reduced, l, prev):
        return {k: _adamw_layer(big_w[k], g, big_m[k], big_v[k], l, None if prev is None else prev[k])
                for k, g in reduced.items()}

    gsm = [dict() for _ in range(DEPTH)]
    dbias = jnp.zeros((8, BLK, 2 * BLK), F32)
    dh, gw1, gs = _bwd_ffn(dh, s1, w1, small, 1, 2)
    gsm[1].update(gs)
    dh, gw, gs, dbias = _bwd_mix(dh, s1, w1, small, 1, bias, dbias)
    gw1.update(gw)
    gsm[1].update(gs)
    dh, gw, gs = _bwd_ffn(dh, s1, w1, small, 1, 1)
    gw1.update(gw)
    gsm[1].update(gs)

    red1 = reduce_begin("1", BIG, gw1)
    dh, gw0, gs = _bwd_ffn(dh, s0, w0, small, 0, 2)
    gsm[0].update(gs)
    reduce_chips(red1)
    dh, gw, gs, dbias = _bwd_mix(dh, s0, w0, small, 0, bias, dbias)
    gw0.update(gw)
    gsm[0].update(gs)
    red0a = reduce_begin("0a", ("ffn2_gu", "ffn2_down", "w_out", "w_in"), gw0)
    reduce_halves(red1)
    dgu = _bwd_ffn_dact(dh, s0, w0, 1)
    reduce_chips(red0a)
    dh, gw, gs = _bwd_ffn_rest(dh, dgu, s0, w0, small, 0, 1)
    gsm[0].update(gs)
    red0b = reduce_begin("0b", ffn1, gw)
    reduced1 = reduce_end(red1)
    stacks = adamw({k: reduced1[k] for k in ffn1}, 1, None)

    gsmall = {k: jnp.stack([gsm[l][k].reshape(-1) for l in range(DEPTH)]) for k in gsm[0]}
    gsmall["rel_bias"] = jnp.transpose(_bias_grad(dbias, buckets)[:, :N_BUCKETS])
    gsmall["norm_final"] = dg_final.reshape(-1)
    small_like = [small[k] for k in SMALL]
    pk = lambda dct: _pack([dct[k] for k in SMALL])
    red = _small_allreduce(_pack([gsmall[k] for k in SMALL] + [loss_row[0, :1]]))
    gs = _unpack(red, small_like + [loss_row[0, :1]])
    loss = gs[-1][0]
    gs = dict(zip(SMALL, gs[:-1]))

    reduce_chips(red0b)
    stacks.update(adamw({k: reduced1[k] for k in mix_in + rest}, 1, None))
    dlt, m2, v2 = _adamw_small(pk(small), pk(gs), pk(small_m), pk(small_v))
    reduce_halves(red0a)
    stacks.update(adamw(reduce_end(red0a), 0, stacks))
    reduce_halves(red0b)
    stacks.update(adamw(reduce_end(red0b), 0, stacks))

    out_g, out_d, out_m, out_v = {}, {}, {}, {}
    for k in BIG:
        out_g[k], out_d[k], out_m[k], out_v[k] = [jnp.swapaxes(a, 1, 2) if k == "w_in" else a for a in stacks[k]]
    for dst, packed in ((out_d, dlt), (out_m, m2), (out_v, v2)):
        dst.update(zip(SMALL, _unpack(packed, small_like)))
    out_g.update(gs)

    order = ("norm_ffn1", "ffn1_gu", "ffn1_down", "norm_mix", "w_in", "sinks", "norm_out_sb", "norm_out_swa", "w_out",
             "norm_ffn2", "ffn2_gu", "ffn2_down", "rel_bias", "norm_final")
    return (loss, dh.reshape(x.shape), *[out_g[k] for k in order], *[out_d[k] for k in order],
            *[out_m[k] for k in order], *[out_v[k] for k in order])
```

```python
import math

import numpy as np
import jax
import jax.numpy as jnp
from jax import lax
from jax.experimental import pallas as pl
from jax.experimental.pallas import tpu as pltpu

F32 = jnp.float32
BF16 = jnp.bfloat16

D_MODEL = 1024
DEPTH = 2
HEAD_DIM = 64
BLK = 128
N_BUCKETS = 32
MAX_DISTANCE = 128
D_FF = 2816
EPS = 1e-6
NEG_INF = -1e30
SB_W = 512
SWA_W = 512
KV_W = 128
IN_W = 2304
SCALE = HEAD_DIM ** -0.5
N_CHIPS = 4
FS = 2 * D_FF // N_CHIPS
LANES = 128
V7X_VMEM_LIMIT = 56 * 2 ** 20
TM = 512
SB_KT = 512
SWA_G = 4

ADAM_LR = 0.001
ADAM_B1 = 0.9
ADAM_B2 = 0.999
ADAM_EPS = 1e-08
ADAM_WD = 0.01
ADAM_STEP = 10

MESH = pl.DeviceIdType.MESH
ANY = pl.BlockSpec(memory_space=pl.ANY)
HBM = pl.BlockSpec(memory_space=pltpu.HBM)
SEM = pl.BlockSpec(memory_space=pltpu.SEMAPHORE)
EFFECT = pltpu.SideEffectType.DATAFLOW_SIDE_EFFECTING


def _params(n_grid):
    return pltpu.CompilerParams(dimension_semantics=("arbitrary",) * n_grid, vmem_limit_bytes=V7X_VMEM_LIMIT)


_PREVIOUS = [None]


def _call(body, *, name, in_specs, out_specs, out_shape, grid=(), num_scalar_prefetch=0, scratch_shapes=(),
          input_output_aliases=None, compiler_params=None, hbm_args=0):
    n_in = len(in_specs)

    def run(*args):
        dep = _PREVIOUS[0]
        if any(dep is a for a in args):
            dep = None
        args = [pltpu.with_memory_space_constraint(a, pltpu.HBM) if i < hbm_args else a for i, a in enumerate(args)]
        specs = list(in_specs) + ([ANY] if dep is not None else [])
        k = num_scalar_prefetch + n_in
        fn = body if dep is None else (lambda *refs: body(*refs[:k], *refs[k + 1:]))
        if num_scalar_prefetch:
            shape = dict(grid_spec=pltpu.PrefetchScalarGridSpec(
                num_scalar_prefetch=num_scalar_prefetch, grid=grid, in_specs=specs, out_specs=out_specs,
                scratch_shapes=scratch_shapes))
        else:
            shape = dict(grid=grid, in_specs=specs, out_specs=out_specs, scratch_shapes=scratch_shapes)
        out = pl.pallas_call(fn, name=name, out_shape=out_shape, input_output_aliases=input_output_aliases or {},
                             compiler_params=compiler_params, **shape)(*args, *([] if dep is None else [dep]))
        _PREVIOUS[0] = jax.tree.leaves(out)[-1]
        return out

    return run


def _dot(a, b):
    return jnp.dot(a, b, preferred_element_type=F32)


def _dot_nt(a, b):
    return lax.dot_general(a, b, (((1,), (1,)), ((), ())), preferred_element_type=F32)


def _dot_tn(a, b):
    return lax.dot_general(a, b, (((0,), (0,)), ((), ())), preferred_element_type=F32)


def _rms_fwd(x, g):
    r = lax.rsqrt(jnp.mean(x * x, axis=-1, keepdims=True) + EPS)
    xh = x * r
    return xh * g, xh, r


def _rms_bwd(dy, xh, r, g):
    u = dy * g
    dx = r * (u - xh * jnp.mean(u * xh, axis=-1, keepdims=True))
    dg = jnp.sum(dy * xh, axis=0, keepdims=True)
    return dx, dg


def _softplus(z):
    neg_abs = lax.bitcast_convert_type(lax.bitcast_convert_type(z, jnp.int32) | jnp.int32(-2 ** 31), F32)
    sp = jnp.maximum(z, 0.0) + jnp.log(1.0 + jnp.exp(neg_abs))
    return sp, z - sp


def _norm_cast(h, g):
    t, w = h.shape

    def body(h_ref, g_ref, n_ref):
        y, _, _ = _rms_fwd(h_ref[...], g_ref[...])
        n_ref[...] = y.astype(BF16)

    return _call(
        body, name="norm_cast", grid=(t // TM,),
        in_specs=[pl.BlockSpec((TM, w), lambda i: (i, 0)), pl.BlockSpec((1, w), lambda i: (0, 0))],
        out_specs=pl.BlockSpec((TM, w), lambda i: (i, 0)),
        out_shape=jax.ShapeDtypeStruct((t, w), BF16), compiler_params=_params(1))(h, g)


def _ffn_gu(n, wgu):
    t, d = n.shape

    def body(n_ref, wg_ref, wu_ref, gu_ref, act_ref):
        x = n_ref[...]
        g = _dot(x, wg_ref[...])
        u = _dot(x, wu_ref[...])
        sig = jax.nn.sigmoid(g)
        silu = g * sig
        gu_ref[0] = (u * (sig + silu * (1.0 - sig))).astype(BF16)
        gu_ref[1] = silu.astype(BF16)
        act_ref[...] = (silu * u).astype(BF16)

    return _call(
        body, name="ffn_gu", grid=(2, t // TM),
        in_specs=[pl.BlockSpec((TM, d), lambda j, i: (i, 0)),
                  pl.BlockSpec((None, d, FS), lambda j, i: (j, 0, 0)),
                  pl.BlockSpec((None, d, FS), lambda j, i: (j + 2, 0, 0))],
        out_specs=[pl.BlockSpec((2, TM, FS), lambda j, i: (0, i, j)), pl.BlockSpec((TM, FS), lambda j, i: (i, j))],
        out_shape=[jax.ShapeDtypeStruct((2, t, D_FF), BF16), jax.ShapeDtypeStruct((t, D_FF), BF16)],
        compiler_params=_params(2))(n, wgu, wgu)


def _down_res(act, wdn, h, g_next):
    t, f = act.shape
    d = h.shape[1]

    def body(a_ref, w_ref, h_ref, g_ref, o_ref, n_ref):
        out = h_ref[...] + 0.5 * _dot(a_ref[...], w_ref[...])
        o_ref[...] = out
        n_ref[...] = _rms_fwd(out, g_ref[...])[0].astype(BF16)

    row = pl.BlockSpec((TM, d), lambda i: (i, 0))
    return _call(
        body, name="down_res", grid=(t // TM,),
        in_specs=[pl.BlockSpec((TM, f), lambda i: (i, 0)), pl.BlockSpec((f, d), lambda i: (0, 0)), row,
                  pl.BlockSpec((1, d), lambda i: (0, 0))],
        out_specs=[row, row],
        out_shape=[jax.ShapeDtypeStruct((t, d), F32), jax.ShapeDtypeStruct((t, d), BF16)],
        compiler_params=_params(1))(act, wdn, h, g_next)


def _proj(n, w_in_t):
    t, d = n.shape
    w = w_in_t.shape[0]

    def body(n_ref, w_ref, o_ref):
        o_ref[...] = _dot_nt(n_ref[...], w_ref[...]).astype(BF16)

    return _call(
        body, name="proj", grid=(t // TM,),
        in_specs=[pl.BlockSpec((TM, d), lambda i: (i, 0)), pl.BlockSpec((w, d), lambda i: (0, 0))],
        out_specs=pl.BlockSpec((TM, w), lambda i: (i, 0)),
        out_shape=jax.ShapeDtypeStruct((t, w), BF16), compiler_params=_params(1))(n, w_in_t)


def _out_res(o_sb, o_sw, g_sb, g_sw, w_out, h, g_next):
    t, d = h.shape

    def body(a_ref, b_ref, ga_ref, gb_ref, w_ref, h_ref, g_ref, o_ref, mix_ref, n_ref):
        ya, _, _ = _rms_fwd(a_ref[...], ga_ref[...])
        yb, _, _ = _rms_fwd(b_ref[...], gb_ref[...])
        mixed = jnp.concatenate([ya.astype(BF16), yb.astype(BF16)], axis=1)
        mix_ref[...] = mixed
        out = h_ref[...] + _dot(mixed, w_ref[...])
        o_ref[...] = out
        n_ref[...] = _rms_fwd(out, g_ref[...])[0].astype(BF16)

    row = pl.BlockSpec((TM, d), lambda i: (i, 0))
    return _call(
        body, name="out_res", grid=(t // TM,),
        in_specs=[pl.BlockSpec((TM, SB_W), lambda i: (i, 0)), pl.BlockSpec((TM, SWA_W), lambda i: (i, 0)),
                  pl.BlockSpec((1, SB_W), lambda i: (0, 0)), pl.BlockSpec((1, SWA_W), lambda i: (0, 0)),
                  pl.BlockSpec((d, d), lambda i: (0, 0)), row, pl.BlockSpec((1, d), lambda i: (0, 0))],
        out_specs=[row, row, row],
        out_shape=[jax.ShapeDtypeStruct((t, d), F32), jax.ShapeDtypeStruct((t, d), BF16),
                   jax.ShapeDtypeStruct((t, d), BF16)],
        compiler_params=_params(1))(o_sb, o_sw, g_sb, g_sw, w_out, h, g_next)


def _loss_head(h, g, tgt):
    t, d = h.shape

    def body(h_ref, g_ref, t_ref, dh_ref, dg_ref, loss_ref):
        @pl.when(pl.program_id(0) == 0)
        def _():
            dg_ref[...] = jnp.zeros_like(dg_ref)
            loss_ref[...] = jnp.zeros_like(loss_ref)

        gg = g_ref[...]
        y, xh, r = _rms_fwd(h_ref[...], gg)
        err = y - t_ref[...]
        part = 0.5 * jnp.sum(jnp.sum(err * err, axis=1, keepdims=True) / d, axis=0, keepdims=True)
        loss_ref[...] += jnp.broadcast_to(part, loss_ref.shape)
        dx, dg = _rms_bwd(err / d, xh, r, gg)
        dh_ref[...] = dx
        dg_ref[...] += dg

    return _call(
        body, name="loss_head", grid=(t // TM,),
        in_specs=[pl.BlockSpec((TM, d), lambda i: (i, 0)), pl.BlockSpec((1, d), lambda i: (0, 0)),
                  pl.BlockSpec((TM, d), lambda i: (i, 0))],
        out_specs=[pl.BlockSpec((TM, d), lambda i: (i, 0)), pl.BlockSpec((1, d), lambda i: (0, 0)),
                   pl.BlockSpec((1, LANES), lambda i: (0, 0))],
        out_shape=[jax.ShapeDtypeStruct((t, d), F32), jax.ShapeDtypeStruct((1, d), F32),
                   jax.ShapeDtypeStruct((1, LANES), F32)],
        compiler_params=_params(1))(h, g, tgt)


def _ffn_dact(dh, wdn, gu):
    t, d = dh.shape
    tm = TM

    def body(dh_ref, w_ref, gu_ref, o_ref):
        da = 0.5 * _dot_nt(dh_ref[...].astype(BF16), w_ref[...])
        o_ref[0] = (da * gu_ref[0].astype(F32)).astype(BF16)
        o_ref[1] = (da * gu_ref[1].astype(F32)).astype(BF16)

    return _call(
        body, name="ffn_dact", grid=(2, t // tm),
        in_specs=[pl.BlockSpec((tm, d), lambda j, i: (i, 0)), pl.BlockSpec((FS, d), lambda j, i: (j, 0)),
                  pl.BlockSpec((2, tm, FS), lambda j, i: (0, i, j))],
        out_specs=pl.BlockSpec((2, tm, FS), lambda j, i: (0, i, j)),
        out_shape=jax.ShapeDtypeStruct((2, t, D_FF), BF16), compiler_params=_params(2))(dh, wdn, gu)


def _dn_norm_bwd(a, a_spec, w, w_spec, nk, dh, h_in, g, w_transposed=False, tm=TM):
    t, d = dh.shape
    mm = _dot if w_transposed else _dot_nt

    def body(a_ref, w_ref, dh_ref, h_ref, g_ref, o_ref, dg_ref, acc_ref):
        i, k = pl.program_id(0), pl.program_id(1)

        if nk > 1:
            @pl.when(k == 0)
            def _():
                acc_ref[...] = mm(a_ref[...], w_ref[...])

            @pl.when((k > 0) & (k < nk - 1))
            def _():
                acc_ref[...] += mm(a_ref[...], w_ref[...])

        @pl.when(k == nk - 1)
        def _():
            gg = g_ref[...]
            dg = jnp.zeros_like(gg)
            for rows in (slice(r, r + TM // 2) for r in range(0, tm, TM // 2)):
                dn = mm(a_ref[rows, :], w_ref[...])
                if nk > 1:
                    dn = dn + acc_ref[rows, :]
                _, xh, r = _rms_fwd(h_ref[rows, :], gg)
                dx, dg_rows = _rms_bwd(dn, xh, r, gg)
                o_ref[rows, :] = dh_ref[rows, :] + dx
                dg = dg + dg_rows

            @pl.when(i == 0)
            def _():
                dg_ref[...] = dg

            @pl.when(i > 0)
            def _():
                dg_ref[...] += dg

    row = pl.BlockSpec((tm, d), lambda i, k: (i, 0))
    return _call(
        body, name="dn_norm_bwd", grid=(t // tm, nk),
        in_specs=[a_spec, w_spec, row, row, pl.BlockSpec((1, d), lambda i, k: (0, 0))],
        out_specs=[row, pl.BlockSpec((1, d), lambda i, k: (0, 0))],
        out_shape=[jax.ShapeDtypeStruct((t, d), F32), jax.ShapeDtypeStruct((1, d), F32)],
        scratch_shapes=[pltpu.VMEM((tm, d), F32)], compiler_params=_params(2))(a, w, dh, h_in, g)


def _ffn_dn(dgu, wgu, dh, h_in, g):
    d = dh.shape[1]
    tm = 2 * TM
    return _dn_norm_bwd(
        dgu, pl.BlockSpec((None, tm, FS), lambda i, k: (k // 2, i, k % 2)),
        wgu, pl.BlockSpec((None, d, FS), lambda i, k: (k, 0, 0)), N_CHIPS, dh, h_in, g, tm=tm)


def _mix_dn(dproj, w_in_t, dh, h_in, g):
    d = dh.shape[1]
    w = dproj.shape[1]
    return _dn_norm_bwd(
        dproj, pl.BlockSpec((TM, w), lambda i, k: (i, 0)),
        w_in_t, pl.BlockSpec((w, d), lambda i, k: (0, 0)), 1, dh, h_in, g, w_transposed=True)


def _dmixed(dh, w_out, o_sb, o_sw, g_sb, g_sw):
    t, d = dh.shape

    def body(dh_ref, w_ref, a_ref, b_ref, ga_ref, gb_ref, o_ref, dga_ref, dgb_ref):
        i = pl.program_id(0)
        dm = _dot_nt(dh_ref[...].astype(BF16), w_ref[...])
        _, xa, ra = _rms_fwd(a_ref[...], ga_ref[...])
        _, xb, rb = _rms_fwd(b_ref[...], gb_ref[...])
        da, dga = _rms_bwd(dm[:, :SB_W], xa, ra, ga_ref[...])
        db, dgb = _rms_bwd(dm[:, SB_W:], xb, rb, gb_ref[...])
        o_ref[...] = jnp.concatenate([da.astype(BF16), db.astype(BF16)], axis=1)

        @pl.when(i == 0)
        def _():
            dga_ref[...] = dga
            dgb_ref[...] = dgb

        @pl.when(i > 0)
        def _():
            dga_ref[...] += dga
            dgb_ref[...] += dgb

    return _call(
        body, name="dmixed", grid=(t // TM,),
        in_specs=[pl.BlockSpec((TM, d), lambda i: (i, 0)), pl.BlockSpec((d, d), lambda i: (0, 0)),
                  pl.BlockSpec((TM, SB_W), lambda i: (i, 0)), pl.BlockSpec((TM, SWA_W), lambda i: (i, 0)),
                  pl.BlockSpec((1, SB_W), lambda i: (0, 0)), pl.BlockSpec((1, SWA_W), lambda i: (0, 0))],
        out_specs=[pl.BlockSpec((TM, d), lambda i: (i, 0)), pl.BlockSpec((1, SB_W), lambda i: (0, 0)),
                   pl.BlockSpec((1, SWA_W), lambda i: (0, 0))],
        out_shape=[jax.ShapeDtypeStruct((t, d), BF16), jax.ShapeDtypeStruct((1, SB_W), F32),
                   jax.ShapeDtypeStruct((1, SWA_W), F32)],
        compiler_params=_params(1))(dh, w_out, o_sb, o_sw, g_sb, g_sw)


def _wgrad(name, a, a_spec, b, b_spec, grid, out_shape, out_spec, scale):
    def body(a_ref, b_ref, o_ref):
        r = _dot_tn(a_ref[...], b_ref[...].astype(BF16))
        o_ref[...] = r if scale == 1.0 else scale * r

    return _call(
        body, name=name, grid=grid, in_specs=[a_spec, b_spec], out_specs=out_spec,
        out_shape=jax.ShapeDtypeStruct(out_shape, F32), compiler_params=_params(len(grid)))(a, b)


def _wgrad_gu(n, dgu):
    t, d = n.shape
    return _wgrad(
        "wgrad_gu", n, pl.BlockSpec((t, TM), lambda s, r: (0, r)),
        dgu, pl.BlockSpec((None, t, FS), lambda s, r: (s // 2, 0, s % 2)), (N_CHIPS, d // TM),
        (N_CHIPS, d, FS), pl.BlockSpec((None, TM, FS), lambda s, r: (s, r, 0)), 1.0)


def _wgrad_down(act, dh):
    t, d = dh.shape
    return _wgrad(
        "wgrad_down", act, pl.BlockSpec((t, FS), lambda s, r: (0, s)), dh, pl.BlockSpec((t, TM), lambda s, r: (0, r)),
        (2, d // TM), (D_FF, d), pl.BlockSpec((FS, TM), lambda s, r: (s, r)), 0.5)


def _wgrad_out(mixed, dh):
    t, d = dh.shape
    return _wgrad(
        "wgrad_out", mixed, pl.BlockSpec((t, TM), lambda s: (0, s)), dh, pl.BlockSpec((t, d), lambda s: (0, 0)),
        (d // TM,), (d, d), pl.BlockSpec((TM, d), lambda s: (s, 0)), 1.0)


def _wgrad_in(n, dproj):
    t, d = n.shape
    w = dproj.shape[1]
    tw = w // 3
    return _wgrad(
        "wgrad_in", dproj, pl.BlockSpec((t, tw), lambda s: (0, s)), n, pl.BlockSpec((t, d), lambda s: (0, 0)),
        (3,), (w, d), pl.BlockSpec((tw, d), lambda s: (s, 0)), 1.0)


def _tri(rel):
    row = lax.broadcasted_iota(jnp.int32, (BLK, BLK), 0)
    col = lax.broadcasted_iota(jnp.int32, (BLK, BLK), 1)
    m = rel(row, col).astype(BF16)
    return jnp.concatenate([m, m], axis=0)


def _scan_dot(x, tri2):
    hi = x.astype(BF16)
    lo = (x - hi.astype(F32)).astype(BF16)
    return _dot(jnp.concatenate([hi, lo], axis=1), tri2)


def _head_masks():
    lane = lax.broadcasted_iota(jnp.int32, (1, LANES), 1)
    return [lane < HEAD_DIM, lane >= HEAD_DIM]


SB_PAIRS = 2
SB_ROWS = 2 * SB_PAIRS * BLK


def _sb_causal():
    row = lax.broadcasted_iota(jnp.int32, (SB_ROWS, BLK), 0) & (BLK - 1)
    return lax.broadcasted_iota(jnp.int32, (SB_ROWS, BLK), 1) < row


def _sb_mask_last(x, causal):
    own = jnp.where(causal, x[:, -BLK:], 0.0)
    return own if x.shape[1] == BLK else jnp.concatenate([x[:, :-BLK], own], axis=1)


def _sb_stack(x, hm):
    return jnp.concatenate([jnp.where(m, x[:, p * LANES:(p + 1) * LANES], jnp.zeros((BLK, LANES), x.dtype))
                            for p in range(SB_PAIRS) for m in hm], axis=0)


def _sb_unstack(y, hm):
    return jnp.concatenate([jnp.where(hm[0], y[2 * p * BLK:(2 * p + 1) * BLK], y[(2 * p + 1) * BLK:(2 * p + 2) * BLK])
                            for p in range(SB_PAIRS)], axis=1)


def _sb_pairs():
    return [(slice(2 * p * BLK, (2 * p + 2) * BLK), slice(p * LANES, (p + 1) * LANES)) for p in range(SB_PAIRS)]


def _sb_fwd(proj):
    t = proj.shape[0]
    nb = SB_KT // BLK
    wide = SB_PAIRS * LANES

    def body(q_ref, k_ref, v_ref, o_ref, tot_ref):
        hm = _head_masks()
        causal = _sb_causal()
        pairs = _sb_pairs()
        after = _tri(lambda r, c: r > c)

        def tile(qh, start, n_blk, carry, acc, own):
            ks = pl.ds(pl.multiple_of(start, BLK), n_blk * BLK)
            z = jnp.concatenate([_dot_nt(qh[rows], k_ref[ks, lanes]) for rows, lanes in pairs], axis=0)
            sp, zs = _softplus(z)
            spm = _sb_mask_last(sp, causal) if own else sp
            sufs = [None] * n_blk
            for b in reversed(range(n_blk)):
                blk = spm[:, b * BLK:(b + 1) * BLK]
                sufs[b] = carry + _scan_dot(blk, after)
                carry = carry + jnp.sum(blk, axis=1, keepdims=True)
            w = jnp.exp(zs - jnp.concatenate(sufs, axis=1))
            wb = (_sb_mask_last(w, causal) if own else w).astype(BF16)
            return carry, acc + jnp.concatenate([_dot(wb[rows], v_ref[ks, lanes]) for rows, lanes in pairs], axis=0)

        def qblock(g, j):
            qs = pl.ds(pl.multiple_of(g * SB_KT + j * BLK, BLK), BLK)
            qh = _sb_stack(q_ref[qs, :] * SCALE, hm)
            c0 = tile(qh, g * SB_KT, j + 1, jnp.zeros((SB_ROWS, 1), F32), jnp.zeros((SB_ROWS, LANES), F32), True)
            carry, acc = lax.fori_loop(0, g, lambda n, c: tile(qh, (g - 1 - n) * SB_KT, nb, c[0], c[1], False), c0)
            o_ref[qs, :] = _sb_unstack(acc, hm)
            for h in range(2 * SB_PAIRS):
                tot_ref[h, qs, :] = carry[h * BLK:(h + 1) * BLK]

        def group(g, _):
            for j in range(nb):
                qblock(g, j)
            return 0

        lax.fori_loop(0, t // SB_KT, group, 0)

    col_blk = lambda off: pl.BlockSpec((t, wide), lambda g: (0, off + g))
    n_steps = SB_W // wide
    return _call(
        body, name="sb_fwd", grid=(n_steps,), in_specs=[col_blk(0), col_blk(n_steps), col_blk(2 * n_steps)],
        out_specs=[col_blk(0), pl.BlockSpec((2 * SB_PAIRS, t, 1), lambda g: (g, 0, 0))],
        out_shape=[jax.ShapeDtypeStruct((t, SB_W), F32), jax.ShapeDtypeStruct((8, t, 1), F32)],
        compiler_params=_params(1))(proj, proj, proj)


def _sb_bwd(proj, d_o, tot):
    t = proj.shape[0]
    nb = SB_KT // BLK
    wide = SB_PAIRS * LANES

    def body(q_ref, k_ref, v_ref, do_ref, tot_ref, dq_ref, dk_ref, dv_ref, dk_acc, dv_acc):
        hm = _head_masks()
        causal = _sb_causal()
        pairs = _sb_pairs()
        before = _tri(lambda r, c: r < c)
        upto = _tri(lambda r, c: r <= c)
        dk_acc[...] = jnp.zeros_like(dk_acc)
        dv_acc[...] = jnp.zeros_like(dv_acc)

        def tile(qh, doh, tt, start, n_blk, pre, ecum, dq, own):
            ks = pl.ds(pl.multiple_of(start, BLK), n_blk * BLK)
            k = k_ref[ks, :]
            v = v_ref[ks, :]
            z = jnp.concatenate([_dot_nt(qh[rows], k[:, lanes]) for rows, lanes in pairs], axis=0)
            sp, zs = _softplus(z)
            spm = _sb_mask_last(sp, causal) if own else sp
            pres = []
            for b in range(n_blk):
                blk = spm[:, b * BLK:(b + 1) * BLK]
                pres.append(pre + _scan_dot(blk, before))
                pre = pre + jnp.sum(blk, axis=1, keepdims=True)
            logw = z - (tt - jnp.concatenate(pres, axis=1))
            if own:
                logw = jnp.minimum(logw, 0.0)
            w = jnp.exp(logw)
            if own:
                w = _sb_mask_last(w, causal)
            e = w * jnp.concatenate([_dot_nt(doh[rows], v[:, lanes]) for rows, lanes in pairs], axis=0)
            incs = []
            for b in range(n_blk):
                blk = e[:, b * BLK:(b + 1) * BLK]
                incs.append(ecum + _scan_dot(blk, upto))
                ecum = ecum + jnp.sum(blk, axis=1, keepdims=True)
            dz = e - jnp.exp(zs) * jnp.concatenate(incs, axis=1)
            if own:
                dz = _sb_mask_last(dz, causal)
            dzb = dz.astype(BF16)
            wb = w.astype(BF16)
            for rows, lanes in pairs:
                dk_acc[ks, lanes] += _dot_tn(dzb[rows], qh[rows])
                dv_acc[ks, lanes] += _dot_tn(wb[rows], doh[rows])
            return pre, ecum, dq + jnp.concatenate([_dot(dzb[rows], k[:, lanes]) for rows, lanes in pairs], axis=0)

        def qblock(g, j):
            qs = pl.ds(pl.multiple_of(g * SB_KT + j * BLK, BLK), BLK)
            qh = _sb_stack(q_ref[qs, :] * SCALE, hm)
            doh = _sb_stack(do_ref[qs, :], hm)
            tt = jnp.concatenate([tot_ref[h, qs, :] for h in range(2 * SB_PAIRS)], axis=0)
            c0 = (jnp.zeros((SB_ROWS, 1), F32), jnp.zeros((SB_ROWS, 1), F32), jnp.zeros((SB_ROWS, LANES), F32))
            c = lax.fori_loop(0, g, lambda kt, c: tile(qh, doh, tt, kt * SB_KT, nb, c[0], c[1], c[2], False), c0)
            dq = tile(qh, doh, tt, g * SB_KT, j + 1, c[0], c[1], c[2], True)[2]
            dq_ref[qs, :] = (_sb_unstack(dq, hm) * SCALE).astype(BF16)

        def group(g, _):
            for j in range(nb):
                qblock(g, j)
            return 0

        lax.fori_loop(0, t // SB_KT, group, 0)
        dk_ref[...] = dk_acc[...].astype(BF16)
        dv_ref[...] = dv_acc[...].astype(BF16)

    col_blk = lambda off: pl.BlockSpec((t, wide), lambda g: (0, off + g))
    n_steps = SB_W // wide
    out = jax.ShapeDtypeStruct((t, SB_W), BF16)
    return _call(
        body, name="sb_bwd", grid=(n_steps,),
        in_specs=[col_blk(0), col_blk(n_steps), col_blk(2 * n_steps), col_blk(0),
                  pl.BlockSpec((2 * SB_PAIRS, t, 1), lambda g: (g, 0, 0))],
        out_specs=[col_blk(0), col_blk(0), col_blk(0)], out_shape=[out, out, out],
        scratch_shapes=[pltpu.VMEM((t, wide), F32), pltpu.VMEM((t, wide), F32)],
        compiler_params=_params(1))(proj, proj, proj, d_o, tot)


def _bucket_table():
    a = np.arange(BLK)[:, None]
    c = np.arange(2 * BLK)[None, :]
    dist = np.maximum(BLK + a - c, 0)
    max_exact = N_BUCKETS // 2
    dd = np.maximum(dist, 1).astype(np.float32)
    large = max_exact + (np.log(dd / max_exact) / math.log(MAX_DISTANCE / max_exact)
                         * (N_BUCKETS - max_exact)).astype(np.int32)
    large = np.minimum(large, N_BUCKETS - 1)
    return np.where(dist < max_exact, dist, large).astype(np.int32)


SWA_H = 8


def _swa_band_masks():
    row = lax.broadcasted_iota(jnp.int32, (SWA_H * BLK, 2 * BLK), 0) & (BLK - 1)
    col = lax.broadcasted_iota(jnp.int32, (SWA_H * BLK, 2 * BLK), 1)
    own = lax.broadcasted_iota(jnp.int32, (SWA_H * BLK, BLK), 1) <= (
        lax.broadcasted_iota(jnp.int32, (SWA_H * BLK, BLK), 0) & (BLK - 1))
    return (col > row) & ((col < BLK) | (col - BLK <= row)), own


def _swa_stack(ref, qs, hm, scale):
    parts = []
    for hq in range(SWA_H):
        kvh = hq // SWA_G
        x = ref[qs, (hq // 2) * LANES:(hq // 2 + 1) * LANES].astype(F32)
        if hq % 2 != kvh:
            x = pltpu.roll(x, HEAD_DIM, 1)
        parts.append(jnp.where(hm[kvh], x * scale, 0.0).astype(BF16))
    return jnp.concatenate(parts, axis=0)


def _swa_unstack(x8, hm):
    heads = []
    for hq in range(SWA_H):
        x = x8[hq * BLK:(hq + 1) * BLK]
        heads.append(pltpu.roll(x, HEAD_DIM, 1) if hq % 2 != hq // SWA_G else x)
    return [jnp.where(hm[0], heads[2 * p], heads[2 * p + 1]) for p in range(SWA_H // 2)]


def _swa_scores(q8, kb, bias_ref, mask, cols):
    bias8 = jnp.concatenate([bias_ref[hq, :, cols] for hq in range(SWA_H)], axis=0)
    return jnp.where(mask, _dot_nt(q8, kb) + bias8, NEG_INF)


def _swa_sinks(sink_ref):
    return jnp.concatenate([jnp.broadcast_to(sink_ref[hq:hq + 1, 0:1], (BLK, 1)) for hq in range(SWA_H)], axis=0)


def _swa_fwd(proj, bias, sinks_b):
    t = proj.shape[0]
    nq = t // BLK

    def body(q_ref, k_ref, v_ref, bias_ref, sink_ref, o_ref, lse_ref):
        hm = _head_masks()
        band, own = _swa_band_masks()

        def qblock(i, prev):
            qs = pl.ds(pl.multiple_of(i * BLK, BLK), BLK)
            if prev:
                ks, mask, cols = pl.ds(pl.multiple_of((i - 1) * BLK, BLK), 2 * BLK), band, slice(None)
            else:
                ks, mask, cols = qs, own, slice(BLK, None)
            q8 = _swa_stack(q_ref, qs, hm, SCALE)
            sink8 = _swa_sinks(sink_ref)
            s = _swa_scores(q8, k_ref[ks, :], bias_ref, mask, cols)
            m = jnp.maximum(jnp.max(s, axis=1, keepdims=True), sink8)
            p = jnp.exp(s - m)
            den = jnp.sum(p, axis=1, keepdims=True) + jnp.exp(sink8 - m)
            o8 = _dot((p * (1.0 / den)).astype(BF16), v_ref[ks, :])
            lse8 = m + jnp.log(den)
            for hq in range(SWA_H):
                lse_ref[hq, qs, :] = lse8[hq * BLK:(hq + 1) * BLK]
            for pp, o in enumerate(_swa_unstack(o8, hm)):
                o_ref[qs, pp * LANES:(pp + 1) * LANES] = o

        qblock(0, False)

        def step(i, _):
            qblock(i, True)
            return 0

        lax.fori_loop(1, nq, step, 0)

    return _call(
        body, name="swa_fwd", grid=(1,),
        in_specs=[pl.BlockSpec((t, SWA_W), lambda i: (0, 3)), pl.BlockSpec((t, KV_W), lambda i: (0, 16)),
                  pl.BlockSpec((t, KV_W), lambda i: (0, 17)), pl.BlockSpec((8, BLK, 2 * BLK), lambda i: (0, 0, 0)),
                  pl.BlockSpec((8, LANES), lambda i: (0, 0))],
        out_specs=[pl.BlockSpec((t, SWA_W), lambda i: (0, 0)), pl.BlockSpec((8, t, 1), lambda i: (0, 0, 0))],
        out_shape=[jax.ShapeDtypeStruct((t, SWA_W), F32), jax.ShapeDtypeStruct((8, t, 1), F32)],
        compiler_params=_params(1))(proj, proj, proj, bias, sinks_b)


def _swa_bwd(proj, d_o, lse, bias, sinks_b, dbias_in):
    t = proj.shape[0]
    nq = t // BLK

    def body(q_ref, k_ref, v_ref, do_ref, lse_ref, bias_ref, sink_ref, dbi_ref,
             dq_ref, dk_ref, dv_ref, dsink_ref, dbias_ref, dk_acc, dv_acc):
        hm = _head_masks()
        band, own = _swa_band_masks()
        dk_acc[...] = jnp.zeros_like(dk_acc)
        dv_acc[...] = jnp.zeros_like(dv_acc)
        dbias_ref[...] = dbi_ref[...]

        def qblock(i, prev, dsink8):
            qs = pl.ds(pl.multiple_of(i * BLK, BLK), BLK)
            if prev:
                ks, mask, cols = pl.ds(pl.multiple_of((i - 1) * BLK, BLK), 2 * BLK), band, slice(None)
            else:
                ks, mask, cols = qs, own, slice(BLK, None)
            q8 = _swa_stack(q_ref, qs, hm, SCALE)
            do8 = _swa_stack(do_ref, qs, hm, 1.0)
            sink8 = _swa_sinks(sink_ref)
            lse8 = jnp.concatenate([lse_ref[hq, qs, :] for hq in range(SWA_H)], axis=0)
            kb = k_ref[ks, :]
            p = jnp.exp(_swa_scores(q8, kb, bias_ref, mask, cols) - lse8)
            dp = _dot_nt(do8, v_ref[ks, :])
            delta = jnp.sum(p * dp, axis=1, keepdims=True)
            ds = p * (dp - delta)
            for hq in range(SWA_H):
                dbias_ref[hq, :, cols] += ds[hq * BLK:(hq + 1) * BLK]
            dsb = ds.astype(BF16)
            dk_acc[ks, :] += _dot_tn(dsb, q8)
            dv_acc[ks, :] += _dot_tn(p.astype(BF16), do8)
            for pp, dq in enumerate(_swa_unstack(_dot(dsb, kb) * SCALE, hm)):
                dq_ref[qs, pp * LANES:(pp + 1) * LANES] = dq.astype(BF16)
            return dsink8 - jnp.exp(sink8 - lse8) * delta

        ds0 = qblock(0, False, jnp.zeros((SWA_H * BLK, 1), F32))
        ds8 = lax.fori_loop(1, nq, lambda i, c: qblock(i, True, c), ds0)
        for hq in range(SWA_H):
            dsink_ref[hq:hq + 1, :] = jnp.broadcast_to(
                jnp.sum(ds8[hq * BLK:(hq + 1) * BLK], axis=0, keepdims=True), (1, LANES))

        dk_ref[...] = dk_acc[...].astype(BF16)
        dv_ref[...] = dv_acc[...].astype(BF16)

    full3 = pl.BlockSpec((8, BLK, 2 * BLK), lambda i: (0, 0, 0))
    kv = jax.ShapeDtypeStruct((t, KV_W), BF16)
    return _call(
        body, name="swa_bwd", grid=(1,),
        in_specs=[pl.BlockSpec((t, SWA_W), lambda i: (0, 3)), pl.BlockSpec((t, KV_W), lambda i: (0, 16)),
                  pl.BlockSpec((t, KV_W), lambda i: (0, 17)), pl.BlockSpec((t, SWA_W), lambda i: (0, 1)),
                  pl.BlockSpec((8, t, 1), lambda i: (0, 0, 0)), full3, pl.BlockSpec((8, LANES), lambda i: (0, 0)),
                  full3],
        out_specs=[pl.BlockSpec((t, SWA_W), lambda i: (0, 0)), pl.BlockSpec((t, KV_W), lambda i: (0, 0)),
                   pl.BlockSpec((t, KV_W), lambda i: (0, 0)), pl.BlockSpec((8, LANES), lambda i: (0, 0)), full3],
        out_shape=[jax.ShapeDtypeStruct((t, SWA_W), BF16), kv, kv, jax.ShapeDtypeStruct((8, LANES), F32),
                   jax.ShapeDtypeStruct((8, BLK, 2 * BLK), F32)],
        scratch_shapes=[pltpu.VMEM((t, KV_W), F32), pltpu.VMEM((t, KV_W), F32)],
        compiler_params=_params(1))(proj, proj, proj, d_o, lse, bias, sinks_b, dbias_in)


def _concat_cols(parts):
    t = parts[0].shape[0]
    widths = [a.shape[1] for a in parts]

    def body(*refs):
        refs[-1][...] = jnp.concatenate([r[...] for r in refs[:-1]], axis=1)

    return _call(
        body, name="concat_cols", grid=(t // TM,),
        in_specs=[pl.BlockSpec((TM, w), lambda i: (i, 0)) for w in widths],
        out_specs=pl.BlockSpec((TM, sum(widths)), lambda i: (i, 0)),
        out_shape=jax.ShapeDtypeStruct((t, sum(widths)), parts[0].dtype), compiler_params=_params(1))(*parts)


def _bias_table(rel_bias, buckets):
    def body(rb_ref, b_ref, o_ref):
        bk = b_ref[...]
        for h in range(8):
            acc = jnp.zeros((BLK, 2 * BLK), F32)
            for b in range(N_BUCKETS):
                acc = jnp.where(bk == b, rb_ref[b, h], acc)
            o_ref[h] = acc

    return _call(
        body, name="bias_table", grid=(1,),
        in_specs=[pl.BlockSpec(memory_space=pltpu.SMEM), pl.BlockSpec((BLK, 2 * BLK), lambda i: (0, 0))],
        out_specs=pl.BlockSpec((8, BLK, 2 * BLK), lambda i: (0, 0, 0)),
        out_shape=jax.ShapeDtypeStruct((8, BLK, 2 * BLK), F32), compiler_params=_params(1))(rel_bias, buckets)


def _bias_grad(dbias, buckets):
    def body(d_ref, b_ref, o_ref):
        lane = lax.broadcasted_iota(jnp.int32, (1, LANES), 1)
        bk = b_ref[...]
        for h in range(8):
            d = d_ref[h]
            acc = jnp.zeros((1, LANES), F32)
            for b in range(N_BUCKETS):
                s = jnp.sum(jnp.sum(jnp.where(bk == b, d, 0.0), axis=0, keepdims=True), axis=1, keepdims=True)
                acc = acc + jnp.where(lane == b, s, 0.0)
            o_ref[h:h + 1, :] = acc

    return _call(
        body, name="bias_grad", grid=(1,),
        in_specs=[pl.BlockSpec((8, BLK, 2 * BLK), lambda i: (0, 0, 0)), pl.BlockSpec((BLK, 2 * BLK), lambda i: (0, 0))],
        out_specs=pl.BlockSpec((8, LANES), lambda i: (0, 0)),
        out_shape=jax.ShapeDtypeStruct((8, LANES), F32), compiler_params=_params(1))(dbias, buckets)


def _row(a):
    return a.reshape(1, -1)


def _fwd_ffn1(h, n1, w, small, l):
    s = {"h0": h, "n1": n1}
    s["gu1"], s["act1"] = _ffn_gu(n1, w["ffn1_gu"])
    s["h1"], s["nm"] = _down_res(s["act1"], w["ffn1_down"], h, _row(small["norm_mix"][l]))
    return s


def _fwd_proj_sb(s, w):
    s["proj"] = _proj(s["nm"], w["w_in"])
    s["o_sb"], s["tot"] = _sb_fwd(s["proj"])


def _fwd_swa(s, small, l, bias):
    s["sinks_b"] = jnp.broadcast_to(small["sinks"][l][:, None], (8, LANES))
    s["o_sw"], s["lse"] = _swa_fwd(s["proj"], bias, s["sinks_b"])


def _fwd_out_ffn2(s, w, small, l, g_after):
    s["h2"], s["mixed"], s["n2"] = _out_res(
        s["o_sb"], s["o_sw"], _row(small["norm_out_sb"][l]), _row(small["norm_out_swa"][l]), w["w_out"], s["h1"],
        _row(small["norm_ffn2"][l]))
    s["gu2"], s["act2"] = _ffn_gu(s["n2"], w["ffn2_gu"])
    return _down_res(s["act2"], w["ffn2_down"], s["h2"], g_after)


def _bwd_ffn_dact(dh, s, w, which):
    return _ffn_dact(dh, w[f"ffn{which}_down"], s[f"gu{which}"])


def _bwd_ffn_rest(dh, dgu, s, w, small, l, which):
    h_in, norm = (s["h0"], "norm_ffn1") if which == 1 else (s["h2"], "norm_ffn2")
    g_down = _wgrad_down(s[f"act{which}"], dh)
    g_gu = _wgrad_gu(s[f"n{which}"], dgu)
    dh, dg = _ffn_dn(dgu, w[f"ffn{which}_gu"], dh, h_in, _row(small[norm][l]))
    return dh, {f"ffn{which}_down": g_down, f"ffn{which}_gu": g_gu}, {norm: dg}


def _bwd_ffn(dh, s, w, small, l, which):
    return _bwd_ffn_rest(dh, _bwd_ffn_dact(dh, s, w, which), s, w, small, l, which)


def _bwd_mix(dh, s, w, small, l, bias, dbias):
    g_out = _wgrad_out(s["mixed"], dh)
    d_o, dg_sb, dg_sw = _dmixed(dh, w["w_out"], s["o_sb"], s["o_sw"], _row(small["norm_out_sb"][l]),
                                _row(small["norm_out_swa"][l]))
    dq_sb, dk_sb, dv_sb = _sb_bwd(s["proj"], d_o, s["tot"])
    dq_sw, dk_sw, dv_sw, dsink, dbias = _swa_bwd(s["proj"], d_o, s["lse"], bias, s["sinks_b"], dbias)
    dproj = _concat_cols([dq_sb, dk_sb, dv_sb, dq_sw, dk_sw, dv_sw])
    g_in = _wgrad_in(s["nm"], dproj)
    dh, dg_mix = _mix_dn(dproj, w["w_in"], dh, s["h1"], _row(small["norm_mix"][l]))
    gs = {"norm_out_sb": dg_sb, "norm_out_swa": dg_sw, "sinks": dsink[:, 0], "norm_mix": dg_mix}
    return dh, {"w_out": g_out, "w_in": g_in}, gs, dbias


def _place():
    x, y, c = lax.axis_index("x"), lax.axis_index("y"), lax.axis_index("c")
    return x, y, c, 2 * x + y


def _chip_core(k, c):
    return (k // 2, k % 2, c)


def _rows_per_block(rows, cols, copies):
    best = 16
    for tr in range(16, rows + 1, 16):
        if rows % tr == 0 and copies * tr * cols * 4 <= 6 * 2 ** 20:
            best = tr
    assert rows % best == 0
    return best


def _place_own(w, l, me1):
    _, rows, cols = w.shape
    tr = _rows_per_block(rows // 2, cols, 1)
    per_half = rows // 2 // tr

    def body(me_ref, w_ref, o_ref):
        o_ref[...] = w_ref[...].astype(BF16)

    return _call(
        body, name="place_own",
        num_scalar_prefetch=1, grid=(rows // tr,),
        in_specs=[pl.BlockSpec((None, tr, cols), lambda r, me: (l, r, 0))],
        out_specs=pl.BlockSpec((None, None, tr, cols), lambda r, me: (me[0], r // per_half, r % per_half, 0)),
        out_shape=jax.ShapeDtypeStruct((N_CHIPS, 2, rows // 2, cols), BF16), compiler_params=_params(1))(me1, w)


def _plan_gather_ici(bufs):
    _, _, c, me = _place()
    return [(b.at[me, c], b.at[me, c], b.at[(me + 3 - j) % N_CHIPS, c], _chip_core((me + 1 + j) % N_CHIPS, c))
            for b in bufs for j in range(3)]


def _plan_gather_d2d(bufs):
    x, y, c, me = _place()
    return [(b.at[(me + 3 - j) % N_CHIPS, c], b.at[(me + 3 - j) % N_CHIPS, c], b.at[(me + 3 - j) % N_CHIPS, 1 - c],
             (x, y, 1 - c)) for b in bufs for j in range(3)]


def _plan_grad_sibling(bufs):
    x, y, c, _ = _place()
    n = len(bufs) // 2
    return [(g.at[:, 1 - c], z, z, (x, y, 1 - c)) for g, z in zip(bufs[:n], bufs[n:])]


def _plan_grad_chips(bufs):
    _, _, c, me = _place()
    n = len(bufs) // 2
    return [(p.at[j], z.at[j], z.at[j], _chip_core((me + 1 + j) % N_CHIPS, c))
            for p, z in zip(bufs[:n], bufs[n:]) for j in range(3)]


def _plan_grad_halves(bufs):
    x, y, c, _ = _place()
    return [(b.at[c], b.at[c], b.at[1 - c], (x, y, 1 - c)) for b in bufs]


def _remote(src, dst, send_sem, recv_sem, to):
    return pltpu.make_async_remote_copy(src_ref=src, dst_ref=dst, send_sem=send_sem, recv_sem=recv_sem,
                                        device_id=to, device_id_type=MESH)


def _exchange_start(name, plan, bufs, n_copies):
    n = len(bufs)

    def body(*refs):
        ins = refs[:n]
        ssem, rsem = refs[n], refs[n + 1]
        token = refs[-1]
        for i, (src, dst, _, to) in enumerate(plan(ins)):
            _remote(src, dst, ssem.at[i], rsem.at[i], to).start()
        token[...] = jnp.zeros_like(token)

    out = _call(
        body, name=name,
        out_shape=(pltpu.SemaphoreType.DMA((n_copies,)), pltpu.SemaphoreType.DMA((n_copies,)),
                   *[pltpu.HBM(a.shape, a.dtype) for a in bufs], jax.ShapeDtypeStruct((8, LANES), F32)),
        in_specs=[HBM] * n, out_specs=(SEM, SEM, *[HBM] * n, pl.BlockSpec(memory_space=pltpu.VMEM)),
        input_output_aliases={t: 2 + t for t in range(n)}, hbm_args=n,
        compiler_params=pltpu.CompilerParams(has_side_effects=EFFECT),
    )(*bufs)
    return (out[0], out[1]), list(out[2:2 + n])


def _exchange_wait(name, plan, bufs, sems):
    n = len(bufs)

    def body(*refs):
        ins = refs[:n]
        ssem, rsem = refs[n], refs[n + 1]
        for i, (src, dst, land, to) in enumerate(plan(ins)):
            _remote(src, dst, ssem.at[i], rsem.at[i], to).wait_send()
            _remote(land, land, ssem.at[i], rsem.at[i], to).wait_recv()

    return list(_call(
        body, name=name, out_shape=[pltpu.HBM(a.shape, a.dtype) for a in bufs],
        in_specs=[HBM] * n + [SEM, SEM], out_specs=[HBM] * n,
        input_output_aliases={t: t for t in range(n)},
        compiler_params=pltpu.CompilerParams(has_side_effects=EFFECT),
    )(*bufs, sems[0], sems[1]))


def _exchange_pass(name, done, plan, bufs, sems, n_copies):
    n = len(bufs)

    def body(*refs):
        ins = refs[:n]
        old_s, old_r, ssem, rsem = refs[n], refs[n + 1], refs[n + 2], refs[n + 3]
        token = refs[-1]
        for i, (src, dst, land, to) in enumerate(done(ins)):
            _remote(src, dst, old_s.at[i], old_r.at[i], to).wait_send()
            _remote(land, land, old_s.at[i], old_r.at[i], to).wait_recv()
        for i, (src, dst, _, to) in enumerate(plan(ins)):
            _remote(src, dst, ssem.at[i], rsem.at[i], to).start()
        token[...] = jnp.zeros_like(token)

    out = _call(
        body, name=name,
        out_shape=(pltpu.SemaphoreType.DMA((n_copies,)), pltpu.SemaphoreType.DMA((n_copies,)),
                   *[pltpu.HBM(a.shape, a.dtype) for a in bufs], jax.ShapeDtypeStruct((8, LANES), F32)),
        in_specs=[HBM] * n + [SEM, SEM], out_specs=(SEM, SEM, *[HBM] * n, pl.BlockSpec(memory_space=pltpu.VMEM)),
        input_output_aliases={t: 2 + t for t in range(n)},
        compiler_params=pltpu.CompilerParams(has_side_effects=EFFECT),
    )(*bufs, sems[0], sems[1])
    return (out[0], out[1]), list(out[2:2 + n])


def _chip_sum(g, xbuf, cm):
    _, _, r2, cols = g.shape
    tr = _rows_per_block(r2, cols, 1)

    def body(cm_ref, g_ref, x_ref, o_ref):
        o_ref[...] = (g_ref[...] + x_ref[...]).astype(BF16)

    return _call(
        body, name="grad_chip_sum",
        num_scalar_prefetch=1, grid=(3, r2 // tr),
        in_specs=[pl.BlockSpec((None, None, tr, cols), lambda j, r, cm: ((cm[1] + 1 + j) % N_CHIPS, cm[0], r, 0)),
                  pl.BlockSpec((None, tr, cols), lambda j, r, cm: ((cm[1] + 1 + j) % N_CHIPS, r, 0))],
        out_specs=pl.BlockSpec((None, tr, cols), lambda j, r, cm: (j, r, 0)),
        out_shape=jax.ShapeDtypeStruct((3, r2, cols), BF16), compiler_params=_params(2))(cm, g, xbuf)


def _total_sum(g, xbuf, rbuf, cm):
    _, _, r2, cols = g.shape
    tr = _rows_per_block(r2, cols, 3)

    def body(cm_ref, g_ref, x_ref, r_ref, o_ref):
        acc = g_ref[...] + x_ref[...]
        for j in range(3):
            acc = acc + r_ref[j].astype(F32)
        o_ref[...] = acc

    return _call(
        body, name="grad_total_sum",
        num_scalar_prefetch=1, grid=(r2 // tr,),
        in_specs=[pl.BlockSpec((None, None, tr, cols), lambda r, cm: (cm[1], cm[0], r, 0)),
                  pl.BlockSpec((None, tr, cols), lambda r, cm: (cm[1], r, 0)),
                  pl.BlockSpec((3, tr, cols), lambda r, cm: (0, r, 0))],
        out_specs=pl.BlockSpec((None, tr, cols), lambda r, cm: (cm[0], r, 0)),
        out_shape=jax.ShapeDtypeStruct((2, r2, cols), F32), compiler_params=_params(1))(cm, g, xbuf, rbuf)


def _small_allreduce(v):
    rows = v.shape[0]
    n_dev = 2 * N_CHIPS

    def body(v_ref, o_ref, buf, ssem, rsem):
        x, y, c, _ = _place()
        me = 4 * x + 2 * y + c
        buf[me] = v_ref[...]

        def copy(d, slot, to):
            return _remote(v_ref, buf.at[slot], ssem.at[d - 1], rsem.at[d - 1], (to // 4, (to // 2) % 2, to % 2))

        cps = [copy(d, me, (me + d) % n_dev) for d in range(1, n_dev)]
        for cp in cps:
            cp.start()
        for d in range(1, n_dev):
            copy(d, (me + n_dev - d) % n_dev, me).wait_recv()
        for cp in cps:
            cp.wait_send()
        acc = buf[0]
        for i in range(1, n_dev):
            acc = acc + buf[i]
        o_ref[...] = acc

    vm = pl.BlockSpec(memory_space=pltpu.VMEM)
    return _call(
        body, name="small_allreduce", in_specs=[vm], out_specs=vm,
        out_shape=jax.ShapeDtypeStruct(v.shape, F32),
        scratch_shapes=[pltpu.VMEM((n_dev, rows, LANES), F32), pltpu.SemaphoreType.DMA((n_dev - 1,)),
                        pltpu.SemaphoreType.DMA((n_dev - 1,))],
        compiler_params=pltpu.CompilerParams(vmem_limit_bytes=V7X_VMEM_LIMIT))(v)


def _adamw_math(w, g, m, v):
    m2 = ADAM_B1 * m + (1.0 - ADAM_B1) * g
    v2 = ADAM_B2 * v + (1.0 - ADAM_B2) * (g * g)
    m_hat = m2 / (1.0 - ADAM_B1 ** ADAM_STEP)
    v_hat = v2 / (1.0 - ADAM_B2 ** ADAM_STEP)
    return -ADAM_LR * (m_hat / (jnp.sqrt(v_hat) + ADAM_EPS) + ADAM_WD * w), m2, v2


def _adamw_layer(w, g, m, v, l, prev):
    _, rows, cols = w.shape
    tr = rows
    for cand in range(8, rows + 1, 8):
        if rows % cand == 0 and cand * cols * 4 <= 2 ** 21:
            tr = cand

    def body(w_ref, g_ref, m_ref, v_ref, *outs):
        go_ref, d_ref, m2_ref, v2_ref = outs[-4:]
        g = g_ref[...]
        go_ref[...] = g
        d_ref[...], m2_ref[...], v2_ref[...] = _adamw_math(w_ref[...], g, m_ref[...], v_ref[...])

    stack = pl.BlockSpec((None, tr, cols), lambda i: (l, i, 0))
    ins, specs, alias = [w, g, m, v], [stack, pl.BlockSpec((tr, cols), lambda i: (i, 0)), stack, stack], {}
    if prev is not None:
        ins += list(prev)
        specs += [ANY] * 4
        alias = {4 + i: i for i in range(4)}
    return _call(
        body, name="adamw", grid=(rows // tr,), in_specs=specs, out_specs=[stack] * 4,
        out_shape=[jax.ShapeDtypeStruct(w.shape, F32)] * 4, input_output_aliases=alias,
        compiler_params=_params(1))(*ins)


def _adamw_small(w, g, m, v):
    def body(w_ref, g_ref, m_ref, v_ref, d_ref, m2_ref, v2_ref):
        d_ref[...], m2_ref[...], v2_ref[...] = _adamw_math(w_ref[...], g_ref[...], m_ref[...], v_ref[...])

    spec = pl.BlockSpec(w.shape, lambda i: (0, 0))
    return _call(
        body, name="adamw_small", grid=(1,), in_specs=[spec] * 4, out_specs=[spec] * 3,
        out_shape=[jax.ShapeDtypeStruct(w.shape, F32)] * 3, compiler_params=_params(1))(w, g, m, v)


SMALL = ("norm_ffn1", "norm_mix", "sinks", "norm_out_sb", "norm_out_swa", "norm_ffn2", "rel_bias", "norm_final")
BIG = ("ffn1_gu", "ffn1_down", "w_in", "w_out", "ffn2_gu", "ffn2_down")


def _pack(parts):
    flat, n = [], 0
    for a in parts:
        a = a.reshape(-1).astype(F32)
        gap = -a.shape[0] % LANES
        flat += [a] + ([jnp.zeros((gap,), F32)] if gap else [])
        n += a.shape[0] + gap
    tail = -(n // LANES) % 8 * LANES
    return jnp.concatenate(flat + ([jnp.zeros((tail,), F32)] if tail else [])).reshape(-1, LANES)


def _unpack(packed, like):
    out, r = [], 0
    for a in like:
        n = math.prod(a.shape)
        nr = -(-n // LANES)
        out.append(packed[r:r + nr].reshape(-1)[:n].reshape(a.shape))
        r += nr
    return out


def _halved(a):
    k, r, cols = a.shape
    return a.reshape(k, 2, r // 2, cols)


def _weight_view(k, buf):
    full = buf.reshape(N_CHIPS, buf.shape[2] * 2, buf.shape[3])
    return full if k.endswith("_gu") else full.reshape(-1, D_MODEL)


def _grad_stack(k, g):
    if not k.endswith("_gu"):
        g = g.reshape(N_CHIPS, g.shape[0] // N_CHIPS, D_MODEL)
    return _halved(g)


def _empty_like_hbm(shape, dtype):
    return pltpu.with_memory_space_constraint(lax.empty(shape, dtype), pltpu.HBM)


def kernel(x, norm_ffn1, w_ffn1_gu, w_ffn1_down, norm_mix, w_in, sinks, norm_out_sb, norm_out_swa, w_out, norm_ffn2, w_ffn2_gu, w_ffn2_down, rel_bias, norm_final, loss_target, m_norm_ffn1, m_w_ffn1_gu, m_w_ffn1_down, m_norm_mix, m_w_in, m_sinks, m_norm_out_sb, m_norm_out_swa, m_w_out, m_norm_ffn2, m_w_ffn2_gu, m_w_ffn2_down, m_rel_bias, m_norm_final, v_norm_ffn1, v_w_ffn1_gu, v_w_ffn1_down, v_norm_mix, v_w_in, v_sinks, v_norm_out_sb, v_norm_out_swa, v_w_out, v_norm_ffn2, v_w_ffn2_gu, v_w_ffn2_down, v_rel_bias, v_norm_final):
    big_w = dict(ffn1_gu=w_ffn1_gu, ffn1_down=w_ffn1_down, w_in=w_in, w_out=w_out, ffn2_gu=w_ffn2_gu, ffn2_down=w_ffn2_down)
    big_m = dict(ffn1_gu=m_w_ffn1_gu, ffn1_down=m_w_ffn1_down, w_in=m_w_in, w_out=m_w_out, ffn2_gu=m_w_ffn2_gu, ffn2_down=m_w_ffn2_down)
    big_v = dict(ffn1_gu=v_w_ffn1_gu, ffn1_down=v_w_ffn1_down, w_in=v_w_in, w_out=v_w_out, ffn2_gu=v_w_ffn2_gu, ffn2_down=v_w_ffn2_down)
    small = dict(norm_ffn1=norm_ffn1, norm_mix=norm_mix, sinks=sinks, norm_out_sb=norm_out_sb, norm_out_swa=norm_out_swa,
                 norm_ffn2=norm_ffn2, rel_bias=rel_bias, norm_final=norm_final)
    small_m = dict(norm_ffn1=m_norm_ffn1, norm_mix=m_norm_mix, sinks=m_sinks, norm_out_sb=m_norm_out_sb,
                   norm_out_swa=m_norm_out_swa, norm_ffn2=m_norm_ffn2, rel_bias=m_rel_bias, norm_final=m_norm_final)
    small_v = dict(norm_ffn1=v_norm_ffn1, norm_mix=v_norm_mix, sinks=v_sinks, norm_out_sb=v_norm_out_sb,
                   norm_out_swa=v_norm_out_swa, norm_ffn2=v_norm_ffn2, rel_bias=v_rel_bias, norm_final=v_norm_final)
    for dct in (big_w, big_m, big_v):
        dct["w_in"] = jnp.swapaxes(dct["w_in"], 1, 2)
    _PREVIOUS[0] = None
    _, _, c, me = _place()
    cm = jnp.stack([c, me]).astype(jnp.int32)
    buckets = jnp.asarray(_bucket_table())
    ffn1, mix_in, rest = ("ffn1_gu", "ffn1_down"), ("w_in",), ("w_out", "ffn2_gu", "ffn2_down")

    def place(l, keys):
        return [_place_own(big_w[k], l, cm[1:]) for k in keys]

    def views(keys, bufs):
        return {k: _weight_view(k, b) for k, b in zip(keys, bufs)}

    def gather_start(tag, bufs):
        return _exchange_start(f"gather{tag}_ici_start", _plan_gather_ici, bufs, 3 * len(bufs))

    def gather_pass(tag, flight):
        return _exchange_pass(f"gather{tag}_pass", _plan_gather_ici, _plan_gather_d2d, flight[1], flight[0],
                              3 * len(flight[1]))

    def gather_done(tag, keys, flight):
        return views(keys, _exchange_wait(f"gather{tag}_d2d_wait", _plan_gather_d2d, flight[1], flight[0]))

    fly_ffn0 = gather_start("0a", place(0, ffn1))
    fly_in0 = gather_start("0b", place(0, mix_in))
    fly_rest0 = gather_start("0c", place(0, rest))
    bias = _bias_table(rel_bias, buckets)
    fly_ffn1 = gather_start("1a", place(1, ffn1))
    fly_rest1 = gather_start("1b", place(1, mix_in + rest))
    n1 = _norm_cast(x[0], _row(norm_ffn1[0]))
    w0 = gather_done("0a", ffn1, gather_pass("0a", fly_ffn0))

    s0 = _fwd_ffn1(x[0], n1, w0, small, 0)
    w0.update(gather_done("0b", mix_in, gather_pass("0b", fly_in0)))
    _fwd_proj_sb(s0, w0)
    fly_rest0 = gather_pass("0c", fly_rest0)
    _fwd_swa(s0, small, 0, bias)
    w0.update(gather_done("0c", rest, fly_rest0))
    h, n1 = _fwd_out_ffn2(s0, w0, small, 0, _row(norm_ffn1[1]))
    fly_ffn1 = gather_pass("1a", fly_ffn1)
    fly_rest1 = gather_pass("1b", fly_rest1)
    w1 = gather_done("1a", ffn1, fly_ffn1)
    s1 = _fwd_ffn1(h, n1, w1, small, 1)
    w1.update(gather_done("1b", mix_in + rest, fly_rest1))
    _fwd_proj_sb(s1, w1)
    _fwd_swa(s1, small, 1, bias)
    h, _ = _fwd_out_ffn2(s1, w1, small, 1, _row(norm_final))
    dh, dg_final, loss_row = _loss_head(h, _row(norm_final), loss_target[0])

    def landing(stacks, lead, dtype):
        return [_empty_like_hbm((lead,) + a.shape[2:], dtype) for a in stacks]

    def reduce_begin(tag, keys, gw):
        stacks = [_grad_stack(k, gw[k]) for k in keys]
        flight = _exchange_start(f"grad{tag}_sibling_start", _plan_grad_sibling,
                                 stacks + landing(stacks, N_CHIPS, F32), len(keys))
        return dict(tag=tag, keys=keys, stacks=stacks, flight=flight)

    def reduce_chips(st):
        n, (sems, bufs) = len(st["keys"]), st["flight"]
        bufs = _exchange_wait(f"grad{st['tag']}_sibling_wait", _plan_grad_sibling, bufs, sems)
        st["own"] = list(zip(bufs[:n], bufs[n:]))
        st["flight"] = _exchange_start(f"grad{st['tag']}_chips_start", _plan_grad_chips,
                                       [_chip_sum(g, z, cm) for g, z in st["own"]] + landing(st["stacks"], 3, BF16),
                                       3 * n)

    def reduce_halves(st):
        n, (sems, bufs) = len(st["keys"]), st["flight"]
        bufs = _exchange_wait(f"grad{st['tag']}_chips_wait", _plan_grad_chips, bufs, sems)
        halves = [_total_sum(g, x, z, cm) for (g, x), z in zip(st["own"], bufs[n:])]
        st["flight"] = _exchange_start(f"grad{st['tag']}_halves_start", _plan_grad_halves, halves, n)

    def reduce_end(st):
        sems, bufs = st["flight"]
        bufs = _exchange_wait(f"grad{st['tag']}_halves_wait", _plan_grad_halves, bufs, sems)
        return {k: b.reshape(big_w[k].shape[1:]) for k, b in zip(st["keys"], bufs)}

    def adamw(reduced, l, prev):
        return {k: _adamw_layer(big_w[k], g, big_m[k], big_v[k], l, None if prev is None else prev[k])
                for k, g in reduced.items()}

    gsm = [dict() for _ in range(DEPTH)]
    dbias = jnp.zeros((8, BLK, 2 * BLK), F32)
    dh, gw1, gs = _bwd_ffn(dh, s1, w1, small, 1, 2)
    gsm[1].update(gs)
    dh, gw, gs, dbias = _bwd_mix(dh, s1, w1, small, 1, bias, dbias)
    gw1.update(gw)
    gsm[1].update(gs)
    dh, gw, gs = _bwd_ffn(dh, s1, w1, small, 1, 1)
    gw1.update(gw)
    gsm[1].update(gs)

    red1 = reduce_begin("1", BIG, gw1)
    dh, gw0, gs = _bwd_ffn(dh, s0, w0, small, 0, 2)
    gsm[0].update(gs)
    reduce_chips(red1)
    dh, gw, gs, dbias = _bwd_mix(dh, s0, w0, small, 0, bias, dbias)
    gw0.update(gw)
    gsm[0].update(gs)
    red0a = reduce_begin("0a", ("ffn2_gu", "ffn2_down", "w_out", "w_in"), gw0)
    reduce_halves(red1)
    dgu = _bwd_ffn_dact(dh, s0, w0, 1)
    reduce_chips(red0a)
    dh, gw, gs = _bwd_ffn_rest(dh, dgu, s0, w0, small, 0, 1)
    gsm[0].update(gs)
    red0b = reduce_begin("0b", ffn1, gw)
    reduced1 = reduce_end(red1)
    stacks = adamw({k: reduced1[k] for k in ffn1}, 1, None)

    gsmall = {k: jnp.stack([gsm[l][k].reshape(-1) for l in range(DEPTH)]) for k in gsm[0]}
    gsmall["rel_bias"] = jnp.transpose(_bias_grad(dbias, buckets)[:, :N_BUCKETS])
    gsmall["norm_final"] = dg_final.reshape(-1)
    small_like = [small[k] for k in SMALL]
    pk = lambda dct: _pack([dct[k] for k in SMALL])
    red = _small_allreduce(_pack([gsmall[k] for k in SMALL] + [loss_row[0, :1]]))
    gs = _unpack(red, small_like + [loss_row[0, :1]])
    loss = gs[-1][0]
    gs = dict(zip(SMALL, gs[:-1]))

    reduce_chips(red0b)
    stacks.update(adamw({k: reduced1[k] for k in mix_in + rest}, 1, None))
    dlt, m2, v2 = _adamw_small(pk(small), pk(gs), pk(small_m), pk(small_v))
    reduce_halves(red0a)
    stacks.update(adamw(reduce_end(red0a), 0, stacks))
    reduce_halves(red0b)
    stacks.update(adamw(reduce_end(red0b), 0, stacks))

    out_g, out_d, out_m, out_v = {}, {}, {}, {}
    for k in BIG:
        out_g[k], out_d[k], out_m[k], out_v[k] = [jnp.swapaxes(a, 1, 2) if k == "w_in" else a for a in stacks[k]]
    for dst, packed in ((out_d, dlt), (out_m, m2), (out_v, v2)):
        dst.update(zip(SMALL, _unpack(packed, small_like)))
    out_g.update(gs)

    order = ("norm_ffn1", "ffn1_gu", "ffn1_down", "norm_mix", "w_in", "sinks", "norm_out_sb", "norm_out_swa", "w_out",
             "norm_ffn2", "ffn2_gu", "ffn2_down", "rel_bias", "norm_final")
    return (loss, dh.reshape(x.shape), *[out_g[k] for k in order], *[out_d[k] for k in order],
            *[out_m[k] for k in order], *[out_v[k] for k in order])
```

```python
import math

import numpy as np
import jax
import jax.numpy as jnp
from jax import lax
from jax.experimental import pallas as pl
from jax.experimental.pallas import tpu as pltpu

F32 = jnp.float32
BF16 = jnp.bfloat16

D_MODEL = 1024
DEPTH = 2
HEAD_DIM = 64
BLK = 128
N_BUCKETS = 32
MAX_DISTANCE = 128
D_FF = 2816
EPS = 1e-6
NEG_INF = -1e30
SB_W = 512
SWA_W = 512
KV_W = 128
IN_W = 2304
SCALE = HEAD_DIM ** -0.5
N_CHIPS = 4
FS = 2 * D_FF // N_CHIPS
LANES = 128
V7X_VMEM_LIMIT = 56 * 2 ** 20
TM = 512
SB_KT = 512
SWA_G = 4

ADAM_LR = 0.001
ADAM_B1 = 0.9
ADAM_B2 = 0.999
ADAM_EPS = 1e-08
ADAM_WD = 0.01
ADAM_STEP = 10

MESH = pl.DeviceIdType.MESH
ANY = pl.BlockSpec(memory_space=pl.ANY)
HBM = pl.BlockSpec(memory_space=pltpu.HBM)
SEM = pl.BlockSpec(memory_space=pltpu.SEMAPHORE)
EFFECT = pltpu.SideEffectType.DATAFLOW_SIDE_EFFECTING


def _params(n_grid):
    return pltpu.CompilerParams(dimension_semantics=("arbitrary",) * n_grid, vmem_limit_bytes=V7X_VMEM_LIMIT)


_PREVIOUS = [None]


def _call(body, *, name, in_specs, out_specs, out_shape, grid=(), num_scalar_prefetch=0, scratch_shapes=(),
          input_output_aliases=None, compiler_params=None, hbm_args=0):
    n_in = len(in_specs)

    def run(*args):
        dep = _PREVIOUS[0]
        if any(dep is a for a in args):
            dep = None
        args = [pltpu.with_memory_space_constraint(a, pltpu.HBM) if i < hbm_args else a for i, a in enumerate(args)]
        specs = list(in_specs) + ([ANY] if dep is not None else [])
        k = num_scalar_prefetch + n_in
        fn = body if dep is None else (lambda *refs: body(*refs[:k], *refs[k + 1:]))
        if num_scalar_prefetch:
            shape = dict(grid_spec=pltpu.PrefetchScalarGridSpec(
                num_scalar_prefetch=num_scalar_prefetch, grid=grid, in_specs=specs, out_specs=out_specs,
                scratch_shapes=scratch_shapes))
        else:
            shape = dict(grid=grid, in_specs=specs, out_specs=out_specs, scratch_shapes=scratch_shapes)
        out = pl.pallas_call(fn, name=name, out_shape=out_shape, input_output_aliases=input_output_aliases or {},
                             compiler_params=compiler_params, **shape)(*args, *([] if dep is None else [dep]))
        _PREVIOUS[0] = jax.tree.leaves(out)[-1]
        return out

    return run


def _dot(a, b):
    return jnp.dot(a, b, preferred_element_type=F32)


def _dot_nt(a, b):
    return lax.dot_general(a, b, (((1,), (1,)), ((), ())), preferred_element_type=F32)


def _dot_tn(a, b):
    return lax.dot_general(a, b, (((0,), (0,)), ((), ())), preferred_element_type=F32)


def _rms_fwd(x, g):
    r = lax.rsqrt(jnp.mean(x * x, axis=-1, keepdims=True) + EPS)
    xh = x * r
    return xh * g, xh, r


def _rms_bwd(dy, xh, r, g):
    u = dy * g
    dx = r * (u - xh * jnp.mean(u * xh, axis=-1, keepdims=True))
    dg = jnp.sum(dy * xh, axis=0, keepdims=True)
    return dx, dg


def _softplus(z):
    neg_abs = lax.bitcast_convert_type(lax.bitcast_convert_type(z, jnp.int32) | jnp.int32(-2 ** 31), F32)
    sp = jnp.maximum(z, 0.0) + jnp.log(1.0 + jnp.exp(neg_abs))
    return sp, z - sp


def _norm_cast(h, g):
    t, w = h.shape

    def body(h_ref, g_ref, n_ref):
        y, _, _ = _rms_fwd(h_ref[...], g_ref[...])
        n_ref[...] = y.astype(BF16)

    return _call(
        body, name="norm_cast", grid=(t // TM,),
        in_specs=[pl.BlockSpec((TM, w), lambda i: (i, 0)), pl.BlockSpec((1, w), lambda i: (0, 0))],
        out_specs=pl.BlockSpec((TM, w), lambda i: (i, 0)),
        out_shape=jax.ShapeDtypeStruct((t, w), BF16), compiler_params=_params(1))(h, g)


def _ffn_gu(n, wgu):
    t, d = n.shape

    def body(n_ref, wg_ref, wu_ref, gu_ref, act_ref):
        x = n_ref[...]
        g = _dot(x, wg_ref[...])
        u = _dot(x, wu_ref[...])
        sig = jax.nn.sigmoid(g)
        silu = g * sig
        gu_ref[0] = (u * (sig + silu * (1.0 - sig))).astype(BF16)
        gu_ref[1] = silu.astype(BF16)
        act_ref[...] = (silu * u).astype(BF16)

    return _call(
        body, name="ffn_gu", grid=(2, t // TM),
        in_specs=[pl.BlockSpec((TM, d), lambda j, i: (i, 0)),
                  pl.BlockSpec((None, d, FS), lambda j, i: (j, 0, 0)),
                  pl.BlockSpec((None, d, FS), lambda j, i: (j + 2, 0, 0))],
        out_specs=[pl.BlockSpec((2, TM, FS), lambda j, i: (0, i, j)), pl.BlockSpec((TM, FS), lambda j, i: (i, j))],
        out_shape=[jax.ShapeDtypeStruct((2, t, D_FF), BF16), jax.ShapeDtypeStruct((t, D_FF), BF16)],
        compiler_params=_params(2))(n, wgu, wgu)


def _down_res(act, wdn, h, g_next):
    t, f = act.shape
    d = h.shape[1]

    def body(a_ref, w_ref, h_ref, g_ref, o_ref, n_ref):
        out = h_ref[...] + 0.5 * _dot(a_ref[...], w_ref[...])
        o_ref[...] = out
        n_ref[...] = _rms_fwd(out, g_ref[...])[0].astype(BF16)

    row = pl.BlockSpec((TM, d), lambda i: (i, 0))
    return _call(
        body, name="down_res", grid=(t // TM,),
        in_specs=[pl.BlockSpec((TM, f), lambda i: (i, 0)), pl.BlockSpec((f, d), lambda i: (0, 0)), row,
                  pl.BlockSpec((1, d), lambda i: (0, 0))],
        out_specs=[row, row],
        out_shape=[jax.ShapeDtypeStruct((t, d), F32), jax.ShapeDtypeStruct((t, d), BF16)],
        compiler_params=_params(1))(act, wdn, h, g_next)


def _proj(n, w_in_t):
    t, d = n.shape
    w = w_in_t.shape[0]

    def body(n_ref, w_ref, o_ref):
        o_ref[...] = _dot_nt(n_ref[...], w_ref[...]).astype(BF16)

    return _call(
        body, name="proj", grid=(t // TM,),
        in_specs=[pl.BlockSpec((TM, d), lambda i: (i, 0)), pl.BlockSpec((w, d), lambda i: (0, 0))],
        out_specs=pl.BlockSpec((TM, w), lambda i: (i, 0)),
        out_shape=jax.ShapeDtypeStruct((t, w), BF16), compiler_params=_params(1))(n, w_in_t)


def _out_res(o_sb, o_sw, g_sb, g_sw, w_out, h, g_next):
    t, d = h.shape

    def body(a_ref, b_ref, ga_ref, gb_ref, w_ref, h_ref, g_ref, o_ref, mix_ref, n_ref):
        ya, _, _ = _rms_fwd(a_ref[...], ga_ref[...])
        yb, _, _ = _rms_fwd(b_ref[...], gb_ref[...])
        mixed = jnp.concatenate([ya.astype(BF16), yb.astype(BF16)], axis=1)
        mix_ref[...] = mixed
        out = h_ref[...] + _dot(mixed, w_ref[...])
        o_ref[...] = out
        n_ref[...] = _rms_fwd(out, g_ref[...])[0].astype(BF16)

    row = pl.BlockSpec((TM, d), lambda i: (i, 0))
    return _call(
        body, name="out_res", grid=(t // TM,),
        in_specs=[pl.BlockSpec((TM, SB_W), lambda i: (i, 0)), pl.BlockSpec((TM, SWA_W), lambda i: (i, 0)),
                  pl.BlockSpec((1, SB_W), lambda i: (0, 0)), pl.BlockSpec((1, SWA_W), lambda i: (0, 0)),
                  pl.BlockSpec((d, d), lambda i: (0, 0)), row, pl.BlockSpec((1, d), lambda i: (0, 0))],
        out_specs=[row, row, row],
        out_shape=[jax.ShapeDtypeStruct((t, d), F32), jax.ShapeDtypeStruct((t, d), BF16),
                   jax.ShapeDtypeStruct((t, d), BF16)],
        compiler_params=_params(1))(o_sb, o_sw, g_sb, g_sw, w_out, h, g_next)


def _loss_head(h, g, tgt):
    t, d = h.shape

    def body(h_ref, g_ref, t_ref, dh_ref, dg_ref, loss_ref):
        @pl.when(pl.program_id(0) == 0)
        def _():
            dg_ref[...] = jnp.zeros_like(dg_ref)
            loss_ref[...] = jnp.zeros_like(loss_ref)

        gg = g_ref[...]
        y, xh, r = _rms_fwd(h_ref[...], gg)
        err = y - t_ref[...]
        part = 0.5 * jnp.sum(jnp.sum(err * err, axis=1, keepdims=True) / d, axis=0, keepdims=True)
        loss_ref[...] += jnp.broadcast_to(part, loss_ref.shape)
        dx, dg = _rms_bwd(err / d, xh, r, gg)
        dh_ref[...] = dx
        dg_ref[...] += dg

    return _call(
        body, name="loss_head", grid=(t // TM,),
        in_specs=[pl.BlockSpec((TM, d), lambda i: (i, 0)), pl.BlockSpec((1, d), lambda i: (0, 0)),
                  pl.BlockSpec((TM, d), lambda i: (i, 0))],
        out_specs=[pl.BlockSpec((TM, d), lambda i: (i, 0)), pl.BlockSpec((1, d), lambda i: (0, 0)),
                   pl.BlockSpec((1, LANES), lambda i: (0, 0))],
        out_shape=[jax.ShapeDtypeStruct((t, d), F32), jax.ShapeDtypeStruct((1, d), F32),
                   jax.ShapeDtypeStruct((1, LANES), F32)],
        compiler_params=_params(1))(h, g, tgt)


def _ffn_dact(dh, wdn, gu):
    t, d = dh.shape
    tm = TM

    def body(dh_ref, w_ref, gu_ref, o_ref):
        da = 0.5 * _dot_nt(dh_ref[...].astype(BF16), w_ref[...])
        o_ref[0] = (da * gu_ref[0].astype(F32)).astype(BF16)
        o_ref[1] = (da * gu_ref[1].astype(F32)).astype(BF16)

    return _call(
        body, name="ffn_dact", grid=(2, t // tm),
        in_specs=[pl.BlockSpec((tm, d), lambda j, i: (i, 0)), pl.BlockSpec((FS, d), lambda j, i: (j, 0)),
                  pl.BlockSpec((2, tm, FS), lambda j, i: (0, i, j))],
        out_specs=pl.BlockSpec((2, tm, FS), lambda j, i: (0, i, j)),
        out_shape=jax.ShapeDtypeStruct((2, t, D_FF), BF16), compiler_params=_params(2))(dh, wdn, gu)


def _dn_norm_bwd(a, a_spec, w, w_spec, nk, dh, h_in, g, w_transposed=False, tm=TM):
    t, d = dh.shape
    mm = _dot if w_transposed else _dot_nt

    def body(a_ref, w_ref, dh_ref, h_ref, g_ref, o_ref, dg_ref, acc_ref):
        i, k = pl.program_id(0), pl.program_id(1)

        if nk > 1:
            @pl.when(k == 0)
            def _():
                acc_ref[...] = mm(a_ref[...], w_ref[...])

            @pl.when((k > 0) & (k < nk - 1))
            def _():
                acc_ref[...] += mm(a_ref[...], w_ref[...])

        @pl.when(k == nk - 1)
        def _():
            gg = g_ref[...]
            dg = jnp.zeros_like(gg)
            for rows in (slice(r, r + TM // 2) for r in range(0, tm, TM // 2)):
                dn = mm(a_ref[rows, :], w_ref[...])
                if nk > 1:
                    dn = dn + acc_ref[rows, :]
                _, xh, r = _rms_fwd(h_ref[rows, :], gg)
                dx, dg_rows = _rms_bwd(dn, xh, r, gg)
                o_ref[rows, :] = dh_ref[rows, :] + dx
                dg = dg + dg_rows

            @pl.when(i == 0)
            def _():
                dg_ref[...] = dg

            @pl.when(i > 0)
            def _():
                dg_ref[...] += dg

    row = pl.BlockSpec((tm, d), lambda i, k: (i, 0))
    return _call(
        body, name="dn_norm_bwd", grid=(t // tm, nk),
        in_specs=[a_spec, w_spec, row, row, pl.BlockSpec((1, d), lambda i, k: (0, 0))],
        out_specs=[row, pl.BlockSpec((1, d), lambda i, k: (0, 0))],
        out_shape=[jax.ShapeDtypeStruct((t, d), F32), jax.ShapeDtypeStruct((1, d), F32)],
        scratch_shapes=[pltpu.VMEM((tm, d), F32)], compiler_params=_params(2))(a, w, dh, h_in, g)


def _ffn_dn(dgu, wgu, dh, h_in, g):
    d = dh.shape[1]
    tm = 2 * TM
    return _dn_norm_bwd(
        dgu, pl.BlockSpec((None, tm, FS), lambda i, k: (k // 2, i, k % 2)),
        wgu, pl.BlockSpec((None, d, FS), lambda i, k: (k, 0, 0)), N_CHIPS, dh, h_in, g, tm=tm)


def _mix_dn(dproj, w_in_t, dh, h_in, g):
    d = dh.shape[1]
    w = dproj.shape[1]
    return _dn_norm_bwd(
        dproj, pl.BlockSpec((TM, w), lambda i, k: (i, 0)),
        w_in_t, pl.BlockSpec((w, d), lambda i, k: (0, 0)), 1, dh, h_in, g, w_transposed=True)


def _dmixed(dh, w_out, o_sb, o_sw, g_sb, g_sw):
    t, d = dh.shape

    def body(dh_ref, w_ref, a_ref, b_ref, ga_ref, gb_ref, o_ref, dga_ref, dgb_ref):
        i = pl.program_id(0)
        dm = _dot_nt(dh_ref[...].astype(BF16), w_ref[...])
        _, xa, ra = _rms_fwd(a_ref[...], ga_ref[...])
        _, xb, rb = _rms_fwd(b_ref[...], gb_ref[...])
        da, dga = _rms_bwd(dm[:, :SB_W], xa, ra, ga_ref[...])
        db, dgb = _rms_bwd(dm[:, SB_W:], xb, rb, gb_ref[...])
        o_ref[...] = jnp.concatenate([da.astype(BF16), db.astype(BF16)], axis=1)

        @pl.when(i == 0)
        def _():
            dga_ref[...] = dga
            dgb_ref[...] = dgb

        @pl.when(i > 0)
        def _():
            dga_ref[...] += dga
            dgb_ref[...] += dgb

    return _call(
        body, name="dmixed", grid=(t // TM,),
        in_specs=[pl.BlockSpec((TM, d), lambda i: (i, 0)), pl.BlockSpec((d, d), lambda i: (0, 0)),
                  pl.BlockSpec((TM, SB_W), lambda i: (i, 0)), pl.BlockSpec((TM, SWA_W), lambda i: (i, 0)),
                  pl.BlockSpec((1, SB_W), lambda i: (0, 0)), pl.BlockSpec((1, SWA_W), lambda i: (0, 0))],
        out_specs=[pl.BlockSpec((TM, d), lambda i: (i, 0)), pl.BlockSpec((1, SB_W), lambda i: (0, 0)),
                   pl.BlockSpec((1, SWA_W), lambda i: (0, 0))],
        out_shape=[jax.ShapeDtypeStruct((t, d), BF16), jax.ShapeDtypeStruct((1, SB_W), F32),
                   jax.ShapeDtypeStruct((1, SWA_W), F32)],
        compiler_params=_params(1))(dh, w_out, o_sb, o_sw, g_sb, g_sw)


def _wgrad(name, a, a_spec, b, b_spec, grid, out_shape, out_spec, scale):
    def body(a_ref, b_ref, o_ref):
        r = _dot_tn(a_ref[...], b_ref[...].astype(BF16))
        o_ref[...] = r if scale == 1.0 else scale * r

    return _call(
        body, name=name, grid=grid, in_specs=[a_spec, b_spec], out_specs=out_spec,
        out_shape=jax.ShapeDtypeStruct(out_shape, F32), compiler_params=_params(len(grid)))(a, b)


def _wgrad_gu(n, dgu):
    t, d = n.shape
    return _wgrad(
        "wgrad_gu", n, pl.BlockSpec((t, TM), lambda s, r: (0, r)),
        dgu, pl.BlockSpec((None, t, FS), lambda s, r: (s // 2, 0, s % 2)), (N_CHIPS, d // TM),
        (N_CHIPS, d, FS), pl.BlockSpec((None, TM, FS), lambda s, r: (s, r, 0)), 1.0)


def _wgrad_down(act, dh):
    t, d = dh.shape
    return _wgrad(
        "wgrad_down", act, pl.BlockSpec((t, FS), lambda s, r: (0, s)), dh, pl.BlockSpec((t, TM), lambda s, r: (0, r)),
        (2, d // TM), (D_FF, d), pl.BlockSpec((FS, TM), lambda s, r: (s, r)), 0.5)


def _wgrad_out(mixed, dh):
    t, d = dh.shape
    return _wgrad(
        "wgrad_out", mixed, pl.BlockSpec((t, TM), lambda s: (0, s)), dh, pl.BlockSpec((t, d), lambda s: (0, 0)),
        (d // TM,), (d, d), pl.BlockSpec((TM, d), lambda s: (s, 0)), 1.0)


def _wgrad_in(n, dproj):
    t, d = n.shape
    w = dproj.shape[1]
    tw = w // 3
    return _wgrad(
        "wgrad_in", dproj, pl.BlockSpec((t, tw), lambda s: (0, s)), n, pl.BlockSpec((t, d), lambda s: (0, 0)),
        (3,), (w, d), pl.BlockSpec((tw, d), lambda s: (s, 0)), 1.0)


def _tri(rel):
    row = lax.broadcasted_iota(jnp.int32, (BLK, BLK), 0)
    col = lax.broadcasted_iota(jnp.int32, (BLK, BLK), 1)
    m = rel(row, col).astype(BF16)
    return jnp.concatenate([m, m], axis=0)


def _scan_dot(x, tri2):
    hi = x.astype(BF16)
    lo = (x - hi.astype(F32)).astype(BF16)
    return _dot(jnp.concatenate([hi, lo], axis=1), tri2)


def _head_masks():
    lane = lax.broadcasted_iota(jnp.int32, (1, LANES), 1)
    return [lane < HEAD_DIM, lane >= HEAD_DIM]


SB_PAIRS = 2
SB_ROWS = 2 * SB_PAIRS * BLK


def _sb_causal():
    row = lax.broadcasted_iota(jnp.int32, (SB_ROWS, BLK), 0) & (BLK - 1)
    return lax.broadcasted_iota(jnp.int32, (SB_ROWS, BLK), 1) < row


def _sb_mask_last(x, causal):
    own = jnp.where(causal, x[:, -BLK:], 0.0)
    return own if x.shape[1] == BLK else jnp.concatenate([x[:, :-BLK], own], axis=1)


def _sb_stack(x, hm):
    return jnp.concatenate([jnp.where(m, x[:, p * LANES:(p + 1) * LANES], jnp.zeros((BLK, LANES), x.dtype))
                            for p in range(SB_PAIRS) for m in hm], axis=0)


def _sb_unstack(y, hm):
    return jnp.concatenate([jnp.where(hm[0], y[2 * p * BLK:(2 * p + 1) * BLK], y[(2 * p + 1) * BLK:(2 * p + 2) * BLK])
                            for p in range(SB_PAIRS)], axis=1)


def _sb_pairs():
    return [(slice(2 * p * BLK, (2 * p + 2) * BLK), slice(p * LANES, (p + 1) * LANES)) for p in range(SB_PAIRS)]


def _sb_fwd(proj):
    t = proj.shape[0]
    nb = SB_KT // BLK
    wide = SB_PAIRS * LANES

    def body(q_ref, k_ref, v_ref, o_ref, tot_ref):
        hm = _head_masks()
        causal = _sb_causal()
        pairs = _sb_pairs()
        after = _tri(lambda r, c: r > c)

        def tile(qh, start, n_blk, carry, acc, own):
            ks = pl.ds(pl.multiple_of(start, BLK), n_blk * BLK)
            z = jnp.concatenate([_dot_nt(qh[rows], k_ref[ks, lanes]) for rows, lanes in pairs], axis=0)
            sp, zs = _softplus(z)
            spm = _sb_mask_last(sp, causal) if own else sp
            sufs = [None] * n_blk
            for b in reversed(range(n_blk)):
                blk = spm[:, b * BLK:(b + 1) * BLK]
                sufs[b] = carry + _scan_dot(blk, after)
                carry = carry + jnp.sum(blk, axis=1, keepdims=True)
            w = jnp.exp(zs - jnp.concatenate(sufs, axis=1))
            wb = (_sb_mask_last(w, causal) if own else w).astype(BF16)
            return carry, acc + jnp.concatenate([_dot(wb[rows], v_ref[ks, lanes]) for rows, lanes in pairs], axis=0)

        def qblock(g, j):
            qs = pl.ds(pl.multiple_of(g * SB_KT + j * BLK, BLK), BLK)
            qh = _sb_stack(q_ref[qs, :] * SCALE, hm)
            c0 = tile(qh, g * SB_KT, j + 1, jnp.zeros((SB_ROWS, 1), F32), jnp.zeros((SB_ROWS, LANES), F32), True)
            carry, acc = lax.fori_loop(0, g, lambda n, c: tile(qh, (g - 1 - n) * SB_KT, nb, c[0], c[1], False), c0)
            o_ref[qs, :] = _sb_unstack(acc, hm)
            for h in range(2 * SB_PAIRS):
                tot_ref[h, qs, :] = carry[h * BLK:(h + 1) * BLK]

        def group(g, _):
            for j in range(nb):
                qblock(g, j)
            return 0

        lax.fori_loop(0, t // SB_KT, group, 0)

    col_blk = lambda off: pl.BlockSpec((t, wide), lambda g: (0, off + g))
    n_steps = SB_W // wide
    return _call(
        body, name="sb_fwd", grid=(n_steps,), in_specs=[col_blk(0), col_blk(n_steps), col_blk(2 * n_steps)],
        out_specs=[col_blk(0), pl.BlockSpec((2 * SB_PAIRS, t, 1), lambda g: (g, 0, 0))],
        out_shape=[jax.ShapeDtypeStruct((t, SB_W), F32), jax.ShapeDtypeStruct((8, t, 1), F32)],
        compiler_params=_params(1))(proj, proj, proj)


def _sb_bwd(proj, d_o, tot):
    t = proj.shape[0]
    nb = SB_KT // BLK
    wide = SB_PAIRS * LANES

    def body(q_ref, k_ref, v_ref, do_ref, tot_ref, dq_ref, dk_ref, dv_ref, dk_acc, dv_acc):
        hm = _head_masks()
        causal = _sb_causal()
        pairs = _sb_pairs()
        before = _tri(lambda r, c: r < c)
        upto = _tri(lambda r, c: r <= c)
        dk_acc[...] = jnp.zeros_like(dk_acc)
        dv_acc[...] = jnp.zeros_like(dv_acc)

        def tile(qh, doh, tt, start, n_blk, pre, ecum, dq, own):
            ks = pl.ds(pl.multiple_of(start, BLK), n_blk * BLK)
            k = k_ref[ks, :]
            v = v_ref[ks, :]
            z = jnp.concatenate([_dot_nt(qh[rows], k[:, lanes]) for rows, lanes in pairs], axis=0)
            sp, zs = _softplus(z)
            spm = _sb_mask_last(sp, causal) if own else sp
            pres = []
            for b in range(n_blk):
                blk = spm[:, b * BLK:(b + 1) * BLK]
                pres.append(pre + _scan_dot(blk, before))
                pre = pre + jnp.sum(blk, axis=1, keepdims=True)
            logw = z - (tt - jnp.concatenate(pres, axis=1))
            if own:
                logw = jnp.minimum(logw, 0.0)
            w = jnp.exp(logw)
            if own:
                w = _sb_mask_last(w, causal)
            e = w * jnp.concatenate([_dot_nt(doh[rows], v[:, lanes]) for rows, lanes in pairs], axis=0)
            incs = []
            for b in range(n_blk):
                blk = e[:, b * BLK:(b + 1) * BLK]
                incs.append(ecum + _scan_dot(blk, upto))
                ecum = ecum + jnp.sum(blk, axis=1, keepdims=True)
            dz = e - jnp.exp(zs) * jnp.concatenate(incs, axis=1)
            if own:
                dz = _sb_mask_last(dz, causal)
            dzb = dz.astype(BF16)
            wb = w.astype(BF16)
            for rows, lanes in pairs:
                dk_acc[ks, lanes] += _dot_tn(dzb[rows], qh[rows])
                dv_acc[ks, lanes] += _dot_tn(wb[rows], doh[rows])
            return pre, ecum, dq + jnp.concatenate([_dot(dzb[rows], k[:, lanes]) for rows, lanes in pairs], axis=0)

        def qblock(g, j):
            qs = pl.ds(pl.multiple_of(g * SB_KT + j * BLK, BLK), BLK)
            qh = _sb_stack(q_ref[qs, :] * SCALE, hm)
            doh = _sb_stack(do_ref[qs, :], hm)
            tt = jnp.concatenate([tot_ref[h, qs, :] for h in range(2 * SB_PAIRS)], axis=0)
            c0 = (jnp.zeros((SB_ROWS, 1), F32), jnp.zeros((SB_ROWS, 1), F32), jnp.zeros((SB_ROWS, LANES), F32))
            c = lax.fori_loop(0, g, lambda kt, c: tile(qh, doh, tt, kt * SB_KT, nb, c[0], c[1], c[2], False), c0)
            dq = tile(qh, doh, tt, g * SB_KT, j + 1, c[0], c[1], c[2], True)[2]
            dq_ref[qs, :] = (_sb_unstack(dq, hm) * SCALE).astype(BF16)

        def group(g, _):
            for j in range(nb):
                qblock(g, j)
            return 0

        lax.fori_loop(0, t // SB_KT, group, 0)
        dk_ref[...] = dk_acc[...].astype(BF16)
        dv_ref[...] = dv_acc[...].astype(BF16)

    col_blk = lambda off: pl.BlockSpec((t, wide), lambda g: (0, off + g))
    n_steps = SB_W // wide
    out = jax.ShapeDtypeStruct((t, SB_W), BF16)
    return _call(
        body, name="sb_bwd", grid=(n_steps,),
        in_specs=[col_blk(0), col_blk(n_steps), col_blk(2 * n_steps), col_blk(0),
                  pl.BlockSpec((2 * SB_PAIRS, t, 1), lambda g: (g, 0, 0))],
        out_specs=[col_blk(0), col_blk(0), col_blk(0)], out_shape=[out, out, out],
        scratch_shapes=[pltpu.VMEM((t, wide), F32), pltpu.VMEM((t, wide), F32)],
        compiler_params=_params(1))(proj, proj, proj, d_o, tot)


def _bucket_table():
    a = np.arange(BLK)[:, None]
    c = np.arange(2 * BLK)[None, :]
    dist = np.maximum(BLK + a - c, 0)
    max_exact = N_BUCKETS // 2
    dd = np.maximum(dist, 1).astype(np.float32)
    large = max_exact + (np.log(dd / max_exact) / math.log(MAX_DISTANCE / max_exact)
                         * (N_BUCKETS - max_exact)).astype(np.int32)
    large = np.minimum(large, N_BUCKETS - 1)
    return np.where(dist < max_exact, dist, large).astype(np.int32)


SWA_H = 8


def _swa_band_masks():
    row = lax.broadcasted_iota(jnp.int32, (SWA_H * BLK, 2 * BLK), 0) & (BLK - 1)
    col = lax.broadcasted_iota(jnp.int32, (SWA_H * BLK, 2 * BLK), 1)
    own = lax.broadcasted_iota(jnp.int32, (SWA_H * BLK, BLK), 1) <= (
        lax.broadcasted_iota(jnp.int32, (SWA_H * BLK, BLK), 0) & (BLK - 1))
    return (col > row) & ((col < BLK) | (col - BLK <= row)), own


def _swa_stack(ref, qs, hm, scale):
    parts = []
    for hq in range(SWA_H):
        kvh = hq // SWA_G
        x = ref[qs, (hq // 2) * LANES:(hq // 2 + 1) * LANES].astype(F32)
        if hq % 2 != kvh:
            x = pltpu.roll(x, HEAD_DIM, 1)
        parts.append(jnp.where(hm[kvh], x * scale, 0.0).astype(BF16))
    return jnp.concatenate(parts, axis=0)


def _swa_unstack(x8, hm):
    heads = []
    for hq in range(SWA_H):
        x = x8[hq * BLK:(hq + 1) * BLK]
        heads.append(pltpu.roll(x, HEAD_DIM, 1) if hq % 2 != hq // SWA_G else x)
    return [jnp.where(hm[0], heads[2 * p], heads[2 * p + 1]) for p in range(SWA_H // 2)]


def _swa_scores(q8, kb, bias_ref, mask, cols):
    bias8 = jnp.concatenate([bias_ref[hq, :, cols] for hq in range(SWA_H)], axis=0)
    return jnp.where(mask, _dot_nt(q8, kb) + bias8, NEG_INF)


def _swa_sinks(sink_ref):
    return jnp.concatenate([jnp.broadcast_to(sink_ref[hq:hq + 1, 0:1], (BLK, 1)) for hq in range(SWA_H)], axis=0)


def _swa_fwd(proj, bias, sinks_b):
    t = proj.shape[0]
    nq = t // BLK

    def body(q_ref, k_ref, v_ref, bias_ref, sink_ref, o_ref, lse_ref):
        hm = _head_masks()
        band, own = _swa_band_masks()

        def qblock(i, prev):
            qs = pl.ds(pl.multiple_of(i * BLK, BLK), BLK)
            if prev:
                ks, mask, cols = pl.ds(pl.multiple_of((i - 1) * BLK, BLK), 2 * BLK), band, slice(None)
            else:
                ks, mask, cols = qs, own, slice(BLK, None)
            q8 = _swa_stack(q_ref, qs, hm, SCALE)
            sink8 = _swa_sinks(sink_ref)
            s = _swa_scores(q8, k_ref[ks, :], bias_ref, mask, cols)
            m = jnp.maximum(jnp.max(s, axis=1, keepdims=True), sink8)
            p = jnp.exp(s - m)
            den = jnp.sum(p, axis=1, keepdims=True) + jnp.exp(sink8 - m)
            o8 = _dot((p * (1.0 / den)).astype(BF16), v_ref[ks, :])
            lse8 = m + jnp.log(den)
            for hq in range(SWA_H):
                lse_ref[hq, qs, :] = lse8[hq * BLK:(hq + 1) * BLK]
            for pp, o in enumerate(_swa_unstack(o8, hm)):
                o_ref[qs, pp * LANES:(pp + 1) * LANES] = o

        qblock(0, False)

        def step(i, _):
            qblock(i, True)
            return 0

        lax.fori_loop(1, nq, step, 0)

    return _call(
        body, name="swa_fwd", grid=(1,),
        in_specs=[pl.BlockSpec((t, SWA_W), lambda i: (0, 3)), pl.BlockSpec((t, KV_W), lambda i: (0, 16)),
                  pl.BlockSpec((t, KV_W), lambda i: (0, 17)), pl.BlockSpec((8, BLK, 2 * BLK), lambda i: (0, 0, 0)),
                  pl.BlockSpec((8, LANES), lambda i: (0, 0))],
        out_specs=[pl.BlockSpec((t, SWA_W), lambda i: (0, 0)), pl.BlockSpec((8, t, 1), lambda i: (0, 0, 0))],
        out_shape=[jax.ShapeDtypeStruct((t, SWA_W), F32), jax.ShapeDtypeStruct((8, t, 1), F32)],
        compiler_params=_params(1))(proj, proj, proj, bias, sinks_b)


def _swa_bwd(proj, d_o, lse, bias, sinks_b, dbias_in):
    t = proj.shape[0]
    nq = t // BLK

    def body(q_ref, k_ref, v_ref, do_ref, lse_ref, bias_ref, sink_ref, dbi_ref,
             dq_ref, dk_ref, dv_ref, dsink_ref, dbias_ref, dk_acc, dv_acc):
        hm = _head_masks()
        band, own = _swa_band_masks()
        dk_acc[...] = jnp.zeros_like(dk_acc)
        dv_acc[...] = jnp.zeros_like(dv_acc)
        dbias_ref[...] = dbi_ref[...]

        def qblock(i, prev, dsink8):
            qs = pl.ds(pl.multiple_of(i * BLK, BLK), BLK)
            if prev:
                ks, mask, cols = pl.ds(pl.multiple_of((i - 1) * BLK, BLK), 2 * BLK), band, slice(None)
            else:
                ks, mask, cols = qs, own, slice(BLK, None)
            q8 = _swa_stack(q_ref, qs, hm, SCALE)
            do8 = _swa_stack(do_ref, qs, hm, 1.0)
            sink8 = _swa_sinks(sink_ref)
            lse8 = jnp.concatenate([lse_ref[hq, qs, :] for hq in range(SWA_H)], axis=0)
            kb = k_ref[ks, :]
            p = jnp.exp(_swa_scores(q8, kb, bias_ref, mask, cols) - lse8)
            dp = _dot_nt(do8, v_ref[ks, :])
            delta = jnp.sum(p * dp, axis=1, keepdims=True)
            ds = p * (dp - delta)
            for hq in range(SWA_H):
                dbias_ref[hq, :, cols] += ds[hq * BLK:(hq + 1) * BLK]
            dsb = ds.astype(BF16)
            dk_acc[ks, :] += _dot_tn(dsb, q8)
            dv_acc[ks, :] += _dot_tn(p.astype(BF16), do8)
            for pp, dq in enumerate(_swa_unstack(_dot(dsb, kb) * SCALE, hm)):
                dq_ref[qs, pp * LANES:(pp + 1) * LANES] = dq.astype(BF16)
            return dsink8 - jnp.exp(sink8 - lse8) * delta

        ds0 = qblock(0, False, jnp.zeros((SWA_H * BLK, 1), F32))
        ds8 = lax.fori_loop(1, nq, lambda i, c: qblock(i, True, c), ds0)
        for hq in range(SWA_H):
            dsink_ref[hq:hq + 1, :] = jnp.broadcast_to(
                jnp.sum(ds8[hq * BLK:(hq + 1) * BLK], axis=0, keepdims=True), (1, LANES))

        dk_ref[...] = dk_acc[...].astype(BF16)
        dv_ref[...] = dv_acc[...].astype(BF16)

    full3 = pl.BlockSpec((8, BLK, 2 * BLK), lambda i: (0, 0, 0))
    kv = jax.ShapeDtypeStruct((t, KV_W), BF16)
    return _call(
        body, name="swa_bwd", grid=(1,),
        in_specs=[pl.BlockSpec((t, SWA_W), lambda i: (0, 3)), pl.BlockSpec((t, KV_W), lambda i: (0, 16)),
                  pl.BlockSpec((t, KV_W), lambda i: (0, 17)), pl.BlockSpec((t, SWA_W), lambda i: (0, 1)),
                  pl.BlockSpec((8, t, 1), lambda i: (0, 0, 0)), full3, pl.BlockSpec((8, LANES), lambda i: (0, 0)),
                  full3],
        out_specs=[pl.BlockSpec((t, SWA_W), lambda i: (0, 0)), pl.BlockSpec((t, KV_W), lambda i: (0, 0)),
                   pl.BlockSpec((t, KV_W), lambda i: (0, 0)), pl.BlockSpec((8, LANES), lambda i: (0, 0)), full3],
        out_shape=[jax.ShapeDtypeStruct((t, SWA_W), BF16), kv, kv, jax.ShapeDtypeStruct((8, LANES), F32),
                   jax.ShapeDtypeStruct((8, BLK, 2 * BLK), F32)],
        scratch_shapes=[pltpu.VMEM((t, KV_W), F32), pltpu.VMEM((t, KV_W), F32)],
        compiler_params=_params(1))(proj, proj, proj, d_o, lse, bias, sinks_b, dbias_in)


def _concat_cols(parts):
    t = parts[0].shape[0]
    widths = [a.shape[1] for a in parts]

    def body(*refs):
        refs[-1][...] = jnp.concatenate([r[...] for r in refs[:-1]], axis=1)

    return _call(
        body, name="concat_cols", grid=(t // TM,),
        in_specs=[pl.BlockSpec((TM, w), lambda i: (i, 0)) for w in widths],
        out_specs=pl.BlockSpec((TM, sum(widths)), lambda i: (i, 0)),
        out_shape=jax.ShapeDtypeStruct((t, sum(widths)), parts[0].dtype), compiler_params=_params(1))(*parts)


def _bias_table(rel_bias, buckets):
    def body(rb_ref, b_ref, o_ref):
        bk = b_ref[...]
        for h in range(8):
            acc = jnp.zeros((BLK, 2 * BLK), F32)
            for b in range(N_BUCKETS):
                acc = jnp.where(bk == b, rb_ref[b, h], acc)
            o_ref[h] = acc

    return _call(
        body, name="bias_table", grid=(1,),
        in_specs=[pl.BlockSpec(memory_space=pltpu.SMEM), pl.BlockSpec((BLK, 2 * BLK), lambda i: (0, 0))],
        out_specs=pl.BlockSpec((8, BLK, 2 * BLK), lambda i: (0, 0, 0)),
        out_shape=jax.ShapeDtypeStruct((8, BLK, 2 * BLK), F32), compiler_params=_params(1))(rel_bias, buckets)


def _bias_grad(dbias, buckets):
    def body(d_ref, b_ref, o_ref):
        lane = lax.broadcasted_iota(jnp.int32, (1, LANES), 1)
        bk = b_ref[...]
        for h in range(8):
            d = d_ref[h]
            acc = jnp.zeros((1, LANES), F32)
            for b in range(N_BUCKETS):
                s = jnp.sum(jnp.sum(jnp.where(bk == b, d, 0.0), axis=0, keepdims=True), axis=1, keepdims=True)
                acc = acc + jnp.where(lane == b, s, 0.0)
            o_ref[h:h + 1, :] = acc

    return _call(
        body, name="bias_grad", grid=(1,),
        in_specs=[pl.BlockSpec((8, BLK, 2 * BLK), lambda i: (0, 0, 0)), pl.BlockSpec((BLK, 2 * BLK), lambda i: (0, 0))],
        out_specs=pl.BlockSpec((8, LANES), lambda i: (0, 0)),
        out_shape=jax.ShapeDtypeStruct((8, LANES), F32), compiler_params=_params(1))(dbias, buckets)


def _row(a):
    return a.reshape(1, -1)


def _fwd_ffn1(h, n1, w, small, l):
    s = {"h0": h, "n1": n1}
    s["gu1"], s["act1"] = _ffn_gu(n1, w["ffn1_gu"])
    s["h1"], s["nm"] = _down_res(s["act1"], w["ffn1_down"], h, _row(small["norm_mix"][l]))
    return s


def _fwd_proj_sb(s, w):
    s["proj"] = _proj(s["nm"], w["w_in"])
    s["o_sb"], s["tot"] = _sb_fwd(s["proj"])


def _fwd_swa(s, small, l, bias):
    s["sinks_b"] = jnp.broadcast_to(small["sinks"][l][:, None], (8, LANES))
    s["o_sw"], s["lse"] = _swa_fwd(s["proj"], bias, s["sinks_b"])


def _fwd_out_ffn2(s, w, small, l, g_after):
    s["h2"], s["mixed"], s["n2"] = _out_res(
        s["o_sb"], s["o_sw"], _row(small["norm_out_sb"][l]), _row(small["norm_out_swa"][l]), w["w_out"], s["h1"],
        _row(small["norm_ffn2"][l]))
    s["gu2"], s["act2"] = _ffn_gu(s["n2"], w["ffn2_gu"])
    return _down_res(s["act2"], w["ffn2_down"], s["h2"], g_after)


def _bwd_ffn_dact(dh, s, w, which):
    return _ffn_dact(dh, w[f"ffn{which}_down"], s[f"gu{which}"])


def _bwd_ffn_rest(dh, dgu, s, w, small, l, which):
    h_in, norm = (s["h0"], "norm_ffn1") if which == 1 else (s["h2"], "norm_ffn2")
    g_down = _wgrad_down(s[f"act{which}"], dh)
    g_gu = _wgrad_gu(s[f"n{which}"], dgu)
    dh, dg = _ffn_dn(dgu, w[f"ffn{which}_gu"], dh, h_in, _row(small[norm][l]))
    return dh, {f"ffn{which}_down": g_down, f"ffn{which}_gu": g_gu}, {norm: dg}


def _bwd_ffn(dh, s, w, small, l, which):
    return _bwd_ffn_rest(dh, _bwd_ffn_dact(dh, s, w, which), s, w, small, l, which)


def _bwd_mix(dh, s, w, small, l, bias, dbias):
    g_out = _wgrad_out(s["mixed"], dh)
    d_o, dg_sb, dg_sw = _dmixed(dh, w["w_out"], s["o_sb"], s["o_sw"], _row(small["norm_out_sb"][l]),
                                _row(small["norm_out_swa"][l]))
    dq_sb, dk_sb, dv_sb = _sb_bwd(s["proj"], d_o, s["tot"])
    dq_sw, dk_sw, dv_sw, dsink, dbias = _swa_bwd(s["proj"], d_o, s["lse"], bias, s["sinks_b"], dbias)
    dproj = _concat_cols([dq_sb, dk_sb, dv_sb, dq_sw, dk_sw, dv_sw])
    g_in = _wgrad_in(s["nm"], dproj)
    dh, dg_mix = _mix_dn(dproj, w["w_in"], dh, s["h1"], _row(small["norm_mix"][l]))
    gs = {"norm_out_sb": dg_sb, "norm_out_swa": dg_sw, "sinks": dsink[:, 0], "norm_mix": dg_mix}
    return dh, {"w_out": g_out, "w_in": g_in}, gs, dbias


def _place():
    x, y, c = lax.axis_index("x"), lax.axis_index("y"), lax.axis_index("c")
    return x, y, c, 2 * x + y


def _chip_core(k, c):
    return (k // 2, k % 2, c)


def _rows_per_block(rows, cols, copies):
    best = 16
    for tr in range(16, rows + 1, 16):
        if rows % tr == 0 and copies * tr * cols * 4 <= 6 * 2 ** 20:
            best = tr
    assert rows % best == 0
    return best


def _place_own(w, l, me1):
    _, rows, cols = w.shape
    tr = _rows_per_block(rows // 2, cols, 1)
    per_half = rows // 2 // tr

    def body(me_ref, w_ref, o_ref):
        o_ref[...] = w_ref[...].astype(BF16)

    return _call(
        body, name="place_own",
        num_scalar_prefetch=1, grid=(rows // tr,),
        in_specs=[pl.BlockSpec((None, tr, cols), lambda r, me: (l, r, 0))],
        out_specs=pl.BlockSpec((None, None, tr, cols), lambda r, me: (me[0], r // per_half, r % per_half, 0)),
        out_shape=jax.ShapeDtypeStruct((N_CHIPS, 2, rows // 2, cols), BF16), compiler_params=_params(1))(me1, w)


def _plan_gather_ici(bufs):
    _, _, c, me = _place()
    return [(b.at[me, c], b.at[me, c], b.at[(me + 3 - j) % N_CHIPS, c], _chip_core((me + 1 + j) % N_CHIPS, c))
            for b in bufs for j in range(3)]


def _plan_gather_d2d(bufs):
    x, y, c, me = _place()
    return [(b.at[(me + 3 - j) % N_CHIPS, c], b.at[(me + 3 - j) % N_CHIPS, c], b.at[(me + 3 - j) % N_CHIPS, 1 - c],
             (x, y, 1 - c)) for b in bufs for j in range(3)]


def _plan_grad_sibling(bufs):
    x, y, c, _ = _place()
    n = len(bufs) // 2
    return [(g.at[:, 1 - c], z, z, (x, y, 1 - c)) for g, z in zip(bufs[:n], bufs[n:])]


def _plan_grad_chips(bufs):
    _, _, c, me = _place()
    n = len(bufs) // 2
    return [(p.at[j], z.at[j], z.at[j], _chip_core((me + 1 + j) % N_CHIPS, c))
            for p, z in zip(bufs[:n], bufs[n:]) for j in range(3)]


def _plan_grad_halves(bufs):
    x, y, c, _ = _place()
    return [(b.at[c], b.at[c], b.at[1 - c], (x, y, 1 - c)) for b in bufs]


def _remote(src, dst, send_sem, recv_sem, to):
    return pltpu.make_async_remote_copy(src_ref=src, dst_ref=dst, send_sem=send_sem, recv_sem=recv_sem,
                                        device_id=to, device_id_type=MESH)


def _exchange_start(name, plan, bufs, n_copies):
    n = len(bufs)

    def body(*refs):
        ins = refs[:n]
        ssem, rsem = refs[n], refs[n + 1]
        token = refs[-1]
        for i, (src, dst, _, to) in enumerate(plan(ins)):
            _remote(src, dst, ssem.at[i], rsem.at[i], to).start()
        token[...] = jnp.zeros_like(token)

    out = _call(
        body, name=name,
        out_shape=(pltpu.SemaphoreType.DMA((n_copies,)), pltpu.SemaphoreType.DMA((n_copies,)),
                   *[pltpu.HBM(a.shape, a.dtype) for a in bufs], jax.ShapeDtypeStruct((8, LANES), F32)),
        in_specs=[HBM] * n, out_specs=(SEM, SEM, *[HBM] * n, pl.BlockSpec(memory_space=pltpu.VMEM)),
        input_output_aliases={t: 2 + t for t in range(n)}, hbm_args=n,
        compiler_params=pltpu.CompilerParams(has_side_effects=EFFECT),
    )(*bufs)
    return (out[0], out[1]), list(out[2:2 + n])


def _exchange_wait(name, plan, bufs, sems):
    n = len(bufs)

    def body(*refs):
        ins = refs[:n]
        ssem, rsem = refs[n], refs[n + 1]
        for i, (src, dst, land, to) in enumerate(plan(ins)):
            _remote(src, dst, ssem.at[i], rsem.at[i], to).wait_send()
            _remote(land, land, ssem.at[i], rsem.at[i], to).wait_recv()

    return list(_call(
        body, name=name, out_shape=[pltpu.HBM(a.shape, a.dtype) for a in bufs],
        in_specs=[HBM] * n + [SEM, SEM], out_specs=[HBM] * n,
        input_output_aliases={t: t for t in range(n)},
        compiler_params=pltpu.CompilerParams(has_side_effects=EFFECT),
    )(*bufs, sems[0], sems[1]))


def _exchange_pass(name, done, plan, bufs, sems, n_copies):
    n = len(bufs)

    def body(*refs):
        ins = refs[:n]
        old_s, old_r, ssem, rsem = refs[n], refs[n + 1], refs[n + 2], refs[n + 3]
        token = refs[-1]
        for i, (src, dst, land, to) in enumerate(done(ins)):
            _remote(src, dst, old_s.at[i], old_r.at[i], to).wait_send()
            _remote(land, land, old_s.at[i], old_r.at[i], to).wait_recv()
        for i, (src, dst, _, to) in enumerate(plan(ins)):
            _remote(src, dst, ssem.at[i], rsem.at[i], to).start()
        token[...] = jnp.zeros_like(token)

    out = _call(
        body, name=name,
        out_shape=(pltpu.SemaphoreType.DMA((n_copies,)), pltpu.SemaphoreType.DMA((n_copies,)),
                   *[pltpu.HBM(a.shape, a.dtype) for a in bufs], jax.ShapeDtypeStruct((8, LANES), F32)),
        in_specs=[HBM] * n + [SEM, SEM], out_specs=(SEM, SEM, *[HBM] * n, pl.BlockSpec(memory_space=pltpu.VMEM)),
        input_output_aliases={t: 2 + t for t in range(n)},
        compiler_params=pltpu.CompilerParams(has_side_effects=EFFECT),
    )(*bufs, sems[0], sems[1])
    return (out[0], out[1]), list(out[2:2 + n])


def _chip_sum(g, xbuf, cm):
    _, _, r2, cols = g.shape
    tr = _rows_per_block(r2, cols, 1)

    def body(cm_ref, g_ref, x_ref, o_ref):
        o_ref[...] = (g_ref[...] + x_ref[...]).astype(BF16)

    return _call(
        body, name="grad_chip_sum",
        num_scalar_prefetch=1, grid=(3, r2 // tr),
        in_specs=[pl.BlockSpec((None, None, tr, cols), lambda j, r, cm: ((cm[1] + 1 + j) % N_CHIPS, cm[0], r, 0)),
                  pl.BlockSpec((None, tr, cols), lambda j, r, cm: ((cm[1] + 1 + j) % N_CHIPS, r, 0))],
        out_specs=pl.BlockSpec((None, tr, cols), lambda j, r, cm: (j, r, 0)),
        out_shape=jax.ShapeDtypeStruct((3, r2, cols), BF16), compiler_params=_params(2))(cm, g, xbuf)


def _total_sum(g, xbuf, rbuf, cm):
    _, _, r2, cols = g.shape
    tr = _rows_per_block(r2, cols, 3)

    def body(cm_ref, g_ref, x_ref, r_ref, o_ref):
        acc = g_ref[...] + x_ref[...]
        for j in range(3):
            acc = acc + r_ref[j].astype(F32)
        o_ref[...] = acc

    return _call(
        body, name="grad_total_sum",
        num_scalar_prefetch=1, grid=(r2 // tr,),
        in_specs=[pl.BlockSpec((None, None, tr, cols), lambda r, cm: (cm[1], cm[0], r, 0)),
                  pl.BlockSpec((None, tr, cols), lambda r, cm: (cm[1], r, 0)),
                  pl.BlockSpec((3, tr, cols), lambda r, cm: (0, r, 0))],
        out_specs=pl.BlockSpec((None, tr, cols), lambda r, cm: (cm[0], r, 0)),
        out_shape=jax.ShapeDtypeStruct((2, r2, cols), F32), compiler_params=_params(1))(cm, g, xbuf, rbuf)


def _small_allreduce(v):
    rows = v.shape[0]
    n_dev = 2 * N_CHIPS

    def body(v_ref, o_ref, buf, ssem, rsem):
        x, y, c, _ = _place()
        me = 4 * x + 2 * y + c
        buf[me] = v_ref[...]

        def copy(d, slot, to):
            return _remote(v_ref, buf.at[slot], ssem.at[d - 1], rsem.at[d - 1], (to // 4, (to // 2) % 2, to % 2))

        cps = [copy(d, me, (me + d) % n_dev) for d in range(1, n_dev)]
        for cp in cps:
            cp.start()
        for d in range(1, n_dev):
            copy(d, (me + n_dev - d) % n_dev, me).wait_recv()
        for cp in cps:
            cp.wait_send()
        acc = buf[0]
        for i in range(1, n_dev):
            acc = acc + buf[i]
        o_ref[...] = acc

    vm = pl.BlockSpec(memory_space=pltpu.VMEM)
    return _call(
        body, name="small_allreduce", in_specs=[vm], out_specs=vm,
        out_shape=jax.ShapeDtypeStruct(v.shape, F32),
        scratch_shapes=[pltpu.VMEM((n_dev, rows, LANES), F32), pltpu.SemaphoreType.DMA((n_dev - 1,)),
                        pltpu.SemaphoreType.DMA((n_dev - 1,))],
        compiler_params=pltpu.CompilerParams(vmem_limit_bytes=V7X_VMEM_LIMIT))(v)


def _adamw_math(w, g, m, v):
    m2 = ADAM_B1 * m + (1.0 - ADAM_B1) * g
    v2 = ADAM_B2 * v + (1.0 - ADAM_B2) * (g * g)
    m_hat = m2 / (1.0 - ADAM_B1 ** ADAM_STEP)
    v_hat = v2 / (1.0 - ADAM_B2 ** ADAM_STEP)
    return -ADAM_LR * (m_hat / (jnp.sqrt(v_hat) + ADAM_EPS) + ADAM_WD * w), m2, v2


def _adamw_layer(w, g, m, v, l, prev):
    _, rows, cols = w.shape
    tr = rows
    for cand in range(8, rows + 1, 8):
        if rows % cand == 0 and cand * cols * 4 <= 2 ** 21:
            tr = cand

    def body(w_ref, g_ref, m_ref, v_ref, *outs):
        go_ref, d_ref, m2_ref, v2_ref = outs[-4:]
        g = g_ref[...]
        go_ref[...] = g
        d_ref[...], m2_ref[...], v2_ref[...] = _adamw_math(w_ref[...], g, m_ref[...], v_ref[...])

    stack = pl.BlockSpec((None, tr, cols), lambda i: (l, i, 0))
    ins, specs, alias = [w, g, m, v], [stack, pl.BlockSpec((tr, cols), lambda i: (i, 0)), stack, stack], {}
    if prev is not None:
        ins += list(prev)
        specs += [ANY] * 4
        alias = {4 + i: i for i in range(4)}
    return _call(
        body, name="adamw", grid=(rows // tr,), in_specs=specs, out_specs=[stack] * 4,
        out_shape=[jax.ShapeDtypeStruct(w.shape, F32)] * 4, input_output_aliases=alias,
        compiler_params=_params(1))(*ins)


def _adamw_small(w, g, m, v):
    def body(w_ref, g_ref, m_ref, v_ref, d_ref, m2_ref, v2_ref):
        d_ref[...], m2_ref[...], v2_ref[...] = _adamw_math(w_ref[...], g_ref[...], m_ref[...], v_ref[...])

    spec = pl.BlockSpec(w.shape, lambda i: (0, 0))
    return _call(
        body, name="adamw_small", grid=(1,), in_specs=[spec] * 4, out_specs=[spec] * 3,
        out_shape=[jax.ShapeDtypeStruct(w.shape, F32)] * 3, compiler_params=_params(1))(w, g, m, v)


SMALL = ("norm_ffn1", "norm_mix", "sinks", "norm_out_sb", "norm_out_swa", "norm_ffn2", "rel_bias", "norm_final")
BIG = ("ffn1_gu", "ffn1_down", "w_in", "w_out", "ffn2_gu", "ffn2_down")


def _pack(parts):
    flat, n = [], 0
    for a in parts:
        a = a.reshape(-1).astype(F32)
        gap = -a.shape[0] % LANES
        flat += [a] + ([jnp.zeros((gap,), F32)] if gap else [])
        n += a.shape[0] + gap
    tail = -(n // LANES) % 8 * LANES
    return jnp.concatenate(flat + ([jnp.zeros((tail,), F32)] if tail else [])).reshape(-1, LANES)


def _unpack(packed, like):
    out, r = [], 0
    for a in like:
        n = math.prod(a.shape)
        nr = -(-n // LANES)
        out.append(packed[r:r + nr].reshape(-1)[:n].reshape(a.shape))
        r += nr
    return out


def _halved(a):
    k, r, cols = a.shape
    return a.reshape(k, 2, r // 2, cols)


def _weight_view(k, buf):
    full = buf.reshape(N_CHIPS, buf.shape[2] * 2, buf.shape[3])
    return full if k.endswith("_gu") else full.reshape(-1, D_MODEL)


def _grad_stack(k, g):
    if not k.endswith("_gu"):
        g = g.reshape(N_CHIPS, g.shape[0] // N_CHIPS, D_MODEL)
    return _halved(g)


def _empty_like_hbm(shape, dtype):
    return pltpu.with_memory_space_constraint(lax.empty(shape, dtype), pltpu.HBM)


def kernel(x, norm_ffn1, w_ffn1_gu, w_ffn1_down, norm_mix, w_in, sinks, norm_out_sb, norm_out_swa, w_out, norm_ffn2, w_ffn2_gu, w_ffn2_down, rel_bias, norm_final, loss_target, m_norm_ffn1, m_w_ffn1_gu, m_w_ffn1_down, m_norm_mix, m_w_in, m_sinks, m_norm_out_sb, m_norm_out_swa, m_w_out, m_norm_ffn2, m_w_ffn2_gu, m_w_ffn2_down, m_rel_bias, m_norm_final, v_norm_ffn1, v_w_ffn1_gu, v_w_ffn1_down, v_norm_mix, v_w_in, v_sinks, v_norm_out_sb, v_norm_out_swa, v_w_out, v_norm_ffn2, v_w_ffn2_gu, v_w_ffn2_down, v_rel_bias, v_norm_final):
    big_w = dict(ffn1_gu=w_ffn1_gu, ffn1_down=w_ffn1_down, w_in=w_in, w_out=w_out, ffn2_gu=w_ffn2_gu, ffn2_down=w_ffn2_down)
    big_m = dict(ffn1_gu=m_w_ffn1_gu, ffn1_down=m_w_ffn1_down, w_in=m_w_in, w_out=m_w_out, ffn2_gu=m_w_ffn2_gu, ffn2_down=m_w_ffn2_down)
    big_v = dict(ffn1_gu=v_w_ffn1_gu, ffn1_down=v_w_ffn1_down, w_in=v_w_in, w_out=v_w_out, ffn2_gu=v_w_ffn2_gu, ffn2_down=v_w_ffn2_down)
    small = dict(norm_ffn1=norm_ffn1, norm_mix=norm_mix, sinks=sinks, norm_out_sb=norm_out_sb, norm_out_swa=norm_out_swa,
                 norm_ffn2=norm_ffn2, rel_bias=rel_bias, norm_final=norm_final)
    small_m = dict(norm_ffn1=m_norm_ffn1, norm_mix=m_norm_mix, sinks=m_sinks, norm_out_sb=m_norm_out_sb,
                   norm_out_swa=m_norm_out_swa, norm_ffn2=m_norm_ffn2, rel_bias=m_rel_bias, norm_final=m_norm_final)
    small_v = dict(norm_ffn1=v_norm_ffn1, norm_mix=v_norm_mix, sinks=v_sinks, norm_out_sb=v_norm_out_sb,
                   norm_out_swa=v_norm_out_swa, norm_ffn2=v_norm_ffn2, rel_bias=v_rel_bias, norm_final=v_norm_final)
    for dct in (big_w, big_m, big_v):
        dct["w_in"] = jnp.swapaxes(dct["w_in"], 1, 2)
    _PREVIOUS[0] = None
    _, _, c, me = _place()
    cm = jnp.stack([c, me]).astype(jnp.int32)
    buckets = jnp.asarray(_bucket_table())
    ffn1, mix_in, rest = ("ffn1_gu", "ffn1_down"), ("w_in",), ("w_out", "ffn2_gu", "ffn2_down")

    def place(l, keys):
        return [_place_own(big_w[k], l, cm[1:]) for k in keys]

    def views(keys, bufs):
        return {k: _weight_view(k, b) for k, b in zip(keys, bufs)}

    def gather_start(tag, bufs):
        return _exchange_start(f"gather{tag}_ici_start", _plan_gather_ici, bufs, 3 * len(bufs))

    def gather_pass(tag, flight):
        return _exchange_pass(f"gather{tag}_pass", _plan_gather_ici, _plan_gather_d2d, flight[1], flight[0],
                              3 * len(flight[1]))

    def gather_done(tag, keys, flight):
        return views(keys, _exchange_wait(f"gather{tag}_d2d_wait", _plan_gather_d2d, flight[1], flight[0]))

    fly_ffn0 = gather_start("0a", place(0, ffn1))
    fly_in0 = gather_start("0b", place(0, mix_in))
    fly_rest0 = gather_start("0c", place(0, rest))
    bias = _bias_table(rel_bias, buckets)
    fly_ffn1 = gather_start("1a", place(1, ffn1))
    fly_in1 = gather_start("1b", place(1, mix_in))
    fly_rest1 = gather_start("1c", place(1, rest))
    n1 = _norm_cast(x[0], _row(norm_ffn1[0]))
    w0 = gather_done("0a", ffn1, gather_pass("0a", fly_ffn0))

    s0 = _fwd_ffn1(x[0], n1, w0, small, 0)
    w0.update(gather_done("0b", mix_in, gather_pass("0b", fly_in0)))
    _fwd_proj_sb(s0, w0)
    fly_rest0 = gather_pass("0c", fly_rest0)
    _fwd_swa(s0, small, 0, bias)
    w0.update(gather_done("0c", rest, fly_rest0))
    h, n1 = _fwd_out_ffn2(s0, w0, small, 0, _row(norm_ffn1[1]))
    w1 = gather_done("1a", ffn1, gather_pass("1a", fly_ffn1))
    s1 = _fwd_ffn1(h, n1, w1, small, 1)
    w1.update(gather_done("1b", mix_in, gather_pass("1b", fly_in1)))
    _fwd_proj_sb(s1, w1)
    fly_rest1 = gather_pass("1c", fly_rest1)
    _fwd_swa(s1, small, 1, bias)
    w1.update(gather_done("1c", rest, fly_rest1))
    h, _ = _fwd_out_ffn2(s1, w1, small, 1, _row(norm_final))
    dh, dg_final, loss_row = _loss_head(h, _row(norm_final), loss_target[0])

    def landing(stacks, lead, dtype):
        return [_empty_like_hbm((lead,) + a.shape[2:], dtype) for a in stacks]

    def reduce_begin(tag, keys, gw):
        stacks = [_grad_stack(k, gw[k]) for k in keys]
        flight = _exchange_start(f"grad{tag}_sibling_start", _plan_grad_sibling,
                                 stacks + landing(stacks, N_CHIPS, F32), len(keys))
        return dict(tag=tag, keys=keys, stacks=stacks, flight=flight)

    def reduce_chips(st):
        n, (sems, bufs) = len(st["keys"]), st["flight"]
        bufs = _exchange_wait(f"grad{st['tag']}_sibling_wait", _plan_grad_sibling, bufs, sems)
        st["own"] = list(zip(bufs[:n], bufs[n:]))
        st["flight"] = _exchange_start(f"grad{st['tag']}_chips_start", _plan_grad_chips,
                                       [_chip_sum(g, z, cm) for g, z in st["own"]] + landing(st["stacks"], 3, BF16),
                                       3 * n)

    def reduce_halves(st):
        n, (sems, bufs) = len(st["keys"]), st["flight"]
        bufs = _exchange_wait(f"grad{st['tag']}_chips_wait", _plan_grad_chips, bufs, sems)
        halves = [_total_sum(g, x, z, cm) for (g, x), z in zip(st["own"], bufs[n:])]
        st["flight"] = _exchange_start(f"grad{st['tag']}_halves_start", _plan_grad_halves, halves, n)

    def reduce_end(st):
        sems, bufs = st["flight"]
        bufs = _exchange_wait(f"grad{st['tag']}_halves_wait", _plan_grad_halves, bufs, sems)
        return {k: b.reshape(big_w[k].shape[1:]) for k, b in zip(st["keys"], bufs)}

    def adamw(reduced, l, prev):
        return {k: _adamw_layer(big_w[k], g, big_m[k], big_v[k], l, None if prev is None else prev[k])
                for k, g in reduced.items()}

    gsm = [dict() for _ in range(DEPTH)]
    dbias = jnp.zeros((8, BLK, 2 * BLK), F32)
    dh, gw1, gs = _bwd_ffn(dh, s1, w1, small, 1, 2)
    gsm[1].update(gs)
    dh, gw, gs, dbias = _bwd_mix(dh, s1, w1, small, 1, bias, dbias)
    gw1.update(gw)
    gsm[1].update(gs)
    dh, gw, gs = _bwd_ffn(dh, s1, w1, small, 1, 1)
    gw1.update(gw)
    gsm[1].update(gs)

    red1 = reduce_begin("1", BIG, gw1)
    dh, gw0, gs = _bwd_ffn(dh, s0, w0, small, 0, 2)
    gsm[0].update(gs)
    reduce_chips(red1)
    dh, gw, gs, dbias = _bwd_mix(dh, s0, w0, small, 0, bias, dbias)
    gw0.update(gw)
    gsm[0].update(gs)
    red0a = reduce_begin("0a", ("ffn2_gu", "ffn2_down", "w_out", "w_in"), gw0)
    reduce_halves(red1)
    dgu = _bwd_ffn_dact(dh, s0, w0, 1)
    reduce_chips(red0a)
    dh, gw, gs = _bwd_ffn_rest(dh, dgu, s0, w0, small, 0, 1)
    gsm[0].update(gs)
    red0b = reduce_begin("0b", ffn1, gw)
    reduced1 = reduce_end(red1)
    stacks = adamw({k: reduced1[k] for k in ffn1}, 1, None)

    gsmall = {k: jnp.stack([gsm[l][k].reshape(-1) for l in range(DEPTH)]) for k in gsm[0]}
    gsmall["rel_bias"] = jnp.transpose(_bias_grad(dbias, buckets)[:, :N_BUCKETS])
    gsmall["norm_final"] = dg_final.reshape(-1)
    small_like = [small[k] for k in SMALL]
    pk = lambda dct: _pack([dct[k] for k in SMALL])
    red = _small_allreduce(_pack([gsmall[k] for k in SMALL] + [loss_row[0, :1]]))
    gs = _unpack(red, small_like + [loss_row[0, :1]])
    loss = gs[-1][0]
    gs = dict(zip(SMALL, gs[:-1]))

    ffn2 = ("ffn2_gu", "ffn2_down")
    reduce_chips(red0b)
    stacks.update(adamw({k: reduced1[k] for k in ("w_in", "w_out")}, 1, None))
    reduce_halves(red0a)
    stacks.update(adamw({k: reduced1[k] for k in ffn2}, 1, None))
    dlt, m2, v2 = _adamw_small(pk(small), pk(gs), pk(small_m), pk(small_v))
    reduced0a = reduce_end(red0a)
    stacks.update(adamw({k: reduced0a[k] for k in ffn2}, 0, stacks))
    reduce_halves(red0b)
    stacks.update(adamw({k: reduced0a[k] for k in ("w_in", "w_out")}, 0, stacks))
    stacks.update(adamw(reduce_end(red0b), 0, stacks))

    out_g, out_d, out_m, out_v = {}, {}, {}, {}
    for k in BIG:
        out_g[k], out_d[k], out_m[k], out_v[k] = [jnp.swapaxes(a, 1, 2) if k == "w_in" else a for a in stacks[k]]
    for dst, packed in ((out_d, dlt), (out_m, m2), (out_v, v2)):
        dst.update(zip(SMALL, _unpack(packed, small_like)))
    out_g.update(gs)

    order = ("norm_ffn1", "ffn1_gu", "ffn1_down", "norm_mix", "w_in", "sinks", "norm_out_sb", "norm_out_swa", "w_out",
             "norm_ffn2", "ffn2_gu", "ffn2_down", "rel_bias", "norm_final")
    return (loss, dh.reshape(x.shape), *[out_g[k] for k in order], *[out_d[k] for k in order],
            *[out_m[k] for k in order], *[out_v[k] for k in order])
```

```python
import math

import numpy as np
import jax
import jax.numpy as jnp
from jax import lax
from jax.experimental import pallas as pl
from jax.experimental.pallas import tpu as pltpu

F32 = jnp.float32
BF16 = jnp.bfloat16

D_MODEL = 1024
DEPTH = 2
HEAD_DIM = 64
BLK = 128
N_BUCKETS = 32
MAX_DISTANCE = 128
D_FF = 2816
EPS = 1e-6
NEG_INF = -1e30
SB_W = 512
SWA_W = 512
KV_W = 128
IN_W = 2304
SCALE = HEAD_DIM ** -0.5
N_CHIPS = 4
FS = 2 * D_FF // N_CHIPS
LANES = 128
V7X_VMEM_LIMIT = 56 * 2 ** 20
TM = 512
SB_KT = 512
SWA_G = 4

ADAM_LR = 0.001
ADAM_B1 = 0.9
ADAM_B2 = 0.999
ADAM_EPS = 1e-08
ADAM_WD = 0.01
ADAM_STEP = 10

MESH = pl.DeviceIdType.MESH
ANY = pl.BlockSpec(memory_space=pl.ANY)
HBM = pl.BlockSpec(memory_space=pltpu.HBM)
SEM = pl.BlockSpec(memory_space=pltpu.SEMAPHORE)
EFFECT = pltpu.SideEffectType.DATAFLOW_SIDE_EFFECTING


def _params(n_grid):
    return pltpu.CompilerParams(dimension_semantics=("arbitrary",) * n_grid, vmem_limit_bytes=V7X_VMEM_LIMIT)


_PREVIOUS = [None]


def _call(body, *, name, in_specs, out_specs, out_shape, grid=(), num_scalar_prefetch=0, scratch_shapes=(),
          input_output_aliases=None, compiler_params=None, hbm_args=0):
    n_in = len(in_specs)

    def run(*args):
        dep = _PREVIOUS[0]
        if any(dep is a for a in args):
            dep = None
        args = [pltpu.with_memory_space_constraint(a, pltpu.HBM) if i < hbm_args else a for i, a in enumerate(args)]
        specs = list(in_specs) + ([ANY] if dep is not None else [])
        k = num_scalar_prefetch + n_in
        fn = body if dep is None else (lambda *refs: body(*refs[:k], *refs[k + 1:]))
        if num_scalar_prefetch:
            shape = dict(grid_spec=pltpu.PrefetchScalarGridSpec(
                num_scalar_prefetch=num_scalar_prefetch, grid=grid, in_specs=specs, out_specs=out_specs,
                scratch_shapes=scratch_shapes))
        else:
            shape = dict(grid=grid, in_specs=specs, out_specs=out_specs, scratch_shapes=scratch_shapes)
        out = pl.pallas_call(fn, name=name, out_shape=out_shape, input_output_aliases=input_output_aliases or {},
                             compiler_params=compiler_params, **shape)(*args, *([] if dep is None else [dep]))
        _PREVIOUS[0] = jax.tree.leaves(out)[-1]
        return out

    return run


def _dot(a, b):
    return jnp.dot(a, b, preferred_element_type=F32)


def _dot_nt(a, b):
    return lax.dot_general(a, b, (((1,), (1,)), ((), ())), preferred_element_type=F32)


def _dot_tn(a, b):
    return lax.dot_general(a, b, (((0,), (0,)), ((), ())), preferred_element_type=F32)


def _rms_fwd(x, g):
    r = lax.rsqrt(jnp.mean(x * x, axis=-1, keepdims=True) + EPS)
    xh = x * r
    return xh * g, xh, r


def _rms_bwd(dy, xh, r, g):
    u = dy * g
    dx = r * (u - xh * jnp.mean(u * xh, axis=-1, keepdims=True))
    dg = jnp.sum(dy * xh, axis=0, keepdims=True)
    return dx, dg


def _softplus(z):
    neg_abs = lax.bitcast_convert_type(lax.bitcast_convert_type(z, jnp.int32) | jnp.int32(-2 ** 31), F32)
    sp = jnp.maximum(z, 0.0) + jnp.log(1.0 + jnp.exp(neg_abs))
    return sp, z - sp


def _norm_cast(h, g):
    t, w = h.shape

    def body(h_ref, g_ref, n_ref):
        y, _, _ = _rms_fwd(h_ref[...], g_ref[...])
        n_ref[...] = y.astype(BF16)

    return _call(
        body, name="norm_cast", grid=(t // TM,),
        in_specs=[pl.BlockSpec((TM, w), lambda i: (i, 0)), pl.BlockSpec((1, w), lambda i: (0, 0))],
        out_specs=pl.BlockSpec((TM, w), lambda i: (i, 0)),
        out_shape=jax.ShapeDtypeStruct((t, w), BF16), compiler_params=_params(1))(h, g)


def _ffn_gu(n, wgu):
    t, d = n.shape

    def body(n_ref, wg_ref, wu_ref, gu_ref, act_ref):
        x = n_ref[...]
        g = _dot(x, wg_ref[...])
        u = _dot(x, wu_ref[...])
        sig = jax.nn.sigmoid(g)
        silu = g * sig
        gu_ref[0] = (u * (sig + silu * (1.0 - sig))).astype(BF16)
        gu_ref[1] = silu.astype(BF16)
        act_ref[...] = (silu * u).astype(BF16)

    return _call(
        body, name="ffn_gu", grid=(2, t // TM),
        in_specs=[pl.BlockSpec((TM, d), lambda j, i: (i, 0)),
                  pl.BlockSpec((None, d, FS), lambda j, i: (j, 0, 0)),
                  pl.BlockSpec((None, d, FS), lambda j, i: (j + 2, 0, 0))],
        out_specs=[pl.BlockSpec((2, TM, FS), lambda j, i: (0, i, j)), pl.BlockSpec((TM, FS), lambda j, i: (i, j))],
        out_shape=[jax.ShapeDtypeStruct((2, t, D_FF), BF16), jax.ShapeDtypeStruct((t, D_FF), BF16)],
        compiler_params=_params(2))(n, wgu, wgu)


def _down_res(act, wdn, h, g_next):
    t, f = act.shape
    d = h.shape[1]

    def body(a_ref, w_ref, h_ref, g_ref, o_ref, n_ref):
        out = h_ref[...] + 0.5 * _dot(a_ref[...], w_ref[...])
        o_ref[...] = out
        n_ref[...] = _rms_fwd(out, g_ref[...])[0].astype(BF16)

    row = pl.BlockSpec((TM, d), lambda i: (i, 0))
    return _call(
        body, name="down_res", grid=(t // TM,),
        in_specs=[pl.BlockSpec((TM, f), lambda i: (i, 0)), pl.BlockSpec((f, d), lambda i: (0, 0)), row,
                  pl.BlockSpec((1, d), lambda i: (0, 0))],
        out_specs=[row, row],
        out_shape=[jax.ShapeDtypeStruct((t, d), F32), jax.ShapeDtypeStruct((t, d), BF16)],
        compiler_params=_params(1))(act, wdn, h, g_next)


def _proj(n, w_in_t):
    t, d = n.shape
    w = w_in_t.shape[0]

    def body(n_ref, w_ref, o_ref):
        o_ref[...] = _dot_nt(n_ref[...], w_ref[...]).astype(BF16)

    return _call(
        body, name="proj", grid=(t // TM,),
        in_specs=[pl.BlockSpec((TM, d), lambda i: (i, 0)), pl.BlockSpec((w, d), lambda i: (0, 0))],
        out_specs=pl.BlockSpec((TM, w), lambda i: (i, 0)),
        out_shape=jax.ShapeDtypeStruct((t, w), BF16), compiler_params=_params(1))(n, w_in_t)


def _out_res(o_sb, o_sw, g_sb, g_sw, w_out, h, g_next):
    t, d = h.shape

    def body(a_ref, b_ref, ga_ref, gb_ref, w_ref, h_ref, g_ref, o_ref, mix_ref, n_ref):
        ya, _, _ = _rms_fwd(a_ref[...], ga_ref[...])
        yb, _, _ = _rms_fwd(b_ref[...], gb_ref[...])
        mixed = jnp.concatenate([ya.astype(BF16), yb.astype(BF16)], axis=1)
        mix_ref[...] = mixed
        out = h_ref[...] + _dot(mixed, w_ref[...])
        o_ref[...] = out
        n_ref[...] = _rms_fwd(out, g_ref[...])[0].astype(BF16)

    row = pl.BlockSpec((TM, d), lambda i: (i, 0))
    return _call(
        body, name="out_res", grid=(t // TM,),
        in_specs=[pl.BlockSpec((TM, SB_W), lambda i: (i, 0)), pl.BlockSpec((TM, SWA_W), lambda i: (i, 0)),
                  pl.BlockSpec((1, SB_W), lambda i: (0, 0)), pl.BlockSpec((1, SWA_W), lambda i: (0, 0)),
                  pl.BlockSpec((d, d), lambda i: (0, 0)), row, pl.BlockSpec((1, d), lambda i: (0, 0))],
        out_specs=[row, row, row],
        out_shape=[jax.ShapeDtypeStruct((t, d), F32), jax.ShapeDtypeStruct((t, d), BF16),
                   jax.ShapeDtypeStruct((t, d), BF16)],
        compiler_params=_params(1))(o_sb, o_sw, g_sb, g_sw, w_out, h, g_next)


def _loss_head(h, g, tgt):
    t, d = h.shape

    def body(h_ref, g_ref, t_ref, dh_ref, dhb_ref, dg_ref, loss_ref):
        @pl.when(pl.program_id(0) == 0)
        def _():
            dg_ref[...] = jnp.zeros_like(dg_ref)
            loss_ref[...] = jnp.zeros_like(loss_ref)

        gg = g_ref[...]
        y, xh, r = _rms_fwd(h_ref[...], gg)
        err = y - t_ref[...]
        part = 0.5 * jnp.sum(jnp.sum(err * err, axis=1, keepdims=True) / d, axis=0, keepdims=True)
        loss_ref[...] += jnp.broadcast_to(part, loss_ref.shape)
        dx, dg = _rms_bwd(err / d, xh, r, gg)
        dh_ref[...] = dx
        dhb_ref[...] = dx.astype(BF16)
        dg_ref[...] += dg

    row = pl.BlockSpec((TM, d), lambda i: (i, 0))
    return _call(
        body, name="loss_head", grid=(t // TM,),
        in_specs=[row, pl.BlockSpec((1, d), lambda i: (0, 0)), row],
        out_specs=[row, row, pl.BlockSpec((1, d), lambda i: (0, 0)), pl.BlockSpec((1, LANES), lambda i: (0, 0))],
        out_shape=[jax.ShapeDtypeStruct((t, d), F32), jax.ShapeDtypeStruct((t, d), BF16),
                   jax.ShapeDtypeStruct((1, d), F32), jax.ShapeDtypeStruct((1, LANES), F32)],
        compiler_params=_params(1))(h, g, tgt)


def _ffn_dact(dh, wdn, gu):
    t, d = dh.shape
    tm = TM

    def body(dh_ref, w_ref, gu_ref, o_ref):
        da = 0.5 * _dot_nt(dh_ref[...].astype(BF16), w_ref[...])
        o_ref[0] = (da * gu_ref[0].astype(F32)).astype(BF16)
        o_ref[1] = (da * gu_ref[1].astype(F32)).astype(BF16)

    return _call(
        body, name="ffn_dact", grid=(2, t // tm),
        in_specs=[pl.BlockSpec((tm, d), lambda j, i: (i, 0)), pl.BlockSpec((FS, d), lambda j, i: (j, 0)),
                  pl.BlockSpec((2, tm, FS), lambda j, i: (0, i, j))],
        out_specs=pl.BlockSpec((2, tm, FS), lambda j, i: (0, i, j)),
        out_shape=jax.ShapeDtypeStruct((2, t, D_FF), BF16), compiler_params=_params(2))(dh, wdn, gu)


def _dn_norm_bwd(a, a_spec, w, w_spec, nk, dh, h_in, g, w_transposed=False, tm=TM):
    t, d = dh.shape
    mm = _dot if w_transposed else _dot_nt

    def body(a_ref, w_ref, dh_ref, h_ref, g_ref, o_ref, ob_ref, dg_ref, acc_ref):
        i, k = pl.program_id(0), pl.program_id(1)

        if nk > 1:
            @pl.when(k == 0)
            def _():
                acc_ref[...] = mm(a_ref[...], w_ref[...])

            @pl.when((k > 0) & (k < nk - 1))
            def _():
                acc_ref[...] += mm(a_ref[...], w_ref[...])

        @pl.when(k == nk - 1)
        def _():
            gg = g_ref[...]
            dg = jnp.zeros_like(gg)
            for rows in (slice(r, r + TM // 2) for r in range(0, tm, TM // 2)):
                dn = mm(a_ref[rows, :], w_ref[...])
                if nk > 1:
                    dn = dn + acc_ref[rows, :]
                _, xh, r = _rms_fwd(h_ref[rows, :], gg)
                dx, dg_rows = _rms_bwd(dn, xh, r, gg)
                out = dh_ref[rows, :] + dx
                o_ref[rows, :] = out
                ob_ref[rows, :] = out.astype(BF16)
                dg = dg + dg_rows

            @pl.when(i == 0)
            def _():
                dg_ref[...] = dg

            @pl.when(i > 0)
            def _():
                dg_ref[...] += dg

    row = pl.BlockSpec((tm, d), lambda i, k: (i, 0))
    return _call(
        body, name="dn_norm_bwd", grid=(t // tm, nk),
        in_specs=[a_spec, w_spec, row, row, pl.BlockSpec((1, d), lambda i, k: (0, 0))],
        out_specs=[row, row, pl.BlockSpec((1, d), lambda i, k: (0, 0))],
        out_shape=[jax.ShapeDtypeStruct((t, d), F32), jax.ShapeDtypeStruct((t, d), BF16),
                   jax.ShapeDtypeStruct((1, d), F32)],
        scratch_shapes=[pltpu.VMEM((tm, d), F32)], compiler_params=_params(2))(a, w, dh, h_in, g)


def _ffn_dn(dgu, wgu, dh, h_in, g):
    d = dh.shape[1]
    tm = 2 * TM
    return _dn_norm_bwd(
        dgu, pl.BlockSpec((None, tm, FS), lambda i, k: (k // 2, i, k % 2)),
        wgu, pl.BlockSpec((None, d, FS), lambda i, k: (k, 0, 0)), N_CHIPS, dh, h_in, g, tm=tm)


def _mix_dn(dproj, w_in_t, dh, h_in, g):
    d = dh.shape[1]
    w = dproj.shape[1]
    return _dn_norm_bwd(
        dproj, pl.BlockSpec((TM, w), lambda i, k: (i, 0)),
        w_in_t, pl.BlockSpec((w, d), lambda i, k: (0, 0)), 1, dh, h_in, g, w_transposed=True)


def _dmixed(dh, w_out, o_sb, o_sw, g_sb, g_sw):
    t, d = dh.shape

    def body(dh_ref, w_ref, a_ref, b_ref, ga_ref, gb_ref, o_ref, dga_ref, dgb_ref):
        i = pl.program_id(0)
        dm = _dot_nt(dh_ref[...].astype(BF16), w_ref[...])
        _, xa, ra = _rms_fwd(a_ref[...], ga_ref[...])
        _, xb, rb = _rms_fwd(b_ref[...], gb_ref[...])
        da, dga = _rms_bwd(dm[:, :SB_W], xa, ra, ga_ref[...])
        db, dgb = _rms_bwd(dm[:, SB_W:], xb, rb, gb_ref[...])
        o_ref[...] = jnp.concatenate([da.astype(BF16), db.astype(BF16)], axis=1)

        @pl.when(i == 0)
        def _():
            dga_ref[...] = dga
            dgb_ref[...] = dgb

        @pl.when(i > 0)
        def _():
            dga_ref[...] += dga
            dgb_ref[...] += dgb

    return _call(
        body, name="dmixed", grid=(t // TM,),
        in_specs=[pl.BlockSpec((TM, d), lambda i: (i, 0)), pl.BlockSpec((d, d), lambda i: (0, 0)),
                  pl.BlockSpec((TM, SB_W), lambda i: (i, 0)), pl.BlockSpec((TM, SWA_W), lambda i: (i, 0)),
                  pl.BlockSpec((1, SB_W), lambda i: (0, 0)), pl.BlockSpec((1, SWA_W), lambda i: (0, 0))],
        out_specs=[pl.BlockSpec((TM, d), lambda i: (i, 0)), pl.BlockSpec((1, SB_W), lambda i: (0, 0)),
                   pl.BlockSpec((1, SWA_W), lambda i: (0, 0))],
        out_shape=[jax.ShapeDtypeStruct((t, d), BF16), jax.ShapeDtypeStruct((1, SB_W), F32),
                   jax.ShapeDtypeStruct((1, SWA_W), F32)],
        compiler_params=_params(1))(dh, w_out, o_sb, o_sw, g_sb, g_sw)


def _wgrad(name, a, a_spec, b, b_spec, grid, out_shape, out_spec, scale):
    def body(a_ref, b_ref, o_ref):
        r = _dot_tn(a_ref[...], b_ref[...].astype(BF16))
        o_ref[...] = r if scale == 1.0 else scale * r

    return _call(
        body, name=name, grid=grid, in_specs=[a_spec, b_spec], out_specs=out_spec,
        out_shape=jax.ShapeDtypeStruct(out_shape, F32), compiler_params=_params(len(grid)))(a, b)


def _wgrad_gu(n, dgu):
    t, d = n.shape
    return _wgrad(
        "wgrad_gu", n, pl.BlockSpec((t, TM), lambda s, r: (0, r)),
        dgu, pl.BlockSpec((None, t, FS), lambda s, r: (s // 2, 0, s % 2)), (N_CHIPS, d // TM),
        (N_CHIPS, d, FS), pl.BlockSpec((None, TM, FS), lambda s, r: (s, r, 0)), 1.0)


def _wgrad_down(act, dh):
    t, d = dh.shape
    return _wgrad(
        "wgrad_down", act, pl.BlockSpec((t, FS), lambda s: (0, s)), dh, pl.BlockSpec((t, d), lambda s: (0, 0)),
        (2,), (D_FF, d), pl.BlockSpec((FS, d), lambda s: (s, 0)), 0.5)


def _wgrad_out(mixed, dh):
    t, d = dh.shape
    return _wgrad(
        "wgrad_out", mixed, pl.BlockSpec((t, TM), lambda s: (0, s)), dh, pl.BlockSpec((t, d), lambda s: (0, 0)),
        (d // TM,), (d, d), pl.BlockSpec((TM, d), lambda s: (s, 0)), 1.0)


def _wgrad_in(n, dproj):
    t, d = n.shape
    w = dproj.shape[1]
    tw = w // 3
    return _wgrad(
        "wgrad_in", dproj, pl.BlockSpec((t, tw), lambda s: (0, s)), n, pl.BlockSpec((t, d), lambda s: (0, 0)),
        (3,), (w, d), pl.BlockSpec((tw, d), lambda s: (s, 0)), 1.0)


def _tri(rel):
    row = lax.broadcasted_iota(jnp.int32, (BLK, BLK), 0)
    col = lax.broadcasted_iota(jnp.int32, (BLK, BLK), 1)
    m = rel(row, col).astype(BF16)
    return jnp.concatenate([m, m], axis=0)


def _scan_dot(x, tri2):
    hi = x.astype(BF16)
    lo = (x - hi.astype(F32)).astype(BF16)
    return _dot(jnp.concatenate([hi, lo], axis=1), tri2)


def _head_masks():
    lane = lax.broadcasted_iota(jnp.int32, (1, LANES), 1)
    return [lane < HEAD_DIM, lane >= HEAD_DIM]


SB_PAIRS = 2
SB_ROWS = 2 * SB_PAIRS * BLK


def _sb_causal():
    row = lax.broadcasted_iota(jnp.int32, (SB_ROWS, BLK), 0) & (BLK - 1)
    return lax.broadcasted_iota(jnp.int32, (SB_ROWS, BLK), 1) < row


def _sb_mask_last(x, causal):
    own = jnp.where(causal, x[:, -BLK:], 0.0)
    return own if x.shape[1] == BLK else jnp.concatenate([x[:, :-BLK], own], axis=1)


def _sb_stack(x, hm):
    return jnp.concatenate([jnp.where(m, x[:, p * LANES:(p + 1) * LANES], jnp.zeros((BLK, LANES), x.dtype))
                            for p in range(SB_PAIRS) for m in hm], axis=0)


def _sb_unstack(y, hm):
    return jnp.concatenate([jnp.where(hm[0], y[2 * p * BLK:(2 * p + 1) * BLK], y[(2 * p + 1) * BLK:(2 * p + 2) * BLK])
                            for p in range(SB_PAIRS)], axis=1)


def _sb_pairs():
    return [(slice(2 * p * BLK, (2 * p + 2) * BLK), slice(p * LANES, (p + 1) * LANES)) for p in range(SB_PAIRS)]


def _sb_fwd(proj):
    t = proj.shape[0]
    nb = SB_KT // BLK
    wide = SB_PAIRS * LANES

    def body(q_ref, k_ref, v_ref, o_ref, tot_ref):
        hm = _head_masks()
        causal = _sb_causal()
        pairs = _sb_pairs()
        after = _tri(lambda r, c: r > c)

        def tile(qh, start, n_blk, carry, acc, own):
            ks = pl.ds(pl.multiple_of(start, BLK), n_blk * BLK)
            z = jnp.concatenate([_dot_nt(qh[rows], k_ref[ks, lanes]) for rows, lanes in pairs], axis=0)
            sp, zs = _softplus(z)
            spm = _sb_mask_last(sp, causal) if own else sp
            sufs = [None] * n_blk
            for b in reversed(range(n_blk)):
                blk = spm[:, b * BLK:(b + 1) * BLK]
                sufs[b] = carry + _scan_dot(blk, after)
                carry = carry + jnp.sum(blk, axis=1, keepdims=True)
            w = jnp.exp(zs - jnp.concatenate(sufs, axis=1))
            wb = (_sb_mask_last(w, causal) if own else w).astype(BF16)
            return carry, acc + jnp.concatenate([_dot(wb[rows], v_ref[ks, lanes]) for rows, lanes in pairs], axis=0)

        def qblock(g, j):
            qs = pl.ds(pl.multiple_of(g * SB_KT + j * BLK, BLK), BLK)
            qh = _sb_stack(q_ref[qs, :] * SCALE, hm)
            c0 = tile(qh, g * SB_KT, j + 1, jnp.zeros((SB_ROWS, 1), F32), jnp.zeros((SB_ROWS, LANES), F32), True)
            carry, acc = lax.fori_loop(0, g, lambda n, c: tile(qh, (g - 1 - n) * SB_KT, nb, c[0], c[1], False), c0)
            o_ref[qs, :] = _sb_unstack(acc, hm)
            for h in range(2 * SB_PAIRS):
                tot_ref[h, qs, :] = carry[h * BLK:(h + 1) * BLK]

        def group(g, _):
            for j in range(nb):
                qblock(g, j)
            return 0

        lax.fori_loop(0, t // SB_KT, group, 0)

    col_blk = lambda off: pl.BlockSpec((t, wide), lambda g: (0, off + g))
    n_steps = SB_W // wide
    return _call(
        body, name="sb_fwd", grid=(n_steps,), in_specs=[col_blk(0), col_blk(n_steps), col_blk(2 * n_steps)],
        out_specs=[col_blk(0), pl.BlockSpec((2 * SB_PAIRS, t, 1), lambda g: (g, 0, 0))],
        out_shape=[jax.ShapeDtypeStruct((t, SB_W), F32), jax.ShapeDtypeStruct((8, t, 1), F32)],
        compiler_params=_params(1))(proj, proj, proj)


def _sb_bwd(proj, d_o, tot):
    t = proj.shape[0]
    nb = SB_KT // BLK
    wide = SB_PAIRS * LANES

    def body(q_ref, k_ref, v_ref, do_ref, tot_ref, dq_ref, dk_ref, dv_ref, dk_acc, dv_acc):
        hm = _head_masks()
        causal = _sb_causal()
        pairs = _sb_pairs()
        before = _tri(lambda r, c: r < c)
        upto = _tri(lambda r, c: r <= c)
        dk_acc[...] = jnp.zeros_like(dk_acc)
        dv_acc[...] = jnp.zeros_like(dv_acc)

        def tile(qh, doh, tt, start, n_blk, pre, ecum, dq, own):
            ks = pl.ds(pl.multiple_of(start, BLK), n_blk * BLK)
            k = k_ref[ks, :]
            v = v_ref[ks, :]
            z = jnp.concatenate([_dot_nt(qh[rows], k[:, lanes]) for rows, lanes in pairs], axis=0)
            sp, zs = _softplus(z)
            spm = _sb_mask_last(sp, causal) if own else sp
            pres = []
            for b in range(n_blk):
                blk = spm[:, b * BLK:(b + 1) * BLK]
                pres.append(pre + _scan_dot(blk, before))
                pre = pre + jnp.sum(blk, axis=1, keepdims=True)
            logw = z - (tt - jnp.concatenate(pres, axis=1))
            if own:
                logw = jnp.minimum(logw, 0.0)
            w = jnp.exp(logw)
            if own:
                w = _sb_mask_last(w, causal)
            e = w * jnp.concatenate([_dot_nt(doh[rows], v[:, lanes]) for rows, lanes in pairs], axis=0)
            incs = []
            for b in range(n_blk):
                blk = e[:, b * BLK:(b + 1) * BLK]
                incs.append(ecum + _scan_dot(blk, upto))
                ecum = ecum + jnp.sum(blk, axis=1, keepdims=True)
            dz = e - jnp.exp(zs) * jnp.concatenate(incs, axis=1)
            if own:
                dz = _sb_mask_last(dz, causal)
            dzb = dz.astype(BF16)
            wb = w.astype(BF16)
            for rows, lanes in pairs:
                dk_acc[ks, lanes] += _dot_tn(dzb[rows], qh[rows])
                dv_acc[ks, lanes] += _dot_tn(wb[rows], doh[rows])
            return pre, ecum, dq + jnp.concatenate([_dot(dzb[rows], k[:, lanes]) for rows, lanes in pairs], axis=0)

        def qblock(g, j):
            qs = pl.ds(pl.multiple_of(g * SB_KT + j * BLK, BLK), BLK)
            qh = _sb_stack(q_ref[qs, :] * SCALE, hm)
            doh = _sb_stack(do_ref[qs, :], hm)
            tt = jnp.concatenate([tot_ref[h, qs, :] for h in range(2 * SB_PAIRS)], axis=0)
            c0 = (jnp.zeros((SB_ROWS, 1), F32), jnp.zeros((SB_ROWS, 1), F32), jnp.zeros((SB_ROWS, LANES), F32))
            c = lax.fori_loop(0, g, lambda kt, c: tile(qh, doh, tt, kt * SB_KT, nb, c[0], c[1], c[2], False), c0)
            dq = tile(qh, doh, tt, g * SB_KT, j + 1, c[0], c[1], c[2], True)[2]
            dq_ref[qs, :] = (_sb_unstack(dq, hm) * SCALE).astype(BF16)

        def group(g, _):
            for j in range(nb):
                qblock(g, j)
            return 0

        lax.fori_loop(0, t // SB_KT, group, 0)
        dk_ref[...] = dk_acc[...].astype(BF16)
        dv_ref[...] = dv_acc[...].astype(BF16)

    col_blk = lambda off: pl.BlockSpec((t, wide), lambda g: (0, off + g))
    n_steps = SB_W // wide
    out = jax.ShapeDtypeStruct((t, SB_W), BF16)
    return _call(
        body, name="sb_bwd", grid=(n_steps,),
        in_specs=[col_blk(0), col_blk(n_steps), col_blk(2 * n_steps), col_blk(0),
                  pl.BlockSpec((2 * SB_PAIRS, t, 1), lambda g: (g, 0, 0))],
        out_specs=[col_blk(0), col_blk(0), col_blk(0)], out_shape=[out, out, out],
        scratch_shapes=[pltpu.VMEM((t, wide), F32), pltpu.VMEM((t, wide), F32)],
        compiler_params=_params(1))(proj, proj, proj, d_o, tot)


def _bucket_table():
    a = np.arange(BLK)[:, None]
    c = np.arange(2 * BLK)[None, :]
    dist = np.maximum(BLK + a - c, 0)
    max_exact = N_BUCKETS // 2
    dd = np.maximum(dist, 1).astype(np.float32)
    large = max_exact + (np.log(dd / max_exact) / math.log(MAX_DISTANCE / max_exact)
                         * (N_BUCKETS - max_exact)).astype(np.int32)
    large = np.minimum(large, N_BUCKETS - 1)
    return np.where(dist < max_exact, dist, large).astype(np.int32)


SWA_H = 8


def _swa_band_masks():
    row = lax.broadcasted_iota(jnp.int32, (SWA_H * BLK, 2 * BLK), 0) & (BLK - 1)
    col = lax.broadcasted_iota(jnp.int32, (SWA_H * BLK, 2 * BLK), 1)
    own = lax.broadcasted_iota(jnp.int32, (SWA_H * BLK, BLK), 1) <= (
        lax.broadcasted_iota(jnp.int32, (SWA_H * BLK, BLK), 0) & (BLK - 1))
    return (col > row) & ((col < BLK) | (col - BLK <= row)), own


def _swa_stack(ref, qs, hm, scale):
    parts = []
    for hq in range(SWA_H):
        kvh = hq // SWA_G
        x = ref[qs, (hq // 2) * LANES:(hq // 2 + 1) * LANES].astype(F32)
        if hq % 2 != kvh:
            x = pltpu.roll(x, HEAD_DIM, 1)
        parts.append(jnp.where(hm[kvh], x * scale, 0.0).astype(BF16))
    return jnp.concatenate(parts, axis=0)


def _swa_unstack(x8, hm):
    heads = []
    for hq in range(SWA_H):
        x = x8[hq * BLK:(hq + 1) * BLK]
        heads.append(pltpu.roll(x, HEAD_DIM, 1) if hq % 2 != hq // SWA_G else x)
    return [jnp.where(hm[0], heads[2 * p], heads[2 * p + 1]) for p in range(SWA_H // 2)]


def _swa_scores(q8, kb, bias_ref, mask, cols):
    bias8 = jnp.concatenate([bias_ref[hq, :, cols] for hq in range(SWA_H)], axis=0)
    return jnp.where(mask, _dot_nt(q8, kb) + bias8, NEG_INF)


def _swa_sinks(sink_ref):
    return jnp.concatenate([jnp.broadcast_to(sink_ref[hq:hq + 1, 0:1], (BLK, 1)) for hq in range(SWA_H)], axis=0)


def _swa_fwd(proj, bias, sinks_b):
    t = proj.shape[0]
    nq = t // BLK

    def body(q_ref, k_ref, v_ref, bias_ref, sink_ref, o_ref, lse_ref):
        hm = _head_masks()
        band, own = _swa_band_masks()

        def qblock(i, prev):
            qs = pl.ds(pl.multiple_of(i * BLK, BLK), BLK)
            if prev:
                ks, mask, cols = pl.ds(pl.multiple_of((i - 1) * BLK, BLK), 2 * BLK), band, slice(None)
            else:
                ks, mask, cols = qs, own, slice(BLK, None)
            q8 = _swa_stack(q_ref, qs, hm, SCALE)
            sink8 = _swa_sinks(sink_ref)
            s = _swa_scores(q8, k_ref[ks, :], bias_ref, mask, cols)
            m = jnp.maximum(jnp.max(s, axis=1, keepdims=True), sink8)
            p = jnp.exp(s - m)
            den = jnp.sum(p, axis=1, keepdims=True) + jnp.exp(sink8 - m)
            o8 = _dot((p * (1.0 / den)).astype(BF16), v_ref[ks, :])
            lse8 = m + jnp.log(den)
            for hq in range(SWA_H):
                lse_ref[hq, qs, :] = lse8[hq * BLK:(hq + 1) * BLK]
            for pp, o in enumerate(_swa_unstack(o8, hm)):
                o_ref[qs, pp * LANES:(pp + 1) * LANES] = o

        qblock(0, False)

        def step(i, _):
            qblock(i, True)
            return 0

        lax.fori_loop(1, nq, step, 0)

    return _call(
        body, name="swa_fwd", grid=(1,),
        in_specs=[pl.BlockSpec((t, SWA_W), lambda i: (0, 3)), pl.BlockSpec((t, KV_W), lambda i: (0, 16)),
                  pl.BlockSpec((t, KV_W), lambda i: (0, 17)), pl.BlockSpec((8, BLK, 2 * BLK), lambda i: (0, 0, 0)),
                  pl.BlockSpec((8, LANES), lambda i: (0, 0))],
        out_specs=[pl.BlockSpec((t, SWA_W), lambda i: (0, 0)), pl.BlockSpec((8, t, 1), lambda i: (0, 0, 0))],
        out_shape=[jax.ShapeDtypeStruct((t, SWA_W), F32), jax.ShapeDtypeStruct((8, t, 1), F32)],
        compiler_params=_params(1))(proj, proj, proj, bias, sinks_b)


def _swa_bwd(proj, d_o, lse, bias, sinks_b, dbias_in):
    t = proj.shape[0]
    nq = t // BLK

    def body(q_ref, k_ref, v_ref, do_ref, lse_ref, bias_ref, sink_ref, dbi_ref,
             dq_ref, dk_ref, dv_ref, dsink_ref, dbias_ref, dk_acc, dv_acc):
        hm = _head_masks()
        band, own = _swa_band_masks()
        dk_acc[...] = jnp.zeros_like(dk_acc)
        dv_acc[...] = jnp.zeros_like(dv_acc)
        dbias_ref[...] = dbi_ref[...]

        def qblock(i, prev, dsink8):
            qs = pl.ds(pl.multiple_of(i * BLK, BLK), BLK)
            if prev:
                ks, mask, cols = pl.ds(pl.multiple_of((i - 1) * BLK, BLK), 2 * BLK), band, slice(None)
            else:
                ks, mask, cols = qs, own, slice(BLK, None)
            q8 = _swa_stack(q_ref, qs, hm, SCALE)
            do8 = _swa_stack(do_ref, qs, hm, 1.0)
            sink8 = _swa_sinks(sink_ref)
            lse8 = jnp.concatenate([lse_ref[hq, qs, :] for hq in range(SWA_H)], axis=0)
            kb = k_ref[ks, :]
            p = jnp.exp(_swa_scores(q8, kb, bias_ref, mask, cols) - lse8)
            dp = _dot_nt(do8, v_ref[ks, :])
            delta = jnp.sum(p * dp, axis=1, keepdims=True)
            ds = p * (dp - delta)
            for hq in range(SWA_H):
                dbias_ref[hq, :, cols] += ds[hq * BLK:(hq + 1) * BLK]
            dsb = ds.astype(BF16)
            dk_acc[ks, :] += _dot_tn(dsb, q8)
            dv_acc[ks, :] += _dot_tn(p.astype(BF16), do8)
            for pp, dq in enumerate(_swa_unstack(_dot(dsb, kb) * SCALE, hm)):
                dq_ref[qs, pp * LANES:(pp + 1) * LANES] = dq.astype(BF16)
            return dsink8 - jnp.exp(sink8 - lse8) * delta

        ds0 = qblock(0, False, jnp.zeros((SWA_H * BLK, 1), F32))
        ds8 = lax.fori_loop(1, nq, lambda i, c: qblock(i, True, c), ds0)
        for hq in range(SWA_H):
            dsink_ref[hq:hq + 1, :] = jnp.broadcast_to(
                jnp.sum(ds8[hq * BLK:(hq + 1) * BLK], axis=0, keepdims=True), (1, LANES))

        dk_ref[...] = dk_acc[...].astype(BF16)
        dv_ref[...] = dv_acc[...].astype(BF16)

    full3 = pl.BlockSpec((8, BLK, 2 * BLK), lambda i: (0, 0, 0))
    kv = jax.ShapeDtypeStruct((t, KV_W), BF16)
    return _call(
        body, name="swa_bwd", grid=(1,),
        in_specs=[pl.BlockSpec((t, SWA_W), lambda i: (0, 3)), pl.BlockSpec((t, KV_W), lambda i: (0, 16)),
                  pl.BlockSpec((t, KV_W), lambda i: (0, 17)), pl.BlockSpec((t, SWA_W), lambda i: (0, 1)),
                  pl.BlockSpec((8, t, 1), lambda i: (0, 0, 0)), full3, pl.BlockSpec((8, LANES), lambda i: (0, 0)),
                  full3],
        out_specs=[pl.BlockSpec((t, SWA_W), lambda i: (0, 0)), pl.BlockSpec((t, KV_W), lambda i: (0, 0)),
                   pl.BlockSpec((t, KV_W), lambda i: (0, 0)), pl.BlockSpec((8, LANES), lambda i: (0, 0)), full3],
        out_shape=[jax.ShapeDtypeStruct((t, SWA_W), BF16), kv, kv, jax.ShapeDtypeStruct((8, LANES), F32),
                   jax.ShapeDtypeStruct((8, BLK, 2 * BLK), F32)],
        scratch_shapes=[pltpu.VMEM((t, KV_W), F32), pltpu.VMEM((t, KV_W), F32)],
        compiler_params=_params(1))(proj, proj, proj, d_o, lse, bias, sinks_b, dbias_in)


def _concat_cols(parts):
    t = parts[0].shape[0]
    widths = [a.shape[1] for a in parts]

    def body(*refs):
        refs[-1][...] = jnp.concatenate([r[...] for r in refs[:-1]], axis=1)

    return _call(
        body, name="concat_cols", grid=(t // TM,),
        in_specs=[pl.BlockSpec((TM, w), lambda i: (i, 0)) for w in widths],
        out_specs=pl.BlockSpec((TM, sum(widths)), lambda i: (i, 0)),
        out_shape=jax.ShapeDtypeStruct((t, sum(widths)), parts[0].dtype), compiler_params=_params(1))(*parts)


def _bias_table(rel_bias, buckets):
    def body(rb_ref, b_ref, o_ref):
        bk = b_ref[...]
        for h in range(8):
            acc = jnp.zeros((BLK, 2 * BLK), F32)
            for b in range(N_BUCKETS):
                acc = jnp.where(bk == b, rb_ref[b, h], acc)
            o_ref[h] = acc

    return _call(
        body, name="bias_table", grid=(1,),
        in_specs=[pl.BlockSpec(memory_space=pltpu.SMEM), pl.BlockSpec((BLK, 2 * BLK), lambda i: (0, 0))],
        out_specs=pl.BlockSpec((8, BLK, 2 * BLK), lambda i: (0, 0, 0)),
        out_shape=jax.ShapeDtypeStruct((8, BLK, 2 * BLK), F32), compiler_params=_params(1))(rel_bias, buckets)


def _bias_grad(dbias, buckets):
    def body(d_ref, b_ref, o_ref):
        lane = lax.broadcasted_iota(jnp.int32, (1, LANES), 1)
        bk = b_ref[...]
        for h in range(8):
            d = d_ref[h]
            acc = jnp.zeros((1, LANES), F32)
            for b in range(N_BUCKETS):
                s = jnp.sum(jnp.sum(jnp.where(bk == b, d, 0.0), axis=0, keepdims=True), axis=1, keepdims=True)
                acc = acc + jnp.where(lane == b, s, 0.0)
            o_ref[h:h + 1, :] = acc

    return _call(
        body, name="bias_grad", grid=(1,),
        in_specs=[pl.BlockSpec((8, BLK, 2 * BLK), lambda i: (0, 0, 0)), pl.BlockSpec((BLK, 2 * BLK), lambda i: (0, 0))],
        out_specs=pl.BlockSpec((8, LANES), lambda i: (0, 0)),
        out_shape=jax.ShapeDtypeStruct((8, LANES), F32), compiler_params=_params(1))(dbias, buckets)


def _row(a):
    return a.reshape(1, -1)


def _fwd_ffn1(h, n1, w, small, l):
    s = {"h0": h, "n1": n1}
    s["gu1"], s["act1"] = _ffn_gu(n1, w["ffn1_gu"])
    s["h1"], s["nm"] = _down_res(s["act1"], w["ffn1_down"], h, _row(small["norm_mix"][l]))
    return s


def _fwd_proj_sb(s, w):
    s["proj"] = _proj(s["nm"], w["w_in"])
    s["o_sb"], s["tot"] = _sb_fwd(s["proj"])


def _fwd_swa(s, small, l, bias):
    s["sinks_b"] = jnp.broadcast_to(small["sinks"][l][:, None], (8, LANES))
    s["o_sw"], s["lse"] = _swa_fwd(s["proj"], bias, s["sinks_b"])


def _fwd_out_ffn2(s, w, small, l, g_after):
    s["h2"], s["mixed"], s["n2"] = _out_res(
        s["o_sb"], s["o_sw"], _row(small["norm_out_sb"][l]), _row(small["norm_out_swa"][l]), w["w_out"], s["h1"],
        _row(small["norm_ffn2"][l]))
    s["gu2"], s["act2"] = _ffn_gu(s["n2"], w["ffn2_gu"])
    return _down_res(s["act2"], w["ffn2_down"], s["h2"], g_after)


def _bwd_ffn_dact(dh, s, w, which):
    return _ffn_dact(dh[1], w[f"ffn{which}_down"], s[f"gu{which}"])


def _bwd_ffn_rest(dh, dgu, s, w, small, l, which):
    h_in, norm = (s["h0"], "norm_ffn1") if which == 1 else (s["h2"], "norm_ffn2")
    g_down = _wgrad_down(s[f"act{which}"], dh[1])
    g_gu = _wgrad_gu(s[f"n{which}"], dgu)
    dh32, dh16, dg = _ffn_dn(dgu, w[f"ffn{which}_gu"], dh[0], h_in, _row(small[norm][l]))
    return (dh32, dh16), {f"ffn{which}_down": g_down, f"ffn{which}_gu": g_gu}, {norm: dg}


def _bwd_ffn(dh, s, w, small, l, which):
    return _bwd_ffn_rest(dh, _bwd_ffn_dact(dh, s, w, which), s, w, small, l, which)


def _bwd_mix(dh, s, w, small, l, bias, dbias):
    g_out = _wgrad_out(s["mixed"], dh[1])
    d_o, dg_sb, dg_sw = _dmixed(dh[1], w["w_out"], s["o_sb"], s["o_sw"], _row(small["norm_out_sb"][l]),
                                _row(small["norm_out_swa"][l]))
    dq_sb, dk_sb, dv_sb = _sb_bwd(s["proj"], d_o, s["tot"])
    dq_sw, dk_sw, dv_sw, dsink, dbias = _swa_bwd(s["proj"], d_o, s["lse"], bias, s["sinks_b"], dbias)
    dproj = _concat_cols([dq_sb, dk_sb, dv_sb, dq_sw, dk_sw, dv_sw])
    g_in = _wgrad_in(s["nm"], dproj)
    dh32, dh16, dg_mix = _mix_dn(dproj, w["w_in"], dh[0], s["h1"], _row(small["norm_mix"][l]))
    gs = {"norm_out_sb": dg_sb, "norm_out_swa": dg_sw, "sinks": dsink[:, 0], "norm_mix": dg_mix}
    return (dh32, dh16), {"w_out": g_out, "w_in": g_in}, gs, dbias


def _place():
    x, y, c = lax.axis_index("x"), lax.axis_index("y"), lax.axis_index("c")
    return x, y, c, 2 * x + y


def _chip_core(k, c):
    return (k // 2, k % 2, c)


def _rows_per_block(rows, cols, copies):
    best = 16
    for tr in range(16, rows + 1, 16):
        if rows % tr == 0 and copies * tr * cols * 4 <= 6 * 2 ** 20:
            best = tr
    assert rows % best == 0
    return best


def _place_own(w, l, me1):
    _, rows, cols = w.shape
    tr = _rows_per_block(rows // 2, cols, 1)
    per_half = rows // 2 // tr

    def body(me_ref, w_ref, o_ref):
        o_ref[...] = w_ref[...].astype(BF16)

    return _call(
        body, name="place_own",
        num_scalar_prefetch=1, grid=(rows // tr,),
        in_specs=[pl.BlockSpec((None, tr, cols), lambda r, me: (l, r, 0))],
        out_specs=pl.BlockSpec((None, None, tr, cols), lambda r, me: (me[0], r // per_half, r % per_half, 0)),
        out_shape=jax.ShapeDtypeStruct((N_CHIPS, 2, rows // 2, cols), BF16), compiler_params=_params(1))(me1, w)


def _plan_gather_ici(bufs):
    _, _, c, me = _place()
    return [(b.at[me, c], b.at[me, c], b.at[(me + 3 - j) % N_CHIPS, c], _chip_core((me + 1 + j) % N_CHIPS, c))
            for b in bufs for j in range(3)]


def _plan_gather_d2d(bufs):
    x, y, c, me = _place()
    return [(b.at[(me + 3 - j) % N_CHIPS, c], b.at[(me + 3 - j) % N_CHIPS, c], b.at[(me + 3 - j) % N_CHIPS, 1 - c],
             (x, y, 1 - c)) for b in bufs for j in range(3)]


def _plan_grad_sibling(bufs):
    x, y, c, _ = _place()
    n = len(bufs) // 2
    return [(g.at[:, 1 - c], z, z, (x, y, 1 - c)) for g, z in zip(bufs[:n], bufs[n:])]


def _plan_grad_chips(bufs):
    _, _, c, me = _place()
    n = len(bufs) // 2
    return [(p.at[j], z.at[j], z.at[j], _chip_core((me + 1 + j) % N_CHIPS, c))
            for p, z in zip(bufs[:n], bufs[n:]) for j in range(3)]


def _plan_grad_halves(bufs):
    x, y, c, _ = _place()
    return [(b.at[c], b.at[c], b.at[1 - c], (x, y, 1 - c)) for b in bufs]


def _remote(src, dst, send_sem, recv_sem, to):
    return pltpu.make_async_remote_copy(src_ref=src, dst_ref=dst, send_sem=send_sem, recv_sem=recv_sem,
                                        device_id=to, device_id_type=MESH)


def _exchange_start(name, plan, bufs, n_copies):
    n = len(bufs)

    def body(*refs):
        ins = refs[:n]
        ssem, rsem = refs[n], refs[n + 1]
        token = refs[-1]
        for i, (src, dst, _, to) in enumerate(plan(ins)):
            _remote(src, dst, ssem.at[i], rsem.at[i], to).start()
        token[...] = jnp.zeros_like(token)

    out = _call(
        body, name=name,
        out_shape=(pltpu.SemaphoreType.DMA((n_copies,)), pltpu.SemaphoreType.DMA((n_copies,)),
                   *[pltpu.HBM(a.shape, a.dtype) for a in bufs], jax.ShapeDtypeStruct((8, LANES), F32)),
        in_specs=[HBM] * n, out_specs=(SEM, SEM, *[HBM] * n, pl.BlockSpec(memory_space=pltpu.VMEM)),
        input_output_aliases={t: 2 + t for t in range(n)}, hbm_args=n,
        compiler_params=pltpu.CompilerParams(has_side_effects=EFFECT),
    )(*bufs)
    return (out[0], out[1]), list(out[2:2 + n])


def _exchange_wait(name, plan, bufs, sems):
    n = len(bufs)

    def body(*refs):
        ins = refs[:n]
        ssem, rsem = refs[n], refs[n + 1]
        for i, (src, dst, land, to) in enumerate(plan(ins)):
            _remote(src, dst, ssem.at[i], rsem.at[i], to).wait_send()
            _remote(land, land, ssem.at[i], rsem.at[i], to).wait_recv()

    return list(_call(
        body, name=name, out_shape=[pltpu.HBM(a.shape, a.dtype) for a in bufs],
        in_specs=[HBM] * n + [SEM, SEM], out_specs=[HBM] * n,
        input_output_aliases={t: t for t in range(n)},
        compiler_params=pltpu.CompilerParams(has_side_effects=EFFECT),
    )(*bufs, sems[0], sems[1]))


def _exchange_pass(name, done, plan, bufs, sems, n_copies):
    n = len(bufs)

    def body(*refs):
        ins = refs[:n]
        old_s, old_r, ssem, rsem = refs[n], refs[n + 1], refs[n + 2], refs[n + 3]
        token = refs[-1]
        for i, (src, dst, land, to) in enumerate(done(ins)):
            _remote(src, dst, old_s.at[i], old_r.at[i], to).wait_send()
            _remote(land, land, old_s.at[i], old_r.at[i], to).wait_recv()
        for i, (src, dst, _, to) in enumerate(plan(ins)):
            _remote(src, dst, ssem.at[i], rsem.at[i], to).start()
        token[...] = jnp.zeros_like(token)

    out = _call(
        body, name=name,
        out_shape=(pltpu.SemaphoreType.DMA((n_copies,)), pltpu.SemaphoreType.DMA((n_copies,)),
                   *[pltpu.HBM(a.shape, a.dtype) for a in bufs], jax.ShapeDtypeStruct((8, LANES), F32)),
        in_specs=[HBM] * n + [SEM, SEM], out_specs=(SEM, SEM, *[HBM] * n, pl.BlockSpec(memory_space=pltpu.VMEM)),
        input_output_aliases={t: 2 + t for t in range(n)},
        compiler_params=pltpu.CompilerParams(has_side_effects=EFFECT),
    )(*bufs, sems[0], sems[1])
    return (out[0], out[1]), list(out[2:2 + n])


def _chip_sum(g, xbuf, cm):
    _, _, r2, cols = g.shape
    tr = _rows_per_block(r2, cols, 1)

    def body(cm_ref, g_ref, x_ref, o_ref):
        o_ref[...] = (g_ref[...] + x_ref[...]).astype(BF16)

    return _call(
        body, name="grad_chip_sum",
        num_scalar_prefetch=1, grid=(3, r2 // tr),
        in_specs=[pl.BlockSpec((None, None, tr, cols), lambda j, r, cm: ((cm[1] + 1 + j) % N_CHIPS, cm[0], r, 0)),
                  pl.BlockSpec((None, tr, cols), lambda j, r, cm: ((cm[1] + 1 + j) % N_CHIPS, r, 0))],
        out_specs=pl.BlockSpec((None, tr, cols), lambda j, r, cm: (j, r, 0)),
        out_shape=jax.ShapeDtypeStruct((3, r2, cols), BF16), compiler_params=_params(2))(cm, g, xbuf)


def _total_sum(g, xbuf, rbuf, cm):
    _, _, r2, cols = g.shape
    tr = _rows_per_block(r2, cols, 3)

    def body(cm_ref, g_ref, x_ref, r_ref, o_ref):
        acc = g_ref[...] + x_ref[...]
        for j in range(3):
            acc = acc + r_ref[j].astype(F32)
        o_ref[...] = acc

    return _call(
        body, name="grad_total_sum",
        num_scalar_prefetch=1, grid=(r2 // tr,),
        in_specs=[pl.BlockSpec((None, None, tr, cols), lambda r, cm: (cm[1], cm[0], r, 0)),
                  pl.BlockSpec((None, tr, cols), lambda r, cm: (cm[1], r, 0)),
                  pl.BlockSpec((3, tr, cols), lambda r, cm: (0, r, 0))],
        out_specs=pl.BlockSpec((None, tr, cols), lambda r, cm: (cm[0], r, 0)),
        out_shape=jax.ShapeDtypeStruct((2, r2, cols), F32), compiler_params=_params(1))(cm, g, xbuf, rbuf)


def _small_allreduce(v):
    rows = v.shape[0]
    n_dev = 2 * N_CHIPS

    def body(v_ref, o_ref, buf, ssem, rsem):
        x, y, c, _ = _place()
        me = 4 * x + 2 * y + c
        buf[me] = v_ref[...]

        def copy(d, slot, to):
            return _remote(v_ref, buf.at[slot], ssem.at[d - 1], rsem.at[d - 1], (to // 4, (to // 2) % 2, to % 2))

        cps = [copy(d, me, (me + d) % n_dev) for d in range(1, n_dev)]
        for cp in cps:
            cp.start()
        for d in range(1, n_dev):
            copy(d, (me + n_dev - d) % n_dev, me).wait_recv()
        for cp in cps:
            cp.wait_send()
        acc = buf[0]
        for i in range(1, n_dev):
            acc = acc + buf[i]
        o_ref[...] = acc

    vm = pl.BlockSpec(memory_space=pltpu.VMEM)
    return _call(
        body, name="small_allreduce", in_specs=[vm], out_specs=vm,
        out_shape=jax.ShapeDtypeStruct(v.shape, F32),
        scratch_shapes=[pltpu.VMEM((n_dev, rows, LANES), F32), pltpu.SemaphoreType.DMA((n_dev - 1,)),
                        pltpu.SemaphoreType.DMA((n_dev - 1,))],
        compiler_params=pltpu.CompilerParams(vmem_limit_bytes=V7X_VMEM_LIMIT))(v)


def _adamw_math(w, g, m, v):
    m2 = ADAM_B1 * m + (1.0 - ADAM_B1) * g
    v2 = ADAM_B2 * v + (1.0 - ADAM_B2) * (g * g)
    m_hat = m2 / (1.0 - ADAM_B1 ** ADAM_STEP)
    v_hat = v2 / (1.0 - ADAM_B2 ** ADAM_STEP)
    return -ADAM_LR * (m_hat / (jnp.sqrt(v_hat) + ADAM_EPS) + ADAM_WD * w), m2, v2


def _adamw_layer(w, g, m, v, l, prev):
    _, rows, cols = w.shape
    tr = rows
    for cand in range(8, rows + 1, 8):
        if rows % cand == 0 and cand * cols * 4 <= 2 ** 21:
            tr = cand

    def body(w_ref, g_ref, m_ref, v_ref, *outs):
        go_ref, d_ref, m2_ref, v2_ref = outs[-4:]
        g = g_ref[...]
        go_ref[...] = g
        d_ref[...], m2_ref[...], v2_ref[...] = _adamw_math(w_ref[...], g, m_ref[...], v_ref[...])

    stack = pl.BlockSpec((None, tr, cols), lambda i: (l, i, 0))
    ins, specs, alias = [w, g, m, v], [stack, pl.BlockSpec((tr, cols), lambda i: (i, 0)), stack, stack], {}
    if prev is not None:
        ins += list(prev)
        specs += [ANY] * 4
        alias = {4 + i: i for i in range(4)}
    return _call(
        body, name="adamw", grid=(rows // tr,), in_specs=specs, out_specs=[stack] * 4,
        out_shape=[jax.ShapeDtypeStruct(w.shape, F32)] * 4, input_output_aliases=alias,
        compiler_params=_params(1))(*ins)


def _adamw_small(w, g, m, v):
    def body(w_ref, g_ref, m_ref, v_ref, d_ref, m2_ref, v2_ref):
        d_ref[...], m2_ref[...], v2_ref[...] = _adamw_math(w_ref[...], g_ref[...], m_ref[...], v_ref[...])

    spec = pl.BlockSpec(w.shape, lambda i: (0, 0))
    return _call(
        body, name="adamw_small", grid=(1,), in_specs=[spec] * 4, out_specs=[spec] * 3,
        out_shape=[jax.ShapeDtypeStruct(w.shape, F32)] * 3, compiler_params=_params(1))(w, g, m, v)


SMALL = ("norm_ffn1", "norm_mix", "sinks", "norm_out_sb", "norm_out_swa", "norm_ffn2", "rel_bias", "norm_final")
BIG = ("ffn1_gu", "ffn1_down", "w_in", "w_out", "ffn2_gu", "ffn2_down")


def _pack(parts):
    flat, n = [], 0
    for a in parts:
        a = a.reshape(-1).astype(F32)
        gap = -a.shape[0] % LANES
        flat += [a] + ([jnp.zeros((gap,), F32)] if gap else [])
        n += a.shape[0] + gap
    tail = -(n // LANES) % 8 * LANES
    return jnp.concatenate(flat + ([jnp.zeros((tail,), F32)] if tail else [])).reshape(-1, LANES)


def _unpack(packed, like):
    out, r = [], 0
    for a in like:
        n = math.prod(a.shape)
        nr = -(-n // LANES)
        out.append(packed[r:r + nr].reshape(-1)[:n].reshape(a.shape))
        r += nr
    return out


def _halved(a):
    k, r, cols = a.shape
    return a.reshape(k, 2, r // 2, cols)


def _weight_view(k, buf):
    full = buf.reshape(N_CHIPS, buf.shape[2] * 2, buf.shape[3])
    return full if k.endswith("_gu") else full.reshape(-1, D_MODEL)


def _grad_stack(k, g):
    if not k.endswith("_gu"):
        g = g.reshape(N_CHIPS, g.shape[0] // N_CHIPS, D_MODEL)
    return _halved(g)


def _empty_like_hbm(shape, dtype):
    return pltpu.with_memory_space_constraint(lax.empty(shape, dtype), pltpu.HBM)


def kernel(x, norm_ffn1, w_ffn1_gu, w_ffn1_down, norm_mix, w_in, sinks, norm_out_sb, norm_out_swa, w_out, norm_ffn2, w_ffn2_gu, w_ffn2_down, rel_bias, norm_final, loss_target, m_norm_ffn1, m_w_ffn1_gu, m_w_ffn1_down, m_norm_mix, m_w_in, m_sinks, m_norm_out_sb, m_norm_out_swa, m_w_out, m_norm_ffn2, m_w_ffn2_gu, m_w_ffn2_down, m_rel_bias, m_norm_final, v_norm_ffn1, v_w_ffn1_gu, v_w_ffn1_down, v_norm_mix, v_w_in, v_sinks, v_norm_out_sb, v_norm_out_swa, v_w_out, v_norm_ffn2, v_w_ffn2_gu, v_w_ffn2_down, v_rel_bias, v_norm_final):
    big_w = dict(ffn1_gu=w_ffn1_gu, ffn1_down=w_ffn1_down, w_in=w_in, w_out=w_out, ffn2_gu=w_ffn2_gu, ffn2_down=w_ffn2_down)
    big_m = dict(ffn1_gu=m_w_ffn1_gu, ffn1_down=m_w_ffn1_down, w_in=m_w_in, w_out=m_w_out, ffn2_gu=m_w_ffn2_gu, ffn2_down=m_w_ffn2_down)
    big_v = dict(ffn1_gu=v_w_ffn1_gu, ffn1_down=v_w_ffn1_down, w_in=v_w_in, w_out=v_w_out, ffn2_gu=v_w_ffn2_gu, ffn2_down=v_w_ffn2_down)
    small = dict(norm_ffn1=norm_ffn1, norm_mix=norm_mix, sinks=sinks, norm_out_sb=norm_out_sb, norm_out_swa=norm_out_swa,
                 norm_ffn2=norm_ffn2, rel_bias=rel_bias, norm_final=norm_final)
    small_m = dict(norm_ffn1=m_norm_ffn1, norm_mix=m_norm_mix, sinks=m_sinks, norm_out_sb=m_norm_out_sb,
                   norm_out_swa=m_norm_out_swa, norm_ffn2=m_norm_ffn2, rel_bias=m_rel_bias, norm_final=m_norm_final)
    small_v = dict(norm_ffn1=v_norm_ffn1, norm_mix=v_norm_mix, sinks=v_sinks, norm_out_sb=v_norm_out_sb,
                   norm_out_swa=v_norm_out_swa, norm_ffn2=v_norm_ffn2, rel_bias=v_rel_bias, norm_final=v_norm_final)
    for dct in (big_w, big_m, big_v):
        dct["w_in"] = jnp.swapaxes(dct["w_in"], 1, 2)
    _PREVIOUS[0] = None
    _, _, c, me = _place()
    cm = jnp.stack([c, me]).astype(jnp.int32)
    buckets = jnp.asarray(_bucket_table())
    ffn1, mix_in, rest = ("ffn1_gu", "ffn1_down"), ("w_in",), ("w_out", "ffn2_gu", "ffn2_down")

    def place(l, keys):
        return [_place_own(big_w[k], l, cm[1:]) for k in keys]

    def views(keys, bufs):
        return {k: _weight_view(k, b) for k, b in zip(keys, bufs)}

    def gather_start(tag, bufs):
        return _exchange_start(f"gather{tag}_ici_start", _plan_gather_ici, bufs, 3 * len(bufs))

    def gather_pass(tag, flight):
        return _exchange_pass(f"gather{tag}_pass", _plan_gather_ici, _plan_gather_d2d, flight[1], flight[0],
                              3 * len(flight[1]))

    def gather_done(tag, keys, flight):
        return views(keys, _exchange_wait(f"gather{tag}_d2d_wait", _plan_gather_d2d, flight[1], flight[0]))

    fly_ffn0 = gather_start("0a", place(0, ffn1))
    fly_in0 = gather_start("0b", place(0, mix_in))
    fly_rest0 = gather_start("0c", place(0, rest))
    bias = _bias_table(rel_bias, buckets)
    fly_ffn1 = gather_start("1a", place(1, ffn1))
    fly_in1 = gather_start("1b", place(1, mix_in))
    fly_rest1 = gather_start("1c", place(1, rest))
    n1 = _norm_cast(x[0], _row(norm_ffn1[0]))
    w0 = gather_done("0a", ffn1, gather_pass("0a", fly_ffn0))

    s0 = _fwd_ffn1(x[0], n1, w0, small, 0)
    w0.update(gather_done("0b", mix_in, gather_pass("0b", fly_in0)))
    _fwd_proj_sb(s0, w0)
    fly_rest0 = gather_pass("0c", fly_rest0)
    _fwd_swa(s0, small, 0, bias)
    w0.update(gather_done("0c", rest, fly_rest0))
    h, n1 = _fwd_out_ffn2(s0, w0, small, 0, _row(norm_ffn1[1]))
    w1 = gather_done("1a", ffn1, gather_pass("1a", fly_ffn1))
    s1 = _fwd_ffn1(h, n1, w1, small, 1)
    w1.update(gather_done("1b", mix_in, gather_pass("1b", fly_in1)))
    _fwd_proj_sb(s1, w1)
    fly_rest1 = gather_pass("1c", fly_rest1)
    _fwd_swa(s1, small, 1, bias)
    w1.update(gather_done("1c", rest, fly_rest1))
    h, _ = _fwd_out_ffn2(s1, w1, small, 1, _row(norm_final))
    dh32, dh16, dg_final, loss_row = _loss_head(h, _row(norm_final), loss_target[0])
    dh = (dh32, dh16)

    def landing(stacks, lead, dtype):
        return [_empty_like_hbm((lead,) + a.shape[2:], dtype) for a in stacks]

    def reduce_begin(tag, keys, gw):
        stacks = [_grad_stack(k, gw[k]) for k in keys]
        flight = _exchange_start(f"grad{tag}_sibling_start", _plan_grad_sibling,
                                 stacks + landing(stacks, N_CHIPS, F32), len(keys))
        return dict(tag=tag, keys=keys, stacks=stacks, flight=flight)

    def reduce_chips(st):
        n, (sems, bufs) = len(st["keys"]), st["flight"]
        bufs = _exchange_wait(f"grad{st['tag']}_sibling_wait", _plan_grad_sibling, bufs, sems)
        st["own"] = list(zip(bufs[:n], bufs[n:]))
        st["flight"] = _exchange_start(f"grad{st['tag']}_chips_start", _plan_grad_chips,
                                       [_chip_sum(g, z, cm) for g, z in st["own"]] + landing(st["stacks"], 3, BF16),
                                       3 * n)

    def reduce_halves(st):
        n, (sems, bufs) = len(st["keys"]), st["flight"]
        bufs = _exchange_wait(f"grad{st['tag']}_chips_wait", _plan_grad_chips, bufs, sems)
        halves = [_total_sum(g, x, z, cm) for (g, x), z in zip(st["own"], bufs[n:])]
        st["flight"] = _exchange_start(f"grad{st['tag']}_halves_start", _plan_grad_halves, halves, n)

    def reduce_end(st):
        sems, bufs = st["flight"]
        bufs = _exchange_wait(f"grad{st['tag']}_halves_wait", _plan_grad_halves, bufs, sems)
        return {k: b.reshape(big_w[k].shape[1:]) for k, b in zip(st["keys"], bufs)}

    def adamw(reduced, l, prev):
        return {k: _adamw_layer(big_w[k], g, big_m[k], big_v[k], l, None if prev is None else prev[k])
                for k, g in reduced.items()}

    gsm = [dict() for _ in range(DEPTH)]
    dbias = jnp.zeros((8, BLK, 2 * BLK), F32)
    dh, gw1, gs = _bwd_ffn(dh, s1, w1, small, 1, 2)
    gsm[1].update(gs)
    dh, gw, gs, dbias = _bwd_mix(dh, s1, w1, small, 1, bias, dbias)
    gw1.update(gw)
    gsm[1].update(gs)
    dh, gw, gs = _bwd_ffn(dh, s1, w1, small, 1, 1)
    gw1.update(gw)
    gsm[1].update(gs)

    red1 = reduce_begin("1", BIG, gw1)
    dh, gw0, gs = _bwd_ffn(dh, s0, w0, small, 0, 2)
    gsm[0].update(gs)
    reduce_chips(red1)
    dh, gw, gs, dbias = _bwd_mix(dh, s0, w0, small, 0, bias, dbias)
    gw0.update(gw)
    gsm[0].update(gs)
    red0a = reduce_begin("0a", ("ffn2_gu", "ffn2_down", "w_out", "w_in"), gw0)
    reduce_halves(red1)
    dgu = _bwd_ffn_dact(dh, s0, w0, 1)
    reduce_chips(red0a)
    dh, gw, gs = _bwd_ffn_rest(dh, dgu, s0, w0, small, 0, 1)
    gsm[0].update(gs)
    red0b = reduce_begin("0b", ffn1, gw)
    reduced1 = reduce_end(red1)
    stacks = adamw({k: reduced1[k] for k in ffn1}, 1, None)

    gsmall = {k: jnp.stack([gsm[l][k].reshape(-1) for l in range(DEPTH)]) for k in gsm[0]}
    gsmall["rel_bias"] = jnp.transpose(_bias_grad(dbias, buckets)[:, :N_BUCKETS])
    gsmall["norm_final"] = dg_final.reshape(-1)
    small_like = [small[k] for k in SMALL]
    pk = lambda dct: _pack([dct[k] for k in SMALL])
    red = _small_allreduce(_pack([gsmall[k] for k in SMALL] + [loss_row[0, :1]]))
    gs = _unpack(red, small_like + [loss_row[0, :1]])
    loss = gs[-1][0]
    gs = dict(zip(SMALL, gs[:-1]))

    ffn2 = ("ffn2_gu", "ffn2_down")
    reduce_chips(red0b)
    stacks.update(adamw({k: reduced1[k] for k in ("w_in", "w_out")}, 1, None))
    reduce_halves(red0a)
    stacks.update(adamw({k: reduced1[k] for k in ffn2}, 1, None))
    dlt, m2, v2 = _adamw_small(pk(small), pk(gs), pk(small_m), pk(small_v))
    reduced0a = reduce_end(red0a)
    stacks.update(adamw({k: reduced0a[k] for k in ffn2}, 0, stacks))
    reduce_halves(red0b)
    stacks.update(adamw({k: reduced0a[k] for k in ("w_in", "w_out")}, 0, stacks))
    stacks.update(adamw(reduce_end(red0b), 0, stacks))

    out_g, out_d, out_m, out_v = {}, {}, {}, {}
    for k in BIG:
        out_g[k], out_d[k], out_m[k], out_v[k] = [jnp.swapaxes(a, 1, 2) if k == "w_in" else a for a in stacks[k]]
    for dst, packed in ((out_d, dlt), (out_m, m2), (out_v, v2)):
        dst.update(zip(SMALL, _unpack(packed, small_like)))
    out_g.update(gs)

    order = ("norm_ffn1", "ffn1_gu", "ffn1_down", "norm_mix", "w_in", "sinks", "norm_out_sb", "norm_out_swa", "w_out",
             "norm_ffn2", "ffn2_gu", "ffn2_down", "rel_bias", "norm_final")
    return (loss, dh[0].reshape(x.shape), *[out_g[k] for k in order], *[out_d[k] for k in order],
            *[out_m[k] for k in order], *[out_v[k] for k in order])
```

```python
import math

import numpy as np
import jax
import jax.numpy as jnp
from jax import lax
from jax.experimental import pallas as pl
from jax.experimental.pallas import tpu as pltpu

F32 = jnp.float32
BF16 = jnp.bfloat16

D_MODEL = 1024
DEPTH = 2
HEAD_DIM = 64
BLK = 128
N_BUCKETS = 32
MAX_DISTANCE = 128
D_FF = 2816
EPS = 1e-6
NEG_INF = -1e30
SB_W = 512
SWA_W = 512
KV_W = 128
IN_W = 2304
SCALE = HEAD_DIM ** -0.5
N_CHIPS = 4
FS = 2 * D_FF // N_CHIPS
LANES = 128
V7X_VMEM_LIMIT = 56 * 2 ** 20
TM = 512
SB_KT = 512
SWA_G = 4

ADAM_LR = 0.001
ADAM_B1 = 0.9
ADAM_B2 = 0.999
ADAM_EPS = 1e-08
ADAM_WD = 0.01
ADAM_STEP = 10

MESH = pl.DeviceIdType.MESH
ANY = pl.BlockSpec(memory_space=pl.ANY)
HBM = pl.BlockSpec(memory_space=pltpu.HBM)
SEM = pl.BlockSpec(memory_space=pltpu.SEMAPHORE)
EFFECT = pltpu.SideEffectType.DATAFLOW_SIDE_EFFECTING


def _params(n_grid):
    return pltpu.CompilerParams(dimension_semantics=("arbitrary",) * n_grid, vmem_limit_bytes=V7X_VMEM_LIMIT)


_PREVIOUS = [None]


def _call(body, *, name, in_specs, out_specs, out_shape, grid=(), num_scalar_prefetch=0, scratch_shapes=(),
          input_output_aliases=None, compiler_params=None, hbm_args=0):
    n_in = len(in_specs)

    def run(*args):
        dep = _PREVIOUS[0]
        if any(dep is a for a in args):
            dep = None
        args = [pltpu.with_memory_space_constraint(a, pltpu.HBM) if i < hbm_args else a for i, a in enumerate(args)]
        specs = list(in_specs) + ([ANY] if dep is not None else [])
        k = num_scalar_prefetch + n_in
        fn = body if dep is None else (lambda *refs: body(*refs[:k], *refs[k + 1:]))
        if num_scalar_prefetch:
            shape = dict(grid_spec=pltpu.PrefetchScalarGridSpec(
                num_scalar_prefetch=num_scalar_prefetch, grid=grid, in_specs=specs, out_specs=out_specs,
                scratch_shapes=scratch_shapes))
        else:
            shape = dict(grid=grid, in_specs=specs, out_specs=out_specs, scratch_shapes=scratch_shapes)
        out = pl.pallas_call(fn, name=name, out_shape=out_shape, input_output_aliases=input_output_aliases or {},
                             compiler_params=compiler_params, **shape)(*args, *([] if dep is None else [dep]))
        _PREVIOUS[0] = jax.tree.leaves(out)[-1]
        return out

    return run


def _dot(a, b):
    return jnp.dot(a, b, preferred_element_type=F32)


def _dot_nt(a, b):
    return lax.dot_general(a, b, (((1,), (1,)), ((), ())), preferred_element_type=F32)


def _dot_tn(a, b):
    return lax.dot_general(a, b, (((0,), (0,)), ((), ())), preferred_element_type=F32)


def _rms_fwd(x, g):
    r = lax.rsqrt(jnp.mean(x * x, axis=-1, keepdims=True) + EPS)
    xh = x * r
    return xh * g, xh, r


def _rms_bwd(dy, xh, r, g):
    u = dy * g
    dx = r * (u - xh * jnp.mean(u * xh, axis=-1, keepdims=True))
    dg = jnp.sum(dy * xh, axis=0, keepdims=True)
    return dx, dg


def _softplus(z):
    neg_abs = lax.bitcast_convert_type(lax.bitcast_convert_type(z, jnp.int32) | jnp.int32(-2 ** 31), F32)
    sp = jnp.maximum(z, 0.0) + jnp.log(1.0 + jnp.exp(neg_abs))
    return sp, z - sp


def _norm_cast(h, g):
    t, w = h.shape

    def body(h_ref, g_ref, n_ref):
        y, _, _ = _rms_fwd(h_ref[...], g_ref[...])
        n_ref[...] = y.astype(BF16)

    return _call(
        body, name="norm_cast", grid=(t // TM,),
        in_specs=[pl.BlockSpec((TM, w), lambda i: (i, 0)), pl.BlockSpec((1, w), lambda i: (0, 0))],
        out_specs=pl.BlockSpec((TM, w), lambda i: (i, 0)),
        out_shape=jax.ShapeDtypeStruct((t, w), BF16), compiler_params=_params(1))(h, g)


def _ffn_gu(n, wgu):
    t, d = n.shape

    def body(n_ref, wg_ref, wu_ref, gu_ref, act_ref):
        x = n_ref[...]
        g = _dot(x, wg_ref[...])
        u = _dot(x, wu_ref[...])
        sig = jax.nn.sigmoid(g)
        silu = g * sig
        gu_ref[0] = (u * (sig + silu * (1.0 - sig))).astype(BF16)
        gu_ref[1] = silu.astype(BF16)
        act_ref[...] = (silu * u).astype(BF16)

    return _call(
        body, name="ffn_gu", grid=(2, t // TM),
        in_specs=[pl.BlockSpec((TM, d), lambda j, i: (i, 0)),
                  pl.BlockSpec((None, d, FS), lambda j, i: (j, 0, 0)),
                  pl.BlockSpec((None, d, FS), lambda j, i: (j + 2, 0, 0))],
        out_specs=[pl.BlockSpec((2, TM, FS), lambda j, i: (0, i, j)), pl.BlockSpec((TM, FS), lambda j, i: (i, j))],
        out_shape=[jax.ShapeDtypeStruct((2, t, D_FF), BF16), jax.ShapeDtypeStruct((t, D_FF), BF16)],
        compiler_params=_params(2))(n, wgu, wgu)


def _down_res(act, wdn, h, g_next):
    t, f = act.shape
    d = h.shape[1]

    def body(a_ref, w_ref, h_ref, g_ref, o_ref, n_ref):
        out = h_ref[...] + 0.5 * _dot(a_ref[...], w_ref[...])
        o_ref[...] = out
        n_ref[...] = _rms_fwd(out, g_ref[...])[0].astype(BF16)

    row = pl.BlockSpec((TM, d), lambda i: (i, 0))
    return _call(
        body, name="down_res", grid=(t // TM,),
        in_specs=[pl.BlockSpec((TM, f), lambda i: (i, 0)), pl.BlockSpec((f, d), lambda i: (0, 0)), row,
                  pl.BlockSpec((1, d), lambda i: (0, 0))],
        out_specs=[row, row],
        out_shape=[jax.ShapeDtypeStruct((t, d), F32), jax.ShapeDtypeStruct((t, d), BF16)],
        compiler_params=_params(1))(act, wdn, h, g_next)


def _proj(n, w_in_t):
    t, d = n.shape
    w = w_in_t.shape[0]

    def body(n_ref, w_ref, o_ref):
        o_ref[...] = _dot_nt(n_ref[...], w_ref[...]).astype(BF16)

    return _call(
        body, name="proj", grid=(t // TM,),
        in_specs=[pl.BlockSpec((TM, d), lambda i: (i, 0)), pl.BlockSpec((w, d), lambda i: (0, 0))],
        out_specs=pl.BlockSpec((TM, w), lambda i: (i, 0)),
        out_shape=jax.ShapeDtypeStruct((t, w), BF16), compiler_params=_params(1))(n, w_in_t)


def _out_res(o_sb, o_sw, g_sb, g_sw, w_out, h, g_next):
    t, d = h.shape

    def body(a_ref, b_ref, ga_ref, gb_ref, w_ref, h_ref, g_ref, o_ref, mix_ref, n_ref):
        ya, _, _ = _rms_fwd(a_ref[...], ga_ref[...])
        yb, _, _ = _rms_fwd(b_ref[...], gb_ref[...])
        mixed = jnp.concatenate([ya.astype(BF16), yb.astype(BF16)], axis=1)
        mix_ref[...] = mixed
        out = h_ref[...] + _dot(mixed, w_ref[...])
        o_ref[...] = out
        n_ref[...] = _rms_fwd(out, g_ref[...])[0].astype(BF16)

    row = pl.BlockSpec((TM, d), lambda i: (i, 0))
    return _call(
        body, name="out_res", grid=(t // TM,),
        in_specs=[pl.BlockSpec((TM, SB_W), lambda i: (i, 0)), pl.BlockSpec((TM, SWA_W), lambda i: (i, 0)),
                  pl.BlockSpec((1, SB_W), lambda i: (0, 0)), pl.BlockSpec((1, SWA_W), lambda i: (0, 0)),
                  pl.BlockSpec((d, d), lambda i: (0, 0)), row, pl.BlockSpec((1, d), lambda i: (0, 0))],
        out_specs=[row, row, row],
        out_shape=[jax.ShapeDtypeStruct((t, d), F32), jax.ShapeDtypeStruct((t, d), BF16),
                   jax.ShapeDtypeStruct((t, d), BF16)],
        compiler_params=_params(1))(o_sb, o_sw, g_sb, g_sw, w_out, h, g_next)


def _loss_head(h, g, tgt):
    t, d = h.shape

    def body(h_ref, g_ref, t_ref, dh_ref, dhb_ref, dg_ref, loss_ref):
        @pl.when(pl.program_id(0) == 0)
        def _():
            dg_ref[...] = jnp.zeros_like(dg_ref)
            loss_ref[...] = jnp.zeros_like(loss_ref)

        gg = g_ref[...]
        y, xh, r = _rms_fwd(h_ref[...], gg)
        err = y - t_ref[...]
        part = 0.5 * jnp.sum(jnp.sum(err * err, axis=1, keepdims=True) / d, axis=0, keepdims=True)
        loss_ref[...] += jnp.broadcast_to(part, loss_ref.shape)
        dx, dg = _rms_bwd(err / d, xh, r, gg)
        dh_ref[...] = dx
        dhb_ref[...] = dx.astype(BF16)
        dg_ref[...] += dg

    row = pl.BlockSpec((TM, d), lambda i: (i, 0))
    return _call(
        body, name="loss_head", grid=(t // TM,),
        in_specs=[row, pl.BlockSpec((1, d), lambda i: (0, 0)), row],
        out_specs=[row, row, pl.BlockSpec((1, d), lambda i: (0, 0)), pl.BlockSpec((1, LANES), lambda i: (0, 0))],
        out_shape=[jax.ShapeDtypeStruct((t, d), F32), jax.ShapeDtypeStruct((t, d), BF16),
                   jax.ShapeDtypeStruct((1, d), F32), jax.ShapeDtypeStruct((1, LANES), F32)],
        compiler_params=_params(1))(h, g, tgt)


def _ffn_dact(dh, wdn, gu):
    t, d = dh.shape
    tm = TM

    def body(dh_ref, w_ref, gu_ref, o_ref):
        da = 0.5 * _dot_nt(dh_ref[...].astype(BF16), w_ref[...])
        o_ref[0] = (da * gu_ref[0].astype(F32)).astype(BF16)
        o_ref[1] = (da * gu_ref[1].astype(F32)).astype(BF16)

    return _call(
        body, name="ffn_dact", grid=(2, t // tm),
        in_specs=[pl.BlockSpec((tm, d), lambda j, i: (i, 0)), pl.BlockSpec((FS, d), lambda j, i: (j, 0)),
                  pl.BlockSpec((2, tm, FS), lambda j, i: (0, i, j))],
        out_specs=pl.BlockSpec((2, tm, FS), lambda j, i: (0, i, j)),
        out_shape=jax.ShapeDtypeStruct((2, t, D_FF), BF16), compiler_params=_params(2))(dh, wdn, gu)


def _dn_norm_bwd(a, a_spec, w, w_spec, nk, dh, h_in, g, w_transposed=False, tm=TM):
    t, d = dh.shape
    mm = _dot if w_transposed else _dot_nt

    def body(a_ref, w_ref, dh_ref, h_ref, g_ref, o_ref, ob_ref, dg_ref, acc_ref):
        i, k = pl.program_id(0), pl.program_id(1)

        if nk > 1:
            @pl.when(k == 0)
            def _():
                acc_ref[...] = mm(a_ref[...], w_ref[...])

            @pl.when((k > 0) & (k < nk - 1))
            def _():
                acc_ref[...] += mm(a_ref[...], w_ref[...])

        @pl.when(k == nk - 1)
        def _():
            gg = g_ref[...]
            dg = jnp.zeros_like(gg)
            for rows in (slice(r, r + TM // 2) for r in range(0, tm, TM // 2)):
                dn = mm(a_ref[rows, :], w_ref[...])
                if nk > 1:
                    dn = dn + acc_ref[rows, :]
                _, xh, r = _rms_fwd(h_ref[rows, :], gg)
                dx, dg_rows = _rms_bwd(dn, xh, r, gg)
                out = dh_ref[rows, :] + dx
                o_ref[rows, :] = out
                ob_ref[rows, :] = out.astype(BF16)
                dg = dg + dg_rows

            @pl.when(i == 0)
            def _():
                dg_ref[...] = dg

            @pl.when(i > 0)
            def _():
                dg_ref[...] += dg

    row = pl.BlockSpec((tm, d), lambda i, k: (i, 0))
    return _call(
        body, name="dn_norm_bwd", grid=(t // tm, nk),
        in_specs=[a_spec, w_spec, row, row, pl.BlockSpec((1, d), lambda i, k: (0, 0))],
        out_specs=[row, row, pl.BlockSpec((1, d), lambda i, k: (0, 0))],
        out_shape=[jax.ShapeDtypeStruct((t, d), F32), jax.ShapeDtypeStruct((t, d), BF16),
                   jax.ShapeDtypeStruct((1, d), F32)],
        scratch_shapes=[pltpu.VMEM((tm, d), F32)], compiler_params=_params(2))(a, w, dh, h_in, g)


def _ffn_dn(dgu, wgu, dh, h_in, g):
    d = dh.shape[1]
    tm = 2 * TM
    return _dn_norm_bwd(
        dgu, pl.BlockSpec((None, tm, FS), lambda i, k: (k // 2, i, k % 2)),
        wgu, pl.BlockSpec((None, d, FS), lambda i, k: (k, 0, 0)), N_CHIPS, dh, h_in, g, tm=tm)


def _mix_dn(dproj, w_in_t, dh, h_in, g):
    d = dh.shape[1]
    w = dproj.shape[1]
    return _dn_norm_bwd(
        dproj, pl.BlockSpec((TM, w), lambda i, k: (i, 0)),
        w_in_t, pl.BlockSpec((w, d), lambda i, k: (0, 0)), 1, dh, h_in, g, w_transposed=True)


def _dmixed(dh, w_out, o_sb, o_sw, g_sb, g_sw):
    t, d = dh.shape

    def body(dh_ref, w_ref, a_ref, b_ref, ga_ref, gb_ref, o_ref, dga_ref, dgb_ref):
        i = pl.program_id(0)
        dm = _dot_nt(dh_ref[...].astype(BF16), w_ref[...])
        _, xa, ra = _rms_fwd(a_ref[...], ga_ref[...])
        _, xb, rb = _rms_fwd(b_ref[...], gb_ref[...])
        da, dga = _rms_bwd(dm[:, :SB_W], xa, ra, ga_ref[...])
        db, dgb = _rms_bwd(dm[:, SB_W:], xb, rb, gb_ref[...])
        o_ref[...] = jnp.concatenate([da.astype(BF16), db.astype(BF16)], axis=1)

        @pl.when(i == 0)
        def _():
            dga_ref[...] = dga
            dgb_ref[...] = dgb

        @pl.when(i > 0)
        def _():
            dga_ref[...] += dga
            dgb_ref[...] += dgb

    return _call(
        body, name="dmixed", grid=(t // TM,),
        in_specs=[pl.BlockSpec((TM, d), lambda i: (i, 0)), pl.BlockSpec((d, d), lambda i: (0, 0)),
                  pl.BlockSpec((TM, SB_W), lambda i: (i, 0)), pl.BlockSpec((TM, SWA_W), lambda i: (i, 0)),
                  pl.BlockSpec((1, SB_W), lambda i: (0, 0)), pl.BlockSpec((1, SWA_W), lambda i: (0, 0))],
        out_specs=[pl.BlockSpec((TM, d), lambda i: (i, 0)), pl.BlockSpec((1, SB_W), lambda i: (0, 0)),
                   pl.BlockSpec((1, SWA_W), lambda i: (0, 0))],
        out_shape=[jax.ShapeDtypeStruct((t, d), BF16), jax.ShapeDtypeStruct((1, SB_W), F32),
                   jax.ShapeDtypeStruct((1, SWA_W), F32)],
        compiler_params=_params(1))(dh, w_out, o_sb, o_sw, g_sb, g_sw)


def _wgrad(name, a, a_spec, b, b_spec, grid, out_shape, out_spec, scale):
    def body(a_ref, b_ref, o_ref):
        r = _dot_tn(a_ref[...], b_ref[...].astype(BF16))
        o_ref[...] = r if scale == 1.0 else scale * r

    return _call(
        body, name=name, grid=grid, in_specs=[a_spec, b_spec], out_specs=out_spec,
        out_shape=jax.ShapeDtypeStruct(out_shape, F32), compiler_params=_params(len(grid)))(a, b)


def _wgrad_gu(n, dgu):
    t, d = n.shape
    return _wgrad(
        "wgrad_gu", n, pl.BlockSpec((t, TM), lambda s, r: (0, r)),
        dgu, pl.BlockSpec((None, t, FS), lambda s, r: (s // 2, 0, s % 2)), (N_CHIPS, d // TM),
        (N_CHIPS, d, FS), pl.BlockSpec((None, TM, FS), lambda s, r: (s, r, 0)), 1.0)


def _wgrad_down(act, dh):
    t, d = dh.shape
    return _wgrad(
        "wgrad_down", act, pl.BlockSpec((t, FS), lambda s: (0, s)), dh, pl.BlockSpec((t, d), lambda s: (0, 0)),
        (2,), (D_FF, d), pl.BlockSpec((FS, d), lambda s: (s, 0)), 0.5)


def _wgrad_out(mixed, dh):
    t, d = dh.shape
    return _wgrad(
        "wgrad_out", mixed, pl.BlockSpec((t, TM), lambda s: (0, s)), dh, pl.BlockSpec((t, d), lambda s: (0, 0)),
        (d // TM,), (d, d), pl.BlockSpec((TM, d), lambda s: (s, 0)), 1.0)


def _wgrad_in(n, dproj):
    t, d = n.shape
    w = dproj.shape[1]
    tw = w // 3
    return _wgrad(
        "wgrad_in", dproj, pl.BlockSpec((t, tw), lambda s: (0, s)), n, pl.BlockSpec((t, d), lambda s: (0, 0)),
        (3,), (w, d), pl.BlockSpec((tw, d), lambda s: (s, 0)), 1.0)


def _tri(rel):
    row = lax.broadcasted_iota(jnp.int32, (BLK, BLK), 0)
    col = lax.broadcasted_iota(jnp.int32, (BLK, BLK), 1)
    m = rel(row, col).astype(BF16)
    return jnp.concatenate([m, m], axis=0)


def _scan_dot(x, tri2):
    hi = x.astype(BF16)
    lo = (x - hi.astype(F32)).astype(BF16)
    return _dot(jnp.concatenate([hi, lo], axis=1), tri2)


def _head_masks():
    lane = lax.broadcasted_iota(jnp.int32, (1, LANES), 1)
    return [lane < HEAD_DIM, lane >= HEAD_DIM]


SB_PAIRS = 2
SB_ROWS = 2 * SB_PAIRS * BLK


def _sb_causal():
    row = lax.broadcasted_iota(jnp.int32, (SB_ROWS, BLK), 0) & (BLK - 1)
    return lax.broadcasted_iota(jnp.int32, (SB_ROWS, BLK), 1) < row


def _sb_mask_last(x, causal):
    own = jnp.where(causal, x[:, -BLK:], 0.0)
    return own if x.shape[1] == BLK else jnp.concatenate([x[:, :-BLK], own], axis=1)


def _sb_stack(x, hm):
    return jnp.concatenate([jnp.where(m, x[:, p * LANES:(p + 1) * LANES], jnp.zeros((BLK, LANES), x.dtype))
                            for p in range(SB_PAIRS) for m in hm], axis=0)


def _sb_unstack(y, hm):
    return jnp.concatenate([jnp.where(hm[0], y[2 * p * BLK:(2 * p + 1) * BLK], y[(2 * p + 1) * BLK:(2 * p + 2) * BLK])
                            for p in range(SB_PAIRS)], axis=1)


def _sb_pairs():
    return [(slice(2 * p * BLK, (2 * p + 2) * BLK), slice(p * LANES, (p + 1) * LANES)) for p in range(SB_PAIRS)]


def _sb_fwd(proj):
    t = proj.shape[0]
    nb = SB_KT // BLK
    wide = SB_PAIRS * LANES

    def body(q_ref, k_ref, v_ref, o_ref, tot_ref):
        hm = _head_masks()
        causal = _sb_causal()
        pairs = _sb_pairs()
        after = _tri(lambda r, c: r > c)

        def tile(qh, start, n_blk, carry, acc, own):
            ks = pl.ds(pl.multiple_of(start, BLK), n_blk * BLK)
            z = jnp.concatenate([_dot_nt(qh[rows], k_ref[ks, lanes]) for rows, lanes in pairs], axis=0)
            sp, zs = _softplus(z)
            spm = _sb_mask_last(sp, causal) if own else sp
            sufs = [None] * n_blk
            for b in reversed(range(n_blk)):
                blk = spm[:, b * BLK:(b + 1) * BLK]
                sufs[b] = carry + _scan_dot(blk, after)
                carry = carry + jnp.sum(blk, axis=1, keepdims=True)
            w = jnp.exp(zs - jnp.concatenate(sufs, axis=1))
            wb = (_sb_mask_last(w, causal) if own else w).astype(BF16)
            return carry, acc + jnp.concatenate([_dot(wb[rows], v_ref[ks, lanes]) for rows, lanes in pairs], axis=0)

        def qblock(g, j):
            qs = pl.ds(pl.multiple_of(g * SB_KT + j * BLK, BLK), BLK)
            qh = _sb_stack(q_ref[qs, :] * SCALE, hm)
            c0 = tile(qh, g * SB_KT, j + 1, jnp.zeros((SB_ROWS, 1), F32), jnp.zeros((SB_ROWS, LANES), F32), True)
            carry, acc = lax.fori_loop(0, g, lambda n, c: tile(qh, (g - 1 - n) * SB_KT, nb, c[0], c[1], False), c0)
            o_ref[qs, :] = _sb_unstack(acc, hm)
            for h in range(2 * SB_PAIRS):
                tot_ref[h, qs, :] = carry[h * BLK:(h + 1) * BLK]

        def group(g, _):
            for j in range(nb):
                qblock(g, j)
            return 0

        lax.fori_loop(0, t // SB_KT, group, 0)

    col_blk = lambda off: pl.BlockSpec((t, wide), lambda g: (0, off + g))
    n_steps = SB_W // wide
    return _call(
        body, name="sb_fwd", grid=(n_steps,), in_specs=[col_blk(0), col_blk(n_steps), col_blk(2 * n_steps)],
        out_specs=[col_blk(0), pl.BlockSpec((2 * SB_PAIRS, t, 1), lambda g: (g, 0, 0))],
        out_shape=[jax.ShapeDtypeStruct((t, SB_W), F32), jax.ShapeDtypeStruct((8, t, 1), F32)],
        compiler_params=_params(1))(proj, proj, proj)


def _sb_bwd(proj, d_o, tot):
    t = proj.shape[0]
    nb = SB_KT // BLK
    wide = SB_PAIRS * LANES

    def body(q_ref, k_ref, v_ref, do_ref, tot_ref, dq_ref, dk_ref, dv_ref, dk_acc, dv_acc):
        hm = _head_masks()
        causal = _sb_causal()
        pairs = _sb_pairs()
        before = _tri(lambda r, c: r < c)
        upto = _tri(lambda r, c: r <= c)
        dk_acc[...] = jnp.zeros_like(dk_acc)
        dv_acc[...] = jnp.zeros_like(dv_acc)

        def tile(qh, doh, tt, start, n_blk, pre, ecum, dq, own):
            ks = pl.ds(pl.multiple_of(start, BLK), n_blk * BLK)
            k = k_ref[ks, :]
            v = v_ref[ks, :]
            z = jnp.concatenate([_dot_nt(qh[rows], k[:, lanes]) for rows, lanes in pairs], axis=0)
            sp, zs = _softplus(z)
            spm = _sb_mask_last(sp, causal) if own else sp
            pres = []
            for b in range(n_blk):
                blk = spm[:, b * BLK:(b + 1) * BLK]
                pres.append(pre + _scan_dot(blk, before))
                pre = pre + jnp.sum(blk, axis=1, keepdims=True)
            logw = z - (tt - jnp.concatenate(pres, axis=1))
            if own:
                logw = jnp.minimum(logw, 0.0)
            w = jnp.exp(logw)
            if own:
                w = _sb_mask_last(w, causal)
            e = w * jnp.concatenate([_dot_nt(doh[rows], v[:, lanes]) for rows, lanes in pairs], axis=0)
            incs = []
            for b in range(n_blk):
                blk = e[:, b * BLK:(b + 1) * BLK]
                incs.append(ecum + _scan_dot(blk, upto))
                ecum = ecum + jnp.sum(blk, axis=1, keepdims=True)
            dz = e - jnp.exp(zs) * jnp.concatenate(incs, axis=1)
            if own:
                dz = _sb_mask_last(dz, causal)
            dzb = dz.astype(BF16)
            wb = w.astype(BF16)
            for rows, lanes in pairs:
                dk_acc[ks, lanes] += _dot_tn(dzb[rows], qh[rows])
                dv_acc[ks, lanes] += _dot_tn(wb[rows], doh[rows])
            return pre, ecum, dq + jnp.concatenate([_dot(dzb[rows], k[:, lanes]) for rows, lanes in pairs], axis=0)

        def qblock(g, j):
            qs = pl.ds(pl.multiple_of(g * SB_KT + j * BLK, BLK), BLK)
            qh = _sb_stack(q_ref[qs, :] * SCALE, hm)
            doh = _sb_stack(do_ref[qs, :], hm)
            tt = jnp.concatenate([tot_ref[h, qs, :] for h in range(2 * SB_PAIRS)], axis=0)
            c0 = (jnp.zeros((SB_ROWS, 1), F32), jnp.zeros((SB_ROWS, 1), F32), jnp.zeros((SB_ROWS, LANES), F32))
            c = lax.fori_loop(0, g, lambda kt, c: tile(qh, doh, tt, kt * SB_KT, nb, c[0], c[1], c[2], False), c0)
            dq = tile(qh, doh, tt, g * SB_KT, j + 1, c[0], c[1], c[2], True)[2]
            dq_ref[qs, :] = (_sb_unstack(dq, hm) * SCALE).astype(BF16)

        def group(g, _):
            for j in range(nb):
                qblock(g, j)
            return 0

        lax.fori_loop(0, t // SB_KT, group, 0)
        dk_ref[...] = dk_acc[...].astype(BF16)
        dv_ref[...] = dv_acc[...].astype(BF16)

    col_blk = lambda off: pl.BlockSpec((t, wide), lambda g: (0, off + g))
    n_steps = SB_W // wide
    out = jax.ShapeDtypeStruct((t, SB_W), BF16)
    return _call(
        body, name="sb_bwd", grid=(n_steps,),
        in_specs=[col_blk(0), col_blk(n_steps), col_blk(2 * n_steps), col_blk(0),
                  pl.BlockSpec((2 * SB_PAIRS, t, 1), lambda g: (g, 0, 0))],
        out_specs=[col_blk(0), col_blk(0), col_blk(0)], out_shape=[out, out, out],
        scratch_shapes=[pltpu.VMEM((t, wide), F32), pltpu.VMEM((t, wide), F32)],
        compiler_params=_params(1))(proj, proj, proj, d_o, tot)


def _bucket_table():
    a = np.arange(BLK)[:, None]
    c = np.arange(2 * BLK)[None, :]
    dist = np.maximum(BLK + a - c, 0)
    max_exact = N_BUCKETS // 2
    dd = np.maximum(dist, 1).astype(np.float32)
    large = max_exact + (np.log(dd / max_exact) / math.log(MAX_DISTANCE / max_exact)
                         * (N_BUCKETS - max_exact)).astype(np.int32)
    large = np.minimum(large, N_BUCKETS - 1)
    return np.where(dist < max_exact, dist, large).astype(np.int32)


SWA_H = 8


def _swa_band_masks():
    row = lax.broadcasted_iota(jnp.int32, (SWA_H * BLK, 2 * BLK), 0) & (BLK - 1)
    col = lax.broadcasted_iota(jnp.int32, (SWA_H * BLK, 2 * BLK), 1)
    own = lax.broadcasted_iota(jnp.int32, (SWA_H * BLK, BLK), 1) <= (
        lax.broadcasted_iota(jnp.int32, (SWA_H * BLK, BLK), 0) & (BLK - 1))
    return (col > row) & ((col < BLK) | (col - BLK <= row)), own


def _swa_stack(ref, qs, hm, scale):
    parts = []
    for hq in range(SWA_H):
        kvh = hq // SWA_G
        x = ref[qs, (hq // 2) * LANES:(hq // 2 + 1) * LANES].astype(F32)
        if hq % 2 != kvh:
            x = pltpu.roll(x, HEAD_DIM, 1)
        parts.append(jnp.where(hm[kvh], x * scale, 0.0).astype(BF16))
    return jnp.concatenate(parts, axis=0)


def _swa_unstack(x8, hm):
    heads = []
    for hq in range(SWA_H):
        x = x8[hq * BLK:(hq + 1) * BLK]
        heads.append(pltpu.roll(x, HEAD_DIM, 1) if hq % 2 != hq // SWA_G else x)
    return [jnp.where(hm[0], heads[2 * p], heads[2 * p + 1]) for p in range(SWA_H // 2)]


def _swa_scores(q8, kb, bias_ref, mask, cols):
    bias8 = jnp.concatenate([bias_ref[hq, :, cols] for hq in range(SWA_H)], axis=0)
    return jnp.where(mask, _dot_nt(q8, kb) + bias8, NEG_INF)


def _swa_sinks(sink_ref):
    return jnp.concatenate([jnp.broadcast_to(sink_ref[hq:hq + 1, 0:1], (BLK, 1)) for hq in range(SWA_H)], axis=0)


def _swa_fwd(proj, bias, sinks_b):
    t = proj.shape[0]
    nq = t // BLK

    def body(q_ref, k_ref, v_ref, bias_ref, sink_ref, o_ref, lse_ref):
        hm = _head_masks()
        band, own = _swa_band_masks()

        def qblock(i, prev):
            qs = pl.ds(pl.multiple_of(i * BLK, BLK), BLK)
            if prev:
                ks, mask, cols = pl.ds(pl.multiple_of((i - 1) * BLK, BLK), 2 * BLK), band, slice(None)
            else:
                ks, mask, cols = qs, own, slice(BLK, None)
            q8 = _swa_stack(q_ref, qs, hm, SCALE)
            sink8 = _swa_sinks(sink_ref)
            s = _swa_scores(q8, k_ref[ks, :], bias_ref, mask, cols)
            m = jnp.maximum(jnp.max(s, axis=1, keepdims=True), sink8)
            p = jnp.exp(s - m)
            den = jnp.sum(p, axis=1, keepdims=True) + jnp.exp(sink8 - m)
            o8 = _dot((p * (1.0 / den)).astype(BF16), v_ref[ks, :])
            lse8 = m + jnp.log(den)
            for hq in range(SWA_H):
                lse_ref[hq, qs, :] = lse8[hq * BLK:(hq + 1) * BLK]
            for pp, o in enumerate(_swa_unstack(o8, hm)):
                o_ref[qs, pp * LANES:(pp + 1) * LANES] = o

        qblock(0, False)

        def step(i, _):
            qblock(i, True)
            return 0

        lax.fori_loop(1, nq, step, 0)

    return _call(
        body, name="swa_fwd", grid=(1,),
        in_specs=[pl.BlockSpec((t, SWA_W), lambda i: (0, 3)), pl.BlockSpec((t, KV_W), lambda i: (0, 16)),
                  pl.BlockSpec((t, KV_W), lambda i: (0, 17)), pl.BlockSpec((8, BLK, 2 * BLK), lambda i: (0, 0, 0)),
                  pl.BlockSpec((8, LANES), lambda i: (0, 0))],
        out_specs=[pl.BlockSpec((t, SWA_W), lambda i: (0, 0)), pl.BlockSpec((8, t, 1), lambda i: (0, 0, 0))],
        out_shape=[jax.ShapeDtypeStruct((t, SWA_W), F32), jax.ShapeDtypeStruct((8, t, 1), F32)],
        compiler_params=_params(1))(proj, proj, proj, bias, sinks_b)


def _swa_bwd(proj, d_o, lse, bias, sinks_b, dbias_in):
    t = proj.shape[0]
    nq = t // BLK

    def body(q_ref, k_ref, v_ref, do_ref, lse_ref, bias_ref, sink_ref, dbi_ref,
             dq_ref, dk_ref, dv_ref, dsink_ref, dbias_ref, dk_acc, dv_acc):
        hm = _head_masks()
        band, own = _swa_band_masks()
        dk_acc[...] = jnp.zeros_like(dk_acc)
        dv_acc[...] = jnp.zeros_like(dv_acc)
        dbias_ref[...] = dbi_ref[...]

        def qblock(i, prev, dsink8):
            qs = pl.ds(pl.multiple_of(i * BLK, BLK), BLK)
            if prev:
                ks, mask, cols = pl.ds(pl.multiple_of((i - 1) * BLK, BLK), 2 * BLK), band, slice(None)
            else:
                ks, mask, cols = qs, own, slice(BLK, None)
            q8 = _swa_stack(q_ref, qs, hm, SCALE)
            do8 = _swa_stack(do_ref, qs, hm, 1.0)
            sink8 = _swa_sinks(sink_ref)
            lse8 = jnp.concatenate([lse_ref[hq, qs, :] for hq in range(SWA_H)], axis=0)
            kb = k_ref[ks, :]
            p = jnp.exp(_swa_scores(q8, kb, bias_ref, mask, cols) - lse8)
            dp = _dot_nt(do8, v_ref[ks, :])
            delta = jnp.sum(p * dp, axis=1, keepdims=True)
            ds = p * (dp - delta)
            for hq in range(SWA_H):
                dbias_ref[hq, :, cols] += ds[hq * BLK:(hq + 1) * BLK]
            dsb = ds.astype(BF16)
            dk_acc[ks, :] += _dot_tn(dsb, q8)
            dv_acc[ks, :] += _dot_tn(p.astype(BF16), do8)
            for pp, dq in enumerate(_swa_unstack(_dot(dsb, kb) * SCALE, hm)):
                dq_ref[qs, pp * LANES:(pp + 1) * LANES] = dq.astype(BF16)
            return dsink8 - jnp.exp(sink8 - lse8) * delta

        ds0 = qblock(0, False, jnp.zeros((SWA_H * BLK, 1), F32))
        ds8 = lax.fori_loop(1, nq, lambda i, c: qblock(i, True, c), ds0)
        for hq in range(SWA_H):
            dsink_ref[hq:hq + 1, :] = jnp.broadcast_to(
                jnp.sum(ds8[hq * BLK:(hq + 1) * BLK], axis=0, keepdims=True), (1, LANES))

        dk_ref[...] = dk_acc[...].astype(BF16)
        dv_ref[...] = dv_acc[...].astype(BF16)

    full3 = pl.BlockSpec((8, BLK, 2 * BLK), lambda i: (0, 0, 0))
    kv = jax.ShapeDtypeStruct((t, KV_W), BF16)
    return _call(
        body, name="swa_bwd", grid=(1,),
        in_specs=[pl.BlockSpec((t, SWA_W), lambda i: (0, 3)), pl.BlockSpec((t, KV_W), lambda i: (0, 16)),
                  pl.BlockSpec((t, KV_W), lambda i: (0, 17)), pl.BlockSpec((t, SWA_W), lambda i: (0, 1)),
                  pl.BlockSpec((8, t, 1), lambda i: (0, 0, 0)), full3, pl.BlockSpec((8, LANES), lambda i: (0, 0)),
                  full3],
        out_specs=[pl.BlockSpec((t, SWA_W), lambda i: (0, 0)), pl.BlockSpec((t, KV_W), lambda i: (0, 0)),
                   pl.BlockSpec((t, KV_W), lambda i: (0, 0)), pl.BlockSpec((8, LANES), lambda i: (0, 0)), full3],
        out_shape=[jax.ShapeDtypeStruct((t, SWA_W), BF16), kv, kv, jax.ShapeDtypeStruct((8, LANES), F32),
                   jax.ShapeDtypeStruct((8, BLK, 2 * BLK), F32)],
        scratch_shapes=[pltpu.VMEM((t, KV_W), F32), pltpu.VMEM((t, KV_W), F32)],
        compiler_params=_params(1))(proj, proj, proj, d_o, lse, bias, sinks_b, dbias_in)


def _concat_cols(parts):
    t = parts[0].shape[0]
    widths = [a.shape[1] for a in parts]

    def body(*refs):
        refs[-1][...] = jnp.concatenate([r[...] for r in refs[:-1]], axis=1)

    return _call(
        body, name="concat_cols", grid=(t // TM,),
        in_specs=[pl.BlockSpec((TM, w), lambda i: (i, 0)) for w in widths],
        out_specs=pl.BlockSpec((TM, sum(widths)), lambda i: (i, 0)),
        out_shape=jax.ShapeDtypeStruct((t, sum(widths)), parts[0].dtype), compiler_params=_params(1))(*parts)


def _bias_table(rel_bias, buckets):
    def body(rb_ref, b_ref, o_ref):
        bk = b_ref[...]
        for h in range(8):
            acc = jnp.zeros((BLK, 2 * BLK), F32)
            for b in range(N_BUCKETS):
                acc = jnp.where(bk == b, rb_ref[b, h], acc)
            o_ref[h] = acc

    return _call(
        body, name="bias_table", grid=(1,),
        in_specs=[pl.BlockSpec(memory_space=pltpu.SMEM), pl.BlockSpec((BLK, 2 * BLK), lambda i: (0, 0))],
        out_specs=pl.BlockSpec((8, BLK, 2 * BLK), lambda i: (0, 0, 0)),
        out_shape=jax.ShapeDtypeStruct((8, BLK, 2 * BLK), F32), compiler_params=_params(1))(rel_bias, buckets)


def _bias_grad(dbias, buckets):
    def body(d_ref, b_ref, o_ref):
        lane = lax.broadcasted_iota(jnp.int32, (1, LANES), 1)
        bk = b_ref[...]
        for h in range(8):
            d = d_ref[h]
            acc = jnp.zeros((1, LANES), F32)
            for b in range(N_BUCKETS):
                s = jnp.sum(jnp.sum(jnp.where(bk == b, d, 0.0), axis=0, keepdims=True), axis=1, keepdims=True)
                acc = acc + jnp.where(lane == b, s, 0.0)
            o_ref[h:h + 1, :] = acc

    return _call(
        body, name="bias_grad", grid=(1,),
        in_specs=[pl.BlockSpec((8, BLK, 2 * BLK), lambda i: (0, 0, 0)), pl.BlockSpec((BLK, 2 * BLK), lambda i: (0, 0))],
        out_specs=pl.BlockSpec((8, LANES), lambda i: (0, 0)),
        out_shape=jax.ShapeDtypeStruct((8, LANES), F32), compiler_params=_params(1))(dbias, buckets)


def _row(a):
    return a.reshape(1, -1)


def _fwd_ffn1(h, n1, w, small, l):
    s = {"h0": h, "n1": n1}
    s["gu1"], s["act1"] = _ffn_gu(n1, w["ffn1_gu"])
    s["h1"], s["nm"] = _down_res(s["act1"], w["ffn1_down"], h, _row(small["norm_mix"][l]))
    return s


def _fwd_proj_sb(s, w):
    s["proj"] = _proj(s["nm"], w["w_in"])
    s["o_sb"], s["tot"] = _sb_fwd(s["proj"])


def _fwd_swa(s, small, l, bias):
    s["sinks_b"] = jnp.broadcast_to(small["sinks"][l][:, None], (8, LANES))
    s["o_sw"], s["lse"] = _swa_fwd(s["proj"], bias, s["sinks_b"])


def _fwd_out_ffn2(s, w, small, l, g_after):
    s["h2"], s["mixed"], s["n2"] = _out_res(
        s["o_sb"], s["o_sw"], _row(small["norm_out_sb"][l]), _row(small["norm_out_swa"][l]), w["w_out"], s["h1"],
        _row(small["norm_ffn2"][l]))
    s["gu2"], s["act2"] = _ffn_gu(s["n2"], w["ffn2_gu"])
    return _down_res(s["act2"], w["ffn2_down"], s["h2"], g_after)


def _bwd_ffn_dact(dh, s, w, which):
    return _ffn_dact(dh[1], w[f"ffn{which}_down"], s[f"gu{which}"])


def _bwd_ffn_rest(dh, dgu, s, w, small, l, which):
    h_in, norm = (s["h0"], "norm_ffn1") if which == 1 else (s["h2"], "norm_ffn2")
    g_down = _wgrad_down(s[f"act{which}"], dh[1])
    g_gu = _wgrad_gu(s[f"n{which}"], dgu)
    dh32, dh16, dg = _ffn_dn(dgu, w[f"ffn{which}_gu"], dh[0], h_in, _row(small[norm][l]))
    return (dh32, dh16), {f"ffn{which}_down": g_down, f"ffn{which}_gu": g_gu}, {norm: dg}


def _bwd_ffn(dh, s, w, small, l, which):
    return _bwd_ffn_rest(dh, _bwd_ffn_dact(dh, s, w, which), s, w, small, l, which)


def _bwd_mix(dh, s, w, small, l, bias, dbias):
    g_out = _wgrad_out(s["mixed"], dh[1])
    d_o, dg_sb, dg_sw = _dmixed(dh[1], w["w_out"], s["o_sb"], s["o_sw"], _row(small["norm_out_sb"][l]),
                                _row(small["norm_out_swa"][l]))
    dq_sb, dk_sb, dv_sb = _sb_bwd(s["proj"], d_o, s["tot"])
    dq_sw, dk_sw, dv_sw, dsink, dbias = _swa_bwd(s["proj"], d_o, s["lse"], bias, s["sinks_b"], dbias)
    dproj = _concat_cols([dq_sb, dk_sb, dv_sb, dq_sw, dk_sw, dv_sw])
    g_in = _wgrad_in(s["nm"], dproj)
    dh32, dh16, dg_mix = _mix_dn(dproj, w["w_in"], dh[0], s["h1"], _row(small["norm_mix"][l]))
    gs = {"norm_out_sb": dg_sb, "norm_out_swa": dg_sw, "sinks": dsink[:, 0], "norm_mix": dg_mix}
    return (dh32, dh16), {"w_out": g_out, "w_in": g_in}, gs, dbias


def _place():
    x, y, c = lax.axis_index("x"), lax.axis_index("y"), lax.axis_index("c")
    return x, y, c, 2 * x + y


def _chip_core(k, c):
    return (k // 2, k % 2, c)


def _rows_per_block(rows, cols, copies):
    best = 16
    for tr in range(16, rows + 1, 16):
        if rows % tr == 0 and copies * tr * cols * 4 <= 6 * 2 ** 20:
            best = tr
    assert rows % best == 0
    return best


def _place_own(w, l, me1):
    _, rows, cols = w.shape
    tr = _rows_per_block(rows // 2, cols, 1)
    per_half = rows // 2 // tr

    def body(me_ref, w_ref, o_ref):
        o_ref[...] = w_ref[...].astype(BF16)

    return _call(
        body, name="place_own",
        num_scalar_prefetch=1, grid=(rows // tr,),
        in_specs=[pl.BlockSpec((None, tr, cols), lambda r, me: (l, r, 0))],
        out_specs=pl.BlockSpec((None, None, tr, cols), lambda r, me: (me[0], r // per_half, r % per_half, 0)),
        out_shape=jax.ShapeDtypeStruct((N_CHIPS, 2, rows // 2, cols), BF16), compiler_params=_params(1))(me1, w)


def _plan_gather_ici(bufs):
    _, _, c, me = _place()
    return [(b.at[me, c], b.at[me, c], b.at[(me + 3 - j) % N_CHIPS, c], _chip_core((me + 1 + j) % N_CHIPS, c))
            for b in bufs for j in range(3)]


def _plan_gather_d2d(bufs):
    x, y, c, me = _place()
    return [(b.at[(me + 3 - j) % N_CHIPS, c], b.at[(me + 3 - j) % N_CHIPS, c], b.at[(me + 3 - j) % N_CHIPS, 1 - c],
             (x, y, 1 - c)) for b in bufs for j in range(3)]


def _plan_grad_sibling(bufs):
    x, y, c, _ = _place()
    n = len(bufs) // 2
    return [(g.at[:, 1 - c], z, z, (x, y, 1 - c)) for g, z in zip(bufs[:n], bufs[n:])]


def _plan_grad_chips(bufs):
    _, _, c, me = _place()
    n = len(bufs) // 2
    return [(p.at[j], z.at[j], z.at[j], _chip_core((me + 1 + j) % N_CHIPS, c))
            for p, z in zip(bufs[:n], bufs[n:]) for j in range(3)]


def _plan_grad_halves(bufs):
    x, y, c, _ = _place()
    return [(b.at[c], b.at[c], b.at[1 - c], (x, y, 1 - c)) for b in bufs]


def _remote(src, dst, send_sem, recv_sem, to):
    return pltpu.make_async_remote_copy(src_ref=src, dst_ref=dst, send_sem=send_sem, recv_sem=recv_sem,
                                        device_id=to, device_id_type=MESH)


def _exchange_start_groups(name, plan, groups):
    sizes = [len(g) for g, _ in groups]
    bufs = [a for g, _ in groups for a in g]
    n, n_groups = len(bufs), len(groups)

    def body(*refs):
        ins, sems, token = refs[:n], refs[n:n + 2 * n_groups], refs[-1]
        at = 0
        for k, size in enumerate(sizes):
            for i, (src, dst, _, to) in enumerate(plan(ins[at:at + size])):
                _remote(src, dst, sems[2 * k].at[i], sems[2 * k + 1].at[i], to).start()
            at += size
        token[...] = jnp.zeros_like(token)

    sem_shapes = [pltpu.SemaphoreType.DMA((n_copies,)) for _, n_copies in groups for _ in range(2)]
    out = _call(
        body, name=name,
        out_shape=(*sem_shapes, *[pltpu.HBM(a.shape, a.dtype) for a in bufs], jax.ShapeDtypeStruct((8, LANES), F32)),
        in_specs=[HBM] * n,
        out_specs=(*[SEM] * (2 * n_groups), *[HBM] * n, pl.BlockSpec(memory_space=pltpu.VMEM)),
        input_output_aliases={t: 2 * n_groups + t for t in range(n)}, hbm_args=n,
        compiler_params=pltpu.CompilerParams(has_side_effects=EFFECT),
    )(*bufs)
    flights, at = [], 2 * n_groups
    for k, size in enumerate(sizes):
        flights.append(((out[2 * k], out[2 * k + 1]), list(out[at:at + size])))
        at += size
    return flights


def _exchange_start(name, plan, bufs, n_copies):
    return _exchange_start_groups(name, plan, [(bufs, n_copies)])[0]


def _exchange_wait(name, plan, bufs, sems):
    n = len(bufs)

    def body(*refs):
        ins = refs[:n]
        ssem, rsem = refs[n], refs[n + 1]
        for i, (src, dst, land, to) in enumerate(plan(ins)):
            _remote(src, dst, ssem.at[i], rsem.at[i], to).wait_send()
            _remote(land, land, ssem.at[i], rsem.at[i], to).wait_recv()

    return list(_call(
        body, name=name, out_shape=[pltpu.HBM(a.shape, a.dtype) for a in bufs],
        in_specs=[HBM] * n + [SEM, SEM], out_specs=[HBM] * n,
        input_output_aliases={t: t for t in range(n)},
        compiler_params=pltpu.CompilerParams(has_side_effects=EFFECT),
    )(*bufs, sems[0], sems[1]))


def _exchange_pass(name, done, plan, bufs, sems, n_copies):
    n = len(bufs)

    def body(*refs):
        ins = refs[:n]
        old_s, old_r, ssem, rsem = refs[n], refs[n + 1], refs[n + 2], refs[n + 3]
        token = refs[-1]
        for i, (src, dst, land, to) in enumerate(done(ins)):
            _remote(src, dst, old_s.at[i], old_r.at[i], to).wait_send()
            _remote(land, land, old_s.at[i], old_r.at[i], to).wait_recv()
        for i, (src, dst, _, to) in enumerate(plan(ins)):
            _remote(src, dst, ssem.at[i], rsem.at[i], to).start()
        token[...] = jnp.zeros_like(token)

    out = _call(
        body, name=name,
        out_shape=(pltpu.SemaphoreType.DMA((n_copies,)), pltpu.SemaphoreType.DMA((n_copies,)),
                   *[pltpu.HBM(a.shape, a.dtype) for a in bufs], jax.ShapeDtypeStruct((8, LANES), F32)),
        in_specs=[HBM] * n + [SEM, SEM], out_specs=(SEM, SEM, *[HBM] * n, pl.BlockSpec(memory_space=pltpu.VMEM)),
        input_output_aliases={t: 2 + t for t in range(n)},
        compiler_params=pltpu.CompilerParams(has_side_effects=EFFECT),
    )(*bufs, sems[0], sems[1])
    return (out[0], out[1]), list(out[2:2 + n])


def _chip_sum(g, xbuf, cm):
    _, _, r2, cols = g.shape
    tr = _rows_per_block(r2, cols, 1)

    def body(cm_ref, g_ref, x_ref, o_ref):
        o_ref[...] = (g_ref[...] + x_ref[...]).astype(BF16)

    return _call(
        body, name="grad_chip_sum",
        num_scalar_prefetch=1, grid=(3, r2 // tr),
        in_specs=[pl.BlockSpec((None, None, tr, cols), lambda j, r, cm: ((cm[1] + 1 + j) % N_CHIPS, cm[0], r, 0)),
                  pl.BlockSpec((None, tr, cols), lambda j, r, cm: ((cm[1] + 1 + j) % N_CHIPS, r, 0))],
        out_specs=pl.BlockSpec((None, tr, cols), lambda j, r, cm: (j, r, 0)),
        out_shape=jax.ShapeDtypeStruct((3, r2, cols), BF16), compiler_params=_params(2))(cm, g, xbuf)


def _total_sum(g, xbuf, rbuf, cm):
    _, _, r2, cols = g.shape
    tr = _rows_per_block(r2, cols, 3)

    def body(cm_ref, g_ref, x_ref, r_ref, o_ref):
        acc = g_ref[...] + x_ref[...]
        for j in range(3):
            acc = acc + r_ref[j].astype(F32)
        o_ref[...] = acc

    return _call(
        body, name="grad_total_sum",
        num_scalar_prefetch=1, grid=(r2 // tr,),
        in_specs=[pl.BlockSpec((None, None, tr, cols), lambda r, cm: (cm[1], cm[0], r, 0)),
                  pl.BlockSpec((None, tr, cols), lambda r, cm: (cm[1], r, 0)),
                  pl.BlockSpec((3, tr, cols), lambda r, cm: (0, r, 0))],
        out_specs=pl.BlockSpec((None, tr, cols), lambda r, cm: (cm[0], r, 0)),
        out_shape=jax.ShapeDtypeStruct((2, r2, cols), F32), compiler_params=_params(1))(cm, g, xbuf, rbuf)


def _small_allreduce(v):
    rows = v.shape[0]
    n_dev = 2 * N_CHIPS

    def body(v_ref, o_ref, buf, ssem, rsem):
        x, y, c, _ = _place()
        me = 4 * x + 2 * y + c
        buf[me] = v_ref[...]

        def copy(d, slot, to):
            return _remote(v_ref, buf.at[slot], ssem.at[d - 1], rsem.at[d - 1], (to // 4, (to // 2) % 2, to % 2))

        cps = [copy(d, me, (me + d) % n_dev) for d in range(1, n_dev)]
        for cp in cps:
            cp.start()
        for d in range(1, n_dev):
            copy(d, (me + n_dev - d) % n_dev, me).wait_recv()
        for cp in cps:
            cp.wait_send()
        acc = buf[0]
        for i in range(1, n_dev):
            acc = acc + buf[i]
        o_ref[...] = acc

    vm = pl.BlockSpec(memory_space=pltpu.VMEM)
    return _call(
        body, name="small_allreduce", in_specs=[vm], out_specs=vm,
        out_shape=jax.ShapeDtypeStruct(v.shape, F32),
        scratch_shapes=[pltpu.VMEM((n_dev, rows, LANES), F32), pltpu.SemaphoreType.DMA((n_dev - 1,)),
                        pltpu.SemaphoreType.DMA((n_dev - 1,))],
        compiler_params=pltpu.CompilerParams(vmem_limit_bytes=V7X_VMEM_LIMIT))(v)


def _adamw_math(w, g, m, v):
    m2 = ADAM_B1 * m + (1.0 - ADAM_B1) * g
    v2 = ADAM_B2 * v + (1.0 - ADAM_B2) * (g * g)
    m_hat = m2 / (1.0 - ADAM_B1 ** ADAM_STEP)
    v_hat = v2 / (1.0 - ADAM_B2 ** ADAM_STEP)
    return -ADAM_LR * (m_hat / (jnp.sqrt(v_hat) + ADAM_EPS) + ADAM_WD * w), m2, v2


def _adamw_layer(w, g, m, v, l, prev):
    _, rows, cols = w.shape
    tr = rows
    for cand in range(8, rows + 1, 8):
        if rows % cand == 0 and cand * cols * 4 <= 2 ** 21:
            tr = cand

    def body(w_ref, g_ref, m_ref, v_ref, *outs):
        go_ref, d_ref, m2_ref, v2_ref = outs[-4:]
        g = g_ref[...]
        go_ref[...] = g
        d_ref[...], m2_ref[...], v2_ref[...] = _adamw_math(w_ref[...], g, m_ref[...], v_ref[...])

    stack = pl.BlockSpec((None, tr, cols), lambda i: (l, i, 0))
    ins, specs, alias = [w, g, m, v], [stack, pl.BlockSpec((tr, cols), lambda i: (i, 0)), stack, stack], {}
    if prev is not None:
        ins += list(prev)
        specs += [ANY] * 4
        alias = {4 + i: i for i in range(4)}
    return _call(
        body, name="adamw", grid=(rows // tr,), in_specs=specs, out_specs=[stack] * 4,
        out_shape=[jax.ShapeDtypeStruct(w.shape, F32)] * 4, input_output_aliases=alias,
        compiler_params=_params(1))(*ins)


def _adamw_small(w, g, m, v):
    def body(w_ref, g_ref, m_ref, v_ref, d_ref, m2_ref, v2_ref):
        d_ref[...], m2_ref[...], v2_ref[...] = _adamw_math(w_ref[...], g_ref[...], m_ref[...], v_ref[...])

    spec = pl.BlockSpec(w.shape, lambda i: (0, 0))
    return _call(
        body, name="adamw_small", grid=(1,), in_specs=[spec] * 4, out_specs=[spec] * 3,
        out_shape=[jax.ShapeDtypeStruct(w.shape, F32)] * 3, compiler_params=_params(1))(w, g, m, v)


SMALL = ("norm_ffn1", "norm_mix", "sinks", "norm_out_sb", "norm_out_swa", "norm_ffn2", "rel_bias", "norm_final")
BIG = ("ffn1_gu", "ffn1_down", "w_in", "w_out", "ffn2_gu", "ffn2_down")


def _pack(parts):
    flat, n = [], 0
    for a in parts:
        a = a.reshape(-1).astype(F32)
        gap = -a.shape[0] % LANES
        flat += [a] + ([jnp.zeros((gap,), F32)] if gap else [])
        n += a.shape[0] + gap
    tail = -(n // LANES) % 8 * LANES
    return jnp.concatenate(flat + ([jnp.zeros((tail,), F32)] if tail else [])).reshape(-1, LANES)


def _unpack(packed, like):
    out, r = [], 0
    for a in like:
        n = math.prod(a.shape)
        nr = -(-n // LANES)
        out.append(packed[r:r + nr].reshape(-1)[:n].reshape(a.shape))
        r += nr
    return out


def _halved(a):
    k, r, cols = a.shape
    return a.reshape(k, 2, r // 2, cols)


def _weight_view(k, buf):
    full = buf.reshape(N_CHIPS, buf.shape[2] * 2, buf.shape[3])
    return full if k.endswith("_gu") else full.reshape(-1, D_MODEL)


def _grad_stack(k, g):
    if not k.endswith("_gu"):
        g = g.reshape(N_CHIPS, g.shape[0] // N_CHIPS, D_MODEL)
    return _halved(g)


def _empty_like_hbm(shape, dtype):
    return pltpu.with_memory_space_constraint(lax.empty(shape, dtype), pltpu.HBM)


def kernel(x, norm_ffn1, w_ffn1_gu, w_ffn1_down, norm_mix, w_in, sinks, norm_out_sb, norm_out_swa, w_out, norm_ffn2, w_ffn2_gu, w_ffn2_down, rel_bias, norm_final, loss_target, m_norm_ffn1, m_w_ffn1_gu, m_w_ffn1_down, m_norm_mix, m_w_in, m_sinks, m_norm_out_sb, m_norm_out_swa, m_w_out, m_norm_ffn2, m_w_ffn2_gu, m_w_ffn2_down, m_rel_bias, m_norm_final, v_norm_ffn1, v_w_ffn1_gu, v_w_ffn1_down, v_norm_mix, v_w_in, v_sinks, v_norm_out_sb, v_norm_out_swa, v_w_out, v_norm_ffn2, v_w_ffn2_gu, v_w_ffn2_down, v_rel_bias, v_norm_final):
    big_w = dict(ffn1_gu=w_ffn1_gu, ffn1_down=w_ffn1_down, w_in=w_in, w_out=w_out, ffn2_gu=w_ffn2_gu, ffn2_down=w_ffn2_down)
    big_m = dict(ffn1_gu=m_w_ffn1_gu, ffn1_down=m_w_ffn1_down, w_in=m_w_in, w_out=m_w_out, ffn2_gu=m_w_ffn2_gu, ffn2_down=m_w_ffn2_down)
    big_v = dict(ffn1_gu=v_w_ffn1_gu, ffn1_down=v_w_ffn1_down, w_in=v_w_in, w_out=v_w_out, ffn2_gu=v_w_ffn2_gu, ffn2_down=v_w_ffn2_down)
    small = dict(norm_ffn1=norm_ffn1, norm_mix=norm_mix, sinks=sinks, norm_out_sb=norm_out_sb, norm_out_swa=norm_out_swa,
                 norm_ffn2=norm_ffn2, rel_bias=rel_bias, norm_final=norm_final)
    small_m = dict(norm_ffn1=m_norm_ffn1, norm_mix=m_norm_mix, sinks=m_sinks, norm_out_sb=m_norm_out_sb,
                   norm_out_swa=m_norm_out_swa, norm_ffn2=m_norm_ffn2, rel_bias=m_rel_bias, norm_final=m_norm_final)
    small_v = dict(norm_ffn1=v_norm_ffn1, norm_mix=v_norm_mix, sinks=v_sinks, norm_out_sb=v_norm_out_sb,
                   norm_out_swa=v_norm_out_swa, norm_ffn2=v_norm_ffn2, rel_bias=v_rel_bias, norm_final=v_norm_final)
    for dct in (big_w, big_m, big_v):
        dct["w_in"] = jnp.swapaxes(dct["w_in"], 1, 2)
    _PREVIOUS[0] = None
    _, _, c, me = _place()
    cm = jnp.stack([c, me]).astype(jnp.int32)
    buckets = jnp.asarray(_bucket_table())
    ffn1, mix_in, rest = ("ffn1_gu", "ffn1_down"), ("w_in",), ("w_out", "ffn2_gu", "ffn2_down")

    def place(l, keys):
        return [_place_own(big_w[k], l, cm[1:]) for k in keys]

    def views(keys, bufs):
        return {k: _weight_view(k, b) for k, b in zip(keys, bufs)}

    def gather_start(tag, bufs):
        return _exchange_start(f"gather{tag}_ici_start", _plan_gather_ici, bufs, 3 * len(bufs))

    def gather_pass(tag, flight):
        return _exchange_pass(f"gather{tag}_pass", _plan_gather_ici, _plan_gather_d2d, flight[1], flight[0],
                              3 * len(flight[1]))

    def gather_done(tag, keys, flight):
        return views(keys, _exchange_wait(f"gather{tag}_d2d_wait", _plan_gather_d2d, flight[1], flight[0]))

    fly_ffn0 = gather_start("0a", place(0, ffn1))
    later = [place(l, keys) for l in range(DEPTH) for keys in ((mix_in, rest) if l == 0 else (ffn1, mix_in, rest))]
    fly_in0, fly_rest0, fly_ffn1, fly_in1, fly_rest1 = _exchange_start_groups(
        "gather_later_ici_start", _plan_gather_ici, [(bufs, 3 * len(bufs)) for bufs in later])
    bias = _bias_table(rel_bias, buckets)
    n1 = _norm_cast(x[0], _row(norm_ffn1[0]))
    w0 = gather_done("0a", ffn1, gather_pass("0a", fly_ffn0))

    s0 = _fwd_ffn1(x[0], n1, w0, small, 0)
    w0.update(gather_done("0b", mix_in, gather_pass("0b", fly_in0)))
    _fwd_proj_sb(s0, w0)
    fly_rest0 = gather_pass("0c", fly_rest0)
    _fwd_swa(s0, small, 0, bias)
    w0.update(gather_done("0c", rest, fly_rest0))
    h, n1 = _fwd_out_ffn2(s0, w0, small, 0, _row(norm_ffn1[1]))
    w1 = gather_done("1a", ffn1, gather_pass("1a", fly_ffn1))
    s1 = _fwd_ffn1(h, n1, w1, small, 1)
    w1.update(gather_done("1b", mix_in, gather_pass("1b", fly_in1)))
    _fwd_proj_sb(s1, w1)
    fly_rest1 = gather_pass("1c", fly_rest1)
    _fwd_swa(s1, small, 1, bias)
    w1.update(gather_done("1c", rest, fly_rest1))
    h, _ = _fwd_out_ffn2(s1, w1, small, 1, _row(norm_final))
    dh32, dh16, dg_final, loss_row = _loss_head(h, _row(norm_final), loss_target[0])
    dh = (dh32, dh16)

    def landing(stacks, lead, dtype):
        return [_empty_like_hbm((lead,) + a.shape[2:], dtype) for a in stacks]

    def reduce_begin(tag, keys, gw):
        stacks = [_grad_stack(k, gw[k]) for k in keys]
        flight = _exchange_start(f"grad{tag}_sibling_start", _plan_grad_sibling,
                                 stacks + landing(stacks, N_CHIPS, F32), len(keys))
        return dict(tag=tag, keys=keys, stacks=stacks, flight=flight)

    def reduce_chips(st):
        n, (sems, bufs) = len(st["keys"]), st["flight"]
        bufs = _exchange_wait(f"grad{st['tag']}_sibling_wait", _plan_grad_sibling, bufs, sems)
        st["own"] = list(zip(bufs[:n], bufs[n:]))
        st["flight"] = _exchange_start(f"grad{st['tag']}_chips_start", _plan_grad_chips,
                                       [_chip_sum(g, z, cm) for g, z in st["own"]] + landing(st["stacks"], 3, BF16),
                                       3 * n)

    def reduce_halves(st):
        n, (sems, bufs) = len(st["keys"]), st["flight"]
        bufs = _exchange_wait(f"grad{st['tag']}_chips_wait", _plan_grad_chips, bufs, sems)
        halves = [_total_sum(g, x, z, cm) for (g, x), z in zip(st["own"], bufs[n:])]
        st["flight"] = _exchange_start(f"grad{st['tag']}_halves_start", _plan_grad_halves, halves, n)

    def reduce_end(st):
        sems, bufs = st["flight"]
        bufs = _exchange_wait(f"grad{st['tag']}_halves_wait", _plan_grad_halves, bufs, sems)
        return {k: b.reshape(big_w[k].shape[1:]) for k, b in zip(st["keys"], bufs)}

    def adamw(reduced, l, prev):
        return {k: _adamw_layer(big_w[k], g, big_m[k], big_v[k], l, None if prev is None else prev[k])
                for k, g in reduced.items()}

    gsm = [dict() for _ in range(DEPTH)]
    dbias = jnp.zeros((8, BLK, 2 * BLK), F32)
    dh, gw1, gs = _bwd_ffn(dh, s1, w1, small, 1, 2)
    gsm[1].update(gs)
    dh, gw, gs, dbias = _bwd_mix(dh, s1, w1, small, 1, bias, dbias)
    gw1.update(gw)
    gsm[1].update(gs)
    dh, gw, gs = _bwd_ffn(dh, s1, w1, small, 1, 1)
    gw1.update(gw)
    gsm[1].update(gs)

    red1 = reduce_begin("1", BIG, gw1)
    dh, gw0, gs = _bwd_ffn(dh, s0, w0, small, 0, 2)
    gsm[0].update(gs)
    reduce_chips(red1)
    dh, gw, gs, dbias = _bwd_mix(dh, s0, w0, small, 0, bias, dbias)
    gw0.update(gw)
    gsm[0].update(gs)
    red0a = reduce_begin("0a", ("ffn2_gu", "ffn2_down", "w_out", "w_in"), gw0)
    reduce_halves(red1)
    dgu = _bwd_ffn_dact(dh, s0, w0, 1)
    reduce_chips(red0a)
    dh, gw, gs = _bwd_ffn_rest(dh, dgu, s0, w0, small, 0, 1)
    gsm[0].update(gs)
    red0b = reduce_begin("0b", ffn1, gw)
    reduced1 = reduce_end(red1)
    stacks = adamw({k: reduced1[k] for k in ffn1}, 1, None)

    gsmall = {k: jnp.stack([gsm[l][k].reshape(-1) for l in range(DEPTH)]) for k in gsm[0]}
    gsmall["rel_bias"] = jnp.transpose(_bias_grad(dbias, buckets)[:, :N_BUCKETS])
    gsmall["norm_final"] = dg_final.reshape(-1)
    small_like = [small[k] for k in SMALL]
    pk = lambda dct: _pack([dct[k] for k in SMALL])
    red = _small_allreduce(_pack([gsmall[k] for k in SMALL] + [loss_row[0, :1]]))
    gs = _unpack(red, small_like + [loss_row[0, :1]])
    loss = gs[-1][0]
    gs = dict(zip(SMALL, gs[:-1]))

    ffn2 = ("ffn2_gu", "ffn2_down")
    reduce_chips(red0b)
    stacks.update(adamw({k: reduced1[k] for k in ("w_in", "w_out")}, 1, None))
    reduce_halves(red0a)
    stacks.update(adamw({k: reduced1[k] for k in ffn2}, 1, None))
    dlt, m2, v2 = _adamw_small(pk(small), pk(gs), pk(small_m), pk(small_v))
    reduced0a = reduce_end(red0a)
    stacks.update(adamw({k: reduced0a[k] for k in ffn2}, 0, stacks))
    reduce_halves(red0b)
    stacks.update(adamw({k: reduced0a[k] for k in ("w_in", "w_out")}, 0, stacks))
    stacks.update(adamw(reduce_end(red0b), 0, stacks))

    out_g, out_d, out_m, out_v = {}, {}, {}, {}
    for k in BIG:
        out_g[k], out_d[k], out_m[k], out_v[k] = [jnp.swapaxes(a, 1, 2) if k == "w_in" else a for a in stacks[k]]
    for dst, packed in ((out_d, dlt), (out_m, m2), (out_v, v2)):
        dst.update(zip(SMALL, _unpack(packed, small_like)))
    out_g.update(gs)

    order = ("norm_ffn1", "ffn1_gu", "ffn1_down", "norm_mix", "w_in", "sinks", "norm_out_sb", "norm_out_swa", "w_out",
             "norm_ffn2", "ffn2_gu", "ffn2_down", "rel_bias", "norm_final")
    return (loss, dh[0].reshape(x.shape), *[out_g[k] for k in order], *[out_d[k] for k in order],
            *[out_m[k] for k in order], *[out_v[k] for k in order])
```

```python
import math

import numpy as np
import jax
import jax.numpy as jnp
from jax import lax
from jax.experimental import pallas as pl
from jax.experimental.pallas import tpu as pltpu

F32 = jnp.float32
BF16 = jnp.bfloat16

D_MODEL = 1024
DEPTH = 2
HEAD_DIM = 64
BLK = 128
N_BUCKETS = 32
MAX_DISTANCE = 128
D_FF = 2816
EPS = 1e-6
NEG_INF = -1e30
SB_W = 512
SWA_W = 512
KV_W = 128
IN_W = 2304
SCALE = HEAD_DIM ** -0.5
N_CHIPS = 4
FS = 2 * D_FF // N_CHIPS
LANES = 128
V7X_VMEM_LIMIT = 56 * 2 ** 20
TM = 512
SLAB_BLOCK_BYTES = 6 * 2 ** 20
ADAMW_BLOCK_BYTES = 2 ** 21
SB_KT = 512
SWA_G = 4

ADAM_LR = 0.001
ADAM_B1 = 0.9
ADAM_B2 = 0.999
ADAM_EPS = 1e-08
ADAM_WD = 0.01
ADAM_STEP = 10

MESH = pl.DeviceIdType.MESH
ANY = pl.BlockSpec(memory_space=pl.ANY)
HBM = pl.BlockSpec(memory_space=pltpu.HBM)
SEM = pl.BlockSpec(memory_space=pltpu.SEMAPHORE)
EFFECT = pltpu.SideEffectType.DATAFLOW_SIDE_EFFECTING


def _params(n_grid):
    return pltpu.CompilerParams(dimension_semantics=("arbitrary",) * n_grid, vmem_limit_bytes=V7X_VMEM_LIMIT)


_PREVIOUS = [None]


def _call(body, *, name, in_specs, out_specs, out_shape, grid=(), num_scalar_prefetch=0, scratch_shapes=(),
          input_output_aliases=None, compiler_params=None, hbm_args=0):
    n_in = len(in_specs)

    def run(*args):
        dep = _PREVIOUS[0]
        if any(dep is a for a in args):
            dep = None
        args = [pltpu.with_memory_space_constraint(a, pltpu.HBM) if i < hbm_args else a for i, a in enumerate(args)]
        specs = list(in_specs) + ([ANY] if dep is not None else [])
        k = num_scalar_prefetch + n_in
        fn = body if dep is None else (lambda *refs: body(*refs[:k], *refs[k + 1:]))
        if num_scalar_prefetch:
            shape = dict(grid_spec=pltpu.PrefetchScalarGridSpec(
                num_scalar_prefetch=num_scalar_prefetch, grid=grid, in_specs=specs, out_specs=out_specs,
                scratch_shapes=scratch_shapes))
        else:
            shape = dict(grid=grid, in_specs=specs, out_specs=out_specs, scratch_shapes=scratch_shapes)
        out = pl.pallas_call(fn, name=name, out_shape=out_shape, input_output_aliases=input_output_aliases or {},
                             compiler_params=compiler_params, **shape)(*args, *([] if dep is None else [dep]))
        _PREVIOUS[0] = jax.tree.leaves(out)[-1]
        return out

    return run


def _dot(a, b):
    return jnp.dot(a, b, preferred_element_type=F32)


def _dot_nt(a, b):
    return lax.dot_general(a, b, (((1,), (1,)), ((), ())), preferred_element_type=F32)


def _dot_tn(a, b):
    return lax.dot_general(a, b, (((0,), (0,)), ((), ())), preferred_element_type=F32)


def _rms_fwd(x, g):
    r = lax.rsqrt(jnp.mean(x * x, axis=-1, keepdims=True) + EPS)
    xh = x * r
    return xh * g, xh, r


def _rms_bwd(dy, xh, r, g):
    u = dy * g
    dx = r * (u - xh * jnp.mean(u * xh, axis=-1, keepdims=True))
    dg = jnp.sum(dy * xh, axis=0, keepdims=True)
    return dx, dg


def _softplus(z):
    neg_abs = lax.bitcast_convert_type(lax.bitcast_convert_type(z, jnp.int32) | jnp.int32(-2 ** 31), F32)
    sp = jnp.maximum(z, 0.0) + jnp.log(1.0 + jnp.exp(neg_abs))
    return sp, z - sp


def _norm_cast(h, g):
    t, w = h.shape

    def body(h_ref, g_ref, n_ref):
        y, _, _ = _rms_fwd(h_ref[...], g_ref[...])
        n_ref[...] = y.astype(BF16)

    return _call(
        body, name="norm_cast", grid=(t // TM,),
        in_specs=[pl.BlockSpec((TM, w), lambda i: (i, 0)), pl.BlockSpec((1, w), lambda i: (0, 0))],
        out_specs=pl.BlockSpec((TM, w), lambda i: (i, 0)),
        out_shape=jax.ShapeDtypeStruct((t, w), BF16), compiler_params=_params(1))(h, g)


def _ffn_gu(n, wgu):
    t, d = n.shape

    def body(n_ref, wg_ref, wu_ref, gu_ref, act_ref):
        x = n_ref[...]
        g = _dot(x, wg_ref[...])
        u = _dot(x, wu_ref[...])
        sig = jax.nn.sigmoid(g)
        silu = g * sig
        gu_ref[0] = (u * (sig + silu * (1.0 - sig))).astype(BF16)
        gu_ref[1] = silu.astype(BF16)
        act_ref[...] = (silu * u).astype(BF16)

    return _call(
        body, name="ffn_gu", grid=(2, t // TM),
        in_specs=[pl.BlockSpec((TM, d), lambda j, i: (i, 0)),
                  pl.BlockSpec((None, d, FS), lambda j, i: (j, 0, 0)),
                  pl.BlockSpec((None, d, FS), lambda j, i: (j + 2, 0, 0))],
        out_specs=[pl.BlockSpec((2, TM, FS), lambda j, i: (0, i, j)), pl.BlockSpec((TM, FS), lambda j, i: (i, j))],
        out_shape=[jax.ShapeDtypeStruct((2, t, D_FF), BF16), jax.ShapeDtypeStruct((t, D_FF), BF16)],
        compiler_params=_params(2))(n, wgu, wgu)


def _down_res(act, wdn, h, g_next):
    t, f = act.shape
    d = h.shape[1]

    def body(a_ref, w_ref, h_ref, g_ref, o_ref, n_ref):
        out = h_ref[...] + 0.5 * _dot(a_ref[...], w_ref[...])
        o_ref[...] = out
        n_ref[...] = _rms_fwd(out, g_ref[...])[0].astype(BF16)

    row = pl.BlockSpec((TM, d), lambda i: (i, 0))
    return _call(
        body, name="down_res", grid=(t // TM,),
        in_specs=[pl.BlockSpec((TM, f), lambda i: (i, 0)), pl.BlockSpec((f, d), lambda i: (0, 0)), row,
                  pl.BlockSpec((1, d), lambda i: (0, 0))],
        out_specs=[row, row],
        out_shape=[jax.ShapeDtypeStruct((t, d), F32), jax.ShapeDtypeStruct((t, d), BF16)],
        compiler_params=_params(1))(act, wdn, h, g_next)


def _proj(n, w_in_t):
    t, d = n.shape
    w = w_in_t.shape[0]

    def body(n_ref, w_ref, o_ref):
        o_ref[...] = _dot_nt(n_ref[...], w_ref[...]).astype(BF16)

    return _call(
        body, name="proj", grid=(t // TM,),
        in_specs=[pl.BlockSpec((TM, d), lambda i: (i, 0)), pl.BlockSpec((w, d), lambda i: (0, 0))],
        out_specs=pl.BlockSpec((TM, w), lambda i: (i, 0)),
        out_shape=jax.ShapeDtypeStruct((t, w), BF16), compiler_params=_params(1))(n, w_in_t)


def _out_res(o_sb, o_sw, g_sb, g_sw, w_out, h, g_next):
    t, d = h.shape

    def body(a_ref, b_ref, ga_ref, gb_ref, w_ref, h_ref, g_ref, o_ref, mix_ref, n_ref):
        ya, _, _ = _rms_fwd(a_ref[...], ga_ref[...])
        yb, _, _ = _rms_fwd(b_ref[...], gb_ref[...])
        mixed = jnp.concatenate([ya.astype(BF16), yb.astype(BF16)], axis=1)
        mix_ref[...] = mixed
        out = h_ref[...] + _dot(mixed, w_ref[...])
        o_ref[...] = out
        n_ref[...] = _rms_fwd(out, g_ref[...])[0].astype(BF16)

    row = pl.BlockSpec((TM, d), lambda i: (i, 0))
    return _call(
        body, name="out_res", grid=(t // TM,),
        in_specs=[pl.BlockSpec((TM, SB_W), lambda i: (i, 0)), pl.BlockSpec((TM, SWA_W), lambda i: (i, 0)),
                  pl.BlockSpec((1, SB_W), lambda i: (0, 0)), pl.BlockSpec((1, SWA_W), lambda i: (0, 0)),
                  pl.BlockSpec((d, d), lambda i: (0, 0)), row, pl.BlockSpec((1, d), lambda i: (0, 0))],
        out_specs=[row, row, row],
        out_shape=[jax.ShapeDtypeStruct((t, d), F32), jax.ShapeDtypeStruct((t, d), BF16),
                   jax.ShapeDtypeStruct((t, d), BF16)],
        compiler_params=_params(1))(o_sb, o_sw, g_sb, g_sw, w_out, h, g_next)


def _loss_head(h, g, tgt):
    t, d = h.shape

    def body(h_ref, g_ref, t_ref, dh_ref, dhb_ref, dg_ref, loss_ref):
        @pl.when(pl.program_id(0) == 0)
        def _():
            dg_ref[...] = jnp.zeros_like(dg_ref)
            loss_ref[...] = jnp.zeros_like(loss_ref)

        gg = g_ref[...]
        y, xh, r = _rms_fwd(h_ref[...], gg)
        err = y - t_ref[...]
        part = 0.5 * jnp.sum(jnp.sum(err * err, axis=1, keepdims=True) / d, axis=0, keepdims=True)
        loss_ref[...] += jnp.broadcast_to(part, loss_ref.shape)
        dx, dg = _rms_bwd(err / d, xh, r, gg)
        dh_ref[...] = dx
        dhb_ref[...] = dx.astype(BF16)
        dg_ref[...] += dg

    row = pl.BlockSpec((TM, d), lambda i: (i, 0))
    return _call(
        body, name="loss_head", grid=(t // TM,),
        in_specs=[row, pl.BlockSpec((1, d), lambda i: (0, 0)), row],
        out_specs=[row, row, pl.BlockSpec((1, d), lambda i: (0, 0)), pl.BlockSpec((1, LANES), lambda i: (0, 0))],
        out_shape=[jax.ShapeDtypeStruct((t, d), F32), jax.ShapeDtypeStruct((t, d), BF16),
                   jax.ShapeDtypeStruct((1, d), F32), jax.ShapeDtypeStruct((1, LANES), F32)],
        compiler_params=_params(1))(h, g, tgt)


def _ffn_dact(dh, wdn, gu):
    t, d = dh.shape
    tm = TM

    def body(dh_ref, w_ref, gu_ref, o_ref):
        da = 0.5 * _dot_nt(dh_ref[...].astype(BF16), w_ref[...])
        o_ref[0] = (da * gu_ref[0].astype(F32)).astype(BF16)
        o_ref[1] = (da * gu_ref[1].astype(F32)).astype(BF16)

    return _call(
        body, name="ffn_dact", grid=(2, t // tm),
        in_specs=[pl.BlockSpec((tm, d), lambda j, i: (i, 0)), pl.BlockSpec((FS, d), lambda j, i: (j, 0)),
                  pl.BlockSpec((2, tm, FS), lambda j, i: (0, i, j))],
        out_specs=pl.BlockSpec((2, tm, FS), lambda j, i: (0, i, j)),
        out_shape=jax.ShapeDtypeStruct((2, t, D_FF), BF16), compiler_params=_params(2))(dh, wdn, gu)


def _dn_norm_bwd(a, a_spec, w, w_spec, nk, dh, h_in, g, w_transposed=False, tm=TM):
    t, d = dh.shape
    mm = _dot if w_transposed else _dot_nt

    def body(a_ref, w_ref, dh_ref, h_ref, g_ref, o_ref, ob_ref, dg_ref, acc_ref):
        i, k = pl.program_id(0), pl.program_id(1)

        if nk > 1:
            @pl.when(k == 0)
            def _():
                acc_ref[...] = mm(a_ref[...], w_ref[...])

            @pl.when((k > 0) & (k < nk - 1))
            def _():
                acc_ref[...] += mm(a_ref[...], w_ref[...])

        @pl.when(k == nk - 1)
        def _():
            gg = g_ref[...]
            dg = jnp.zeros_like(gg)
            for rows in (slice(r, r + TM // 2) for r in range(0, tm, TM // 2)):
                dn = mm(a_ref[rows, :], w_ref[...])
                if nk > 1:
                    dn = dn + acc_ref[rows, :]
                _, xh, r = _rms_fwd(h_ref[rows, :], gg)
                dx, dg_rows = _rms_bwd(dn, xh, r, gg)
                out = dh_ref[rows, :] + dx
                o_ref[rows, :] = out
                ob_ref[rows, :] = out.astype(BF16)
                dg = dg + dg_rows

            @pl.when(i == 0)
            def _():
                dg_ref[...] = dg

            @pl.when(i > 0)
            def _():
                dg_ref[...] += dg

    row = pl.BlockSpec((tm, d), lambda i, k: (i, 0))
    return _call(
        body, name="dn_norm_bwd", grid=(t // tm, nk),
        in_specs=[a_spec, w_spec, row, row, pl.BlockSpec((1, d), lambda i, k: (0, 0))],
        out_specs=[row, row, pl.BlockSpec((1, d), lambda i, k: (0, 0))],
        out_shape=[jax.ShapeDtypeStruct((t, d), F32), jax.ShapeDtypeStruct((t, d), BF16),
                   jax.ShapeDtypeStruct((1, d), F32)],
        scratch_shapes=[pltpu.VMEM((tm, d), F32)], compiler_params=_params(2))(a, w, dh, h_in, g)


def _ffn_dn(dgu, wgu, dh, h_in, g):
    d = dh.shape[1]
    tm = 2 * TM
    return _dn_norm_bwd(
        dgu, pl.BlockSpec((None, tm, FS), lambda i, k: (k // 2, i, k % 2)),
        wgu, pl.BlockSpec((None, d, FS), lambda i, k: (k, 0, 0)), N_CHIPS, dh, h_in, g, tm=tm)


def _mix_dn(dproj, w_in_t, dh, h_in, g):
    d = dh.shape[1]
    w = dproj.shape[1]
    return _dn_norm_bwd(
        dproj, pl.BlockSpec((TM, w), lambda i, k: (i, 0)),
        w_in_t, pl.BlockSpec((w, d), lambda i, k: (0, 0)), 1, dh, h_in, g, w_transposed=True)


def _dmixed(dh, w_out, o_sb, o_sw, g_sb, g_sw):
    t, d = dh.shape

    def body(dh_ref, w_ref, a_ref, b_ref, ga_ref, gb_ref, o_ref, dga_ref, dgb_ref):
        i = pl.program_id(0)
        dm = _dot_nt(dh_ref[...].astype(BF16), w_ref[...])
        _, xa, ra = _rms_fwd(a_ref[...], ga_ref[...])
        _, xb, rb = _rms_fwd(b_ref[...], gb_ref[...])
        da, dga = _rms_bwd(dm[:, :SB_W], xa, ra, ga_ref[...])
        db, dgb = _rms_bwd(dm[:, SB_W:], xb, rb, gb_ref[...])
        o_ref[...] = jnp.concatenate([da.astype(BF16), db.astype(BF16)], axis=1)

        @pl.when(i == 0)
        def _():
            dga_ref[...] = dga
            dgb_ref[...] = dgb

        @pl.when(i > 0)
        def _():
            dga_ref[...] += dga
            dgb_ref[...] += dgb

    return _call(
        body, name="dmixed", grid=(t // TM,),
        in_specs=[pl.BlockSpec((TM, d), lambda i: (i, 0)), pl.BlockSpec((d, d), lambda i: (0, 0)),
                  pl.BlockSpec((TM, SB_W), lambda i: (i, 0)), pl.BlockSpec((TM, SWA_W), lambda i: (i, 0)),
                  pl.BlockSpec((1, SB_W), lambda i: (0, 0)), pl.BlockSpec((1, SWA_W), lambda i: (0, 0))],
        out_specs=[pl.BlockSpec((TM, d), lambda i: (i, 0)), pl.BlockSpec((1, SB_W), lambda i: (0, 0)),
                   pl.BlockSpec((1, SWA_W), lambda i: (0, 0))],
        out_shape=[jax.ShapeDtypeStruct((t, d), BF16), jax.ShapeDtypeStruct((1, SB_W), F32),
                   jax.ShapeDtypeStruct((1, SWA_W), F32)],
        compiler_params=_params(1))(dh, w_out, o_sb, o_sw, g_sb, g_sw)


def _wgrad(name, a, a_spec, b, b_spec, grid, out_shape, out_spec, scale):
    def body(a_ref, b_ref, o_ref):
        r = _dot_tn(a_ref[...], b_ref[...].astype(BF16))
        o_ref[...] = r if scale == 1.0 else scale * r

    return _call(
        body, name=name, grid=grid, in_specs=[a_spec, b_spec], out_specs=out_spec,
        out_shape=jax.ShapeDtypeStruct(out_shape, F32), compiler_params=_params(len(grid)))(a, b)


def _wgrad_gu(n, dgu):
    t, d = n.shape
    return _wgrad(
        "wgrad_gu", n, pl.BlockSpec((t, TM), lambda s, r: (0, r)),
        dgu, pl.BlockSpec((None, t, FS), lambda s, r: (s // 2, 0, s % 2)), (N_CHIPS, d // TM),
        (N_CHIPS, d, FS), pl.BlockSpec((None, TM, FS), lambda s, r: (s, r, 0)), 1.0)


def _wgrad_down(act, dh):
    t, d = dh.shape
    return _wgrad(
        "wgrad_down", act, pl.BlockSpec((t, FS), lambda s: (0, s)), dh, pl.BlockSpec((t, d), lambda s: (0, 0)),
        (2,), (D_FF, d), pl.BlockSpec((FS, d), lambda s: (s, 0)), 0.5)


def _wgrad_out(mixed, dh):
    t, d = dh.shape
    return _wgrad(
        "wgrad_out", mixed, pl.BlockSpec((t, TM), lambda s: (0, s)), dh, pl.BlockSpec((t, d), lambda s: (0, 0)),
        (d // TM,), (d, d), pl.BlockSpec((TM, d), lambda s: (s, 0)), 1.0)


def _wgrad_in(n, dproj):
    t, d = n.shape
    w = dproj.shape[1]
    tw = w // 3
    return _wgrad(
        "wgrad_in", dproj, pl.BlockSpec((t, tw), lambda s: (0, s)), n, pl.BlockSpec((t, d), lambda s: (0, 0)),
        (3,), (w, d), pl.BlockSpec((tw, d), lambda s: (s, 0)), 1.0)


def _tri(rel):
    row = lax.broadcasted_iota(jnp.int32, (BLK, BLK), 0)
    col = lax.broadcasted_iota(jnp.int32, (BLK, BLK), 1)
    m = rel(row, col).astype(BF16)
    return jnp.concatenate([m, m], axis=0)


def _scan_dot(x, tri2):
    hi = x.astype(BF16)
    lo = (x - hi.astype(F32)).astype(BF16)
    return _dot(jnp.concatenate([hi, lo], axis=1), tri2)


def _head_masks():
    lane = lax.broadcasted_iota(jnp.int32, (1, LANES), 1)
    return [lane < HEAD_DIM, lane >= HEAD_DIM]


SB_PAIRS = 2
SB_ROWS = 2 * SB_PAIRS * BLK


def _sb_causal():
    row = lax.broadcasted_iota(jnp.int32, (SB_ROWS, BLK), 0) & (BLK - 1)
    return lax.broadcasted_iota(jnp.int32, (SB_ROWS, BLK), 1) < row


def _sb_mask_last(x, causal):
    own = jnp.where(causal, x[:, -BLK:], 0.0)
    return own if x.shape[1] == BLK else jnp.concatenate([x[:, :-BLK], own], axis=1)


def _sb_stack(x, hm):
    return jnp.concatenate([jnp.where(m, x[:, p * LANES:(p + 1) * LANES], jnp.zeros((BLK, LANES), x.dtype))
                            for p in range(SB_PAIRS) for m in hm], axis=0)


def _sb_unstack(y, hm):
    return jnp.concatenate([jnp.where(hm[0], y[2 * p * BLK:(2 * p + 1) * BLK], y[(2 * p + 1) * BLK:(2 * p + 2) * BLK])
                            for p in range(SB_PAIRS)], axis=1)


def _sb_pairs():
    return [(slice(2 * p * BLK, (2 * p + 2) * BLK), slice(p * LANES, (p + 1) * LANES)) for p in range(SB_PAIRS)]


def _sb_fwd(proj):
    t = proj.shape[0]
    nb = SB_KT // BLK
    wide = SB_PAIRS * LANES

    def body(q_ref, k_ref, v_ref, o_ref, tot_ref):
        hm = _head_masks()
        causal = _sb_causal()
        pairs = _sb_pairs()
        after = _tri(lambda r, c: r > c)

        def tile(qh, start, n_blk, carry, acc, own):
            ks = pl.ds(pl.multiple_of(start, BLK), n_blk * BLK)
            z = jnp.concatenate([_dot_nt(qh[rows], k_ref[ks, lanes]) for rows, lanes in pairs], axis=0)
            sp, zs = _softplus(z)
            spm = _sb_mask_last(sp, causal) if own else sp
            sufs = [None] * n_blk
            for b in reversed(range(n_blk)):
                blk = spm[:, b * BLK:(b + 1) * BLK]
                sufs[b] = carry + _scan_dot(blk, after)
                carry = carry + jnp.sum(blk, axis=1, keepdims=True)
            w = jnp.exp(zs - jnp.concatenate(sufs, axis=1))
            wb = (_sb_mask_last(w, causal) if own else w).astype(BF16)
            return carry, acc + jnp.concatenate([_dot(wb[rows], v_ref[ks, lanes]) for rows, lanes in pairs], axis=0)

        def qblock(g, j):
            qs = pl.ds(pl.multiple_of(g * SB_KT + j * BLK, BLK), BLK)
            qh = _sb_stack(q_ref[qs, :] * SCALE, hm)
            c0 = tile(qh, g * SB_KT, j + 1, jnp.zeros((SB_ROWS, 1), F32), jnp.zeros((SB_ROWS, LANES), F32), True)
            carry, acc = lax.fori_loop(0, g, lambda n, c: tile(qh, (g - 1 - n) * SB_KT, nb, c[0], c[1], False), c0)
            o_ref[qs, :] = _sb_unstack(acc, hm)
            for h in range(2 * SB_PAIRS):
                tot_ref[h, qs, :] = carry[h * BLK:(h + 1) * BLK]

        def group(g, _):
            for j in range(nb):
                qblock(g, j)
            return 0

        lax.fori_loop(0, t // SB_KT, group, 0)

    col_blk = lambda off: pl.BlockSpec((t, wide), lambda g: (0, off + g))
    n_steps = SB_W // wide
    return _call(
        body, name="sb_fwd", grid=(n_steps,), in_specs=[col_blk(0), col_blk(n_steps), col_blk(2 * n_steps)],
        out_specs=[col_blk(0), pl.BlockSpec((2 * SB_PAIRS, t, 1), lambda g: (g, 0, 0))],
        out_shape=[jax.ShapeDtypeStruct((t, SB_W), F32), jax.ShapeDtypeStruct((8, t, 1), F32)],
        compiler_params=_params(1))(proj, proj, proj)


def _sb_bwd(proj, d_o, tot):
    t = proj.shape[0]
    nb = SB_KT // BLK
    wide = SB_PAIRS * LANES

    def body(q_ref, k_ref, v_ref, do_ref, tot_ref, dq_ref, dk_ref, dv_ref, dk_acc, dv_acc):
        hm = _head_masks()
        causal = _sb_causal()
        pairs = _sb_pairs()
        before = _tri(lambda r, c: r < c)
        upto = _tri(lambda r, c: r <= c)
        dk_acc[...] = jnp.zeros_like(dk_acc)
        dv_acc[...] = jnp.zeros_like(dv_acc)

        def tile(qh, doh, tt, start, n_blk, pre, ecum, dq, own):
            ks = pl.ds(pl.multiple_of(start, BLK), n_blk * BLK)
            k = k_ref[ks, :]
            v = v_ref[ks, :]
            z = jnp.concatenate([_dot_nt(qh[rows], k[:, lanes]) for rows, lanes in pairs], axis=0)
            sp, zs = _softplus(z)
            spm = _sb_mask_last(sp, causal) if own else sp
            pres = []
            for b in range(n_blk):
                blk = spm[:, b * BLK:(b + 1) * BLK]
                pres.append(pre + _scan_dot(blk, before))
                pre = pre + jnp.sum(blk, axis=1, keepdims=True)
            logw = z - (tt - jnp.concatenate(pres, axis=1))
            if own:
                logw = jnp.minimum(logw, 0.0)
            w = jnp.exp(logw)
            if own:
                w = _sb_mask_last(w, causal)
            e = w * jnp.concatenate([_dot_nt(doh[rows], v[:, lanes]) for rows, lanes in pairs], axis=0)
            incs = []
            for b in range(n_blk):
                blk = e[:, b * BLK:(b + 1) * BLK]
                incs.append(ecum + _scan_dot(blk, upto))
                ecum = ecum + jnp.sum(blk, axis=1, keepdims=True)
            dz = e - jnp.exp(zs) * jnp.concatenate(incs, axis=1)
            if own:
                dz = _sb_mask_last(dz, causal)
            dzb = dz.astype(BF16)
            wb = w.astype(BF16)
            for rows, lanes in pairs:
                dk_acc[ks, lanes] += _dot_tn(dzb[rows], qh[rows])
                dv_acc[ks, lanes] += _dot_tn(wb[rows], doh[rows])
            return pre, ecum, dq + jnp.concatenate([_dot(dzb[rows], k[:, lanes]) for rows, lanes in pairs], axis=0)

        def qblock(g, j):
            qs = pl.ds(pl.multiple_of(g * SB_KT + j * BLK, BLK), BLK)
            qh = _sb_stack(q_ref[qs, :] * SCALE, hm)
            doh = _sb_stack(do_ref[qs, :], hm)
            tt = jnp.concatenate([tot_ref[h, qs, :] for h in range(2 * SB_PAIRS)], axis=0)
            c0 = (jnp.zeros((SB_ROWS, 1), F32), jnp.zeros((SB_ROWS, 1), F32), jnp.zeros((SB_ROWS, LANES), F32))
            c = lax.fori_loop(0, g, lambda kt, c: tile(qh, doh, tt, kt * SB_KT, nb, c[0], c[1], c[2], False), c0)
            dq = tile(qh, doh, tt, g * SB_KT, j + 1, c[0], c[1], c[2], True)[2]
            dq_ref[qs, :] = (_sb_unstack(dq, hm) * SCALE).astype(BF16)

        def group(g, _):
            for j in range(nb):
                qblock(g, j)
            return 0

        lax.fori_loop(0, t // SB_KT, group, 0)
        dk_ref[...] = dk_acc[...].astype(BF16)
        dv_ref[...] = dv_acc[...].astype(BF16)

    col_blk = lambda off: pl.BlockSpec((t, wide), lambda g: (0, off + g))
    n_steps = SB_W // wide
    out = jax.ShapeDtypeStruct((t, SB_W), BF16)
    return _call(
        body, name="sb_bwd", grid=(n_steps,),
        in_specs=[col_blk(0), col_blk(n_steps), col_blk(2 * n_steps), col_blk(0),
                  pl.BlockSpec((2 * SB_PAIRS, t, 1), lambda g: (g, 0, 0))],
        out_specs=[col_blk(0), col_blk(0), col_blk(0)], out_shape=[out, out, out],
        scratch_shapes=[pltpu.VMEM((t, wide), F32), pltpu.VMEM((t, wide), F32)],
        compiler_params=_params(1))(proj, proj, proj, d_o, tot)


def _bucket_table():
    a = np.arange(BLK)[:, None]
    c = np.arange(2 * BLK)[None, :]
    dist = np.maximum(BLK + a - c, 0)
    max_exact = N_BUCKETS // 2
    dd = np.maximum(dist, 1).astype(np.float32)
    large = max_exact + (np.log(dd / max_exact) / math.log(MAX_DISTANCE / max_exact)
                         * (N_BUCKETS - max_exact)).astype(np.int32)
    large = np.minimum(large, N_BUCKETS - 1)
    return np.where(dist < max_exact, dist, large).astype(np.int32)


SWA_H = 8


def _swa_band_masks():
    row = lax.broadcasted_iota(jnp.int32, (SWA_H * BLK, 2 * BLK), 0) & (BLK - 1)
    col = lax.broadcasted_iota(jnp.int32, (SWA_H * BLK, 2 * BLK), 1)
    own = lax.broadcasted_iota(jnp.int32, (SWA_H * BLK, BLK), 1) <= (
        lax.broadcasted_iota(jnp.int32, (SWA_H * BLK, BLK), 0) & (BLK - 1))
    return (col > row) & ((col < BLK) | (col - BLK <= row)), own


def _swa_stack(ref, qs, hm, scale):
    parts = []
    for hq in range(SWA_H):
        kvh = hq // SWA_G
        x = ref[qs, (hq // 2) * LANES:(hq // 2 + 1) * LANES].astype(F32)
        if hq % 2 != kvh:
            x = pltpu.roll(x, HEAD_DIM, 1)
        parts.append(jnp.where(hm[kvh], x * scale, 0.0).astype(BF16))
    return jnp.concatenate(parts, axis=0)


def _swa_unstack(x8, hm):
    heads = []
    for hq in range(SWA_H):
        x = x8[hq * BLK:(hq + 1) * BLK]
        heads.append(pltpu.roll(x, HEAD_DIM, 1) if hq % 2 != hq // SWA_G else x)
    return [jnp.where(hm[0], heads[2 * p], heads[2 * p + 1]) for p in range(SWA_H // 2)]


def _swa_scores(q8, kb, bias_ref, mask, cols):
    bias8 = jnp.concatenate([bias_ref[hq, :, cols] for hq in range(SWA_H)], axis=0)
    return jnp.where(mask, _dot_nt(q8, kb) + bias8, NEG_INF)


def _swa_sinks(sink_ref):
    return jnp.concatenate([jnp.broadcast_to(sink_ref[hq:hq + 1, 0:1], (BLK, 1)) for hq in range(SWA_H)], axis=0)


def _swa_fwd(proj, bias, sinks_b):
    t = proj.shape[0]
    nq = t // BLK

    def body(q_ref, k_ref, v_ref, bias_ref, sink_ref, o_ref, lse_ref):
        hm = _head_masks()
        band, own = _swa_band_masks()

        def qblock(i, prev):
            qs = pl.ds(pl.multiple_of(i * BLK, BLK), BLK)
            if prev:
                ks, mask, cols = pl.ds(pl.multiple_of((i - 1) * BLK, BLK), 2 * BLK), band, slice(None)
            else:
                ks, mask, cols = qs, own, slice(BLK, None)
            q8 = _swa_stack(q_ref, qs, hm, SCALE)
            sink8 = _swa_sinks(sink_ref)
            s = _swa_scores(q8, k_ref[ks, :], bias_ref, mask, cols)
            m = jnp.maximum(jnp.max(s, axis=1, keepdims=True), sink8)
            p = jnp.exp(s - m)
            den = jnp.sum(p, axis=1, keepdims=True) + jnp.exp(sink8 - m)
            o8 = _dot((p * (1.0 / den)).astype(BF16), v_ref[ks, :])
            lse8 = m + jnp.log(den)
            for hq in range(SWA_H):
                lse_ref[hq, qs, :] = lse8[hq * BLK:(hq + 1) * BLK]
            for pp, o in enumerate(_swa_unstack(o8, hm)):
                o_ref[qs, pp * LANES:(pp + 1) * LANES] = o

        qblock(0, False)

        def step(i, _):
            qblock(i, True)
            return 0

        lax.fori_loop(1, nq, step, 0)

    return _call(
        body, name="swa_fwd", grid=(1,),
        in_specs=[pl.BlockSpec((t, SWA_W), lambda i: (0, 3)), pl.BlockSpec((t, KV_W), lambda i: (0, 16)),
                  pl.BlockSpec((t, KV_W), lambda i: (0, 17)), pl.BlockSpec((8, BLK, 2 * BLK), lambda i: (0, 0, 0)),
                  pl.BlockSpec((8, LANES), lambda i: (0, 0))],
        out_specs=[pl.BlockSpec((t, SWA_W), lambda i: (0, 0)), pl.BlockSpec((8, t, 1), lambda i: (0, 0, 0))],
        out_shape=[jax.ShapeDtypeStruct((t, SWA_W), F32), jax.ShapeDtypeStruct((8, t, 1), F32)],
        compiler_params=_params(1))(proj, proj, proj, bias, sinks_b)


def _swa_bwd(proj, d_o, lse, bias, sinks_b, dbias_in):
    t = proj.shape[0]
    nq = t // BLK

    def body(q_ref, k_ref, v_ref, do_ref, lse_ref, bias_ref, sink_ref, dbi_ref,
             dq_ref, dk_ref, dv_ref, dsink_ref, dbias_ref, dk_acc, dv_acc):
        hm = _head_masks()
        band, own = _swa_band_masks()
        dk_acc[...] = jnp.zeros_like(dk_acc)
        dv_acc[...] = jnp.zeros_like(dv_acc)
        dbias_ref[...] = dbi_ref[...]

        def qblock(i, prev, dsink8):
            qs = pl.ds(pl.multiple_of(i * BLK, BLK), BLK)
            if prev:
                ks, mask, cols = pl.ds(pl.multiple_of((i - 1) * BLK, BLK), 2 * BLK), band, slice(None)
            else:
                ks, mask, cols = qs, own, slice(BLK, None)
            q8 = _swa_stack(q_ref, qs, hm, SCALE)
            do8 = _swa_stack(do_ref, qs, hm, 1.0)
            sink8 = _swa_sinks(sink_ref)
            lse8 = jnp.concatenate([lse_ref[hq, qs, :] for hq in range(SWA_H)], axis=0)
            kb = k_ref[ks, :]
            p = jnp.exp(_swa_scores(q8, kb, bias_ref, mask, cols) - lse8)
            dp = _dot_nt(do8, v_ref[ks, :])
            delta = jnp.sum(p * dp, axis=1, keepdims=True)
            ds = p * (dp - delta)
            for hq in range(SWA_H):
                dbias_ref[hq, :, cols] += ds[hq * BLK:(hq + 1) * BLK]
            dsb = ds.astype(BF16)
            dk_acc[ks, :] += _dot_tn(dsb, q8)
            dv_acc[ks, :] += _dot_tn(p.astype(BF16), do8)
            for pp, dq in enumerate(_swa_unstack(_dot(dsb, kb) * SCALE, hm)):
                dq_ref[qs, pp * LANES:(pp + 1) * LANES] = dq.astype(BF16)
            return dsink8 - jnp.exp(sink8 - lse8) * delta

        ds0 = qblock(0, False, jnp.zeros((SWA_H * BLK, 1), F32))
        ds8 = lax.fori_loop(1, nq, lambda i, c: qblock(i, True, c), ds0)
        for hq in range(SWA_H):
            dsink_ref[hq:hq + 1, :] = jnp.broadcast_to(
                jnp.sum(ds8[hq * BLK:(hq + 1) * BLK], axis=0, keepdims=True), (1, LANES))

        dk_ref[...] = dk_acc[...].astype(BF16)
        dv_ref[...] = dv_acc[...].astype(BF16)

    full3 = pl.BlockSpec((8, BLK, 2 * BLK), lambda i: (0, 0, 0))
    kv = jax.ShapeDtypeStruct((t, KV_W), BF16)
    return _call(
        body, name="swa_bwd", grid=(1,),
        in_specs=[pl.BlockSpec((t, SWA_W), lambda i: (0, 3)), pl.BlockSpec((t, KV_W), lambda i: (0, 16)),
                  pl.BlockSpec((t, KV_W), lambda i: (0, 17)), pl.BlockSpec((t, SWA_W), lambda i: (0, 1)),
                  pl.BlockSpec((8, t, 1), lambda i: (0, 0, 0)), full3, pl.BlockSpec((8, LANES), lambda i: (0, 0)),
                  full3],
        out_specs=[pl.BlockSpec((t, SWA_W), lambda i: (0, 0)), pl.BlockSpec((t, KV_W), lambda i: (0, 0)),
                   pl.BlockSpec((t, KV_W), lambda i: (0, 0)), pl.BlockSpec((8, LANES), lambda i: (0, 0)), full3],
        out_shape=[jax.ShapeDtypeStruct((t, SWA_W), BF16), kv, kv, jax.ShapeDtypeStruct((8, LANES), F32),
                   jax.ShapeDtypeStruct((8, BLK, 2 * BLK), F32)],
        scratch_shapes=[pltpu.VMEM((t, KV_W), F32), pltpu.VMEM((t, KV_W), F32)],
        compiler_params=_params(1))(proj, proj, proj, d_o, lse, bias, sinks_b, dbias_in)


def _concat_cols(parts):
    t = parts[0].shape[0]
    widths = [a.shape[1] for a in parts]

    def body(*refs):
        refs[-1][...] = jnp.concatenate([r[...] for r in refs[:-1]], axis=1)

    return _call(
        body, name="concat_cols", grid=(t // TM,),
        in_specs=[pl.BlockSpec((TM, w), lambda i: (i, 0)) for w in widths],
        out_specs=pl.BlockSpec((TM, sum(widths)), lambda i: (i, 0)),
        out_shape=jax.ShapeDtypeStruct((t, sum(widths)), parts[0].dtype), compiler_params=_params(1))(*parts)


def _bias_table(rel_bias, buckets):
    def body(rb_ref, b_ref, o_ref):
        bk = b_ref[...]
        for h in range(8):
            acc = jnp.zeros((BLK, 2 * BLK), F32)
            for b in range(N_BUCKETS):
                acc = jnp.where(bk == b, rb_ref[b, h], acc)
            o_ref[h] = acc

    return _call(
        body, name="bias_table", grid=(1,),
        in_specs=[pl.BlockSpec(memory_space=pltpu.SMEM), pl.BlockSpec((BLK, 2 * BLK), lambda i: (0, 0))],
        out_specs=pl.BlockSpec((8, BLK, 2 * BLK), lambda i: (0, 0, 0)),
        out_shape=jax.ShapeDtypeStruct((8, BLK, 2 * BLK), F32), compiler_params=_params(1))(rel_bias, buckets)


def _bias_grad(dbias, buckets):
    def body(d_ref, b_ref, o_ref):
        lane = lax.broadcasted_iota(jnp.int32, (1, LANES), 1)
        bk = b_ref[...]
        for h in range(8):
            d = d_ref[h]
            acc = jnp.zeros((1, LANES), F32)
            for b in range(N_BUCKETS):
                s = jnp.sum(jnp.sum(jnp.where(bk == b, d, 0.0), axis=0, keepdims=True), axis=1, keepdims=True)
                acc = acc + jnp.where(lane == b, s, 0.0)
            o_ref[h:h + 1, :] = acc

    return _call(
        body, name="bias_grad", grid=(1,),
        in_specs=[pl.BlockSpec((8, BLK, 2 * BLK), lambda i: (0, 0, 0)), pl.BlockSpec((BLK, 2 * BLK), lambda i: (0, 0))],
        out_specs=pl.BlockSpec((8, LANES), lambda i: (0, 0)),
        out_shape=jax.ShapeDtypeStruct((8, LANES), F32), compiler_params=_params(1))(dbias, buckets)


def _row(a):
    return a.reshape(1, -1)


def _fwd_ffn1(h, n1, w, small, l):
    s = {"h0": h, "n1": n1}
    s["gu1"], s["act1"] = _ffn_gu(n1, w["ffn1_gu"])
    s["h1"], s["nm"] = _down_res(s["act1"], w["ffn1_down"], h, _row(small["norm_mix"][l]))
    return s


def _fwd_proj_sb(s, w):
    s["proj"] = _proj(s["nm"], w["w_in"])
    s["o_sb"], s["tot"] = _sb_fwd(s["proj"])


def _fwd_swa(s, small, l, bias):
    s["sinks_b"] = jnp.broadcast_to(small["sinks"][l][:, None], (8, LANES))
    s["o_sw"], s["lse"] = _swa_fwd(s["proj"], bias, s["sinks_b"])


def _fwd_out_ffn2(s, w, small, l, g_after):
    s["h2"], s["mixed"], s["n2"] = _out_res(
        s["o_sb"], s["o_sw"], _row(small["norm_out_sb"][l]), _row(small["norm_out_swa"][l]), w["w_out"], s["h1"],
        _row(small["norm_ffn2"][l]))
    s["gu2"], s["act2"] = _ffn_gu(s["n2"], w["ffn2_gu"])
    return _down_res(s["act2"], w["ffn2_down"], s["h2"], g_after)


def _bwd_ffn_dact(dh, s, w, which):
    return _ffn_dact(dh[1], w[f"ffn{which}_down"], s[f"gu{which}"])


def _bwd_ffn_rest(dh, dgu, s, w, small, l, which):
    h_in, norm = (s["h0"], "norm_ffn1") if which == 1 else (s["h2"], "norm_ffn2")
    g_down = _wgrad_down(s[f"act{which}"], dh[1])
    g_gu = _wgrad_gu(s[f"n{which}"], dgu)
    dh32, dh16, dg = _ffn_dn(dgu, w[f"ffn{which}_gu"], dh[0], h_in, _row(small[norm][l]))
    return (dh32, dh16), {f"ffn{which}_down": g_down, f"ffn{which}_gu": g_gu}, {norm: dg}


def _bwd_ffn(dh, s, w, small, l, which):
    return _bwd_ffn_rest(dh, _bwd_ffn_dact(dh, s, w, which), s, w, small, l, which)


def _bwd_mix(dh, s, w, small, l, bias, dbias):
    g_out = _wgrad_out(s["mixed"], dh[1])
    d_o, dg_sb, dg_sw = _dmixed(dh[1], w["w_out"], s["o_sb"], s["o_sw"], _row(small["norm_out_sb"][l]),
                                _row(small["norm_out_swa"][l]))
    dq_sb, dk_sb, dv_sb = _sb_bwd(s["proj"], d_o, s["tot"])
    dq_sw, dk_sw, dv_sw, dsink, dbias = _swa_bwd(s["proj"], d_o, s["lse"], bias, s["sinks_b"], dbias)
    dproj = _concat_cols([dq_sb, dk_sb, dv_sb, dq_sw, dk_sw, dv_sw])
    g_in = _wgrad_in(s["nm"], dproj)
    dh32, dh16, dg_mix = _mix_dn(dproj, w["w_in"], dh[0], s["h1"], _row(small["norm_mix"][l]))
    gs = {"norm_out_sb": dg_sb, "norm_out_swa": dg_sw, "sinks": dsink[:, 0], "norm_mix": dg_mix}
    return (dh32, dh16), {"w_out": g_out, "w_in": g_in}, gs, dbias


def _place():
    x, y, c = lax.axis_index("x"), lax.axis_index("y"), lax.axis_index("c")
    return x, y, c, 2 * x + y


def _chip_core(k, c):
    return (k // 2, k % 2, c)


def _rows_per_block(rows, cols, copies):
    best = 16
    for tr in range(16, rows + 1, 16):
        if rows % tr == 0 and copies * tr * cols * 4 <= SLAB_BLOCK_BYTES:
            best = tr
    assert rows % best == 0
    return best


def _place_own(w, l, me1):
    _, rows, cols = w.shape
    tr = _rows_per_block(rows // 2, cols, 1)
    per_half = rows // 2 // tr

    def body(me_ref, w_ref, o_ref):
        o_ref[...] = w_ref[...].astype(BF16)

    return _call(
        body, name="place_own",
        num_scalar_prefetch=1, grid=(rows // tr,),
        in_specs=[pl.BlockSpec((None, tr, cols), lambda r, me: (l, r, 0))],
        out_specs=pl.BlockSpec((None, None, tr, cols), lambda r, me: (me[0], r // per_half, r % per_half, 0)),
        out_shape=jax.ShapeDtypeStruct((N_CHIPS, 2, rows // 2, cols), BF16), compiler_params=_params(1))(me1, w)


def _plan_gather_ici(bufs):
    _, _, c, me = _place()
    return [(b.at[me, c], b.at[me, c], b.at[(me + 3 - j) % N_CHIPS, c], _chip_core((me + 1 + j) % N_CHIPS, c))
            for b in bufs for j in range(3)]


def _plan_gather_d2d(bufs):
    x, y, c, me = _place()
    return [(b.at[(me + 3 - j) % N_CHIPS, c], b.at[(me + 3 - j) % N_CHIPS, c], b.at[(me + 3 - j) % N_CHIPS, 1 - c],
             (x, y, 1 - c)) for b in bufs for j in range(3)]


def _plan_grad_sibling(bufs):
    x, y, c, _ = _place()
    n = len(bufs) // 2
    return [(g.at[:, 1 - c], z, z, (x, y, 1 - c)) for g, z in zip(bufs[:n], bufs[n:])]


def _plan_grad_chips(bufs):
    _, _, c, me = _place()
    n = len(bufs) // 2
    return [(p.at[j], z.at[j], z.at[j], _chip_core((me + 1 + j) % N_CHIPS, c))
            for p, z in zip(bufs[:n], bufs[n:]) for j in range(3)]


def _plan_grad_halves(bufs):
    x, y, c, _ = _place()
    return [(b.at[c], b.at[c], b.at[1 - c], (x, y, 1 - c)) for b in bufs]


def _remote(src, dst, send_sem, recv_sem, to):
    return pltpu.make_async_remote_copy(src_ref=src, dst_ref=dst, send_sem=send_sem, recv_sem=recv_sem,
                                        device_id=to, device_id_type=MESH)


def _exchange_start_groups(name, plan, groups):
    sizes = [len(g) for g, _ in groups]
    bufs = [a for g, _ in groups for a in g]
    n, n_groups = len(bufs), len(groups)

    def body(*refs):
        ins, sems, token = refs[:n], refs[n:n + 2 * n_groups], refs[-1]
        at = 0
        for k, size in enumerate(sizes):
            for i, (src, dst, _, to) in enumerate(plan(ins[at:at + size])):
                _remote(src, dst, sems[2 * k].at[i], sems[2 * k + 1].at[i], to).start()
            at += size
        token[...] = jnp.zeros_like(token)

    sem_shapes = [pltpu.SemaphoreType.DMA((n_copies,)) for _, n_copies in groups for _ in range(2)]
    out = _call(
        body, name=name,
        out_shape=(*sem_shapes, *[pltpu.HBM(a.shape, a.dtype) for a in bufs], jax.ShapeDtypeStruct((8, LANES), F32)),
        in_specs=[HBM] * n,
        out_specs=(*[SEM] * (2 * n_groups), *[HBM] * n, pl.BlockSpec(memory_space=pltpu.VMEM)),
        input_output_aliases={t: 2 * n_groups + t for t in range(n)}, hbm_args=n,
        compiler_params=pltpu.CompilerParams(has_side_effects=EFFECT),
    )(*bufs)
    flights, at = [], 2 * n_groups
    for k, size in enumerate(sizes):
        flights.append(((out[2 * k], out[2 * k + 1]), list(out[at:at + size])))
        at += size
    return flights


def _exchange_start(name, plan, bufs, n_copies):
    return _exchange_start_groups(name, plan, [(bufs, n_copies)])[0]


def _exchange_wait(name, plan, bufs, sems):
    n = len(bufs)

    def body(*refs):
        ins = refs[:n]
        ssem, rsem = refs[n], refs[n + 1]
        for i, (src, dst, land, to) in enumerate(plan(ins)):
            _remote(src, dst, ssem.at[i], rsem.at[i], to).wait_send()
            _remote(land, land, ssem.at[i], rsem.at[i], to).wait_recv()

    return list(_call(
        body, name=name, out_shape=[pltpu.HBM(a.shape, a.dtype) for a in bufs],
        in_specs=[HBM] * n + [SEM, SEM], out_specs=[HBM] * n,
        input_output_aliases={t: t for t in range(n)},
        compiler_params=pltpu.CompilerParams(has_side_effects=EFFECT),
    )(*bufs, sems[0], sems[1]))


def _exchange_pass(name, done, plan, bufs, sems, n_copies):
    n = len(bufs)

    def body(*refs):
        ins = refs[:n]
        old_s, old_r, ssem, rsem = refs[n], refs[n + 1], refs[n + 2], refs[n + 3]
        token = refs[-1]
        for i, (src, dst, land, to) in enumerate(done(ins)):
            _remote(src, dst, old_s.at[i], old_r.at[i], to).wait_send()
            _remote(land, land, old_s.at[i], old_r.at[i], to).wait_recv()
        for i, (src, dst, _, to) in enumerate(plan(ins)):
            _remote(src, dst, ssem.at[i], rsem.at[i], to).start()
        token[...] = jnp.zeros_like(token)

    out = _call(
        body, name=name,
        out_shape=(pltpu.SemaphoreType.DMA((n_copies,)), pltpu.SemaphoreType.DMA((n_copies,)),
                   *[pltpu.HBM(a.shape, a.dtype) for a in bufs], jax.ShapeDtypeStruct((8, LANES), F32)),
        in_specs=[HBM] * n + [SEM, SEM], out_specs=(SEM, SEM, *[HBM] * n, pl.BlockSpec(memory_space=pltpu.VMEM)),
        input_output_aliases={t: 2 + t for t in range(n)},
        compiler_params=pltpu.CompilerParams(has_side_effects=EFFECT),
    )(*bufs, sems[0], sems[1])
    return (out[0], out[1]), list(out[2:2 + n])


def _chip_sum(g, xbuf, cm):
    _, _, r2, cols = g.shape
    tr = _rows_per_block(r2, cols, 1)

    def body(cm_ref, g_ref, x_ref, o_ref):
        o_ref[...] = (g_ref[...] + x_ref[...]).astype(BF16)

    return _call(
        body, name="grad_chip_sum",
        num_scalar_prefetch=1, grid=(3, r2 // tr),
        in_specs=[pl.BlockSpec((None, None, tr, cols), lambda j, r, cm: ((cm[1] + 1 + j) % N_CHIPS, cm[0], r, 0)),
                  pl.BlockSpec((None, tr, cols), lambda j, r, cm: ((cm[1] + 1 + j) % N_CHIPS, r, 0))],
        out_specs=pl.BlockSpec((None, tr, cols), lambda j, r, cm: (j, r, 0)),
        out_shape=jax.ShapeDtypeStruct((3, r2, cols), BF16), compiler_params=_params(2))(cm, g, xbuf)


def _total_sum(g, xbuf, rbuf, cm):
    _, _, r2, cols = g.shape
    tr = _rows_per_block(r2, cols, 3)

    def body(cm_ref, g_ref, x_ref, r_ref, o_ref):
        acc = g_ref[...] + x_ref[...]
        for j in range(3):
            acc = acc + r_ref[j].astype(F32)
        o_ref[...] = acc

    return _call(
        body, name="grad_total_sum",
        num_scalar_prefetch=1, grid=(r2 // tr,),
        in_specs=[pl.BlockSpec((None, None, tr, cols), lambda r, cm: (cm[1], cm[0], r, 0)),
                  pl.BlockSpec((None, tr, cols), lambda r, cm: (cm[1], r, 0)),
                  pl.BlockSpec((3, tr, cols), lambda r, cm: (0, r, 0))],
        out_specs=pl.BlockSpec((None, tr, cols), lambda r, cm: (cm[0], r, 0)),
        out_shape=jax.ShapeDtypeStruct((2, r2, cols), F32), compiler_params=_params(1))(cm, g, xbuf, rbuf)


def _small_allreduce(v):
    rows = v.shape[0]
    n_dev = 2 * N_CHIPS

    def body(v_ref, o_ref, buf, ssem, rsem):
        x, y, c, _ = _place()
        me = 4 * x + 2 * y + c
        buf[me] = v_ref[...]

        def copy(d, slot, to):
            return _remote(v_ref, buf.at[slot], ssem.at[d - 1], rsem.at[d - 1], (to // 4, (to // 2) % 2, to % 2))

        cps = [copy(d, me, (me + d) % n_dev) for d in range(1, n_dev)]
        for cp in cps:
            cp.start()
        for d in range(1, n_dev):
            copy(d, (me + n_dev - d) % n_dev, me).wait_recv()
        for cp in cps:
            cp.wait_send()
        acc = buf[0]
        for i in range(1, n_dev):
            acc = acc + buf[i]
        o_ref[...] = acc

    vm = pl.BlockSpec(memory_space=pltpu.VMEM)
    return _call(
        body, name="small_allreduce", in_specs=[vm], out_specs=vm,
        out_shape=jax.ShapeDtypeStruct(v.shape, F32),
        scratch_shapes=[pltpu.VMEM((n_dev, rows, LANES), F32), pltpu.SemaphoreType.DMA((n_dev - 1,)),
                        pltpu.SemaphoreType.DMA((n_dev - 1,))],
        compiler_params=pltpu.CompilerParams(vmem_limit_bytes=V7X_VMEM_LIMIT))(v)


def _adamw_math(w, g, m, v):
    m2 = ADAM_B1 * m + (1.0 - ADAM_B1) * g
    v2 = ADAM_B2 * v + (1.0 - ADAM_B2) * (g * g)
    m_hat = m2 / (1.0 - ADAM_B1 ** ADAM_STEP)
    v_hat = v2 / (1.0 - ADAM_B2 ** ADAM_STEP)
    return -ADAM_LR * (m_hat / (jnp.sqrt(v_hat) + ADAM_EPS) + ADAM_WD * w), m2, v2


def _adamw_layer(w, g, m, v, l, prev):
    _, rows, cols = w.shape
    tr = rows
    for cand in range(8, rows + 1, 8):
        if rows % cand == 0 and cand * cols * 4 <= ADAMW_BLOCK_BYTES:
            tr = cand

    def body(w_ref, g_ref, m_ref, v_ref, *outs):
        go_ref, d_ref, m2_ref, v2_ref = outs[-4:]
        g = g_ref[...]
        go_ref[...] = g
        d_ref[...], m2_ref[...], v2_ref[...] = _adamw_math(w_ref[...], g, m_ref[...], v_ref[...])

    stack = pl.BlockSpec((None, tr, cols), lambda i: (l, i, 0))
    ins, specs, alias = [w, g, m, v], [stack, pl.BlockSpec((tr, cols), lambda i: (i, 0)), stack, stack], {}
    if prev is not None:
        ins += list(prev)
        specs += [ANY] * 4
        alias = {4 + i: i for i in range(4)}
    return _call(
        body, name="adamw", grid=(rows // tr,), in_specs=specs, out_specs=[stack] * 4,
        out_shape=[jax.ShapeDtypeStruct(w.shape, F32)] * 4, input_output_aliases=alias,
        compiler_params=_params(1))(*ins)


def _adamw_small(w, g, m, v):
    def body(w_ref, g_ref, m_ref, v_ref, d_ref, m2_ref, v2_ref):
        d_ref[...], m2_ref[...], v2_ref[...] = _adamw_math(w_ref[...], g_ref[...], m_ref[...], v_ref[...])

    spec = pl.BlockSpec(w.shape, lambda i: (0, 0))
    return _call(
        body, name="adamw_small", grid=(1,), in_specs=[spec] * 4, out_specs=[spec] * 3,
        out_shape=[jax.ShapeDtypeStruct(w.shape, F32)] * 3, compiler_params=_params(1))(w, g, m, v)


SMALL = ("norm_ffn1", "norm_mix", "sinks", "norm_out_sb", "norm_out_swa", "norm_ffn2", "rel_bias", "norm_final")
BIG = ("ffn1_gu", "ffn1_down", "w_in", "w_out", "ffn2_gu", "ffn2_down")


def _pack(parts):
    flat, n = [], 0
    for a in parts:
        a = a.reshape(-1).astype(F32)
        gap = -a.shape[0] % LANES
        flat += [a] + ([jnp.zeros((gap,), F32)] if gap else [])
        n += a.shape[0] + gap
    tail = -(n // LANES) % 8 * LANES
    return jnp.concatenate(flat + ([jnp.zeros((tail,), F32)] if tail else [])).reshape(-1, LANES)


def _unpack(packed, like):
    out, r = [], 0
    for a in like:
        n = math.prod(a.shape)
        nr = -(-n // LANES)
        out.append(packed[r:r + nr].reshape(-1)[:n].reshape(a.shape))
        r += nr
    return out


def _halved(a):
    k, r, cols = a.shape
    return a.reshape(k, 2, r // 2, cols)


def _weight_view(k, buf):
    full = buf.reshape(N_CHIPS, buf.shape[2] * 2, buf.shape[3])
    return full if k.endswith("_gu") else full.reshape(-1, D_MODEL)


def _grad_stack(k, g):
    if not k.endswith("_gu"):
        g = g.reshape(N_CHIPS, g.shape[0] // N_CHIPS, D_MODEL)
    return _halved(g)


def _empty_like_hbm(shape, dtype):
    return pltpu.with_memory_space_constraint(lax.empty(shape, dtype), pltpu.HBM)


def kernel(x, norm_ffn1, w_ffn1_gu, w_ffn1_down, norm_mix, w_in, sinks, norm_out_sb, norm_out_swa, w_out, norm_ffn2, w_ffn2_gu, w_ffn2_down, rel_bias, norm_final, loss_target, m_norm_ffn1, m_w_ffn1_gu, m_w_ffn1_down, m_norm_mix, m_w_in, m_sinks, m_norm_out_sb, m_norm_out_swa, m_w_out, m_norm_ffn2, m_w_ffn2_gu, m_w_ffn2_down, m_rel_bias, m_norm_final, v_norm_ffn1, v_w_ffn1_gu, v_w_ffn1_down, v_norm_mix, v_w_in, v_sinks, v_norm_out_sb, v_norm_out_swa, v_w_out, v_norm_ffn2, v_w_ffn2_gu, v_w_ffn2_down, v_rel_bias, v_norm_final):
    big_w = dict(ffn1_gu=w_ffn1_gu, ffn1_down=w_ffn1_down, w_in=w_in, w_out=w_out, ffn2_gu=w_ffn2_gu, ffn2_down=w_ffn2_down)
    big_m = dict(ffn1_gu=m_w_ffn1_gu, ffn1_down=m_w_ffn1_down, w_in=m_w_in, w_out=m_w_out, ffn2_gu=m_w_ffn2_gu, ffn2_down=m_w_ffn2_down)
    big_v = dict(ffn1_gu=v_w_ffn1_gu, ffn1_down=v_w_ffn1_down, w_in=v_w_in, w_out=v_w_out, ffn2_gu=v_w_ffn2_gu, ffn2_down=v_w_ffn2_down)
    small = dict(norm_ffn1=norm_ffn1, norm_mix=norm_mix, sinks=sinks, norm_out_sb=norm_out_sb, norm_out_swa=norm_out_swa,
                 norm_ffn2=norm_ffn2, rel_bias=rel_bias, norm_final=norm_final)
    small_m = dict(norm_ffn1=m_norm_ffn1, norm_mix=m_norm_mix, sinks=m_sinks, norm_out_sb=m_norm_out_sb,
                   norm_out_swa=m_norm_out_swa, norm_ffn2=m_norm_ffn2, rel_bias=m_rel_bias, norm_final=m_norm_final)
    small_v = dict(norm_ffn1=v_norm_ffn1, norm_mix=v_norm_mix, sinks=v_sinks, norm_out_sb=v_norm_out_sb,
                   norm_out_swa=v_norm_out_swa, norm_ffn2=v_norm_ffn2, rel_bias=v_rel_bias, norm_final=v_norm_final)
    for dct in (big_w, big_m, big_v):
        dct["w_in"] = jnp.swapaxes(dct["w_in"], 1, 2)
    _PREVIOUS[0] = None
    _, _, c, me = _place()
    cm = jnp.stack([c, me]).astype(jnp.int32)
    buckets = jnp.asarray(_bucket_table())
    ffn1, mix_in, rest = ("ffn1_gu", "ffn1_down"), ("w_in",), ("w_out", "ffn2_gu", "ffn2_down")

    def place(l, keys):
        return [_place_own(big_w[k], l, cm[1:]) for k in keys]

    def views(keys, bufs):
        return {k: _weight_view(k, b) for k, b in zip(keys, bufs)}

    def gather_start(tag, bufs):
        return _exchange_start(f"gather{tag}_ici_start", _plan_gather_ici, bufs, 3 * len(bufs))

    def gather_pass(tag, flight):
        return _exchange_pass(f"gather{tag}_pass", _plan_gather_ici, _plan_gather_d2d, flight[1], flight[0],
                              3 * len(flight[1]))

    def gather_done(tag, keys, flight):
        return views(keys, _exchange_wait(f"gather{tag}_d2d_wait", _plan_gather_d2d, flight[1], flight[0]))

    fly_ffn0 = gather_start("0a", place(0, ffn1))
    later = [place(l, keys) for l in range(DEPTH) for keys in ((mix_in, rest) if l == 0 else (ffn1, mix_in, rest))]
    fly_in0, fly_rest0, fly_ffn1, fly_in1, fly_rest1 = _exchange_start_groups(
        "gather_later_ici_start", _plan_gather_ici, [(bufs, 3 * len(bufs)) for bufs in later])
    bias = _bias_table(rel_bias, buckets)
    n1 = _norm_cast(x[0], _row(norm_ffn1[0]))
    w0 = gather_done("0a", ffn1, gather_pass("0a", fly_ffn0))

    fly_in0 = gather_pass("0b", fly_in0)
    s0 = _fwd_ffn1(x[0], n1, w0, small, 0)
    w0.update(gather_done("0b", mix_in, fly_in0))
    _fwd_proj_sb(s0, w0)
    fly_rest0 = gather_pass("0c", fly_rest0)
    _fwd_swa(s0, small, 0, bias)
    w0.update(gather_done("0c", rest, fly_rest0))
    fly_ffn1 = gather_pass("1a", fly_ffn1)
    h, n1 = _fwd_out_ffn2(s0, w0, small, 0, _row(norm_ffn1[1]))
    w1 = gather_done("1a", ffn1, fly_ffn1)
    fly_in1 = gather_pass("1b", fly_in1)
    s1 = _fwd_ffn1(h, n1, w1, small, 1)
    w1.update(gather_done("1b", mix_in, fly_in1))
    _fwd_proj_sb(s1, w1)
    fly_rest1 = gather_pass("1c", fly_rest1)
    _fwd_swa(s1, small, 1, bias)
    w1.update(gather_done("1c", rest, fly_rest1))
    h, _ = _fwd_out_ffn2(s1, w1, small, 1, _row(norm_final))
    dh32, dh16, dg_final, loss_row = _loss_head(h, _row(norm_final), loss_target[0])
    dh = (dh32, dh16)

    def landing(stacks, lead, dtype):
        return [_empty_like_hbm((lead,) + a.shape[2:], dtype) for a in stacks]

    def reduce_begin(tag, keys, gw):
        stacks = [_grad_stack(k, gw[k]) for k in keys]
        flight = _exchange_start(f"grad{tag}_sibling_start", _plan_grad_sibling,
                                 stacks + landing(stacks, N_CHIPS, F32), len(keys))
        return dict(tag=tag, keys=keys, stacks=stacks, flight=flight)

    def reduce_chips(st):
        n, (sems, bufs) = len(st["keys"]), st["flight"]
        bufs = _exchange_wait(f"grad{st['tag']}_sibling_wait", _plan_grad_sibling, bufs, sems)
        st["own"] = list(zip(bufs[:n], bufs[n:]))
        st["flight"] = _exchange_start(f"grad{st['tag']}_chips_start", _plan_grad_chips,
                                       [_chip_sum(g, z, cm) for g, z in st["own"]] + landing(st["stacks"], 3, BF16),
                                       3 * n)

    def reduce_halves(st):
        n, (sems, bufs) = len(st["keys"]), st["flight"]
        bufs = _exchange_wait(f"grad{st['tag']}_chips_wait", _plan_grad_chips, bufs, sems)
        halves = [_total_sum(g, x, z, cm) for (g, x), z in zip(st["own"], bufs[n:])]
        st["flight"] = _exchange_start(f"grad{st['tag']}_halves_start", _plan_grad_halves, halves, n)

    def reduce_end(st):
        sems, bufs = st["flight"]
        bufs = _exchange_wait(f"grad{st['tag']}_halves_wait", _plan_grad_halves, bufs, sems)
        return {k: b.reshape(big_w[k].shape[1:]) for k, b in zip(st["keys"], bufs)}

    def adamw(reduced, l, prev):
        return {k: _adamw_layer(big_w[k], g, big_m[k], big_v[k], l, None if prev is None else prev[k])
                for k, g in reduced.items()}

    gsm = [dict() for _ in range(DEPTH)]
    dbias = jnp.zeros((8, BLK, 2 * BLK), F32)
    dh, gw1, gs = _bwd_ffn(dh, s1, w1, small, 1, 2)
    gsm[1].update(gs)
    dh, gw, gs, dbias = _bwd_mix(dh, s1, w1, small, 1, bias, dbias)
    gw1.update(gw)
    gsm[1].update(gs)
    dh, gw, gs = _bwd_ffn(dh, s1, w1, small, 1, 1)
    gw1.update(gw)
    gsm[1].update(gs)

    red1 = reduce_begin("1", BIG, gw1)
    dh, gw0, gs = _bwd_ffn(dh, s0, w0, small, 0, 2)
    gsm[0].update(gs)
    reduce_chips(red1)
    dh, gw, gs, dbias = _bwd_mix(dh, s0, w0, small, 0, bias, dbias)
    gw0.update(gw)
    gsm[0].update(gs)
    red0a = reduce_begin("0a", ("ffn2_gu", "ffn2_down", "w_out", "w_in"), gw0)
    reduce_halves(red1)
    dgu = _bwd_ffn_dact(dh, s0, w0, 1)
    reduce_chips(red0a)
    dh, gw, gs = _bwd_ffn_rest(dh, dgu, s0, w0, small, 0, 1)
    gsm[0].update(gs)
    red0b = reduce_begin("0b", ffn1, gw)
    reduced1 = reduce_end(red1)
    stacks = adamw({k: reduced1[k] for k in ffn1}, 1, None)

    gsmall = {k: jnp.stack([gsm[l][k].reshape(-1) for l in range(DEPTH)]) for k in gsm[0]}
    gsmall["rel_bias"] = jnp.transpose(_bias_grad(dbias, buckets)[:, :N_BUCKETS])
    gsmall["norm_final"] = dg_final.reshape(-1)
    small_like = [small[k] for k in SMALL]
    pk = lambda dct: _pack([dct[k] for k in SMALL])
    red = _small_allreduce(_pack([gsmall[k] for k in SMALL] + [loss_row[0, :1]]))
    gs = _unpack(red, small_like + [loss_row[0, :1]])
    loss = gs[-1][0]
    gs = dict(zip(SMALL, gs[:-1]))

    ffn2 = ("ffn2_gu", "ffn2_down")
    reduce_chips(red0b)
    stacks.update(adamw({k: reduced1[k] for k in ("w_in", "w_out")}, 1, None))
    reduce_halves(red0a)
    stacks.update(adamw({k: reduced1[k] for k in ffn2}, 1, None))
    dlt, m2, v2 = _adamw_small(pk(small), pk(gs), pk(small_m), pk(small_v))
    reduced0a = reduce_end(red0a)
    stacks.update(adamw({k: reduced0a[k] for k in ffn2}, 0, stacks))
    reduce_halves(red0b)
    stacks.update(adamw({k: reduced0a[k] for k in ("w_in", "w_out")}, 0, stacks))
    stacks.update(adamw(reduce_end(red0b), 0, stacks))

    out_g, out_d, out_m, out_v = {}, {}, {}, {}
    for k in BIG:
        out_g[k], out_d[k], out_m[k], out_v[k] = [jnp.swapaxes(a, 1, 2) if k == "w_in" else a for a in stacks[k]]
    for dst, packed in ((out_d, dlt), (out_m, m2), (out_v, v2)):
        dst.update(zip(SMALL, _unpack(packed, small_like)))
    out_g.update(gs)

    order = ("norm_ffn1", "ffn1_gu", "ffn1_down", "norm_mix", "w_in", "sinks", "norm_out_sb", "norm_out_swa", "w_out",
             "norm_ffn2", "ffn2_gu", "ffn2_down", "rel_bias", "norm_final")
    return (loss, dh[0].reshape(x.shape), *[out_g[k] for k in order], *[out_d[k] for k in order],
            *[out_m[k] for k in order], *[out_v[k] for k in order])
```

```python
import math

import numpy as np
import jax
import jax.numpy as jnp
from jax import lax
from jax.experimental import pallas as pl
from jax.experimental.pallas import tpu as pltpu

F32 = jnp.float32
BF16 = jnp.bfloat16

D_MODEL = 1024
DEPTH = 2
HEAD_DIM = 64
BLK = 128
N_BUCKETS = 32
MAX_DISTANCE = 128
D_FF = 2816
EPS = 1e-6
NEG_INF = -1e30
SB_W = 512
SWA_W = 512
KV_W = 128
IN_W = 2304
SCALE = HEAD_DIM ** -0.5
N_CHIPS = 4
FS = 2 * D_FF // N_CHIPS
LANES = 128
V7X_VMEM_LIMIT = 56 * 2 ** 20
TM = 512
SLAB_BLOCK_BYTES = 6 * 2 ** 20
ADAMW_BLOCK_BYTES = 2 ** 21
SB_KT = 512
SWA_G = 4

ADAM_LR = 0.001
ADAM_B1 = 0.9
ADAM_B2 = 0.999
ADAM_EPS = 1e-08
ADAM_WD = 0.01
ADAM_STEP = 10

MESH = pl.DeviceIdType.MESH
ANY = pl.BlockSpec(memory_space=pl.ANY)
HBM = pl.BlockSpec(memory_space=pltpu.HBM)
SEM = pl.BlockSpec(memory_space=pltpu.SEMAPHORE)
EFFECT = pltpu.SideEffectType.DATAFLOW_SIDE_EFFECTING


def _params(n_grid):
    return pltpu.CompilerParams(dimension_semantics=("arbitrary",) * n_grid, vmem_limit_bytes=V7X_VMEM_LIMIT)


_PREVIOUS = [None]


def _call(body, *, name, in_specs, out_specs, out_shape, grid=(), num_scalar_prefetch=0, scratch_shapes=(),
          input_output_aliases=None, compiler_params=None, hbm_args=0):
    n_in = len(in_specs)

    def run(*args):
        dep = _PREVIOUS[0]
        if any(dep is a for a in args):
            dep = None
        args = [pltpu.with_memory_space_constraint(a, pltpu.HBM) if i < hbm_args else a for i, a in enumerate(args)]
        specs = list(in_specs) + ([ANY] if dep is not None else [])
        k = num_scalar_prefetch + n_in
        fn = body if dep is None else (lambda *refs: body(*refs[:k], *refs[k + 1:]))
        if num_scalar_prefetch:
            shape = dict(grid_spec=pltpu.PrefetchScalarGridSpec(
                num_scalar_prefetch=num_scalar_prefetch, grid=grid, in_specs=specs, out_specs=out_specs,
                scratch_shapes=scratch_shapes))
        else:
            shape = dict(grid=grid, in_specs=specs, out_specs=out_specs, scratch_shapes=scratch_shapes)
        out = pl.pallas_call(fn, name=name, out_shape=out_shape, input_output_aliases=input_output_aliases or {},
                             compiler_params=compiler_params, **shape)(*args, *([] if dep is None else [dep]))
        _PREVIOUS[0] = jax.tree.leaves(out)[-1]
        return out

    return run


def _dot(a, b):
    return jnp.dot(a, b, preferred_element_type=F32)


def _dot_nt(a, b):
    return lax.dot_general(a, b, (((1,), (1,)), ((), ())), preferred_element_type=F32)


def _dot_tn(a, b):
    return lax.dot_general(a, b, (((0,), (0,)), ((), ())), preferred_element_type=F32)


def _rms_fwd(x, g):
    r = lax.rsqrt(jnp.mean(x * x, axis=-1, keepdims=True) + EPS)
    xh = x * r
    return xh * g, xh, r


def _rms_bwd(dy, xh, r, g):
    u = dy * g
    dx = r * (u - xh * jnp.mean(u * xh, axis=-1, keepdims=True))
    dg = jnp.sum(dy * xh, axis=0, keepdims=True)
    return dx, dg


def _softplus(z):
    neg_abs = lax.bitcast_convert_type(lax.bitcast_convert_type(z, jnp.int32) | jnp.int32(-2 ** 31), F32)
    sp = jnp.maximum(z, 0.0) + jnp.log(1.0 + jnp.exp(neg_abs))
    return sp, z - sp


def _norm_cast(h, g):
    t, w = h.shape

    def body(h_ref, g_ref, n_ref):
        y, _, _ = _rms_fwd(h_ref[...], g_ref[...])
        n_ref[...] = y.astype(BF16)

    return _call(
        body, name="norm_cast", grid=(t // TM,),
        in_specs=[pl.BlockSpec((TM, w), lambda i: (i, 0)), pl.BlockSpec((1, w), lambda i: (0, 0))],
        out_specs=pl.BlockSpec((TM, w), lambda i: (i, 0)),
        out_shape=jax.ShapeDtypeStruct((t, w), BF16), compiler_params=_params(1))(h, g)


def _ffn_gu(n, wgu):
    t, d = n.shape

    def body(n_ref, wg_ref, wu_ref, gu_ref, act_ref):
        x = n_ref[...]
        g = _dot(x, wg_ref[...])
        u = _dot(x, wu_ref[...])
        sig = jax.nn.sigmoid(g)
        silu = g * sig
        gu_ref[0] = (u * (sig + silu * (1.0 - sig))).astype(BF16)
        gu_ref[1] = silu.astype(BF16)
        act_ref[...] = (silu * u).astype(BF16)

    return _call(
        body, name="ffn_gu", grid=(2, t // TM),
        in_specs=[pl.BlockSpec((TM, d), lambda j, i: (i, 0)),
                  pl.BlockSpec((None, d, FS), lambda j, i: (j, 0, 0)),
                  pl.BlockSpec((None, d, FS), lambda j, i: (j + 2, 0, 0))],
        out_specs=[pl.BlockSpec((2, TM, FS), lambda j, i: (0, i, j)), pl.BlockSpec((TM, FS), lambda j, i: (i, j))],
        out_shape=[jax.ShapeDtypeStruct((2, t, D_FF), BF16), jax.ShapeDtypeStruct((t, D_FF), BF16)],
        compiler_params=_params(2))(n, wgu, wgu)


def _down_res(act, wdn, h, g_next):
    t, f = act.shape
    d = h.shape[1]

    def body(a_ref, w_ref, h_ref, g_ref, o_ref, n_ref):
        out = h_ref[...] + 0.5 * _dot(a_ref[...], w_ref[...])
        o_ref[...] = out
        n_ref[...] = _rms_fwd(out, g_ref[...])[0].astype(BF16)

    row = pl.BlockSpec((TM, d), lambda i: (i, 0))
    return _call(
        body, name="down_res", grid=(t // TM,),
        in_specs=[pl.BlockSpec((TM, f), lambda i: (i, 0)), pl.BlockSpec((f, d), lambda i: (0, 0)), row,
                  pl.BlockSpec((1, d), lambda i: (0, 0))],
        out_specs=[row, row],
        out_shape=[jax.ShapeDtypeStruct((t, d), F32), jax.ShapeDtypeStruct((t, d), BF16)],
        compiler_params=_params(1))(act, wdn, h, g_next)


def _proj(n, w_in_t):
    t, d = n.shape
    w = w_in_t.shape[0]

    def body(n_ref, w_ref, o_ref):
        o_ref[...] = _dot_nt(n_ref[...], w_ref[...]).astype(BF16)

    return _call(
        body, name="proj", grid=(t // TM,),
        in_specs=[pl.BlockSpec((TM, d), lambda i: (i, 0)), pl.BlockSpec((w, d), lambda i: (0, 0))],
        out_specs=pl.BlockSpec((TM, w), lambda i: (i, 0)),
        out_shape=jax.ShapeDtypeStruct((t, w), BF16), compiler_params=_params(1))(n, w_in_t)


def _out_res(o_sb, o_sw, g_sb, g_sw, w_out, h, g_next):
    t, d = h.shape

    def body(a_ref, b_ref, ga_ref, gb_ref, w_ref, h_ref, g_ref, o_ref, mix_ref, n_ref):
        ya, _, _ = _rms_fwd(a_ref[...], ga_ref[...])
        yb, _, _ = _rms_fwd(b_ref[...], gb_ref[...])
        mixed = jnp.concatenate([ya.astype(BF16), yb.astype(BF16)], axis=1)
        mix_ref[...] = mixed
        out = h_ref[...] + _dot(mixed, w_ref[...])
        o_ref[...] = out
        n_ref[...] = _rms_fwd(out, g_ref[...])[0].astype(BF16)

    row = pl.BlockSpec((TM, d), lambda i: (i, 0))
    return _call(
        body, name="out_res", grid=(t // TM,),
        in_specs=[pl.BlockSpec((TM, SB_W), lambda i: (i, 0)), pl.BlockSpec((TM, SWA_W), lambda i: (i, 0)),
                  pl.BlockSpec((1, SB_W), lambda i: (0, 0)), pl.BlockSpec((1, SWA_W), lambda i: (0, 0)),
                  pl.BlockSpec((d, d), lambda i: (0, 0)), row, pl.BlockSpec((1, d), lambda i: (0, 0))],
        out_specs=[row, row, row],
        out_shape=[jax.ShapeDtypeStruct((t, d), F32), jax.ShapeDtypeStruct((t, d), BF16),
                   jax.ShapeDtypeStruct((t, d), BF16)],
        compiler_params=_params(1))(o_sb, o_sw, g_sb, g_sw, w_out, h, g_next)


def _loss_head(h, g, tgt):
    t, d = h.shape

    def body(h_ref, g_ref, t_ref, dh_ref, dhb_ref, dg_ref, loss_ref):
        @pl.when(pl.program_id(0) == 0)
        def _():
            dg_ref[...] = jnp.zeros_like(dg_ref)
            loss_ref[...] = jnp.zeros_like(loss_ref)

        gg = g_ref[...]
        y, xh, r = _rms_fwd(h_ref[...], gg)
        err = y - t_ref[...]
        part = 0.5 * jnp.sum(jnp.sum(err * err, axis=1, keepdims=True) / d, axis=0, keepdims=True)
        loss_ref[...] += jnp.broadcast_to(part, loss_ref.shape)
        dx, dg = _rms_bwd(err / d, xh, r, gg)
        dh_ref[...] = dx
        dhb_ref[...] = dx.astype(BF16)
        dg_ref[...] += dg

    row = pl.BlockSpec((TM, d), lambda i: (i, 0))
    return _call(
        body, name="loss_head", grid=(t // TM,),
        in_specs=[row, pl.BlockSpec((1, d), lambda i: (0, 0)), row],
        out_specs=[row, row, pl.BlockSpec((1, d), lambda i: (0, 0)), pl.BlockSpec((1, LANES), lambda i: (0, 0))],
        out_shape=[jax.ShapeDtypeStruct((t, d), F32), jax.ShapeDtypeStruct((t, d), BF16),
                   jax.ShapeDtypeStruct((1, d), F32), jax.ShapeDtypeStruct((1, LANES), F32)],
        compiler_params=_params(1))(h, g, tgt)


def _ffn_dact(dh, wdn, gu):
    t, d = dh.shape
    tm = TM

    def body(dh_ref, w_ref, gu_ref, o_ref):
        da = 0.5 * _dot_nt(dh_ref[...].astype(BF16), w_ref[...])
        o_ref[0] = (da * gu_ref[0].astype(F32)).astype(BF16)
        o_ref[1] = (da * gu_ref[1].astype(F32)).astype(BF16)

    return _call(
        body, name="ffn_dact", grid=(2, t // tm),
        in_specs=[pl.BlockSpec((tm, d), lambda j, i: (i, 0)), pl.BlockSpec((FS, d), lambda j, i: (j, 0)),
                  pl.BlockSpec((2, tm, FS), lambda j, i: (0, i, j))],
        out_specs=pl.BlockSpec((2, tm, FS), lambda j, i: (0, i, j)),
        out_shape=jax.ShapeDtypeStruct((2, t, D_FF), BF16), compiler_params=_params(2))(dh, wdn, gu)


def _dn_norm_bwd(a, a_spec, w, w_spec, nk, dh, h_in, g, w_transposed=False, tm=TM):
    t, d = dh.shape
    mm = _dot if w_transposed else _dot_nt

    def body(a_ref, w_ref, dh_ref, h_ref, g_ref, o_ref, ob_ref, dg_ref, acc_ref):
        i, k = pl.program_id(0), pl.program_id(1)

        if nk > 1:
            @pl.when(k == 0)
            def _():
                acc_ref[...] = mm(a_ref[...], w_ref[...])

            @pl.when((k > 0) & (k < nk - 1))
            def _():
                acc_ref[...] += mm(a_ref[...], w_ref[...])

        @pl.when(k == nk - 1)
        def _():
            gg = g_ref[...]
            dg = jnp.zeros_like(gg)
            for rows in (slice(r, r + TM // 2) for r in range(0, tm, TM // 2)):
                dn = mm(a_ref[rows, :], w_ref[...])
                if nk > 1:
                    dn = dn + acc_ref[rows, :]
                _, xh, r = _rms_fwd(h_ref[rows, :], gg)
                dx, dg_rows = _rms_bwd(dn, xh, r, gg)
                out = dh_ref[rows, :] + dx
                o_ref[rows, :] = out
                ob_ref[rows, :] = out.astype(BF16)
                dg = dg + dg_rows

            @pl.when(i == 0)
            def _():
                dg_ref[...] = dg

            @pl.when(i > 0)
            def _():
                dg_ref[...] += dg

    row = pl.BlockSpec((tm, d), lambda i, k: (i, 0))
    return _call(
        body, name="dn_norm_bwd", grid=(t // tm, nk),
        in_specs=[a_spec, w_spec, row, row, pl.BlockSpec((1, d), lambda i, k: (0, 0))],
        out_specs=[row, row, pl.BlockSpec((1, d), lambda i, k: (0, 0))],
        out_shape=[jax.ShapeDtypeStruct((t, d), F32), jax.ShapeDtypeStruct((t, d), BF16),
                   jax.ShapeDtypeStruct((1, d), F32)],
        scratch_shapes=[pltpu.VMEM((tm, d), F32)], compiler_params=_params(2))(a, w, dh, h_in, g)


def _ffn_dn(dgu, wgu, dh, h_in, g):
    d = dh.shape[1]
    tm = 2 * TM
    return _dn_norm_bwd(
        dgu, pl.BlockSpec((None, tm, FS), lambda i, k: (k // 2, i, k % 2)),
        wgu, pl.BlockSpec((None, d, FS), lambda i, k: (k, 0, 0)), N_CHIPS, dh, h_in, g, tm=tm)


def _mix_dn(dproj, w_in_t, dh, h_in, g):
    d = dh.shape[1]
    w = dproj.shape[1]
    return _dn_norm_bwd(
        dproj, pl.BlockSpec((TM, w), lambda i, k: (i, 0)),
        w_in_t, pl.BlockSpec((w, d), lambda i, k: (0, 0)), 1, dh, h_in, g, w_transposed=True)


def _dmixed(dh, w_out, o_sb, o_sw, g_sb, g_sw):
    t, d = dh.shape

    def body(dh_ref, w_ref, a_ref, b_ref, ga_ref, gb_ref, o_ref, dga_ref, dgb_ref):
        i = pl.program_id(0)
        dm = _dot_nt(dh_ref[...].astype(BF16), w_ref[...])
        _, xa, ra = _rms_fwd(a_ref[...], ga_ref[...])
        _, xb, rb = _rms_fwd(b_ref[...], gb_ref[...])
        da, dga = _rms_bwd(dm[:, :SB_W], xa, ra, ga_ref[...])
        db, dgb = _rms_bwd(dm[:, SB_W:], xb, rb, gb_ref[...])
        o_ref[...] = jnp.concatenate([da.astype(BF16), db.astype(BF16)], axis=1)

        @pl.when(i == 0)
        def _():
            dga_ref[...] = dga
            dgb_ref[...] = dgb

        @pl.when(i > 0)
        def _():
            dga_ref[...] += dga
            dgb_ref[...] += dgb

    return _call(
        body, name="dmixed", grid=(t // TM,),
        in_specs=[pl.BlockSpec((TM, d), lambda i: (i, 0)), pl.BlockSpec((d, d), lambda i: (0, 0)),
                  pl.BlockSpec((TM, SB_W), lambda i: (i, 0)), pl.BlockSpec((TM, SWA_W), lambda i: (i, 0)),
                  pl.BlockSpec((1, SB_W), lambda i: (0, 0)), pl.BlockSpec((1, SWA_W), lambda i: (0, 0))],
        out_specs=[pl.BlockSpec((TM, d), lambda i: (i, 0)), pl.BlockSpec((1, SB_W), lambda i: (0, 0)),
                   pl.BlockSpec((1, SWA_W), lambda i: (0, 0))],
        out_shape=[jax.ShapeDtypeStruct((t, d), BF16), jax.ShapeDtypeStruct((1, SB_W), F32),
                   jax.ShapeDtypeStruct((1, SWA_W), F32)],
        compiler_params=_params(1))(dh, w_out, o_sb, o_sw, g_sb, g_sw)


def _wgrad(name, a, a_spec, b, b_spec, grid, out_shape, out_spec, scale):
    def body(a_ref, b_ref, o_ref):
        r = _dot_tn(a_ref[...], b_ref[...].astype(BF16))
        o_ref[...] = r if scale == 1.0 else scale * r

    return _call(
        body, name=name, grid=grid, in_specs=[a_spec, b_spec], out_specs=out_spec,
        out_shape=jax.ShapeDtypeStruct(out_shape, F32), compiler_params=_params(len(grid)))(a, b)


def _wgrad_gu(n, dgu):
    t, d = n.shape
    return _wgrad(
        "wgrad_gu", n, pl.BlockSpec((t, TM), lambda s, r: (0, r)),
        dgu, pl.BlockSpec((None, t, FS), lambda s, r: (s // 2, 0, s % 2)), (N_CHIPS, d // TM),
        (N_CHIPS, d, FS), pl.BlockSpec((None, TM, FS), lambda s, r: (s, r, 0)), 1.0)


def _wgrad_down(act, dh):
    t, d = dh.shape
    return _wgrad(
        "wgrad_down", act, pl.BlockSpec((t, FS), lambda s: (0, s)), dh, pl.BlockSpec((t, d), lambda s: (0, 0)),
        (2,), (D_FF, d), pl.BlockSpec((FS, d), lambda s: (s, 0)), 0.5)


def _wgrad_out(mixed, dh):
    t, d = dh.shape
    return _wgrad(
        "wgrad_out", mixed, pl.BlockSpec((t, TM), lambda s: (0, s)), dh, pl.BlockSpec((t, d), lambda s: (0, 0)),
        (d // TM,), (d, d), pl.BlockSpec((TM, d), lambda s: (s, 0)), 1.0)


def _wgrad_in(n, dproj):
    t, d = n.shape
    w = dproj.shape[1]
    tw = w // 3
    return _wgrad(
        "wgrad_in", dproj, pl.BlockSpec((t, tw), lambda s: (0, s)), n, pl.BlockSpec((t, d), lambda s: (0, 0)),
        (3,), (w, d), pl.BlockSpec((tw, d), lambda s: (s, 0)), 1.0)


def _tri(rel):
    row = lax.broadcasted_iota(jnp.int32, (BLK, BLK), 0)
    col = lax.broadcasted_iota(jnp.int32, (BLK, BLK), 1)
    m = rel(row, col).astype(BF16)
    return jnp.concatenate([m, m], axis=0)


def _scan_dot(x, tri2):
    hi = x.astype(BF16)
    lo = (x - hi.astype(F32)).astype(BF16)
    return _dot(jnp.concatenate([hi, lo], axis=1), tri2)


def _head_masks():
    lane = lax.broadcasted_iota(jnp.int32, (1, LANES), 1)
    return [lane < HEAD_DIM, lane >= HEAD_DIM]


SB_PAIRS = 2
SB_ROWS = 2 * SB_PAIRS * BLK


def _sb_causal():
    row = lax.broadcasted_iota(jnp.int32, (SB_ROWS, BLK), 0) & (BLK - 1)
    return lax.broadcasted_iota(jnp.int32, (SB_ROWS, BLK), 1) < row


def _sb_mask_last(x, causal):
    own = jnp.where(causal, x[:, -BLK:], 0.0)
    return own if x.shape[1] == BLK else jnp.concatenate([x[:, :-BLK], own], axis=1)


def _sb_stack(x, hm):
    return jnp.concatenate([jnp.where(m, x[:, p * LANES:(p + 1) * LANES], jnp.zeros((BLK, LANES), x.dtype))
                            for p in range(SB_PAIRS) for m in hm], axis=0)


def _sb_unstack(y, hm):
    return jnp.concatenate([jnp.where(hm[0], y[2 * p * BLK:(2 * p + 1) * BLK], y[(2 * p + 1) * BLK:(2 * p + 2) * BLK])
                            for p in range(SB_PAIRS)], axis=1)


def _sb_pairs():
    return [(slice(2 * p * BLK, (2 * p + 2) * BLK), slice(p * LANES, (p + 1) * LANES)) for p in range(SB_PAIRS)]


def _sb_fwd(proj):
    t = proj.shape[0]
    nb = SB_KT // BLK
    wide = SB_PAIRS * LANES

    def body(q_ref, k_ref, v_ref, o_ref, tot_ref):
        hm = _head_masks()
        causal = _sb_causal()
        pairs = _sb_pairs()
        after = _tri(lambda r, c: r > c)

        def tile(qh, start, n_blk, carry, acc, own):
            ks = pl.ds(pl.multiple_of(start, BLK), n_blk * BLK)
            z = jnp.concatenate([_dot_nt(qh[rows], k_ref[ks, lanes]) for rows, lanes in pairs], axis=0)
            sp, zs = _softplus(z)
            spm = _sb_mask_last(sp, causal) if own else sp
            sufs = [None] * n_blk
            for b in reversed(range(n_blk)):
                blk = spm[:, b * BLK:(b + 1) * BLK]
                sufs[b] = carry + _scan_dot(blk, after)
                carry = carry + jnp.sum(blk, axis=1, keepdims=True)
            w = jnp.exp(zs - jnp.concatenate(sufs, axis=1))
            wb = (_sb_mask_last(w, causal) if own else w).astype(BF16)
            return carry, acc + jnp.concatenate([_dot(wb[rows], v_ref[ks, lanes]) for rows, lanes in pairs], axis=0)

        def qblock(g, j):
            qs = pl.ds(pl.multiple_of(g * SB_KT + j * BLK, BLK), BLK)
            qh = _sb_stack(q_ref[qs, :] * SCALE, hm)
            c0 = tile(qh, g * SB_KT, j + 1, jnp.zeros((SB_ROWS, 1), F32), jnp.zeros((SB_ROWS, LANES), F32), True)
            carry, acc = lax.fori_loop(0, g, lambda n, c: tile(qh, (g - 1 - n) * SB_KT, nb, c[0], c[1], False), c0)
            o_ref[qs, :] = _sb_unstack(acc, hm)
            for h in range(2 * SB_PAIRS):
                tot_ref[h, qs, :] = carry[h * BLK:(h + 1) * BLK]

        def group(g, _):
            for j in range(nb):
                qblock(g, j)
            return 0

        lax.fori_loop(0, t // SB_KT, group, 0)

    col_blk = lambda off: pl.BlockSpec((t, wide), lambda g: (0, off + g))
    n_steps = SB_W // wide
    return _call(
        body, name="sb_fwd", grid=(n_steps,), in_specs=[col_blk(0), col_blk(n_steps), col_blk(2 * n_steps)],
        out_specs=[col_blk(0), pl.BlockSpec((2 * SB_PAIRS, t, 1), lambda g: (g, 0, 0))],
        out_shape=[jax.ShapeDtypeStruct((t, SB_W), F32), jax.ShapeDtypeStruct((8, t, 1), F32)],
        compiler_params=_params(1))(proj, proj, proj)


def _sb_bwd(proj, d_o, tot):
    t = proj.shape[0]
    nb = SB_KT // BLK
    wide = SB_PAIRS * LANES

    def body(q_ref, k_ref, v_ref, do_ref, tot_ref, dq_ref, dk_ref, dv_ref, dk_acc, dv_acc):
        hm = _head_masks()
        causal = _sb_causal()
        pairs = _sb_pairs()
        before = _tri(lambda r, c: r < c)
        upto = _tri(lambda r, c: r <= c)
        dk_acc[...] = jnp.zeros_like(dk_acc)
        dv_acc[...] = jnp.zeros_like(dv_acc)

        def tile(qh, doh, tt, start, n_blk, pre, ecum, dq, own):
            ks = pl.ds(pl.multiple_of(start, BLK), n_blk * BLK)
            k = k_ref[ks, :]
            v = v_ref[ks, :]
            z = jnp.concatenate([_dot_nt(qh[rows], k[:, lanes]) for rows, lanes in pairs], axis=0)
            sp, zs = _softplus(z)
            spm = _sb_mask_last(sp, causal) if own else sp
            pres = []
            for b in range(n_blk):
                blk = spm[:, b * BLK:(b + 1) * BLK]
                pres.append(pre + _scan_dot(blk, before))
                pre = pre + jnp.sum(blk, axis=1, keepdims=True)
            logw = z - (tt - jnp.concatenate(pres, axis=1))
            if own:
                logw = jnp.minimum(logw, 0.0)
            w = jnp.exp(logw)
            if own:
                w = _sb_mask_last(w, causal)
            e = w * jnp.concatenate([_dot_nt(doh[rows], v[:, lanes]) for rows, lanes in pairs], axis=0)
            incs = []
            for b in range(n_blk):
                blk = e[:, b * BLK:(b + 1) * BLK]
                incs.append(ecum + _scan_dot(blk, upto))
                ecum = ecum + jnp.sum(blk, axis=1, keepdims=True)
            dz = e - jnp.exp(zs) * jnp.concatenate(incs, axis=1)
            if own:
                dz = _sb_mask_last(dz, causal)
            dzb = dz.astype(BF16)
            wb = w.astype(BF16)
            for rows, lanes in pairs:
                dk_acc[ks, lanes] += _dot_tn(dzb[rows], qh[rows])
                dv_acc[ks, lanes] += _dot_tn(wb[rows], doh[rows])
            return pre, ecum, dq + jnp.concatenate([_dot(dzb[rows], k[:, lanes]) for rows, lanes in pairs], axis=0)

        def qblock(g, j):
            qs = pl.ds(pl.multiple_of(g * SB_KT + j * BLK, BLK), BLK)
            qh = _sb_stack(q_ref[qs, :] * SCALE, hm)
            doh = _sb_stack(do_ref[qs, :], hm)
            tt = jnp.concatenate([tot_ref[h, qs, :] for h in range(2 * SB_PAIRS)], axis=0)
            c0 = (jnp.zeros((SB_ROWS, 1), F32), jnp.zeros((SB_ROWS, 1), F32), jnp.zeros((SB_ROWS, LANES), F32))
            c = lax.fori_loop(0, g, lambda kt, c: tile(qh, doh, tt, kt * SB_KT, nb, c[0], c[1], c[2], False), c0)
            dq = tile(qh, doh, tt, g * SB_KT, j + 1, c[0], c[1], c[2], True)[2]
            dq_ref[qs, :] = (_sb_unstack(dq, hm) * SCALE).astype(BF16)

        def group(g, _):
            for j in range(nb):
                qblock(g, j)
            return 0

        lax.fori_loop(0, t // SB_KT, group, 0)
        dk_ref[...] = dk_acc[...].astype(BF16)
        dv_ref[...] = dv_acc[...].astype(BF16)

    col_blk = lambda off: pl.BlockSpec((t, wide), lambda g: (0, off + g))
    n_steps = SB_W // wide
    out = jax.ShapeDtypeStruct((t, SB_W), BF16)
    return _call(
        body, name="sb_bwd", grid=(n_steps,),
        in_specs=[col_blk(0), col_blk(n_steps), col_blk(2 * n_steps), col_blk(0),
                  pl.BlockSpec((2 * SB_PAIRS, t, 1), lambda g: (g, 0, 0))],
        out_specs=[col_blk(0), col_blk(0), col_blk(0)], out_shape=[out, out, out],
        scratch_shapes=[pltpu.VMEM((t, wide), F32), pltpu.VMEM((t, wide), F32)],
        compiler_params=_params(1))(proj, proj, proj, d_o, tot)


def _bucket_table():
    a = np.arange(BLK)[:, None]
    c = np.arange(2 * BLK)[None, :]
    dist = np.maximum(BLK + a - c, 0)
    max_exact = N_BUCKETS // 2
    dd = np.maximum(dist, 1).astype(np.float32)
    large = max_exact + (np.log(dd / max_exact) / math.log(MAX_DISTANCE / max_exact)
                         * (N_BUCKETS - max_exact)).astype(np.int32)
    large = np.minimum(large, N_BUCKETS - 1)
    return np.where(dist < max_exact, dist, large).astype(np.int32)


SWA_H = 8


def _swa_band_masks():
    row = lax.broadcasted_iota(jnp.int32, (SWA_H * BLK, 2 * BLK), 0) & (BLK - 1)
    col = lax.broadcasted_iota(jnp.int32, (SWA_H * BLK, 2 * BLK), 1)
    own = lax.broadcasted_iota(jnp.int32, (SWA_H * BLK, BLK), 1) <= (
        lax.broadcasted_iota(jnp.int32, (SWA_H * BLK, BLK), 0) & (BLK - 1))
    return (col > row) & ((col < BLK) | (col - BLK <= row)), own


def _swa_stack(ref, qs, hm, scale):
    parts = []
    for hq in range(SWA_H):
        kvh = hq // SWA_G
        x = ref[qs, (hq // 2) * LANES:(hq // 2 + 1) * LANES].astype(F32)
        if hq % 2 != kvh:
            x = pltpu.roll(x, HEAD_DIM, 1)
        parts.append(jnp.where(hm[kvh], x * scale, 0.0).astype(BF16))
    return jnp.concatenate(parts, axis=0)


def _swa_unstack(x8, hm):
    heads = []
    for hq in range(SWA_H):
        x = x8[hq * BLK:(hq + 1) * BLK]
        heads.append(pltpu.roll(x, HEAD_DIM, 1) if hq % 2 != hq // SWA_G else x)
    return [jnp.where(hm[0], heads[2 * p], heads[2 * p + 1]) for p in range(SWA_H // 2)]


def _swa_scores(q8, kb, bias_ref, mask, cols):
    bias8 = jnp.concatenate([bias_ref[hq, :, cols] for hq in range(SWA_H)], axis=0)
    return jnp.where(mask, _dot_nt(q8, kb) + bias8, NEG_INF)


def _swa_sinks(sink_ref):
    return jnp.concatenate([jnp.broadcast_to(sink_ref[hq:hq + 1, 0:1], (BLK, 1)) for hq in range(SWA_H)], axis=0)


def _swa_fwd(proj, bias, sinks_b):
    t = proj.shape[0]
    nq = t // BLK

    def body(q_ref, k_ref, v_ref, bias_ref, sink_ref, o_ref, lse_ref):
        hm = _head_masks()
        band, own = _swa_band_masks()

        def qblock(i, prev):
            qs = pl.ds(pl.multiple_of(i * BLK, BLK), BLK)
            if prev:
                ks, mask, cols = pl.ds(pl.multiple_of((i - 1) * BLK, BLK), 2 * BLK), band, slice(None)
            else:
                ks, mask, cols = qs, own, slice(BLK, None)
            q8 = _swa_stack(q_ref, qs, hm, SCALE)
            sink8 = _swa_sinks(sink_ref)
            s = _swa_scores(q8, k_ref[ks, :], bias_ref, mask, cols)
            m = jnp.maximum(jnp.max(s, axis=1, keepdims=True), sink8)
            p = jnp.exp(s - m)
            den = jnp.sum(p, axis=1, keepdims=True) + jnp.exp(sink8 - m)
            o8 = _dot((p * (1.0 / den)).astype(BF16), v_ref[ks, :])
            lse8 = m + jnp.log(den)
            for hq in range(SWA_H):
                lse_ref[hq, qs, :] = lse8[hq * BLK:(hq + 1) * BLK]
            for pp, o in enumerate(_swa_unstack(o8, hm)):
                o_ref[qs, pp * LANES:(pp + 1) * LANES] = o

        qblock(0, False)

        def step(i, _):
            qblock(i, True)
            return 0

        lax.fori_loop(1, nq, step, 0)

    return _call(
        body, name="swa_fwd", grid=(1,),
        in_specs=[pl.BlockSpec((t, SWA_W), lambda i: (0, 3)), pl.BlockSpec((t, KV_W), lambda i: (0, 16)),
                  pl.BlockSpec((t, KV_W), lambda i: (0, 17)), pl.BlockSpec((8, BLK, 2 * BLK), lambda i: (0, 0, 0)),
                  pl.BlockSpec((8, LANES), lambda i: (0, 0))],
        out_specs=[pl.BlockSpec((t, SWA_W), lambda i: (0, 0)), pl.BlockSpec((8, t, 1), lambda i: (0, 0, 0))],
        out_shape=[jax.ShapeDtypeStruct((t, SWA_W), F32), jax.ShapeDtypeStruct((8, t, 1), F32)],
        compiler_params=_params(1))(proj, proj, proj, bias, sinks_b)


def _swa_bwd(proj, d_o, lse, bias, sinks_b, dbias_in):
    t = proj.shape[0]
    nq = t // BLK

    def body(q_ref, k_ref, v_ref, do_ref, lse_ref, bias_ref, sink_ref, dbi_ref,
             dq_ref, dk_ref, dv_ref, dsink_ref, dbias_ref, dk_acc, dv_acc):
        hm = _head_masks()
        band, own = _swa_band_masks()
        dk_acc[...] = jnp.zeros_like(dk_acc)
        dv_acc[...] = jnp.zeros_like(dv_acc)
        dbias_ref[...] = dbi_ref[...]

        def qblock(i, prev, dsink8):
            qs = pl.ds(pl.multiple_of(i * BLK, BLK), BLK)
            if prev:
                ks, mask, cols = pl.ds(pl.multiple_of((i - 1) * BLK, BLK), 2 * BLK), band, slice(None)
            else:
                ks, mask, cols = qs, own, slice(BLK, None)
            q8 = _swa_stack(q_ref, qs, hm, SCALE)
            do8 = _swa_stack(do_ref, qs, hm, 1.0)
            sink8 = _swa_sinks(sink_ref)
            lse8 = jnp.concatenate([lse_ref[hq, qs, :] for hq in range(SWA_H)], axis=0)
            kb = k_ref[ks, :]
            p = jnp.exp(_swa_scores(q8, kb, bias_ref, mask, cols) - lse8)
            dp = _dot_nt(do8, v_ref[ks, :])
            delta = jnp.sum(p * dp, axis=1, keepdims=True)
            ds = p * (dp - delta)
            for hq in range(SWA_H):
                dbias_ref[hq, :, cols] += ds[hq * BLK:(hq + 1) * BLK]
            dsb = ds.astype(BF16)
            dk_acc[ks, :] += _dot_tn(dsb, q8)
            dv_acc[ks, :] += _dot_tn(p.astype(BF16), do8)
            for pp, dq in enumerate(_swa_unstack(_dot(dsb, kb) * SCALE, hm)):
                dq_ref[qs, pp * LANES:(pp + 1) * LANES] = dq.astype(BF16)
            return dsink8 - jnp.exp(sink8 - lse8) * delta

        ds0 = qblock(0, False, jnp.zeros((SWA_H * BLK, 1), F32))
        ds8 = lax.fori_loop(1, nq, lambda i, c: qblock(i, True, c), ds0)
        for hq in range(SWA_H):
            dsink_ref[hq:hq + 1, :] = jnp.broadcast_to(
                jnp.sum(ds8[hq * BLK:(hq + 1) * BLK], axis=0, keepdims=True), (1, LANES))

        dk_ref[...] = dk_acc[...].astype(BF16)
        dv_ref[...] = dv_acc[...].astype(BF16)

    full3 = pl.BlockSpec((8, BLK, 2 * BLK), lambda i: (0, 0, 0))
    kv = jax.ShapeDtypeStruct((t, KV_W), BF16)
    return _call(
        body, name="swa_bwd", grid=(1,),
        in_specs=[pl.BlockSpec((t, SWA_W), lambda i: (0, 3)), pl.BlockSpec((t, KV_W), lambda i: (0, 16)),
                  pl.BlockSpec((t, KV_W), lambda i: (0, 17)), pl.BlockSpec((t, SWA_W), lambda i: (0, 1)),
                  pl.BlockSpec((8, t, 1), lambda i: (0, 0, 0)), full3, pl.BlockSpec((8, LANES), lambda i: (0, 0)),
                  full3],
        out_specs=[pl.BlockSpec((t, SWA_W), lambda i: (0, 0)), pl.BlockSpec((t, KV_W), lambda i: (0, 0)),
                   pl.BlockSpec((t, KV_W), lambda i: (0, 0)), pl.BlockSpec((8, LANES), lambda i: (0, 0)), full3],
        out_shape=[jax.ShapeDtypeStruct((t, SWA_W), BF16), kv, kv, jax.ShapeDtypeStruct((8, LANES), F32),
                   jax.ShapeDtypeStruct((8, BLK, 2 * BLK), F32)],
        scratch_shapes=[pltpu.VMEM((t, KV_W), F32), pltpu.VMEM((t, KV_W), F32)],
        compiler_params=_params(1))(proj, proj, proj, d_o, lse, bias, sinks_b, dbias_in)


def _concat_cols(parts):
    t = parts[0].shape[0]
    widths = [a.shape[1] for a in parts]

    def body(*refs):
        refs[-1][...] = jnp.concatenate([r[...] for r in refs[:-1]], axis=1)

    return _call(
        body, name="concat_cols", grid=(t // TM,),
        in_specs=[pl.BlockSpec((TM, w), lambda i: (i, 0)) for w in widths],
        out_specs=pl.BlockSpec((TM, sum(widths)), lambda i: (i, 0)),
        out_shape=jax.ShapeDtypeStruct((t, sum(widths)), parts[0].dtype), compiler_params=_params(1))(*parts)


def _bias_table(rel_bias, buckets):
    def body(rb_ref, b_ref, o_ref):
        bk = b_ref[...]
        for h in range(8):
            acc = jnp.zeros((BLK, 2 * BLK), F32)
            for b in range(N_BUCKETS):
                acc = jnp.where(bk == b, rb_ref[b, h], acc)
            o_ref[h] = acc

    return _call(
        body, name="bias_table", grid=(1,),
        in_specs=[pl.BlockSpec(memory_space=pltpu.SMEM), pl.BlockSpec((BLK, 2 * BLK), lambda i: (0, 0))],
        out_specs=pl.BlockSpec((8, BLK, 2 * BLK), lambda i: (0, 0, 0)),
        out_shape=jax.ShapeDtypeStruct((8, BLK, 2 * BLK), F32), compiler_params=_params(1))(rel_bias, buckets)


def _bias_grad(dbias, buckets):
    def body(d_ref, b_ref, o_ref):
        lane = lax.broadcasted_iota(jnp.int32, (1, LANES), 1)
        bk = b_ref[...]
        for h in range(8):
            d = d_ref[h]
            acc = jnp.zeros((1, LANES), F32)
            for b in range(N_BUCKETS):
                s = jnp.sum(jnp.sum(jnp.where(bk == b, d, 0.0), axis=0, keepdims=True), axis=1, keepdims=True)
                acc = acc + jnp.where(lane == b, s, 0.0)
            o_ref[h:h + 1, :] = acc

    return _call(
        body, name="bias_grad", grid=(1,),
        in_specs=[pl.BlockSpec((8, BLK, 2 * BLK), lambda i: (0, 0, 0)), pl.BlockSpec((BLK, 2 * BLK), lambda i: (0, 0))],
        out_specs=pl.BlockSpec((8, LANES), lambda i: (0, 0)),
        out_shape=jax.ShapeDtypeStruct((8, LANES), F32), compiler_params=_params(1))(dbias, buckets)


def _row(a):
    return a.reshape(1, -1)


def _fwd_ffn1_gu(h, n1, w):
    s = {"h0": h, "n1": n1}
    s["gu1"], s["act1"] = _ffn_gu(n1, w["ffn1_gu"])
    return s


def _fwd_ffn1_down(s, w, small, l):
    s["h1"], s["nm"] = _down_res(s["act1"], w["ffn1_down"], s["h0"], _row(small["norm_mix"][l]))


def _fwd_ffn1(h, n1, w, small, l):
    s = _fwd_ffn1_gu(h, n1, w)
    _fwd_ffn1_down(s, w, small, l)
    return s


def _fwd_proj_sb(s, w):
    s["proj"] = _proj(s["nm"], w["w_in"])
    s["o_sb"], s["tot"] = _sb_fwd(s["proj"])


def _fwd_swa(s, small, l, bias):
    s["sinks_b"] = jnp.broadcast_to(small["sinks"][l][:, None], (8, LANES))
    s["o_sw"], s["lse"] = _swa_fwd(s["proj"], bias, s["sinks_b"])


def _fwd_out_ffn2(s, w, small, l, g_after):
    s["h2"], s["mixed"], s["n2"] = _out_res(
        s["o_sb"], s["o_sw"], _row(small["norm_out_sb"][l]), _row(small["norm_out_swa"][l]), w["w_out"], s["h1"],
        _row(small["norm_ffn2"][l]))
    s["gu2"], s["act2"] = _ffn_gu(s["n2"], w["ffn2_gu"])
    return _down_res(s["act2"], w["ffn2_down"], s["h2"], g_after)


def _bwd_ffn_dact(dh, s, w, which):
    return _ffn_dact(dh[1], w[f"ffn{which}_down"], s[f"gu{which}"])


def _bwd_ffn_rest(dh, dgu, s, w, small, l, which):
    h_in, norm = (s["h0"], "norm_ffn1") if which == 1 else (s["h2"], "norm_ffn2")
    g_down = _wgrad_down(s[f"act{which}"], dh[1])
    g_gu = _wgrad_gu(s[f"n{which}"], dgu)
    dh32, dh16, dg = _ffn_dn(dgu, w[f"ffn{which}_gu"], dh[0], h_in, _row(small[norm][l]))
    return (dh32, dh16), {f"ffn{which}_down": g_down, f"ffn{which}_gu": g_gu}, {norm: dg}


def _bwd_ffn(dh, s, w, small, l, which):
    return _bwd_ffn_rest(dh, _bwd_ffn_dact(dh, s, w, which), s, w, small, l, which)


def _bwd_mix(dh, s, w, small, l, bias, dbias):
    g_out = _wgrad_out(s["mixed"], dh[1])
    d_o, dg_sb, dg_sw = _dmixed(dh[1], w["w_out"], s["o_sb"], s["o_sw"], _row(small["norm_out_sb"][l]),
                                _row(small["norm_out_swa"][l]))
    dq_sb, dk_sb, dv_sb = _sb_bwd(s["proj"], d_o, s["tot"])
    dq_sw, dk_sw, dv_sw, dsink, dbias = _swa_bwd(s["proj"], d_o, s["lse"], bias, s["sinks_b"], dbias)
    dproj = _concat_cols([dq_sb, dk_sb, dv_sb, dq_sw, dk_sw, dv_sw])
    g_in = _wgrad_in(s["nm"], dproj)
    dh32, dh16, dg_mix = _mix_dn(dproj, w["w_in"], dh[0], s["h1"], _row(small["norm_mix"][l]))
    gs = {"norm_out_sb": dg_sb, "norm_out_swa": dg_sw, "sinks": dsink[:, 0], "norm_mix": dg_mix}
    return (dh32, dh16), {"w_out": g_out, "w_in": g_in}, gs, dbias


def _place():
    x, y, c = lax.axis_index("x"), lax.axis_index("y"), lax.axis_index("c")
    return x, y, c, 2 * x + y


def _chip_core(k, c):
    return (k // 2, k % 2, c)


def _rows_per_block(rows, cols, copies):
    best = 16
    for tr in range(16, rows + 1, 16):
        if rows % tr == 0 and copies * tr * cols * 4 <= SLAB_BLOCK_BYTES:
            best = tr
    assert rows % best == 0
    return best


def _place_own(w, l, me1):
    _, rows, cols = w.shape
    tr = _rows_per_block(rows // 2, cols, 1)
    per_half = rows // 2 // tr

    def body(me_ref, w_ref, o_ref):
        o_ref[...] = w_ref[...].astype(BF16)

    return _call(
        body, name="place_own",
        num_scalar_prefetch=1, grid=(rows // tr,),
        in_specs=[pl.BlockSpec((None, tr, cols), lambda r, me: (l, r, 0))],
        out_specs=pl.BlockSpec((None, None, tr, cols), lambda r, me: (me[0], r // per_half, r % per_half, 0)),
        out_shape=jax.ShapeDtypeStruct((N_CHIPS, 2, rows // 2, cols), BF16), compiler_params=_params(1))(me1, w)


def _plan_gather_ici(bufs):
    _, _, c, me = _place()
    return [(b.at[me, c], b.at[me, c], b.at[(me + 3 - j) % N_CHIPS, c], _chip_core((me + 1 + j) % N_CHIPS, c))
            for b in bufs for j in range(3)]


def _plan_gather_d2d(bufs):
    x, y, c, me = _place()
    return [(b.at[(me + 3 - j) % N_CHIPS, c], b.at[(me + 3 - j) % N_CHIPS, c], b.at[(me + 3 - j) % N_CHIPS, 1 - c],
             (x, y, 1 - c)) for b in bufs for j in range(3)]


def _plan_grad_sibling(bufs):
    x, y, c, _ = _place()
    n = len(bufs) // 2
    return [(g.at[:, 1 - c], z, z, (x, y, 1 - c)) for g, z in zip(bufs[:n], bufs[n:])]


def _plan_grad_chips(bufs):
    _, _, c, me = _place()
    n = len(bufs) // 2
    return [(p.at[j], z.at[j], z.at[j], _chip_core((me + 1 + j) % N_CHIPS, c))
            for p, z in zip(bufs[:n], bufs[n:]) for j in range(3)]


def _plan_grad_halves(bufs):
    x, y, c, _ = _place()
    return [(b.at[c], b.at[c], b.at[1 - c], (x, y, 1 - c)) for b in bufs]


def _remote(src, dst, send_sem, recv_sem, to):
    return pltpu.make_async_remote_copy(src_ref=src, dst_ref=dst, send_sem=send_sem, recv_sem=recv_sem,
                                        device_id=to, device_id_type=MESH)


def _exchange_start_groups(name, plan, groups):
    sizes = [len(g) for g, _ in groups]
    bufs = [a for g, _ in groups for a in g]
    n, n_groups = len(bufs), len(groups)

    def body(*refs):
        ins, sems, token = refs[:n], refs[n:n + 2 * n_groups], refs[-1]
        at = 0
        for k, size in enumerate(sizes):
            for i, (src, dst, _, to) in enumerate(plan(ins[at:at + size])):
                _remote(src, dst, sems[2 * k].at[i], sems[2 * k + 1].at[i], to).start()
            at += size
        token[...] = jnp.zeros_like(token)

    sem_shapes = [pltpu.SemaphoreType.DMA((n_copies,)) for _, n_copies in groups for _ in range(2)]
    out = _call(
        body, name=name,
        out_shape=(*sem_shapes, *[pltpu.HBM(a.shape, a.dtype) for a in bufs], jax.ShapeDtypeStruct((8, LANES), F32)),
        in_specs=[HBM] * n,
        out_specs=(*[SEM] * (2 * n_groups), *[HBM] * n, pl.BlockSpec(memory_space=pltpu.VMEM)),
        input_output_aliases={t: 2 * n_groups + t for t in range(n)}, hbm_args=n,
        compiler_params=pltpu.CompilerParams(has_side_effects=EFFECT),
    )(*bufs)
    flights, at = [], 2 * n_groups
    for k, size in enumerate(sizes):
        flights.append(((out[2 * k], out[2 * k + 1]), list(out[at:at + size])))
        at += size
    return flights


def _exchange_start(name, plan, bufs, n_copies):
    return _exchange_start_groups(name, plan, [(bufs, n_copies)])[0]


def _exchange_wait(name, plan, bufs, sems):
    n = len(bufs)

    def body(*refs):
        ins = refs[:n]
        ssem, rsem = refs[n], refs[n + 1]
        for i, (src, dst, land, to) in enumerate(plan(ins)):
            _remote(src, dst, ssem.at[i], rsem.at[i], to).wait_send()
            _remote(land, land, ssem.at[i], rsem.at[i], to).wait_recv()

    return list(_call(
        body, name=name, out_shape=[pltpu.HBM(a.shape, a.dtype) for a in bufs],
        in_specs=[HBM] * n + [SEM, SEM], out_specs=[HBM] * n,
        input_output_aliases={t: t for t in range(n)},
        compiler_params=pltpu.CompilerParams(has_side_effects=EFFECT),
    )(*bufs, sems[0], sems[1]))


def _exchange_pass(name, done, plan, bufs, sems, n_copies):
    n = len(bufs)

    def body(*refs):
        ins = refs[:n]
        old_s, old_r, ssem, rsem = refs[n], refs[n + 1], refs[n + 2], refs[n + 3]
        token = refs[-1]
        for i, (src, dst, land, to) in enumerate(done(ins)):
            _remote(src, dst, old_s.at[i], old_r.at[i], to).wait_send()
            _remote(land, land, old_s.at[i], old_r.at[i], to).wait_recv()
        for i, (src, dst, _, to) in enumerate(plan(ins)):
            _remote(src, dst, ssem.at[i], rsem.at[i], to).start()
        token[...] = jnp.zeros_like(token)

    out = _call(
        body, name=name,
        out_shape=(pltpu.SemaphoreType.DMA((n_copies,)), pltpu.SemaphoreType.DMA((n_copies,)),
                   *[pltpu.HBM(a.shape, a.dtype) for a in bufs], jax.ShapeDtypeStruct((8, LANES), F32)),
        in_specs=[HBM] * n + [SEM, SEM], out_specs=(SEM, SEM, *[HBM] * n, pl.BlockSpec(memory_space=pltpu.VMEM)),
        input_output_aliases={t: 2 + t for t in range(n)},
        compiler_params=pltpu.CompilerParams(has_side_effects=EFFECT),
    )(*bufs, sems[0], sems[1])
    return (out[0], out[1]), list(out[2:2 + n])


def _chip_sum(g, xbuf, cm):
    _, _, r2, cols = g.shape
    tr = _rows_per_block(r2, cols, 1)

    def body(cm_ref, g_ref, x_ref, o_ref):
        o_ref[...] = (g_ref[...] + x_ref[...]).astype(BF16)

    return _call(
        body, name="grad_chip_sum",
        num_scalar_prefetch=1, grid=(3, r2 // tr),
        in_specs=[pl.BlockSpec((None, None, tr, cols), lambda j, r, cm: ((cm[1] + 1 + j) % N_CHIPS, cm[0], r, 0)),
                  pl.BlockSpec((None, tr, cols), lambda j, r, cm: ((cm[1] + 1 + j) % N_CHIPS, r, 0))],
        out_specs=pl.BlockSpec((None, tr, cols), lambda j, r, cm: (j, r, 0)),
        out_shape=jax.ShapeDtypeStruct((3, r2, cols), BF16), compiler_params=_params(2))(cm, g, xbuf)


def _total_sum(g, xbuf, rbuf, cm):
    _, _, r2, cols = g.shape
    tr = _rows_per_block(r2, cols, 3)

    def body(cm_ref, g_ref, x_ref, r_ref, o_ref):
        acc = g_ref[...] + x_ref[...]
        for j in range(3):
            acc = acc + r_ref[j].astype(F32)
        o_ref[...] = acc

    return _call(
        body, name="grad_total_sum",
        num_scalar_prefetch=1, grid=(r2 // tr,),
        in_specs=[pl.BlockSpec((None, None, tr, cols), lambda r, cm: (cm[1], cm[0], r, 0)),
                  pl.BlockSpec((None, tr, cols), lambda r, cm: (cm[1], r, 0)),
                  pl.BlockSpec((3, tr, cols), lambda r, cm: (0, r, 0))],
        out_specs=pl.BlockSpec((None, tr, cols), lambda r, cm: (cm[0], r, 0)),
        out_shape=jax.ShapeDtypeStruct((2, r2, cols), F32), compiler_params=_params(1))(cm, g, xbuf, rbuf)


def _small_allreduce(v):
    rows = v.shape[0]
    n_dev = 2 * N_CHIPS

    def body(v_ref, o_ref, buf, ssem, rsem):
        x, y, c, _ = _place()
        me = 4 * x + 2 * y + c
        buf[me] = v_ref[...]

        def copy(d, slot, to):
            return _remote(v_ref, buf.at[slot], ssem.at[d - 1], rsem.at[d - 1], (to // 4, (to // 2) % 2, to % 2))

        cps = [copy(d, me, (me + d) % n_dev) for d in range(1, n_dev)]
        for cp in cps:
            cp.start()
        for d in range(1, n_dev):
            copy(d, (me + n_dev - d) % n_dev, me).wait_recv()
        for cp in cps:
            cp.wait_send()
        acc = buf[0]
        for i in range(1, n_dev):
            acc = acc + buf[i]
        o_ref[...] = acc

    vm = pl.BlockSpec(memory_space=pltpu.VMEM)
    return _call(
        body, name="small_allreduce", in_specs=[vm], out_specs=vm,
        out_shape=jax.ShapeDtypeStruct(v.shape, F32),
        scratch_shapes=[pltpu.VMEM((n_dev, rows, LANES), F32), pltpu.SemaphoreType.DMA((n_dev - 1,)),
                        pltpu.SemaphoreType.DMA((n_dev - 1,))],
        compiler_params=pltpu.CompilerParams(vmem_limit_bytes=V7X_VMEM_LIMIT))(v)


def _adamw_math(w, g, m, v):
    m2 = ADAM_B1 * m + (1.0 - ADAM_B1) * g
    v2 = ADAM_B2 * v + (1.0 - ADAM_B2) * (g * g)
    m_hat = m2 / (1.0 - ADAM_B1 ** ADAM_STEP)
    v_hat = v2 / (1.0 - ADAM_B2 ** ADAM_STEP)
    return -ADAM_LR * (m_hat / (jnp.sqrt(v_hat) + ADAM_EPS) + ADAM_WD * w), m2, v2


def _adamw_layer(w, g, m, v, l, prev):
    _, rows, cols = w.shape
    tr = rows
    for cand in range(8, rows + 1, 8):
        if rows % cand == 0 and cand * cols * 4 <= ADAMW_BLOCK_BYTES:
            tr = cand

    def body(w_ref, g_ref, m_ref, v_ref, *outs):
        go_ref, d_ref, m2_ref, v2_ref = outs[-4:]
        g = g_ref[...]
        go_ref[...] = g
        d_ref[...], m2_ref[...], v2_ref[...] = _adamw_math(w_ref[...], g, m_ref[...], v_ref[...])

    stack = pl.BlockSpec((None, tr, cols), lambda i: (l, i, 0))
    ins, specs, alias = [w, g, m, v], [stack, pl.BlockSpec((tr, cols), lambda i: (i, 0)), stack, stack], {}
    if prev is not None:
        ins += list(prev)
        specs += [ANY] * 4
        alias = {4 + i: i for i in range(4)}
    return _call(
        body, name="adamw", grid=(rows // tr,), in_specs=specs, out_specs=[stack] * 4,
        out_shape=[jax.ShapeDtypeStruct(w.shape, F32)] * 4, input_output_aliases=alias,
        compiler_params=_params(1))(*ins)


def _adamw_small(w, g, m, v):
    def body(w_ref, g_ref, m_ref, v_ref, d_ref, m2_ref, v2_ref):
        d_ref[...], m2_ref[...], v2_ref[...] = _adamw_math(w_ref[...], g_ref[...], m_ref[...], v_ref[...])

    spec = pl.BlockSpec(w.shape, lambda i: (0, 0))
    return _call(
        body, name="adamw_small", grid=(1,), in_specs=[spec] * 4, out_specs=[spec] * 3,
        out_shape=[jax.ShapeDtypeStruct(w.shape, F32)] * 3, compiler_params=_params(1))(w, g, m, v)


SMALL = ("norm_ffn1", "norm_mix", "sinks", "norm_out_sb", "norm_out_swa", "norm_ffn2", "rel_bias", "norm_final")
BIG = ("ffn1_gu", "ffn1_down", "w_in", "w_out", "ffn2_gu", "ffn2_down")


def _pack(parts):
    flat, n = [], 0
    for a in parts:
        a = a.reshape(-1).astype(F32)
        gap = -a.shape[0] % LANES
        flat += [a] + ([jnp.zeros((gap,), F32)] if gap else [])
        n += a.shape[0] + gap
    tail = -(n // LANES) % 8 * LANES
    return jnp.concatenate(flat + ([jnp.zeros((tail,), F32)] if tail else [])).reshape(-1, LANES)


def _unpack(packed, like):
    out, r = [], 0
    for a in like:
        n = math.prod(a.shape)
        nr = -(-n // LANES)
        out.append(packed[r:r + nr].reshape(-1)[:n].reshape(a.shape))
        r += nr
    return out


def _halved(a):
    k, r, cols = a.shape
    return a.reshape(k, 2, r // 2, cols)


def _weight_view(k, buf):
    full = buf.reshape(N_CHIPS, buf.shape[2] * 2, buf.shape[3])
    return full if k.endswith("_gu") else full.reshape(-1, D_MODEL)


def _grad_stack(k, g):
    if not k.endswith("_gu"):
        g = g.reshape(N_CHIPS, g.shape[0] // N_CHIPS, D_MODEL)
    return _halved(g)


def _empty_like_hbm(shape, dtype):
    return pltpu.with_memory_space_constraint(lax.empty(shape, dtype), pltpu.HBM)


def kernel(x, norm_ffn1, w_ffn1_gu, w_ffn1_down, norm_mix, w_in, sinks, norm_out_sb, norm_out_swa, w_out, norm_ffn2, w_ffn2_gu, w_ffn2_down, rel_bias, norm_final, loss_target, m_norm_ffn1, m_w_ffn1_gu, m_w_ffn1_down, m_norm_mix, m_w_in, m_sinks, m_norm_out_sb, m_norm_out_swa, m_w_out, m_norm_ffn2, m_w_ffn2_gu, m_w_ffn2_down, m_rel_bias, m_norm_final, v_norm_ffn1, v_w_ffn1_gu, v_w_ffn1_down, v_norm_mix, v_w_in, v_sinks, v_norm_out_sb, v_norm_out_swa, v_w_out, v_norm_ffn2, v_w_ffn2_gu, v_w_ffn2_down, v_rel_bias, v_norm_final):
    big_w = dict(ffn1_gu=w_ffn1_gu, ffn1_down=w_ffn1_down, w_in=w_in, w_out=w_out, ffn2_gu=w_ffn2_gu, ffn2_down=w_ffn2_down)
    big_m = dict(ffn1_gu=m_w_ffn1_gu, ffn1_down=m_w_ffn1_down, w_in=m_w_in, w_out=m_w_out, ffn2_gu=m_w_ffn2_gu, ffn2_down=m_w_ffn2_down)
    big_v = dict(ffn1_gu=v_w_ffn1_gu, ffn1_down=v_w_ffn1_down, w_in=v_w_in, w_out=v_w_out, ffn2_gu=v_w_ffn2_gu, ffn2_down=v_w_ffn2_down)
    small = dict(norm_ffn1=norm_ffn1, norm_mix=norm_mix, sinks=sinks, norm_out_sb=norm_out_sb, norm_out_swa=norm_out_swa,
                 norm_ffn2=norm_ffn2, rel_bias=rel_bias, norm_final=norm_final)
    small_m = dict(norm_ffn1=m_norm_ffn1, norm_mix=m_norm_mix, sinks=m_sinks, norm_out_sb=m_norm_out_sb,
                   norm_out_swa=m_norm_out_swa, norm_ffn2=m_norm_ffn2, rel_bias=m_rel_bias, norm_final=m_norm_final)
    small_v = dict(norm_ffn1=v_norm_ffn1, norm_mix=v_norm_mix, sinks=v_sinks, norm_out_sb=v_norm_out_sb,
                   norm_out_swa=v_norm_out_swa, norm_ffn2=v_norm_ffn2, rel_bias=v_rel_bias, norm_final=v_norm_final)
    for dct in (big_w, big_m, big_v):
        dct["w_in"] = jnp.swapaxes(dct["w_in"], 1, 2)
    _PREVIOUS[0] = None
    _, _, c, me = _place()
    cm = jnp.stack([c, me]).astype(jnp.int32)
    buckets = jnp.asarray(_bucket_table())
    ffn1, mix_in, rest = ("ffn1_gu", "ffn1_down"), ("w_in",), ("w_out", "ffn2_gu", "ffn2_down")

    def place(l, keys):
        return [_place_own(big_w[k], l, cm[1:]) for k in keys]

    def views(keys, bufs):
        return {k: _weight_view(k, b) for k, b in zip(keys, bufs)}

    def gather_start(tag, bufs):
        return _exchange_start(f"gather{tag}_ici_start", _plan_gather_ici, bufs, 3 * len(bufs))

    def gather_pass(tag, flight):
        return _exchange_pass(f"gather{tag}_pass", _plan_gather_ici, _plan_gather_d2d, flight[1], flight[0],
                              3 * len(flight[1]))

    def gather_done(tag, keys, flight):
        return views(keys, _exchange_wait(f"gather{tag}_d2d_wait", _plan_gather_d2d, flight[1], flight[0]))

    fly_gu0 = gather_start("0a", place(0, ffn1[:1]))
    fly_down0 = gather_start("0a2", place(0, ffn1[1:]))
    fly_in0 = gather_start("0b", place(0, mix_in))
    later = [place(l, keys) for l in range(DEPTH) for keys in ((rest,) if l == 0 else (ffn1, mix_in, rest))]
    fly_rest0, fly_ffn1, fly_in1, fly_rest1 = _exchange_start_groups(
        "gather_later_ici_start", _plan_gather_ici, [(bufs, 3 * len(bufs)) for bufs in later])
    bias = _bias_table(rel_bias, buckets)
    n1 = _norm_cast(x[0], _row(norm_ffn1[0]))
    w0 = gather_done("0a", ffn1[:1], gather_pass("0a", fly_gu0))

    s0 = _fwd_ffn1_gu(x[0], n1, w0)
    w0.update(gather_done("0a2", ffn1[1:], gather_pass("0a2", fly_down0)))
    fly_in0 = gather_pass("0b", fly_in0)
    _fwd_ffn1_down(s0, w0, small, 0)
    w0.update(gather_done("0b", mix_in, fly_in0))
    _fwd_proj_sb(s0, w0)
    fly_rest0 = gather_pass("0c", fly_rest0)
    _fwd_swa(s0, small, 0, bias)
    w0.update(gather_done("0c", rest, fly_rest0))
    fly_ffn1 = gather_pass("1a", fly_ffn1)
    h, n1 = _fwd_out_ffn2(s0, w0, small, 0, _row(norm_ffn1[1]))
    w1 = gather_done("1a", ffn1, fly_ffn1)
    fly_in1 = gather_pass("1b", fly_in1)
    s1 = _fwd_ffn1(h, n1, w1, small, 1)
    w1.update(gather_done("1b", mix_in, fly_in1))
    _fwd_proj_sb(s1, w1)
    fly_rest1 = gather_pass("1c", fly_rest1)
    _fwd_swa(s1, small, 1, bias)
    w1.update(gather_done("1c", rest, fly_rest1))
    h, _ = _fwd_out_ffn2(s1, w1, small, 1, _row(norm_final))
    dh32, dh16, dg_final, loss_row = _loss_head(h, _row(norm_final), loss_target[0])
    dh = (dh32, dh16)

    def landing(stacks, lead, dtype):
        return [_empty_like_hbm((lead,) + a.shape[2:], dtype) for a in stacks]

    def reduce_begin(tag, keys, gw):
        stacks = [_grad_stack(k, gw[k]) for k in keys]
        flight = _exchange_start(f"grad{tag}_sibling_start", _plan_grad_sibling,
                                 stacks + landing(stacks, N_CHIPS, F32), len(keys))
        return dict(tag=tag, keys=keys, stacks=stacks, flight=flight)

    def reduce_chips(st):
        n, (sems, bufs) = len(st["keys"]), st["flight"]
        bufs = _exchange_wait(f"grad{st['tag']}_sibling_wait", _plan_grad_sibling, bufs, sems)
        st["own"] = list(zip(bufs[:n], bufs[n:]))
        st["flight"] = _exchange_start(f"grad{st['tag']}_chips_start", _plan_grad_chips,
                                       [_chip_sum(g, z, cm) for g, z in st["own"]] + landing(st["stacks"], 3, BF16),
                                       3 * n)

    def reduce_halves(st):
        n, (sems, bufs) = len(st["keys"]), st["flight"]
        bufs = _exchange_wait(f"grad{st['tag']}_chips_wait", _plan_grad_chips, bufs, sems)
        halves = [_total_sum(g, x, z, cm) for (g, x), z in zip(st["own"], bufs[n:])]
        st["flight"] = _exchange_start(f"grad{st['tag']}_halves_start", _plan_grad_halves, halves, n)

    def reduce_end(st):
        sems, bufs = st["flight"]
        bufs = _exchange_wait(f"grad{st['tag']}_halves_wait", _plan_grad_halves, bufs, sems)
        return {k: b.reshape(big_w[k].shape[1:]) for k, b in zip(st["keys"], bufs)}

    def adamw(reduced, l, prev):
        return {k: _adamw_layer(big_w[k], g, big_m[k], big_v[k], l, None if prev is None else prev[k])
                for k, g in reduced.items()}

    gsm = [dict() for _ in range(DEPTH)]
    dbias = jnp.zeros((8, BLK, 2 * BLK), F32)
    dh, gw1, gs = _bwd_ffn(dh, s1, w1, small, 1, 2)
    gsm[1].update(gs)
    dh, gw, gs, dbias = _bwd_mix(dh, s1, w1, small, 1, bias, dbias)
    gw1.update(gw)
    gsm[1].update(gs)
    dh, gw, gs = _bwd_ffn(dh, s1, w1, small, 1, 1)
    gw1.update(gw)
    gsm[1].update(gs)

    red1 = reduce_begin("1", BIG, gw1)
    dh, gw0, gs = _bwd_ffn(dh, s0, w0, small, 0, 2)
    gsm[0].update(gs)
    reduce_chips(red1)
    dh, gw, gs, dbias = _bwd_mix(dh, s0, w0, small, 0, bias, dbias)
    gw0.update(gw)
    gsm[0].update(gs)
    red0a = reduce_begin("0a", ("ffn2_gu", "ffn2_down", "w_out", "w_in"), gw0)
    reduce_halves(red1)
    dgu = _bwd_ffn_dact(dh, s0, w0, 1)
    reduce_chips(red0a)
    dh, gw, gs = _bwd_ffn_rest(dh, dgu, s0, w0, small, 0, 1)
    gsm[0].update(gs)
    red0b = reduce_begin("0b", ffn1, gw)
    reduced1 = reduce_end(red1)
    stacks = adamw({k: reduced1[k] for k in ffn1}, 1, None)

    gsmall = {k: jnp.stack([gsm[l][k].reshape(-1) for l in range(DEPTH)]) for k in gsm[0]}
    gsmall["rel_bias"] = jnp.transpose(_bias_grad(dbias, buckets)[:, :N_BUCKETS])
    gsmall["norm_final"] = dg_final.reshape(-1)
    small_like = [small[k] for k in SMALL]
    pk = lambda dct: _pack([dct[k] for k in SMALL])
    red = _small_allreduce(_pack([gsmall[k] for k in SMALL] + [loss_row[0, :1]]))
    gs = _unpack(red, small_like + [loss_row[0, :1]])
    loss = gs[-1][0]
    gs = dict(zip(SMALL, gs[:-1]))

    ffn2 = ("ffn2_gu", "ffn2_down")
    reduce_chips(red0b)
    stacks.update(adamw({k: reduced1[k] for k in ("w_in", "w_out")}, 1, None))
    reduce_halves(red0a)
    stacks.update(adamw({k: reduced1[k] for k in ffn2}, 1, None))
    dlt, m2, v2 = _adamw_small(pk(small), pk(gs), pk(small_m), pk(small_v))
    reduced0a = reduce_end(red0a)
    stacks.update(adamw({k: reduced0a[k] for k in ffn2}, 0, stacks))
    reduce_halves(red0b)
    stacks.update(adamw({k: reduced0a[k] for k in ("w_in", "w_out")}, 0, stacks))
    stacks.update(adamw(reduce_end(red0b), 0, stacks))

    out_g, out_d, out_m, out_v = {}, {}, {}, {}
    for k in BIG:
        out_g[k], out_d[k], out_m[k], out_v[k] = [jnp.swapaxes(a, 1, 2) if k == "w_in" else a for a in stacks[k]]
    for dst, packed in ((out_d, dlt), (out_m, m2), (out_v, v2)):
        dst.update(zip(SMALL, _unpack(packed, small_like)))
    out_g.update(gs)

    order = ("norm_ffn1", "ffn1_gu", "ffn1_down", "norm_mix", "w_in", "sinks", "norm_out_sb", "norm_out_swa", "w_out",
             "norm_ffn2", "ffn2_gu", "ffn2_down", "rel_bias", "norm_final")
    return (loss, dh[0].reshape(x.shape), *[out_g[k] for k in order], *[out_d[k] for k in order],
            *[out_m[k] for k in order], *[out_v[k] for k in order])
```

```python
import math

import numpy as np
import jax
import jax.numpy as jnp
from jax import lax
from jax.experimental import pallas as pl
from jax.experimental.pallas import tpu as pltpu

F32 = jnp.float32
BF16 = jnp.bfloat16

D_MODEL = 1024
DEPTH = 2
HEAD_DIM = 64
BLK = 128
N_BUCKETS = 32
MAX_DISTANCE = 128
D_FF = 2816
EPS = 1e-6
NEG_INF = -1e30
SB_W = 512
SWA_W = 512
KV_W = 128
IN_W = 2304
SCALE = HEAD_DIM ** -0.5
N_CHIPS = 4
FS = 2 * D_FF // N_CHIPS
LANES = 128
V7X_VMEM_LIMIT = 56 * 2 ** 20
TM = 512
SLAB_BLOCK_BYTES = 6 * 2 ** 20
ADAMW_BLOCK_BYTES = 2 ** 21
SB_KT = 512
SWA_G = 4

ADAM_LR = 0.001
ADAM_B1 = 0.9
ADAM_B2 = 0.999
ADAM_EPS = 1e-08
ADAM_WD = 0.01
ADAM_STEP = 10

MESH = pl.DeviceIdType.MESH
ANY = pl.BlockSpec(memory_space=pl.ANY)
HBM = pl.BlockSpec(memory_space=pltpu.HBM)
SEM = pl.BlockSpec(memory_space=pltpu.SEMAPHORE)
EFFECT = pltpu.SideEffectType.DATAFLOW_SIDE_EFFECTING


def _params(n_grid):
    return pltpu.CompilerParams(dimension_semantics=("arbitrary",) * n_grid, vmem_limit_bytes=V7X_VMEM_LIMIT)


_PREVIOUS = [None]


def _call(body, *, name, in_specs, out_specs, out_shape, grid=(), num_scalar_prefetch=0, scratch_shapes=(),
          input_output_aliases=None, compiler_params=None, hbm_args=0):
    n_in = len(in_specs)

    def run(*args):
        dep = _PREVIOUS[0]
        if any(dep is a for a in args):
            dep = None
        args = [pltpu.with_memory_space_constraint(a, pltpu.HBM) if i < hbm_args else a for i, a in enumerate(args)]
        specs = list(in_specs) + ([ANY] if dep is not None else [])
        k = num_scalar_prefetch + n_in
        fn = body if dep is None else (lambda *refs: body(*refs[:k], *refs[k + 1:]))
        if num_scalar_prefetch:
            shape = dict(grid_spec=pltpu.PrefetchScalarGridSpec(
                num_scalar_prefetch=num_scalar_prefetch, grid=grid, in_specs=specs, out_specs=out_specs,
                scratch_shapes=scratch_shapes))
        else:
            shape = dict(grid=grid, in_specs=specs, out_specs=out_specs, scratch_shapes=scratch_shapes)
        out = pl.pallas_call(fn, name=name, out_shape=out_shape, input_output_aliases=input_output_aliases or {},
                             compiler_params=compiler_params, **shape)(*args, *([] if dep is None else [dep]))
        _PREVIOUS[0] = jax.tree.leaves(out)[-1]
        return out

    return run


def _dot(a, b):
    return jnp.dot(a, b, preferred_element_type=F32)


def _dot_nt(a, b):
    return lax.dot_general(a, b, (((1,), (1,)), ((), ())), preferred_element_type=F32)


def _dot_tn(a, b):
    return lax.dot_general(a, b, (((0,), (0,)), ((), ())), preferred_element_type=F32)


def _rms_fwd(x, g):
    r = lax.rsqrt(jnp.mean(x * x, axis=-1, keepdims=True) + EPS)
    xh = x * r
    return xh * g, xh, r


def _rms_bwd(dy, xh, r, g):
    u = dy * g
    dx = r * (u - xh * jnp.mean(u * xh, axis=-1, keepdims=True))
    dg = jnp.sum(dy * xh, axis=0, keepdims=True)
    return dx, dg


def _softplus(z):
    neg_abs = lax.bitcast_convert_type(lax.bitcast_convert_type(z, jnp.int32) | jnp.int32(-2 ** 31), F32)
    sp = jnp.maximum(z, 0.0) + jnp.log(1.0 + jnp.exp(neg_abs))
    return sp, z - sp


def _norm_cast(h, g):
    t, w = h.shape

    def body(h_ref, g_ref, n_ref):
        y, _, _ = _rms_fwd(h_ref[...], g_ref[...])
        n_ref[...] = y.astype(BF16)

    return _call(
        body, name="norm_cast", grid=(t // TM,),
        in_specs=[pl.BlockSpec((TM, w), lambda i: (i, 0)), pl.BlockSpec((1, w), lambda i: (0, 0))],
        out_specs=pl.BlockSpec((TM, w), lambda i: (i, 0)),
        out_shape=jax.ShapeDtypeStruct((t, w), BF16), compiler_params=_params(1))(h, g)


def _ffn_gu(n, wgu):
    t, d = n.shape

    def body(n_ref, wg_ref, wu_ref, gu_ref, act_ref):
        x = n_ref[...]
        g = _dot(x, wg_ref[...])
        u = _dot(x, wu_ref[...])
        sig = jax.nn.sigmoid(g)
        silu = g * sig
        gu_ref[0] = (u * (sig + silu * (1.0 - sig))).astype(BF16)
        gu_ref[1] = silu.astype(BF16)
        act_ref[...] = (silu * u).astype(BF16)

    return _call(
        body, name="ffn_gu", grid=(2, t // TM),
        in_specs=[pl.BlockSpec((TM, d), lambda j, i: (i, 0)),
                  pl.BlockSpec((None, d, FS), lambda j, i: (j, 0, 0)),
                  pl.BlockSpec((None, d, FS), lambda j, i: (j + 2, 0, 0))],
        out_specs=[pl.BlockSpec((2, TM, FS), lambda j, i: (0, i, j)), pl.BlockSpec((TM, FS), lambda j, i: (i, j))],
        out_shape=[jax.ShapeDtypeStruct((2, t, D_FF), BF16), jax.ShapeDtypeStruct((t, D_FF), BF16)],
        compiler_params=_params(2))(n, wgu, wgu)


def _down_res(act, wdn, h, g_next):
    t, f = act.shape
    d = h.shape[1]

    def body(a_ref, w_ref, h_ref, g_ref, o_ref, n_ref):
        out = h_ref[...] + 0.5 * _dot(a_ref[...], w_ref[...])
        o_ref[...] = out
        n_ref[...] = _rms_fwd(out, g_ref[...])[0].astype(BF16)

    row = pl.BlockSpec((TM, d), lambda i: (i, 0))
    return _call(
        body, name="down_res", grid=(t // TM,),
        in_specs=[pl.BlockSpec((TM, f), lambda i: (i, 0)), pl.BlockSpec((f, d), lambda i: (0, 0)), row,
                  pl.BlockSpec((1, d), lambda i: (0, 0))],
        out_specs=[row, row],
        out_shape=[jax.ShapeDtypeStruct((t, d), F32), jax.ShapeDtypeStruct((t, d), BF16)],
        compiler_params=_params(1))(act, wdn, h, g_next)


def _proj(n, w_in_t):
    t, d = n.shape
    w = w_in_t.shape[0]

    def body(n_ref, w_ref, o_ref):
        o_ref[...] = _dot_nt(n_ref[...], w_ref[...]).astype(BF16)

    return _call(
        body, name="proj", grid=(t // TM,),
        in_specs=[pl.BlockSpec((TM, d), lambda i: (i, 0)), pl.BlockSpec((w, d), lambda i: (0, 0))],
        out_specs=pl.BlockSpec((TM, w), lambda i: (i, 0)),
        out_shape=jax.ShapeDtypeStruct((t, w), BF16), compiler_params=_params(1))(n, w_in_t)


def _out_res(o_sb, o_sw, g_sb, g_sw, w_out, h, g_next):
    t, d = h.shape

    def body(a_ref, b_ref, ga_ref, gb_ref, w_ref, h_ref, g_ref, o_ref, mix_ref, n_ref):
        ya, _, _ = _rms_fwd(a_ref[...], ga_ref[...])
        yb, _, _ = _rms_fwd(b_ref[...], gb_ref[...])
        mixed = jnp.concatenate([ya.astype(BF16), yb.astype(BF16)], axis=1)
        mix_ref[...] = mixed
        out = h_ref[...] + _dot(mixed, w_ref[...])
        o_ref[...] = out
        n_ref[...] = _rms_fwd(out, g_ref[...])[0].astype(BF16)

    row = pl.BlockSpec((TM, d), lambda i: (i, 0))
    return _call(
        body, name="out_res", grid=(t // TM,),
        in_specs=[pl.BlockSpec((TM, SB_W), lambda i: (i, 0)), pl.BlockSpec((TM, SWA_W), lambda i: (i, 0)),
                  pl.BlockSpec((1, SB_W), lambda i: (0, 0)), pl.BlockSpec((1, SWA_W), lambda i: (0, 0)),
                  pl.BlockSpec((d, d), lambda i: (0, 0)), row, pl.BlockSpec((1, d), lambda i: (0, 0))],
        out_specs=[row, row, row],
        out_shape=[jax.ShapeDtypeStruct((t, d), F32), jax.ShapeDtypeStruct((t, d), BF16),
                   jax.ShapeDtypeStruct((t, d), BF16)],
        compiler_params=_params(1))(o_sb, o_sw, g_sb, g_sw, w_out, h, g_next)


def _loss_head(h, g, tgt):
    t, d = h.shape

    def body(h_ref, g_ref, t_ref, dh_ref, dhb_ref, dg_ref, loss_ref):
        @pl.when(pl.program_id(0) == 0)
        def _():
            dg_ref[...] = jnp.zeros_like(dg_ref)
            loss_ref[...] = jnp.zeros_like(loss_ref)

        gg = g_ref[...]
        y, xh, r = _rms_fwd(h_ref[...], gg)
        err = y - t_ref[...]
        part = 0.5 * jnp.sum(jnp.sum(err * err, axis=1, keepdims=True) / d, axis=0, keepdims=True)
        loss_ref[...] += jnp.broadcast_to(part, loss_ref.shape)
        dx, dg = _rms_bwd(err / d, xh, r, gg)
        dh_ref[...] = dx
        dhb_ref[...] = dx.astype(BF16)
        dg_ref[...] += dg

    row = pl.BlockSpec((TM, d), lambda i: (i, 0))
    return _call(
        body, name="loss_head", grid=(t // TM,),
        in_specs=[row, pl.BlockSpec((1, d), lambda i: (0, 0)), row],
        out_specs=[row, row, pl.BlockSpec((1, d), lambda i: (0, 0)), pl.BlockSpec((1, LANES), lambda i: (0, 0))],
        out_shape=[jax.ShapeDtypeStruct((t, d), F32), jax.ShapeDtypeStruct((t, d), BF16),
                   jax.ShapeDtypeStruct((1, d), F32), jax.ShapeDtypeStruct((1, LANES), F32)],
        compiler_params=_params(1))(h, g, tgt)


def _ffn_dact(dh, wdn, gu):
    t, d = dh.shape
    tm = TM

    def body(dh_ref, w_ref, gu_ref, o_ref):
        da = 0.5 * _dot_nt(dh_ref[...].astype(BF16), w_ref[...])
        o_ref[0] = (da * gu_ref[0].astype(F32)).astype(BF16)
        o_ref[1] = (da * gu_ref[1].astype(F32)).astype(BF16)

    return _call(
        body, name="ffn_dact", grid=(2, t // tm),
        in_specs=[pl.BlockSpec((tm, d), lambda j, i: (i, 0)), pl.BlockSpec((FS, d), lambda j, i: (j, 0)),
                  pl.BlockSpec((2, tm, FS), lambda j, i: (0, i, j))],
        out_specs=pl.BlockSpec((2, tm, FS), lambda j, i: (0, i, j)),
        out_shape=jax.ShapeDtypeStruct((2, t, D_FF), BF16), compiler_params=_params(2))(dh, wdn, gu)


def _dn_norm_bwd(a, a_spec, w, w_spec, nk, dh, h_in, g, w_transposed=False, tm=TM):
    t, d = dh.shape
    mm = _dot if w_transposed else _dot_nt

    def body(a_ref, w_ref, dh_ref, h_ref, g_ref, o_ref, ob_ref, dg_ref, acc_ref):
        i, k = pl.program_id(0), pl.program_id(1)

        if nk > 1:
            @pl.when(k == 0)
            def _():
                acc_ref[...] = mm(a_ref[...], w_ref[...])

            @pl.when((k > 0) & (k < nk - 1))
            def _():
                acc_ref[...] += mm(a_ref[...], w_ref[...])

        @pl.when(k == nk - 1)
        def _():
            gg = g_ref[...]
            dg = jnp.zeros_like(gg)
            for rows in (slice(r, r + TM // 2) for r in range(0, tm, TM // 2)):
                dn = mm(a_ref[rows, :], w_ref[...])
                if nk > 1:
                    dn = dn + acc_ref[rows, :]
                _, xh, r = _rms_fwd(h_ref[rows, :], gg)
                dx, dg_rows = _rms_bwd(dn, xh, r, gg)
                out = dh_ref[rows, :] + dx
                o_ref[rows, :] = out
                ob_ref[rows, :] = out.astype(BF16)
                dg = dg + dg_rows

            @pl.when(i == 0)
            def _():
                dg_ref[...] = dg

            @pl.when(i > 0)
            def _():
                dg_ref[...] += dg

    row = pl.BlockSpec((tm, d), lambda i, k: (i, 0))
    return _call(
        body, name="dn_norm_bwd", grid=(t // tm, nk),
        in_specs=[a_spec, w_spec, row, row, pl.BlockSpec((1, d), lambda i, k: (0, 0))],
        out_specs=[row, row, pl.BlockSpec((1, d), lambda i, k: (0, 0))],
        out_shape=[jax.ShapeDtypeStruct((t, d), F32), jax.ShapeDtypeStruct((t, d), BF16),
                   jax.ShapeDtypeStruct((1, d), F32)],
        scratch_shapes=[pltpu.VMEM((tm, d), F32)], compiler_params=_params(2))(a, w, dh, h_in, g)


def _ffn_dn(dgu, wgu, dh, h_in, g):
    d = dh.shape[1]
    tm = 2 * TM
    return _dn_norm_bwd(
        dgu, pl.BlockSpec((None, tm, FS), lambda i, k: (k // 2, i, k % 2)),
        wgu, pl.BlockSpec((None, d, FS), lambda i, k: (k, 0, 0)), N_CHIPS, dh, h_in, g, tm=tm)


def _mix_dn(dproj, w_in_t, dh, h_in, g):
    d = dh.shape[1]
    w = dproj.shape[1]
    return _dn_norm_bwd(
        dproj, pl.BlockSpec((TM, w), lambda i, k: (i, 0)),
        w_in_t, pl.BlockSpec((w, d), lambda i, k: (0, 0)), 1, dh, h_in, g, w_transposed=True)


def _dmixed(dh, w_out, o_sb, o_sw, g_sb, g_sw):
    t, d = dh.shape

    def body(dh_ref, w_ref, a_ref, b_ref, ga_ref, gb_ref, o_ref, dga_ref, dgb_ref):
        i = pl.program_id(0)
        dm = _dot_nt(dh_ref[...].astype(BF16), w_ref[...])
        _, xa, ra = _rms_fwd(a_ref[...], ga_ref[...])
        _, xb, rb = _rms_fwd(b_ref[...], gb_ref[...])
        da, dga = _rms_bwd(dm[:, :SB_W], xa, ra, ga_ref[...])
        db, dgb = _rms_bwd(dm[:, SB_W:], xb, rb, gb_ref[...])
        o_ref[...] = jnp.concatenate([da.astype(BF16), db.astype(BF16)], axis=1)

        @pl.when(i == 0)
        def _():
            dga_ref[...] = dga
            dgb_ref[...] = dgb

        @pl.when(i > 0)
        def _():
            dga_ref[...] += dga
            dgb_ref[...] += dgb

    return _call(
        body, name="dmixed", grid=(t // TM,),
        in_specs=[pl.BlockSpec((TM, d), lambda i: (i, 0)), pl.BlockSpec((d, d), lambda i: (0, 0)),
                  pl.BlockSpec((TM, SB_W), lambda i: (i, 0)), pl.BlockSpec((TM, SWA_W), lambda i: (i, 0)),
                  pl.BlockSpec((1, SB_W), lambda i: (0, 0)), pl.BlockSpec((1, SWA_W), lambda i: (0, 0))],
        out_specs=[pl.BlockSpec((TM, d), lambda i: (i, 0)), pl.BlockSpec((1, SB_W), lambda i: (0, 0)),
                   pl.BlockSpec((1, SWA_W), lambda i: (0, 0))],
        out_shape=[jax.ShapeDtypeStruct((t, d), BF16), jax.ShapeDtypeStruct((1, SB_W), F32),
                   jax.ShapeDtypeStruct((1, SWA_W), F32)],
        compiler_params=_params(1))(dh, w_out, o_sb, o_sw, g_sb, g_sw)


def _wgrad(name, a, a_spec, b, b_spec, grid, out_shape, out_spec, scale):
    def body(a_ref, b_ref, o_ref):
        r = _dot_tn(a_ref[...], b_ref[...].astype(BF16))
        o_ref[...] = r if scale == 1.0 else scale * r

    return _call(
        body, name=name, grid=grid, in_specs=[a_spec, b_spec], out_specs=out_spec,
        out_shape=jax.ShapeDtypeStruct(out_shape, F32), compiler_params=_params(len(grid)))(a, b)


def _wgrad_gu(n, dgu):
    t, d = n.shape
    return _wgrad(
        "wgrad_gu", n, pl.BlockSpec((t, TM), lambda s, r: (0, r)),
        dgu, pl.BlockSpec((None, t, FS), lambda s, r: (s // 2, 0, s % 2)), (N_CHIPS, d // TM),
        (N_CHIPS, d, FS), pl.BlockSpec((None, TM, FS), lambda s, r: (s, r, 0)), 1.0)


def _wgrad_down(act, dh):
    t, d = dh.shape
    return _wgrad(
        "wgrad_down", act, pl.BlockSpec((t, FS), lambda s: (0, s)), dh, pl.BlockSpec((t, d), lambda s: (0, 0)),
        (2,), (D_FF, d), pl.BlockSpec((FS, d), lambda s: (s, 0)), 0.5)


def _wgrad_out(mixed, dh):
    t, d = dh.shape
    return _wgrad(
        "wgrad_out", mixed, pl.BlockSpec((t, TM), lambda s: (0, s)), dh, pl.BlockSpec((t, d), lambda s: (0, 0)),
        (d // TM,), (d, d), pl.BlockSpec((TM, d), lambda s: (s, 0)), 1.0)


def _wgrad_in(n, dproj):
    t, d = n.shape
    w = dproj.shape[1]
    tw = w // 3
    return _wgrad(
        "wgrad_in", dproj, pl.BlockSpec((t, tw), lambda s: (0, s)), n, pl.BlockSpec((t, d), lambda s: (0, 0)),
        (3,), (w, d), pl.BlockSpec((tw, d), lambda s: (s, 0)), 1.0)


def _tri(rel):
    row = lax.broadcasted_iota(jnp.int32, (BLK, BLK), 0)
    col = lax.broadcasted_iota(jnp.int32, (BLK, BLK), 1)
    m = rel(row, col).astype(BF16)
    return jnp.concatenate([m, m], axis=0)


def _scan_dot(x, tri2):
    hi = x.astype(BF16)
    lo = (x - hi.astype(F32)).astype(BF16)
    return _dot(jnp.concatenate([hi, lo], axis=1), tri2)


def _head_masks():
    lane = lax.broadcasted_iota(jnp.int32, (1, LANES), 1)
    return [lane < HEAD_DIM, lane >= HEAD_DIM]


SB_PAIRS = 2
SB_ROWS = 2 * SB_PAIRS * BLK


def _sb_causal():
    row = lax.broadcasted_iota(jnp.int32, (SB_ROWS, BLK), 0) & (BLK - 1)
    return lax.broadcasted_iota(jnp.int32, (SB_ROWS, BLK), 1) < row


def _sb_mask_last(x, causal):
    own = jnp.where(causal, x[:, -BLK:], 0.0)
    return own if x.shape[1] == BLK else jnp.concatenate([x[:, :-BLK], own], axis=1)


def _sb_stack(x, hm):
    return jnp.concatenate([jnp.where(m, x[:, p * LANES:(p + 1) * LANES], jnp.zeros((BLK, LANES), x.dtype))
                            for p in range(SB_PAIRS) for m in hm], axis=0)


def _sb_unstack(y, hm):
    return jnp.concatenate([jnp.where(hm[0], y[2 * p * BLK:(2 * p + 1) * BLK], y[(2 * p + 1) * BLK:(2 * p + 2) * BLK])
                            for p in range(SB_PAIRS)], axis=1)


def _sb_pairs():
    return [(slice(2 * p * BLK, (2 * p + 2) * BLK), slice(p * LANES, (p + 1) * LANES)) for p in range(SB_PAIRS)]


def _sb_fwd(proj):
    t = proj.shape[0]
    nb = SB_KT // BLK
    wide = SB_PAIRS * LANES

    def body(q_ref, k_ref, v_ref, o_ref, tot_ref):
        hm = _head_masks()
        causal = _sb_causal()
        pairs = _sb_pairs()
        after = _tri(lambda r, c: r > c)

        def tile(qh, start, n_blk, carry, acc, own):
            ks = pl.ds(pl.multiple_of(start, BLK), n_blk * BLK)
            z = jnp.concatenate([_dot_nt(qh[rows], k_ref[ks, lanes]) for rows, lanes in pairs], axis=0)
            sp, zs = _softplus(z)
            spm = _sb_mask_last(sp, causal) if own else sp
            sufs = [None] * n_blk
            for b in reversed(range(n_blk)):
                blk = spm[:, b * BLK:(b + 1) * BLK]
                sufs[b] = carry + _scan_dot(blk, after)
                carry = carry + jnp.sum(blk, axis=1, keepdims=True)
            w = jnp.exp(zs - jnp.concatenate(sufs, axis=1))
            wb = (_sb_mask_last(w, causal) if own else w).astype(BF16)
            return carry, acc + jnp.concatenate([_dot(wb[rows], v_ref[ks, lanes]) for rows, lanes in pairs], axis=0)

        def qblock(g, j):
            qs = pl.ds(pl.multiple_of(g * SB_KT + j * BLK, BLK), BLK)
            qh = _sb_stack(q_ref[qs, :] * SCALE, hm)
            c0 = tile(qh, g * SB_KT, j + 1, jnp.zeros((SB_ROWS, 1), F32), jnp.zeros((SB_ROWS, LANES), F32), True)
            carry, acc = lax.fori_loop(0, g, lambda n, c: tile(qh, (g - 1 - n) * SB_KT, nb, c[0], c[1], False), c0)
            o_ref[qs, :] = _sb_unstack(acc, hm)
            for h in range(2 * SB_PAIRS):
                tot_ref[h, qs, :] = carry[h * BLK:(h + 1) * BLK]

        def group(g, _):
            for j in range(nb):
                qblock(g, j)
            return 0

        lax.fori_loop(0, t // SB_KT, group, 0)

    col_blk = lambda off: pl.BlockSpec((t, wide), lambda g: (0, off + g))
    n_steps = SB_W // wide
    return _call(
        body, name="sb_fwd", grid=(n_steps,), in_specs=[col_blk(0), col_blk(n_steps), col_blk(2 * n_steps)],
        out_specs=[col_blk(0), pl.BlockSpec((2 * SB_PAIRS, t, 1), lambda g: (g, 0, 0))],
        out_shape=[jax.ShapeDtypeStruct((t, SB_W), F32), jax.ShapeDtypeStruct((8, t, 1), F32)],
        compiler_params=_params(1))(proj, proj, proj)


def _sb_bwd(proj, d_o, tot):
    t = proj.shape[0]
    nb = SB_KT // BLK
    wide = SB_PAIRS * LANES

    def body(q_ref, k_ref, v_ref, do_ref, tot_ref, dq_ref, dk_ref, dv_ref, dk_acc, dv_acc):
        hm = _head_masks()
        causal = _sb_causal()
        pairs = _sb_pairs()
        before = _tri(lambda r, c: r < c)
        upto = _tri(lambda r, c: r <= c)
        dk_acc[...] = jnp.zeros_like(dk_acc)
        dv_acc[...] = jnp.zeros_like(dv_acc)

        def tile(qh, doh, tt, start, n_blk, pre, ecum, dq, own):
            ks = pl.ds(pl.multiple_of(start, BLK), n_blk * BLK)
            k = k_ref[ks, :]
            v = v_ref[ks, :]
            z = jnp.concatenate([_dot_nt(qh[rows], k[:, lanes]) for rows, lanes in pairs], axis=0)
            sp, zs = _softplus(z)
            spm = _sb_mask_last(sp, causal) if own else sp
            pres = []
            for b in range(n_blk):
                blk = spm[:, b * BLK:(b + 1) * BLK]
                pres.append(pre + _scan_dot(blk, before))
                pre = pre + jnp.sum(blk, axis=1, keepdims=True)
            logw = z - (tt - jnp.concatenate(pres, axis=1))
            if own:
                logw = jnp.minimum(logw, 0.0)
            w = jnp.exp(logw)
            if own:
                w = _sb_mask_last(w, causal)
            e = w * jnp.concatenate([_dot_nt(doh[rows], v[:, lanes]) for rows, lanes in pairs], axis=0)
            incs = []
            for b in range(n_blk):
                blk = e[:, b * BLK:(b + 1) * BLK]
                incs.append(ecum + _scan_dot(blk, upto))
                ecum = ecum + jnp.sum(blk, axis=1, keepdims=True)
            dz = e - jnp.exp(zs) * jnp.concatenate(incs, axis=1)
            if own:
                dz = _sb_mask_last(dz, causal)
            dzb = dz.astype(BF16)
            wb = w.astype(BF16)
            for rows, lanes in pairs:
                dk_acc[ks, lanes] += _dot_tn(dzb[rows], qh[rows])
                dv_acc[ks, lanes] += _dot_tn(wb[rows], doh[rows])
            return pre, ecum, dq + jnp.concatenate([_dot(dzb[rows], k[:, lanes]) for rows, lanes in pairs], axis=0)

        def qblock(g, j):
            qs = pl.ds(pl.multiple_of(g * SB_KT + j * BLK, BLK), BLK)
            qh = _sb_stack(q_ref[qs, :] * SCALE, hm)
            doh = _sb_stack(do_ref[qs, :], hm)
            tt = jnp.concatenate([tot_ref[h, qs, :] for h in range(2 * SB_PAIRS)], axis=0)
            c0 = (jnp.zeros((SB_ROWS, 1), F32), jnp.zeros((SB_ROWS, 1), F32), jnp.zeros((SB_ROWS, LANES), F32))
            c = lax.fori_loop(0, g, lambda kt, c: tile(qh, doh, tt, kt * SB_KT, nb, c[0], c[1], c[2], False), c0)
            dq = tile(qh, doh, tt, g * SB_KT, j + 1, c[0], c[1], c[2], True)[2]
            dq_ref[qs, :] = (_sb_unstack(dq, hm) * SCALE).astype(BF16)

        def group(g, _):
            for j in range(nb):
                qblock(g, j)
            return 0

        lax.fori_loop(0, t // SB_KT, group, 0)
        dk_ref[...] = dk_acc[...].astype(BF16)
        dv_ref[...] = dv_acc[...].astype(BF16)

    col_blk = lambda off: pl.BlockSpec((t, wide), lambda g: (0, off + g))
    n_steps = SB_W // wide
    out = jax.ShapeDtypeStruct((t, SB_W), BF16)
    return _call(
        body, name="sb_bwd", grid=(n_steps,),
        in_specs=[col_blk(0), col_blk(n_steps), col_blk(2 * n_steps), col_blk(0),
                  pl.BlockSpec((2 * SB_PAIRS, t, 1), lambda g: (g, 0, 0))],
        out_specs=[col_blk(0), col_blk(0), col_blk(0)], out_shape=[out, out, out],
        scratch_shapes=[pltpu.VMEM((t, wide), F32), pltpu.VMEM((t, wide), F32)],
        compiler_params=_params(1))(proj, proj, proj, d_o, tot)


def _bucket_table():
    a = np.arange(BLK)[:, None]
    c = np.arange(2 * BLK)[None, :]
    dist = np.maximum(BLK + a - c, 0)
    max_exact = N_BUCKETS // 2
    dd = np.maximum(dist, 1).astype(np.float32)
    large = max_exact + (np.log(dd / max_exact) / math.log(MAX_DISTANCE / max_exact)
                         * (N_BUCKETS - max_exact)).astype(np.int32)
    large = np.minimum(large, N_BUCKETS - 1)
    return np.where(dist < max_exact, dist, large).astype(np.int32)


SWA_H = 8


def _swa_band_masks():
    row = lax.broadcasted_iota(jnp.int32, (SWA_H * BLK, 2 * BLK), 0) & (BLK - 1)
    col = lax.broadcasted_iota(jnp.int32, (SWA_H * BLK, 2 * BLK), 1)
    own = lax.broadcasted_iota(jnp.int32, (SWA_H * BLK, BLK), 1) <= (
        lax.broadcasted_iota(jnp.int32, (SWA_H * BLK, BLK), 0) & (BLK - 1))
    return (col > row) & ((col < BLK) | (col - BLK <= row)), own


def _swa_stack(ref, qs, hm, scale):
    parts = []
    for hq in range(SWA_H):
        kvh = hq // SWA_G
        x = ref[qs, (hq // 2) * LANES:(hq // 2 + 1) * LANES].astype(F32)
        if hq % 2 != kvh:
            x = pltpu.roll(x, HEAD_DIM, 1)
        parts.append(jnp.where(hm[kvh], x * scale, 0.0).astype(BF16))
    return jnp.concatenate(parts, axis=0)


def _swa_unstack(x8, hm):
    heads = []
    for hq in range(SWA_H):
        x = x8[hq * BLK:(hq + 1) * BLK]
        heads.append(pltpu.roll(x, HEAD_DIM, 1) if hq % 2 != hq // SWA_G else x)
    return [jnp.where(hm[0], heads[2 * p], heads[2 * p + 1]) for p in range(SWA_H // 2)]


def _swa_scores(q8, kb, bias_ref, mask, cols):
    bias8 = jnp.concatenate([bias_ref[hq, :, cols] for hq in range(SWA_H)], axis=0)
    return jnp.where(mask, _dot_nt(q8, kb) + bias8, NEG_INF)


def _swa_sinks(sink_ref):
    return jnp.concatenate([jnp.broadcast_to(sink_ref[hq:hq + 1, 0:1], (BLK, 1)) for hq in range(SWA_H)], axis=0)


def _swa_fwd(proj, bias, sinks_b):
    t = proj.shape[0]
    nq = t // BLK

    def body(q_ref, k_ref, v_ref, bias_ref, sink_ref, o_ref, lse_ref):
        hm = _head_masks()
        band, own = _swa_band_masks()

        def qblock(i, prev):
            qs = pl.ds(pl.multiple_of(i * BLK, BLK), BLK)
            if prev:
                ks, mask, cols = pl.ds(pl.multiple_of((i - 1) * BLK, BLK), 2 * BLK), band, slice(None)
            else:
                ks, mask, cols = qs, own, slice(BLK, None)
            q8 = _swa_stack(q_ref, qs, hm, SCALE)
            sink8 = _swa_sinks(sink_ref)
            s = _swa_scores(q8, k_ref[ks, :], bias_ref, mask, cols)
            m = jnp.maximum(jnp.max(s, axis=1, keepdims=True), sink8)
            p = jnp.exp(s - m)
            den = jnp.sum(p, axis=1, keepdims=True) + jnp.exp(sink8 - m)
            o8 = _dot((p * (1.0 / den)).astype(BF16), v_ref[ks, :])
            lse8 = m + jnp.log(den)
            for hq in range(SWA_H):
                lse_ref[hq, qs, :] = lse8[hq * BLK:(hq + 1) * BLK]
            for pp, o in enumerate(_swa_unstack(o8, hm)):
                o_ref[qs, pp * LANES:(pp + 1) * LANES] = o

        qblock(0, False)

        def step(i, _):
            qblock(i, True)
            return 0

        lax.fori_loop(1, nq, step, 0)

    return _call(
        body, name="swa_fwd", grid=(1,),
        in_specs=[pl.BlockSpec((t, SWA_W), lambda i: (0, 3)), pl.BlockSpec((t, KV_W), lambda i: (0, 16)),
                  pl.BlockSpec((t, KV_W), lambda i: (0, 17)), pl.BlockSpec((8, BLK, 2 * BLK), lambda i: (0, 0, 0)),
                  pl.BlockSpec((8, LANES), lambda i: (0, 0))],
        out_specs=[pl.BlockSpec((t, SWA_W), lambda i: (0, 0)), pl.BlockSpec((8, t, 1), lambda i: (0, 0, 0))],
        out_shape=[jax.ShapeDtypeStruct((t, SWA_W), F32), jax.ShapeDtypeStruct((8, t, 1), F32)],
        compiler_params=_params(1))(proj, proj, proj, bias, sinks_b)


def _swa_bwd(proj, d_o, lse, bias, sinks_b, dbias_in):
    t = proj.shape[0]
    nq = t // BLK

    def body(q_ref, k_ref, v_ref, do_ref, lse_ref, bias_ref, sink_ref, dbi_ref,
             dq_ref, dk_ref, dv_ref, dsink_ref, dbias_ref, dk_acc, dv_acc):
        hm = _head_masks()
        band, own = _swa_band_masks()
        dk_acc[...] = jnp.zeros_like(dk_acc)
        dv_acc[...] = jnp.zeros_like(dv_acc)
        dbias_ref[...] = dbi_ref[...]

        def qblock(i, prev, dsink8):
            qs = pl.ds(pl.multiple_of(i * BLK, BLK), BLK)
            if prev:
                ks, mask, cols = pl.ds(pl.multiple_of((i - 1) * BLK, BLK), 2 * BLK), band, slice(None)
            else:
                ks, mask, cols = qs, own, slice(BLK, None)
            q8 = _swa_stack(q_ref, qs, hm, SCALE)
            do8 = _swa_stack(do_ref, qs, hm, 1.0)
            sink8 = _swa_sinks(sink_ref)
            lse8 = jnp.concatenate([lse_ref[hq, qs, :] for hq in range(SWA_H)], axis=0)
            kb = k_ref[ks, :]
            p = jnp.exp(_swa_scores(q8, kb, bias_ref, mask, cols) - lse8)
            dp = _dot_nt(do8, v_ref[ks, :])
            delta = jnp.sum(p * dp, axis=1, keepdims=True)
            ds = p * (dp - delta)
            for hq in range(SWA_H):
                dbias_ref[hq, :, cols] += ds[hq * BLK:(hq + 1) * BLK]
            dsb = ds.astype(BF16)
            dk_acc[ks, :] += _dot_tn(dsb, q8)
            dv_acc[ks, :] += _dot_tn(p.astype(BF16), do8)
            for pp, dq in enumerate(_swa_unstack(_dot(dsb, kb) * SCALE, hm)):
                dq_ref[qs, pp * LANES:(pp + 1) * LANES] = dq.astype(BF16)
            return dsink8 - jnp.exp(sink8 - lse8) * delta

        ds0 = qblock(0, False, jnp.zeros((SWA_H * BLK, 1), F32))
        ds8 = lax.fori_loop(1, nq, lambda i, c: qblock(i, True, c), ds0)
        for hq in range(SWA_H):
            dsink_ref[hq:hq + 1, :] = jnp.broadcast_to(
                jnp.sum(ds8[hq * BLK:(hq + 1) * BLK], axis=0, keepdims=True), (1, LANES))

        dk_ref[...] = dk_acc[...].astype(BF16)
        dv_ref[...] = dv_acc[...].astype(BF16)

    full3 = pl.BlockSpec((8, BLK, 2 * BLK), lambda i: (0, 0, 0))
    kv = jax.ShapeDtypeStruct((t, KV_W), BF16)
    return _call(
        body, name="swa_bwd", grid=(1,),
        in_specs=[pl.BlockSpec((t, SWA_W), lambda i: (0, 3)), pl.BlockSpec((t, KV_W), lambda i: (0, 16)),
                  pl.BlockSpec((t, KV_W), lambda i: (0, 17)), pl.BlockSpec((t, SWA_W), lambda i: (0, 1)),
                  pl.BlockSpec((8, t, 1), lambda i: (0, 0, 0)), full3, pl.BlockSpec((8, LANES), lambda i: (0, 0)),
                  full3],
        out_specs=[pl.BlockSpec((t, SWA_W), lambda i: (0, 0)), pl.BlockSpec((t, KV_W), lambda i: (0, 0)),
                   pl.BlockSpec((t, KV_W), lambda i: (0, 0)), pl.BlockSpec((8, LANES), lambda i: (0, 0)), full3],
        out_shape=[jax.ShapeDtypeStruct((t, SWA_W), BF16), kv, kv, jax.ShapeDtypeStruct((8, LANES), F32),
                   jax.ShapeDtypeStruct((8, BLK, 2 * BLK), F32)],
        scratch_shapes=[pltpu.VMEM((t, KV_W), F32), pltpu.VMEM((t, KV_W), F32)],
        compiler_params=_params(1))(proj, proj, proj, d_o, lse, bias, sinks_b, dbias_in)


def _concat_cols(parts):
    t = parts[0].shape[0]
    widths = [a.shape[1] for a in parts]

    def body(*refs):
        refs[-1][...] = jnp.concatenate([r[...] for r in refs[:-1]], axis=1)

    return _call(
        body, name="concat_cols", grid=(t // TM,),
        in_specs=[pl.BlockSpec((TM, w), lambda i: (i, 0)) for w in widths],
        out_specs=pl.BlockSpec((TM, sum(widths)), lambda i: (i, 0)),
        out_shape=jax.ShapeDtypeStruct((t, sum(widths)), parts[0].dtype), compiler_params=_params(1))(*parts)


def _bias_table(rel_bias, buckets):
    def body(rb_ref, b_ref, o_ref):
        bk = b_ref[...]
        for h in range(8):
            acc = jnp.zeros((BLK, 2 * BLK), F32)
            for b in range(N_BUCKETS):
                acc = jnp.where(bk == b, rb_ref[b, h], acc)
            o_ref[h] = acc

    return _call(
        body, name="bias_table", grid=(1,),
        in_specs=[pl.BlockSpec(memory_space=pltpu.SMEM), pl.BlockSpec((BLK, 2 * BLK), lambda i: (0, 0))],
        out_specs=pl.BlockSpec((8, BLK, 2 * BLK), lambda i: (0, 0, 0)),
        out_shape=jax.ShapeDtypeStruct((8, BLK, 2 * BLK), F32), compiler_params=_params(1))(rel_bias, buckets)


def _bias_grad(dbias, buckets):
    def body(d_ref, b_ref, o_ref):
        lane = lax.broadcasted_iota(jnp.int32, (1, LANES), 1)
        bk = b_ref[...]
        for h in range(8):
            d = d_ref[h]
            acc = jnp.zeros((1, LANES), F32)
            for b in range(N_BUCKETS):
                s = jnp.sum(jnp.sum(jnp.where(bk == b, d, 0.0), axis=0, keepdims=True), axis=1, keepdims=True)
                acc = acc + jnp.where(lane == b, s, 0.0)
            o_ref[h:h + 1, :] = acc

    return _call(
        body, name="bias_grad", grid=(1,),
        in_specs=[pl.BlockSpec((8, BLK, 2 * BLK), lambda i: (0, 0, 0)), pl.BlockSpec((BLK, 2 * BLK), lambda i: (0, 0))],
        out_specs=pl.BlockSpec((8, LANES), lambda i: (0, 0)),
        out_shape=jax.ShapeDtypeStruct((8, LANES), F32), compiler_params=_params(1))(dbias, buckets)


def _row(a):
    return a.reshape(1, -1)


def _fwd_ffn1_gu(h, n1, w):
    s = {"h0": h, "n1": n1}
    s["gu1"], s["act1"] = _ffn_gu(n1, w["ffn1_gu"])
    return s


def _fwd_ffn1_down(s, w, small, l):
    s["h1"], s["nm"] = _down_res(s["act1"], w["ffn1_down"], s["h0"], _row(small["norm_mix"][l]))


def _fwd_ffn1(h, n1, w, small, l):
    s = _fwd_ffn1_gu(h, n1, w)
    _fwd_ffn1_down(s, w, small, l)
    return s


def _fwd_proj_sb(s, w):
    s["proj"] = _proj(s["nm"], w["w_in"])
    s["o_sb"], s["tot"] = _sb_fwd(s["proj"])


def _fwd_swa(s, small, l, bias):
    s["sinks_b"] = jnp.broadcast_to(small["sinks"][l][:, None], (8, LANES))
    s["o_sw"], s["lse"] = _swa_fwd(s["proj"], bias, s["sinks_b"])


def _fwd_out_gu2(s, w, small, l):
    s["h2"], s["mixed"], s["n2"] = _out_res(
        s["o_sb"], s["o_sw"], _row(small["norm_out_sb"][l]), _row(small["norm_out_swa"][l]), w["w_out"], s["h1"],
        _row(small["norm_ffn2"][l]))
    s["gu2"], s["act2"] = _ffn_gu(s["n2"], w["ffn2_gu"])


def _fwd_ffn2_down(s, w, g_after):
    return _down_res(s["act2"], w["ffn2_down"], s["h2"], g_after)


def _fwd_out_ffn2(s, w, small, l, g_after):
    _fwd_out_gu2(s, w, small, l)
    return _fwd_ffn2_down(s, w, g_after)


def _bwd_ffn_dact(dh, s, w, which):
    return _ffn_dact(dh[1], w[f"ffn{which}_down"], s[f"gu{which}"])


def _bwd_ffn_rest(dh, dgu, s, w, small, l, which):
    h_in, norm = (s["h0"], "norm_ffn1") if which == 1 else (s["h2"], "norm_ffn2")
    g_down = _wgrad_down(s[f"act{which}"], dh[1])
    g_gu = _wgrad_gu(s[f"n{which}"], dgu)
    dh32, dh16, dg = _ffn_dn(dgu, w[f"ffn{which}_gu"], dh[0], h_in, _row(small[norm][l]))
    return (dh32, dh16), {f"ffn{which}_down": g_down, f"ffn{which}_gu": g_gu}, {norm: dg}


def _bwd_ffn(dh, s, w, small, l, which):
    return _bwd_ffn_rest(dh, _bwd_ffn_dact(dh, s, w, which), s, w, small, l, which)


def _bwd_mix(dh, s, w, small, l, bias, dbias):
    g_out = _wgrad_out(s["mixed"], dh[1])
    d_o, dg_sb, dg_sw = _dmixed(dh[1], w["w_out"], s["o_sb"], s["o_sw"], _row(small["norm_out_sb"][l]),
                                _row(small["norm_out_swa"][l]))
    dq_sb, dk_sb, dv_sb = _sb_bwd(s["proj"], d_o, s["tot"])
    dq_sw, dk_sw, dv_sw, dsink, dbias = _swa_bwd(s["proj"], d_o, s["lse"], bias, s["sinks_b"], dbias)
    dproj = _concat_cols([dq_sb, dk_sb, dv_sb, dq_sw, dk_sw, dv_sw])
    g_in = _wgrad_in(s["nm"], dproj)
    dh32, dh16, dg_mix = _mix_dn(dproj, w["w_in"], dh[0], s["h1"], _row(small["norm_mix"][l]))
    gs = {"norm_out_sb": dg_sb, "norm_out_swa": dg_sw, "sinks": dsink[:, 0], "norm_mix": dg_mix}
    return (dh32, dh16), {"w_out": g_out, "w_in": g_in}, gs, dbias


def _place():
    x, y, c = lax.axis_index("x"), lax.axis_index("y"), lax.axis_index("c")
    return x, y, c, 2 * x + y


def _chip_core(k, c):
    return (k // 2, k % 2, c)


def _rows_per_block(rows, cols, copies):
    best = 16
    for tr in range(16, rows + 1, 16):
        if rows % tr == 0 and copies * tr * cols * 4 <= SLAB_BLOCK_BYTES:
            best = tr
    assert rows % best == 0
    return best


def _place_own(w, l, me1):
    _, rows, cols = w.shape
    tr = _rows_per_block(rows // 2, cols, 1)
    per_half = rows // 2 // tr

    def body(me_ref, w_ref, o_ref):
        o_ref[...] = w_ref[...].astype(BF16)

    return _call(
        body, name="place_own",
        num_scalar_prefetch=1, grid=(rows // tr,),
        in_specs=[pl.BlockSpec((None, tr, cols), lambda r, me: (l, r, 0))],
        out_specs=pl.BlockSpec((None, None, tr, cols), lambda r, me: (me[0], r // per_half, r % per_half, 0)),
        out_shape=jax.ShapeDtypeStruct((N_CHIPS, 2, rows // 2, cols), BF16), compiler_params=_params(1))(me1, w)


def _plan_gather_ici(bufs):
    _, _, c, me = _place()
    return [(b.at[me, c], b.at[me, c], b.at[(me + 3 - j) % N_CHIPS, c], _chip_core((me + 1 + j) % N_CHIPS, c))
            for b in bufs for j in range(3)]


def _plan_gather_d2d(bufs):
    x, y, c, me = _place()
    return [(b.at[(me + 3 - j) % N_CHIPS, c], b.at[(me + 3 - j) % N_CHIPS, c], b.at[(me + 3 - j) % N_CHIPS, 1 - c],
             (x, y, 1 - c)) for b in bufs for j in range(3)]


def _plan_grad_sibling(bufs):
    x, y, c, _ = _place()
    n = len(bufs) // 2
    return [(g.at[:, 1 - c], z, z, (x, y, 1 - c)) for g, z in zip(bufs[:n], bufs[n:])]


def _plan_grad_chips(bufs):
    _, _, c, me = _place()
    n = len(bufs) // 2
    return [(p.at[j], z.at[j], z.at[j], _chip_core((me + 1 + j) % N_CHIPS, c))
            for p, z in zip(bufs[:n], bufs[n:]) for j in range(3)]


def _plan_grad_halves(bufs):
    x, y, c, _ = _place()
    return [(b.at[c], b.at[c], b.at[1 - c], (x, y, 1 - c)) for b in bufs]


def _remote(src, dst, send_sem, recv_sem, to):
    return pltpu.make_async_remote_copy(src_ref=src, dst_ref=dst, send_sem=send_sem, recv_sem=recv_sem,
                                        device_id=to, device_id_type=MESH)


def _exchange_start_groups(name, plan, groups):
    sizes = [len(g) for g, _ in groups]
    bufs = [a for g, _ in groups for a in g]
    n, n_groups = len(bufs), len(groups)

    def body(*refs):
        ins, sems, token = refs[:n], refs[n:n + 2 * n_groups], refs[-1]
        at = 0
        for k, size in enumerate(sizes):
            for i, (src, dst, _, to) in enumerate(plan(ins[at:at + size])):
                _remote(src, dst, sems[2 * k].at[i], sems[2 * k + 1].at[i], to).start()
            at += size
        token[...] = jnp.zeros_like(token)

    sem_shapes = [pltpu.SemaphoreType.DMA((n_copies,)) for _, n_copies in groups for _ in range(2)]
    out = _call(
        body, name=name,
        out_shape=(*sem_shapes, *[pltpu.HBM(a.shape, a.dtype) for a in bufs], jax.ShapeDtypeStruct((8, LANES), F32)),
        in_specs=[HBM] * n,
        out_specs=(*[SEM] * (2 * n_groups), *[HBM] * n, pl.BlockSpec(memory_space=pltpu.VMEM)),
        input_output_aliases={t: 2 * n_groups + t for t in range(n)}, hbm_args=n,
        compiler_params=pltpu.CompilerParams(has_side_effects=EFFECT),
    )(*bufs)
    flights, at = [], 2 * n_groups
    for k, size in enumerate(sizes):
        flights.append(((out[2 * k], out[2 * k + 1]), list(out[at:at + size])))
        at += size
    return flights


def _exchange_start(name, plan, bufs, n_copies):
    return _exchange_start_groups(name, plan, [(bufs, n_copies)])[0]


def _exchange_wait(name, plan, bufs, sems):
    n = len(bufs)

    def body(*refs):
        ins = refs[:n]
        ssem, rsem = refs[n], refs[n + 1]
        for i, (src, dst, land, to) in enumerate(plan(ins)):
            _remote(src, dst, ssem.at[i], rsem.at[i], to).wait_send()
            _remote(land, land, ssem.at[i], rsem.at[i], to).wait_recv()

    return list(_call(
        body, name=name, out_shape=[pltpu.HBM(a.shape, a.dtype) for a in bufs],
        in_specs=[HBM] * n + [SEM, SEM], out_specs=[HBM] * n,
        input_output_aliases={t: t for t in range(n)},
        compiler_params=pltpu.CompilerParams(has_side_effects=EFFECT),
    )(*bufs, sems[0], sems[1]))


def _exchange_pass(name, done, plan, bufs, sems, n_copies):
    n = len(bufs)

    def body(*refs):
        ins = refs[:n]
        old_s, old_r, ssem, rsem = refs[n], refs[n + 1], refs[n + 2], refs[n + 3]
        token = refs[-1]
        for i, (src, dst, land, to) in enumerate(done(ins)):
            _remote(src, dst, old_s.at[i], old_r.at[i], to).wait_send()
            _remote(land, land, old_s.at[i], old_r.at[i], to).wait_recv()
        for i, (src, dst, _, to) in enumerate(plan(ins)):
            _remote(src, dst, ssem.at[i], rsem.at[i], to).start()
        token[...] = jnp.zeros_like(token)

    out = _call(
        body, name=name,
        out_shape=(pltpu.SemaphoreType.DMA((n_copies,)), pltpu.SemaphoreType.DMA((n_copies,)),
                   *[pltpu.HBM(a.shape, a.dtype) for a in bufs], jax.ShapeDtypeStruct((8, LANES), F32)),
        in_specs=[HBM] * n + [SEM, SEM], out_specs=(SEM, SEM, *[HBM] * n, pl.BlockSpec(memory_space=pltpu.VMEM)),
        input_output_aliases={t: 2 + t for t in range(n)},
        compiler_params=pltpu.CompilerParams(has_side_effects=EFFECT),
    )(*bufs, sems[0], sems[1])
    return (out[0], out[1]), list(out[2:2 + n])


def _chip_sum(g, xbuf, cm):
    _, _, r2, cols = g.shape
    tr = _rows_per_block(r2, cols, 1)

    def body(cm_ref, g_ref, x_ref, o_ref):
        o_ref[...] = (g_ref[...] + x_ref[...]).astype(BF16)

    return _call(
        body, name="grad_chip_sum",
        num_scalar_prefetch=1, grid=(3, r2 // tr),
        in_specs=[pl.BlockSpec((None, None, tr, cols), lambda j, r, cm: ((cm[1] + 1 + j) % N_CHIPS, cm[0], r, 0)),
                  pl.BlockSpec((None, tr, cols), lambda j, r, cm: ((cm[1] + 1 + j) % N_CHIPS, r, 0))],
        out_specs=pl.BlockSpec((None, tr, cols), lambda j, r, cm: (j, r, 0)),
        out_shape=jax.ShapeDtypeStruct((3, r2, cols), BF16), compiler_params=_params(2))(cm, g, xbuf)


def _total_sum(g, xbuf, rbuf, cm):
    _, _, r2, cols = g.shape
    tr = _rows_per_block(r2, cols, 3)

    def body(cm_ref, g_ref, x_ref, r_ref, o_ref):
        acc = g_ref[...] + x_ref[...]
        for j in range(3):
            acc = acc + r_ref[j].astype(F32)
        o_ref[...] = acc

    return _call(
        body, name="grad_total_sum",
        num_scalar_prefetch=1, grid=(r2 // tr,),
        in_specs=[pl.BlockSpec((None, None, tr, cols), lambda r, cm: (cm[1], cm[0], r, 0)),
                  pl.BlockSpec((None, tr, cols), lambda r, cm: (cm[1], r, 0)),
                  pl.BlockSpec((3, tr, cols), lambda r, cm: (0, r, 0))],
        out_specs=pl.BlockSpec((None, tr, cols), lambda r, cm: (cm[0], r, 0)),
        out_shape=jax.ShapeDtypeStruct((2, r2, cols), F32), compiler_params=_params(1))(cm, g, xbuf, rbuf)


def _small_allreduce(v):
    rows = v.shape[0]
    n_dev = 2 * N_CHIPS

    def body(v_ref, o_ref, buf, ssem, rsem):
        x, y, c, _ = _place()
        me = 4 * x + 2 * y + c
        buf[me] = v_ref[...]

        def copy(d, slot, to):
            return _remote(v_ref, buf.at[slot], ssem.at[d - 1], rsem.at[d - 1], (to // 4, (to // 2) % 2, to % 2))

        cps = [copy(d, me, (me + d) % n_dev) for d in range(1, n_dev)]
        for cp in cps:
            cp.start()
        for d in range(1, n_dev):
            copy(d, (me + n_dev - d) % n_dev, me).wait_recv()
        for cp in cps:
            cp.wait_send()
        acc = buf[0]
        for i in range(1, n_dev):
            acc = acc + buf[i]
        o_ref[...] = acc

    vm = pl.BlockSpec(memory_space=pltpu.VMEM)
    return _call(
        body, name="small_allreduce", in_specs=[vm], out_specs=vm,
        out_shape=jax.ShapeDtypeStruct(v.shape, F32),
        scratch_shapes=[pltpu.VMEM((n_dev, rows, LANES), F32), pltpu.SemaphoreType.DMA((n_dev - 1,)),
                        pltpu.SemaphoreType.DMA((n_dev - 1,))],
        compiler_params=pltpu.CompilerParams(vmem_limit_bytes=V7X_VMEM_LIMIT))(v)


def _adamw_math(w, g, m, v):
    m2 = ADAM_B1 * m + (1.0 - ADAM_B1) * g
    v2 = ADAM_B2 * v + (1.0 - ADAM_B2) * (g * g)
    m_hat = m2 / (1.0 - ADAM_B1 ** ADAM_STEP)
    v_hat = v2 / (1.0 - ADAM_B2 ** ADAM_STEP)
    return -ADAM_LR * (m_hat / (jnp.sqrt(v_hat) + ADAM_EPS) + ADAM_WD * w), m2, v2


def _adamw_layer(w, g, m, v, l, prev):
    _, rows, cols = w.shape
    tr = rows
    for cand in range(8, rows + 1, 8):
        if rows % cand == 0 and cand * cols * 4 <= ADAMW_BLOCK_BYTES:
            tr = cand

    def body(w_ref, g_ref, m_ref, v_ref, *outs):
        go_ref, d_ref, m2_ref, v2_ref = outs[-4:]
        g = g_ref[...]
        go_ref[...] = g
        d_ref[...], m2_ref[...], v2_ref[...] = _adamw_math(w_ref[...], g, m_ref[...], v_ref[...])

    stack = pl.BlockSpec((None, tr, cols), lambda i: (l, i, 0))
    ins, specs, alias = [w, g, m, v], [stack, pl.BlockSpec((tr, cols), lambda i: (i, 0)), stack, stack], {}
    if prev is not None:
        ins += list(prev)
        specs += [ANY] * 4
        alias = {4 + i: i for i in range(4)}
    return _call(
        body, name="adamw", grid=(rows // tr,), in_specs=specs, out_specs=[stack] * 4,
        out_shape=[jax.ShapeDtypeStruct(w.shape, F32)] * 4, input_output_aliases=alias,
        compiler_params=_params(1))(*ins)


def _adamw_small(w, g, m, v):
    def body(w_ref, g_ref, m_ref, v_ref, d_ref, m2_ref, v2_ref):
        d_ref[...], m2_ref[...], v2_ref[...] = _adamw_math(w_ref[...], g_ref[...], m_ref[...], v_ref[...])

    spec = pl.BlockSpec(w.shape, lambda i: (0, 0))
    return _call(
        body, name="adamw_small", grid=(1,), in_specs=[spec] * 4, out_specs=[spec] * 3,
        out_shape=[jax.ShapeDtypeStruct(w.shape, F32)] * 3, compiler_params=_params(1))(w, g, m, v)


SMALL = ("norm_ffn1", "norm_mix", "sinks", "norm_out_sb", "norm_out_swa", "norm_ffn2", "rel_bias", "norm_final")
BIG = ("ffn1_gu", "ffn1_down", "w_in", "w_out", "ffn2_gu", "ffn2_down")


def _pack(parts):
    flat, n = [], 0
    for a in parts:
        a = a.reshape(-1).astype(F32)
        gap = -a.shape[0] % LANES
        flat += [a] + ([jnp.zeros((gap,), F32)] if gap else [])
        n += a.shape[0] + gap
    tail = -(n // LANES) % 8 * LANES
    return jnp.concatenate(flat + ([jnp.zeros((tail,), F32)] if tail else [])).reshape(-1, LANES)


def _unpack(packed, like):
    out, r = [], 0
    for a in like:
        n = math.prod(a.shape)
        nr = -(-n // LANES)
        out.append(packed[r:r + nr].reshape(-1)[:n].reshape(a.shape))
        r += nr
    return out


def _halved(a):
    k, r, cols = a.shape
    return a.reshape(k, 2, r // 2, cols)


def _weight_view(k, buf):
    full = buf.reshape(N_CHIPS, buf.shape[2] * 2, buf.shape[3])
    return full if k.endswith("_gu") else full.reshape(-1, D_MODEL)


def _grad_stack(k, g):
    if not k.endswith("_gu"):
        g = g.reshape(N_CHIPS, g.shape[0] // N_CHIPS, D_MODEL)
    return _halved(g)


def _empty_like_hbm(shape, dtype):
    return pltpu.with_memory_space_constraint(lax.empty(shape, dtype), pltpu.HBM)


def kernel(x, norm_ffn1, w_ffn1_gu, w_ffn1_down, norm_mix, w_in, sinks, norm_out_sb, norm_out_swa, w_out, norm_ffn2, w_ffn2_gu, w_ffn2_down, rel_bias, norm_final, loss_target, m_norm_ffn1, m_w_ffn1_gu, m_w_ffn1_down, m_norm_mix, m_w_in, m_sinks, m_norm_out_sb, m_norm_out_swa, m_w_out, m_norm_ffn2, m_w_ffn2_gu, m_w_ffn2_down, m_rel_bias, m_norm_final, v_norm_ffn1, v_w_ffn1_gu, v_w_ffn1_down, v_norm_mix, v_w_in, v_sinks, v_norm_out_sb, v_norm_out_swa, v_w_out, v_norm_ffn2, v_w_ffn2_gu, v_w_ffn2_down, v_rel_bias, v_norm_final):
    big_w = dict(ffn1_gu=w_ffn1_gu, ffn1_down=w_ffn1_down, w_in=w_in, w_out=w_out, ffn2_gu=w_ffn2_gu, ffn2_down=w_ffn2_down)
    big_m = dict(ffn1_gu=m_w_ffn1_gu, ffn1_down=m_w_ffn1_down, w_in=m_w_in, w_out=m_w_out, ffn2_gu=m_w_ffn2_gu, ffn2_down=m_w_ffn2_down)
    big_v = dict(ffn1_gu=v_w_ffn1_gu, ffn1_down=v_w_ffn1_down, w_in=v_w_in, w_out=v_w_out, ffn2_gu=v_w_ffn2_gu, ffn2_down=v_w_ffn2_down)
    small = dict(norm_ffn1=norm_ffn1, norm_mix=norm_mix, sinks=sinks, norm_out_sb=norm_out_sb, norm_out_swa=norm_out_swa,
                 norm_ffn2=norm_ffn2, rel_bias=rel_bias, norm_final=norm_final)
    small_m = dict(norm_ffn1=m_norm_ffn1, norm_mix=m_norm_mix, sinks=m_sinks, norm_out_sb=m_norm_out_sb,
                   norm_out_swa=m_norm_out_swa, norm_ffn2=m_norm_ffn2, rel_bias=m_rel_bias, norm_final=m_norm_final)
    small_v = dict(norm_ffn1=v_norm_ffn1, norm_mix=v_norm_mix, sinks=v_sinks, norm_out_sb=v_norm_out_sb,
                   norm_out_swa=v_norm_out_swa, norm_ffn2=v_norm_ffn2, rel_bias=v_rel_bias, norm_final=v_norm_final)
    for dct in (big_w, big_m, big_v):
        dct["w_in"] = jnp.swapaxes(dct["w_in"], 1, 2)
    _PREVIOUS[0] = None
    _, _, c, me = _place()
    cm = jnp.stack([c, me]).astype(jnp.int32)
    buckets = jnp.asarray(_bucket_table())
    ffn1, mix_in, rest = ("ffn1_gu", "ffn1_down"), ("w_in",), ("w_out", "ffn2_gu", "ffn2_down")

    def place(l, keys):
        return [_place_own(big_w[k], l, cm[1:]) for k in keys]

    def views(keys, bufs):
        return {k: _weight_view(k, b) for k, b in zip(keys, bufs)}

    def gather_start(tag, bufs):
        return _exchange_start(f"gather{tag}_ici_start", _plan_gather_ici, bufs, 3 * len(bufs))

    def gather_pass(tag, flight):
        return _exchange_pass(f"gather{tag}_pass", _plan_gather_ici, _plan_gather_d2d, flight[1], flight[0],
                              3 * len(flight[1]))

    def gather_done(tag, keys, flight):
        return views(keys, _exchange_wait(f"gather{tag}_d2d_wait", _plan_gather_d2d, flight[1], flight[0]))

    fly_gu0 = gather_start("0a", place(0, ffn1[:1]))
    fly_down0 = gather_start("0a2", place(0, ffn1[1:]))
    fly_in0 = gather_start("0b", place(0, mix_in))
    later = [place(l, keys) for l in range(DEPTH) for keys in ((rest,) if l == 0 else (ffn1, mix_in, rest))]
    fly_rest0, fly_ffn1, fly_in1, fly_rest1 = _exchange_start_groups(
        "gather_later_ici_start", _plan_gather_ici, [(bufs, 3 * len(bufs)) for bufs in later])
    bias = _bias_table(rel_bias, buckets)
    n1 = _norm_cast(x[0], _row(norm_ffn1[0]))
    w0 = gather_done("0a", ffn1[:1], gather_pass("0a", fly_gu0))

    s0 = _fwd_ffn1_gu(x[0], n1, w0)
    w0.update(gather_done("0a2", ffn1[1:], gather_pass("0a2", fly_down0)))
    fly_in0 = gather_pass("0b", fly_in0)
    _fwd_ffn1_down(s0, w0, small, 0)
    w0.update(gather_done("0b", mix_in, fly_in0))
    _fwd_proj_sb(s0, w0)
    fly_rest0 = gather_pass("0c", fly_rest0)
    _fwd_swa(s0, small, 0, bias)
    w0.update(gather_done("0c", rest, fly_rest0))
    _fwd_out_gu2(s0, w0, small, 0)
    fly_ffn1 = gather_pass("1a", fly_ffn1)
    h, n1 = _fwd_ffn2_down(s0, w0, _row(norm_ffn1[1]))
    w1 = gather_done("1a", ffn1, fly_ffn1)
    fly_in1 = gather_pass("1b", fly_in1)
    s1 = _fwd_ffn1(h, n1, w1, small, 1)
    w1.update(gather_done("1b", mix_in, fly_in1))
    _fwd_proj_sb(s1, w1)
    fly_rest1 = gather_pass("1c", fly_rest1)
    _fwd_swa(s1, small, 1, bias)
    w1.update(gather_done("1c", rest, fly_rest1))
    h, _ = _fwd_out_ffn2(s1, w1, small, 1, _row(norm_final))
    dh32, dh16, dg_final, loss_row = _loss_head(h, _row(norm_final), loss_target[0])
    dh = (dh32, dh16)

    def landing(stacks, lead, dtype):
        return [_empty_like_hbm((lead,) + a.shape[2:], dtype) for a in stacks]

    def reduce_begin(tag, keys, gw):
        stacks = [_grad_stack(k, gw[k]) for k in keys]
        flight = _exchange_start(f"grad{tag}_sibling_start", _plan_grad_sibling,
                                 stacks + landing(stacks, N_CHIPS, F32), len(keys))
        return dict(tag=tag, keys=keys, stacks=stacks, flight=flight)

    def reduce_chips(st):
        n, (sems, bufs) = len(st["keys"]), st["flight"]
        bufs = _exchange_wait(f"grad{st['tag']}_sibling_wait", _plan_grad_sibling, bufs, sems)
        st["own"] = list(zip(bufs[:n], bufs[n:]))
        st["flight"] = _exchange_start(f"grad{st['tag']}_chips_start", _plan_grad_chips,
                                       [_chip_sum(g, z, cm) for g, z in st["own"]] + landing(st["stacks"], 3, BF16),
                                       3 * n)

    def reduce_halves(st):
        n, (sems, bufs) = len(st["keys"]), st["flight"]
        bufs = _exchange_wait(f"grad{st['tag']}_chips_wait", _plan_grad_chips, bufs, sems)
        halves = [_total_sum(g, x, z, cm) for (g, x), z in zip(st["own"], bufs[n:])]
        st["flight"] = _exchange_start(f"grad{st['tag']}_halves_start", _plan_grad_halves, halves, n)

    def reduce_end(st):
        sems, bufs = st["flight"]
        bufs = _exchange_wait(f"grad{st['tag']}_halves_wait", _plan_grad_halves, bufs, sems)
        return {k: b.reshape(big_w[k].shape[1:]) for k, b in zip(st["keys"], bufs)}

    def adamw(reduced, l, prev):
        return {k: _adamw_layer(big_w[k], g, big_m[k], big_v[k], l, None if prev is None else prev[k])
                for k, g in reduced.items()}

    gsm = [dict() for _ in range(DEPTH)]
    dbias = jnp.zeros((8, BLK, 2 * BLK), F32)
    dh, gw1, gs = _bwd_ffn(dh, s1, w1, small, 1, 2)
    gsm[1].update(gs)
    dh, gw, gs, dbias = _bwd_mix(dh, s1, w1, small, 1, bias, dbias)
    gw1.update(gw)
    gsm[1].update(gs)
    dh, gw, gs = _bwd_ffn(dh, s1, w1, small, 1, 1)
    gw1.update(gw)
    gsm[1].update(gs)

    red1 = reduce_begin("1", BIG, gw1)
    dh, gw0, gs = _bwd_ffn(dh, s0, w0, small, 0, 2)
    gsm[0].update(gs)
    reduce_chips(red1)
    dh, gw, gs, dbias = _bwd_mix(dh, s0, w0, small, 0, bias, dbias)
    gw0.update(gw)
    gsm[0].update(gs)
    red0a = reduce_begin("0a", ("ffn2_gu", "ffn2_down", "w_out", "w_in"), gw0)
    reduce_halves(red1)
    dgu = _bwd_ffn_dact(dh, s0, w0, 1)
    reduce_chips(red0a)
    dh, gw, gs = _bwd_ffn_rest(dh, dgu, s0, w0, small, 0, 1)
    gsm[0].update(gs)
    red0b = reduce_begin("0b", ffn1, gw)
    reduced1 = reduce_end(red1)
    stacks = adamw({k: reduced1[k] for k in ffn1}, 1, None)

    gsmall = {k: jnp.stack([gsm[l][k].reshape(-1) for l in range(DEPTH)]) for k in gsm[0]}
    gsmall["rel_bias"] = jnp.transpose(_bias_grad(dbias, buckets)[:, :N_BUCKETS])
    gsmall["norm_final"] = dg_final.reshape(-1)
    small_like = [small[k] for k in SMALL]
    pk = lambda dct: _pack([dct[k] for k in SMALL])
    red = _small_allreduce(_pack([gsmall[k] for k in SMALL] + [loss_row[0, :1]]))
    gs = _unpack(red, small_like + [loss_row[0, :1]])
    loss = gs[-1][0]
    gs = dict(zip(SMALL, gs[:-1]))

    ffn2 = ("ffn2_gu", "ffn2_down")
    reduce_chips(red0b)
    stacks.update(adamw({k: reduced1[k] for k in ("w_in", "w_out")}, 1, None))
    reduce_halves(red0a)
    stacks.update(adamw({k: reduced1[k] for k in ffn2}, 1, None))
    dlt, m2, v2 = _adamw_small(pk(small), pk(gs), pk(small_m), pk(small_v))
    reduced0a = reduce_end(red0a)
    stacks.update(adamw({k: reduced0a[k] for k in ffn2}, 0, stacks))
    reduce_halves(red0b)
    stacks.update(adamw({k: reduced0a[k] for k in ("w_in", "w_out")}, 0, stacks))
    stacks.update(adamw(reduce_end(red0b), 0, stacks))

    out_g, out_d, out_m, out_v = {}, {}, {}, {}
    for k in BIG:
        out_g[k], out_d[k], out_m[k], out_v[k] = [jnp.swapaxes(a, 1, 2) if k == "w_in" else a for a in stacks[k]]
    for dst, packed in ((out_d, dlt), (out_m, m2), (out_v, v2)):
        dst.update(zip(SMALL, _unpack(packed, small_like)))
    out_g.update(gs)

    order = ("norm_ffn1", "ffn1_gu", "ffn1_down", "norm_mix", "w_in", "sinks", "norm_out_sb", "norm_out_swa", "w_out",
             "norm_ffn2", "ffn2_gu", "ffn2_down", "rel_bias", "norm_final")
    return (loss, dh[0].reshape(x.shape), *[out_g[k] for k in order], *[out_d[k] for k in order],
            *[out_m[k] for k in order], *[out_v[k] for k in order])
```

```python
import math

import numpy as np
import jax
import jax.numpy as jnp
from jax import lax
from jax.experimental import pallas as pl
from jax.experimental.pallas import tpu as pltpu

F32 = jnp.float32
BF16 = jnp.bfloat16

D_MODEL = 1024
DEPTH = 2
HEAD_DIM = 64
BLK = 128
N_BUCKETS = 32
MAX_DISTANCE = 128
D_FF = 2816
EPS = 1e-6
NEG_INF = -1e30
SB_W = 512
SWA_W = 512
KV_W = 128
IN_W = 2304
SCALE = HEAD_DIM ** -0.5
N_CHIPS = 4
FS = 2 * D_FF // N_CHIPS
LANES = 128
V7X_VMEM_LIMIT = 56 * 2 ** 20
TM = 512
SLAB_BLOCK_BYTES = 6 * 2 ** 20
ADAMW_BLOCK_BYTES = 2 ** 21
SB_KT = 512
SWA_G = 4

ADAM_LR = 0.001
ADAM_B1 = 0.9
ADAM_B2 = 0.999
ADAM_EPS = 1e-08
ADAM_WD = 0.01
ADAM_STEP = 10

MESH = pl.DeviceIdType.MESH
ANY = pl.BlockSpec(memory_space=pl.ANY)
HBM = pl.BlockSpec(memory_space=pltpu.HBM)
SEM = pl.BlockSpec(memory_space=pltpu.SEMAPHORE)
EFFECT = pltpu.SideEffectType.DATAFLOW_SIDE_EFFECTING


def _params(n_grid):
    return pltpu.CompilerParams(dimension_semantics=("arbitrary",) * n_grid, vmem_limit_bytes=V7X_VMEM_LIMIT)


_PREVIOUS = [None]


def _call(body, *, name, in_specs, out_specs, out_shape, grid=(), num_scalar_prefetch=0, scratch_shapes=(),
          input_output_aliases=None, compiler_params=None, hbm_args=0):
    n_in = len(in_specs)

    def run(*args):
        dep = _PREVIOUS[0]
        if any(dep is a for a in args):
            dep = None
        args = [pltpu.with_memory_space_constraint(a, pltpu.HBM) if i < hbm_args else a for i, a in enumerate(args)]
        specs = list(in_specs) + ([ANY] if dep is not None else [])
        k = num_scalar_prefetch + n_in
        fn = body if dep is None else (lambda *refs: body(*refs[:k], *refs[k + 1:]))
        if num_scalar_prefetch:
            shape = dict(grid_spec=pltpu.PrefetchScalarGridSpec(
                num_scalar_prefetch=num_scalar_prefetch, grid=grid, in_specs=specs, out_specs=out_specs,
                scratch_shapes=scratch_shapes))
        else:
            shape = dict(grid=grid, in_specs=specs, out_specs=out_specs, scratch_shapes=scratch_shapes)
        out = pl.pallas_call(fn, name=name, out_shape=out_shape, input_output_aliases=input_output_aliases or {},
                             compiler_params=compiler_params, **shape)(*args, *([] if dep is None else [dep]))
        _PREVIOUS[0] = jax.tree.leaves(out)[-1]
        return out

    return run


def _dot(a, b):
    return jnp.dot(a, b, preferred_element_type=F32)


def _dot_nt(a, b):
    return lax.dot_general(a, b, (((1,), (1,)), ((), ())), preferred_element_type=F32)


def _dot_tn(a, b):
    return lax.dot_general(a, b, (((0,), (0,)), ((), ())), preferred_element_type=F32)


def _rms_fwd(x, g):
    r = lax.rsqrt(jnp.mean(x * x, axis=-1, keepdims=True) + EPS)
    xh = x * r
    return xh * g, xh, r


def _rms_bwd(dy, xh, r, g):
    u = dy * g
    dx = r * (u - xh * jnp.mean(u * xh, axis=-1, keepdims=True))
    dg = jnp.sum(dy * xh, axis=0, keepdims=True)
    return dx, dg


def _softplus(z):
    neg_abs = lax.bitcast_convert_type(lax.bitcast_convert_type(z, jnp.int32) | jnp.int32(-2 ** 31), F32)
    sp = jnp.maximum(z, 0.0) + jnp.log(1.0 + jnp.exp(neg_abs))
    return sp, z - sp


def _norm_cast(h, g):
    t, w = h.shape

    def body(h_ref, g_ref, n_ref):
        y, _, _ = _rms_fwd(h_ref[...], g_ref[...])
        n_ref[...] = y.astype(BF16)

    return _call(
        body, name="norm_cast", grid=(t // TM,),
        in_specs=[pl.BlockSpec((TM, w), lambda i: (i, 0)), pl.BlockSpec((1, w), lambda i: (0, 0))],
        out_specs=pl.BlockSpec((TM, w), lambda i: (i, 0)),
        out_shape=jax.ShapeDtypeStruct((t, w), BF16), compiler_params=_params(1))(h, g)


def _ffn_gu(n, wgu):
    t, d = n.shape

    def body(n_ref, wg_ref, wu_ref, gu_ref, act_ref):
        x = n_ref[...]
        g = _dot(x, wg_ref[...])
        u = _dot(x, wu_ref[...])
        sig = jax.nn.sigmoid(g)
        silu = g * sig
        gu_ref[0] = (u * (sig + silu * (1.0 - sig))).astype(BF16)
        gu_ref[1] = silu.astype(BF16)
        act_ref[...] = (silu * u).astype(BF16)

    return _call(
        body, name="ffn_gu", grid=(2, t // TM),
        in_specs=[pl.BlockSpec((TM, d), lambda j, i: (i, 0)),
                  pl.BlockSpec((None, d, FS), lambda j, i: (j, 0, 0)),
                  pl.BlockSpec((None, d, FS), lambda j, i: (j + 2, 0, 0))],
        out_specs=[pl.BlockSpec((2, TM, FS), lambda j, i: (0, i, j)), pl.BlockSpec((TM, FS), lambda j, i: (i, j))],
        out_shape=[jax.ShapeDtypeStruct((2, t, D_FF), BF16), jax.ShapeDtypeStruct((t, D_FF), BF16)],
        compiler_params=_params(2))(n, wgu, wgu)


def _down_res(act, wdn, h, g_next):
    t, f = act.shape
    d = h.shape[1]

    def body(a_ref, w_ref, h_ref, g_ref, o_ref, n_ref):
        out = h_ref[...] + 0.5 * _dot(a_ref[...], w_ref[...])
        o_ref[...] = out
        n_ref[...] = _rms_fwd(out, g_ref[...])[0].astype(BF16)

    row = pl.BlockSpec((TM, d), lambda i: (i, 0))
    return _call(
        body, name="down_res", grid=(t // TM,),
        in_specs=[pl.BlockSpec((TM, f), lambda i: (i, 0)), pl.BlockSpec((f, d), lambda i: (0, 0)), row,
                  pl.BlockSpec((1, d), lambda i: (0, 0))],
        out_specs=[row, row],
        out_shape=[jax.ShapeDtypeStruct((t, d), F32), jax.ShapeDtypeStruct((t, d), BF16)],
        compiler_params=_params(1))(act, wdn, h, g_next)


def _proj(n, w_in_t):
    t, d = n.shape
    w = w_in_t.shape[0]

    def body(n_ref, w_ref, o_ref):
        o_ref[...] = _dot_nt(n_ref[...], w_ref[...]).astype(BF16)

    return _call(
        body, name="proj", grid=(t // TM,),
        in_specs=[pl.BlockSpec((TM, d), lambda i: (i, 0)), pl.BlockSpec((w, d), lambda i: (0, 0))],
        out_specs=pl.BlockSpec((TM, w), lambda i: (i, 0)),
        out_shape=jax.ShapeDtypeStruct((t, w), BF16), compiler_params=_params(1))(n, w_in_t)


def _out_res(o_sb, o_sw, g_sb, g_sw, w_out, h, g_next):
    t, d = h.shape

    def body(a_ref, b_ref, ga_ref, gb_ref, w_ref, h_ref, g_ref, o_ref, mix_ref, n_ref):
        ya, _, _ = _rms_fwd(a_ref[...], ga_ref[...])
        yb, _, _ = _rms_fwd(b_ref[...], gb_ref[...])
        mixed = jnp.concatenate([ya.astype(BF16), yb.astype(BF16)], axis=1)
        mix_ref[...] = mixed
        out = h_ref[...] + _dot(mixed, w_ref[...])
        o_ref[...] = out
        n_ref[...] = _rms_fwd(out, g_ref[...])[0].astype(BF16)

    row = pl.BlockSpec((TM, d), lambda i: (i, 0))
    return _call(
        body, name="out_res", grid=(t // TM,),
        in_specs=[pl.BlockSpec((TM, SB_W), lambda i: (i, 0)), pl.BlockSpec((TM, SWA_W), lambda i: (i, 0)),
                  pl.BlockSpec((1, SB_W), lambda i: (0, 0)), pl.BlockSpec((1, SWA_W), lambda i: (0, 0)),
                  pl.BlockSpec((d, d), lambda i: (0, 0)), row, pl.BlockSpec((1, d), lambda i: (0, 0))],
        out_specs=[row, row, row],
        out_shape=[jax.ShapeDtypeStruct((t, d), F32), jax.ShapeDtypeStruct((t, d), BF16),
                   jax.ShapeDtypeStruct((t, d), BF16)],
        compiler_params=_params(1))(o_sb, o_sw, g_sb, g_sw, w_out, h, g_next)


def _loss_head(h, g, tgt):
    t, d = h.shape

    def body(h_ref, g_ref, t_ref, dh_ref, dhb_ref, dg_ref, loss_ref):
        @pl.when(pl.program_id(0) == 0)
        def _():
            dg_ref[...] = jnp.zeros_like(dg_ref)
            loss_ref[...] = jnp.zeros_like(loss_ref)

        gg = g_ref[...]
        y, xh, r = _rms_fwd(h_ref[...], gg)
        err = y - t_ref[...]
        part = 0.5 * jnp.sum(jnp.sum(err * err, axis=1, keepdims=True) / d, axis=0, keepdims=True)
        loss_ref[...] += jnp.broadcast_to(part, loss_ref.shape)
        dx, dg = _rms_bwd(err / d, xh, r, gg)
        dh_ref[...] = dx
        dhb_ref[...] = dx.astype(BF16)
        dg_ref[...] += dg

    row = pl.BlockSpec((TM, d), lambda i: (i, 0))
    return _call(
        body, name="loss_head", grid=(t // TM,),
        in_specs=[row, pl.BlockSpec((1, d), lambda i: (0, 0)), row],
        out_specs=[row, row, pl.BlockSpec((1, d), lambda i: (0, 0)), pl.BlockSpec((1, LANES), lambda i: (0, 0))],
        out_shape=[jax.ShapeDtypeStruct((t, d), F32), jax.ShapeDtypeStruct((t, d), BF16),
                   jax.ShapeDtypeStruct((1, d), F32), jax.ShapeDtypeStruct((1, LANES), F32)],
        compiler_params=_params(1))(h, g, tgt)


def _ffn_dact(dh, wdn, gu):
    t, d = dh.shape
    tm = TM

    def body(dh_ref, w_ref, gu_ref, o_ref):
        da = 0.5 * _dot_nt(dh_ref[...].astype(BF16), w_ref[...])
        o_ref[0] = (da * gu_ref[0].astype(F32)).astype(BF16)
        o_ref[1] = (da * gu_ref[1].astype(F32)).astype(BF16)

    return _call(
        body, name="ffn_dact", grid=(2, t // tm),
        in_specs=[pl.BlockSpec((tm, d), lambda j, i: (i, 0)), pl.BlockSpec((FS, d), lambda j, i: (j, 0)),
                  pl.BlockSpec((2, tm, FS), lambda j, i: (0, i, j))],
        out_specs=pl.BlockSpec((2, tm, FS), lambda j, i: (0, i, j)),
        out_shape=jax.ShapeDtypeStruct((2, t, D_FF), BF16), compiler_params=_params(2))(dh, wdn, gu)


def _dn_norm_bwd(a, a_spec, w, w_spec, nk, dh, h_in, g, w_transposed=False, tm=TM):
    t, d = dh.shape
    mm = _dot if w_transposed else _dot_nt

    def body(a_ref, w_ref, dh_ref, h_ref, g_ref, o_ref, ob_ref, dg_ref, acc_ref):
        i, k = pl.program_id(0), pl.program_id(1)

        if nk > 1:
            @pl.when(k == 0)
            def _():
                acc_ref[...] = mm(a_ref[...], w_ref[...])

            @pl.when((k > 0) & (k < nk - 1))
            def _():
                acc_ref[...] += mm(a_ref[...], w_ref[...])

        @pl.when(k == nk - 1)
        def _():
            gg = g_ref[...]
            dg = jnp.zeros_like(gg)
            for rows in (slice(r, r + TM // 2) for r in range(0, tm, TM // 2)):
                dn = mm(a_ref[rows, :], w_ref[...])
                if nk > 1:
                    dn = dn + acc_ref[rows, :]
                _, xh, r = _rms_fwd(h_ref[rows, :], gg)
                dx, dg_rows = _rms_bwd(dn, xh, r, gg)
                out = dh_ref[rows, :] + dx
                o_ref[rows, :] = out
                ob_ref[rows, :] = out.astype(BF16)
                dg = dg + dg_rows

            @pl.when(i == 0)
            def _():
                dg_ref[...] = dg

            @pl.when(i > 0)
            def _():
                dg_ref[...] += dg

    row = pl.BlockSpec((tm, d), lambda i, k: (i, 0))
    return _call(
        body, name="dn_norm_bwd", grid=(t // tm, nk),
        in_specs=[a_spec, w_spec, row, row, pl.BlockSpec((1, d), lambda i, k: (0, 0))],
        out_specs=[row, row, pl.BlockSpec((1, d), lambda i, k: (0, 0))],
        out_shape=[jax.ShapeDtypeStruct((t, d), F32), jax.ShapeDtypeStruct((t, d), BF16),
                   jax.ShapeDtypeStruct((1, d), F32)],
        scratch_shapes=[pltpu.VMEM((tm, d), F32)], compiler_params=_params(2))(a, w, dh, h_in, g)


def _ffn_dn(dgu, wgu, dh, h_in, g):
    d = dh.shape[1]
    tm = 2 * TM
    return _dn_norm_bwd(
        dgu, pl.BlockSpec((None, tm, FS), lambda i, k: (k // 2, i, k % 2)),
        wgu, pl.BlockSpec((None, d, FS), lambda i, k: (k, 0, 0)), N_CHIPS, dh, h_in, g, tm=tm)


def _mix_dn(dproj, w_in_t, dh, h_in, g):
    d = dh.shape[1]
    w = dproj.shape[1]
    return _dn_norm_bwd(
        dproj, pl.BlockSpec((TM, w), lambda i, k: (i, 0)),
        w_in_t, pl.BlockSpec((w, d), lambda i, k: (0, 0)), 1, dh, h_in, g, w_transposed=True)


def _dmixed(dh, w_out, o_sb, o_sw, g_sb, g_sw):
    t, d = dh.shape

    def body(dh_ref, w_ref, a_ref, b_ref, ga_ref, gb_ref, o_ref, dga_ref, dgb_ref):
        i = pl.program_id(0)
        dm = _dot_nt(dh_ref[...].astype(BF16), w_ref[...])
        _, xa, ra = _rms_fwd(a_ref[...], ga_ref[...])
        _, xb, rb = _rms_fwd(b_ref[...], gb_ref[...])
        da, dga = _rms_bwd(dm[:, :SB_W], xa, ra, ga_ref[...])
        db, dgb = _rms_bwd(dm[:, SB_W:], xb, rb, gb_ref[...])
        o_ref[...] = jnp.concatenate([da.astype(BF16), db.astype(BF16)], axis=1)

        @pl.when(i == 0)
        def _():
            dga_ref[...] = dga
            dgb_ref[...] = dgb

        @pl.when(i > 0)
        def _():
            dga_ref[...] += dga
            dgb_ref[...] += dgb

    return _call(
        body, name="dmixed", grid=(t // TM,),
        in_specs=[pl.BlockSpec((TM, d), lambda i: (i, 0)), pl.BlockSpec((d, d), lambda i: (0, 0)),
                  pl.BlockSpec((TM, SB_W), lambda i: (i, 0)), pl.BlockSpec((TM, SWA_W), lambda i: (i, 0)),
                  pl.BlockSpec((1, SB_W), lambda i: (0, 0)), pl.BlockSpec((1, SWA_W), lambda i: (0, 0))],
        out_specs=[pl.BlockSpec((TM, d), lambda i: (i, 0)), pl.BlockSpec((1, SB_W), lambda i: (0, 0)),
                   pl.BlockSpec((1, SWA_W), lambda i: (0, 0))],
        out_shape=[jax.ShapeDtypeStruct((t, d), BF16), jax.ShapeDtypeStruct((1, SB_W), F32),
                   jax.ShapeDtypeStruct((1, SWA_W), F32)],
        compiler_params=_params(1))(dh, w_out, o_sb, o_sw, g_sb, g_sw)


def _wgrad(name, a, a_spec, b, b_spec, grid, out_shape, out_spec, scale):
    def body(a_ref, b_ref, o_ref):
        r = _dot_tn(a_ref[...], b_ref[...].astype(BF16))
        o_ref[...] = r if scale == 1.0 else scale * r

    return _call(
        body, name=name, grid=grid, in_specs=[a_spec, b_spec], out_specs=out_spec,
        out_shape=jax.ShapeDtypeStruct(out_shape, F32), compiler_params=_params(len(grid)))(a, b)


def _wgrad_gu(n, dgu):
    t, d = n.shape
    return _wgrad(
        "wgrad_gu", n, pl.BlockSpec((t, TM), lambda s, r: (0, r)),
        dgu, pl.BlockSpec((None, t, FS), lambda s, r: (s // 2, 0, s % 2)), (N_CHIPS, d // TM),
        (N_CHIPS, d, FS), pl.BlockSpec((None, TM, FS), lambda s, r: (s, r, 0)), 1.0)


def _wgrad_down(act, dh):
    t, d = dh.shape
    return _wgrad(
        "wgrad_down", act, pl.BlockSpec((t, FS), lambda s: (0, s)), dh, pl.BlockSpec((t, d), lambda s: (0, 0)),
        (2,), (D_FF, d), pl.BlockSpec((FS, d), lambda s: (s, 0)), 0.5)


def _wgrad_out(mixed, dh):
    t, d = dh.shape
    return _wgrad(
        "wgrad_out", mixed, pl.BlockSpec((t, TM), lambda s: (0, s)), dh, pl.BlockSpec((t, d), lambda s: (0, 0)),
        (d // TM,), (d, d), pl.BlockSpec((TM, d), lambda s: (s, 0)), 1.0)


def _wgrad_in(n, dproj):
    t, d = n.shape
    w = dproj.shape[1]
    tw = w // 3
    return _wgrad(
        "wgrad_in", dproj, pl.BlockSpec((t, tw), lambda s: (0, s)), n, pl.BlockSpec((t, d), lambda s: (0, 0)),
        (3,), (w, d), pl.BlockSpec((tw, d), lambda s: (s, 0)), 1.0)


def _tri(rel):
    row = lax.broadcasted_iota(jnp.int32, (BLK, BLK), 0)
    col = lax.broadcasted_iota(jnp.int32, (BLK, BLK), 1)
    m = rel(row, col).astype(BF16)
    return jnp.concatenate([m, m], axis=0)


def _scan_dot(x, tri2):
    hi = x.astype(BF16)
    lo = (x - hi.astype(F32)).astype(BF16)
    return _dot(jnp.concatenate([hi, lo], axis=1), tri2)


def _head_masks():
    lane = lax.broadcasted_iota(jnp.int32, (1, LANES), 1)
    return [lane < HEAD_DIM, lane >= HEAD_DIM]


SB_PAIRS = 2
SB_ROWS = 2 * SB_PAIRS * BLK


def _sb_causal():
    row = lax.broadcasted_iota(jnp.int32, (SB_ROWS, BLK), 0) & (BLK - 1)
    return lax.broadcasted_iota(jnp.int32, (SB_ROWS, BLK), 1) < row


def _sb_mask_last(x, causal):
    own = jnp.where(causal, x[:, -BLK:], 0.0)
    return own if x.shape[1] == BLK else jnp.concatenate([x[:, :-BLK], own], axis=1)


def _sb_stack(x, hm):
    return jnp.concatenate([jnp.where(m, x[:, p * LANES:(p + 1) * LANES], jnp.zeros((BLK, LANES), x.dtype))
                            for p in range(SB_PAIRS) for m in hm], axis=0)


def _sb_unstack(y, hm):
    return jnp.concatenate([jnp.where(hm[0], y[2 * p * BLK:(2 * p + 1) * BLK], y[(2 * p + 1) * BLK:(2 * p + 2) * BLK])
                            for p in range(SB_PAIRS)], axis=1)


def _sb_pairs():
    return [(slice(2 * p * BLK, (2 * p + 2) * BLK), slice(p * LANES, (p + 1) * LANES)) for p in range(SB_PAIRS)]


def _sb_fwd(proj):
    t = proj.shape[0]
    nb = SB_KT // BLK
    wide = SB_PAIRS * LANES

    def body(q_ref, k_ref, v_ref, o_ref, tot_ref):
        hm = _head_masks()
        causal = _sb_causal()
        pairs = _sb_pairs()
        after = _tri(lambda r, c: r > c)

        def tile(qh, start, n_blk, carry, acc, own):
            ks = pl.ds(pl.multiple_of(start, BLK), n_blk * BLK)
            z = jnp.concatenate([_dot_nt(qh[rows], k_ref[ks, lanes]) for rows, lanes in pairs], axis=0)
            sp, zs = _softplus(z)
            spm = _sb_mask_last(sp, causal) if own else sp
            sufs = [None] * n_blk
            for b in reversed(range(n_blk)):
                blk = spm[:, b * BLK:(b + 1) * BLK]
                sufs[b] = carry + _scan_dot(blk, after)
                carry = carry + jnp.sum(blk, axis=1, keepdims=True)
            w = jnp.exp(zs - jnp.concatenate(sufs, axis=1))
            wb = (_sb_mask_last(w, causal) if own else w).astype(BF16)
            return carry, acc + jnp.concatenate([_dot(wb[rows], v_ref[ks, lanes]) for rows, lanes in pairs], axis=0)

        def qblock(g, j):
            qs = pl.ds(pl.multiple_of(g * SB_KT + j * BLK, BLK), BLK)
            qh = _sb_stack(q_ref[qs, :] * SCALE, hm)
            c0 = tile(qh, g * SB_KT, j + 1, jnp.zeros((SB_ROWS, 1), F32), jnp.zeros((SB_ROWS, LANES), F32), True)
            carry, acc = lax.fori_loop(0, g, lambda n, c: tile(qh, (g - 1 - n) * SB_KT, nb, c[0], c[1], False), c0)
            o_ref[qs, :] = _sb_unstack(acc, hm)
            for h in range(2 * SB_PAIRS):
                tot_ref[h, qs, :] = carry[h * BLK:(h + 1) * BLK]

        def group(g, _):
            for j in range(nb):
                qblock(g, j)
            return 0

        lax.fori_loop(0, t // SB_KT, group, 0)

    col_blk = lambda off: pl.BlockSpec((t, wide), lambda g: (0, off + g))
    n_steps = SB_W // wide
    return _call(
        body, name="sb_fwd", grid=(n_steps,), in_specs=[col_blk(0), col_blk(n_steps), col_blk(2 * n_steps)],
        out_specs=[col_blk(0), pl.BlockSpec((2 * SB_PAIRS, t, 1), lambda g: (g, 0, 0))],
        out_shape=[jax.ShapeDtypeStruct((t, SB_W), F32), jax.ShapeDtypeStruct((8, t, 1), F32)],
        compiler_params=_params(1))(proj, proj, proj)


def _sb_bwd(proj, d_o, tot):
    t = proj.shape[0]
    nb = SB_KT // BLK
    wide = SB_PAIRS * LANES

    def body(q_ref, k_ref, v_ref, do_ref, tot_ref, dq_ref, dk_ref, dv_ref, dk_acc, dv_acc):
        hm = _head_masks()
        causal = _sb_causal()
        pairs = _sb_pairs()
        before = _tri(lambda r, c: r < c)
        upto = _tri(lambda r, c: r <= c)
        dk_acc[...] = jnp.zeros_like(dk_acc)
        dv_acc[...] = jnp.zeros_like(dv_acc)

        def tile(qh, doh, tt, start, n_blk, pre, ecum, dq, own):
            ks = pl.ds(pl.multiple_of(start, BLK), n_blk * BLK)
            k = k_ref[ks, :]
            v = v_ref[ks, :]
            z = jnp.concatenate([_dot_nt(qh[rows], k[:, lanes]) for rows, lanes in pairs], axis=0)
            sp, zs = _softplus(z)
            spm = _sb_mask_last(sp, causal) if own else sp
            pres = []
            for b in range(n_blk):
                blk = spm[:, b * BLK:(b + 1) * BLK]
                pres.append(pre + _scan_dot(blk, before))
                pre = pre + jnp.sum(blk, axis=1, keepdims=True)
            logw = z - (tt - jnp.concatenate(pres, axis=1))
            if own:
                logw = jnp.minimum(logw, 0.0)
            w = jnp.exp(logw)
            if own:
                w = _sb_mask_last(w, causal)
            e = w * jnp.concatenate([_dot_nt(doh[rows], v[:, lanes]) for rows, lanes in pairs], axis=0)
            incs = []
            for b in range(n_blk):
                blk = e[:, b * BLK:(b + 1) * BLK]
                incs.append(ecum + _scan_dot(blk, upto))
                ecum = ecum + jnp.sum(blk, axis=1, keepdims=True)
            dz = e - jnp.exp(zs) * jnp.concatenate(incs, axis=1)
            if own:
                dz = _sb_mask_last(dz, causal)
            dzb = dz.astype(BF16)
            wb = w.astype(BF16)
            for rows, lanes in pairs:
                dk_acc[ks, lanes] += _dot_tn(dzb[rows], qh[rows])
                dv_acc[ks, lanes] += _dot_tn(wb[rows], doh[rows])
            return pre, ecum, dq + jnp.concatenate([_dot(dzb[rows], k[:, lanes]) for rows, lanes in pairs], axis=0)

        def qblock(g, j):
            qs = pl.ds(pl.multiple_of(g * SB_KT + j * BLK, BLK), BLK)
            qh = _sb_stack(q_ref[qs, :] * SCALE, hm)
            doh = _sb_stack(do_ref[qs, :], hm)
            tt = jnp.concatenate([tot_ref[h, qs, :] for h in range(2 * SB_PAIRS)], axis=0)
            c0 = (jnp.zeros((SB_ROWS, 1), F32), jnp.zeros((SB_ROWS, 1), F32), jnp.zeros((SB_ROWS, LANES), F32))
            c = lax.fori_loop(0, g, lambda kt, c: tile(qh, doh, tt, kt * SB_KT, nb, c[0], c[1], c[2], False), c0)
            dq = tile(qh, doh, tt, g * SB_KT, j + 1, c[0], c[1], c[2], True)[2]
            dq_ref[qs, :] = (_sb_unstack(dq, hm) * SCALE).astype(BF16)

        def group(g, _):
            for j in range(nb):
                qblock(g, j)
            return 0

        lax.fori_loop(0, t // SB_KT, group, 0)
        dk_ref[...] = dk_acc[...].astype(BF16)
        dv_ref[...] = dv_acc[...].astype(BF16)

    col_blk = lambda off: pl.BlockSpec((t, wide), lambda g: (0, off + g))
    n_steps = SB_W // wide
    out = jax.ShapeDtypeStruct((t, SB_W), BF16)
    return _call(
        body, name="sb_bwd", grid=(n_steps,),
        in_specs=[col_blk(0), col_blk(n_steps), col_blk(2 * n_steps), col_blk(0),
                  pl.BlockSpec((2 * SB_PAIRS, t, 1), lambda g: (g, 0, 0))],
        out_specs=[col_blk(0), col_blk(0), col_blk(0)], out_shape=[out, out, out],
        scratch_shapes=[pltpu.VMEM((t, wide), F32), pltpu.VMEM((t, wide), F32)],
        compiler_params=_params(1))(proj, proj, proj, d_o, tot)


def _bucket_table():
    a = np.arange(BLK)[:, None]
    c = np.arange(2 * BLK)[None, :]
    dist = np.maximum(BLK + a - c, 0)
    max_exact = N_BUCKETS // 2
    dd = np.maximum(dist, 1).astype(np.float32)
    large = max_exact + (np.log(dd / max_exact) / math.log(MAX_DISTANCE / max_exact)
                         * (N_BUCKETS - max_exact)).astype(np.int32)
    large = np.minimum(large, N_BUCKETS - 1)
    return np.where(dist < max_exact, dist, large).astype(np.int32)


SWA_H = 8


def _swa_band_masks():
    row = lax.broadcasted_iota(jnp.int32, (SWA_H * BLK, 2 * BLK), 0) & (BLK - 1)
    col = lax.broadcasted_iota(jnp.int32, (SWA_H * BLK, 2 * BLK), 1)
    own = lax.broadcasted_iota(jnp.int32, (SWA_H * BLK, BLK), 1) <= (
        lax.broadcasted_iota(jnp.int32, (SWA_H * BLK, BLK), 0) & (BLK - 1))
    return (col > row) & ((col < BLK) | (col - BLK <= row)), own


def _swa_stack(ref, qs, hm, scale):
    parts = []
    for hq in range(SWA_H):
        kvh = hq // SWA_G
        x = ref[qs, (hq // 2) * LANES:(hq // 2 + 1) * LANES].astype(F32)
        if hq % 2 != kvh:
            x = pltpu.roll(x, HEAD_DIM, 1)
        parts.append(jnp.where(hm[kvh], x * scale, 0.0).astype(BF16))
    return jnp.concatenate(parts, axis=0)


def _swa_unstack(x8, hm):
    heads = []
    for hq in range(SWA_H):
        x = x8[hq * BLK:(hq + 1) * BLK]
        heads.append(pltpu.roll(x, HEAD_DIM, 1) if hq % 2 != hq // SWA_G else x)
    return [jnp.where(hm[0], heads[2 * p], heads[2 * p + 1]) for p in range(SWA_H // 2)]


def _swa_scores(q8, kb, bias_ref, mask, cols):
    bias8 = jnp.concatenate([bias_ref[hq, :, cols] for hq in range(SWA_H)], axis=0)
    return jnp.where(mask, _dot_nt(q8, kb) + bias8, NEG_INF)


def _swa_sinks(sink_ref):
    return jnp.concatenate([jnp.broadcast_to(sink_ref[hq:hq + 1, 0:1], (BLK, 1)) for hq in range(SWA_H)], axis=0)


def _swa_fwd(proj, bias, sinks_b):
    t = proj.shape[0]
    nq = t // BLK

    def body(q_ref, k_ref, v_ref, bias_ref, sink_ref, o_ref, lse_ref):
        hm = _head_masks()
        band, own = _swa_band_masks()

        def qblock(i, prev):
            qs = pl.ds(pl.multiple_of(i * BLK, BLK), BLK)
            if prev:
                ks, mask, cols = pl.ds(pl.multiple_of((i - 1) * BLK, BLK), 2 * BLK), band, slice(None)
            else:
                ks, mask, cols = qs, own, slice(BLK, None)
            q8 = _swa_stack(q_ref, qs, hm, SCALE)
            sink8 = _swa_sinks(sink_ref)
            s = _swa_scores(q8, k_ref[ks, :], bias_ref, mask, cols)
            m = jnp.maximum(jnp.max(s, axis=1, keepdims=True), sink8)
            p = jnp.exp(s - m)
            den = jnp.sum(p, axis=1, keepdims=True) + jnp.exp(sink8 - m)
            o8 = _dot((p * (1.0 / den)).astype(BF16), v_ref[ks, :])
            lse8 = m + jnp.log(den)
            for hq in range(SWA_H):
                lse_ref[hq, qs, :] = lse8[hq * BLK:(hq + 1) * BLK]
            for pp, o in enumerate(_swa_unstack(o8, hm)):
                o_ref[qs, pp * LANES:(pp + 1) * LANES] = o

        qblock(0, False)

        def step(i, _):
            qblock(i, True)
            return 0

        lax.fori_loop(1, nq, step, 0)

    return _call(
        body, name="swa_fwd", grid=(1,),
        in_specs=[pl.BlockSpec((t, SWA_W), lambda i: (0, 3)), pl.BlockSpec((t, KV_W), lambda i: (0, 16)),
                  pl.BlockSpec((t, KV_W), lambda i: (0, 17)), pl.BlockSpec((8, BLK, 2 * BLK), lambda i: (0, 0, 0)),
                  pl.BlockSpec((8, LANES), lambda i: (0, 0))],
        out_specs=[pl.BlockSpec((t, SWA_W), lambda i: (0, 0)), pl.BlockSpec((8, t, 1), lambda i: (0, 0, 0))],
        out_shape=[jax.ShapeDtypeStruct((t, SWA_W), F32), jax.ShapeDtypeStruct((8, t, 1), F32)],
        compiler_params=_params(1))(proj, proj, proj, bias, sinks_b)


def _swa_bwd(proj, d_o, lse, bias, sinks_b, dbias_in):
    t = proj.shape[0]
    nq = t // BLK

    def body(q_ref, k_ref, v_ref, do_ref, lse_ref, bias_ref, sink_ref, dbi_ref,
             dq_ref, dk_ref, dv_ref, dsink_ref, dbias_ref, dk_acc, dv_acc):
        hm = _head_masks()
        band, own = _swa_band_masks()
        dk_acc[...] = jnp.zeros_like(dk_acc)
        dv_acc[...] = jnp.zeros_like(dv_acc)
        dbias_ref[...] = dbi_ref[...]

        def qblock(i, prev, dsink8):
            qs = pl.ds(pl.multiple_of(i * BLK, BLK), BLK)
            if prev:
                ks, mask, cols = pl.ds(pl.multiple_of((i - 1) * BLK, BLK), 2 * BLK), band, slice(None)
            else:
                ks, mask, cols = qs, own, slice(BLK, None)
            q8 = _swa_stack(q_ref, qs, hm, SCALE)
            do8 = _swa_stack(do_ref, qs, hm, 1.0)
            sink8 = _swa_sinks(sink_ref)
            lse8 = jnp.concatenate([lse_ref[hq, qs, :] for hq in range(SWA_H)], axis=0)
            kb = k_ref[ks, :]
            p = jnp.exp(_swa_scores(q8, kb, bias_ref, mask, cols) - lse8)
            dp = _dot_nt(do8, v_ref[ks, :])
            delta = jnp.sum(p * dp, axis=1, keepdims=True)
            ds = p * (dp - delta)
            for hq in range(SWA_H):
                dbias_ref[hq, :, cols] += ds[hq * BLK:(hq + 1) * BLK]
            dsb = ds.astype(BF16)
            dk_acc[ks, :] += _dot_tn(dsb, q8)
            dv_acc[ks, :] += _dot_tn(p.astype(BF16), do8)
            for pp, dq in enumerate(_swa_unstack(_dot(dsb, kb) * SCALE, hm)):
                dq_ref[qs, pp * LANES:(pp + 1) * LANES] = dq.astype(BF16)
            return dsink8 - jnp.exp(sink8 - lse8) * delta

        ds0 = qblock(0, False, jnp.zeros((SWA_H * BLK, 1), F32))
        ds8 = lax.fori_loop(1, nq, lambda i, c: qblock(i, True, c), ds0)
        for hq in range(SWA_H):
            dsink_ref[hq:hq + 1, :] = jnp.broadcast_to(
                jnp.sum(ds8[hq * BLK:(hq + 1) * BLK], axis=0, keepdims=True), (1, LANES))

        dk_ref[...] = dk_acc[...].astype(BF16)
        dv_ref[...] = dv_acc[...].astype(BF16)

    full3 = pl.BlockSpec((8, BLK, 2 * BLK), lambda i: (0, 0, 0))
    kv = jax.ShapeDtypeStruct((t, KV_W), BF16)
    return _call(
        body, name="swa_bwd", grid=(1,),
        in_specs=[pl.BlockSpec((t, SWA_W), lambda i: (0, 3)), pl.BlockSpec((t, KV_W), lambda i: (0, 16)),
                  pl.BlockSpec((t, KV_W), lambda i: (0, 17)), pl.BlockSpec((t, SWA_W), lambda i: (0, 1)),
                  pl.BlockSpec((8, t, 1), lambda i: (0, 0, 0)), full3, pl.BlockSpec((8, LANES), lambda i: (0, 0)),
                  full3],
        out_specs=[pl.BlockSpec((t, SWA_W), lambda i: (0, 0)), pl.BlockSpec((t, KV_W), lambda i: (0, 0)),
                   pl.BlockSpec((t, KV_W), lambda i: (0, 0)), pl.BlockSpec((8, LANES), lambda i: (0, 0)), full3],
        out_shape=[jax.ShapeDtypeStruct((t, SWA_W), BF16), kv, kv, jax.ShapeDtypeStruct((8, LANES), F32),
                   jax.ShapeDtypeStruct((8, BLK, 2 * BLK), F32)],
        scratch_shapes=[pltpu.VMEM((t, KV_W), F32), pltpu.VMEM((t, KV_W), F32)],
        compiler_params=_params(1))(proj, proj, proj, d_o, lse, bias, sinks_b, dbias_in)


def _concat_cols(parts):
    t = parts[0].shape[0]
    widths = [a.shape[1] for a in parts]

    def body(*refs):
        refs[-1][...] = jnp.concatenate([r[...] for r in refs[:-1]], axis=1)

    return _call(
        body, name="concat_cols", grid=(t // TM,),
        in_specs=[pl.BlockSpec((TM, w), lambda i: (i, 0)) for w in widths],
        out_specs=pl.BlockSpec((TM, sum(widths)), lambda i: (i, 0)),
        out_shape=jax.ShapeDtypeStruct((t, sum(widths)), parts[0].dtype), compiler_params=_params(1))(*parts)


def _bias_table(rel_bias, buckets):
    def body(rb_ref, b_ref, o_ref):
        bk = b_ref[...]
        for h in range(8):
            acc = jnp.zeros((BLK, 2 * BLK), F32)
            for b in range(N_BUCKETS):
                acc = jnp.where(bk == b, rb_ref[b, h], acc)
            o_ref[h] = acc

    return _call(
        body, name="bias_table", grid=(1,),
        in_specs=[pl.BlockSpec(memory_space=pltpu.SMEM), pl.BlockSpec((BLK, 2 * BLK), lambda i: (0, 0))],
        out_specs=pl.BlockSpec((8, BLK, 2 * BLK), lambda i: (0, 0, 0)),
        out_shape=jax.ShapeDtypeStruct((8, BLK, 2 * BLK), F32), compiler_params=_params(1))(rel_bias, buckets)


def _bias_grad(dbias, buckets):
    def body(d_ref, b_ref, o_ref):
        lane = lax.broadcasted_iota(jnp.int32, (1, LANES), 1)
        bk = b_ref[...]
        for h in range(8):
            d = d_ref[h]
            acc = jnp.zeros((1, LANES), F32)
            for b in range(N_BUCKETS):
                s = jnp.sum(jnp.sum(jnp.where(bk == b, d, 0.0), axis=0, keepdims=True), axis=1, keepdims=True)
                acc = acc + jnp.where(lane == b, s, 0.0)
            o_ref[h:h + 1, :] = acc

    return _call(
        body, name="bias_grad", grid=(1,),
        in_specs=[pl.BlockSpec((8, BLK, 2 * BLK), lambda i: (0, 0, 0)), pl.BlockSpec((BLK, 2 * BLK), lambda i: (0, 0))],
        out_specs=pl.BlockSpec((8, LANES), lambda i: (0, 0)),
        out_shape=jax.ShapeDtypeStruct((8, LANES), F32), compiler_params=_params(1))(dbias, buckets)


def _row(a):
    return a.reshape(1, -1)


def _fwd_ffn1_gu(h, n1, w):
    s = {"h0": h, "n1": n1}
    s["gu1"], s["act1"] = _ffn_gu(n1, w["ffn1_gu"])
    return s


def _fwd_ffn1_down(s, w, small, l):
    s["h1"], s["nm"] = _down_res(s["act1"], w["ffn1_down"], s["h0"], _row(small["norm_mix"][l]))


def _fwd_ffn1(h, n1, w, small, l):
    s = _fwd_ffn1_gu(h, n1, w)
    _fwd_ffn1_down(s, w, small, l)
    return s


def _fwd_proj_sb(s, w):
    s["proj"] = _proj(s["nm"], w["w_in"])
    s["o_sb"], s["tot"] = _sb_fwd(s["proj"])


def _fwd_swa(s, small, l, bias):
    s["sinks_b"] = jnp.broadcast_to(small["sinks"][l][:, None], (8, LANES))
    s["o_sw"], s["lse"] = _swa_fwd(s["proj"], bias, s["sinks_b"])


def _fwd_out_gu2(s, w, small, l):
    s["h2"], s["mixed"], s["n2"] = _out_res(
        s["o_sb"], s["o_sw"], _row(small["norm_out_sb"][l]), _row(small["norm_out_swa"][l]), w["w_out"], s["h1"],
        _row(small["norm_ffn2"][l]))
    s["gu2"], s["act2"] = _ffn_gu(s["n2"], w["ffn2_gu"])


def _fwd_ffn2_down(s, w, g_after):
    return _down_res(s["act2"], w["ffn2_down"], s["h2"], g_after)


def _fwd_out_ffn2(s, w, small, l, g_after):
    _fwd_out_gu2(s, w, small, l)
    return _fwd_ffn2_down(s, w, g_after)


def _bwd_ffn_dact(dh, s, w, which):
    return _ffn_dact(dh[1], w[f"ffn{which}_down"], s[f"gu{which}"])


def _bwd_ffn_rest(dh, dgu, s, w, small, l, which):
    h_in, norm = (s["h0"], "norm_ffn1") if which == 1 else (s["h2"], "norm_ffn2")
    g_down = _wgrad_down(s[f"act{which}"], dh[1])
    g_gu = _wgrad_gu(s[f"n{which}"], dgu)
    dh32, dh16, dg = _ffn_dn(dgu, w[f"ffn{which}_gu"], dh[0], h_in, _row(small[norm][l]))
    return (dh32, dh16), {f"ffn{which}_down": g_down, f"ffn{which}_gu": g_gu}, {norm: dg}


def _bwd_ffn(dh, s, w, small, l, which):
    return _bwd_ffn_rest(dh, _bwd_ffn_dact(dh, s, w, which), s, w, small, l, which)


def _bwd_mix(dh, s, w, small, l, bias, dbias):
    g_out = _wgrad_out(s["mixed"], dh[1])
    d_o, dg_sb, dg_sw = _dmixed(dh[1], w["w_out"], s["o_sb"], s["o_sw"], _row(small["norm_out_sb"][l]),
                                _row(small["norm_out_swa"][l]))
    dq_sb, dk_sb, dv_sb = _sb_bwd(s["proj"], d_o, s["tot"])
    dq_sw, dk_sw, dv_sw, dsink, dbias = _swa_bwd(s["proj"], d_o, s["lse"], bias, s["sinks_b"], dbias)
    dproj = _concat_cols([dq_sb, dk_sb, dv_sb, dq_sw, dk_sw, dv_sw])
    g_in = _wgrad_in(s["nm"], dproj)
    dh32, dh16, dg_mix = _mix_dn(dproj, w["w_in"], dh[0], s["h1"], _row(small["norm_mix"][l]))
    gs = {"norm_out_sb": dg_sb, "norm_out_swa": dg_sw, "sinks": dsink[:, 0], "norm_mix": dg_mix}
    return (dh32, dh16), {"w_out": g_out, "w_in": g_in}, gs, dbias


def _place():
    x, y, c = lax.axis_index("x"), lax.axis_index("y"), lax.axis_index("c")
    return x, y, c, 2 * x + y


def _chip_core(k, c):
    return (k // 2, k % 2, c)


def _rows_per_block(rows, cols, copies):
    best = 16
    for tr in range(16, rows + 1, 16):
        if rows % tr == 0 and copies * tr * cols * 4 <= SLAB_BLOCK_BYTES:
            best = tr
    assert rows % best == 0
    return best


def _place_own(w, l, me1):
    _, rows, cols = w.shape
    tr = _rows_per_block(rows // 2, cols, 1)
    per_half = rows // 2 // tr

    def body(me_ref, w_ref, o_ref):
        o_ref[...] = w_ref[...].astype(BF16)

    return _call(
        body, name="place_own",
        num_scalar_prefetch=1, grid=(rows // tr,),
        in_specs=[pl.BlockSpec((None, tr, cols), lambda r, me: (l, r, 0))],
        out_specs=pl.BlockSpec((None, None, tr, cols), lambda r, me: (me[0], r // per_half, r % per_half, 0)),
        out_shape=jax.ShapeDtypeStruct((N_CHIPS, 2, rows // 2, cols), BF16), compiler_params=_params(1))(me1, w)


def _plan_gather_ici(bufs):
    _, _, c, me = _place()
    return [(b.at[me, c], b.at[me, c], b.at[(me + 3 - j) % N_CHIPS, c], _chip_core((me + 1 + j) % N_CHIPS, c))
            for b in bufs for j in range(3)]


def _plan_gather_d2d(bufs):
    x, y, c, me = _place()
    return [(b.at[(me + 3 - j) % N_CHIPS, c], b.at[(me + 3 - j) % N_CHIPS, c], b.at[(me + 3 - j) % N_CHIPS, 1 - c],
             (x, y, 1 - c)) for b in bufs for j in range(3)]


def _plan_grad_sibling(bufs):
    x, y, c, _ = _place()
    n = len(bufs) // 2
    return [(g.at[:, 1 - c], z, z, (x, y, 1 - c)) for g, z in zip(bufs[:n], bufs[n:])]


def _plan_grad_chips(bufs):
    _, _, c, me = _place()
    n = len(bufs) // 2
    return [(p.at[j], z.at[j], z.at[j], _chip_core((me + 1 + j) % N_CHIPS, c))
            for p, z in zip(bufs[:n], bufs[n:]) for j in range(3)]


def _plan_grad_halves(bufs):
    x, y, c, _ = _place()
    return [(b.at[c], b.at[c], b.at[1 - c], (x, y, 1 - c)) for b in bufs]


def _remote(src, dst, send_sem, recv_sem, to):
    return pltpu.make_async_remote_copy(src_ref=src, dst_ref=dst, send_sem=send_sem, recv_sem=recv_sem,
                                        device_id=to, device_id_type=MESH)


def _exchange_start_groups(name, plan, groups):
    sizes = [len(g) for g, _ in groups]
    bufs = [a for g, _ in groups for a in g]
    n, n_groups = len(bufs), len(groups)

    def body(*refs):
        ins, sems, token = refs[:n], refs[n:n + 2 * n_groups], refs[-1]
        at = 0
        for k, size in enumerate(sizes):
            for i, (src, dst, _, to) in enumerate(plan(ins[at:at + size])):
                _remote(src, dst, sems[2 * k].at[i], sems[2 * k + 1].at[i], to).start()
            at += size
        token[...] = jnp.zeros_like(token)

    sem_shapes = [pltpu.SemaphoreType.DMA((n_copies,)) for _, n_copies in groups for _ in range(2)]
    out = _call(
        body, name=name,
        out_shape=(*sem_shapes, *[pltpu.HBM(a.shape, a.dtype) for a in bufs], jax.ShapeDtypeStruct((8, LANES), F32)),
        in_specs=[HBM] * n,
        out_specs=(*[SEM] * (2 * n_groups), *[HBM] * n, pl.BlockSpec(memory_space=pltpu.VMEM)),
        input_output_aliases={t: 2 * n_groups + t for t in range(n)}, hbm_args=n,
        compiler_params=pltpu.CompilerParams(has_side_effects=EFFECT),
    )(*bufs)
    flights, at = [], 2 * n_groups
    for k, size in enumerate(sizes):
        flights.append(((out[2 * k], out[2 * k + 1]), list(out[at:at + size])))
        at += size
    return flights


def _exchange_start(name, plan, bufs, n_copies):
    return _exchange_start_groups(name, plan, [(bufs, n_copies)])[0]


def _exchange_wait(name, plan, bufs, sems):
    n = len(bufs)

    def body(*refs):
        ins = refs[:n]
        ssem, rsem = refs[n], refs[n + 1]
        for i, (src, dst, land, to) in enumerate(plan(ins)):
            _remote(src, dst, ssem.at[i], rsem.at[i], to).wait_send()
            _remote(land, land, ssem.at[i], rsem.at[i], to).wait_recv()

    return list(_call(
        body, name=name, out_shape=[pltpu.HBM(a.shape, a.dtype) for a in bufs],
        in_specs=[HBM] * n + [SEM, SEM], out_specs=[HBM] * n,
        input_output_aliases={t: t for t in range(n)},
        compiler_params=pltpu.CompilerParams(has_side_effects=EFFECT),
    )(*bufs, sems[0], sems[1]))


def _exchange_pass(name, done, plan, bufs, sems, n_copies):
    n = len(bufs)

    def body(*refs):
        ins = refs[:n]
        old_s, old_r, ssem, rsem = refs[n], refs[n + 1], refs[n + 2], refs[n + 3]
        token = refs[-1]
        for i, (src, dst, land, to) in enumerate(done(ins)):
            _remote(src, dst, old_s.at[i], old_r.at[i], to).wait_send()
            _remote(land, land, old_s.at[i], old_r.at[i], to).wait_recv()
        for i, (src, dst, _, to) in enumerate(plan(ins)):
            _remote(src, dst, ssem.at[i], rsem.at[i], to).start()
        token[...] = jnp.zeros_like(token)

    out = _call(
        body, name=name,
        out_shape=(pltpu.SemaphoreType.DMA((n_copies,)), pltpu.SemaphoreType.DMA((n_copies,)),
                   *[pltpu.HBM(a.shape, a.dtype) for a in bufs], jax.ShapeDtypeStruct((8, LANES), F32)),
        in_specs=[HBM] * n + [SEM, SEM], out_specs=(SEM, SEM, *[HBM] * n, pl.BlockSpec(memory_space=pltpu.VMEM)),
        input_output_aliases={t: 2 + t for t in range(n)},
        compiler_params=pltpu.CompilerParams(has_side_effects=EFFECT),
    )(*bufs, sems[0], sems[1])
    return (out[0], out[1]), list(out[2:2 + n])


def _chip_sum(g, xbuf, cm):
    _, _, r2, cols = g.shape
    tr = _rows_per_block(r2, cols, 1)

    def body(cm_ref, g_ref, x_ref, o_ref):
        o_ref[...] = (g_ref[...] + x_ref[...]).astype(BF16)

    return _call(
        body, name="grad_chip_sum",
        num_scalar_prefetch=1, grid=(3, r2 // tr),
        in_specs=[pl.BlockSpec((None, None, tr, cols), lambda j, r, cm: ((cm[1] + 1 + j) % N_CHIPS, cm[0], r, 0)),
                  pl.BlockSpec((None, tr, cols), lambda j, r, cm: ((cm[1] + 1 + j) % N_CHIPS, r, 0))],
        out_specs=pl.BlockSpec((None, tr, cols), lambda j, r, cm: (j, r, 0)),
        out_shape=jax.ShapeDtypeStruct((3, r2, cols), BF16), compiler_params=_params(2))(cm, g, xbuf)


def _total_sum(g, xbuf, rbuf, cm):
    _, _, r2, cols = g.shape
    tr = _rows_per_block(r2, cols, 3)

    def body(cm_ref, g_ref, x_ref, r_ref, o_ref):
        acc = g_ref[...] + x_ref[...]
        for j in range(3):
            acc = acc + r_ref[j].astype(F32)
        o_ref[...] = acc

    return _call(
        body, name="grad_total_sum",
        num_scalar_prefetch=1, grid=(r2 // tr,),
        in_specs=[pl.BlockSpec((None, None, tr, cols), lambda r, cm: (cm[1], cm[0], r, 0)),
                  pl.BlockSpec((None, tr, cols), lambda r, cm: (cm[1], r, 0)),
                  pl.BlockSpec((3, tr, cols), lambda r, cm: (0, r, 0))],
        out_specs=pl.BlockSpec((None, tr, cols), lambda r, cm: (cm[0], r, 0)),
        out_shape=jax.ShapeDtypeStruct((2, r2, cols), F32), compiler_params=_params(1))(cm, g, xbuf, rbuf)


def _small_allreduce(v):
    rows = v.shape[0]
    n_dev = 2 * N_CHIPS

    def body(v_ref, o_ref, buf, ssem, rsem):
        x, y, c, _ = _place()
        me = 4 * x + 2 * y + c
        buf[me] = v_ref[...]

        def copy(d, slot, to):
            return _remote(v_ref, buf.at[slot], ssem.at[d - 1], rsem.at[d - 1], (to // 4, (to // 2) % 2, to % 2))

        cps = [copy(d, me, (me + d) % n_dev) for d in range(1, n_dev)]
        for cp in cps:
            cp.start()
        for d in range(1, n_dev):
            copy(d, (me + n_dev - d) % n_dev, me).wait_recv()
        for cp in cps:
            cp.wait_send()
        acc = buf[0]
        for i in range(1, n_dev):
            acc = acc + buf[i]
        o_ref[...] = acc

    vm = pl.BlockSpec(memory_space=pltpu.VMEM)
    return _call(
        body, name="small_allreduce", in_specs=[vm], out_specs=vm,
        out_shape=jax.ShapeDtypeStruct(v.shape, F32),
        scratch_shapes=[pltpu.VMEM((n_dev, rows, LANES), F32), pltpu.SemaphoreType.DMA((n_dev - 1,)),
                        pltpu.SemaphoreType.DMA((n_dev - 1,))],
        compiler_params=pltpu.CompilerParams(vmem_limit_bytes=V7X_VMEM_LIMIT))(v)


def _adamw_math(w, g, m, v):
    m2 = ADAM_B1 * m + (1.0 - ADAM_B1) * g
    v2 = ADAM_B2 * v + (1.0 - ADAM_B2) * (g * g)
    v_hat = v2 / (1.0 - ADAM_B2 ** ADAM_STEP)
    step = (-ADAM_LR / (1.0 - ADAM_B1 ** ADAM_STEP)) * m2 / (jnp.sqrt(v_hat) + ADAM_EPS)
    return step + (-ADAM_LR * ADAM_WD) * w, m2, v2


def _adamw_layer(w, g, m, v, l, prev):
    _, rows, cols = w.shape
    tr = rows
    for cand in range(8, rows + 1, 8):
        if rows % cand == 0 and cand * cols * 4 <= ADAMW_BLOCK_BYTES:
            tr = cand

    def body(w_ref, g_ref, m_ref, v_ref, *outs):
        go_ref, d_ref, m2_ref, v2_ref = outs[-4:]
        g = g_ref[...]
        go_ref[...] = g
        d_ref[...], m2_ref[...], v2_ref[...] = _adamw_math(w_ref[...], g, m_ref[...], v_ref[...])

    stack = pl.BlockSpec((None, tr, cols), lambda i: (l, i, 0))
    ins, specs, alias = [w, g, m, v], [stack, pl.BlockSpec((tr, cols), lambda i: (i, 0)), stack, stack], {}
    if prev is not None:
        ins += list(prev)
        specs += [ANY] * 4
        alias = {4 + i: i for i in range(4)}
    return _call(
        body, name="adamw", grid=(rows // tr,), in_specs=specs, out_specs=[stack] * 4,
        out_shape=[jax.ShapeDtypeStruct(w.shape, F32)] * 4, input_output_aliases=alias,
        compiler_params=_params(1))(*ins)


def _adamw_small(w, g, m, v):
    def body(w_ref, g_ref, m_ref, v_ref, d_ref, m2_ref, v2_ref):
        d_ref[...], m2_ref[...], v2_ref[...] = _adamw_math(w_ref[...], g_ref[...], m_ref[...], v_ref[...])

    spec = pl.BlockSpec(w.shape, lambda i: (0, 0))
    return _call(
        body, name="adamw_small", grid=(1,), in_specs=[spec] * 4, out_specs=[spec] * 3,
        out_shape=[jax.ShapeDtypeStruct(w.shape, F32)] * 3, compiler_params=_params(1))(w, g, m, v)


SMALL = ("norm_ffn1", "norm_mix", "sinks", "norm_out_sb", "norm_out_swa", "norm_ffn2", "rel_bias", "norm_final")
BIG = ("ffn1_gu", "ffn1_down", "w_in", "w_out", "ffn2_gu", "ffn2_down")


def _pack(parts):
    flat, n = [], 0
    for a in parts:
        a = a.reshape(-1).astype(F32)
        gap = -a.shape[0] % LANES
        flat += [a] + ([jnp.zeros((gap,), F32)] if gap else [])
        n += a.shape[0] + gap
    tail = -(n // LANES) % 8 * LANES
    return jnp.concatenate(flat + ([jnp.zeros((tail,), F32)] if tail else [])).reshape(-1, LANES)


def _unpack(packed, like):
    out, r = [], 0
    for a in like:
        n = math.prod(a.shape)
        nr = -(-n // LANES)
        out.append(packed[r:r + nr].reshape(-1)[:n].reshape(a.shape))
        r += nr
    return out


def _halved(a):
    k, r, cols = a.shape
    return a.reshape(k, 2, r // 2, cols)


def _weight_view(k, buf):
    full = buf.reshape(N_CHIPS, buf.shape[2] * 2, buf.shape[3])
    return full if k.endswith("_gu") else full.reshape(-1, D_MODEL)


def _grad_stack(k, g):
    if not k.endswith("_gu"):
        g = g.reshape(N_CHIPS, g.shape[0] // N_CHIPS, D_MODEL)
    return _halved(g)


def _empty_like_hbm(shape, dtype):
    return pltpu.with_memory_space_constraint(lax.empty(shape, dtype), pltpu.HBM)


def kernel(x, norm_ffn1, w_ffn1_gu, w_ffn1_down, norm_mix, w_in, sinks, norm_out_sb, norm_out_swa, w_out, norm_ffn2, w_ffn2_gu, w_ffn2_down, rel_bias, norm_final, loss_target, m_norm_ffn1, m_w_ffn1_gu, m_w_ffn1_down, m_norm_mix, m_w_in, m_sinks, m_norm_out_sb, m_norm_out_swa, m_w_out, m_norm_ffn2, m_w_ffn2_gu, m_w_ffn2_down, m_rel_bias, m_norm_final, v_norm_ffn1, v_w_ffn1_gu, v_w_ffn1_down, v_norm_mix, v_w_in, v_sinks, v_norm_out_sb, v_norm_out_swa, v_w_out, v_norm_ffn2, v_w_ffn2_gu, v_w_ffn2_down, v_rel_bias, v_norm_final):
    big_w = dict(ffn1_gu=w_ffn1_gu, ffn1_down=w_ffn1_down, w_in=w_in, w_out=w_out, ffn2_gu=w_ffn2_gu, ffn2_down=w_ffn2_down)
    big_m = dict(ffn1_gu=m_w_ffn1_gu, ffn1_down=m_w_ffn1_down, w_in=m_w_in, w_out=m_w_out, ffn2_gu=m_w_ffn2_gu, ffn2_down=m_w_ffn2_down)
    big_v = dict(ffn1_gu=v_w_ffn1_gu, ffn1_down=v_w_ffn1_down, w_in=v_w_in, w_out=v_w_out, ffn2_gu=v_w_ffn2_gu, ffn2_down=v_w_ffn2_down)
    small = dict(norm_ffn1=norm_ffn1, norm_mix=norm_mix, sinks=sinks, norm_out_sb=norm_out_sb, norm_out_swa=norm_out_swa,
                 norm_ffn2=norm_ffn2, rel_bias=rel_bias, norm_final=norm_final)
    small_m = dict(norm_ffn1=m_norm_ffn1, norm_mix=m_norm_mix, sinks=m_sinks, norm_out_sb=m_norm_out_sb,
                   norm_out_swa=m_norm_out_swa, norm_ffn2=m_norm_ffn2, rel_bias=m_rel_bias, norm_final=m_norm_final)
    small_v = dict(norm_ffn1=v_norm_ffn1, norm_mix=v_norm_mix, sinks=v_sinks, norm_out_sb=v_norm_out_sb,
                   norm_out_swa=v_norm_out_swa, norm_ffn2=v_norm_ffn2, rel_bias=v_rel_bias, norm_final=v_norm_final)
    for dct in (big_w, big_m, big_v):
        dct["w_in"] = jnp.swapaxes(dct["w_in"], 1, 2)
    _PREVIOUS[0] = None
    _, _, c, me = _place()
    cm = jnp.stack([c, me]).astype(jnp.int32)
    buckets = jnp.asarray(_bucket_table())
    ffn1, mix_in, rest = ("ffn1_gu", "ffn1_down"), ("w_in",), ("w_out", "ffn2_gu", "ffn2_down")

    def place(l, keys):
        return [_place_own(big_w[k], l, cm[1:]) for k in keys]

    def views(keys, bufs):
        return {k: _weight_view(k, b) for k, b in zip(keys, bufs)}

    def gather_start(tag, bufs):
        return _exchange_start(f"gather{tag}_ici_start", _plan_gather_ici, bufs, 3 * len(bufs))

    def gather_pass(tag, flight):
        return _exchange_pass(f"gather{tag}_pass", _plan_gather_ici, _plan_gather_d2d, flight[1], flight[0],
                              3 * len(flight[1]))

    def gather_done(tag, keys, flight):
        return views(keys, _exchange_wait(f"gather{tag}_d2d_wait", _plan_gather_d2d, flight[1], flight[0]))

    fly_gu0 = gather_start("0a", place(0, ffn1[:1]))
    fly_down0 = gather_start("0a2", place(0, ffn1[1:]))
    fly_in0 = gather_start("0b", place(0, mix_in))
    later = [place(l, keys) for l in range(DEPTH) for keys in ((rest,) if l == 0 else (ffn1, mix_in, rest))]
    fly_rest0, fly_ffn1, fly_in1, fly_rest1 = _exchange_start_groups(
        "gather_later_ici_start", _plan_gather_ici, [(bufs, 3 * len(bufs)) for bufs in later])
    bias = _bias_table(rel_bias, buckets)
    n1 = _norm_cast(x[0], _row(norm_ffn1[0]))
    w0 = gather_done("0a", ffn1[:1], gather_pass("0a", fly_gu0))

    s0 = _fwd_ffn1_gu(x[0], n1, w0)
    w0.update(gather_done("0a2", ffn1[1:], gather_pass("0a2", fly_down0)))
    fly_in0 = gather_pass("0b", fly_in0)
    _fwd_ffn1_down(s0, w0, small, 0)
    w0.update(gather_done("0b", mix_in, fly_in0))
    _fwd_proj_sb(s0, w0)
    fly_rest0 = gather_pass("0c", fly_rest0)
    _fwd_swa(s0, small, 0, bias)
    w0.update(gather_done("0c", rest, fly_rest0))
    _fwd_out_gu2(s0, w0, small, 0)
    fly_ffn1 = gather_pass("1a", fly_ffn1)
    h, n1 = _fwd_ffn2_down(s0, w0, _row(norm_ffn1[1]))
    w1 = gather_done("1a", ffn1, fly_ffn1)
    fly_in1 = gather_pass("1b", fly_in1)
    s1 = _fwd_ffn1(h, n1, w1, small, 1)
    w1.update(gather_done("1b", mix_in, fly_in1))
    _fwd_proj_sb(s1, w1)
    fly_rest1 = gather_pass("1c", fly_rest1)
    _fwd_swa(s1, small, 1, bias)
    w1.update(gather_done("1c", rest, fly_rest1))
    h, _ = _fwd_out_ffn2(s1, w1, small, 1, _row(norm_final))
    dh32, dh16, dg_final, loss_row = _loss_head(h, _row(norm_final), loss_target[0])
    dh = (dh32, dh16)

    def landing(stacks, lead, dtype):
        return [_empty_like_hbm((lead,) + a.shape[2:], dtype) for a in stacks]

    def reduce_begin(tag, keys, gw):
        stacks = [_grad_stack(k, gw[k]) for k in keys]
        flight = _exchange_start(f"grad{tag}_sibling_start", _plan_grad_sibling,
                                 stacks + landing(stacks, N_CHIPS, F32), len(keys))
        return dict(tag=tag, keys=keys, stacks=stacks, flight=flight)

    def reduce_chips(st):
        n, (sems, bufs) = len(st["keys"]), st["flight"]
        bufs = _exchange_wait(f"grad{st['tag']}_sibling_wait", _plan_grad_sibling, bufs, sems)
        st["own"] = list(zip(bufs[:n], bufs[n:]))
        st["flight"] = _exchange_start(f"grad{st['tag']}_chips_start", _plan_grad_chips,
                                       [_chip_sum(g, z, cm) for g, z in st["own"]] + landing(st["stacks"], 3, BF16),
                                       3 * n)

    def reduce_halves(st):
        n, (sems, bufs) = len(st["keys"]), st["flight"]
        bufs = _exchange_wait(f"grad{st['tag']}_chips_wait", _plan_grad_chips, bufs, sems)
        halves = [_total_sum(g, x, z, cm) for (g, x), z in zip(st["own"], bufs[n:])]
        st["flight"] = _exchange_start(f"grad{st['tag']}_halves_start", _plan_grad_halves, halves, n)

    def reduce_end(st):
        sems, bufs = st["flight"]
        bufs = _exchange_wait(f"grad{st['tag']}_halves_wait", _plan_grad_halves, bufs, sems)
        return {k: b.reshape(big_w[k].shape[1:]) for k, b in zip(st["keys"], bufs)}

    def adamw(reduced, l, prev):
        return {k: _adamw_layer(big_w[k], g, big_m[k], big_v[k], l, None if prev is None else prev[k])
                for k, g in reduced.items()}

    gsm = [dict() for _ in range(DEPTH)]
    dbias = jnp.zeros((8, BLK, 2 * BLK), F32)
    dh, gw1, gs = _bwd_ffn(dh, s1, w1, small, 1, 2)
    gsm[1].update(gs)
    dh, gw, gs, dbias = _bwd_mix(dh, s1, w1, small, 1, bias, dbias)
    gw1.update(gw)
    gsm[1].update(gs)
    dh, gw, gs = _bwd_ffn(dh, s1, w1, small, 1, 1)
    gw1.update(gw)
    gsm[1].update(gs)

    red1 = reduce_begin("1", BIG, gw1)
    dh, gw0, gs = _bwd_ffn(dh, s0, w0, small, 0, 2)
    gsm[0].update(gs)
    reduce_chips(red1)
    dh, gw, gs, dbias = _bwd_mix(dh, s0, w0, small, 0, bias, dbias)
    gw0.update(gw)
    gsm[0].update(gs)
    red0a = reduce_begin("0a", ("ffn2_gu", "ffn2_down", "w_out", "w_in"), gw0)
    reduce_halves(red1)
    dgu = _bwd_ffn_dact(dh, s0, w0, 1)
    reduce_chips(red0a)
    dh, gw, gs = _bwd_ffn_rest(dh, dgu, s0, w0, small, 0, 1)
    gsm[0].update(gs)
    red0b = reduce_begin("0b", ffn1, gw)
    reduced1 = reduce_end(red1)
    stacks = adamw({k: reduced1[k] for k in ffn1}, 1, None)

    gsmall = {k: jnp.stack([gsm[l][k].reshape(-1) for l in range(DEPTH)]) for k in gsm[0]}
    gsmall["rel_bias"] = jnp.transpose(_bias_grad(dbias, buckets)[:, :N_BUCKETS])
    gsmall["norm_final"] = dg_final.reshape(-1)
    small_like = [small[k] for k in SMALL]
    pk = lambda dct: _pack([dct[k] for k in SMALL])
    red = _small_allreduce(_pack([gsmall[k] for k in SMALL] + [loss_row[0, :1]]))
    gs = _unpack(red, small_like + [loss_row[0, :1]])
    loss = gs[-1][0]
    gs = dict(zip(SMALL, gs[:-1]))

    ffn2 = ("ffn2_gu", "ffn2_down")
    reduce_chips(red0b)
    stacks.update(adamw({k: reduced1[k] for k in ("w_in", "w_out")}, 1, None))
    reduce_halves(red0a)
    stacks.update(adamw({k: reduced1[k] for k in ffn2}, 1, None))
    dlt, m2, v2 = _adamw_small(pk(small), pk(gs), pk(small_m), pk(small_v))
    reduced0a = reduce_end(red0a)
    stacks.update(adamw({k: reduced0a[k] for k in ffn2}, 0, stacks))
    reduce_halves(red0b)
    stacks.update(adamw({k: reduced0a[k] for k in ("w_in", "w_out")}, 0, stacks))
    stacks.update(adamw(reduce_end(red0b), 0, stacks))

    out_g, out_d, out_m, out_v = {}, {}, {}, {}
    for k in BIG:
        out_g[k], out_d[k], out_m[k], out_v[k] = [jnp.swapaxes(a, 1, 2) if k == "w_in" else a for a in stacks[k]]
    for dst, packed in ((out_d, dlt), (out_m, m2), (out_v, v2)):
        dst.update(zip(SMALL, _unpack(packed, small_like)))
    out_g.update(gs)

    order = ("norm_ffn1", "ffn1_gu", "ffn1_down", "norm_mix", "w_in", "sinks", "norm_out_sb", "norm_out_swa", "w_out",
             "norm_ffn2", "ffn2_gu", "ffn2_down", "rel_bias", "norm_final")
    return (loss, dh[0].reshape(x.shape), *[out_g[k] for k in order], *[out_d[k] for k in order],
            *[out_m[k] for k in order], *[out_v[k] for k in order])
```

```python
import math

import numpy as np
import jax
import jax.numpy as jnp
from jax import lax
from jax.experimental import pallas as pl
from jax.experimental.pallas import tpu as pltpu

F32 = jnp.float32
BF16 = jnp.bfloat16

D_MODEL = 1024
DEPTH = 2
HEAD_DIM = 64
BLK = 128
N_BUCKETS = 32
MAX_DISTANCE = 128
D_FF = 2816
EPS = 1e-6
NEG_INF = -1e30
SB_W = 512
SWA_W = 512
KV_W = 128
IN_W = 2304
SCALE = HEAD_DIM ** -0.5
N_CHIPS = 4
FS = 2 * D_FF // N_CHIPS
LANES = 128
V7X_VMEM_LIMIT = 56 * 2 ** 20
TM = 512
SLAB_BLOCK_BYTES = 6 * 2 ** 20
ADAMW_BLOCK_BYTES = 2 ** 21
SB_KT = 512
SWA_G = 4

ADAM_LR = 0.001
ADAM_B1 = 0.9
ADAM_B2 = 0.999
ADAM_EPS = 1e-08
ADAM_WD = 0.01
ADAM_STEP = 10

MESH = pl.DeviceIdType.MESH
ANY = pl.BlockSpec(memory_space=pl.ANY)
HBM = pl.BlockSpec(memory_space=pltpu.HBM)
SEM = pl.BlockSpec(memory_space=pltpu.SEMAPHORE)
EFFECT = pltpu.SideEffectType.DATAFLOW_SIDE_EFFECTING


def _params(n_grid):
    return pltpu.CompilerParams(dimension_semantics=("arbitrary",) * n_grid, vmem_limit_bytes=V7X_VMEM_LIMIT)


_PREVIOUS = [None]


def _call(body, *, name, in_specs, out_specs, out_shape, grid=(), num_scalar_prefetch=0, scratch_shapes=(),
          input_output_aliases=None, compiler_params=None, hbm_args=0):
    n_in = len(in_specs)

    def run(*args):
        dep = _PREVIOUS[0]
        if any(dep is a for a in args):
            dep = None
        args = [pltpu.with_memory_space_constraint(a, pltpu.HBM) if i < hbm_args else a for i, a in enumerate(args)]
        specs = list(in_specs) + ([ANY] if dep is not None else [])
        k = num_scalar_prefetch + n_in
        fn = body if dep is None else (lambda *refs: body(*refs[:k], *refs[k + 1:]))
        if num_scalar_prefetch:
            shape = dict(grid_spec=pltpu.PrefetchScalarGridSpec(
                num_scalar_prefetch=num_scalar_prefetch, grid=grid, in_specs=specs, out_specs=out_specs,
                scratch_shapes=scratch_shapes))
        else:
            shape = dict(grid=grid, in_specs=specs, out_specs=out_specs, scratch_shapes=scratch_shapes)
        out = pl.pallas_call(fn, name=name, out_shape=out_shape, input_output_aliases=input_output_aliases or {},
                             compiler_params=compiler_params, **shape)(*args, *([] if dep is None else [dep]))
        _PREVIOUS[0] = jax.tree.leaves(out)[-1]
        return out

    return run


def _dot(a, b):
    return jnp.dot(a, b, preferred_element_type=F32)


def _dot_nt(a, b):
    return lax.dot_general(a, b, (((1,), (1,)), ((), ())), preferred_element_type=F32)


def _dot_tn(a, b):
    return lax.dot_general(a, b, (((0,), (0,)), ((), ())), preferred_element_type=F32)


def _rms_fwd(x, g):
    r = lax.rsqrt(jnp.mean(x * x, axis=-1, keepdims=True) + EPS)
    xh = x * r
    return xh * g, xh, r


def _rms_bwd(dy, xh, r, g):
    u = dy * g
    dx = r * (u - xh * jnp.mean(u * xh, axis=-1, keepdims=True))
    dg = jnp.sum(dy * xh, axis=0, keepdims=True)
    return dx, dg


def _softplus(z):
    neg_abs = lax.bitcast_convert_type(lax.bitcast_convert_type(z, jnp.int32) | jnp.int32(-2 ** 31), F32)
    sp = jnp.maximum(z, 0.0) + jnp.log(1.0 + jnp.exp(neg_abs))
    return sp, z - sp


def _norm_cast(h, g):
    t, w = h.shape

    def body(h_ref, g_ref, n_ref):
        y, _, _ = _rms_fwd(h_ref[...], g_ref[...])
        n_ref[...] = y.astype(BF16)

    return _call(
        body, name="norm_cast", grid=(t // TM,),
        in_specs=[pl.BlockSpec((TM, w), lambda i: (i, 0)), pl.BlockSpec((1, w), lambda i: (0, 0))],
        out_specs=pl.BlockSpec((TM, w), lambda i: (i, 0)),
        out_shape=jax.ShapeDtypeStruct((t, w), BF16), compiler_params=_params(1))(h, g)


def _ffn_gu(n, wgu):
    t, d = n.shape

    def body(n_ref, wg_ref, wu_ref, gu_ref, act_ref):
        x = n_ref[...]
        g = _dot(x, wg_ref[...])
        u = _dot(x, wu_ref[...])
        sig = jax.nn.sigmoid(g)
        silu = g * sig
        gu_ref[0] = (u * (sig + silu * (1.0 - sig))).astype(BF16)
        gu_ref[1] = silu.astype(BF16)
        act_ref[...] = (silu * u).astype(BF16)

    return _call(
        body, name="ffn_gu", grid=(2, t // TM),
        in_specs=[pl.BlockSpec((TM, d), lambda j, i: (i, 0)),
                  pl.BlockSpec((None, d, FS), lambda j, i: (j, 0, 0)),
                  pl.BlockSpec((None, d, FS), lambda j, i: (j + 2, 0, 0))],
        out_specs=[pl.BlockSpec((2, TM, FS), lambda j, i: (0, i, j)), pl.BlockSpec((TM, FS), lambda j, i: (i, j))],
        out_shape=[jax.ShapeDtypeStruct((2, t, D_FF), BF16), jax.ShapeDtypeStruct((t, D_FF), BF16)],
        compiler_params=_params(2))(n, wgu, wgu)


def _down_res(act, wdn, h, g_next):
    t, f = act.shape
    d = h.shape[1]

    def body(a_ref, w_ref, h_ref, g_ref, o_ref, n_ref):
        out = h_ref[...] + 0.5 * _dot(a_ref[...], w_ref[...])
        o_ref[...] = out
        n_ref[...] = _rms_fwd(out, g_ref[...])[0].astype(BF16)

    row = pl.BlockSpec((TM, d), lambda i: (i, 0))
    return _call(
        body, name="down_res", grid=(t // TM,),
        in_specs=[pl.BlockSpec((TM, f), lambda i: (i, 0)), pl.BlockSpec((f, d), lambda i: (0, 0)), row,
                  pl.BlockSpec((1, d), lambda i: (0, 0))],
        out_specs=[row, row],
        out_shape=[jax.ShapeDtypeStruct((t, d), F32), jax.ShapeDtypeStruct((t, d), BF16)],
        compiler_params=_params(1))(act, wdn, h, g_next)


def _proj(n, w_in_t):
    t, d = n.shape
    w = w_in_t.shape[0]

    def body(n_ref, w_ref, o_ref):
        o_ref[...] = _dot_nt(n_ref[...], w_ref[...]).astype(BF16)

    return _call(
        body, name="proj", grid=(t // TM,),
        in_specs=[pl.BlockSpec((TM, d), lambda i: (i, 0)), pl.BlockSpec((w, d), lambda i: (0, 0))],
        out_specs=pl.BlockSpec((TM, w), lambda i: (i, 0)),
        out_shape=jax.ShapeDtypeStruct((t, w), BF16), compiler_params=_params(1))(n, w_in_t)


def _out_res(o_sb, o_sw, g_sb, g_sw, w_out, h, g_next):
    t, d = h.shape

    def body(a_ref, b_ref, ga_ref, gb_ref, w_ref, h_ref, g_ref, o_ref, mix_ref, n_ref):
        ya, _, _ = _rms_fwd(a_ref[...], ga_ref[...])
        yb, _, _ = _rms_fwd(b_ref[...], gb_ref[...])
        mixed = jnp.concatenate([ya.astype(BF16), yb.astype(BF16)], axis=1)
        mix_ref[...] = mixed
        out = h_ref[...] + _dot(mixed, w_ref[...])
        o_ref[...] = out
        n_ref[...] = _rms_fwd(out, g_ref[...])[0].astype(BF16)

    row = pl.BlockSpec((TM, d), lambda i: (i, 0))
    return _call(
        body, name="out_res", grid=(t // TM,),
        in_specs=[pl.BlockSpec((TM, SB_W), lambda i: (i, 0)), pl.BlockSpec((TM, SWA_W), lambda i: (i, 0)),
                  pl.BlockSpec((1, SB_W), lambda i: (0, 0)), pl.BlockSpec((1, SWA_W), lambda i: (0, 0)),
                  pl.BlockSpec((d, d), lambda i: (0, 0)), row, pl.BlockSpec((1, d), lambda i: (0, 0))],
        out_specs=[row, row, row],
        out_shape=[jax.ShapeDtypeStruct((t, d), F32), jax.ShapeDtypeStruct((t, d), BF16),
                   jax.ShapeDtypeStruct((t, d), BF16)],
        compiler_params=_params(1))(o_sb, o_sw, g_sb, g_sw, w_out, h, g_next)


def _loss_head(h, g, tgt):
    t, d = h.shape

    def body(h_ref, g_ref, t_ref, dh_ref, dhb_ref, dg_ref, loss_ref):
        @pl.when(pl.program_id(0) == 0)
        def _():
            dg_ref[...] = jnp.zeros_like(dg_ref)
            loss_ref[...] = jnp.zeros_like(loss_ref)

        gg = g_ref[...]
        y, xh, r = _rms_fwd(h_ref[...], gg)
        err = y - t_ref[...]
        part = 0.5 * jnp.sum(jnp.sum(err * err, axis=1, keepdims=True) / d, axis=0, keepdims=True)
        loss_ref[...] += jnp.broadcast_to(part, loss_ref.shape)
        dx, dg = _rms_bwd(err / d, xh, r, gg)
        dh_ref[...] = dx
        dhb_ref[...] = dx.astype(BF16)
        dg_ref[...] += dg

    row = pl.BlockSpec((TM, d), lambda i: (i, 0))
    return _call(
        body, name="loss_head", grid=(t // TM,),
        in_specs=[row, pl.BlockSpec((1, d), lambda i: (0, 0)), row],
        out_specs=[row, row, pl.BlockSpec((1, d), lambda i: (0, 0)), pl.BlockSpec((1, LANES), lambda i: (0, 0))],
        out_shape=[jax.ShapeDtypeStruct((t, d), F32), jax.ShapeDtypeStruct((t, d), BF16),
                   jax.ShapeDtypeStruct((1, d), F32), jax.ShapeDtypeStruct((1, LANES), F32)],
        compiler_params=_params(1))(h, g, tgt)


def _ffn_dact(dh, wdn, gu):
    t, d = dh.shape
    tm = TM

    def body(dh_ref, w_ref, gu_ref, o_ref):
        da = 0.5 * _dot_nt(dh_ref[...].astype(BF16), w_ref[...])
        o_ref[0] = (da * gu_ref[0].astype(F32)).astype(BF16)
        o_ref[1] = (da * gu_ref[1].astype(F32)).astype(BF16)

    return _call(
        body, name="ffn_dact", grid=(2, t // tm),
        in_specs=[pl.BlockSpec((tm, d), lambda j, i: (i, 0)), pl.BlockSpec((FS, d), lambda j, i: (j, 0)),
                  pl.BlockSpec((2, tm, FS), lambda j, i: (0, i, j))],
        out_specs=pl.BlockSpec((2, tm, FS), lambda j, i: (0, i, j)),
        out_shape=jax.ShapeDtypeStruct((2, t, D_FF), BF16), compiler_params=_params(2))(dh, wdn, gu)


def _dn_norm_bwd(a, a_spec, w, w_spec, nk, dh, h_in, g, w_transposed=False, tm=TM):
    t, d = dh.shape
    mm = _dot if w_transposed else _dot_nt

    def body(a_ref, w_ref, dh_ref, h_ref, g_ref, o_ref, ob_ref, dg_ref, acc_ref):
        i, k = pl.program_id(0), pl.program_id(1)

        if nk > 1:
            @pl.when(k == 0)
            def _():
                acc_ref[...] = mm(a_ref[...], w_ref[...])

            @pl.when((k > 0) & (k < nk - 1))
            def _():
                acc_ref[...] += mm(a_ref[...], w_ref[...])

        @pl.when(k == nk - 1)
        def _():
            gg = g_ref[...]
            dg = jnp.zeros_like(gg)
            for rows in (slice(r, r + TM // 2) for r in range(0, tm, TM // 2)):
                dn = mm(a_ref[rows, :], w_ref[...])
                if nk > 1:
                    dn = dn + acc_ref[rows, :]
                _, xh, r = _rms_fwd(h_ref[rows, :], gg)
                dx, dg_rows = _rms_bwd(dn, xh, r, gg)
                out = dh_ref[rows, :] + dx
                o_ref[rows, :] = out
                ob_ref[rows, :] = out.astype(BF16)
                dg = dg + dg_rows

            @pl.when(i == 0)
            def _():
                dg_ref[...] = dg

            @pl.when(i > 0)
            def _():
                dg_ref[...] += dg

    row = pl.BlockSpec((tm, d), lambda i, k: (i, 0))
    return _call(
        body, name="dn_norm_bwd", grid=(t // tm, nk),
        in_specs=[a_spec, w_spec, row, row, pl.BlockSpec((1, d), lambda i, k: (0, 0))],
        out_specs=[row, row, pl.BlockSpec((1, d), lambda i, k: (0, 0))],
        out_shape=[jax.ShapeDtypeStruct((t, d), F32), jax.ShapeDtypeStruct((t, d), BF16),
                   jax.ShapeDtypeStruct((1, d), F32)],
        scratch_shapes=[pltpu.VMEM((tm, d), F32)], compiler_params=_params(2))(a, w, dh, h_in, g)


def _ffn_dn(dgu, wgu, dh, h_in, g):
    d = dh.shape[1]
    tm = 2 * TM
    return _dn_norm_bwd(
        dgu, pl.BlockSpec((None, tm, FS), lambda i, k: (k // 2, i, k % 2)),
        wgu, pl.BlockSpec((None, d, FS), lambda i, k: (k, 0, 0)), N_CHIPS, dh, h_in, g, tm=tm)


def _mix_dn(dproj, w_in_t, dh, h_in, g):
    d = dh.shape[1]
    w = dproj.shape[1]
    return _dn_norm_bwd(
        dproj, pl.BlockSpec((TM, w), lambda i, k: (i, 0)),
        w_in_t, pl.BlockSpec((w, d), lambda i, k: (0, 0)), 1, dh, h_in, g, w_transposed=True)


def _dmixed(dh, w_out, o_sb, o_sw, g_sb, g_sw):
    t, d = dh.shape

    def body(dh_ref, w_ref, a_ref, b_ref, ga_ref, gb_ref, o_ref, dga_ref, dgb_ref):
        i = pl.program_id(0)
        dm = _dot_nt(dh_ref[...].astype(BF16), w_ref[...])
        _, xa, ra = _rms_fwd(a_ref[...], ga_ref[...])
        _, xb, rb = _rms_fwd(b_ref[...], gb_ref[...])
        da, dga = _rms_bwd(dm[:, :SB_W], xa, ra, ga_ref[...])
        db, dgb = _rms_bwd(dm[:, SB_W:], xb, rb, gb_ref[...])
        o_ref[...] = jnp.concatenate([da.astype(BF16), db.astype(BF16)], axis=1)

        @pl.when(i == 0)
        def _():
            dga_ref[...] = dga
            dgb_ref[...] = dgb

        @pl.when(i > 0)
        def _():
            dga_ref[...] += dga
            dgb_ref[...] += dgb

    return _call(
        body, name="dmixed", grid=(t // TM,),
        in_specs=[pl.BlockSpec((TM, d), lambda i: (i, 0)), pl.BlockSpec((d, d), lambda i: (0, 0)),
                  pl.BlockSpec((TM, SB_W), lambda i: (i, 0)), pl.BlockSpec((TM, SWA_W), lambda i: (i, 0)),
                  pl.BlockSpec((1, SB_W), lambda i: (0, 0)), pl.BlockSpec((1, SWA_W), lambda i: (0, 0))],
        out_specs=[pl.BlockSpec((TM, d), lambda i: (i, 0)), pl.BlockSpec((1, SB_W), lambda i: (0, 0)),
                   pl.BlockSpec((1, SWA_W), lambda i: (0, 0))],
        out_shape=[jax.ShapeDtypeStruct((t, d), BF16), jax.ShapeDtypeStruct((1, SB_W), F32),
                   jax.ShapeDtypeStruct((1, SWA_W), F32)],
        compiler_params=_params(1))(dh, w_out, o_sb, o_sw, g_sb, g_sw)


def _wgrad(name, a, a_spec, b, b_spec, grid, out_shape, out_spec, scale):
    def body(a_ref, b_ref, o_ref):
        r = _dot_tn(a_ref[...], b_ref[...].astype(BF16))
        o_ref[...] = r if scale == 1.0 else scale * r

    return _call(
        body, name=name, grid=grid, in_specs=[a_spec, b_spec], out_specs=out_spec,
        out_shape=jax.ShapeDtypeStruct(out_shape, F32), compiler_params=_params(len(grid)))(a, b)


def _wgrad_gu(n, dgu):
    t, d = n.shape
    return _wgrad(
        "wgrad_gu", n, pl.BlockSpec((t, TM), lambda s, r: (0, r)),
        dgu, pl.BlockSpec((None, t, FS), lambda s, r: (s // 2, 0, s % 2)), (N_CHIPS, d // TM),
        (N_CHIPS, d, FS), pl.BlockSpec((None, TM, FS), lambda s, r: (s, r, 0)), 1.0)


def _wgrad_down(act, dh):
    t, d = dh.shape
    return _wgrad(
        "wgrad_down", act, pl.BlockSpec((t, FS), lambda s: (0, s)), dh, pl.BlockSpec((t, d), lambda s: (0, 0)),
        (2,), (D_FF, d), pl.BlockSpec((FS, d), lambda s: (s, 0)), 0.5)


def _wgrad_out(mixed, dh):
    t, d = dh.shape
    return _wgrad(
        "wgrad_out", mixed, pl.BlockSpec((t, TM), lambda s: (0, s)), dh, pl.BlockSpec((t, d), lambda s: (0, 0)),
        (d // TM,), (d, d), pl.BlockSpec((TM, d), lambda s: (s, 0)), 1.0)


def _wgrad_in(n, dproj):
    t, d = n.shape
    w = dproj.shape[1]
    tw = w // 3
    return _wgrad(
        "wgrad_in", dproj, pl.BlockSpec((t, tw), lambda s: (0, s)), n, pl.BlockSpec((t, d), lambda s: (0, 0)),
        (3,), (w, d), pl.BlockSpec((tw, d), lambda s: (s, 0)), 1.0)


def _tri(rel):
    row = lax.broadcasted_iota(jnp.int32, (BLK, BLK), 0)
    col = lax.broadcasted_iota(jnp.int32, (BLK, BLK), 1)
    m = rel(row, col).astype(BF16)
    return jnp.concatenate([m, m], axis=0)


def _scan_dot(x, tri2):
    hi = x.astype(BF16)
    lo = (x - hi.astype(F32)).astype(BF16)
    return _dot(jnp.concatenate([hi, lo], axis=1), tri2)


def _head_masks():
    lane = lax.broadcasted_iota(jnp.int32, (1, LANES), 1)
    return [lane < HEAD_DIM, lane >= HEAD_DIM]


SB_PAIRS = 2
SB_ROWS = 2 * SB_PAIRS * BLK


def _sb_causal():
    row = lax.broadcasted_iota(jnp.int32, (SB_ROWS, BLK), 0) & (BLK - 1)
    return lax.broadcasted_iota(jnp.int32, (SB_ROWS, BLK), 1) < row


def _sb_mask_last(x, causal):
    own = jnp.where(causal, x[:, -BLK:], 0.0)
    return own if x.shape[1] == BLK else jnp.concatenate([x[:, :-BLK], own], axis=1)


def _sb_stack(x, hm):
    return jnp.concatenate([jnp.where(m, x[:, p * LANES:(p + 1) * LANES], jnp.zeros((BLK, LANES), x.dtype))
                            for p in range(SB_PAIRS) for m in hm], axis=0)


def _sb_unstack(y, hm):
    return jnp.concatenate([jnp.where(hm[0], y[2 * p * BLK:(2 * p + 1) * BLK], y[(2 * p + 1) * BLK:(2 * p + 2) * BLK])
                            for p in range(SB_PAIRS)], axis=1)


def _sb_pairs():
    return [(slice(2 * p * BLK, (2 * p + 2) * BLK), slice(p * LANES, (p + 1) * LANES)) for p in range(SB_PAIRS)]


def _sb_fwd(proj):
    t = proj.shape[0]
    nb = SB_KT // BLK
    wide = SB_PAIRS * LANES

    def body(q_ref, k_ref, v_ref, o_ref, tot_ref):
        hm = _head_masks()
        causal = _sb_causal()
        pairs = _sb_pairs()
        after = _tri(lambda r, c: r > c)

        def tile(qh, start, n_blk, carry, acc, own):
            ks = pl.ds(pl.multiple_of(start, BLK), n_blk * BLK)
            z = jnp.concatenate([_dot_nt(qh[rows], k_ref[ks, lanes]) for rows, lanes in pairs], axis=0)
            sp, zs = _softplus(z)
            spm = _sb_mask_last(sp, causal) if own else sp
            sufs = [None] * n_blk
            for b in reversed(range(n_blk)):
                blk = spm[:, b * BLK:(b + 1) * BLK]
                sufs[b] = carry + _scan_dot(blk, after)
                carry = carry + jnp.sum(blk, axis=1, keepdims=True)
            w = jnp.exp(zs - jnp.concatenate(sufs, axis=1))
            wb = (_sb_mask_last(w, causal) if own else w).astype(BF16)
            return carry, acc + jnp.concatenate([_dot(wb[rows], v_ref[ks, lanes]) for rows, lanes in pairs], axis=0)

        def qblock(g, j):
            qs = pl.ds(pl.multiple_of(g * SB_KT + j * BLK, BLK), BLK)
            qh = _sb_stack(q_ref[qs, :] * SCALE, hm)
            c0 = tile(qh, g * SB_KT, j + 1, jnp.zeros((SB_ROWS, 1), F32), jnp.zeros((SB_ROWS, LANES), F32), True)
            carry, acc = lax.fori_loop(0, g, lambda n, c: tile(qh, (g - 1 - n) * SB_KT, nb, c[0], c[1], False), c0)
            o_ref[qs, :] = _sb_unstack(acc, hm)
            for h in range(2 * SB_PAIRS):
                tot_ref[h, qs, :] = carry[h * BLK:(h + 1) * BLK]

        def group(g, _):
            for j in range(nb):
                qblock(g, j)
            return 0

        lax.fori_loop(0, t // SB_KT, group, 0)

    col_blk = lambda off: pl.BlockSpec((t, wide), lambda g: (0, off + g))
    n_steps = SB_W // wide
    return _call(
        body, name="sb_fwd", grid=(n_steps,), in_specs=[col_blk(0), col_blk(n_steps), col_blk(2 * n_steps)],
        out_specs=[col_blk(0), pl.BlockSpec((2 * SB_PAIRS, t, 1), lambda g: (g, 0, 0))],
        out_shape=[jax.ShapeDtypeStruct((t, SB_W), F32), jax.ShapeDtypeStruct((8, t, 1), F32)],
        compiler_params=_params(1))(proj, proj, proj)


def _sb_bwd(proj, d_o, tot):
    t = proj.shape[0]
    nb = SB_KT // BLK
    wide = SB_PAIRS * LANES

    def body(q_ref, k_ref, v_ref, do_ref, tot_ref, dq_ref, dk_ref, dv_ref, dk_acc, dv_acc):
        hm = _head_masks()
        causal = _sb_causal()
        pairs = _sb_pairs()
        before = _tri(lambda r, c: r < c)
        upto = _tri(lambda r, c: r <= c)
        dk_acc[...] = jnp.zeros_like(dk_acc)
        dv_acc[...] = jnp.zeros_like(dv_acc)

        def tile(qh, doh, tt, start, n_blk, pre, ecum, dq, own):
            ks = pl.ds(pl.multiple_of(start, BLK), n_blk * BLK)
            k = k_ref[ks, :]
            v = v_ref[ks, :]
            z = jnp.concatenate([_dot_nt(qh[rows], k[:, lanes]) for rows, lanes in pairs], axis=0)
            sp, zs = _softplus(z)
            spm = _sb_mask_last(sp, causal) if own else sp
            pres = []
            for b in range(n_blk):
                blk = spm[:, b * BLK:(b + 1) * BLK]
                pres.append(pre + _scan_dot(blk, before))
                pre = pre + jnp.sum(blk, axis=1, keepdims=True)
            logw = z - (tt - jnp.concatenate(pres, axis=1))
            if own:
                logw = jnp.minimum(logw, 0.0)
            w = jnp.exp(logw)
            if own:
                w = _sb_mask_last(w, causal)
            e = w * jnp.concatenate([_dot_nt(doh[rows], v[:, lanes]) for rows, lanes in pairs], axis=0)
            incs = []
            for b in range(n_blk):
                blk = e[:, b * BLK:(b + 1) * BLK]
                incs.append(ecum + _scan_dot(blk, upto))
                ecum = ecum + jnp.sum(blk, axis=1, keepdims=True)
            dz = e - jnp.exp(zs) * jnp.concatenate(incs, axis=1)
            if own:
                dz = _sb_mask_last(dz, causal)
            dzb = dz.astype(BF16)
            wb = w.astype(BF16)
            for rows, lanes in pairs:
                dk_acc[ks, lanes] += _dot_tn(dzb[rows], qh[rows])
                dv_acc[ks, lanes] += _dot_tn(wb[rows], doh[rows])
            return pre, ecum, dq + jnp.concatenate([_dot(dzb[rows], k[:, lanes]) for rows, lanes in pairs], axis=0)

        def qblock(g, j):
            qs = pl.ds(pl.multiple_of(g * SB_KT + j * BLK, BLK), BLK)
            qh = _sb_stack(q_ref[qs, :] * SCALE, hm)
            doh = _sb_stack(do_ref[qs, :], hm)
            tt = jnp.concatenate([tot_ref[h, qs, :] for h in range(2 * SB_PAIRS)], axis=0)
            c0 = (jnp.zeros((SB_ROWS, 1), F32), jnp.zeros((SB_ROWS, 1), F32), jnp.zeros((SB_ROWS, LANES), F32))
            c = lax.fori_loop(0, g, lambda kt, c: tile(qh, doh, tt, kt * SB_KT, nb, c[0], c[1], c[2], False), c0)
            dq = tile(qh, doh, tt, g * SB_KT, j + 1, c[0], c[1], c[2], True)[2]
            dq_ref[qs, :] = (_sb_unstack(dq, hm) * SCALE).astype(BF16)

        def group(g, _):
            for j in range(nb):
                qblock(g, j)
            return 0

        lax.fori_loop(0, t // SB_KT, group, 0)
        dk_ref[...] = dk_acc[...].astype(BF16)
        dv_ref[...] = dv_acc[...].astype(BF16)

    col_blk = lambda off: pl.BlockSpec((t, wide), lambda g: (0, off + g))
    n_steps = SB_W // wide
    out = jax.ShapeDtypeStruct((t, SB_W), BF16)
    return _call(
        body, name="sb_bwd", grid=(n_steps,),
        in_specs=[col_blk(0), col_blk(n_steps), col_blk(2 * n_steps), col_blk(0),
                  pl.BlockSpec((2 * SB_PAIRS, t, 1), lambda g: (g, 0, 0))],
        out_specs=[col_blk(0), col_blk(0), col_blk(0)], out_shape=[out, out, out],
        scratch_shapes=[pltpu.VMEM((t, wide), F32), pltpu.VMEM((t, wide), F32)],
        compiler_params=_params(1))(proj, proj, proj, d_o, tot)


def _bucket_table():
    a = np.arange(BLK)[:, None]
    c = np.arange(2 * BLK)[None, :]
    dist = np.maximum(BLK + a - c, 0)
    max_exact = N_BUCKETS // 2
    dd = np.maximum(dist, 1).astype(np.float32)
    large = max_exact + (np.log(dd / max_exact) / math.log(MAX_DISTANCE / max_exact)
                         * (N_BUCKETS - max_exact)).astype(np.int32)
    large = np.minimum(large, N_BUCKETS - 1)
    return np.where(dist < max_exact, dist, large).astype(np.int32)


SWA_H = 8


def _swa_band_masks():
    row = lax.broadcasted_iota(jnp.int32, (SWA_H * BLK, 2 * BLK), 0) & (BLK - 1)
    col = lax.broadcasted_iota(jnp.int32, (SWA_H * BLK, 2 * BLK), 1)
    own = lax.broadcasted_iota(jnp.int32, (SWA_H * BLK, BLK), 1) <= (
        lax.broadcasted_iota(jnp.int32, (SWA_H * BLK, BLK), 0) & (BLK - 1))
    return (col > row) & ((col < BLK) | (col - BLK <= row)), own


def _swa_stack(ref, qs, hm, scale):
    parts = []
    for hq in range(SWA_H):
        kvh = hq // SWA_G
        x = ref[qs, (hq // 2) * LANES:(hq // 2 + 1) * LANES].astype(F32)
        if hq % 2 != kvh:
            x = pltpu.roll(x, HEAD_DIM, 1)
        parts.append(jnp.where(hm[kvh], x * scale, 0.0).astype(BF16))
    return jnp.concatenate(parts, axis=0)


def _swa_unstack(x8, hm):
    heads = []
    for hq in range(SWA_H):
        x = x8[hq * BLK:(hq + 1) * BLK]
        heads.append(pltpu.roll(x, HEAD_DIM, 1) if hq % 2 != hq // SWA_G else x)
    return [jnp.where(hm[0], heads[2 * p], heads[2 * p + 1]) for p in range(SWA_H // 2)]


def _swa_scores(q8, kb, bias_ref, mask, cols):
    bias8 = jnp.concatenate([bias_ref[hq, :, cols] for hq in range(SWA_H)], axis=0)
    return jnp.where(mask, _dot_nt(q8, kb) + bias8, NEG_INF)


def _swa_sinks(sink_ref):
    return jnp.concatenate([jnp.broadcast_to(sink_ref[hq:hq + 1, 0:1], (BLK, 1)) for hq in range(SWA_H)], axis=0)


def _swa_fwd(proj, bias, sinks_b):
    t = proj.shape[0]
    nq = t // BLK

    def body(q_ref, k_ref, v_ref, bias_ref, sink_ref, o_ref, lse_ref):
        hm = _head_masks()
        band, own = _swa_band_masks()

        def qblock(i, prev):
            qs = pl.ds(pl.multiple_of(i * BLK, BLK), BLK)
            if prev:
                ks, mask, cols = pl.ds(pl.multiple_of((i - 1) * BLK, BLK), 2 * BLK), band, slice(None)
            else:
                ks, mask, cols = qs, own, slice(BLK, None)
            q8 = _swa_stack(q_ref, qs, hm, SCALE)
            sink8 = _swa_sinks(sink_ref)
            s = _swa_scores(q8, k_ref[ks, :], bias_ref, mask, cols)
            m = jnp.maximum(jnp.max(s, axis=1, keepdims=True), sink8)
            p = jnp.exp(s - m)
            den = jnp.sum(p, axis=1, keepdims=True) + jnp.exp(sink8 - m)
            o8 = _dot((p * (1.0 / den)).astype(BF16), v_ref[ks, :])
            lse8 = m + jnp.log(den)
            for hq in range(SWA_H):
                lse_ref[hq, qs, :] = lse8[hq * BLK:(hq + 1) * BLK]
            for pp, o in enumerate(_swa_unstack(o8, hm)):
                o_ref[qs, pp * LANES:(pp + 1) * LANES] = o

        qblock(0, False)

        def step(i, _):
            qblock(i, True)
            return 0

        lax.fori_loop(1, nq, step, 0)

    return _call(
        body, name="swa_fwd", grid=(1,),
        in_specs=[pl.BlockSpec((t, SWA_W), lambda i: (0, 3)), pl.BlockSpec((t, KV_W), lambda i: (0, 16)),
                  pl.BlockSpec((t, KV_W), lambda i: (0, 17)), pl.BlockSpec((8, BLK, 2 * BLK), lambda i: (0, 0, 0)),
                  pl.BlockSpec((8, LANES), lambda i: (0, 0))],
        out_specs=[pl.BlockSpec((t, SWA_W), lambda i: (0, 0)), pl.BlockSpec((8, t, 1), lambda i: (0, 0, 0))],
        out_shape=[jax.ShapeDtypeStruct((t, SWA_W), F32), jax.ShapeDtypeStruct((8, t, 1), F32)],
        compiler_params=_params(1))(proj, proj, proj, bias, sinks_b)


def _swa_bwd(proj, d_o, lse, bias, sinks_b, dbias_in):
    t = proj.shape[0]
    nq = t // BLK

    def body(q_ref, k_ref, v_ref, do_ref, lse_ref, bias_ref, sink_ref, dbi_ref,
             dq_ref, dk_ref, dv_ref, dsink_ref, dbias_ref, dk_acc, dv_acc):
        hm = _head_masks()
        band, own = _swa_band_masks()
        dk_acc[...] = jnp.zeros_like(dk_acc)
        dv_acc[...] = jnp.zeros_like(dv_acc)
        dbias_ref[...] = dbi_ref[...]

        def qblock(i, prev, dsink8):
            qs = pl.ds(pl.multiple_of(i * BLK, BLK), BLK)
            if prev:
                ks, mask, cols = pl.ds(pl.multiple_of((i - 1) * BLK, BLK), 2 * BLK), band, slice(None)
            else:
                ks, mask, cols = qs, own, slice(BLK, None)
            q8 = _swa_stack(q_ref, qs, hm, SCALE)
            do8 = _swa_stack(do_ref, qs, hm, 1.0)
            sink8 = _swa_sinks(sink_ref)
            lse8 = jnp.concatenate([lse_ref[hq, qs, :] for hq in range(SWA_H)], axis=0)
            kb = k_ref[ks, :]
            p = jnp.exp(_swa_scores(q8, kb, bias_ref, mask, cols) - lse8)
            dp = _dot_nt(do8, v_ref[ks, :])
            delta = jnp.sum(p * dp, axis=1, keepdims=True)
            ds = p * (dp - delta)
            for hq in range(SWA_H):
                dbias_ref[hq, :, cols] += ds[hq * BLK:(hq + 1) * BLK]
            dsb = ds.astype(BF16)
            dk_acc[ks, :] += _dot_tn(dsb, q8)
            dv_acc[ks, :] += _dot_tn(p.astype(BF16), do8)
            for pp, dq in enumerate(_swa_unstack(_dot(dsb, kb) * SCALE, hm)):
                dq_ref[qs, pp * LANES:(pp + 1) * LANES] = dq.astype(BF16)
            return dsink8 - jnp.exp(sink8 - lse8) * delta

        ds0 = qblock(0, False, jnp.zeros((SWA_H * BLK, 1), F32))
        ds8 = lax.fori_loop(1, nq, lambda i, c: qblock(i, True, c), ds0)
        for hq in range(SWA_H):
            dsink_ref[hq:hq + 1, :] = jnp.broadcast_to(
                jnp.sum(ds8[hq * BLK:(hq + 1) * BLK], axis=0, keepdims=True), (1, LANES))

        dk_ref[...] = dk_acc[...].astype(BF16)
        dv_ref[...] = dv_acc[...].astype(BF16)

    full3 = pl.BlockSpec((8, BLK, 2 * BLK), lambda i: (0, 0, 0))
    kv = jax.ShapeDtypeStruct((t, KV_W), BF16)
    return _call(
        body, name="swa_bwd", grid=(1,),
        in_specs=[pl.BlockSpec((t, SWA_W), lambda i: (0, 3)), pl.BlockSpec((t, KV_W), lambda i: (0, 16)),
                  pl.BlockSpec((t, KV_W), lambda i: (0, 17)), pl.BlockSpec((t, SWA_W), lambda i: (0, 1)),
                  pl.BlockSpec((8, t, 1), lambda i: (0, 0, 0)), full3, pl.BlockSpec((8, LANES), lambda i: (0, 0)),
                  full3],
        out_specs=[pl.BlockSpec((t, SWA_W), lambda i: (0, 0)), pl.BlockSpec((t, KV_W), lambda i: (0, 0)),
                   pl.BlockSpec((t, KV_W), lambda i: (0, 0)), pl.BlockSpec((8, LANES), lambda i: (0, 0)), full3],
        out_shape=[jax.ShapeDtypeStruct((t, SWA_W), BF16), kv, kv, jax.ShapeDtypeStruct((8, LANES), F32),
                   jax.ShapeDtypeStruct((8, BLK, 2 * BLK), F32)],
        scratch_shapes=[pltpu.VMEM((t, KV_W), F32), pltpu.VMEM((t, KV_W), F32)],
        compiler_params=_params(1))(proj, proj, proj, d_o, lse, bias, sinks_b, dbias_in)


def _concat_cols(parts):
    t = parts[0].shape[0]
    widths = [a.shape[1] for a in parts]

    def body(*refs):
        refs[-1][...] = jnp.concatenate([r[...] for r in refs[:-1]], axis=1)

    return _call(
        body, name="concat_cols", grid=(t // TM,),
        in_specs=[pl.BlockSpec((TM, w), lambda i: (i, 0)) for w in widths],
        out_specs=pl.BlockSpec((TM, sum(widths)), lambda i: (i, 0)),
        out_shape=jax.ShapeDtypeStruct((t, sum(widths)), parts[0].dtype), compiler_params=_params(1))(*parts)


def _bias_table(rel_bias, buckets):
    def body(rb_ref, b_ref, o_ref):
        bk = b_ref[...]
        for h in range(8):
            acc = jnp.zeros((BLK, 2 * BLK), F32)
            for b in range(N_BUCKETS):
                acc = jnp.where(bk == b, rb_ref[b, h], acc)
            o_ref[h] = acc

    return _call(
        body, name="bias_table", grid=(1,),
        in_specs=[pl.BlockSpec(memory_space=pltpu.SMEM), pl.BlockSpec((BLK, 2 * BLK), lambda i: (0, 0))],
        out_specs=pl.BlockSpec((8, BLK, 2 * BLK), lambda i: (0, 0, 0)),
        out_shape=jax.ShapeDtypeStruct((8, BLK, 2 * BLK), F32), compiler_params=_params(1))(rel_bias, buckets)


def _bias_grad(dbias, buckets):
    def body(d_ref, b_ref, o_ref):
        lane = lax.broadcasted_iota(jnp.int32, (1, LANES), 1)
        bk = b_ref[...]
        for h in range(8):
            d = d_ref[h]
            acc = jnp.zeros((1, LANES), F32)
            for b in range(N_BUCKETS):
                s = jnp.sum(jnp.sum(jnp.where(bk == b, d, 0.0), axis=0, keepdims=True), axis=1, keepdims=True)
                acc = acc + jnp.where(lane == b, s, 0.0)
            o_ref[h:h + 1, :] = acc

    return _call(
        body, name="bias_grad", grid=(1,),
        in_specs=[pl.BlockSpec((8, BLK, 2 * BLK), lambda i: (0, 0, 0)), pl.BlockSpec((BLK, 2 * BLK), lambda i: (0, 0))],
        out_specs=pl.BlockSpec((8, LANES), lambda i: (0, 0)),
        out_shape=jax.ShapeDtypeStruct((8, LANES), F32), compiler_params=_params(1))(dbias, buckets)


def _row(a):
    return a.reshape(1, -1)


def _fwd_ffn1_gu(h, n1, w):
    s = {"h0": h, "n1": n1}
    s["gu1"], s["act1"] = _ffn_gu(n1, w["ffn1_gu"])
    return s


def _fwd_ffn1_down(s, w, small, l):
    s["h1"], s["nm"] = _down_res(s["act1"], w["ffn1_down"], s["h0"], _row(small["norm_mix"][l]))


def _fwd_ffn1(h, n1, w, small, l):
    s = _fwd_ffn1_gu(h, n1, w)
    _fwd_ffn1_down(s, w, small, l)
    return s


def _fwd_proj_sb(s, w):
    s["proj"] = _proj(s["nm"], w["w_in"])
    s["o_sb"], s["tot"] = _sb_fwd(s["proj"])


def _fwd_swa(s, small, l, bias):
    s["sinks_b"] = jnp.broadcast_to(small["sinks"][l][:, None], (8, LANES))
    s["o_sw"], s["lse"] = _swa_fwd(s["proj"], bias, s["sinks_b"])


def _fwd_out_gu2(s, w, small, l):
    s["h2"], s["mixed"], s["n2"] = _out_res(
        s["o_sb"], s["o_sw"], _row(small["norm_out_sb"][l]), _row(small["norm_out_swa"][l]), w["w_out"], s["h1"],
        _row(small["norm_ffn2"][l]))
    s["gu2"], s["act2"] = _ffn_gu(s["n2"], w["ffn2_gu"])


def _fwd_ffn2_down(s, w, g_after):
    return _down_res(s["act2"], w["ffn2_down"], s["h2"], g_after)


def _fwd_out_ffn2(s, w, small, l, g_after):
    _fwd_out_gu2(s, w, small, l)
    return _fwd_ffn2_down(s, w, g_after)


def _bwd_ffn_dact(dh, s, w, which):
    return _ffn_dact(dh[1], w[f"ffn{which}_down"], s[f"gu{which}"])


def _bwd_ffn_rest(dh, dgu, s, w, small, l, which):
    h_in, norm = (s["h0"], "norm_ffn1") if which == 1 else (s["h2"], "norm_ffn2")
    g_down = _wgrad_down(s[f"act{which}"], dh[1])
    g_gu = _wgrad_gu(s[f"n{which}"], dgu)
    dh32, dh16, dg = _ffn_dn(dgu, w[f"ffn{which}_gu"], dh[0], h_in, _row(small[norm][l]))
    return (dh32, dh16), {f"ffn{which}_down": g_down, f"ffn{which}_gu": g_gu}, {norm: dg}


def _bwd_ffn(dh, s, w, small, l, which):
    return _bwd_ffn_rest(dh, _bwd_ffn_dact(dh, s, w, which), s, w, small, l, which)


def _bwd_mix(dh, s, w, small, l, bias, dbias):
    g_out = _wgrad_out(s["mixed"], dh[1])
    d_o, dg_sb, dg_sw = _dmixed(dh[1], w["w_out"], s["o_sb"], s["o_sw"], _row(small["norm_out_sb"][l]),
                                _row(small["norm_out_swa"][l]))
    dq_sb, dk_sb, dv_sb = _sb_bwd(s["proj"], d_o, s["tot"])
    dq_sw, dk_sw, dv_sw, dsink, dbias = _swa_bwd(s["proj"], d_o, s["lse"], bias, s["sinks_b"], dbias)
    dproj = _concat_cols([dq_sb, dk_sb, dv_sb, dq_sw, dk_sw, dv_sw])
    g_in = _wgrad_in(s["nm"], dproj)
    dh32, dh16, dg_mix = _mix_dn(dproj, w["w_in"], dh[0], s["h1"], _row(small["norm_mix"][l]))
    gs = {"norm_out_sb": dg_sb, "norm_out_swa": dg_sw, "sinks": dsink[:, 0], "norm_mix": dg_mix}
    return (dh32, dh16), {"w_out": g_out, "w_in": g_in}, gs, dbias


def _place():
    x, y, c = lax.axis_index("x"), lax.axis_index("y"), lax.axis_index("c")
    return x, y, c, 2 * x + y


def _chip_core(k, c):
    return (k // 2, k % 2, c)


def _rows_per_block(rows, cols, copies):
    best = 16
    for tr in range(16, rows + 1, 16):
        if rows % tr == 0 and copies * tr * cols * 4 <= SLAB_BLOCK_BYTES:
            best = tr
    assert rows % best == 0
    return best


def _place_own(w, l, me1):
    _, rows, cols = w.shape
    tr = _rows_per_block(rows // 2, cols, 1)
    per_half = rows // 2 // tr

    def body(me_ref, w_ref, o_ref):
        o_ref[...] = w_ref[...].astype(BF16)

    return _call(
        body, name="place_own",
        num_scalar_prefetch=1, grid=(rows // tr,),
        in_specs=[pl.BlockSpec((None, tr, cols), lambda r, me: (l, r, 0))],
        out_specs=pl.BlockSpec((None, None, tr, cols), lambda r, me: (me[0], r // per_half, r % per_half, 0)),
        out_shape=jax.ShapeDtypeStruct((N_CHIPS, 2, rows // 2, cols), BF16), compiler_params=_params(1))(me1, w)


def _plan_gather_ici(bufs):
    _, _, c, me = _place()
    return [(b.at[me, c], b.at[me, c], b.at[(me + 3 - j) % N_CHIPS, c], _chip_core((me + 1 + j) % N_CHIPS, c))
            for b in bufs for j in range(3)]


def _plan_gather_d2d(bufs):
    x, y, c, me = _place()
    return [(b.at[(me + 3 - j) % N_CHIPS, c], b.at[(me + 3 - j) % N_CHIPS, c], b.at[(me + 3 - j) % N_CHIPS, 1 - c],
             (x, y, 1 - c)) for b in bufs for j in range(3)]


def _plan_grad_sibling(bufs):
    x, y, c, _ = _place()
    n = len(bufs) // 2
    return [(g.at[:, 1 - c], z, z, (x, y, 1 - c)) for g, z in zip(bufs[:n], bufs[n:])]


def _plan_grad_chips(bufs):
    _, _, c, me = _place()
    n = len(bufs) // 2
    return [(p.at[j], z.at[j], z.at[j], _chip_core((me + 1 + j) % N_CHIPS, c))
            for p, z in zip(bufs[:n], bufs[n:]) for j in range(3)]


def _plan_grad_halves(bufs):
    x, y, c, _ = _place()
    return [(b.at[c], b.at[c], b.at[1 - c], (x, y, 1 - c)) for b in bufs]


def _remote(src, dst, send_sem, recv_sem, to):
    return pltpu.make_async_remote_copy(src_ref=src, dst_ref=dst, send_sem=send_sem, recv_sem=recv_sem,
                                        device_id=to, device_id_type=MESH)


SIBLING_BARRIER_ID = 0


def _sibling_handshake():
    x, y, c, _ = _place()
    barrier = pltpu.get_barrier_semaphore()
    pl.semaphore_signal(barrier, inc=1, device_id=(x, y, 1 - c), device_id_type=MESH)
    pl.semaphore_wait(barrier, 1)


def _split_params(sibling_only):
    return pltpu.CompilerParams(has_side_effects=EFFECT, collective_id=SIBLING_BARRIER_ID if sibling_only else None)


def _exchange_start_groups(name, plan, groups, sibling_only=False):
    sizes = [len(g) for g, _ in groups]
    bufs = [a for g, _ in groups for a in g]
    n, n_groups = len(bufs), len(groups)

    def body(*refs):
        if sibling_only:
            _sibling_handshake()
        ins, sems, token = refs[:n], refs[n:n + 2 * n_groups], refs[-1]
        at = 0
        for k, size in enumerate(sizes):
            for i, (src, dst, _, to) in enumerate(plan(ins[at:at + size])):
                _remote(src, dst, sems[2 * k].at[i], sems[2 * k + 1].at[i], to).start()
            at += size
        token[...] = jnp.zeros_like(token)

    sem_shapes = [pltpu.SemaphoreType.DMA((n_copies,)) for _, n_copies in groups for _ in range(2)]
    out = _call(
        body, name=name,
        out_shape=(*sem_shapes, *[pltpu.HBM(a.shape, a.dtype) for a in bufs], jax.ShapeDtypeStruct((8, LANES), F32)),
        in_specs=[HBM] * n,
        out_specs=(*[SEM] * (2 * n_groups), *[HBM] * n, pl.BlockSpec(memory_space=pltpu.VMEM)),
        input_output_aliases={t: 2 * n_groups + t for t in range(n)}, hbm_args=n,
        compiler_params=_split_params(sibling_only),
    )(*bufs)
    flights, at = [], 2 * n_groups
    for k, size in enumerate(sizes):
        flights.append(((out[2 * k], out[2 * k + 1]), list(out[at:at + size])))
        at += size
    return flights


def _exchange_start(name, plan, bufs, n_copies, sibling_only=False):
    return _exchange_start_groups(name, plan, [(bufs, n_copies)], sibling_only)[0]


def _exchange_wait(name, plan, bufs, sems):
    n = len(bufs)

    def body(*refs):
        ins = refs[:n]
        ssem, rsem = refs[n], refs[n + 1]
        for i, (src, dst, land, to) in enumerate(plan(ins)):
            _remote(src, dst, ssem.at[i], rsem.at[i], to).wait_send()
            _remote(land, land, ssem.at[i], rsem.at[i], to).wait_recv()

    return list(_call(
        body, name=name, out_shape=[pltpu.HBM(a.shape, a.dtype) for a in bufs],
        in_specs=[HBM] * n + [SEM, SEM], out_specs=[HBM] * n,
        input_output_aliases={t: t for t in range(n)},
        compiler_params=pltpu.CompilerParams(has_side_effects=EFFECT),
    )(*bufs, sems[0], sems[1]))


def _exchange_pass(name, done, plan, bufs, sems, n_copies):
    n = len(bufs)

    def body(*refs):
        _sibling_handshake()
        ins = refs[:n]
        old_s, old_r, ssem, rsem = refs[n], refs[n + 1], refs[n + 2], refs[n + 3]
        token = refs[-1]
        for i, (src, dst, land, to) in enumerate(done(ins)):
            _remote(src, dst, old_s.at[i], old_r.at[i], to).wait_send()
            _remote(land, land, old_s.at[i], old_r.at[i], to).wait_recv()
        for i, (src, dst, _, to) in enumerate(plan(ins)):
            _remote(src, dst, ssem.at[i], rsem.at[i], to).start()
        token[...] = jnp.zeros_like(token)

    out = _call(
        body, name=name,
        out_shape=(pltpu.SemaphoreType.DMA((n_copies,)), pltpu.SemaphoreType.DMA((n_copies,)),
                   *[pltpu.HBM(a.shape, a.dtype) for a in bufs], jax.ShapeDtypeStruct((8, LANES), F32)),
        in_specs=[HBM] * n + [SEM, SEM], out_specs=(SEM, SEM, *[HBM] * n, pl.BlockSpec(memory_space=pltpu.VMEM)),
        input_output_aliases={t: 2 + t for t in range(n)},
        compiler_params=_split_params(True),
    )(*bufs, sems[0], sems[1])
    return (out[0], out[1]), list(out[2:2 + n])


def _chip_sum(g, xbuf, cm):
    _, _, r2, cols = g.shape
    tr = _rows_per_block(r2, cols, 1)

    def body(cm_ref, g_ref, x_ref, o_ref):
        o_ref[...] = (g_ref[...] + x_ref[...]).astype(BF16)

    return _call(
        body, name="grad_chip_sum",
        num_scalar_prefetch=1, grid=(3, r2 // tr),
        in_specs=[pl.BlockSpec((None, None, tr, cols), lambda j, r, cm: ((cm[1] + 1 + j) % N_CHIPS, cm[0], r, 0)),
                  pl.BlockSpec((None, tr, cols), lambda j, r, cm: ((cm[1] + 1 + j) % N_CHIPS, r, 0))],
        out_specs=pl.BlockSpec((None, tr, cols), lambda j, r, cm: (j, r, 0)),
        out_shape=jax.ShapeDtypeStruct((3, r2, cols), BF16), compiler_params=_params(2))(cm, g, xbuf)


def _total_sum(g, xbuf, rbuf, cm):
    _, _, r2, cols = g.shape
    tr = _rows_per_block(r2, cols, 3)

    def body(cm_ref, g_ref, x_ref, r_ref, o_ref):
        acc = g_ref[...] + x_ref[...]
        for j in range(3):
            acc = acc + r_ref[j].astype(F32)
        o_ref[...] = acc

    return _call(
        body, name="grad_total_sum",
        num_scalar_prefetch=1, grid=(r2 // tr,),
        in_specs=[pl.BlockSpec((None, None, tr, cols), lambda r, cm: (cm[1], cm[0], r, 0)),
                  pl.BlockSpec((None, tr, cols), lambda r, cm: (cm[1], r, 0)),
                  pl.BlockSpec((3, tr, cols), lambda r, cm: (0, r, 0))],
        out_specs=pl.BlockSpec((None, tr, cols), lambda r, cm: (cm[0], r, 0)),
        out_shape=jax.ShapeDtypeStruct((2, r2, cols), F32), compiler_params=_params(1))(cm, g, xbuf, rbuf)


def _small_allreduce(v):
    rows = v.shape[0]
    n_dev = 2 * N_CHIPS

    def body(v_ref, o_ref, buf, ssem, rsem):
        x, y, c, _ = _place()
        me = 4 * x + 2 * y + c
        buf[me] = v_ref[...]

        def copy(d, slot, to):
            return _remote(v_ref, buf.at[slot], ssem.at[d - 1], rsem.at[d - 1], (to // 4, (to // 2) % 2, to % 2))

        cps = [copy(d, me, (me + d) % n_dev) for d in range(1, n_dev)]
        for cp in cps:
            cp.start()
        for d in range(1, n_dev):
            copy(d, (me + n_dev - d) % n_dev, me).wait_recv()
        for cp in cps:
            cp.wait_send()
        acc = buf[0]
        for i in range(1, n_dev):
            acc = acc + buf[i]
        o_ref[...] = acc

    vm = pl.BlockSpec(memory_space=pltpu.VMEM)
    return _call(
        body, name="small_allreduce", in_specs=[vm], out_specs=vm,
        out_shape=jax.ShapeDtypeStruct(v.shape, F32),
        scratch_shapes=[pltpu.VMEM((n_dev, rows, LANES), F32), pltpu.SemaphoreType.DMA((n_dev - 1,)),
                        pltpu.SemaphoreType.DMA((n_dev - 1,))],
        compiler_params=pltpu.CompilerParams(vmem_limit_bytes=V7X_VMEM_LIMIT))(v)


def _adamw_math(w, g, m, v):
    m2 = ADAM_B1 * m + (1.0 - ADAM_B1) * g
    v2 = ADAM_B2 * v + (1.0 - ADAM_B2) * (g * g)
    v_hat = v2 / (1.0 - ADAM_B2 ** ADAM_STEP)
    step = (-ADAM_LR / (1.0 - ADAM_B1 ** ADAM_STEP)) * m2 / (jnp.sqrt(v_hat) + ADAM_EPS)
    return step + (-ADAM_LR * ADAM_WD) * w, m2, v2


def _adamw_layer(w, g, m, v, l, prev):
    _, rows, cols = w.shape
    tr = rows
    for cand in range(8, rows + 1, 8):
        if rows % cand == 0 and cand * cols * 4 <= ADAMW_BLOCK_BYTES:
            tr = cand

    def body(w_ref, g_ref, m_ref, v_ref, *outs):
        go_ref, d_ref, m2_ref, v2_ref = outs[-4:]
        g = g_ref[...]
        go_ref[...] = g
        d_ref[...], m2_ref[...], v2_ref[...] = _adamw_math(w_ref[...], g, m_ref[...], v_ref[...])

    stack = pl.BlockSpec((None, tr, cols), lambda i: (l, i, 0))
    ins, specs, alias = [w, g, m, v], [stack, pl.BlockSpec((tr, cols), lambda i: (i, 0)), stack, stack], {}
    if prev is not None:
        ins += list(prev)
        specs += [ANY] * 4
        alias = {4 + i: i for i in range(4)}
    return _call(
        body, name="adamw", grid=(rows // tr,), in_specs=specs, out_specs=[stack] * 4,
        out_shape=[jax.ShapeDtypeStruct(w.shape, F32)] * 4, input_output_aliases=alias,
        compiler_params=_params(1))(*ins)


def _adamw_small(w, g, m, v):
    def body(w_ref, g_ref, m_ref, v_ref, d_ref, m2_ref, v2_ref):
        d_ref[...], m2_ref[...], v2_ref[...] = _adamw_math(w_ref[...], g_ref[...], m_ref[...], v_ref[...])

    spec = pl.BlockSpec(w.shape, lambda i: (0, 0))
    return _call(
        body, name="adamw_small", grid=(1,), in_specs=[spec] * 4, out_specs=[spec] * 3,
        out_shape=[jax.ShapeDtypeStruct(w.shape, F32)] * 3, compiler_params=_params(1))(w, g, m, v)


SMALL = ("norm_ffn1", "norm_mix", "sinks", "norm_out_sb", "norm_out_swa", "norm_ffn2", "rel_bias", "norm_final")
BIG = ("ffn1_gu", "ffn1_down", "w_in", "w_out", "ffn2_gu", "ffn2_down")


def _pack(parts):
    flat, n = [], 0
    for a in parts:
        a = a.reshape(-1).astype(F32)
        gap = -a.shape[0] % LANES
        flat += [a] + ([jnp.zeros((gap,), F32)] if gap else [])
        n += a.shape[0] + gap
    tail = -(n // LANES) % 8 * LANES
    return jnp.concatenate(flat + ([jnp.zeros((tail,), F32)] if tail else [])).reshape(-1, LANES)


def _unpack(packed, like):
    out, r = [], 0
    for a in like:
        n = math.prod(a.shape)
        nr = -(-n // LANES)
        out.append(packed[r:r + nr].reshape(-1)[:n].reshape(a.shape))
        r += nr
    return out


def _halved(a):
    k, r, cols = a.shape
    return a.reshape(k, 2, r // 2, cols)


def _weight_view(k, buf):
    full = buf.reshape(N_CHIPS, buf.shape[2] * 2, buf.shape[3])
    return full if k.endswith("_gu") else full.reshape(-1, D_MODEL)


def _grad_stack(k, g):
    if not k.endswith("_gu"):
        g = g.reshape(N_CHIPS, g.shape[0] // N_CHIPS, D_MODEL)
    return _halved(g)


def _empty_like_hbm(shape, dtype):
    return pltpu.with_memory_space_constraint(lax.empty(shape, dtype), pltpu.HBM)


def kernel(x, norm_ffn1, w_ffn1_gu, w_ffn1_down, norm_mix, w_in, sinks, norm_out_sb, norm_out_swa, w_out, norm_ffn2, w_ffn2_gu, w_ffn2_down, rel_bias, norm_final, loss_target, m_norm_ffn1, m_w_ffn1_gu, m_w_ffn1_down, m_norm_mix, m_w_in, m_sinks, m_norm_out_sb, m_norm_out_swa, m_w_out, m_norm_ffn2, m_w_ffn2_gu, m_w_ffn2_down, m_rel_bias, m_norm_final, v_norm_ffn1, v_w_ffn1_gu, v_w_ffn1_down, v_norm_mix, v_w_in, v_sinks, v_norm_out_sb, v_norm_out_swa, v_w_out, v_norm_ffn2, v_w_ffn2_gu, v_w_ffn2_down, v_rel_bias, v_norm_final):
    big_w = dict(ffn1_gu=w_ffn1_gu, ffn1_down=w_ffn1_down, w_in=w_in, w_out=w_out, ffn2_gu=w_ffn2_gu, ffn2_down=w_ffn2_down)
    big_m = dict(ffn1_gu=m_w_ffn1_gu, ffn1_down=m_w_ffn1_down, w_in=m_w_in, w_out=m_w_out, ffn2_gu=m_w_ffn2_gu, ffn2_down=m_w_ffn2_down)
    big_v = dict(ffn1_gu=v_w_ffn1_gu, ffn1_down=v_w_ffn1_down, w_in=v_w_in, w_out=v_w_out, ffn2_gu=v_w_ffn2_gu, ffn2_down=v_w_ffn2_down)
    small = dict(norm_ffn1=norm_ffn1, norm_mix=norm_mix, sinks=sinks, norm_out_sb=norm_out_sb, norm_out_swa=norm_out_swa,
                 norm_ffn2=norm_ffn2, rel_bias=rel_bias, norm_final=norm_final)
    small_m = dict(norm_ffn1=m_norm_ffn1, norm_mix=m_norm_mix, sinks=m_sinks, norm_out_sb=m_norm_out_sb,
                   norm_out_swa=m_norm_out_swa, norm_ffn2=m_norm_ffn2, rel_bias=m_rel_bias, norm_final=m_norm_final)
    small_v = dict(norm_ffn1=v_norm_ffn1, norm_mix=v_norm_mix, sinks=v_sinks, norm_out_sb=v_norm_out_sb,
                   norm_out_swa=v_norm_out_swa, norm_ffn2=v_norm_ffn2, rel_bias=v_rel_bias, norm_final=v_norm_final)
    for dct in (big_w, big_m, big_v):
        dct["w_in"] = jnp.swapaxes(dct["w_in"], 1, 2)
    _PREVIOUS[0] = None
    _, _, c, me = _place()
    cm = jnp.stack([c, me]).astype(jnp.int32)
    buckets = jnp.asarray(_bucket_table())
    ffn1, mix_in, rest = ("ffn1_gu", "ffn1_down"), ("w_in",), ("w_out", "ffn2_gu", "ffn2_down")

    def place(l, keys):
        return [_place_own(big_w[k], l, cm[1:]) for k in keys]

    def views(keys, bufs):
        return {k: _weight_view(k, b) for k, b in zip(keys, bufs)}

    def gather_start(tag, bufs):
        return _exchange_start(f"gather{tag}_ici_start", _plan_gather_ici, bufs, 3 * len(bufs))

    def gather_pass(tag, flight):
        return _exchange_pass(f"gather{tag}_pass", _plan_gather_ici, _plan_gather_d2d, flight[1], flight[0],
                              3 * len(flight[1]))

    def gather_done(tag, keys, flight):
        return views(keys, _exchange_wait(f"gather{tag}_d2d_wait", _plan_gather_d2d, flight[1], flight[0]))

    fly_gu0 = gather_start("0a", place(0, ffn1[:1]))
    fly_down0 = gather_start("0a2", place(0, ffn1[1:]))
    fly_in0 = gather_start("0b", place(0, mix_in))
    later = [place(l, keys) for l in range(DEPTH) for keys in ((rest,) if l == 0 else (ffn1, mix_in, rest))]
    fly_rest0, fly_ffn1, fly_in1, fly_rest1 = _exchange_start_groups(
        "gather_later_ici_start", _plan_gather_ici, [(bufs, 3 * len(bufs)) for bufs in later])
    bias = _bias_table(rel_bias, buckets)
    n1 = _norm_cast(x[0], _row(norm_ffn1[0]))
    w0 = gather_done("0a", ffn1[:1], gather_pass("0a", fly_gu0))

    s0 = _fwd_ffn1_gu(x[0], n1, w0)
    w0.update(gather_done("0a2", ffn1[1:], gather_pass("0a2", fly_down0)))
    fly_in0 = gather_pass("0b", fly_in0)
    _fwd_ffn1_down(s0, w0, small, 0)
    w0.update(gather_done("0b", mix_in, fly_in0))
    _fwd_proj_sb(s0, w0)
    fly_rest0 = gather_pass("0c", fly_rest0)
    _fwd_swa(s0, small, 0, bias)
    w0.update(gather_done("0c", rest, fly_rest0))
    _fwd_out_gu2(s0, w0, small, 0)
    fly_ffn1 = gather_pass("1a", fly_ffn1)
    h, n1 = _fwd_ffn2_down(s0, w0, _row(norm_ffn1[1]))
    w1 = gather_done("1a", ffn1, fly_ffn1)
    fly_in1 = gather_pass("1b", fly_in1)
    s1 = _fwd_ffn1(h, n1, w1, small, 1)
    w1.update(gather_done("1b", mix_in, fly_in1))
    _fwd_proj_sb(s1, w1)
    fly_rest1 = gather_pass("1c", fly_rest1)
    _fwd_swa(s1, small, 1, bias)
    w1.update(gather_done("1c", rest, fly_rest1))
    h, _ = _fwd_out_ffn2(s1, w1, small, 1, _row(norm_final))
    dh32, dh16, dg_final, loss_row = _loss_head(h, _row(norm_final), loss_target[0])
    dh = (dh32, dh16)

    def landing(stacks, lead, dtype):
        return [_empty_like_hbm((lead,) + a.shape[2:], dtype) for a in stacks]

    def reduce_begin(tag, keys, gw):
        stacks = [_grad_stack(k, gw[k]) for k in keys]
        flight = _exchange_start(f"grad{tag}_sibling_start", _plan_grad_sibling,
                                 stacks + landing(stacks, N_CHIPS, F32), len(keys), sibling_only=True)
        return dict(tag=tag, keys=keys, stacks=stacks, flight=flight)

    def reduce_chips(st):
        n, (sems, bufs) = len(st["keys"]), st["flight"]
        bufs = _exchange_wait(f"grad{st['tag']}_sibling_wait", _plan_grad_sibling, bufs, sems)
        st["own"] = list(zip(bufs[:n], bufs[n:]))
        st["flight"] = _exchange_start(f"grad{st['tag']}_chips_start", _plan_grad_chips,
                                       [_chip_sum(g, z, cm) for g, z in st["own"]] + landing(st["stacks"], 3, BF16),
                                       3 * n)

    def reduce_halves(st):
        n, (sems, bufs) = len(st["keys"]), st["flight"]
        bufs = _exchange_wait(f"grad{st['tag']}_chips_wait", _plan_grad_chips, bufs, sems)
        halves = [_total_sum(g, x, z, cm) for (g, x), z in zip(st["own"], bufs[n:])]
        st["flight"] = _exchange_start(f"grad{st['tag']}_halves_start", _plan_grad_halves, halves, n,
                                       sibling_only=True)

    def reduce_end(st):
        sems, bufs = st["flight"]
        bufs = _exchange_wait(f"grad{st['tag']}_halves_wait", _plan_grad_halves, bufs, sems)
        return {k: b.reshape(big_w[k].shape[1:]) for k, b in zip(st["keys"], bufs)}

    def adamw(reduced, l, prev):
        return {k: _adamw_layer(big_w[k], g, big_m[k], big_v[k], l, None if prev is None else prev[k])
                for k, g in reduced.items()}

    gsm = [dict() for _ in range(DEPTH)]
    dbias = jnp.zeros((8, BLK, 2 * BLK), F32)
    dh, gw1, gs = _bwd_ffn(dh, s1, w1, small, 1, 2)
    gsm[1].update(gs)
    dh, gw, gs, dbias = _bwd_mix(dh, s1, w1, small, 1, bias, dbias)
    gw1.update(gw)
    gsm[1].update(gs)
    dh, gw, gs = _bwd_ffn(dh, s1, w1, small, 1, 1)
    gw1.update(gw)
    gsm[1].update(gs)

    red1 = reduce_begin("1", BIG, gw1)
    dh, gw0, gs = _bwd_ffn(dh, s0, w0, small, 0, 2)
    gsm[0].update(gs)
    reduce_chips(red1)
    dh, gw, gs, dbias = _bwd_mix(dh, s0, w0, small, 0, bias, dbias)
    gw0.update(gw)
    gsm[0].update(gs)
    red0a = reduce_begin("0a", ("ffn2_gu", "ffn2_down", "w_out", "w_in"), gw0)
    reduce_halves(red1)
    dgu = _bwd_ffn_dact(dh, s0, w0, 1)
    reduce_chips(red0a)
    dh, gw, gs = _bwd_ffn_rest(dh, dgu, s0, w0, small, 0, 1)
    gsm[0].update(gs)
    red0b = reduce_begin("0b", ffn1, gw)
    reduced1 = reduce_end(red1)
    stacks = adamw({k: reduced1[k] for k in ffn1}, 1, None)

    gsmall = {k: jnp.stack([gsm[l][k].reshape(-1) for l in range(DEPTH)]) for k in gsm[0]}
    gsmall["rel_bias"] = jnp.transpose(_bias_grad(dbias, buckets)[:, :N_BUCKETS])
    gsmall["norm_final"] = dg_final.reshape(-1)
    small_like = [small[k] for k in SMALL]
    pk = lambda dct: _pack([dct[k] for k in SMALL])
    red = _small_allreduce(_pack([gsmall[k] for k in SMALL] + [loss_row[0, :1]]))
    gs = _unpack(red, small_like + [loss_row[0, :1]])
    loss = gs[-1][0]
    gs = dict(zip(SMALL, gs[:-1]))

    ffn2 = ("ffn2_gu", "ffn2_down")
    reduce_chips(red0b)
    stacks.update(adamw({k: reduced1[k] for k in ("w_in", "w_out")}, 1, None))
    reduce_halves(red0a)
    stacks.update(adamw({k: reduced1[k] for k in ffn2}, 1, None))
    dlt, m2, v2 = _adamw_small(pk(small), pk(gs), pk(small_m), pk(small_v))
    reduced0a = reduce_end(red0a)
    stacks.update(adamw({k: reduced0a[k] for k in ffn2}, 0, stacks))
    reduce_halves(red0b)
    stacks.update(adamw({k: reduced0a[k] for k in ("w_in", "w_out")}, 0, stacks))
    stacks.update(adamw(reduce_end(red0b), 0, stacks))

    out_g, out_d, out_m, out_v = {}, {}, {}, {}
    for k in BIG:
        out_g[k], out_d[k], out_m[k], out_v[k] = [jnp.swapaxes(a, 1, 2) if k == "w_in" else a for a in stacks[k]]
    for dst, packed in ((out_d, dlt), (out_m, m2), (out_v, v2)):
        dst.update(zip(SMALL, _unpack(packed, small_like)))
    out_g.update(gs)

    order = ("norm_ffn1", "ffn1_gu", "ffn1_down", "norm_mix", "w_in", "sinks", "norm_out_sb", "norm_out_swa", "w_out",
             "norm_ffn2", "ffn2_gu", "ffn2_down", "rel_bias", "norm_final")
    return (loss, dh[0].reshape(x.shape), *[out_g[k] for k in order], *[out_d[k] for k in order],
            *[out_m[k] for k in order], *[out_v[k] for k in order])
```

```python
import math

import numpy as np
import jax
import jax.numpy as jnp
from jax import lax
from jax.experimental import pallas as pl
from jax.experimental.pallas import tpu as pltpu

F32 = jnp.float32
BF16 = jnp.bfloat16

D_MODEL = 1024
DEPTH = 2
HEAD_DIM = 64
BLK = 128
N_BUCKETS = 32
MAX_DISTANCE = 128
D_FF = 2816
EPS = 1e-6
NEG_INF = -1e30
SB_W = 512
SWA_W = 512
KV_W = 128
IN_W = 2304
SCALE = HEAD_DIM ** -0.5
N_CHIPS = 4
FS = 2 * D_FF // N_CHIPS
LANES = 128
V7X_VMEM_LIMIT = 56 * 2 ** 20
TM = 512
SLAB_BLOCK_BYTES = 6 * 2 ** 20
ADAMW_BLOCK_BYTES = 2 ** 21
SB_KT = 512
SWA_G = 4

ADAM_LR = 0.001
ADAM_B1 = 0.9
ADAM_B2 = 0.999
ADAM_EPS = 1e-08
ADAM_WD = 0.01
ADAM_STEP = 10

MESH = pl.DeviceIdType.MESH
ANY = pl.BlockSpec(memory_space=pl.ANY)
HBM = pl.BlockSpec(memory_space=pltpu.HBM)
SEM = pl.BlockSpec(memory_space=pltpu.SEMAPHORE)
EFFECT = pltpu.SideEffectType.DATAFLOW_SIDE_EFFECTING


def _params(n_grid):
    return pltpu.CompilerParams(dimension_semantics=("arbitrary",) * n_grid, vmem_limit_bytes=V7X_VMEM_LIMIT)


_PREVIOUS = [None]


def _call(body, *, name, in_specs, out_specs, out_shape, grid=(), num_scalar_prefetch=0, scratch_shapes=(),
          input_output_aliases=None, compiler_params=None, hbm_args=0):
    n_in = len(in_specs)

    def run(*args):
        dep = _PREVIOUS[0]
        if any(dep is a for a in args):
            dep = None
        args = [pltpu.with_memory_space_constraint(a, pltpu.HBM) if i < hbm_args else a for i, a in enumerate(args)]
        specs = list(in_specs) + ([ANY] if dep is not None else [])
        k = num_scalar_prefetch + n_in
        fn = body if dep is None else (lambda *refs: body(*refs[:k], *refs[k + 1:]))
        if num_scalar_prefetch:
            shape = dict(grid_spec=pltpu.PrefetchScalarGridSpec(
                num_scalar_prefetch=num_scalar_prefetch, grid=grid, in_specs=specs, out_specs=out_specs,
                scratch_shapes=scratch_shapes))
        else:
            shape = dict(grid=grid, in_specs=specs, out_specs=out_specs, scratch_shapes=scratch_shapes)
        out = pl.pallas_call(fn, name=name, out_shape=out_shape, input_output_aliases=input_output_aliases or {},
                             compiler_params=compiler_params, **shape)(*args, *([] if dep is None else [dep]))
        _PREVIOUS[0] = jax.tree.leaves(out)[-1]
        return out

    return run


def _dot(a, b):
    return jnp.dot(a, b, preferred_element_type=F32)


def _dot_nt(a, b):
    return lax.dot_general(a, b, (((1,), (1,)), ((), ())), preferred_element_type=F32)


def _dot_tn(a, b):
    return lax.dot_general(a, b, (((0,), (0,)), ((), ())), preferred_element_type=F32)


def _rms_fwd(x, g):
    r = lax.rsqrt(jnp.mean(x * x, axis=-1, keepdims=True) + EPS)
    xh = x * r
    return xh * g, xh, r


def _rms_bwd(dy, xh, r, g):
    u = dy * g
    dx = r * (u - xh * jnp.mean(u * xh, axis=-1, keepdims=True))
    dg = jnp.sum(dy * xh, axis=0, keepdims=True)
    return dx, dg


def _softplus(z):
    neg_abs = lax.bitcast_convert_type(lax.bitcast_convert_type(z, jnp.int32) | jnp.int32(-2 ** 31), F32)
    sp = jnp.maximum(z, 0.0) + jnp.log(1.0 + jnp.exp(neg_abs))
    return sp, z - sp


def _norm_cast(h, g):
    t, w = h.shape

    def body(h_ref, g_ref, n_ref):
        y, _, _ = _rms_fwd(h_ref[...], g_ref[...])
        n_ref[...] = y.astype(BF16)

    return _call(
        body, name="norm_cast", grid=(t // TM,),
        in_specs=[pl.BlockSpec((TM, w), lambda i: (i, 0)), pl.BlockSpec((1, w), lambda i: (0, 0))],
        out_specs=pl.BlockSpec((TM, w), lambda i: (i, 0)),
        out_shape=jax.ShapeDtypeStruct((t, w), BF16), compiler_params=_params(1))(h, g)


def _ffn_gu(n, wgu):
    t, d = n.shape

    def body(n_ref, wg_ref, wu_ref, gu_ref, act_ref):
        x = n_ref[...]
        g = _dot(x, wg_ref[...])
        u = _dot(x, wu_ref[...])
        sig = jax.nn.sigmoid(g)
        silu = g * sig
        gu_ref[0] = (u * (sig + silu * (1.0 - sig))).astype(BF16)
        gu_ref[1] = silu.astype(BF16)
        act_ref[...] = (silu * u).astype(BF16)

    return _call(
        body, name="ffn_gu", grid=(2, t // TM),
        in_specs=[pl.BlockSpec((TM, d), lambda j, i: (i, 0)),
                  pl.BlockSpec((None, d, FS), lambda j, i: (j, 0, 0)),
                  pl.BlockSpec((None, d, FS), lambda j, i: (j + 2, 0, 0))],
        out_specs=[pl.BlockSpec((2, TM, FS), lambda j, i: (0, i, j)), pl.BlockSpec((TM, FS), lambda j, i: (i, j))],
        out_shape=[jax.ShapeDtypeStruct((2, t, D_FF), BF16), jax.ShapeDtypeStruct((t, D_FF), BF16)],
        compiler_params=_params(2))(n, wgu, wgu)


def _down_res(act, wdn, h, g_next):
    t, f = act.shape
    d = h.shape[1]

    def body(a_ref, w_ref, h_ref, g_ref, o_ref, n_ref):
        out = h_ref[...] + 0.5 * _dot(a_ref[...], w_ref[...])
        o_ref[...] = out
        n_ref[...] = _rms_fwd(out, g_ref[...])[0].astype(BF16)

    row = pl.BlockSpec((TM, d), lambda i: (i, 0))
    return _call(
        body, name="down_res", grid=(t // TM,),
        in_specs=[pl.BlockSpec((TM, f), lambda i: (i, 0)), pl.BlockSpec((f, d), lambda i: (0, 0)), row,
                  pl.BlockSpec((1, d), lambda i: (0, 0))],
        out_specs=[row, row],
        out_shape=[jax.ShapeDtypeStruct((t, d), F32), jax.ShapeDtypeStruct((t, d), BF16)],
        compiler_params=_params(1))(act, wdn, h, g_next)


def _proj(n, w_in_t):
    t, d = n.shape
    w = w_in_t.shape[0]

    def body(n_ref, w_ref, o_ref):
        o_ref[...] = _dot_nt(n_ref[...], w_ref[...]).astype(BF16)

    return _call(
        body, name="proj", grid=(t // TM,),
        in_specs=[pl.BlockSpec((TM, d), lambda i: (i, 0)), pl.BlockSpec((w, d), lambda i: (0, 0))],
        out_specs=pl.BlockSpec((TM, w), lambda i: (i, 0)),
        out_shape=jax.ShapeDtypeStruct((t, w), BF16), compiler_params=_params(1))(n, w_in_t)


def _out_res(o_sb, o_sw, g_sb, g_sw, w_out, h, g_next):
    t, d = h.shape

    def body(a_ref, b_ref, ga_ref, gb_ref, w_ref, h_ref, g_ref, o_ref, mix_ref, n_ref):
        ya, _, _ = _rms_fwd(a_ref[...], ga_ref[...])
        yb, _, _ = _rms_fwd(b_ref[...], gb_ref[...])
        mixed = jnp.concatenate([ya.astype(BF16), yb.astype(BF16)], axis=1)
        mix_ref[...] = mixed
        out = h_ref[...] + _dot(mixed, w_ref[...])
        o_ref[...] = out
        n_ref[...] = _rms_fwd(out, g_ref[...])[0].astype(BF16)

    row = pl.BlockSpec((TM, d), lambda i: (i, 0))
    return _call(
        body, name="out_res", grid=(t // TM,),
        in_specs=[pl.BlockSpec((TM, SB_W), lambda i: (i, 0)), pl.BlockSpec((TM, SWA_W), lambda i: (i, 0)),
                  pl.BlockSpec((1, SB_W), lambda i: (0, 0)), pl.BlockSpec((1, SWA_W), lambda i: (0, 0)),
                  pl.BlockSpec((d, d), lambda i: (0, 0)), row, pl.BlockSpec((1, d), lambda i: (0, 0))],
        out_specs=[row, row, row],
        out_shape=[jax.ShapeDtypeStruct((t, d), F32), jax.ShapeDtypeStruct((t, d), BF16),
                   jax.ShapeDtypeStruct((t, d), BF16)],
        compiler_params=_params(1))(o_sb, o_sw, g_sb, g_sw, w_out, h, g_next)


def _loss_head(h, g, tgt):
    t, d = h.shape

    def body(h_ref, g_ref, t_ref, dh_ref, dhb_ref, dg_ref, loss_ref):
        @pl.when(pl.program_id(0) == 0)
        def _():
            dg_ref[...] = jnp.zeros_like(dg_ref)
            loss_ref[...] = jnp.zeros_like(loss_ref)

        gg = g_ref[...]
        y, xh, r = _rms_fwd(h_ref[...], gg)
        err = y - t_ref[...]
        part = 0.5 * jnp.sum(jnp.sum(err * err, axis=1, keepdims=True) / d, axis=0, keepdims=True)
        loss_ref[...] += jnp.broadcast_to(part, loss_ref.shape)
        dx, dg = _rms_bwd(err / d, xh, r, gg)
        dh_ref[...] = dx
        dhb_ref[...] = dx.astype(BF16)
        dg_ref[...] += dg

    row = pl.BlockSpec((TM, d), lambda i: (i, 0))
    return _call(
        body, name="loss_head", grid=(t // TM,),
        in_specs=[row, pl.BlockSpec((1, d), lambda i: (0, 0)), row],
        out_specs=[row, row, pl.BlockSpec((1, d), lambda i: (0, 0)), pl.BlockSpec((1, LANES), lambda i: (0, 0))],
        out_shape=[jax.ShapeDtypeStruct((t, d), F32), jax.ShapeDtypeStruct((t, d), BF16),
                   jax.ShapeDtypeStruct((1, d), F32), jax.ShapeDtypeStruct((1, LANES), F32)],
        compiler_params=_params(1))(h, g, tgt)


def _ffn_dact(dh, wdn, gu):
    t, d = dh.shape
    tm = TM

    def body(dh_ref, w_ref, gu_ref, o_ref):
        da = 0.5 * _dot_nt(dh_ref[...].astype(BF16), w_ref[...])
        o_ref[0] = (da * gu_ref[0].astype(F32)).astype(BF16)
        o_ref[1] = (da * gu_ref[1].astype(F32)).astype(BF16)

    return _call(
        body, name="ffn_dact", grid=(2, t // tm),
        in_specs=[pl.BlockSpec((tm, d), lambda j, i: (i, 0)), pl.BlockSpec((FS, d), lambda j, i: (j, 0)),
                  pl.BlockSpec((2, tm, FS), lambda j, i: (0, i, j))],
        out_specs=pl.BlockSpec((2, tm, FS), lambda j, i: (0, i, j)),
        out_shape=jax.ShapeDtypeStruct((2, t, D_FF), BF16), compiler_params=_params(2))(dh, wdn, gu)


def _dn_norm_bwd(a, a_spec, w, w_spec, nk, dh, h_in, g, w_transposed=False, tm=TM):
    t, d = dh.shape
    mm = _dot if w_transposed else _dot_nt

    def body(a_ref, w_ref, dh_ref, h_ref, g_ref, o_ref, ob_ref, dg_ref, acc_ref):
        i, k = pl.program_id(0), pl.program_id(1)

        if nk > 1:
            @pl.when(k == 0)
            def _():
                acc_ref[...] = mm(a_ref[...], w_ref[...])

            @pl.when((k > 0) & (k < nk - 1))
            def _():
                acc_ref[...] += mm(a_ref[...], w_ref[...])

        @pl.when(k == nk - 1)
        def _():
            gg = g_ref[...]
            dg = jnp.zeros_like(gg)
            for rows in (slice(r, r + TM // 2) for r in range(0, tm, TM // 2)):
                dn = mm(a_ref[rows, :], w_ref[...])
                if nk > 1:
                    dn = dn + acc_ref[rows, :]
                _, xh, r = _rms_fwd(h_ref[rows, :], gg)
                dx, dg_rows = _rms_bwd(dn, xh, r, gg)
                out = dh_ref[rows, :] + dx
                o_ref[rows, :] = out
                ob_ref[rows, :] = out.astype(BF16)
                dg = dg + dg_rows

            @pl.when(i == 0)
            def _():
                dg_ref[...] = dg

            @pl.when(i > 0)
            def _():
                dg_ref[...] += dg

    row = pl.BlockSpec((tm, d), lambda i, k: (i, 0))
    return _call(
        body, name="dn_norm_bwd", grid=(t // tm, nk),
        in_specs=[a_spec, w_spec, row, row, pl.BlockSpec((1, d), lambda i, k: (0, 0))],
        out_specs=[row, row, pl.BlockSpec((1, d), lambda i, k: (0, 0))],
        out_shape=[jax.ShapeDtypeStruct((t, d), F32), jax.ShapeDtypeStruct((t, d), BF16),
                   jax.ShapeDtypeStruct((1, d), F32)],
        scratch_shapes=[pltpu.VMEM((tm, d), F32)], compiler_params=_params(2))(a, w, dh, h_in, g)


def _ffn_dn(dgu, wgu, dh, h_in, g):
    d = dh.shape[1]
    tm = 2 * TM
    return _dn_norm_bwd(
        dgu, pl.BlockSpec((None, tm, FS), lambda i, k: (k // 2, i, k % 2)),
        wgu, pl.BlockSpec((None, d, FS), lambda i, k: (k, 0, 0)), N_CHIPS, dh, h_in, g, tm=tm)


def _mix_dn(dproj, w_in_t, dh, h_in, g):
    d = dh.shape[1]
    w = dproj.shape[1]
    return _dn_norm_bwd(
        dproj, pl.BlockSpec((TM, w), lambda i, k: (i, 0)),
        w_in_t, pl.BlockSpec((w, d), lambda i, k: (0, 0)), 1, dh, h_in, g, w_transposed=True)


def _dmixed(dh, w_out, o_sb, o_sw, g_sb, g_sw):
    t, d = dh.shape

    def body(dh_ref, w_ref, a_ref, b_ref, ga_ref, gb_ref, o_ref, dga_ref, dgb_ref):
        i = pl.program_id(0)
        dm = _dot_nt(dh_ref[...].astype(BF16), w_ref[...])
        _, xa, ra = _rms_fwd(a_ref[...], ga_ref[...])
        _, xb, rb = _rms_fwd(b_ref[...], gb_ref[...])
        da, dga = _rms_bwd(dm[:, :SB_W], xa, ra, ga_ref[...])
        db, dgb = _rms_bwd(dm[:, SB_W:], xb, rb, gb_ref[...])
        o_ref[...] = jnp.concatenate([da.astype(BF16), db.astype(BF16)], axis=1)

        @pl.when(i == 0)
        def _():
            dga_ref[...] = dga
            dgb_ref[...] = dgb

        @pl.when(i > 0)
        def _():
            dga_ref[...] += dga
            dgb_ref[...] += dgb

    return _call(
        body, name="dmixed", grid=(t // TM,),
        in_specs=[pl.BlockSpec((TM, d), lambda i: (i, 0)), pl.BlockSpec((d, d), lambda i: (0, 0)),
                  pl.BlockSpec((TM, SB_W), lambda i: (i, 0)), pl.BlockSpec((TM, SWA_W), lambda i: (i, 0)),
                  pl.BlockSpec((1, SB_W), lambda i: (0, 0)), pl.BlockSpec((1, SWA_W), lambda i: (0, 0))],
        out_specs=[pl.BlockSpec((TM, d), lambda i: (i, 0)), pl.BlockSpec((1, SB_W), lambda i: (0, 0)),
                   pl.BlockSpec((1, SWA_W), lambda i: (0, 0))],
        out_shape=[jax.ShapeDtypeStruct((t, d), BF16), jax.ShapeDtypeStruct((1, SB_W), F32),
                   jax.ShapeDtypeStruct((1, SWA_W), F32)],
        compiler_params=_params(1))(dh, w_out, o_sb, o_sw, g_sb, g_sw)


def _wgrad(name, a, a_spec, b, b_spec, grid, out_shape, out_spec, scale):
    def body(a_ref, b_ref, o_ref):
        r = _dot_tn(a_ref[...], b_ref[...].astype(BF16))
        o_ref[...] = r if scale == 1.0 else scale * r

    return _call(
        body, name=name, grid=grid, in_specs=[a_spec, b_spec], out_specs=out_spec,
        out_shape=jax.ShapeDtypeStruct(out_shape, F32), compiler_params=_params(len(grid)))(a, b)


def _wgrad_gu(n, dgu):
    t, d = n.shape
    return _wgrad(
        "wgrad_gu", n, pl.BlockSpec((t, TM), lambda s, r: (0, r)),
        dgu, pl.BlockSpec((None, t, FS), lambda s, r: (s // 2, 0, s % 2)), (N_CHIPS, d // TM),
        (N_CHIPS, d, FS), pl.BlockSpec((None, TM, FS), lambda s, r: (s, r, 0)), 1.0)


def _wgrad_down(act, dh):
    t, d = dh.shape
    return _wgrad(
        "wgrad_down", act, pl.BlockSpec((t, FS), lambda s: (0, s)), dh, pl.BlockSpec((t, d), lambda s: (0, 0)),
        (2,), (D_FF, d), pl.BlockSpec((FS, d), lambda s: (s, 0)), 0.5)


def _wgrad_out(mixed, dh):
    t, d = dh.shape
    return _wgrad(
        "wgrad_out", mixed, pl.BlockSpec((t, TM), lambda s: (0, s)), dh, pl.BlockSpec((t, d), lambda s: (0, 0)),
        (d // TM,), (d, d), pl.BlockSpec((TM, d), lambda s: (s, 0)), 1.0)


def _wgrad_in(n, dproj):
    t, d = n.shape
    w = dproj.shape[1]
    tw = w // 3
    return _wgrad(
        "wgrad_in", dproj, pl.BlockSpec((t, tw), lambda s: (0, s)), n, pl.BlockSpec((t, d), lambda s: (0, 0)),
        (3,), (w, d), pl.BlockSpec((tw, d), lambda s: (s, 0)), 1.0)


def _tri(rel):
    row = lax.broadcasted_iota(jnp.int32, (BLK, BLK), 0)
    col = lax.broadcasted_iota(jnp.int32, (BLK, BLK), 1)
    m = rel(row, col).astype(BF16)
    return jnp.concatenate([m, m], axis=0)


def _scan_dot(x, tri2):
    hi = x.astype(BF16)
    lo = (x - hi.astype(F32)).astype(BF16)
    return _dot(jnp.concatenate([hi, lo], axis=1), tri2)


def _head_masks():
    lane = lax.broadcasted_iota(jnp.int32, (1, LANES), 1)
    return [lane < HEAD_DIM, lane >= HEAD_DIM]


SB_PAIRS = 2
SB_ROWS = 2 * SB_PAIRS * BLK


def _sb_causal():
    row = lax.broadcasted_iota(jnp.int32, (SB_ROWS, BLK), 0) & (BLK - 1)
    return lax.broadcasted_iota(jnp.int32, (SB_ROWS, BLK), 1) < row


def _sb_mask_last(x, causal):
    own = jnp.where(causal, x[:, -BLK:], 0.0)
    return own if x.shape[1] == BLK else jnp.concatenate([x[:, :-BLK], own], axis=1)


def _sb_stack(x, hm):
    return jnp.concatenate([jnp.where(m, x[:, p * LANES:(p + 1) * LANES], jnp.zeros((BLK, LANES), x.dtype))
                            for p in range(SB_PAIRS) for m in hm], axis=0)


def _sb_unstack(y, hm):
    return jnp.concatenate([jnp.where(hm[0], y[2 * p * BLK:(2 * p + 1) * BLK], y[(2 * p + 1) * BLK:(2 * p + 2) * BLK])
                            for p in range(SB_PAIRS)], axis=1)


def _sb_pairs():
    return [(slice(2 * p * BLK, (2 * p + 2) * BLK), slice(p * LANES, (p + 1) * LANES)) for p in range(SB_PAIRS)]


def _sb_fwd(proj):
    t = proj.shape[0]
    nb = SB_KT // BLK
    wide = SB_PAIRS * LANES

    def body(q_ref, k_ref, v_ref, o_ref, tot_ref):
        hm = _head_masks()
        causal = _sb_causal()
        pairs = _sb_pairs()
        after = _tri(lambda r, c: r > c)

        def tile(qh, start, n_blk, carry, acc, own):
            ks = pl.ds(pl.multiple_of(start, BLK), n_blk * BLK)
            z = jnp.concatenate([_dot_nt(qh[rows], k_ref[ks, lanes]) for rows, lanes in pairs], axis=0)
            sp, zs = _softplus(z)
            spm = _sb_mask_last(sp, causal) if own else sp
            sufs = [None] * n_blk
            for b in reversed(range(n_blk)):
                blk = spm[:, b * BLK:(b + 1) * BLK]
                sufs[b] = carry + _scan_dot(blk, after)
                carry = carry + jnp.sum(blk, axis=1, keepdims=True)
            w = jnp.exp(zs - jnp.concatenate(sufs, axis=1))
            wb = (_sb_mask_last(w, causal) if own else w).astype(BF16)
            return carry, acc + jnp.concatenate([_dot(wb[rows], v_ref[ks, lanes]) for rows, lanes in pairs], axis=0)

        def qblock(g, j):
            qs = pl.ds(pl.multiple_of(g * SB_KT + j * BLK, BLK), BLK)
            qh = _sb_stack(q_ref[qs, :] * SCALE, hm)
            c0 = tile(qh, g * SB_KT, j + 1, jnp.zeros((SB_ROWS, 1), F32), jnp.zeros((SB_ROWS, LANES), F32), True)
            carry, acc = lax.fori_loop(0, g, lambda n, c: tile(qh, (g - 1 - n) * SB_KT, nb, c[0], c[1], False), c0)
            o_ref[qs, :] = _sb_unstack(acc, hm)
            for h in range(2 * SB_PAIRS):
                tot_ref[h, qs, :] = carry[h * BLK:(h + 1) * BLK]

        def group(g, _):
            for j in range(nb):
                qblock(g, j)
            return 0

        lax.fori_loop(0, t // SB_KT, group, 0)

    col_blk = lambda off: pl.BlockSpec((t, wide), lambda g: (0, off + g))
    n_steps = SB_W // wide
    return _call(
        body, name="sb_fwd", grid=(n_steps,), in_specs=[col_blk(0), col_blk(n_steps), col_blk(2 * n_steps)],
        out_specs=[col_blk(0), pl.BlockSpec((2 * SB_PAIRS, t, 1), lambda g: (g, 0, 0))],
        out_shape=[jax.ShapeDtypeStruct((t, SB_W), F32), jax.ShapeDtypeStruct((8, t, 1), F32)],
        compiler_params=_params(1))(proj, proj, proj)


def _sb_bwd(proj, d_o, tot):
    t = proj.shape[0]
    nb = SB_KT // BLK
    wide = SB_PAIRS * LANES

    def body(q_ref, k_ref, v_ref, do_ref, tot_ref, dq_ref, dk_ref, dv_ref, dk_acc, dv_acc):
        hm = _head_masks()
        causal = _sb_causal()
        pairs = _sb_pairs()
        before = _tri(lambda r, c: r < c)
        upto = _tri(lambda r, c: r <= c)
        dk_acc[...] = jnp.zeros_like(dk_acc)
        dv_acc[...] = jnp.zeros_like(dv_acc)

        def tile(qh, doh, tt, start, n_blk, pre, ecum, dq, own):
            ks = pl.ds(pl.multiple_of(start, BLK), n_blk * BLK)
            k = k_ref[ks, :]
            v = v_ref[ks, :]
            z = jnp.concatenate([_dot_nt(qh[rows], k[:, lanes]) for rows, lanes in pairs], axis=0)
            sp, zs = _softplus(z)
            spm = _sb_mask_last(sp, causal) if own else sp
            pres = []
            for b in range(n_blk):
                blk = spm[:, b * BLK:(b + 1) * BLK]
                pres.append(pre + _scan_dot(blk, before))
                pre = pre + jnp.sum(blk, axis=1, keepdims=True)
            logw = z - (tt - jnp.concatenate(pres, axis=1))
            if own:
                logw = jnp.minimum(logw, 0.0)
            w = jnp.exp(logw)
            if own:
                w = _sb_mask_last(w, causal)
            e = w * jnp.concatenate([_dot_nt(doh[rows], v[:, lanes]) for rows, lanes in pairs], axis=0)
            incs = []
            for b in range(n_blk):
                blk = e[:, b * BLK:(b + 1) * BLK]
                incs.append(ecum + _scan_dot(blk, upto))
                ecum = ecum + jnp.sum(blk, axis=1, keepdims=True)
            dz = e - jnp.exp(zs) * jnp.concatenate(incs, axis=1)
            if own:
                dz = _sb_mask_last(dz, causal)
            dzb = dz.astype(BF16)
            wb = w.astype(BF16)
            for rows, lanes in pairs:
                dk_acc[ks, lanes] += _dot_tn(dzb[rows], qh[rows])
                dv_acc[ks, lanes] += _dot_tn(wb[rows], doh[rows])
            return pre, ecum, dq + jnp.concatenate([_dot(dzb[rows], k[:, lanes]) for rows, lanes in pairs], axis=0)

        def qblock(g, j):
            qs = pl.ds(pl.multiple_of(g * SB_KT + j * BLK, BLK), BLK)
            qh = _sb_stack(q_ref[qs, :] * SCALE, hm)
            doh = _sb_stack(do_ref[qs, :], hm)
            tt = jnp.concatenate([tot_ref[h, qs, :] for h in range(2 * SB_PAIRS)], axis=0)
            c0 = (jnp.zeros((SB_ROWS, 1), F32), jnp.zeros((SB_ROWS, 1), F32), jnp.zeros((SB_ROWS, LANES), F32))
            c = lax.fori_loop(0, g, lambda kt, c: tile(qh, doh, tt, kt * SB_KT, nb, c[0], c[1], c[2], False), c0)
            dq = tile(qh, doh, tt, g * SB_KT, j + 1, c[0], c[1], c[2], True)[2]
            dq_ref[qs, :] = (_sb_unstack(dq, hm) * SCALE).astype(BF16)

        def group(g, _):
            for j in range(nb):
                qblock(g, j)
            return 0

        lax.fori_loop(0, t // SB_KT, group, 0)
        dk_ref[...] = dk_acc[...].astype(BF16)
        dv_ref[...] = dv_acc[...].astype(BF16)

    col_blk = lambda off: pl.BlockSpec((t, wide), lambda g: (0, off + g))
    n_steps = SB_W // wide
    out = jax.ShapeDtypeStruct((t, SB_W), BF16)
    return _call(
        body, name="sb_bwd", grid=(n_steps,),
        in_specs=[col_blk(0), col_blk(n_steps), col_blk(2 * n_steps), col_blk(0),
                  pl.BlockSpec((2 * SB_PAIRS, t, 1), lambda g: (g, 0, 0))],
        out_specs=[col_blk(0), col_blk(0), col_blk(0)], out_shape=[out, out, out],
        scratch_shapes=[pltpu.VMEM((t, wide), F32), pltpu.VMEM((t, wide), F32)],
        compiler_params=_params(1))(proj, proj, proj, d_o, tot)


def _bucket_table():
    a = np.arange(BLK)[:, None]
    c = np.arange(2 * BLK)[None, :]
    dist = np.maximum(BLK + a - c, 0)
    max_exact = N_BUCKETS // 2
    dd = np.maximum(dist, 1).astype(np.float32)
    large = max_exact + (np.log(dd / max_exact) / math.log(MAX_DISTANCE / max_exact)
                         * (N_BUCKETS - max_exact)).astype(np.int32)
    large = np.minimum(large, N_BUCKETS - 1)
    return np.where(dist < max_exact, dist, large).astype(np.int32)


SWA_H = 8


def _swa_band_masks():
    row = lax.broadcasted_iota(jnp.int32, (SWA_H * BLK, 2 * BLK), 0) & (BLK - 1)
    col = lax.broadcasted_iota(jnp.int32, (SWA_H * BLK, 2 * BLK), 1)
    own = lax.broadcasted_iota(jnp.int32, (SWA_H * BLK, BLK), 1) <= (
        lax.broadcasted_iota(jnp.int32, (SWA_H * BLK, BLK), 0) & (BLK - 1))
    return (col > row) & ((col < BLK) | (col - BLK <= row)), own


def _swa_stack(ref, qs, hm, scale):
    parts = []
    for hq in range(SWA_H):
        kvh = hq // SWA_G
        x = ref[qs, (hq // 2) * LANES:(hq // 2 + 1) * LANES].astype(F32)
        if hq % 2 != kvh:
            x = pltpu.roll(x, HEAD_DIM, 1)
        parts.append(jnp.where(hm[kvh], x * scale, 0.0).astype(BF16))
    return jnp.concatenate(parts, axis=0)


def _swa_unstack(x8, hm):
    heads = []
    for hq in range(SWA_H):
        x = x8[hq * BLK:(hq + 1) * BLK]
        heads.append(pltpu.roll(x, HEAD_DIM, 1) if hq % 2 != hq // SWA_G else x)
    return [jnp.where(hm[0], heads[2 * p], heads[2 * p + 1]) for p in range(SWA_H // 2)]


def _swa_scores(q8, kb, bias_ref, mask, cols):
    bias8 = jnp.concatenate([bias_ref[hq, :, cols] for hq in range(SWA_H)], axis=0)
    return jnp.where(mask, _dot_nt(q8, kb) + bias8, NEG_INF)


def _swa_sinks(sink_ref):
    return jnp.concatenate([jnp.broadcast_to(sink_ref[hq:hq + 1, 0:1], (BLK, 1)) for hq in range(SWA_H)], axis=0)


def _swa_fwd(proj, bias, sinks_b):
    t = proj.shape[0]
    nq = t // BLK

    def body(q_ref, k_ref, v_ref, bias_ref, sink_ref, o_ref, lse_ref):
        hm = _head_masks()
        band, own = _swa_band_masks()

        def qblock(i, prev):
            qs = pl.ds(pl.multiple_of(i * BLK, BLK), BLK)
            if prev:
                ks, mask, cols = pl.ds(pl.multiple_of((i - 1) * BLK, BLK), 2 * BLK), band, slice(None)
            else:
                ks, mask, cols = qs, own, slice(BLK, None)
            q8 = _swa_stack(q_ref, qs, hm, SCALE)
            sink8 = _swa_sinks(sink_ref)
            s = _swa_scores(q8, k_ref[ks, :], bias_ref, mask, cols)
            m = jnp.maximum(jnp.max(s, axis=1, keepdims=True), sink8)
            p = jnp.exp(s - m)
            den = jnp.sum(p, axis=1, keepdims=True) + jnp.exp(sink8 - m)
            o8 = _dot((p * (1.0 / den)).astype(BF16), v_ref[ks, :])
            lse8 = m + jnp.log(den)
            for hq in range(SWA_H):
                lse_ref[hq, qs, :] = lse8[hq * BLK:(hq + 1) * BLK]
            for pp, o in enumerate(_swa_unstack(o8, hm)):
                o_ref[qs, pp * LANES:(pp + 1) * LANES] = o

        qblock(0, False)

        def step(i, _):
            qblock(i, True)
            return 0

        lax.fori_loop(1, nq, step, 0)

    return _call(
        body, name="swa_fwd", grid=(1,),
        in_specs=[pl.BlockSpec((t, SWA_W), lambda i: (0, 3)), pl.BlockSpec((t, KV_W), lambda i: (0, 16)),
                  pl.BlockSpec((t, KV_W), lambda i: (0, 17)), pl.BlockSpec((8, BLK, 2 * BLK), lambda i: (0, 0, 0)),
                  pl.BlockSpec((8, LANES), lambda i: (0, 0))],
        out_specs=[pl.BlockSpec((t, SWA_W), lambda i: (0, 0)), pl.BlockSpec((8, t, 1), lambda i: (0, 0, 0))],
        out_shape=[jax.ShapeDtypeStruct((t, SWA_W), F32), jax.ShapeDtypeStruct((8, t, 1), F32)],
        compiler_params=_params(1))(proj, proj, proj, bias, sinks_b)


def _swa_bwd(proj, d_o, lse, bias, sinks_b, dbias_in):
    t = proj.shape[0]
    nq = t // BLK

    def body(q_ref, k_ref, v_ref, do_ref, lse_ref, bias_ref, sink_ref, dbi_ref,
             dq_ref, dk_ref, dv_ref, dsink_ref, dbias_ref, dk_acc, dv_acc):
        hm = _head_masks()
        band, own = _swa_band_masks()
        dk_acc[...] = jnp.zeros_like(dk_acc)
        dv_acc[...] = jnp.zeros_like(dv_acc)
        dbias_ref[...] = dbi_ref[...]

        def qblock(i, prev, dsink8):
            qs = pl.ds(pl.multiple_of(i * BLK, BLK), BLK)
            if prev:
                ks, mask, cols = pl.ds(pl.multiple_of((i - 1) * BLK, BLK), 2 * BLK), band, slice(None)
            else:
                ks, mask, cols = qs, own, slice(BLK, None)
            q8 = _swa_stack(q_ref, qs, hm, SCALE)
            do8 = _swa_stack(do_ref, qs, hm, 1.0)
            sink8 = _swa_sinks(sink_ref)
            lse8 = jnp.concatenate([lse_ref[hq, qs, :] for hq in range(SWA_H)], axis=0)
            kb = k_ref[ks, :]
            p = jnp.exp(_swa_scores(q8, kb, bias_ref, mask, cols) - lse8)
            dp = _dot_nt(do8, v_ref[ks, :])
            delta = jnp.sum(p * dp, axis=1, keepdims=True)
            ds = p * (dp - delta)
            for hq in range(SWA_H):
                dbias_ref[hq, :, cols] += ds[hq * BLK:(hq + 1) * BLK]
            dsb = ds.astype(BF16)
            dk_acc[ks, :] += _dot_tn(dsb, q8)
            dv_acc[ks, :] += _dot_tn(p.astype(BF16), do8)
            for pp, dq in enumerate(_swa_unstack(_dot(dsb, kb) * SCALE, hm)):
                dq_ref[qs, pp * LANES:(pp + 1) * LANES] = dq.astype(BF16)
            return dsink8 - jnp.exp(sink8 - lse8) * delta

        ds0 = qblock(0, False, jnp.zeros((SWA_H * BLK, 1), F32))
        ds8 = lax.fori_loop(1, nq, lambda i, c: qblock(i, True, c), ds0)
        for hq in range(SWA_H):
            dsink_ref[hq:hq + 1, :] = jnp.broadcast_to(
                jnp.sum(ds8[hq * BLK:(hq + 1) * BLK], axis=0, keepdims=True), (1, LANES))

        dk_ref[...] = dk_acc[...].astype(BF16)
        dv_ref[...] = dv_acc[...].astype(BF16)

    full3 = pl.BlockSpec((8, BLK, 2 * BLK), lambda i: (0, 0, 0))
    kv = jax.ShapeDtypeStruct((t, KV_W), BF16)
    return _call(
        body, name="swa_bwd", grid=(1,),
        in_specs=[pl.BlockSpec((t, SWA_W), lambda i: (0, 3)), pl.BlockSpec((t, KV_W), lambda i: (0, 16)),
                  pl.BlockSpec((t, KV_W), lambda i: (0, 17)), pl.BlockSpec((t, SWA_W), lambda i: (0, 1)),
                  pl.BlockSpec((8, t, 1), lambda i: (0, 0, 0)), full3, pl.BlockSpec((8, LANES), lambda i: (0, 0)),
                  full3],
        out_specs=[pl.BlockSpec((t, SWA_W), lambda i: (0, 0)), pl.BlockSpec((t, KV_W), lambda i: (0, 0)),
                   pl.BlockSpec((t, KV_W), lambda i: (0, 0)), pl.BlockSpec((8, LANES), lambda i: (0, 0)), full3],
        out_shape=[jax.ShapeDtypeStruct((t, SWA_W), BF16), kv, kv, jax.ShapeDtypeStruct((8, LANES), F32),
                   jax.ShapeDtypeStruct((8, BLK, 2 * BLK), F32)],
        scratch_shapes=[pltpu.VMEM((t, KV_W), F32), pltpu.VMEM((t, KV_W), F32)],
        compiler_params=_params(1))(proj, proj, proj, d_o, lse, bias, sinks_b, dbias_in)


def _concat_cols(parts):
    t = parts[0].shape[0]
    widths = [a.shape[1] for a in parts]

    def body(*refs):
        refs[-1][...] = jnp.concatenate([r[...] for r in refs[:-1]], axis=1)

    return _call(
        body, name="concat_cols", grid=(t // TM,),
        in_specs=[pl.BlockSpec((TM, w), lambda i: (i, 0)) for w in widths],
        out_specs=pl.BlockSpec((TM, sum(widths)), lambda i: (i, 0)),
        out_shape=jax.ShapeDtypeStruct((t, sum(widths)), parts[0].dtype), compiler_params=_params(1))(*parts)


def _bias_table(rel_bias, buckets):
    def body(rb_ref, b_ref, o_ref):
        bk = b_ref[...]
        for h in range(8):
            acc = jnp.zeros((BLK, 2 * BLK), F32)
            for b in range(N_BUCKETS):
                acc = jnp.where(bk == b, rb_ref[b, h], acc)
            o_ref[h] = acc

    return _call(
        body, name="bias_table", grid=(1,),
        in_specs=[pl.BlockSpec(memory_space=pltpu.SMEM), pl.BlockSpec((BLK, 2 * BLK), lambda i: (0, 0))],
        out_specs=pl.BlockSpec((8, BLK, 2 * BLK), lambda i: (0, 0, 0)),
        out_shape=jax.ShapeDtypeStruct((8, BLK, 2 * BLK), F32), compiler_params=_params(1))(rel_bias, buckets)


def _bias_grad(dbias, buckets):
    def body(d_ref, b_ref, o_ref):
        lane = lax.broadcasted_iota(jnp.int32, (1, LANES), 1)
        bk = b_ref[...]
        for h in range(8):
            d = d_ref[h]
            acc = jnp.zeros((1, LANES), F32)
            for b in range(N_BUCKETS):
                s = jnp.sum(jnp.sum(jnp.where(bk == b, d, 0.0), axis=0, keepdims=True), axis=1, keepdims=True)
                acc = acc + jnp.where(lane == b, s, 0.0)
            o_ref[h:h + 1, :] = acc

    return _call(
        body, name="bias_grad", grid=(1,),
        in_specs=[pl.BlockSpec((8, BLK, 2 * BLK), lambda i: (0, 0, 0)), pl.BlockSpec((BLK, 2 * BLK), lambda i: (0, 0))],
        out_specs=pl.BlockSpec((8, LANES), lambda i: (0, 0)),
        out_shape=jax.ShapeDtypeStruct((8, LANES), F32), compiler_params=_params(1))(dbias, buckets)


def _row(a):
    return a.reshape(1, -1)


def _fwd_ffn1_gu(h, n1, w):
    s = {"h0": h, "n1": n1}
    s["gu1"], s["act1"] = _ffn_gu(n1, w["ffn1_gu"])
    return s


def _fwd_ffn1_down(s, w, small, l):
    s["h1"], s["nm"] = _down_res(s["act1"], w["ffn1_down"], s["h0"], _row(small["norm_mix"][l]))


def _fwd_ffn1(h, n1, w, small, l):
    s = _fwd_ffn1_gu(h, n1, w)
    _fwd_ffn1_down(s, w, small, l)
    return s


def _fwd_proj_sb(s, w):
    s["proj"] = _proj(s["nm"], w["w_in"])
    s["o_sb"], s["tot"] = _sb_fwd(s["proj"])


def _fwd_swa(s, small, l, bias):
    s["sinks_b"] = jnp.broadcast_to(small["sinks"][l][:, None], (8, LANES))
    s["o_sw"], s["lse"] = _swa_fwd(s["proj"], bias, s["sinks_b"])


def _fwd_out_gu2(s, w, small, l):
    s["h2"], s["mixed"], s["n2"] = _out_res(
        s["o_sb"], s["o_sw"], _row(small["norm_out_sb"][l]), _row(small["norm_out_swa"][l]), w["w_out"], s["h1"],
        _row(small["norm_ffn2"][l]))
    s["gu2"], s["act2"] = _ffn_gu(s["n2"], w["ffn2_gu"])


def _fwd_ffn2_down(s, w, g_after):
    return _down_res(s["act2"], w["ffn2_down"], s["h2"], g_after)


def _fwd_out_ffn2(s, w, small, l, g_after):
    _fwd_out_gu2(s, w, small, l)
    return _fwd_ffn2_down(s, w, g_after)


def _bwd_ffn_dact(dh, s, w, which):
    return _ffn_dact(dh[1], w[f"ffn{which}_down"], s[f"gu{which}"])


def _bwd_ffn_rest(dh, dgu, s, w, small, l, which):
    h_in, norm = (s["h0"], "norm_ffn1") if which == 1 else (s["h2"], "norm_ffn2")
    g_down = _wgrad_down(s[f"act{which}"], dh[1])
    g_gu = _wgrad_gu(s[f"n{which}"], dgu)
    dh32, dh16, dg = _ffn_dn(dgu, w[f"ffn{which}_gu"], dh[0], h_in, _row(small[norm][l]))
    return (dh32, dh16), {f"ffn{which}_down": g_down, f"ffn{which}_gu": g_gu}, {norm: dg}


def _bwd_ffn(dh, s, w, small, l, which):
    return _bwd_ffn_rest(dh, _bwd_ffn_dact(dh, s, w, which), s, w, small, l, which)


def _bwd_mix(dh, s, w, small, l, bias, dbias):
    g_out = _wgrad_out(s["mixed"], dh[1])
    d_o, dg_sb, dg_sw = _dmixed(dh[1], w["w_out"], s["o_sb"], s["o_sw"], _row(small["norm_out_sb"][l]),
                                _row(small["norm_out_swa"][l]))
    dq_sb, dk_sb, dv_sb = _sb_bwd(s["proj"], d_o, s["tot"])
    dq_sw, dk_sw, dv_sw, dsink, dbias = _swa_bwd(s["proj"], d_o, s["lse"], bias, s["sinks_b"], dbias)
    dproj = _concat_cols([dq_sb, dk_sb, dv_sb, dq_sw, dk_sw, dv_sw])
    g_in = _wgrad_in(s["nm"], dproj)
    dh32, dh16, dg_mix = _mix_dn(dproj, w["w_in"], dh[0], s["h1"], _row(small["norm_mix"][l]))
    gs = {"norm_out_sb": dg_sb, "norm_out_swa": dg_sw, "sinks": dsink[:, 0], "norm_mix": dg_mix}
    return (dh32, dh16), {"w_out": g_out, "w_in": g_in}, gs, dbias


def _place():
    x, y, c = lax.axis_index("x"), lax.axis_index("y"), lax.axis_index("c")
    return x, y, c, 2 * x + y


def _chip_core(k, c):
    return (k // 2, k % 2, c)


def _rows_per_block(rows, cols, copies):
    best = 16
    for tr in range(16, rows + 1, 16):
        if rows % tr == 0 and copies * tr * cols * 4 <= SLAB_BLOCK_BYTES:
            best = tr
    assert rows % best == 0
    return best


def _place_own(w, l, me1):
    _, rows, cols = w.shape
    tr = _rows_per_block(rows // 2, cols, 1)
    per_half = rows // 2 // tr

    def body(me_ref, w_ref, o_ref):
        o_ref[...] = w_ref[...].astype(BF16)

    return _call(
        body, name="place_own",
        num_scalar_prefetch=1, grid=(rows // tr,),
        in_specs=[pl.BlockSpec((None, tr, cols), lambda r, me: (l, r, 0))],
        out_specs=pl.BlockSpec((None, None, tr, cols), lambda r, me: (me[0], r // per_half, r % per_half, 0)),
        out_shape=jax.ShapeDtypeStruct((N_CHIPS, 2, rows // 2, cols), BF16), compiler_params=_params(1))(me1, w)


def _plan_gather_ici(bufs):
    _, _, c, me = _place()
    return [(b.at[me, c], b.at[me, c], b.at[(me + 3 - j) % N_CHIPS, c], _chip_core((me + 1 + j) % N_CHIPS, c))
            for b in bufs for j in range(3)]


def _plan_gather_d2d(bufs):
    x, y, c, me = _place()
    return [(b.at[(me + 3 - j) % N_CHIPS, c], b.at[(me + 3 - j) % N_CHIPS, c], b.at[(me + 3 - j) % N_CHIPS, 1 - c],
             (x, y, 1 - c)) for b in bufs for j in range(3)]


def _plan_grad_sibling(bufs):
    x, y, c, _ = _place()
    n = len(bufs) // 2
    return [(g.at[:, 1 - c], z, z, (x, y, 1 - c)) for g, z in zip(bufs[:n], bufs[n:])]


def _plan_grad_chips(bufs):
    _, _, c, me = _place()
    n = len(bufs) // 2
    return [(p.at[j], z.at[j], z.at[j], _chip_core((me + 1 + j) % N_CHIPS, c))
            for p, z in zip(bufs[:n], bufs[n:]) for j in range(3)]


def _plan_grad_halves(bufs):
    x, y, c, _ = _place()
    return [(b.at[c], b.at[c], b.at[1 - c], (x, y, 1 - c)) for b in bufs]


def _remote(src, dst, send_sem, recv_sem, to):
    return pltpu.make_async_remote_copy(src_ref=src, dst_ref=dst, send_sem=send_sem, recv_sem=recv_sem,
                                        device_id=to, device_id_type=MESH)


SIBLING_BARRIER_ID = 0


def _sibling_handshake():
    x, y, c, _ = _place()
    barrier = pltpu.get_barrier_semaphore()
    pl.semaphore_signal(barrier, inc=1, device_id=(x, y, 1 - c), device_id_type=MESH)
    pl.semaphore_wait(barrier, 1)


def _split_params(sibling_only):
    return pltpu.CompilerParams(has_side_effects=EFFECT, collective_id=SIBLING_BARRIER_ID if sibling_only else None)


def _exchange_start_groups(name, plan, groups, sibling_only=False):
    sizes = [len(g) for g, _ in groups]
    bufs = [a for g, _ in groups for a in g]
    n, n_groups = len(bufs), len(groups)

    def body(*refs):
        if sibling_only:
            _sibling_handshake()
        ins, sems, token = refs[:n], refs[n:n + 2 * n_groups], refs[-1]
        at = 0
        for k, size in enumerate(sizes):
            for i, (src, dst, _, to) in enumerate(plan(ins[at:at + size])):
                _remote(src, dst, sems[2 * k].at[i], sems[2 * k + 1].at[i], to).start()
            at += size
        token[...] = jnp.zeros_like(token)

    sem_shapes = [pltpu.SemaphoreType.DMA((n_copies,)) for _, n_copies in groups for _ in range(2)]
    out = _call(
        body, name=name,
        out_shape=(*sem_shapes, *[pltpu.HBM(a.shape, a.dtype) for a in bufs], jax.ShapeDtypeStruct((8, LANES), F32)),
        in_specs=[HBM] * n,
        out_specs=(*[SEM] * (2 * n_groups), *[HBM] * n, pl.BlockSpec(memory_space=pltpu.VMEM)),
        input_output_aliases={t: 2 * n_groups + t for t in range(n)}, hbm_args=n,
        compiler_params=_split_params(sibling_only),
    )(*bufs)
    flights, at = [], 2 * n_groups
    for k, size in enumerate(sizes):
        flights.append(((out[2 * k], out[2 * k + 1]), list(out[at:at + size])))
        at += size
    return flights


def _exchange_start(name, plan, bufs, n_copies, sibling_only=False):
    return _exchange_start_groups(name, plan, [(bufs, n_copies)], sibling_only)[0]


def _exchange_wait(name, plan, bufs, sems):
    n = len(bufs)

    def body(*refs):
        ins = refs[:n]
        ssem, rsem = refs[n], refs[n + 1]
        for i, (src, dst, land, to) in enumerate(plan(ins)):
            _remote(src, dst, ssem.at[i], rsem.at[i], to).wait_send()
            _remote(land, land, ssem.at[i], rsem.at[i], to).wait_recv()

    return list(_call(
        body, name=name, out_shape=[pltpu.HBM(a.shape, a.dtype) for a in bufs],
        in_specs=[HBM] * n + [SEM, SEM], out_specs=[HBM] * n,
        input_output_aliases={t: t for t in range(n)},
        compiler_params=pltpu.CompilerParams(has_side_effects=EFFECT),
    )(*bufs, sems[0], sems[1]))


def _exchange_pass(name, done, plan, bufs, sems, n_copies):
    n = len(bufs)

    def body(*refs):
        _sibling_handshake()
        ins = refs[:n]
        old_s, old_r, ssem, rsem = refs[n], refs[n + 1], refs[n + 2], refs[n + 3]
        token = refs[-1]
        for i, (src, dst, land, to) in enumerate(done(ins)):
            _remote(src, dst, old_s.at[i], old_r.at[i], to).wait_send()
            _remote(land, land, old_s.at[i], old_r.at[i], to).wait_recv()
        for i, (src, dst, _, to) in enumerate(plan(ins)):
            _remote(src, dst, ssem.at[i], rsem.at[i], to).start()
        token[...] = jnp.zeros_like(token)

    out = _call(
        body, name=name,
        out_shape=(pltpu.SemaphoreType.DMA((n_copies,)), pltpu.SemaphoreType.DMA((n_copies,)),
                   *[pltpu.HBM(a.shape, a.dtype) for a in bufs], jax.ShapeDtypeStruct((8, LANES), F32)),
        in_specs=[HBM] * n + [SEM, SEM], out_specs=(SEM, SEM, *[HBM] * n, pl.BlockSpec(memory_space=pltpu.VMEM)),
        input_output_aliases={t: 2 + t for t in range(n)},
        compiler_params=_split_params(True),
    )(*bufs, sems[0], sems[1])
    return (out[0], out[1]), list(out[2:2 + n])


def _chip_sum(g, xbuf, cm):
    _, _, r2, cols = g.shape
    tr = _rows_per_block(r2, cols, 1)

    def body(cm_ref, g_ref, x_ref, o_ref):
        o_ref[...] = (g_ref[...] + x_ref[...]).astype(BF16)

    return _call(
        body, name="grad_chip_sum",
        num_scalar_prefetch=1, grid=(3, r2 // tr),
        in_specs=[pl.BlockSpec((None, None, tr, cols), lambda j, r, cm: ((cm[1] + 1 + j) % N_CHIPS, cm[0], r, 0)),
                  pl.BlockSpec((None, tr, cols), lambda j, r, cm: ((cm[1] + 1 + j) % N_CHIPS, r, 0))],
        out_specs=pl.BlockSpec((None, tr, cols), lambda j, r, cm: (j, r, 0)),
        out_shape=jax.ShapeDtypeStruct((3, r2, cols), BF16), compiler_params=_params(2))(cm, g, xbuf)


def _total_sum(g, xbuf, rbuf, cm):
    _, _, r2, cols = g.shape
    tr = _rows_per_block(r2, cols, 3)

    def body(cm_ref, g_ref, x_ref, r_ref, o_ref):
        acc = g_ref[...] + x_ref[...]
        for j in range(3):
            acc = acc + r_ref[j].astype(F32)
        o_ref[...] = acc

    return _call(
        body, name="grad_total_sum",
        num_scalar_prefetch=1, grid=(r2 // tr,),
        in_specs=[pl.BlockSpec((None, None, tr, cols), lambda r, cm: (cm[1], cm[0], r, 0)),
                  pl.BlockSpec((None, tr, cols), lambda r, cm: (cm[1], r, 0)),
                  pl.BlockSpec((3, tr, cols), lambda r, cm: (0, r, 0))],
        out_specs=pl.BlockSpec((None, tr, cols), lambda r, cm: (cm[0], r, 0)),
        out_shape=jax.ShapeDtypeStruct((2, r2, cols), F32), compiler_params=_params(1))(cm, g, xbuf, rbuf)


def _small_allreduce(v):
    rows = v.shape[0]
    n_dev = 2 * N_CHIPS

    def body(v_ref, o_ref, buf, ssem, rsem):
        x, y, c, _ = _place()
        me = 4 * x + 2 * y + c
        buf[me] = v_ref[...]

        def copy(d, slot, to):
            return _remote(v_ref, buf.at[slot], ssem.at[d - 1], rsem.at[d - 1], (to // 4, (to // 2) % 2, to % 2))

        cps = [copy(d, me, (me + d) % n_dev) for d in range(1, n_dev)]
        for cp in cps:
            cp.start()
        for d in range(1, n_dev):
            copy(d, (me + n_dev - d) % n_dev, me).wait_recv()
        for cp in cps:
            cp.wait_send()
        acc = buf[0]
        for i in range(1, n_dev):
            acc = acc + buf[i]
        o_ref[...] = acc

    vm = pl.BlockSpec(memory_space=pltpu.VMEM)
    return _call(
        body, name="small_allreduce", in_specs=[vm], out_specs=vm,
        out_shape=jax.ShapeDtypeStruct(v.shape, F32),
        scratch_shapes=[pltpu.VMEM((n_dev, rows, LANES), F32), pltpu.SemaphoreType.DMA((n_dev - 1,)),
                        pltpu.SemaphoreType.DMA((n_dev - 1,))],
        compiler_params=pltpu.CompilerParams(vmem_limit_bytes=V7X_VMEM_LIMIT))(v)


def _adamw_math(w, g, m, v):
    m2 = ADAM_B1 * m + (1.0 - ADAM_B1) * g
    v2 = ADAM_B2 * v + (1.0 - ADAM_B2) * (g * g)
    v_hat = v2 / (1.0 - ADAM_B2 ** ADAM_STEP)
    step = (-ADAM_LR / (1.0 - ADAM_B1 ** ADAM_STEP)) * m2 / (jnp.sqrt(v_hat) + ADAM_EPS)
    return step + (-ADAM_LR * ADAM_WD) * w, m2, v2


def _adamw_layer(w, g, m, v, l, prev):
    _, rows, cols = w.shape
    tr = rows
    for cand in range(8, rows + 1, 8):
        if rows % cand == 0 and cand * cols * 4 <= ADAMW_BLOCK_BYTES:
            tr = cand

    def body(w_ref, g_ref, m_ref, v_ref, *outs):
        go_ref, d_ref, m2_ref, v2_ref = outs[-4:]
        g = g_ref[...]
        go_ref[...] = g
        d_ref[...], m2_ref[...], v2_ref[...] = _adamw_math(w_ref[...], g, m_ref[...], v_ref[...])

    stack = pl.BlockSpec((None, tr, cols), lambda i: (l, i, 0))
    ins, specs, alias = [w, g, m, v], [stack, pl.BlockSpec((tr, cols), lambda i: (i, 0)), stack, stack], {}
    if prev is not None:
        ins += list(prev)
        specs += [ANY] * 4
        alias = {4 + i: i for i in range(4)}
    return _call(
        body, name="adamw", grid=(rows // tr,), in_specs=specs, out_specs=[stack] * 4,
        out_shape=[jax.ShapeDtypeStruct(w.shape, F32)] * 4, input_output_aliases=alias,
        compiler_params=_params(1))(*ins)


def _adamw_small(ws, gs, ms, vs):
    n = len(ws)
    row = lambda a: a.reshape(1, -1) if a.ndim == 1 else a
    ins = [row(a) for group in (ws, gs, ms, vs) for a in group]

    def body(*refs):
        w, g, m, v = (refs[i * n:(i + 1) * n] for i in range(4))
        outs = refs[4 * n:]
        for i in range(n):
            d, m2, v2 = _adamw_math(w[i][...], g[i][...], m[i][...], v[i][...])
            outs[i][...], outs[n + i][...], outs[2 * n + i][...] = d, m2, v2

    vm = pl.BlockSpec(memory_space=pltpu.VMEM)
    outs = _call(
        body, name="adamw_small", in_specs=[vm] * (4 * n), out_specs=[vm] * (3 * n),
        out_shape=[jax.ShapeDtypeStruct(a.shape, F32) for a in ins[:n]] * 3,
        compiler_params=pltpu.CompilerParams(vmem_limit_bytes=V7X_VMEM_LIMIT))(*ins)
    outs = [o.reshape(w.shape) for o, w in zip(outs, list(ws) * 3)]
    return outs[:n], outs[n:2 * n], outs[2 * n:]


SMALL = ("norm_ffn1", "norm_mix", "sinks", "norm_out_sb", "norm_out_swa", "norm_ffn2", "rel_bias", "norm_final")
BIG = ("ffn1_gu", "ffn1_down", "w_in", "w_out", "ffn2_gu", "ffn2_down")


def _pack(parts):
    flat, n = [], 0
    for a in parts:
        a = a.reshape(-1).astype(F32)
        gap = -a.shape[0] % LANES
        flat += [a] + ([jnp.zeros((gap,), F32)] if gap else [])
        n += a.shape[0] + gap
    tail = -(n // LANES) % 8 * LANES
    return jnp.concatenate(flat + ([jnp.zeros((tail,), F32)] if tail else [])).reshape(-1, LANES)


def _unpack(packed, like):
    out, r = [], 0
    for a in like:
        n = math.prod(a.shape)
        nr = -(-n // LANES)
        out.append(packed[r:r + nr].reshape(-1)[:n].reshape(a.shape))
        r += nr
    return out


def _halved(a):
    k, r, cols = a.shape
    return a.reshape(k, 2, r // 2, cols)


def _weight_view(k, buf):
    full = buf.reshape(N_CHIPS, buf.shape[2] * 2, buf.shape[3])
    return full if k.endswith("_gu") else full.reshape(-1, D_MODEL)


def _grad_stack(k, g):
    if not k.endswith("_gu"):
        g = g.reshape(N_CHIPS, g.shape[0] // N_CHIPS, D_MODEL)
    return _halved(g)


def _empty_like_hbm(shape, dtype):
    return pltpu.with_memory_space_constraint(lax.empty(shape, dtype), pltpu.HBM)


def kernel(x, norm_ffn1, w_ffn1_gu, w_ffn1_down, norm_mix, w_in, sinks, norm_out_sb, norm_out_swa, w_out, norm_ffn2, w_ffn2_gu, w_ffn2_down, rel_bias, norm_final, loss_target, m_norm_ffn1, m_w_ffn1_gu, m_w_ffn1_down, m_norm_mix, m_w_in, m_sinks, m_norm_out_sb, m_norm_out_swa, m_w_out, m_norm_ffn2, m_w_ffn2_gu, m_w_ffn2_down, m_rel_bias, m_norm_final, v_norm_ffn1, v_w_ffn1_gu, v_w_ffn1_down, v_norm_mix, v_w_in, v_sinks, v_norm_out_sb, v_norm_out_swa, v_w_out, v_norm_ffn2, v_w_ffn2_gu, v_w_ffn2_down, v_rel_bias, v_norm_final):
    big_w = dict(ffn1_gu=w_ffn1_gu, ffn1_down=w_ffn1_down, w_in=w_in, w_out=w_out, ffn2_gu=w_ffn2_gu, ffn2_down=w_ffn2_down)
    big_m = dict(ffn1_gu=m_w_ffn1_gu, ffn1_down=m_w_ffn1_down, w_in=m_w_in, w_out=m_w_out, ffn2_gu=m_w_ffn2_gu, ffn2_down=m_w_ffn2_down)
    big_v = dict(ffn1_gu=v_w_ffn1_gu, ffn1_down=v_w_ffn1_down, w_in=v_w_in, w_out=v_w_out, ffn2_gu=v_w_ffn2_gu, ffn2_down=v_w_ffn2_down)
    small = dict(norm_ffn1=norm_ffn1, norm_mix=norm_mix, sinks=sinks, norm_out_sb=norm_out_sb, norm_out_swa=norm_out_swa,
                 norm_ffn2=norm_ffn2, rel_bias=rel_bias, norm_final=norm_final)
    small_m = dict(norm_ffn1=m_norm_ffn1, norm_mix=m_norm_mix, sinks=m_sinks, norm_out_sb=m_norm_out_sb,
                   norm_out_swa=m_norm_out_swa, norm_ffn2=m_norm_ffn2, rel_bias=m_rel_bias, norm_final=m_norm_final)
    small_v = dict(norm_ffn1=v_norm_ffn1, norm_mix=v_norm_mix, sinks=v_sinks, norm_out_sb=v_norm_out_sb,
                   norm_out_swa=v_norm_out_swa, norm_ffn2=v_norm_ffn2, rel_bias=v_rel_bias, norm_final=v_norm_final)
    for dct in (big_w, big_m, big_v):
        dct["w_in"] = jnp.swapaxes(dct["w_in"], 1, 2)
    _PREVIOUS[0] = None
    _, _, c, me = _place()
    cm = jnp.stack([c, me]).astype(jnp.int32)
    buckets = jnp.asarray(_bucket_table())
    ffn1, mix_in, rest = ("ffn1_gu", "ffn1_down"), ("w_in",), ("w_out", "ffn2_gu", "ffn2_down")

    def place(l, keys):
        return [_place_own(big_w[k], l, cm[1:]) for k in keys]

    def views(keys, bufs):
        return {k: _weight_view(k, b) for k, b in zip(keys, bufs)}

    def gather_start(tag, bufs):
        return _exchange_start(f"gather{tag}_ici_start", _plan_gather_ici, bufs, 3 * len(bufs))

    def gather_pass(tag, flight):
        return _exchange_pass(f"gather{tag}_pass", _plan_gather_ici, _plan_gather_d2d, flight[1], flight[0],
                              3 * len(flight[1]))

    def gather_done(tag, keys, flight):
        return views(keys, _exchange_wait(f"gather{tag}_d2d_wait", _plan_gather_d2d, flight[1], flight[0]))

    fly_gu0 = gather_start("0a", place(0, ffn1[:1]))
    fly_down0 = gather_start("0a2", place(0, ffn1[1:]))
    fly_in0 = gather_start("0b", place(0, mix_in))
    later = [place(l, keys) for l in range(DEPTH) for keys in ((rest,) if l == 0 else (ffn1, mix_in, rest))]
    fly_rest0, fly_ffn1, fly_in1, fly_rest1 = _exchange_start_groups(
        "gather_later_ici_start", _plan_gather_ici, [(bufs, 3 * len(bufs)) for bufs in later])
    bias = _bias_table(rel_bias, buckets)
    n1 = _norm_cast(x[0], _row(norm_ffn1[0]))
    w0 = gather_done("0a", ffn1[:1], gather_pass("0a", fly_gu0))

    s0 = _fwd_ffn1_gu(x[0], n1, w0)
    w0.update(gather_done("0a2", ffn1[1:], gather_pass("0a2", fly_down0)))
    fly_in0 = gather_pass("0b", fly_in0)
    _fwd_ffn1_down(s0, w0, small, 0)
    w0.update(gather_done("0b", mix_in, fly_in0))
    _fwd_proj_sb(s0, w0)
    fly_rest0 = gather_pass("0c", fly_rest0)
    _fwd_swa(s0, small, 0, bias)
    w0.update(gather_done("0c", rest, fly_rest0))
    _fwd_out_gu2(s0, w0, small, 0)
    fly_ffn1 = gather_pass("1a", fly_ffn1)
    h, n1 = _fwd_ffn2_down(s0, w0, _row(norm_ffn1[1]))
    w1 = gather_done("1a", ffn1, fly_ffn1)
    fly_in1 = gather_pass("1b", fly_in1)
    s1 = _fwd_ffn1(h, n1, w1, small, 1)
    w1.update(gather_done("1b", mix_in, fly_in1))
    _fwd_proj_sb(s1, w1)
    fly_rest1 = gather_pass("1c", fly_rest1)
    _fwd_swa(s1, small, 1, bias)
    w1.update(gather_done("1c", rest, fly_rest1))
    h, _ = _fwd_out_ffn2(s1, w1, small, 1, _row(norm_final))
    dh32, dh16, dg_final, loss_row = _loss_head(h, _row(norm_final), loss_target[0])
    dh = (dh32, dh16)

    def landing(stacks, lead, dtype):
        return [_empty_like_hbm((lead,) + a.shape[2:], dtype) for a in stacks]

    def reduce_begin(tag, keys, gw):
        stacks = [_grad_stack(k, gw[k]) for k in keys]
        flight = _exchange_start(f"grad{tag}_sibling_start", _plan_grad_sibling,
                                 stacks + landing(stacks, N_CHIPS, F32), len(keys), sibling_only=True)
        return dict(tag=tag, keys=keys, stacks=stacks, flight=flight)

    def reduce_chips(st):
        n, (sems, bufs) = len(st["keys"]), st["flight"]
        bufs = _exchange_wait(f"grad{st['tag']}_sibling_wait", _plan_grad_sibling, bufs, sems)
        st["own"] = list(zip(bufs[:n], bufs[n:]))
        st["flight"] = _exchange_start(f"grad{st['tag']}_chips_start", _plan_grad_chips,
                                       [_chip_sum(g, z, cm) for g, z in st["own"]] + landing(st["stacks"], 3, BF16),
                                       3 * n)

    def reduce_halves(st):
        n, (sems, bufs) = len(st["keys"]), st["flight"]
        bufs = _exchange_wait(f"grad{st['tag']}_chips_wait", _plan_grad_chips, bufs, sems)
        halves = [_total_sum(g, x, z, cm) for (g, x), z in zip(st["own"], bufs[n:])]
        st["flight"] = _exchange_start(f"grad{st['tag']}_halves_start", _plan_grad_halves, halves, n,
                                       sibling_only=True)

    def reduce_end(st):
        sems, bufs = st["flight"]
        bufs = _exchange_wait(f"grad{st['tag']}_halves_wait", _plan_grad_halves, bufs, sems)
        return {k: b.reshape(big_w[k].shape[1:]) for k, b in zip(st["keys"], bufs)}

    def adamw(reduced, l, prev):
        return {k: _adamw_layer(big_w[k], g, big_m[k], big_v[k], l, None if prev is None else prev[k])
                for k, g in reduced.items()}

    gsm = [dict() for _ in range(DEPTH)]
    dbias = jnp.zeros((8, BLK, 2 * BLK), F32)
    dh, gw1, gs = _bwd_ffn(dh, s1, w1, small, 1, 2)
    gsm[1].update(gs)
    dh, gw, gs, dbias = _bwd_mix(dh, s1, w1, small, 1, bias, dbias)
    gw1.update(gw)
    gsm[1].update(gs)
    dh, gw, gs = _bwd_ffn(dh, s1, w1, small, 1, 1)
    gw1.update(gw)
    gsm[1].update(gs)

    red1 = reduce_begin("1", BIG, gw1)
    dh, gw0, gs = _bwd_ffn(dh, s0, w0, small, 0, 2)
    gsm[0].update(gs)
    reduce_chips(red1)
    dh, gw, gs, dbias = _bwd_mix(dh, s0, w0, small, 0, bias, dbias)
    gw0.update(gw)
    gsm[0].update(gs)
    red0a = reduce_begin("0a", ("ffn2_gu", "ffn2_down", "w_out", "w_in"), gw0)
    reduce_halves(red1)
    dgu = _bwd_ffn_dact(dh, s0, w0, 1)
    reduce_chips(red0a)
    dh, gw, gs = _bwd_ffn_rest(dh, dgu, s0, w0, small, 0, 1)
    gsm[0].update(gs)
    red0b = reduce_begin("0b", ffn1, gw)
    reduced1 = reduce_end(red1)
    stacks = adamw({k: reduced1[k] for k in ffn1}, 1, None)

    gsmall = {k: jnp.stack([gsm[l][k].reshape(-1) for l in range(DEPTH)]) for k in gsm[0]}
    gsmall["rel_bias"] = jnp.transpose(_bias_grad(dbias, buckets)[:, :N_BUCKETS])
    gsmall["norm_final"] = dg_final.reshape(-1)
    small_like = [small[k] for k in SMALL]
    red = _small_allreduce(_pack([gsmall[k] for k in SMALL] + [loss_row[0, :1]]))
    gs = _unpack(red, small_like + [loss_row[0, :1]])
    loss = gs[-1][0]
    gs = dict(zip(SMALL, gs[:-1]))

    ffn2 = ("ffn2_gu", "ffn2_down")
    reduce_chips(red0b)
    stacks.update(adamw({k: reduced1[k] for k in ("w_in", "w_out")}, 1, None))
    reduce_halves(red0a)
    stacks.update(adamw({k: reduced1[k] for k in ffn2}, 1, None))
    dlt, m2, v2 = _adamw_small(*[[dct[k] for k in SMALL] for dct in (small, gs, small_m, small_v)])
    reduced0a = reduce_end(red0a)
    stacks.update(adamw({k: reduced0a[k] for k in ffn2}, 0, stacks))
    reduce_halves(red0b)
    stacks.update(adamw({k: reduced0a[k] for k in ("w_in", "w_out")}, 0, stacks))
    stacks.update(adamw(reduce_end(red0b), 0, stacks))

    out_g, out_d, out_m, out_v = {}, {}, {}, {}
    for k in BIG:
        out_g[k], out_d[k], out_m[k], out_v[k] = [jnp.swapaxes(a, 1, 2) if k == "w_in" else a for a in stacks[k]]
    for dst, parts in ((out_d, dlt), (out_m, m2), (out_v, v2)):
        dst.update(zip(SMALL, parts))
    out_g.update(gs)

    order = ("norm_ffn1", "ffn1_gu", "ffn1_down", "norm_mix", "w_in", "sinks", "norm_out_sb", "norm_out_swa", "w_out",
             "norm_ffn2", "ffn2_gu", "ffn2_down", "rel_bias", "norm_final")
    return (loss, dh[0].reshape(x.shape), *[out_g[k] for k in order], *[out_d[k] for k in order],
            *[out_m[k] for k in order], *[out_v[k] for k in order])
```

```python
import math

import numpy as np
import jax
import jax.numpy as jnp
from jax import lax
from jax.experimental import pallas as pl
from jax.experimental.pallas import tpu as pltpu

F32 = jnp.float32
BF16 = jnp.bfloat16

D_MODEL = 1024
DEPTH = 2
HEAD_DIM = 64
BLK = 128
N_BUCKETS = 32
MAX_DISTANCE = 128
D_FF = 2816
EPS = 1e-6
NEG_INF = -1e30
SB_W = 512
SWA_W = 512
KV_W = 128
IN_W = 2304
SCALE = HEAD_DIM ** -0.5
N_CHIPS = 4
FS = 2 * D_FF // N_CHIPS
LANES = 128
V7X_VMEM_LIMIT = 56 * 2 ** 20
TM = 512
SLAB_BLOCK_BYTES = 6 * 2 ** 20
ADAMW_BLOCK_BYTES = 2 ** 21
SB_KT = 512
SWA_G = 4

ADAM_LR = 0.001
ADAM_B1 = 0.9
ADAM_B2 = 0.999
ADAM_EPS = 1e-08
ADAM_WD = 0.01
ADAM_STEP = 10

MESH = pl.DeviceIdType.MESH
ANY = pl.BlockSpec(memory_space=pl.ANY)
HBM = pl.BlockSpec(memory_space=pltpu.HBM)
SEM = pl.BlockSpec(memory_space=pltpu.SEMAPHORE)
EFFECT = pltpu.SideEffectType.DATAFLOW_SIDE_EFFECTING


def _params(n_grid):
    return pltpu.CompilerParams(dimension_semantics=("arbitrary",) * n_grid, vmem_limit_bytes=V7X_VMEM_LIMIT)


_PREVIOUS = [None]


def _call(body, *, name, in_specs, out_specs, out_shape, grid=(), num_scalar_prefetch=0, scratch_shapes=(),
          input_output_aliases=None, compiler_params=None, hbm_args=0):
    n_in = len(in_specs)

    def run(*args):
        dep = _PREVIOUS[0]
        if any(dep is a for a in args):
            dep = None
        args = [pltpu.with_memory_space_constraint(a, pltpu.HBM) if i < hbm_args else a for i, a in enumerate(args)]
        specs = list(in_specs) + ([ANY] if dep is not None else [])
        k = num_scalar_prefetch + n_in
        fn = body if dep is None else (lambda *refs: body(*refs[:k], *refs[k + 1:]))
        if num_scalar_prefetch:
            shape = dict(grid_spec=pltpu.PrefetchScalarGridSpec(
                num_scalar_prefetch=num_scalar_prefetch, grid=grid, in_specs=specs, out_specs=out_specs,
                scratch_shapes=scratch_shapes))
        else:
            shape = dict(grid=grid, in_specs=specs, out_specs=out_specs, scratch_shapes=scratch_shapes)
        out = pl.pallas_call(fn, name=name, out_shape=out_shape, input_output_aliases=input_output_aliases or {},
                             compiler_params=compiler_params, **shape)(*args, *([] if dep is None else [dep]))
        _PREVIOUS[0] = jax.tree.leaves(out)[-1]
        return out

    return run


def _dot(a, b):
    return jnp.dot(a, b, preferred_element_type=F32)


def _dot_nt(a, b):
    return lax.dot_general(a, b, (((1,), (1,)), ((), ())), preferred_element_type=F32)


def _dot_tn(a, b):
    return lax.dot_general(a, b, (((0,), (0,)), ((), ())), preferred_element_type=F32)


def _rms_fwd(x, g):
    r = lax.rsqrt(jnp.mean(x * x, axis=-1, keepdims=True) + EPS)
    xh = x * r
    return xh * g, xh, r


def _rms_bwd(dy, xh, r, g):
    u = dy * g
    dx = r * (u - xh * jnp.mean(u * xh, axis=-1, keepdims=True))
    dg = jnp.sum(dy * xh, axis=0, keepdims=True)
    return dx, dg


def _softplus(z):
    neg_abs = lax.bitcast_convert_type(lax.bitcast_convert_type(z, jnp.int32) | jnp.int32(-2 ** 31), F32)
    sp = jnp.maximum(z, 0.0) + jnp.log(1.0 + jnp.exp(neg_abs))
    return sp, z - sp


def _norm_cast(h, g):
    t, w = h.shape

    def body(h_ref, g_ref, n_ref):
        y, _, _ = _rms_fwd(h_ref[...], g_ref[...])
        n_ref[...] = y.astype(BF16)

    return _call(
        body, name="norm_cast", grid=(t // TM,),
        in_specs=[pl.BlockSpec((TM, w), lambda i: (i, 0)), pl.BlockSpec((1, w), lambda i: (0, 0))],
        out_specs=pl.BlockSpec((TM, w), lambda i: (i, 0)),
        out_shape=jax.ShapeDtypeStruct((t, w), BF16), compiler_params=_params(1))(h, g)


def _ffn_gu(n, wgu):
    t, d = n.shape

    def body(n_ref, wg_ref, wu_ref, gu_ref, act_ref):
        x = n_ref[...]
        g = _dot(x, wg_ref[...])
        u = _dot(x, wu_ref[...])
        sig = jax.nn.sigmoid(g)
        silu = g * sig
        gu_ref[0] = (u * (sig + silu * (1.0 - sig))).astype(BF16)
        gu_ref[1] = silu.astype(BF16)
        act_ref[...] = (silu * u).astype(BF16)

    return _call(
        body, name="ffn_gu", grid=(2, t // TM),
        in_specs=[pl.BlockSpec((TM, d), lambda j, i: (i, 0)),
                  pl.BlockSpec((None, d, FS), lambda j, i: (j, 0, 0)),
                  pl.BlockSpec((None, d, FS), lambda j, i: (j + 2, 0, 0))],
        out_specs=[pl.BlockSpec((2, TM, FS), lambda j, i: (0, i, j)), pl.BlockSpec((TM, FS), lambda j, i: (i, j))],
        out_shape=[jax.ShapeDtypeStruct((2, t, D_FF), BF16), jax.ShapeDtypeStruct((t, D_FF), BF16)],
        compiler_params=_params(2))(n, wgu, wgu)


def _down_res(act, wdn, h, g_next):
    t, f = act.shape
    d = h.shape[1]

    def body(a_ref, w_ref, h_ref, g_ref, o_ref, n_ref):
        out = h_ref[...] + 0.5 * _dot(a_ref[...], w_ref[...])
        o_ref[...] = out
        n_ref[...] = _rms_fwd(out, g_ref[...])[0].astype(BF16)

    row = pl.BlockSpec((TM, d), lambda i: (i, 0))
    return _call(
        body, name="down_res", grid=(t // TM,),
        in_specs=[pl.BlockSpec((TM, f), lambda i: (i, 0)), pl.BlockSpec((f, d), lambda i: (0, 0)), row,
                  pl.BlockSpec((1, d), lambda i: (0, 0))],
        out_specs=[row, row],
        out_shape=[jax.ShapeDtypeStruct((t, d), F32), jax.ShapeDtypeStruct((t, d), BF16)],
        compiler_params=_params(1))(act, wdn, h, g_next)


def _proj(n, w_in_t):
    t, d = n.shape
    w = w_in_t.shape[0]

    def body(n_ref, w_ref, o_ref):
        o_ref[...] = _dot_nt(n_ref[...], w_ref[...]).astype(BF16)

    return _call(
        body, name="proj", grid=(t // TM,),
        in_specs=[pl.BlockSpec((TM, d), lambda i: (i, 0)), pl.BlockSpec((w, d), lambda i: (0, 0))],
        out_specs=pl.BlockSpec((TM, w), lambda i: (i, 0)),
        out_shape=jax.ShapeDtypeStruct((t, w), BF16), compiler_params=_params(1))(n, w_in_t)


def _out_res(o_sb, o_sw, g_sb, g_sw, w_out, h, g_next):
    t, d = h.shape

    def body(a_ref, b_ref, ga_ref, gb_ref, w_ref, h_ref, g_ref, o_ref, mix_ref, n_ref):
        ya, _, _ = _rms_fwd(a_ref[...], ga_ref[...])
        yb, _, _ = _rms_fwd(b_ref[...], gb_ref[...])
        mixed = jnp.concatenate([ya.astype(BF16), yb.astype(BF16)], axis=1)
        mix_ref[...] = mixed
        out = h_ref[...] + _dot(mixed, w_ref[...])
        o_ref[...] = out
        n_ref[...] = _rms_fwd(out, g_ref[...])[0].astype(BF16)

    row = pl.BlockSpec((TM, d), lambda i: (i, 0))
    return _call(
        body, name="out_res", grid=(t // TM,),
        in_specs=[pl.BlockSpec((TM, SB_W), lambda i: (i, 0)), pl.BlockSpec((TM, SWA_W), lambda i: (i, 0)),
                  pl.BlockSpec((1, SB_W), lambda i: (0, 0)), pl.BlockSpec((1, SWA_W), lambda i: (0, 0)),
                  pl.BlockSpec((d, d), lambda i: (0, 0)), row, pl.BlockSpec((1, d), lambda i: (0, 0))],
        out_specs=[row, row, row],
        out_shape=[jax.ShapeDtypeStruct((t, d), F32), jax.ShapeDtypeStruct((t, d), BF16),
                   jax.ShapeDtypeStruct((t, d), BF16)],
        compiler_params=_params(1))(o_sb, o_sw, g_sb, g_sw, w_out, h, g_next)


def _loss_head(h, g, tgt):
    t, d = h.shape

    def body(h_ref, g_ref, t_ref, dh_ref, dhb_ref, dg_ref, loss_ref):
        @pl.when(pl.program_id(0) == 0)
        def _():
            dg_ref[...] = jnp.zeros_like(dg_ref)
            loss_ref[...] = jnp.zeros_like(loss_ref)

        gg = g_ref[...]
        y, xh, r = _rms_fwd(h_ref[...], gg)
        err = y - t_ref[...]
        part = 0.5 * jnp.sum(jnp.sum(err * err, axis=1, keepdims=True) / d, axis=0, keepdims=True)
        loss_ref[...] += jnp.broadcast_to(part, loss_ref.shape)
        dx, dg = _rms_bwd(err / d, xh, r, gg)
        dh_ref[...] = dx
        dhb_ref[...] = dx.astype(BF16)
        dg_ref[...] += dg

    row = pl.BlockSpec((TM, d), lambda i: (i, 0))
    return _call(
        body, name="loss_head", grid=(t // TM,),
        in_specs=[row, pl.BlockSpec((1, d), lambda i: (0, 0)), row],
        out_specs=[row, row, pl.BlockSpec((1, d), lambda i: (0, 0)), pl.BlockSpec((1, LANES), lambda i: (0, 0))],
        out_shape=[jax.ShapeDtypeStruct((t, d), F32), jax.ShapeDtypeStruct((t, d), BF16),
                   jax.ShapeDtypeStruct((1, d), F32), jax.ShapeDtypeStruct((1, LANES), F32)],
        compiler_params=_params(1))(h, g, tgt)


def _ffn_dact(dh, wdn, gu):
    t, d = dh.shape
    tm = TM

    def body(dh_ref, w_ref, gu_ref, o_ref):
        da = 0.5 * _dot_nt(dh_ref[...].astype(BF16), w_ref[...])
        o_ref[0] = (da * gu_ref[0].astype(F32)).astype(BF16)
        o_ref[1] = (da * gu_ref[1].astype(F32)).astype(BF16)

    return _call(
        body, name="ffn_dact", grid=(2, t // tm),
        in_specs=[pl.BlockSpec((tm, d), lambda j, i: (i, 0)), pl.BlockSpec((FS, d), lambda j, i: (j, 0)),
                  pl.BlockSpec((2, tm, FS), lambda j, i: (0, i, j))],
        out_specs=pl.BlockSpec((2, tm, FS), lambda j, i: (0, i, j)),
        out_shape=jax.ShapeDtypeStruct((2, t, D_FF), BF16), compiler_params=_params(2))(dh, wdn, gu)


def _dn_norm_bwd(a, a_spec, w, w_spec, nk, dh, h_in, g, w_transposed=False, tm=TM):
    t, d = dh.shape
    mm = _dot if w_transposed else _dot_nt

    def body(a_ref, w_ref, dh_ref, h_ref, g_ref, o_ref, ob_ref, dg_ref, acc_ref):
        i, k = pl.program_id(0), pl.program_id(1)

        if nk > 1:
            @pl.when(k == 0)
            def _():
                acc_ref[...] = mm(a_ref[...], w_ref[...])

            @pl.when((k > 0) & (k < nk - 1))
            def _():
                acc_ref[...] += mm(a_ref[...], w_ref[...])

        @pl.when(k == nk - 1)
        def _():
            gg = g_ref[...]
            dg = jnp.zeros_like(gg)
            for rows in (slice(r, r + TM // 2) for r in range(0, tm, TM // 2)):
                dn = mm(a_ref[rows, :], w_ref[...])
                if nk > 1:
                    dn = dn + acc_ref[rows, :]
                _, xh, r = _rms_fwd(h_ref[rows, :], gg)
                dx, dg_rows = _rms_bwd(dn, xh, r, gg)
                out = dh_ref[rows, :] + dx
                o_ref[rows, :] = out
                ob_ref[rows, :] = out.astype(BF16)
                dg = dg + dg_rows

            @pl.when(i == 0)
            def _():
                dg_ref[...] = dg

            @pl.when(i > 0)
            def _():
                dg_ref[...] += dg

    row = pl.BlockSpec((tm, d), lambda i, k: (i, 0))
    return _call(
        body, name="dn_norm_bwd", grid=(t // tm, nk),
        in_specs=[a_spec, w_spec, row, row, pl.BlockSpec((1, d), lambda i, k: (0, 0))],
        out_specs=[row, row, pl.BlockSpec((1, d), lambda i, k: (0, 0))],
        out_shape=[jax.ShapeDtypeStruct((t, d), F32), jax.ShapeDtypeStruct((t, d), BF16),
                   jax.ShapeDtypeStruct((1, d), F32)],
        scratch_shapes=[pltpu.VMEM((tm, d), F32)], compiler_params=_params(2))(a, w, dh, h_in, g)


def _ffn_dn(dgu, wgu, dh, h_in, g):
    d = dh.shape[1]
    tm = 2 * TM
    return _dn_norm_bwd(
        dgu, pl.BlockSpec((None, tm, FS), lambda i, k: (k // 2, i, k % 2)),
        wgu, pl.BlockSpec((None, d, FS), lambda i, k: (k, 0, 0)), N_CHIPS, dh, h_in, g, tm=tm)


def _mix_dn(dproj, w_in_t, dh, h_in, g):
    d = dh.shape[1]
    w = dproj.shape[1]
    return _dn_norm_bwd(
        dproj, pl.BlockSpec((TM, w), lambda i, k: (i, 0)),
        w_in_t, pl.BlockSpec((w, d), lambda i, k: (0, 0)), 1, dh, h_in, g, w_transposed=True)


def _dmixed(dh, w_out, o_sb, o_sw, g_sb, g_sw):
    t, d = dh.shape

    def body(dh_ref, w_ref, a_ref, b_ref, ga_ref, gb_ref, o_ref, dga_ref, dgb_ref):
        i = pl.program_id(0)
        dm = _dot_nt(dh_ref[...].astype(BF16), w_ref[...])
        _, xa, ra = _rms_fwd(a_ref[...], ga_ref[...])
        _, xb, rb = _rms_fwd(b_ref[...], gb_ref[...])
        da, dga = _rms_bwd(dm[:, :SB_W], xa, ra, ga_ref[...])
        db, dgb = _rms_bwd(dm[:, SB_W:], xb, rb, gb_ref[...])
        o_ref[...] = jnp.concatenate([da.astype(BF16), db.astype(BF16)], axis=1)

        @pl.when(i == 0)
        def _():
            dga_ref[...] = dga
            dgb_ref[...] = dgb

        @pl.when(i > 0)
        def _():
            dga_ref[...] += dga
            dgb_ref[...] += dgb

    return _call(
        body, name="dmixed", grid=(t // TM,),
        in_specs=[pl.BlockSpec((TM, d), lambda i: (i, 0)), pl.BlockSpec((d, d), lambda i: (0, 0)),
                  pl.BlockSpec((TM, SB_W), lambda i: (i, 0)), pl.BlockSpec((TM, SWA_W), lambda i: (i, 0)),
                  pl.BlockSpec((1, SB_W), lambda i: (0, 0)), pl.BlockSpec((1, SWA_W), lambda i: (0, 0))],
        out_specs=[pl.BlockSpec((TM, d), lambda i: (i, 0)), pl.BlockSpec((1, SB_W), lambda i: (0, 0)),
                   pl.BlockSpec((1, SWA_W), lambda i: (0, 0))],
        out_shape=[jax.ShapeDtypeStruct((t, d), BF16), jax.ShapeDtypeStruct((1, SB_W), F32),
                   jax.ShapeDtypeStruct((1, SWA_W), F32)],
        compiler_params=_params(1))(dh, w_out, o_sb, o_sw, g_sb, g_sw)


def _wgrad(name, a, a_spec, b, b_spec, grid, out_shape, out_spec, scale):
    def body(a_ref, b_ref, o_ref):
        r = _dot_tn(a_ref[...], b_ref[...].astype(BF16))
        o_ref[...] = r if scale == 1.0 else scale * r

    return _call(
        body, name=name, grid=grid, in_specs=[a_spec, b_spec], out_specs=out_spec,
        out_shape=jax.ShapeDtypeStruct(out_shape, F32), compiler_params=_params(len(grid)))(a, b)


def _wgrad_gu(n, dgu):
    t, d = n.shape
    return _wgrad(
        "wgrad_gu", n, pl.BlockSpec((t, TM), lambda s, r: (0, r)),
        dgu, pl.BlockSpec((None, t, FS), lambda s, r: (s // 2, 0, s % 2)), (N_CHIPS, d // TM),
        (N_CHIPS, d, FS), pl.BlockSpec((None, TM, FS), lambda s, r: (s, r, 0)), 1.0)


def _wgrad_down(act, dh):
    t, d = dh.shape
    return _wgrad(
        "wgrad_down", act, pl.BlockSpec((t, FS), lambda s: (0, s)), dh, pl.BlockSpec((t, d), lambda s: (0, 0)),
        (2,), (D_FF, d), pl.BlockSpec((FS, d), lambda s: (s, 0)), 0.5)


def _wgrad_out(mixed, dh):
    t, d = dh.shape
    return _wgrad(
        "wgrad_out", mixed, pl.BlockSpec((t, TM), lambda s: (0, s)), dh, pl.BlockSpec((t, d), lambda s: (0, 0)),
        (d // TM,), (d, d), pl.BlockSpec((TM, d), lambda s: (s, 0)), 1.0)


def _wgrad_in(n, dproj):
    t, d = n.shape
    w = dproj.shape[1]
    tw = w // 3
    return _wgrad(
        "wgrad_in", dproj, pl.BlockSpec((t, tw), lambda s: (0, s)), n, pl.BlockSpec((t, d), lambda s: (0, 0)),
        (3,), (w, d), pl.BlockSpec((tw, d), lambda s: (s, 0)), 1.0)


def _tri(rel):
    row = lax.broadcasted_iota(jnp.int32, (BLK, BLK), 0)
    col = lax.broadcasted_iota(jnp.int32, (BLK, BLK), 1)
    m = rel(row, col).astype(BF16)
    return jnp.concatenate([m, m], axis=0)


def _scan_dot(x, tri2):
    hi = x.astype(BF16)
    lo = (x - hi.astype(F32)).astype(BF16)
    return _dot(jnp.concatenate([hi, lo], axis=1), tri2)


def _head_masks():
    lane = lax.broadcasted_iota(jnp.int32, (1, LANES), 1)
    return [lane < HEAD_DIM, lane >= HEAD_DIM]


SB_PAIRS = 2
SB_ROWS = 2 * SB_PAIRS * BLK


def _sb_causal():
    row = lax.broadcasted_iota(jnp.int32, (SB_ROWS, BLK), 0) & (BLK - 1)
    return lax.broadcasted_iota(jnp.int32, (SB_ROWS, BLK), 1) < row


def _sb_mask_last(x, causal):
    own = jnp.where(causal, x[:, -BLK:], 0.0)
    return own if x.shape[1] == BLK else jnp.concatenate([x[:, :-BLK], own], axis=1)


def _sb_stack(x, hm):
    return jnp.concatenate([jnp.where(m, x[:, p * LANES:(p + 1) * LANES], jnp.zeros((BLK, LANES), x.dtype))
                            for p in range(SB_PAIRS) for m in hm], axis=0)


def _sb_unstack(y, hm):
    return jnp.concatenate([jnp.where(hm[0], y[2 * p * BLK:(2 * p + 1) * BLK], y[(2 * p + 1) * BLK:(2 * p + 2) * BLK])
                            for p in range(SB_PAIRS)], axis=1)


def _sb_pairs():
    return [(slice(2 * p * BLK, (2 * p + 2) * BLK), slice(p * LANES, (p + 1) * LANES)) for p in range(SB_PAIRS)]


def _sb_fwd(proj):
    t = proj.shape[0]
    nb = SB_KT // BLK
    wide = SB_PAIRS * LANES

    def body(q_ref, k_ref, v_ref, o_ref, tot_ref):
        hm = _head_masks()
        causal = _sb_causal()
        pairs = _sb_pairs()
        after = _tri(lambda r, c: r > c)

        def tile(qh, start, n_blk, carry, acc, own):
            ks = pl.ds(pl.multiple_of(start, BLK), n_blk * BLK)
            z = jnp.concatenate([_dot_nt(qh[rows], k_ref[ks, lanes]) for rows, lanes in pairs], axis=0)
            sp, zs = _softplus(z)
            spm = _sb_mask_last(sp, causal) if own else sp
            sufs = [None] * n_blk
            for b in reversed(range(n_blk)):
                blk = spm[:, b * BLK:(b + 1) * BLK]
                sufs[b] = carry + _scan_dot(blk, after)
                carry = carry + jnp.sum(blk, axis=1, keepdims=True)
            w = jnp.exp(zs - jnp.concatenate(sufs, axis=1))
            wb = (_sb_mask_last(w, causal) if own else w).astype(BF16)
            return carry, acc + jnp.concatenate([_dot(wb[rows], v_ref[ks, lanes]) for rows, lanes in pairs], axis=0)

        def qblock(g, j):
            qs = pl.ds(pl.multiple_of(g * SB_KT + j * BLK, BLK), BLK)
            qh = _sb_stack(q_ref[qs, :] * SCALE, hm)
            c0 = tile(qh, g * SB_KT, j + 1, jnp.zeros((SB_ROWS, 1), F32), jnp.zeros((SB_ROWS, LANES), F32), True)
            carry, acc = lax.fori_loop(0, g, lambda n, c: tile(qh, (g - 1 - n) * SB_KT, nb, c[0], c[1], False), c0)
            o_ref[qs, :] = _sb_unstack(acc, hm)
            for h in range(2 * SB_PAIRS):
                tot_ref[h, qs, :] = carry[h * BLK:(h + 1) * BLK]

        def group(g, _):
            for j in range(nb):
                qblock(g, j)
            return 0

        lax.fori_loop(0, t // SB_KT, group, 0)

    col_blk = lambda off: pl.BlockSpec((t, wide), lambda g: (0, off + g))
    n_steps = SB_W // wide
    return _call(
        body, name="sb_fwd", grid=(n_steps,), in_specs=[col_blk(0), col_blk(n_steps), col_blk(2 * n_steps)],
        out_specs=[col_blk(0), pl.BlockSpec((2 * SB_PAIRS, t, 1), lambda g: (g, 0, 0))],
        out_shape=[jax.ShapeDtypeStruct((t, SB_W), F32), jax.ShapeDtypeStruct((8, t, 1), F32)],
        compiler_params=_params(1))(proj, proj, proj)


def _sb_bwd(proj, d_o, tot):
    t = proj.shape[0]
    nb = SB_KT // BLK
    wide = SB_PAIRS * LANES

    def body(q_ref, k_ref, v_ref, do_ref, tot_ref, dq_ref, dk_ref, dv_ref, dk_acc, dv_acc):
        hm = _head_masks()
        causal = _sb_causal()
        pairs = _sb_pairs()
        before = _tri(lambda r, c: r < c)
        upto = _tri(lambda r, c: r <= c)
        dk_acc[...] = jnp.zeros_like(dk_acc)
        dv_acc[...] = jnp.zeros_like(dv_acc)

        def tile(qh, doh, tt, start, n_blk, pre, ecum, dq, own):
            ks = pl.ds(pl.multiple_of(start, BLK), n_blk * BLK)
            k = k_ref[ks, :]
            v = v_ref[ks, :]
            z = jnp.concatenate([_dot_nt(qh[rows], k[:, lanes]) for rows, lanes in pairs], axis=0)
            sp, zs = _softplus(z)
            spm = _sb_mask_last(sp, causal) if own else sp
            pres = []
            for b in range(n_blk):
                blk = spm[:, b * BLK:(b + 1) * BLK]
                pres.append(pre + _scan_dot(blk, before))
                pre = pre + jnp.sum(blk, axis=1, keepdims=True)
            logw = z - (tt - jnp.concatenate(pres, axis=1))
            if own:
                logw = jnp.minimum(logw, 0.0)
            w = jnp.exp(logw)
            if own:
                w = _sb_mask_last(w, causal)
            e = w * jnp.concatenate([_dot_nt(doh[rows], v[:, lanes]) for rows, lanes in pairs], axis=0)
            incs = []
            for b in range(n_blk):
                blk = e[:, b * BLK:(b + 1) * BLK]
                incs.append(ecum + _scan_dot(blk, upto))
                ecum = ecum + jnp.sum(blk, axis=1, keepdims=True)
            dz = e - jnp.exp(zs) * jnp.concatenate(incs, axis=1)
            if own:
                dz = _sb_mask_last(dz, causal)
            dzb = dz.astype(BF16)
            wb = w.astype(BF16)
            for rows, lanes in pairs:
                dk_acc[ks, lanes] += _dot_tn(dzb[rows], qh[rows])
                dv_acc[ks, lanes] += _dot_tn(wb[rows], doh[rows])
            return pre, ecum, dq + jnp.concatenate([_dot(dzb[rows], k[:, lanes]) for rows, lanes in pairs], axis=0)

        def qblock(g, j):
            qs = pl.ds(pl.multiple_of(g * SB_KT + j * BLK, BLK), BLK)
            qh = _sb_stack(q_ref[qs, :] * SCALE, hm)
            doh = _sb_stack(do_ref[qs, :], hm)
            tt = jnp.concatenate([tot_ref[h, qs, :] for h in range(2 * SB_PAIRS)], axis=0)
            c0 = (jnp.zeros((SB_ROWS, 1), F32), jnp.zeros((SB_ROWS, 1), F32), jnp.zeros((SB_ROWS, LANES), F32))
            c = lax.fori_loop(0, g, lambda kt, c: tile(qh, doh, tt, kt * SB_KT, nb, c[0], c[1], c[2], False), c0)
            dq = tile(qh, doh, tt, g * SB_KT, j + 1, c[0], c[1], c[2], True)[2]
            dq_ref[qs, :] = (_sb_unstack(dq, hm) * SCALE).astype(BF16)

        def group(g, _):
            for j in range(nb):
                qblock(g, j)
            return 0

        lax.fori_loop(0, t // SB_KT, group, 0)
        dk_ref[...] = dk_acc[...].astype(BF16)
        dv_ref[...] = dv_acc[...].astype(BF16)

    col_blk = lambda off: pl.BlockSpec((t, wide), lambda g: (0, off + g))
    n_steps = SB_W // wide
    out = jax.ShapeDtypeStruct((t, SB_W), BF16)
    return _call(
        body, name="sb_bwd", grid=(n_steps,),
        in_specs=[col_blk(0), col_blk(n_steps), col_blk(2 * n_steps), col_blk(0),
                  pl.BlockSpec((2 * SB_PAIRS, t, 1), lambda g: (g, 0, 0))],
        out_specs=[col_blk(0), col_blk(0), col_blk(0)], out_shape=[out, out, out],
        scratch_shapes=[pltpu.VMEM((t, wide), F32), pltpu.VMEM((t, wide), F32)],
        compiler_params=_params(1))(proj, proj, proj, d_o, tot)


def _bucket_table():
    a = np.arange(BLK)[:, None]
    c = np.arange(2 * BLK)[None, :]
    dist = np.maximum(BLK + a - c, 0)
    max_exact = N_BUCKETS // 2
    dd = np.maximum(dist, 1).astype(np.float32)
    large = max_exact + (np.log(dd / max_exact) / math.log(MAX_DISTANCE / max_exact)
                         * (N_BUCKETS - max_exact)).astype(np.int32)
    large = np.minimum(large, N_BUCKETS - 1)
    return np.where(dist < max_exact, dist, large).astype(np.int32)


SWA_H = 8


def _swa_band_masks():
    row = lax.broadcasted_iota(jnp.int32, (SWA_H * BLK, 2 * BLK), 0) & (BLK - 1)
    col = lax.broadcasted_iota(jnp.int32, (SWA_H * BLK, 2 * BLK), 1)
    own = lax.broadcasted_iota(jnp.int32, (SWA_H * BLK, BLK), 1) <= (
        lax.broadcasted_iota(jnp.int32, (SWA_H * BLK, BLK), 0) & (BLK - 1))
    return (col > row) & ((col < BLK) | (col - BLK <= row)), own


def _swa_stack(ref, qs, hm, scale):
    parts = []
    for hq in range(SWA_H):
        kvh = hq // SWA_G
        x = ref[qs, (hq // 2) * LANES:(hq // 2 + 1) * LANES].astype(F32)
        if hq % 2 != kvh:
            x = pltpu.roll(x, HEAD_DIM, 1)
        parts.append(jnp.where(hm[kvh], x * scale, 0.0).astype(BF16))
    return jnp.concatenate(parts, axis=0)


def _swa_unstack(x8, hm):
    heads = []
    for hq in range(SWA_H):
        x = x8[hq * BLK:(hq + 1) * BLK]
        heads.append(pltpu.roll(x, HEAD_DIM, 1) if hq % 2 != hq // SWA_G else x)
    return [jnp.where(hm[0], heads[2 * p], heads[2 * p + 1]) for p in range(SWA_H // 2)]


def _swa_scores(q8, kb, bias_ref, mask, cols):
    bias8 = jnp.concatenate([bias_ref[hq, :, cols] for hq in range(SWA_H)], axis=0)
    return jnp.where(mask, _dot_nt(q8, kb) + bias8, NEG_INF)


def _swa_sinks(sink_ref):
    return jnp.concatenate([jnp.broadcast_to(sink_ref[hq:hq + 1, 0:1], (BLK, 1)) for hq in range(SWA_H)], axis=0)


def _swa_fwd(proj, bias, sinks_b):
    t = proj.shape[0]
    nq = t // BLK

    def body(q_ref, k_ref, v_ref, bias_ref, sink_ref, o_ref, lse_ref):
        hm = _head_masks()
        band, own = _swa_band_masks()

        def qblock(i, prev):
            qs = pl.ds(pl.multiple_of(i * BLK, BLK), BLK)
            if prev:
                ks, mask, cols = pl.ds(pl.multiple_of((i - 1) * BLK, BLK), 2 * BLK), band, slice(None)
            else:
                ks, mask, cols = qs, own, slice(BLK, None)
            q8 = _swa_stack(q_ref, qs, hm, SCALE)
            sink8 = _swa_sinks(sink_ref)
            s = _swa_scores(q8, k_ref[ks, :], bias_ref, mask, cols)
            m = jnp.maximum(jnp.max(s, axis=1, keepdims=True), sink8)
            p = jnp.exp(s - m)
            den = jnp.sum(p, axis=1, keepdims=True) + jnp.exp(sink8 - m)
            o8 = _dot((p * (1.0 / den)).astype(BF16), v_ref[ks, :])
            lse8 = m + jnp.log(den)
            for hq in range(SWA_H):
                lse_ref[hq, qs, :] = lse8[hq * BLK:(hq + 1) * BLK]
            for pp, o in enumerate(_swa_unstack(o8, hm)):
                o_ref[qs, pp * LANES:(pp + 1) * LANES] = o

        qblock(0, False)

        def step(i, _):
            qblock(i, True)
            return 0

        lax.fori_loop(1, nq, step, 0)

    return _call(
        body, name="swa_fwd", grid=(1,),
        in_specs=[pl.BlockSpec((t, SWA_W), lambda i: (0, 3)), pl.BlockSpec((t, KV_W), lambda i: (0, 16)),
                  pl.BlockSpec((t, KV_W), lambda i: (0, 17)), pl.BlockSpec((8, BLK, 2 * BLK), lambda i: (0, 0, 0)),
                  pl.BlockSpec((8, LANES), lambda i: (0, 0))],
        out_specs=[pl.BlockSpec((t, SWA_W), lambda i: (0, 0)), pl.BlockSpec((8, t, 1), lambda i: (0, 0, 0))],
        out_shape=[jax.ShapeDtypeStruct((t, SWA_W), F32), jax.ShapeDtypeStruct((8, t, 1), F32)],
        compiler_params=_params(1))(proj, proj, proj, bias, sinks_b)


def _swa_bwd(proj, d_o, lse, bias, sinks_b, dbias_in):
    t = proj.shape[0]
    nq = t // BLK

    def body(q_ref, k_ref, v_ref, do_ref, lse_ref, bias_ref, sink_ref, dbi_ref,
             dq_ref, dk_ref, dv_ref, dsink_ref, dbias_ref, dk_acc, dv_acc):
        hm = _head_masks()
        band, own = _swa_band_masks()
        dk_acc[...] = jnp.zeros_like(dk_acc)
        dv_acc[...] = jnp.zeros_like(dv_acc)
        dbias_ref[...] = dbi_ref[...]

        def qblock(i, prev, dsink8):
            qs = pl.ds(pl.multiple_of(i * BLK, BLK), BLK)
            if prev:
                ks, mask, cols = pl.ds(pl.multiple_of((i - 1) * BLK, BLK), 2 * BLK), band, slice(None)
            else:
                ks, mask, cols = qs, own, slice(BLK, None)
            q8 = _swa_stack(q_ref, qs, hm, SCALE)
            do8 = _swa_stack(do_ref, qs, hm, 1.0)
            sink8 = _swa_sinks(sink_ref)
            lse8 = jnp.concatenate([lse_ref[hq, qs, :] for hq in range(SWA_H)], axis=0)
            kb = k_ref[ks, :]
            p = jnp.exp(_swa_scores(q8, kb, bias_ref, mask, cols) - lse8)
            dp = _dot_nt(do8, v_ref[ks, :])
            delta = jnp.sum(p * dp, axis=1, keepdims=True)
            ds = p * (dp - delta)
            for hq in range(SWA_H):
                dbias_ref[hq, :, cols] += ds[hq * BLK:(hq + 1) * BLK]
            dsb = ds.astype(BF16)
            dk_acc[ks, :] += _dot_tn(dsb, q8)
            dv_acc[ks, :] += _dot_tn(p.astype(BF16), do8)
            for pp, dq in enumerate(_swa_unstack(_dot(dsb, kb) * SCALE, hm)):
                dq_ref[qs, pp * LANES:(pp + 1) * LANES] = dq.astype(BF16)
            return dsink8 - jnp.exp(sink8 - lse8) * delta

        ds0 = qblock(0, False, jnp.zeros((SWA_H * BLK, 1), F32))
        ds8 = lax.fori_loop(1, nq, lambda i, c: qblock(i, True, c), ds0)
        for hq in range(SWA_H):
            dsink_ref[hq:hq + 1, :] = jnp.broadcast_to(
                jnp.sum(ds8[hq * BLK:(hq + 1) * BLK], axis=0, keepdims=True), (1, LANES))

        dk_ref[...] = dk_acc[...].astype(BF16)
        dv_ref[...] = dv_acc[...].astype(BF16)

    full3 = pl.BlockSpec((8, BLK, 2 * BLK), lambda i: (0, 0, 0))
    kv = jax.ShapeDtypeStruct((t, KV_W), BF16)
    return _call(
        body, name="swa_bwd", grid=(1,),
        in_specs=[pl.BlockSpec((t, SWA_W), lambda i: (0, 3)), pl.BlockSpec((t, KV_W), lambda i: (0, 16)),
                  pl.BlockSpec((t, KV_W), lambda i: (0, 17)), pl.BlockSpec((t, SWA_W), lambda i: (0, 1)),
                  pl.BlockSpec((8, t, 1), lambda i: (0, 0, 0)), full3, pl.BlockSpec((8, LANES), lambda i: (0, 0)),
                  full3],
        out_specs=[pl.BlockSpec((t, SWA_W), lambda i: (0, 0)), pl.BlockSpec((t, KV_W), lambda i: (0, 0)),
                   pl.BlockSpec((t, KV_W), lambda i: (0, 0)), pl.BlockSpec((8, LANES), lambda i: (0, 0)), full3],
        out_shape=[jax.ShapeDtypeStruct((t, SWA_W), BF16), kv, kv, jax.ShapeDtypeStruct((8, LANES), F32),
                   jax.ShapeDtypeStruct((8, BLK, 2 * BLK), F32)],
        scratch_shapes=[pltpu.VMEM((t, KV_W), F32), pltpu.VMEM((t, KV_W), F32)],
        compiler_params=_params(1))(proj, proj, proj, d_o, lse, bias, sinks_b, dbias_in)


def _concat_cols(parts):
    t = parts[0].shape[0]
    widths = [a.shape[1] for a in parts]

    def body(*refs):
        refs[-1][...] = jnp.concatenate([r[...] for r in refs[:-1]], axis=1)

    return _call(
        body, name="concat_cols", grid=(t // TM,),
        in_specs=[pl.BlockSpec((TM, w), lambda i: (i, 0)) for w in widths],
        out_specs=pl.BlockSpec((TM, sum(widths)), lambda i: (i, 0)),
        out_shape=jax.ShapeDtypeStruct((t, sum(widths)), parts[0].dtype), compiler_params=_params(1))(*parts)


def _bias_table(rel_bias, buckets):
    def body(rb_ref, b_ref, o_ref):
        bk = b_ref[...]
        for h in range(8):
            acc = jnp.zeros((BLK, 2 * BLK), F32)
            for b in range(N_BUCKETS):
                acc = jnp.where(bk == b, rb_ref[b, h], acc)
            o_ref[h] = acc

    return _call(
        body, name="bias_table", grid=(1,),
        in_specs=[pl.BlockSpec(memory_space=pltpu.SMEM), pl.BlockSpec((BLK, 2 * BLK), lambda i: (0, 0))],
        out_specs=pl.BlockSpec((8, BLK, 2 * BLK), lambda i: (0, 0, 0)),
        out_shape=jax.ShapeDtypeStruct((8, BLK, 2 * BLK), F32), compiler_params=_params(1))(rel_bias, buckets)


def _bias_grad(dbias, buckets):
    def body(d_ref, b_ref, o_ref):
        lane = lax.broadcasted_iota(jnp.int32, (1, LANES), 1)
        bk = b_ref[...]
        for h in range(8):
            d = d_ref[h]
            acc = jnp.zeros((1, LANES), F32)
            for b in range(N_BUCKETS):
                s = jnp.sum(jnp.sum(jnp.where(bk == b, d, 0.0), axis=0, keepdims=True), axis=1, keepdims=True)
                acc = acc + jnp.where(lane == b, s, 0.0)
            o_ref[h:h + 1, :] = acc

    return _call(
        body, name="bias_grad", grid=(1,),
        in_specs=[pl.BlockSpec((8, BLK, 2 * BLK), lambda i: (0, 0, 0)), pl.BlockSpec((BLK, 2 * BLK), lambda i: (0, 0))],
        out_specs=pl.BlockSpec((8, LANES), lambda i: (0, 0)),
        out_shape=jax.ShapeDtypeStruct((8, LANES), F32), compiler_params=_params(1))(dbias, buckets)


def _row(a):
    return a.reshape(1, -1)


def _fwd_ffn1_gu(h, n1, w):
    s = {"h0": h, "n1": n1}
    s["gu1"], s["act1"] = _ffn_gu(n1, w["ffn1_gu"])
    return s


def _fwd_ffn1_down(s, w, small, l):
    s["h1"], s["nm"] = _down_res(s["act1"], w["ffn1_down"], s["h0"], _row(small["norm_mix"][l]))


def _fwd_ffn1(h, n1, w, small, l):
    s = _fwd_ffn1_gu(h, n1, w)
    _fwd_ffn1_down(s, w, small, l)
    return s


def _fwd_proj_sb(s, w):
    s["proj"] = _proj(s["nm"], w["w_in"])
    s["o_sb"], s["tot"] = _sb_fwd(s["proj"])


def _fwd_swa(s, small, l, bias):
    s["sinks_b"] = jnp.broadcast_to(small["sinks"][l][:, None], (8, LANES))
    s["o_sw"], s["lse"] = _swa_fwd(s["proj"], bias, s["sinks_b"])


def _fwd_out_gu2(s, w, small, l):
    s["h2"], s["mixed"], s["n2"] = _out_res(
        s["o_sb"], s["o_sw"], _row(small["norm_out_sb"][l]), _row(small["norm_out_swa"][l]), w["w_out"], s["h1"],
        _row(small["norm_ffn2"][l]))
    s["gu2"], s["act2"] = _ffn_gu(s["n2"], w["ffn2_gu"])


def _fwd_ffn2_down(s, w, g_after):
    return _down_res(s["act2"], w["ffn2_down"], s["h2"], g_after)


def _fwd_out_ffn2(s, w, small, l, g_after):
    _fwd_out_gu2(s, w, small, l)
    return _fwd_ffn2_down(s, w, g_after)


def _bwd_ffn_dact(dh, s, w, which):
    return _ffn_dact(dh[1], w[f"ffn{which}_down"], s[f"gu{which}"])


def _bwd_ffn_rest(dh, dgu, s, w, small, l, which):
    h_in, norm = (s["h0"], "norm_ffn1") if which == 1 else (s["h2"], "norm_ffn2")
    g_down = _wgrad_down(s[f"act{which}"], dh[1])
    g_gu = _wgrad_gu(s[f"n{which}"], dgu)
    dh32, dh16, dg = _ffn_dn(dgu, w[f"ffn{which}_gu"], dh[0], h_in, _row(small[norm][l]))
    return (dh32, dh16), {f"ffn{which}_down": g_down, f"ffn{which}_gu": g_gu}, {norm: dg}


def _bwd_ffn(dh, s, w, small, l, which):
    return _bwd_ffn_rest(dh, _bwd_ffn_dact(dh, s, w, which), s, w, small, l, which)


def _bwd_mix(dh, s, w, small, l, bias, dbias):
    g_out = _wgrad_out(s["mixed"], dh[1])
    d_o, dg_sb, dg_sw = _dmixed(dh[1], w["w_out"], s["o_sb"], s["o_sw"], _row(small["norm_out_sb"][l]),
                                _row(small["norm_out_swa"][l]))
    dq_sb, dk_sb, dv_sb = _sb_bwd(s["proj"], d_o, s["tot"])
    dq_sw, dk_sw, dv_sw, dsink, dbias = _swa_bwd(s["proj"], d_o, s["lse"], bias, s["sinks_b"], dbias)
    dproj = _concat_cols([dq_sb, dk_sb, dv_sb, dq_sw, dk_sw, dv_sw])
    g_in = _wgrad_in(s["nm"], dproj)
    dh32, dh16, dg_mix = _mix_dn(dproj, w["w_in"], dh[0], s["h1"], _row(small["norm_mix"][l]))
    gs = {"norm_out_sb": dg_sb, "norm_out_swa": dg_sw, "sinks": dsink[:, 0], "norm_mix": dg_mix}
    return (dh32, dh16), {"w_out": g_out, "w_in": g_in}, gs, dbias


def _place():
    x, y, c = lax.axis_index("x"), lax.axis_index("y"), lax.axis_index("c")
    return x, y, c, 2 * x + y


def _chip_core(k, c):
    return (k // 2, k % 2, c)


def _rows_per_block(rows, cols, copies):
    best = 16
    for tr in range(16, rows + 1, 16):
        if rows % tr == 0 and copies * tr * cols * 4 <= SLAB_BLOCK_BYTES:
            best = tr
    assert rows % best == 0
    return best


def _place_own(w, l, me1):
    _, rows, cols = w.shape
    tr = _rows_per_block(rows // 2, cols, 1)
    per_half = rows // 2 // tr

    def body(me_ref, w_ref, o_ref):
        o_ref[...] = w_ref[...].astype(BF16)

    return _call(
        body, name="place_own",
        num_scalar_prefetch=1, grid=(rows // tr,),
        in_specs=[pl.BlockSpec((None, tr, cols), lambda r, me: (l, r, 0))],
        out_specs=pl.BlockSpec((None, None, tr, cols), lambda r, me: (me[0], r // per_half, r % per_half, 0)),
        out_shape=jax.ShapeDtypeStruct((N_CHIPS, 2, rows // 2, cols), BF16), compiler_params=_params(1))(me1, w)


def _plan_gather_ici(bufs):
    _, _, c, me = _place()
    return [(b.at[me, c], b.at[me, c], b.at[(me + 3 - j) % N_CHIPS, c], _chip_core((me + 1 + j) % N_CHIPS, c))
            for b in bufs for j in range(3)]


def _plan_gather_d2d(bufs):
    x, y, c, me = _place()
    return [(b.at[(me + 3 - j) % N_CHIPS, c], b.at[(me + 3 - j) % N_CHIPS, c], b.at[(me + 3 - j) % N_CHIPS, 1 - c],
             (x, y, 1 - c)) for b in bufs for j in range(3)]


def _plan_grad_sibling(bufs):
    x, y, c, _ = _place()
    n = len(bufs) // 2
    return [(g.at[:, 1 - c], z, z, (x, y, 1 - c)) for g, z in zip(bufs[:n], bufs[n:])]


def _plan_grad_chips(bufs):
    _, _, c, me = _place()
    n = len(bufs) // 2
    return [(p.at[j], z.at[j], z.at[j], _chip_core((me + 1 + j) % N_CHIPS, c))
            for p, z in zip(bufs[:n], bufs[n:]) for j in range(3)]


def _plan_grad_halves(l):
    def plan(bufs):
        x, y, c, _ = _place()
        return [(b.at[l, c], b.at[l, c], b.at[l, 1 - c], (x, y, 1 - c)) for b in bufs]
    return plan


def _remote(src, dst, send_sem, recv_sem, to):
    return pltpu.make_async_remote_copy(src_ref=src, dst_ref=dst, send_sem=send_sem, recv_sem=recv_sem,
                                        device_id=to, device_id_type=MESH)


SIBLING_BARRIER_ID = 0


def _sibling_handshake():
    x, y, c, _ = _place()
    barrier = pltpu.get_barrier_semaphore()
    pl.semaphore_signal(barrier, inc=1, device_id=(x, y, 1 - c), device_id_type=MESH)
    pl.semaphore_wait(barrier, 1)


def _split_params(sibling_only):
    return pltpu.CompilerParams(has_side_effects=EFFECT, collective_id=SIBLING_BARRIER_ID if sibling_only else None)


def _exchange_start_groups(name, plan, groups, sibling_only=False):
    sizes = [len(g) for g, _ in groups]
    bufs = [a for g, _ in groups for a in g]
    n, n_groups = len(bufs), len(groups)

    def body(*refs):
        if sibling_only:
            _sibling_handshake()
        ins, sems, token = refs[:n], refs[n:n + 2 * n_groups], refs[-1]
        at = 0
        for k, size in enumerate(sizes):
            for i, (src, dst, _, to) in enumerate(plan(ins[at:at + size])):
                _remote(src, dst, sems[2 * k].at[i], sems[2 * k + 1].at[i], to).start()
            at += size
        token[...] = jnp.zeros_like(token)

    sem_shapes = [pltpu.SemaphoreType.DMA((n_copies,)) for _, n_copies in groups for _ in range(2)]
    out = _call(
        body, name=name,
        out_shape=(*sem_shapes, *[pltpu.HBM(a.shape, a.dtype) for a in bufs], jax.ShapeDtypeStruct((8, LANES), F32)),
        in_specs=[HBM] * n,
        out_specs=(*[SEM] * (2 * n_groups), *[HBM] * n, pl.BlockSpec(memory_space=pltpu.VMEM)),
        input_output_aliases={t: 2 * n_groups + t for t in range(n)}, hbm_args=n,
        compiler_params=_split_params(sibling_only),
    )(*bufs)
    flights, at = [], 2 * n_groups
    for k, size in enumerate(sizes):
        flights.append(((out[2 * k], out[2 * k + 1]), list(out[at:at + size])))
        at += size
    return flights


def _exchange_start(name, plan, bufs, n_copies, sibling_only=False):
    return _exchange_start_groups(name, plan, [(bufs, n_copies)], sibling_only)[0]


def _exchange_wait(name, plan, bufs, sems):
    n = len(bufs)

    def body(*refs):
        ins = refs[:n]
        ssem, rsem = refs[n], refs[n + 1]
        for i, (src, dst, land, to) in enumerate(plan(ins)):
            _remote(src, dst, ssem.at[i], rsem.at[i], to).wait_send()
            _remote(land, land, ssem.at[i], rsem.at[i], to).wait_recv()

    return list(_call(
        body, name=name, out_shape=[pltpu.HBM(a.shape, a.dtype) for a in bufs],
        in_specs=[HBM] * n + [SEM, SEM], out_specs=[HBM] * n,
        input_output_aliases={t: t for t in range(n)},
        compiler_params=pltpu.CompilerParams(has_side_effects=EFFECT),
    )(*bufs, sems[0], sems[1]))


def _exchange_pass(name, done, plan, bufs, sems, n_copies):
    n = len(bufs)

    def body(*refs):
        _sibling_handshake()
        ins = refs[:n]
        old_s, old_r, ssem, rsem = refs[n], refs[n + 1], refs[n + 2], refs[n + 3]
        token = refs[-1]
        for i, (src, dst, land, to) in enumerate(done(ins)):
            _remote(src, dst, old_s.at[i], old_r.at[i], to).wait_send()
            _remote(land, land, old_s.at[i], old_r.at[i], to).wait_recv()
        for i, (src, dst, _, to) in enumerate(plan(ins)):
            _remote(src, dst, ssem.at[i], rsem.at[i], to).start()
        token[...] = jnp.zeros_like(token)

    out = _call(
        body, name=name,
        out_shape=(pltpu.SemaphoreType.DMA((n_copies,)), pltpu.SemaphoreType.DMA((n_copies,)),
                   *[pltpu.HBM(a.shape, a.dtype) for a in bufs], jax.ShapeDtypeStruct((8, LANES), F32)),
        in_specs=[HBM] * n + [SEM, SEM], out_specs=(SEM, SEM, *[HBM] * n, pl.BlockSpec(memory_space=pltpu.VMEM)),
        input_output_aliases={t: 2 + t for t in range(n)},
        compiler_params=_split_params(True),
    )(*bufs, sems[0], sems[1])
    return (out[0], out[1]), list(out[2:2 + n])


def _chip_sum(g, xbuf, cm):
    _, _, r2, cols = g.shape
    tr = _rows_per_block(r2, cols, 1)

    def body(cm_ref, g_ref, x_ref, o_ref):
        o_ref[...] = (g_ref[...] + x_ref[...]).astype(BF16)

    return _call(
        body, name="grad_chip_sum",
        num_scalar_prefetch=1, grid=(3, r2 // tr),
        in_specs=[pl.BlockSpec((None, None, tr, cols), lambda j, r, cm: ((cm[1] + 1 + j) % N_CHIPS, cm[0], r, 0)),
                  pl.BlockSpec((None, tr, cols), lambda j, r, cm: ((cm[1] + 1 + j) % N_CHIPS, r, 0))],
        out_specs=pl.BlockSpec((None, tr, cols), lambda j, r, cm: (j, r, 0)),
        out_shape=jax.ShapeDtypeStruct((3, r2, cols), BF16), compiler_params=_params(2))(cm, g, xbuf)


def _total_sum(g, xbuf, rbuf, cm, l, prev):
    _, _, r2, cols = g.shape
    tr = _rows_per_block(r2, cols, 3)

    def body(cm_ref, g_ref, x_ref, r_ref, *rest):
        acc = g_ref[...] + x_ref[...]
        for j in range(3):
            acc = acc + r_ref[j].astype(F32)
        rest[-1][...] = acc

    return _call(
        body, name="grad_total_sum",
        num_scalar_prefetch=1, grid=(r2 // tr,),
        in_specs=[pl.BlockSpec((None, None, tr, cols), lambda r, cm: (cm[1], cm[0], r, 0)),
                  pl.BlockSpec((None, tr, cols), lambda r, cm: (cm[1], r, 0)),
                  pl.BlockSpec((3, tr, cols), lambda r, cm: (0, r, 0))] + ([] if prev is None else [ANY]),
        out_specs=pl.BlockSpec((None, None, tr, cols), lambda r, cm: (l, cm[0], r, 0)),
        out_shape=jax.ShapeDtypeStruct((DEPTH, 2, r2, cols), F32),
        input_output_aliases={} if prev is None else {4: 0},
        compiler_params=_params(1))(cm, g, xbuf, rbuf, *([] if prev is None else [prev]))


def _small_allreduce(v):
    rows = v.shape[0]
    n_dev = 2 * N_CHIPS

    def body(v_ref, o_ref, buf, ssem, rsem):
        x, y, c, _ = _place()
        me = 4 * x + 2 * y + c
        buf[me] = v_ref[...]

        def copy(d, slot, to):
            return _remote(v_ref, buf.at[slot], ssem.at[d - 1], rsem.at[d - 1], (to // 4, (to // 2) % 2, to % 2))

        cps = [copy(d, me, (me + d) % n_dev) for d in range(1, n_dev)]
        for cp in cps:
            cp.start()
        for d in range(1, n_dev):
            copy(d, (me + n_dev - d) % n_dev, me).wait_recv()
        for cp in cps:
            cp.wait_send()
        acc = buf[0]
        for i in range(1, n_dev):
            acc = acc + buf[i]
        o_ref[...] = acc

    vm = pl.BlockSpec(memory_space=pltpu.VMEM)
    return _call(
        body, name="small_allreduce", in_specs=[vm], out_specs=vm,
        out_shape=jax.ShapeDtypeStruct(v.shape, F32),
        scratch_shapes=[pltpu.VMEM((n_dev, rows, LANES), F32), pltpu.SemaphoreType.DMA((n_dev - 1,)),
                        pltpu.SemaphoreType.DMA((n_dev - 1,))],
        compiler_params=pltpu.CompilerParams(vmem_limit_bytes=V7X_VMEM_LIMIT))(v)


def _adamw_math(w, g, m, v):
    m2 = ADAM_B1 * m + (1.0 - ADAM_B1) * g
    v2 = ADAM_B2 * v + (1.0 - ADAM_B2) * (g * g)
    v_hat = v2 / (1.0 - ADAM_B2 ** ADAM_STEP)
    step = (-ADAM_LR / (1.0 - ADAM_B1 ** ADAM_STEP)) * m2 / (jnp.sqrt(v_hat) + ADAM_EPS)
    return step + (-ADAM_LR * ADAM_WD) * w, m2, v2


def _adamw_layer(w, g, m, v, l, prev):
    _, rows, cols = w.shape
    tr = rows
    for cand in range(8, rows + 1, 8):
        if rows % cand == 0 and cand * cols * 4 <= ADAMW_BLOCK_BYTES:
            tr = cand

    def body(w_ref, g_ref, m_ref, v_ref, *outs):
        d_ref, m2_ref, v2_ref = outs[-3:]
        d_ref[...], m2_ref[...], v2_ref[...] = _adamw_math(w_ref[...], g_ref[...], m_ref[...], v_ref[...])

    stack = pl.BlockSpec((None, tr, cols), lambda i: (l, i, 0))
    ins, specs, alias = [w, g, m, v], [stack] * 4, {}
    if prev is not None:
        ins += list(prev)
        specs += [ANY] * 3
        alias = {4 + i: i for i in range(3)}
    return _call(
        body, name="adamw", grid=(rows // tr,), in_specs=specs, out_specs=[stack] * 3,
        out_shape=[jax.ShapeDtypeStruct(w.shape, F32)] * 3, input_output_aliases=alias,
        compiler_params=_params(1))(*ins)


def _adamw_small(ws, gs, ms, vs):
    n = len(ws)
    row = lambda a: a.reshape(1, -1) if a.ndim == 1 else a
    ins = [row(a) for group in (ws, gs, ms, vs) for a in group]

    def body(*refs):
        w, g, m, v = (refs[i * n:(i + 1) * n] for i in range(4))
        outs = refs[4 * n:]
        for i in range(n):
            d, m2, v2 = _adamw_math(w[i][...], g[i][...], m[i][...], v[i][...])
            outs[i][...], outs[n + i][...], outs[2 * n + i][...] = d, m2, v2

    vm = pl.BlockSpec(memory_space=pltpu.VMEM)
    outs = _call(
        body, name="adamw_small", in_specs=[vm] * (4 * n), out_specs=[vm] * (3 * n),
        out_shape=[jax.ShapeDtypeStruct(a.shape, F32) for a in ins[:n]] * 3,
        compiler_params=pltpu.CompilerParams(vmem_limit_bytes=V7X_VMEM_LIMIT))(*ins)
    outs = [o.reshape(w.shape) for o, w in zip(outs, list(ws) * 3)]
    return outs[:n], outs[n:2 * n], outs[2 * n:]


SMALL = ("norm_ffn1", "norm_mix", "sinks", "norm_out_sb", "norm_out_swa", "norm_ffn2", "rel_bias", "norm_final")
BIG = ("ffn1_gu", "ffn1_down", "w_in", "w_out", "ffn2_gu", "ffn2_down")


def _pack(parts):
    flat, n = [], 0
    for a in parts:
        a = a.reshape(-1).astype(F32)
        gap = -a.shape[0] % LANES
        flat += [a] + ([jnp.zeros((gap,), F32)] if gap else [])
        n += a.shape[0] + gap
    tail = -(n // LANES) % 8 * LANES
    return jnp.concatenate(flat + ([jnp.zeros((tail,), F32)] if tail else [])).reshape(-1, LANES)


def _unpack(packed, like):
    out, r = [], 0
    for a in like:
        n = math.prod(a.shape)
        nr = -(-n // LANES)
        out.append(packed[r:r + nr].reshape(-1)[:n].reshape(a.shape))
        r += nr
    return out


def _halved(a):
    k, r, cols = a.shape
    return a.reshape(k, 2, r // 2, cols)


def _weight_view(k, buf):
    full = buf.reshape(N_CHIPS, buf.shape[2] * 2, buf.shape[3])
    return full if k.endswith("_gu") else full.reshape(-1, D_MODEL)


def _grad_stack(k, g):
    if not k.endswith("_gu"):
        g = g.reshape(N_CHIPS, g.shape[0] // N_CHIPS, D_MODEL)
    return _halved(g)


def _empty_like_hbm(shape, dtype):
    return pltpu.with_memory_space_constraint(lax.empty(shape, dtype), pltpu.HBM)


def kernel(x, norm_ffn1, w_ffn1_gu, w_ffn1_down, norm_mix, w_in, sinks, norm_out_sb, norm_out_swa, w_out, norm_ffn2, w_ffn2_gu, w_ffn2_down, rel_bias, norm_final, loss_target, m_norm_ffn1, m_w_ffn1_gu, m_w_ffn1_down, m_norm_mix, m_w_in, m_sinks, m_norm_out_sb, m_norm_out_swa, m_w_out, m_norm_ffn2, m_w_ffn2_gu, m_w_ffn2_down, m_rel_bias, m_norm_final, v_norm_ffn1, v_w_ffn1_gu, v_w_ffn1_down, v_norm_mix, v_w_in, v_sinks, v_norm_out_sb, v_norm_out_swa, v_w_out, v_norm_ffn2, v_w_ffn2_gu, v_w_ffn2_down, v_rel_bias, v_norm_final):
    big_w = dict(ffn1_gu=w_ffn1_gu, ffn1_down=w_ffn1_down, w_in=w_in, w_out=w_out, ffn2_gu=w_ffn2_gu, ffn2_down=w_ffn2_down)
    big_m = dict(ffn1_gu=m_w_ffn1_gu, ffn1_down=m_w_ffn1_down, w_in=m_w_in, w_out=m_w_out, ffn2_gu=m_w_ffn2_gu, ffn2_down=m_w_ffn2_down)
    big_v = dict(ffn1_gu=v_w_ffn1_gu, ffn1_down=v_w_ffn1_down, w_in=v_w_in, w_out=v_w_out, ffn2_gu=v_w_ffn2_gu, ffn2_down=v_w_ffn2_down)
    small = dict(norm_ffn1=norm_ffn1, norm_mix=norm_mix, sinks=sinks, norm_out_sb=norm_out_sb, norm_out_swa=norm_out_swa,
                 norm_ffn2=norm_ffn2, rel_bias=rel_bias, norm_final=norm_final)
    small_m = dict(norm_ffn1=m_norm_ffn1, norm_mix=m_norm_mix, sinks=m_sinks, norm_out_sb=m_norm_out_sb,
                   norm_out_swa=m_norm_out_swa, norm_ffn2=m_norm_ffn2, rel_bias=m_rel_bias, norm_final=m_norm_final)
    small_v = dict(norm_ffn1=v_norm_ffn1, norm_mix=v_norm_mix, sinks=v_sinks, norm_out_sb=v_norm_out_sb,
                   norm_out_swa=v_norm_out_swa, norm_ffn2=v_norm_ffn2, rel_bias=v_rel_bias, norm_final=v_norm_final)
    for dct in (big_w, big_m, big_v):
        dct["w_in"] = jnp.swapaxes(dct["w_in"], 1, 2)
    _PREVIOUS[0] = None
    _, _, c, me = _place()
    cm = jnp.stack([c, me]).astype(jnp.int32)
    buckets = jnp.asarray(_bucket_table())
    ffn1, mix_in, rest = ("ffn1_gu", "ffn1_down"), ("w_in",), ("w_out", "ffn2_gu", "ffn2_down")

    def place(l, keys):
        return [_place_own(big_w[k], l, cm[1:]) for k in keys]

    def views(keys, bufs):
        return {k: _weight_view(k, b) for k, b in zip(keys, bufs)}

    def gather_start(tag, bufs):
        return _exchange_start(f"gather{tag}_ici_start", _plan_gather_ici, bufs, 3 * len(bufs))

    def gather_pass(tag, flight):
        return _exchange_pass(f"gather{tag}_pass", _plan_gather_ici, _plan_gather_d2d, flight[1], flight[0],
                              3 * len(flight[1]))

    def gather_done(tag, keys, flight):
        return views(keys, _exchange_wait(f"gather{tag}_d2d_wait", _plan_gather_d2d, flight[1], flight[0]))

    fly_gu0 = gather_start("0a", place(0, ffn1[:1]))
    fly_down0 = gather_start("0a2", place(0, ffn1[1:]))
    fly_in0 = gather_start("0b", place(0, mix_in))
    later = [place(l, keys) for l in range(DEPTH) for keys in ((rest,) if l == 0 else (ffn1, mix_in, rest))]
    fly_rest0, fly_ffn1, fly_in1, fly_rest1 = _exchange_start_groups(
        "gather_later_ici_start", _plan_gather_ici, [(bufs, 3 * len(bufs)) for bufs in later])
    bias = _bias_table(rel_bias, buckets)
    n1 = _norm_cast(x[0], _row(norm_ffn1[0]))
    w0 = gather_done("0a", ffn1[:1], gather_pass("0a", fly_gu0))

    s0 = _fwd_ffn1_gu(x[0], n1, w0)
    w0.update(gather_done("0a2", ffn1[1:], gather_pass("0a2", fly_down0)))
    fly_in0 = gather_pass("0b", fly_in0)
    _fwd_ffn1_down(s0, w0, small, 0)
    w0.update(gather_done("0b", mix_in, fly_in0))
    _fwd_proj_sb(s0, w0)
    fly_rest0 = gather_pass("0c", fly_rest0)
    _fwd_swa(s0, small, 0, bias)
    w0.update(gather_done("0c", rest, fly_rest0))
    _fwd_out_gu2(s0, w0, small, 0)
    fly_ffn1 = gather_pass("1a", fly_ffn1)
    h, n1 = _fwd_ffn2_down(s0, w0, _row(norm_ffn1[1]))
    w1 = gather_done("1a", ffn1, fly_ffn1)
    fly_in1 = gather_pass("1b", fly_in1)
    s1 = _fwd_ffn1(h, n1, w1, small, 1)
    w1.update(gather_done("1b", mix_in, fly_in1))
    _fwd_proj_sb(s1, w1)
    fly_rest1 = gather_pass("1c", fly_rest1)
    _fwd_swa(s1, small, 1, bias)
    w1.update(gather_done("1c", rest, fly_rest1))
    h, _ = _fwd_out_ffn2(s1, w1, small, 1, _row(norm_final))
    dh32, dh16, dg_final, loss_row = _loss_head(h, _row(norm_final), loss_target[0])
    dh = (dh32, dh16)

    def landing(stacks, lead, dtype):
        return [_empty_like_hbm((lead,) + a.shape[2:], dtype) for a in stacks]

    def reduce_begin(tag, keys, gw):
        stacks = [_grad_stack(k, gw[k]) for k in keys]
        flight = _exchange_start(f"grad{tag}_sibling_start", _plan_grad_sibling,
                                 stacks + landing(stacks, N_CHIPS, F32), len(keys), sibling_only=True)
        return dict(tag=tag, keys=keys, stacks=stacks, flight=flight)

    def reduce_chips(st):
        n, (sems, bufs) = len(st["keys"]), st["flight"]
        bufs = _exchange_wait(f"grad{st['tag']}_sibling_wait", _plan_grad_sibling, bufs, sems)
        st["own"] = list(zip(bufs[:n], bufs[n:]))
        st["flight"] = _exchange_start(f"grad{st['tag']}_chips_start", _plan_grad_chips,
                                       [_chip_sum(g, z, cm) for g, z in st["own"]] + landing(st["stacks"], 3, BF16),
                                       3 * n)

    def reduce_halves(st, l, prev):
        n, (sems, bufs) = len(st["keys"]), st["flight"]
        bufs = _exchange_wait(f"grad{st['tag']}_chips_wait", _plan_grad_chips, bufs, sems)
        halves = [_total_sum(g, x, z, cm, l, None if prev is None else prev[k].reshape(DEPTH, 2, *g.shape[2:]))
                  for k, (g, x), z in zip(st["keys"], st["own"], bufs[n:])]
        st["plan"] = _plan_grad_halves(l)
        st["flight"] = _exchange_start(f"grad{st['tag']}_halves_start", st["plan"], halves, n, sibling_only=True)

    def reduce_end(st):
        sems, bufs = st["flight"]
        bufs = _exchange_wait(f"grad{st['tag']}_halves_wait", st["plan"], bufs, sems)
        return {k: b.reshape(big_w[k].shape) for k, b in zip(st["keys"], bufs)}

    def adamw(reduced, keys, l, prev):
        return {k: _adamw_layer(big_w[k], reduced[k], big_m[k], big_v[k], l, None if prev is None else prev[k])
                for k in keys}

    gsm = [dict() for _ in range(DEPTH)]
    dbias = jnp.zeros((8, BLK, 2 * BLK), F32)
    dh, gw1, gs = _bwd_ffn(dh, s1, w1, small, 1, 2)
    gsm[1].update(gs)
    dh, gw, gs, dbias = _bwd_mix(dh, s1, w1, small, 1, bias, dbias)
    gw1.update(gw)
    gsm[1].update(gs)
    dh, gw, gs = _bwd_ffn(dh, s1, w1, small, 1, 1)
    gw1.update(gw)
    gsm[1].update(gs)

    red1 = reduce_begin("1", BIG, gw1)
    dh, gw0, gs = _bwd_ffn(dh, s0, w0, small, 0, 2)
    gsm[0].update(gs)
    reduce_chips(red1)
    dh, gw, gs, dbias = _bwd_mix(dh, s0, w0, small, 0, bias, dbias)
    gw0.update(gw)
    gsm[0].update(gs)
    red0a = reduce_begin("0a", ("ffn2_gu", "ffn2_down", "w_out", "w_in"), gw0)
    reduce_halves(red1, 1, None)
    dgu = _bwd_ffn_dact(dh, s0, w0, 1)
    reduce_chips(red0a)
    dh, gw, gs = _bwd_ffn_rest(dh, dgu, s0, w0, small, 0, 1)
    gsm[0].update(gs)
    red0b = reduce_begin("0b", ffn1, gw)
    reduced1 = reduce_end(red1)
    ffn2 = ("ffn2_gu", "ffn2_down")
    stacks = adamw(reduced1, ffn2, 1, None)

    gsmall = {k: jnp.stack([gsm[l][k].reshape(-1) for l in range(DEPTH)]) for k in gsm[0]}
    gsmall["rel_bias"] = jnp.transpose(_bias_grad(dbias, buckets)[:, :N_BUCKETS])
    gsmall["norm_final"] = dg_final.reshape(-1)
    small_like = [small[k] for k in SMALL]
    red = _small_allreduce(_pack([gsmall[k] for k in SMALL] + [loss_row[0, :1]]))
    gs = _unpack(red, small_like + [loss_row[0, :1]])
    loss = gs[-1][0]
    gs = dict(zip(SMALL, gs[:-1]))

    reduce_chips(red0b)
    stacks.update(adamw(reduced1, ("w_in", "w_out"), 1, None))
    reduce_halves(red0a, 0, reduced1)
    stacks.update(adamw(reduced1, ffn1, 1, None))
    dlt, m2, v2 = _adamw_small(*[[dct[k] for k in SMALL] for dct in (small, gs, small_m, small_v)])
    reduced0 = reduce_end(red0a)
    stacks.update(adamw(reduced0, ffn2, 0, stacks))
    reduce_halves(red0b, 0, reduced1)
    stacks.update(adamw(reduced0, ("w_in", "w_out"), 0, stacks))
    reduced0.update(reduce_end(red0b))
    stacks.update(adamw(reduced0, ffn1, 0, stacks))

    out_g, out_d, out_m, out_v = {}, {}, {}, {}
    for k in BIG:
        out_g[k], out_d[k], out_m[k], out_v[k] = [jnp.swapaxes(a, 1, 2) if k == "w_in" else a
                                                  for a in (reduced0[k], *stacks[k])]
    for dst, parts in ((out_d, dlt), (out_m, m2), (out_v, v2)):
        dst.update(zip(SMALL, parts))
    out_g.update(gs)

    order = ("norm_ffn1", "ffn1_gu", "ffn1_down", "norm_mix", "w_in", "sinks", "norm_out_sb", "norm_out_swa", "w_out",
             "norm_ffn2", "ffn2_gu", "ffn2_down", "rel_bias", "norm_final")
    return (loss, dh[0].reshape(x.shape), *[out_g[k] for k in order], *[out_d[k] for k in order],
            *[out_m[k] for k in order], *[out_v[k] for k in order])
```

```python
import math

import numpy as np
import jax
import jax.numpy as jnp
from jax import lax
from jax.experimental import pallas as pl
from jax.experimental.pallas import tpu as pltpu

F32 = jnp.float32
BF16 = jnp.bfloat16

D_MODEL = 1024
DEPTH = 2
HEAD_DIM = 64
BLK = 128
N_BUCKETS = 32
MAX_DISTANCE = 128
D_FF = 2816
EPS = 1e-6
NEG_INF = -1e30
SB_W = 512
SWA_W = 512
KV_W = 128
IN_W = 2304
SCALE = HEAD_DIM ** -0.5
N_CHIPS = 4
FS = 2 * D_FF // N_CHIPS
LANES = 128
V7X_VMEM_LIMIT = 56 * 2 ** 20
TM = 512
SLAB_BLOCK_BYTES = 6 * 2 ** 20
ADAMW_BLOCK_BYTES = 2 ** 21
SB_KT = 512
SWA_G = 4

ADAM_LR = 0.001
ADAM_B1 = 0.9
ADAM_B2 = 0.999
ADAM_EPS = 1e-08
ADAM_WD = 0.01
ADAM_STEP = 10

MESH = pl.DeviceIdType.MESH
ANY = pl.BlockSpec(memory_space=pl.ANY)
HBM = pl.BlockSpec(memory_space=pltpu.HBM)
SEM = pl.BlockSpec(memory_space=pltpu.SEMAPHORE)
EFFECT = pltpu.SideEffectType.DATAFLOW_SIDE_EFFECTING


def _params(n_grid):
    return pltpu.CompilerParams(dimension_semantics=("arbitrary",) * n_grid, vmem_limit_bytes=V7X_VMEM_LIMIT)


_PREVIOUS = [None]


def _call(body, *, name, in_specs, out_specs, out_shape, grid=(), num_scalar_prefetch=0, scratch_shapes=(),
          input_output_aliases=None, compiler_params=None, hbm_args=0):
    n_in = len(in_specs)

    def run(*args):
        dep = _PREVIOUS[0]
        if any(dep is a for a in args):
            dep = None
        args = [pltpu.with_memory_space_constraint(a, pltpu.HBM) if i < hbm_args else a for i, a in enumerate(args)]
        specs = list(in_specs) + ([ANY] if dep is not None else [])
        k = num_scalar_prefetch + n_in
        fn = body if dep is None else (lambda *refs: body(*refs[:k], *refs[k + 1:]))
        if num_scalar_prefetch:
            shape = dict(grid_spec=pltpu.PrefetchScalarGridSpec(
                num_scalar_prefetch=num_scalar_prefetch, grid=grid, in_specs=specs, out_specs=out_specs,
                scratch_shapes=scratch_shapes))
        else:
            shape = dict(grid=grid, in_specs=specs, out_specs=out_specs, scratch_shapes=scratch_shapes)
        out = pl.pallas_call(fn, name=name, out_shape=out_shape, input_output_aliases=input_output_aliases or {},
                             compiler_params=compiler_params, **shape)(*args, *([] if dep is None else [dep]))
        _PREVIOUS[0] = jax.tree.leaves(out)[-1]
        return out

    return run


def _dot(a, b):
    return jnp.dot(a, b, preferred_element_type=F32)


def _dot_nt(a, b):
    return lax.dot_general(a, b, (((1,), (1,)), ((), ())), preferred_element_type=F32)


def _dot_tn(a, b):
    return lax.dot_general(a, b, (((0,), (0,)), ((), ())), preferred_element_type=F32)


def _rms_fwd(x, g):
    r = lax.rsqrt(jnp.mean(x * x, axis=-1, keepdims=True) + EPS)
    xh = x * r
    return xh * g, xh, r


def _rms_bwd(dy, xh, r, g):
    u = dy * g
    dx = r * (u - xh * jnp.mean(u * xh, axis=-1, keepdims=True))
    dg = jnp.sum(dy * xh, axis=0, keepdims=True)
    return dx, dg


def _softplus(z):
    neg_abs = lax.bitcast_convert_type(lax.bitcast_convert_type(z, jnp.int32) | jnp.int32(-2 ** 31), F32)
    sp = jnp.maximum(z, 0.0) + jnp.log(1.0 + jnp.exp(neg_abs))
    return sp, z - sp


def _norm_cast(h, g):
    t, w = h.shape

    def body(h_ref, g_ref, n_ref):
        y, _, _ = _rms_fwd(h_ref[...], g_ref[...])
        n_ref[...] = y.astype(BF16)

    return _call(
        body, name="norm_cast", grid=(t // TM,),
        in_specs=[pl.BlockSpec((TM, w), lambda i: (i, 0)), pl.BlockSpec((1, w), lambda i: (0, 0))],
        out_specs=pl.BlockSpec((TM, w), lambda i: (i, 0)),
        out_shape=jax.ShapeDtypeStruct((t, w), BF16), compiler_params=_params(1))(h, g)


def _ffn_gu(n, wgu):
    t, d = n.shape

    def body(n_ref, wg_ref, wu_ref, gu_ref, act_ref):
        x = n_ref[...]
        g = _dot(x, wg_ref[...])
        u = _dot(x, wu_ref[...])
        sig = jax.nn.sigmoid(g)
        silu = g * sig
        gu_ref[0] = (u * (sig + silu * (1.0 - sig))).astype(BF16)
        gu_ref[1] = silu.astype(BF16)
        act_ref[...] = (silu * u).astype(BF16)

    return _call(
        body, name="ffn_gu", grid=(2, t // TM),
        in_specs=[pl.BlockSpec((TM, d), lambda j, i: (i, 0)),
                  pl.BlockSpec((None, d, FS), lambda j, i: (j, 0, 0)),
                  pl.BlockSpec((None, d, FS), lambda j, i: (j + 2, 0, 0))],
        out_specs=[pl.BlockSpec((2, TM, FS), lambda j, i: (0, i, j)), pl.BlockSpec((TM, FS), lambda j, i: (i, j))],
        out_shape=[jax.ShapeDtypeStruct((2, t, D_FF), BF16), jax.ShapeDtypeStruct((t, D_FF), BF16)],
        compiler_params=_params(2))(n, wgu, wgu)


def _down_res(act, wdn, h, g_next):
    t, f = act.shape
    d = h.shape[1]

    def body(a_ref, w_ref, h_ref, g_ref, o_ref, n_ref):
        out = h_ref[...] + 0.5 * _dot(a_ref[...], w_ref[...])
        o_ref[...] = out
        n_ref[...] = _rms_fwd(out, g_ref[...])[0].astype(BF16)

    row = pl.BlockSpec((TM, d), lambda i: (i, 0))
    return _call(
        body, name="down_res", grid=(t // TM,),
        in_specs=[pl.BlockSpec((TM, f), lambda i: (i, 0)), pl.BlockSpec((f, d), lambda i: (0, 0)), row,
                  pl.BlockSpec((1, d), lambda i: (0, 0))],
        out_specs=[row, row],
        out_shape=[jax.ShapeDtypeStruct((t, d), F32), jax.ShapeDtypeStruct((t, d), BF16)],
        compiler_params=_params(1))(act, wdn, h, g_next)


def _proj(n, w_in_t):
    t, d = n.shape
    w = w_in_t.shape[0]

    def body(n_ref, w_ref, o_ref):
        o_ref[...] = _dot_nt(n_ref[...], w_ref[...]).astype(BF16)

    return _call(
        body, name="proj", grid=(t // TM,),
        in_specs=[pl.BlockSpec((TM, d), lambda i: (i, 0)), pl.BlockSpec((w, d), lambda i: (0, 0))],
        out_specs=pl.BlockSpec((TM, w), lambda i: (i, 0)),
        out_shape=jax.ShapeDtypeStruct((t, w), BF16), compiler_params=_params(1))(n, w_in_t)


def _out_res(o_sb, o_sw, g_sb, g_sw, w_out, h, g_next):
    t, d = h.shape

    def body(a_ref, b_ref, ga_ref, gb_ref, w_ref, h_ref, g_ref, o_ref, mix_ref, n_ref):
        ya, _, _ = _rms_fwd(a_ref[...], ga_ref[...])
        yb, _, _ = _rms_fwd(b_ref[...], gb_ref[...])
        mixed = jnp.concatenate([ya.astype(BF16), yb.astype(BF16)], axis=1)
        mix_ref[...] = mixed
        out = h_ref[...] + _dot(mixed, w_ref[...])
        o_ref[...] = out
        n_ref[...] = _rms_fwd(out, g_ref[...])[0].astype(BF16)

    row = pl.BlockSpec((TM, d), lambda i: (i, 0))
    return _call(
        body, name="out_res", grid=(t // TM,),
        in_specs=[pl.BlockSpec((TM, SB_W), lambda i: (i, 0)), pl.BlockSpec((TM, SWA_W), lambda i: (i, 0)),
                  pl.BlockSpec((1, SB_W), lambda i: (0, 0)), pl.BlockSpec((1, SWA_W), lambda i: (0, 0)),
                  pl.BlockSpec((d, d), lambda i: (0, 0)), row, pl.BlockSpec((1, d), lambda i: (0, 0))],
        out_specs=[row, row, row],
        out_shape=[jax.ShapeDtypeStruct((t, d), F32), jax.ShapeDtypeStruct((t, d), BF16),
                   jax.ShapeDtypeStruct((t, d), BF16)],
        compiler_params=_params(1))(o_sb, o_sw, g_sb, g_sw, w_out, h, g_next)


def _loss_head(h, g, tgt):
    t, d = h.shape

    def body(h_ref, g_ref, t_ref, dh_ref, dhb_ref, dg_ref, loss_ref):
        @pl.when(pl.program_id(0) == 0)
        def _():
            dg_ref[...] = jnp.zeros_like(dg_ref)
            loss_ref[...] = jnp.zeros_like(loss_ref)

        gg = g_ref[...]
        y, xh, r = _rms_fwd(h_ref[...], gg)
        err = y - t_ref[...]
        part = 0.5 * jnp.sum(jnp.sum(err * err, axis=1, keepdims=True) / d, axis=0, keepdims=True)
        loss_ref[...] += jnp.broadcast_to(part, loss_ref.shape)
        dx, dg = _rms_bwd(err / d, xh, r, gg)
        dh_ref[...] = dx
        dhb_ref[...] = dx.astype(BF16)
        dg_ref[...] += dg

    row = pl.BlockSpec((TM, d), lambda i: (i, 0))
    return _call(
        body, name="loss_head", grid=(t // TM,),
        in_specs=[row, pl.BlockSpec((1, d), lambda i: (0, 0)), row],
        out_specs=[row, row, pl.BlockSpec((1, d), lambda i: (0, 0)), pl.BlockSpec((1, LANES), lambda i: (0, 0))],
        out_shape=[jax.ShapeDtypeStruct((t, d), F32), jax.ShapeDtypeStruct((t, d), BF16),
                   jax.ShapeDtypeStruct((1, d), F32), jax.ShapeDtypeStruct((1, LANES), F32)],
        compiler_params=_params(1))(h, g, tgt)


def _ffn_dact(dh, wdn, gu):
    t, d = dh.shape
    tm = TM

    def body(dh_ref, w_ref, gu_ref, o_ref):
        da = 0.5 * _dot_nt(dh_ref[...].astype(BF16), w_ref[...])
        o_ref[0] = (da * gu_ref[0].astype(F32)).astype(BF16)
        o_ref[1] = (da * gu_ref[1].astype(F32)).astype(BF16)

    return _call(
        body, name="ffn_dact", grid=(2, t // tm),
        in_specs=[pl.BlockSpec((tm, d), lambda j, i: (i, 0)), pl.BlockSpec((FS, d), lambda j, i: (j, 0)),
                  pl.BlockSpec((2, tm, FS), lambda j, i: (0, i, j))],
        out_specs=pl.BlockSpec((2, tm, FS), lambda j, i: (0, i, j)),
        out_shape=jax.ShapeDtypeStruct((2, t, D_FF), BF16), compiler_params=_params(2))(dh, wdn, gu)


def _dn_norm_bwd(a, a_spec, w, w_spec, nk, dh, h_in, g, w_transposed=False, tm=TM):
    t, d = dh.shape
    mm = _dot if w_transposed else _dot_nt

    def body(a_ref, w_ref, dh_ref, h_ref, g_ref, o_ref, ob_ref, dg_ref, acc_ref):
        i, k = pl.program_id(0), pl.program_id(1)

        if nk > 1:
            @pl.when(k == 0)
            def _():
                acc_ref[...] = mm(a_ref[...], w_ref[...])

            @pl.when((k > 0) & (k < nk - 1))
            def _():
                acc_ref[...] += mm(a_ref[...], w_ref[...])

        @pl.when(k == nk - 1)
        def _():
            gg = g_ref[...]
            dg = jnp.zeros_like(gg)
            for rows in (slice(r, r + TM // 2) for r in range(0, tm, TM // 2)):
                dn = mm(a_ref[rows, :], w_ref[...])
                if nk > 1:
                    dn = dn + acc_ref[rows, :]
                _, xh, r = _rms_fwd(h_ref[rows, :], gg)
                dx, dg_rows = _rms_bwd(dn, xh, r, gg)
                out = dh_ref[rows, :] + dx
                o_ref[rows, :] = out
                ob_ref[rows, :] = out.astype(BF16)
                dg = dg + dg_rows

            @pl.when(i == 0)
            def _():
                dg_ref[...] = dg

            @pl.when(i > 0)
            def _():
                dg_ref[...] += dg

    row = pl.BlockSpec((tm, d), lambda i, k: (i, 0))
    return _call(
        body, name="dn_norm_bwd", grid=(t // tm, nk),
        in_specs=[a_spec, w_spec, row, row, pl.BlockSpec((1, d), lambda i, k: (0, 0))],
        out_specs=[row, row, pl.BlockSpec((1, d), lambda i, k: (0, 0))],
        out_shape=[jax.ShapeDtypeStruct((t, d), F32), jax.ShapeDtypeStruct((t, d), BF16),
                   jax.ShapeDtypeStruct((1, d), F32)],
        scratch_shapes=[pltpu.VMEM((tm, d), F32)], compiler_params=_params(2))(a, w, dh, h_in, g)


def _ffn_dn(dgu, wgu, dh, h_in, g):
    d = dh.shape[1]
    tm = 2 * TM
    return _dn_norm_bwd(
        dgu, pl.BlockSpec((None, tm, FS), lambda i, k: (k // 2, i, k % 2)),
        wgu, pl.BlockSpec((None, d, FS), lambda i, k: (k, 0, 0)), N_CHIPS, dh, h_in, g, tm=tm)


def _mix_dn(dproj, w_in_t, dh, h_in, g):
    d = dh.shape[1]
    w = dproj.shape[1]
    return _dn_norm_bwd(
        dproj, pl.BlockSpec((TM, w), lambda i, k: (i, 0)),
        w_in_t, pl.BlockSpec((w, d), lambda i, k: (0, 0)), 1, dh, h_in, g, w_transposed=True)


def _dmixed(dh, w_out, o_sb, o_sw, g_sb, g_sw):
    t, d = dh.shape

    def body(dh_ref, w_ref, a_ref, b_ref, ga_ref, gb_ref, o_ref, dga_ref, dgb_ref):
        i = pl.program_id(0)
        dm = _dot_nt(dh_ref[...].astype(BF16), w_ref[...])
        _, xa, ra = _rms_fwd(a_ref[...], ga_ref[...])
        _, xb, rb = _rms_fwd(b_ref[...], gb_ref[...])
        da, dga = _rms_bwd(dm[:, :SB_W], xa, ra, ga_ref[...])
        db, dgb = _rms_bwd(dm[:, SB_W:], xb, rb, gb_ref[...])
        o_ref[...] = jnp.concatenate([da.astype(BF16), db.astype(BF16)], axis=1)

        @pl.when(i == 0)
        def _():
            dga_ref[...] = dga
            dgb_ref[...] = dgb

        @pl.when(i > 0)
        def _():
            dga_ref[...] += dga
            dgb_ref[...] += dgb

    return _call(
        body, name="dmixed", grid=(t // TM,),
        in_specs=[pl.BlockSpec((TM, d), lambda i: (i, 0)), pl.BlockSpec((d, d), lambda i: (0, 0)),
                  pl.BlockSpec((TM, SB_W), lambda i: (i, 0)), pl.BlockSpec((TM, SWA_W), lambda i: (i, 0)),
                  pl.BlockSpec((1, SB_W), lambda i: (0, 0)), pl.BlockSpec((1, SWA_W), lambda i: (0, 0))],
        out_specs=[pl.BlockSpec((TM, d), lambda i: (i, 0)), pl.BlockSpec((1, SB_W), lambda i: (0, 0)),
                   pl.BlockSpec((1, SWA_W), lambda i: (0, 0))],
        out_shape=[jax.ShapeDtypeStruct((t, d), BF16), jax.ShapeDtypeStruct((1, SB_W), F32),
                   jax.ShapeDtypeStruct((1, SWA_W), F32)],
        compiler_params=_params(1))(dh, w_out, o_sb, o_sw, g_sb, g_sw)


def _wgrad(name, a, a_spec, b, b_spec, grid, out_shape, out_spec, scale):
    def body(a_ref, b_ref, o_ref):
        r = _dot_tn(a_ref[...], b_ref[...].astype(BF16))
        o_ref[...] = r if scale == 1.0 else scale * r

    return _call(
        body, name=name, grid=grid, in_specs=[a_spec, b_spec], out_specs=out_spec,
        out_shape=jax.ShapeDtypeStruct(out_shape, F32), compiler_params=_params(len(grid)))(a, b)


def _wgrad_gu(n, dgu):
    t, d = n.shape
    return _wgrad(
        "wgrad_gu", n, pl.BlockSpec((t, TM), lambda s, r: (0, r)),
        dgu, pl.BlockSpec((None, t, FS), lambda s, r: (s // 2, 0, s % 2)), (N_CHIPS, d // TM),
        (N_CHIPS, d, FS), pl.BlockSpec((None, TM, FS), lambda s, r: (s, r, 0)), 1.0)


def _wgrad_down(act, dh):
    t, d = dh.shape
    return _wgrad(
        "wgrad_down", act, pl.BlockSpec((t, FS), lambda s: (0, s)), dh, pl.BlockSpec((t, d), lambda s: (0, 0)),
        (2,), (D_FF, d), pl.BlockSpec((FS, d), lambda s: (s, 0)), 0.5)


def _wgrad_out(mixed, dh):
    t, d = dh.shape
    return _wgrad(
        "wgrad_out", mixed, pl.BlockSpec((t, TM), lambda s: (0, s)), dh, pl.BlockSpec((t, d), lambda s: (0, 0)),
        (d // TM,), (d, d), pl.BlockSpec((TM, d), lambda s: (s, 0)), 1.0)


def _wgrad_in(n, dproj):
    t, d = n.shape
    w = dproj.shape[1]
    tw = w // 3
    return _wgrad(
        "wgrad_in", dproj, pl.BlockSpec((t, tw), lambda s: (0, s)), n, pl.BlockSpec((t, d), lambda s: (0, 0)),
        (3,), (w, d), pl.BlockSpec((tw, d), lambda s: (s, 0)), 1.0)


def _tri(rel):
    row = lax.broadcasted_iota(jnp.int32, (BLK, BLK), 0)
    col = lax.broadcasted_iota(jnp.int32, (BLK, BLK), 1)
    m = rel(row, col).astype(BF16)
    return jnp.concatenate([m, m], axis=0)


def _scan_dot(x, tri2):
    hi = x.astype(BF16)
    lo = (x - hi.astype(F32)).astype(BF16)
    return _dot(jnp.concatenate([hi, lo], axis=1), tri2)


def _head_masks():
    lane = lax.broadcasted_iota(jnp.int32, (1, LANES), 1)
    return [lane < HEAD_DIM, lane >= HEAD_DIM]


SB_PAIRS = 2
SB_ROWS = 2 * SB_PAIRS * BLK


def _sb_causal():
    row = lax.broadcasted_iota(jnp.int32, (SB_ROWS, BLK), 0) & (BLK - 1)
    return lax.broadcasted_iota(jnp.int32, (SB_ROWS, BLK), 1) < row


def _sb_mask_last(x, causal):
    own = jnp.where(causal, x[:, -BLK:], 0.0)
    return own if x.shape[1] == BLK else jnp.concatenate([x[:, :-BLK], own], axis=1)


def _sb_stack(x, hm):
    return jnp.concatenate([jnp.where(m, x[:, p * LANES:(p + 1) * LANES], jnp.zeros((BLK, LANES), x.dtype))
                            for p in range(SB_PAIRS) for m in hm], axis=0)


def _sb_unstack(y, hm):
    return jnp.concatenate([jnp.where(hm[0], y[2 * p * BLK:(2 * p + 1) * BLK], y[(2 * p + 1) * BLK:(2 * p + 2) * BLK])
                            for p in range(SB_PAIRS)], axis=1)


def _sb_pairs():
    return [(slice(2 * p * BLK, (2 * p + 2) * BLK), slice(p * LANES, (p + 1) * LANES)) for p in range(SB_PAIRS)]


def _sb_fwd(proj):
    t = proj.shape[0]
    nb = SB_KT // BLK
    wide = SB_PAIRS * LANES

    def body(q_ref, k_ref, v_ref, o_ref, tot_ref):
        hm = _head_masks()
        causal = _sb_causal()
        pairs = _sb_pairs()
        after = _tri(lambda r, c: r > c)

        def tile(qh, start, n_blk, carry, acc, own):
            ks = pl.ds(pl.multiple_of(start, BLK), n_blk * BLK)
            z = jnp.concatenate([_dot_nt(qh[rows], k_ref[ks, lanes]) for rows, lanes in pairs], axis=0)
            sp, zs = _softplus(z)
            spm = _sb_mask_last(sp, causal) if own else sp
            sufs = [None] * n_blk
            for b in reversed(range(n_blk)):
                blk = spm[:, b * BLK:(b + 1) * BLK]
                sufs[b] = carry + _scan_dot(blk, after)
                carry = carry + jnp.sum(blk, axis=1, keepdims=True)
            w = jnp.exp(zs - jnp.concatenate(sufs, axis=1))
            wb = (_sb_mask_last(w, causal) if own else w).astype(BF16)
            return carry, acc + jnp.concatenate([_dot(wb[rows], v_ref[ks, lanes]) for rows, lanes in pairs], axis=0)

        def qblock(g, j):
            qs = pl.ds(pl.multiple_of(g * SB_KT + j * BLK, BLK), BLK)
            qh = _sb_stack(q_ref[qs, :] * SCALE, hm)
            c0 = tile(qh, g * SB_KT, j + 1, jnp.zeros((SB_ROWS, 1), F32), jnp.zeros((SB_ROWS, LANES), F32), True)
            carry, acc = lax.fori_loop(0, g, lambda n, c: tile(qh, (g - 1 - n) * SB_KT, nb, c[0], c[1], False), c0)
            o_ref[qs, :] = _sb_unstack(acc, hm)
            for h in range(2 * SB_PAIRS):
                tot_ref[h, qs, :] = carry[h * BLK:(h + 1) * BLK]

        def group(g, _):
            for j in range(nb):
                qblock(g, j)
            return 0

        lax.fori_loop(0, t // SB_KT, group, 0)

    col_blk = lambda off: pl.BlockSpec((t, wide), lambda g: (0, off + g))
    n_steps = SB_W // wide
    return _call(
        body, name="sb_fwd", grid=(n_steps,), in_specs=[col_blk(0), col_blk(n_steps), col_blk(2 * n_steps)],
        out_specs=[col_blk(0), pl.BlockSpec((2 * SB_PAIRS, t, 1), lambda g: (g, 0, 0))],
        out_shape=[jax.ShapeDtypeStruct((t, SB_W), F32), jax.ShapeDtypeStruct((8, t, 1), F32)],
        compiler_params=_params(1))(proj, proj, proj)


def _sb_bwd(proj, d_o, tot):
    t = proj.shape[0]
    nb = SB_KT // BLK
    wide = SB_PAIRS * LANES

    def body(q_ref, k_ref, v_ref, do_ref, tot_ref, dq_ref, dk_ref, dv_ref, dk_acc, dv_acc):
        hm = _head_masks()
        causal = _sb_causal()
        pairs = _sb_pairs()
        before = _tri(lambda r, c: r < c)
        upto = _tri(lambda r, c: r <= c)
        dk_acc[...] = jnp.zeros_like(dk_acc)
        dv_acc[...] = jnp.zeros_like(dv_acc)

        def tile(qh, doh, tt, start, n_blk, pre, ecum, dq, own):
            ks = pl.ds(pl.multiple_of(start, BLK), n_blk * BLK)
            k = k_ref[ks, :]
            v = v_ref[ks, :]
            z = jnp.concatenate([_dot_nt(qh[rows], k[:, lanes]) for rows, lanes in pairs], axis=0)
            sp, zs = _softplus(z)
            spm = _sb_mask_last(sp, causal) if own else sp
            pres = []
            for b in range(n_blk):
                blk = spm[:, b * BLK:(b + 1) * BLK]
                pres.append(pre + _scan_dot(blk, before))
                pre = pre + jnp.sum(blk, axis=1, keepdims=True)
            logw = z - (tt - jnp.concatenate(pres, axis=1))
            if own:
                logw = jnp.minimum(logw, 0.0)
            w = jnp.exp(logw)
            if own:
                w = _sb_mask_last(w, causal)
            e = w * jnp.concatenate([_dot_nt(doh[rows], v[:, lanes]) for rows, lanes in pairs], axis=0)
            incs = []
            for b in range(n_blk):
                blk = e[:, b * BLK:(b + 1) * BLK]
                incs.append(ecum + _scan_dot(blk, upto))
                ecum = ecum + jnp.sum(blk, axis=1, keepdims=True)
            dz = e - jnp.exp(zs) * jnp.concatenate(incs, axis=1)
            if own:
                dz = _sb_mask_last(dz, causal)
            dzb = dz.astype(BF16)
            wb = w.astype(BF16)
            for rows, lanes in pairs:
                dk_acc[ks, lanes] += _dot_tn(dzb[rows], qh[rows])
                dv_acc[ks, lanes] += _dot_tn(wb[rows], doh[rows])
            return pre, ecum, dq + jnp.concatenate([_dot(dzb[rows], k[:, lanes]) for rows, lanes in pairs], axis=0)

        def qblock(g, j):
            qs = pl.ds(pl.multiple_of(g * SB_KT + j * BLK, BLK), BLK)
            qh = _sb_stack(q_ref[qs, :] * SCALE, hm)
            doh = _sb_stack(do_ref[qs, :], hm)
            tt = jnp.concatenate([tot_ref[h, qs, :] for h in range(2 * SB_PAIRS)], axis=0)
            c0 = (jnp.zeros((SB_ROWS, 1), F32), jnp.zeros((SB_ROWS, 1), F32), jnp.zeros((SB_ROWS, LANES), F32))
            c = lax.fori_loop(0, g, lambda kt, c: tile(qh, doh, tt, kt * SB_KT, nb, c[0], c[1], c[2], False), c0)
            dq = tile(qh, doh, tt, g * SB_KT, j + 1, c[0], c[1], c[2], True)[2]
            dq_ref[qs, :] = (_sb_unstack(dq, hm) * SCALE).astype(BF16)

        def group(g, _):
            for j in range(nb):
                qblock(g, j)
            return 0

        lax.fori_loop(0, t // SB_KT, group, 0)
        dk_ref[...] = dk_acc[...].astype(BF16)
        dv_ref[...] = dv_acc[...].astype(BF16)

    col_blk = lambda off: pl.BlockSpec((t, wide), lambda g: (0, off + g))
    n_steps = SB_W // wide
    out = jax.ShapeDtypeStruct((t, SB_W), BF16)
    return _call(
        body, name="sb_bwd", grid=(n_steps,),
        in_specs=[col_blk(0), col_blk(n_steps), col_blk(2 * n_steps), col_blk(0),
                  pl.BlockSpec((2 * SB_PAIRS, t, 1), lambda g: (g, 0, 0))],
        out_specs=[col_blk(0), col_blk(0), col_blk(0)], out_shape=[out, out, out],
        scratch_shapes=[pltpu.VMEM((t, wide), F32), pltpu.VMEM((t, wide), F32)],
        compiler_params=_params(1))(proj, proj, proj, d_o, tot)


def _bucket_table():
    a = np.arange(BLK)[:, None]
    c = np.arange(2 * BLK)[None, :]
    dist = np.maximum(BLK + a - c, 0)
    max_exact = N_BUCKETS // 2
    dd = np.maximum(dist, 1).astype(np.float32)
    large = max_exact + (np.log(dd / max_exact) / math.log(MAX_DISTANCE / max_exact)
                         * (N_BUCKETS - max_exact)).astype(np.int32)
    large = np.minimum(large, N_BUCKETS - 1)
    return np.where(dist < max_exact, dist, large).astype(np.int32)


SWA_H = 8


def _swa_band_masks():
    row = lax.broadcasted_iota(jnp.int32, (SWA_H * BLK, 2 * BLK), 0) & (BLK - 1)
    col = lax.broadcasted_iota(jnp.int32, (SWA_H * BLK, 2 * BLK), 1)
    own = lax.broadcasted_iota(jnp.int32, (SWA_H * BLK, BLK), 1) <= (
        lax.broadcasted_iota(jnp.int32, (SWA_H * BLK, BLK), 0) & (BLK - 1))
    return (col > row) & ((col < BLK) | (col - BLK <= row)), own


def _swa_stack(ref, qs, hm, scale):
    parts = []
    for hq in range(SWA_H):
        kvh = hq // SWA_G
        x = ref[qs, (hq // 2) * LANES:(hq // 2 + 1) * LANES].astype(F32)
        if hq % 2 != kvh:
            x = pltpu.roll(x, HEAD_DIM, 1)
        parts.append(jnp.where(hm[kvh], x * scale, 0.0).astype(BF16))
    return jnp.concatenate(parts, axis=0)


def _swa_unstack(x8, hm):
    heads = []
    for hq in range(SWA_H):
        x = x8[hq * BLK:(hq + 1) * BLK]
        heads.append(pltpu.roll(x, HEAD_DIM, 1) if hq % 2 != hq // SWA_G else x)
    return [jnp.where(hm[0], heads[2 * p], heads[2 * p + 1]) for p in range(SWA_H // 2)]


def _swa_scores(q8, kb, bias_ref, mask, cols):
    bias8 = jnp.concatenate([bias_ref[hq, :, cols] for hq in range(SWA_H)], axis=0)
    return jnp.where(mask, _dot_nt(q8, kb) + bias8, NEG_INF)


def _swa_sinks(sink_ref):
    return jnp.concatenate([jnp.broadcast_to(sink_ref[hq:hq + 1, 0:1], (BLK, 1)) for hq in range(SWA_H)], axis=0)


def _swa_fwd(proj, bias, sinks_b):
    t = proj.shape[0]
    nq = t // BLK

    def body(q_ref, k_ref, v_ref, bias_ref, sink_ref, o_ref, lse_ref):
        hm = _head_masks()
        band, own = _swa_band_masks()

        def qblock(i, prev):
            qs = pl.ds(pl.multiple_of(i * BLK, BLK), BLK)
            if prev:
                ks, mask, cols = pl.ds(pl.multiple_of((i - 1) * BLK, BLK), 2 * BLK), band, slice(None)
            else:
                ks, mask, cols = qs, own, slice(BLK, None)
            q8 = _swa_stack(q_ref, qs, hm, SCALE)
            sink8 = _swa_sinks(sink_ref)
            s = _swa_scores(q8, k_ref[ks, :], bias_ref, mask, cols)
            m = jnp.maximum(jnp.max(s, axis=1, keepdims=True), sink8)
            p = jnp.exp(s - m)
            den = jnp.sum(p, axis=1, keepdims=True) + jnp.exp(sink8 - m)
            o8 = _dot((p * (1.0 / den)).astype(BF16), v_ref[ks, :])
            lse8 = m + jnp.log(den)
            for hq in range(SWA_H):
                lse_ref[hq, qs, :] = lse8[hq * BLK:(hq + 1) * BLK]
            for pp, o in enumerate(_swa_unstack(o8, hm)):
                o_ref[qs, pp * LANES:(pp + 1) * LANES] = o

        qblock(0, False)

        def step(i, _):
            qblock(i, True)
            return 0

        lax.fori_loop(1, nq, step, 0)

    return _call(
        body, name="swa_fwd", grid=(1,),
        in_specs=[pl.BlockSpec((t, SWA_W), lambda i: (0, 3)), pl.BlockSpec((t, KV_W), lambda i: (0, 16)),
                  pl.BlockSpec((t, KV_W), lambda i: (0, 17)), pl.BlockSpec((8, BLK, 2 * BLK), lambda i: (0, 0, 0)),
                  pl.BlockSpec((8, LANES), lambda i: (0, 0))],
        out_specs=[pl.BlockSpec((t, SWA_W), lambda i: (0, 0)), pl.BlockSpec((8, t, 1), lambda i: (0, 0, 0))],
        out_shape=[jax.ShapeDtypeStruct((t, SWA_W), F32), jax.ShapeDtypeStruct((8, t, 1), F32)],
        compiler_params=_params(1))(proj, proj, proj, bias, sinks_b)


def _swa_bwd(proj, d_o, lse, bias, sinks_b, dbias_in):
    t = proj.shape[0]
    nq = t // BLK

    def body(q_ref, k_ref, v_ref, do_ref, lse_ref, bias_ref, sink_ref, dbi_ref,
             dq_ref, dk_ref, dv_ref, dsink_ref, dbias_ref, dk_acc, dv_acc):
        hm = _head_masks()
        band, own = _swa_band_masks()
        dk_acc[...] = jnp.zeros_like(dk_acc)
        dv_acc[...] = jnp.zeros_like(dv_acc)
        dbias_ref[...] = dbi_ref[...]

        def qblock(i, prev, dsink8):
            qs = pl.ds(pl.multiple_of(i * BLK, BLK), BLK)
            if prev:
                ks, mask, cols = pl.ds(pl.multiple_of((i - 1) * BLK, BLK), 2 * BLK), band, slice(None)
            else:
                ks, mask, cols = qs, own, slice(BLK, None)
            q8 = _swa_stack(q_ref, qs, hm, SCALE)
            do8 = _swa_stack(do_ref, qs, hm, 1.0)
            sink8 = _swa_sinks(sink_ref)
            lse8 = jnp.concatenate([lse_ref[hq, qs, :] for hq in range(SWA_H)], axis=0)
            kb = k_ref[ks, :]
            p = jnp.exp(_swa_scores(q8, kb, bias_ref, mask, cols) - lse8)
            dp = _dot_nt(do8, v_ref[ks, :])
            delta = jnp.sum(p * dp, axis=1, keepdims=True)
            ds = p * (dp - delta)
            for hq in range(SWA_H):
                dbias_ref[hq, :, cols] += ds[hq * BLK:(hq + 1) * BLK]
            dsb = ds.astype(BF16)
            dk_acc[ks, :] += _dot_tn(dsb, q8)
            dv_acc[ks, :] += _dot_tn(p.astype(BF16), do8)
            for pp, dq in enumerate(_swa_unstack(_dot(dsb, kb) * SCALE, hm)):
                dq_ref[qs, pp * LANES:(pp + 1) * LANES] = dq.astype(BF16)
            return dsink8 - jnp.exp(sink8 - lse8) * delta

        ds0 = qblock(0, False, jnp.zeros((SWA_H * BLK, 1), F32))
        ds8 = lax.fori_loop(1, nq, lambda i, c: qblock(i, True, c), ds0)
        for hq in range(SWA_H):
            dsink_ref[hq:hq + 1, :] = jnp.broadcast_to(
                jnp.sum(ds8[hq * BLK:(hq + 1) * BLK], axis=0, keepdims=True), (1, LANES))

        dk_ref[...] = dk_acc[...].astype(BF16)
        dv_ref[...] = dv_acc[...].astype(BF16)

    full3 = pl.BlockSpec((8, BLK, 2 * BLK), lambda i: (0, 0, 0))
    kv = jax.ShapeDtypeStruct((t, KV_W), BF16)
    return _call(
        body, name="swa_bwd", grid=(1,),
        in_specs=[pl.BlockSpec((t, SWA_W), lambda i: (0, 3)), pl.BlockSpec((t, KV_W), lambda i: (0, 16)),
                  pl.BlockSpec((t, KV_W), lambda i: (0, 17)), pl.BlockSpec((t, SWA_W), lambda i: (0, 1)),
                  pl.BlockSpec((8, t, 1), lambda i: (0, 0, 0)), full3, pl.BlockSpec((8, LANES), lambda i: (0, 0)),
                  full3],
        out_specs=[pl.BlockSpec((t, SWA_W), lambda i: (0, 0)), pl.BlockSpec((t, KV_W), lambda i: (0, 0)),
                   pl.BlockSpec((t, KV_W), lambda i: (0, 0)), pl.BlockSpec((8, LANES), lambda i: (0, 0)), full3],
        out_shape=[jax.ShapeDtypeStruct((t, SWA_W), BF16), kv, kv, jax.ShapeDtypeStruct((8, LANES), F32),
                   jax.ShapeDtypeStruct((8, BLK, 2 * BLK), F32)],
        scratch_shapes=[pltpu.VMEM((t, KV_W), F32), pltpu.VMEM((t, KV_W), F32)],
        compiler_params=_params(1))(proj, proj, proj, d_o, lse, bias, sinks_b, dbias_in)


def _concat_cols(parts):
    t = parts[0].shape[0]
    widths = [a.shape[1] for a in parts]

    def body(*refs):
        refs[-1][...] = jnp.concatenate([r[...] for r in refs[:-1]], axis=1)

    return _call(
        body, name="concat_cols", grid=(t // TM,),
        in_specs=[pl.BlockSpec((TM, w), lambda i: (i, 0)) for w in widths],
        out_specs=pl.BlockSpec((TM, sum(widths)), lambda i: (i, 0)),
        out_shape=jax.ShapeDtypeStruct((t, sum(widths)), parts[0].dtype), compiler_params=_params(1))(*parts)


def _bias_table(rel_bias, buckets):
    def body(rb_ref, b_ref, o_ref):
        bk = b_ref[...]
        for h in range(8):
            acc = jnp.zeros((BLK, 2 * BLK), F32)
            for b in range(N_BUCKETS):
                acc = jnp.where(bk == b, rb_ref[b, h], acc)
            o_ref[h] = acc

    return _call(
        body, name="bias_table", grid=(1,),
        in_specs=[pl.BlockSpec(memory_space=pltpu.SMEM), pl.BlockSpec((BLK, 2 * BLK), lambda i: (0, 0))],
        out_specs=pl.BlockSpec((8, BLK, 2 * BLK), lambda i: (0, 0, 0)),
        out_shape=jax.ShapeDtypeStruct((8, BLK, 2 * BLK), F32), compiler_params=_params(1))(rel_bias, buckets)


def _bias_grad(dbias, buckets):
    def body(d_ref, b_ref, o_ref):
        lane = lax.broadcasted_iota(jnp.int32, (1, LANES), 1)
        bk = b_ref[...]
        for h in range(8):
            d = d_ref[h]
            acc = jnp.zeros((1, LANES), F32)
            for b in range(N_BUCKETS):
                s = jnp.sum(jnp.sum(jnp.where(bk == b, d, 0.0), axis=0, keepdims=True), axis=1, keepdims=True)
                acc = acc + jnp.where(lane == b, s, 0.0)
            o_ref[h:h + 1, :] = acc

    return _call(
        body, name="bias_grad", grid=(1,),
        in_specs=[pl.BlockSpec((8, BLK, 2 * BLK), lambda i: (0, 0, 0)), pl.BlockSpec((BLK, 2 * BLK), lambda i: (0, 0))],
        out_specs=pl.BlockSpec((8, LANES), lambda i: (0, 0)),
        out_shape=jax.ShapeDtypeStruct((8, LANES), F32), compiler_params=_params(1))(dbias, buckets)


def _row(a):
    return a.reshape(1, -1)


def _fwd_ffn1_gu(h, n1, w):
    s = {"h0": h, "n1": n1}
    s["gu1"], s["act1"] = _ffn_gu(n1, w["ffn1_gu"])
    return s


def _fwd_ffn1_down(s, w, small, l):
    s["h1"], s["nm"] = _down_res(s["act1"], w["ffn1_down"], s["h0"], _row(small["norm_mix"][l]))


def _fwd_ffn1(h, n1, w, small, l):
    s = _fwd_ffn1_gu(h, n1, w)
    _fwd_ffn1_down(s, w, small, l)
    return s


def _fwd_proj_sb(s, w):
    s["proj"] = _proj(s["nm"], w["w_in"])
    s["o_sb"], s["tot"] = _sb_fwd(s["proj"])


def _fwd_swa(s, small, l, bias):
    s["sinks_b"] = jnp.broadcast_to(small["sinks"][l][:, None], (8, LANES))
    s["o_sw"], s["lse"] = _swa_fwd(s["proj"], bias, s["sinks_b"])


def _fwd_out_gu2(s, w, small, l):
    s["h2"], s["mixed"], s["n2"] = _out_res(
        s["o_sb"], s["o_sw"], _row(small["norm_out_sb"][l]), _row(small["norm_out_swa"][l]), w["w_out"], s["h1"],
        _row(small["norm_ffn2"][l]))
    s["gu2"], s["act2"] = _ffn_gu(s["n2"], w["ffn2_gu"])


def _fwd_ffn2_down(s, w, g_after):
    return _down_res(s["act2"], w["ffn2_down"], s["h2"], g_after)


def _fwd_out_ffn2(s, w, small, l, g_after):
    _fwd_out_gu2(s, w, small, l)
    return _fwd_ffn2_down(s, w, g_after)


def _bwd_ffn_dact(dh, s, w, which):
    return _ffn_dact(dh[1], w[f"ffn{which}_down"], s[f"gu{which}"])


def _bwd_ffn_rest(dh, dgu, s, w, small, l, which):
    h_in, norm = (s["h0"], "norm_ffn1") if which == 1 else (s["h2"], "norm_ffn2")
    g_down = _wgrad_down(s[f"act{which}"], dh[1])
    g_gu = _wgrad_gu(s[f"n{which}"], dgu)
    dh32, dh16, dg = _ffn_dn(dgu, w[f"ffn{which}_gu"], dh[0], h_in, _row(small[norm][l]))
    return (dh32, dh16), {f"ffn{which}_down": g_down, f"ffn{which}_gu": g_gu}, {norm: dg}


def _bwd_ffn(dh, s, w, small, l, which):
    return _bwd_ffn_rest(dh, _bwd_ffn_dact(dh, s, w, which), s, w, small, l, which)


def _bwd_mix(dh, s, w, small, l, bias, dbias):
    g_out = _wgrad_out(s["mixed"], dh[1])
    d_o, dg_sb, dg_sw = _dmixed(dh[1], w["w_out"], s["o_sb"], s["o_sw"], _row(small["norm_out_sb"][l]),
                                _row(small["norm_out_swa"][l]))
    dq_sb, dk_sb, dv_sb = _sb_bwd(s["proj"], d_o, s["tot"])
    dq_sw, dk_sw, dv_sw, dsink, dbias = _swa_bwd(s["proj"], d_o, s["lse"], bias, s["sinks_b"], dbias)
    dproj = _concat_cols([dq_sb, dk_sb, dv_sb, dq_sw, dk_sw, dv_sw])
    g_in = _wgrad_in(s["nm"], dproj)
    dh32, dh16, dg_mix = _mix_dn(dproj, w["w_in"], dh[0], s["h1"], _row(small["norm_mix"][l]))
    gs = {"norm_out_sb": dg_sb, "norm_out_swa": dg_sw, "sinks": dsink[:, 0], "norm_mix": dg_mix}
    return (dh32, dh16), {"w_out": g_out, "w_in": g_in}, gs, dbias


def _place():
    x, y, c = lax.axis_index("x"), lax.axis_index("y"), lax.axis_index("c")
    return x, y, c, 2 * x + y


def _chip_core(k, c):
    return (k // 2, k % 2, c)


def _rows_per_block(rows, cols, copies):
    best = 16
    for tr in range(16, rows + 1, 16):
        if rows % tr == 0 and copies * tr * cols * 4 <= SLAB_BLOCK_BYTES:
            best = tr
    assert rows % best == 0
    return best


def _place_own(w, l, me1):
    _, rows, cols = w.shape
    tr = _rows_per_block(rows // 2, cols, 1)
    per_half = rows // 2 // tr

    def body(me_ref, w_ref, o_ref):
        o_ref[...] = w_ref[...].astype(BF16)

    return _call(
        body, name="place_own",
        num_scalar_prefetch=1, grid=(rows // tr,),
        in_specs=[pl.BlockSpec((None, tr, cols), lambda r, me: (l, r, 0))],
        out_specs=pl.BlockSpec((None, None, tr, cols), lambda r, me: (me[0], r // per_half, r % per_half, 0)),
        out_shape=jax.ShapeDtypeStruct((N_CHIPS, 2, rows // 2, cols), BF16), compiler_params=_params(1))(me1, w)


def _plan_gather_ici(bufs):
    _, _, c, me = _place()
    return [(b.at[me, c], b.at[me, c], b.at[(me + 3 - j) % N_CHIPS, c], _chip_core((me + 1 + j) % N_CHIPS, c))
            for b in bufs for j in range(3)]


def _plan_gather_d2d(bufs):
    x, y, c, me = _place()
    return [(b.at[(me + 3 - j) % N_CHIPS, c], b.at[(me + 3 - j) % N_CHIPS, c], b.at[(me + 3 - j) % N_CHIPS, 1 - c],
             (x, y, 1 - c)) for b in bufs for j in range(3)]


def _plan_grad_sibling(bufs):
    x, y, c, _ = _place()
    n = len(bufs) // 2
    return [(g.at[:, 1 - c], z, z, (x, y, 1 - c)) for g, z in zip(bufs[:n], bufs[n:])]


def _plan_grad_chips(bufs):
    _, _, c, me = _place()
    n = len(bufs) // 2
    return [(p.at[j], z.at[j], z.at[j], _chip_core((me + 1 + j) % N_CHIPS, c))
            for p, z in zip(bufs[:n], bufs[n:]) for j in range(3)]


def _plan_grad_halves(l):
    def plan(bufs):
        x, y, c, _ = _place()
        return [(b.at[l, c], b.at[l, c], b.at[l, 1 - c], (x, y, 1 - c)) for b in bufs]
    return plan


def _remote(src, dst, send_sem, recv_sem, to):
    return pltpu.make_async_remote_copy(src_ref=src, dst_ref=dst, send_sem=send_sem, recv_sem=recv_sem,
                                        device_id=to, device_id_type=MESH)


SIBLING_BARRIER_ID = 0


def _sibling_handshake():
    x, y, c, _ = _place()
    barrier = pltpu.get_barrier_semaphore()
    pl.semaphore_signal(barrier, inc=1, device_id=(x, y, 1 - c), device_id_type=MESH)
    pl.semaphore_wait(barrier, 1)


def _split_params(sibling_only):
    return pltpu.CompilerParams(has_side_effects=EFFECT, collective_id=SIBLING_BARRIER_ID if sibling_only else None)


def _exchange_start_groups(name, plan, groups, sibling_only=False):
    sizes = [len(g) for g, _ in groups]
    bufs = [a for g, _ in groups for a in g]
    n, n_groups = len(bufs), len(groups)

    def body(*refs):
        if sibling_only:
            _sibling_handshake()
        ins, sems, token = refs[:n], refs[n:n + 2 * n_groups], refs[-1]
        at = 0
        for k, size in enumerate(sizes):
            for i, (src, dst, _, to) in enumerate(plan(ins[at:at + size])):
                _remote(src, dst, sems[2 * k].at[i], sems[2 * k + 1].at[i], to).start()
            at += size
        token[...] = jnp.zeros_like(token)

    sem_shapes = [pltpu.SemaphoreType.DMA((n_copies,)) for _, n_copies in groups for _ in range(2)]
    out = _call(
        body, name=name,
        out_shape=(*sem_shapes, *[pltpu.HBM(a.shape, a.dtype) for a in bufs], jax.ShapeDtypeStruct((8, LANES), F32)),
        in_specs=[HBM] * n,
        out_specs=(*[SEM] * (2 * n_groups), *[HBM] * n, pl.BlockSpec(memory_space=pltpu.VMEM)),
        input_output_aliases={t: 2 * n_groups + t for t in range(n)}, hbm_args=n,
        compiler_params=_split_params(sibling_only),
    )(*bufs)
    flights, at = [], 2 * n_groups
    for k, size in enumerate(sizes):
        flights.append(((out[2 * k], out[2 * k + 1]), list(out[at:at + size])))
        at += size
    return flights


def _exchange_start(name, plan, bufs, n_copies, sibling_only=False):
    return _exchange_start_groups(name, plan, [(bufs, n_copies)], sibling_only)[0]


def _exchange_wait(name, plan, bufs, sems, step_output=False):
    n = len(bufs)
    shape = jax.ShapeDtypeStruct if step_output else pltpu.HBM

    def body(*refs):
        ins = refs[:n]
        ssem, rsem = refs[n], refs[n + 1]
        for i, (src, dst, land, to) in enumerate(plan(ins)):
            _remote(src, dst, ssem.at[i], rsem.at[i], to).wait_send()
            _remote(land, land, ssem.at[i], rsem.at[i], to).wait_recv()

    return list(_call(
        body, name=name, out_shape=[shape(a.shape, a.dtype) for a in bufs],
        in_specs=[HBM] * n + [SEM, SEM], out_specs=[HBM] * n,
        input_output_aliases={t: t for t in range(n)},
        compiler_params=pltpu.CompilerParams(has_side_effects=EFFECT),
    )(*bufs, sems[0], sems[1]))


def _exchange_pass(name, done, plan, bufs, sems, n_copies):
    n = len(bufs)

    def body(*refs):
        _sibling_handshake()
        ins = refs[:n]
        old_s, old_r, ssem, rsem = refs[n], refs[n + 1], refs[n + 2], refs[n + 3]
        token = refs[-1]
        for i, (src, dst, land, to) in enumerate(done(ins)):
            _remote(src, dst, old_s.at[i], old_r.at[i], to).wait_send()
            _remote(land, land, old_s.at[i], old_r.at[i], to).wait_recv()
        for i, (src, dst, _, to) in enumerate(plan(ins)):
            _remote(src, dst, ssem.at[i], rsem.at[i], to).start()
        token[...] = jnp.zeros_like(token)

    out = _call(
        body, name=name,
        out_shape=(pltpu.SemaphoreType.DMA((n_copies,)), pltpu.SemaphoreType.DMA((n_copies,)),
                   *[pltpu.HBM(a.shape, a.dtype) for a in bufs], jax.ShapeDtypeStruct((8, LANES), F32)),
        in_specs=[HBM] * n + [SEM, SEM], out_specs=(SEM, SEM, *[HBM] * n, pl.BlockSpec(memory_space=pltpu.VMEM)),
        input_output_aliases={t: 2 + t for t in range(n)},
        compiler_params=_split_params(True),
    )(*bufs, sems[0], sems[1])
    return (out[0], out[1]), list(out[2:2 + n])


def _chip_sum(g, xbuf, cm):
    _, _, r2, cols = g.shape
    tr = _rows_per_block(r2, cols, 1)

    def body(cm_ref, g_ref, x_ref, o_ref):
        o_ref[...] = (g_ref[...] + x_ref[...]).astype(BF16)

    return _call(
        body, name="grad_chip_sum",
        num_scalar_prefetch=1, grid=(3, r2 // tr),
        in_specs=[pl.BlockSpec((None, None, tr, cols), lambda j, r, cm: ((cm[1] + 1 + j) % N_CHIPS, cm[0], r, 0)),
                  pl.BlockSpec((None, tr, cols), lambda j, r, cm: ((cm[1] + 1 + j) % N_CHIPS, r, 0))],
        out_specs=pl.BlockSpec((None, tr, cols), lambda j, r, cm: (j, r, 0)),
        out_shape=jax.ShapeDtypeStruct((3, r2, cols), BF16), compiler_params=_params(2))(cm, g, xbuf)


def _total_sum(g, xbuf, rbuf, cm, l, prev):
    _, _, r2, cols = g.shape
    tr = _rows_per_block(r2, cols, 3)

    def body(cm_ref, g_ref, x_ref, r_ref, *rest):
        acc = g_ref[...] + x_ref[...]
        for j in range(3):
            acc = acc + r_ref[j].astype(F32)
        rest[-1][...] = acc

    return _call(
        body, name="grad_total_sum",
        num_scalar_prefetch=1, grid=(r2 // tr,),
        in_specs=[pl.BlockSpec((None, None, tr, cols), lambda r, cm: (cm[1], cm[0], r, 0)),
                  pl.BlockSpec((None, tr, cols), lambda r, cm: (cm[1], r, 0)),
                  pl.BlockSpec((3, tr, cols), lambda r, cm: (0, r, 0))] + ([] if prev is None else [ANY]),
        out_specs=pl.BlockSpec((None, None, tr, cols), lambda r, cm: (l, cm[0], r, 0)),
        out_shape=jax.ShapeDtypeStruct((DEPTH, 2, r2, cols), F32),
        input_output_aliases={} if prev is None else {4: 0},
        compiler_params=_params(1))(cm, g, xbuf, rbuf, *([] if prev is None else [prev]))


def _small_allreduce(v):
    rows = v.shape[0]
    n_dev = 2 * N_CHIPS

    def body(v_ref, o_ref, buf, ssem, rsem):
        x, y, c, _ = _place()
        me = 4 * x + 2 * y + c
        buf[me] = v_ref[...]

        def copy(d, slot, to):
            return _remote(v_ref, buf.at[slot], ssem.at[d - 1], rsem.at[d - 1], (to // 4, (to // 2) % 2, to % 2))

        cps = [copy(d, me, (me + d) % n_dev) for d in range(1, n_dev)]
        for cp in cps:
            cp.start()
        for d in range(1, n_dev):
            copy(d, (me + n_dev - d) % n_dev, me).wait_recv()
        for cp in cps:
            cp.wait_send()
        acc = buf[0]
        for i in range(1, n_dev):
            acc = acc + buf[i]
        o_ref[...] = acc

    vm = pl.BlockSpec(memory_space=pltpu.VMEM)
    return _call(
        body, name="small_allreduce", in_specs=[vm], out_specs=vm,
        out_shape=jax.ShapeDtypeStruct(v.shape, F32),
        scratch_shapes=[pltpu.VMEM((n_dev, rows, LANES), F32), pltpu.SemaphoreType.DMA((n_dev - 1,)),
                        pltpu.SemaphoreType.DMA((n_dev - 1,))],
        compiler_params=pltpu.CompilerParams(vmem_limit_bytes=V7X_VMEM_LIMIT))(v)


def _adamw_math(w, g, m, v):
    m2 = ADAM_B1 * m + (1.0 - ADAM_B1) * g
    v2 = ADAM_B2 * v + (1.0 - ADAM_B2) * (g * g)
    v_hat = v2 / (1.0 - ADAM_B2 ** ADAM_STEP)
    step = (-ADAM_LR / (1.0 - ADAM_B1 ** ADAM_STEP)) * m2 / (jnp.sqrt(v_hat) + ADAM_EPS)
    return step + (-ADAM_LR * ADAM_WD) * w, m2, v2


def _adamw_layer(w, g, m, v, l, prev):
    _, rows, cols = w.shape
    tr = rows
    for cand in range(8, rows + 1, 8):
        if rows % cand == 0 and cand * cols * 4 <= ADAMW_BLOCK_BYTES:
            tr = cand

    def body(w_ref, g_ref, m_ref, v_ref, *outs):
        d_ref, m2_ref, v2_ref = outs[-3:]
        d_ref[...], m2_ref[...], v2_ref[...] = _adamw_math(w_ref[...], g_ref[...], m_ref[...], v_ref[...])

    stack = pl.BlockSpec((None, tr, cols), lambda i: (l, i, 0))
    ins, specs, alias = [w, g, m, v], [stack] * 4, {}
    if prev is not None:
        ins += list(prev)
        specs += [ANY] * 3
        alias = {4 + i: i for i in range(3)}
    return _call(
        body, name="adamw", grid=(rows // tr,), in_specs=specs, out_specs=[stack] * 3,
        out_shape=[jax.ShapeDtypeStruct(w.shape, F32)] * 3, input_output_aliases=alias,
        compiler_params=_params(1))(*ins)


def _adamw_small(ws, gs, ms, vs):
    n = len(ws)
    row = lambda a: a.reshape(1, -1) if a.ndim == 1 else a
    ins = [row(a) for group in (ws, gs, ms, vs) for a in group]

    def body(*refs):
        w, g, m, v = (refs[i * n:(i + 1) * n] for i in range(4))
        outs = refs[4 * n:]
        for i in range(n):
            d, m2, v2 = _adamw_math(w[i][...], g[i][...], m[i][...], v[i][...])
            outs[i][...], outs[n + i][...], outs[2 * n + i][...] = d, m2, v2

    vm = pl.BlockSpec(memory_space=pltpu.VMEM)
    outs = _call(
        body, name="adamw_small", in_specs=[vm] * (4 * n), out_specs=[vm] * (3 * n),
        out_shape=[jax.ShapeDtypeStruct(a.shape, F32) for a in ins[:n]] * 3,
        compiler_params=pltpu.CompilerParams(vmem_limit_bytes=V7X_VMEM_LIMIT))(*ins)
    outs = [o.reshape(w.shape) for o, w in zip(outs, list(ws) * 3)]
    return outs[:n], outs[n:2 * n], outs[2 * n:]


SMALL = ("norm_ffn1", "norm_mix", "sinks", "norm_out_sb", "norm_out_swa", "norm_ffn2", "rel_bias", "norm_final")
BIG = ("ffn1_gu", "ffn1_down", "w_in", "w_out", "ffn2_gu", "ffn2_down")


def _pack(parts):
    flat, n = [], 0
    for a in parts:
        a = a.reshape(-1).astype(F32)
        gap = -a.shape[0] % LANES
        flat += [a] + ([jnp.zeros((gap,), F32)] if gap else [])
        n += a.shape[0] + gap
    tail = -(n // LANES) % 8 * LANES
    return jnp.concatenate(flat + ([jnp.zeros((tail,), F32)] if tail else [])).reshape(-1, LANES)


def _unpack(packed, like):
    out, r = [], 0
    for a in like:
        n = math.prod(a.shape)
        nr = -(-n // LANES)
        out.append(packed[r:r + nr].reshape(-1)[:n].reshape(a.shape))
        r += nr
    return out


def _halved(a):
    k, r, cols = a.shape
    return a.reshape(k, 2, r // 2, cols)


def _weight_view(k, buf):
    full = buf.reshape(N_CHIPS, buf.shape[2] * 2, buf.shape[3])
    return full if k.endswith("_gu") else full.reshape(-1, D_MODEL)


def _grad_stack(k, g):
    if not k.endswith("_gu"):
        g = g.reshape(N_CHIPS, g.shape[0] // N_CHIPS, D_MODEL)
    return _halved(g)


def _empty_like_hbm(shape, dtype):
    return pltpu.with_memory_space_constraint(lax.empty(shape, dtype), pltpu.HBM)


def kernel(x, norm_ffn1, w_ffn1_gu, w_ffn1_down, norm_mix, w_in, sinks, norm_out_sb, norm_out_swa, w_out, norm_ffn2, w_ffn2_gu, w_ffn2_down, rel_bias, norm_final, loss_target, m_norm_ffn1, m_w_ffn1_gu, m_w_ffn1_down, m_norm_mix, m_w_in, m_sinks, m_norm_out_sb, m_norm_out_swa, m_w_out, m_norm_ffn2, m_w_ffn2_gu, m_w_ffn2_down, m_rel_bias, m_norm_final, v_norm_ffn1, v_w_ffn1_gu, v_w_ffn1_down, v_norm_mix, v_w_in, v_sinks, v_norm_out_sb, v_norm_out_swa, v_w_out, v_norm_ffn2, v_w_ffn2_gu, v_w_ffn2_down, v_rel_bias, v_norm_final):
    big_w = dict(ffn1_gu=w_ffn1_gu, ffn1_down=w_ffn1_down, w_in=w_in, w_out=w_out, ffn2_gu=w_ffn2_gu, ffn2_down=w_ffn2_down)
    big_m = dict(ffn1_gu=m_w_ffn1_gu, ffn1_down=m_w_ffn1_down, w_in=m_w_in, w_out=m_w_out, ffn2_gu=m_w_ffn2_gu, ffn2_down=m_w_ffn2_down)
    big_v = dict(ffn1_gu=v_w_ffn1_gu, ffn1_down=v_w_ffn1_down, w_in=v_w_in, w_out=v_w_out, ffn2_gu=v_w_ffn2_gu, ffn2_down=v_w_ffn2_down)
    small = dict(norm_ffn1=norm_ffn1, norm_mix=norm_mix, sinks=sinks, norm_out_sb=norm_out_sb, norm_out_swa=norm_out_swa,
                 norm_ffn2=norm_ffn2, rel_bias=rel_bias, norm_final=norm_final)
    small_m = dict(norm_ffn1=m_norm_ffn1, norm_mix=m_norm_mix, sinks=m_sinks, norm_out_sb=m_norm_out_sb,
                   norm_out_swa=m_norm_out_swa, norm_ffn2=m_norm_ffn2, rel_bias=m_rel_bias, norm_final=m_norm_final)
    small_v = dict(norm_ffn1=v_norm_ffn1, norm_mix=v_norm_mix, sinks=v_sinks, norm_out_sb=v_norm_out_sb,
                   norm_out_swa=v_norm_out_swa, norm_ffn2=v_norm_ffn2, rel_bias=v_rel_bias, norm_final=v_norm_final)
    for dct in (big_w, big_m, big_v):
        dct["w_in"] = jnp.swapaxes(dct["w_in"], 1, 2)
    _PREVIOUS[0] = None
    _, _, c, me = _place()
    cm = jnp.stack([c, me]).astype(jnp.int32)
    buckets = jnp.asarray(_bucket_table())
    ffn1, mix_in, rest = ("ffn1_gu", "ffn1_down"), ("w_in",), ("w_out", "ffn2_gu", "ffn2_down")

    def place(l, keys):
        return [_place_own(big_w[k], l, cm[1:]) for k in keys]

    def views(keys, bufs):
        return {k: _weight_view(k, b) for k, b in zip(keys, bufs)}

    def gather_start(tag, bufs):
        return _exchange_start(f"gather{tag}_ici_start", _plan_gather_ici, bufs, 3 * len(bufs))

    def gather_pass(tag, flight):
        return _exchange_pass(f"gather{tag}_pass", _plan_gather_ici, _plan_gather_d2d, flight[1], flight[0],
                              3 * len(flight[1]))

    def gather_done(tag, keys, flight):
        return views(keys, _exchange_wait(f"gather{tag}_d2d_wait", _plan_gather_d2d, flight[1], flight[0]))

    fly_gu0 = gather_start("0a", place(0, ffn1[:1]))
    fly_down0 = gather_start("0a2", place(0, ffn1[1:]))
    fly_in0 = gather_start("0b", place(0, mix_in))
    later = [place(l, keys) for l in range(DEPTH) for keys in ((rest,) if l == 0 else (ffn1, mix_in, rest))]
    fly_rest0, fly_ffn1, fly_in1, fly_rest1 = _exchange_start_groups(
        "gather_later_ici_start", _plan_gather_ici, [(bufs, 3 * len(bufs)) for bufs in later])
    bias = _bias_table(rel_bias, buckets)
    n1 = _norm_cast(x[0], _row(norm_ffn1[0]))
    w0 = gather_done("0a", ffn1[:1], gather_pass("0a", fly_gu0))

    s0 = _fwd_ffn1_gu(x[0], n1, w0)
    w0.update(gather_done("0a2", ffn1[1:], gather_pass("0a2", fly_down0)))
    fly_in0 = gather_pass("0b", fly_in0)
    _fwd_ffn1_down(s0, w0, small, 0)
    w0.update(gather_done("0b", mix_in, fly_in0))
    _fwd_proj_sb(s0, w0)
    fly_rest0 = gather_pass("0c", fly_rest0)
    _fwd_swa(s0, small, 0, bias)
    w0.update(gather_done("0c", rest, fly_rest0))
    _fwd_out_gu2(s0, w0, small, 0)
    fly_ffn1 = gather_pass("1a", fly_ffn1)
    h, n1 = _fwd_ffn2_down(s0, w0, _row(norm_ffn1[1]))
    w1 = gather_done("1a", ffn1, fly_ffn1)
    fly_in1 = gather_pass("1b", fly_in1)
    s1 = _fwd_ffn1(h, n1, w1, small, 1)
    w1.update(gather_done("1b", mix_in, fly_in1))
    _fwd_proj_sb(s1, w1)
    fly_rest1 = gather_pass("1c", fly_rest1)
    _fwd_swa(s1, small, 1, bias)
    w1.update(gather_done("1c", rest, fly_rest1))
    h, _ = _fwd_out_ffn2(s1, w1, small, 1, _row(norm_final))
    dh32, dh16, dg_final, loss_row = _loss_head(h, _row(norm_final), loss_target[0])
    dh = (dh32, dh16)

    def landing(stacks, lead, dtype):
        return [_empty_like_hbm((lead,) + a.shape[2:], dtype) for a in stacks]

    def reduce_begin(tag, keys, gw):
        stacks = [_grad_stack(k, gw[k]) for k in keys]
        flight = _exchange_start(f"grad{tag}_sibling_start", _plan_grad_sibling,
                                 stacks + landing(stacks, N_CHIPS, F32), len(keys), sibling_only=True)
        return dict(tag=tag, keys=keys, stacks=stacks, flight=flight)

    def reduce_chips(st):
        n, (sems, bufs) = len(st["keys"]), st["flight"]
        bufs = _exchange_wait(f"grad{st['tag']}_sibling_wait", _plan_grad_sibling, bufs, sems)
        st["own"] = list(zip(bufs[:n], bufs[n:]))
        st["flight"] = _exchange_start(f"grad{st['tag']}_chips_start", _plan_grad_chips,
                                       [_chip_sum(g, z, cm) for g, z in st["own"]] + landing(st["stacks"], 3, BF16),
                                       3 * n)

    def reduce_halves(st, l, prev):
        n, (sems, bufs) = len(st["keys"]), st["flight"]
        bufs = _exchange_wait(f"grad{st['tag']}_chips_wait", _plan_grad_chips, bufs, sems)
        halves = [_total_sum(g, x, z, cm, l, None if prev is None else prev[k].reshape(DEPTH, 2, *g.shape[2:]))
                  for k, (g, x), z in zip(st["keys"], st["own"], bufs[n:])]
        st["plan"] = _plan_grad_halves(l)
        st["flight"] = _exchange_start(f"grad{st['tag']}_halves_start", st["plan"], halves, n, sibling_only=True)

    def reduce_end(st):
        sems, bufs = st["flight"]
        bufs = _exchange_wait(f"grad{st['tag']}_halves_wait", st["plan"], bufs, sems, step_output=st["tag"] != "1")
        return {k: b.reshape(big_w[k].shape) for k, b in zip(st["keys"], bufs)}

    def adamw(reduced, keys, l, prev):
        return {k: _adamw_layer(big_w[k], reduced[k], big_m[k], big_v[k], l, None if prev is None else prev[k])
                for k in keys}

    gsm = [dict() for _ in range(DEPTH)]
    dbias = jnp.zeros((8, BLK, 2 * BLK), F32)
    dh, gw1, gs = _bwd_ffn(dh, s1, w1, small, 1, 2)
    gsm[1].update(gs)
    dh, gw, gs, dbias = _bwd_mix(dh, s1, w1, small, 1, bias, dbias)
    gw1.update(gw)
    gsm[1].update(gs)
    dh, gw, gs = _bwd_ffn(dh, s1, w1, small, 1, 1)
    gw1.update(gw)
    gsm[1].update(gs)

    red1 = reduce_begin("1", BIG, gw1)
    dh, gw0, gs = _bwd_ffn(dh, s0, w0, small, 0, 2)
    gsm[0].update(gs)
    reduce_chips(red1)
    dh, gw, gs, dbias = _bwd_mix(dh, s0, w0, small, 0, bias, dbias)
    gw0.update(gw)
    gsm[0].update(gs)
    red0a = reduce_begin("0a", ("ffn2_gu", "ffn2_down", "w_out", "w_in"), gw0)
    reduce_halves(red1, 1, None)
    dgu = _bwd_ffn_dact(dh, s0, w0, 1)
    reduce_chips(red0a)
    dh, gw, gs = _bwd_ffn_rest(dh, dgu, s0, w0, small, 0, 1)
    gsm[0].update(gs)
    red0b = reduce_begin("0b", ffn1, gw)
    reduced1 = reduce_end(red1)
    ffn2 = ("ffn2_gu", "ffn2_down")
    stacks = adamw(reduced1, ffn2, 1, None)

    gsmall = {k: jnp.stack([gsm[l][k].reshape(-1) for l in range(DEPTH)]) for k in gsm[0]}
    gsmall["rel_bias"] = jnp.transpose(_bias_grad(dbias, buckets)[:, :N_BUCKETS])
    gsmall["norm_final"] = dg_final.reshape(-1)
    small_like = [small[k] for k in SMALL]
    red = _small_allreduce(_pack([gsmall[k] for k in SMALL] + [loss_row[0, :1]]))
    gs = _unpack(red, small_like + [loss_row[0, :1]])
    loss = gs[-1][0]
    gs = dict(zip(SMALL, gs[:-1]))

    reduce_chips(red0b)
    stacks.update(adamw(reduced1, ("w_in", "w_out"), 1, None))
    reduce_halves(red0a, 0, reduced1)
    stacks.update(adamw(reduced1, ffn1, 1, None))
    dlt, m2, v2 = _adamw_small(*[[dct[k] for k in SMALL] for dct in (small, gs, small_m, small_v)])
    reduced0 = reduce_end(red0a)
    stacks.update(adamw(reduced0, ffn2, 0, stacks))
    reduce_halves(red0b, 0, reduced1)
    stacks.update(adamw(reduced0, ("w_in", "w_out"), 0, stacks))
    reduced0.update(reduce_end(red0b))
    stacks.update(adamw(reduced0, ffn1, 0, stacks))

    out_g, out_d, out_m, out_v = {}, {}, {}, {}
    for k in BIG:
        out_g[k], out_d[k], out_m[k], out_v[k] = [jnp.swapaxes(a, 1, 2) if k == "w_in" else a
                                                  for a in (reduced0[k], *stacks[k])]
    for dst, parts in ((out_d, dlt), (out_m, m2), (out_v, v2)):
        dst.update(zip(SMALL, parts))
    out_g.update(gs)

    order = ("norm_ffn1", "ffn1_gu", "ffn1_down", "norm_mix", "w_in", "sinks", "norm_out_sb", "norm_out_swa", "w_out",
             "norm_ffn2", "ffn2_gu", "ffn2_down", "rel_bias", "norm_final")
    return (loss, dh[0].reshape(x.shape), *[out_g[k] for k in order], *[out_d[k] for k in order],
            *[out_m[k] for k in order], *[out_v[k] for k in order])
```

```python
import math

import numpy as np
import jax
import jax.numpy as jnp
from jax import lax
from jax.experimental import pallas as pl
from jax.experimental.pallas import tpu as pltpu

F32 = jnp.float32
BF16 = jnp.bfloat16

D_MODEL = 1024
DEPTH = 2
HEAD_DIM = 64
BLK = 128
N_BUCKETS = 32
MAX_DISTANCE = 128
D_FF = 2816
EPS = 1e-6
NEG_INF = -1e30
SB_W = 512
SWA_W = 512
KV_W = 128
IN_W = 2304
SCALE = HEAD_DIM ** -0.5
N_CHIPS = 4
FS = 2 * D_FF // N_CHIPS
LANES = 128
V7X_VMEM_LIMIT = 56 * 2 ** 20
TM = 512
SLAB_BLOCK_BYTES = 6 * 2 ** 20
ADAMW_BLOCK_BYTES = 2 ** 21
SB_KT = 512
SWA_G = 4

ADAM_LR = 0.001
ADAM_B1 = 0.9
ADAM_B2 = 0.999
ADAM_EPS = 1e-08
ADAM_WD = 0.01
ADAM_STEP = 10

MESH = pl.DeviceIdType.MESH
ANY = pl.BlockSpec(memory_space=pl.ANY)
HBM = pl.BlockSpec(memory_space=pltpu.HBM)
SEM = pl.BlockSpec(memory_space=pltpu.SEMAPHORE)
EFFECT = pltpu.SideEffectType.DATAFLOW_SIDE_EFFECTING


def _params(n_grid):
    return pltpu.CompilerParams(dimension_semantics=("arbitrary",) * n_grid, vmem_limit_bytes=V7X_VMEM_LIMIT)


_PREVIOUS = [None]


def _call(body, *, name, in_specs, out_specs, out_shape, grid=(), num_scalar_prefetch=0, scratch_shapes=(),
          input_output_aliases=None, compiler_params=None, hbm_args=0):
    n_in = len(in_specs)

    def run(*args):
        dep = _PREVIOUS[0]
        if any(dep is a for a in args):
            dep = None
        args = [pltpu.with_memory_space_constraint(a, pltpu.HBM) if i < hbm_args else a for i, a in enumerate(args)]
        specs = list(in_specs) + ([ANY] if dep is not None else [])
        k = num_scalar_prefetch + n_in
        fn = body if dep is None else (lambda *refs: body(*refs[:k], *refs[k + 1:]))
        if num_scalar_prefetch:
            shape = dict(grid_spec=pltpu.PrefetchScalarGridSpec(
                num_scalar_prefetch=num_scalar_prefetch, grid=grid, in_specs=specs, out_specs=out_specs,
                scratch_shapes=scratch_shapes))
        else:
            shape = dict(grid=grid, in_specs=specs, out_specs=out_specs, scratch_shapes=scratch_shapes)
        out = pl.pallas_call(fn, name=name, out_shape=out_shape, input_output_aliases=input_output_aliases or {},
                             compiler_params=compiler_params, **shape)(*args, *([] if dep is None else [dep]))
        _PREVIOUS[0] = jax.tree.leaves(out)[-1]
        return out

    return run


def _dot(a, b):
    return jnp.dot(a, b, preferred_element_type=F32)


def _dot_nt(a, b):
    return lax.dot_general(a, b, (((1,), (1,)), ((), ())), preferred_element_type=F32)


def _dot_tn(a, b):
    return lax.dot_general(a, b, (((0,), (0,)), ((), ())), preferred_element_type=F32)


def _rms_fwd(x, g):
    r = lax.rsqrt(jnp.mean(x * x, axis=-1, keepdims=True) + EPS)
    xh = x * r
    return xh * g, xh, r


def _rms_bwd(dy, xh, r, g):
    u = dy * g
    dx = r * (u - xh * jnp.mean(u * xh, axis=-1, keepdims=True))
    dg = jnp.sum(dy * xh, axis=0, keepdims=True)
    return dx, dg


def _softplus(z):
    neg_abs = lax.bitcast_convert_type(lax.bitcast_convert_type(z, jnp.int32) | jnp.int32(-2 ** 31), F32)
    sp = jnp.maximum(z, 0.0) + jnp.log(1.0 + jnp.exp(neg_abs))
    return sp, z - sp


def _norm_cast(h, g):
    t, w = h.shape

    def body(h_ref, g_ref, n_ref):
        y, _, _ = _rms_fwd(h_ref[...], g_ref[...])
        n_ref[...] = y.astype(BF16)

    return _call(
        body, name="norm_cast", grid=(t // TM,),
        in_specs=[pl.BlockSpec((TM, w), lambda i: (i, 0)), pl.BlockSpec((1, w), lambda i: (0, 0))],
        out_specs=pl.BlockSpec((TM, w), lambda i: (i, 0)),
        out_shape=jax.ShapeDtypeStruct((t, w), BF16), compiler_params=_params(1))(h, g)


def _ffn_gu(n, wgu):
    t, d = n.shape

    def body(n_ref, wg_ref, wu_ref, gu_ref, act_ref):
        x = n_ref[...]
        g = _dot(x, wg_ref[...])
        u = _dot(x, wu_ref[...])
        sig = jax.nn.sigmoid(g)
        silu = g * sig
        gu_ref[0] = (u * (sig + silu * (1.0 - sig))).astype(BF16)
        gu_ref[1] = silu.astype(BF16)
        act_ref[...] = (silu * u).astype(BF16)

    return _call(
        body, name="ffn_gu", grid=(2, t // TM),
        in_specs=[pl.BlockSpec((TM, d), lambda j, i: (i, 0)),
                  pl.BlockSpec((None, d, FS), lambda j, i: (j, 0, 0)),
                  pl.BlockSpec((None, d, FS), lambda j, i: (j + 2, 0, 0))],
        out_specs=[pl.BlockSpec((2, TM, FS), lambda j, i: (0, i, j)), pl.BlockSpec((TM, FS), lambda j, i: (i, j))],
        out_shape=[jax.ShapeDtypeStruct((2, t, D_FF), BF16), jax.ShapeDtypeStruct((t, D_FF), BF16)],
        compiler_params=_params(2))(n, wgu, wgu)


def _down_res(act, wdn, h, g_next):
    t, f = act.shape
    d = h.shape[1]

    def body(a_ref, w_ref, h_ref, g_ref, o_ref, n_ref):
        out = h_ref[...] + 0.5 * _dot(a_ref[...], w_ref[...])
        o_ref[...] = out
        n_ref[...] = _rms_fwd(out, g_ref[...])[0].astype(BF16)

    row = pl.BlockSpec((TM, d), lambda i: (i, 0))
    return _call(
        body, name="down_res", grid=(t // TM,),
        in_specs=[pl.BlockSpec((TM, f), lambda i: (i, 0)), pl.BlockSpec((f, d), lambda i: (0, 0)), row,
                  pl.BlockSpec((1, d), lambda i: (0, 0))],
        out_specs=[row, row],
        out_shape=[jax.ShapeDtypeStruct((t, d), F32), jax.ShapeDtypeStruct((t, d), BF16)],
        compiler_params=_params(1))(act, wdn, h, g_next)


def _proj(n, w_in_t):
    t, d = n.shape
    w = w_in_t.shape[0]

    def body(n_ref, w_ref, o_ref):
        o_ref[...] = _dot_nt(n_ref[...], w_ref[...]).astype(BF16)

    return _call(
        body, name="proj", grid=(t // TM,),
        in_specs=[pl.BlockSpec((TM, d), lambda i: (i, 0)), pl.BlockSpec((w, d), lambda i: (0, 0))],
        out_specs=pl.BlockSpec((TM, w), lambda i: (i, 0)),
        out_shape=jax.ShapeDtypeStruct((t, w), BF16), compiler_params=_params(1))(n, w_in_t)


def _out_res(o_sb, o_sw, g_sb, g_sw, w_out, h, g_next):
    t, d = h.shape

    def body(a_ref, b_ref, ga_ref, gb_ref, w_ref, h_ref, g_ref, o_ref, mix_ref, n_ref):
        ya, _, _ = _rms_fwd(a_ref[...], ga_ref[...])
        yb, _, _ = _rms_fwd(b_ref[...], gb_ref[...])
        mixed = jnp.concatenate([ya.astype(BF16), yb.astype(BF16)], axis=1)
        mix_ref[...] = mixed
        out = h_ref[...] + _dot(mixed, w_ref[...])
        o_ref[...] = out
        n_ref[...] = _rms_fwd(out, g_ref[...])[0].astype(BF16)

    row = pl.BlockSpec((TM, d), lambda i: (i, 0))
    return _call(
        body, name="out_res", grid=(t // TM,),
        in_specs=[pl.BlockSpec((TM, SB_W), lambda i: (i, 0)), pl.BlockSpec((TM, SWA_W), lambda i: (i, 0)),
                  pl.BlockSpec((1, SB_W), lambda i: (0, 0)), pl.BlockSpec((1, SWA_W), lambda i: (0, 0)),
                  pl.BlockSpec((d, d), lambda i: (0, 0)), row, pl.BlockSpec((1, d), lambda i: (0, 0))],
        out_specs=[row, row, row],
        out_shape=[jax.ShapeDtypeStruct((t, d), F32), jax.ShapeDtypeStruct((t, d), BF16),
                   jax.ShapeDtypeStruct((t, d), BF16)],
        compiler_params=_params(1))(o_sb, o_sw, g_sb, g_sw, w_out, h, g_next)


def _loss_head(h, g, tgt):
    t, d = h.shape

    def body(h_ref, g_ref, t_ref, dh_ref, dhb_ref, dg_ref, loss_ref):
        @pl.when(pl.program_id(0) == 0)
        def _():
            dg_ref[...] = jnp.zeros_like(dg_ref)
            loss_ref[...] = jnp.zeros_like(loss_ref)

        gg = g_ref[...]
        y, xh, r = _rms_fwd(h_ref[...], gg)
        err = y - t_ref[...]
        part = 0.5 * jnp.sum(jnp.sum(err * err, axis=1, keepdims=True) / d, axis=0, keepdims=True)
        loss_ref[...] += jnp.broadcast_to(part, loss_ref.shape)
        dx, dg = _rms_bwd(err / d, xh, r, gg)
        dh_ref[...] = dx
        dhb_ref[...] = dx.astype(BF16)
        dg_ref[...] += dg

    row = pl.BlockSpec((TM, d), lambda i: (i, 0))
    return _call(
        body, name="loss_head", grid=(t // TM,),
        in_specs=[row, pl.BlockSpec((1, d), lambda i: (0, 0)), row],
        out_specs=[row, row, pl.BlockSpec((1, d), lambda i: (0, 0)), pl.BlockSpec((1, LANES), lambda i: (0, 0))],
        out_shape=[jax.ShapeDtypeStruct((t, d), F32), jax.ShapeDtypeStruct((t, d), BF16),
                   jax.ShapeDtypeStruct((1, d), F32), jax.ShapeDtypeStruct((1, LANES), F32)],
        compiler_params=_params(1))(h, g, tgt)


def _ffn_dact(dh, wdn, gu):
    t, d = dh.shape
    tm = TM

    def body(dh_ref, w_ref, gu_ref, o_ref):
        da = 0.5 * _dot_nt(dh_ref[...].astype(BF16), w_ref[...])
        o_ref[0] = (da * gu_ref[0].astype(F32)).astype(BF16)
        o_ref[1] = (da * gu_ref[1].astype(F32)).astype(BF16)

    return _call(
        body, name="ffn_dact", grid=(2, t // tm),
        in_specs=[pl.BlockSpec((tm, d), lambda j, i: (i, 0)), pl.BlockSpec((FS, d), lambda j, i: (j, 0)),
                  pl.BlockSpec((2, tm, FS), lambda j, i: (0, i, j))],
        out_specs=pl.BlockSpec((2, tm, FS), lambda j, i: (0, i, j)),
        out_shape=jax.ShapeDtypeStruct((2, t, D_FF), BF16), compiler_params=_params(2))(dh, wdn, gu)


def _dn_norm_bwd(a, a_spec, w, w_spec, nk, dh, h_in, g, w_transposed=False, tm=TM):
    t, d = dh.shape
    mm = _dot if w_transposed else _dot_nt

    def body(a_ref, w_ref, dh_ref, h_ref, g_ref, o_ref, ob_ref, dg_ref, acc_ref):
        i, k = pl.program_id(0), pl.program_id(1)

        if nk > 1:
            @pl.when(k == 0)
            def _():
                acc_ref[...] = mm(a_ref[...], w_ref[...])

            @pl.when((k > 0) & (k < nk - 1))
            def _():
                acc_ref[...] += mm(a_ref[...], w_ref[...])

        @pl.when(k == nk - 1)
        def _():
            gg = g_ref[...]
            dg = jnp.zeros_like(gg)
            for rows in (slice(r, r + TM // 2) for r in range(0, tm, TM // 2)):
                dn = mm(a_ref[rows, :], w_ref[...])
                if nk > 1:
                    dn = dn + acc_ref[rows, :]
                _, xh, r = _rms_fwd(h_ref[rows, :], gg)
                dx, dg_rows = _rms_bwd(dn, xh, r, gg)
                out = dh_ref[rows, :] + dx
                o_ref[rows, :] = out
                ob_ref[rows, :] = out.astype(BF16)
                dg = dg + dg_rows

            @pl.when(i == 0)
            def _():
                dg_ref[...] = dg

            @pl.when(i > 0)
            def _():
                dg_ref[...] += dg

    row = pl.BlockSpec((tm, d), lambda i, k: (i, 0))
    return _call(
        body, name="dn_norm_bwd", grid=(t // tm, nk),
        in_specs=[a_spec, w_spec, row, row, pl.BlockSpec((1, d), lambda i, k: (0, 0))],
        out_specs=[row, row, pl.BlockSpec((1, d), lambda i, k: (0, 0))],
        out_shape=[jax.ShapeDtypeStruct((t, d), F32), jax.ShapeDtypeStruct((t, d), BF16),
                   jax.ShapeDtypeStruct((1, d), F32)],
        scratch_shapes=[pltpu.VMEM((tm, d), F32)], compiler_params=_params(2))(a, w, dh, h_in, g)


def _ffn_dn(dgu, wgu, dh, h_in, g):
    d = dh.shape[1]
    tm = 2 * TM
    return _dn_norm_bwd(
        dgu, pl.BlockSpec((None, tm, FS), lambda i, k: (k // 2, i, k % 2)),
        wgu, pl.BlockSpec((None, d, FS), lambda i, k: (k, 0, 0)), N_CHIPS, dh, h_in, g, tm=tm)


def _mix_dn(dproj, w_in_t, dh, h_in, g):
    d = dh.shape[1]
    w = dproj.shape[1]
    return _dn_norm_bwd(
        dproj, pl.BlockSpec((TM, w), lambda i, k: (i, 0)),
        w_in_t, pl.BlockSpec((w, d), lambda i, k: (0, 0)), 1, dh, h_in, g, w_transposed=True)


def _dmixed(dh, w_out, o_sb, o_sw, g_sb, g_sw):
    t, d = dh.shape

    def body(dh_ref, w_ref, a_ref, b_ref, ga_ref, gb_ref, o_ref, dga_ref, dgb_ref):
        i = pl.program_id(0)
        dm = _dot_nt(dh_ref[...].astype(BF16), w_ref[...])
        _, xa, ra = _rms_fwd(a_ref[...], ga_ref[...])
        _, xb, rb = _rms_fwd(b_ref[...], gb_ref[...])
        da, dga = _rms_bwd(dm[:, :SB_W], xa, ra, ga_ref[...])
        db, dgb = _rms_bwd(dm[:, SB_W:], xb, rb, gb_ref[...])
        o_ref[...] = jnp.concatenate([da.astype(BF16), db.astype(BF16)], axis=1)

        @pl.when(i == 0)
        def _():
            dga_ref[...] = dga
            dgb_ref[...] = dgb

        @pl.when(i > 0)
        def _():
            dga_ref[...] += dga
            dgb_ref[...] += dgb

    return _call(
        body, name="dmixed", grid=(t // TM,),
        in_specs=[pl.BlockSpec((TM, d), lambda i: (i, 0)), pl.BlockSpec((d, d), lambda i: (0, 0)),
                  pl.BlockSpec((TM, SB_W), lambda i: (i, 0)), pl.BlockSpec((TM, SWA_W), lambda i: (i, 0)),
                  pl.BlockSpec((1, SB_W), lambda i: (0, 0)), pl.BlockSpec((1, SWA_W), lambda i: (0, 0))],
        out_specs=[pl.BlockSpec((TM, d), lambda i: (i, 0)), pl.BlockSpec((1, SB_W), lambda i: (0, 0)),
                   pl.BlockSpec((1, SWA_W), lambda i: (0, 0))],
        out_shape=[jax.ShapeDtypeStruct((t, d), BF16), jax.ShapeDtypeStruct((1, SB_W), F32),
                   jax.ShapeDtypeStruct((1, SWA_W), F32)],
        compiler_params=_params(1))(dh, w_out, o_sb, o_sw, g_sb, g_sw)


def _wgrad(name, a, a_spec, b, b_spec, grid, out_shape, out_spec, scale):
    def body(a_ref, b_ref, o_ref):
        r = _dot_tn(a_ref[...], b_ref[...].astype(BF16))
        o_ref[...] = r if scale == 1.0 else scale * r

    return _call(
        body, name=name, grid=grid, in_specs=[a_spec, b_spec], out_specs=out_spec,
        out_shape=jax.ShapeDtypeStruct(out_shape, F32), compiler_params=_params(len(grid)))(a, b)


def _wgrad_gu(n, dgu):
    t, d = n.shape
    return _wgrad(
        "wgrad_gu", n, pl.BlockSpec((t, TM), lambda s, r: (0, r)),
        dgu, pl.BlockSpec((None, t, FS), lambda s, r: (s // 2, 0, s % 2)), (N_CHIPS, d // TM),
        (N_CHIPS, d, FS), pl.BlockSpec((None, TM, FS), lambda s, r: (s, r, 0)), 1.0)


def _wgrad_down(act, dh):
    t, d = dh.shape
    return _wgrad(
        "wgrad_down", act, pl.BlockSpec((t, FS), lambda s: (0, s)), dh, pl.BlockSpec((t, d), lambda s: (0, 0)),
        (2,), (D_FF, d), pl.BlockSpec((FS, d), lambda s: (s, 0)), 0.5)


def _wgrad_out(mixed, dh):
    t, d = dh.shape
    return _wgrad(
        "wgrad_out", mixed, pl.BlockSpec((t, TM), lambda s: (0, s)), dh, pl.BlockSpec((t, d), lambda s: (0, 0)),
        (d // TM,), (d, d), pl.BlockSpec((TM, d), lambda s: (s, 0)), 1.0)


def _wgrad_in(n, dproj):
    t, d = n.shape
    w = dproj.shape[1]
    tw = w // 3
    return _wgrad(
        "wgrad_in", dproj, pl.BlockSpec((t, tw), lambda s: (0, s)), n, pl.BlockSpec((t, d), lambda s: (0, 0)),
        (3,), (w, d), pl.BlockSpec((tw, d), lambda s: (s, 0)), 1.0)


def _tri(rel):
    row = lax.broadcasted_iota(jnp.int32, (BLK, BLK), 0)
    col = lax.broadcasted_iota(jnp.int32, (BLK, BLK), 1)
    m = rel(row, col).astype(BF16)
    return jnp.concatenate([m, m], axis=0)


def _scan_dot(x, tri2):
    hi = x.astype(BF16)
    lo = (x - hi.astype(F32)).astype(BF16)
    return _dot(jnp.concatenate([hi, lo], axis=1), tri2)


def _head_masks():
    lane = lax.broadcasted_iota(jnp.int32, (1, LANES), 1)
    return [lane < HEAD_DIM, lane >= HEAD_DIM]


SB_PAIRS = 2
SB_ROWS = 2 * SB_PAIRS * BLK


def _sb_causal():
    row = lax.broadcasted_iota(jnp.int32, (SB_ROWS, BLK), 0) & (BLK - 1)
    return lax.broadcasted_iota(jnp.int32, (SB_ROWS, BLK), 1) < row


def _sb_mask_last(x, causal):
    own = jnp.where(causal, x[:, -BLK:], 0.0)
    return own if x.shape[1] == BLK else jnp.concatenate([x[:, :-BLK], own], axis=1)


def _sb_stack(x, hm):
    return jnp.concatenate([jnp.where(m, x[:, p * LANES:(p + 1) * LANES], jnp.zeros((BLK, LANES), x.dtype))
                            for p in range(SB_PAIRS) for m in hm], axis=0)


def _sb_unstack(y, hm):
    return jnp.concatenate([jnp.where(hm[0], y[2 * p * BLK:(2 * p + 1) * BLK], y[(2 * p + 1) * BLK:(2 * p + 2) * BLK])
                            for p in range(SB_PAIRS)], axis=1)


def _sb_pairs():
    return [(slice(2 * p * BLK, (2 * p + 2) * BLK), slice(p * LANES, (p + 1) * LANES)) for p in range(SB_PAIRS)]


def _sb_fwd(proj):
    t = proj.shape[0]
    nb = SB_KT // BLK
    wide = SB_PAIRS * LANES

    def body(q_ref, k_ref, v_ref, o_ref, tot_ref):
        hm = _head_masks()
        causal = _sb_causal()
        pairs = _sb_pairs()
        after = _tri(lambda r, c: r > c)

        def tile(qh, start, n_blk, carry, acc, own):
            ks = pl.ds(pl.multiple_of(start, BLK), n_blk * BLK)
            z = jnp.concatenate([_dot_nt(qh[rows], k_ref[ks, lanes]) for rows, lanes in pairs], axis=0)
            sp, zs = _softplus(z)
            spm = _sb_mask_last(sp, causal) if own else sp
            sufs = [None] * n_blk
            for b in reversed(range(n_blk)):
                blk = spm[:, b * BLK:(b + 1) * BLK]
                sufs[b] = carry + _scan_dot(blk, after)
                carry = carry + jnp.sum(blk, axis=1, keepdims=True)
            w = jnp.exp(zs - jnp.concatenate(sufs, axis=1))
            wb = (_sb_mask_last(w, causal) if own else w).astype(BF16)
            return carry, acc + jnp.concatenate([_dot(wb[rows], v_ref[ks, lanes]) for rows, lanes in pairs], axis=0)

        def qblock(g, j):
            qs = pl.ds(pl.multiple_of(g * SB_KT + j * BLK, BLK), BLK)
            qh = _sb_stack(q_ref[qs, :] * SCALE, hm)
            c0 = tile(qh, g * SB_KT, j + 1, jnp.zeros((SB_ROWS, 1), F32), jnp.zeros((SB_ROWS, LANES), F32), True)
            carry, acc = lax.fori_loop(0, g, lambda n, c: tile(qh, (g - 1 - n) * SB_KT, nb, c[0], c[1], False), c0)
            o_ref[qs, :] = _sb_unstack(acc, hm)
            for h in range(2 * SB_PAIRS):
                tot_ref[h, qs, :] = carry[h * BLK:(h + 1) * BLK]

        def group(g, _):
            for j in range(nb):
                qblock(g, j)
            return 0

        lax.fori_loop(0, t // SB_KT, group, 0)

    col_blk = lambda off: pl.BlockSpec((t, wide), lambda g: (0, off + g))
    n_steps = SB_W // wide
    return _call(
        body, name="sb_fwd", grid=(n_steps,), in_specs=[col_blk(0), col_blk(n_steps), col_blk(2 * n_steps)],
        out_specs=[col_blk(0), pl.BlockSpec((2 * SB_PAIRS, t, 1), lambda g: (g, 0, 0))],
        out_shape=[jax.ShapeDtypeStruct((t, SB_W), F32), jax.ShapeDtypeStruct((8, t, 1), F32)],
        compiler_params=_params(1))(proj, proj, proj)


def _sb_bwd(proj, d_o, tot):
    t = proj.shape[0]
    nb = SB_KT // BLK
    wide = SB_PAIRS * LANES

    def body(q_ref, k_ref, v_ref, do_ref, tot_ref, dq_ref, dk_ref, dv_ref, dk_acc, dv_acc):
        hm = _head_masks()
        causal = _sb_causal()
        pairs = _sb_pairs()
        before = _tri(lambda r, c: r < c)
        upto = _tri(lambda r, c: r <= c)
        dk_acc[...] = jnp.zeros_like(dk_acc)
        dv_acc[...] = jnp.zeros_like(dv_acc)

        def tile(qh, doh, tt, start, n_blk, pre, ecum, dq, own):
            ks = pl.ds(pl.multiple_of(start, BLK), n_blk * BLK)
            k = k_ref[ks, :]
            v = v_ref[ks, :]
            z = jnp.concatenate([_dot_nt(qh[rows], k[:, lanes]) for rows, lanes in pairs], axis=0)
            sp, zs = _softplus(z)
            spm = _sb_mask_last(sp, causal) if own else sp
            pres = []
            for b in range(n_blk):
                blk = spm[:, b * BLK:(b + 1) * BLK]
                pres.append(pre + _scan_dot(blk, before))
                pre = pre + jnp.sum(blk, axis=1, keepdims=True)
            logw = z - (tt - jnp.concatenate(pres, axis=1))
            if own:
                logw = jnp.minimum(logw, 0.0)
            w = jnp.exp(logw)
            if own:
                w = _sb_mask_last(w, causal)
            e = w * jnp.concatenate([_dot_nt(doh[rows], v[:, lanes]) for rows, lanes in pairs], axis=0)
            incs = []
            for b in range(n_blk):
                blk = e[:, b * BLK:(b + 1) * BLK]
                incs.append(ecum + _scan_dot(blk, upto))
                ecum = ecum + jnp.sum(blk, axis=1, keepdims=True)
            dz = e - jnp.exp(zs) * jnp.concatenate(incs, axis=1)
            if own:
                dz = _sb_mask_last(dz, causal)
            dzb = dz.astype(BF16)
            wb = w.astype(BF16)
            for rows, lanes in pairs:
                dk_acc[ks, lanes] += _dot_tn(dzb[rows], qh[rows])
                dv_acc[ks, lanes] += _dot_tn(wb[rows], doh[rows])
            return pre, ecum, dq + jnp.concatenate([_dot(dzb[rows], k[:, lanes]) for rows, lanes in pairs], axis=0)

        def qblock(g, j):
            qs = pl.ds(pl.multiple_of(g * SB_KT + j * BLK, BLK), BLK)
            qh = _sb_stack(q_ref[qs, :] * SCALE, hm)
            doh = _sb_stack(do_ref[qs, :], hm)
            tt = jnp.concatenate([tot_ref[h, qs, :] for h in range(2 * SB_PAIRS)], axis=0)
            c0 = (jnp.zeros((SB_ROWS, 1), F32), jnp.zeros((SB_ROWS, 1), F32), jnp.zeros((SB_ROWS, LANES), F32))
            c = lax.fori_loop(0, g, lambda kt, c: tile(qh, doh, tt, kt * SB_KT, nb, c[0], c[1], c[2], False), c0)
            dq = tile(qh, doh, tt, g * SB_KT, j + 1, c[0], c[1], c[2], True)[2]
            dq_ref[qs, :] = (_sb_unstack(dq, hm) * SCALE).astype(BF16)

        def group(g, _):
            for j in range(nb):
                qblock(g, j)
            return 0

        lax.fori_loop(0, t // SB_KT, group, 0)
        dk_ref[...] = dk_acc[...].astype(BF16)
        dv_ref[...] = dv_acc[...].astype(BF16)

    col_blk = lambda off: pl.BlockSpec((t, wide), lambda g: (0, off + g))
    n_steps = SB_W // wide
    out = jax.ShapeDtypeStruct((t, SB_W), BF16)
    return _call(
        body, name="sb_bwd", grid=(n_steps,),
        in_specs=[col_blk(0), col_blk(n_steps), col_blk(2 * n_steps), col_blk(0),
                  pl.BlockSpec((2 * SB_PAIRS, t, 1), lambda g: (g, 0, 0))],
        out_specs=[col_blk(0), col_blk(0), col_blk(0)], out_shape=[out, out, out],
        scratch_shapes=[pltpu.VMEM((t, wide), F32), pltpu.VMEM((t, wide), F32)],
        compiler_params=_params(1))(proj, proj, proj, d_o, tot)


def _bucket_table():
    a = np.arange(BLK)[:, None]
    c = np.arange(2 * BLK)[None, :]
    dist = np.maximum(BLK + a - c, 0)
    max_exact = N_BUCKETS // 2
    dd = np.maximum(dist, 1).astype(np.float32)
    large = max_exact + (np.log(dd / max_exact) / math.log(MAX_DISTANCE / max_exact)
                         * (N_BUCKETS - max_exact)).astype(np.int32)
    large = np.minimum(large, N_BUCKETS - 1)
    return np.where(dist < max_exact, dist, large).astype(np.int32)


SWA_H = 8


def _swa_band_masks():
    row = lax.broadcasted_iota(jnp.int32, (SWA_H * BLK, 2 * BLK), 0) & (BLK - 1)
    col = lax.broadcasted_iota(jnp.int32, (SWA_H * BLK, 2 * BLK), 1)
    own = lax.broadcasted_iota(jnp.int32, (SWA_H * BLK, BLK), 1) <= (
        lax.broadcasted_iota(jnp.int32, (SWA_H * BLK, BLK), 0) & (BLK - 1))
    return (col > row) & ((col < BLK) | (col - BLK <= row)), own


def _swa_stack(ref, qs, hm, scale):
    parts = []
    for hq in range(SWA_H):
        kvh = hq // SWA_G
        x = ref[qs, (hq // 2) * LANES:(hq // 2 + 1) * LANES].astype(F32)
        if hq % 2 != kvh:
            x = pltpu.roll(x, HEAD_DIM, 1)
        parts.append(jnp.where(hm[kvh], x * scale, 0.0).astype(BF16))
    return jnp.concatenate(parts, axis=0)


def _swa_unstack(x8, hm):
    heads = []
    for hq in range(SWA_H):
        x = x8[hq * BLK:(hq + 1) * BLK]
        heads.append(pltpu.roll(x, HEAD_DIM, 1) if hq % 2 != hq // SWA_G else x)
    return [jnp.where(hm[0], heads[2 * p], heads[2 * p + 1]) for p in range(SWA_H // 2)]


def _swa_scores(q8, kb, bias_ref, mask, cols):
    bias8 = jnp.concatenate([bias_ref[hq, :, cols] for hq in range(SWA_H)], axis=0)
    return jnp.where(mask, _dot_nt(q8, kb) + bias8, NEG_INF)


def _swa_sinks(sink_ref):
    return jnp.concatenate([jnp.broadcast_to(sink_ref[hq:hq + 1, 0:1], (BLK, 1)) for hq in range(SWA_H)], axis=0)


def _swa_fwd(proj, bias, sinks_b):
    t = proj.shape[0]
    nq = t // BLK

    def body(q_ref, k_ref, v_ref, bias_ref, sink_ref, o_ref, lse_ref):
        hm = _head_masks()
        band, own = _swa_band_masks()

        def qblock(i, prev):
            qs = pl.ds(pl.multiple_of(i * BLK, BLK), BLK)
            if prev:
                ks, mask, cols = pl.ds(pl.multiple_of((i - 1) * BLK, BLK), 2 * BLK), band, slice(None)
            else:
                ks, mask, cols = qs, own, slice(BLK, None)
            q8 = _swa_stack(q_ref, qs, hm, SCALE)
            sink8 = _swa_sinks(sink_ref)
            s = _swa_scores(q8, k_ref[ks, :], bias_ref, mask, cols)
            m = jnp.maximum(jnp.max(s, axis=1, keepdims=True), sink8)
            p = jnp.exp(s - m)
            den = jnp.sum(p, axis=1, keepdims=True) + jnp.exp(sink8 - m)
            o8 = _dot((p * (1.0 / den)).astype(BF16), v_ref[ks, :])
            lse8 = m + jnp.log(den)
            for hq in range(SWA_H):
                lse_ref[hq, qs, :] = lse8[hq * BLK:(hq + 1) * BLK]
            for pp, o in enumerate(_swa_unstack(o8, hm)):
                o_ref[qs, pp * LANES:(pp + 1) * LANES] = o

        qblock(0, False)

        def step(i, _):
            qblock(i, True)
            return 0

        lax.fori_loop(1, nq, step, 0)

    return _call(
        body, name="swa_fwd", grid=(1,),
        in_specs=[pl.BlockSpec((t, SWA_W), lambda i: (0, 3)), pl.BlockSpec((t, KV_W), lambda i: (0, 16)),
                  pl.BlockSpec((t, KV_W), lambda i: (0, 17)), pl.BlockSpec((8, BLK, 2 * BLK), lambda i: (0, 0, 0)),
                  pl.BlockSpec((8, LANES), lambda i: (0, 0))],
        out_specs=[pl.BlockSpec((t, SWA_W), lambda i: (0, 0)), pl.BlockSpec((8, t, 1), lambda i: (0, 0, 0))],
        out_shape=[jax.ShapeDtypeStruct((t, SWA_W), F32), jax.ShapeDtypeStruct((8, t, 1), F32)],
        compiler_params=_params(1))(proj, proj, proj, bias, sinks_b)


def _swa_bwd(proj, d_o, lse, bias, sinks_b, dbias_in):
    t = proj.shape[0]
    nq = t // BLK

    def body(q_ref, k_ref, v_ref, do_ref, lse_ref, bias_ref, sink_ref, dbi_ref,
             dq_ref, dk_ref, dv_ref, dsink_ref, dbias_ref, dk_acc, dv_acc):
        hm = _head_masks()
        band, own = _swa_band_masks()
        dk_acc[...] = jnp.zeros_like(dk_acc)
        dv_acc[...] = jnp.zeros_like(dv_acc)
        dbias_ref[...] = dbi_ref[...]

        def qblock(i, prev, dsink8):
            qs = pl.ds(pl.multiple_of(i * BLK, BLK), BLK)
            if prev:
                ks, mask, cols = pl.ds(pl.multiple_of((i - 1) * BLK, BLK), 2 * BLK), band, slice(None)
            else:
                ks, mask, cols = qs, own, slice(BLK, None)
            q8 = _swa_stack(q_ref, qs, hm, SCALE)
            do8 = _swa_stack(do_ref, qs, hm, 1.0)
            sink8 = _swa_sinks(sink_ref)
            lse8 = jnp.concatenate([lse_ref[hq, qs, :] for hq in range(SWA_H)], axis=0)
            kb = k_ref[ks, :]
            p = jnp.exp(_swa_scores(q8, kb, bias_ref, mask, cols) - lse8)
            dp = _dot_nt(do8, v_ref[ks, :])
            delta = jnp.sum(p * dp, axis=1, keepdims=True)
            ds = p * (dp - delta)
            for hq in range(SWA_H):
                dbias_ref[hq, :, cols] += ds[hq * BLK:(hq + 1) * BLK]
            dsb = ds.astype(BF16)
            dk_acc[ks, :] += _dot_tn(dsb, q8)
            dv_acc[ks, :] += _dot_tn(p.astype(BF16), do8)
            for pp, dq in enumerate(_swa_unstack(_dot(dsb, kb) * SCALE, hm)):
                dq_ref[qs, pp * LANES:(pp + 1) * LANES] = dq.astype(BF16)
            return dsink8 - jnp.exp(sink8 - lse8) * delta

        ds0 = qblock(0, False, jnp.zeros((SWA_H * BLK, 1), F32))
        ds8 = lax.fori_loop(1, nq, lambda i, c: qblock(i, True, c), ds0)
        for hq in range(SWA_H):
            dsink_ref[hq:hq + 1, :] = jnp.broadcast_to(
                jnp.sum(ds8[hq * BLK:(hq + 1) * BLK], axis=0, keepdims=True), (1, LANES))

        dk_ref[...] = dk_acc[...].astype(BF16)
        dv_ref[...] = dv_acc[...].astype(BF16)

    full3 = pl.BlockSpec((8, BLK, 2 * BLK), lambda i: (0, 0, 0))
    kv = jax.ShapeDtypeStruct((t, KV_W), BF16)
    return _call(
        body, name="swa_bwd", grid=(1,),
        in_specs=[pl.BlockSpec((t, SWA_W), lambda i: (0, 3)), pl.BlockSpec((t, KV_W), lambda i: (0, 16)),
                  pl.BlockSpec((t, KV_W), lambda i: (0, 17)), pl.BlockSpec((t, SWA_W), lambda i: (0, 1)),
                  pl.BlockSpec((8, t, 1), lambda i: (0, 0, 0)), full3, pl.BlockSpec((8, LANES), lambda i: (0, 0)),
                  full3],
        out_specs=[pl.BlockSpec((t, SWA_W), lambda i: (0, 0)), pl.BlockSpec((t, KV_W), lambda i: (0, 0)),
                   pl.BlockSpec((t, KV_W), lambda i: (0, 0)), pl.BlockSpec((8, LANES), lambda i: (0, 0)), full3],
        out_shape=[jax.ShapeDtypeStruct((t, SWA_W), BF16), kv, kv, jax.ShapeDtypeStruct((8, LANES), F32),
                   jax.ShapeDtypeStruct((8, BLK, 2 * BLK), F32)],
        scratch_shapes=[pltpu.VMEM((t, KV_W), F32), pltpu.VMEM((t, KV_W), F32)],
        compiler_params=_params(1))(proj, proj, proj, d_o, lse, bias, sinks_b, dbias_in)


def _concat_cols(parts):
    t = parts[0].shape[0]
    widths = [a.shape[1] for a in parts]

    def body(*refs):
        refs[-1][...] = jnp.concatenate([r[...] for r in refs[:-1]], axis=1)

    return _call(
        body, name="concat_cols", grid=(t // TM,),
        in_specs=[pl.BlockSpec((TM, w), lambda i: (i, 0)) for w in widths],
        out_specs=pl.BlockSpec((TM, sum(widths)), lambda i: (i, 0)),
        out_shape=jax.ShapeDtypeStruct((t, sum(widths)), parts[0].dtype), compiler_params=_params(1))(*parts)


def _bias_table(rel_bias, buckets):
    def body(rb_ref, b_ref, o_ref):
        bk = b_ref[...]
        for h in range(8):
            acc = jnp.zeros((BLK, 2 * BLK), F32)
            for b in range(N_BUCKETS):
                acc = jnp.where(bk == b, rb_ref[b, h], acc)
            o_ref[h] = acc

    return _call(
        body, name="bias_table", grid=(1,),
        in_specs=[pl.BlockSpec(memory_space=pltpu.SMEM), pl.BlockSpec((BLK, 2 * BLK), lambda i: (0, 0))],
        out_specs=pl.BlockSpec((8, BLK, 2 * BLK), lambda i: (0, 0, 0)),
        out_shape=jax.ShapeDtypeStruct((8, BLK, 2 * BLK), F32), compiler_params=_params(1))(rel_bias, buckets)


def _bias_grad(dbias, buckets):
    def body(d_ref, b_ref, o_ref):
        lane = lax.broadcasted_iota(jnp.int32, (1, LANES), 1)
        bk = b_ref[...]
        for h in range(8):
            d = d_ref[h]
            acc = jnp.zeros((1, LANES), F32)
            for b in range(N_BUCKETS):
                s = jnp.sum(jnp.sum(jnp.where(bk == b, d, 0.0), axis=0, keepdims=True), axis=1, keepdims=True)
                acc = acc + jnp.where(lane == b, s, 0.0)
            o_ref[h:h + 1, :] = acc

    return _call(
        body, name="bias_grad", grid=(1,),
        in_specs=[pl.BlockSpec((8, BLK, 2 * BLK), lambda i: (0, 0, 0)), pl.BlockSpec((BLK, 2 * BLK), lambda i: (0, 0))],
        out_specs=pl.BlockSpec((8, LANES), lambda i: (0, 0)),
        out_shape=jax.ShapeDtypeStruct((8, LANES), F32), compiler_params=_params(1))(dbias, buckets)


def _row(a):
    return a.reshape(1, -1)


def _fwd_ffn1_gu(h, n1, w):
    s = {"h0": h, "n1": n1}
    s["gu1"], s["act1"] = _ffn_gu(n1, w["ffn1_gu"])
    return s


def _fwd_ffn1_down(s, w, small, l):
    s["h1"], s["nm"] = _down_res(s["act1"], w["ffn1_down"], s["h0"], _row(small["norm_mix"][l]))


def _fwd_ffn1(h, n1, w, small, l):
    s = _fwd_ffn1_gu(h, n1, w)
    _fwd_ffn1_down(s, w, small, l)
    return s


def _fwd_proj_sb(s, w):
    s["proj"] = _proj(s["nm"], w["w_in"])
    s["o_sb"], s["tot"] = _sb_fwd(s["proj"])


def _fwd_swa(s, small, l, bias):
    s["sinks_b"] = jnp.broadcast_to(small["sinks"][l][:, None], (8, LANES))
    s["o_sw"], s["lse"] = _swa_fwd(s["proj"], bias, s["sinks_b"])


def _fwd_out_gu2(s, w, small, l):
    s["h2"], s["mixed"], s["n2"] = _out_res(
        s["o_sb"], s["o_sw"], _row(small["norm_out_sb"][l]), _row(small["norm_out_swa"][l]), w["w_out"], s["h1"],
        _row(small["norm_ffn2"][l]))
    s["gu2"], s["act2"] = _ffn_gu(s["n2"], w["ffn2_gu"])


def _fwd_ffn2_down(s, w, g_after):
    return _down_res(s["act2"], w["ffn2_down"], s["h2"], g_after)


def _fwd_out_ffn2(s, w, small, l, g_after):
    _fwd_out_gu2(s, w, small, l)
    return _fwd_ffn2_down(s, w, g_after)


def _bwd_ffn_dact(dh, s, w, which):
    return _ffn_dact(dh[1], w[f"ffn{which}_down"], s[f"gu{which}"])


def _bwd_ffn_rest(dh, dgu, s, w, small, l, which):
    h_in, norm = (s["h0"], "norm_ffn1") if which == 1 else (s["h2"], "norm_ffn2")
    g_down = _wgrad_down(s[f"act{which}"], dh[1])
    g_gu = _wgrad_gu(s[f"n{which}"], dgu)
    dh32, dh16, dg = _ffn_dn(dgu, w[f"ffn{which}_gu"], dh[0], h_in, _row(small[norm][l]))
    return (dh32, dh16), {f"ffn{which}_down": g_down, f"ffn{which}_gu": g_gu}, {norm: dg}


def _bwd_ffn(dh, s, w, small, l, which):
    return _bwd_ffn_rest(dh, _bwd_ffn_dact(dh, s, w, which), s, w, small, l, which)


def _bwd_mix(dh, s, w, small, l, bias, dbias):
    g_out = _wgrad_out(s["mixed"], dh[1])
    d_o, dg_sb, dg_sw = _dmixed(dh[1], w["w_out"], s["o_sb"], s["o_sw"], _row(small["norm_out_sb"][l]),
                                _row(small["norm_out_swa"][l]))
    dq_sb, dk_sb, dv_sb = _sb_bwd(s["proj"], d_o, s["tot"])
    dq_sw, dk_sw, dv_sw, dsink, dbias = _swa_bwd(s["proj"], d_o, s["lse"], bias, s["sinks_b"], dbias)
    dproj = _concat_cols([dq_sb, dk_sb, dv_sb, dq_sw, dk_sw, dv_sw])
    g_in = _wgrad_in(s["nm"], dproj)
    dh32, dh16, dg_mix = _mix_dn(dproj, w["w_in"], dh[0], s["h1"], _row(small["norm_mix"][l]))
    gs = {"norm_out_sb": dg_sb, "norm_out_swa": dg_sw, "sinks": dsink[:, 0], "norm_mix": dg_mix}
    return (dh32, dh16), {"w_out": g_out, "w_in": g_in}, gs, dbias


def _place():
    x, y, c = lax.axis_index("x"), lax.axis_index("y"), lax.axis_index("c")
    return x, y, c, 2 * x + y


def _chip_core(k, c):
    return (k // 2, k % 2, c)


def _rows_per_block(rows, cols, copies):
    best = 16
    for tr in range(16, rows + 1, 16):
        if rows % tr == 0 and copies * tr * cols * 4 <= SLAB_BLOCK_BYTES:
            best = tr
    assert rows % best == 0
    return best


def _place_own(w, l, me1):
    _, rows, cols = w.shape
    tr = _rows_per_block(rows // 2, cols, 1)
    per_half = rows // 2 // tr

    def body(me_ref, w_ref, o_ref):
        o_ref[...] = w_ref[...].astype(BF16)

    return _call(
        body, name="place_own",
        num_scalar_prefetch=1, grid=(rows // tr,),
        in_specs=[pl.BlockSpec((None, tr, cols), lambda r, me: (l, r, 0))],
        out_specs=pl.BlockSpec((None, None, tr, cols), lambda r, me: (me[0], r // per_half, r % per_half, 0)),
        out_shape=jax.ShapeDtypeStruct((N_CHIPS, 2, rows // 2, cols), BF16), compiler_params=_params(1))(me1, w)


def _plan_gather_ici(bufs):
    _, _, c, me = _place()
    return [(b.at[me, c], b.at[me, c], b.at[(me + 3 - j) % N_CHIPS, c], _chip_core((me + 1 + j) % N_CHIPS, c))
            for b in bufs for j in range(3)]


def _plan_gather_d2d(bufs):
    x, y, c, me = _place()
    return [(b.at[(me + 3 - j) % N_CHIPS, c], b.at[(me + 3 - j) % N_CHIPS, c], b.at[(me + 3 - j) % N_CHIPS, 1 - c],
             (x, y, 1 - c)) for b in bufs for j in range(3)]


def _plan_grad_sibling(bufs):
    x, y, c, _ = _place()
    n = len(bufs) // 2
    return [(g.at[:, 1 - c], z, z, (x, y, 1 - c)) for g, z in zip(bufs[:n], bufs[n:])]


def _plan_grad_chips(bufs):
    _, _, c, me = _place()
    n = len(bufs) // 2
    return [(p.at[j], z.at[j], z.at[j], _chip_core((me + 1 + j) % N_CHIPS, c))
            for p, z in zip(bufs[:n], bufs[n:]) for j in range(3)]


def _plan_grad_halves(l):
    def plan(bufs):
        x, y, c, _ = _place()
        half = lambda b, i: b.at[l, pl.ds(pl.multiple_of(i * (b.shape[1] // 2), 8), b.shape[1] // 2)]
        return [(half(b, c), half(b, c), half(b, 1 - c), (x, y, 1 - c)) for b in bufs]
    return plan


def _remote(src, dst, send_sem, recv_sem, to):
    return pltpu.make_async_remote_copy(src_ref=src, dst_ref=dst, send_sem=send_sem, recv_sem=recv_sem,
                                        device_id=to, device_id_type=MESH)


SIBLING_BARRIER_ID = 0


def _sibling_handshake():
    x, y, c, _ = _place()
    barrier = pltpu.get_barrier_semaphore()
    pl.semaphore_signal(barrier, inc=1, device_id=(x, y, 1 - c), device_id_type=MESH)
    pl.semaphore_wait(barrier, 1)


def _split_params(sibling_only):
    return pltpu.CompilerParams(has_side_effects=EFFECT, collective_id=SIBLING_BARRIER_ID if sibling_only else None)


def _exchange_start_groups(name, plan, groups, sibling_only=False):
    sizes = [len(g) for g, _ in groups]
    bufs = [a for g, _ in groups for a in g]
    n, n_groups = len(bufs), len(groups)

    def body(*refs):
        if sibling_only:
            _sibling_handshake()
        ins, sems, token = refs[:n], refs[n:n + 2 * n_groups], refs[-1]
        at = 0
        for k, size in enumerate(sizes):
            for i, (src, dst, _, to) in enumerate(plan(ins[at:at + size])):
                _remote(src, dst, sems[2 * k].at[i], sems[2 * k + 1].at[i], to).start()
            at += size
        token[...] = jnp.zeros_like(token)

    sem_shapes = [pltpu.SemaphoreType.DMA((n_copies,)) for _, n_copies in groups for _ in range(2)]
    out = _call(
        body, name=name,
        out_shape=(*sem_shapes, *[pltpu.HBM(a.shape, a.dtype) for a in bufs], jax.ShapeDtypeStruct((8, LANES), F32)),
        in_specs=[HBM] * n,
        out_specs=(*[SEM] * (2 * n_groups), *[HBM] * n, pl.BlockSpec(memory_space=pltpu.VMEM)),
        input_output_aliases={t: 2 * n_groups + t for t in range(n)}, hbm_args=n,
        compiler_params=_split_params(sibling_only),
    )(*bufs)
    flights, at = [], 2 * n_groups
    for k, size in enumerate(sizes):
        flights.append(((out[2 * k], out[2 * k + 1]), list(out[at:at + size])))
        at += size
    return flights


def _exchange_start(name, plan, bufs, n_copies, sibling_only=False):
    return _exchange_start_groups(name, plan, [(bufs, n_copies)], sibling_only)[0]


def _exchange_wait(name, plan, bufs, sems, step_output=False):
    n = len(bufs)
    shape = jax.ShapeDtypeStruct if step_output else pltpu.HBM

    def body(*refs):
        ins = refs[:n]
        ssem, rsem = refs[n], refs[n + 1]
        for i, (src, dst, land, to) in enumerate(plan(ins)):
            _remote(src, dst, ssem.at[i], rsem.at[i], to).wait_send()
            _remote(land, land, ssem.at[i], rsem.at[i], to).wait_recv()

    return list(_call(
        body, name=name, out_shape=[shape(a.shape, a.dtype) for a in bufs],
        in_specs=[HBM] * n + [SEM, SEM], out_specs=[HBM] * n,
        input_output_aliases={t: t for t in range(n)},
        compiler_params=pltpu.CompilerParams(has_side_effects=EFFECT),
    )(*bufs, sems[0], sems[1]))


def _exchange_pass(name, done, plan, bufs, sems, n_copies):
    n = len(bufs)

    def body(*refs):
        _sibling_handshake()
        ins = refs[:n]
        old_s, old_r, ssem, rsem = refs[n], refs[n + 1], refs[n + 2], refs[n + 3]
        token = refs[-1]
        for i, (src, dst, land, to) in enumerate(done(ins)):
            _remote(src, dst, old_s.at[i], old_r.at[i], to).wait_send()
            _remote(land, land, old_s.at[i], old_r.at[i], to).wait_recv()
        for i, (src, dst, _, to) in enumerate(plan(ins)):
            _remote(src, dst, ssem.at[i], rsem.at[i], to).start()
        token[...] = jnp.zeros_like(token)

    out = _call(
        body, name=name,
        out_shape=(pltpu.SemaphoreType.DMA((n_copies,)), pltpu.SemaphoreType.DMA((n_copies,)),
                   *[pltpu.HBM(a.shape, a.dtype) for a in bufs], jax.ShapeDtypeStruct((8, LANES), F32)),
        in_specs=[HBM] * n + [SEM, SEM], out_specs=(SEM, SEM, *[HBM] * n, pl.BlockSpec(memory_space=pltpu.VMEM)),
        input_output_aliases={t: 2 + t for t in range(n)},
        compiler_params=_split_params(True),
    )(*bufs, sems[0], sems[1])
    return (out[0], out[1]), list(out[2:2 + n])


def _chip_sum(g, xbuf, cm):
    _, _, r2, cols = g.shape
    tr = _rows_per_block(r2, cols, 1)

    def body(cm_ref, g_ref, x_ref, o_ref):
        o_ref[...] = (g_ref[...] + x_ref[...]).astype(BF16)

    return _call(
        body, name="grad_chip_sum",
        num_scalar_prefetch=1, grid=(3, r2 // tr),
        in_specs=[pl.BlockSpec((None, None, tr, cols), lambda j, r, cm: ((cm[1] + 1 + j) % N_CHIPS, cm[0], r, 0)),
                  pl.BlockSpec((None, tr, cols), lambda j, r, cm: ((cm[1] + 1 + j) % N_CHIPS, r, 0))],
        out_specs=pl.BlockSpec((None, tr, cols), lambda j, r, cm: (j, r, 0)),
        out_shape=jax.ShapeDtypeStruct((3, r2, cols), BF16), compiler_params=_params(2))(cm, g, xbuf)


def _total_sum(g, xbuf, rbuf, cm, l, prev):
    _, _, r2, cols = g.shape
    tr = _rows_per_block(r2, cols, 3)

    def body(cm_ref, g_ref, x_ref, r_ref, *rest):
        acc = g_ref[...] + x_ref[...]
        for j in range(3):
            acc = acc + r_ref[j].astype(F32)
        rest[-1][...] = acc

    return _call(
        body, name="grad_total_sum",
        num_scalar_prefetch=1, grid=(r2 // tr,),
        in_specs=[pl.BlockSpec((None, None, tr, cols), lambda r, cm: (cm[1], cm[0], r, 0)),
                  pl.BlockSpec((None, tr, cols), lambda r, cm: (cm[1], r, 0)),
                  pl.BlockSpec((3, tr, cols), lambda r, cm: (0, r, 0))] + ([] if prev is None else [ANY]),
        out_specs=pl.BlockSpec((None, tr, cols), lambda r, cm: (l, cm[0] * (r2 // tr) + r, 0)),
        out_shape=jax.ShapeDtypeStruct((DEPTH, 2 * r2, cols), F32),
        input_output_aliases={} if prev is None else {4: 0},
        compiler_params=_params(1))(cm, g, xbuf, rbuf, *([] if prev is None else [prev]))


def _small_allreduce(v):
    rows = v.shape[0]
    n_dev = 2 * N_CHIPS

    def body(v_ref, o_ref, buf, ssem, rsem):
        x, y, c, _ = _place()
        me = 4 * x + 2 * y + c
        buf[me] = v_ref[...]

        def copy(d, slot, to):
            return _remote(v_ref, buf.at[slot], ssem.at[d - 1], rsem.at[d - 1], (to // 4, (to // 2) % 2, to % 2))

        cps = [copy(d, me, (me + d) % n_dev) for d in range(1, n_dev)]
        for cp in cps:
            cp.start()
        for d in range(1, n_dev):
            copy(d, (me + n_dev - d) % n_dev, me).wait_recv()
        for cp in cps:
            cp.wait_send()
        acc = buf[0]
        for i in range(1, n_dev):
            acc = acc + buf[i]
        o_ref[...] = acc

    vm = pl.BlockSpec(memory_space=pltpu.VMEM)
    return _call(
        body, name="small_allreduce", in_specs=[vm], out_specs=vm,
        out_shape=jax.ShapeDtypeStruct(v.shape, F32),
        scratch_shapes=[pltpu.VMEM((n_dev, rows, LANES), F32), pltpu.SemaphoreType.DMA((n_dev - 1,)),
                        pltpu.SemaphoreType.DMA((n_dev - 1,))],
        compiler_params=pltpu.CompilerParams(vmem_limit_bytes=V7X_VMEM_LIMIT))(v)


def _adamw_math(w, g, m, v):
    m2 = ADAM_B1 * m + (1.0 - ADAM_B1) * g
    v2 = ADAM_B2 * v + (1.0 - ADAM_B2) * (g * g)
    v_hat = v2 / (1.0 - ADAM_B2 ** ADAM_STEP)
    step = (-ADAM_LR / (1.0 - ADAM_B1 ** ADAM_STEP)) * m2 / (jnp.sqrt(v_hat) + ADAM_EPS)
    return step + (-ADAM_LR * ADAM_WD) * w, m2, v2


def _adamw_layer(w, g, m, v, l, prev):
    _, rows, cols = w.shape
    tr = rows
    for cand in range(8, rows + 1, 8):
        if rows % cand == 0 and cand * cols * 4 <= ADAMW_BLOCK_BYTES:
            tr = cand

    def body(w_ref, g_ref, m_ref, v_ref, *outs):
        d_ref, m2_ref, v2_ref = outs[-3:]
        d_ref[...], m2_ref[...], v2_ref[...] = _adamw_math(w_ref[...], g_ref[...], m_ref[...], v_ref[...])

    stack = pl.BlockSpec((None, tr, cols), lambda i: (l, i, 0))
    ins, specs, alias = [w, g, m, v], [stack] * 4, {}
    if prev is not None:
        ins += list(prev)
        specs += [ANY] * 3
        alias = {4 + i: i for i in range(3)}
    return _call(
        body, name="adamw", grid=(rows // tr,), in_specs=specs, out_specs=[stack] * 3,
        out_shape=[jax.ShapeDtypeStruct(w.shape, F32)] * 3, input_output_aliases=alias,
        compiler_params=_params(1))(*ins)


def _adamw_small(ws, gs, ms, vs):
    n = len(ws)
    row = lambda a: a.reshape(1, -1) if a.ndim == 1 else a
    ins = [row(a) for group in (ws, gs, ms, vs) for a in group]

    def body(*refs):
        w, g, m, v = (refs[i * n:(i + 1) * n] for i in range(4))
        outs = refs[4 * n:]
        for i in range(n):
            d, m2, v2 = _adamw_math(w[i][...], g[i][...], m[i][...], v[i][...])
            outs[i][...], outs[n + i][...], outs[2 * n + i][...] = d, m2, v2

    vm = pl.BlockSpec(memory_space=pltpu.VMEM)
    outs = _call(
        body, name="adamw_small", in_specs=[vm] * (4 * n), out_specs=[vm] * (3 * n),
        out_shape=[jax.ShapeDtypeStruct(a.shape, F32) for a in ins[:n]] * 3,
        compiler_params=pltpu.CompilerParams(vmem_limit_bytes=V7X_VMEM_LIMIT))(*ins)
    outs = [o.reshape(w.shape) for o, w in zip(outs, list(ws) * 3)]
    return outs[:n], outs[n:2 * n], outs[2 * n:]


SMALL = ("norm_ffn1", "norm_mix", "sinks", "norm_out_sb", "norm_out_swa", "norm_ffn2", "rel_bias", "norm_final")
BIG = ("ffn1_gu", "ffn1_down", "w_in", "w_out", "ffn2_gu", "ffn2_down")


def _pack(parts):
    flat, n = [], 0
    for a in parts:
        a = a.reshape(-1).astype(F32)
        gap = -a.shape[0] % LANES
        flat += [a] + ([jnp.zeros((gap,), F32)] if gap else [])
        n += a.shape[0] + gap
    tail = -(n // LANES) % 8 * LANES
    return jnp.concatenate(flat + ([jnp.zeros((tail,), F32)] if tail else [])).reshape(-1, LANES)


def _unpack(packed, like):
    out, r = [], 0
    for a in like:
        n = math.prod(a.shape)
        nr = -(-n // LANES)
        out.append(packed[r:r + nr].reshape(-1)[:n].reshape(a.shape))
        r += nr
    return out


def _halved(a):
    k, r, cols = a.shape
    return a.reshape(k, 2, r // 2, cols)


def _weight_view(k, buf):
    full = buf.reshape(N_CHIPS, buf.shape[2] * 2, buf.shape[3])
    return full if k.endswith("_gu") else full.reshape(-1, D_MODEL)


def _grad_stack(k, g):
    if not k.endswith("_gu"):
        g = g.reshape(N_CHIPS, g.shape[0] // N_CHIPS, D_MODEL)
    return _halved(g)


def _empty_like_hbm(shape, dtype):
    return pltpu.with_memory_space_constraint(lax.empty(shape, dtype), pltpu.HBM)


def kernel(x, norm_ffn1, w_ffn1_gu, w_ffn1_down, norm_mix, w_in, sinks, norm_out_sb, norm_out_swa, w_out, norm_ffn2, w_ffn2_gu, w_ffn2_down, rel_bias, norm_final, loss_target, m_norm_ffn1, m_w_ffn1_gu, m_w_ffn1_down, m_norm_mix, m_w_in, m_sinks, m_norm_out_sb, m_norm_out_swa, m_w_out, m_norm_ffn2, m_w_ffn2_gu, m_w_ffn2_down, m_rel_bias, m_norm_final, v_norm_ffn1, v_w_ffn1_gu, v_w_ffn1_down, v_norm_mix, v_w_in, v_sinks, v_norm_out_sb, v_norm_out_swa, v_w_out, v_norm_ffn2, v_w_ffn2_gu, v_w_ffn2_down, v_rel_bias, v_norm_final):
    big_w = dict(ffn1_gu=w_ffn1_gu, ffn1_down=w_ffn1_down, w_in=w_in, w_out=w_out, ffn2_gu=w_ffn2_gu, ffn2_down=w_ffn2_down)
    big_m = dict(ffn1_gu=m_w_ffn1_gu, ffn1_down=m_w_ffn1_down, w_in=m_w_in, w_out=m_w_out, ffn2_gu=m_w_ffn2_gu, ffn2_down=m_w_ffn2_down)
    big_v = dict(ffn1_gu=v_w_ffn1_gu, ffn1_down=v_w_ffn1_down, w_in=v_w_in, w_out=v_w_out, ffn2_gu=v_w_ffn2_gu, ffn2_down=v_w_ffn2_down)
    small = dict(norm_ffn1=norm_ffn1, norm_mix=norm_mix, sinks=sinks, norm_out_sb=norm_out_sb, norm_out_swa=norm_out_swa,
                 norm_ffn2=norm_ffn2, rel_bias=rel_bias, norm_final=norm_final)
    small_m = dict(norm_ffn1=m_norm_ffn1, norm_mix=m_norm_mix, sinks=m_sinks, norm_out_sb=m_norm_out_sb,
                   norm_out_swa=m_norm_out_swa, norm_ffn2=m_norm_ffn2, rel_bias=m_rel_bias, norm_final=m_norm_final)
    small_v = dict(norm_ffn1=v_norm_ffn1, norm_mix=v_norm_mix, sinks=v_sinks, norm_out_sb=v_norm_out_sb,
                   norm_out_swa=v_norm_out_swa, norm_ffn2=v_norm_ffn2, rel_bias=v_rel_bias, norm_final=v_norm_final)
    for dct in (big_w, big_m, big_v):
        dct["w_in"] = jnp.swapaxes(dct["w_in"], 1, 2)
    _PREVIOUS[0] = None
    _, _, c, me = _place()
    cm = jnp.stack([c, me]).astype(jnp.int32)
    buckets = jnp.asarray(_bucket_table())
    ffn1, mix_in, rest = ("ffn1_gu", "ffn1_down"), ("w_in",), ("w_out", "ffn2_gu", "ffn2_down")

    def place(l, keys):
        return [_place_own(big_w[k], l, cm[1:]) for k in keys]

    def views(keys, bufs):
        return {k: _weight_view(k, b) for k, b in zip(keys, bufs)}

    def gather_start(tag, bufs):
        return _exchange_start(f"gather{tag}_ici_start", _plan_gather_ici, bufs, 3 * len(bufs))

    def gather_pass(tag, flight):
        return _exchange_pass(f"gather{tag}_pass", _plan_gather_ici, _plan_gather_d2d, flight[1], flight[0],
                              3 * len(flight[1]))

    def gather_done(tag, keys, flight):
        return views(keys, _exchange_wait(f"gather{tag}_d2d_wait", _plan_gather_d2d, flight[1], flight[0]))

    fly_gu0 = gather_start("0a", place(0, ffn1[:1]))
    fly_down0 = gather_start("0a2", place(0, ffn1[1:]))
    fly_in0 = gather_start("0b", place(0, mix_in))
    later = [place(l, keys) for l in range(DEPTH) for keys in ((rest,) if l == 0 else (ffn1, mix_in, rest))]
    fly_rest0, fly_ffn1, fly_in1, fly_rest1 = _exchange_start_groups(
        "gather_later_ici_start", _plan_gather_ici, [(bufs, 3 * len(bufs)) for bufs in later])
    bias = _bias_table(rel_bias, buckets)
    n1 = _norm_cast(x[0], _row(norm_ffn1[0]))
    w0 = gather_done("0a", ffn1[:1], gather_pass("0a", fly_gu0))

    s0 = _fwd_ffn1_gu(x[0], n1, w0)
    w0.update(gather_done("0a2", ffn1[1:], gather_pass("0a2", fly_down0)))
    fly_in0 = gather_pass("0b", fly_in0)
    _fwd_ffn1_down(s0, w0, small, 0)
    w0.update(gather_done("0b", mix_in, fly_in0))
    _fwd_proj_sb(s0, w0)
    fly_rest0 = gather_pass("0c", fly_rest0)
    _fwd_swa(s0, small, 0, bias)
    w0.update(gather_done("0c", rest, fly_rest0))
    _fwd_out_gu2(s0, w0, small, 0)
    fly_ffn1 = gather_pass("1a", fly_ffn1)
    h, n1 = _fwd_ffn2_down(s0, w0, _row(norm_ffn1[1]))
    w1 = gather_done("1a", ffn1, fly_ffn1)
    fly_in1 = gather_pass("1b", fly_in1)
    s1 = _fwd_ffn1(h, n1, w1, small, 1)
    w1.update(gather_done("1b", mix_in, fly_in1))
    _fwd_proj_sb(s1, w1)
    fly_rest1 = gather_pass("1c", fly_rest1)
    _fwd_swa(s1, small, 1, bias)
    w1.update(gather_done("1c", rest, fly_rest1))
    h, _ = _fwd_out_ffn2(s1, w1, small, 1, _row(norm_final))
    dh32, dh16, dg_final, loss_row = _loss_head(h, _row(norm_final), loss_target[0])
    dh = (dh32, dh16)

    def landing(stacks, lead, dtype):
        return [_empty_like_hbm((lead,) + a.shape[2:], dtype) for a in stacks]

    def reduce_begin(tag, keys, gw):
        stacks = [_grad_stack(k, gw[k]) for k in keys]
        flight = _exchange_start(f"grad{tag}_sibling_start", _plan_grad_sibling,
                                 stacks + landing(stacks, N_CHIPS, F32), len(keys), sibling_only=True)
        return dict(tag=tag, keys=keys, stacks=stacks, flight=flight)

    def reduce_chips(st):
        n, (sems, bufs) = len(st["keys"]), st["flight"]
        bufs = _exchange_wait(f"grad{st['tag']}_sibling_wait", _plan_grad_sibling, bufs, sems)
        st["own"] = list(zip(bufs[:n], bufs[n:]))
        st["flight"] = _exchange_start(f"grad{st['tag']}_chips_start", _plan_grad_chips,
                                       [_chip_sum(g, z, cm) for g, z in st["own"]] + landing(st["stacks"], 3, BF16),
                                       3 * n)

    def reduce_halves(st, l, prev):
        n, (sems, bufs) = len(st["keys"]), st["flight"]
        bufs = _exchange_wait(f"grad{st['tag']}_chips_wait", _plan_grad_chips, bufs, sems)
        halves = [_total_sum(g, x, z, cm, l, None if prev is None else prev[k])
                  for k, (g, x), z in zip(st["keys"], st["own"], bufs[n:])]
        st["plan"] = _plan_grad_halves(l)
        st["flight"] = _exchange_start(f"grad{st['tag']}_halves_start", st["plan"], halves, n, sibling_only=True)

    def reduce_end(st):
        sems, bufs = st["flight"]
        bufs = _exchange_wait(f"grad{st['tag']}_halves_wait", st["plan"], bufs, sems)
        return dict(zip(st["keys"], bufs))

    def adamw(reduced, keys, l, prev):
        return {k: _adamw_layer(big_w[k], reduced[k], big_m[k], big_v[k], l, None if prev is None else prev[k])
                for k in keys}

    gsm = [dict() for _ in range(DEPTH)]
    dbias = jnp.zeros((8, BLK, 2 * BLK), F32)
    dh, gw1, gs = _bwd_ffn(dh, s1, w1, small, 1, 2)
    gsm[1].update(gs)
    dh, gw, gs, dbias = _bwd_mix(dh, s1, w1, small, 1, bias, dbias)
    gw1.update(gw)
    gsm[1].update(gs)
    dh, gw, gs = _bwd_ffn(dh, s1, w1, small, 1, 1)
    gw1.update(gw)
    gsm[1].update(gs)

    red1 = reduce_begin("1", BIG, gw1)
    dh, gw0, gs = _bwd_ffn(dh, s0, w0, small, 0, 2)
    gsm[0].update(gs)
    reduce_chips(red1)
    dh, gw, gs, dbias = _bwd_mix(dh, s0, w0, small, 0, bias, dbias)
    gw0.update(gw)
    gsm[0].update(gs)
    red0a = reduce_begin("0a", ("ffn2_gu", "ffn2_down", "w_out", "w_in"), gw0)
    reduce_halves(red1, 1, None)
    dgu = _bwd_ffn_dact(dh, s0, w0, 1)
    reduce_chips(red0a)
    dh, gw, gs = _bwd_ffn_rest(dh, dgu, s0, w0, small, 0, 1)
    gsm[0].update(gs)
    red0b = reduce_begin("0b", ffn1, gw)
    reduced1 = reduce_end(red1)
    ffn2 = ("ffn2_gu", "ffn2_down")
    stacks = adamw(reduced1, ffn2, 1, None)

    gsmall = {k: jnp.stack([gsm[l][k].reshape(-1) for l in range(DEPTH)]) for k in gsm[0]}
    gsmall["rel_bias"] = jnp.transpose(_bias_grad(dbias, buckets)[:, :N_BUCKETS])
    gsmall["norm_final"] = dg_final.reshape(-1)
    small_like = [small[k] for k in SMALL]
    red = _small_allreduce(_pack([gsmall[k] for k in SMALL] + [loss_row[0, :1]]))
    gs = _unpack(red, small_like + [loss_row[0, :1]])
    loss = gs[-1][0]
    gs = dict(zip(SMALL, gs[:-1]))

    reduce_chips(red0b)
    stacks.update(adamw(reduced1, ("w_in", "w_out"), 1, None))
    reduce_halves(red0a, 0, reduced1)
    stacks.update(adamw(reduced1, ffn1, 1, None))
    dlt, m2, v2 = _adamw_small(*[[dct[k] for k in SMALL] for dct in (small, gs, small_m, small_v)])
    reduced0 = reduce_end(red0a)
    stacks.update(adamw(reduced0, ffn2, 0, stacks))
    reduce_halves(red0b, 0, reduced1)
    stacks.update(adamw(reduced0, ("w_in", "w_out"), 0, stacks))
    reduced0.update(reduce_end(red0b))
    stacks.update(adamw(reduced0, ffn1, 0, stacks))

    out_g, out_d, out_m, out_v = {}, {}, {}, {}
    for k in BIG:
        out_g[k], out_d[k], out_m[k], out_v[k] = [jnp.swapaxes(a, 1, 2) if k == "w_in" else a
                                                  for a in (reduced0[k], *stacks[k])]
    for dst, parts in ((out_d, dlt), (out_m, m2), (out_v, v2)):
        dst.update(zip(SMALL, parts))
    out_g.update(gs)

    order = ("norm_ffn1", "ffn1_gu", "ffn1_down", "norm_mix", "w_in", "sinks", "norm_out_sb", "norm_out_swa", "w_out",
             "norm_ffn2", "ffn2_gu", "ffn2_down", "rel_bias", "norm_final")
    return (loss, dh[0].reshape(x.shape), *[out_g[k] for k in order], *[out_d[k] for k in order],
            *[out_m[k] for k in order], *[out_v[k] for k in order])
```
